```python
import math
import jax, jax.numpy as jnp
from jax import lax
import numpy as np

D_MODEL = 2048
BATCH = 8
SEQ = 4096
DEPTH = 1

CHUNK = 64
N_META = 16
PAD_LEAD = CHUNK - N_META
EPS = 1e-6

SSD_HEADS = 32
SSD_HEAD_DIM = 64
D_SSD = SSD_HEADS * SSD_HEAD_DIM
SSD_GROUPS = 8
SSD_HEADS_PER_GROUP = SSD_HEADS // SSD_GROUPS
D_STATE = 128
CONV_WIDTH = 4
D_CONV = D_SSD + 2 * SSD_GROUPS * D_STATE

ATT_Q_HEADS = 16
ATT_KV_HEADS = 4
ATT_REP = ATT_Q_HEADS // ATT_KV_HEADS
ATT_HEAD_DIM = 64
D_ATT = ATT_Q_HEADS * ATT_HEAD_DIM
D_KV = ATT_KV_HEADS * ATT_HEAD_DIM
WINDOW = 128
WINDOW_CHUNKS = WINDOW // CHUNK
BAND = (WINDOW_CHUNKS + 1) * CHUNK
ROPE_THETA = 10000.0

D_MIX = D_SSD + D_ATT
D_IN_PROJ = D_SSD + D_CONV + SSD_HEADS + D_ATT + 2 * D_KV + D_ATT

kernel_name = "hymba_ssd_swa_sink_streaming_layer"


def rmsnorm(x, w):
    x32 = x.astype(jnp.float32)
    y = x32 * lax.rsqrt(jnp.mean(x32 * x32, axis=-1, keepdims=True) + EPS)
    return (y * w.astype(jnp.float32)).astype(x.dtype)


def causal_depthwise_conv(u, w, b):
    out = lax.conv_general_dilated(
        u.astype(jnp.float32), w.astype(jnp.float32)[:, None, :],
        window_strides=(1,), padding=[(CONV_WIDTH - 1, 0)],
        dimension_numbers=("NWC", "WIO", "NWC"), feature_group_count=u.shape[-1])
    return out + b.astype(jnp.float32)


def ssd_chunked_scan(xs, dt, a, b_mat, c_mat):
    bsz, lp, g, r, p = xs.shape
    n = b_mat.shape[-1]
    nc = lp // CHUNK
    xs = xs.reshape(bsz, nc, CHUNK, g, r, p)
    dt = dt.reshape(bsz, nc, CHUNK, g, r)
    b_mat = b_mat.reshape(bsz, nc, CHUNK, g, n)
    c_mat = c_mat.reshape(bsz, nc, CHUNK, g, n)
    a_cs = jnp.cumsum(dt * a, axis=2)
    x_dt = xs * dt[..., None]
    causal = jnp.tril(jnp.ones((CHUNK, CHUNK), dtype=bool))[:, :, None, None]
    seg = a_cs[:, :, :, None] - a_cs[:, :, None, :]
    decay_ls = jnp.exp(jnp.where(causal, seg, -jnp.inf))
    cb = jnp.einsum("bclgn,bcsgn->bclsg", c_mat, b_mat)
    y_diag = jnp.einsum("bclsg,bclsgr,bcsgrp->bclgrp", cb, decay_ls, x_dt)
    decay_to_end = jnp.exp(a_cs[:, :, -1:] - a_cs)
    states = jnp.einsum("bclgn,bclgr,bclgrp->bcgrpn", b_mat, decay_to_end, x_dt)
    chunk_decay = jnp.exp(a_cs[:, :, -1])

    def step(h, inp):
        st, dec = inp
        return h * dec[..., None, None] + st, h

    h0 = jnp.zeros((bsz, g, r, p, n), xs.dtype)
    _, prev = lax.scan(step, h0, (jnp.moveaxis(states, 1, 0), jnp.moveaxis(chunk_decay, 1, 0)))
    prev = jnp.moveaxis(prev, 0, 1)
    y_off = jnp.einsum("bclgn,bcgrpn,bclgr->bclgrp", c_mat, prev, jnp.exp(a_cs))
    return (y_diag + y_off).reshape(bsz, lp, g, r, p)


def rope(t, pos):
    half = t.shape[-1] // 2
    inv = ROPE_THETA ** (-jnp.arange(half, dtype=jnp.float32) / half)
    ang = pos.astype(jnp.float32)[:, None] * inv[None, :]
    cos = jnp.cos(ang)[None, :, None, :]
    sin = jnp.sin(ang)[None, :, None, :]
    t1, t2 = t[..., :half], t[..., half:]
    return jnp.concatenate([t1 * cos - t2 * sin, t1 * sin + t2 * cos], axis=-1)


def banded_sink_attention(q, k, v, sinks):
    bsz, lp = q.shape[:2]
    nc = lp // CHUNK
    q = q.reshape(bsz, nc, CHUNK, ATT_KV_HEADS, ATT_REP, ATT_HEAD_DIM)
    k = k.reshape(bsz, nc, CHUNK, ATT_KV_HEADS, ATT_HEAD_DIM)
    v = v.reshape(bsz, nc, CHUNK, ATT_KV_HEADS, ATT_HEAD_DIM)
    padw = ((0, 0), (WINDOW_CHUNKS, 0), (0, 0), (0, 0), (0, 0))
    kp, vp = jnp.pad(k, padw), jnp.pad(v, padw)
    k_band = jnp.concatenate([kp[:, j:j + nc] for j in range(WINDOW_CHUNKS + 1)], axis=2)
    v_band = jnp.concatenate([vp[:, j:j + nc] for j in range(WINDOW_CHUNKS + 1)], axis=2)
    scale = ATT_HEAD_DIM ** -0.5
    s = jnp.einsum("bclhrd,bcshd->bchrls", q, k_band) * scale
    key_abs = (jnp.arange(nc)[:, None] - WINDOW_CHUNKS) * CHUNK + jnp.arange(BAND)[None, :]
    valid = key_abs >= PAD_LEAD
    s = jnp.where(valid[None, :, None, None, None, :], s, -jnp.inf)
    sink = sinks.astype(jnp.float32).reshape(ATT_KV_HEADS, ATT_REP)[None, None, :, :, None, None]
    m = jnp.maximum(jnp.max(s, axis=-1, keepdims=True), sink)
    pr = jnp.exp(s - m)
    denom = jnp.sum(pr, axis=-1, keepdims=True) + jnp.exp(sink - m)
    out = jnp.einsum("bchrls,bcshd->bclhrd", pr / denom, v_band)
    return out.reshape(bsz, lp, D_ATT)


def _fwd_setup_inputs(seed: int = 0) -> dict:
    key = jax.random.key(seed)
    ks = jax.random.split(key, 14)
    f32 = jnp.float32
    x = jax.random.normal(ks[0], (BATCH, SEQ, D_MODEL), f32)
    meta_tokens = jax.random.normal(ks[1], (N_META, D_MODEL), f32)
    norm_pre_w = 1.0 + 0.01 * jax.random.normal(ks[2], (DEPTH, D_MODEL), f32)
    w_in = jax.random.normal(ks[3], (DEPTH, D_MODEL, D_IN_PROJ), f32) * D_MODEL ** -0.5
    conv_w = jax.random.normal(ks[4], (DEPTH, CONV_WIDTH, D_CONV), f32) * CONV_WIDTH ** -0.5
    conv_b = 0.02 * jax.random.normal(ks[5], (DEPTH, D_CONV), f32)
    dt0 = jnp.exp(jax.random.uniform(ks[6], (DEPTH, SSD_HEADS), f32,
                                     minval=math.log(1e-3), maxval=math.log(1e-1)))
    dt_bias = dt0 + jnp.log(-jnp.expm1(-dt0))
    a_log = jnp.log(jax.random.uniform(ks[7], (DEPTH, SSD_HEADS), f32, minval=1.0, maxval=16.0))
    d_skip = 1.0 + 0.01 * jax.random.normal(ks[8], (DEPTH, SSD_HEADS), f32)
    ssd_norm_w = 1.0 + 0.01 * jax.random.normal(ks[9], (DEPTH, D_SSD), f32)
    attn_sinks = 0.5 * jax.random.normal(ks[10], (DEPTH, ATT_Q_HEADS), f32)
    w_out = jax.random.normal(ks[11], (DEPTH, D_MIX, D_MODEL), f32) * D_MIX ** -0.5
    norm_post_w = 1.0 + 0.01 * jax.random.normal(ks[12], (DEPTH, D_MODEL), f32)
    return {"x": x, "meta_tokens": meta_tokens, "norm_pre_w": norm_pre_w, "w_in": w_in,
            "conv_w": conv_w, "conv_b": conv_b, "dt_bias": dt_bias, "a_log": a_log,
            "d_skip": d_skip, "ssd_norm_w": ssd_norm_w, "attn_sinks": attn_sinks,
            "w_out": w_out, "norm_post_w": norm_post_w}


def _fwd_reference(x, meta_tokens, norm_pre_w, w_in, conv_w, conv_b, dt_bias, a_log, d_skip,
              ssd_norm_w, attn_sinks, w_out, norm_post_w):
    bsz, seq, _ = x.shape
    meta = jnp.broadcast_to(meta_tokens[None].astype(x.dtype), (bsz, N_META, D_MODEL))
    h = jnp.concatenate([meta, x], axis=1)
    lp = N_META + seq + PAD_LEAD
    idx = jnp.arange(lp)
    pos = idx - PAD_LEAD
    valid = (idx >= PAD_LEAD).astype(jnp.float32)
    split_pts = [D_SSD, D_SSD + D_CONV, D_SSD + D_CONV + SSD_HEADS,
                 D_SSD + D_CONV + SSD_HEADS + D_ATT,
                 D_SSD + D_CONV + SSD_HEADS + D_ATT + D_KV,
                 D_SSD + D_CONV + SSD_HEADS + D_ATT + 2 * D_KV]
    for layer in range(DEPTH):
        hn = rmsnorm(h, norm_pre_w[layer])
        proj = jnp.matmul(hn, w_in[layer]).astype(jnp.float32)
        proj = jnp.pad(proj, ((0, 0), (PAD_LEAD, 0), (0, 0)))
        z, xbc, dt_raw, q, k, v, g_att = jnp.split(proj, split_pts, axis=-1)

        xbc = jax.nn.silu(causal_depthwise_conv(xbc, conv_w[layer], conv_b[layer]))
        xs, b_mat, c_mat = jnp.split(xbc, [D_SSD, D_SSD + SSD_GROUPS * D_STATE], axis=-1)
        dt = jax.nn.softplus(dt_raw + dt_bias[layer].astype(jnp.float32)) * valid[None, :, None]
        a = -jnp.exp(a_log[layer].astype(jnp.float32))
        xs = xs.reshape(bsz, lp, SSD_GROUPS, SSD_HEADS_PER_GROUP, SSD_HEAD_DIM)
        y = ssd_chunked_scan(xs, dt.reshape(bsz, lp, SSD_GROUPS, SSD_HEADS_PER_GROUP),
                             a.reshape(SSD_GROUPS, SSD_HEADS_PER_GROUP),
                             b_mat.reshape(bsz, lp, SSD_GROUPS, D_STATE),
                             c_mat.reshape(bsz, lp, SSD_GROUPS, D_STATE))
        y = y + d_skip[layer].astype(jnp.float32).reshape(SSD_GROUPS, SSD_HEADS_PER_GROUP)[..., None] * xs
        y = y.reshape(bsz, lp, D_SSD) * jax.nn.silu(z)
        y = rmsnorm(y.reshape(bsz, lp, SSD_GROUPS, D_SSD // SSD_GROUPS),
                    jnp.ones((D_SSD // SSD_GROUPS,), jnp.float32)).reshape(bsz, lp, D_SSD)
        y = y * ssd_norm_w[layer].astype(jnp.float32)

        q = rope(q.reshape(bsz, lp, ATT_Q_HEADS, ATT_HEAD_DIM), pos)
        k = rope(k.reshape(bsz, lp, ATT_KV_HEADS, ATT_HEAD_DIM), pos)
        v = v.reshape(bsz, lp, ATT_KV_HEADS, ATT_HEAD_DIM)
        att = banded_sink_attention(q, k, v, attn_sinks[layer]) * jax.nn.silu(g_att)

        mix = jnp.concatenate([y, att], axis=-1)[:, PAD_LEAD:].astype(h.dtype)
        out = jnp.matmul(mix, w_out[layer])
        h = h + rmsnorm(out, norm_post_w[layer])
    return h[:, N_META:]


import jax as _jax
import jax.numpy as _jnp

TWIN_FORMAT = 'train_step'
FWD_PARAMS = ['x', 'meta_tokens', 'norm_pre_w', 'w_in', 'conv_w', 'conv_b', 'dt_bias', 'a_log', 'd_skip', 'ssd_norm_w', 'attn_sinks', 'w_out', 'norm_post_w']
TWIN_WEIGHTS = ['meta_tokens', 'norm_pre_w', 'w_in', 'conv_w', 'conv_b', 'dt_bias', 'a_log', 'd_skip', 'ssd_norm_w', 'attn_sinks', 'w_out', 'norm_post_w']
TWIN_DIFF_INPUT = 'x'
TWIN_INPUTS = ['x', 'meta_tokens', 'norm_pre_w', 'w_in', 'conv_w', 'conv_b', 'dt_bias', 'a_log', 'd_skip', 'ssd_norm_w', 'attn_sinks', 'w_out', 'norm_post_w', 'loss_target', 'm_meta_tokens', 'm_norm_pre_w', 'm_w_in', 'm_conv_w', 'm_conv_b', 'm_dt_bias', 'm_a_log', 'm_d_skip', 'm_ssd_norm_w', 'm_attn_sinks', 'm_w_out', 'm_norm_post_w', 'v_meta_tokens', 'v_norm_pre_w', 'v_w_in', 'v_conv_w', 'v_conv_b', 'v_dt_bias', 'v_a_log', 'v_d_skip', 'v_ssd_norm_w', 'v_attn_sinks', 'v_w_out', 'v_norm_post_w']
TWIN_OUTPUTS = ['loss', 'grad_x', 'grad_meta_tokens', 'grad_norm_pre_w', 'grad_w_in', 'grad_conv_w', 'grad_conv_b', 'grad_dt_bias', 'grad_a_log', 'grad_d_skip', 'grad_ssd_norm_w', 'grad_attn_sinks', 'grad_w_out', 'grad_norm_post_w', 'delta_meta_tokens', 'delta_norm_pre_w', 'delta_w_in', 'delta_conv_w', 'delta_conv_b', 'delta_dt_bias', 'delta_a_log', 'delta_d_skip', 'delta_ssd_norm_w', 'delta_attn_sinks', 'delta_w_out', 'delta_norm_post_w', 'new_m_meta_tokens', 'new_m_norm_pre_w', 'new_m_w_in', 'new_m_conv_w', 'new_m_conv_b', 'new_m_dt_bias', 'new_m_a_log', 'new_m_d_skip', 'new_m_ssd_norm_w', 'new_m_attn_sinks', 'new_m_w_out', 'new_m_norm_post_w', 'new_v_meta_tokens', 'new_v_norm_pre_w', 'new_v_w_in', 'new_v_conv_w', 'new_v_conv_b', 'new_v_dt_bias', 'new_v_a_log', 'new_v_d_skip', 'new_v_ssd_norm_w', 'new_v_attn_sinks', 'new_v_w_out', 'new_v_norm_post_w']
TWIN_LEAF_KINDS = {'loss': 'loss', 'grad_x': 'grad_x', 'grad_meta_tokens': 'grad_w', 'grad_norm_pre_w': 'grad_w', 'grad_w_in': 'grad_w', 'grad_conv_w': 'grad_w', 'grad_conv_b': 'grad_w', 'grad_dt_bias': 'grad_w', 'grad_a_log': 'grad_w', 'grad_d_skip': 'grad_w', 'grad_ssd_norm_w': 'grad_w', 'grad_attn_sinks': 'grad_w', 'grad_w_out': 'grad_w', 'grad_norm_post_w': 'grad_w', 'delta_meta_tokens': 'delta_w', 'delta_norm_pre_w': 'delta_w', 'delta_w_in': 'delta_w', 'delta_conv_w': 'delta_w', 'delta_conv_b': 'delta_w', 'delta_dt_bias': 'delta_w', 'delta_a_log': 'delta_w', 'delta_d_skip': 'delta_w', 'delta_ssd_norm_w': 'delta_w', 'delta_attn_sinks': 'delta_w', 'delta_w_out': 'delta_w', 'delta_norm_post_w': 'delta_w', 'new_m_meta_tokens': 'new_m', 'new_m_norm_pre_w': 'new_m', 'new_m_w_in': 'new_m', 'new_m_conv_w': 'new_m', 'new_m_conv_b': 'new_m', 'new_m_dt_bias': 'new_m', 'new_m_a_log': 'new_m', 'new_m_d_skip': 'new_m', 'new_m_ssd_norm_w': 'new_m', 'new_m_attn_sinks': 'new_m', 'new_m_w_out': 'new_m', 'new_m_norm_post_w': 'new_m', 'new_v_meta_tokens': 'new_v', 'new_v_norm_pre_w': 'new_v', 'new_v_w_in': 'new_v', 'new_v_conv_w': 'new_v', 'new_v_conv_b': 'new_v', 'new_v_dt_bias': 'new_v', 'new_v_a_log': 'new_v', 'new_v_d_skip': 'new_v', 'new_v_ssd_norm_w': 'new_v', 'new_v_attn_sinks': 'new_v', 'new_v_w_out': 'new_v', 'new_v_norm_post_w': 'new_v'}


def _forward(args):
    return _fwd_reference(*[args[k] for k in FWD_PARAMS])


def _output_shape():
    def fwd():
        inp = _fwd_setup_inputs(0)
        return _fwd_reference(*[inp[k] for k in FWD_PARAMS])
    out = _jax.eval_shape(fwd)
    return out.shape, out.dtype

N_MICROBATCH = 1
ADAM_LR = 0.001
ADAM_B1 = 0.9
ADAM_B2 = 0.999
ADAM_EPS = 1e-08
ADAM_WD = 0.01
ADAM_STEP = 10
PER_EXAMPLE_BATCH_AXIS = {'x': 0, 'loss_target': 0}
SHARED_INPUTS = []
_WEIGHT_DTYPES = {'meta_tokens': _jnp.float32, 'norm_pre_w': _jnp.float32, 'w_in': _jnp.float32, 'conv_w': _jnp.float32, 'conv_b': _jnp.float32, 'dt_bias': _jnp.float32, 'a_log': _jnp.float32, 'd_skip': _jnp.float32, 'ssd_norm_w': _jnp.float32, 'attn_sinks': _jnp.float32, 'w_out': _jnp.float32, 'norm_post_w': _jnp.float32}
MOMENT_SCALE = {'meta_tokens': 3.690181e-03, 'norm_pre_w': 2.196586e-01, 'w_in': 1.071011e-01, 'conv_w': 1.117711e-01, 'conv_b': 2.599335e-01, 'dt_bias': 2.335668e-01, 'a_log': 4.099496e-01, 'd_skip': 8.389508e-01, 'ssd_norm_w': 1.869906e-01, 'attn_sinks': 7.751197e-04, 'w_out': 1.762717e-01, 'norm_post_w': 1.601959e+01}


def _to_microbatches(a, axis):
    t = _jnp.moveaxis(a, axis, 0)
    t = t.reshape((N_MICROBATCH, t.shape[0] // N_MICROBATCH) + t.shape[1:])
    return _jnp.moveaxis(t, 1, axis + 1)


def setup_inputs(seed: int = 0) -> dict:
    inp = _fwd_setup_inputs(seed)
    key = _jax.random.fold_in(_jax.random.key(seed), 7919)
    shape, _ = _output_shape()
    out = dict(inp)
    out["loss_target"] = _jax.random.normal(_jax.random.fold_in(key, 0), shape, _jnp.float32)
    for i, name in enumerate(TWIN_WEIGHTS):
        w = inp[name].astype(_jnp.float32)
        if MOMENT_SCALE is None:
            s = _jnp.sqrt(_jnp.mean(_jnp.square(w)) + 1e-30)
        else:
            s = MOMENT_SCALE[name]
        km, kv = _jax.random.split(_jax.random.fold_in(key, i + 1))
        out[name] = w
        out["m_" + name] = s * _jax.random.normal(km, w.shape, _jnp.float32)
        out["v_" + name] = (s * s) * _jax.random.uniform(kv, w.shape, _jnp.float32, 0.5, 1.5)
    if N_MICROBATCH > 1:
        for name, axis in PER_EXAMPLE_BATCH_AXIS.items():
            out[name] = _to_microbatches(out[name], axis)
    return {'x': out['x'], 'meta_tokens': out['meta_tokens'], 'norm_pre_w': out['norm_pre_w'], 'w_in': out['w_in'], 'conv_w': out['conv_w'], 'conv_b': out['conv_b'], 'dt_bias': out['dt_bias'], 'a_log': out['a_log'], 'd_skip': out['d_skip'], 'ssd_norm_w': out['ssd_norm_w'], 'attn_sinks': out['attn_sinks'], 'w_out': out['w_out'], 'norm_post_w': out['norm_post_w'], 'loss_target': out['loss_target'], 'm_meta_tokens': out['m_meta_tokens'], 'm_norm_pre_w': out['m_norm_pre_w'], 'm_w_in': out['m_w_in'], 'm_conv_w': out['m_conv_w'], 'm_conv_b': out['m_conv_b'], 'm_dt_bias': out['m_dt_bias'], 'm_a_log': out['m_a_log'], 'm_d_skip': out['m_d_skip'], 'm_ssd_norm_w': out['m_ssd_norm_w'], 'm_attn_sinks': out['m_attn_sinks'], 'm_w_out': out['m_w_out'], 'm_norm_post_w': out['m_norm_post_w'], 'v_meta_tokens': out['v_meta_tokens'], 'v_norm_pre_w': out['v_norm_pre_w'], 'v_w_in': out['v_w_in'], 'v_conv_w': out['v_conv_w'], 'v_conv_b': out['v_conv_b'], 'v_dt_bias': out['v_dt_bias'], 'v_a_log': out['v_a_log'], 'v_d_skip': out['v_d_skip'], 'v_ssd_norm_w': out['v_ssd_norm_w'], 'v_attn_sinks': out['v_attn_sinks'], 'v_w_out': out['v_w_out'], 'v_norm_post_w': out['v_norm_post_w']}


def _loss(weights, diff, rest, loss_target):
    with _jax.named_scope("forward"):
        args = {**rest, TWIN_DIFF_INPUT: diff, **{k: w.astype(_WEIGHT_DTYPES[k]) for k, w in weights.items()}}
        y = _forward(args)
    with _jax.named_scope("loss_head"):
        err = _jnp.square(y.astype(_jnp.float32) - loss_target)
        return 0.5 * _jnp.sum(_jnp.mean(err, axis=-1)) if err.ndim else 0.5 * err


def _adamw(w, g, m, v):
    m = ADAM_B1 * m + (1.0 - ADAM_B1) * g
    v = ADAM_B2 * v + (1.0 - ADAM_B2) * _jnp.square(g)
    m_hat = m / (1.0 - ADAM_B1 ** ADAM_STEP)
    v_hat = v / (1.0 - ADAM_B2 ** ADAM_STEP)
    delta = -ADAM_LR * (m_hat / (_jnp.sqrt(v_hat) + ADAM_EPS) + ADAM_WD * w)
    return delta, m, v


def reference(x, meta_tokens, norm_pre_w, w_in, conv_w, conv_b, dt_bias, a_log, d_skip, ssd_norm_w, attn_sinks, w_out, norm_post_w, loss_target, m_meta_tokens, m_norm_pre_w, m_w_in, m_conv_w, m_conv_b, m_dt_bias, m_a_log, m_d_skip, m_ssd_norm_w, m_attn_sinks, m_w_out, m_norm_post_w, v_meta_tokens, v_norm_pre_w, v_w_in, v_conv_w, v_conv_b, v_dt_bias, v_a_log, v_d_skip, v_ssd_norm_w, v_attn_sinks, v_w_out, v_norm_post_w):
    given = dict(x=x, meta_tokens=meta_tokens, norm_pre_w=norm_pre_w, w_in=w_in, conv_w=conv_w, conv_b=conv_b, dt_bias=dt_bias, a_log=a_log, d_skip=d_skip, ssd_norm_w=ssd_norm_w, attn_sinks=attn_sinks, w_out=w_out, norm_post_w=norm_post_w, loss_target=loss_target, m_meta_tokens=m_meta_tokens, m_norm_pre_w=m_norm_pre_w, m_w_in=m_w_in, m_conv_w=m_conv_w, m_conv_b=m_conv_b, m_dt_bias=m_dt_bias, m_a_log=m_a_log, m_d_skip=m_d_skip, m_ssd_norm_w=m_ssd_norm_w, m_attn_sinks=m_attn_sinks, m_w_out=m_w_out, m_norm_post_w=m_norm_post_w, v_meta_tokens=v_meta_tokens, v_norm_pre_w=v_norm_pre_w, v_w_in=v_w_in, v_conv_w=v_conv_w, v_conv_b=v_conv_b, v_dt_bias=v_dt_bias, v_a_log=v_a_log, v_d_skip=v_d_skip, v_ssd_norm_w=v_ssd_norm_w, v_attn_sinks=v_attn_sinks, v_w_out=v_w_out, v_norm_post_w=v_norm_post_w)
    weights = {n: given[n] for n in TWIN_WEIGHTS}
    shared = {n: given[n] for n in SHARED_INPUTS}
    per_example = {n: given[n] for n in ['x']}
    grad_fn = _jax.value_and_grad(_loss, argnums=(0, 1))

    def one_microbatch(ex, loss_target):
        ex = dict(ex)
        diff = ex.pop(TWIN_DIFF_INPUT)
        return grad_fn(weights, diff, {**shared, **ex}, loss_target)

    if N_MICROBATCH == 1:
        loss, (grad_w, grad_x) = one_microbatch(per_example, given["loss_target"])
    else:
        def body(carry, xs):
            loss_sum, grad_sum = carry
            l_k, (gw_k, gx_k) = one_microbatch(xs[0], xs[1])
            with _jax.named_scope("update"):
                return (loss_sum + l_k, _jax.tree.map(_jnp.add, grad_sum, gw_k)), gx_k

        init = (_jnp.zeros((), _jnp.float32), _jax.tree.map(_jnp.zeros_like, weights))
        (loss, grad_w), grad_x = _jax.lax.scan(body, init, (per_example, given["loss_target"]))
    with _jax.named_scope("update"):
        delta_w, new_m, new_v = {}, {}, {}
        for n in TWIN_WEIGHTS:
            delta_w[n], new_m[n], new_v[n] = _adamw(weights[n], grad_w[n], given["m_" + n], given["v_" + n])
    return (loss, grad_x, *[grad_w[n] for n in TWIN_WEIGHTS], *[delta_w[n] for n in TWIN_WEIGHTS],
            *[new_m[n] for n in TWIN_WEIGHTS], *[new_v[n] for n in TWIN_WEIGHTS])
```

```python
import functools
import math

import numpy as np
import jax
import jax.numpy as jnp
from jax import lax
from jax.experimental import pallas as pl
from jax.experimental.pallas import tpu as pltpu

F32 = jnp.float32
BF16 = jnp.bfloat16
HIGHEST = lax.Precision.HIGHEST

N_DEV = 8
D_MODEL = 2048
CHUNK = 64
N_META = 16
PAD_LEAD = CHUNK - N_META
EPS = 1e-6
N_GROUPS = 8
HEADS_PER_GROUP = 4
HEAD_DIM = 64
GROUP_W = HEADS_PER_GROUP * HEAD_DIM
D_STATE = 128
D_SSD = 2048
D_CONV = 4096
SSD_HEADS = 32
CONV_WIDTH = 4
Q_HEADS = 16
KV_HEADS = 4
REP = 4
D_ATT = 1024
D_KV = 256
WINDOW_CHUNKS = 2
BAND = (WINDOW_CHUNKS + 1) * CHUNK
ROPE_THETA = 10000.0
D_MIX = D_SSD + D_ATT
D_IN_PROJ = 8736
SHARD_IN = D_IN_PROJ // N_DEV

OFF_Z, OFF_XBC, OFF_Q, OFF_G, OFF_K, OFF_V, OFF_DT = 0, 2048, 6144, 7168, 8192, 8448, 8704
NP = 8960
XBC_BLK = 512

ADAM_LR, ADAM_B1, ADAM_B2, ADAM_EPS, ADAM_WD, ADAM_STEP = 0.001, 0.9, 0.999, 1e-08, 0.01, 10

VMEM_LIMIT = 48 * 1024 * 1024

NN = (((1,), (0,)), ((), ()))
NT = (((1,), (1,)), ((), ()))
TN = (((0,), (0,)), ((), ()))


def _dot(a, b, dims=NN, precision=None):
    return lax.dot_general(a, b, dims, precision=precision, preferred_element_type=F32)


def _tile(n, prefs):
    for t in prefs:
        if n % t == 0:
            return t
    return n


def _params(sem, vmem=None):
    return pltpu.CompilerParams(dimension_semantics=sem, vmem_limit_bytes=vmem)


def _sigmoid(x):
    return 1.0 / (1.0 + jnp.exp(-x))


def _matmul(a, b, *, trans_b, tm, tn, tk, out_dtype, name):
    m, k = a.shape
    n = b.shape[0] if trans_b else b.shape[1]
    nk = k // tk
    dims = NT if trans_b else NN

    def body(a_ref, b_ref, o_ref, *scratch):
        if nk == 1:
            o_ref[...] = _dot(a_ref[...], b_ref[...], dims).astype(out_dtype)
            return
        acc_ref, = scratch
        kk = pl.program_id(2)

        @pl.when(kk == 0)
        def _():
            acc_ref[...] = jnp.zeros_like(acc_ref)

        acc_ref[...] += _dot(a_ref[...], b_ref[...], dims)

        @pl.when(kk == nk - 1)
        def _():
            o_ref[...] = acc_ref[...].astype(out_dtype)

    b_spec = (pl.BlockSpec((tn, tk), lambda i, j, kk: (j, kk)) if trans_b
              else pl.BlockSpec((tk, tn), lambda i, j, kk: (kk, j)))
    return pl.pallas_call(
        body, name=name,
        grid=(m // tm, n // tn, nk),
        in_specs=[pl.BlockSpec((tm, tk), lambda i, j, kk: (i, kk)), b_spec],
        out_specs=pl.BlockSpec((tm, tn), lambda i, j, kk: (i, j)),
        out_shape=jax.ShapeDtypeStruct((m, n), out_dtype),
        scratch_shapes=[] if nk == 1 else [pltpu.VMEM((tm, tn), F32)],
        compiler_params=_params(("parallel", "parallel", "arbitrary"), VMEM_LIMIT),
    )(a, b)


def _prenorm(h_pad, w):
    lp = h_pad.shape[0]
    t = _tile(lp, (832, 416, 128, 64))

    def body(h_ref, w_ref, o_ref):
        h = h_ref[...]
        rstd = lax.rsqrt(jnp.mean(h * h, axis=-1, keepdims=True) + EPS)
        o_ref[...] = (h * rstd * w_ref[...]).astype(BF16)

    return pl.pallas_call(
        body, name="prenorm", grid=(lp // t,),
        in_specs=[pl.BlockSpec((t, D_MODEL), lambda i: (i, 0)), pl.BlockSpec((1, D_MODEL), lambda i: (0, 0))],
        out_specs=pl.BlockSpec((t, D_MODEL), lambda i: (i, 0)),
        out_shape=jax.ShapeDtypeStruct((lp, D_MODEL), BF16),
        compiler_params=_params(("parallel",), VMEM_LIMIT),
    )(h_pad, w)


CONV_COLS = 512
HALO = 8


def _conv_pre(ext, w, b, rows):
    acc = b + w[3:4, :] * ext[HALO:, :]
    for j in range(1, CONV_WIDTH):
        acc = acc + w[3 - j:4 - j, :] * pltpu.roll(ext, j, 0)[HALO:, :]
    return acc


def _conv_fwd(proj, conv_w, conv_b):
    lp = proj.shape[0]
    t = _tile(lp, (832, 416, 128, 64))
    hb = t // HALO
    c0 = OFF_XBC // CONV_COLS

    def body(u_ref, halo_ref, w_ref, b_ref, o_ref):
        i = pl.program_id(0)
        halo = halo_ref[...] * (i > 0).astype(F32)
        ext = jnp.concatenate([halo, u_ref[...]], axis=0)
        pre = _conv_pre(ext, w_ref[...], b_ref[...], t)
        o_ref[...] = pre * _sigmoid(pre)

    return pl.pallas_call(
        body, name="conv_fwd", grid=(lp // t, D_CONV // CONV_COLS),
        in_specs=[
            pl.BlockSpec((t, CONV_COLS), lambda i, j: (i, c0 + j)),
            pl.BlockSpec((HALO, CONV_COLS), lambda i, j: (jnp.maximum(i * hb - 1, 0), c0 + j)),
            pl.BlockSpec((CONV_WIDTH, CONV_COLS), lambda i, j: (0, j)),
            pl.BlockSpec((1, CONV_COLS), lambda i, j: (0, j)),
        ],
        out_specs=pl.BlockSpec((t, CONV_COLS), lambda i, j: (i, j)),
        out_shape=jax.ShapeDtypeStruct((lp, D_CONV), F32),
        compiler_params=_params(("parallel", "parallel"), VMEM_LIMIT),
    )(proj, proj, conv_w, conv_b)


def _conv_bwd(dxbc, proj, conv_w, conv_b):
    lp = proj.shape[0]
    t = _tile(lp, (832, 416, 128, 64))
    hb = t // HALO
    nt = lp // t
    c0 = OFF_XBC // CONV_COLS

    def body(dx_ref, dxn_ref, u_ref, up_ref, un_ref, w_ref, b_ref, du_ref, dw_ref, db_ref):
        i = pl.program_id(1)
        w = w_ref[...]
        up = up_ref[...] * (i > 0).astype(F32)
        ext = jnp.concatenate([up, u_ref[...], un_ref[...]], axis=0)
        pre = _conv_pre(ext, w, b_ref[...], t + HALO)
        dxn = dxn_ref[...] * (i < nt - 1).astype(F32)
        dxe = jnp.concatenate([dx_ref[...], dxn], axis=0)
        sg = _sigmoid(pre)
        dpre = dxe * sg * (1.0 + pre * (1.0 - sg))
        du = w[3:4, :] * dpre[:t, :]
        for j in range(1, CONV_WIDTH):
            du = du + w[3 - j:4 - j, :] * pltpu.roll(dpre, t + HALO - j, 0)[:t, :]
        du_ref[...] = du.astype(BF16)

        @pl.when(i == 0)
        def _():
            dw_ref[...] = jnp.zeros_like(dw_ref)
            db_ref[...] = jnp.zeros_like(db_ref)

        dp = dpre[:t, :]
        db_ref[...] += jnp.sum(dp, axis=0, keepdims=True)
        hist = ext[:HALO + t, :]
        dw_ref[3:4, :] += jnp.sum(dp * hist[HALO:, :], axis=0, keepdims=True)
        for j in range(1, CONV_WIDTH):
            dw_ref[3 - j:4 - j, :] += jnp.sum(dp * pltpu.roll(hist, j, 0)[HALO:, :], axis=0, keepdims=True)

    return pl.pallas_call(
        body, name="conv_bwd", grid=(D_CONV // CONV_COLS, nt),
        in_specs=[
            pl.BlockSpec((t, CONV_COLS), lambda j, i: (i, j)),
            pl.BlockSpec((HALO, CONV_COLS), lambda j, i: (jnp.minimum((i + 1) * hb, lp // HALO - 1), j)),
            pl.BlockSpec((t, CONV_COLS), lambda j, i: (i, c0 + j)),
            pl.BlockSpec((HALO, CONV_COLS), lambda j, i: (jnp.maximum(i * hb - 1, 0), c0 + j)),
            pl.BlockSpec((HALO, CONV_COLS), lambda j, i: (jnp.minimum((i + 1) * hb, lp // HALO - 1), c0 + j)),
            pl.BlockSpec((CONV_WIDTH, CONV_COLS), lambda j, i: (0, j)),
            pl.BlockSpec((1, CONV_COLS), lambda j, i: (0, j)),
        ],
        out_specs=[
            pl.BlockSpec((t, CONV_COLS), lambda j, i: (i, j)),
            pl.BlockSpec((CONV_WIDTH, CONV_COLS), lambda j, i: (0, j)),
            pl.BlockSpec((1, CONV_COLS), lambda j, i: (0, j)),
        ],
        out_shape=[
            jax.ShapeDtypeStruct((lp, D_CONV), BF16),
            jax.ShapeDtypeStruct((CONV_WIDTH, D_CONV), F32),
            jax.ShapeDtypeStruct((1, D_CONV), F32),
        ],
        compiler_params=_params(("parallel", "arbitrary"), VMEM_LIMIT),
    )(dxbc, dxbc, proj, proj, proj, conv_w, conv_b)


def _swap_halves(t):
    w = t.shape[-1]
    lane = lax.broadcasted_iota(jnp.int32, t.shape, 1)
    return jnp.where((lane % HEAD_DIM) < HEAD_DIM // 2, pltpu.roll(t, w - HEAD_DIM // 2, 1),
                     pltpu.roll(t, HEAD_DIM // 2, 1))


def _act_fwd(proj, cos_t, sin_t, expand, dt_bias_rep):
    lp = proj.shape[0]
    t = _tile(lp, (416, 128, 64))

    def body(q_ref, k_ref, v_ref, dt_ref, cos_ref, sin_ref, ex_ref, bias_ref, qo_ref, ko_ref, vo_ref, dto_ref):
        i = pl.program_id(0)
        cos = cos_ref[...]
        sin = sin_ref[...]
        q = q_ref[...]
        qo_ref[...] = (q * jnp.tile(cos, (1, D_ATT // 128)) + _swap_halves(q) * jnp.tile(sin, (1, D_ATT // 128))).astype(BF16)
        k = k_ref[...]
        ko_ref[...] = (k * jnp.tile(cos, (1, D_KV // 128)) + _swap_halves(k) * jnp.tile(sin, (1, D_KV // 128))).astype(BF16)
        vo_ref[...] = v_ref[...].astype(BF16)
        raw = _dot(dt_ref[...], ex_ref[...], NN, HIGHEST) + bias_ref[...]
        sp = jnp.maximum(raw, 0.0) + jnp.log1p(jnp.exp(-jnp.abs(raw)))
        row = i * t + lax.broadcasted_iota(jnp.int32, sp.shape, 0)
        dto_ref[...] = jnp.where(row >= PAD_LEAD, sp, 0.0)

    return pl.pallas_call(
        body, name="act_fwd", grid=(lp // t,),
        in_specs=[
            pl.BlockSpec((t, D_ATT), lambda i: (i, OFF_Q // D_ATT)),
            pl.BlockSpec((t, D_KV), lambda i: (i, OFF_K // D_KV)),
            pl.BlockSpec((t, D_KV), lambda i: (i, OFF_V // D_KV)),
            pl.BlockSpec((t, 128), lambda i: (i, OFF_DT // 128)),
            pl.BlockSpec((t, 128), lambda i: (i, 0)),
            pl.BlockSpec((t, 128), lambda i: (i, 0)),
            pl.BlockSpec((128, D_SSD), lambda i: (0, 0)),
            pl.BlockSpec((1, D_SSD), lambda i: (0, 0)),
        ],
        out_specs=[
            pl.BlockSpec((t, D_ATT), lambda i: (i, 0)),
            pl.BlockSpec((t, D_KV), lambda i: (i, 0)),
            pl.BlockSpec((t, D_KV), lambda i: (i, 0)),
            pl.BlockSpec((t, D_SSD), lambda i: (i, 0)),
        ],
        out_shape=[
            jax.ShapeDtypeStruct((lp, D_ATT), BF16),
            jax.ShapeDtypeStruct((lp, D_KV), BF16),
            jax.ShapeDtypeStruct((lp, D_KV), BF16),
            jax.ShapeDtypeStruct((lp, D_SSD), F32),
        ],
        compiler_params=_params(("parallel",), VMEM_LIMIT),
    )(proj, proj, proj, proj, cos_t, sin_t, expand, dt_bias_rep)


def _act_bwd(dqr, dkr, dv, dg, ddt_part, cos_t, sin_t, reduce_t):
    lp = dqr.shape[0]
    t = _tile(lp, (416, 128, 64))

    def body(dq_ref, dk_ref, dv_ref, dg_ref, ddt_ref, cos_ref, sin_ref, red_ref,
             qo_ref, go_ref, ko_ref, vo_ref, dto_ref, db_ref):
        i = pl.program_id(0)
        cos = cos_ref[...]
        sin = sin_ref[...]
        dq = dq_ref[...]
        qo_ref[...] = (dq * jnp.tile(cos, (1, D_ATT // 128)) + _swap_halves(dq * jnp.tile(sin, (1, D_ATT // 128)))).astype(BF16)
        dk = dk_ref[...]
        ko_ref[...] = (dk * jnp.tile(cos, (1, D_KV // 128)) + _swap_halves(dk * jnp.tile(sin, (1, D_KV // 128)))).astype(BF16)
        vo_ref[...] = dv_ref[...].astype(BF16)
        go_ref[...] = dg_ref[...].astype(BF16)
        ddt = _dot(ddt_ref[...], red_ref[...], NN, HIGHEST)
        dto_ref[...] = ddt.astype(BF16)

        @pl.when(i == 0)
        def _():
            db_ref[...] = jnp.zeros_like(db_ref)

        db_ref[...] += jnp.sum(ddt, axis=0, keepdims=True)

    return pl.pallas_call(
        body, name="act_bwd", grid=(lp // t,),
        in_specs=[
            pl.BlockSpec((t, D_ATT), lambda i: (i, 0)),
            pl.BlockSpec((t, D_KV), lambda i: (i, 0)),
            pl.BlockSpec((t, D_KV), lambda i: (i, 0)),
            pl.BlockSpec((t, D_ATT), lambda i: (i, 0)),
            pl.BlockSpec((t, D_SSD), lambda i: (i, 0)),
            pl.BlockSpec((t, 128), lambda i: (i, 0)),
            pl.BlockSpec((t, 128), lambda i: (i, 0)),
            pl.BlockSpec((D_SSD, 128), lambda i: (0, 0)),
        ],
        out_specs=[
            pl.BlockSpec((t, D_ATT), lambda i: (i, 0)),
            pl.BlockSpec((t, D_ATT), lambda i: (i, 0)),
            pl.BlockSpec((t, D_KV), lambda i: (i, 0)),
            pl.BlockSpec((t, D_KV), lambda i: (i, 0)),
            pl.BlockSpec((t, 128), lambda i: (i, 0)),
            pl.BlockSpec((1, 128), lambda i: (0, 0)),
        ],
        out_shape=[
            jax.ShapeDtypeStruct((lp, D_ATT), BF16),
            jax.ShapeDtypeStruct((lp, D_ATT), BF16),
            jax.ShapeDtypeStruct((lp, D_KV), BF16),
            jax.ShapeDtypeStruct((lp, D_KV), BF16),
            jax.ShapeDtypeStruct((lp, 128), BF16),
            jax.ShapeDtypeStruct((1, 128), F32),
        ],
        compiler_params=_params(("arbitrary",), VMEM_LIMIT),
    )(dqr, dkr, dv, dg, ddt_part, cos_t, sin_t, reduce_t)


def _ssd_common(xbc, dt, a, tri, sel):
    xs = xbc[:, :GROUP_W]
    b = xbc[:, GROUP_W:GROUP_W + D_STATE]
    c = xbc[:, GROUP_W + D_STATE:]
    cs = _dot(tri, dt * a, NN, HIGHEST)
    cs_t = _dot(sel, cs, NT, HIGHEST)
    xdt = xs * dt
    cs_last = cs[CHUNK - 1:CHUNK, :]
    return xs, b, c, cs, cs_t, xdt, jnp.exp(cs), jnp.exp(cs_last - cs), jnp.exp(cs_last)


def _decay_matrix(cs, cs_t, r, causal):
    seg = cs[:, r * HEAD_DIM:r * HEAD_DIM + 1] - cs_t[r:r + 1, :]
    return jnp.exp(jnp.where(causal, seg, -jnp.inf))


def _causal_mask():
    row = lax.broadcasted_iota(jnp.int32, (CHUNK, CHUNK), 0)
    col = lax.broadcasted_iota(jnp.int32, (CHUNK, CHUNK), 1)
    return row >= col


def _ssd_fwd(xbc, dt_rep, proj, a_rep, dsk_rep, wn, tri, sel):
    lp = xbc.shape[0]
    nc = lp // CHUNK

    def body(xbc_ref, dt_ref, z_ref, a_ref, dsk_ref, wn_ref, tri_ref, sel_ref, yn_ref, ytot_ref, hprev_ref, h_scr):
        @pl.when(pl.program_id(1) == 0)
        def _():
            h_scr[...] = jnp.zeros_like(h_scr)

        xs, b, c, cs, cs_t, xdt, e, dte, cd = _ssd_common(xbc_ref[...], dt_ref[...], a_ref[...], tri_ref[...], sel_ref[...])
        hprev = h_scr[...]
        hprev_ref[...] = hprev
        cb = _dot(c.astype(BF16), b.astype(BF16), NT)
        causal = _causal_mask()
        ydiag = []
        for r in range(HEADS_PER_GROUP):
            m = (cb * _decay_matrix(cs, cs_t, r, causal)).astype(BF16)
            ydiag.append(_dot(m, xdt[:, r * HEAD_DIM:(r + 1) * HEAD_DIM].astype(BF16)))
        ytot = jnp.concatenate(ydiag, axis=1) + _dot(c.astype(BF16), hprev.astype(BF16)) * e + dsk_ref[...] * xs
        h_scr[...] = hprev * cd + _dot(b.astype(BF16), (xdt * dte).astype(BF16), TN)
        ytot_ref[...] = ytot
        z = z_ref[...]
        gz = ytot * (z * _sigmoid(z))
        rstd = lax.rsqrt(jnp.mean(gz * gz, axis=-1, keepdims=True) + EPS)
        yn_ref[...] = (gz * rstd * wn_ref[...]).astype(BF16)

    vec = pl.BlockSpec((1, GROUP_W), lambda g, c: (0, g))
    return pl.pallas_call(
        body, name="ssd_fwd", grid=(N_GROUPS, nc),
        in_specs=[
            pl.BlockSpec((CHUNK, XBC_BLK), lambda g, c: (c, g)),
            pl.BlockSpec((CHUNK, GROUP_W), lambda g, c: (c, g)),
            pl.BlockSpec((CHUNK, GROUP_W), lambda g, c: (c, OFF_Z // GROUP_W + g)),
            vec, vec, vec,
            pl.BlockSpec((CHUNK, CHUNK), lambda g, c: (0, 0)),
            pl.BlockSpec((8, GROUP_W), lambda g, c: (0, 0)),
        ],
        out_specs=[
            pl.BlockSpec((CHUNK, GROUP_W), lambda g, c: (c, g)),
            pl.BlockSpec((CHUNK, GROUP_W), lambda g, c: (c, g)),
            pl.BlockSpec((None, None, D_STATE, GROUP_W), lambda g, c: (c, g, 0, 0)),
        ],
        out_shape=[
            jax.ShapeDtypeStruct((lp, D_SSD), BF16),
            jax.ShapeDtypeStruct((lp, D_SSD), F32),
            jax.ShapeDtypeStruct((nc, N_GROUPS, D_STATE, GROUP_W), F32),
        ],
        scratch_shapes=[pltpu.VMEM((D_STATE, GROUP_W), F32)],
        compiler_params=_params(("parallel", "arbitrary"), VMEM_LIMIT),
    )(xbc, dt_rep, proj, a_rep, dsk_rep, wn, tri, sel)


def _ssd_bwd(dmix, ytot, proj, xbc, dt_rep, hprev, a_rep, dsk_rep, wn, tri, sel):
    lp = xbc.shape[0]
    nc = lp // CHUNK

    def body(dyn_ref, ytot_ref, z_ref, xbc_ref, dt_ref, hprev_ref, a_ref, dsk_ref, wn_ref, tri_ref, sel_ref,
             dz_ref, dxbc_ref, ddt_ref, dd_ref, da_ref, dwn_ref, dh_scr):
        @pl.when(pl.program_id(1) == 0)
        def _():
            dh_scr[...] = jnp.zeros_like(dh_scr)
            dd_ref[...] = jnp.zeros_like(dd_ref)
            da_ref[...] = jnp.zeros_like(da_ref)
            dwn_ref[...] = jnp.zeros_like(dwn_ref)

        dt = dt_ref[...]
        a = a_ref[...]
        tri = tri_ref[...]
        xs, b, c, cs, cs_t, xdt, e, dte, cd = _ssd_common(xbc_ref[...], dt, a, tri, sel_ref[...])
        hprev = hprev_ref[...]
        ytot = ytot_ref[...]
        z = z_ref[...]
        dyn = dyn_ref[...]

        sz = _sigmoid(z)
        silu_z = z * sz
        gz = ytot * silu_z
        rstd = lax.rsqrt(jnp.mean(gz * gz, axis=-1, keepdims=True) + EPS)
        xhat = gz * rstd
        dwn_ref[...] += jnp.sum(dyn * xhat, axis=0, keepdims=True)
        dxhat = dyn * wn_ref[...]
        dgz = rstd * (dxhat - xhat * jnp.mean(dxhat * xhat, axis=-1, keepdims=True))
        dy = dgz * silu_z
        dz_ref[...] = (dgz * ytot * (sz * (1.0 + z * (1.0 - sz)))).astype(BF16)

        dd_ref[...] += jnp.sum(dy * xs, axis=0, keepdims=True)
        dxs = dsk_ref[...] * dy

        cbf = c.astype(BF16)
        bbf = b.astype(BF16)
        hbf = hprev.astype(BF16)
        g = _dot(cbf, hbf)
        dcs = dy * g * e
        dgb = (dy * e).astype(BF16)
        dc = _dot(dgb, hbf, NT)
        dhprev = _dot(cbf, dgb, TN)
        dhn = dh_scr[...]
        dhprev = dhprev + dhn * cd
        dcs_last = jnp.sum(dhn * hprev, axis=0, keepdims=True) * cd
        dhnb = dhn.astype(BF16)
        dxw = _dot(bbf, dhnb)
        db = _dot((xdt * dte).astype(BF16), dhnb, NT)
        dxdt = dxw * dte
        t_dte = dxw * xdt * dte
        dcs = dcs - t_dte
        dcs_last = dcs_last + jnp.sum(t_dte, axis=0, keepdims=True)
        cb = _dot(cbf, bbf, NT)
        causal = _causal_mask()
        ones = jnp.ones((CHUNK, HEAD_DIM), F32)
        dcb = jnp.zeros((CHUNK, CHUNK), F32)
        dxdt_diag = []
        dcs_diag = []
        for r in range(HEADS_PER_GROUP):
            sl = slice(r * HEAD_DIM, (r + 1) * HEAD_DIM)
            lm = _decay_matrix(cs, cs_t, r, causal)
            m = cb * lm
            dyr = dy[:, sl].astype(BF16)
            dm = _dot(dyr, xdt[:, sl].astype(BF16), NT)
            dxdt_diag.append(_dot(m.astype(BF16), dyr, TN))
            dcb = dcb + dm * lm
            dseg = dm * m
            rows = jnp.sum(dseg, axis=1, keepdims=True)
            cols = _dot(dseg, ones, TN, HIGHEST)
            dcs_diag.append((rows - cols) * (1.0 / HEAD_DIM))
        dxdt = dxdt + jnp.concatenate(dxdt_diag, axis=1)
        dcs = dcs + jnp.concatenate(dcs_diag, axis=1)
        dcbb = dcb.astype(BF16)
        dc = dc + _dot(dcbb, bbf)
        db = db + _dot(dcbb, cbf, TN)
        row = lax.broadcasted_iota(jnp.int32, dcs.shape, 0)
        dcs = dcs + jnp.where(row == CHUNK - 1, dcs_last, 0.0)
        dda = _dot(tri, dcs, TN, HIGHEST)
        da_ref[...] += jnp.sum(dda * dt, axis=0, keepdims=True)
        ddt = dda * a + dxdt * xs
        dxs = dxs + dxdt * dt
        ddt_ref[...] = ddt * (1.0 - jnp.exp(-dt))
        dxbc_ref[...] = jnp.concatenate([dxs, db, dc], axis=1)
        dh_scr[...] = dhprev

    rev = lambda c: nc - 1 - c
    vec = pl.BlockSpec((1, GROUP_W), lambda g, c: (0, g))
    blk = pl.BlockSpec((CHUNK, GROUP_W), lambda g, c: (rev(c), g))
    return pl.pallas_call(
        body, name="ssd_bwd", grid=(N_GROUPS, nc),
        in_specs=[
            blk, blk,
            pl.BlockSpec((CHUNK, GROUP_W), lambda g, c: (rev(c), OFF_Z // GROUP_W + g)),
            pl.BlockSpec((CHUNK, XBC_BLK), lambda g, c: (rev(c), g)),
            blk,
            pl.BlockSpec((None, None, D_STATE, GROUP_W), lambda g, c: (rev(c), g, 0, 0)),
            vec, vec, vec,
            pl.BlockSpec((CHUNK, CHUNK), lambda g, c: (0, 0)),
            pl.BlockSpec((8, GROUP_W), lambda g, c: (0, 0)),
        ],
        out_specs=[
            blk,
            pl.BlockSpec((CHUNK, XBC_BLK), lambda g, c: (rev(c), g)),
            blk, vec, vec, vec,
        ],
        out_shape=[
            jax.ShapeDtypeStruct((lp, D_SSD), BF16),
            jax.ShapeDtypeStruct((lp, D_CONV), F32),
            jax.ShapeDtypeStruct((lp, D_SSD), F32),
            jax.ShapeDtypeStruct((1, D_SSD), F32),
            jax.ShapeDtypeStruct((1, D_SSD), F32),
            jax.ShapeDtypeStruct((1, D_SSD), F32),
        ],
        scratch_shapes=[pltpu.VMEM((D_STATE, GROUP_W), F32)],
        compiler_params=_params(("parallel", "arbitrary"), VMEM_LIMIT),
    )(dmix, ytot, proj, xbc, dt_rep, hprev, a_rep, dsk_rep, wn, tri, sel)


def _stack_heads(t, h):
    return jnp.concatenate([t[:, (REP * h + r) * HEAD_DIM:(REP * h + r + 1) * HEAD_DIM] for r in range(REP)], axis=0)


def _band(t2, t1, t0, h):
    sl = slice(h * HEAD_DIM, (h + 1) * HEAD_DIM)
    return jnp.concatenate([t2[:, sl], t1[:, sl], t0[:, sl]], axis=0)


def _attn_probs(qs, kb, sink, qc):
    s = _dot(qs, kb, NT) * (HEAD_DIM ** -0.5)
    key_abs = (qc - WINDOW_CHUNKS) * CHUNK + lax.broadcasted_iota(jnp.int32, s.shape, 1)
    s = jnp.where(key_abs >= PAD_LEAD, s, -jnp.inf)
    m = jnp.maximum(jnp.max(s, axis=-1, keepdims=True), sink)
    p = jnp.exp(s - m)
    ps = jnp.exp(sink - m)
    denom = jnp.sum(p, axis=-1, keepdims=True) + ps
    return p / denom, ps / denom


def _kv_specs(width, chunk_of):
    return [pl.BlockSpec((CHUNK, width), functools.partial(lambda j, c: (jnp.maximum(chunk_of(c) - j, 0), 0), j))
            for j in (2, 1, 0)]


def _attn_fwd(qr, kr, vb, proj, sink_stack):
    lp = qr.shape[0]
    nc = lp // CHUNK

    def body(q_ref, k2_ref, k1_ref, k0_ref, v2_ref, v1_ref, v0_ref, g_ref, sink_ref, att_ref, mix_ref):
        qc = pl.program_id(0)
        q = q_ref[...]
        ks = (k2_ref[...], k1_ref[...], k0_ref[...])
        vs = (v2_ref[...], v1_ref[...], v0_ref[...])
        cols = [None] * Q_HEADS
        for h in range(KV_HEADS):
            pn, _ = _attn_probs(_stack_heads(q, h), _band(*ks, h), sink_ref[h], qc)
            o = _dot(pn.astype(BF16), _band(*vs, h))
            for r in range(REP):
                cols[REP * h + r] = o[r * CHUNK:(r + 1) * CHUNK, :]
        att = jnp.concatenate(cols, axis=1)
        att_ref[...] = att
        g = g_ref[...]
        mix_ref[...] = (att * (g * _sigmoid(g))).astype(BF16)

    here = lambda c: c
    return pl.pallas_call(
        body, name="attn_fwd", grid=(nc,),
        in_specs=[pl.BlockSpec((CHUNK, D_ATT), lambda c: (c, 0))] + _kv_specs(D_KV, here) + _kv_specs(D_KV, here) + [
            pl.BlockSpec((CHUNK, D_ATT), lambda c: (c, OFF_G // D_ATT)),
            pl.BlockSpec((KV_HEADS, REP * CHUNK, 1), lambda c: (0, 0, 0)),
        ],
        out_specs=[pl.BlockSpec((CHUNK, D_ATT), lambda c: (c, 0)), pl.BlockSpec((CHUNK, D_ATT), lambda c: (c, 0))],
        out_shape=[jax.ShapeDtypeStruct((lp, D_ATT), F32), jax.ShapeDtypeStruct((lp, D_ATT), BF16)],
        compiler_params=_params(("parallel",), VMEM_LIMIT),
    )(qr, kr, kr, kr, vb, vb, vb, proj, sink_stack)


def _attn_bwd(qr, kr, vb, att, proj, dmix, sink_stack):
    lp = qr.shape[0]
    nc = lp // CHUNK

    def body(q_ref, k2_ref, k1_ref, k0_ref, v2_ref, v1_ref, v0_ref, att_ref, g_ref, do_ref, sink_ref,
             dq_ref, dk_ref, dv_ref, dg_ref, dsink_ref, dk_acc, dv_acc):
        step = pl.program_id(0)

        @pl.when(step == 0)
        def _():
            dk_acc[...] = jnp.zeros_like(dk_acc)
            dv_acc[...] = jnp.zeros_like(dv_acc)
            dsink_ref[...] = jnp.zeros_like(dsink_ref)

        @pl.when(step < nc)
        def _():
            q = q_ref[...]
            ks = (k2_ref[...], k1_ref[...], k0_ref[...])
            vs = (v2_ref[...], v1_ref[...], v0_ref[...])
            att = att_ref[...]
            g = g_ref[...]
            dog = do_ref[...]
            sg = _sigmoid(g)
            dg_ref[...] = dog * att * (sg * (1.0 + g * (1.0 - sg)))
            do = dog * (g * sg)
            dq_cols = [None] * Q_HEADS
            for h in range(KV_HEADS):
                sl = slice(h * HEAD_DIM, (h + 1) * HEAD_DIM)
                qs = _stack_heads(q, h)
                kb = _band(*ks, h)
                vbh = _band(*vs, h)
                sink = sink_ref[h]
                pn, psink = _attn_probs(qs, kb, sink, step)
                dos = _stack_heads(do, h)
                delta = jnp.sum(dos * _stack_heads(att, h), axis=-1, keepdims=True)
                dosb = dos.astype(BF16)
                dp = _dot(dosb, vbh, NT)
                ds = (pn * (dp - delta)).astype(BF16)
                dsink_ref[h] += -(psink * delta)
                dqs = _dot(ds, kb) * (HEAD_DIM ** -0.5)
                for r in range(REP):
                    dq_cols[REP * h + r] = dqs[r * CHUNK:(r + 1) * CHUNK, :]
                dk_acc[:, sl] += _dot(ds, qs, TN) * (HEAD_DIM ** -0.5)
                dv_acc[:, sl] += _dot(pn.astype(BF16), dosb, TN)
            dq_ref[...] = jnp.concatenate(dq_cols, axis=1)

        dk_ref[...] = dk_acc[0:CHUNK, :]
        dv_ref[...] = dv_acc[0:CHUNK, :]
        for acc in (dk_acc, dv_acc):
            rest = acc[CHUNK:BAND, :]
            acc[0:BAND - CHUNK, :] = rest
            acc[BAND - CHUNK:BAND, :] = jnp.zeros((CHUNK, D_KV), F32)

    qc = lambda c: jnp.minimum(c, nc - 1)
    qblk = pl.BlockSpec((CHUNK, D_ATT), lambda c: (qc(c), 0))
    oldest = pl.BlockSpec((CHUNK, D_KV), lambda c: (jnp.maximum(c - WINDOW_CHUNKS, 0), 0))
    return pl.pallas_call(
        body, name="attn_bwd", grid=(nc + WINDOW_CHUNKS,),
        in_specs=[qblk] + _kv_specs(D_KV, qc) + _kv_specs(D_KV, qc) + [
            qblk,
            pl.BlockSpec((CHUNK, D_ATT), lambda c: (qc(c), OFF_G // D_ATT)),
            pl.BlockSpec((CHUNK, D_ATT), lambda c: (qc(c), D_SSD // D_ATT)),
            pl.BlockSpec((KV_HEADS, REP * CHUNK, 1), lambda c: (0, 0, 0)),
        ],
        out_specs=[qblk, oldest, oldest, qblk, pl.BlockSpec((KV_HEADS, REP * CHUNK, 1), lambda c: (0, 0, 0))],
        out_shape=[
            jax.ShapeDtypeStruct((lp, D_ATT), F32),
            jax.ShapeDtypeStruct((lp, D_KV), F32),
            jax.ShapeDtypeStruct((lp, D_KV), F32),
            jax.ShapeDtypeStruct((lp, D_ATT), F32),
            jax.ShapeDtypeStruct((KV_HEADS, REP * CHUNK, 1), F32),
        ],
        scratch_shapes=[pltpu.VMEM((BAND, D_KV), F32), pltpu.VMEM((BAND, D_KV), F32)],
        compiler_params=_params(("arbitrary",), VMEM_LIMIT),
    )(qr, kr, kr, kr, vb, vb, vb, att, proj, dmix, sink_stack)


def _post_loss(out, h_pad, target, w):
    lp = out.shape[0]
    nc = lp // CHUNK

    def body(o_ref, h_ref, t_ref, w_ref, do_ref, dy_ref, gw_ref, loss_ref):
        i = pl.program_id(0)

        @pl.when(i == 0)
        def _():
            gw_ref[...] = jnp.zeros_like(gw_ref)
            loss_ref[...] = jnp.zeros_like(loss_ref)

        o = o_ref[...]
        w = w_ref[...]
        rstd = lax.rsqrt(jnp.mean(o * o, axis=-1, keepdims=True) + EPS)
        xhat = o * rstd
        err = (h_ref[...] + xhat * w - t_ref[...]) * (i > 0).astype(F32)
        loss_ref[...] += 0.5 * jnp.sum(jnp.mean(err * err, axis=-1, keepdims=True), axis=0, keepdims=True)
        dy = err * (1.0 / D_MODEL)
        dy_ref[...] = dy
        gw_ref[...] += jnp.sum(dy * xhat, axis=0, keepdims=True)
        dxhat = dy * w
        do_ref[...] = (rstd * (dxhat - xhat * jnp.mean(dxhat * xhat, axis=-1, keepdims=True))).astype(BF16)

    row = pl.BlockSpec((CHUNK, D_MODEL), lambda i: (i, 0))
    return pl.pallas_call(
        body, name="post_loss", grid=(nc,),
        in_specs=[row, row, pl.BlockSpec((CHUNK, D_MODEL), lambda i: (jnp.maximum(i - 1, 0), 0)),
                  pl.BlockSpec((1, D_MODEL), lambda i: (0, 0))],
        out_specs=[row, row, pl.BlockSpec((1, D_MODEL), lambda i: (0, 0)), pl.BlockSpec((1, 128), lambda i: (0, 0))],
        out_shape=[
            jax.ShapeDtypeStruct((lp, D_MODEL), BF16),
            jax.ShapeDtypeStruct((lp, D_MODEL), F32),
            jax.ShapeDtypeStruct((1, D_MODEL), F32),
            jax.ShapeDtypeStruct((1, 128), F32),
        ],
        compiler_params=_params(("arbitrary",), VMEM_LIMIT),
    )(out, h_pad, target, w)


def _prenorm_bwd(dhn, h_pad, dy, w):
    lp = dhn.shape[0]
    nc = lp // CHUNK

    def body(dhn_ref, h_ref, dy_ref, w_ref, gx_ref, d0_ref, gw_ref):
        i = pl.program_id(0)

        @pl.when(i == 0)
        def _():
            gw_ref[...] = jnp.zeros_like(gw_ref)

        h = h_ref[...]
        dhn = dhn_ref[...]
        rstd = lax.rsqrt(jnp.mean(h * h, axis=-1, keepdims=True) + EPS)
        xhat = h * rstd
        gw_ref[...] += jnp.sum(dhn * xhat, axis=0, keepdims=True)
        dxhat = dhn * w_ref[...]
        dh = rstd * (dxhat - xhat * jnp.mean(dxhat * xhat, axis=-1, keepdims=True)) + dy_ref[...]

        @pl.when(i == 0)
        def _():
            d0_ref[...] = dh

        @pl.when(i > 0)
        def _():
            gx_ref[...] = dh

    row = pl.BlockSpec((CHUNK, D_MODEL), lambda i: (i, 0))
    return pl.pallas_call(
        body, name="prenorm_bwd", grid=(nc,),
        in_specs=[row, row, row, pl.BlockSpec((1, D_MODEL), lambda i: (0, 0))],
        out_specs=[pl.BlockSpec((CHUNK, D_MODEL), lambda i: (jnp.maximum(i - 1, 0), 0)),
                   pl.BlockSpec((CHUNK, D_MODEL), lambda i: (0, 0)),
                   pl.BlockSpec((1, D_MODEL), lambda i: (0, 0))],
        out_shape=[
            jax.ShapeDtypeStruct((lp - CHUNK, D_MODEL), F32),
            jax.ShapeDtypeStruct((CHUNK, D_MODEL), F32),
            jax.ShapeDtypeStruct((1, D_MODEL), F32),
        ],
        compiler_params=_params(("arbitrary",), VMEM_LIMIT),
    )(dhn, h_pad, dy, w)


def _exchange(items, name):
    n = len(items)

    def body(*refs):
        ins, outs = refs[:n], refs[n:2 * n]
        send_sems, recv_sems, local_sems = refs[2 * n:]
        pos = (lax.axis_index("x"), lax.axis_index("y"), lax.axis_index("c"))
        me = 4 * pos[0] + 2 * pos[1] + pos[2]
        peers = []
        for k in range(1, N_DEV):
            p = tuple(1 - pos[b] if (k >> (2 - b)) & 1 else pos[b] for b in range(3))
            peers.append((p, 4 * p[0] + 2 * p[1] + p[2]))
        started = []
        for a, (_, kind) in enumerate(items):
            own = ins[a] if kind == "gather" else ins[a].at[me]
            loc = pltpu.make_async_copy(own, outs[a].at[me], local_sems.at[a])
            loc.start()
            started.append(loc)
            for k, (p, pid) in enumerate(peers):
                cp = pltpu.make_async_remote_copy(
                    src_ref=ins[a] if kind == "gather" else ins[a].at[pid], dst_ref=outs[a].at[me],
                    send_sem=send_sems.at[a, k], recv_sem=recv_sems.at[a, k],
                    device_id=p, device_id_type=pl.DeviceIdType.MESH)
                cp.start()
        for a, (_, kind) in enumerate(items):
            for k, (p, pid) in enumerate(peers):
                pltpu.make_async_remote_copy(
                    src_ref=ins[a] if kind == "gather" else ins[a].at[pid], dst_ref=outs[a].at[pid],
                    send_sem=send_sems.at[a, k], recv_sem=recv_sems.at[a, k],
                    device_id=p, device_id_type=pl.DeviceIdType.MESH).wait()
        for loc in started:
            loc.wait()

    def slab_shape(arr, kind):
        return (N_DEV,) + tuple(arr.shape) if kind == "gather" else tuple(arr.shape)

    any_spec = pl.BlockSpec(memory_space=pl.ANY)
    return pl.pallas_call(
        body, name=name,
        in_specs=[any_spec] * n, out_specs=[any_spec] * n,
        out_shape=[jax.ShapeDtypeStruct(slab_shape(arr, kind), arr.dtype) for arr, kind in items],
        scratch_shapes=[pltpu.SemaphoreType.DMA((n, N_DEV - 1)), pltpu.SemaphoreType.DMA((n, N_DEV - 1)),
                        pltpu.SemaphoreType.DMA((n,))],
        compiler_params=pltpu.CompilerParams(has_side_effects=True),
    )(*[arr for arr, _ in items])


def _adamw(slabs, w, m, v, name):
    rows, cols = w.shape
    tr = _tile(rows, (256, 128, 64, 16, 8))
    c1 = 1.0 - ADAM_B1 ** ADAM_STEP
    c2 = 1.0 - ADAM_B2 ** ADAM_STEP

    def body(s_ref, w_ref, m_ref, v_ref, g_ref, d_ref, mo_ref, vo_ref):
        g = s_ref[0].astype(F32)
        for k in range(1, N_DEV):
            g = g + s_ref[k].astype(F32)
        w = w_ref[...]
        m = ADAM_B1 * m_ref[...] + (1.0 - ADAM_B1) * g
        v = ADAM_B2 * v_ref[...] + (1.0 - ADAM_B2) * (g * g)
        g_ref[...] = g
        mo_ref[...] = m
        vo_ref[...] = v
        d_ref[...] = -ADAM_LR * ((m / c1) / (jnp.sqrt(v / c2) + ADAM_EPS) + ADAM_WD * w)

    blk = pl.BlockSpec((tr, cols), lambda i: (i, 0))
    shape = jax.ShapeDtypeStruct((rows, cols), F32)
    return pl.pallas_call(
        body, name=name, grid=(rows // tr,),
        in_specs=[pl.BlockSpec((N_DEV, tr, cols), lambda i: (0, i, 0)), blk, blk, blk],
        out_specs=[blk, blk, blk, blk],
        out_shape=[shape, shape, shape, shape],
        compiler_params=_params(("parallel",), VMEM_LIMIT),
    )(slabs, w, m, v)


def _perm_xbc(a):
    lead = a.shape[:-1]
    xs = a[..., :D_SSD].reshape(lead + (N_GROUPS, GROUP_W))
    b = a[..., D_SSD:D_SSD + N_GROUPS * D_STATE].reshape(lead + (N_GROUPS, D_STATE))
    c = a[..., D_SSD + N_GROUPS * D_STATE:].reshape(lead + (N_GROUPS, D_STATE))
    return jnp.concatenate([xs, b, c], axis=-1).reshape(lead + (D_CONV,))


def _unperm_xbc(a):
    lead = a.shape[:-1]
    t = a.reshape(lead + (N_GROUPS, XBC_BLK))
    xs = t[..., :GROUP_W].reshape(lead + (D_SSD,))
    b = t[..., GROUP_W:GROUP_W + D_STATE].reshape(lead + (N_GROUPS * D_STATE,))
    c = t[..., GROUP_W + D_STATE:].reshape(lead + (N_GROUPS * D_STATE,))
    return jnp.concatenate([xs, b, c], axis=-1)


R_Z, R_XBC, R_DT, R_Q, R_K, R_V, R_G = 0, 2048, 6144, 6176, 7200, 7456, 7712


def _w_in_internal(w_gathered):
    w = jnp.transpose(w_gathered, (1, 0, 2)).reshape(D_MODEL, D_IN_PROJ)
    pad = jnp.zeros((D_MODEL, NP - OFF_DT - SSD_HEADS), w.dtype)
    return jnp.concatenate([
        w[:, R_Z:R_XBC], _perm_xbc(w[:, R_XBC:R_DT]), w[:, R_Q:R_K], w[:, R_G:], w[:, R_K:R_V], w[:, R_V:R_G],
        w[:, R_DT:R_Q], pad], axis=1)


def _w_in_slabs(dw):
    w = jnp.concatenate([
        dw[:, OFF_Z:OFF_XBC], _unperm_xbc(dw[:, OFF_XBC:OFF_Q]), dw[:, OFF_DT:OFF_DT + SSD_HEADS],
        dw[:, OFF_Q:OFF_G], dw[:, OFF_K:OFF_V], dw[:, OFF_V:OFF_DT], dw[:, OFF_G:OFF_K]], axis=1)
    return jnp.transpose(w.reshape(D_MODEL, N_DEV, SHARD_IN), (1, 0, 2))


def _rep_heads(a):
    return jnp.repeat(a, HEAD_DIM, axis=1)


SMALL = (("norm_pre_w", 2048), ("conv_b", 4096), ("dt_bias", 32), ("a_log", 32), ("d_skip", 32),
         ("ssd_norm_w", 2048), ("attn_sinks", 16), ("norm_post_w", 2048))
SMALL_LEN = 10368


def _pack_small(d):
    parts = [d[name].reshape(1, size) for name, size in SMALL]
    used = sum(size for _, size in SMALL)
    return jnp.concatenate(parts + [jnp.zeros((1, SMALL_LEN - used), F32)], axis=1)


def _unpack_small(vec):
    out, off = {}, 0
    for name, size in SMALL:
        out[name] = vec[:, off:off + size]
        off += size
    return out


def kernel(x, meta_tokens, norm_pre_w, w_in, conv_w, conv_b, dt_bias, a_log, d_skip, ssd_norm_w, attn_sinks, w_out, norm_post_w, loss_target, m_meta_tokens, m_norm_pre_w, m_w_in, m_conv_w, m_conv_b, m_dt_bias, m_a_log, m_d_skip, m_ssd_norm_w, m_attn_sinks, m_w_out, m_norm_post_w, v_meta_tokens, v_norm_pre_w, v_w_in, v_conv_w, v_conv_b, v_dt_bias, v_a_log, v_d_skip, v_ssd_norm_w, v_attn_sinks, v_w_out, v_norm_post_w):
    seq = x.shape[1]
    lp = seq + CHUNK
    nc = lp // CHUNK

    w_in_g, w_out_g, conv_w_g, meta_g = _exchange(
        [(w_in[0].astype(BF16), "gather"), (w_out[0].astype(BF16), "gather"),
         (conv_w[0], "gather"), (meta_tokens, "gather")], "gather_weights")
    w_all = _w_in_internal(w_in_g)
    w_out_full = w_out_g.reshape(D_MIX, D_MODEL)
    conv_w_full = _perm_xbc(jnp.transpose(conv_w_g, (1, 0, 2)).reshape(CONV_WIDTH, D_CONV))
    conv_b_int = _perm_xbc(conv_b)
    meta_full = jnp.transpose(meta_g, (1, 0, 2)).reshape(N_META, D_MODEL)

    pos = (jnp.arange(lp) - PAD_LEAD).astype(F32)
    half = HEAD_DIM // 2
    inv = ROPE_THETA ** (-jnp.arange(half, dtype=F32) / half)
    ang = pos[:, None] * inv[None, :]
    cos_t = jnp.tile(jnp.cos(ang), (1, 4))
    sin_t = jnp.tile(jnp.concatenate([-jnp.sin(ang), jnp.sin(ang)], axis=1), (1, 2))
    head_of_col = np.arange(D_SSD) // HEAD_DIM
    expand = jnp.asarray((np.arange(128)[:, None] == head_of_col[None, :]).astype(np.float32))
    reduce_t = jnp.asarray((head_of_col[:, None] == np.arange(128)[None, :]).astype(np.float32))
    tri = jnp.asarray(np.tril(np.ones((CHUNK, CHUNK), np.float32)))
    sel = jnp.asarray((np.arange(8)[:, None] * HEAD_DIM == np.arange(GROUP_W)[None, :]).astype(np.float32))
    a_rep = _rep_heads(-jnp.exp(a_log))
    dsk_rep = _rep_heads(d_skip)
    dt_bias_rep = _rep_heads(dt_bias)
    sink_stack = jnp.repeat(attn_sinks.reshape(KV_HEADS, REP), CHUNK, axis=1).reshape(KV_HEADS, REP * CHUNK, 1)

    h_pad = jnp.concatenate([jnp.zeros((PAD_LEAD, D_MODEL), F32), meta_full, x[0]], axis=0)
    hn = _prenorm(h_pad, norm_pre_w)
    tm = _tile(lp, (832, 416, 128, 64))
    proj = _matmul(hn, w_all, trans_b=False, tm=tm, tn=1792, tk=D_MODEL, out_dtype=F32, name="in_proj")
    xbc = _conv_fwd(proj, conv_w_full, conv_b_int)
    qr, kr, vb, dt_rep = _act_fwd(proj, cos_t, sin_t, expand, dt_bias_rep)
    yn, ytot, hprev = _ssd_fwd(xbc, dt_rep, proj, a_rep, dsk_rep, ssd_norm_w, tri, sel)
    att, attg = _attn_fwd(qr, kr, vb, proj, sink_stack)
    mix = jnp.concatenate([yn, attg], axis=1)
    out = _matmul(mix, w_out_full, trans_b=False, tm=tm, tn=1024, tk=D_MIX, out_dtype=F32, name="out_proj")
    dout, dy, g_norm_post, loss_part = _post_loss(out, h_pad, loss_target[0], norm_post_w)

    dmix = _matmul(dout, w_out_full, trans_b=True, tm=tm, tn=1024, tk=D_MODEL, out_dtype=F32, name="dmix")
    dw_out = _matmul(mix.T, dout, trans_b=False, tm=512, tn=1024, tk=lp, out_dtype=F32, name="dw_out")
    dqr, dkr, dv, dg, dsink_rows = _attn_bwd(qr, kr, vb, att, proj, dmix, sink_stack)
    dz, dxbc, ddt_part, dd_part, da_part, g_ssd_norm = _ssd_bwd(
        dmix, ytot, proj, xbc, dt_rep, hprev, a_rep, dsk_rep, ssd_norm_w, tri, sel)
    du, dconv_w_int, dconv_b_int = _conv_bwd(dxbc, proj, conv_w_full, conv_b_int)
    dq, dgate, dk, dvb, ddt_raw, ddt_bias = _act_bwd(dqr, dkr, dv, dg, ddt_part, cos_t, sin_t, reduce_t)
    dproj = jnp.concatenate([dz, du, dq, dgate, dk, dvb, ddt_raw, jnp.zeros((lp, NP - OFF_DT - 128), BF16)], axis=1)
    dhn = _matmul(dproj, w_all, trans_b=True, tm=tm, tn=1024, tk=1792, out_dtype=F32, name="dhn")
    dw_all = _matmul(hn.T, dproj, trans_b=False, tm=512, tn=1280, tk=lp, out_dtype=F32, name="dw_in")
    grad_x, dh0, g_norm_pre = _prenorm_bwd(dhn, h_pad, dy, norm_pre_w)

    dmeta = dh0[PAD_LEAD:, :]
    dconv_w_ref = _unperm_xbc(dconv_w_int)
    heads = lambda part: part.reshape(SSD_HEADS, HEAD_DIM).sum(axis=1).reshape(1, SSD_HEADS)
    small_local = _pack_small({
        "norm_pre_w": g_norm_pre, "conv_b": _unperm_xbc(dconv_b_int), "dt_bias": ddt_bias[:, :SSD_HEADS],
        "a_log": heads(da_part) * (-jnp.exp(a_log)), "d_skip": heads(dd_part), "ssd_norm_w": g_ssd_norm,
        "attn_sinks": dsink_rows.reshape(Q_HEADS, CHUNK).sum(axis=1).reshape(1, Q_HEADS),
        "norm_post_w": g_norm_post})

    g_in, g_out, g_conv, g_meta, g_small = _exchange(
        [(_w_in_slabs(dw_all).astype(BF16), "scatter"),
         (dw_out.reshape(N_DEV, D_MIX // N_DEV, D_MODEL).astype(BF16), "scatter"),
         (jnp.transpose(dconv_w_ref.reshape(CONV_WIDTH, N_DEV, D_CONV // N_DEV), (1, 0, 2)), "scatter"),
         (jnp.transpose(dmeta.reshape(N_META, N_DEV, D_MODEL // N_DEV), (1, 0, 2)), "scatter"),
         (small_local, "gather")], "exchange_grads")

    res = {}
    res["w_in"] = _adamw(g_in, w_in[0], m_w_in[0], v_w_in[0], "adamw_w_in")
    res["w_out"] = _adamw(g_out, w_out[0], m_w_out[0], v_w_out[0], "adamw_w_out")
    res["conv_w"] = _adamw(g_conv, conv_w[0], m_conv_w[0], v_conv_w[0], "adamw_conv_w")
    res["meta_tokens"] = _adamw(g_meta, meta_tokens, m_meta_tokens, v_meta_tokens, "adamw_meta")
    given = dict(norm_pre_w=(norm_pre_w, m_norm_pre_w, v_norm_pre_w), conv_b=(conv_b, m_conv_b, v_conv_b),
                 dt_bias=(dt_bias, m_dt_bias, v_dt_bias), a_log=(a_log, m_a_log, v_a_log),
                 d_skip=(d_skip, m_d_skip, v_d_skip), ssd_norm_w=(ssd_norm_w, m_ssd_norm_w, v_ssd_norm_w),
                 attn_sinks=(attn_sinks, m_attn_sinks, v_attn_sinks),
                 norm_post_w=(norm_post_w, m_norm_post_w, v_norm_post_w))
    packed = [_pack_small({k: t[j] for k, t in given.items()}) for j in range(3)]
    small_res = [_unpack_small(r) for r in _adamw(g_small, packed[0], packed[1], packed[2], "adamw_small")]

    loss = lax.psum(loss_part[0, 0], ("x", "y", "c"))
    order = ["meta_tokens", "norm_pre_w", "w_in", "conv_w", "conv_b", "dt_bias", "a_log", "d_skip", "ssd_norm_w",
             "attn_sinks", "w_out", "norm_post_w"]
    lead = {"w_in": True, "conv_w": True, "w_out": True}
    outs = []
    for j in range(4):
        for name in order:
            if name in res:
                o = res[name][j]
                outs.append(o[None] if name in lead else o)
            else:
                outs.append(small_res[j][name])
    return (loss, grad_x[None], *outs)
```

```python
import functools

import numpy as np
import jax
import jax.numpy as jnp
from jax import lax
from jax.experimental import pallas as pl
from jax.experimental.pallas import tpu as pltpu

F32 = jnp.float32
BF16 = jnp.bfloat16
HIGHEST = lax.Precision.HIGHEST

N_DEV = 8
D_MODEL = 2048
CHUNK = 64
N_META = 16
PAD_LEAD = CHUNK - N_META
EPS = 1e-6
N_GROUPS = 8
HEADS_PER_GROUP = 4
HEAD_DIM = 64
GROUP_W = HEADS_PER_GROUP * HEAD_DIM
D_STATE = 128
D_SSD = 2048
D_CONV = 4096
SSD_HEADS = 32
CONV_WIDTH = 4
Q_HEADS = 16
KV_HEADS = 4
REP = 4
D_ATT = 1024
D_KV = 256
WINDOW_CHUNKS = 2
BAND = (WINDOW_CHUNKS + 1) * CHUNK
ROPE_THETA = 10000.0
D_MIX = D_SSD + D_ATT
D_IN_PROJ = 8736
SHARD_IN = D_IN_PROJ // N_DEV

OFF_Z, OFF_XBC, OFF_Q, OFF_G, OFF_K, OFF_V, OFF_DT = 0, 2048, 6144, 7168, 8192, 8448, 8704
NP = 9216
TAIL_W = NP - OFF_Q
XBC_BLK = 512
SSD_GPS = 2

ADAM_LR, ADAM_B1, ADAM_B2, ADAM_EPS, ADAM_WD, ADAM_STEP = 0.001, 0.9, 0.999, 1e-08, 0.01, 10

VMEM_LIMIT = 48 * 1024 * 1024

NN = (((1,), (0,)), ((), ()))
NT = (((1,), (1,)), ((), ()))
TN = (((0,), (0,)), ((), ()))
ANY = pl.BlockSpec(memory_space=pl.ANY)


def _dot(a, b, dims=NN, precision=None):
    return lax.dot_general(a, b, dims, precision=precision, preferred_element_type=F32)


def _tile(n, prefs):
    for t in prefs:
        if n % t == 0:
            return t
    return n


def _params(sem, vmem=VMEM_LIMIT, side_effects=False):
    return pltpu.CompilerParams(dimension_semantics=sem, vmem_limit_bytes=vmem, has_side_effects=side_effects)


def _sigmoid(x):
    return 1.0 / (1.0 + jnp.exp(-x))


class _Comm:
    def __init__(self, items):
        self.items = items
        self.n = n = len(items)
        self.operands = [arr for arr, _ in items]
        self.in_specs = [ANY] * n
        self.out_specs = [ANY] * n
        self.out_shape = [jax.ShapeDtypeStruct((N_DEV,) + tuple(arr.shape) if kind == "gather" else tuple(arr.shape),
                                               arr.dtype) for arr, kind in items]
        self.scratch = [pltpu.SemaphoreType.DMA((n, N_DEV - 1)), pltpu.SemaphoreType.DMA((n, N_DEV - 1)),
                        pltpu.SemaphoreType.DMA((n,))]

    @staticmethod
    def _places():
        pos = (lax.axis_index("x"), lax.axis_index("y"), lax.axis_index("c"))
        me = 4 * pos[0] + 2 * pos[1] + pos[2]
        peers = []
        for k in range(1, N_DEV):
            p = tuple(1 - pos[b] if (k >> (2 - b)) & 1 else pos[b] for b in range(3))
            peers.append((p, 4 * p[0] + 2 * p[1] + p[2]))
        return me, peers

    def _copies(self, ins, outs, sems, landed):
        send_sems, recv_sems, local_sems = sems
        me, peers = self._places()
        local, remote = [], []
        for a, (_, kind) in enumerate(self.items):
            own = ins[a] if kind == "gather" else ins[a].at[me]
            local.append(pltpu.make_async_copy(own, outs[a].at[me], local_sems.at[a]))
            for k, (p, pid) in enumerate(peers):
                remote.append(pltpu.make_async_remote_copy(
                    src_ref=ins[a] if kind == "gather" else ins[a].at[pid],
                    dst_ref=outs[a].at[pid if landed else me],
                    send_sem=send_sems.at[a, k], recv_sem=recv_sems.at[a, k],
                    device_id=p, device_id_type=pl.DeviceIdType.MESH))
        return local, remote

    def start(self, ins, outs, sems):
        local, remote = self._copies(ins, outs, sems, landed=False)
        for cp in local + remote:
            cp.start()

    def wait(self, ins, outs, sems):
        local, remote = self._copies(ins, outs, sems, landed=True)
        for cp in remote + local:
            cp.wait()


def _exchange(items, name):
    comm = _Comm(items)
    n = comm.n

    def body(*refs):
        ins, outs, sems = refs[:n], refs[n:2 * n], refs[2 * n:]
        comm.start(ins, outs, sems)
        comm.wait(ins, outs, sems)

    return pl.pallas_call(
        body, name=name, in_specs=comm.in_specs, out_specs=comm.out_specs, out_shape=comm.out_shape,
        scratch_shapes=comm.scratch, compiler_params=pltpu.CompilerParams(has_side_effects=True),
    )(*comm.operands)


def _gather_two_level(arrays, name):
    n = len(arrays)

    def body(*refs):
        ins, outs = refs[:n], refs[n:2 * n]
        send_sems, recv_sems, local_sems = refs[2 * n:]
        x, y, c = lax.axis_index("x"), lax.axis_index("y"), lax.axis_index("c")
        me, sibling = (x, y, c), (x, y, 1 - c)
        chips = [(1 - x, y), (x, 1 - y), (1 - x, 1 - y)]

        def slab(a, place):
            return outs[a].at[4 * place[0] + 2 * place[1] + place[2]]

        def copy(a, k, block, to, src=None):
            return pltpu.make_async_remote_copy(
                src_ref=slab(a, block) if src is None else src, dst_ref=slab(a, block),
                send_sem=send_sems.at[a, k], recv_sem=recv_sems.at[a, k],
                device_id=to, device_id_type=pl.DeviceIdType.MESH)

        sends, mine = [], []
        for a in range(n):
            loc = pltpu.make_async_copy(ins[a], slab(a, me), local_sems.at[a])
            loc.start()
            mine.append(loc)
            sends.append(copy(a, 0, me, sibling, src=ins[a]))
            sends += [copy(a, 1 + j, me, (*chip, c), src=ins[a]) for j, chip in enumerate(chips)]
        for cp in sends:
            cp.start()
        for j, chip in enumerate(chips):
            for a in range(n):
                copy(a, 1 + j, (*chip, c), me).wait_recv()
                fwd = copy(a, 4 + j, (*chip, c), sibling)
                fwd.start()
                sends.append(fwd)
        for a in range(n):
            copy(a, 0, sibling, me).wait_recv()
            for j, chip in enumerate(chips):
                copy(a, 4 + j, (*chip, 1 - c), me).wait_recv()
        for cp in sends:
            cp.wait_send()
        for loc in mine:
            loc.wait()

    return pl.pallas_call(
        body, name=name, in_specs=[ANY] * n, out_specs=[ANY] * n,
        out_shape=[jax.ShapeDtypeStruct((N_DEV,) + tuple(a.shape), a.dtype) for a in arrays],
        scratch_shapes=[pltpu.SemaphoreType.DMA((n, N_DEV - 1)), pltpu.SemaphoreType.DMA((n, N_DEV - 1)),
                        pltpu.SemaphoreType.DMA((n,))],
        compiler_params=pltpu.CompilerParams(has_side_effects=True),
    )(*arrays)


def _matmul(a, b, *, tm, tn, tk, out_dtype, name, trans_a=False, trans_b=False, comm=None):
    m, k = (a.shape[1], a.shape[0]) if trans_a else a.shape
    n = b.shape[0] if trans_b else b.shape[1]
    nk = k // tk
    dims = TN if trans_a else (NT if trans_b else NN)
    assert not (trans_a and trans_b)
    nc = comm.n if comm else 0
    grid = (m // tm, n // tn, nk)

    def body(*refs):
        a_ref, b_ref = refs[:2]
        cin = refs[2:2 + nc]
        o_ref = refs[2 + nc]
        cout = refs[3 + nc:3 + 2 * nc]
        scratch = refs[3 + 2 * nc:]
        sems = scratch[len(scratch) - 3:] if comm else None
        i, j, kk = pl.program_id(0), pl.program_id(1), pl.program_id(2)
        if comm:
            @pl.when((i == 0) & (j == 0) & (kk == 0))
            def _():
                comm.start(cin, cout, sems)

        if nk == 1:
            o_ref[...] = _dot(a_ref[...], b_ref[...], dims).astype(out_dtype)
        else:
            acc_ref = scratch[0]

            @pl.when(kk == 0)
            def _():
                acc_ref[...] = jnp.zeros_like(acc_ref)

            acc_ref[...] += _dot(a_ref[...], b_ref[...], dims)

            @pl.when(kk == nk - 1)
            def _():
                o_ref[...] = acc_ref[...].astype(out_dtype)

        if comm:
            @pl.when((i == grid[0] - 1) & (j == grid[1] - 1) & (kk == nk - 1))
            def _():
                comm.wait(cin, cout, sems)

    a_spec = (pl.BlockSpec((tk, tm), lambda i, j, kk: (kk, i)) if trans_a
              else pl.BlockSpec((tm, tk), lambda i, j, kk: (i, kk)))
    b_spec = (pl.BlockSpec((tn, tk), lambda i, j, kk: (j, kk)) if trans_b
              else pl.BlockSpec((tk, tn), lambda i, j, kk: (kk, j)))
    sem = ("arbitrary",) * 3 if comm else ("parallel", "parallel", "arbitrary")
    res = pl.pallas_call(
        body, name=name, grid=grid,
        in_specs=[a_spec, b_spec] + (comm.in_specs if comm else []),
        out_specs=[pl.BlockSpec((tm, tn), lambda i, j, kk: (i, j))] + (comm.out_specs if comm else []),
        out_shape=[jax.ShapeDtypeStruct((m, n), out_dtype)] + (comm.out_shape if comm else []),
        scratch_shapes=([] if nk == 1 else [pltpu.VMEM((tm, tn), F32)]) + (comm.scratch if comm else []),
        compiler_params=_params(sem, side_effects=bool(comm)),
    )(a, b, *(comm.operands if comm else []))
    return res if comm else res[0]


def _h_chunk(i, nx, h0_ref, x_ref):
    h = jnp.where(i == 0, h0_ref[...], x_ref[...])
    return h * (i <= nx).astype(F32)


def _x_spec(nx):
    return pl.BlockSpec((CHUNK, D_MODEL), lambda i: (jnp.clip(i - 1, 0, nx - 1), 0))


def _prenorm(x2d, h0, w):
    nx = x2d.shape[0] // CHUNK
    nc = nx + 2

    def body(x_ref, h0_ref, w_ref, o_ref):
        h = _h_chunk(pl.program_id(0), nx, h0_ref, x_ref)
        rstd = lax.rsqrt(jnp.mean(h * h, axis=-1, keepdims=True) + EPS)
        o_ref[...] = (h * rstd * w_ref[...]).astype(BF16)

    return pl.pallas_call(
        body, name="prenorm", grid=(nc,),
        in_specs=[_x_spec(nx), pl.BlockSpec((CHUNK, D_MODEL), lambda i: (0, 0)), pl.BlockSpec((1, D_MODEL), lambda i: (0, 0))],
        out_specs=pl.BlockSpec((CHUNK, D_MODEL), lambda i: (i, 0)),
        out_shape=jax.ShapeDtypeStruct((nc * CHUNK, D_MODEL), BF16),
        compiler_params=_params(("parallel",)),
    )(x2d, h0, w)


CONV_COLS = 512
HALO = 8


def _conv_pre(ext, w, b):
    acc = b + w[3:4, :] * ext[HALO:, :]
    for j in range(1, CONV_WIDTH):
        acc = acc + w[3 - j:4 - j, :] * pltpu.roll(ext, j, 0)[HALO:, :]
    return acc


def _conv_fwd(proj, conv_w, conv_b):
    lp = proj.shape[0]
    t = _tile(lp, (704, 384, 128, 64))
    hb = t // HALO
    c0 = OFF_XBC // CONV_COLS

    def body(u_ref, halo_ref, w_ref, b_ref, o_ref):
        i = pl.program_id(0)
        halo = halo_ref[...] * (i > 0).astype(F32)
        pre = _conv_pre(jnp.concatenate([halo, u_ref[...]], axis=0), w_ref[...], b_ref[...])
        o_ref[...] = pre * _sigmoid(pre)

    return pl.pallas_call(
        body, name="conv_fwd", grid=(lp // t, D_CONV // CONV_COLS),
        in_specs=[
            pl.BlockSpec((t, CONV_COLS), lambda i, j: (i, c0 + j)),
            pl.BlockSpec((HALO, CONV_COLS), lambda i, j: (jnp.maximum(i * hb - 1, 0), c0 + j)),
            pl.BlockSpec((CONV_WIDTH, CONV_COLS), lambda i, j: (0, j)),
            pl.BlockSpec((1, CONV_COLS), lambda i, j: (0, j)),
        ],
        out_specs=pl.BlockSpec((t, CONV_COLS), lambda i, j: (i, j)),
        out_shape=jax.ShapeDtypeStruct((lp, D_CONV), F32),
        compiler_params=_params(("parallel", "parallel")),
    )(proj, proj, conv_w, conv_b)


def _conv_bwd(dxbc, proj, conv_w, conv_b, dproj):
    lp = proj.shape[0]
    t = _tile(lp, (704, 384, 128, 64))
    hb = t // HALO
    nt = lp // t
    c0 = OFF_XBC // CONV_COLS

    def body(dx_ref, dxn_ref, u_ref, up_ref, un_ref, w_ref, b_ref, _, du_ref, dw_ref, db_ref):
        i = pl.program_id(1)
        w = w_ref[...]
        up = up_ref[...] * (i > 0).astype(F32)
        ext = jnp.concatenate([up, u_ref[...], un_ref[...]], axis=0)
        pre = _conv_pre(ext, w, b_ref[...])
        dxn = dxn_ref[...] * (i < nt - 1).astype(F32)
        dxe = jnp.concatenate([dx_ref[...], dxn], axis=0)
        sg = _sigmoid(pre)
        dpre = dxe * sg * (1.0 + pre * (1.0 - sg))
        du = w[3:4, :] * dpre[:t, :]
        for j in range(1, CONV_WIDTH):
            du = du + w[3 - j:4 - j, :] * pltpu.roll(dpre, t + HALO - j, 0)[:t, :]
        du_ref[...] = du.astype(BF16)

        @pl.when(i == 0)
        def _():
            dw_ref[...] = jnp.zeros_like(dw_ref)
            db_ref[...] = jnp.zeros_like(db_ref)

        dp = dpre[:t, :]
        db_ref[...] += jnp.sum(dp, axis=0, keepdims=True)
        hist = ext[:HALO + t, :]
        dw_ref[3:4, :] += jnp.sum(dp * hist[HALO:, :], axis=0, keepdims=True)
        for j in range(1, CONV_WIDTH):
            dw_ref[3 - j:4 - j, :] += jnp.sum(dp * pltpu.roll(hist, j, 0)[HALO:, :], axis=0, keepdims=True)

    nxt = lambda i: jnp.minimum((i + 1) * hb, lp // HALO - 1)
    return pl.pallas_call(
        body, name="conv_bwd", grid=(D_CONV // CONV_COLS, nt),
        in_specs=[
            pl.BlockSpec((t, CONV_COLS), lambda j, i: (i, j)),
            pl.BlockSpec((HALO, CONV_COLS), lambda j, i: (nxt(i), j)),
            pl.BlockSpec((t, CONV_COLS), lambda j, i: (i, c0 + j)),
            pl.BlockSpec((HALO, CONV_COLS), lambda j, i: (jnp.maximum(i * hb - 1, 0), c0 + j)),
            pl.BlockSpec((HALO, CONV_COLS), lambda j, i: (nxt(i), c0 + j)),
            pl.BlockSpec((CONV_WIDTH, CONV_COLS), lambda j, i: (0, j)),
            pl.BlockSpec((1, CONV_COLS), lambda j, i: (0, j)),
            ANY,
        ],
        out_specs=[
            pl.BlockSpec((t, CONV_COLS), lambda j, i: (i, c0 + j)),
            pl.BlockSpec((CONV_WIDTH, CONV_COLS), lambda j, i: (0, j)),
            pl.BlockSpec((1, CONV_COLS), lambda j, i: (0, j)),
        ],
        out_shape=[
            jax.ShapeDtypeStruct((lp, NP), BF16),
            jax.ShapeDtypeStruct((CONV_WIDTH, D_CONV), F32),
            jax.ShapeDtypeStruct((1, D_CONV), F32),
        ],
        input_output_aliases={7: 0},
        compiler_params=_params(("parallel", "arbitrary")),
    )(dxbc, dxbc, proj, proj, proj, conv_w, conv_b, dproj)


def _swap_halves(t):
    w = t.shape[-1]
    lane = lax.broadcasted_iota(jnp.int32, t.shape, 1)
    return jnp.where((lane % HEAD_DIM) < HEAD_DIM // 2, pltpu.roll(t, w - HEAD_DIM // 2, 1),
                     pltpu.roll(t, HEAD_DIM // 2, 1))


def _act_fwd(proj, cos_t, sin_t, expand, dt_bias_rep):
    lp = proj.shape[0]
    t = _tile(lp, (384, 128, 64))

    def body(q_ref, k_ref, v_ref, dt_ref, cos_ref, sin_ref, ex_ref, bias_ref, qo_ref, ko_ref, vo_ref, dto_ref):
        i = pl.program_id(0)
        cos = cos_ref[...]
        sin = sin_ref[...]
        q = q_ref[...]
        qo_ref[...] = (q * jnp.tile(cos, (1, D_ATT // 128)) + _swap_halves(q) * jnp.tile(sin, (1, D_ATT // 128))).astype(BF16)
        k = k_ref[...]
        ko_ref[...] = (k * jnp.tile(cos, (1, D_KV // 128)) + _swap_halves(k) * jnp.tile(sin, (1, D_KV // 128))).astype(BF16)
        vo_ref[...] = v_ref[...].astype(BF16)
        raw = _dot(dt_ref[...], ex_ref[...], NN, HIGHEST) + bias_ref[...]
        sp = jnp.maximum(raw, 0.0) + jnp.log1p(jnp.exp(-jnp.abs(raw)))
        row = i * t + lax.broadcasted_iota(jnp.int32, sp.shape, 0)
        dto_ref[...] = jnp.where(row >= PAD_LEAD, sp, 0.0)

    return pl.pallas_call(
        body, name="act_fwd", grid=(lp // t,),
        in_specs=[
            pl.BlockSpec((t, D_ATT), lambda i: (i, OFF_Q // D_ATT)),
            pl.BlockSpec((t, D_KV), lambda i: (i, OFF_K // D_KV)),
            pl.BlockSpec((t, D_KV), lambda i: (i, OFF_V // D_KV)),
            pl.BlockSpec((t, 128), lambda i: (i, OFF_DT // 128)),
            pl.BlockSpec((t, 128), lambda i: (i, 0)),
            pl.BlockSpec((t, 128), lambda i: (i, 0)),
            pl.BlockSpec((128, D_SSD), lambda i: (0, 0)),
            pl.BlockSpec((1, D_SSD), lambda i: (0, 0)),
        ],
        out_specs=[
            pl.BlockSpec((t, D_ATT), lambda i: (i, 0)),
            pl.BlockSpec((t, D_KV), lambda i: (i, 0)),
            pl.BlockSpec((t, D_KV), lambda i: (i, 0)),
            pl.BlockSpec((t, D_SSD), lambda i: (i, 0)),
        ],
        out_shape=[
            jax.ShapeDtypeStruct((lp, D_ATT), BF16),
            jax.ShapeDtypeStruct((lp, D_KV), BF16),
            jax.ShapeDtypeStruct((lp, D_KV), BF16),
            jax.ShapeDtypeStruct((lp, D_SSD), F32),
        ],
        compiler_params=_params(("parallel",)),
    )(proj, proj, proj, proj, cos_t, sin_t, expand, dt_bias_rep)


def _act_bwd(dqr, dkr, dv, dg, ddt_part, cos_t, sin_t, reduce_t, dproj):
    lp = dqr.shape[0]
    t = _tile(lp, (384, 128, 64))

    def body(dq_ref, dk_ref, dv_ref, dg_ref, ddt_ref, cos_ref, sin_ref, red_ref, _, o_ref, db_ref):
        i = pl.program_id(0)
        cos = cos_ref[...]
        sin = sin_ref[...]
        dq = dq_ref[...]
        dq = dq * jnp.tile(cos, (1, D_ATT // 128)) + _swap_halves(dq * jnp.tile(sin, (1, D_ATT // 128)))
        dk = dk_ref[...]
        dk = dk * jnp.tile(cos, (1, D_KV // 128)) + _swap_halves(dk * jnp.tile(sin, (1, D_KV // 128)))
        ddt = _dot(ddt_ref[...], red_ref[...], NN, HIGHEST)
        o_ref[...] = jnp.concatenate(
            [dq.astype(BF16), dg_ref[...].astype(BF16), dk.astype(BF16), dv_ref[...].astype(BF16), ddt.astype(BF16),
             jnp.zeros((t, NP - OFF_DT - 128), BF16)], axis=1)

        @pl.when(i == 0)
        def _():
            db_ref[...] = jnp.zeros_like(db_ref)

        db_ref[...] += jnp.sum(ddt, axis=0, keepdims=True)

    return pl.pallas_call(
        body, name="act_bwd", grid=(lp // t,),
        in_specs=[
            pl.BlockSpec((t, D_ATT), lambda i: (i, 0)),
            pl.BlockSpec((t, D_KV), lambda i: (i, 0)),
            pl.BlockSpec((t, D_KV), lambda i: (i, 0)),
            pl.BlockSpec((t, D_ATT), lambda i: (i, 0)),
            pl.BlockSpec((t, D_SSD), lambda i: (i, 0)),
            pl.BlockSpec((t, 128), lambda i: (i, 0)),
            pl.BlockSpec((t, 128), lambda i: (i, 0)),
            pl.BlockSpec((D_SSD, 128), lambda i: (0, 0)),
            ANY,
        ],
        out_specs=[pl.BlockSpec((t, TAIL_W), lambda i: (i, OFF_Q // TAIL_W)), pl.BlockSpec((1, 128), lambda i: (0, 0))],
        out_shape=[jax.ShapeDtypeStruct((lp, NP), BF16), jax.ShapeDtypeStruct((1, 128), F32)],
        input_output_aliases={8: 0},
        compiler_params=_params(("arbitrary",)),
    )(dqr, dkr, dv, dg, ddt_part, cos_t, sin_t, reduce_t, dproj)


def _ssd_common(xbc, dt, a, tri, sel):
    xs = xbc[:, :GROUP_W]
    b = xbc[:, GROUP_W:GROUP_W + D_STATE]
    c = xbc[:, GROUP_W + D_STATE:]
    cs = _dot(tri, dt * a, NN, HIGHEST)
    cs_t = _dot(sel, cs, NT, HIGHEST)
    xdt = xs * dt
    cs_last = cs[CHUNK - 1:CHUNK, :]
    return xs, b, c, cs, cs_t, xdt, jnp.exp(cs), jnp.exp(cs_last - cs), jnp.exp(cs_last)


def _decay_matrix(cs, cs_t, r, causal):
    seg = cs[:, r * HEAD_DIM:r * HEAD_DIM + 1] - cs_t[r:r + 1, :]
    return jnp.exp(jnp.where(causal, seg, -jnp.inf))


def _causal_mask():
    row = lax.broadcasted_iota(jnp.int32, (CHUNK, CHUNK), 0)
    col = lax.broadcasted_iota(jnp.int32, (CHUNK, CHUNK), 1)
    return row >= col


def _ssd_fwd(xbc, dt_rep, proj, a_rep, dsk_rep, wn, tri, sel):
    lp = xbc.shape[0]
    nc = lp // CHUNK
    gw = SSD_GPS * GROUP_W

    def body(xbc_ref, dt_ref, z_ref, a_ref, dsk_ref, wn_ref, tri_ref, sel_ref, yn_ref, ytot_ref, hprev_ref, h_scr):
        @pl.when(pl.program_id(1) == 0)
        def _():
            h_scr[...] = jnp.zeros_like(h_scr)

        causal = _causal_mask()
        for gi in range(SSD_GPS):
            cols = slice(gi * GROUP_W, (gi + 1) * GROUP_W)
            xs, b, c, cs, cs_t, xdt, e, dte, cd = _ssd_common(
                xbc_ref[:, gi * XBC_BLK:(gi + 1) * XBC_BLK], dt_ref[:, cols], a_ref[:, cols], tri_ref[...], sel_ref[...])
            hprev = h_scr[gi]
            hprev_ref[gi] = hprev
            cb = _dot(c.astype(BF16), b.astype(BF16), NT)
            ydiag = []
            for r in range(HEADS_PER_GROUP):
                m = (cb * _decay_matrix(cs, cs_t, r, causal)).astype(BF16)
                ydiag.append(_dot(m, xdt[:, r * HEAD_DIM:(r + 1) * HEAD_DIM].astype(BF16)))
            ytot = (jnp.concatenate(ydiag, axis=1) + _dot(c.astype(BF16), hprev.astype(BF16)) * e
                    + dsk_ref[:, cols] * xs)
            h_scr[gi] = hprev * cd + _dot(b.astype(BF16), (xdt * dte).astype(BF16), TN)
            ytot_ref[:, cols] = ytot
            z = z_ref[:, cols]
            gz = ytot * (z * _sigmoid(z))
            rstd = lax.rsqrt(jnp.mean(gz * gz, axis=-1, keepdims=True) + EPS)
            yn_ref[:, cols] = (gz * rstd * wn_ref[:, cols]).astype(BF16)

    vec = pl.BlockSpec((1, gw), lambda g, c: (0, g))
    blk = pl.BlockSpec((CHUNK, gw), lambda g, c: (c, g))
    return pl.pallas_call(
        body, name="ssd_fwd", grid=(N_GROUPS // SSD_GPS, nc),
        in_specs=[
            pl.BlockSpec((CHUNK, SSD_GPS * XBC_BLK), lambda g, c: (c, g)),
            blk, blk, vec, vec, vec,
            pl.BlockSpec((CHUNK, CHUNK), lambda g, c: (0, 0)),
            pl.BlockSpec((8, GROUP_W), lambda g, c: (0, 0)),
        ],
        out_specs=[blk, blk, pl.BlockSpec((None, SSD_GPS, D_STATE, GROUP_W), lambda g, c: (c, g, 0, 0))],
        out_shape=[
            jax.ShapeDtypeStruct((lp, D_MIX), BF16),
            jax.ShapeDtypeStruct((lp, D_SSD), F32),
            jax.ShapeDtypeStruct((nc, N_GROUPS, D_STATE, GROUP_W), F32),
        ],
        scratch_shapes=[pltpu.VMEM((SSD_GPS, D_STATE, GROUP_W), F32)],
        compiler_params=_params(("parallel", "arbitrary")),
    )(xbc, dt_rep, proj, a_rep, dsk_rep, wn, tri, sel)


def _ssd_bwd(dmix, ytot, proj, xbc, dt_rep, hprev, a_rep, dsk_rep, wn, tri, sel, comm):
    lp = xbc.shape[0]
    nc = lp // CHUNK
    gw = SSD_GPS * GROUP_W
    ncm = comm.n
    n_in, n_out = 11, 6
    grid = (N_GROUPS // SSD_GPS, nc)

    def one_group(gi, refs, causal):
        (dyn_ref, ytot_ref, z_ref, xbc_ref, dt_ref, hprev_ref, a_ref, dsk_ref, wn_ref, tri_ref, sel_ref,
         dz_ref, dxbc_ref, ddt_ref, dd_ref, da_ref, dwn_ref, dh_scr) = refs
        cols = slice(gi * GROUP_W, (gi + 1) * GROUP_W)
        dt = dt_ref[:, cols]
        a = a_ref[:, cols]
        tri = tri_ref[...]
        xs, b, c, cs, cs_t, xdt, e, dte, cd = _ssd_common(
            xbc_ref[:, gi * XBC_BLK:(gi + 1) * XBC_BLK], dt, a, tri, sel_ref[...])
        hprev = hprev_ref[gi]
        ytot = ytot_ref[:, cols]
        z = z_ref[:, cols]
        dyn = dyn_ref[:, cols]

        sz = _sigmoid(z)
        silu_z = z * sz
        gz = ytot * silu_z
        rstd = lax.rsqrt(jnp.mean(gz * gz, axis=-1, keepdims=True) + EPS)
        xhat = gz * rstd
        dwn_ref[:, cols] += jnp.sum(dyn * xhat, axis=0, keepdims=True)
        dxhat = dyn * wn_ref[:, cols]
        dgz = rstd * (dxhat - xhat * jnp.mean(dxhat * xhat, axis=-1, keepdims=True))
        dy = dgz * silu_z
        dz_ref[:, cols] = (dgz * ytot * (sz * (1.0 + z * (1.0 - sz)))).astype(BF16)

        dd_ref[:, cols] += jnp.sum(dy * xs, axis=0, keepdims=True)
        dxs = dsk_ref[:, cols] * dy

        cbf = c.astype(BF16)
        bbf = b.astype(BF16)
        hbf = hprev.astype(BF16)
        g = _dot(cbf, hbf)
        dcs = dy * g * e
        dgb = (dy * e).astype(BF16)
        dc = _dot(dgb, hbf, NT)
        dhprev = _dot(cbf, dgb, TN)
        dhn = dh_scr[gi]
        dhprev = dhprev + dhn * cd
        dcs_last = jnp.sum(dhn * hprev, axis=0, keepdims=True) * cd
        dhnb = dhn.astype(BF16)
        dxw = _dot(bbf, dhnb)
        db = _dot((xdt * dte).astype(BF16), dhnb, NT)
        dxdt = dxw * dte
        t_dte = dxw * xdt * dte
        dcs = dcs - t_dte
        dcs_last = dcs_last + jnp.sum(t_dte, axis=0, keepdims=True)
        cb = _dot(cbf, bbf, NT)
        ones = jnp.ones((CHUNK, HEAD_DIM), F32)
        dcb = jnp.zeros((CHUNK, CHUNK), F32)
        dxdt_diag = []
        dcs_diag = []
        for r in range(HEADS_PER_GROUP):
            sl = slice(r * HEAD_DIM, (r + 1) * HEAD_DIM)
            lm = _decay_matrix(cs, cs_t, r, causal)
            m = cb * lm
            dyr = dy[:, sl].astype(BF16)
            dm = _dot(dyr, xdt[:, sl].astype(BF16), NT)
            dxdt_diag.append(_dot(m.astype(BF16), dyr, TN))
            dcb = dcb + dm * lm
            dseg = dm * m
            rows = jnp.sum(dseg, axis=1, keepdims=True)
            colsum = _dot(dseg, ones, TN, HIGHEST)
            dcs_diag.append((rows - colsum) * (1.0 / HEAD_DIM))
        dxdt = dxdt + jnp.concatenate(dxdt_diag, axis=1)
        dcs = dcs + jnp.concatenate(dcs_diag, axis=1)
        dcbb = dcb.astype(BF16)
        dc = dc + _dot(dcbb, bbf)
        db = db + _dot(dcbb, cbf, TN)
        row = lax.broadcasted_iota(jnp.int32, dcs.shape, 0)
        dcs = dcs + jnp.where(row == CHUNK - 1, dcs_last, 0.0)
        dda = _dot(tri, dcs, TN, HIGHEST)
        da_ref[:, cols] += jnp.sum(dda * dt, axis=0, keepdims=True)
        ddt = dda * a + dxdt * xs
        dxs = dxs + dxdt * dt
        ddt_ref[:, cols] = ddt * (1.0 - jnp.exp(-dt))
        dxbc_ref[:, gi * XBC_BLK:(gi + 1) * XBC_BLK] = jnp.concatenate([dxs, db, dc], axis=1)
        dh_scr[gi] = dhprev

    def body(*refs):
        ins = refs[:n_in]
        cin = refs[n_in:n_in + ncm]
        outs = refs[n_in + ncm:n_in + ncm + n_out]
        cout = refs[n_in + ncm + n_out:n_in + 2 * ncm + n_out]
        dh_scr = refs[n_in + 2 * ncm + n_out]
        sems = refs[n_in + 2 * ncm + n_out + 1:]
        g, c = pl.program_id(0), pl.program_id(1)

        @pl.when((g == 0) & (c == 0))
        def _():
            comm.start(cin, cout, sems)

        @pl.when(c == 0)
        def _():
            dh_scr[...] = jnp.zeros_like(dh_scr)
            for ref in outs[3:]:
                ref[...] = jnp.zeros_like(ref)

        causal = _causal_mask()
        for gi in range(SSD_GPS):
            one_group(gi, ins + outs + (dh_scr,), causal)

        @pl.when((g == grid[0] - 1) & (c == nc - 1))
        def _():
            comm.wait(cin, cout, sems)

    rev = lambda c: nc - 1 - c
    vec = pl.BlockSpec((1, gw), lambda g, c: (0, g))
    blk = pl.BlockSpec((CHUNK, gw), lambda g, c: (rev(c), g))
    xblk = pl.BlockSpec((CHUNK, SSD_GPS * XBC_BLK), lambda g, c: (rev(c), g))
    res = pl.pallas_call(
        body, name="ssd_bwd", grid=grid,
        in_specs=[blk, blk, blk, xblk, blk,
                  pl.BlockSpec((None, SSD_GPS, D_STATE, GROUP_W), lambda g, c: (rev(c), g, 0, 0)),
                  vec, vec, vec,
                  pl.BlockSpec((CHUNK, CHUNK), lambda g, c: (0, 0)),
                  pl.BlockSpec((8, GROUP_W), lambda g, c: (0, 0))] + comm.in_specs,
        out_specs=[blk, xblk, blk, vec, vec, vec] + comm.out_specs,
        out_shape=[
            jax.ShapeDtypeStruct((lp, NP), BF16),
            jax.ShapeDtypeStruct((lp, D_CONV), F32),
            jax.ShapeDtypeStruct((lp, D_SSD), F32),
            jax.ShapeDtypeStruct((1, D_SSD), F32),
            jax.ShapeDtypeStruct((1, D_SSD), F32),
            jax.ShapeDtypeStruct((1, D_SSD), F32),
        ] + comm.out_shape,
        scratch_shapes=[pltpu.VMEM((SSD_GPS, D_STATE, GROUP_W), F32)] + comm.scratch,
        compiler_params=_params(("arbitrary", "arbitrary"), side_effects=True),
    )(dmix, ytot, proj, xbc, dt_rep, hprev, a_rep, dsk_rep, wn, tri, sel, *comm.operands)
    return res


def _stack_heads(t, h):
    return jnp.concatenate([t[:, (REP * h + r) * HEAD_DIM:(REP * h + r + 1) * HEAD_DIM] for r in range(REP)], axis=0)


def _band(t2, t1, t0, h):
    sl = slice(h * HEAD_DIM, (h + 1) * HEAD_DIM)
    return jnp.concatenate([t2[:, sl], t1[:, sl], t0[:, sl]], axis=0)


def _attn_probs(qs, kb, sink, qc):
    s = _dot(qs, kb, NT) * (HEAD_DIM ** -0.5)
    key_abs = (qc - WINDOW_CHUNKS) * CHUNK + lax.broadcasted_iota(jnp.int32, s.shape, 1)
    s = jnp.where(key_abs >= PAD_LEAD, s, -jnp.inf)
    m = jnp.maximum(jnp.max(s, axis=-1, keepdims=True), sink)
    p = jnp.exp(s - m)
    ps = jnp.exp(sink - m)
    denom = jnp.sum(p, axis=-1, keepdims=True) + ps
    return p / denom, ps / denom


def _kv_specs(width, chunk_of):
    return [pl.BlockSpec((CHUNK, width), functools.partial(lambda j, c: (jnp.maximum(chunk_of(c) - j, 0), 0), j))
            for j in (2, 1, 0)]


def _attn_fwd(qr, kr, vb, proj, sink_stack, mix):
    lp = qr.shape[0]
    nc = lp // CHUNK

    def body(q_ref, k2_ref, k1_ref, k0_ref, v2_ref, v1_ref, v0_ref, g_ref, sink_ref, _, att_ref, mix_ref):
        qc = pl.program_id(0)
        q = q_ref[...]
        ks = (k2_ref[...], k1_ref[...], k0_ref[...])
        vs = (v2_ref[...], v1_ref[...], v0_ref[...])
        cols = [None] * Q_HEADS
        for h in range(KV_HEADS):
            pn, _ = _attn_probs(_stack_heads(q, h), _band(*ks, h), sink_ref[h], qc)
            o = _dot(pn.astype(BF16), _band(*vs, h))
            for r in range(REP):
                cols[REP * h + r] = o[r * CHUNK:(r + 1) * CHUNK, :]
        att = jnp.concatenate(cols, axis=1)
        att_ref[...] = att
        g = g_ref[...]
        mix_ref[...] = (att * (g * _sigmoid(g))).astype(BF16)

    here = lambda c: c
    return pl.pallas_call(
        body, name="attn_fwd", grid=(nc,),
        in_specs=[pl.BlockSpec((CHUNK, D_ATT), lambda c: (c, 0))] + _kv_specs(D_KV, here) + _kv_specs(D_KV, here) + [
            pl.BlockSpec((CHUNK, D_ATT), lambda c: (c, OFF_G // D_ATT)),
            pl.BlockSpec((KV_HEADS, REP * CHUNK, 1), lambda c: (0, 0, 0)),
            ANY,
        ],
        out_specs=[pl.BlockSpec((CHUNK, D_ATT), lambda c: (c, 0)),
                   pl.BlockSpec((CHUNK, D_ATT), lambda c: (c, D_SSD // D_ATT))],
        out_shape=[jax.ShapeDtypeStruct((lp, D_ATT), F32), jax.ShapeDtypeStruct((lp, D_MIX), BF16)],
        input_output_aliases={9: 1},
        compiler_params=_params(("parallel",)),
    )(qr, kr, kr, kr, vb, vb, vb, proj, sink_stack, mix)


def _attn_bwd(qr, kr, vb, att, proj, dmix, sink_stack):
    lp = qr.shape[0]
    nc = lp // CHUNK

    def body(q_ref, k2_ref, k1_ref, k0_ref, v2_ref, v1_ref, v0_ref, att_ref, g_ref, do_ref, sink_ref,
             dq_ref, dk_ref, dv_ref, dg_ref, dsink_ref, dk_acc, dv_acc):
        step = pl.program_id(0)

        @pl.when(step == 0)
        def _():
            dk_acc[...] = jnp.zeros_like(dk_acc)
            dv_acc[...] = jnp.zeros_like(dv_acc)
            dsink_ref[...] = jnp.zeros_like(dsink_ref)

        @pl.when(step < nc)
        def _():
            q = q_ref[...]
            ks = (k2_ref[...], k1_ref[...], k0_ref[...])
            vs = (v2_ref[...], v1_ref[...], v0_ref[...])
            att = att_ref[...]
            g = g_ref[...]
            dog = do_ref[...]
            sg = _sigmoid(g)
            dg_ref[...] = dog * att * (sg * (1.0 + g * (1.0 - sg)))
            do = dog * (g * sg)
            dq_cols = [None] * Q_HEADS
            for h in range(KV_HEADS):
                sl = slice(h * HEAD_DIM, (h + 1) * HEAD_DIM)
                qs = _stack_heads(q, h)
                kb = _band(*ks, h)
                vbh = _band(*vs, h)
                sink = sink_ref[h]
                pn, psink = _attn_probs(qs, kb, sink, step)
                dos = _stack_heads(do, h)
                delta = jnp.sum(dos * _stack_heads(att, h), axis=-1, keepdims=True)
                dosb = dos.astype(BF16)
                dp = _dot(dosb, vbh, NT)
                ds = (pn * (dp - delta)).astype(BF16)
                dsink_ref[h] += -(psink * delta)
                dqs = _dot(ds, kb) * (HEAD_DIM ** -0.5)
                for r in range(REP):
                    dq_cols[REP * h + r] = dqs[r * CHUNK:(r + 1) * CHUNK, :]
                dk_acc[:, sl] += _dot(ds, qs, TN) * (HEAD_DIM ** -0.5)
                dv_acc[:, sl] += _dot(pn.astype(BF16), dosb, TN)
            dq_ref[...] = jnp.concatenate(dq_cols, axis=1)

        dk_ref[...] = dk_acc[0:CHUNK, :]
        dv_ref[...] = dv_acc[0:CHUNK, :]
        for acc in (dk_acc, dv_acc):
            rest = acc[CHUNK:BAND, :]
            acc[0:BAND - CHUNK, :] = rest
            acc[BAND - CHUNK:BAND, :] = jnp.zeros((CHUNK, D_KV), F32)

    qc = lambda c: jnp.minimum(c, nc - 1)
    qblk = pl.BlockSpec((CHUNK, D_ATT), lambda c: (qc(c), 0))
    oldest = pl.BlockSpec((CHUNK, D_KV), lambda c: (jnp.maximum(c - WINDOW_CHUNKS, 0), 0))
    return pl.pallas_call(
        body, name="attn_bwd", grid=(nc + WINDOW_CHUNKS,),
        in_specs=[qblk] + _kv_specs(D_KV, qc) + _kv_specs(D_KV, qc) + [
            qblk,
            pl.BlockSpec((CHUNK, D_ATT), lambda c: (qc(c), OFF_G // D_ATT)),
            pl.BlockSpec((CHUNK, D_ATT), lambda c: (qc(c), D_SSD // D_ATT)),
            pl.BlockSpec((KV_HEADS, REP * CHUNK, 1), lambda c: (0, 0, 0)),
        ],
        out_specs=[qblk, oldest, oldest, qblk, pl.BlockSpec((KV_HEADS, REP * CHUNK, 1), lambda c: (0, 0, 0))],
        out_shape=[
            jax.ShapeDtypeStruct((lp, D_ATT), F32),
            jax.ShapeDtypeStruct((lp, D_KV), F32),
            jax.ShapeDtypeStruct((lp, D_KV), F32),
            jax.ShapeDtypeStruct((lp, D_ATT), F32),
            jax.ShapeDtypeStruct((KV_HEADS, REP * CHUNK, 1), F32),
        ],
        scratch_shapes=[pltpu.VMEM((BAND, D_KV), F32), pltpu.VMEM((BAND, D_KV), F32)],
        compiler_params=_params(("arbitrary",)),
    )(qr, kr, kr, kr, vb, vb, vb, att, proj, dmix, sink_stack)


def _post_loss(out, x2d, target, w):
    lp = out.shape[0]
    nc = lp // CHUNK
    nx = x2d.shape[0] // CHUNK

    def body(o_ref, x_ref, t_ref, w_ref, do_ref, dy_ref, gw_ref, loss_ref):
        i = pl.program_id(0)

        @pl.when(i == 0)
        def _():
            gw_ref[...] = jnp.zeros_like(gw_ref)
            loss_ref[...] = jnp.zeros_like(loss_ref)

        o = o_ref[...]
        w = w_ref[...]
        rstd = lax.rsqrt(jnp.mean(o * o, axis=-1, keepdims=True) + EPS)
        xhat = o * rstd
        err = (x_ref[...] + xhat * w - t_ref[...]) * ((i > 0) & (i <= nx)).astype(F32)
        loss_ref[...] += 0.5 * jnp.sum(jnp.mean(err * err, axis=-1, keepdims=True), axis=0, keepdims=True)
        dy = err * (1.0 / D_MODEL)
        dy_ref[...] = dy
        gw_ref[...] += jnp.sum(dy * xhat, axis=0, keepdims=True)
        dxhat = dy * w
        do_ref[...] = (rstd * (dxhat - xhat * jnp.mean(dxhat * xhat, axis=-1, keepdims=True))).astype(BF16)

    row = pl.BlockSpec((CHUNK, D_MODEL), lambda i: (i, 0))
    return pl.pallas_call(
        body, name="post_loss", grid=(nc,),
        in_specs=[row, _x_spec(nx), _x_spec(nx), pl.BlockSpec((1, D_MODEL), lambda i: (0, 0))],
        out_specs=[row, row, pl.BlockSpec((1, D_MODEL), lambda i: (0, 0)), pl.BlockSpec((1, 128), lambda i: (0, 0))],
        out_shape=[
            jax.ShapeDtypeStruct((lp, D_MODEL), BF16),
            jax.ShapeDtypeStruct((lp, D_MODEL), F32),
            jax.ShapeDtypeStruct((1, D_MODEL), F32),
            jax.ShapeDtypeStruct((1, 128), F32),
        ],
        compiler_params=_params(("arbitrary",)),
    )(out, x2d, target, w)


def _prenorm_bwd(dhn, x2d, h0, dy, w):
    lp = dhn.shape[0]
    nc = lp // CHUNK
    nx = x2d.shape[0] // CHUNK

    def body(dhn_ref, x_ref, h0_ref, dy_ref, w_ref, gx_ref, d0_ref, gw_ref):
        i = pl.program_id(0)

        @pl.when(i == 0)
        def _():
            gw_ref[...] = jnp.zeros_like(gw_ref)

        h = _h_chunk(i, nx, h0_ref, x_ref)
        dhn = dhn_ref[...]
        rstd = lax.rsqrt(jnp.mean(h * h, axis=-1, keepdims=True) + EPS)
        xhat = h * rstd
        gw_ref[...] += jnp.sum(dhn * xhat, axis=0, keepdims=True)
        dxhat = dhn * w_ref[...]
        dh = rstd * (dxhat - xhat * jnp.mean(dxhat * xhat, axis=-1, keepdims=True)) + dy_ref[...]

        @pl.when(i == 0)
        def _():
            d0_ref[...] = dh

        @pl.when((i > 0) & (i <= nx))
        def _():
            gx_ref[...] = dh

    row = pl.BlockSpec((CHUNK, D_MODEL), lambda i: (i, 0))
    return pl.pallas_call(
        body, name="prenorm_bwd", grid=(nc,),
        in_specs=[row, _x_spec(nx), pl.BlockSpec((CHUNK, D_MODEL), lambda i: (0, 0)), row,
                  pl.BlockSpec((1, D_MODEL), lambda i: (0, 0))],
        out_specs=[_x_spec(nx), pl.BlockSpec((CHUNK, D_MODEL), lambda i: (0, 0)),
                   pl.BlockSpec((1, D_MODEL), lambda i: (0, 0))],
        out_shape=[
            jax.ShapeDtypeStruct((nx * CHUNK, D_MODEL), F32),
            jax.ShapeDtypeStruct((CHUNK, D_MODEL), F32),
            jax.ShapeDtypeStruct((1, D_MODEL), F32),
        ],
        compiler_params=_params(("arbitrary",)),
    )(dhn, x2d, h0, dy, w)


def _adamw(slabs, w, m, v, name):
    rows, cols = w.shape
    tr = _tile(rows, (256, 128, 64, 16, 8))
    c1 = 1.0 - ADAM_B1 ** ADAM_STEP
    c2 = 1.0 - ADAM_B2 ** ADAM_STEP

    def body(s_ref, w_ref, m_ref, v_ref, g_ref, d_ref, mo_ref, vo_ref):
        g = s_ref[0].astype(F32)
        for k in range(1, N_DEV):
            g = g + s_ref[k].astype(F32)
        w = w_ref[...]
        m = ADAM_B1 * m_ref[...] + (1.0 - ADAM_B1) * g
        v = ADAM_B2 * v_ref[...] + (1.0 - ADAM_B2) * (g * g)
        g_ref[...] = g
        mo_ref[...] = m
        vo_ref[...] = v
        d_ref[...] = -ADAM_LR * ((m / c1) / (jnp.sqrt(v / c2) + ADAM_EPS) + ADAM_WD * w)

    blk = pl.BlockSpec((tr, cols), lambda i: (i, 0))
    shape = jax.ShapeDtypeStruct((rows, cols), F32)
    return pl.pallas_call(
        body, name=name, grid=(rows // tr,),
        in_specs=[pl.BlockSpec((N_DEV, tr, cols), lambda i: (0, i, 0)), blk, blk, blk],
        out_specs=[blk, blk, blk, blk],
        out_shape=[shape, shape, shape, shape],
        compiler_params=_params(("parallel",)),
    )(slabs, w, m, v)


def _perm_xbc(a):
    lead = a.shape[:-1]
    xs = a[..., :D_SSD].reshape(lead + (N_GROUPS, GROUP_W))
    b = a[..., D_SSD:D_SSD + N_GROUPS * D_STATE].reshape(lead + (N_GROUPS, D_STATE))
    c = a[..., D_SSD + N_GROUPS * D_STATE:].reshape(lead + (N_GROUPS, D_STATE))
    return jnp.concatenate([xs, b, c], axis=-1).reshape(lead + (D_CONV,))


def _unperm_xbc(a):
    lead = a.shape[:-1]
    t = a.reshape(lead + (N_GROUPS, XBC_BLK))
    xs = t[..., :GROUP_W].reshape(lead + (D_SSD,))
    b = t[..., GROUP_W:GROUP_W + D_STATE].reshape(lead + (N_GROUPS * D_STATE,))
    c = t[..., GROUP_W + D_STATE:].reshape(lead + (N_GROUPS * D_STATE,))
    return jnp.concatenate([xs, b, c], axis=-1)


R_Z, R_XBC, R_DT, R_Q, R_K, R_V, R_G = 0, 2048, 6144, 6176, 7200, 7456, 7712


def _w_in_internal(w_gathered):
    w = jnp.transpose(w_gathered, (1, 0, 2)).reshape(D_MODEL, D_IN_PROJ)
    pad = jnp.zeros((D_MODEL, NP - OFF_DT - SSD_HEADS), w.dtype)
    return jnp.concatenate([
        w[:, R_Z:R_XBC], _perm_xbc(w[:, R_XBC:R_DT]), w[:, R_Q:R_K], w[:, R_G:], w[:, R_K:R_V], w[:, R_V:R_G],
        w[:, R_DT:R_Q], pad], axis=1)


def _w_in_slabs(dw):
    w = jnp.concatenate([
        dw[:, OFF_Z:OFF_XBC], _unperm_xbc(dw[:, OFF_XBC:OFF_Q]), dw[:, OFF_DT:OFF_DT + SSD_HEADS],
        dw[:, OFF_Q:OFF_G], dw[:, OFF_K:OFF_V], dw[:, OFF_V:OFF_DT], dw[:, OFF_G:OFF_K]], axis=1)
    return jnp.transpose(w.reshape(D_MODEL, N_DEV, SHARD_IN), (1, 0, 2))


def _rep_heads(a):
    return jnp.repeat(a, HEAD_DIM, axis=1)


SMALL = (("norm_pre_w", 2048), ("conv_b", 4096), ("dt_bias", 32), ("a_log", 32), ("d_skip", 32),
         ("ssd_norm_w", 2048), ("attn_sinks", 16), ("norm_post_w", 2048))
SMALL_USED = sum(size for _, size in SMALL)
SMALL_LEN = 10368


def _pack_small(d, loss=None):
    parts = [d[name].reshape(1, size) for name, size in SMALL]
    tail = jnp.zeros((1, SMALL_LEN - SMALL_USED), F32)
    if loss is not None:
        tail = tail.at[0, 0].set(loss)
    return jnp.concatenate(parts + [tail], axis=1)


def _unpack_small(vec):
    out, off = {}, 0
    for name, size in SMALL:
        out[name] = vec[:, off:off + size]
        off += size
    return out


def kernel(x, meta_tokens, norm_pre_w, w_in, conv_w, conv_b, dt_bias, a_log, d_skip, ssd_norm_w, attn_sinks, w_out, norm_post_w, loss_target, m_meta_tokens, m_norm_pre_w, m_w_in, m_conv_w, m_conv_b, m_dt_bias, m_a_log, m_d_skip, m_ssd_norm_w, m_attn_sinks, m_w_out, m_norm_post_w, v_meta_tokens, v_norm_pre_w, v_w_in, v_conv_w, v_conv_b, v_dt_bias, v_a_log, v_d_skip, v_ssd_norm_w, v_attn_sinks, v_w_out, v_norm_post_w):
    seq = x.shape[1]
    lp = seq + 2 * CHUNK
    x2d = x[0]

    w_in_g, conv_w_g, meta_g = _gather_two_level([w_in[0].astype(BF16), conv_w[0], meta_tokens], "gather_w_in")
    w_all = _w_in_internal(w_in_g)
    conv_w_full = _perm_xbc(jnp.transpose(conv_w_g, (1, 0, 2)).reshape(CONV_WIDTH, D_CONV))
    conv_b_int = _perm_xbc(conv_b)
    meta_full = jnp.transpose(meta_g, (1, 0, 2)).reshape(N_META, D_MODEL)
    h0 = jnp.concatenate([jnp.zeros((PAD_LEAD, D_MODEL), F32), meta_full], axis=0)

    pos = (jnp.arange(lp) - PAD_LEAD).astype(F32)
    half = HEAD_DIM // 2
    inv = ROPE_THETA ** (-jnp.arange(half, dtype=F32) / half)
    ang = pos[:, None] * inv[None, :]
    cos_t = jnp.tile(jnp.cos(ang), (1, 4))
    sin_t = jnp.tile(jnp.concatenate([-jnp.sin(ang), jnp.sin(ang)], axis=1), (1, 2))
    head_of_col = np.arange(D_SSD) // HEAD_DIM
    expand = jnp.asarray((np.arange(128)[:, None] == head_of_col[None, :]).astype(np.float32))
    reduce_t = jnp.asarray((head_of_col[:, None] == np.arange(128)[None, :]).astype(np.float32))
    tri = jnp.asarray(np.tril(np.ones((CHUNK, CHUNK), np.float32)))
    sel = jnp.asarray((np.arange(8)[:, None] * HEAD_DIM == np.arange(GROUP_W)[None, :]).astype(np.float32))
    a_rep = _rep_heads(-jnp.exp(a_log))
    dsk_rep = _rep_heads(d_skip)
    dt_bias_rep = _rep_heads(dt_bias)
    sink_stack = jnp.repeat(attn_sinks.reshape(KV_HEADS, REP), CHUNK, axis=1).reshape(KV_HEADS, REP * CHUNK, 1)

    hn = _prenorm(x2d, h0, norm_pre_w)
    tm = _tile(lp, (1056, 704, 128, 64))
    proj, w_out_g = _matmul(hn, w_all, tm=tm, tn=1536, tk=D_MODEL, out_dtype=F32, name="in_proj",
                            comm=_Comm([(w_out[0].astype(BF16), "gather")]))
    w_out_full = w_out_g.reshape(D_MIX, D_MODEL)
    xbc = _conv_fwd(proj, conv_w_full, conv_b_int)
    qr, kr, vb, dt_rep = _act_fwd(proj, cos_t, sin_t, expand, dt_bias_rep)
    mix, ytot, hprev = _ssd_fwd(xbc, dt_rep, proj, a_rep, dsk_rep, ssd_norm_w, tri, sel)
    att, mix = _attn_fwd(qr, kr, vb, proj, sink_stack, mix)
    out = _matmul(mix, w_out_full, tm=tm, tn=1024, tk=D_MIX, out_dtype=F32, name="out_proj")
    dout, dy, g_norm_post, loss_part = _post_loss(out, x2d, loss_target[0], norm_post_w)

    dmix = _matmul(dout, w_out_full, trans_b=True, tm=tm, tn=1024, tk=D_MODEL, out_dtype=F32, name="dmix")
    dw_out = _matmul(mix, dout, trans_a=True, tm=512, tn=1024, tk=lp, out_dtype=BF16, name="dw_out")
    dqr, dkr, dv, dg, dsink_rows = _attn_bwd(qr, kr, vb, att, proj, dmix, sink_stack)
    dproj, dxbc, ddt_part, dd_part, da_part, g_ssd_norm, g_out = _ssd_bwd(
        dmix, ytot, proj, xbc, dt_rep, hprev, a_rep, dsk_rep, ssd_norm_w, tri, sel,
        _Comm([(dw_out.reshape(N_DEV, D_MIX // N_DEV, D_MODEL), "scatter")]))
    dproj, dconv_w_int, dconv_b_int = _conv_bwd(dxbc, proj, conv_w_full, conv_b_int, dproj)
    dproj, ddt_bias = _act_bwd(dqr, dkr, dv, dg, ddt_part, cos_t, sin_t, reduce_t, dproj)
    dw_all = _matmul(hn, dproj, trans_a=True, tm=512, tn=1024, tk=lp, out_dtype=BF16, name="dw_in")
    dhn, g_in = _matmul(dproj, w_all, trans_b=True, tm=tm, tn=1024, tk=1536, out_dtype=F32, name="dhn",
                        comm=_Comm([(_w_in_slabs(dw_all), "scatter")]))
    grad_x, dh0, g_norm_pre = _prenorm_bwd(dhn, x2d, h0, dy, norm_pre_w)

    dmeta = dh0[PAD_LEAD:, :]
    dconv_w_ref = _unperm_xbc(dconv_w_int)
    heads = lambda part: part.reshape(SSD_HEADS, HEAD_DIM).sum(axis=1).reshape(1, SSD_HEADS)
    small_local = _pack_small({
        "norm_pre_w": g_norm_pre, "conv_b": _unperm_xbc(dconv_b_int), "dt_bias": ddt_bias[:, :SSD_HEADS],
        "a_log": heads(da_part) * (-jnp.exp(a_log)), "d_skip": heads(dd_part), "ssd_norm_w": g_ssd_norm,
        "attn_sinks": dsink_rows.reshape(Q_HEADS, CHUNK).sum(axis=1).reshape(1, Q_HEADS),
        "norm_post_w": g_norm_post}, loss=loss_part[0, 0])
    g_conv, g_meta, g_small = _exchange(
        [(jnp.transpose(dconv_w_ref.reshape(CONV_WIDTH, N_DEV, D_CONV // N_DEV), (1, 0, 2)), "scatter"),
         (jnp.transpose(dmeta.reshape(N_META, N_DEV, D_MODEL // N_DEV), (1, 0, 2)), "scatter"),
         (small_local, "gather")], "exchange_small")

    res = {}
    res["w_in"] = _adamw(g_in, w_in[0], m_w_in[0], v_w_in[0], "adamw_w_in")
    res["w_out"] = _adamw(g_out, w_out[0], m_w_out[0], v_w_out[0], "adamw_w_out")
    res["conv_w"] = _adamw(g_conv, conv_w[0], m_conv_w[0], v_conv_w[0], "adamw_conv_w")
    res["meta_tokens"] = _adamw(g_meta, meta_tokens, m_meta_tokens, v_meta_tokens, "adamw_meta")
    given = dict(norm_pre_w=(norm_pre_w, m_norm_pre_w, v_norm_pre_w), conv_b=(conv_b, m_conv_b, v_conv_b),
                 dt_bias=(dt_bias, m_dt_bias, v_dt_bias), a_log=(a_log, m_a_log, v_a_log),
                 d_skip=(d_skip, m_d_skip, v_d_skip), ssd_norm_w=(ssd_norm_w, m_ssd_norm_w, v_ssd_norm_w),
                 attn_sinks=(attn_sinks, m_attn_sinks, v_attn_sinks),
                 norm_post_w=(norm_post_w, m_norm_post_w, v_norm_post_w))
    packed = [_pack_small({k: t[j] for k, t in given.items()}) for j in range(3)]
    small_out = _adamw(g_small, packed[0], packed[1], packed[2], "adamw_small")
    small_res = [_unpack_small(r) for r in small_out]
    loss = small_out[0][0, SMALL_USED]

    order = ["meta_tokens", "norm_pre_w", "w_in", "conv_w", "conv_b", "dt_bias", "a_log", "d_skip", "ssd_norm_w",
             "attn_sinks", "w_out", "norm_post_w"]
    lead = {"w_in": True, "conv_w": True, "w_out": True}
    outs = []
    for j in range(4):
        for name in order:
            if name in res:
                o = res[name][j]
                outs.append(o[None] if name in lead else o)
            else:
                outs.append(small_res[j][name])
    return (loss, grad_x[None], *outs)
```

```python
import functools

import numpy as np
import jax
import jax.numpy as jnp
from jax import lax
from jax.experimental import pallas as pl
from jax.experimental.pallas import tpu as pltpu

F32 = jnp.float32
BF16 = jnp.bfloat16
HIGHEST = lax.Precision.HIGHEST

N_DEV = 8
D_MODEL = 2048
CHUNK = 64
N_META = 16
PAD_LEAD = CHUNK - N_META
EPS = 1e-6
N_GROUPS = 8
HEADS_PER_GROUP = 4
HEAD_DIM = 64
GROUP_W = HEADS_PER_GROUP * HEAD_DIM
D_STATE = 128
D_SSD = 2048
D_CONV = 4096
SSD_HEADS = 32
CONV_WIDTH = 4
Q_HEADS = 16
KV_HEADS = 4
REP = 4
D_ATT = 1024
D_KV = 256
WINDOW_CHUNKS = 2
BAND = (WINDOW_CHUNKS + 1) * CHUNK
ROPE_THETA = 10000.0
D_MIX = D_SSD + D_ATT
D_IN_PROJ = 8736
SHARD_IN = D_IN_PROJ // N_DEV

OFF_Z, OFF_XBC, OFF_Q, OFF_G, OFF_K, OFF_V, OFF_DT = 0, 2048, 6144, 7168, 8192, 8448, 8704
NP = 9216
TAIL_W = NP - OFF_Q
XBC_BLK = 512
SSD_GPS = 8

ADAM_LR, ADAM_B1, ADAM_B2, ADAM_EPS, ADAM_WD, ADAM_STEP = 0.001, 0.9, 0.999, 1e-08, 0.01, 10

VMEM_LIMIT = 48 * 1024 * 1024

NN = (((1,), (0,)), ((), ()))
NT = (((1,), (1,)), ((), ()))
TN = (((0,), (0,)), ((), ()))
ANY = pl.BlockSpec(memory_space=pl.ANY)


def _dot(a, b, dims=NN, precision=None):
    return lax.dot_general(a, b, dims, precision=precision, preferred_element_type=F32)


def _tile(n, prefs):
    for t in prefs:
        if n % t == 0:
            return t
    return n


def _params(sem, vmem=VMEM_LIMIT, side_effects=False):
    return pltpu.CompilerParams(dimension_semantics=sem, vmem_limit_bytes=vmem, has_side_effects=side_effects)


def _sigmoid(x):
    return 1.0 / (1.0 + jnp.exp(-x))


class _Comm:
    def __init__(self, items):
        self.items = items
        self.n = n = len(items)
        self.operands = [arr for arr, _ in items]
        self.in_specs = [ANY] * n
        self.out_specs = [ANY] * n
        self.out_shape = [jax.ShapeDtypeStruct((N_DEV,) + tuple(arr.shape) if kind == "gather" else tuple(arr.shape),
                                               arr.dtype) for arr, kind in items]
        self.scratch = [pltpu.SemaphoreType.DMA((n, N_DEV - 1)), pltpu.SemaphoreType.DMA((n, N_DEV - 1)),
                        pltpu.SemaphoreType.DMA((n,))]

    @staticmethod
    def _places():
        pos = (lax.axis_index("x"), lax.axis_index("y"), lax.axis_index("c"))
        me = 4 * pos[0] + 2 * pos[1] + pos[2]
        peers = []
        for k in range(1, N_DEV):
            p = tuple(1 - pos[b] if (k >> (2 - b)) & 1 else pos[b] for b in range(3))
            peers.append((p, 4 * p[0] + 2 * p[1] + p[2]))
        return me, peers

    def _copies(self, ins, outs, sems, landed):
        send_sems, recv_sems, local_sems = sems
        me, peers = self._places()
        local, remote = [], []
        for a, (_, kind) in enumerate(self.items):
            own = ins[a] if kind == "gather" else ins[a].at[me]
            local.append(pltpu.make_async_copy(own, outs[a].at[me], local_sems.at[a]))
            for k, (p, pid) in enumerate(peers):
                remote.append(pltpu.make_async_remote_copy(
                    src_ref=ins[a] if kind == "gather" else ins[a].at[pid],
                    dst_ref=outs[a].at[pid if landed else me],
                    send_sem=send_sems.at[a, k], recv_sem=recv_sems.at[a, k],
                    device_id=p, device_id_type=pl.DeviceIdType.MESH))
        return local, remote

    def start(self, ins, outs, sems):
        local, remote = self._copies(ins, outs, sems, landed=False)
        for cp in local + remote:
            cp.start()

    def wait(self, ins, outs, sems):
        local, remote = self._copies(ins, outs, sems, landed=True)
        for cp in remote + local:
            cp.wait()


def _exchange(items, name):
    comm = _Comm(items)
    n = comm.n

    def body(*refs):
        ins, outs, sems = refs[:n], refs[n:2 * n], refs[2 * n:]
        comm.start(ins, outs, sems)
        comm.wait(ins, outs, sems)

    return pl.pallas_call(
        body, name=name, in_specs=comm.in_specs, out_specs=comm.out_specs, out_shape=comm.out_shape,
        scratch_shapes=comm.scratch, compiler_params=pltpu.CompilerParams(has_side_effects=True),
    )(*comm.operands)


def _gather_two_level(arrays, name):
    n = len(arrays)

    def body(*refs):
        ins, outs = refs[:n], refs[n:2 * n]
        send_sems, recv_sems, local_sems = refs[2 * n:]
        x, y, c = lax.axis_index("x"), lax.axis_index("y"), lax.axis_index("c")
        me, sibling = (x, y, c), (x, y, 1 - c)
        chips = [(1 - x, y), (x, 1 - y), (1 - x, 1 - y)]

        def slab(a, place):
            return outs[a].at[4 * place[0] + 2 * place[1] + place[2]]

        def copy(a, k, block, to, src=None):
            return pltpu.make_async_remote_copy(
                src_ref=slab(a, block) if src is None else src, dst_ref=slab(a, block),
                send_sem=send_sems.at[a, k], recv_sem=recv_sems.at[a, k],
                device_id=to, device_id_type=pl.DeviceIdType.MESH)

        sends, mine = [], []
        for a in range(n):
            loc = pltpu.make_async_copy(ins[a], slab(a, me), local_sems.at[a])
            loc.start()
            mine.append(loc)
            sends.append(copy(a, 0, me, sibling, src=ins[a]))
            sends += [copy(a, 1 + j, me, (*chip, c), src=ins[a]) for j, chip in enumerate(chips)]
        for cp in sends:
            cp.start()
        for j, chip in enumerate(chips):
            for a in range(n):
                copy(a, 1 + j, (*chip, c), me).wait_recv()
                fwd = copy(a, 4 + j, (*chip, c), sibling)
                fwd.start()
                sends.append(fwd)
        for a in range(n):
            copy(a, 0, sibling, me).wait_recv()
            for j, chip in enumerate(chips):
                copy(a, 4 + j, (*chip, 1 - c), me).wait_recv()
        for cp in sends:
            cp.wait_send()
        for loc in mine:
            loc.wait()

    return pl.pallas_call(
        body, name=name, in_specs=[ANY] * n, out_specs=[ANY] * n,
        out_shape=[jax.ShapeDtypeStruct((N_DEV,) + tuple(a.shape), a.dtype) for a in arrays],
        scratch_shapes=[pltpu.SemaphoreType.DMA((n, N_DEV - 1)), pltpu.SemaphoreType.DMA((n, N_DEV - 1)),
                        pltpu.SemaphoreType.DMA((n,))],
        compiler_params=pltpu.CompilerParams(has_side_effects=True),
    )(*arrays)


def _matmul(a, b, *, tm, tn, tk, out_dtype, name, trans_a=False, trans_b=False, comm=None):
    m, k = (a.shape[1], a.shape[0]) if trans_a else a.shape
    n = b.shape[0] if trans_b else b.shape[1]
    nk = k // tk
    dims = TN if trans_a else (NT if trans_b else NN)
    assert not (trans_a and trans_b)
    nc = comm.n if comm else 0
    grid = (m // tm, n // tn, nk)

    def body(*refs):
        a_ref, b_ref = refs[:2]
        cin = refs[2:2 + nc]
        o_ref = refs[2 + nc]
        cout = refs[3 + nc:3 + 2 * nc]
        scratch = refs[3 + 2 * nc:]
        sems = scratch[len(scratch) - 3:] if comm else None
        i, j, kk = pl.program_id(0), pl.program_id(1), pl.program_id(2)
        if comm:
            @pl.when((i == 0) & (j == 0) & (kk == 0))
            def _():
                comm.start(cin, cout, sems)

        if nk == 1:
            o_ref[...] = _dot(a_ref[...], b_ref[...], dims).astype(out_dtype)
        else:
            acc_ref = scratch[0]

            @pl.when(kk == 0)
            def _():
                acc_ref[...] = jnp.zeros_like(acc_ref)

            acc_ref[...] += _dot(a_ref[...], b_ref[...], dims)

            @pl.when(kk == nk - 1)
            def _():
                o_ref[...] = acc_ref[...].astype(out_dtype)

        if comm:
            @pl.when((i == grid[0] - 1) & (j == grid[1] - 1) & (kk == nk - 1))
            def _():
                comm.wait(cin, cout, sems)

    a_spec = (pl.BlockSpec((tk, tm), lambda i, j, kk: (kk, i)) if trans_a
              else pl.BlockSpec((tm, tk), lambda i, j, kk: (i, kk)))
    b_spec = (pl.BlockSpec((tn, tk), lambda i, j, kk: (j, kk)) if trans_b
              else pl.BlockSpec((tk, tn), lambda i, j, kk: (kk, j)))
    sem = ("arbitrary",) * 3 if comm else ("parallel", "parallel", "arbitrary")
    res = pl.pallas_call(
        body, name=name, grid=grid,
        in_specs=[a_spec, b_spec] + (comm.in_specs if comm else []),
        out_specs=[pl.BlockSpec((tm, tn), lambda i, j, kk: (i, j))] + (comm.out_specs if comm else []),
        out_shape=[jax.ShapeDtypeStruct((m, n), out_dtype)] + (comm.out_shape if comm else []),
        scratch_shapes=([] if nk == 1 else [pltpu.VMEM((tm, tn), F32)]) + (comm.scratch if comm else []),
        compiler_params=_params(sem, side_effects=bool(comm)),
    )(a, b, *(comm.operands if comm else []))
    return res if comm else res[0]


def _h_chunk(i, nx, h0_ref, x_ref):
    h = jnp.where(i == 0, h0_ref[...], x_ref[...])
    return h * (i <= nx).astype(F32)


def _x_spec(nx):
    return pl.BlockSpec((CHUNK, D_MODEL), lambda i: (jnp.clip(i - 1, 0, nx - 1), 0))


def _prenorm(x2d, h0, w):
    nx = x2d.shape[0] // CHUNK
    nc = nx + 2

    def body(x_ref, h0_ref, w_ref, o_ref):
        h = _h_chunk(pl.program_id(0), nx, h0_ref, x_ref)
        rstd = lax.rsqrt(jnp.mean(h * h, axis=-1, keepdims=True) + EPS)
        o_ref[...] = (h * rstd * w_ref[...]).astype(BF16)

    return pl.pallas_call(
        body, name="prenorm", grid=(nc,),
        in_specs=[_x_spec(nx), pl.BlockSpec((CHUNK, D_MODEL), lambda i: (0, 0)), pl.BlockSpec((1, D_MODEL), lambda i: (0, 0))],
        out_specs=pl.BlockSpec((CHUNK, D_MODEL), lambda i: (i, 0)),
        out_shape=jax.ShapeDtypeStruct((nc * CHUNK, D_MODEL), BF16),
        compiler_params=_params(("parallel",)),
    )(x2d, h0, w)


CONV_COLS = 512
HALO = 8


def _conv_pre(ext, w, b):
    acc = b + w[3:4, :] * ext[HALO:, :]
    for j in range(1, CONV_WIDTH):
        acc = acc + w[3 - j:4 - j, :] * pltpu.roll(ext, j, 0)[HALO:, :]
    return acc


def _conv_fwd(proj, conv_w, conv_b):
    lp = proj.shape[0]
    t = _tile(lp, (704, 384, 128, 64))
    hb = t // HALO
    c0 = OFF_XBC // CONV_COLS

    def body(u_ref, halo_ref, w_ref, b_ref, o_ref):
        i = pl.program_id(0)
        halo = halo_ref[...] * (i > 0).astype(F32)
        pre = _conv_pre(jnp.concatenate([halo, u_ref[...]], axis=0), w_ref[...], b_ref[...])
        o_ref[...] = pre * _sigmoid(pre)

    return pl.pallas_call(
        body, name="conv_fwd", grid=(lp // t, D_CONV // CONV_COLS),
        in_specs=[
            pl.BlockSpec((t, CONV_COLS), lambda i, j: (i, c0 + j)),
            pl.BlockSpec((HALO, CONV_COLS), lambda i, j: (jnp.maximum(i * hb - 1, 0), c0 + j)),
            pl.BlockSpec((CONV_WIDTH, CONV_COLS), lambda i, j: (0, j)),
            pl.BlockSpec((1, CONV_COLS), lambda i, j: (0, j)),
        ],
        out_specs=pl.BlockSpec((t, CONV_COLS), lambda i, j: (i, j)),
        out_shape=jax.ShapeDtypeStruct((lp, D_CONV), F32),
        compiler_params=_params(("parallel", "parallel")),
    )(proj, proj, conv_w, conv_b)


def _conv_bwd(dxbc, proj, conv_w, conv_b, dproj):
    lp = proj.shape[0]
    t = _tile(lp, (704, 384, 128, 64))
    hb = t // HALO
    nt = lp // t
    c0 = OFF_XBC // CONV_COLS

    def body(dx_ref, dxn_ref, u_ref, up_ref, un_ref, w_ref, b_ref, _, du_ref, dw_ref, db_ref):
        i = pl.program_id(1)
        w = w_ref[...]
        up = up_ref[...] * (i > 0).astype(F32)
        ext = jnp.concatenate([up, u_ref[...], un_ref[...]], axis=0)
        pre = _conv_pre(ext, w, b_ref[...])
        dxn = dxn_ref[...] * (i < nt - 1).astype(F32)
        dxe = jnp.concatenate([dx_ref[...], dxn], axis=0)
        sg = _sigmoid(pre)
        dpre = dxe * sg * (1.0 + pre * (1.0 - sg))
        du = w[3:4, :] * dpre[:t, :]
        for j in range(1, CONV_WIDTH):
            du = du + w[3 - j:4 - j, :] * pltpu.roll(dpre, t + HALO - j, 0)[:t, :]
        du_ref[...] = du.astype(BF16)

        @pl.when(i == 0)
        def _():
            dw_ref[...] = jnp.zeros_like(dw_ref)
            db_ref[...] = jnp.zeros_like(db_ref)

        dp = dpre[:t, :]
        db_ref[...] += jnp.sum(dp, axis=0, keepdims=True)
        hist = ext[:HALO + t, :]
        dw_ref[3:4, :] += jnp.sum(dp * hist[HALO:, :], axis=0, keepdims=True)
        for j in range(1, CONV_WIDTH):
            dw_ref[3 - j:4 - j, :] += jnp.sum(dp * pltpu.roll(hist, j, 0)[HALO:, :], axis=0, keepdims=True)

    nxt = lambda i: jnp.minimum((i + 1) * hb, lp // HALO - 1)
    return pl.pallas_call(
        body, name="conv_bwd", grid=(D_CONV // CONV_COLS, nt),
        in_specs=[
            pl.BlockSpec((t, CONV_COLS), lambda j, i: (i, j)),
            pl.BlockSpec((HALO, CONV_COLS), lambda j, i: (nxt(i), j)),
            pl.BlockSpec((t, CONV_COLS), lambda j, i: (i, c0 + j)),
            pl.BlockSpec((HALO, CONV_COLS), lambda j, i: (jnp.maximum(i * hb - 1, 0), c0 + j)),
            pl.BlockSpec((HALO, CONV_COLS), lambda j, i: (nxt(i), c0 + j)),
            pl.BlockSpec((CONV_WIDTH, CONV_COLS), lambda j, i: (0, j)),
            pl.BlockSpec((1, CONV_COLS), lambda j, i: (0, j)),
            ANY,
        ],
        out_specs=[
            pl.BlockSpec((t, CONV_COLS), lambda j, i: (i, c0 + j)),
            pl.BlockSpec((CONV_WIDTH, CONV_COLS), lambda j, i: (0, j)),
            pl.BlockSpec((1, CONV_COLS), lambda j, i: (0, j)),
        ],
        out_shape=[
            jax.ShapeDtypeStruct((lp, NP), BF16),
            jax.ShapeDtypeStruct((CONV_WIDTH, D_CONV), F32),
            jax.ShapeDtypeStruct((1, D_CONV), F32),
        ],
        input_output_aliases={7: 0},
        compiler_params=_params(("parallel", "arbitrary")),
    )(dxbc, dxbc, proj, proj, proj, conv_w, conv_b, dproj)


def _swap_halves(t):
    w = t.shape[-1]
    lane = lax.broadcasted_iota(jnp.int32, t.shape, 1)
    return jnp.where((lane % HEAD_DIM) < HEAD_DIM // 2, pltpu.roll(t, w - HEAD_DIM // 2, 1),
                     pltpu.roll(t, HEAD_DIM // 2, 1))


def _act_fwd(proj, cos_t, sin_t, expand, dt_bias_rep):
    lp = proj.shape[0]
    t = _tile(lp, (384, 128, 64))

    def body(q_ref, k_ref, v_ref, dt_ref, cos_ref, sin_ref, ex_ref, bias_ref, qo_ref, ko_ref, vo_ref, dto_ref):
        i = pl.program_id(0)
        cos = cos_ref[...]
        sin = sin_ref[...]
        q = q_ref[...]
        qo_ref[...] = (q * jnp.tile(cos, (1, D_ATT // 128)) + _swap_halves(q) * jnp.tile(sin, (1, D_ATT // 128))).astype(BF16)
        k = k_ref[...]
        ko_ref[...] = (k * jnp.tile(cos, (1, D_KV // 128)) + _swap_halves(k) * jnp.tile(sin, (1, D_KV // 128))).astype(BF16)
        vo_ref[...] = v_ref[...].astype(BF16)
        raw = _dot(dt_ref[...], ex_ref[...], NN, HIGHEST) + bias_ref[...]
        sp = jnp.maximum(raw, 0.0) + jnp.log1p(jnp.exp(-jnp.abs(raw)))
        row = i * t + lax.broadcasted_iota(jnp.int32, sp.shape, 0)
        dto_ref[...] = jnp.where(row >= PAD_LEAD, sp, 0.0)

    return pl.pallas_call(
        body, name="act_fwd", grid=(lp // t,),
        in_specs=[
            pl.BlockSpec((t, D_ATT), lambda i: (i, OFF_Q // D_ATT)),
            pl.BlockSpec((t, D_KV), lambda i: (i, OFF_K // D_KV)),
            pl.BlockSpec((t, D_KV), lambda i: (i, OFF_V // D_KV)),
            pl.BlockSpec((t, 128), lambda i: (i, OFF_DT // 128)),
            pl.BlockSpec((t, 128), lambda i: (i, 0)),
            pl.BlockSpec((t, 128), lambda i: (i, 0)),
            pl.BlockSpec((128, D_SSD), lambda i: (0, 0)),
            pl.BlockSpec((1, D_SSD), lambda i: (0, 0)),
        ],
        out_specs=[
            pl.BlockSpec((t, D_ATT), lambda i: (i, 0)),
            pl.BlockSpec((t, D_KV), lambda i: (i, 0)),
            pl.BlockSpec((t, D_KV), lambda i: (i, 0)),
            pl.BlockSpec((t, D_SSD), lambda i: (i, 0)),
        ],
        out_shape=[
            jax.ShapeDtypeStruct((lp, D_ATT), BF16),
            jax.ShapeDtypeStruct((lp, D_KV), BF16),
            jax.ShapeDtypeStruct((lp, D_KV), BF16),
            jax.ShapeDtypeStruct((lp, D_SSD), F32),
        ],
        compiler_params=_params(("parallel",)),
    )(proj, proj, proj, proj, cos_t, sin_t, expand, dt_bias_rep)


def _act_bwd(dqr, dkr, dv, dg, ddt_part, cos_t, sin_t, reduce_t, dproj):
    lp = dqr.shape[0]
    t = _tile(lp, (384, 128, 64))

    def body(dq_ref, dk_ref, dv_ref, dg_ref, ddt_ref, cos_ref, sin_ref, red_ref, _, o_ref, db_ref):
        i = pl.program_id(0)
        cos = cos_ref[...]
        sin = sin_ref[...]
        dq = dq_ref[...]
        dq = dq * jnp.tile(cos, (1, D_ATT // 128)) + _swap_halves(dq * jnp.tile(sin, (1, D_ATT // 128)))
        dk = dk_ref[...]
        dk = dk * jnp.tile(cos, (1, D_KV // 128)) + _swap_halves(dk * jnp.tile(sin, (1, D_KV // 128)))
        ddt = _dot(ddt_ref[...], red_ref[...], NN, HIGHEST)
        o_ref[...] = jnp.concatenate(
            [dq.astype(BF16), dg_ref[...].astype(BF16), dk.astype(BF16), dv_ref[...].astype(BF16), ddt.astype(BF16),
             jnp.zeros((t, NP - OFF_DT - 128), BF16)], axis=1)

        @pl.when(i == 0)
        def _():
            db_ref[...] = jnp.zeros_like(db_ref)

        db_ref[...] += jnp.sum(ddt, axis=0, keepdims=True)

    return pl.pallas_call(
        body, name="act_bwd", grid=(lp // t,),
        in_specs=[
            pl.BlockSpec((t, D_ATT), lambda i: (i, 0)),
            pl.BlockSpec((t, D_KV), lambda i: (i, 0)),
            pl.BlockSpec((t, D_KV), lambda i: (i, 0)),
            pl.BlockSpec((t, D_ATT), lambda i: (i, 0)),
            pl.BlockSpec((t, D_SSD), lambda i: (i, 0)),
            pl.BlockSpec((t, 128), lambda i: (i, 0)),
            pl.BlockSpec((t, 128), lambda i: (i, 0)),
            pl.BlockSpec((D_SSD, 128), lambda i: (0, 0)),
            ANY,
        ],
        out_specs=[pl.BlockSpec((t, TAIL_W), lambda i: (i, OFF_Q // TAIL_W)), pl.BlockSpec((1, 128), lambda i: (0, 0))],
        out_shape=[jax.ShapeDtypeStruct((lp, NP), BF16), jax.ShapeDtypeStruct((1, 128), F32)],
        input_output_aliases={8: 0},
        compiler_params=_params(("arbitrary",)),
    )(dqr, dkr, dv, dg, ddt_part, cos_t, sin_t, reduce_t, dproj)


def _ssd_common(xbc, dt, a, tri, sel):
    xs = xbc[:, :GROUP_W]
    b = xbc[:, GROUP_W:GROUP_W + D_STATE]
    c = xbc[:, GROUP_W + D_STATE:]
    cs = _dot(tri, dt * a, NN, HIGHEST)
    cs_t = _dot(sel, cs, NT, HIGHEST)
    xdt = xs * dt
    cs_last = cs[CHUNK - 1:CHUNK, :]
    return xs, b, c, cs, cs_t, xdt, jnp.exp(cs), jnp.exp(cs_last - cs), jnp.exp(cs_last)


def _decay_matrix(cs, cs_t, r, causal):
    seg = cs[:, r * HEAD_DIM:r * HEAD_DIM + 1] - cs_t[r:r + 1, :]
    return jnp.exp(jnp.where(causal, seg, -jnp.inf))


def _causal_mask():
    row = lax.broadcasted_iota(jnp.int32, (CHUNK, CHUNK), 0)
    col = lax.broadcasted_iota(jnp.int32, (CHUNK, CHUNK), 1)
    return row >= col


def _ssd_fwd(xbc, dt_rep, proj, a_rep, dsk_rep, wn, tri, sel):
    lp = xbc.shape[0]
    nc = lp // CHUNK
    gw = SSD_GPS * GROUP_W

    def body(xbc_ref, dt_ref, z_ref, a_ref, dsk_ref, wn_ref, tri_ref, sel_ref, yn_ref, ytot_ref, hprev_ref, h_scr):
        @pl.when(pl.program_id(1) == 0)
        def _():
            h_scr[...] = jnp.zeros_like(h_scr)

        causal = _causal_mask()
        G = range(SSD_GPS)
        colsl = [slice(gi * GROUP_W, (gi + 1) * GROUP_W) for gi in G]
        xbc = [xbc_ref[:, gi * XBC_BLK:(gi + 1) * XBC_BLK] for gi in G]
        dt = [dt_ref[:, colsl[gi]] for gi in G]
        xs = [xbc[gi][:, :GROUP_W] for gi in G]
        b = [xbc[gi][:, GROUP_W:GROUP_W + D_STATE].astype(BF16) for gi in G]
        c = [xbc[gi][:, GROUP_W + D_STATE:].astype(BF16) for gi in G]
        hprev = [h_scr[gi] for gi in G]
        cs = [_dot(tri_ref[...], dt[gi] * a_ref[:, colsl[gi]], NN, HIGHEST) for gi in G]
        cb = [_dot(c[gi], b[gi], NT) for gi in G]
        yoff = [_dot(c[gi], hprev[gi].astype(BF16)) for gi in G]
        cs_t = [_dot(sel_ref[...], cs[gi], NT, HIGHEST) for gi in G]
        xdt = [xs[gi] * dt[gi] for gi in G]
        cs_last = [cs[gi][CHUNK - 1:CHUNK, :] for gi in G]
        st = [_dot(b[gi], (xdt[gi] * jnp.exp(cs_last[gi] - cs[gi])).astype(BF16), TN) for gi in G]
        m = [[(cb[gi] * _decay_matrix(cs[gi], cs_t[gi], r, causal)).astype(BF16) for r in range(HEADS_PER_GROUP)] for gi in G]
        ydiag = [[_dot(m[gi][r], xdt[gi][:, r * HEAD_DIM:(r + 1) * HEAD_DIM].astype(BF16)) for r in range(HEADS_PER_GROUP)]
                 for gi in G]
        for gi in G:
            cols = colsl[gi]
            ytot = jnp.concatenate(ydiag[gi], axis=1) + yoff[gi] * jnp.exp(cs[gi]) + dsk_ref[:, cols] * xs[gi]
            z = z_ref[:, cols]
            gz = ytot * (z * _sigmoid(z))
            rstd = lax.rsqrt(jnp.mean(gz * gz, axis=-1, keepdims=True) + EPS)
            hprev_ref[gi] = hprev[gi]
            h_scr[gi] = hprev[gi] * jnp.exp(cs_last[gi]) + st[gi]
            ytot_ref[:, cols] = ytot
            yn_ref[:, cols] = (gz * rstd * wn_ref[:, cols]).astype(BF16)

    vec = pl.BlockSpec((1, gw), lambda g, c: (0, g))
    blk = pl.BlockSpec((CHUNK, gw), lambda g, c: (c, g))
    return pl.pallas_call(
        body, name="ssd_fwd", grid=(N_GROUPS // SSD_GPS, nc),
        in_specs=[
            pl.BlockSpec((CHUNK, SSD_GPS * XBC_BLK), lambda g, c: (c, g)),
            blk, blk, vec, vec, vec,
            pl.BlockSpec((CHUNK, CHUNK), lambda g, c: (0, 0)),
            pl.BlockSpec((8, GROUP_W), lambda g, c: (0, 0)),
        ],
        out_specs=[blk, blk, pl.BlockSpec((None, SSD_GPS, D_STATE, GROUP_W), lambda g, c: (c, g, 0, 0))],
        out_shape=[
            jax.ShapeDtypeStruct((lp, D_MIX), BF16),
            jax.ShapeDtypeStruct((lp, D_SSD), F32),
            jax.ShapeDtypeStruct((nc, N_GROUPS, D_STATE, GROUP_W), F32),
        ],
        scratch_shapes=[pltpu.VMEM((SSD_GPS, D_STATE, GROUP_W), F32)],
        compiler_params=_params(("parallel", "arbitrary")),
    )(xbc, dt_rep, proj, a_rep, dsk_rep, wn, tri, sel)


def _ssd_bwd(dmix, ytot, proj, xbc, dt_rep, hprev, a_rep, dsk_rep, wn, tri, sel, comm):
    lp = xbc.shape[0]
    nc = lp // CHUNK
    gw = SSD_GPS * GROUP_W
    ncm = comm.n
    n_in, n_out = 11, 6
    grid = (N_GROUPS // SSD_GPS, nc)

    def all_groups(refs, causal):
        (dyn_ref, ytot_ref, z_ref, xbc_ref, dt_ref, hprev_ref, a_ref, dsk_ref, wn_ref, tri_ref, sel_ref,
         dz_ref, dxbc_ref, ddt_ref, dd_ref, da_ref, dwn_ref, dh_scr) = refs
        G = range(SSD_GPS)
        H = range(HEADS_PER_GROUP)
        cl = [slice(gi * GROUP_W, (gi + 1) * GROUP_W) for gi in G]
        hl = [slice(r * HEAD_DIM, (r + 1) * HEAD_DIM) for r in H]
        tri = tri_ref[...]
        sel = sel_ref[...]
        xbc = [xbc_ref[:, gi * XBC_BLK:(gi + 1) * XBC_BLK] for gi in G]
        dt = [dt_ref[:, cl[gi]] for gi in G]
        a = [a_ref[:, cl[gi]] for gi in G]
        xs = [xbc[gi][:, :GROUP_W] for gi in G]
        bbf = [xbc[gi][:, GROUP_W:GROUP_W + D_STATE].astype(BF16) for gi in G]
        cbf = [xbc[gi][:, GROUP_W + D_STATE:].astype(BF16) for gi in G]
        hprev = [hprev_ref[gi] for gi in G]
        hbf = [hprev[gi].astype(BF16) for gi in G]
        dhn = [dh_scr[gi] for gi in G]
        dhnb = [dhn[gi].astype(BF16) for gi in G]
        cs = [_dot(tri, dt[gi] * a[gi], NN, HIGHEST) for gi in G]
        cb = [_dot(cbf[gi], bbf[gi], NT) for gi in G]
        g = [_dot(cbf[gi], hbf[gi]) for gi in G]
        dxw = [_dot(bbf[gi], dhnb[gi]) for gi in G]
        cs_t = [_dot(sel, cs[gi], NT, HIGHEST) for gi in G]
        dy = []
        for gi in G:
            ytot = ytot_ref[:, cl[gi]]
            z = z_ref[:, cl[gi]]
            dyn = dyn_ref[:, cl[gi]]
            sz = _sigmoid(z)
            silu_z = z * sz
            gz = ytot * silu_z
            rstd = lax.rsqrt(jnp.mean(gz * gz, axis=-1, keepdims=True) + EPS)
            xhat = gz * rstd
            dwn_ref[:, cl[gi]] += jnp.sum(dyn * xhat, axis=0, keepdims=True)
            dxhat = dyn * wn_ref[:, cl[gi]]
            dgz = rstd * (dxhat - xhat * jnp.mean(dxhat * xhat, axis=-1, keepdims=True))
            dy.append(dgz * silu_z)
            dz_ref[:, cl[gi]] = (dgz * ytot * (sz * (1.0 + z * (1.0 - sz)))).astype(BF16)
            dd_ref[:, cl[gi]] += jnp.sum(dy[gi] * xs[gi], axis=0, keepdims=True)
        xdt = [xs[gi] * dt[gi] for gi in G]
        e = [jnp.exp(cs[gi]) for gi in G]
        cs_last = [cs[gi][CHUNK - 1:CHUNK, :] for gi in G]
        dte = [jnp.exp(cs_last[gi] - cs[gi]) for gi in G]
        cd = [jnp.exp(cs_last[gi]) for gi in G]
        dgb = [(dy[gi] * e[gi]).astype(BF16) for gi in G]
        dyb = [dy[gi].astype(BF16) for gi in G]
        xdtb = [xdt[gi].astype(BF16) for gi in G]
        dc = [_dot(dgb[gi], hbf[gi], NT) for gi in G]
        dhprev = [_dot(cbf[gi], dgb[gi], TN) for gi in G]
        db = [_dot((xdt[gi] * dte[gi]).astype(BF16), dhnb[gi], NT) for gi in G]
        lm = [[_decay_matrix(cs[gi], cs_t[gi], r, causal) for r in H] for gi in G]
        m = [[cb[gi] * lm[gi][r] for r in H] for gi in G]
        dm = [[_dot(dyb[gi][:, hl[r]], xdtb[gi][:, hl[r]], NT) for r in H] for gi in G]
        dxdt_diag = [[_dot(m[gi][r].astype(BF16), dyb[gi][:, hl[r]], TN) for r in H] for gi in G]
        dseg = [[dm[gi][r] * m[gi][r] for r in H] for gi in G]
        ones = jnp.ones((CHUNK, HEAD_DIM), F32)
        colsum = [[_dot(dseg[gi][r], ones, TN, HIGHEST) for r in H] for gi in G]
        dcbb = []
        for gi in G:
            dcb = dm[gi][0] * lm[gi][0]
            for r in range(1, HEADS_PER_GROUP):
                dcb = dcb + dm[gi][r] * lm[gi][r]
            dcbb.append(dcb.astype(BF16))
        dc2 = [_dot(dcbb[gi], bbf[gi]) for gi in G]
        db2 = [_dot(dcbb[gi], cbf[gi], TN) for gi in G]
        dcs = []
        for gi in G:
            t_dte = dxw[gi] * xdt[gi] * dte[gi]
            dcs_last = (jnp.sum(dhn[gi] * hprev[gi], axis=0, keepdims=True) * cd[gi]
                        + jnp.sum(t_dte, axis=0, keepdims=True))
            diag = jnp.concatenate(
                [(jnp.sum(dseg[gi][r], axis=1, keepdims=True) - colsum[gi][r]) * (1.0 / HEAD_DIM) for r in H], axis=1)
            d = dy[gi] * g[gi] * e[gi] - t_dte + diag
            row = lax.broadcasted_iota(jnp.int32, d.shape, 0)
            dcs.append(d + jnp.where(row == CHUNK - 1, dcs_last, 0.0))
        dda = [_dot(tri, dcs[gi], TN, HIGHEST) for gi in G]
        for gi in G:
            dxdt = dxw[gi] * dte[gi] + jnp.concatenate(dxdt_diag[gi], axis=1)
            da_ref[:, cl[gi]] += jnp.sum(dda[gi] * dt[gi], axis=0, keepdims=True)
            ddt = dda[gi] * a[gi] + dxdt * xs[gi]
            dxs = dsk_ref[:, cl[gi]] * dy[gi] + dxdt * dt[gi]
            ddt_ref[:, cl[gi]] = ddt * (1.0 - jnp.exp(-dt[gi]))
            dxbc_ref[:, gi * XBC_BLK:(gi + 1) * XBC_BLK] = jnp.concatenate(
                [dxs, db[gi] + db2[gi], dc[gi] + dc2[gi]], axis=1)
            dh_scr[gi] = dhprev[gi] + dhn[gi] * cd[gi]

    def body(*refs):
        ins = refs[:n_in]
        cin = refs[n_in:n_in + ncm]
        outs = refs[n_in + ncm:n_in + ncm + n_out]
        cout = refs[n_in + ncm + n_out:n_in + 2 * ncm + n_out]
        dh_scr = refs[n_in + 2 * ncm + n_out]
        sems = refs[n_in + 2 * ncm + n_out + 1:]
        g, c = pl.program_id(0), pl.program_id(1)

        @pl.when((g == 0) & (c == 0))
        def _():
            comm.start(cin, cout, sems)

        @pl.when(c == 0)
        def _():
            dh_scr[...] = jnp.zeros_like(dh_scr)
            for ref in outs[3:]:
                ref[...] = jnp.zeros_like(ref)

        all_groups(ins + outs + (dh_scr,), _causal_mask())

        @pl.when((g == grid[0] - 1) & (c == nc - 1))
        def _():
            comm.wait(cin, cout, sems)

    rev = lambda c: nc - 1 - c
    vec = pl.BlockSpec((1, gw), lambda g, c: (0, g))
    blk = pl.BlockSpec((CHUNK, gw), lambda g, c: (rev(c), g))
    xblk = pl.BlockSpec((CHUNK, SSD_GPS * XBC_BLK), lambda g, c: (rev(c), g))
    res = pl.pallas_call(
        body, name="ssd_bwd", grid=grid,
        in_specs=[blk, blk, blk, xblk, blk,
                  pl.BlockSpec((None, SSD_GPS, D_STATE, GROUP_W), lambda g, c: (rev(c), g, 0, 0)),
                  vec, vec, vec,
                  pl.BlockSpec((CHUNK, CHUNK), lambda g, c: (0, 0)),
                  pl.BlockSpec((8, GROUP_W), lambda g, c: (0, 0))] + comm.in_specs,
        out_specs=[blk, xblk, blk, vec, vec, vec] + comm.out_specs,
        out_shape=[
            jax.ShapeDtypeStruct((lp, NP), BF16),
            jax.ShapeDtypeStruct((lp, D_CONV), F32),
            jax.ShapeDtypeStruct((lp, D_SSD), F32),
            jax.ShapeDtypeStruct((1, D_SSD), F32),
            jax.ShapeDtypeStruct((1, D_SSD), F32),
            jax.ShapeDtypeStruct((1, D_SSD), F32),
        ] + comm.out_shape,
        scratch_shapes=[pltpu.VMEM((SSD_GPS, D_STATE, GROUP_W), F32)] + comm.scratch,
        compiler_params=_params(("arbitrary", "arbitrary"), side_effects=True),
    )(dmix, ytot, proj, xbc, dt_rep, hprev, a_rep, dsk_rep, wn, tri, sel, *comm.operands)
    return res


def _stack_heads(t, h):
    return jnp.concatenate([t[:, (REP * h + r) * HEAD_DIM:(REP * h + r + 1) * HEAD_DIM] for r in range(REP)], axis=0)


def _band(t2, t1, t0, h):
    sl = slice(h * HEAD_DIM, (h + 1) * HEAD_DIM)
    return jnp.concatenate([t2[:, sl], t1[:, sl], t0[:, sl]], axis=0)


def _attn_probs(s, sink, qc):
    s = s * (HEAD_DIM ** -0.5)
    key_abs = (qc - WINDOW_CHUNKS) * CHUNK + lax.broadcasted_iota(jnp.int32, s.shape, 1)
    s = jnp.where(key_abs >= PAD_LEAD, s, -jnp.inf)
    m = jnp.maximum(jnp.max(s, axis=-1, keepdims=True), sink)
    p = jnp.exp(s - m)
    ps = jnp.exp(sink - m)
    denom = jnp.sum(p, axis=-1, keepdims=True) + ps
    return p / denom, ps / denom


def _kv_specs(width, chunk_of):
    return [pl.BlockSpec((CHUNK, width), functools.partial(lambda j, c: (jnp.maximum(chunk_of(c) - j, 0), 0), j))
            for j in (2, 1, 0)]


def _attn_fwd(qr, kr, vb, proj, sink_stack, mix):
    lp = qr.shape[0]
    nc = lp // CHUNK

    def body(q_ref, k2_ref, k1_ref, k0_ref, v2_ref, v1_ref, v0_ref, g_ref, sink_ref, _, att_ref, mix_ref):
        qc = pl.program_id(0)
        q = q_ref[...]
        ks = (k2_ref[...], k1_ref[...], k0_ref[...])
        vs = (v2_ref[...], v1_ref[...], v0_ref[...])
        heads = range(KV_HEADS)
        s = [_dot(_stack_heads(q, h), _band(*ks, h), NT) for h in heads]
        vbh = [_band(*vs, h) for h in heads]
        pn = [_attn_probs(s[h], sink_ref[h], qc)[0].astype(BF16) for h in heads]
        o = [_dot(pn[h], vbh[h]) for h in heads]
        att = jnp.concatenate([o[h][r * CHUNK:(r + 1) * CHUNK, :] for h in heads for r in range(REP)], axis=1)
        att_ref[...] = att
        g = g_ref[...]
        mix_ref[...] = (att * (g * _sigmoid(g))).astype(BF16)

    here = lambda c: c
    return pl.pallas_call(
        body, name="attn_fwd", grid=(nc,),
        in_specs=[pl.BlockSpec((CHUNK, D_ATT), lambda c: (c, 0))] + _kv_specs(D_KV, here) + _kv_specs(D_KV, here) + [
            pl.BlockSpec((CHUNK, D_ATT), lambda c: (c, OFF_G // D_ATT)),
            pl.BlockSpec((KV_HEADS, REP * CHUNK, 1), lambda c: (0, 0, 0)),
            ANY,
        ],
        out_specs=[pl.BlockSpec((CHUNK, D_ATT), lambda c: (c, 0)),
                   pl.BlockSpec((CHUNK, D_ATT), lambda c: (c, D_SSD // D_ATT))],
        out_shape=[jax.ShapeDtypeStruct((lp, D_ATT), F32), jax.ShapeDtypeStruct((lp, D_MIX), BF16)],
        input_output_aliases={9: 1},
        compiler_params=_params(("parallel",)),
    )(qr, kr, kr, kr, vb, vb, vb, proj, sink_stack, mix)


def _attn_bwd(qr, kr, vb, att, proj, dmix, sink_stack):
    lp = qr.shape[0]
    nc = lp // CHUNK

    def body(q_ref, k2_ref, k1_ref, k0_ref, v2_ref, v1_ref, v0_ref, att_ref, g_ref, do_ref, sink_ref,
             dq_ref, dk_ref, dv_ref, dg_ref, dsink_ref, dk_acc, dv_acc):
        step = pl.program_id(0)

        @pl.when(step == 0)
        def _():
            dk_acc[...] = jnp.zeros_like(dk_acc)
            dv_acc[...] = jnp.zeros_like(dv_acc)
            dsink_ref[...] = jnp.zeros_like(dsink_ref)

        @pl.when(step < nc)
        def _():
            q = q_ref[...]
            ks = (k2_ref[...], k1_ref[...], k0_ref[...])
            vs = (v2_ref[...], v1_ref[...], v0_ref[...])
            att = att_ref[...]
            g = g_ref[...]
            dog = do_ref[...]
            sg = _sigmoid(g)
            dg_ref[...] = dog * att * (sg * (1.0 + g * (1.0 - sg)))
            do = dog * (g * sg)
            heads = range(KV_HEADS)
            qs = [_stack_heads(q, h) for h in heads]
            kb = [_band(*ks, h) for h in heads]
            vbh = [_band(*vs, h) for h in heads]
            dos = [_stack_heads(do, h) for h in heads]
            dosb = [dos[h].astype(BF16) for h in heads]
            s = [_dot(qs[h], kb[h], NT) for h in heads]
            dp = [_dot(dosb[h], vbh[h], NT) for h in heads]
            ds, pnb = [], []
            for h in heads:
                pn, psink = _attn_probs(s[h], sink_ref[h], step)
                delta = jnp.sum(dos[h] * _stack_heads(att, h), axis=-1, keepdims=True)
                ds.append((pn * (dp[h] - delta)).astype(BF16))
                pnb.append(pn.astype(BF16))
                dsink_ref[h] += -(psink * delta)
            dqs = [_dot(ds[h], kb[h]) for h in heads]
            dks = [_dot(ds[h], qs[h], TN) for h in heads]
            dvs = [_dot(pnb[h], dosb[h], TN) for h in heads]
            dq_ref[...] = jnp.concatenate(
                [dqs[h][r * CHUNK:(r + 1) * CHUNK, :] for h in heads for r in range(REP)], axis=1) * (HEAD_DIM ** -0.5)
            dk_acc[...] += jnp.concatenate(dks, axis=1) * (HEAD_DIM ** -0.5)
            dv_acc[...] += jnp.concatenate(dvs, axis=1)

        dk_ref[...] = dk_acc[0:CHUNK, :]
        dv_ref[...] = dv_acc[0:CHUNK, :]
        for acc in (dk_acc, dv_acc):
            rest = acc[CHUNK:BAND, :]
            acc[0:BAND - CHUNK, :] = rest
            acc[BAND - CHUNK:BAND, :] = jnp.zeros((CHUNK, D_KV), F32)

    qc = lambda c: jnp.minimum(c, nc - 1)
    qblk = pl.BlockSpec((CHUNK, D_ATT), lambda c: (qc(c), 0))
    oldest = pl.BlockSpec((CHUNK, D_KV), lambda c: (jnp.maximum(c - WINDOW_CHUNKS, 0), 0))
    return pl.pallas_call(
        body, name="attn_bwd", grid=(nc + WINDOW_CHUNKS,),
        in_specs=[qblk] + _kv_specs(D_KV, qc) + _kv_specs(D_KV, qc) + [
            qblk,
            pl.BlockSpec((CHUNK, D_ATT), lambda c: (qc(c), OFF_G // D_ATT)),
            pl.BlockSpec((CHUNK, D_ATT), lambda c: (qc(c), D_SSD // D_ATT)),
            pl.BlockSpec((KV_HEADS, REP * CHUNK, 1), lambda c: (0, 0, 0)),
        ],
        out_specs=[qblk, oldest, oldest, qblk, pl.BlockSpec((KV_HEADS, REP * CHUNK, 1), lambda c: (0, 0, 0))],
        out_shape=[
            jax.ShapeDtypeStruct((lp, D_ATT), F32),
            jax.ShapeDtypeStruct((lp, D_KV), F32),
            jax.ShapeDtypeStruct((lp, D_KV), F32),
            jax.ShapeDtypeStruct((lp, D_ATT), F32),
            jax.ShapeDtypeStruct((KV_HEADS, REP * CHUNK, 1), F32),
        ],
        scratch_shapes=[pltpu.VMEM((BAND, D_KV), F32), pltpu.VMEM((BAND, D_KV), F32)],
        compiler_params=_params(("arbitrary",)),
    )(qr, kr, kr, kr, vb, vb, vb, att, proj, dmix, sink_stack)


def _post_loss(out, x2d, target, w):
    lp = out.shape[0]
    nc = lp // CHUNK
    nx = x2d.shape[0] // CHUNK

    def body(o_ref, x_ref, t_ref, w_ref, do_ref, dy_ref, gw_ref, loss_ref):
        i = pl.program_id(0)

        @pl.when(i == 0)
        def _():
            gw_ref[...] = jnp.zeros_like(gw_ref)
            loss_ref[...] = jnp.zeros_like(loss_ref)

        o = o_ref[...]
        w = w_ref[...]
        rstd = lax.rsqrt(jnp.mean(o * o, axis=-1, keepdims=True) + EPS)
        xhat = o * rstd
        err = (x_ref[...] + xhat * w - t_ref[...]) * ((i > 0) & (i <= nx)).astype(F32)
        loss_ref[...] += 0.5 * jnp.sum(jnp.mean(err * err, axis=-1, keepdims=True), axis=0, keepdims=True)
        dy = err * (1.0 / D_MODEL)
        dy_ref[...] = dy
        gw_ref[...] += jnp.sum(dy * xhat, axis=0, keepdims=True)
        dxhat = dy * w
        do_ref[...] = (rstd * (dxhat - xhat * jnp.mean(dxhat * xhat, axis=-1, keepdims=True))).astype(BF16)

    row = pl.BlockSpec((CHUNK, D_MODEL), lambda i: (i, 0))
    return pl.pallas_call(
        body, name="post_loss", grid=(nc,),
        in_specs=[row, _x_spec(nx), _x_spec(nx), pl.BlockSpec((1, D_MODEL), lambda i: (0, 0))],
        out_specs=[row, row, pl.BlockSpec((1, D_MODEL), lambda i: (0, 0)), pl.BlockSpec((1, 128), lambda i: (0, 0))],
        out_shape=[
            jax.ShapeDtypeStruct((lp, D_MODEL), BF16),
            jax.ShapeDtypeStruct((lp, D_MODEL), F32),
            jax.ShapeDtypeStruct((1, D_MODEL), F32),
            jax.ShapeDtypeStruct((1, 128), F32),
        ],
        compiler_params=_params(("arbitrary",)),
    )(out, x2d, target, w)


def _prenorm_bwd(dhn, x2d, h0, dy, w):
    lp = dhn.shape[0]
    nc = lp // CHUNK
    nx = x2d.shape[0] // CHUNK

    def body(dhn_ref, x_ref, h0_ref, dy_ref, w_ref, gx_ref, d0_ref, gw_ref):
        i = pl.program_id(0)

        @pl.when(i == 0)
        def _():
            gw_ref[...] = jnp.zeros_like(gw_ref)

        h = _h_chunk(i, nx, h0_ref, x_ref)
        dhn = dhn_ref[...]
        rstd = lax.rsqrt(jnp.mean(h * h, axis=-1, keepdims=True) + EPS)
        xhat = h * rstd
        gw_ref[...] += jnp.sum(dhn * xhat, axis=0, keepdims=True)
        dxhat = dhn * w_ref[...]
        dh = rstd * (dxhat - xhat * jnp.mean(dxhat * xhat, axis=-1, keepdims=True)) + dy_ref[...]

        @pl.when(i == 0)
        def _():
            d0_ref[...] = dh

        @pl.when((i > 0) & (i <= nx))
        def _():
            gx_ref[...] = dh

    row = pl.BlockSpec((CHUNK, D_MODEL), lambda i: (i, 0))
    return pl.pallas_call(
        body, name="prenorm_bwd", grid=(nc,),
        in_specs=[row, _x_spec(nx), pl.BlockSpec((CHUNK, D_MODEL), lambda i: (0, 0)), row,
                  pl.BlockSpec((1, D_MODEL), lambda i: (0, 0))],
        out_specs=[_x_spec(nx), pl.BlockSpec((CHUNK, D_MODEL), lambda i: (0, 0)),
                   pl.BlockSpec((1, D_MODEL), lambda i: (0, 0))],
        out_shape=[
            jax.ShapeDtypeStruct((nx * CHUNK, D_MODEL), F32),
            jax.ShapeDtypeStruct((CHUNK, D_MODEL), F32),
            jax.ShapeDtypeStruct((1, D_MODEL), F32),
        ],
        compiler_params=_params(("arbitrary",)),
    )(dhn, x2d, h0, dy, w)


def _adamw(slabs, w, m, v, name):
    lead = w.ndim == 3
    rows, cols = w.shape[-2:]
    tr = _tile(rows, (256, 128, 64, 16, 8))
    c1 = 1.0 - ADAM_B1 ** ADAM_STEP
    c2 = 1.0 - ADAM_B2 ** ADAM_STEP

    def body(s_ref, w_ref, m_ref, v_ref, g_ref, d_ref, mo_ref, vo_ref):
        g = s_ref[0].astype(F32)
        for k in range(1, N_DEV):
            g = g + s_ref[k].astype(F32)
        w = w_ref[...]
        m = ADAM_B1 * m_ref[...] + (1.0 - ADAM_B1) * g
        v = ADAM_B2 * v_ref[...] + (1.0 - ADAM_B2) * (g * g)
        g_ref[...] = g
        mo_ref[...] = m
        vo_ref[...] = v
        d_ref[...] = -ADAM_LR * ((m / c1) / (jnp.sqrt(v / c2) + ADAM_EPS) + ADAM_WD * w)

    blk = (pl.BlockSpec((None, tr, cols), lambda i: (0, i, 0)) if lead else pl.BlockSpec((tr, cols), lambda i: (i, 0)))
    shape = jax.ShapeDtypeStruct(w.shape, F32)
    return pl.pallas_call(
        body, name=name, grid=(rows // tr,),
        in_specs=[pl.BlockSpec((N_DEV, tr, cols), lambda i: (0, i, 0)), blk, blk, blk],
        out_specs=[blk, blk, blk, blk],
        out_shape=[shape, shape, shape, shape],
        compiler_params=_params(("parallel",)),
    )(slabs, w, m, v)


def _perm_xbc(a):
    lead = a.shape[:-1]
    xs = a[..., :D_SSD].reshape(lead + (N_GROUPS, GROUP_W))
    b = a[..., D_SSD:D_SSD + N_GROUPS * D_STATE].reshape(lead + (N_GROUPS, D_STATE))
    c = a[..., D_SSD + N_GROUPS * D_STATE:].reshape(lead + (N_GROUPS, D_STATE))
    return jnp.concatenate([xs, b, c], axis=-1).reshape(lead + (D_CONV,))


def _unperm_xbc(a):
    lead = a.shape[:-1]
    t = a.reshape(lead + (N_GROUPS, XBC_BLK))
    xs = t[..., :GROUP_W].reshape(lead + (D_SSD,))
    b = t[..., GROUP_W:GROUP_W + D_STATE].reshape(lead + (N_GROUPS * D_STATE,))
    c = t[..., GROUP_W + D_STATE:].reshape(lead + (N_GROUPS * D_STATE,))
    return jnp.concatenate([xs, b, c], axis=-1)


R_Z, R_XBC, R_DT, R_Q, R_K, R_V, R_G = 0, 2048, 6144, 6176, 7200, 7456, 7712


def _w_in_internal(w_gathered):
    w = jnp.transpose(w_gathered, (1, 0, 2)).reshape(D_MODEL, D_IN_PROJ)
    pad = jnp.zeros((D_MODEL, NP - OFF_DT - SSD_HEADS), w.dtype)
    return jnp.concatenate([
        w[:, R_Z:R_XBC], _perm_xbc(w[:, R_XBC:R_DT]), w[:, R_Q:R_K], w[:, R_G:], w[:, R_K:R_V], w[:, R_V:R_G],
        w[:, R_DT:R_Q], pad], axis=1)


def _w_in_slabs(dw):
    w = jnp.concatenate([
        dw[:, OFF_Z:OFF_XBC], _unperm_xbc(dw[:, OFF_XBC:OFF_Q]), dw[:, OFF_DT:OFF_DT + SSD_HEADS],
        dw[:, OFF_Q:OFF_G], dw[:, OFF_K:OFF_V], dw[:, OFF_V:OFF_DT], dw[:, OFF_G:OFF_K]], axis=1)
    return jnp.transpose(w.reshape(D_MODEL, N_DEV, SHARD_IN), (1, 0, 2))


def _rep_heads(a):
    return jnp.repeat(a, HEAD_DIM, axis=1)


SMALL = (("norm_pre_w", 2048), ("conv_b", 4096), ("dt_bias", 32), ("a_log", 32), ("d_skip", 32),
         ("ssd_norm_w", 2048), ("attn_sinks", 16), ("norm_post_w", 2048))
SMALL_USED = sum(size for _, size in SMALL)
SMALL_LEN = 10368


def _pack_small(d, loss=None):
    parts = [d[name].reshape(1, size) for name, size in SMALL]
    tail = jnp.zeros((1, SMALL_LEN - SMALL_USED), F32)
    if loss is not None:
        tail = tail.at[0, 0].set(loss)
    return jnp.concatenate(parts + [tail], axis=1)


def _unpack_small(vec):
    out, off = {}, 0
    for name, size in SMALL:
        out[name] = vec[:, off:off + size]
        off += size
    return out


def kernel(x, meta_tokens, norm_pre_w, w_in, conv_w, conv_b, dt_bias, a_log, d_skip, ssd_norm_w, attn_sinks, w_out, norm_post_w, loss_target, m_meta_tokens, m_norm_pre_w, m_w_in, m_conv_w, m_conv_b, m_dt_bias, m_a_log, m_d_skip, m_ssd_norm_w, m_attn_sinks, m_w_out, m_norm_post_w, v_meta_tokens, v_norm_pre_w, v_w_in, v_conv_w, v_conv_b, v_dt_bias, v_a_log, v_d_skip, v_ssd_norm_w, v_attn_sinks, v_w_out, v_norm_post_w):
    seq = x.shape[1]
    lp = seq + 2 * CHUNK
    x2d = x[0]

    w_in_g, conv_w_g, meta_g = _gather_two_level([w_in[0].astype(BF16), conv_w[0], meta_tokens], "gather_w_in")
    w_all = _w_in_internal(w_in_g)
    conv_w_full = _perm_xbc(jnp.transpose(conv_w_g, (1, 0, 2)).reshape(CONV_WIDTH, D_CONV))
    conv_b_int = _perm_xbc(conv_b)
    meta_full = jnp.transpose(meta_g, (1, 0, 2)).reshape(N_META, D_MODEL)
    h0 = jnp.concatenate([jnp.zeros((PAD_LEAD, D_MODEL), F32), meta_full], axis=0)

    pos = (jnp.arange(lp) - PAD_LEAD).astype(F32)
    half = HEAD_DIM // 2
    inv = ROPE_THETA ** (-jnp.arange(half, dtype=F32) / half)
    ang = pos[:, None] * inv[None, :]
    cos_t = jnp.tile(jnp.cos(ang), (1, 4))
    sin_t = jnp.tile(jnp.concatenate([-jnp.sin(ang), jnp.sin(ang)], axis=1), (1, 2))
    head_of_col = np.arange(D_SSD) // HEAD_DIM
    expand = jnp.asarray((np.arange(128)[:, None] == head_of_col[None, :]).astype(np.float32))
    reduce_t = jnp.asarray((head_of_col[:, None] == np.arange(128)[None, :]).astype(np.float32))
    tri = jnp.asarray(np.tril(np.ones((CHUNK, CHUNK), np.float32)))
    sel = jnp.asarray((np.arange(8)[:, None] * HEAD_DIM == np.arange(GROUP_W)[None, :]).astype(np.float32))
    a_rep = _rep_heads(-jnp.exp(a_log))
    dsk_rep = _rep_heads(d_skip)
    dt_bias_rep = _rep_heads(dt_bias)
    sink_stack = jnp.repeat(attn_sinks.reshape(KV_HEADS, REP), CHUNK, axis=1).reshape(KV_HEADS, REP * CHUNK, 1)

    hn = _prenorm(x2d, h0, norm_pre_w)
    tm = _tile(lp, (1056, 704, 128, 64))
    proj, w_out_g = _matmul(hn, w_all, tm=tm, tn=1536, tk=D_MODEL, out_dtype=F32, name="in_proj",
                            comm=_Comm([(w_out[0].astype(BF16), "gather")]))
    w_out_full = w_out_g.reshape(D_MIX, D_MODEL)
    xbc = _conv_fwd(proj, conv_w_full, conv_b_int)
    qr, kr, vb, dt_rep = _act_fwd(proj, cos_t, sin_t, expand, dt_bias_rep)
    mix, ytot, hprev = _ssd_fwd(xbc, dt_rep, proj, a_rep, dsk_rep, ssd_norm_w, tri, sel)
    att, mix = _attn_fwd(qr, kr, vb, proj, sink_stack, mix)
    out = _matmul(mix, w_out_full, tm=tm, tn=1024, tk=D_MIX, out_dtype=F32, name="out_proj")
    dout, dy, g_norm_post, loss_part = _post_loss(out, x2d, loss_target[0], norm_post_w)

    dmix = _matmul(dout, w_out_full, trans_b=True, tm=tm, tn=1024, tk=D_MODEL, out_dtype=F32, name="dmix")
    dw_out = _matmul(mix, dout, trans_a=True, tm=512, tn=1024, tk=lp, out_dtype=BF16, name="dw_out")
    dqr, dkr, dv, dg, dsink_rows = _attn_bwd(qr, kr, vb, att, proj, dmix, sink_stack)
    dproj, dxbc, ddt_part, dd_part, da_part, g_ssd_norm, g_out = _ssd_bwd(
        dmix, ytot, proj, xbc, dt_rep, hprev, a_rep, dsk_rep, ssd_norm_w, tri, sel,
        _Comm([(dw_out.reshape(N_DEV, D_MIX // N_DEV, D_MODEL), "scatter")]))
    dproj, dconv_w_int, dconv_b_int = _conv_bwd(dxbc, proj, conv_w_full, conv_b_int, dproj)
    dproj, ddt_bias = _act_bwd(dqr, dkr, dv, dg, ddt_part, cos_t, sin_t, reduce_t, dproj)
    dw_all = _matmul(hn, dproj, trans_a=True, tm=512, tn=1024, tk=lp, out_dtype=BF16, name="dw_in")
    dhn, g_in = _matmul(dproj, w_all, trans_b=True, tm=tm, tn=1024, tk=1536, out_dtype=F32, name="dhn",
                        comm=_Comm([(_w_in_slabs(dw_all), "scatter")]))
    grad_x, dh0, g_norm_pre = _prenorm_bwd(dhn, x2d, h0, dy, norm_pre_w)

    dmeta = dh0[PAD_LEAD:, :]
    dconv_w_ref = _unperm_xbc(dconv_w_int)
    heads = lambda part: part.reshape(SSD_HEADS, HEAD_DIM).sum(axis=1).reshape(1, SSD_HEADS)
    small_local = _pack_small({
        "norm_pre_w": g_norm_pre, "conv_b": _unperm_xbc(dconv_b_int), "dt_bias": ddt_bias[:, :SSD_HEADS],
        "a_log": heads(da_part) * (-jnp.exp(a_log)), "d_skip": heads(dd_part), "ssd_norm_w": g_ssd_norm,
        "attn_sinks": dsink_rows.reshape(Q_HEADS, CHUNK).sum(axis=1).reshape(1, Q_HEADS),
        "norm_post_w": g_norm_post}, loss=loss_part[0, 0])
    g_conv, g_meta, g_small = _exchange(
        [(jnp.transpose(dconv_w_ref.reshape(CONV_WIDTH, N_DEV, D_CONV // N_DEV), (1, 0, 2)), "scatter"),
         (jnp.transpose(dmeta.reshape(N_META, N_DEV, D_MODEL // N_DEV), (1, 0, 2)), "scatter"),
         (small_local, "gather")], "exchange_small")

    res = {}
    res["w_in"] = _adamw(g_in, w_in, m_w_in, v_w_in, "adamw_w_in")
    res["w_out"] = _adamw(g_out, w_out, m_w_out, v_w_out, "adamw_w_out")
    res["conv_w"] = _adamw(g_conv, conv_w, m_conv_w, v_conv_w, "adamw_conv_w")
    res["meta_tokens"] = _adamw(g_meta, meta_tokens, m_meta_tokens, v_meta_tokens, "adamw_meta")
    given = dict(norm_pre_w=(norm_pre_w, m_norm_pre_w, v_norm_pre_w), conv_b=(conv_b, m_conv_b, v_conv_b),
                 dt_bias=(dt_bias, m_dt_bias, v_dt_bias), a_log=(a_log, m_a_log, v_a_log),
                 d_skip=(d_skip, m_d_skip, v_d_skip), ssd_norm_w=(ssd_norm_w, m_ssd_norm_w, v_ssd_norm_w),
                 attn_sinks=(attn_sinks, m_attn_sinks, v_attn_sinks),
                 norm_post_w=(norm_post_w, m_norm_post_w, v_norm_post_w))
    packed = [_pack_small({k: t[j] for k, t in given.items()}) for j in range(3)]
    small_out = _adamw(g_small, packed[0], packed[1], packed[2], "adamw_small")
    small_res = [_unpack_small(r) for r in small_out]
    loss = small_out[0][0, SMALL_USED]

    order = ["meta_tokens", "norm_pre_w", "w_in", "conv_w", "conv_b", "dt_bias", "a_log", "d_skip", "ssd_norm_w",
             "attn_sinks", "w_out", "norm_post_w"]
    outs = []
    for j in range(4):
        for name in order:
            outs.append(res[name][j] if name in res else small_res[j][name])
    return (loss, grad_x[None], *outs)
```

```python
import functools

import numpy as np
import jax
import jax.numpy as jnp
from jax import lax
from jax.experimental import pallas as pl
from jax.experimental.pallas import tpu as pltpu

F32 = jnp.float32
BF16 = jnp.bfloat16
HIGHEST = lax.Precision.HIGHEST

N_DEV = 8
D_MODEL = 2048
CHUNK = 64
N_META = 16
PAD_LEAD = CHUNK - N_META
EPS = 1e-6
N_GROUPS = 8
HEADS_PER_GROUP = 4
HEAD_DIM = 64
GROUP_W = HEADS_PER_GROUP * HEAD_DIM
D_STATE = 128
D_SSD = 2048
D_CONV = 4096
SSD_HEADS = 32
CONV_WIDTH = 4
Q_HEADS = 16
KV_HEADS = 4
REP = 4
D_ATT = 1024
D_KV = 256
WINDOW_CHUNKS = 2
BAND = (WINDOW_CHUNKS + 1) * CHUNK
ROPE_THETA = 10000.0
D_MIX = D_SSD + D_ATT
D_IN_PROJ = 8736
SHARD_IN = D_IN_PROJ // N_DEV

OFF_Z, OFF_XBC, OFF_Q, OFF_G, OFF_K, OFF_V, OFF_DT = 0, 2048, 6144, 7168, 8192, 8448, 8704
NP = 9216
TAIL_W = NP - OFF_Q
XBC_BLK = 512
SSD_GPS = 8

ADAM_LR, ADAM_B1, ADAM_B2, ADAM_EPS, ADAM_WD, ADAM_STEP = 0.001, 0.9, 0.999, 1e-08, 0.01, 10

VMEM_LIMIT = 48 * 1024 * 1024

NN = (((1,), (0,)), ((), ()))
NT = (((1,), (1,)), ((), ()))
TN = (((0,), (0,)), ((), ()))
ANY = pl.BlockSpec(memory_space=pl.ANY)


def _dot(a, b, dims=NN, precision=None):
    return lax.dot_general(a, b, dims, precision=precision, preferred_element_type=F32)


def _tile(n, prefs):
    for t in prefs:
        if n % t == 0:
            return t
    return n


def _params(sem, vmem=VMEM_LIMIT, side_effects=False):
    return pltpu.CompilerParams(dimension_semantics=sem, vmem_limit_bytes=vmem, has_side_effects=side_effects)


def _sigmoid(x):
    return 1.0 / (1.0 + jnp.exp(-x))


class _Comm:
    def __init__(self, items):
        self.items = items
        self.n = n = len(items)
        self.operands = [arr for arr, _ in items]
        self.in_specs = [ANY] * n
        self.out_specs = [ANY] * n
        self.out_shape = [jax.ShapeDtypeStruct((N_DEV,) + tuple(arr.shape) if kind == "gather" else tuple(arr.shape),
                                               arr.dtype) for arr, kind in items]
        self.scratch = [pltpu.SemaphoreType.DMA((n, N_DEV - 1)), pltpu.SemaphoreType.DMA((n, N_DEV - 1)),
                        pltpu.SemaphoreType.DMA((n,))]

    @staticmethod
    def _places():
        pos = (lax.axis_index("x"), lax.axis_index("y"), lax.axis_index("c"))
        me = 4 * pos[0] + 2 * pos[1] + pos[2]
        peers = []
        for k in range(1, N_DEV):
            p = tuple(1 - pos[b] if (k >> (2 - b)) & 1 else pos[b] for b in range(3))
            peers.append((p, 4 * p[0] + 2 * p[1] + p[2]))
        return me, peers

    def _copies(self, ins, outs, sems, landed):
        send_sems, recv_sems, local_sems = sems
        me, peers = self._places()
        local, remote = [], []
        for a, (_, kind) in enumerate(self.items):
            own = ins[a] if kind == "gather" else ins[a].at[me]
            local.append(pltpu.make_async_copy(own, outs[a].at[me], local_sems.at[a]))
            for k, (p, pid) in enumerate(peers):
                remote.append(pltpu.make_async_remote_copy(
                    src_ref=ins[a] if kind == "gather" else ins[a].at[pid],
                    dst_ref=outs[a].at[pid if landed else me],
                    send_sem=send_sems.at[a, k], recv_sem=recv_sems.at[a, k],
                    device_id=p, device_id_type=pl.DeviceIdType.MESH))
        return local, remote

    def start(self, ins, outs, sems):
        local, remote = self._copies(ins, outs, sems, landed=False)
        for cp in local + remote:
            cp.start()

    def wait(self, ins, outs, sems):
        local, remote = self._copies(ins, outs, sems, landed=True)
        for cp in remote + local:
            cp.wait()


def _exchange(items, name):
    comm = _Comm(items)
    n = comm.n

    def body(*refs):
        ins, outs, sems = refs[:n], refs[n:2 * n], refs[2 * n:]
        comm.start(ins, outs, sems)
        comm.wait(ins, outs, sems)

    return pl.pallas_call(
        body, name=name, in_specs=comm.in_specs, out_specs=comm.out_specs, out_shape=comm.out_shape,
        scratch_shapes=comm.scratch, compiler_params=pltpu.CompilerParams(has_side_effects=True),
    )(*comm.operands)


def _gather_two_level(arrays, name):
    n = len(arrays)

    def body(*refs):
        ins, outs = refs[:n], refs[n:2 * n]
        send_sems, recv_sems, local_sems = refs[2 * n:]
        x, y, c = lax.axis_index("x"), lax.axis_index("y"), lax.axis_index("c")
        me, sibling = (x, y, c), (x, y, 1 - c)
        chips = [(1 - x, y), (x, 1 - y), (1 - x, 1 - y)]

        def slab(a, place):
            return outs[a].at[4 * place[0] + 2 * place[1] + place[2]]

        def copy(a, k, block, to, src=None):
            return pltpu.make_async_remote_copy(
                src_ref=slab(a, block) if src is None else src, dst_ref=slab(a, block),
                send_sem=send_sems.at[a, k], recv_sem=recv_sems.at[a, k],
                device_id=to, device_id_type=pl.DeviceIdType.MESH)

        sends, mine = [], []
        for a in range(n):
            loc = pltpu.make_async_copy(ins[a], slab(a, me), local_sems.at[a])
            loc.start()
            mine.append(loc)
            sends.append(copy(a, 0, me, sibling, src=ins[a]))
            sends += [copy(a, 1 + j, me, (*chip, c), src=ins[a]) for j, chip in enumerate(chips)]
        for cp in sends:
            cp.start()
        for j, chip in enumerate(chips):
            for a in range(n):
                copy(a, 1 + j, (*chip, c), me).wait_recv()
                fwd = copy(a, 4 + j, (*chip, c), sibling)
                fwd.start()
                sends.append(fwd)
        for a in range(n):
            copy(a, 0, sibling, me).wait_recv()
            for j, chip in enumerate(chips):
                copy(a, 4 + j, (*chip, 1 - c), me).wait_recv()
        for cp in sends:
            cp.wait_send()
        for loc in mine:
            loc.wait()

    return pl.pallas_call(
        body, name=name, in_specs=[ANY] * n, out_specs=[ANY] * n,
        out_shape=[jax.ShapeDtypeStruct((N_DEV,) + tuple(a.shape), a.dtype) for a in arrays],
        scratch_shapes=[pltpu.SemaphoreType.DMA((n, N_DEV - 1)), pltpu.SemaphoreType.DMA((n, N_DEV - 1)),
                        pltpu.SemaphoreType.DMA((n,))],
        compiler_params=pltpu.CompilerParams(has_side_effects=True),
    )(*arrays)


def _matmul(a, b, *, tm, tn, tk, out_dtype, name, trans_a=False, trans_b=False, comm=None):
    m, k = (a.shape[1], a.shape[0]) if trans_a else a.shape
    n = b.shape[0] if trans_b else b.shape[1]
    nk = k // tk
    dims = TN if trans_a else (NT if trans_b else NN)
    assert not (trans_a and trans_b)
    nc = comm.n if comm else 0
    grid = (m // tm, n // tn, nk)

    def body(*refs):
        a_ref, b_ref = refs[:2]
        cin = refs[2:2 + nc]
        o_ref = refs[2 + nc]
        cout = refs[3 + nc:3 + 2 * nc]
        scratch = refs[3 + 2 * nc:]
        sems = scratch[len(scratch) - 3:] if comm else None
        i, j, kk = pl.program_id(0), pl.program_id(1), pl.program_id(2)
        if comm:
            @pl.when((i == 0) & (j == 0) & (kk == 0))
            def _():
                comm.start(cin, cout, sems)

        if nk == 1:
            o_ref[...] = _dot(a_ref[...], b_ref[...], dims).astype(out_dtype)
        else:
            acc_ref = scratch[0]

            @pl.when(kk == 0)
            def _():
                acc_ref[...] = jnp.zeros_like(acc_ref)

            acc_ref[...] += _dot(a_ref[...], b_ref[...], dims)

            @pl.when(kk == nk - 1)
            def _():
                o_ref[...] = acc_ref[...].astype(out_dtype)

        if comm:
            @pl.when((i == grid[0] - 1) & (j == grid[1] - 1) & (kk == nk - 1))
            def _():
                comm.wait(cin, cout, sems)

    a_spec = (pl.BlockSpec((tk, tm), lambda i, j, kk: (kk, i)) if trans_a
              else pl.BlockSpec((tm, tk), lambda i, j, kk: (i, kk)))
    b_spec = (pl.BlockSpec((tn, tk), lambda i, j, kk: (j, kk)) if trans_b
              else pl.BlockSpec((tk, tn), lambda i, j, kk: (kk, j)))
    sem = ("arbitrary",) * 3 if comm else ("parallel", "parallel", "arbitrary")
    res = pl.pallas_call(
        body, name=name, grid=grid,
        in_specs=[a_spec, b_spec] + (comm.in_specs if comm else []),
        out_specs=[pl.BlockSpec((tm, tn), lambda i, j, kk: (i, j))] + (comm.out_specs if comm else []),
        out_shape=[jax.ShapeDtypeStruct((m, n), out_dtype)] + (comm.out_shape if comm else []),
        scratch_shapes=([] if nk == 1 else [pltpu.VMEM((tm, tn), F32)]) + (comm.scratch if comm else []),
        compiler_params=_params(sem, side_effects=bool(comm)),
    )(a, b, *(comm.operands if comm else []))
    return res if comm else res[0]


def _h_chunk(i, nx, h0_ref, x_ref):
    h = jnp.where(i == 0, h0_ref[...], x_ref[...])
    return h * (i <= nx).astype(F32)


def _x_spec(nx):
    return pl.BlockSpec((CHUNK, D_MODEL), lambda i: (jnp.clip(i - 1, 0, nx - 1), 0))


def _prenorm(x2d, h0, w):
    nx = x2d.shape[0] // CHUNK
    nc = nx + 2

    def body(x_ref, h0_ref, w_ref, o_ref):
        h = _h_chunk(pl.program_id(0), nx, h0_ref, x_ref)
        rstd = lax.rsqrt(jnp.mean(h * h, axis=-1, keepdims=True) + EPS)
        o_ref[...] = (h * rstd * w_ref[...]).astype(BF16)

    return pl.pallas_call(
        body, name="prenorm", grid=(nc,),
        in_specs=[_x_spec(nx), pl.BlockSpec((CHUNK, D_MODEL), lambda i: (0, 0)), pl.BlockSpec((1, D_MODEL), lambda i: (0, 0))],
        out_specs=pl.BlockSpec((CHUNK, D_MODEL), lambda i: (i, 0)),
        out_shape=jax.ShapeDtypeStruct((nc * CHUNK, D_MODEL), BF16),
        compiler_params=_params(("parallel",)),
    )(x2d, h0, w)


CONV_COLS = 512
HALO = 8


def _conv_pre(ext, w, b):
    acc = b + w[3:4, :] * ext[HALO:, :]
    for j in range(1, CONV_WIDTH):
        acc = acc + w[3 - j:4 - j, :] * pltpu.roll(ext, j, 0)[HALO:, :]
    return acc


def _conv_fwd(proj, conv_w, conv_b):
    lp = proj.shape[0]
    t = _tile(lp, (704, 384, 128, 64))
    hb = t // HALO
    c0 = OFF_XBC // CONV_COLS

    def body(u_ref, halo_ref, w_ref, b_ref, o_ref):
        i = pl.program_id(0)
        halo = halo_ref[...] * (i > 0).astype(F32)
        pre = _conv_pre(jnp.concatenate([halo, u_ref[...]], axis=0), w_ref[...], b_ref[...])
        o_ref[...] = pre * _sigmoid(pre)

    return pl.pallas_call(
        body, name="conv_fwd", grid=(lp // t, D_CONV // CONV_COLS),
        in_specs=[
            pl.BlockSpec((t, CONV_COLS), lambda i, j: (i, c0 + j)),
            pl.BlockSpec((HALO, CONV_COLS), lambda i, j: (jnp.maximum(i * hb - 1, 0), c0 + j)),
            pl.BlockSpec((CONV_WIDTH, CONV_COLS), lambda i, j: (0, j)),
            pl.BlockSpec((1, CONV_COLS), lambda i, j: (0, j)),
        ],
        out_specs=pl.BlockSpec((t, CONV_COLS), lambda i, j: (i, j)),
        out_shape=jax.ShapeDtypeStruct((lp, D_CONV), F32),
        compiler_params=_params(("parallel", "parallel")),
    )(proj, proj, conv_w, conv_b)


def _conv_bwd(dxbc, proj, conv_w, conv_b, dproj):
    lp = proj.shape[0]
    t = _tile(lp, (704, 384, 128, 64))
    hb = t // HALO
    nt = lp // t
    c0 = OFF_XBC // CONV_COLS

    def body(dx_ref, dxn_ref, u_ref, up_ref, un_ref, w_ref, b_ref, _, du_ref, dw_ref, db_ref):
        i = pl.program_id(1)
        w = w_ref[...]
        up = up_ref[...] * (i > 0).astype(F32)
        ext = jnp.concatenate([up, u_ref[...], un_ref[...]], axis=0)
        pre = _conv_pre(ext, w, b_ref[...])
        dxn = dxn_ref[...] * (i < nt - 1).astype(F32)
        dxe = jnp.concatenate([dx_ref[...], dxn], axis=0)
        sg = _sigmoid(pre)
        dpre = dxe * sg * (1.0 + pre * (1.0 - sg))
        du = w[3:4, :] * dpre[:t, :]
        for j in range(1, CONV_WIDTH):
            du = du + w[3 - j:4 - j, :] * pltpu.roll(dpre, t + HALO - j, 0)[:t, :]
        du_ref[...] = du.astype(BF16)

        @pl.when(i == 0)
        def _():
            dw_ref[...] = jnp.zeros_like(dw_ref)
            db_ref[...] = jnp.zeros_like(db_ref)

        dp = dpre[:t, :]
        db_ref[...] += jnp.sum(dp, axis=0, keepdims=True)
        hist = ext[:HALO + t, :]
        dw_ref[3:4, :] += jnp.sum(dp * hist[HALO:, :], axis=0, keepdims=True)
        for j in range(1, CONV_WIDTH):
            dw_ref[3 - j:4 - j, :] += jnp.sum(dp * pltpu.roll(hist, j, 0)[HALO:, :], axis=0, keepdims=True)

    nxt = lambda i: jnp.minimum((i + 1) * hb, lp // HALO - 1)
    return pl.pallas_call(
        body, name="conv_bwd", grid=(D_CONV // CONV_COLS, nt),
        in_specs=[
            pl.BlockSpec((t, CONV_COLS), lambda j, i: (i, j)),
            pl.BlockSpec((HALO, CONV_COLS), lambda j, i: (nxt(i), j)),
            pl.BlockSpec((t, CONV_COLS), lambda j, i: (i, c0 + j)),
            pl.BlockSpec((HALO, CONV_COLS), lambda j, i: (jnp.maximum(i * hb - 1, 0), c0 + j)),
            pl.BlockSpec((HALO, CONV_COLS), lambda j, i: (nxt(i), c0 + j)),
            pl.BlockSpec((CONV_WIDTH, CONV_COLS), lambda j, i: (0, j)),
            pl.BlockSpec((1, CONV_COLS), lambda j, i: (0, j)),
            ANY,
        ],
        out_specs=[
            pl.BlockSpec((t, CONV_COLS), lambda j, i: (i, c0 + j)),
            pl.BlockSpec((CONV_WIDTH, CONV_COLS), lambda j, i: (0, j)),
            pl.BlockSpec((1, CONV_COLS), lambda j, i: (0, j)),
        ],
        out_shape=[
            jax.ShapeDtypeStruct((lp, NP), BF16),
            jax.ShapeDtypeStruct((CONV_WIDTH, D_CONV), F32),
            jax.ShapeDtypeStruct((1, D_CONV), F32),
        ],
        input_output_aliases={7: 0},
        compiler_params=_params(("parallel", "arbitrary")),
    )(dxbc, dxbc, proj, proj, proj, conv_w, conv_b, dproj)


def _swap_halves(t):
    w = t.shape[-1]
    lane = lax.broadcasted_iota(jnp.int32, t.shape, 1)
    return jnp.where((lane % HEAD_DIM) < HEAD_DIM // 2, pltpu.roll(t, w - HEAD_DIM // 2, 1),
                     pltpu.roll(t, HEAD_DIM // 2, 1))


def _act_fwd(proj, cos_t, sin_t, expand, dt_bias_rep):
    lp = proj.shape[0]
    t = _tile(lp, (384, 128, 64))

    def body(q_ref, k_ref, v_ref, dt_ref, cos_ref, sin_ref, ex_ref, bias_ref, qo_ref, ko_ref, vo_ref, dto_ref):
        i = pl.program_id(0)
        cos = cos_ref[...]
        sin = sin_ref[...]
        q = q_ref[...]
        qo_ref[...] = (q * jnp.tile(cos, (1, D_ATT // 128)) + _swap_halves(q) * jnp.tile(sin, (1, D_ATT // 128))).astype(BF16)
        k = k_ref[...]
        ko_ref[...] = (k * jnp.tile(cos, (1, D_KV // 128)) + _swap_halves(k) * jnp.tile(sin, (1, D_KV // 128))).astype(BF16)
        vo_ref[...] = v_ref[...].astype(BF16)
        raw = _dot(dt_ref[...], ex_ref[...], NN, HIGHEST) + bias_ref[...]
        sp = jnp.maximum(raw, 0.0) + jnp.log1p(jnp.exp(-jnp.abs(raw)))
        row = i * t + lax.broadcasted_iota(jnp.int32, sp.shape, 0)
        dto_ref[...] = jnp.where(row >= PAD_LEAD, sp, 0.0)

    return pl.pallas_call(
        body, name="act_fwd", grid=(lp // t,),
        in_specs=[
            pl.BlockSpec((t, D_ATT), lambda i: (i, OFF_Q // D_ATT)),
            pl.BlockSpec((t, D_KV), lambda i: (i, OFF_K // D_KV)),
            pl.BlockSpec((t, D_KV), lambda i: (i, OFF_V // D_KV)),
            pl.BlockSpec((t, 128), lambda i: (i, OFF_DT // 128)),
            pl.BlockSpec((t, 128), lambda i: (i, 0)),
            pl.BlockSpec((t, 128), lambda i: (i, 0)),
            pl.BlockSpec((128, D_SSD), lambda i: (0, 0)),
            pl.BlockSpec((1, D_SSD), lambda i: (0, 0)),
        ],
        out_specs=[
            pl.BlockSpec((t, D_ATT), lambda i: (i, 0)),
            pl.BlockSpec((t, D_KV), lambda i: (i, 0)),
            pl.BlockSpec((t, D_KV), lambda i: (i, 0)),
            pl.BlockSpec((t, D_SSD), lambda i: (i, 0)),
        ],
        out_shape=[
            jax.ShapeDtypeStruct((lp, D_ATT), BF16),
            jax.ShapeDtypeStruct((lp, D_KV), BF16),
            jax.ShapeDtypeStruct((lp, D_KV), BF16),
            jax.ShapeDtypeStruct((lp, D_SSD), F32),
        ],
        compiler_params=_params(("parallel",)),
    )(proj, proj, proj, proj, cos_t, sin_t, expand, dt_bias_rep)


def _act_bwd(dqr, dkr, dv, dg, ddt_part, cos_t, sin_t, reduce_t, dproj):
    lp = dqr.shape[0]
    t = _tile(lp, (384, 128, 64))

    def body(dq_ref, dk_ref, dv_ref, dg_ref, ddt_ref, cos_ref, sin_ref, red_ref, _, o_ref, db_ref):
        i = pl.program_id(0)
        cos = cos_ref[...]
        sin = sin_ref[...]
        dq = dq_ref[...]
        dq = dq * jnp.tile(cos, (1, D_ATT // 128)) + _swap_halves(dq * jnp.tile(sin, (1, D_ATT // 128)))
        dk = dk_ref[...]
        dk = dk * jnp.tile(cos, (1, D_KV // 128)) + _swap_halves(dk * jnp.tile(sin, (1, D_KV // 128)))
        ddt = _dot(ddt_ref[...], red_ref[...], NN, HIGHEST)
        o_ref[...] = jnp.concatenate(
            [dq.astype(BF16), dg_ref[...].astype(BF16), dk.astype(BF16), dv_ref[...].astype(BF16), ddt.astype(BF16),
             jnp.zeros((t, NP - OFF_DT - 128), BF16)], axis=1)

        @pl.when(i == 0)
        def _():
            db_ref[...] = jnp.zeros_like(db_ref)

        db_ref[...] += jnp.sum(ddt, axis=0, keepdims=True)

    return pl.pallas_call(
        body, name="act_bwd", grid=(lp // t,),
        in_specs=[
            pl.BlockSpec((t, D_ATT), lambda i: (i, 0)),
            pl.BlockSpec((t, D_KV), lambda i: (i, 0)),
            pl.BlockSpec((t, D_KV), lambda i: (i, 0)),
            pl.BlockSpec((t, D_ATT), lambda i: (i, 0)),
            pl.BlockSpec((t, D_SSD), lambda i: (i, 0)),
            pl.BlockSpec((t, 128), lambda i: (i, 0)),
            pl.BlockSpec((t, 128), lambda i: (i, 0)),
            pl.BlockSpec((D_SSD, 128), lambda i: (0, 0)),
            ANY,
        ],
        out_specs=[pl.BlockSpec((t, TAIL_W), lambda i: (i, OFF_Q // TAIL_W)), pl.BlockSpec((1, 128), lambda i: (0, 0))],
        out_shape=[jax.ShapeDtypeStruct((lp, NP), BF16), jax.ShapeDtypeStruct((1, 128), F32)],
        input_output_aliases={8: 0},
        compiler_params=_params(("arbitrary",)),
    )(dqr, dkr, dv, dg, ddt_part, cos_t, sin_t, reduce_t, dproj)


def _ssd_common(xbc, dt, a, tri, sel):
    xs = xbc[:, :GROUP_W]
    b = xbc[:, GROUP_W:GROUP_W + D_STATE]
    c = xbc[:, GROUP_W + D_STATE:]
    cs = _dot(tri, dt * a, NN, HIGHEST)
    cs_t = _dot(sel, cs, NT, HIGHEST)
    xdt = xs * dt
    cs_last = cs[CHUNK - 1:CHUNK, :]
    return xs, b, c, cs, cs_t, xdt, jnp.exp(cs), jnp.exp(cs_last - cs), jnp.exp(cs_last)


def _decay_matrix(cs, cs_t, r, causal):
    seg = cs[:, r * HEAD_DIM:r * HEAD_DIM + 1] - cs_t[r:r + 1, :]
    return jnp.exp(jnp.where(causal, seg, -jnp.inf))


def _causal_mask():
    row = lax.broadcasted_iota(jnp.int32, (CHUNK, CHUNK), 0)
    col = lax.broadcasted_iota(jnp.int32, (CHUNK, CHUNK), 1)
    return row >= col


def _ssd_fwd(xbc, dt_rep, proj, a_rep, dsk_rep, wn, tri, sel):
    lp = xbc.shape[0]
    nc = lp // CHUNK
    gw = SSD_GPS * GROUP_W

    def body(xbc_ref, dt_ref, z_ref, a_ref, dsk_ref, wn_ref, tri_ref, sel_ref, yn_ref, ytot_ref, hprev_ref, h_scr):
        @pl.when(pl.program_id(1) == 0)
        def _():
            h_scr[...] = jnp.zeros_like(h_scr)

        causal = _causal_mask()
        G = range(SSD_GPS)
        colsl = [slice(gi * GROUP_W, (gi + 1) * GROUP_W) for gi in G]
        xbc = [xbc_ref[:, gi * XBC_BLK:(gi + 1) * XBC_BLK] for gi in G]
        dt = [dt_ref[:, colsl[gi]] for gi in G]
        xs = [xbc[gi][:, :GROUP_W] for gi in G]
        b = [xbc[gi][:, GROUP_W:GROUP_W + D_STATE].astype(BF16) for gi in G]
        c = [xbc[gi][:, GROUP_W + D_STATE:].astype(BF16) for gi in G]
        hprev = [h_scr[gi] for gi in G]
        cs = [_dot(tri_ref[...], dt[gi] * a_ref[:, colsl[gi]], NN, HIGHEST) for gi in G]
        cb = [_dot(c[gi], b[gi], NT) for gi in G]
        yoff = [_dot(c[gi], hprev[gi].astype(BF16)) for gi in G]
        cs_t = [_dot(sel_ref[...], cs[gi], NT, HIGHEST) for gi in G]
        xdt = [xs[gi] * dt[gi] for gi in G]
        cs_last = [cs[gi][CHUNK - 1:CHUNK, :] for gi in G]
        st = [_dot(b[gi], (xdt[gi] * jnp.exp(cs_last[gi] - cs[gi])).astype(BF16), TN) for gi in G]
        m = [[(cb[gi] * _decay_matrix(cs[gi], cs_t[gi], r, causal)).astype(BF16) for r in range(HEADS_PER_GROUP)] for gi in G]
        ydiag = [[_dot(m[gi][r], xdt[gi][:, r * HEAD_DIM:(r + 1) * HEAD_DIM].astype(BF16)) for r in range(HEADS_PER_GROUP)]
                 for gi in G]
        for gi in G:
            cols = colsl[gi]
            ytot = jnp.concatenate(ydiag[gi], axis=1) + yoff[gi] * jnp.exp(cs[gi]) + dsk_ref[:, cols] * xs[gi]
            z = z_ref[:, cols]
            gz = ytot * (z * _sigmoid(z))
            rstd = lax.rsqrt(jnp.mean(gz * gz, axis=-1, keepdims=True) + EPS)
            hprev_ref[gi] = hprev[gi]
            h_scr[gi] = hprev[gi] * jnp.exp(cs_last[gi]) + st[gi]
            ytot_ref[:, cols] = ytot
            yn_ref[:, cols] = (gz * rstd * wn_ref[:, cols]).astype(BF16)

    vec = pl.BlockSpec((1, gw), lambda g, c: (0, g))
    blk = pl.BlockSpec((CHUNK, gw), lambda g, c: (c, g))
    return pl.pallas_call(
        body, name="ssd_fwd", grid=(N_GROUPS // SSD_GPS, nc),
        in_specs=[
            pl.BlockSpec((CHUNK, SSD_GPS * XBC_BLK), lambda g, c: (c, g)),
            blk, blk, vec, vec, vec,
            pl.BlockSpec((CHUNK, CHUNK), lambda g, c: (0, 0)),
            pl.BlockSpec((8, GROUP_W), lambda g, c: (0, 0)),
        ],
        out_specs=[blk, blk, pl.BlockSpec((None, SSD_GPS, D_STATE, GROUP_W), lambda g, c: (c, g, 0, 0))],
        out_shape=[
            jax.ShapeDtypeStruct((lp, D_MIX), BF16),
            jax.ShapeDtypeStruct((lp, D_SSD), F32),
            jax.ShapeDtypeStruct((nc, N_GROUPS, D_STATE, GROUP_W), F32),
        ],
        scratch_shapes=[pltpu.VMEM((SSD_GPS, D_STATE, GROUP_W), F32)],
        compiler_params=_params(("parallel", "arbitrary")),
    )(xbc, dt_rep, proj, a_rep, dsk_rep, wn, tri, sel)


def _ssd_bwd(dmix, ytot, proj, xbc, dt_rep, hprev, a_rep, dsk_rep, wn, tri, sel, comm):
    lp = xbc.shape[0]
    nc = lp // CHUNK
    gw = SSD_GPS * GROUP_W
    ncm = comm.n
    n_in, n_out = 11, 6
    grid = (N_GROUPS // SSD_GPS, nc)

    def all_groups(refs, causal):
        (dyn_ref, ytot_ref, z_ref, xbc_ref, dt_ref, hprev_ref, a_ref, dsk_ref, wn_ref, tri_ref, sel_ref,
         dz_ref, dxbc_ref, ddt_ref, dd_ref, da_ref, dwn_ref, dh_scr) = refs
        G = range(SSD_GPS)
        H = range(HEADS_PER_GROUP)
        cl = [slice(gi * GROUP_W, (gi + 1) * GROUP_W) for gi in G]
        hl = [slice(r * HEAD_DIM, (r + 1) * HEAD_DIM) for r in H]
        tri = tri_ref[...]
        sel = sel_ref[...]
        xbc = [xbc_ref[:, gi * XBC_BLK:(gi + 1) * XBC_BLK] for gi in G]
        dt = [dt_ref[:, cl[gi]] for gi in G]
        a = [a_ref[:, cl[gi]] for gi in G]
        xs = [xbc[gi][:, :GROUP_W] for gi in G]
        bbf = [xbc[gi][:, GROUP_W:GROUP_W + D_STATE].astype(BF16) for gi in G]
        cbf = [xbc[gi][:, GROUP_W + D_STATE:].astype(BF16) for gi in G]
        hprev = [hprev_ref[gi] for gi in G]
        hbf = [hprev[gi].astype(BF16) for gi in G]
        dhn = [dh_scr[gi] for gi in G]
        dhnb = [dhn[gi].astype(BF16) for gi in G]
        cs = [_dot(tri, dt[gi] * a[gi], NN, HIGHEST) for gi in G]
        cb = [_dot(cbf[gi], bbf[gi], NT) for gi in G]
        g = [_dot(cbf[gi], hbf[gi]) for gi in G]
        dxw = [_dot(bbf[gi], dhnb[gi]) for gi in G]
        cs_t = [_dot(sel, cs[gi], NT, HIGHEST) for gi in G]
        dy = []
        for gi in G:
            ytot = ytot_ref[:, cl[gi]]
            z = z_ref[:, cl[gi]]
            dyn = dyn_ref[:, cl[gi]]
            sz = _sigmoid(z)
            silu_z = z * sz
            gz = ytot * silu_z
            rstd = lax.rsqrt(jnp.mean(gz * gz, axis=-1, keepdims=True) + EPS)
            xhat = gz * rstd
            dwn_ref[:, cl[gi]] += jnp.sum(dyn * xhat, axis=0, keepdims=True)
            dxhat = dyn * wn_ref[:, cl[gi]]
            dgz = rstd * (dxhat - xhat * jnp.mean(dxhat * xhat, axis=-1, keepdims=True))
            dy.append(dgz * silu_z)
            dz_ref[:, cl[gi]] = (dgz * ytot * (sz * (1.0 + z * (1.0 - sz)))).astype(BF16)
            dd_ref[:, cl[gi]] += jnp.sum(dy[gi] * xs[gi], axis=0, keepdims=True)
        xdt = [xs[gi] * dt[gi] for gi in G]
        e = [jnp.exp(cs[gi]) for gi in G]
        cs_last = [cs[gi][CHUNK - 1:CHUNK, :] for gi in G]
        dte = [jnp.exp(cs_last[gi] - cs[gi]) for gi in G]
        cd = [jnp.exp(cs_last[gi]) for gi in G]
        dgb = [(dy[gi] * e[gi]).astype(BF16) for gi in G]
        dyb = [dy[gi].astype(BF16) for gi in G]
        xdtb = [xdt[gi].astype(BF16) for gi in G]
        dc = [_dot(dgb[gi], hbf[gi], NT) for gi in G]
        dhprev = [_dot(cbf[gi], dgb[gi], TN) for gi in G]
        db = [_dot((xdt[gi] * dte[gi]).astype(BF16), dhnb[gi], NT) for gi in G]
        lm = [[_decay_matrix(cs[gi], cs_t[gi], r, causal) for r in H] for gi in G]
        m = [[cb[gi] * lm[gi][r] for r in H] for gi in G]
        dm = [[_dot(dyb[gi][:, hl[r]], xdtb[gi][:, hl[r]], NT) for r in H] for gi in G]
        dxdt_diag = [[_dot(m[gi][r].astype(BF16), dyb[gi][:, hl[r]], TN) for r in H] for gi in G]
        dseg = [[dm[gi][r] * m[gi][r] for r in H] for gi in G]
        ones = jnp.ones((CHUNK, HEAD_DIM), F32)
        colsum = [[_dot(dseg[gi][r], ones, TN, HIGHEST) for r in H] for gi in G]
        dcbb = []
        for gi in G:
            dcb = dm[gi][0] * lm[gi][0]
            for r in range(1, HEADS_PER_GROUP):
                dcb = dcb + dm[gi][r] * lm[gi][r]
            dcbb.append(dcb.astype(BF16))
        dc2 = [_dot(dcbb[gi], bbf[gi]) for gi in G]
        db2 = [_dot(dcbb[gi], cbf[gi], TN) for gi in G]
        dcs = []
        for gi in G:
            t_dte = dxw[gi] * xdt[gi] * dte[gi]
            dcs_last = (jnp.sum(dhn[gi] * hprev[gi], axis=0, keepdims=True) * cd[gi]
                        + jnp.sum(t_dte, axis=0, keepdims=True))
            diag = jnp.concatenate(
                [(jnp.sum(dseg[gi][r], axis=1, keepdims=True) - colsum[gi][r]) * (1.0 / HEAD_DIM) for r in H], axis=1)
            d = dy[gi] * g[gi] * e[gi] - t_dte + diag
            row = lax.broadcasted_iota(jnp.int32, d.shape, 0)
            dcs.append(d + jnp.where(row == CHUNK - 1, dcs_last, 0.0))
        dda = [_dot(tri, dcs[gi], TN, HIGHEST) for gi in G]
        for gi in G:
            dxdt = dxw[gi] * dte[gi] + jnp.concatenate(dxdt_diag[gi], axis=1)
            da_ref[:, cl[gi]] += jnp.sum(dda[gi] * dt[gi], axis=0, keepdims=True)
            ddt = dda[gi] * a[gi] + dxdt * xs[gi]
            dxs = dsk_ref[:, cl[gi]] * dy[gi] + dxdt * dt[gi]
            ddt_ref[:, cl[gi]] = ddt * (1.0 - jnp.exp(-dt[gi]))
            dxbc_ref[:, gi * XBC_BLK:(gi + 1) * XBC_BLK] = jnp.concatenate(
                [dxs, db[gi] + db2[gi], dc[gi] + dc2[gi]], axis=1)
            dh_scr[gi] = dhprev[gi] + dhn[gi] * cd[gi]

    def body(*refs):
        ins = refs[:n_in]
        cin = refs[n_in:n_in + ncm]
        outs = refs[n_in + ncm:n_in + ncm + n_out]
        cout = refs[n_in + ncm + n_out:n_in + 2 * ncm + n_out]
        dh_scr = refs[n_in + 2 * ncm + n_out]
        sems = refs[n_in + 2 * ncm + n_out + 1:]
        g, c = pl.program_id(0), pl.program_id(1)

        @pl.when((g == 0) & (c == 0))
        def _():
            comm.start(cin, cout, sems)

        @pl.when(c == 0)
        def _():
            dh_scr[...] = jnp.zeros_like(dh_scr)
            for ref in outs[3:]:
                ref[...] = jnp.zeros_like(ref)

        all_groups(ins + outs + (dh_scr,), _causal_mask())

        @pl.when((g == grid[0] - 1) & (c == nc - 1))
        def _():
            comm.wait(cin, cout, sems)

    rev = lambda c: nc - 1 - c
    vec = pl.BlockSpec((1, gw), lambda g, c: (0, g))
    blk = pl.BlockSpec((CHUNK, gw), lambda g, c: (rev(c), g))
    xblk = pl.BlockSpec((CHUNK, SSD_GPS * XBC_BLK), lambda g, c: (rev(c), g))
    res = pl.pallas_call(
        body, name="ssd_bwd", grid=grid,
        in_specs=[blk, blk, blk, xblk, blk,
                  pl.BlockSpec((None, SSD_GPS, D_STATE, GROUP_W), lambda g, c: (rev(c), g, 0, 0)),
                  vec, vec, vec,
                  pl.BlockSpec((CHUNK, CHUNK), lambda g, c: (0, 0)),
                  pl.BlockSpec((8, GROUP_W), lambda g, c: (0, 0))] + comm.in_specs,
        out_specs=[blk, xblk, blk, vec, vec, vec] + comm.out_specs,
        out_shape=[
            jax.ShapeDtypeStruct((lp, NP), BF16),
            jax.ShapeDtypeStruct((lp, D_CONV), F32),
            jax.ShapeDtypeStruct((lp, D_SSD), F32),
            jax.ShapeDtypeStruct((1, D_SSD), F32),
            jax.ShapeDtypeStruct((1, D_SSD), F32),
            jax.ShapeDtypeStruct((1, D_SSD), F32),
        ] + comm.out_shape,
        scratch_shapes=[pltpu.VMEM((SSD_GPS, D_STATE, GROUP_W), F32)] + comm.scratch,
        compiler_params=_params(("arbitrary", "arbitrary"), side_effects=True),
    )(dmix, ytot, proj, xbc, dt_rep, hprev, a_rep, dsk_rep, wn, tri, sel, *comm.operands)
    return res


def _stack_heads(t, h):
    return jnp.concatenate([t[:, (REP * h + r) * HEAD_DIM:(REP * h + r + 1) * HEAD_DIM] for r in range(REP)], axis=0)


def _band(t2, t1, t0, h):
    sl = slice(h * HEAD_DIM, (h + 1) * HEAD_DIM)
    return jnp.concatenate([t2[:, sl], t1[:, sl], t0[:, sl]], axis=0)


def _attn_probs(s, sink, qc):
    s = s * (HEAD_DIM ** -0.5)
    key_abs = (qc - WINDOW_CHUNKS) * CHUNK + lax.broadcasted_iota(jnp.int32, s.shape, 1)
    s = jnp.where(key_abs >= PAD_LEAD, s, -jnp.inf)
    m = jnp.maximum(jnp.max(s, axis=-1, keepdims=True), sink)
    p = jnp.exp(s - m)
    ps = jnp.exp(sink - m)
    denom = jnp.sum(p, axis=-1, keepdims=True) + ps
    return p / denom, ps / denom


ATT_QC = 2
ATT_KC = WINDOW_CHUNKS + ATT_QC


def _kv_specs(width, newest_chunk_of):
    return [pl.BlockSpec((CHUNK, width), functools.partial(lambda j, p: (jnp.maximum(newest_chunk_of(p) - j, 0), 0), j))
            for j in range(ATT_KC - 1, -1, -1)]


def _attn_fwd(qr, kr, vb, proj, sink_stack, mix):
    lp = qr.shape[0]
    nc = lp // CHUNK

    assert nc % ATT_QC == 0
    qrows = ATT_QC * CHUNK

    def body(q_ref, *rest):
        k_refs, v_refs = rest[:ATT_KC], rest[ATT_KC:2 * ATT_KC]
        g_ref, sink_ref, _, att_ref, mix_ref = rest[2 * ATT_KC:]
        p = pl.program_id(0)
        q = q_ref[...]
        ks = [r[...] for r in k_refs]
        vs = [r[...] for r in v_refs]
        units = [(u, h) for u in range(ATT_QC) for h in range(KV_HEADS)]
        s = [_dot(_stack_heads(q[u * CHUNK:(u + 1) * CHUNK, :], h), _band(*ks[u:u + 3], h), NT) for u, h in units]
        vbh = [_band(*vs[u:u + 3], h) for u, h in units]
        pn = [_attn_probs(s[i], sink_ref[h], ATT_QC * p + u)[0].astype(BF16) for i, (u, h) in enumerate(units)]
        o = [_dot(pn[i], vbh[i]) for i in range(len(units))]
        att = jnp.concatenate(
            [jnp.concatenate([o[u * KV_HEADS + h][r * CHUNK:(r + 1) * CHUNK, :] for h in range(KV_HEADS) for r in range(REP)],
                             axis=1) for u in range(ATT_QC)], axis=0)
        att_ref[...] = att
        g = g_ref[...]
        mix_ref[...] = (att * (g * _sigmoid(g))).astype(BF16)

    newest = lambda p: ATT_QC * p + ATT_QC - 1
    return pl.pallas_call(
        body, name="attn_fwd", grid=(nc // ATT_QC,),
        in_specs=[pl.BlockSpec((qrows, D_ATT), lambda p: (p, 0))] + _kv_specs(D_KV, newest) + _kv_specs(D_KV, newest) + [
            pl.BlockSpec((qrows, D_ATT), lambda p: (p, OFF_G // D_ATT)),
            pl.BlockSpec((KV_HEADS, REP * CHUNK, 1), lambda p: (0, 0, 0)),
            ANY,
        ],
        out_specs=[pl.BlockSpec((qrows, D_ATT), lambda p: (p, 0)),
                   pl.BlockSpec((qrows, D_ATT), lambda p: (p, D_SSD // D_ATT))],
        out_shape=[jax.ShapeDtypeStruct((lp, D_ATT), F32), jax.ShapeDtypeStruct((lp, D_MIX), BF16)],
        input_output_aliases={2 * ATT_KC + 3: 1},
        compiler_params=_params(("parallel",)),
    )(qr, *([kr] * ATT_KC), *([vb] * ATT_KC), proj, sink_stack, mix)


def _attn_bwd(qr, kr, vb, att, proj, dmix, sink_stack):
    lp = qr.shape[0]
    nc = lp // CHUNK
    assert nc % ATT_QC == 0
    npairs = nc // ATT_QC
    qrows = ATT_QC * CHUNK
    wrows = ATT_KC * CHUNK

    def body(q_ref, *rest):
        k_refs, v_refs = rest[:ATT_KC], rest[ATT_KC:2 * ATT_KC]
        (att_ref, g_ref, do_ref, sink_ref, dq_ref, dk_ref, dv_ref, dg_ref, dsink_ref, dk_acc, dv_acc) = rest[2 * ATT_KC:]
        step = pl.program_id(0)

        @pl.when(step == 0)
        def _():
            dk_acc[...] = jnp.zeros_like(dk_acc)
            dv_acc[...] = jnp.zeros_like(dv_acc)
            dsink_ref[...] = jnp.zeros_like(dsink_ref)

        @pl.when(step < npairs)
        def _():
            q = q_ref[...]
            ks = [r[...] for r in k_refs]
            vs = [r[...] for r in v_refs]
            att = att_ref[...]
            g = g_ref[...]
            dog = do_ref[...]
            sg = _sigmoid(g)
            dg_ref[...] = dog * att * (sg * (1.0 + g * (1.0 - sg)))
            do = dog * (g * sg)
            units = [(u, h) for u in range(ATT_QC) for h in range(KV_HEADS)]
            n = range(len(units))
            rows = [slice(u * CHUNK, (u + 1) * CHUNK) for u in range(ATT_QC)]
            qs = [_stack_heads(q[rows[u], :], h) for u, h in units]
            kb = [_band(*ks[u:u + 3], h) for u, h in units]
            vbh = [_band(*vs[u:u + 3], h) for u, h in units]
            dos = [_stack_heads(do[rows[u], :], h) for u, h in units]
            dosb = [dos[i].astype(BF16) for i in n]
            s = [_dot(qs[i], kb[i], NT) for i in n]
            dp = [_dot(dosb[i], vbh[i], NT) for i in n]
            ds, pnb = [], []
            for i, (u, h) in enumerate(units):
                pn, psink = _attn_probs(s[i], sink_ref[h], ATT_QC * step + u)
                delta = jnp.sum(dos[i] * _stack_heads(att[rows[u], :], h), axis=-1, keepdims=True)
                ds.append((pn * (dp[i] - delta)).astype(BF16))
                pnb.append(pn.astype(BF16))
                dsink_ref[h] += -(psink * delta)
            dqs = [_dot(ds[i], kb[i]) for i in n]
            dks = [_dot(ds[i], qs[i], TN) for i in n]
            dvs = [_dot(pnb[i], dosb[i], TN) for i in n]
            dq_ref[...] = jnp.concatenate(
                [jnp.concatenate([dqs[u * KV_HEADS + h][r * CHUNK:(r + 1) * CHUNK, :]
                                  for h in range(KV_HEADS) for r in range(REP)], axis=1) for u in range(ATT_QC)],
                axis=0) * (HEAD_DIM ** -0.5)
            for u in range(ATT_QC):
                band = slice(u * CHUNK, u * CHUNK + BAND)
                dk_acc[band, :] += jnp.concatenate(dks[u * KV_HEADS:(u + 1) * KV_HEADS], axis=1) * (HEAD_DIM ** -0.5)
                dv_acc[band, :] += jnp.concatenate(dvs[u * KV_HEADS:(u + 1) * KV_HEADS], axis=1)

        dk_ref[...] = dk_acc[0:qrows, :]
        dv_ref[...] = dv_acc[0:qrows, :]
        for acc in (dk_acc, dv_acc):
            rest_rows = acc[qrows:wrows, :]
            acc[0:wrows - qrows, :] = rest_rows
            acc[wrows - qrows:wrows, :] = jnp.zeros((qrows, D_KV), F32)

    qp = lambda p: jnp.minimum(p, npairs - 1)
    newest = lambda p: ATT_QC * qp(p) + ATT_QC - 1
    qblk = pl.BlockSpec((qrows, D_ATT), lambda p: (qp(p), 0))
    oldest = pl.BlockSpec((qrows, D_KV), lambda p: (jnp.maximum(p - 1, 0), 0))
    return pl.pallas_call(
        body, name="attn_bwd", grid=(npairs + 1,),
        in_specs=[qblk] + _kv_specs(D_KV, newest) + _kv_specs(D_KV, newest) + [
            qblk,
            pl.BlockSpec((qrows, D_ATT), lambda p: (qp(p), OFF_G // D_ATT)),
            pl.BlockSpec((qrows, D_ATT), lambda p: (qp(p), D_SSD // D_ATT)),
            pl.BlockSpec((KV_HEADS, REP * CHUNK, 1), lambda p: (0, 0, 0)),
        ],
        out_specs=[qblk, oldest, oldest, qblk, pl.BlockSpec((KV_HEADS, REP * CHUNK, 1), lambda p: (0, 0, 0))],
        out_shape=[
            jax.ShapeDtypeStruct((lp, D_ATT), F32),
            jax.ShapeDtypeStruct((lp, D_KV), F32),
            jax.ShapeDtypeStruct((lp, D_KV), F32),
            jax.ShapeDtypeStruct((lp, D_ATT), F32),
            jax.ShapeDtypeStruct((KV_HEADS, REP * CHUNK, 1), F32),
        ],
        scratch_shapes=[pltpu.VMEM((wrows, D_KV), F32), pltpu.VMEM((wrows, D_KV), F32)],
        compiler_params=_params(("arbitrary",)),
    )(qr, *([kr] * ATT_KC), *([vb] * ATT_KC), att, proj, dmix, sink_stack)


def _post_loss(out, x2d, target, w):
    lp = out.shape[0]
    nc = lp // CHUNK
    nx = x2d.shape[0] // CHUNK

    def body(o_ref, x_ref, t_ref, w_ref, do_ref, dy_ref, gw_ref, loss_ref):
        i = pl.program_id(0)

        @pl.when(i == 0)
        def _():
            gw_ref[...] = jnp.zeros_like(gw_ref)
            loss_ref[...] = jnp.zeros_like(loss_ref)

        o = o_ref[...]
        w = w_ref[...]
        rstd = lax.rsqrt(jnp.mean(o * o, axis=-1, keepdims=True) + EPS)
        xhat = o * rstd
        err = (x_ref[...] + xhat * w - t_ref[...]) * ((i > 0) & (i <= nx)).astype(F32)
        loss_ref[...] += 0.5 * jnp.sum(jnp.mean(err * err, axis=-1, keepdims=True), axis=0, keepdims=True)
        dy = err * (1.0 / D_MODEL)
        dy_ref[...] = dy
        gw_ref[...] += jnp.sum(dy * xhat, axis=0, keepdims=True)
        dxhat = dy * w
        do_ref[...] = (rstd * (dxhat - xhat * jnp.mean(dxhat * xhat, axis=-1, keepdims=True))).astype(BF16)

    row = pl.BlockSpec((CHUNK, D_MODEL), lambda i: (i, 0))
    return pl.pallas_call(
        body, name="post_loss", grid=(nc,),
        in_specs=[row, _x_spec(nx), _x_spec(nx), pl.BlockSpec((1, D_MODEL), lambda i: (0, 0))],
        out_specs=[row, row, pl.BlockSpec((1, D_MODEL), lambda i: (0, 0)), pl.BlockSpec((1, 128), lambda i: (0, 0))],
        out_shape=[
            jax.ShapeDtypeStruct((lp, D_MODEL), BF16),
            jax.ShapeDtypeStruct((lp, D_MODEL), F32),
            jax.ShapeDtypeStruct((1, D_MODEL), F32),
            jax.ShapeDtypeStruct((1, 128), F32),
        ],
        compiler_params=_params(("arbitrary",)),
    )(out, x2d, target, w)


def _prenorm_bwd(dhn, x2d, h0, dy, w):
    lp = dhn.shape[0]
    nc = lp // CHUNK
    nx = x2d.shape[0] // CHUNK

    def body(dhn_ref, x_ref, h0_ref, dy_ref, w_ref, gx_ref, d0_ref, gw_ref):
        i = pl.program_id(0)

        @pl.when(i == 0)
        def _():
            gw_ref[...] = jnp.zeros_like(gw_ref)

        h = _h_chunk(i, nx, h0_ref, x_ref)
        dhn = dhn_ref[...]
        rstd = lax.rsqrt(jnp.mean(h * h, axis=-1, keepdims=True) + EPS)
        xhat = h * rstd
        gw_ref[...] += jnp.sum(dhn * xhat, axis=0, keepdims=True)
        dxhat = dhn * w_ref[...]
        dh = rstd * (dxhat - xhat * jnp.mean(dxhat * xhat, axis=-1, keepdims=True)) + dy_ref[...]

        @pl.when(i == 0)
        def _():
            d0_ref[...] = dh

        @pl.when((i > 0) & (i <= nx))
        def _():
            gx_ref[...] = dh

    row = pl.BlockSpec((CHUNK, D_MODEL), lambda i: (i, 0))
    return pl.pallas_call(
        body, name="prenorm_bwd", grid=(nc,),
        in_specs=[row, _x_spec(nx), pl.BlockSpec((CHUNK, D_MODEL), lambda i: (0, 0)), row,
                  pl.BlockSpec((1, D_MODEL), lambda i: (0, 0))],
        out_specs=[_x_spec(nx), pl.BlockSpec((CHUNK, D_MODEL), lambda i: (0, 0)),
                   pl.BlockSpec((1, D_MODEL), lambda i: (0, 0))],
        out_shape=[
            jax.ShapeDtypeStruct((nx * CHUNK, D_MODEL), F32),
            jax.ShapeDtypeStruct((CHUNK, D_MODEL), F32),
            jax.ShapeDtypeStruct((1, D_MODEL), F32),
        ],
        compiler_params=_params(("arbitrary",)),
    )(dhn, x2d, h0, dy, w)


def _adamw(slabs, w, m, v, name):
    rows, cols = w.shape
    tr = _tile(rows, (256, 128, 64, 16, 8))
    c1 = 1.0 - ADAM_B1 ** ADAM_STEP
    c2 = 1.0 - ADAM_B2 ** ADAM_STEP

    def body(s_ref, w_ref, m_ref, v_ref, g_ref, d_ref, mo_ref, vo_ref):
        g = s_ref[0].astype(F32)
        for k in range(1, N_DEV):
            g = g + s_ref[k].astype(F32)
        w = w_ref[...]
        m = ADAM_B1 * m_ref[...] + (1.0 - ADAM_B1) * g
        v = ADAM_B2 * v_ref[...] + (1.0 - ADAM_B2) * (g * g)
        g_ref[...] = g
        mo_ref[...] = m
        vo_ref[...] = v
        d_ref[...] = -ADAM_LR * ((m / c1) / (jnp.sqrt(v / c2) + ADAM_EPS) + ADAM_WD * w)

    blk = pl.BlockSpec((tr, cols), lambda i: (i, 0))
    shape = jax.ShapeDtypeStruct((rows, cols), F32)
    return pl.pallas_call(
        body, name=name, grid=(rows // tr,),
        in_specs=[pl.BlockSpec((N_DEV, tr, cols), lambda i: (0, i, 0)), blk, blk, blk],
        out_specs=[blk, blk, blk, blk],
        out_shape=[shape, shape, shape, shape],
        compiler_params=_params(("parallel",)),
    )(slabs, w, m, v)


def _perm_xbc(a):
    lead = a.shape[:-1]
    xs = a[..., :D_SSD].reshape(lead + (N_GROUPS, GROUP_W))
    b = a[..., D_SSD:D_SSD + N_GROUPS * D_STATE].reshape(lead + (N_GROUPS, D_STATE))
    c = a[..., D_SSD + N_GROUPS * D_STATE:].reshape(lead + (N_GROUPS, D_STATE))
    return jnp.concatenate([xs, b, c], axis=-1).reshape(lead + (D_CONV,))


def _unperm_xbc(a):
    lead = a.shape[:-1]
    t = a.reshape(lead + (N_GROUPS, XBC_BLK))
    xs = t[..., :GROUP_W].reshape(lead + (D_SSD,))
    b = t[..., GROUP_W:GROUP_W + D_STATE].reshape(lead + (N_GROUPS * D_STATE,))
    c = t[..., GROUP_W + D_STATE:].reshape(lead + (N_GROUPS * D_STATE,))
    return jnp.concatenate([xs, b, c], axis=-1)


R_Z, R_XBC, R_DT, R_Q, R_K, R_V, R_G = 0, 2048, 6144, 6176, 7200, 7456, 7712


def _internal_of_reference():
    ref = np.arange(D_IN_PROJ)
    out = np.empty(D_IN_PROJ, np.int64)
    out[R_Z:R_XBC] = OFF_Z + ref[:D_SSD]
    xs = np.arange(D_SSD)
    out[R_XBC:R_XBC + D_SSD] = OFF_XBC + (xs // GROUP_W) * XBC_BLK + xs % GROUP_W
    bc = np.arange(N_GROUPS * D_STATE)
    out[R_XBC + D_SSD:R_XBC + D_SSD + N_GROUPS * D_STATE] = OFF_XBC + (bc // D_STATE) * XBC_BLK + GROUP_W + bc % D_STATE
    out[R_XBC + D_SSD + N_GROUPS * D_STATE:R_DT] = OFF_XBC + (bc // D_STATE) * XBC_BLK + GROUP_W + D_STATE + bc % D_STATE
    out[R_DT:R_Q] = OFF_DT + np.arange(SSD_HEADS)
    out[R_Q:R_K] = OFF_Q + np.arange(D_ATT)
    out[R_K:R_V] = OFF_K + np.arange(D_KV)
    out[R_V:R_G] = OFF_V + np.arange(D_KV)
    out[R_G:] = OFF_G + np.arange(D_ATT)
    return out


def _runs(src, dst_break):
    runs, lo = [], 0
    for i in range(1, len(src) + 1):
        if i == len(src) or src[i] != src[i - 1] + 1 or dst_break[i] != dst_break[i - 1]:
            runs.append((lo, i))
            lo = i
    return runs


def _w_in_internal(w_gathered):
    int_of_ref = _internal_of_reference()
    ref_of_int = np.full(NP, -1, np.int64)
    ref_of_int[int_of_ref] = np.arange(D_IN_PROJ)
    shard = np.where(ref_of_int >= 0, ref_of_int // SHARD_IN, -1)
    src = np.where(ref_of_int >= 0, ref_of_int, -10 - 2 * np.arange(NP))
    pieces, zeros = [], 0
    for lo, hi in _runs(src, shard):
        if ref_of_int[lo] < 0:
            zeros += hi - lo
            continue
        if zeros:
            pieces.append(jnp.zeros((D_MODEL, zeros), w_gathered.dtype))
            zeros = 0
        s, c0 = ref_of_int[lo] // SHARD_IN, ref_of_int[lo] % SHARD_IN
        pieces.append(w_gathered[s, :, c0:c0 + hi - lo])
    if zeros:
        pieces.append(jnp.zeros((D_MODEL, zeros), w_gathered.dtype))
    return jnp.concatenate(pieces, axis=1)


def _w_in_slabs(dw):
    int_of_ref = _internal_of_reference()
    slabs = []
    for s in range(N_DEV):
        cols = int_of_ref[s * SHARD_IN:(s + 1) * SHARD_IN]
        slabs.append(jnp.concatenate([dw[:, cols[lo]:cols[lo] + hi - lo]
                                      for lo, hi in _runs(cols, np.zeros_like(cols))], axis=1))
    return jnp.stack(slabs, axis=0)


def _rep_heads(a):
    return jnp.repeat(a, HEAD_DIM, axis=1)


SMALL = (("norm_pre_w", 2048), ("conv_b", 4096), ("dt_bias", 32), ("a_log", 32), ("d_skip", 32),
         ("ssd_norm_w", 2048), ("attn_sinks", 16), ("norm_post_w", 2048))
SMALL_USED = sum(size for _, size in SMALL)
SMALL_LEN = 10368


def _pack_small(d, loss=None):
    parts = [d[name].reshape(1, size) for name, size in SMALL]
    tail = jnp.zeros((1, SMALL_LEN - SMALL_USED), F32)
    if loss is not None:
        tail = tail.at[0, 0].set(loss)
    return jnp.concatenate(parts + [tail], axis=1)


def _unpack_small(vec):
    out, off = {}, 0
    for name, size in SMALL:
        out[name] = vec[:, off:off + size]
        off += size
    return out


def kernel(x, meta_tokens, norm_pre_w, w_in, conv_w, conv_b, dt_bias, a_log, d_skip, ssd_norm_w, attn_sinks, w_out, norm_post_w, loss_target, m_meta_tokens, m_norm_pre_w, m_w_in, m_conv_w, m_conv_b, m_dt_bias, m_a_log, m_d_skip, m_ssd_norm_w, m_attn_sinks, m_w_out, m_norm_post_w, v_meta_tokens, v_norm_pre_w, v_w_in, v_conv_w, v_conv_b, v_dt_bias, v_a_log, v_d_skip, v_ssd_norm_w, v_attn_sinks, v_w_out, v_norm_post_w):
    seq = x.shape[1]
    lp = seq + 2 * CHUNK
    x2d = x[0]

    w_in_g, conv_w_g, meta_g = _gather_two_level([w_in[0].astype(BF16), conv_w[0], meta_tokens], "gather_w_in")
    w_all = _w_in_internal(w_in_g)
    conv_w_full = _perm_xbc(jnp.transpose(conv_w_g, (1, 0, 2)).reshape(CONV_WIDTH, D_CONV))
    conv_b_int = _perm_xbc(conv_b)
    meta_full = jnp.transpose(meta_g, (1, 0, 2)).reshape(N_META, D_MODEL)
    h0 = jnp.concatenate([jnp.zeros((PAD_LEAD, D_MODEL), F32), meta_full], axis=0)

    pos = (jnp.arange(lp) - PAD_LEAD).astype(F32)
    half = HEAD_DIM // 2
    inv = ROPE_THETA ** (-jnp.arange(half, dtype=F32) / half)
    ang = pos[:, None] * inv[None, :]
    cos_t = jnp.tile(jnp.cos(ang), (1, 4))
    sin_t = jnp.tile(jnp.concatenate([-jnp.sin(ang), jnp.sin(ang)], axis=1), (1, 2))
    head_of_col = np.arange(D_SSD) // HEAD_DIM
    expand = jnp.asarray((np.arange(128)[:, None] == head_of_col[None, :]).astype(np.float32))
    reduce_t = jnp.asarray((head_of_col[:, None] == np.arange(128)[None, :]).astype(np.float32))
    tri = jnp.asarray(np.tril(np.ones((CHUNK, CHUNK), np.float32)))
    sel = jnp.asarray((np.arange(8)[:, None] * HEAD_DIM == np.arange(GROUP_W)[None, :]).astype(np.float32))
    a_rep = _rep_heads(-jnp.exp(a_log))
    dsk_rep = _rep_heads(d_skip)
    dt_bias_rep = _rep_heads(dt_bias)
    sink_stack = jnp.repeat(attn_sinks.reshape(KV_HEADS, REP), CHUNK, axis=1).reshape(KV_HEADS, REP * CHUNK, 1)

    hn = _prenorm(x2d, h0, norm_pre_w)
    tm = _tile(lp, (1056, 704, 128, 64))
    proj, w_out_g = _matmul(hn, w_all, tm=tm, tn=1536, tk=D_MODEL, out_dtype=F32, name="in_proj",
                            comm=_Comm([(w_out[0].astype(BF16), "gather")]))
    w_out_full = w_out_g.reshape(D_MIX, D_MODEL)
    xbc = _conv_fwd(proj, conv_w_full, conv_b_int)
    qr, kr, vb, dt_rep = _act_fwd(proj, cos_t, sin_t, expand, dt_bias_rep)
    mix, ytot, hprev = _ssd_fwd(xbc, dt_rep, proj, a_rep, dsk_rep, ssd_norm_w, tri, sel)
    att, mix = _attn_fwd(qr, kr, vb, proj, sink_stack, mix)
    out = _matmul(mix, w_out_full, tm=tm, tn=1024, tk=D_MIX, out_dtype=F32, name="out_proj")
    dout, dy, g_norm_post, loss_part = _post_loss(out, x2d, loss_target[0], norm_post_w)

    dmix = _matmul(dout, w_out_full, trans_b=True, tm=tm, tn=1024, tk=D_MODEL, out_dtype=F32, name="dmix")
    dw_out = _matmul(mix, dout, trans_a=True, tm=512, tn=1024, tk=lp, out_dtype=BF16, name="dw_out")
    dqr, dkr, dv, dg, dsink_rows = _attn_bwd(qr, kr, vb, att, proj, dmix, sink_stack)
    dproj, dxbc, ddt_part, dd_part, da_part, g_ssd_norm, g_out = _ssd_bwd(
        dmix, ytot, proj, xbc, dt_rep, hprev, a_rep, dsk_rep, ssd_norm_w, tri, sel,
        _Comm([(dw_out.reshape(N_DEV, D_MIX // N_DEV, D_MODEL), "scatter")]))
    dproj, dconv_w_int, dconv_b_int = _conv_bwd(dxbc, proj, conv_w_full, conv_b_int, dproj)
    dproj, ddt_bias = _act_bwd(dqr, dkr, dv, dg, ddt_part, cos_t, sin_t, reduce_t, dproj)
    dw_all = _matmul(hn, dproj, trans_a=True, tm=512, tn=1024, tk=lp, out_dtype=BF16, name="dw_in")
    dhn, g_in = _matmul(dproj, w_all, trans_b=True, tm=tm, tn=1024, tk=1536, out_dtype=F32, name="dhn",
                        comm=_Comm([(_w_in_slabs(dw_all), "scatter")]))
    grad_x, dh0, g_norm_pre = _prenorm_bwd(dhn, x2d, h0, dy, norm_pre_w)

    dmeta = dh0[PAD_LEAD:, :]
    dconv_w_ref = _unperm_xbc(dconv_w_int)
    heads = lambda part: part.reshape(SSD_HEADS, HEAD_DIM).sum(axis=1).reshape(1, SSD_HEADS)
    small_local = _pack_small({
        "norm_pre_w": g_norm_pre, "conv_b": _unperm_xbc(dconv_b_int), "dt_bias": ddt_bias[:, :SSD_HEADS],
        "a_log": heads(da_part) * (-jnp.exp(a_log)), "d_skip": heads(dd_part), "ssd_norm_w": g_ssd_norm,
        "attn_sinks": dsink_rows.reshape(Q_HEADS, CHUNK).sum(axis=1).reshape(1, Q_HEADS),
        "norm_post_w": g_norm_post}, loss=loss_part[0, 0])
    g_conv, g_meta, g_small = _exchange(
        [(jnp.transpose(dconv_w_ref.reshape(CONV_WIDTH, N_DEV, D_CONV // N_DEV), (1, 0, 2)), "scatter"),
         (jnp.transpose(dmeta.reshape(N_META, N_DEV, D_MODEL // N_DEV), (1, 0, 2)), "scatter"),
         (small_local, "gather")], "exchange_small")

    res = {}
    res["w_in"] = [o[None] for o in _adamw(g_in, w_in[0], m_w_in[0], v_w_in[0], "adamw_w_in")]
    res["w_out"] = [o[None] for o in _adamw(g_out, w_out[0], m_w_out[0], v_w_out[0], "adamw_w_out")]
    res["conv_w"] = [o[None] for o in _adamw(g_conv, conv_w[0], m_conv_w[0], v_conv_w[0], "adamw_conv_w")]
    res["meta_tokens"] = _adamw(g_meta, meta_tokens, m_meta_tokens, v_meta_tokens, "adamw_meta")
    given = dict(norm_pre_w=(norm_pre_w, m_norm_pre_w, v_norm_pre_w), conv_b=(conv_b, m_conv_b, v_conv_b),
                 dt_bias=(dt_bias, m_dt_bias, v_dt_bias), a_log=(a_log, m_a_log, v_a_log),
                 d_skip=(d_skip, m_d_skip, v_d_skip), ssd_norm_w=(ssd_norm_w, m_ssd_norm_w, v_ssd_norm_w),
                 attn_sinks=(attn_sinks, m_attn_sinks, v_attn_sinks),
                 norm_post_w=(norm_post_w, m_norm_post_w, v_norm_post_w))
    packed = [_pack_small({k: t[j] for k, t in given.items()}) for j in range(3)]
    small_out = _adamw(g_small, packed[0], packed[1], packed[2], "adamw_small")
    small_res = [_unpack_small(r) for r in small_out]
    loss = small_out[0][0, SMALL_USED]

    order = ["meta_tokens", "norm_pre_w", "w_in", "conv_w", "conv_b", "dt_bias", "a_log", "d_skip", "ssd_norm_w",
             "attn_sinks", "w_out", "norm_post_w"]
    outs = []
    for j in range(4):
        for name in order:
            outs.append(res[name][j] if name in res else small_res[j][name])
    return (loss, grad_x[None], *outs)
```

```python
import functools

import numpy as np
import jax
import jax.numpy as jnp
from jax import lax
from jax.experimental import pallas as pl
from jax.experimental.pallas import tpu as pltpu

F32 = jnp.float32
BF16 = jnp.bfloat16
HIGHEST = lax.Precision.HIGHEST

N_DEV = 8
D_MODEL = 2048
CHUNK = 64
N_META = 16
PAD_LEAD = CHUNK - N_META
EPS = 1e-6
N_GROUPS = 8
HEADS_PER_GROUP = 4
HEAD_DIM = 64
GROUP_W = HEADS_PER_GROUP * HEAD_DIM
D_STATE = 128
D_SSD = 2048
D_CONV = 4096
SSD_HEADS = 32
CONV_WIDTH = 4
Q_HEADS = 16
KV_HEADS = 4
REP = 4
D_ATT = 1024
D_KV = 256
WINDOW_CHUNKS = 2
BAND = (WINDOW_CHUNKS + 1) * CHUNK
ROPE_THETA = 10000.0
D_MIX = D_SSD + D_ATT
D_IN_PROJ = 8736
SHARD_IN = D_IN_PROJ // N_DEV

OFF_Z, OFF_XBC, OFF_Q, OFF_G, OFF_K, OFF_V, OFF_DT = 0, 2048, 6144, 7168, 8192, 8448, 8704
NP = 9216
TAIL_W = NP - OFF_Q
XBC_BLK = 512
SSD_GPS = 8

ADAM_LR, ADAM_B1, ADAM_B2, ADAM_EPS, ADAM_WD, ADAM_STEP = 0.001, 0.9, 0.999, 1e-08, 0.01, 10

VMEM_LIMIT = 48 * 1024 * 1024

NN = (((1,), (0,)), ((), ()))
NT = (((1,), (1,)), ((), ()))
TN = (((0,), (0,)), ((), ()))
ANY = pl.BlockSpec(memory_space=pl.ANY)


def _dot(a, b, dims=NN, precision=None):
    return lax.dot_general(a, b, dims, precision=precision, preferred_element_type=F32)


def _tile(n, prefs):
    for t in prefs:
        if n % t == 0:
            return t
    return n


def _params(sem, vmem=VMEM_LIMIT, side_effects=False):
    return pltpu.CompilerParams(dimension_semantics=sem, vmem_limit_bytes=vmem, has_side_effects=side_effects)


def _sigmoid(x):
    return 1.0 / (1.0 + jnp.exp(-x))


class _Comm:
    def __init__(self, items, scope="devices"):
        self.items = items
        self.scope = scope
        self.slabs = slabs = N_DEV if scope == "devices" else N_DEV // 2
        self.n = n = len(items)
        self.operands = [arr for arr, _ in items]
        self.in_specs = [ANY] * n
        self.out_specs = [ANY] * n
        self.out_shape = [jax.ShapeDtypeStruct((slabs,) + tuple(arr.shape) if kind == "gather" else tuple(arr.shape),
                                               arr.dtype) for arr, kind in items]
        self.scratch = [pltpu.SemaphoreType.DMA((n, slabs - 1)), pltpu.SemaphoreType.DMA((n, slabs - 1)),
                        pltpu.SemaphoreType.DMA((n,))]

    def _places(self):
        pos = (lax.axis_index("x"), lax.axis_index("y"), lax.axis_index("c"))
        if self.scope == "devices":
            index = lambda p: 4 * p[0] + 2 * p[1] + p[2]
            masks = range(1, N_DEV)
        else:
            index = lambda p: 2 * p[0] + p[1]
            masks = (2, 4, 6)
        peers = []
        for k in masks:
            p = tuple(1 - pos[b] if (k >> (2 - b)) & 1 else pos[b] for b in range(3))
            peers.append((p, index(p)))
        return index(pos), peers

    def _copies(self, ins, outs, sems, landed):
        send_sems, recv_sems, local_sems = sems
        me, peers = self._places()
        local, remote = [], []
        for a, (_, kind) in enumerate(self.items):
            own = ins[a] if kind == "gather" else ins[a].at[me]
            local.append(pltpu.make_async_copy(own, outs[a].at[me], local_sems.at[a]))
            for k, (p, pid) in enumerate(peers):
                remote.append(pltpu.make_async_remote_copy(
                    src_ref=ins[a] if kind == "gather" else ins[a].at[pid],
                    dst_ref=outs[a].at[pid if landed else me],
                    send_sem=send_sems.at[a, k], recv_sem=recv_sems.at[a, k],
                    device_id=p, device_id_type=pl.DeviceIdType.MESH))
        return local, remote

    def start(self, ins, outs, sems):
        local, remote = self._copies(ins, outs, sems, landed=False)
        for cp in local + remote:
            cp.start()

    def wait(self, ins, outs, sems):
        local, remote = self._copies(ins, outs, sems, landed=True)
        for cp in remote + local:
            cp.wait()


def _exchange(items, name):
    comm = _Comm(items)
    n = comm.n

    def body(*refs):
        ins, outs, sems = refs[:n], refs[n:2 * n], refs[2 * n:]
        comm.start(ins, outs, sems)
        comm.wait(ins, outs, sems)

    return pl.pallas_call(
        body, name=name, in_specs=comm.in_specs, out_specs=comm.out_specs, out_shape=comm.out_shape,
        scratch_shapes=comm.scratch, compiler_params=pltpu.CompilerParams(has_side_effects=True),
    )(*comm.operands)


def _exchange_sibling(slabs, name):
    def body(g_ref, land_ref, send_sems, recv_sems):
        x, y, c = lax.axis_index("x"), lax.axis_index("y"), lax.axis_index("c")
        copies = [pltpu.make_async_remote_copy(
            src_ref=g_ref.at[2 * q + (1 - c)], dst_ref=land_ref.at[q], send_sem=send_sems.at[q], recv_sem=recv_sems.at[q],
            device_id=(x, y, 1 - c), device_id_type=pl.DeviceIdType.MESH) for q in range(N_DEV // 2)]
        for cp in copies:
            cp.start()
        for cp in copies:
            cp.wait()

    return pl.pallas_call(
        body, name=name, in_specs=[ANY], out_specs=ANY,
        out_shape=jax.ShapeDtypeStruct((N_DEV // 2,) + tuple(slabs.shape[1:]), slabs.dtype),
        scratch_shapes=[pltpu.SemaphoreType.DMA((N_DEV // 2,)), pltpu.SemaphoreType.DMA((N_DEV // 2,))],
        compiler_params=pltpu.CompilerParams(has_side_effects=True),
    )(slabs)


def _pair_sum(slabs, landed, name):
    _, rows, cols = slabs.shape
    tr = _tile(rows, (256, 128, 64, 16, 8))

    def body(g_ref, l_ref, o_ref):
        mine = g_ref[lax.axis_index("c")]
        o_ref[...] = (mine.astype(F32) + l_ref[...].astype(F32)).astype(o_ref.dtype)

    return pl.pallas_call(
        body, name=name, grid=(N_DEV // 2, rows // tr),
        in_specs=[pl.BlockSpec((None, 2, tr, cols), lambda q, i: (q, 0, i, 0)),
                  pl.BlockSpec((None, tr, cols), lambda q, i: (q, i, 0))],
        out_specs=pl.BlockSpec((None, tr, cols), lambda q, i: (q, i, 0)),
        out_shape=jax.ShapeDtypeStruct((N_DEV // 2, rows, cols), slabs.dtype),
        compiler_params=_params(("parallel", "parallel")),
    )(slabs.reshape(N_DEV // 2, 2, rows, cols), landed)


def _gather_two_level(arrays, name):
    n = len(arrays)

    def body(*refs):
        ins, outs = refs[:n], refs[n:2 * n]
        send_sems, recv_sems, local_sems = refs[2 * n:]
        x, y, c = lax.axis_index("x"), lax.axis_index("y"), lax.axis_index("c")
        me, sibling = (x, y, c), (x, y, 1 - c)
        chips = [(1 - x, y), (x, 1 - y), (1 - x, 1 - y)]

        def slab(a, place):
            return outs[a].at[4 * place[0] + 2 * place[1] + place[2]]

        def copy(a, k, block, to, src=None):
            return pltpu.make_async_remote_copy(
                src_ref=slab(a, block) if src is None else src, dst_ref=slab(a, block),
                send_sem=send_sems.at[a, k], recv_sem=recv_sems.at[a, k],
                device_id=to, device_id_type=pl.DeviceIdType.MESH)

        sends, mine = [], []
        for a in range(n):
            loc = pltpu.make_async_copy(ins[a], slab(a, me), local_sems.at[a])
            loc.start()
            mine.append(loc)
            sends.append(copy(a, 0, me, sibling, src=ins[a]))
            sends += [copy(a, 1 + j, me, (*chip, c), src=ins[a]) for j, chip in enumerate(chips)]
        for cp in sends:
            cp.start()
        for j, chip in enumerate(chips):
            for a in range(n):
                copy(a, 1 + j, (*chip, c), me).wait_recv()
                fwd = copy(a, 4 + j, (*chip, c), sibling)
                fwd.start()
                sends.append(fwd)
        for a in range(n):
            copy(a, 0, sibling, me).wait_recv()
            for j, chip in enumerate(chips):
                copy(a, 4 + j, (*chip, 1 - c), me).wait_recv()
        for cp in sends:
            cp.wait_send()
        for loc in mine:
            loc.wait()

    return pl.pallas_call(
        body, name=name, in_specs=[ANY] * n, out_specs=[ANY] * n,
        out_shape=[jax.ShapeDtypeStruct((N_DEV,) + tuple(a.shape), a.dtype) for a in arrays],
        scratch_shapes=[pltpu.SemaphoreType.DMA((n, N_DEV - 1)), pltpu.SemaphoreType.DMA((n, N_DEV - 1)),
                        pltpu.SemaphoreType.DMA((n,))],
        compiler_params=pltpu.CompilerParams(has_side_effects=True),
    )(*arrays)


def _matmul(a, b, *, tm, tn, tk, out_dtype, name, trans_a=False, trans_b=False, comm=None):
    m, k = (a.shape[1], a.shape[0]) if trans_a else a.shape
    n = b.shape[0] if trans_b else b.shape[1]
    nk = k // tk
    dims = TN if trans_a else (NT if trans_b else NN)
    assert not (trans_a and trans_b)
    nc = comm.n if comm else 0
    grid = (m // tm, n // tn, nk)

    def body(*refs):
        a_ref, b_ref = refs[:2]
        cin = refs[2:2 + nc]
        o_ref = refs[2 + nc]
        cout = refs[3 + nc:3 + 2 * nc]
        scratch = refs[3 + 2 * nc:]
        sems = scratch[len(scratch) - 3:] if comm else None
        i, j, kk = pl.program_id(0), pl.program_id(1), pl.program_id(2)
        if comm:
            @pl.when((i == 0) & (j == 0) & (kk == 0))
            def _():
                comm.start(cin, cout, sems)

        if nk == 1:
            o_ref[...] = _dot(a_ref[...], b_ref[...], dims).astype(out_dtype)
        else:
            acc_ref = scratch[0]

            @pl.when(kk == 0)
            def _():
                acc_ref[...] = jnp.zeros_like(acc_ref)

            acc_ref[...] += _dot(a_ref[...], b_ref[...], dims)

            @pl.when(kk == nk - 1)
            def _():
                o_ref[...] = acc_ref[...].astype(out_dtype)

        if comm:
            @pl.when((i == grid[0] - 1) & (j == grid[1] - 1) & (kk == nk - 1))
            def _():
                comm.wait(cin, cout, sems)

    a_spec = (pl.BlockSpec((tk, tm), lambda i, j, kk: (kk, i)) if trans_a
              else pl.BlockSpec((tm, tk), lambda i, j, kk: (i, kk)))
    b_spec = (pl.BlockSpec((tn, tk), lambda i, j, kk: (j, kk)) if trans_b
              else pl.BlockSpec((tk, tn), lambda i, j, kk: (kk, j)))
    sem = ("arbitrary",) * 3 if comm else ("parallel", "parallel", "arbitrary")
    res = pl.pallas_call(
        body, name=name, grid=grid,
        in_specs=[a_spec, b_spec] + (comm.in_specs if comm else []),
        out_specs=[pl.BlockSpec((tm, tn), lambda i, j, kk: (i, j))] + (comm.out_specs if comm else []),
        out_shape=[jax.ShapeDtypeStruct((m, n), out_dtype)] + (comm.out_shape if comm else []),
        scratch_shapes=([] if nk == 1 else [pltpu.VMEM((tm, tn), F32)]) + (comm.scratch if comm else []),
        compiler_params=_params(sem, side_effects=bool(comm)),
    )(a, b, *(comm.operands if comm else []))
    return res if comm else res[0]


def _h_chunk(i, nx, h0_ref, x_ref):
    h = jnp.where(i == 0, h0_ref[...], x_ref[...])
    return h * (i <= nx).astype(F32)


def _x_spec(nx):
    return pl.BlockSpec((CHUNK, D_MODEL), lambda i: (jnp.clip(i - 1, 0, nx - 1), 0))


def _prenorm(x2d, h0, w):
    nx = x2d.shape[0] // CHUNK
    nc = nx + 2

    def body(x_ref, h0_ref, w_ref, o_ref):
        h = _h_chunk(pl.program_id(0), nx, h0_ref, x_ref)
        rstd = lax.rsqrt(jnp.mean(h * h, axis=-1, keepdims=True) + EPS)
        o_ref[...] = (h * rstd * w_ref[...]).astype(BF16)

    return pl.pallas_call(
        body, name="prenorm", grid=(nc,),
        in_specs=[_x_spec(nx), pl.BlockSpec((CHUNK, D_MODEL), lambda i: (0, 0)), pl.BlockSpec((1, D_MODEL), lambda i: (0, 0))],
        out_specs=pl.BlockSpec((CHUNK, D_MODEL), lambda i: (i, 0)),
        out_shape=jax.ShapeDtypeStruct((nc * CHUNK, D_MODEL), BF16),
        compiler_params=_params(("parallel",)),
    )(x2d, h0, w)


CONV_COLS = 512
HALO = 8


def _conv_pre(ext, w, b):
    acc = b + w[3:4, :] * ext[HALO:, :]
    for j in range(1, CONV_WIDTH):
        acc = acc + w[3 - j:4 - j, :] * pltpu.roll(ext, j, 0)[HALO:, :]
    return acc


def _conv_fwd(proj, conv_w, conv_b):
    lp = proj.shape[0]
    t = _tile(lp, (704, 384, 128, 64))
    hb = t // HALO
    c0 = OFF_XBC // CONV_COLS

    def body(u_ref, halo_ref, w_ref, b_ref, o_ref):
        i = pl.program_id(0)
        halo = halo_ref[...] * (i > 0).astype(F32)
        pre = _conv_pre(jnp.concatenate([halo, u_ref[...]], axis=0), w_ref[...], b_ref[...])
        o_ref[...] = pre * _sigmoid(pre)

    return pl.pallas_call(
        body, name="conv_fwd", grid=(lp // t, D_CONV // CONV_COLS),
        in_specs=[
            pl.BlockSpec((t, CONV_COLS), lambda i, j: (i, c0 + j)),
            pl.BlockSpec((HALO, CONV_COLS), lambda i, j: (jnp.maximum(i * hb - 1, 0), c0 + j)),
            pl.BlockSpec((CONV_WIDTH, CONV_COLS), lambda i, j: (0, j)),
            pl.BlockSpec((1, CONV_COLS), lambda i, j: (0, j)),
        ],
        out_specs=pl.BlockSpec((t, CONV_COLS), lambda i, j: (i, j)),
        out_shape=jax.ShapeDtypeStruct((lp, D_CONV), F32),
        compiler_params=_params(("parallel", "parallel")),
    )(proj, proj, conv_w, conv_b)


def _conv_bwd(dxbc, proj, conv_w, conv_b, dproj):
    lp = proj.shape[0]
    t = _tile(lp, (704, 384, 128, 64))
    hb = t // HALO
    nt = lp // t
    c0 = OFF_XBC // CONV_COLS

    def body(dx_ref, dxn_ref, u_ref, up_ref, un_ref, w_ref, b_ref, _, du_ref, dw_ref, db_ref):
        i = pl.program_id(1)
        w = w_ref[...]
        up = up_ref[...] * (i > 0).astype(F32)
        ext = jnp.concatenate([up, u_ref[...], un_ref[...]], axis=0)
        pre = _conv_pre(ext, w, b_ref[...])
        dxn = dxn_ref[...] * (i < nt - 1).astype(F32)
        dxe = jnp.concatenate([dx_ref[...], dxn], axis=0)
        sg = _sigmoid(pre)
        dpre = dxe * sg * (1.0 + pre * (1.0 - sg))
        du = w[3:4, :] * dpre[:t, :]
        for j in range(1, CONV_WIDTH):
            du = du + w[3 - j:4 - j, :] * pltpu.roll(dpre, t + HALO - j, 0)[:t, :]
        du_ref[...] = du.astype(BF16)

        @pl.when(i == 0)
        def _():
            dw_ref[...] = jnp.zeros_like(dw_ref)
            db_ref[...] = jnp.zeros_like(db_ref)

        dp = dpre[:t, :]
        db_ref[...] += jnp.sum(dp, axis=0, keepdims=True)
        hist = ext[:HALO + t, :]
        dw_ref[3:4, :] += jnp.sum(dp * hist[HALO:, :], axis=0, keepdims=True)
        for j in range(1, CONV_WIDTH):
            dw_ref[3 - j:4 - j, :] += jnp.sum(dp * pltpu.roll(hist, j, 0)[HALO:, :], axis=0, keepdims=True)

    nxt = lambda i: jnp.minimum((i + 1) * hb, lp // HALO - 1)
    return pl.pallas_call(
        body, name="conv_bwd", grid=(D_CONV // CONV_COLS, nt),
        in_specs=[
            pl.BlockSpec((t, CONV_COLS), lambda j, i: (i, j)),
            pl.BlockSpec((HALO, CONV_COLS), lambda j, i: (nxt(i), j)),
            pl.BlockSpec((t, CONV_COLS), lambda j, i: (i, c0 + j)),
            pl.BlockSpec((HALO, CONV_COLS), lambda j, i: (jnp.maximum(i * hb - 1, 0), c0 + j)),
            pl.BlockSpec((HALO, CONV_COLS), lambda j, i: (nxt(i), c0 + j)),
            pl.BlockSpec((CONV_WIDTH, CONV_COLS), lambda j, i: (0, j)),
            pl.BlockSpec((1, CONV_COLS), lambda j, i: (0, j)),
            ANY,
        ],
        out_specs=[
            pl.BlockSpec((t, CONV_COLS), lambda j, i: (i, c0 + j)),
            pl.BlockSpec((CONV_WIDTH, CONV_COLS), lambda j, i: (0, j)),
            pl.BlockSpec((1, CONV_COLS), lambda j, i: (0, j)),
        ],
        out_shape=[
            jax.ShapeDtypeStruct((lp, NP), BF16),
            jax.ShapeDtypeStruct((CONV_WIDTH, D_CONV), F32),
            jax.ShapeDtypeStruct((1, D_CONV), F32),
        ],
        input_output_aliases={7: 0},
        compiler_params=_params(("parallel", "arbitrary")),
    )(dxbc, dxbc, proj, proj, proj, conv_w, conv_b, dproj)


def _swap_halves(t):
    w = t.shape[-1]
    lane = lax.broadcasted_iota(jnp.int32, t.shape, 1)
    return jnp.where((lane % HEAD_DIM) < HEAD_DIM // 2, pltpu.roll(t, w - HEAD_DIM // 2, 1),
                     pltpu.roll(t, HEAD_DIM // 2, 1))


def _act_fwd(proj, cos_t, sin_t, expand, dt_bias_rep):
    lp = proj.shape[0]
    t = _tile(lp, (384, 128, 64))

    def body(q_ref, k_ref, v_ref, dt_ref, cos_ref, sin_ref, ex_ref, bias_ref, qo_ref, ko_ref, vo_ref, dto_ref):
        i = pl.program_id(0)
        cos = cos_ref[...]
        sin = sin_ref[...]
        q = q_ref[...]
        qo_ref[...] = (q * jnp.tile(cos, (1, D_ATT // 128)) + _swap_halves(q) * jnp.tile(sin, (1, D_ATT // 128))).astype(BF16)
        k = k_ref[...]
        ko_ref[...] = (k * jnp.tile(cos, (1, D_KV // 128)) + _swap_halves(k) * jnp.tile(sin, (1, D_KV // 128))).astype(BF16)
        vo_ref[...] = v_ref[...].astype(BF16)
        raw = _dot(dt_ref[...], ex_ref[...], NN, HIGHEST) + bias_ref[...]
        sp = jnp.maximum(raw, 0.0) + jnp.log1p(jnp.exp(-jnp.abs(raw)))
        row = i * t + lax.broadcasted_iota(jnp.int32, sp.shape, 0)
        dto_ref[...] = jnp.where(row >= PAD_LEAD, sp, 0.0)

    return pl.pallas_call(
        body, name="act_fwd", grid=(lp // t,),
        in_specs=[
            pl.BlockSpec((t, D_ATT), lambda i: (i, OFF_Q // D_ATT)),
            pl.BlockSpec((t, D_KV), lambda i: (i, OFF_K // D_KV)),
            pl.BlockSpec((t, D_KV), lambda i: (i, OFF_V // D_KV)),
            pl.BlockSpec((t, 128), lambda i: (i, OFF_DT // 128)),
            pl.BlockSpec((t, 128), lambda i: (i, 0)),
            pl.BlockSpec((t, 128), lambda i: (i, 0)),
            pl.BlockSpec((128, D_SSD), lambda i: (0, 0)),
            pl.BlockSpec((1, D_SSD), lambda i: (0, 0)),
        ],
        out_specs=[
            pl.BlockSpec((t, D_ATT), lambda i: (i, 0)),
            pl.BlockSpec((t, D_KV), lambda i: (i, 0)),
            pl.BlockSpec((t, D_KV), lambda i: (i, 0)),
            pl.BlockSpec((t, D_SSD), lambda i: (i, 0)),
        ],
        out_shape=[
            jax.ShapeDtypeStruct((lp, D_ATT), BF16),
            jax.ShapeDtypeStruct((lp, D_KV), BF16),
            jax.ShapeDtypeStruct((lp, D_KV), BF16),
            jax.ShapeDtypeStruct((lp, D_SSD), F32),
        ],
        compiler_params=_params(("parallel",)),
    )(proj, proj, proj, proj, cos_t, sin_t, expand, dt_bias_rep)


def _act_bwd(dqr, dkr, dv, dg, ddt_part, cos_t, sin_t, reduce_t, dproj):
    lp = dqr.shape[0]
    t = _tile(lp, (384, 128, 64))

    def body(dq_ref, dk_ref, dv_ref, dg_ref, ddt_ref, cos_ref, sin_ref, red_ref, _, o_ref, db_ref):
        i = pl.program_id(0)
        cos = cos_ref[...]
        sin = sin_ref[...]
        dq = dq_ref[...]
        dq = dq * jnp.tile(cos, (1, D_ATT // 128)) + _swap_halves(dq * jnp.tile(sin, (1, D_ATT // 128)))
        dk = dk_ref[...]
        dk = dk * jnp.tile(cos, (1, D_KV // 128)) + _swap_halves(dk * jnp.tile(sin, (1, D_KV // 128)))
        ddt = _dot(ddt_ref[...], red_ref[...], NN, HIGHEST)
        o_ref[...] = jnp.concatenate(
            [dq.astype(BF16), dg_ref[...].astype(BF16), dk.astype(BF16), dv_ref[...].astype(BF16), ddt.astype(BF16),
             jnp.zeros((t, NP - OFF_DT - 128), BF16)], axis=1)

        @pl.when(i == 0)
        def _():
            db_ref[...] = jnp.zeros_like(db_ref)

        db_ref[...] += jnp.sum(ddt, axis=0, keepdims=True)

    return pl.pallas_call(
        body, name="act_bwd", grid=(lp // t,),
        in_specs=[
            pl.BlockSpec((t, D_ATT), lambda i: (i, 0)),
            pl.BlockSpec((t, D_KV), lambda i: (i, 0)),
            pl.BlockSpec((t, D_KV), lambda i: (i, 0)),
            pl.BlockSpec((t, D_ATT), lambda i: (i, 0)),
            pl.BlockSpec((t, D_SSD), lambda i: (i, 0)),
            pl.BlockSpec((t, 128), lambda i: (i, 0)),
            pl.BlockSpec((t, 128), lambda i: (i, 0)),
            pl.BlockSpec((D_SSD, 128), lambda i: (0, 0)),
            ANY,
        ],
        out_specs=[pl.BlockSpec((t, TAIL_W), lambda i: (i, OFF_Q // TAIL_W)), pl.BlockSpec((1, 128), lambda i: (0, 0))],
        out_shape=[jax.ShapeDtypeStruct((lp, NP), BF16), jax.ShapeDtypeStruct((1, 128), F32)],
        input_output_aliases={8: 0},
        compiler_params=_params(("arbitrary",)),
    )(dqr, dkr, dv, dg, ddt_part, cos_t, sin_t, reduce_t, dproj)


def _ssd_common(xbc, dt, a, tri, sel):
    xs = xbc[:, :GROUP_W]
    b = xbc[:, GROUP_W:GROUP_W + D_STATE]
    c = xbc[:, GROUP_W + D_STATE:]
    cs = _dot(tri, dt * a, NN, HIGHEST)
    cs_t = _dot(sel, cs, NT, HIGHEST)
    xdt = xs * dt
    cs_last = cs[CHUNK - 1:CHUNK, :]
    return xs, b, c, cs, cs_t, xdt, jnp.exp(cs), jnp.exp(cs_last - cs), jnp.exp(cs_last)


def _decay_matrix(cs, cs_t, r, causal):
    seg = cs[:, r * HEAD_DIM:r * HEAD_DIM + 1] - cs_t[r:r + 1, :]
    return jnp.exp(jnp.where(causal, seg, -jnp.inf))


def _causal_mask():
    row = lax.broadcasted_iota(jnp.int32, (CHUNK, CHUNK), 0)
    col = lax.broadcasted_iota(jnp.int32, (CHUNK, CHUNK), 1)
    return row >= col


def _ssd_fwd(xbc, dt_rep, proj, a_rep, dsk_rep, wn, tri, sel):
    lp = xbc.shape[0]
    nc = lp // CHUNK
    gw = SSD_GPS * GROUP_W

    def body(xbc_ref, dt_ref, z_ref, a_ref, dsk_ref, wn_ref, tri_ref, sel_ref, yn_ref, ytot_ref, hprev_ref, h_scr):
        @pl.when(pl.program_id(1) == 0)
        def _():
            h_scr[...] = jnp.zeros_like(h_scr)

        causal = _causal_mask()
        G = range(SSD_GPS)
        colsl = [slice(gi * GROUP_W, (gi + 1) * GROUP_W) for gi in G]
        xbc = [xbc_ref[:, gi * XBC_BLK:(gi + 1) * XBC_BLK] for gi in G]
        dt = [dt_ref[:, colsl[gi]] for gi in G]
        xs = [xbc[gi][:, :GROUP_W] for gi in G]
        b = [xbc[gi][:, GROUP_W:GROUP_W + D_STATE].astype(BF16) for gi in G]
        c = [xbc[gi][:, GROUP_W + D_STATE:].astype(BF16) for gi in G]
        hprev = [h_scr[gi] for gi in G]
        cs = [_dot(tri_ref[...], dt[gi] * a_ref[:, colsl[gi]], NN, HIGHEST) for gi in G]
        cb = [_dot(c[gi], b[gi], NT) for gi in G]
        yoff = [_dot(c[gi], hprev[gi].astype(BF16)) for gi in G]
        cs_t = [_dot(sel_ref[...], cs[gi], NT, HIGHEST) for gi in G]
        xdt = [xs[gi] * dt[gi] for gi in G]
        cs_last = [cs[gi][CHUNK - 1:CHUNK, :] for gi in G]
        st = [_dot(b[gi], (xdt[gi] * jnp.exp(cs_last[gi] - cs[gi])).astype(BF16), TN) for gi in G]
        m = [[(cb[gi] * _decay_matrix(cs[gi], cs_t[gi], r, causal)).astype(BF16) for r in range(HEADS_PER_GROUP)] for gi in G]
        ydiag = [[_dot(m[gi][r], xdt[gi][:, r * HEAD_DIM:(r + 1) * HEAD_DIM].astype(BF16)) for r in range(HEADS_PER_GROUP)]
                 for gi in G]
        for gi in G:
            cols = colsl[gi]
            ytot = jnp.concatenate(ydiag[gi], axis=1) + yoff[gi] * jnp.exp(cs[gi]) + dsk_ref[:, cols] * xs[gi]
            z = z_ref[:, cols]
            gz = ytot * (z * _sigmoid(z))
            rstd = lax.rsqrt(jnp.mean(gz * gz, axis=-1, keepdims=True) + EPS)
            hprev_ref[gi] = hprev[gi]
            h_scr[gi] = hprev[gi] * jnp.exp(cs_last[gi]) + st[gi]
            ytot_ref[:, cols] = ytot
            yn_ref[:, cols] = (gz * rstd * wn_ref[:, cols]).astype(BF16)

    vec = pl.BlockSpec((1, gw), lambda g, c: (0, g))
    blk = pl.BlockSpec((CHUNK, gw), lambda g, c: (c, g))
    return pl.pallas_call(
        body, name="ssd_fwd", grid=(N_GROUPS // SSD_GPS, nc),
        in_specs=[
            pl.BlockSpec((CHUNK, SSD_GPS * XBC_BLK), lambda g, c: (c, g)),
            blk, blk, vec, vec, vec,
            pl.BlockSpec((CHUNK, CHUNK), lambda g, c: (0, 0)),
            pl.BlockSpec((8, GROUP_W), lambda g, c: (0, 0)),
        ],
        out_specs=[blk, blk, pl.BlockSpec((None, SSD_GPS, D_STATE, GROUP_W), lambda g, c: (c, g, 0, 0))],
        out_shape=[
            jax.ShapeDtypeStruct((lp, D_MIX), BF16),
            jax.ShapeDtypeStruct((lp, D_SSD), F32),
            jax.ShapeDtypeStruct((nc, N_GROUPS, D_STATE, GROUP_W), F32),
        ],
        scratch_shapes=[pltpu.VMEM((SSD_GPS, D_STATE, GROUP_W), F32)],
        compiler_params=_params(("parallel", "arbitrary")),
    )(xbc, dt_rep, proj, a_rep, dsk_rep, wn, tri, sel)


def _ssd_bwd(dmix, ytot, proj, xbc, dt_rep, hprev, a_rep, dsk_rep, wn, tri, sel, comm):
    lp = xbc.shape[0]
    nc = lp // CHUNK
    gw = SSD_GPS * GROUP_W
    ncm = comm.n
    n_in, n_out = 11, 6
    grid = (N_GROUPS // SSD_GPS, nc)

    def all_groups(refs, causal):
        (dyn_ref, ytot_ref, z_ref, xbc_ref, dt_ref, hprev_ref, a_ref, dsk_ref, wn_ref, tri_ref, sel_ref,
         dz_ref, dxbc_ref, ddt_ref, dd_ref, da_ref, dwn_ref, dh_scr) = refs
        G = range(SSD_GPS)
        H = range(HEADS_PER_GROUP)
        cl = [slice(gi * GROUP_W, (gi + 1) * GROUP_W) for gi in G]
        hl = [slice(r * HEAD_DIM, (r + 1) * HEAD_DIM) for r in H]
        tri = tri_ref[...]
        sel = sel_ref[...]
        xbc = [xbc_ref[:, gi * XBC_BLK:(gi + 1) * XBC_BLK] for gi in G]
        dt = [dt_ref[:, cl[gi]] for gi in G]
        a = [a_ref[:, cl[gi]] for gi in G]
        xs = [xbc[gi][:, :GROUP_W] for gi in G]
        bbf = [xbc[gi][:, GROUP_W:GROUP_W + D_STATE].astype(BF16) for gi in G]
        cbf = [xbc[gi][:, GROUP_W + D_STATE:].astype(BF16) for gi in G]
        hprev = [hprev_ref[gi] for gi in G]
        hbf = [hprev[gi].astype(BF16) for gi in G]
        dhn = [dh_scr[gi] for gi in G]
        dhnb = [dhn[gi].astype(BF16) for gi in G]
        cs = [_dot(tri, dt[gi] * a[gi], NN, HIGHEST) for gi in G]
        cb = [_dot(cbf[gi], bbf[gi], NT) for gi in G]
        g = [_dot(cbf[gi], hbf[gi]) for gi in G]
        dxw = [_dot(bbf[gi], dhnb[gi]) for gi in G]
        cs_t = [_dot(sel, cs[gi], NT, HIGHEST) for gi in G]
        dy = []
        for gi in G:
            ytot = ytot_ref[:, cl[gi]]
            z = z_ref[:, cl[gi]]
            dyn = dyn_ref[:, cl[gi]]
            sz = _sigmoid(z)
            silu_z = z * sz
            gz = ytot * silu_z
            rstd = lax.rsqrt(jnp.mean(gz * gz, axis=-1, keepdims=True) + EPS)
            xhat = gz * rstd
            dwn_ref[:, cl[gi]] += jnp.sum(dyn * xhat, axis=0, keepdims=True)
            dxhat = dyn * wn_ref[:, cl[gi]]
            dgz = rstd * (dxhat - xhat * jnp.mean(dxhat * xhat, axis=-1, keepdims=True))
            dy.append(dgz * silu_z)
            dz_ref[:, cl[gi]] = (dgz * ytot * (sz * (1.0 + z * (1.0 - sz)))).astype(BF16)
            dd_ref[:, cl[gi]] += jnp.sum(dy[gi] * xs[gi], axis=0, keepdims=True)
        xdt = [xs[gi] * dt[gi] for gi in G]
        e = [jnp.exp(cs[gi]) for gi in G]
        cs_last = [cs[gi][CHUNK - 1:CHUNK, :] for gi in G]
        dte = [jnp.exp(cs_last[gi] - cs[gi]) for gi in G]
        cd = [jnp.exp(cs_last[gi]) for gi in G]
        dgb = [(dy[gi] * e[gi]).astype(BF16) for gi in G]
        dyb = [dy[gi].astype(BF16) for gi in G]
        xdtb = [xdt[gi].astype(BF16) for gi in G]
        dc = [_dot(dgb[gi], hbf[gi], NT) for gi in G]
        dhprev = [_dot(cbf[gi], dgb[gi], TN) for gi in G]
        db = [_dot((xdt[gi] * dte[gi]).astype(BF16), dhnb[gi], NT) for gi in G]
        lm = [[_decay_matrix(cs[gi], cs_t[gi], r, causal) for r in H] for gi in G]
        m = [[cb[gi] * lm[gi][r] for r in H] for gi in G]
        dm = [[_dot(dyb[gi][:, hl[r]], xdtb[gi][:, hl[r]], NT) for r in H] for gi in G]
        dxdt_diag = [[_dot(m[gi][r].astype(BF16), dyb[gi][:, hl[r]], TN) for r in H] for gi in G]
        dseg = [[dm[gi][r] * m[gi][r] for r in H] for gi in G]
        ones = jnp.ones((CHUNK, HEAD_DIM), F32)
        colsum = [[_dot(dseg[gi][r], ones, TN, HIGHEST) for r in H] for gi in G]
        dcbb = []
        for gi in G:
            dcb = dm[gi][0] * lm[gi][0]
            for r in range(1, HEADS_PER_GROUP):
                dcb = dcb + dm[gi][r] * lm[gi][r]
            dcbb.append(dcb.astype(BF16))
        dc2 = [_dot(dcbb[gi], bbf[gi]) for gi in G]
        db2 = [_dot(dcbb[gi], cbf[gi], TN) for gi in G]
        dcs = []
        for gi in G:
            t_dte = dxw[gi] * xdt[gi] * dte[gi]
            dcs_last = (jnp.sum(dhn[gi] * hprev[gi], axis=0, keepdims=True) * cd[gi]
                        + jnp.sum(t_dte, axis=0, keepdims=True))
            diag = jnp.concatenate(
                [(jnp.sum(dseg[gi][r], axis=1, keepdims=True) - colsum[gi][r]) * (1.0 / HEAD_DIM) for r in H], axis=1)
            d = dy[gi] * g[gi] * e[gi] - t_dte + diag
            row = lax.broadcasted_iota(jnp.int32, d.shape, 0)
            dcs.append(d + jnp.where(row == CHUNK - 1, dcs_last, 0.0))
        dda = [_dot(tri, dcs[gi], TN, HIGHEST) for gi in G]
        for gi in G:
            dxdt = dxw[gi] * dte[gi] + jnp.concatenate(dxdt_diag[gi], axis=1)
            da_ref[:, cl[gi]] += jnp.sum(dda[gi] * dt[gi], axis=0, keepdims=True)
            ddt = dda[gi] * a[gi] + dxdt * xs[gi]
            dxs = dsk_ref[:, cl[gi]] * dy[gi] + dxdt * dt[gi]
            ddt_ref[:, cl[gi]] = ddt * (1.0 - jnp.exp(-dt[gi]))
            dxbc_ref[:, gi * XBC_BLK:(gi + 1) * XBC_BLK] = jnp.concatenate(
                [dxs, db[gi] + db2[gi], dc[gi] + dc2[gi]], axis=1)
            dh_scr[gi] = dhprev[gi] + dhn[gi] * cd[gi]

    def body(*refs):
        ins = refs[:n_in]
        cin = refs[n_in:n_in + ncm]
        outs = refs[n_in + ncm:n_in + ncm + n_out]
        cout = refs[n_in + ncm + n_out:n_in + 2 * ncm + n_out]
        dh_scr = refs[n_in + 2 * ncm + n_out]
        sems = refs[n_in + 2 * ncm + n_out + 1:]
        g, c = pl.program_id(0), pl.program_id(1)

        @pl.when((g == 0) & (c == 0))
        def _():
            comm.start(cin, cout, sems)

        @pl.when(c == 0)
        def _():
            dh_scr[...] = jnp.zeros_like(dh_scr)
            for ref in outs[3:]:
                ref[...] = jnp.zeros_like(ref)

        all_groups(ins + outs + (dh_scr,), _causal_mask())

        @pl.when((g == grid[0] - 1) & (c == nc - 1))
        def _():
            comm.wait(cin, cout, sems)

    rev = lambda c: nc - 1 - c
    vec = pl.BlockSpec((1, gw), lambda g, c: (0, g))
    blk = pl.BlockSpec((CHUNK, gw), lambda g, c: (rev(c), g))
    xblk = pl.BlockSpec((CHUNK, SSD_GPS * XBC_BLK), lambda g, c: (rev(c), g))
    res = pl.pallas_call(
        body, name="ssd_bwd", grid=grid,
        in_specs=[blk, blk, blk, xblk, blk,
                  pl.BlockSpec((None, SSD_GPS, D_STATE, GROUP_W), lambda g, c: (rev(c), g, 0, 0)),
                  vec, vec, vec,
                  pl.BlockSpec((CHUNK, CHUNK), lambda g, c: (0, 0)),
                  pl.BlockSpec((8, GROUP_W), lambda g, c: (0, 0))] + comm.in_specs,
        out_specs=[blk, xblk, blk, vec, vec, vec] + comm.out_specs,
        out_shape=[
            jax.ShapeDtypeStruct((lp, NP), BF16),
            jax.ShapeDtypeStruct((lp, D_CONV), F32),
            jax.ShapeDtypeStruct((lp, D_SSD), F32),
            jax.ShapeDtypeStruct((1, D_SSD), F32),
            jax.ShapeDtypeStruct((1, D_SSD), F32),
            jax.ShapeDtypeStruct((1, D_SSD), F32),
        ] + comm.out_shape,
        scratch_shapes=[pltpu.VMEM((SSD_GPS, D_STATE, GROUP_W), F32)] + comm.scratch,
        compiler_params=_params(("arbitrary", "arbitrary"), side_effects=True),
    )(dmix, ytot, proj, xbc, dt_rep, hprev, a_rep, dsk_rep, wn, tri, sel, *comm.operands)
    return res


def _stack_heads(t, h):
    return jnp.concatenate([t[:, (REP * h + r) * HEAD_DIM:(REP * h + r + 1) * HEAD_DIM] for r in range(REP)], axis=0)


def _band(t2, t1, t0, h):
    sl = slice(h * HEAD_DIM, (h + 1) * HEAD_DIM)
    return jnp.concatenate([t2[:, sl], t1[:, sl], t0[:, sl]], axis=0)


def _attn_probs(s, sink, qc):
    s = s * (HEAD_DIM ** -0.5)
    key_abs = (qc - WINDOW_CHUNKS) * CHUNK + lax.broadcasted_iota(jnp.int32, s.shape, 1)
    s = jnp.where(key_abs >= PAD_LEAD, s, -jnp.inf)
    m = jnp.maximum(jnp.max(s, axis=-1, keepdims=True), sink)
    p = jnp.exp(s - m)
    ps = jnp.exp(sink - m)
    denom = jnp.sum(p, axis=-1, keepdims=True) + ps
    return p / denom, ps / denom


ATT_QC = 2
ATT_KC = WINDOW_CHUNKS + ATT_QC


def _kv_specs(width, newest_chunk_of):
    return [pl.BlockSpec((CHUNK, width), functools.partial(lambda j, p: (jnp.maximum(newest_chunk_of(p) - j, 0), 0), j))
            for j in range(ATT_KC - 1, -1, -1)]


def _attn_fwd(qr, kr, vb, proj, sink_stack, mix):
    lp = qr.shape[0]
    nc = lp // CHUNK

    assert nc % ATT_QC == 0
    qrows = ATT_QC * CHUNK

    def body(q_ref, *rest):
        k_refs, v_refs = rest[:ATT_KC], rest[ATT_KC:2 * ATT_KC]
        g_ref, sink_ref, _, att_ref, mix_ref = rest[2 * ATT_KC:]
        p = pl.program_id(0)
        q = q_ref[...]
        ks = [r[...] for r in k_refs]
        vs = [r[...] for r in v_refs]
        units = [(u, h) for u in range(ATT_QC) for h in range(KV_HEADS)]
        s = [_dot(_stack_heads(q[u * CHUNK:(u + 1) * CHUNK, :], h), _band(*ks[u:u + 3], h), NT) for u, h in units]
        vbh = [_band(*vs[u:u + 3], h) for u, h in units]
        pn = [_attn_probs(s[i], sink_ref[h], ATT_QC * p + u)[0].astype(BF16) for i, (u, h) in enumerate(units)]
        o = [_dot(pn[i], vbh[i]) for i in range(len(units))]
        att = jnp.concatenate(
            [jnp.concatenate([o[u * KV_HEADS + h][r * CHUNK:(r + 1) * CHUNK, :] for h in range(KV_HEADS) for r in range(REP)],
                             axis=1) for u in range(ATT_QC)], axis=0)
        att_ref[...] = att
        g = g_ref[...]
        mix_ref[...] = (att * (g * _sigmoid(g))).astype(BF16)

    newest = lambda p: ATT_QC * p + ATT_QC - 1
    return pl.pallas_call(
        body, name="attn_fwd", grid=(nc // ATT_QC,),
        in_specs=[pl.BlockSpec((qrows, D_ATT), lambda p: (p, 0))] + _kv_specs(D_KV, newest) + _kv_specs(D_KV, newest) + [
            pl.BlockSpec((qrows, D_ATT), lambda p: (p, OFF_G // D_ATT)),
            pl.BlockSpec((KV_HEADS, REP * CHUNK, 1), lambda p: (0, 0, 0)),
            ANY,
        ],
        out_specs=[pl.BlockSpec((qrows, D_ATT), lambda p: (p, 0)),
                   pl.BlockSpec((qrows, D_ATT), lambda p: (p, D_SSD // D_ATT))],
        out_shape=[jax.ShapeDtypeStruct((lp, D_ATT), F32), jax.ShapeDtypeStruct((lp, D_MIX), BF16)],
        input_output_aliases={2 * ATT_KC + 3: 1},
        compiler_params=_params(("parallel",)),
    )(qr, *([kr] * ATT_KC), *([vb] * ATT_KC), proj, sink_stack, mix)


def _attn_bwd(qr, kr, vb, att, proj, dmix, sink_stack):
    lp = qr.shape[0]
    nc = lp // CHUNK
    assert nc % ATT_QC == 0
    npairs = nc // ATT_QC
    qrows = ATT_QC * CHUNK
    wrows = ATT_KC * CHUNK

    def body(q_ref, *rest):
        k_refs, v_refs = rest[:ATT_KC], rest[ATT_KC:2 * ATT_KC]
        (att_ref, g_ref, do_ref, sink_ref, dq_ref, dk_ref, dv_ref, dg_ref, dsink_ref, dk_acc, dv_acc) = rest[2 * ATT_KC:]
        step = pl.program_id(0)

        @pl.when(step == 0)
        def _():
            dk_acc[...] = jnp.zeros_like(dk_acc)
            dv_acc[...] = jnp.zeros_like(dv_acc)
            dsink_ref[...] = jnp.zeros_like(dsink_ref)

        @pl.when(step < npairs)
        def _():
            q = q_ref[...]
            ks = [r[...] for r in k_refs]
            vs = [r[...] for r in v_refs]
            att = att_ref[...]
            g = g_ref[...]
            dog = do_ref[...]
            sg = _sigmoid(g)
            dg_ref[...] = dog * att * (sg * (1.0 + g * (1.0 - sg)))
            do = dog * (g * sg)
            units = [(u, h) for u in range(ATT_QC) for h in range(KV_HEADS)]
            n = range(len(units))
            rows = [slice(u * CHUNK, (u + 1) * CHUNK) for u in range(ATT_QC)]
            qs = [_stack_heads(q[rows[u], :], h) for u, h in units]
            kb = [_band(*ks[u:u + 3], h) for u, h in units]
            vbh = [_band(*vs[u:u + 3], h) for u, h in units]
            dos = [_stack_heads(do[rows[u], :], h) for u, h in units]
            dosb = [dos[i].astype(BF16) for i in n]
            s = [_dot(qs[i], kb[i], NT) for i in n]
            dp = [_dot(dosb[i], vbh[i], NT) for i in n]
            ds, pnb = [], []
            for i, (u, h) in enumerate(units):
                pn, psink = _attn_probs(s[i], sink_ref[h], ATT_QC * step + u)
                delta = jnp.sum(dos[i] * _stack_heads(att[rows[u], :], h), axis=-1, keepdims=True)
                ds.append((pn * (dp[i] - delta)).astype(BF16))
                pnb.append(pn.astype(BF16))
                dsink_ref[h] += -(psink * delta)
            dqs = [_dot(ds[i], kb[i]) for i in n]
            dks = [_dot(ds[i], qs[i], TN) for i in n]
            dvs = [_dot(pnb[i], dosb[i], TN) for i in n]
            dq_ref[...] = jnp.concatenate(
                [jnp.concatenate([dqs[u * KV_HEADS + h][r * CHUNK:(r + 1) * CHUNK, :]
                                  for h in range(KV_HEADS) for r in range(REP)], axis=1) for u in range(ATT_QC)],
                axis=0) * (HEAD_DIM ** -0.5)
            for u in range(ATT_QC):
                band = slice(u * CHUNK, u * CHUNK + BAND)
                dk_acc[band, :] += jnp.concatenate(dks[u * KV_HEADS:(u + 1) * KV_HEADS], axis=1) * (HEAD_DIM ** -0.5)
                dv_acc[band, :] += jnp.concatenate(dvs[u * KV_HEADS:(u + 1) * KV_HEADS], axis=1)

        dk_ref[...] = dk_acc[0:qrows, :]
        dv_ref[...] = dv_acc[0:qrows, :]
        for acc in (dk_acc, dv_acc):
            rest_rows = acc[qrows:wrows, :]
            acc[0:wrows - qrows, :] = rest_rows
            acc[wrows - qrows:wrows, :] = jnp.zeros((qrows, D_KV), F32)

    qp = lambda p: jnp.minimum(p, npairs - 1)
    newest = lambda p: ATT_QC * qp(p) + ATT_QC - 1
    qblk = pl.BlockSpec((qrows, D_ATT), lambda p: (qp(p), 0))
    oldest = pl.BlockSpec((qrows, D_KV), lambda p: (jnp.maximum(p - 1, 0), 0))
    return pl.pallas_call(
        body, name="attn_bwd", grid=(npairs + 1,),
        in_specs=[qblk] + _kv_specs(D_KV, newest) + _kv_specs(D_KV, newest) + [
            qblk,
            pl.BlockSpec((qrows, D_ATT), lambda p: (qp(p), OFF_G // D_ATT)),
            pl.BlockSpec((qrows, D_ATT), lambda p: (qp(p), D_SSD // D_ATT)),
            pl.BlockSpec((KV_HEADS, REP * CHUNK, 1), lambda p: (0, 0, 0)),
        ],
        out_specs=[qblk, oldest, oldest, qblk, pl.BlockSpec((KV_HEADS, REP * CHUNK, 1), lambda p: (0, 0, 0))],
        out_shape=[
            jax.ShapeDtypeStruct((lp, D_ATT), F32),
            jax.ShapeDtypeStruct((lp, D_KV), F32),
            jax.ShapeDtypeStruct((lp, D_KV), F32),
            jax.ShapeDtypeStruct((lp, D_ATT), F32),
            jax.ShapeDtypeStruct((KV_HEADS, REP * CHUNK, 1), F32),
        ],
        scratch_shapes=[pltpu.VMEM((wrows, D_KV), F32), pltpu.VMEM((wrows, D_KV), F32)],
        compiler_params=_params(("arbitrary",)),
    )(qr, *([kr] * ATT_KC), *([vb] * ATT_KC), att, proj, dmix, sink_stack)


def _post_loss(out, x2d, target, w):
    lp = out.shape[0]
    nc = lp // CHUNK
    nx = x2d.shape[0] // CHUNK

    def body(o_ref, x_ref, t_ref, w_ref, do_ref, dy_ref, gw_ref, loss_ref):
        i = pl.program_id(0)

        @pl.when(i == 0)
        def _():
            gw_ref[...] = jnp.zeros_like(gw_ref)
            loss_ref[...] = jnp.zeros_like(loss_ref)

        o = o_ref[...]
        w = w_ref[...]
        rstd = lax.rsqrt(jnp.mean(o * o, axis=-1, keepdims=True) + EPS)
        xhat = o * rstd
        err = (x_ref[...] + xhat * w - t_ref[...]) * ((i > 0) & (i <= nx)).astype(F32)
        loss_ref[...] += 0.5 * jnp.sum(jnp.mean(err * err, axis=-1, keepdims=True), axis=0, keepdims=True)
        dy = err * (1.0 / D_MODEL)
        dy_ref[...] = dy
        gw_ref[...] += jnp.sum(dy * xhat, axis=0, keepdims=True)
        dxhat = dy * w
        do_ref[...] = (rstd * (dxhat - xhat * jnp.mean(dxhat * xhat, axis=-1, keepdims=True))).astype(BF16)

    row = pl.BlockSpec((CHUNK, D_MODEL), lambda i: (i, 0))
    return pl.pallas_call(
        body, name="post_loss", grid=(nc,),
        in_specs=[row, _x_spec(nx), _x_spec(nx), pl.BlockSpec((1, D_MODEL), lambda i: (0, 0))],
        out_specs=[row, row, pl.BlockSpec((1, D_MODEL), lambda i: (0, 0)), pl.BlockSpec((1, 128), lambda i: (0, 0))],
        out_shape=[
            jax.ShapeDtypeStruct((lp, D_MODEL), BF16),
            jax.ShapeDtypeStruct((lp, D_MODEL), F32),
            jax.ShapeDtypeStruct((1, D_MODEL), F32),
            jax.ShapeDtypeStruct((1, 128), F32),
        ],
        compiler_params=_params(("arbitrary",)),
    )(out, x2d, target, w)


def _prenorm_bwd(dhn, x2d, h0, dy, w):
    lp = dhn.shape[0]
    nc = lp // CHUNK
    nx = x2d.shape[0] // CHUNK

    def body(dhn_ref, x_ref, h0_ref, dy_ref, w_ref, gx_ref, d0_ref, gw_ref):
        i = pl.program_id(0)

        @pl.when(i == 0)
        def _():
            gw_ref[...] = jnp.zeros_like(gw_ref)

        h = _h_chunk(i, nx, h0_ref, x_ref)
        dhn = dhn_ref[...]
        rstd = lax.rsqrt(jnp.mean(h * h, axis=-1, keepdims=True) + EPS)
        xhat = h * rstd
        gw_ref[...] += jnp.sum(dhn * xhat, axis=0, keepdims=True)
        dxhat = dhn * w_ref[...]
        dh = rstd * (dxhat - xhat * jnp.mean(dxhat * xhat, axis=-1, keepdims=True)) + dy_ref[...]

        @pl.when(i == 0)
        def _():
            d0_ref[...] = dh

        @pl.when((i > 0) & (i <= nx))
        def _():
            gx_ref[...] = dh

    row = pl.BlockSpec((CHUNK, D_MODEL), lambda i: (i, 0))
    return pl.pallas_call(
        body, name="prenorm_bwd", grid=(nc,),
        in_specs=[row, _x_spec(nx), pl.BlockSpec((CHUNK, D_MODEL), lambda i: (0, 0)), row,
                  pl.BlockSpec((1, D_MODEL), lambda i: (0, 0))],
        out_specs=[_x_spec(nx), pl.BlockSpec((CHUNK, D_MODEL), lambda i: (0, 0)),
                   pl.BlockSpec((1, D_MODEL), lambda i: (0, 0))],
        out_shape=[
            jax.ShapeDtypeStruct((nx * CHUNK, D_MODEL), F32),
            jax.ShapeDtypeStruct((CHUNK, D_MODEL), F32),
            jax.ShapeDtypeStruct((1, D_MODEL), F32),
        ],
        compiler_params=_params(("arbitrary",)),
    )(dhn, x2d, h0, dy, w)


def _adamw(slabs, w, m, v, name):
    rows, cols = w.shape
    tr = _tile(rows, (256, 128, 64, 16, 8))
    c1 = 1.0 - ADAM_B1 ** ADAM_STEP
    c2 = 1.0 - ADAM_B2 ** ADAM_STEP

    def body(s_ref, w_ref, m_ref, v_ref, g_ref, d_ref, mo_ref, vo_ref):
        g = s_ref[0].astype(F32)
        for k in range(1, slabs.shape[0]):
            g = g + s_ref[k].astype(F32)
        w = w_ref[...]
        m = ADAM_B1 * m_ref[...] + (1.0 - ADAM_B1) * g
        v = ADAM_B2 * v_ref[...] + (1.0 - ADAM_B2) * (g * g)
        g_ref[...] = g
        mo_ref[...] = m
        vo_ref[...] = v
        d_ref[...] = -ADAM_LR * ((m / c1) / (jnp.sqrt(v / c2) + ADAM_EPS) + ADAM_WD * w)

    blk = pl.BlockSpec((tr, cols), lambda i: (i, 0))
    shape = jax.ShapeDtypeStruct((rows, cols), F32)
    return pl.pallas_call(
        body, name=name, grid=(rows // tr,),
        in_specs=[pl.BlockSpec((slabs.shape[0], tr, cols), lambda i: (0, i, 0)), blk, blk, blk],
        out_specs=[blk, blk, blk, blk],
        out_shape=[shape, shape, shape, shape],
        compiler_params=_params(("parallel",)),
    )(slabs, w, m, v)


def _perm_xbc(a):
    lead = a.shape[:-1]
    xs = a[..., :D_SSD].reshape(lead + (N_GROUPS, GROUP_W))
    b = a[..., D_SSD:D_SSD + N_GROUPS * D_STATE].reshape(lead + (N_GROUPS, D_STATE))
    c = a[..., D_SSD + N_GROUPS * D_STATE:].reshape(lead + (N_GROUPS, D_STATE))
    return jnp.concatenate([xs, b, c], axis=-1).reshape(lead + (D_CONV,))


def _unperm_xbc(a):
    lead = a.shape[:-1]
    t = a.reshape(lead + (N_GROUPS, XBC_BLK))
    xs = t[..., :GROUP_W].reshape(lead + (D_SSD,))
    b = t[..., GROUP_W:GROUP_W + D_STATE].reshape(lead + (N_GROUPS * D_STATE,))
    c = t[..., GROUP_W + D_STATE:].reshape(lead + (N_GROUPS * D_STATE,))
    return jnp.concatenate([xs, b, c], axis=-1)


R_Z, R_XBC, R_DT, R_Q, R_K, R_V, R_G = 0, 2048, 6144, 6176, 7200, 7456, 7712


def _internal_of_reference():
    ref = np.arange(D_IN_PROJ)
    out = np.empty(D_IN_PROJ, np.int64)
    out[R_Z:R_XBC] = OFF_Z + ref[:D_SSD]
    xs = np.arange(D_SSD)
    out[R_XBC:R_XBC + D_SSD] = OFF_XBC + (xs // GROUP_W) * XBC_BLK + xs % GROUP_W
    bc = np.arange(N_GROUPS * D_STATE)
    out[R_XBC + D_SSD:R_XBC + D_SSD + N_GROUPS * D_STATE] = OFF_XBC + (bc // D_STATE) * XBC_BLK + GROUP_W + bc % D_STATE
    out[R_XBC + D_SSD + N_GROUPS * D_STATE:R_DT] = OFF_XBC + (bc // D_STATE) * XBC_BLK + GROUP_W + D_STATE + bc % D_STATE
    out[R_DT:R_Q] = OFF_DT + np.arange(SSD_HEADS)
    out[R_Q:R_K] = OFF_Q + np.arange(D_ATT)
    out[R_K:R_V] = OFF_K + np.arange(D_KV)
    out[R_V:R_G] = OFF_V + np.arange(D_KV)
    out[R_G:] = OFF_G + np.arange(D_ATT)
    return out


def _runs(src, dst_break):
    runs, lo = [], 0
    for i in range(1, len(src) + 1):
        if i == len(src) or src[i] != src[i - 1] + 1 or dst_break[i] != dst_break[i - 1]:
            runs.append((lo, i))
            lo = i
    return runs


def _w_in_internal(w_gathered):
    int_of_ref = _internal_of_reference()
    ref_of_int = np.full(NP, -1, np.int64)
    ref_of_int[int_of_ref] = np.arange(D_IN_PROJ)
    shard = np.where(ref_of_int >= 0, ref_of_int // SHARD_IN, -1)
    src = np.where(ref_of_int >= 0, ref_of_int, -10 - 2 * np.arange(NP))
    pieces, zeros = [], 0
    for lo, hi in _runs(src, shard):
        if ref_of_int[lo] < 0:
            zeros += hi - lo
            continue
        if zeros:
            pieces.append(jnp.zeros((D_MODEL, zeros), w_gathered.dtype))
            zeros = 0
        s, c0 = ref_of_int[lo] // SHARD_IN, ref_of_int[lo] % SHARD_IN
        pieces.append(w_gathered[s, :, c0:c0 + hi - lo])
    if zeros:
        pieces.append(jnp.zeros((D_MODEL, zeros), w_gathered.dtype))
    return jnp.concatenate(pieces, axis=1)


def _w_in_slabs(dw):
    int_of_ref = _internal_of_reference()
    slabs = []
    for s in range(N_DEV):
        cols = int_of_ref[s * SHARD_IN:(s + 1) * SHARD_IN]
        slabs.append(jnp.concatenate([dw[:, cols[lo]:cols[lo] + hi - lo]
                                      for lo, hi in _runs(cols, np.zeros_like(cols))], axis=1))
    return jnp.stack(slabs, axis=0)


def _rep_heads(a):
    return jnp.repeat(a, HEAD_DIM, axis=1)


SMALL = (("norm_pre_w", 2048), ("conv_b", 4096), ("dt_bias", 32), ("a_log", 32), ("d_skip", 32),
         ("ssd_norm_w", 2048), ("attn_sinks", 16), ("norm_post_w", 2048))
SMALL_USED = sum(size for _, size in SMALL)
SMALL_LEN = 10368


def _pack_small(d, loss=None):
    parts = [d[name].reshape(1, size) for name, size in SMALL]
    tail = jnp.zeros((1, SMALL_LEN - SMALL_USED), F32)
    if loss is not None:
        tail = tail.at[0, 0].set(loss)
    return jnp.concatenate(parts + [tail], axis=1)


def _unpack_small(vec):
    out, off = {}, 0
    for name, size in SMALL:
        out[name] = vec[:, off:off + size]
        off += size
    return out


def kernel(x, meta_tokens, norm_pre_w, w_in, conv_w, conv_b, dt_bias, a_log, d_skip, ssd_norm_w, attn_sinks, w_out, norm_post_w, loss_target, m_meta_tokens, m_norm_pre_w, m_w_in, m_conv_w, m_conv_b, m_dt_bias, m_a_log, m_d_skip, m_ssd_norm_w, m_attn_sinks, m_w_out, m_norm_post_w, v_meta_tokens, v_norm_pre_w, v_w_in, v_conv_w, v_conv_b, v_dt_bias, v_a_log, v_d_skip, v_ssd_norm_w, v_attn_sinks, v_w_out, v_norm_post_w):
    seq = x.shape[1]
    lp = seq + 2 * CHUNK
    x2d = x[0]

    w_in_g, conv_w_g, meta_g = _gather_two_level([w_in[0].astype(BF16), conv_w[0], meta_tokens], "gather_w_in")
    w_all = _w_in_internal(w_in_g)
    conv_w_full = _perm_xbc(jnp.transpose(conv_w_g, (1, 0, 2)).reshape(CONV_WIDTH, D_CONV))
    conv_b_int = _perm_xbc(conv_b)
    meta_full = jnp.transpose(meta_g, (1, 0, 2)).reshape(N_META, D_MODEL)
    h0 = jnp.concatenate([jnp.zeros((PAD_LEAD, D_MODEL), F32), meta_full], axis=0)

    pos = (jnp.arange(lp) - PAD_LEAD).astype(F32)
    half = HEAD_DIM // 2
    inv = ROPE_THETA ** (-jnp.arange(half, dtype=F32) / half)
    ang = pos[:, None] * inv[None, :]
    cos_t = jnp.tile(jnp.cos(ang), (1, 4))
    sin_t = jnp.tile(jnp.concatenate([-jnp.sin(ang), jnp.sin(ang)], axis=1), (1, 2))
    head_of_col = np.arange(D_SSD) // HEAD_DIM
    expand = jnp.asarray((np.arange(128)[:, None] == head_of_col[None, :]).astype(np.float32))
    reduce_t = jnp.asarray((head_of_col[:, None] == np.arange(128)[None, :]).astype(np.float32))
    tri = jnp.asarray(np.tril(np.ones((CHUNK, CHUNK), np.float32)))
    sel = jnp.asarray((np.arange(8)[:, None] * HEAD_DIM == np.arange(GROUP_W)[None, :]).astype(np.float32))
    a_rep = _rep_heads(-jnp.exp(a_log))
    dsk_rep = _rep_heads(d_skip)
    dt_bias_rep = _rep_heads(dt_bias)
    sink_stack = jnp.repeat(attn_sinks.reshape(KV_HEADS, REP), CHUNK, axis=1).reshape(KV_HEADS, REP * CHUNK, 1)

    hn = _prenorm(x2d, h0, norm_pre_w)
    tm = _tile(lp, (1056, 704, 128, 64))
    proj, w_out_g = _matmul(hn, w_all, tm=tm, tn=1536, tk=D_MODEL, out_dtype=F32, name="in_proj",
                            comm=_Comm([(w_out[0].astype(BF16), "gather")]))
    w_out_full = w_out_g.reshape(D_MIX, D_MODEL)
    xbc = _conv_fwd(proj, conv_w_full, conv_b_int)
    qr, kr, vb, dt_rep = _act_fwd(proj, cos_t, sin_t, expand, dt_bias_rep)
    mix, ytot, hprev = _ssd_fwd(xbc, dt_rep, proj, a_rep, dsk_rep, ssd_norm_w, tri, sel)
    att, mix = _attn_fwd(qr, kr, vb, proj, sink_stack, mix)
    out = _matmul(mix, w_out_full, tm=tm, tn=1024, tk=D_MIX, out_dtype=F32, name="out_proj")
    dout, dy, g_norm_post, loss_part = _post_loss(out, x2d, loss_target[0], norm_post_w)

    dmix = _matmul(dout, w_out_full, trans_b=True, tm=tm, tn=1024, tk=D_MODEL, out_dtype=F32, name="dmix")
    dw_out = _matmul(mix, dout, trans_a=True, tm=512, tn=1024, tk=lp, out_dtype=BF16, name="dw_out")
    dqr, dkr, dv, dg, dsink_rows = _attn_bwd(qr, kr, vb, att, proj, dmix, sink_stack)
    dproj, dxbc, ddt_part, dd_part, da_part, g_ssd_norm, g_out = _ssd_bwd(
        dmix, ytot, proj, xbc, dt_rep, hprev, a_rep, dsk_rep, ssd_norm_w, tri, sel,
        _Comm([(dw_out.reshape(N_DEV, D_MIX // N_DEV, D_MODEL), "scatter")]))
    dproj, dconv_w_int, dconv_b_int = _conv_bwd(dxbc, proj, conv_w_full, conv_b_int, dproj)
    dproj, ddt_bias = _act_bwd(dqr, dkr, dv, dg, ddt_part, cos_t, sin_t, reduce_t, dproj)
    dw_all = _matmul(hn, dproj, trans_a=True, tm=512, tn=1024, tk=lp, out_dtype=BF16, name="dw_in")
    dw_slabs = _w_in_slabs(dw_all)
    dw_chip = _pair_sum(dw_slabs, _exchange_sibling(dw_slabs, "dw_in_sibling"), "dw_in_pair_sum")
    dhn, g_in = _matmul(dproj, w_all, trans_b=True, tm=tm, tn=1024, tk=1536, out_dtype=F32, name="dhn",
                        comm=_Comm([(dw_chip, "scatter")], scope="chips"))
    grad_x, dh0, g_norm_pre = _prenorm_bwd(dhn, x2d, h0, dy, norm_pre_w)

    dmeta = dh0[PAD_LEAD:, :]
    dconv_w_ref = _unperm_xbc(dconv_w_int)
    heads = lambda part: part.reshape(SSD_HEADS, HEAD_DIM).sum(axis=1).reshape(1, SSD_HEADS)
    small_local = _pack_small({
        "norm_pre_w": g_norm_pre, "conv_b": _unperm_xbc(dconv_b_int), "dt_bias": ddt_bias[:, :SSD_HEADS],
        "a_log": heads(da_part) * (-jnp.exp(a_log)), "d_skip": heads(dd_part), "ssd_norm_w": g_ssd_norm,
        "attn_sinks": dsink_rows.reshape(Q_HEADS, CHUNK).sum(axis=1).reshape(1, Q_HEADS),
        "norm_post_w": g_norm_post}, loss=loss_part[0, 0])
    g_conv, g_meta, g_small = _exchange(
        [(jnp.transpose(dconv_w_ref.reshape(CONV_WIDTH, N_DEV, D_CONV // N_DEV), (1, 0, 2)), "scatter"),
         (jnp.transpose(dmeta.reshape(N_META, N_DEV, D_MODEL // N_DEV), (1, 0, 2)), "scatter"),
         (small_local, "gather")], "exchange_small")

    res = {}
    res["w_in"] = [o[None] for o in _adamw(g_in, w_in[0], m_w_in[0], v_w_in[0], "adamw_w_in")]
    res["w_out"] = [o[None] for o in _adamw(g_out, w_out[0], m_w_out[0], v_w_out[0], "adamw_w_out")]
    res["conv_w"] = [o[None] for o in _adamw(g_conv, conv_w[0], m_conv_w[0], v_conv_w[0], "adamw_conv_w")]
    res["meta_tokens"] = _adamw(g_meta, meta_tokens, m_meta_tokens, v_meta_tokens, "adamw_meta")
    given = dict(norm_pre_w=(norm_pre_w, m_norm_pre_w, v_norm_pre_w), conv_b=(conv_b, m_conv_b, v_conv_b),
                 dt_bias=(dt_bias, m_dt_bias, v_dt_bias), a_log=(a_log, m_a_log, v_a_log),
                 d_skip=(d_skip, m_d_skip, v_d_skip), ssd_norm_w=(ssd_norm_w, m_ssd_norm_w, v_ssd_norm_w),
                 attn_sinks=(attn_sinks, m_attn_sinks, v_attn_sinks),
                 norm_post_w=(norm_post_w, m_norm_post_w, v_norm_post_w))
    packed = [_pack_small({k: t[j] for k, t in given.items()}) for j in range(3)]
    small_out = _adamw(g_small, packed[0], packed[1], packed[2], "adamw_small")
    small_res = [_unpack_small(r) for r in small_out]
    loss = small_out[0][0, SMALL_USED]

    order = ["meta_tokens", "norm_pre_w", "w_in", "conv_w", "conv_b", "dt_bias", "a_log", "d_skip", "ssd_norm_w",
             "attn_sinks", "w_out", "norm_post_w"]
    outs = []
    for j in range(4):
        for name in order:
            outs.append(res[name][j] if name in res else small_res[j][name])
    return (loss, grad_x[None], *outs)
```

```python
import functools

import numpy as np
import jax
import jax.numpy as jnp
from jax import lax
from jax.experimental import pallas as pl
from jax.experimental.pallas import tpu as pltpu

F32 = jnp.float32
BF16 = jnp.bfloat16
HIGHEST = lax.Precision.HIGHEST

N_DEV = 8
D_MODEL = 2048
CHUNK = 64
N_META = 16
PAD_LEAD = CHUNK - N_META
EPS = 1e-6
N_GROUPS = 8
HEADS_PER_GROUP = 4
HEAD_DIM = 64
GROUP_W = HEADS_PER_GROUP * HEAD_DIM
D_STATE = 128
D_SSD = 2048
D_CONV = 4096
SSD_HEADS = 32
CONV_WIDTH = 4
Q_HEADS = 16
KV_HEADS = 4
REP = 4
D_ATT = 1024
D_KV = 256
WINDOW_CHUNKS = 2
BAND = (WINDOW_CHUNKS + 1) * CHUNK
ROPE_THETA = 10000.0
D_MIX = D_SSD + D_ATT
D_IN_PROJ = 8736
SHARD_IN = D_IN_PROJ // N_DEV

OFF_Z, OFF_XBC, OFF_Q, OFF_G, OFF_K, OFF_V, OFF_DT = 0, 2048, 6144, 7168, 8192, 8448, 8704
NP = 9216
TAIL_W = NP - OFF_Q
XBC_BLK = 512
SSD_GPS = 8
SSD_GPS_BWD = 8

ADAM_LR, ADAM_B1, ADAM_B2, ADAM_EPS, ADAM_WD, ADAM_STEP = 0.001, 0.9, 0.999, 1e-08, 0.01, 10

VMEM_LIMIT = 48 * 1024 * 1024

NN = (((1,), (0,)), ((), ()))
NT = (((1,), (1,)), ((), ()))
TN = (((0,), (0,)), ((), ()))
ANY = pl.BlockSpec(memory_space=pl.ANY)


def _dot(a, b, dims=NN, precision=None):
    return lax.dot_general(a, b, dims, precision=precision, preferred_element_type=F32)


def _tile(n, prefs):
    for t in prefs:
        if n % t == 0:
            return t
    return n


def _params(sem, vmem=VMEM_LIMIT, side_effects=False):
    return pltpu.CompilerParams(dimension_semantics=sem, vmem_limit_bytes=vmem, has_side_effects=side_effects)


def _sigmoid(x):
    return 1.0 / (1.0 + jnp.exp(-x))


class _Comm:
    def __init__(self, items, scope="devices"):
        self.items = items
        self.scope = scope
        self.slabs = slabs = N_DEV if scope == "devices" else N_DEV // 2
        self.n = n = len(items)
        self.operands = [arr for arr, _ in items]
        self.in_specs = [ANY] * n
        self.out_specs = [ANY] * n
        self.out_shape = [jax.ShapeDtypeStruct((slabs,) + tuple(arr.shape) if kind == "gather" else tuple(arr.shape),
                                               arr.dtype) for arr, kind in items]
        self.scratch = [pltpu.SemaphoreType.DMA((n, slabs - 1)), pltpu.SemaphoreType.DMA((n, slabs - 1)),
                        pltpu.SemaphoreType.DMA((n,))]

    def _places(self):
        pos = (lax.axis_index("x"), lax.axis_index("y"), lax.axis_index("c"))
        if self.scope == "devices":
            index = lambda p: 4 * p[0] + 2 * p[1] + p[2]
            masks = range(1, N_DEV)
        else:
            index = lambda p: 2 * p[0] + p[1]
            masks = (2, 4, 6)
        peers = []
        for k in masks:
            p = tuple(1 - pos[b] if (k >> (2 - b)) & 1 else pos[b] for b in range(3))
            peers.append((p, index(p)))
        return index(pos), peers

    def _copies(self, ins, outs, sems, landed):
        send_sems, recv_sems, local_sems = sems
        me, peers = self._places()
        local, remote = [], []
        for a, (_, kind) in enumerate(self.items):
            own = ins[a] if kind == "gather" else ins[a].at[me]
            local.append(pltpu.make_async_copy(own, outs[a].at[me], local_sems.at[a]))
            for k, (p, pid) in enumerate(peers):
                remote.append(pltpu.make_async_remote_copy(
                    src_ref=ins[a] if kind == "gather" else ins[a].at[pid],
                    dst_ref=outs[a].at[pid if landed else me],
                    send_sem=send_sems.at[a, k], recv_sem=recv_sems.at[a, k],
                    device_id=p, device_id_type=pl.DeviceIdType.MESH))
        return local, remote

    def start(self, ins, outs, sems):
        local, remote = self._copies(ins, outs, sems, landed=False)
        for cp in local + remote:
            cp.start()

    def wait(self, ins, outs, sems):
        local, remote = self._copies(ins, outs, sems, landed=True)
        for cp in remote + local:
            cp.wait()


def _exchange(items, name):
    comm = _Comm(items)
    n = comm.n

    def body(*refs):
        ins, outs, sems = refs[:n], refs[n:2 * n], refs[2 * n:]
        comm.start(ins, outs, sems)
        comm.wait(ins, outs, sems)

    return pl.pallas_call(
        body, name=name, in_specs=comm.in_specs, out_specs=comm.out_specs, out_shape=comm.out_shape,
        scratch_shapes=comm.scratch, compiler_params=pltpu.CompilerParams(has_side_effects=True),
    )(*comm.operands)


def _exchange_sibling(slabs, name):
    def body(g_ref, land_ref, send_sems, recv_sems):
        x, y, c = lax.axis_index("x"), lax.axis_index("y"), lax.axis_index("c")
        copies = [pltpu.make_async_remote_copy(
            src_ref=g_ref.at[2 * q + (1 - c)], dst_ref=land_ref.at[q], send_sem=send_sems.at[q], recv_sem=recv_sems.at[q],
            device_id=(x, y, 1 - c), device_id_type=pl.DeviceIdType.MESH) for q in range(N_DEV // 2)]
        for cp in copies:
            cp.start()
        for cp in copies:
            cp.wait()

    return pl.pallas_call(
        body, name=name, in_specs=[ANY], out_specs=ANY,
        out_shape=jax.ShapeDtypeStruct((N_DEV // 2,) + tuple(slabs.shape[1:]), slabs.dtype),
        scratch_shapes=[pltpu.SemaphoreType.DMA((N_DEV // 2,)), pltpu.SemaphoreType.DMA((N_DEV // 2,))],
        compiler_params=pltpu.CompilerParams(has_side_effects=True),
    )(slabs)


def _pair_sum(slabs, landed, name):
    _, rows, cols = slabs.shape
    tr = _tile(rows, (256, 128, 64, 16, 8))

    def body(g_ref, l_ref, o_ref):
        mine = g_ref[lax.axis_index("c")]
        o_ref[...] = (mine.astype(F32) + l_ref[...].astype(F32)).astype(o_ref.dtype)

    return pl.pallas_call(
        body, name=name, grid=(N_DEV // 2, rows // tr),
        in_specs=[pl.BlockSpec((None, 2, tr, cols), lambda q, i: (q, 0, i, 0)),
                  pl.BlockSpec((None, tr, cols), lambda q, i: (q, i, 0))],
        out_specs=pl.BlockSpec((None, tr, cols), lambda q, i: (q, i, 0)),
        out_shape=jax.ShapeDtypeStruct((N_DEV // 2, rows, cols), slabs.dtype),
        compiler_params=_params(("parallel", "parallel")),
    )(slabs.reshape(N_DEV // 2, 2, rows, cols), landed)


def _gather_two_level(arrays, name):
    n = len(arrays)

    def body(*refs):
        ins, outs = refs[:n], refs[n:2 * n]
        send_sems, recv_sems, local_sems = refs[2 * n:]
        x, y, c = lax.axis_index("x"), lax.axis_index("y"), lax.axis_index("c")
        me, sibling = (x, y, c), (x, y, 1 - c)
        chips = [(1 - x, y), (x, 1 - y), (1 - x, 1 - y)]

        def slab(a, place):
            return outs[a].at[4 * place[0] + 2 * place[1] + place[2]]

        def copy(a, k, block, to, src=None):
            return pltpu.make_async_remote_copy(
                src_ref=slab(a, block) if src is None else src, dst_ref=slab(a, block),
                send_sem=send_sems.at[a, k], recv_sem=recv_sems.at[a, k],
                device_id=to, device_id_type=pl.DeviceIdType.MESH)

        sends, mine = [], []
        for a in range(n):
            loc = pltpu.make_async_copy(ins[a], slab(a, me), local_sems.at[a])
            loc.start()
            mine.append(loc)
            sends.append(copy(a, 0, me, sibling, src=ins[a]))
            sends += [copy(a, 1 + j, me, (*chip, c), src=ins[a]) for j, chip in enumerate(chips)]
        for cp in sends:
            cp.start()
        for j, chip in enumerate(chips):
            for a in range(n):
                copy(a, 1 + j, (*chip, c), me).wait_recv()
                fwd = copy(a, 4 + j, (*chip, c), sibling)
                fwd.start()
                sends.append(fwd)
        for a in range(n):
            copy(a, 0, sibling, me).wait_recv()
            for j, chip in enumerate(chips):
                copy(a, 4 + j, (*chip, 1 - c), me).wait_recv()
        for cp in sends:
            cp.wait_send()
        for loc in mine:
            loc.wait()

    return pl.pallas_call(
        body, name=name, in_specs=[ANY] * n, out_specs=[ANY] * n,
        out_shape=[jax.ShapeDtypeStruct((N_DEV,) + tuple(a.shape), a.dtype) for a in arrays],
        scratch_shapes=[pltpu.SemaphoreType.DMA((n, N_DEV - 1)), pltpu.SemaphoreType.DMA((n, N_DEV - 1)),
                        pltpu.SemaphoreType.DMA((n,))],
        compiler_params=pltpu.CompilerParams(has_side_effects=True),
    )(*arrays)


def _matmul(a, b, *, tm, tn, tk, out_dtype, name, trans_a=False, trans_b=False, comm=None):
    m, k = (a.shape[1], a.shape[0]) if trans_a else a.shape
    n = b.shape[0] if trans_b else b.shape[1]
    nk = k // tk
    dims = TN if trans_a else (NT if trans_b else NN)
    assert not (trans_a and trans_b)
    nc = comm.n if comm else 0
    grid = (m // tm, n // tn, nk)

    def body(*refs):
        a_ref, b_ref = refs[:2]
        cin = refs[2:2 + nc]
        o_ref = refs[2 + nc]
        cout = refs[3 + nc:3 + 2 * nc]
        scratch = refs[3 + 2 * nc:]
        sems = scratch[len(scratch) - 3:] if comm else None
        i, j, kk = pl.program_id(0), pl.program_id(1), pl.program_id(2)
        if comm:
            @pl.when((i == 0) & (j == 0) & (kk == 0))
            def _():
                comm.start(cin, cout, sems)

        if nk == 1:
            o_ref[...] = _dot(a_ref[...], b_ref[...], dims).astype(out_dtype)
        else:
            acc_ref = scratch[0]

            @pl.when(kk == 0)
            def _():
                acc_ref[...] = jnp.zeros_like(acc_ref)

            acc_ref[...] += _dot(a_ref[...], b_ref[...], dims)

            @pl.when(kk == nk - 1)
            def _():
                o_ref[...] = acc_ref[...].astype(out_dtype)

        if comm:
            @pl.when((i == grid[0] - 1) & (j == grid[1] - 1) & (kk == nk - 1))
            def _():
                comm.wait(cin, cout, sems)

    a_spec = (pl.BlockSpec((tk, tm), lambda i, j, kk: (kk, i)) if trans_a
              else pl.BlockSpec((tm, tk), lambda i, j, kk: (i, kk)))
    b_spec = (pl.BlockSpec((tn, tk), lambda i, j, kk: (j, kk)) if trans_b
              else pl.BlockSpec((tk, tn), lambda i, j, kk: (kk, j)))
    sem = ("arbitrary",) * 3 if comm else ("parallel", "parallel", "arbitrary")
    res = pl.pallas_call(
        body, name=name, grid=grid,
        in_specs=[a_spec, b_spec] + (comm.in_specs if comm else []),
        out_specs=[pl.BlockSpec((tm, tn), lambda i, j, kk: (i, j))] + (comm.out_specs if comm else []),
        out_shape=[jax.ShapeDtypeStruct((m, n), out_dtype)] + (comm.out_shape if comm else []),
        scratch_shapes=([] if nk == 1 else [pltpu.VMEM((tm, tn), F32)]) + (comm.scratch if comm else []),
        compiler_params=_params(sem, side_effects=bool(comm)),
    )(a, b, *(comm.operands if comm else []))
    return res if comm else res[0]


ROW_K = 6
X_K = 8


def _x_specs(nx, k):
    return [pl.BlockSpec((CHUNK, D_MODEL), functools.partial(lambda u, i: (jnp.clip(k * i + u - 1, 0, nx - 1), 0), u))
            for u in range(k)]


def _chunk_of_row(i, k):
    return k * i + lax.broadcasted_iota(jnp.int32, (k * CHUNK, 1), 0) // CHUNK


def _prenorm(x2d, h0, w):
    nx = x2d.shape[0] // CHUNK
    nc = nx + 2
    k = _tile(nc, (ROW_K, 3, 2, 1))

    def body(*refs):
        x_refs = refs[:k]
        h0_ref, w_ref, o_ref = refs[k:]
        i = pl.program_id(0)
        x = jnp.concatenate([r[...] for r in x_refs], axis=0)
        head = jnp.concatenate([h0_ref[...], x[CHUNK:, :]], axis=0)
        chunk = _chunk_of_row(i, k)
        h = jnp.where(i == 0, head, x) * (chunk <= nx).astype(F32)
        rstd = lax.rsqrt(jnp.mean(h * h, axis=-1, keepdims=True) + EPS)
        o_ref[...] = (h * rstd * w_ref[...]).astype(BF16)

    rows = k * CHUNK
    return pl.pallas_call(
        body, name="prenorm", grid=(nc // k,),
        in_specs=_x_specs(nx, k) + [pl.BlockSpec((CHUNK, D_MODEL), lambda i: (0, 0)), pl.BlockSpec((1, D_MODEL), lambda i: (0, 0))],
        out_specs=pl.BlockSpec((rows, D_MODEL), lambda i: (i, 0)),
        out_shape=jax.ShapeDtypeStruct((nc * CHUNK, D_MODEL), BF16),
        compiler_params=_params(("parallel",)),
    )(*([x2d] * k), h0, w)


CONV_COLS = 512
HALO = 8


def _conv_pre(ext, w, b):
    taps = [ext[HALO:, :]] + [pltpu.roll(ext, j, 0)[HALO:, :] for j in range(1, CONV_WIDTH)]
    acc = b + w[3:4, :] * taps[0]
    for j in range(1, CONV_WIDTH):
        acc = acc + w[3 - j:4 - j, :] * taps[j]
    return acc, taps


def _conv_fwd(proj, conv_w, conv_b):
    lp = proj.shape[0]
    t = _tile(lp, (704, 384, 128, 64))
    hb = t // HALO
    c0 = OFF_XBC // CONV_COLS

    def body(u_ref, halo_ref, w_ref, b_ref, o_ref):
        i = pl.program_id(0)
        halo = halo_ref[...] * (i > 0).astype(F32)
        pre, _ = _conv_pre(jnp.concatenate([halo, u_ref[...]], axis=0), w_ref[...], b_ref[...])
        o_ref[...] = pre * _sigmoid(pre)

    return pl.pallas_call(
        body, name="conv_fwd", grid=(lp // t, D_CONV // CONV_COLS),
        in_specs=[
            pl.BlockSpec((t, CONV_COLS), lambda i, j: (i, c0 + j)),
            pl.BlockSpec((HALO, CONV_COLS), lambda i, j: (jnp.maximum(i * hb - 1, 0), c0 + j)),
            pl.BlockSpec((CONV_WIDTH, CONV_COLS), lambda i, j: (0, j)),
            pl.BlockSpec((1, CONV_COLS), lambda i, j: (0, j)),
        ],
        out_specs=pl.BlockSpec((t, CONV_COLS), lambda i, j: (i, j)),
        out_shape=jax.ShapeDtypeStruct((lp, D_CONV), F32),
        compiler_params=_params(("parallel", "parallel")),
    )(proj, proj, conv_w, conv_b)


def _conv_bwd(dxbc, proj, conv_w, conv_b, dproj):
    lp = proj.shape[0]
    t = _tile(lp, (704, 384, 128, 64))
    hb = t // HALO
    nt = lp // t
    c0 = OFF_XBC // CONV_COLS

    def body(dx_ref, dxn_ref, u_ref, up_ref, un_ref, w_ref, b_ref, _, du_ref, dw_ref, db_ref):
        i = pl.program_id(1)
        w = w_ref[...]
        up = up_ref[...] * (i > 0).astype(F32)
        ext = jnp.concatenate([up, u_ref[...], un_ref[...]], axis=0)
        pre, taps = _conv_pre(ext, w, b_ref[...])
        dxn = dxn_ref[...] * (i < nt - 1).astype(F32)
        dxe = jnp.concatenate([dx_ref[...], dxn], axis=0)
        sg = _sigmoid(pre)
        dpre = dxe * sg * (1.0 + pre * (1.0 - sg))
        du = w[3:4, :] * dpre[:t, :]
        for j in range(1, CONV_WIDTH):
            du = du + w[3 - j:4 - j, :] * pltpu.roll(dpre, t + HALO - j, 0)[:t, :]
        du_ref[...] = du.astype(BF16)

        @pl.when(i == 0)
        def _():
            dw_ref[...] = jnp.zeros_like(dw_ref)
            db_ref[...] = jnp.zeros_like(db_ref)

        dp = dpre[:t, :]
        db_ref[...] += jnp.sum(dp, axis=0, keepdims=True)
        for j in range(CONV_WIDTH):
            dw_ref[3 - j:4 - j, :] += jnp.sum(dp * taps[j][:t, :], axis=0, keepdims=True)

    nxt = lambda i: jnp.minimum((i + 1) * hb, lp // HALO - 1)
    return pl.pallas_call(
        body, name="conv_bwd", grid=(D_CONV // CONV_COLS, nt),
        in_specs=[
            pl.BlockSpec((t, CONV_COLS), lambda j, i: (i, j)),
            pl.BlockSpec((HALO, CONV_COLS), lambda j, i: (nxt(i), j)),
            pl.BlockSpec((t, CONV_COLS), lambda j, i: (i, c0 + j)),
            pl.BlockSpec((HALO, CONV_COLS), lambda j, i: (jnp.maximum(i * hb - 1, 0), c0 + j)),
            pl.BlockSpec((HALO, CONV_COLS), lambda j, i: (nxt(i), c0 + j)),
            pl.BlockSpec((CONV_WIDTH, CONV_COLS), lambda j, i: (0, j)),
            pl.BlockSpec((1, CONV_COLS), lambda j, i: (0, j)),
            ANY,
        ],
        out_specs=[
            pl.BlockSpec((t, CONV_COLS), lambda j, i: (i, c0 + j)),
            pl.BlockSpec((CONV_WIDTH, CONV_COLS), lambda j, i: (0, j)),
            pl.BlockSpec((1, CONV_COLS), lambda j, i: (0, j)),
        ],
        out_shape=[
            jax.ShapeDtypeStruct((lp, NP), BF16),
            jax.ShapeDtypeStruct((CONV_WIDTH, D_CONV), F32),
            jax.ShapeDtypeStruct((1, D_CONV), F32),
        ],
        input_output_aliases={7: 0},
        compiler_params=_params(("parallel", "arbitrary")),
    )(dxbc, dxbc, proj, proj, proj, conv_w, conv_b, dproj)


def _swap_halves(t):
    w = t.shape[-1]
    lane = lax.broadcasted_iota(jnp.int32, t.shape, 1)
    return jnp.where((lane % HEAD_DIM) < HEAD_DIM // 2, pltpu.roll(t, w - HEAD_DIM // 2, 1),
                     pltpu.roll(t, HEAD_DIM // 2, 1))


def _act_fwd(proj, cos_t, sin_t, expand, dt_bias_rep):
    lp = proj.shape[0]
    t = _tile(lp, (384, 128, 64))

    def body(q_ref, k_ref, v_ref, dt_ref, cos_ref, sin_ref, ex_ref, bias_ref, qo_ref, ko_ref, vo_ref, dto_ref):
        i = pl.program_id(0)
        cos = cos_ref[...]
        sin = sin_ref[...]
        q = q_ref[...]
        qo_ref[...] = (q * jnp.tile(cos, (1, D_ATT // 128)) + _swap_halves(q) * jnp.tile(sin, (1, D_ATT // 128))).astype(BF16)
        k = k_ref[...]
        ko_ref[...] = (k * jnp.tile(cos, (1, D_KV // 128)) + _swap_halves(k) * jnp.tile(sin, (1, D_KV // 128))).astype(BF16)
        vo_ref[...] = v_ref[...].astype(BF16)
        raw = _dot(dt_ref[...], ex_ref[...], NN, HIGHEST) + bias_ref[...]
        sp = jnp.maximum(raw, 0.0) + jnp.log1p(jnp.exp(-jnp.abs(raw)))
        row = i * t + lax.broadcasted_iota(jnp.int32, sp.shape, 0)
        dto_ref[...] = jnp.where(row >= PAD_LEAD, sp, 0.0)

    return pl.pallas_call(
        body, name="act_fwd", grid=(lp // t,),
        in_specs=[
            pl.BlockSpec((t, D_ATT), lambda i: (i, OFF_Q // D_ATT)),
            pl.BlockSpec((t, D_KV), lambda i: (i, OFF_K // D_KV)),
            pl.BlockSpec((t, D_KV), lambda i: (i, OFF_V // D_KV)),
            pl.BlockSpec((t, 128), lambda i: (i, OFF_DT // 128)),
            pl.BlockSpec((t, 128), lambda i: (i, 0)),
            pl.BlockSpec((t, 128), lambda i: (i, 0)),
            pl.BlockSpec((128, D_SSD), lambda i: (0, 0)),
            pl.BlockSpec((1, D_SSD), lambda i: (0, 0)),
        ],
        out_specs=[
            pl.BlockSpec((t, D_ATT), lambda i: (i, 0)),
            pl.BlockSpec((t, D_KV), lambda i: (i, 0)),
            pl.BlockSpec((t, D_KV), lambda i: (i, 0)),
            pl.BlockSpec((t, D_SSD), lambda i: (i, 0)),
        ],
        out_shape=[
            jax.ShapeDtypeStruct((lp, D_ATT), BF16),
            jax.ShapeDtypeStruct((lp, D_KV), BF16),
            jax.ShapeDtypeStruct((lp, D_KV), BF16),
            jax.ShapeDtypeStruct((lp, D_SSD), F32),
        ],
        compiler_params=_params(("parallel",)),
    )(proj, proj, proj, proj, cos_t, sin_t, expand, dt_bias_rep)


def _act_bwd(dqr, dkr, dv, dg, ddt_part, cos_t, sin_t, reduce_t, dproj):
    lp = dqr.shape[0]
    t = _tile(lp, (384, 128, 64))

    def body(dq_ref, dk_ref, dv_ref, dg_ref, ddt_ref, cos_ref, sin_ref, red_ref, _, o_ref, db_ref):
        i = pl.program_id(0)
        cos = cos_ref[...]
        sin = sin_ref[...]
        dq = dq_ref[...]
        dq = dq * jnp.tile(cos, (1, D_ATT // 128)) + _swap_halves(dq * jnp.tile(sin, (1, D_ATT // 128)))
        dk = dk_ref[...]
        dk = dk * jnp.tile(cos, (1, D_KV // 128)) + _swap_halves(dk * jnp.tile(sin, (1, D_KV // 128)))
        ddt = _dot(ddt_ref[...], red_ref[...], NN, HIGHEST)
        o_ref[...] = jnp.concatenate(
            [dq.astype(BF16), dg_ref[...].astype(BF16), dk.astype(BF16), dv_ref[...].astype(BF16), ddt.astype(BF16),
             jnp.zeros((t, NP - OFF_DT - 128), BF16)], axis=1)

        @pl.when(i == 0)
        def _():
            db_ref[...] = jnp.zeros_like(db_ref)

        db_ref[...] += jnp.sum(ddt, axis=0, keepdims=True)

    return pl.pallas_call(
        body, name="act_bwd", grid=(lp // t,),
        in_specs=[
            pl.BlockSpec((t, D_ATT), lambda i: (i, 0)),
            pl.BlockSpec((t, D_KV), lambda i: (i, 0)),
            pl.BlockSpec((t, D_KV), lambda i: (i, 0)),
            pl.BlockSpec((t, D_ATT), lambda i: (i, 0)),
            pl.BlockSpec((t, D_SSD), lambda i: (i, 0)),
            pl.BlockSpec((t, 128), lambda i: (i, 0)),
            pl.BlockSpec((t, 128), lambda i: (i, 0)),
            pl.BlockSpec((D_SSD, 128), lambda i: (0, 0)),
            ANY,
        ],
        out_specs=[pl.BlockSpec((t, TAIL_W), lambda i: (i, OFF_Q // TAIL_W)), pl.BlockSpec((1, 128), lambda i: (0, 0))],
        out_shape=[jax.ShapeDtypeStruct((lp, NP), BF16), jax.ShapeDtypeStruct((1, 128), F32)],
        input_output_aliases={8: 0},
        compiler_params=_params(("arbitrary",)),
    )(dqr, dkr, dv, dg, ddt_part, cos_t, sin_t, reduce_t, dproj)


def _cs_row(cs):
    row = lax.broadcasted_iota(jnp.int32, cs.shape, 0)
    lane = lax.broadcasted_iota(jnp.int32, cs.shape, 1)
    return jnp.sum(jnp.where(row == lane % HEAD_DIM, cs, 0.0), axis=0, keepdims=True)


def _decay_matrix(cs, cs_row, r, mask):
    seg = cs[:, r * HEAD_DIM:r * HEAD_DIM + 1] - cs_row[:, r * HEAD_DIM:(r + 1) * HEAD_DIM]
    return jnp.exp(jnp.where(mask, seg, -jnp.inf))


def _causal_mask():
    row = lax.broadcasted_iota(jnp.int32, (CHUNK, CHUNK), 0)
    col = lax.broadcasted_iota(jnp.int32, (CHUNK, CHUNK), 1)
    return row >= col


def _ssd_fwd(xbc, dt_rep, proj, a_rep, dsk_rep, wn, tri):
    lp = xbc.shape[0]
    nc = lp // CHUNK
    gw = SSD_GPS * GROUP_W

    def body(xbc_ref, dt_ref, z_ref, a_ref, dsk_ref, wn_ref, tri_ref, yn_ref, ytot_ref, hprev_ref, h_scr):
        @pl.when(pl.program_id(1) == 0)
        def _():
            h_scr[...] = jnp.zeros_like(h_scr)

        causal = _causal_mask()
        G = range(SSD_GPS)
        colsl = [slice(gi * GROUP_W, (gi + 1) * GROUP_W) for gi in G]
        xbc = [xbc_ref[:, gi * XBC_BLK:(gi + 1) * XBC_BLK] for gi in G]
        dt = [dt_ref[:, colsl[gi]] for gi in G]
        xs = [xbc[gi][:, :GROUP_W] for gi in G]
        b = [xbc[gi][:, GROUP_W:GROUP_W + D_STATE].astype(BF16) for gi in G]
        c = [xbc[gi][:, GROUP_W + D_STATE:].astype(BF16) for gi in G]
        hprev = [h_scr[gi] for gi in G]
        cs = [_dot(tri_ref[...], dt[gi] * a_ref[:, colsl[gi]], NN, HIGHEST) for gi in G]
        cb = [_dot(c[gi], b[gi], NT) for gi in G]
        yoff = [_dot(c[gi], hprev[gi].astype(BF16)) for gi in G]
        cs_t = [_cs_row(cs[gi]) for gi in G]
        xdt = [xs[gi] * dt[gi] for gi in G]
        cs_last = [cs[gi][CHUNK - 1:CHUNK, :] for gi in G]
        st = [_dot(b[gi], (xdt[gi] * jnp.exp(cs_last[gi] - cs[gi])).astype(BF16), TN) for gi in G]
        m = [[(cb[gi] * _decay_matrix(cs[gi], cs_t[gi], r, causal)).astype(BF16) for r in range(HEADS_PER_GROUP)] for gi in G]
        ydiag = [[_dot(m[gi][r], xdt[gi][:, r * HEAD_DIM:(r + 1) * HEAD_DIM].astype(BF16)) for r in range(HEADS_PER_GROUP)]
                 for gi in G]
        for gi in G:
            cols = colsl[gi]
            ytot = jnp.concatenate(ydiag[gi], axis=1) + yoff[gi] * jnp.exp(cs[gi]) + dsk_ref[:, cols] * xs[gi]
            z = z_ref[:, cols]
            gz = ytot * (z * _sigmoid(z))
            rstd = lax.rsqrt(jnp.mean(gz * gz, axis=-1, keepdims=True) + EPS)
            hprev_ref[gi] = hprev[gi]
            h_scr[gi] = hprev[gi] * jnp.exp(cs_last[gi]) + st[gi]
            ytot_ref[:, cols] = ytot
            yn_ref[:, cols] = (gz * rstd * wn_ref[:, cols]).astype(BF16)

    vec = pl.BlockSpec((1, gw), lambda g, c: (0, g))
    blk = pl.BlockSpec((CHUNK, gw), lambda g, c: (c, g))
    return pl.pallas_call(
        body, name="ssd_fwd", grid=(N_GROUPS // SSD_GPS, nc),
        in_specs=[
            pl.BlockSpec((CHUNK, SSD_GPS * XBC_BLK), lambda g, c: (c, g)),
            blk, blk, vec, vec, vec,
            pl.BlockSpec((CHUNK, CHUNK), lambda g, c: (0, 0)),
        ],
        out_specs=[blk, blk, pl.BlockSpec((None, SSD_GPS, D_STATE, GROUP_W), lambda g, c: (c, g, 0, 0))],
        out_shape=[
            jax.ShapeDtypeStruct((lp, D_MIX), BF16),
            jax.ShapeDtypeStruct((lp, D_SSD), F32),
            jax.ShapeDtypeStruct((nc, N_GROUPS, D_STATE, GROUP_W), F32),
        ],
        scratch_shapes=[pltpu.VMEM((SSD_GPS, D_STATE, GROUP_W), F32)],
        compiler_params=_params(("parallel", "arbitrary")),
    )(xbc, dt_rep, proj, a_rep, dsk_rep, wn, tri)


def _ssd_bwd(dmix, ytot, proj, xbc, dt_rep, hprev, a_rep, dsk_rep, wn, tri, comm):
    lp = xbc.shape[0]
    nc = lp // CHUNK
    gps = SSD_GPS_BWD
    gw = gps * GROUP_W
    ncm = comm.n
    n_in, n_out = 10, 6
    grid = (N_GROUPS // gps, nc)

    def all_groups(refs, causal):
        (dyn_ref, ytot_ref, z_ref, xbc_ref, dt_ref, hprev_ref, a_ref, dsk_ref, wn_ref, tri_ref,
         dz_ref, dxbc_ref, ddt_ref, dd_ref, da_ref, dwn_ref, dh_scr) = refs
        G = range(gps)
        H = range(HEADS_PER_GROUP)
        cl = [slice(gi * GROUP_W, (gi + 1) * GROUP_W) for gi in G]
        hl = [slice(r * HEAD_DIM, (r + 1) * HEAD_DIM) for r in H]
        tri = tri_ref[...]
        xbc = [xbc_ref[:, gi * XBC_BLK:(gi + 1) * XBC_BLK] for gi in G]
        dt = [dt_ref[:, cl[gi]] for gi in G]
        a = [a_ref[:, cl[gi]] for gi in G]
        xs = [xbc[gi][:, :GROUP_W] for gi in G]
        bbf = [xbc[gi][:, GROUP_W:GROUP_W + D_STATE].astype(BF16) for gi in G]
        cbf = [xbc[gi][:, GROUP_W + D_STATE:].astype(BF16) for gi in G]
        hprev = [hprev_ref[gi] for gi in G]
        hbf = [hprev[gi].astype(BF16) for gi in G]
        dhn = [dh_scr[gi] for gi in G]
        dhnb = [dhn[gi].astype(BF16) for gi in G]
        cs = [_dot(tri, dt[gi] * a[gi], NN, HIGHEST) for gi in G]
        cb = [_dot(cbf[gi], bbf[gi], NT) for gi in G]
        g = [_dot(cbf[gi], hbf[gi]) for gi in G]
        dxw = [_dot(bbf[gi], dhnb[gi]) for gi in G]
        cs_t = [_cs_row(cs[gi]) for gi in G]
        dy = []
        for gi in G:
            ytot = ytot_ref[:, cl[gi]]
            z = z_ref[:, cl[gi]]
            dyn = dyn_ref[:, cl[gi]]
            sz = _sigmoid(z)
            silu_z = z * sz
            gz = ytot * silu_z
            rstd = lax.rsqrt(jnp.mean(gz * gz, axis=-1, keepdims=True) + EPS)
            xhat = gz * rstd
            dwn_ref[:, cl[gi]] += jnp.sum(dyn * xhat, axis=0, keepdims=True)
            dxhat = dyn * wn_ref[:, cl[gi]]
            dgz = rstd * (dxhat - xhat * jnp.mean(dxhat * xhat, axis=-1, keepdims=True))
            dy.append(dgz * silu_z)
            dz_ref[:, cl[gi]] = (dgz * ytot * (sz * (1.0 + z * (1.0 - sz)))).astype(BF16)
            dd_ref[:, cl[gi]] += jnp.sum(dy[gi] * xs[gi], axis=0, keepdims=True)
        xdt = [xs[gi] * dt[gi] for gi in G]
        e = [jnp.exp(cs[gi]) for gi in G]
        cs_last = [cs[gi][CHUNK - 1:CHUNK, :] for gi in G]
        dte = [jnp.exp(cs_last[gi] - cs[gi]) for gi in G]
        cd = [jnp.exp(cs_last[gi]) for gi in G]
        dgb = [(dy[gi] * e[gi]).astype(BF16) for gi in G]
        dyb = [dy[gi].astype(BF16) for gi in G]
        xdtb = [xdt[gi].astype(BF16) for gi in G]
        dc = [_dot(dgb[gi], hbf[gi], NT) for gi in G]
        dhprev = [_dot(cbf[gi], dgb[gi], TN) for gi in G]
        db = [_dot((xdt[gi] * dte[gi]).astype(BF16), dhnb[gi], NT) for gi in G]
        dm = [[_dot(dyb[gi][:, hl[r]], xdtb[gi][:, hl[r]], NT) for r in H] for gi in G]
        mb, dseg, dcbb = [], [], []
        for gi in G:
            mb.append([])
            dseg.append([])
            dcb = None
            for r in H:
                lm = _decay_matrix(cs[gi], cs_t[gi], r, causal)
                m = cb[gi] * lm
                mb[gi].append(m.astype(BF16))
                dseg[gi].append(dm[gi][r] * m)
                dcb = dm[gi][r] * lm if r == 0 else dcb + dm[gi][r] * lm
            dcbb.append(dcb.astype(BF16))
        dxdt_diag = [[_dot(mb[gi][r], dyb[gi][:, hl[r]], TN) for r in H] for gi in G]
        ones = jnp.ones((CHUNK, HEAD_DIM), F32)
        colsum = [[_dot(dseg[gi][r], ones, TN, HIGHEST) for r in H] for gi in G]
        dc2 = [_dot(dcbb[gi], bbf[gi]) for gi in G]
        db2 = [_dot(dcbb[gi], cbf[gi], TN) for gi in G]
        dcs = []
        for gi in G:
            t_dte = dxw[gi] * xdt[gi] * dte[gi]
            dcs_last = (jnp.sum(dhn[gi] * hprev[gi], axis=0, keepdims=True) * cd[gi]
                        + jnp.sum(t_dte, axis=0, keepdims=True))
            diag = jnp.concatenate(
                [(jnp.sum(dseg[gi][r], axis=1, keepdims=True) - colsum[gi][r]) * (1.0 / HEAD_DIM) for r in H], axis=1)
            d = dy[gi] * g[gi] * e[gi] - t_dte + diag
            row = lax.broadcasted_iota(jnp.int32, d.shape, 0)
            dcs.append(d + jnp.where(row == CHUNK - 1, dcs_last, 0.0))
        dda = [_dot(tri, dcs[gi], TN, HIGHEST) for gi in G]
        for gi in G:
            dxdt = dxw[gi] * dte[gi] + jnp.concatenate(dxdt_diag[gi], axis=1)
            da_ref[:, cl[gi]] += jnp.sum(dda[gi] * dt[gi], axis=0, keepdims=True)
            ddt = dda[gi] * a[gi] + dxdt * xs[gi]
            dxs = dsk_ref[:, cl[gi]] * dy[gi] + dxdt * dt[gi]
            ddt_ref[:, cl[gi]] = ddt * (1.0 - jnp.exp(-dt[gi]))
            dxbc_ref[:, gi * XBC_BLK:(gi + 1) * XBC_BLK] = jnp.concatenate(
                [dxs, db[gi] + db2[gi], dc[gi] + dc2[gi]], axis=1)
            dh_scr[gi] = dhprev[gi] + dhn[gi] * cd[gi]

    def body(*refs):
        ins = refs[:n_in]
        cin = refs[n_in:n_in + ncm]
        outs = refs[n_in + ncm:n_in + ncm + n_out]
        cout = refs[n_in + ncm + n_out:n_in + 2 * ncm + n_out]
        dh_scr = refs[n_in + 2 * ncm + n_out]
        sems = refs[n_in + 2 * ncm + n_out + 1:]
        g, c = pl.program_id(0), pl.program_id(1)

        @pl.when((g == 0) & (c == 0))
        def _():
            comm.start(cin, cout, sems)

        @pl.when(c == 0)
        def _():
            dh_scr[...] = jnp.zeros_like(dh_scr)
            for ref in outs[3:]:
                ref[...] = jnp.zeros_like(ref)

        all_groups(ins + outs + (dh_scr,), _causal_mask())

        @pl.when((g == grid[0] - 1) & (c == nc - 1))
        def _():
            comm.wait(cin, cout, sems)

    rev = lambda c: nc - 1 - c
    vec = pl.BlockSpec((1, gw), lambda g, c: (0, g))
    blk = pl.BlockSpec((CHUNK, gw), lambda g, c: (rev(c), g))
    xblk = pl.BlockSpec((CHUNK, gps * XBC_BLK), lambda g, c: (rev(c), g))
    res = pl.pallas_call(
        body, name="ssd_bwd", grid=grid,
        in_specs=[blk, blk, blk, xblk, blk,
                  pl.BlockSpec((None, gps, D_STATE, GROUP_W), lambda g, c: (rev(c), g, 0, 0)),
                  vec, vec, vec,
                  pl.BlockSpec((CHUNK, CHUNK), lambda g, c: (0, 0))] + comm.in_specs,
        out_specs=[blk, xblk, blk, vec, vec, vec] + comm.out_specs,
        out_shape=[
            jax.ShapeDtypeStruct((lp, NP), BF16),
            jax.ShapeDtypeStruct((lp, D_CONV), F32),
            jax.ShapeDtypeStruct((lp, D_SSD), F32),
            jax.ShapeDtypeStruct((1, D_SSD), F32),
            jax.ShapeDtypeStruct((1, D_SSD), F32),
            jax.ShapeDtypeStruct((1, D_SSD), F32),
        ] + comm.out_shape,
        scratch_shapes=[pltpu.VMEM((gps, D_STATE, GROUP_W), F32)] + comm.scratch,
        compiler_params=_params(("arbitrary", "arbitrary"), side_effects=True),
    )(dmix, ytot, proj, xbc, dt_rep, hprev, a_rep, dsk_rep, wn, tri, *comm.operands)
    return res


def _stack_heads(t, h):
    return jnp.concatenate([t[:, (REP * h + r) * HEAD_DIM:(REP * h + r + 1) * HEAD_DIM] for r in range(REP)], axis=0)


def _band(t2, t1, t0, h):
    sl = slice(h * HEAD_DIM, (h + 1) * HEAD_DIM)
    return jnp.concatenate([t2[:, sl], t1[:, sl], t0[:, sl]], axis=0)


def _attn_probs(s, sink, qc):
    s = s * (HEAD_DIM ** -0.5)
    key_abs = (qc - WINDOW_CHUNKS) * CHUNK + lax.broadcasted_iota(jnp.int32, s.shape, 1)
    s = jnp.where(key_abs >= PAD_LEAD, s, -jnp.inf)
    m = jnp.maximum(jnp.max(s, axis=-1, keepdims=True), sink)
    p = jnp.exp(s - m)
    ps = jnp.exp(sink - m)
    denom = jnp.sum(p, axis=-1, keepdims=True) + ps
    return p / denom, ps / denom


ATT_QC = 2
ATT_KC = WINDOW_CHUNKS + ATT_QC


def _kv_specs(width, newest_chunk_of):
    return [pl.BlockSpec((CHUNK, width), functools.partial(lambda j, p: (jnp.maximum(newest_chunk_of(p) - j, 0), 0), j))
            for j in range(ATT_KC - 1, -1, -1)]


def _attn_fwd(qr, kr, vb, proj, sink_stack, mix):
    lp = qr.shape[0]
    nc = lp // CHUNK

    assert nc % ATT_QC == 0
    qrows = ATT_QC * CHUNK

    def body(q_ref, *rest):
        k_refs, v_refs = rest[:ATT_KC], rest[ATT_KC:2 * ATT_KC]
        g_ref, sink_ref, _, att_ref, mix_ref = rest[2 * ATT_KC:]
        p = pl.program_id(0)
        q = q_ref[...]
        ks = [r[...] for r in k_refs]
        vs = [r[...] for r in v_refs]
        units = [(u, h) for u in range(ATT_QC) for h in range(KV_HEADS)]
        s = [_dot(_stack_heads(q[u * CHUNK:(u + 1) * CHUNK, :], h), _band(*ks[u:u + 3], h), NT) for u, h in units]
        vbh = [_band(*vs[u:u + 3], h) for u, h in units]
        pn = [_attn_probs(s[i], sink_ref[h], ATT_QC * p + u)[0].astype(BF16) for i, (u, h) in enumerate(units)]
        o = [_dot(pn[i], vbh[i]) for i in range(len(units))]
        att = jnp.concatenate(
            [jnp.concatenate([o[u * KV_HEADS + h][r * CHUNK:(r + 1) * CHUNK, :] for h in range(KV_HEADS) for r in range(REP)],
                             axis=1) for u in range(ATT_QC)], axis=0)
        att_ref[...] = att
        g = g_ref[...]
        mix_ref[...] = (att * (g * _sigmoid(g))).astype(BF16)

    newest = lambda p: ATT_QC * p + ATT_QC - 1
    return pl.pallas_call(
        body, name="attn_fwd", grid=(nc // ATT_QC,),
        in_specs=[pl.BlockSpec((qrows, D_ATT), lambda p: (p, 0))] + _kv_specs(D_KV, newest) + _kv_specs(D_KV, newest) + [
            pl.BlockSpec((qrows, D_ATT), lambda p: (p, OFF_G // D_ATT)),
            pl.BlockSpec((KV_HEADS, REP * CHUNK, 1), lambda p: (0, 0, 0)),
            ANY,
        ],
        out_specs=[pl.BlockSpec((qrows, D_ATT), lambda p: (p, 0)),
                   pl.BlockSpec((qrows, D_ATT), lambda p: (p, D_SSD // D_ATT))],
        out_shape=[jax.ShapeDtypeStruct((lp, D_ATT), F32), jax.ShapeDtypeStruct((lp, D_MIX), BF16)],
        input_output_aliases={2 * ATT_KC + 3: 1},
        compiler_params=_params(("parallel",)),
    )(qr, *([kr] * ATT_KC), *([vb] * ATT_KC), proj, sink_stack, mix)


def _attn_bwd(qr, kr, vb, att, proj, dmix, sink_stack):
    lp = qr.shape[0]
    nc = lp // CHUNK
    assert nc % ATT_QC == 0
    npairs = nc // ATT_QC
    qrows = ATT_QC * CHUNK
    wrows = ATT_KC * CHUNK

    def body(q_ref, *rest):
        k_refs, v_refs = rest[:ATT_KC], rest[ATT_KC:2 * ATT_KC]
        (att_ref, g_ref, do_ref, sink_ref, dq_ref, dk_ref, dv_ref, dg_ref, dsink_ref, dk_acc, dv_acc) = rest[2 * ATT_KC:]
        step = pl.program_id(0)

        @pl.when(step == 0)
        def _():
            dk_acc[...] = jnp.zeros_like(dk_acc)
            dv_acc[...] = jnp.zeros_like(dv_acc)
            dsink_ref[...] = jnp.zeros_like(dsink_ref)

        @pl.when(step < npairs)
        def _():
            q = q_ref[...]
            ks = [r[...] for r in k_refs]
            vs = [r[...] for r in v_refs]
            att = att_ref[...]
            g = g_ref[...]
            dog = do_ref[...]
            sg = _sigmoid(g)
            dg_ref[...] = dog * att * (sg * (1.0 + g * (1.0 - sg)))
            do = dog * (g * sg)
            units = [(u, h) for u in range(ATT_QC) for h in range(KV_HEADS)]
            n = range(len(units))
            rows = [slice(u * CHUNK, (u + 1) * CHUNK) for u in range(ATT_QC)]
            qs = [_stack_heads(q[rows[u], :], h) for u, h in units]
            kb = [_band(*ks[u:u + 3], h) for u, h in units]
            vbh = [_band(*vs[u:u + 3], h) for u, h in units]
            dos = [_stack_heads(do[rows[u], :], h) for u, h in units]
            dosb = [dos[i].astype(BF16) for i in n]
            s = [_dot(qs[i], kb[i], NT) for i in n]
            dp = [_dot(dosb[i], vbh[i], NT) for i in n]
            ds, pnb = [], []
            for i, (u, h) in enumerate(units):
                pn, psink = _attn_probs(s[i], sink_ref[h], ATT_QC * step + u)
                delta = jnp.sum(dos[i] * _stack_heads(att[rows[u], :], h), axis=-1, keepdims=True)
                ds.append((pn * (dp[i] - delta)).astype(BF16))
                pnb.append(pn.astype(BF16))
                dsink_ref[h] += -(psink * delta)
            dqs = [_dot(ds[i], kb[i]) for i in n]
            dks = [_dot(ds[i], qs[i], TN) for i in n]
            dvs = [_dot(pnb[i], dosb[i], TN) for i in n]
            dq_ref[...] = jnp.concatenate(
                [jnp.concatenate([dqs[u * KV_HEADS + h][r * CHUNK:(r + 1) * CHUNK, :]
                                  for h in range(KV_HEADS) for r in range(REP)], axis=1) for u in range(ATT_QC)],
                axis=0) * (HEAD_DIM ** -0.5)
            for u in range(ATT_QC):
                band = slice(u * CHUNK, u * CHUNK + BAND)
                dk_acc[band, :] += jnp.concatenate(dks[u * KV_HEADS:(u + 1) * KV_HEADS], axis=1) * (HEAD_DIM ** -0.5)
                dv_acc[band, :] += jnp.concatenate(dvs[u * KV_HEADS:(u + 1) * KV_HEADS], axis=1)

        dk_ref[...] = dk_acc[0:qrows, :]
        dv_ref[...] = dv_acc[0:qrows, :]
        for acc in (dk_acc, dv_acc):
            rest_rows = acc[qrows:wrows, :]
            acc[0:wrows - qrows, :] = rest_rows
            acc[wrows - qrows:wrows, :] = jnp.zeros((qrows, D_KV), F32)

    qp = lambda p: jnp.minimum(p, npairs - 1)
    newest = lambda p: ATT_QC * qp(p) + ATT_QC - 1
    qblk = pl.BlockSpec((qrows, D_ATT), lambda p: (qp(p), 0))
    oldest = pl.BlockSpec((qrows, D_KV), lambda p: (jnp.maximum(p - 1, 0), 0))
    return pl.pallas_call(
        body, name="attn_bwd", grid=(npairs + 1,),
        in_specs=[qblk] + _kv_specs(D_KV, newest) + _kv_specs(D_KV, newest) + [
            qblk,
            pl.BlockSpec((qrows, D_ATT), lambda p: (qp(p), OFF_G // D_ATT)),
            pl.BlockSpec((qrows, D_ATT), lambda p: (qp(p), D_SSD // D_ATT)),
            pl.BlockSpec((KV_HEADS, REP * CHUNK, 1), lambda p: (0, 0, 0)),
        ],
        out_specs=[qblk, oldest, oldest, qblk, pl.BlockSpec((KV_HEADS, REP * CHUNK, 1), lambda p: (0, 0, 0))],
        out_shape=[
            jax.ShapeDtypeStruct((lp, D_ATT), F32),
            jax.ShapeDtypeStruct((lp, D_KV), F32),
            jax.ShapeDtypeStruct((lp, D_KV), F32),
            jax.ShapeDtypeStruct((lp, D_ATT), F32),
            jax.ShapeDtypeStruct((KV_HEADS, REP * CHUNK, 1), F32),
        ],
        scratch_shapes=[pltpu.VMEM((wrows, D_KV), F32), pltpu.VMEM((wrows, D_KV), F32)],
        compiler_params=_params(("arbitrary",)),
    )(qr, *([kr] * ATT_KC), *([vb] * ATT_KC), att, proj, dmix, sink_stack)


def _post_loss(out, x2d, target, w):
    lp = out.shape[0]
    nc = lp // CHUNK
    nx = x2d.shape[0] // CHUNK

    k = _tile(nc, (ROW_K, 3, 2, 1))

    def body(o_ref, *rest):
        x_refs, t_refs = rest[:k], rest[k:2 * k]
        w_ref, do_ref, dy_ref, gw_ref, loss_ref = rest[2 * k:]
        i = pl.program_id(0)

        @pl.when(i == 0)
        def _():
            gw_ref[...] = jnp.zeros_like(gw_ref)
            loss_ref[...] = jnp.zeros_like(loss_ref)

        o = o_ref[...]
        w = w_ref[...]
        rstd = lax.rsqrt(jnp.mean(o * o, axis=-1, keepdims=True) + EPS)
        xhat = o * rstd
        x = jnp.concatenate([r[...] for r in x_refs], axis=0)
        t = jnp.concatenate([r[...] for r in t_refs], axis=0)
        chunk = _chunk_of_row(i, k)
        err = (x + xhat * w - t) * ((chunk > 0) & (chunk <= nx)).astype(F32)
        loss_ref[...] += 0.5 * jnp.sum(jnp.mean(err * err, axis=-1, keepdims=True), axis=0, keepdims=True)
        dy = err * (1.0 / D_MODEL)
        dy_ref[...] = dy
        gw_ref[...] += jnp.sum(dy * xhat, axis=0, keepdims=True)
        dxhat = dy * w
        do_ref[...] = (rstd * (dxhat - xhat * jnp.mean(dxhat * xhat, axis=-1, keepdims=True))).astype(BF16)

    row = pl.BlockSpec((k * CHUNK, D_MODEL), lambda i: (i, 0))
    return pl.pallas_call(
        body, name="post_loss", grid=(nc // k,),
        in_specs=[row] + _x_specs(nx, k) + _x_specs(nx, k) + [pl.BlockSpec((1, D_MODEL), lambda i: (0, 0))],
        out_specs=[row, row, pl.BlockSpec((1, D_MODEL), lambda i: (0, 0)), pl.BlockSpec((1, 128), lambda i: (0, 0))],
        out_shape=[
            jax.ShapeDtypeStruct((lp, D_MODEL), BF16),
            jax.ShapeDtypeStruct((lp, D_MODEL), F32),
            jax.ShapeDtypeStruct((1, D_MODEL), F32),
            jax.ShapeDtypeStruct((1, 128), F32),
        ],
        compiler_params=_params(("arbitrary",)),
    )(out, *([x2d] * k), *([target] * k), w)


def _prenorm_bwd(dhn, x2d, h0, dy, w):
    nx = x2d.shape[0] // CHUNK
    k = _tile(nx, (X_K, 4, 2, 1))
    rows = k * CHUNK

    def backward(h, dhn, w):
        rstd = lax.rsqrt(jnp.mean(h * h, axis=-1, keepdims=True) + EPS)
        xhat = h * rstd
        dxhat = dhn * w
        dh = rstd * (dxhat - xhat * jnp.mean(dxhat * xhat, axis=-1, keepdims=True))
        return dh, jnp.sum(dhn * xhat, axis=0, keepdims=True)

    def body(*refs):
        dhn_refs, dy_refs = refs[:k], refs[k:2 * k]
        x_ref, w_ref, gx_ref, gw_ref = refs[2 * k:]

        @pl.when(pl.program_id(0) == 0)
        def _():
            gw_ref[...] = jnp.zeros_like(gw_ref)

        dh, gw = backward(x_ref[...], jnp.concatenate([r[...] for r in dhn_refs], axis=0), w_ref[...])
        gx_ref[...] = dh + jnp.concatenate([r[...] for r in dy_refs], axis=0)
        gw_ref[...] += gw

    def body_meta(dhn_ref, h0_ref, w_ref, d0_ref, gw_ref):
        d0_ref[...], gw_ref[...] = backward(h0_ref[...], dhn_ref[...], w_ref[...])

    chunk_specs = [pl.BlockSpec((CHUNK, D_MODEL), functools.partial(lambda u, i: (k * i + 1 + u, 0), u)) for u in range(k)]
    first = pl.BlockSpec((CHUNK, D_MODEL), lambda i: (0, 0))
    vec = pl.BlockSpec((1, D_MODEL), lambda i: (0, 0))
    wide = pl.BlockSpec((rows, D_MODEL), lambda i: (i, 0))
    gx, gw_x = pl.pallas_call(
        body, name="prenorm_bwd", grid=(nx // k,),
        in_specs=chunk_specs + chunk_specs + [wide, vec],
        out_specs=[wide, vec],
        out_shape=[jax.ShapeDtypeStruct((nx * CHUNK, D_MODEL), F32), jax.ShapeDtypeStruct((1, D_MODEL), F32)],
        compiler_params=_params(("arbitrary",)),
    )(*([dhn] * k), *([dy] * k), x2d, w)
    d0, gw_0 = pl.pallas_call(
        body_meta, name="prenorm_bwd_meta", grid=(1,),
        in_specs=[first, first, vec], out_specs=[first, vec],
        out_shape=[jax.ShapeDtypeStruct((CHUNK, D_MODEL), F32), jax.ShapeDtypeStruct((1, D_MODEL), F32)],
        compiler_params=_params(("arbitrary",)),
    )(dhn, h0, w)
    return gx, d0, gw_x + gw_0


def _adamw(slabs, w, m, v, name):
    rows, cols = w.shape
    tr = _tile(rows, (256, 128, 64, 16, 8))
    c1 = 1.0 - ADAM_B1 ** ADAM_STEP
    c2 = 1.0 - ADAM_B2 ** ADAM_STEP

    def body(s_ref, w_ref, m_ref, v_ref, g_ref, d_ref, mo_ref, vo_ref):
        g = s_ref[0].astype(F32)
        for k in range(1, slabs.shape[0]):
            g = g + s_ref[k].astype(F32)
        w = w_ref[...]
        m = ADAM_B1 * m_ref[...] + (1.0 - ADAM_B1) * g
        v = ADAM_B2 * v_ref[...] + (1.0 - ADAM_B2) * (g * g)
        g_ref[...] = g
        mo_ref[...] = m
        vo_ref[...] = v
        d_ref[...] = -ADAM_LR * ((m / c1) / (jnp.sqrt(v / c2) + ADAM_EPS) + ADAM_WD * w)

    blk = pl.BlockSpec((tr, cols), lambda i: (i, 0))
    shape = jax.ShapeDtypeStruct((rows, cols), F32)
    return pl.pallas_call(
        body, name=name, grid=(rows // tr,),
        in_specs=[pl.BlockSpec((slabs.shape[0], tr, cols), lambda i: (0, i, 0)), blk, blk, blk],
        out_specs=[blk, blk, blk, blk],
        out_shape=[shape, shape, shape, shape],
        compiler_params=_params(("parallel",)),
    )(slabs, w, m, v)


def _perm_xbc(a):
    lead = a.shape[:-1]
    xs = a[..., :D_SSD].reshape(lead + (N_GROUPS, GROUP_W))
    b = a[..., D_SSD:D_SSD + N_GROUPS * D_STATE].reshape(lead + (N_GROUPS, D_STATE))
    c = a[..., D_SSD + N_GROUPS * D_STATE:].reshape(lead + (N_GROUPS, D_STATE))
    return jnp.concatenate([xs, b, c], axis=-1).reshape(lead + (D_CONV,))


def _unperm_xbc(a):
    lead = a.shape[:-1]
    t = a.reshape(lead + (N_GROUPS, XBC_BLK))
    xs = t[..., :GROUP_W].reshape(lead + (D_SSD,))
    b = t[..., GROUP_W:GROUP_W + D_STATE].reshape(lead + (N_GROUPS * D_STATE,))
    c = t[..., GROUP_W + D_STATE:].reshape(lead + (N_GROUPS * D_STATE,))
    return jnp.concatenate([xs, b, c], axis=-1)


R_Z, R_XBC, R_DT, R_Q, R_K, R_V, R_G = 0, 2048, 6144, 6176, 7200, 7456, 7712


def _internal_of_reference():
    ref = np.arange(D_IN_PROJ)
    out = np.empty(D_IN_PROJ, np.int64)
    out[R_Z:R_XBC] = OFF_Z + ref[:D_SSD]
    xs = np.arange(D_SSD)
    out[R_XBC:R_XBC + D_SSD] = OFF_XBC + (xs // GROUP_W) * XBC_BLK + xs % GROUP_W
    bc = np.arange(N_GROUPS * D_STATE)
    out[R_XBC + D_SSD:R_XBC + D_SSD + N_GROUPS * D_STATE] = OFF_XBC + (bc // D_STATE) * XBC_BLK + GROUP_W + bc % D_STATE
    out[R_XBC + D_SSD + N_GROUPS * D_STATE:R_DT] = OFF_XBC + (bc // D_STATE) * XBC_BLK + GROUP_W + D_STATE + bc % D_STATE
    out[R_DT:R_Q] = OFF_DT + np.arange(SSD_HEADS)
    out[R_Q:R_K] = OFF_Q + np.arange(D_ATT)
    out[R_K:R_V] = OFF_K + np.arange(D_KV)
    out[R_V:R_G] = OFF_V + np.arange(D_KV)
    out[R_G:] = OFF_G + np.arange(D_ATT)
    return out


def _runs(src, dst_break):
    runs, lo = [], 0
    for i in range(1, len(src) + 1):
        if i == len(src) or src[i] != src[i - 1] + 1 or dst_break[i] != dst_break[i - 1]:
            runs.append((lo, i))
            lo = i
    return runs


def _w_in_internal(w_gathered):
    int_of_ref = _internal_of_reference()
    ref_of_int = np.full(NP, -1, np.int64)
    ref_of_int[int_of_ref] = np.arange(D_IN_PROJ)
    shard = np.where(ref_of_int >= 0, ref_of_int // SHARD_IN, -1)
    src = np.where(ref_of_int >= 0, ref_of_int, -10 - 2 * np.arange(NP))
    pieces, zeros = [], 0
    for lo, hi in _runs(src, shard):
        if ref_of_int[lo] < 0:
            zeros += hi - lo
            continue
        if zeros:
            pieces.append(jnp.zeros((D_MODEL, zeros), w_gathered.dtype))
            zeros = 0
        s, c0 = ref_of_int[lo] // SHARD_IN, ref_of_int[lo] % SHARD_IN
        pieces.append(w_gathered[s, :, c0:c0 + hi - lo])
    if zeros:
        pieces.append(jnp.zeros((D_MODEL, zeros), w_gathered.dtype))
    return jnp.concatenate(pieces, axis=1)


def _w_in_slabs(dw):
    int_of_ref = _internal_of_reference()
    slabs = []
    for s in range(N_DEV):
        cols = int_of_ref[s * SHARD_IN:(s + 1) * SHARD_IN]
        slabs.append(jnp.concatenate([dw[:, cols[lo]:cols[lo] + hi - lo]
                                      for lo, hi in _runs(cols, np.zeros_like(cols))], axis=1))
    return jnp.stack(slabs, axis=0)


def _rep_heads(a):
    return jnp.repeat(a, HEAD_DIM, axis=1)


SMALL = (("norm_pre_w", 2048), ("conv_b", 4096), ("dt_bias", 32), ("a_log", 32), ("d_skip", 32),
         ("ssd_norm_w", 2048), ("attn_sinks", 16), ("norm_post_w", 2048))
SMALL_USED = sum(size for _, size in SMALL)
SMALL_LEN = 10368


def _pack_small(d, loss=None):
    parts = [d[name].reshape(1, size) for name, size in SMALL]
    tail = jnp.zeros((1, SMALL_LEN - SMALL_USED), F32)
    if loss is not None:
        tail = tail.at[0, 0].set(loss)
    return jnp.concatenate(parts + [tail], axis=1)


def _unpack_small(vec):
    out, off = {}, 0
    for name, size in SMALL:
        out[name] = vec[:, off:off + size]
        off += size
    return out


def kernel(x, meta_tokens, norm_pre_w, w_in, conv_w, conv_b, dt_bias, a_log, d_skip, ssd_norm_w, attn_sinks, w_out, norm_post_w, loss_target, m_meta_tokens, m_norm_pre_w, m_w_in, m_conv_w, m_conv_b, m_dt_bias, m_a_log, m_d_skip, m_ssd_norm_w, m_attn_sinks, m_w_out, m_norm_post_w, v_meta_tokens, v_norm_pre_w, v_w_in, v_conv_w, v_conv_b, v_dt_bias, v_a_log, v_d_skip, v_ssd_norm_w, v_attn_sinks, v_w_out, v_norm_post_w):
    seq = x.shape[1]
    lp = seq + 2 * CHUNK
    x2d = x[0]

    w_in_g, conv_w_g, meta_g = _gather_two_level([w_in[0].astype(BF16), conv_w[0], meta_tokens], "gather_w_in")
    w_all = _w_in_internal(w_in_g)
    conv_w_full = _perm_xbc(jnp.transpose(conv_w_g, (1, 0, 2)).reshape(CONV_WIDTH, D_CONV))
    conv_b_int = _perm_xbc(conv_b)
    meta_full = jnp.transpose(meta_g, (1, 0, 2)).reshape(N_META, D_MODEL)
    h0 = jnp.concatenate([jnp.zeros((PAD_LEAD, D_MODEL), F32), meta_full], axis=0)

    pos = (jnp.arange(lp) - PAD_LEAD).astype(F32)
    half = HEAD_DIM // 2
    inv = ROPE_THETA ** (-jnp.arange(half, dtype=F32) / half)
    ang = pos[:, None] * inv[None, :]
    cos_t = jnp.tile(jnp.cos(ang), (1, 4))
    sin_t = jnp.tile(jnp.concatenate([-jnp.sin(ang), jnp.sin(ang)], axis=1), (1, 2))
    head_of_col = np.arange(D_SSD) // HEAD_DIM
    expand = jnp.asarray((np.arange(128)[:, None] == head_of_col[None, :]).astype(np.float32))
    reduce_t = jnp.asarray((head_of_col[:, None] == np.arange(128)[None, :]).astype(np.float32))
    tri = jnp.asarray(np.tril(np.ones((CHUNK, CHUNK), np.float32)))
    a_rep = _rep_heads(-jnp.exp(a_log))
    dsk_rep = _rep_heads(d_skip)
    dt_bias_rep = _rep_heads(dt_bias)
    sink_stack = jnp.repeat(attn_sinks.reshape(KV_HEADS, REP), CHUNK, axis=1).reshape(KV_HEADS, REP * CHUNK, 1)

    hn = _prenorm(x2d, h0, norm_pre_w)
    tm = _tile(lp, (1056, 704, 128, 64))
    proj, w_out_g = _matmul(hn, w_all, tm=tm, tn=1536, tk=D_MODEL, out_dtype=F32, name="in_proj",
                            comm=_Comm([(w_out[0].astype(BF16), "gather")]))
    w_out_full = w_out_g.reshape(D_MIX, D_MODEL)
    xbc = _conv_fwd(proj, conv_w_full, conv_b_int)
    qr, kr, vb, dt_rep = _act_fwd(proj, cos_t, sin_t, expand, dt_bias_rep)
    mix, ytot, hprev = _ssd_fwd(xbc, dt_rep, proj, a_rep, dsk_rep, ssd_norm_w, tri)
    att, mix = _attn_fwd(qr, kr, vb, proj, sink_stack, mix)
    out = _matmul(mix, w_out_full, tm=tm, tn=1024, tk=D_MIX, out_dtype=F32, name="out_proj")
    dout, dy, g_norm_post, loss_part = _post_loss(out, x2d, loss_target[0], norm_post_w)

    dmix = _matmul(dout, w_out_full, trans_b=True, tm=tm, tn=1024, tk=D_MODEL, out_dtype=F32, name="dmix")
    dw_out = _matmul(mix, dout, trans_a=True, tm=512, tn=1024, tk=lp, out_dtype=BF16, name="dw_out")
    dqr, dkr, dv, dg, dsink_rows = _attn_bwd(qr, kr, vb, att, proj, dmix, sink_stack)
    dproj, dxbc, ddt_part, dd_part, da_part, g_ssd_norm, g_out = _ssd_bwd(
        dmix, ytot, proj, xbc, dt_rep, hprev, a_rep, dsk_rep, ssd_norm_w, tri,
        _Comm([(dw_out.reshape(N_DEV, D_MIX // N_DEV, D_MODEL), "scatter")]))
    dproj, dconv_w_int, dconv_b_int = _conv_bwd(dxbc, proj, conv_w_full, conv_b_int, dproj)
    dproj, ddt_bias = _act_bwd(dqr, dkr, dv, dg, ddt_part, cos_t, sin_t, reduce_t, dproj)
    dw_all = _matmul(hn, dproj, trans_a=True, tm=512, tn=1024, tk=lp, out_dtype=BF16, name="dw_in")
    dw_slabs = _w_in_slabs(dw_all)
    dw_chip = _pair_sum(dw_slabs, _exchange_sibling(dw_slabs, "dw_in_sibling"), "dw_in_pair_sum")
    dhn, g_in = _matmul(dproj, w_all, trans_b=True, tm=tm, tn=1024, tk=1536, out_dtype=F32, name="dhn",
                        comm=_Comm([(dw_chip, "scatter")], scope="chips"))
    grad_x, dh0, g_norm_pre = _prenorm_bwd(dhn, x2d, h0, dy, norm_pre_w)

    dmeta = dh0[PAD_LEAD:, :]
    dconv_w_ref = _unperm_xbc(dconv_w_int)
    heads = lambda part: part.reshape(SSD_HEADS, HEAD_DIM).sum(axis=1).reshape(1, SSD_HEADS)
    small_local = _pack_small({
        "norm_pre_w": g_norm_pre, "conv_b": _unperm_xbc(dconv_b_int), "dt_bias": ddt_bias[:, :SSD_HEADS],
        "a_log": heads(da_part) * (-jnp.exp(a_log)), "d_skip": heads(dd_part), "ssd_norm_w": g_ssd_norm,
        "attn_sinks": dsink_rows.reshape(Q_HEADS, CHUNK).sum(axis=1).reshape(1, Q_HEADS),
        "norm_post_w": g_norm_post}, loss=loss_part[0, 0])
    g_conv, g_meta, g_small = _exchange(
        [(jnp.transpose(dconv_w_ref.reshape(CONV_WIDTH, N_DEV, D_CONV // N_DEV), (1, 0, 2)), "scatter"),
         (jnp.transpose(dmeta.reshape(N_META, N_DEV, D_MODEL // N_DEV), (1, 0, 2)), "scatter"),
         (small_local, "gather")], "exchange_small")

    res = {}
    res["w_in"] = [o[None] for o in _adamw(g_in, w_in[0], m_w_in[0], v_w_in[0], "adamw_w_in")]
    res["w_out"] = [o[None] for o in _adamw(g_out, w_out[0], m_w_out[0], v_w_out[0], "adamw_w_out")]
    res["conv_w"] = [o[None] for o in _adamw(g_conv, conv_w[0], m_conv_w[0], v_conv_w[0], "adamw_conv_w")]
    res["meta_tokens"] = _adamw(g_meta, meta_tokens, m_meta_tokens, v_meta_tokens, "adamw_meta")
    given = dict(norm_pre_w=(norm_pre_w, m_norm_pre_w, v_norm_pre_w), conv_b=(conv_b, m_conv_b, v_conv_b),
                 dt_bias=(dt_bias, m_dt_bias, v_dt_bias), a_log=(a_log, m_a_log, v_a_log),
                 d_skip=(d_skip, m_d_skip, v_d_skip), ssd_norm_w=(ssd_norm_w, m_ssd_norm_w, v_ssd_norm_w),
                 attn_sinks=(attn_sinks, m_attn_sinks, v_attn_sinks),
                 norm_post_w=(norm_post_w, m_norm_post_w, v_norm_post_w))
    packed = [_pack_small({k: t[j] for k, t in given.items()}) for j in range(3)]
    small_out = _adamw(g_small, packed[0], packed[1], packed[2], "adamw_small")
    small_res = [_unpack_small(r) for r in small_out]
    loss = small_out[0][0, SMALL_USED]

    order = ["meta_tokens", "norm_pre_w", "w_in", "conv_w", "conv_b", "dt_bias", "a_log", "d_skip", "ssd_norm_w",
             "attn_sinks", "w_out", "norm_post_w"]
    outs = []
    for j in range(4):
        for name in order:
            outs.append(res[name][j] if name in res else small_res[j][name])
    return (loss, grad_x[None], *outs)
```

```python
import functools

import numpy as np
import jax
import jax.numpy as jnp
from jax import lax
from jax.experimental import pallas as pl
from jax.experimental.pallas import tpu as pltpu

F32 = jnp.float32
BF16 = jnp.bfloat16
HIGHEST = lax.Precision.HIGHEST

N_DEV = 8
D_MODEL = 2048
CHUNK = 64
N_META = 16
PAD_LEAD = CHUNK - N_META
EPS = 1e-6
N_GROUPS = 8
HEADS_PER_GROUP = 4
HEAD_DIM = 64
GROUP_W = HEADS_PER_GROUP * HEAD_DIM
D_STATE = 128
D_SSD = 2048
D_CONV = 4096
SSD_HEADS = 32
CONV_WIDTH = 4
Q_HEADS = 16
KV_HEADS = 4
REP = 4
D_ATT = 1024
D_KV = 256
WINDOW_CHUNKS = 2
BAND = (WINDOW_CHUNKS + 1) * CHUNK
ROPE_THETA = 10000.0
D_MIX = D_SSD + D_ATT
D_IN_PROJ = 8736
SHARD_IN = D_IN_PROJ // N_DEV

OFF_Z, OFF_XBC, OFF_Q, OFF_G, OFF_K, OFF_V, OFF_DT = 0, 2048, 6144, 7168, 8192, 8448, 8704
NP = 9216
TAIL_W = NP - OFF_Q
XBC_BLK = 512
SSD_GPS = 8
SSD_GPS_BWD = 8

ADAM_LR, ADAM_B1, ADAM_B2, ADAM_EPS, ADAM_WD, ADAM_STEP = 0.001, 0.9, 0.999, 1e-08, 0.01, 10

VMEM_LIMIT = 48 * 1024 * 1024

NN = (((1,), (0,)), ((), ()))
NT = (((1,), (1,)), ((), ()))
TN = (((0,), (0,)), ((), ()))
ANY = pl.BlockSpec(memory_space=pl.ANY)


def _dot(a, b, dims=NN, precision=None):
    return lax.dot_general(a, b, dims, precision=precision, preferred_element_type=F32)


def _tile(n, prefs):
    for t in prefs:
        if n % t == 0:
            return t
    return n


def _params(sem, vmem=VMEM_LIMIT, side_effects=False):
    return pltpu.CompilerParams(dimension_semantics=sem, vmem_limit_bytes=vmem, has_side_effects=side_effects)


def _sigmoid(x):
    return 1.0 / (1.0 + jnp.exp(-x))


class _Comm:
    def __init__(self, items, scope="devices"):
        self.items = items
        self.scope = scope
        self.slabs = slabs = N_DEV if scope == "devices" else N_DEV // 2
        self.n = n = len(items)
        self.operands = [arr for arr, _ in items]
        self.in_specs = [ANY] * n
        self.out_specs = [ANY] * n
        self.out_shape = [jax.ShapeDtypeStruct((slabs,) + tuple(arr.shape) if kind == "gather" else tuple(arr.shape),
                                               arr.dtype) for arr, kind in items]
        self.scratch = [pltpu.SemaphoreType.DMA((n, slabs - 1)), pltpu.SemaphoreType.DMA((n, slabs - 1)),
                        pltpu.SemaphoreType.DMA((n,))]

    def _places(self):
        pos = (lax.axis_index("x"), lax.axis_index("y"), lax.axis_index("c"))
        if self.scope == "devices":
            index = lambda p: 4 * p[0] + 2 * p[1] + p[2]
            masks = range(1, N_DEV)
        else:
            index = lambda p: 2 * p[0] + p[1]
            masks = (2, 4, 6)
        peers = []
        for k in masks:
            p = tuple(1 - pos[b] if (k >> (2 - b)) & 1 else pos[b] for b in range(3))
            peers.append((p, index(p)))
        return index(pos), peers

    def _copies(self, ins, outs, sems, landed):
        send_sems, recv_sems, local_sems = sems
        me, peers = self._places()
        local, remote = [], []
        for a, (_, kind) in enumerate(self.items):
            own = ins[a] if kind == "gather" else ins[a].at[me]
            local.append(pltpu.make_async_copy(own, outs[a].at[me], local_sems.at[a]))
            for k, (p, pid) in enumerate(peers):
                remote.append(pltpu.make_async_remote_copy(
                    src_ref=ins[a] if kind == "gather" else ins[a].at[pid],
                    dst_ref=outs[a].at[pid if landed else me],
                    send_sem=send_sems.at[a, k], recv_sem=recv_sems.at[a, k],
                    device_id=p, device_id_type=pl.DeviceIdType.MESH))
        return local, remote

    def start(self, ins, outs, sems):
        local, remote = self._copies(ins, outs, sems, landed=False)
        for cp in local + remote:
            cp.start()

    def wait(self, ins, outs, sems):
        local, remote = self._copies(ins, outs, sems, landed=True)
        for cp in remote + local:
            cp.wait()


def _exchange(items, name):
    comm = _Comm(items)
    n = comm.n

    def body(*refs):
        ins, outs, sems = refs[:n], refs[n:2 * n], refs[2 * n:]
        comm.start(ins, outs, sems)
        comm.wait(ins, outs, sems)

    return pl.pallas_call(
        body, name=name, in_specs=comm.in_specs, out_specs=comm.out_specs, out_shape=comm.out_shape,
        scratch_shapes=comm.scratch, compiler_params=pltpu.CompilerParams(has_side_effects=True),
    )(*comm.operands)


def _exchange_sibling(slabs, name):
    def body(g_ref, land_ref, send_sems, recv_sems):
        x, y, c = lax.axis_index("x"), lax.axis_index("y"), lax.axis_index("c")
        copies = [pltpu.make_async_remote_copy(
            src_ref=g_ref.at[2 * q + (1 - c)], dst_ref=land_ref.at[q], send_sem=send_sems.at[q], recv_sem=recv_sems.at[q],
            device_id=(x, y, 1 - c), device_id_type=pl.DeviceIdType.MESH) for q in range(N_DEV // 2)]
        for cp in copies:
            cp.start()
        for cp in copies:
            cp.wait()

    return pl.pallas_call(
        body, name=name, in_specs=[ANY], out_specs=ANY,
        out_shape=jax.ShapeDtypeStruct((N_DEV // 2,) + tuple(slabs.shape[1:]), slabs.dtype),
        scratch_shapes=[pltpu.SemaphoreType.DMA((N_DEV // 2,)), pltpu.SemaphoreType.DMA((N_DEV // 2,))],
        compiler_params=pltpu.CompilerParams(has_side_effects=True),
    )(slabs)


def _pair_sum(slabs, landed, name):
    _, rows, cols = slabs.shape
    tr = _tile(rows, (256, 128, 64, 16, 8))

    def body(g_ref, l_ref, o_ref):
        mine = g_ref[lax.axis_index("c")]
        o_ref[...] = (mine.astype(F32) + l_ref[...].astype(F32)).astype(o_ref.dtype)

    return pl.pallas_call(
        body, name=name, grid=(N_DEV // 2, rows // tr),
        in_specs=[pl.BlockSpec((None, 2, tr, cols), lambda q, i: (q, 0, i, 0)),
                  pl.BlockSpec((None, tr, cols), lambda q, i: (q, i, 0))],
        out_specs=pl.BlockSpec((None, tr, cols), lambda q, i: (q, i, 0)),
        out_shape=jax.ShapeDtypeStruct((N_DEV // 2, rows, cols), slabs.dtype),
        compiler_params=_params(("parallel", "parallel")),
    )(slabs.reshape(N_DEV // 2, 2, rows, cols), landed)


def _gather_two_level(arrays, name):
    n = len(arrays)

    def body(*refs):
        ins, outs = refs[:n], refs[n:2 * n]
        send_sems, recv_sems, local_sems = refs[2 * n:]
        x, y, c = lax.axis_index("x"), lax.axis_index("y"), lax.axis_index("c")
        me, sibling = (x, y, c), (x, y, 1 - c)
        chips = [(1 - x, y), (x, 1 - y), (1 - x, 1 - y)]

        def slab(a, place):
            return outs[a].at[4 * place[0] + 2 * place[1] + place[2]]

        def copy(a, k, block, to, src=None):
            return pltpu.make_async_remote_copy(
                src_ref=slab(a, block) if src is None else src, dst_ref=slab(a, block),
                send_sem=send_sems.at[a, k], recv_sem=recv_sems.at[a, k],
                device_id=to, device_id_type=pl.DeviceIdType.MESH)

        sends, mine = [], []
        for a in range(n):
            loc = pltpu.make_async_copy(ins[a], slab(a, me), local_sems.at[a])
            loc.start()
            mine.append(loc)
            sends.append(copy(a, 0, me, sibling, src=ins[a]))
            sends += [copy(a, 1 + j, me, (*chip, c), src=ins[a]) for j, chip in enumerate(chips)]
        for cp in sends:
            cp.start()
        for j, chip in enumerate(chips):
            for a in range(n):
                copy(a, 1 + j, (*chip, c), me).wait_recv()
                fwd = copy(a, 4 + j, (*chip, c), sibling)
                fwd.start()
                sends.append(fwd)
        for a in range(n):
            copy(a, 0, sibling, me).wait_recv()
            for j, chip in enumerate(chips):
                copy(a, 4 + j, (*chip, 1 - c), me).wait_recv()
        for cp in sends:
            cp.wait_send()
        for loc in mine:
            loc.wait()

    return pl.pallas_call(
        body, name=name, in_specs=[ANY] * n, out_specs=[ANY] * n,
        out_shape=[jax.ShapeDtypeStruct((N_DEV,) + tuple(a.shape), a.dtype) for a in arrays],
        scratch_shapes=[pltpu.SemaphoreType.DMA((n, N_DEV - 1)), pltpu.SemaphoreType.DMA((n, N_DEV - 1)),
                        pltpu.SemaphoreType.DMA((n,))],
        compiler_params=pltpu.CompilerParams(has_side_effects=True),
    )(*arrays)


def _matmul(a, b, *, tm, tn, tk, out_dtype, name, trans_a=False, trans_b=False, comm=None):
    m, k = (a.shape[1], a.shape[0]) if trans_a else a.shape
    n = b.shape[0] if trans_b else b.shape[1]
    nk = k // tk
    dims = TN if trans_a else (NT if trans_b else NN)
    assert not (trans_a and trans_b)
    nc = comm.n if comm else 0
    grid = (m // tm, n // tn, nk)

    def body(*refs):
        a_ref, b_ref = refs[:2]
        cin = refs[2:2 + nc]
        o_ref = refs[2 + nc]
        cout = refs[3 + nc:3 + 2 * nc]
        scratch = refs[3 + 2 * nc:]
        sems = scratch[len(scratch) - 3:] if comm else None
        i, j, kk = pl.program_id(0), pl.program_id(1), pl.program_id(2)
        if comm:
            @pl.when((i == 0) & (j == 0) & (kk == 0))
            def _():
                comm.start(cin, cout, sems)

        if nk == 1:
            o_ref[...] = _dot(a_ref[...], b_ref[...], dims).astype(out_dtype)
        else:
            acc_ref = scratch[0]

            @pl.when(kk == 0)
            def _():
                acc_ref[...] = jnp.zeros_like(acc_ref)

            acc_ref[...] += _dot(a_ref[...], b_ref[...], dims)

            @pl.when(kk == nk - 1)
            def _():
                o_ref[...] = acc_ref[...].astype(out_dtype)

        if comm:
            @pl.when((i == grid[0] - 1) & (j == grid[1] - 1) & (kk == nk - 1))
            def _():
                comm.wait(cin, cout, sems)

    a_spec = (pl.BlockSpec((tk, tm), lambda i, j, kk: (kk, i)) if trans_a
              else pl.BlockSpec((tm, tk), lambda i, j, kk: (i, kk)))
    b_spec = (pl.BlockSpec((tn, tk), lambda i, j, kk: (j, kk)) if trans_b
              else pl.BlockSpec((tk, tn), lambda i, j, kk: (kk, j)))
    sem = ("arbitrary",) * 3 if comm else ("parallel", "parallel", "arbitrary")
    res = pl.pallas_call(
        body, name=name, grid=grid,
        in_specs=[a_spec, b_spec] + (comm.in_specs if comm else []),
        out_specs=[pl.BlockSpec((tm, tn), lambda i, j, kk: (i, j))] + (comm.out_specs if comm else []),
        out_shape=[jax.ShapeDtypeStruct((m, n), out_dtype)] + (comm.out_shape if comm else []),
        scratch_shapes=([] if nk == 1 else [pltpu.VMEM((tm, tn), F32)]) + (comm.scratch if comm else []),
        compiler_params=_params(sem, side_effects=bool(comm)),
    )(a, b, *(comm.operands if comm else []))
    return res if comm else res[0]


ROW_K = 6
X_K = 8


def _x_specs(nx, k):
    return [pl.BlockSpec((CHUNK, D_MODEL), functools.partial(lambda u, i: (jnp.clip(k * i + u - 1, 0, nx - 1), 0), u))
            for u in range(k)]


def _chunk_of_row(i, k):
    return k * i + lax.broadcasted_iota(jnp.int32, (k * CHUNK, 1), 0) // CHUNK


def _prenorm(x2d, h0, w):
    nx = x2d.shape[0] // CHUNK
    nc = nx + 2
    k = _tile(nc, (ROW_K, 3, 2, 1))

    def body(*refs):
        x_refs = refs[:k]
        h0_ref, w_ref, o_ref = refs[k:]
        i = pl.program_id(0)
        x = jnp.concatenate([r[...] for r in x_refs], axis=0)
        head = jnp.concatenate([h0_ref[...], x[CHUNK:, :]], axis=0)
        chunk = _chunk_of_row(i, k)
        h = jnp.where(i == 0, head, x) * (chunk <= nx).astype(F32)
        rstd = lax.rsqrt(jnp.mean(h * h, axis=-1, keepdims=True) + EPS)
        o_ref[...] = (h * rstd * w_ref[...]).astype(BF16)

    rows = k * CHUNK
    return pl.pallas_call(
        body, name="prenorm", grid=(nc // k,),
        in_specs=_x_specs(nx, k) + [pl.BlockSpec((CHUNK, D_MODEL), lambda i: (0, 0)), pl.BlockSpec((1, D_MODEL), lambda i: (0, 0))],
        out_specs=pl.BlockSpec((rows, D_MODEL), lambda i: (i, 0)),
        out_shape=jax.ShapeDtypeStruct((nc * CHUNK, D_MODEL), BF16),
        compiler_params=_params(("parallel",)),
    )(*([x2d] * k), h0, w)


CONV_COLS = 512
HALO = 8


def _conv_pre(ext, w, b):
    taps = [ext[HALO:, :]] + [pltpu.roll(ext, j, 0)[HALO:, :] for j in range(1, CONV_WIDTH)]
    acc = b + w[3:4, :] * taps[0]
    for j in range(1, CONV_WIDTH):
        acc = acc + w[3 - j:4 - j, :] * taps[j]
    return acc, taps


def _conv_fwd(proj, conv_w, conv_b):
    lp = proj.shape[0]
    t = _tile(lp, (704, 384, 128, 64))
    hb = t // HALO
    c0 = OFF_XBC // CONV_COLS

    def body(u_ref, halo_ref, w_ref, b_ref, o_ref):
        i = pl.program_id(0)
        halo = halo_ref[...] * (i > 0).astype(F32)
        pre, _ = _conv_pre(jnp.concatenate([halo, u_ref[...]], axis=0), w_ref[...], b_ref[...])
        o_ref[...] = pre * _sigmoid(pre)

    return pl.pallas_call(
        body, name="conv_fwd", grid=(lp // t, D_CONV // CONV_COLS),
        in_specs=[
            pl.BlockSpec((t, CONV_COLS), lambda i, j: (i, c0 + j)),
            pl.BlockSpec((HALO, CONV_COLS), lambda i, j: (jnp.maximum(i * hb - 1, 0), c0 + j)),
            pl.BlockSpec((CONV_WIDTH, CONV_COLS), lambda i, j: (0, j)),
            pl.BlockSpec((1, CONV_COLS), lambda i, j: (0, j)),
        ],
        out_specs=pl.BlockSpec((t, CONV_COLS), lambda i, j: (i, j)),
        out_shape=jax.ShapeDtypeStruct((lp, D_CONV), F32),
        compiler_params=_params(("parallel", "parallel")),
    )(proj, proj, conv_w, conv_b)


def _conv_bwd(dxbc, proj, conv_w, conv_b, dproj):
    lp = proj.shape[0]
    t = _tile(lp, (704, 384, 128, 64))
    hb = t // HALO
    nt = lp // t
    c0 = OFF_XBC // CONV_COLS

    def body(dx_ref, dxn_ref, u_ref, up_ref, un_ref, w_ref, b_ref, _, du_ref, dw_ref, db_ref):
        i = pl.program_id(1)
        w = w_ref[...]
        up = up_ref[...] * (i > 0).astype(F32)
        ext = jnp.concatenate([up, u_ref[...], un_ref[...]], axis=0)
        pre, taps = _conv_pre(ext, w, b_ref[...])
        dxn = dxn_ref[...] * (i < nt - 1).astype(F32)
        dxe = jnp.concatenate([dx_ref[...], dxn], axis=0)
        sg = _sigmoid(pre)
        dpre = dxe * sg * (1.0 + pre * (1.0 - sg))
        du = w[3:4, :] * dpre[:t, :]
        for j in range(1, CONV_WIDTH):
            du = du + w[3 - j:4 - j, :] * pltpu.roll(dpre, t + HALO - j, 0)[:t, :]
        du_ref[...] = du.astype(BF16)

        @pl.when(i == 0)
        def _():
            dw_ref[...] = jnp.zeros_like(dw_ref)
            db_ref[...] = jnp.zeros_like(db_ref)

        dp = dpre[:t, :]
        db_ref[...] += jnp.sum(dp, axis=0, keepdims=True)
        for j in range(CONV_WIDTH):
            dw_ref[3 - j:4 - j, :] += jnp.sum(dp * taps[j][:t, :], axis=0, keepdims=True)

    nxt = lambda i: jnp.minimum((i + 1) * hb, lp // HALO - 1)
    return pl.pallas_call(
        body, name="conv_bwd", grid=(D_CONV // CONV_COLS, nt),
        in_specs=[
            pl.BlockSpec((t, CONV_COLS), lambda j, i: (i, j)),
            pl.BlockSpec((HALO, CONV_COLS), lambda j, i: (nxt(i), j)),
            pl.BlockSpec((t, CONV_COLS), lambda j, i: (i, c0 + j)),
            pl.BlockSpec((HALO, CONV_COLS), lambda j, i: (jnp.maximum(i * hb - 1, 0), c0 + j)),
            pl.BlockSpec((HALO, CONV_COLS), lambda j, i: (nxt(i), c0 + j)),
            pl.BlockSpec((CONV_WIDTH, CONV_COLS), lambda j, i: (0, j)),
            pl.BlockSpec((1, CONV_COLS), lambda j, i: (0, j)),
            ANY,
        ],
        out_specs=[
            pl.BlockSpec((t, CONV_COLS), lambda j, i: (i, c0 + j)),
            pl.BlockSpec((CONV_WIDTH, CONV_COLS), lambda j, i: (0, j)),
            pl.BlockSpec((1, CONV_COLS), lambda j, i: (0, j)),
        ],
        out_shape=[
            jax.ShapeDtypeStruct((lp, NP), BF16),
            jax.ShapeDtypeStruct((CONV_WIDTH, D_CONV), F32),
            jax.ShapeDtypeStruct((1, D_CONV), F32),
        ],
        input_output_aliases={7: 0},
        compiler_params=_params(("parallel", "arbitrary")),
    )(dxbc, dxbc, proj, proj, proj, conv_w, conv_b, dproj)


def _swap_halves(t):
    w = t.shape[-1]
    lane = lax.broadcasted_iota(jnp.int32, t.shape, 1)
    return jnp.where((lane % HEAD_DIM) < HEAD_DIM // 2, pltpu.roll(t, w - HEAD_DIM // 2, 1),
                     pltpu.roll(t, HEAD_DIM // 2, 1))


def _act_fwd(proj, cos_t, sin_t, expand, dt_bias_rep):
    lp = proj.shape[0]
    t = _tile(lp, (384, 128, 64))

    def body(q_ref, k_ref, v_ref, dt_ref, cos_ref, sin_ref, ex_ref, bias_ref, qo_ref, ko_ref, vo_ref, dto_ref):
        i = pl.program_id(0)
        cos = cos_ref[...]
        sin = sin_ref[...]
        q = q_ref[...]
        qo_ref[...] = (q * jnp.tile(cos, (1, D_ATT // 128)) + _swap_halves(q) * jnp.tile(sin, (1, D_ATT // 128))).astype(BF16)
        k = k_ref[...]
        ko_ref[...] = (k * jnp.tile(cos, (1, D_KV // 128)) + _swap_halves(k) * jnp.tile(sin, (1, D_KV // 128))).astype(BF16)
        vo_ref[...] = v_ref[...].astype(BF16)
        raw = _dot(dt_ref[...], ex_ref[...], NN, HIGHEST) + bias_ref[...]
        sp = jnp.maximum(raw, 0.0) + jnp.log1p(jnp.exp(-jnp.abs(raw)))
        row = i * t + lax.broadcasted_iota(jnp.int32, sp.shape, 0)
        dto_ref[...] = jnp.where(row >= PAD_LEAD, sp, 0.0)

    return pl.pallas_call(
        body, name="act_fwd", grid=(lp // t,),
        in_specs=[
            pl.BlockSpec((t, D_ATT), lambda i: (i, OFF_Q // D_ATT)),
            pl.BlockSpec((t, D_KV), lambda i: (i, OFF_K // D_KV)),
            pl.BlockSpec((t, D_KV), lambda i: (i, OFF_V // D_KV)),
            pl.BlockSpec((t, 128), lambda i: (i, OFF_DT // 128)),
            pl.BlockSpec((t, 128), lambda i: (i, 0)),
            pl.BlockSpec((t, 128), lambda i: (i, 0)),
            pl.BlockSpec((128, D_SSD), lambda i: (0, 0)),
            pl.BlockSpec((1, D_SSD), lambda i: (0, 0)),
        ],
        out_specs=[
            pl.BlockSpec((t, D_ATT), lambda i: (i, 0)),
            pl.BlockSpec((t, D_KV), lambda i: (i, 0)),
            pl.BlockSpec((t, D_KV), lambda i: (i, 0)),
            pl.BlockSpec((t, D_SSD), lambda i: (i, 0)),
        ],
        out_shape=[
            jax.ShapeDtypeStruct((lp, D_ATT), BF16),
            jax.ShapeDtypeStruct((lp, D_KV), BF16),
            jax.ShapeDtypeStruct((lp, D_KV), BF16),
            jax.ShapeDtypeStruct((lp, D_SSD), F32),
        ],
        compiler_params=_params(("parallel",)),
    )(proj, proj, proj, proj, cos_t, sin_t, expand, dt_bias_rep)


def _act_bwd(dqr, dkr, dv, dg, ddt_part, cos_t, sin_t, reduce_t, dproj):
    lp = dqr.shape[0]
    t = _tile(lp, (384, 128, 64))

    def body(dq_ref, dk_ref, dv_ref, dg_ref, ddt_ref, cos_ref, sin_ref, red_ref, _, o_ref, db_ref):
        i = pl.program_id(0)
        cos = cos_ref[...]
        sin = sin_ref[...]
        dq = dq_ref[...]
        dq = dq * jnp.tile(cos, (1, D_ATT // 128)) + _swap_halves(dq * jnp.tile(sin, (1, D_ATT // 128)))
        dk = dk_ref[...]
        dk = dk * jnp.tile(cos, (1, D_KV // 128)) + _swap_halves(dk * jnp.tile(sin, (1, D_KV // 128)))
        ddt = _dot(ddt_ref[...], red_ref[...], NN, HIGHEST)
        o_ref[...] = jnp.concatenate(
            [dq.astype(BF16), dg_ref[...].astype(BF16), dk.astype(BF16), dv_ref[...].astype(BF16), ddt.astype(BF16),
             jnp.zeros((t, NP - OFF_DT - 128), BF16)], axis=1)

        @pl.when(i == 0)
        def _():
            db_ref[...] = jnp.zeros_like(db_ref)

        db_ref[...] += jnp.sum(ddt, axis=0, keepdims=True)

    return pl.pallas_call(
        body, name="act_bwd", grid=(lp // t,),
        in_specs=[
            pl.BlockSpec((t, D_ATT), lambda i: (i, 0)),
            pl.BlockSpec((t, D_KV), lambda i: (i, 0)),
            pl.BlockSpec((t, D_KV), lambda i: (i, 0)),
            pl.BlockSpec((t, D_ATT), lambda i: (i, 0)),
            pl.BlockSpec((t, D_SSD), lambda i: (i, 0)),
            pl.BlockSpec((t, 128), lambda i: (i, 0)),
            pl.BlockSpec((t, 128), lambda i: (i, 0)),
            pl.BlockSpec((D_SSD, 128), lambda i: (0, 0)),
            ANY,
        ],
        out_specs=[pl.BlockSpec((t, TAIL_W), lambda i: (i, OFF_Q // TAIL_W)), pl.BlockSpec((1, 128), lambda i: (0, 0))],
        out_shape=[jax.ShapeDtypeStruct((lp, NP), BF16), jax.ShapeDtypeStruct((1, 128), F32)],
        input_output_aliases={8: 0},
        compiler_params=_params(("arbitrary",)),
    )(dqr, dkr, dv, dg, ddt_part, cos_t, sin_t, reduce_t, dproj)


def _cs_row(cs):
    row = lax.broadcasted_iota(jnp.int32, cs.shape, 0)
    lane = lax.broadcasted_iota(jnp.int32, cs.shape, 1)
    return jnp.sum(jnp.where(row == lane % HEAD_DIM, cs, 0.0), axis=0, keepdims=True)


def _decay_matrix(cs, cs_row, r, mask):
    seg = cs[:, r * HEAD_DIM:r * HEAD_DIM + 1] - cs_row[:, r * HEAD_DIM:(r + 1) * HEAD_DIM]
    return jnp.exp(jnp.where(mask, seg, -jnp.inf))


def _causal_mask():
    row = lax.broadcasted_iota(jnp.int32, (CHUNK, CHUNK), 0)
    col = lax.broadcasted_iota(jnp.int32, (CHUNK, CHUNK), 1)
    return row >= col


def _ssd_fwd(xbc, dt_rep, proj, a_rep, dsk_rep, wn, tri):
    lp = xbc.shape[0]
    nc = lp // CHUNK
    gw = SSD_GPS * GROUP_W

    def body(xbc_ref, dt_ref, z_ref, a_ref, dsk_ref, wn_ref, tri_ref, yn_ref, ytot_ref, hprev_ref, h_scr):
        @pl.when(pl.program_id(1) == 0)
        def _():
            h_scr[...] = jnp.zeros_like(h_scr)

        causal = _causal_mask()
        G = range(SSD_GPS)
        colsl = [slice(gi * GROUP_W, (gi + 1) * GROUP_W) for gi in G]
        xbc = [xbc_ref[:, gi * XBC_BLK:(gi + 1) * XBC_BLK] for gi in G]
        dt = [dt_ref[:, colsl[gi]] for gi in G]
        xs = [xbc[gi][:, :GROUP_W] for gi in G]
        b = [xbc[gi][:, GROUP_W:GROUP_W + D_STATE].astype(BF16) for gi in G]
        c = [xbc[gi][:, GROUP_W + D_STATE:].astype(BF16) for gi in G]
        hprev = [h_scr[gi] for gi in G]
        cs = [_dot(tri_ref[...], dt[gi] * a_ref[:, colsl[gi]], NN, HIGHEST) for gi in G]
        cb = [_dot(c[gi], b[gi], NT) for gi in G]
        yoff = [_dot(c[gi], hprev[gi].astype(BF16)) for gi in G]
        cs_t = [_cs_row(cs[gi]) for gi in G]
        xdt = [xs[gi] * dt[gi] for gi in G]
        cs_last = [cs[gi][CHUNK - 1:CHUNK, :] for gi in G]
        st = [_dot(b[gi], (xdt[gi] * jnp.exp(cs_last[gi] - cs[gi])).astype(BF16), TN) for gi in G]
        m = [[(cb[gi] * _decay_matrix(cs[gi], cs_t[gi], r, causal)).astype(BF16) for r in range(HEADS_PER_GROUP)] for gi in G]
        ydiag = [[_dot(m[gi][r], xdt[gi][:, r * HEAD_DIM:(r + 1) * HEAD_DIM].astype(BF16)) for r in range(HEADS_PER_GROUP)]
                 for gi in G]
        for gi in G:
            cols = colsl[gi]
            ytot = jnp.concatenate(ydiag[gi], axis=1) + yoff[gi] * jnp.exp(cs[gi]) + dsk_ref[:, cols] * xs[gi]
            z = z_ref[:, cols]
            gz = ytot * (z * _sigmoid(z))
            rstd = lax.rsqrt(jnp.mean(gz * gz, axis=-1, keepdims=True) + EPS)
            hprev_ref[gi] = hprev[gi]
            h_scr[gi] = hprev[gi] * jnp.exp(cs_last[gi]) + st[gi]
            ytot_ref[:, cols] = ytot
            yn_ref[:, cols] = (gz * rstd * wn_ref[:, cols]).astype(BF16)

    vec = pl.BlockSpec((1, gw), lambda g, c: (0, g))
    blk = pl.BlockSpec((CHUNK, gw), lambda g, c: (c, g))
    return pl.pallas_call(
        body, name="ssd_fwd", grid=(N_GROUPS // SSD_GPS, nc),
        in_specs=[
            pl.BlockSpec((CHUNK, SSD_GPS * XBC_BLK), lambda g, c: (c, g)),
            blk, blk, vec, vec, vec,
            pl.BlockSpec((CHUNK, CHUNK), lambda g, c: (0, 0)),
        ],
        out_specs=[blk, blk, pl.BlockSpec((None, SSD_GPS, D_STATE, GROUP_W), lambda g, c: (c, g, 0, 0))],
        out_shape=[
            jax.ShapeDtypeStruct((lp, D_MIX), BF16),
            jax.ShapeDtypeStruct((lp, D_SSD), F32),
            jax.ShapeDtypeStruct((nc, N_GROUPS, D_STATE, GROUP_W), F32),
        ],
        scratch_shapes=[pltpu.VMEM((SSD_GPS, D_STATE, GROUP_W), F32)],
        compiler_params=_params(("parallel", "arbitrary")),
    )(xbc, dt_rep, proj, a_rep, dsk_rep, wn, tri)


def _ssd_bwd(dmix, ytot, proj, xbc, dt_rep, hprev, a_rep, dsk_rep, wn, tri, comm):
    lp = xbc.shape[0]
    nc = lp // CHUNK
    gps = SSD_GPS_BWD
    gw = gps * GROUP_W
    ncm = comm.n
    n_in, n_out = 10, 6
    grid = (N_GROUPS // gps, nc)

    def all_groups(refs, causal):
        (dyn_ref, ytot_ref, z_ref, xbc_ref, dt_ref, hprev_ref, a_ref, dsk_ref, wn_ref, tri_ref,
         dz_ref, dxbc_ref, ddt_ref, dd_ref, da_ref, dwn_ref, dh_scr) = refs
        G = range(gps)
        H = range(HEADS_PER_GROUP)
        cl = [slice(gi * GROUP_W, (gi + 1) * GROUP_W) for gi in G]
        hl = [slice(r * HEAD_DIM, (r + 1) * HEAD_DIM) for r in H]
        tri = tri_ref[...]
        xbc = [xbc_ref[:, gi * XBC_BLK:(gi + 1) * XBC_BLK] for gi in G]
        dt = [dt_ref[:, cl[gi]] for gi in G]
        a = [a_ref[:, cl[gi]] for gi in G]
        xs = [xbc[gi][:, :GROUP_W] for gi in G]
        bbf = [xbc[gi][:, GROUP_W:GROUP_W + D_STATE].astype(BF16) for gi in G]
        cbf = [xbc[gi][:, GROUP_W + D_STATE:].astype(BF16) for gi in G]
        hprev = [hprev_ref[gi] for gi in G]
        hbf = [hprev[gi].astype(BF16) for gi in G]
        dhn = [dh_scr[gi] for gi in G]
        dhnb = [dhn[gi].astype(BF16) for gi in G]
        cs = [_dot(tri, dt[gi] * a[gi], NN, HIGHEST) for gi in G]
        cb = [_dot(cbf[gi], bbf[gi], NT) for gi in G]
        g = [_dot(cbf[gi], hbf[gi]) for gi in G]
        dxw = [_dot(bbf[gi], dhnb[gi]) for gi in G]
        cs_t = [_cs_row(cs[gi]) for gi in G]
        dy = []
        for gi in G:
            ytot = ytot_ref[:, cl[gi]]
            z = z_ref[:, cl[gi]]
            dyn = dyn_ref[:, cl[gi]]
            sz = _sigmoid(z)
            silu_z = z * sz
            gz = ytot * silu_z
            rstd = lax.rsqrt(jnp.mean(gz * gz, axis=-1, keepdims=True) + EPS)
            xhat = gz * rstd
            dwn_ref[:, cl[gi]] += jnp.sum(dyn * xhat, axis=0, keepdims=True)
            dxhat = dyn * wn_ref[:, cl[gi]]
            dgz = rstd * (dxhat - xhat * jnp.mean(dxhat * xhat, axis=-1, keepdims=True))
            dy.append(dgz * silu_z)
            dz_ref[:, cl[gi]] = (dgz * ytot * (sz * (1.0 + z * (1.0 - sz)))).astype(BF16)
            dd_ref[:, cl[gi]] += jnp.sum(dy[gi] * xs[gi], axis=0, keepdims=True)
        xdt = [xs[gi] * dt[gi] for gi in G]
        e = [jnp.exp(cs[gi]) for gi in G]
        cs_last = [cs[gi][CHUNK - 1:CHUNK, :] for gi in G]
        dte = [jnp.exp(cs_last[gi] - cs[gi]) for gi in G]
        cd = [jnp.exp(cs_last[gi]) for gi in G]
        dgb = [(dy[gi] * e[gi]).astype(BF16) for gi in G]
        dyb = [dy[gi].astype(BF16) for gi in G]
        xdtb = [xdt[gi].astype(BF16) for gi in G]
        dc = [_dot(dgb[gi], hbf[gi], NT) for gi in G]
        dhprev = [_dot(cbf[gi], dgb[gi], TN) for gi in G]
        db = [_dot((xdt[gi] * dte[gi]).astype(BF16), dhnb[gi], NT) for gi in G]
        dm = [[_dot(dyb[gi][:, hl[r]], xdtb[gi][:, hl[r]], NT) for r in H] for gi in G]
        mb, dseg, dcbb = [], [], []
        for gi in G:
            mb.append([])
            dseg.append([])
            dcb = None
            for r in H:
                lm = _decay_matrix(cs[gi], cs_t[gi], r, causal)
                m = cb[gi] * lm
                mb[gi].append(m.astype(BF16))
                dseg[gi].append(dm[gi][r] * m)
                dcb = dm[gi][r] * lm if r == 0 else dcb + dm[gi][r] * lm
            dcbb.append(dcb.astype(BF16))
        dxdt_diag = [[_dot(mb[gi][r], dyb[gi][:, hl[r]], TN) for r in H] for gi in G]
        ones = jnp.ones((CHUNK, HEAD_DIM), F32)
        colsum = [[_dot(dseg[gi][r], ones, TN, HIGHEST) for r in H] for gi in G]
        dc2 = [_dot(dcbb[gi], bbf[gi]) for gi in G]
        db2 = [_dot(dcbb[gi], cbf[gi], TN) for gi in G]
        dcs = []
        for gi in G:
            t_dte = dxw[gi] * xdt[gi] * dte[gi]
            dcs_last = (jnp.sum(dhn[gi] * hprev[gi], axis=0, keepdims=True) * cd[gi]
                        + jnp.sum(t_dte, axis=0, keepdims=True))
            diag = jnp.concatenate(
                [(jnp.sum(dseg[gi][r], axis=1, keepdims=True) - colsum[gi][r]) * (1.0 / HEAD_DIM) for r in H], axis=1)
            d = dy[gi] * g[gi] * e[gi] - t_dte + diag
            row = lax.broadcasted_iota(jnp.int32, d.shape, 0)
            dcs.append(d + jnp.where(row == CHUNK - 1, dcs_last, 0.0))
        dda = [_dot(tri, dcs[gi], TN, HIGHEST) for gi in G]
        for gi in G:
            dxdt = dxw[gi] * dte[gi] + jnp.concatenate(dxdt_diag[gi], axis=1)
            da_ref[:, cl[gi]] += jnp.sum(dda[gi] * dt[gi], axis=0, keepdims=True)
            ddt = dda[gi] * a[gi] + dxdt * xs[gi]
            dxs = dsk_ref[:, cl[gi]] * dy[gi] + dxdt * dt[gi]
            ddt_ref[:, cl[gi]] = ddt * (1.0 - jnp.exp(-dt[gi]))
            dxbc_ref[:, gi * XBC_BLK:(gi + 1) * XBC_BLK] = jnp.concatenate(
                [dxs, db[gi] + db2[gi], dc[gi] + dc2[gi]], axis=1)
            dh_scr[gi] = dhprev[gi] + dhn[gi] * cd[gi]

    def body(*refs):
        ins = refs[:n_in]
        cin = refs[n_in:n_in + ncm]
        outs = refs[n_in + ncm:n_in + ncm + n_out]
        cout = refs[n_in + ncm + n_out:n_in + 2 * ncm + n_out]
        dh_scr = refs[n_in + 2 * ncm + n_out]
        sems = refs[n_in + 2 * ncm + n_out + 1:]
        g, c = pl.program_id(0), pl.program_id(1)

        @pl.when((g == 0) & (c == 0))
        def _():
            comm.start(cin, cout, sems)

        @pl.when(c == 0)
        def _():
            dh_scr[...] = jnp.zeros_like(dh_scr)
            for ref in outs[3:]:
                ref[...] = jnp.zeros_like(ref)

        all_groups(ins + outs + (dh_scr,), _causal_mask())

        @pl.when((g == grid[0] - 1) & (c == nc - 1))
        def _():
            comm.wait(cin, cout, sems)

    rev = lambda c: nc - 1 - c
    vec = pl.BlockSpec((1, gw), lambda g, c: (0, g))
    blk = pl.BlockSpec((CHUNK, gw), lambda g, c: (rev(c), g))
    xblk = pl.BlockSpec((CHUNK, gps * XBC_BLK), lambda g, c: (rev(c), g))
    res = pl.pallas_call(
        body, name="ssd_bwd", grid=grid,
        in_specs=[blk, blk, blk, xblk, blk,
                  pl.BlockSpec((None, gps, D_STATE, GROUP_W), lambda g, c: (rev(c), g, 0, 0)),
                  vec, vec, vec,
                  pl.BlockSpec((CHUNK, CHUNK), lambda g, c: (0, 0))] + comm.in_specs,
        out_specs=[blk, xblk, blk, vec, vec, vec] + comm.out_specs,
        out_shape=[
            jax.ShapeDtypeStruct((lp, NP), BF16),
            jax.ShapeDtypeStruct((lp, D_CONV), F32),
            jax.ShapeDtypeStruct((lp, D_SSD), F32),
            jax.ShapeDtypeStruct((1, D_SSD), F32),
            jax.ShapeDtypeStruct((1, D_SSD), F32),
            jax.ShapeDtypeStruct((1, D_SSD), F32),
        ] + comm.out_shape,
        scratch_shapes=[pltpu.VMEM((gps, D_STATE, GROUP_W), F32)] + comm.scratch,
        compiler_params=_params(("arbitrary", "arbitrary"), side_effects=True),
    )(dmix, ytot, proj, xbc, dt_rep, hprev, a_rep, dsk_rep, wn, tri, *comm.operands)
    return res


def _stack_heads(t, h):
    return jnp.concatenate([t[:, (REP * h + r) * HEAD_DIM:(REP * h + r + 1) * HEAD_DIM] for r in range(REP)], axis=0)


def _band(t2, t1, t0, h):
    sl = slice(h * HEAD_DIM, (h + 1) * HEAD_DIM)
    return jnp.concatenate([t2[:, sl], t1[:, sl], t0[:, sl]], axis=0)


def _attn_probs(s, sink, qc):
    s = s * (HEAD_DIM ** -0.5)
    key_abs = (qc - WINDOW_CHUNKS) * CHUNK + lax.broadcasted_iota(jnp.int32, s.shape, 1)
    s = jnp.where(key_abs >= PAD_LEAD, s, -jnp.inf)
    m = jnp.maximum(jnp.max(s, axis=-1, keepdims=True), sink)
    p = jnp.exp(s - m)
    ps = jnp.exp(sink - m)
    denom = jnp.sum(p, axis=-1, keepdims=True) + ps
    return p / denom, ps / denom


ATT_QC = 2
ATT_KC = WINDOW_CHUNKS + ATT_QC


def _kv_specs(width, newest_chunk_of):
    return [pl.BlockSpec((CHUNK, width), functools.partial(lambda j, p: (jnp.maximum(newest_chunk_of(p) - j, 0), 0), j))
            for j in range(ATT_KC - 1, -1, -1)]


def _attn_fwd(qr, kr, vb, proj, sink_stack, mix):
    lp = qr.shape[0]
    nc = lp // CHUNK

    assert nc % ATT_QC == 0
    qrows = ATT_QC * CHUNK

    def body(q_ref, *rest):
        k_refs, v_refs = rest[:ATT_KC], rest[ATT_KC:2 * ATT_KC]
        g_ref, sink_ref, _, att_ref, mix_ref = rest[2 * ATT_KC:]
        p = pl.program_id(0)
        q = q_ref[...]
        ks = [r[...] for r in k_refs]
        vs = [r[...] for r in v_refs]
        units = [(u, h) for u in range(ATT_QC) for h in range(KV_HEADS)]
        s = [_dot(_stack_heads(q[u * CHUNK:(u + 1) * CHUNK, :], h), _band(*ks[u:u + 3], h), NT) for u, h in units]
        vbh = [_band(*vs[u:u + 3], h) for u, h in units]
        pn = [_attn_probs(s[i], sink_ref[h], ATT_QC * p + u)[0].astype(BF16) for i, (u, h) in enumerate(units)]
        o = [_dot(pn[i], vbh[i]) for i in range(len(units))]
        att = jnp.concatenate(
            [jnp.concatenate([o[u * KV_HEADS + h][r * CHUNK:(r + 1) * CHUNK, :] for h in range(KV_HEADS) for r in range(REP)],
                             axis=1) for u in range(ATT_QC)], axis=0)
        att_ref[...] = att
        g = g_ref[...]
        mix_ref[...] = (att * (g * _sigmoid(g))).astype(BF16)

    newest = lambda p: ATT_QC * p + ATT_QC - 1
    return pl.pallas_call(
        body, name="attn_fwd", grid=(nc // ATT_QC,),
        in_specs=[pl.BlockSpec((qrows, D_ATT), lambda p: (p, 0))] + _kv_specs(D_KV, newest) + _kv_specs(D_KV, newest) + [
            pl.BlockSpec((qrows, D_ATT), lambda p: (p, OFF_G // D_ATT)),
            pl.BlockSpec((KV_HEADS, REP * CHUNK, 1), lambda p: (0, 0, 0)),
            ANY,
        ],
        out_specs=[pl.BlockSpec((qrows, D_ATT), lambda p: (p, 0)),
                   pl.BlockSpec((qrows, D_ATT), lambda p: (p, D_SSD // D_ATT))],
        out_shape=[jax.ShapeDtypeStruct((lp, D_ATT), F32), jax.ShapeDtypeStruct((lp, D_MIX), BF16)],
        input_output_aliases={2 * ATT_KC + 3: 1},
        compiler_params=_params(("parallel",)),
    )(qr, *([kr] * ATT_KC), *([vb] * ATT_KC), proj, sink_stack, mix)


def _attn_bwd(qr, kr, vb, att, proj, dmix, sink_stack):
    lp = qr.shape[0]
    nc = lp // CHUNK
    assert nc % ATT_QC == 0
    npairs = nc // ATT_QC
    qrows = ATT_QC * CHUNK
    wrows = ATT_KC * CHUNK

    def body(q_ref, *rest):
        k_refs, v_refs = rest[:ATT_KC], rest[ATT_KC:2 * ATT_KC]
        (att_ref, g_ref, do_ref, sink_ref, dq_ref, dk_ref, dv_ref, dg_ref, dsink_ref, dk_acc, dv_acc) = rest[2 * ATT_KC:]
        step = pl.program_id(0)

        @pl.when(step == 0)
        def _():
            dk_acc[...] = jnp.zeros_like(dk_acc)
            dv_acc[...] = jnp.zeros_like(dv_acc)
            dsink_ref[...] = jnp.zeros_like(dsink_ref)

        @pl.when(step < npairs)
        def _():
            q = q_ref[...]
            ks = [r[...] for r in k_refs]
            vs = [r[...] for r in v_refs]
            att = att_ref[...]
            g = g_ref[...]
            dog = do_ref[...]
            sg = _sigmoid(g)
            dg_ref[...] = dog * att * (sg * (1.0 + g * (1.0 - sg)))
            do = dog * (g * sg)
            units = [(u, h) for u in range(ATT_QC) for h in range(KV_HEADS)]
            n = range(len(units))
            rows = [slice(u * CHUNK, (u + 1) * CHUNK) for u in range(ATT_QC)]
            qs = [_stack_heads(q[rows[u], :], h) for u, h in units]
            kb = [_band(*ks[u:u + 3], h) for u, h in units]
            vbh = [_band(*vs[u:u + 3], h) for u, h in units]
            dos = [_stack_heads(do[rows[u], :], h) for u, h in units]
            dosb = [dos[i].astype(BF16) for i in n]
            s = [_dot(qs[i], kb[i], NT) for i in n]
            dp = [_dot(dosb[i], vbh[i], NT) for i in n]
            ds, pnb = [], []
            for i, (u, h) in enumerate(units):
                pn, psink = _attn_probs(s[i], sink_ref[h], ATT_QC * step + u)
                delta = jnp.sum(dos[i] * _stack_heads(att[rows[u], :], h), axis=-1, keepdims=True)
                ds.append((pn * (dp[i] - delta)).astype(BF16))
                pnb.append(pn.astype(BF16))
                dsink_ref[h] += -(psink * delta)
            dqs = [_dot(ds[i], kb[i]) for i in n]
            dks = [_dot(ds[i], qs[i], TN) for i in n]
            dvs = [_dot(pnb[i], dosb[i], TN) for i in n]
            dq_ref[...] = jnp.concatenate(
                [jnp.concatenate([dqs[u * KV_HEADS + h][r * CHUNK:(r + 1) * CHUNK, :]
                                  for h in range(KV_HEADS) for r in range(REP)], axis=1) for u in range(ATT_QC)],
                axis=0) * (HEAD_DIM ** -0.5)
            for u in range(ATT_QC):
                band = slice(u * CHUNK, u * CHUNK + BAND)
                dk_acc[band, :] += jnp.concatenate(dks[u * KV_HEADS:(u + 1) * KV_HEADS], axis=1) * (HEAD_DIM ** -0.5)
                dv_acc[band, :] += jnp.concatenate(dvs[u * KV_HEADS:(u + 1) * KV_HEADS], axis=1)

        dk_ref[...] = dk_acc[0:qrows, :]
        dv_ref[...] = dv_acc[0:qrows, :]
        for acc in (dk_acc, dv_acc):
            rest_rows = acc[qrows:wrows, :]
            acc[0:wrows - qrows, :] = rest_rows
            acc[wrows - qrows:wrows, :] = jnp.zeros((qrows, D_KV), F32)

    qp = lambda p: jnp.minimum(p, npairs - 1)
    newest = lambda p: ATT_QC * qp(p) + ATT_QC - 1
    qblk = pl.BlockSpec((qrows, D_ATT), lambda p: (qp(p), 0))
    oldest = pl.BlockSpec((qrows, D_KV), lambda p: (jnp.maximum(p - 1, 0), 0))
    return pl.pallas_call(
        body, name="attn_bwd", grid=(npairs + 1,),
        in_specs=[qblk] + _kv_specs(D_KV, newest) + _kv_specs(D_KV, newest) + [
            qblk,
            pl.BlockSpec((qrows, D_ATT), lambda p: (qp(p), OFF_G // D_ATT)),
            pl.BlockSpec((qrows, D_ATT), lambda p: (qp(p), D_SSD // D_ATT)),
            pl.BlockSpec((KV_HEADS, REP * CHUNK, 1), lambda p: (0, 0, 0)),
        ],
        out_specs=[qblk, oldest, oldest, qblk, pl.BlockSpec((KV_HEADS, REP * CHUNK, 1), lambda p: (0, 0, 0))],
        out_shape=[
            jax.ShapeDtypeStruct((lp, D_ATT), F32),
            jax.ShapeDtypeStruct((lp, D_KV), F32),
            jax.ShapeDtypeStruct((lp, D_KV), F32),
            jax.ShapeDtypeStruct((lp, D_ATT), F32),
            jax.ShapeDtypeStruct((KV_HEADS, REP * CHUNK, 1), F32),
        ],
        scratch_shapes=[pltpu.VMEM((wrows, D_KV), F32), pltpu.VMEM((wrows, D_KV), F32)],
        compiler_params=_params(("arbitrary",)),
    )(qr, *([kr] * ATT_KC), *([vb] * ATT_KC), att, proj, dmix, sink_stack)


def _post_loss(out, x2d, target, w):
    lp = out.shape[0]
    nc = lp // CHUNK
    nx = x2d.shape[0] // CHUNK

    k = _tile(nc, (ROW_K, 3, 2, 1))

    def body(o_ref, *rest):
        x_refs, t_refs = rest[:k], rest[k:2 * k]
        w_ref, do_ref, dy_ref, gw_ref, loss_ref = rest[2 * k:]
        i = pl.program_id(0)

        @pl.when(i == 0)
        def _():
            gw_ref[...] = jnp.zeros_like(gw_ref)
            loss_ref[...] = jnp.zeros_like(loss_ref)

        o = o_ref[...]
        w = w_ref[...]
        rstd = lax.rsqrt(jnp.mean(o * o, axis=-1, keepdims=True) + EPS)
        xhat = o * rstd
        x = jnp.concatenate([r[...] for r in x_refs], axis=0)
        t = jnp.concatenate([r[...] for r in t_refs], axis=0)
        chunk = _chunk_of_row(i, k)
        err = (x + xhat * w - t) * ((chunk > 0) & (chunk <= nx)).astype(F32)
        loss_ref[...] += 0.5 * jnp.sum(jnp.mean(err * err, axis=-1, keepdims=True), axis=0, keepdims=True)
        dy = err * (1.0 / D_MODEL)
        dy_ref[...] = dy
        gw_ref[...] += jnp.sum(dy * xhat, axis=0, keepdims=True)
        dxhat = dy * w
        do_ref[...] = (rstd * (dxhat - xhat * jnp.mean(dxhat * xhat, axis=-1, keepdims=True))).astype(BF16)

    row = pl.BlockSpec((k * CHUNK, D_MODEL), lambda i: (i, 0))
    return pl.pallas_call(
        body, name="post_loss", grid=(nc // k,),
        in_specs=[row] + _x_specs(nx, k) + _x_specs(nx, k) + [pl.BlockSpec((1, D_MODEL), lambda i: (0, 0))],
        out_specs=[row, row, pl.BlockSpec((1, D_MODEL), lambda i: (0, 0)), pl.BlockSpec((1, 128), lambda i: (0, 0))],
        out_shape=[
            jax.ShapeDtypeStruct((lp, D_MODEL), BF16),
            jax.ShapeDtypeStruct((lp, D_MODEL), F32),
            jax.ShapeDtypeStruct((1, D_MODEL), F32),
            jax.ShapeDtypeStruct((1, 128), F32),
        ],
        compiler_params=_params(("arbitrary",)),
    )(out, *([x2d] * k), *([target] * k), w)


def _prenorm_bwd(dhn, x2d, h0, dy, w):
    nx = x2d.shape[0] // CHUNK
    k = _tile(nx, (X_K, 4, 2, 1))
    rows = k * CHUNK

    def backward(h, dhn, w):
        rstd = lax.rsqrt(jnp.mean(h * h, axis=-1, keepdims=True) + EPS)
        xhat = h * rstd
        dxhat = dhn * w
        dh = rstd * (dxhat - xhat * jnp.mean(dxhat * xhat, axis=-1, keepdims=True))
        return dh, jnp.sum(dhn * xhat, axis=0, keepdims=True)

    def body(*refs):
        dhn_refs, dy_refs = refs[:k], refs[k:2 * k]
        x_ref, w_ref, gx_ref, gw_ref = refs[2 * k:]

        @pl.when(pl.program_id(0) == 0)
        def _():
            gw_ref[...] = jnp.zeros_like(gw_ref)

        dh, gw = backward(x_ref[...], jnp.concatenate([r[...] for r in dhn_refs], axis=0), w_ref[...])
        gx_ref[...] = dh + jnp.concatenate([r[...] for r in dy_refs], axis=0)
        gw_ref[...] += gw

    def body_meta(dhn_ref, h0_ref, w_ref, d0_ref, gw_ref):
        d0_ref[...], gw_ref[...] = backward(h0_ref[...], dhn_ref[...], w_ref[...])

    chunk_specs = [pl.BlockSpec((CHUNK, D_MODEL), functools.partial(lambda u, i: (k * i + 1 + u, 0), u)) for u in range(k)]
    first = pl.BlockSpec((CHUNK, D_MODEL), lambda i: (0, 0))
    vec = pl.BlockSpec((1, D_MODEL), lambda i: (0, 0))
    wide = pl.BlockSpec((rows, D_MODEL), lambda i: (i, 0))
    gx, gw_x = pl.pallas_call(
        body, name="prenorm_bwd", grid=(nx // k,),
        in_specs=chunk_specs + chunk_specs + [wide, vec],
        out_specs=[wide, vec],
        out_shape=[jax.ShapeDtypeStruct((nx * CHUNK, D_MODEL), F32), jax.ShapeDtypeStruct((1, D_MODEL), F32)],
        compiler_params=_params(("arbitrary",)),
    )(*([dhn] * k), *([dy] * k), x2d, w)
    d0, gw_0 = pl.pallas_call(
        body_meta, name="prenorm_bwd_meta", grid=(1,),
        in_specs=[first, first, vec], out_specs=[first, vec],
        out_shape=[jax.ShapeDtypeStruct((CHUNK, D_MODEL), F32), jax.ShapeDtypeStruct((1, D_MODEL), F32)],
        compiler_params=_params(("arbitrary",)),
    )(dhn, h0, w)
    return gx, d0, gw_x + gw_0


def _adamw(slabs, w, m, v, name):
    rows, cols = w.shape
    tr = _tile(rows, (256, 128, 64, 16, 8))
    c1 = 1.0 - ADAM_B1 ** ADAM_STEP
    c2 = 1.0 - ADAM_B2 ** ADAM_STEP

    def body(s_ref, w_ref, m_ref, v_ref, g_ref, d_ref, mo_ref, vo_ref):
        g = s_ref[0].astype(F32)
        for k in range(1, slabs.shape[0]):
            g = g + s_ref[k].astype(F32)
        w = w_ref[...]
        m = ADAM_B1 * m_ref[...] + (1.0 - ADAM_B1) * g
        v = ADAM_B2 * v_ref[...] + (1.0 - ADAM_B2) * (g * g)
        g_ref[...] = g
        mo_ref[...] = m
        vo_ref[...] = v
        d_ref[...] = -ADAM_LR * ((m / c1) / (jnp.sqrt(v / c2) + ADAM_EPS) + ADAM_WD * w)

    blk = pl.BlockSpec((tr, cols), lambda i: (i, 0))
    shape = jax.ShapeDtypeStruct((rows, cols), F32)
    return pl.pallas_call(
        body, name=name, grid=(rows // tr,),
        in_specs=[pl.BlockSpec((slabs.shape[0], tr, cols), lambda i: (0, i, 0)), blk, blk, blk],
        out_specs=[blk, blk, blk, blk],
        out_shape=[shape, shape, shape, shape],
        compiler_params=_params(("parallel",)),
    )(slabs, w, m, v)


def _perm_xbc(a):
    lead = a.shape[:-1]
    xs = a[..., :D_SSD].reshape(lead + (N_GROUPS, GROUP_W))
    b = a[..., D_SSD:D_SSD + N_GROUPS * D_STATE].reshape(lead + (N_GROUPS, D_STATE))
    c = a[..., D_SSD + N_GROUPS * D_STATE:].reshape(lead + (N_GROUPS, D_STATE))
    return jnp.concatenate([xs, b, c], axis=-1).reshape(lead + (D_CONV,))


def _unperm_xbc(a):
    lead = a.shape[:-1]
    t = a.reshape(lead + (N_GROUPS, XBC_BLK))
    xs = t[..., :GROUP_W].reshape(lead + (D_SSD,))
    b = t[..., GROUP_W:GROUP_W + D_STATE].reshape(lead + (N_GROUPS * D_STATE,))
    c = t[..., GROUP_W + D_STATE:].reshape(lead + (N_GROUPS * D_STATE,))
    return jnp.concatenate([xs, b, c], axis=-1)


R_Z, R_XBC, R_DT, R_Q, R_K, R_V, R_G = 0, 2048, 6144, 6176, 7200, 7456, 7712


def _internal_of_reference():
    ref = np.arange(D_IN_PROJ)
    out = np.empty(D_IN_PROJ, np.int64)
    out[R_Z:R_XBC] = OFF_Z + ref[:D_SSD]
    xs = np.arange(D_SSD)
    out[R_XBC:R_XBC + D_SSD] = OFF_XBC + (xs // GROUP_W) * XBC_BLK + xs % GROUP_W
    bc = np.arange(N_GROUPS * D_STATE)
    out[R_XBC + D_SSD:R_XBC + D_SSD + N_GROUPS * D_STATE] = OFF_XBC + (bc // D_STATE) * XBC_BLK + GROUP_W + bc % D_STATE
    out[R_XBC + D_SSD + N_GROUPS * D_STATE:R_DT] = OFF_XBC + (bc // D_STATE) * XBC_BLK + GROUP_W + D_STATE + bc % D_STATE
    out[R_DT:R_Q] = OFF_DT + np.arange(SSD_HEADS)
    out[R_Q:R_K] = OFF_Q + np.arange(D_ATT)
    out[R_K:R_V] = OFF_K + np.arange(D_KV)
    out[R_V:R_G] = OFF_V + np.arange(D_KV)
    out[R_G:] = OFF_G + np.arange(D_ATT)
    return out


def _runs(src, dst_break):
    runs, lo = [], 0
    for i in range(1, len(src) + 1):
        if i == len(src) or src[i] != src[i - 1] + 1 or dst_break[i] != dst_break[i - 1]:
            runs.append((lo, i))
            lo = i
    return runs


RELAYOUT_ROWS = 256


def _lane_window(ref, lead, c0, n):
    a0 = c0 // 128 * 128
    a1 = min(-(-(c0 + n) // 128) * 128, ref.shape[-1])
    return ref[lead + (slice(None), slice(a0, a1))][:, c0 - a0:c0 - a0 + n]


def _w_in_internal(w_gathered):
    int_of_ref = _internal_of_reference()
    ref_of_int = np.full(NP, -1, np.int64)
    ref_of_int[int_of_ref] = np.arange(D_IN_PROJ)
    shard = np.where(ref_of_int >= 0, ref_of_int // SHARD_IN, -1)
    src = np.where(ref_of_int >= 0, ref_of_int, -10 - 2 * np.arange(NP))
    plan, zeros = [], 0
    for lo, hi in _runs(src, shard):
        if ref_of_int[lo] < 0:
            zeros += hi - lo
            continue
        if zeros:
            plan.append((None, 0, zeros))
            zeros = 0
        plan.append((int(ref_of_int[lo] // SHARD_IN), int(ref_of_int[lo] % SHARD_IN), hi - lo))
    if zeros:
        plan.append((None, 0, zeros))
    tr = RELAYOUT_ROWS

    def body(w_ref, o_ref):
        o_ref[...] = jnp.concatenate(
            [jnp.zeros((tr, n), o_ref.dtype) if s is None else _lane_window(w_ref, (s,), c0, n) for s, c0, n in plan], axis=1)

    return pl.pallas_call(
        body, name="w_in_relayout", grid=(D_MODEL // tr,),
        in_specs=[pl.BlockSpec((N_DEV, tr, SHARD_IN), lambda i: (0, i, 0))],
        out_specs=pl.BlockSpec((tr, NP), lambda i: (i, 0)),
        out_shape=jax.ShapeDtypeStruct((D_MODEL, NP), w_gathered.dtype),
        compiler_params=_params(("parallel",)),
    )(w_gathered)


def _w_in_slabs(dw):
    int_of_ref = _internal_of_reference()
    plan = []
    for s in range(N_DEV):
        cols = int_of_ref[s * SHARD_IN:(s + 1) * SHARD_IN]
        plan.append([(int(cols[lo]), hi - lo) for lo, hi in _runs(cols, np.zeros_like(cols))])
    tr = RELAYOUT_ROWS

    def body(dw_ref, o_ref):
        for s in range(N_DEV):
            o_ref[s] = jnp.concatenate([_lane_window(dw_ref, (), c0, n) for c0, n in plan[s]], axis=1)

    return pl.pallas_call(
        body, name="dw_in_relayout", grid=(D_MODEL // tr,),
        in_specs=[pl.BlockSpec((tr, NP), lambda i: (i, 0))],
        out_specs=pl.BlockSpec((N_DEV, tr, SHARD_IN), lambda i: (0, i, 0)),
        out_shape=jax.ShapeDtypeStruct((N_DEV, D_MODEL, SHARD_IN), dw.dtype),
        compiler_params=_params(("parallel",)),
    )(dw)


def _rep_heads(a):
    return jnp.repeat(a, HEAD_DIM, axis=1)


SMALL = (("norm_pre_w", 2048), ("conv_b", 4096), ("dt_bias", 32), ("a_log", 32), ("d_skip", 32),
         ("ssd_norm_w", 2048), ("attn_sinks", 16), ("norm_post_w", 2048))
SMALL_USED = sum(size for _, size in SMALL)
SMALL_LEN = 10368


def _pack_small(d, loss=None):
    parts = [d[name].reshape(1, size) for name, size in SMALL]
    tail = jnp.zeros((1, SMALL_LEN - SMALL_USED), F32)
    if loss is not None:
        tail = tail.at[0, 0].set(loss)
    return jnp.concatenate(parts + [tail], axis=1)


def _unpack_small(vec):
    out, off = {}, 0
    for name, size in SMALL:
        out[name] = vec[:, off:off + size]
        off += size
    return out


def kernel(x, meta_tokens, norm_pre_w, w_in, conv_w, conv_b, dt_bias, a_log, d_skip, ssd_norm_w, attn_sinks, w_out, norm_post_w, loss_target, m_meta_tokens, m_norm_pre_w, m_w_in, m_conv_w, m_conv_b, m_dt_bias, m_a_log, m_d_skip, m_ssd_norm_w, m_attn_sinks, m_w_out, m_norm_post_w, v_meta_tokens, v_norm_pre_w, v_w_in, v_conv_w, v_conv_b, v_dt_bias, v_a_log, v_d_skip, v_ssd_norm_w, v_attn_sinks, v_w_out, v_norm_post_w):
    seq = x.shape[1]
    lp = seq + 2 * CHUNK
    x2d = x[0]

    w_in_g, conv_w_g, meta_g = _gather_two_level([w_in[0].astype(BF16), conv_w[0], meta_tokens], "gather_w_in")
    w_all = _w_in_internal(w_in_g)
    conv_w_full = _perm_xbc(jnp.transpose(conv_w_g, (1, 0, 2)).reshape(CONV_WIDTH, D_CONV))
    conv_b_int = _perm_xbc(conv_b)
    meta_full = jnp.transpose(meta_g, (1, 0, 2)).reshape(N_META, D_MODEL)
    h0 = jnp.concatenate([jnp.zeros((PAD_LEAD, D_MODEL), F32), meta_full], axis=0)

    pos = (jnp.arange(lp) - PAD_LEAD).astype(F32)
    half = HEAD_DIM // 2
    inv = ROPE_THETA ** (-jnp.arange(half, dtype=F32) / half)
    ang = pos[:, None] * inv[None, :]
    cos_t = jnp.tile(jnp.cos(ang), (1, 4))
    sin_t = jnp.tile(jnp.concatenate([-jnp.sin(ang), jnp.sin(ang)], axis=1), (1, 2))
    head_of_col = np.arange(D_SSD) // HEAD_DIM
    expand = jnp.asarray((np.arange(128)[:, None] == head_of_col[None, :]).astype(np.float32))
    reduce_t = jnp.asarray((head_of_col[:, None] == np.arange(128)[None, :]).astype(np.float32))
    tri = jnp.asarray(np.tril(np.ones((CHUNK, CHUNK), np.float32)))
    a_rep = _rep_heads(-jnp.exp(a_log))
    dsk_rep = _rep_heads(d_skip)
    dt_bias_rep = _rep_heads(dt_bias)
    sink_stack = jnp.repeat(attn_sinks.reshape(KV_HEADS, REP), CHUNK, axis=1).reshape(KV_HEADS, REP * CHUNK, 1)

    hn = _prenorm(x2d, h0, norm_pre_w)
    tm = _tile(lp, (1056, 704, 128, 64))
    proj, w_out_g = _matmul(hn, w_all, tm=tm, tn=1536, tk=D_MODEL, out_dtype=F32, name="in_proj",
                            comm=_Comm([(w_out[0].astype(BF16), "gather")]))
    w_out_full = w_out_g.reshape(D_MIX, D_MODEL)
    xbc = _conv_fwd(proj, conv_w_full, conv_b_int)
    qr, kr, vb, dt_rep = _act_fwd(proj, cos_t, sin_t, expand, dt_bias_rep)
    mix, ytot, hprev = _ssd_fwd(xbc, dt_rep, proj, a_rep, dsk_rep, ssd_norm_w, tri)
    att, mix = _attn_fwd(qr, kr, vb, proj, sink_stack, mix)
    out = _matmul(mix, w_out_full, tm=tm, tn=1024, tk=D_MIX, out_dtype=F32, name="out_proj")
    dout, dy, g_norm_post, loss_part = _post_loss(out, x2d, loss_target[0], norm_post_w)

    dmix = _matmul(dout, w_out_full, trans_b=True, tm=tm, tn=1024, tk=D_MODEL, out_dtype=F32, name="dmix")
    dw_out = _matmul(mix, dout, trans_a=True, tm=512, tn=1024, tk=lp, out_dtype=BF16, name="dw_out")
    dqr, dkr, dv, dg, dsink_rows = _attn_bwd(qr, kr, vb, att, proj, dmix, sink_stack)
    dproj, dxbc, ddt_part, dd_part, da_part, g_ssd_norm, g_out = _ssd_bwd(
        dmix, ytot, proj, xbc, dt_rep, hprev, a_rep, dsk_rep, ssd_norm_w, tri,
        _Comm([(dw_out.reshape(N_DEV, D_MIX // N_DEV, D_MODEL), "scatter")]))
    dproj, dconv_w_int, dconv_b_int = _conv_bwd(dxbc, proj, conv_w_full, conv_b_int, dproj)
    dproj, ddt_bias = _act_bwd(dqr, dkr, dv, dg, ddt_part, cos_t, sin_t, reduce_t, dproj)
    dw_all = _matmul(hn, dproj, trans_a=True, tm=512, tn=1024, tk=lp, out_dtype=BF16, name="dw_in")
    dw_slabs = _w_in_slabs(dw_all)
    dw_chip = _pair_sum(dw_slabs, _exchange_sibling(dw_slabs, "dw_in_sibling"), "dw_in_pair_sum")
    dhn, g_in = _matmul(dproj, w_all, trans_b=True, tm=tm, tn=1024, tk=1536, out_dtype=F32, name="dhn",
                        comm=_Comm([(dw_chip, "scatter")], scope="chips"))
    grad_x, dh0, g_norm_pre = _prenorm_bwd(dhn, x2d, h0, dy, norm_pre_w)

    dmeta = dh0[PAD_LEAD:, :]
    dconv_w_ref = _unperm_xbc(dconv_w_int)
    heads = lambda part: part.reshape(SSD_HEADS, HEAD_DIM).sum(axis=1).reshape(1, SSD_HEADS)
    small_local = _pack_small({
        "norm_pre_w": g_norm_pre, "conv_b": _unperm_xbc(dconv_b_int), "dt_bias": ddt_bias[:, :SSD_HEADS],
        "a_log": heads(da_part) * (-jnp.exp(a_log)), "d_skip": heads(dd_part), "ssd_norm_w": g_ssd_norm,
        "attn_sinks": dsink_rows.reshape(Q_HEADS, CHUNK).sum(axis=1).reshape(1, Q_HEADS),
        "norm_post_w": g_norm_post}, loss=loss_part[0, 0])
    g_conv, g_meta, g_small = _exchange(
        [(jnp.transpose(dconv_w_ref.reshape(CONV_WIDTH, N_DEV, D_CONV // N_DEV), (1, 0, 2)), "scatter"),
         (jnp.transpose(dmeta.reshape(N_META, N_DEV, D_MODEL // N_DEV), (1, 0, 2)), "scatter"),
         (small_local, "gather")], "exchange_small")

    res = {}
    res["w_in"] = [o[None] for o in _adamw(g_in, w_in[0], m_w_in[0], v_w_in[0], "adamw_w_in")]
    res["w_out"] = [o[None] for o in _adamw(g_out, w_out[0], m_w_out[0], v_w_out[0], "adamw_w_out")]
    res["conv_w"] = [o[None] for o in _adamw(g_conv, conv_w[0], m_conv_w[0], v_conv_w[0], "adamw_conv_w")]
    res["meta_tokens"] = _adamw(g_meta, meta_tokens, m_meta_tokens, v_meta_tokens, "adamw_meta")
    given = dict(norm_pre_w=(norm_pre_w, m_norm_pre_w, v_norm_pre_w), conv_b=(conv_b, m_conv_b, v_conv_b),
                 dt_bias=(dt_bias, m_dt_bias, v_dt_bias), a_log=(a_log, m_a_log, v_a_log),
                 d_skip=(d_skip, m_d_skip, v_d_skip), ssd_norm_w=(ssd_norm_w, m_ssd_norm_w, v_ssd_norm_w),
                 attn_sinks=(attn_sinks, m_attn_sinks, v_attn_sinks),
                 norm_post_w=(norm_post_w, m_norm_post_w, v_norm_post_w))
    packed = [_pack_small({k: t[j] for k, t in given.items()}) for j in range(3)]
    small_out = _adamw(g_small, packed[0], packed[1], packed[2], "adamw_small")
    small_res = [_unpack_small(r) for r in small_out]
    loss = small_out[0][0, SMALL_USED]

    order = ["meta_tokens", "norm_pre_w", "w_in", "conv_w", "conv_b", "dt_bias", "a_log", "d_skip", "ssd_norm_w",
             "attn_sinks", "w_out", "norm_post_w"]
    outs = []
    for j in range(4):
        for name in order:
            outs.append(res[name][j] if name in res else small_res[j][name])
    return (loss, grad_x[None], *outs)
```

```python
import functools

import numpy as np
import jax
import jax.numpy as jnp
from jax import lax
from jax.experimental import pallas as pl
from jax.experimental.pallas import tpu as pltpu

F32 = jnp.float32
BF16 = jnp.bfloat16
HIGHEST = lax.Precision.HIGHEST

N_DEV = 8
D_MODEL = 2048
CHUNK = 64
N_META = 16
PAD_LEAD = CHUNK - N_META
EPS = 1e-6
N_GROUPS = 8
HEADS_PER_GROUP = 4
HEAD_DIM = 64
GROUP_W = HEADS_PER_GROUP * HEAD_DIM
D_STATE = 128
D_SSD = 2048
D_CONV = 4096
SSD_HEADS = 32
CONV_WIDTH = 4
Q_HEADS = 16
KV_HEADS = 4
REP = 4
D_ATT = 1024
D_KV = 256
WINDOW_CHUNKS = 2
BAND = (WINDOW_CHUNKS + 1) * CHUNK
ROPE_THETA = 10000.0
D_MIX = D_SSD + D_ATT
D_IN_PROJ = 8736
SHARD_IN = D_IN_PROJ // N_DEV

OFF_Z, OFF_XBC, OFF_Q, OFF_G, OFF_K, OFF_V, OFF_DT = 0, 2048, 6144, 7168, 8192, 8448, 8704
NP = 9216
TAIL_W = NP - OFF_Q
XBC_BLK = 512
SSD_GPS = 8
SSD_GPS_BWD = 8

ADAM_LR, ADAM_B1, ADAM_B2, ADAM_EPS, ADAM_WD, ADAM_STEP = 0.001, 0.9, 0.999, 1e-08, 0.01, 10

VMEM_LIMIT = 48 * 1024 * 1024

NN = (((1,), (0,)), ((), ()))
NT = (((1,), (1,)), ((), ()))
TN = (((0,), (0,)), ((), ()))
ANY = pl.BlockSpec(memory_space=pl.ANY)


def _dot(a, b, dims=NN, precision=None):
    return lax.dot_general(a, b, dims, precision=precision, preferred_element_type=F32)


def _tile(n, prefs):
    for t in prefs:
        if n % t == 0:
            return t
    return n


def _params(sem, vmem=VMEM_LIMIT, side_effects=False):
    return pltpu.CompilerParams(dimension_semantics=sem, vmem_limit_bytes=vmem, has_side_effects=side_effects)


def _sigmoid(x):
    return 1.0 / (1.0 + jnp.exp(-x))


class _Comm:
    def __init__(self, items, scope="devices"):
        self.items = items
        self.scope = scope
        self.slabs = slabs = N_DEV if scope == "devices" else N_DEV // 2
        self.n = n = len(items)
        self.operands = [arr for arr, _ in items]
        self.in_specs = [ANY] * n
        self.out_specs = [ANY] * n
        self.out_shape = [jax.ShapeDtypeStruct((slabs,) + tuple(arr.shape) if kind == "gather" else tuple(arr.shape),
                                               arr.dtype) for arr, kind in items]
        self.scratch = [pltpu.SemaphoreType.DMA((n, slabs - 1)), pltpu.SemaphoreType.DMA((n, slabs - 1)),
                        pltpu.SemaphoreType.DMA((n,))]

    def _places(self):
        pos = (lax.axis_index("x"), lax.axis_index("y"), lax.axis_index("c"))
        if self.scope == "devices":
            index = lambda p: 4 * p[0] + 2 * p[1] + p[2]
            masks = range(1, N_DEV)
        else:
            index = lambda p: 2 * p[0] + p[1]
            masks = (2, 4, 6)
        peers = []
        for k in masks:
            p = tuple(1 - pos[b] if (k >> (2 - b)) & 1 else pos[b] for b in range(3))
            peers.append((p, index(p)))
        return index(pos), peers

    def _copies(self, ins, outs, sems, landed):
        send_sems, recv_sems, local_sems = sems
        me, peers = self._places()
        local, remote = [], []
        for a, (_, kind) in enumerate(self.items):
            own = ins[a] if kind == "gather" else ins[a].at[me]
            local.append(pltpu.make_async_copy(own, outs[a].at[me], local_sems.at[a]))
            for k, (p, pid) in enumerate(peers):
                remote.append(pltpu.make_async_remote_copy(
                    src_ref=ins[a] if kind == "gather" else ins[a].at[pid],
                    dst_ref=outs[a].at[pid if landed else me],
                    send_sem=send_sems.at[a, k], recv_sem=recv_sems.at[a, k],
                    device_id=p, device_id_type=pl.DeviceIdType.MESH))
        return local, remote

    def start(self, ins, outs, sems):
        local, remote = self._copies(ins, outs, sems, landed=False)
        for cp in local + remote:
            cp.start()

    def wait(self, ins, outs, sems):
        local, remote = self._copies(ins, outs, sems, landed=True)
        for cp in remote + local:
            cp.wait()


def _exchange(items, name):
    comm = _Comm(items)
    n = comm.n

    def body(*refs):
        ins, outs, sems = refs[:n], refs[n:2 * n], refs[2 * n:]
        comm.start(ins, outs, sems)
        comm.wait(ins, outs, sems)

    return pl.pallas_call(
        body, name=name, in_specs=comm.in_specs, out_specs=comm.out_specs, out_shape=comm.out_shape,
        scratch_shapes=comm.scratch, compiler_params=pltpu.CompilerParams(has_side_effects=True),
    )(*comm.operands)


def _exchange_sibling(slabs, name):
    def body(g_ref, land_ref, send_sems, recv_sems):
        x, y, c = lax.axis_index("x"), lax.axis_index("y"), lax.axis_index("c")
        copies = [pltpu.make_async_remote_copy(
            src_ref=g_ref.at[2 * q + (1 - c)], dst_ref=land_ref.at[q], send_sem=send_sems.at[q], recv_sem=recv_sems.at[q],
            device_id=(x, y, 1 - c), device_id_type=pl.DeviceIdType.MESH) for q in range(N_DEV // 2)]
        for cp in copies:
            cp.start()
        for cp in copies:
            cp.wait()

    return pl.pallas_call(
        body, name=name, in_specs=[ANY], out_specs=ANY,
        out_shape=jax.ShapeDtypeStruct((N_DEV // 2,) + tuple(slabs.shape[1:]), slabs.dtype),
        scratch_shapes=[pltpu.SemaphoreType.DMA((N_DEV // 2,)), pltpu.SemaphoreType.DMA((N_DEV // 2,))],
        compiler_params=pltpu.CompilerParams(has_side_effects=True),
    )(slabs)


def _pair_sum(slabs, landed, name):
    _, rows, cols = slabs.shape
    tr = _tile(rows, (256, 128, 64, 16, 8))

    def body(g_ref, l_ref, o_ref):
        mine = g_ref[lax.axis_index("c")]
        o_ref[...] = (mine.astype(F32) + l_ref[...].astype(F32)).astype(o_ref.dtype)

    return pl.pallas_call(
        body, name=name, grid=(N_DEV // 2, rows // tr),
        in_specs=[pl.BlockSpec((None, 2, tr, cols), lambda q, i: (q, 0, i, 0)),
                  pl.BlockSpec((None, tr, cols), lambda q, i: (q, i, 0))],
        out_specs=pl.BlockSpec((None, tr, cols), lambda q, i: (q, i, 0)),
        out_shape=jax.ShapeDtypeStruct((N_DEV // 2, rows, cols), slabs.dtype),
        compiler_params=_params(("parallel", "parallel")),
    )(slabs.reshape(N_DEV // 2, 2, rows, cols), landed)


class _GatherTwoLevel:
    def __init__(self, arrays):
        self.arrays = arrays
        self.n = n = len(arrays)
        self.operands = list(arrays)
        self.in_specs = [ANY] * n
        self.out_specs = [ANY] * n
        self.out_shape = [jax.ShapeDtypeStruct((N_DEV,) + tuple(a.shape), a.dtype) for a in arrays]
        self.scratch = [pltpu.SemaphoreType.DMA((n, N_DEV - 1)), pltpu.SemaphoreType.DMA((n, N_DEV - 1)),
                        pltpu.SemaphoreType.DMA((n,))]

    def _plan(self, ins, outs, sems):
        send_sems, recv_sems, local_sems = sems
        x, y, c = lax.axis_index("x"), lax.axis_index("y"), lax.axis_index("c")
        me, sibling = (x, y, c), (x, y, 1 - c)
        chips = [(1 - x, y), (x, 1 - y), (1 - x, 1 - y)]

        def slab(a, place):
            return outs[a].at[4 * place[0] + 2 * place[1] + place[2]]

        def copy(a, k, block, to, src=None):
            return pltpu.make_async_remote_copy(
                src_ref=slab(a, block) if src is None else src, dst_ref=slab(a, block),
                send_sem=send_sems.at[a, k], recv_sem=recv_sems.at[a, k],
                device_id=to, device_id_type=pl.DeviceIdType.MESH)

        own, mine = [], []
        for a in range(self.n):
            mine.append(pltpu.make_async_copy(ins[a], slab(a, me), local_sems.at[a]))
            own.append(copy(a, 0, me, sibling, src=ins[a]))
            own += [copy(a, 1 + j, me, (*chip, c), src=ins[a]) for j, chip in enumerate(chips)]
        return me, sibling, chips, c, copy, own, mine

    def start(self, ins, outs, sems):
        _, _, _, _, _, own, mine = self._plan(ins, outs, sems)
        for cp in mine + own:
            cp.start()

    def wait(self, ins, outs, sems):
        me, sibling, chips, c, copy, own, mine = self._plan(ins, outs, sems)
        forwards = []
        for j, chip in enumerate(chips):
            for a in range(self.n):
                copy(a, 1 + j, (*chip, c), me).wait_recv()
                fwd = copy(a, 4 + j, (*chip, c), sibling)
                fwd.start()
                forwards.append(fwd)
        for a in range(self.n):
            copy(a, 0, sibling, me).wait_recv()
            for j, chip in enumerate(chips):
                copy(a, 4 + j, (*chip, 1 - c), me).wait_recv()
        for cp in own + forwards:
            cp.wait_send()
        for loc in mine:
            loc.wait()


def _gather_two_level(arrays, name):
    comm = _GatherTwoLevel(arrays)
    n = comm.n

    def body(*refs):
        ins, outs, sems = refs[:n], refs[n:2 * n], refs[2 * n:]
        comm.start(ins, outs, sems)
        comm.wait(ins, outs, sems)

    return pl.pallas_call(
        body, name=name, in_specs=comm.in_specs, out_specs=comm.out_specs, out_shape=comm.out_shape,
        scratch_shapes=comm.scratch, compiler_params=pltpu.CompilerParams(has_side_effects=True),
    )(*arrays)


def _matmul(a, b, *, tm, tn, tk, out_dtype, name, trans_a=False, trans_b=False, comm=None):
    m, k = (a.shape[1], a.shape[0]) if trans_a else a.shape
    n = b.shape[0] if trans_b else b.shape[1]
    nk = k // tk
    dims = TN if trans_a else (NT if trans_b else NN)
    assert not (trans_a and trans_b)
    nc = comm.n if comm else 0
    grid = (m // tm, n // tn, nk)

    def body(*refs):
        a_ref, b_ref = refs[:2]
        cin = refs[2:2 + nc]
        o_ref = refs[2 + nc]
        cout = refs[3 + nc:3 + 2 * nc]
        scratch = refs[3 + 2 * nc:]
        sems = scratch[len(scratch) - 3:] if comm else None
        i, j, kk = pl.program_id(0), pl.program_id(1), pl.program_id(2)
        if comm:
            @pl.when((i == 0) & (j == 0) & (kk == 0))
            def _():
                comm.start(cin, cout, sems)

        if nk == 1:
            o_ref[...] = _dot(a_ref[...], b_ref[...], dims).astype(out_dtype)
        else:
            acc_ref = scratch[0]

            @pl.when(kk == 0)
            def _():
                acc_ref[...] = jnp.zeros_like(acc_ref)

            acc_ref[...] += _dot(a_ref[...], b_ref[...], dims)

            @pl.when(kk == nk - 1)
            def _():
                o_ref[...] = acc_ref[...].astype(out_dtype)

        if comm:
            @pl.when((i == grid[0] - 1) & (j == grid[1] - 1) & (kk == nk - 1))
            def _():
                comm.wait(cin, cout, sems)

    a_spec = (pl.BlockSpec((tk, tm), lambda i, j, kk: (kk, i)) if trans_a
              else pl.BlockSpec((tm, tk), lambda i, j, kk: (i, kk)))
    b_spec = (pl.BlockSpec((tn, tk), lambda i, j, kk: (j, kk)) if trans_b
              else pl.BlockSpec((tk, tn), lambda i, j, kk: (kk, j)))
    sem = ("arbitrary",) * 3 if comm else ("parallel", "parallel", "arbitrary")
    res = pl.pallas_call(
        body, name=name, grid=grid,
        in_specs=[a_spec, b_spec] + (comm.in_specs if comm else []),
        out_specs=[pl.BlockSpec((tm, tn), lambda i, j, kk: (i, j))] + (comm.out_specs if comm else []),
        out_shape=[jax.ShapeDtypeStruct((m, n), out_dtype)] + (comm.out_shape if comm else []),
        scratch_shapes=([] if nk == 1 else [pltpu.VMEM((tm, tn), F32)]) + (comm.scratch if comm else []),
        compiler_params=_params(sem, side_effects=bool(comm)),
    )(a, b, *(comm.operands if comm else []))
    return res if comm else res[0]


def _in_proj_half(hn, w_all, half, *, tm, tn, name, comm, prev=None):
    lp = hn.shape[0]
    kh = D_MODEL // 2
    nc = comm.n
    npv = 0 if prev is None else 1
    grid = (lp // tm, NP // tn)

    def body(*refs):
        a_ref, b_ref = refs[:2]
        cin = refs[2 + npv:2 + npv + nc]
        o_ref = refs[2 + npv + nc]
        cout = refs[3 + npv + nc:3 + npv + 2 * nc]
        sems = refs[3 + npv + 2 * nc:]
        i, j = pl.program_id(0), pl.program_id(1)

        @pl.when((i == 0) & (j == 0))
        def _():
            comm.start(cin, cout, sems)

        acc = _dot(a_ref[...], b_ref[...])
        o_ref[...] = acc if prev is None else acc + refs[2][...]

        @pl.when((i == grid[0] - 1) & (j == grid[1] - 1))
        def _():
            comm.wait(cin, cout, sems)

    tile = pl.BlockSpec((tm, tn), lambda i, j: (i, j))
    return pl.pallas_call(
        body, name=name, grid=grid,
        in_specs=[pl.BlockSpec((tm, kh), lambda i, j: (i, half)), pl.BlockSpec((kh, tn), lambda i, j: (half, j))]
        + ([] if prev is None else [tile]) + comm.in_specs,
        out_specs=[tile] + comm.out_specs,
        out_shape=[jax.ShapeDtypeStruct((lp, NP), F32)] + comm.out_shape,
        scratch_shapes=comm.scratch,
        input_output_aliases={} if prev is None else {2: 0},
        compiler_params=_params(("arbitrary", "arbitrary"), side_effects=True),
    )(hn, w_all, *([] if prev is None else [prev]), *comm.operands)


ROW_K = 6
X_K = 8


def _x_specs(nx, k):
    return [pl.BlockSpec((CHUNK, D_MODEL), functools.partial(lambda u, i: (jnp.clip(k * i + u - 1, 0, nx - 1), 0), u))
            for u in range(k)]


def _chunk_of_row(i, k):
    return k * i + lax.broadcasted_iota(jnp.int32, (k * CHUNK, 1), 0) // CHUNK


def _prenorm(x2d, h0, w):
    nx = x2d.shape[0] // CHUNK
    nc = nx + 2
    k = _tile(nc, (ROW_K, 3, 2, 1))

    def body(*refs):
        x_refs = refs[:k]
        h0_ref, w_ref, o_ref = refs[k:]
        i = pl.program_id(0)
        x = jnp.concatenate([r[...] for r in x_refs], axis=0)
        head = jnp.concatenate([h0_ref[...], x[CHUNK:, :]], axis=0)
        chunk = _chunk_of_row(i, k)
        h = jnp.where(i == 0, head, x) * (chunk <= nx).astype(F32)
        rstd = lax.rsqrt(jnp.mean(h * h, axis=-1, keepdims=True) + EPS)
        o_ref[...] = (h * rstd * w_ref[...]).astype(BF16)

    rows = k * CHUNK
    return pl.pallas_call(
        body, name="prenorm", grid=(nc // k,),
        in_specs=_x_specs(nx, k) + [pl.BlockSpec((CHUNK, D_MODEL), lambda i: (0, 0)), pl.BlockSpec((1, D_MODEL), lambda i: (0, 0))],
        out_specs=pl.BlockSpec((rows, D_MODEL), lambda i: (i, 0)),
        out_shape=jax.ShapeDtypeStruct((nc * CHUNK, D_MODEL), BF16),
        compiler_params=_params(("parallel",)),
    )(*([x2d] * k), h0, w)


CONV_COLS = 512
HALO = 8


def _conv_pre(ext, w, b):
    taps = [ext[HALO:, :]] + [pltpu.roll(ext, j, 0)[HALO:, :] for j in range(1, CONV_WIDTH)]
    acc = b + w[3:4, :] * taps[0]
    for j in range(1, CONV_WIDTH):
        acc = acc + w[3 - j:4 - j, :] * taps[j]
    return acc, taps


def _conv_fwd(proj, conv_w, conv_b):
    lp = proj.shape[0]
    t = _tile(lp, (704, 384, 128, 64))
    hb = t // HALO
    c0 = OFF_XBC // CONV_COLS

    def body(u_ref, halo_ref, w_ref, b_ref, o_ref):
        i = pl.program_id(0)
        halo = halo_ref[...] * (i > 0).astype(F32)
        pre, _ = _conv_pre(jnp.concatenate([halo, u_ref[...]], axis=0), w_ref[...], b_ref[...])
        o_ref[...] = pre * _sigmoid(pre)

    return pl.pallas_call(
        body, name="conv_fwd", grid=(lp // t, D_CONV // CONV_COLS),
        in_specs=[
            pl.BlockSpec((t, CONV_COLS), lambda i, j: (i, c0 + j)),
            pl.BlockSpec((HALO, CONV_COLS), lambda i, j: (jnp.maximum(i * hb - 1, 0), c0 + j)),
            pl.BlockSpec((CONV_WIDTH, CONV_COLS), lambda i, j: (0, j)),
            pl.BlockSpec((1, CONV_COLS), lambda i, j: (0, j)),
        ],
        out_specs=pl.BlockSpec((t, CONV_COLS), lambda i, j: (i, j)),
        out_shape=jax.ShapeDtypeStruct((lp, D_CONV), F32),
        compiler_params=_params(("parallel", "parallel")),
    )(proj, proj, conv_w, conv_b)


def _conv_bwd(dxbc, proj, conv_w, conv_b, dproj):
    lp = proj.shape[0]
    t = _tile(lp, (704, 384, 128, 64))
    hb = t // HALO
    nt = lp // t
    c0 = OFF_XBC // CONV_COLS

    def body(dx_ref, dxn_ref, u_ref, up_ref, un_ref, w_ref, b_ref, _, du_ref, dw_ref, db_ref):
        i = pl.program_id(1)
        w = w_ref[...]
        up = up_ref[...] * (i > 0).astype(F32)
        ext = jnp.concatenate([up, u_ref[...], un_ref[...]], axis=0)
        pre, taps = _conv_pre(ext, w, b_ref[...])
        dxn = dxn_ref[...] * (i < nt - 1).astype(F32)
        dxe = jnp.concatenate([dx_ref[...], dxn], axis=0)
        sg = _sigmoid(pre)
        dpre = dxe * sg * (1.0 + pre * (1.0 - sg))
        du = w[3:4, :] * dpre[:t, :]
        for j in range(1, CONV_WIDTH):
            du = du + w[3 - j:4 - j, :] * pltpu.roll(dpre, t + HALO - j, 0)[:t, :]
        du_ref[...] = du.astype(BF16)

        @pl.when(i == 0)
        def _():
            dw_ref[...] = jnp.zeros_like(dw_ref)
            db_ref[...] = jnp.zeros_like(db_ref)

        dp = dpre[:t, :]
        db_ref[...] += jnp.sum(dp, axis=0, keepdims=True)
        for j in range(CONV_WIDTH):
            dw_ref[3 - j:4 - j, :] += jnp.sum(dp * taps[j][:t, :], axis=0, keepdims=True)

    nxt = lambda i: jnp.minimum((i + 1) * hb, lp // HALO - 1)
    return pl.pallas_call(
        body, name="conv_bwd", grid=(D_CONV // CONV_COLS, nt),
        in_specs=[
            pl.BlockSpec((t, CONV_COLS), lambda j, i: (i, j)),
            pl.BlockSpec((HALO, CONV_COLS), lambda j, i: (nxt(i), j)),
            pl.BlockSpec((t, CONV_COLS), lambda j, i: (i, c0 + j)),
            pl.BlockSpec((HALO, CONV_COLS), lambda j, i: (jnp.maximum(i * hb - 1, 0), c0 + j)),
            pl.BlockSpec((HALO, CONV_COLS), lambda j, i: (nxt(i), c0 + j)),
            pl.BlockSpec((CONV_WIDTH, CONV_COLS), lambda j, i: (0, j)),
            pl.BlockSpec((1, CONV_COLS), lambda j, i: (0, j)),
            ANY,
        ],
        out_specs=[
            pl.BlockSpec((t, CONV_COLS), lambda j, i: (i, c0 + j)),
            pl.BlockSpec((CONV_WIDTH, CONV_COLS), lambda j, i: (0, j)),
            pl.BlockSpec((1, CONV_COLS), lambda j, i: (0, j)),
        ],
        out_shape=[
            jax.ShapeDtypeStruct((lp, NP), BF16),
            jax.ShapeDtypeStruct((CONV_WIDTH, D_CONV), F32),
            jax.ShapeDtypeStruct((1, D_CONV), F32),
        ],
        input_output_aliases={7: 0},
        compiler_params=_params(("parallel", "arbitrary")),
    )(dxbc, dxbc, proj, proj, proj, conv_w, conv_b, dproj)


def _swap_halves(t):
    w = t.shape[-1]
    lane = lax.broadcasted_iota(jnp.int32, t.shape, 1)
    return jnp.where((lane % HEAD_DIM) < HEAD_DIM // 2, pltpu.roll(t, w - HEAD_DIM // 2, 1),
                     pltpu.roll(t, HEAD_DIM // 2, 1))


def _act_fwd(proj, cos_t, sin_t, expand, dt_bias_rep):
    lp = proj.shape[0]
    t = _tile(lp, (384, 128, 64))

    def body(q_ref, k_ref, v_ref, dt_ref, cos_ref, sin_ref, ex_ref, bias_ref, qo_ref, ko_ref, vo_ref, dto_ref):
        i = pl.program_id(0)
        cos = cos_ref[...]
        sin = sin_ref[...]
        q = q_ref[...]
        qo_ref[...] = (q * jnp.tile(cos, (1, D_ATT // 128)) + _swap_halves(q) * jnp.tile(sin, (1, D_ATT // 128))).astype(BF16)
        k = k_ref[...]
        ko_ref[...] = (k * jnp.tile(cos, (1, D_KV // 128)) + _swap_halves(k) * jnp.tile(sin, (1, D_KV // 128))).astype(BF16)
        vo_ref[...] = v_ref[...].astype(BF16)
        raw = _dot(dt_ref[...], ex_ref[...], NN, HIGHEST) + bias_ref[...]
        sp = jnp.maximum(raw, 0.0) + jnp.log1p(jnp.exp(-jnp.abs(raw)))
        row = i * t + lax.broadcasted_iota(jnp.int32, sp.shape, 0)
        dto_ref[...] = jnp.where(row >= PAD_LEAD, sp, 0.0)

    return pl.pallas_call(
        body, name="act_fwd", grid=(lp // t,),
        in_specs=[
            pl.BlockSpec((t, D_ATT), lambda i: (i, OFF_Q // D_ATT)),
            pl.BlockSpec((t, D_KV), lambda i: (i, OFF_K // D_KV)),
            pl.BlockSpec((t, D_KV), lambda i: (i, OFF_V // D_KV)),
            pl.BlockSpec((t, 128), lambda i: (i, OFF_DT // 128)),
            pl.BlockSpec((t, 128), lambda i: (i, 0)),
            pl.BlockSpec((t, 128), lambda i: (i, 0)),
            pl.BlockSpec((128, D_SSD), lambda i: (0, 0)),
            pl.BlockSpec((1, D_SSD), lambda i: (0, 0)),
        ],
        out_specs=[
            pl.BlockSpec((t, D_ATT), lambda i: (i, 0)),
            pl.BlockSpec((t, D_KV), lambda i: (i, 0)),
            pl.BlockSpec((t, D_KV), lambda i: (i, 0)),
            pl.BlockSpec((t, D_SSD), lambda i: (i, 0)),
        ],
        out_shape=[
            jax.ShapeDtypeStruct((lp, D_ATT), BF16),
            jax.ShapeDtypeStruct((lp, D_KV), BF16),
            jax.ShapeDtypeStruct((lp, D_KV), BF16),
            jax.ShapeDtypeStruct((lp, D_SSD), F32),
        ],
        compiler_params=_params(("parallel",)),
    )(proj, proj, proj, proj, cos_t, sin_t, expand, dt_bias_rep)


def _act_bwd(dqr, dkr, dv, dg, ddt_part, cos_t, sin_t, reduce_t, dproj):
    lp = dqr.shape[0]
    t = _tile(lp, (384, 128, 64))

    def body(dq_ref, dk_ref, dv_ref, dg_ref, ddt_ref, cos_ref, sin_ref, red_ref, _, o_ref, db_ref):
        i = pl.program_id(0)
        cos = cos_ref[...]
        sin = sin_ref[...]
        dq = dq_ref[...]
        dq = dq * jnp.tile(cos, (1, D_ATT // 128)) + _swap_halves(dq * jnp.tile(sin, (1, D_ATT // 128)))
        dk = dk_ref[...]
        dk = dk * jnp.tile(cos, (1, D_KV // 128)) + _swap_halves(dk * jnp.tile(sin, (1, D_KV // 128)))
        ddt = _dot(ddt_ref[...], red_ref[...], NN, HIGHEST)
        o_ref[...] = jnp.concatenate(
            [dq.astype(BF16), dg_ref[...].astype(BF16), dk.astype(BF16), dv_ref[...].astype(BF16), ddt.astype(BF16),
             jnp.zeros((t, NP - OFF_DT - 128), BF16)], axis=1)

        @pl.when(i == 0)
        def _():
            db_ref[...] = jnp.zeros_like(db_ref)

        db_ref[...] += jnp.sum(ddt, axis=0, keepdims=True)

    return pl.pallas_call(
        body, name="act_bwd", grid=(lp // t,),
        in_specs=[
            pl.BlockSpec((t, D_ATT), lambda i: (i, 0)),
            pl.BlockSpec((t, D_KV), lambda i: (i, 0)),
            pl.BlockSpec((t, D_KV), lambda i: (i, 0)),
            pl.BlockSpec((t, D_ATT), lambda i: (i, 0)),
            pl.BlockSpec((t, D_SSD), lambda i: (i, 0)),
            pl.BlockSpec((t, 128), lambda i: (i, 0)),
            pl.BlockSpec((t, 128), lambda i: (i, 0)),
            pl.BlockSpec((D_SSD, 128), lambda i: (0, 0)),
            ANY,
        ],
        out_specs=[pl.BlockSpec((t, TAIL_W), lambda i: (i, OFF_Q // TAIL_W)), pl.BlockSpec((1, 128), lambda i: (0, 0))],
        out_shape=[jax.ShapeDtypeStruct((lp, NP), BF16), jax.ShapeDtypeStruct((1, 128), F32)],
        input_output_aliases={8: 0},
        compiler_params=_params(("arbitrary",)),
    )(dqr, dkr, dv, dg, ddt_part, cos_t, sin_t, reduce_t, dproj)


def _cs_row(cs):
    row = lax.broadcasted_iota(jnp.int32, cs.shape, 0)
    lane = lax.broadcasted_iota(jnp.int32, cs.shape, 1)
    return jnp.sum(jnp.where(row == lane % HEAD_DIM, cs, 0.0), axis=0, keepdims=True)


def _decay_matrix(cs, cs_row, r, mask):
    seg = cs[:, r * HEAD_DIM:r * HEAD_DIM + 1] - cs_row[:, r * HEAD_DIM:(r + 1) * HEAD_DIM]
    return jnp.exp(jnp.where(mask, seg, -jnp.inf))


def _causal_mask():
    row = lax.broadcasted_iota(jnp.int32, (CHUNK, CHUNK), 0)
    col = lax.broadcasted_iota(jnp.int32, (CHUNK, CHUNK), 1)
    return row >= col


def _ssd_fwd(xbc, dt_rep, proj, a_rep, dsk_rep, wn, tri):
    lp = xbc.shape[0]
    nc = lp // CHUNK
    gw = SSD_GPS * GROUP_W

    def body(xbc_ref, dt_ref, z_ref, a_ref, dsk_ref, wn_ref, tri_ref, yn_ref, ytot_ref, hprev_ref, h_scr):
        @pl.when(pl.program_id(1) == 0)
        def _():
            h_scr[...] = jnp.zeros_like(h_scr)

        causal = _causal_mask()
        G = range(SSD_GPS)
        colsl = [slice(gi * GROUP_W, (gi + 1) * GROUP_W) for gi in G]
        xbc = [xbc_ref[:, gi * XBC_BLK:(gi + 1) * XBC_BLK] for gi in G]
        dt = [dt_ref[:, colsl[gi]] for gi in G]
        xs = [xbc[gi][:, :GROUP_W] for gi in G]
        b = [xbc[gi][:, GROUP_W:GROUP_W + D_STATE].astype(BF16) for gi in G]
        c = [xbc[gi][:, GROUP_W + D_STATE:].astype(BF16) for gi in G]
        hprev = [h_scr[gi] for gi in G]
        cs = [_dot(tri_ref[...], dt[gi] * a_ref[:, colsl[gi]], NN, HIGHEST) for gi in G]
        cb = [_dot(c[gi], b[gi], NT) for gi in G]
        yoff = [_dot(c[gi], hprev[gi].astype(BF16)) for gi in G]
        cs_t = [_cs_row(cs[gi]) for gi in G]
        xdt = [xs[gi] * dt[gi] for gi in G]
        cs_last = [cs[gi][CHUNK - 1:CHUNK, :] for gi in G]
        st = [_dot(b[gi], (xdt[gi] * jnp.exp(cs_last[gi] - cs[gi])).astype(BF16), TN) for gi in G]
        m = [[(cb[gi] * _decay_matrix(cs[gi], cs_t[gi], r, causal)).astype(BF16) for r in range(HEADS_PER_GROUP)] for gi in G]
        ydiag = [[_dot(m[gi][r], xdt[gi][:, r * HEAD_DIM:(r + 1) * HEAD_DIM].astype(BF16)) for r in range(HEADS_PER_GROUP)]
                 for gi in G]
        for gi in G:
            cols = colsl[gi]
            ytot = jnp.concatenate(ydiag[gi], axis=1) + yoff[gi] * jnp.exp(cs[gi]) + dsk_ref[:, cols] * xs[gi]
            z = z_ref[:, cols]
            gz = ytot * (z * _sigmoid(z))
            rstd = lax.rsqrt(jnp.mean(gz * gz, axis=-1, keepdims=True) + EPS)
            hprev_ref[gi] = hprev[gi]
            h_scr[gi] = hprev[gi] * jnp.exp(cs_last[gi]) + st[gi]
            ytot_ref[:, cols] = ytot
            yn_ref[:, cols] = (gz * rstd * wn_ref[:, cols]).astype(BF16)

    vec = pl.BlockSpec((1, gw), lambda g, c: (0, g))
    blk = pl.BlockSpec((CHUNK, gw), lambda g, c: (c, g))
    return pl.pallas_call(
        body, name="ssd_fwd", grid=(N_GROUPS // SSD_GPS, nc),
        in_specs=[
            pl.BlockSpec((CHUNK, SSD_GPS * XBC_BLK), lambda g, c: (c, g)),
            blk, blk, vec, vec, vec,
            pl.BlockSpec((CHUNK, CHUNK), lambda g, c: (0, 0)),
        ],
        out_specs=[blk, blk, pl.BlockSpec((None, SSD_GPS, D_STATE, GROUP_W), lambda g, c: (c, g, 0, 0))],
        out_shape=[
            jax.ShapeDtypeStruct((lp, D_MIX), BF16),
            jax.ShapeDtypeStruct((lp, D_SSD), F32),
            jax.ShapeDtypeStruct((nc, N_GROUPS, D_STATE, GROUP_W), F32),
        ],
        scratch_shapes=[pltpu.VMEM((SSD_GPS, D_STATE, GROUP_W), F32)],
        compiler_params=_params(("parallel", "arbitrary")),
    )(xbc, dt_rep, proj, a_rep, dsk_rep, wn, tri)


def _ssd_bwd(dmix, ytot, proj, xbc, dt_rep, hprev, a_rep, dsk_rep, wn, tri, comm):
    lp = xbc.shape[0]
    nc = lp // CHUNK
    gps = SSD_GPS_BWD
    gw = gps * GROUP_W
    ncm = comm.n
    n_in, n_out = 10, 6
    grid = (N_GROUPS // gps, nc)

    def all_groups(refs, causal):
        (dyn_ref, ytot_ref, z_ref, xbc_ref, dt_ref, hprev_ref, a_ref, dsk_ref, wn_ref, tri_ref,
         dz_ref, dxbc_ref, ddt_ref, dd_ref, da_ref, dwn_ref, dh_scr) = refs
        G = range(gps)
        H = range(HEADS_PER_GROUP)
        cl = [slice(gi * GROUP_W, (gi + 1) * GROUP_W) for gi in G]
        hl = [slice(r * HEAD_DIM, (r + 1) * HEAD_DIM) for r in H]
        tri = tri_ref[...]
        xbc = [xbc_ref[:, gi * XBC_BLK:(gi + 1) * XBC_BLK] for gi in G]
        dt = [dt_ref[:, cl[gi]] for gi in G]
        a = [a_ref[:, cl[gi]] for gi in G]
        xs = [xbc[gi][:, :GROUP_W] for gi in G]
        bbf = [xbc[gi][:, GROUP_W:GROUP_W + D_STATE].astype(BF16) for gi in G]
        cbf = [xbc[gi][:, GROUP_W + D_STATE:].astype(BF16) for gi in G]
        hprev = [hprev_ref[gi] for gi in G]
        hbf = [hprev[gi].astype(BF16) for gi in G]
        dhn = [dh_scr[gi] for gi in G]
        dhnb = [dhn[gi].astype(BF16) for gi in G]
        cs = [_dot(tri, dt[gi] * a[gi], NN, HIGHEST) for gi in G]
        cb = [_dot(cbf[gi], bbf[gi], NT) for gi in G]
        g = [_dot(cbf[gi], hbf[gi]) for gi in G]
        dxw = [_dot(bbf[gi], dhnb[gi]) for gi in G]
        cs_t = [_cs_row(cs[gi]) for gi in G]
        dy = []
        for gi in G:
            ytot = ytot_ref[:, cl[gi]]
            z = z_ref[:, cl[gi]]
            dyn = dyn_ref[:, cl[gi]]
            sz = _sigmoid(z)
            silu_z = z * sz
            gz = ytot * silu_z
            rstd = lax.rsqrt(jnp.mean(gz * gz, axis=-1, keepdims=True) + EPS)
            xhat = gz * rstd
            dwn_ref[:, cl[gi]] += jnp.sum(dyn * xhat, axis=0, keepdims=True)
            dxhat = dyn * wn_ref[:, cl[gi]]
            dgz = rstd * (dxhat - xhat * jnp.mean(dxhat * xhat, axis=-1, keepdims=True))
            dy.append(dgz * silu_z)
            dz_ref[:, cl[gi]] = (dgz * ytot * (sz * (1.0 + z * (1.0 - sz)))).astype(BF16)
            dd_ref[:, cl[gi]] += jnp.sum(dy[gi] * xs[gi], axis=0, keepdims=True)
        xdt = [xs[gi] * dt[gi] for gi in G]
        e = [jnp.exp(cs[gi]) for gi in G]
        cs_last = [cs[gi][CHUNK - 1:CHUNK, :] for gi in G]
        dte = [jnp.exp(cs_last[gi] - cs[gi]) for gi in G]
        cd = [jnp.exp(cs_last[gi]) for gi in G]
        dgb = [(dy[gi] * e[gi]).astype(BF16) for gi in G]
        dyb = [dy[gi].astype(BF16) for gi in G]
        xdtb = [xdt[gi].astype(BF16) for gi in G]
        dc = [_dot(dgb[gi], hbf[gi], NT) for gi in G]
        dhprev = [_dot(cbf[gi], dgb[gi], TN) for gi in G]
        db = [_dot((xdt[gi] * dte[gi]).astype(BF16), dhnb[gi], NT) for gi in G]
        dm = [[_dot(dyb[gi][:, hl[r]], xdtb[gi][:, hl[r]], NT) for r in H] for gi in G]
        mb, dseg, dcbb = [], [], []
        for gi in G:
            mb.append([])
            dseg.append([])
            dcb = None
            for r in H:
                lm = _decay_matrix(cs[gi], cs_t[gi], r, causal)
                m = cb[gi] * lm
                mb[gi].append(m.astype(BF16))
                dseg[gi].append(dm[gi][r] * m)
                dcb = dm[gi][r] * lm if r == 0 else dcb + dm[gi][r] * lm
            dcbb.append(dcb.astype(BF16))
        dxdt_diag = [[_dot(mb[gi][r], dyb[gi][:, hl[r]], TN) for r in H] for gi in G]
        ones = jnp.ones((CHUNK, HEAD_DIM), F32)
        colsum = [[_dot(dseg[gi][r], ones, TN, HIGHEST) for r in H] for gi in G]
        dc2 = [_dot(dcbb[gi], bbf[gi]) for gi in G]
        db2 = [_dot(dcbb[gi], cbf[gi], TN) for gi in G]
        dcs = []
        for gi in G:
            t_dte = dxw[gi] * xdt[gi] * dte[gi]
            dcs_last = (jnp.sum(dhn[gi] * hprev[gi], axis=0, keepdims=True) * cd[gi]
                        + jnp.sum(t_dte, axis=0, keepdims=True))
            diag = jnp.concatenate(
                [(jnp.sum(dseg[gi][r], axis=1, keepdims=True) - colsum[gi][r]) * (1.0 / HEAD_DIM) for r in H], axis=1)
            d = dy[gi] * g[gi] * e[gi] - t_dte + diag
            row = lax.broadcasted_iota(jnp.int32, d.shape, 0)
            dcs.append(d + jnp.where(row == CHUNK - 1, dcs_last, 0.0))
        dda = [_dot(tri, dcs[gi], TN, HIGHEST) for gi in G]
        for gi in G:
            dxdt = dxw[gi] * dte[gi] + jnp.concatenate(dxdt_diag[gi], axis=1)
            da_ref[:, cl[gi]] += jnp.sum(dda[gi] * dt[gi], axis=0, keepdims=True)
            ddt = dda[gi] * a[gi] + dxdt * xs[gi]
            dxs = dsk_ref[:, cl[gi]] * dy[gi] + dxdt * dt[gi]
            ddt_ref[:, cl[gi]] = ddt * (1.0 - jnp.exp(-dt[gi]))
            dxbc_ref[:, gi * XBC_BLK:(gi + 1) * XBC_BLK] = jnp.concatenate(
                [dxs, db[gi] + db2[gi], dc[gi] + dc2[gi]], axis=1)
            dh_scr[gi] = dhprev[gi] + dhn[gi] * cd[gi]

    def body(*refs):
        ins = refs[:n_in]
        cin = refs[n_in:n_in + ncm]
        outs = refs[n_in + ncm:n_in + ncm + n_out]
        cout = refs[n_in + ncm + n_out:n_in + 2 * ncm + n_out]
        dh_scr = refs[n_in + 2 * ncm + n_out]
        sems = refs[n_in + 2 * ncm + n_out + 1:]
        g, c = pl.program_id(0), pl.program_id(1)

        @pl.when((g == 0) & (c == 0))
        def _():
            comm.start(cin, cout, sems)

        @pl.when(c == 0)
        def _():
            dh_scr[...] = jnp.zeros_like(dh_scr)
            for ref in outs[3:]:
                ref[...] = jnp.zeros_like(ref)

        all_groups(ins + outs + (dh_scr,), _causal_mask())

        @pl.when((g == grid[0] - 1) & (c == nc - 1))
        def _():
            comm.wait(cin, cout, sems)

    rev = lambda c: nc - 1 - c
    vec = pl.BlockSpec((1, gw), lambda g, c: (0, g))
    blk = pl.BlockSpec((CHUNK, gw), lambda g, c: (rev(c), g))
    xblk = pl.BlockSpec((CHUNK, gps * XBC_BLK), lambda g, c: (rev(c), g))
    res = pl.pallas_call(
        body, name="ssd_bwd", grid=grid,
        in_specs=[blk, blk, blk, xblk, blk,
                  pl.BlockSpec((None, gps, D_STATE, GROUP_W), lambda g, c: (rev(c), g, 0, 0)),
                  vec, vec, vec,
                  pl.BlockSpec((CHUNK, CHUNK), lambda g, c: (0, 0))] + comm.in_specs,
        out_specs=[blk, xblk, blk, vec, vec, vec] + comm.out_specs,
        out_shape=[
            jax.ShapeDtypeStruct((lp, NP), BF16),
            jax.ShapeDtypeStruct((lp, D_CONV), F32),
            jax.ShapeDtypeStruct((lp, D_SSD), F32),
            jax.ShapeDtypeStruct((1, D_SSD), F32),
            jax.ShapeDtypeStruct((1, D_SSD), F32),
            jax.ShapeDtypeStruct((1, D_SSD), F32),
        ] + comm.out_shape,
        scratch_shapes=[pltpu.VMEM((gps, D_STATE, GROUP_W), F32)] + comm.scratch,
        compiler_params=_params(("arbitrary", "arbitrary"), side_effects=True),
    )(dmix, ytot, proj, xbc, dt_rep, hprev, a_rep, dsk_rep, wn, tri, *comm.operands)
    return res


def _stack_heads(t, h):
    return jnp.concatenate([t[:, (REP * h + r) * HEAD_DIM:(REP * h + r + 1) * HEAD_DIM] for r in range(REP)], axis=0)


def _band(t2, t1, t0, h):
    sl = slice(h * HEAD_DIM, (h + 1) * HEAD_DIM)
    return jnp.concatenate([t2[:, sl], t1[:, sl], t0[:, sl]], axis=0)


def _attn_probs(s, sink, qc):
    s = s * (HEAD_DIM ** -0.5)
    key_abs = (qc - WINDOW_CHUNKS) * CHUNK + lax.broadcasted_iota(jnp.int32, s.shape, 1)
    s = jnp.where(key_abs >= PAD_LEAD, s, -jnp.inf)
    m = jnp.maximum(jnp.max(s, axis=-1, keepdims=True), sink)
    p = jnp.exp(s - m)
    ps = jnp.exp(sink - m)
    denom = jnp.sum(p, axis=-1, keepdims=True) + ps
    return p / denom, ps / denom


ATT_QC = 2
ATT_KC = WINDOW_CHUNKS + ATT_QC


def _kv_specs(width, newest_chunk_of):
    return [pl.BlockSpec((CHUNK, width), functools.partial(lambda j, p: (jnp.maximum(newest_chunk_of(p) - j, 0), 0), j))
            for j in range(ATT_KC - 1, -1, -1)]


def _attn_fwd(qr, kr, vb, proj, sink_stack, mix):
    lp = qr.shape[0]
    nc = lp // CHUNK

    assert nc % ATT_QC == 0
    qrows = ATT_QC * CHUNK

    def body(q_ref, *rest):
        k_refs, v_refs = rest[:ATT_KC], rest[ATT_KC:2 * ATT_KC]
        g_ref, sink_ref, _, att_ref, mix_ref = rest[2 * ATT_KC:]
        p = pl.program_id(0)
        q = q_ref[...]
        ks = [r[...] for r in k_refs]
        vs = [r[...] for r in v_refs]
        units = [(u, h) for u in range(ATT_QC) for h in range(KV_HEADS)]
        s = [_dot(_stack_heads(q[u * CHUNK:(u + 1) * CHUNK, :], h), _band(*ks[u:u + 3], h), NT) for u, h in units]
        vbh = [_band(*vs[u:u + 3], h) for u, h in units]
        pn = [_attn_probs(s[i], sink_ref[h], ATT_QC * p + u)[0].astype(BF16) for i, (u, h) in enumerate(units)]
        o = [_dot(pn[i], vbh[i]) for i in range(len(units))]
        att = jnp.concatenate(
            [jnp.concatenate([o[u * KV_HEADS + h][r * CHUNK:(r + 1) * CHUNK, :] for h in range(KV_HEADS) for r in range(REP)],
                             axis=1) for u in range(ATT_QC)], axis=0)
        att_ref[...] = att
        g = g_ref[...]
        mix_ref[...] = (att * (g * _sigmoid(g))).astype(BF16)

    newest = lambda p: ATT_QC * p + ATT_QC - 1
    return pl.pallas_call(
        body, name="attn_fwd", grid=(nc // ATT_QC,),
        in_specs=[pl.BlockSpec((qrows, D_ATT), lambda p: (p, 0))] + _kv_specs(D_KV, newest) + _kv_specs(D_KV, newest) + [
            pl.BlockSpec((qrows, D_ATT), lambda p: (p, OFF_G // D_ATT)),
            pl.BlockSpec((KV_HEADS, REP * CHUNK, 1), lambda p: (0, 0, 0)),
            ANY,
        ],
        out_specs=[pl.BlockSpec((qrows, D_ATT), lambda p: (p, 0)),
                   pl.BlockSpec((qrows, D_ATT), lambda p: (p, D_SSD // D_ATT))],
        out_shape=[jax.ShapeDtypeStruct((lp, D_ATT), F32), jax.ShapeDtypeStruct((lp, D_MIX), BF16)],
        input_output_aliases={2 * ATT_KC + 3: 1},
        compiler_params=_params(("parallel",)),
    )(qr, *([kr] * ATT_KC), *([vb] * ATT_KC), proj, sink_stack, mix)


def _attn_bwd(qr, kr, vb, att, proj, dmix, sink_stack):
    lp = qr.shape[0]
    nc = lp // CHUNK
    assert nc % ATT_QC == 0
    npairs = nc // ATT_QC
    qrows = ATT_QC * CHUNK
    wrows = ATT_KC * CHUNK

    def body(q_ref, *rest):
        k_refs, v_refs = rest[:ATT_KC], rest[ATT_KC:2 * ATT_KC]
        (att_ref, g_ref, do_ref, sink_ref, dq_ref, dk_ref, dv_ref, dg_ref, dsink_ref, dk_acc, dv_acc) = rest[2 * ATT_KC:]
        step = pl.program_id(0)

        @pl.when(step == 0)
        def _():
            dk_acc[...] = jnp.zeros_like(dk_acc)
            dv_acc[...] = jnp.zeros_like(dv_acc)
            dsink_ref[...] = jnp.zeros_like(dsink_ref)

        @pl.when(step < npairs)
        def _():
            q = q_ref[...]
            ks = [r[...] for r in k_refs]
            vs = [r[...] for r in v_refs]
            att = att_ref[...]
            g = g_ref[...]
            dog = do_ref[...]
            sg = _sigmoid(g)
            dg_ref[...] = dog * att * (sg * (1.0 + g * (1.0 - sg)))
            do = dog * (g * sg)
            units = [(u, h) for u in range(ATT_QC) for h in range(KV_HEADS)]
            n = range(len(units))
            rows = [slice(u * CHUNK, (u + 1) * CHUNK) for u in range(ATT_QC)]
            qs = [_stack_heads(q[rows[u], :], h) for u, h in units]
            kb = [_band(*ks[u:u + 3], h) for u, h in units]
            vbh = [_band(*vs[u:u + 3], h) for u, h in units]
            dos = [_stack_heads(do[rows[u], :], h) for u, h in units]
            dosb = [dos[i].astype(BF16) for i in n]
            s = [_dot(qs[i], kb[i], NT) for i in n]
            dp = [_dot(dosb[i], vbh[i], NT) for i in n]
            ds, pnb = [], []
            for i, (u, h) in enumerate(units):
                pn, psink = _attn_probs(s[i], sink_ref[h], ATT_QC * step + u)
                delta = jnp.sum(dos[i] * _stack_heads(att[rows[u], :], h), axis=-1, keepdims=True)
                ds.append((pn * (dp[i] - delta)).astype(BF16))
                pnb.append(pn.astype(BF16))
                dsink_ref[h] += -(psink * delta)
            dqs = [_dot(ds[i], kb[i]) for i in n]
            dks = [_dot(ds[i], qs[i], TN) for i in n]
            dvs = [_dot(pnb[i], dosb[i], TN) for i in n]
            dq_ref[...] = jnp.concatenate(
                [jnp.concatenate([dqs[u * KV_HEADS + h][r * CHUNK:(r + 1) * CHUNK, :]
                                  for h in range(KV_HEADS) for r in range(REP)], axis=1) for u in range(ATT_QC)],
                axis=0) * (HEAD_DIM ** -0.5)
            for u in range(ATT_QC):
                band = slice(u * CHUNK, u * CHUNK + BAND)
                dk_acc[band, :] += jnp.concatenate(dks[u * KV_HEADS:(u + 1) * KV_HEADS], axis=1) * (HEAD_DIM ** -0.5)
                dv_acc[band, :] += jnp.concatenate(dvs[u * KV_HEADS:(u + 1) * KV_HEADS], axis=1)

        dk_ref[...] = dk_acc[0:qrows, :]
        dv_ref[...] = dv_acc[0:qrows, :]
        for acc in (dk_acc, dv_acc):
            rest_rows = acc[qrows:wrows, :]
            acc[0:wrows - qrows, :] = rest_rows
            acc[wrows - qrows:wrows, :] = jnp.zeros((qrows, D_KV), F32)

    qp = lambda p: jnp.minimum(p, npairs - 1)
    newest = lambda p: ATT_QC * qp(p) + ATT_QC - 1
    qblk = pl.BlockSpec((qrows, D_ATT), lambda p: (qp(p), 0))
    oldest = pl.BlockSpec((qrows, D_KV), lambda p: (jnp.maximum(p - 1, 0), 0))
    return pl.pallas_call(
        body, name="attn_bwd", grid=(npairs + 1,),
        in_specs=[qblk] + _kv_specs(D_KV, newest) + _kv_specs(D_KV, newest) + [
            qblk,
            pl.BlockSpec((qrows, D_ATT), lambda p: (qp(p), OFF_G // D_ATT)),
            pl.BlockSpec((qrows, D_ATT), lambda p: (qp(p), D_SSD // D_ATT)),
            pl.BlockSpec((KV_HEADS, REP * CHUNK, 1), lambda p: (0, 0, 0)),
        ],
        out_specs=[qblk, oldest, oldest, qblk, pl.BlockSpec((KV_HEADS, REP * CHUNK, 1), lambda p: (0, 0, 0))],
        out_shape=[
            jax.ShapeDtypeStruct((lp, D_ATT), F32),
            jax.ShapeDtypeStruct((lp, D_KV), F32),
            jax.ShapeDtypeStruct((lp, D_KV), F32),
            jax.ShapeDtypeStruct((lp, D_ATT), F32),
            jax.ShapeDtypeStruct((KV_HEADS, REP * CHUNK, 1), F32),
        ],
        scratch_shapes=[pltpu.VMEM((wrows, D_KV), F32), pltpu.VMEM((wrows, D_KV), F32)],
        compiler_params=_params(("arbitrary",)),
    )(qr, *([kr] * ATT_KC), *([vb] * ATT_KC), att, proj, dmix, sink_stack)


def _post_loss(out, x2d, target, w):
    lp = out.shape[0]
    nc = lp // CHUNK
    nx = x2d.shape[0] // CHUNK

    k = _tile(nc, (ROW_K, 3, 2, 1))

    def body(o_ref, *rest):
        x_refs, t_refs = rest[:k], rest[k:2 * k]
        w_ref, do_ref, dy_ref, gw_ref, loss_ref = rest[2 * k:]
        i = pl.program_id(0)

        @pl.when(i == 0)
        def _():
            gw_ref[...] = jnp.zeros_like(gw_ref)
            loss_ref[...] = jnp.zeros_like(loss_ref)

        o = o_ref[...]
        w = w_ref[...]
        rstd = lax.rsqrt(jnp.mean(o * o, axis=-1, keepdims=True) + EPS)
        xhat = o * rstd
        x = jnp.concatenate([r[...] for r in x_refs], axis=0)
        t = jnp.concatenate([r[...] for r in t_refs], axis=0)
        chunk = _chunk_of_row(i, k)
        err = (x + xhat * w - t) * ((chunk > 0) & (chunk <= nx)).astype(F32)
        loss_ref[...] += 0.5 * jnp.sum(jnp.mean(err * err, axis=-1, keepdims=True), axis=0, keepdims=True)
        dy = err * (1.0 / D_MODEL)
        dy_ref[...] = dy
        gw_ref[...] += jnp.sum(dy * xhat, axis=0, keepdims=True)
        dxhat = dy * w
        do_ref[...] = (rstd * (dxhat - xhat * jnp.mean(dxhat * xhat, axis=-1, keepdims=True))).astype(BF16)

    row = pl.BlockSpec((k * CHUNK, D_MODEL), lambda i: (i, 0))
    return pl.pallas_call(
        body, name="post_loss", grid=(nc // k,),
        in_specs=[row] + _x_specs(nx, k) + _x_specs(nx, k) + [pl.BlockSpec((1, D_MODEL), lambda i: (0, 0))],
        out_specs=[row, row, pl.BlockSpec((1, D_MODEL), lambda i: (0, 0)), pl.BlockSpec((1, 128), lambda i: (0, 0))],
        out_shape=[
            jax.ShapeDtypeStruct((lp, D_MODEL), BF16),
            jax.ShapeDtypeStruct((lp, D_MODEL), F32),
            jax.ShapeDtypeStruct((1, D_MODEL), F32),
            jax.ShapeDtypeStruct((1, 128), F32),
        ],
        compiler_params=_params(("arbitrary",)),
    )(out, *([x2d] * k), *([target] * k), w)


def _prenorm_bwd(dhn, x2d, h0, dy, w):
    nx = x2d.shape[0] // CHUNK
    k = _tile(nx, (X_K, 4, 2, 1))
    rows = k * CHUNK

    def backward(h, dhn, w):
        rstd = lax.rsqrt(jnp.mean(h * h, axis=-1, keepdims=True) + EPS)
        xhat = h * rstd
        dxhat = dhn * w
        dh = rstd * (dxhat - xhat * jnp.mean(dxhat * xhat, axis=-1, keepdims=True))
        return dh, jnp.sum(dhn * xhat, axis=0, keepdims=True)

    def body(*refs):
        dhn_refs, dy_refs = refs[:k], refs[k:2 * k]
        x_ref, w_ref, gx_ref, gw_ref = refs[2 * k:]

        @pl.when(pl.program_id(0) == 0)
        def _():
            gw_ref[...] = jnp.zeros_like(gw_ref)

        dh, gw = backward(x_ref[...], jnp.concatenate([r[...] for r in dhn_refs], axis=0), w_ref[...])
        gx_ref[...] = dh + jnp.concatenate([r[...] for r in dy_refs], axis=0)
        gw_ref[...] += gw

    def body_meta(dhn_ref, h0_ref, w_ref, d0_ref, gw_ref):
        d0_ref[...], gw_ref[...] = backward(h0_ref[...], dhn_ref[...], w_ref[...])

    chunk_specs = [pl.BlockSpec((CHUNK, D_MODEL), functools.partial(lambda u, i: (k * i + 1 + u, 0), u)) for u in range(k)]
    first = pl.BlockSpec((CHUNK, D_MODEL), lambda i: (0, 0))
    vec = pl.BlockSpec((1, D_MODEL), lambda i: (0, 0))
    wide = pl.BlockSpec((rows, D_MODEL), lambda i: (i, 0))
    gx, gw_x = pl.pallas_call(
        body, name="prenorm_bwd", grid=(nx // k,),
        in_specs=chunk_specs + chunk_specs + [wide, vec],
        out_specs=[wide, vec],
        out_shape=[jax.ShapeDtypeStruct((nx * CHUNK, D_MODEL), F32), jax.ShapeDtypeStruct((1, D_MODEL), F32)],
        compiler_params=_params(("arbitrary",)),
    )(*([dhn] * k), *([dy] * k), x2d, w)
    d0, gw_0 = pl.pallas_call(
        body_meta, name="prenorm_bwd_meta", grid=(1,),
        in_specs=[first, first, vec], out_specs=[first, vec],
        out_shape=[jax.ShapeDtypeStruct((CHUNK, D_MODEL), F32), jax.ShapeDtypeStruct((1, D_MODEL), F32)],
        compiler_params=_params(("arbitrary",)),
    )(dhn, h0, w)
    return gx, d0, gw_x + gw_0


def _adamw(slabs, w, m, v, name):
    rows, cols = w.shape
    tr = _tile(rows, (256, 128, 64, 16, 8))
    c1 = 1.0 - ADAM_B1 ** ADAM_STEP
    c2 = 1.0 - ADAM_B2 ** ADAM_STEP

    def body(s_ref, w_ref, m_ref, v_ref, g_ref, d_ref, mo_ref, vo_ref):
        g = s_ref[0].astype(F32)
        for k in range(1, slabs.shape[0]):
            g = g + s_ref[k].astype(F32)
        w = w_ref[...]
        m = ADAM_B1 * m_ref[...] + (1.0 - ADAM_B1) * g
        v = ADAM_B2 * v_ref[...] + (1.0 - ADAM_B2) * (g * g)
        g_ref[...] = g
        mo_ref[...] = m
        vo_ref[...] = v
        d_ref[...] = -ADAM_LR * ((m / c1) / (jnp.sqrt(v / c2) + ADAM_EPS) + ADAM_WD * w)

    blk = pl.BlockSpec((tr, cols), lambda i: (i, 0))
    shape = jax.ShapeDtypeStruct((rows, cols), F32)
    return pl.pallas_call(
        body, name=name, grid=(rows // tr,),
        in_specs=[pl.BlockSpec((slabs.shape[0], tr, cols), lambda i: (0, i, 0)), blk, blk, blk],
        out_specs=[blk, blk, blk, blk],
        out_shape=[shape, shape, shape, shape],
        compiler_params=_params(("parallel",)),
    )(slabs, w, m, v)


def _perm_xbc(a):
    lead = a.shape[:-1]
    xs = a[..., :D_SSD].reshape(lead + (N_GROUPS, GROUP_W))
    b = a[..., D_SSD:D_SSD + N_GROUPS * D_STATE].reshape(lead + (N_GROUPS, D_STATE))
    c = a[..., D_SSD + N_GROUPS * D_STATE:].reshape(lead + (N_GROUPS, D_STATE))
    return jnp.concatenate([xs, b, c], axis=-1).reshape(lead + (D_CONV,))


def _unperm_xbc(a):
    lead = a.shape[:-1]
    t = a.reshape(lead + (N_GROUPS, XBC_BLK))
    xs = t[..., :GROUP_W].reshape(lead + (D_SSD,))
    b = t[..., GROUP_W:GROUP_W + D_STATE].reshape(lead + (N_GROUPS * D_STATE,))
    c = t[..., GROUP_W + D_STATE:].reshape(lead + (N_GROUPS * D_STATE,))
    return jnp.concatenate([xs, b, c], axis=-1)


R_Z, R_XBC, R_DT, R_Q, R_K, R_V, R_G = 0, 2048, 6144, 6176, 7200, 7456, 7712


def _internal_of_reference():
    ref = np.arange(D_IN_PROJ)
    out = np.empty(D_IN_PROJ, np.int64)
    out[R_Z:R_XBC] = OFF_Z + ref[:D_SSD]
    xs = np.arange(D_SSD)
    out[R_XBC:R_XBC + D_SSD] = OFF_XBC + (xs // GROUP_W) * XBC_BLK + xs % GROUP_W
    bc = np.arange(N_GROUPS * D_STATE)
    out[R_XBC + D_SSD:R_XBC + D_SSD + N_GROUPS * D_STATE] = OFF_XBC + (bc // D_STATE) * XBC_BLK + GROUP_W + bc % D_STATE
    out[R_XBC + D_SSD + N_GROUPS * D_STATE:R_DT] = OFF_XBC + (bc // D_STATE) * XBC_BLK + GROUP_W + D_STATE + bc % D_STATE
    out[R_DT:R_Q] = OFF_DT + np.arange(SSD_HEADS)
    out[R_Q:R_K] = OFF_Q + np.arange(D_ATT)
    out[R_K:R_V] = OFF_K + np.arange(D_KV)
    out[R_V:R_G] = OFF_V + np.arange(D_KV)
    out[R_G:] = OFF_G + np.arange(D_ATT)
    return out


def _runs(src, dst_break):
    runs, lo = [], 0
    for i in range(1, len(src) + 1):
        if i == len(src) or src[i] != src[i - 1] + 1 or dst_break[i] != dst_break[i - 1]:
            runs.append((lo, i))
            lo = i
    return runs


RELAYOUT_ROWS = 256


def _lane_window(ref, lead, c0, n):
    a0 = c0 // 128 * 128
    a1 = min(-(-(c0 + n) // 128) * 128, ref.shape[-1])
    return ref[lead + (slice(None), slice(a0, a1))][:, c0 - a0:c0 - a0 + n]


def _w_in_internal(w_gathered, half, w_all=None):
    int_of_ref = _internal_of_reference()
    ref_of_int = np.full(NP, -1, np.int64)
    ref_of_int[int_of_ref] = np.arange(D_IN_PROJ)
    shard = np.where(ref_of_int >= 0, ref_of_int // SHARD_IN, -1)
    src = np.where(ref_of_int >= 0, ref_of_int, -10 - 2 * np.arange(NP))
    plan, zeros = [], 0
    for lo, hi in _runs(src, shard):
        if ref_of_int[lo] < 0:
            zeros += hi - lo
            continue
        if zeros:
            plan.append((None, 0, zeros))
            zeros = 0
        plan.append((int(ref_of_int[lo] // SHARD_IN), int(ref_of_int[lo] % SHARD_IN), hi - lo))
    if zeros:
        plan.append((None, 0, zeros))
    tr = RELAYOUT_ROWS

    def body(w_ref, *rest):
        o_ref = rest[-1]
        o_ref[...] = jnp.concatenate(
            [jnp.zeros((tr, n), o_ref.dtype) if s is None else _lane_window(w_ref, (s,), c0, n) for s, c0, n in plan], axis=1)

    steps = w_gathered.shape[1] // tr
    return pl.pallas_call(
        body, name=f"w_in_relayout_{half}", grid=(steps,),
        in_specs=[pl.BlockSpec((N_DEV, tr, SHARD_IN), lambda i: (0, i, 0))] + ([] if w_all is None else [ANY]),
        out_specs=pl.BlockSpec((tr, NP), lambda i: (half * steps + i, 0)),
        out_shape=jax.ShapeDtypeStruct((D_MODEL, NP), w_gathered.dtype),
        input_output_aliases={} if w_all is None else {1: 0},
        compiler_params=_params(("parallel",)),
    )(w_gathered, *([] if w_all is None else [w_all]))


def _w_in_slabs(dw):
    int_of_ref = _internal_of_reference()
    plan = []
    for s in range(N_DEV):
        cols = int_of_ref[s * SHARD_IN:(s + 1) * SHARD_IN]
        plan.append([(int(cols[lo]), hi - lo) for lo, hi in _runs(cols, np.zeros_like(cols))])
    tr = RELAYOUT_ROWS

    def body(dw_ref, o_ref):
        for s in range(N_DEV):
            o_ref[s] = jnp.concatenate([_lane_window(dw_ref, (), c0, n) for c0, n in plan[s]], axis=1)

    return pl.pallas_call(
        body, name="dw_in_relayout", grid=(D_MODEL // tr,),
        in_specs=[pl.BlockSpec((tr, NP), lambda i: (i, 0))],
        out_specs=pl.BlockSpec((N_DEV, tr, SHARD_IN), lambda i: (0, i, 0)),
        out_shape=jax.ShapeDtypeStruct((N_DEV, D_MODEL, SHARD_IN), dw.dtype),
        compiler_params=_params(("parallel",)),
    )(dw)


def _rep_heads(a):
    return jnp.repeat(a, HEAD_DIM, axis=1)


SMALL = (("norm_pre_w", 2048), ("conv_b", 4096), ("dt_bias", 32), ("a_log", 32), ("d_skip", 32),
         ("ssd_norm_w", 2048), ("attn_sinks", 16), ("norm_post_w", 2048))
SMALL_USED = sum(size for _, size in SMALL)
SMALL_LEN = 10368


def _pack_small(d, loss=None):
    parts = [d[name].reshape(1, size) for name, size in SMALL]
    tail = jnp.zeros((1, SMALL_LEN - SMALL_USED), F32)
    if loss is not None:
        tail = tail.at[0, 0].set(loss)
    return jnp.concatenate(parts + [tail], axis=1)


def _unpack_small(vec):
    out, off = {}, 0
    for name, size in SMALL:
        out[name] = vec[:, off:off + size]
        off += size
    return out


def kernel(x, meta_tokens, norm_pre_w, w_in, conv_w, conv_b, dt_bias, a_log, d_skip, ssd_norm_w, attn_sinks, w_out, norm_post_w, loss_target, m_meta_tokens, m_norm_pre_w, m_w_in, m_conv_w, m_conv_b, m_dt_bias, m_a_log, m_d_skip, m_ssd_norm_w, m_attn_sinks, m_w_out, m_norm_post_w, v_meta_tokens, v_norm_pre_w, v_w_in, v_conv_w, v_conv_b, v_dt_bias, v_a_log, v_d_skip, v_ssd_norm_w, v_attn_sinks, v_w_out, v_norm_post_w):
    seq = x.shape[1]
    lp = seq + 2 * CHUNK
    x2d = x[0]

    w_in_bf = w_in[0].astype(BF16)
    kh = D_MODEL // 2
    w_top_g, conv_w_g, meta_g = _gather_two_level([w_in_bf[:kh], conv_w[0], meta_tokens], "gather_w_in_top")
    w_all = _w_in_internal(w_top_g, 0)
    conv_w_full = _perm_xbc(jnp.transpose(conv_w_g, (1, 0, 2)).reshape(CONV_WIDTH, D_CONV))
    conv_b_int = _perm_xbc(conv_b)
    meta_full = jnp.transpose(meta_g, (1, 0, 2)).reshape(N_META, D_MODEL)
    h0 = jnp.concatenate([jnp.zeros((PAD_LEAD, D_MODEL), F32), meta_full], axis=0)

    pos = (jnp.arange(lp) - PAD_LEAD).astype(F32)
    half = HEAD_DIM // 2
    inv = ROPE_THETA ** (-jnp.arange(half, dtype=F32) / half)
    ang = pos[:, None] * inv[None, :]
    cos_t = jnp.tile(jnp.cos(ang), (1, 4))
    sin_t = jnp.tile(jnp.concatenate([-jnp.sin(ang), jnp.sin(ang)], axis=1), (1, 2))
    head_of_col = np.arange(D_SSD) // HEAD_DIM
    expand = jnp.asarray((np.arange(128)[:, None] == head_of_col[None, :]).astype(np.float32))
    reduce_t = jnp.asarray((head_of_col[:, None] == np.arange(128)[None, :]).astype(np.float32))
    tri = jnp.asarray(np.tril(np.ones((CHUNK, CHUNK), np.float32)))
    a_rep = _rep_heads(-jnp.exp(a_log))
    dsk_rep = _rep_heads(d_skip)
    dt_bias_rep = _rep_heads(dt_bias)
    sink_stack = jnp.repeat(attn_sinks.reshape(KV_HEADS, REP), CHUNK, axis=1).reshape(KV_HEADS, REP * CHUNK, 1)

    hn = _prenorm(x2d, h0, norm_pre_w)
    tm = _tile(lp, (1056, 704, 128, 64))
    proj, w_bot_g = _in_proj_half(hn, w_all, 0, tm=tm, tn=1536, name="in_proj_top",
                                  comm=_GatherTwoLevel([w_in_bf[kh:]]))
    w_all = _w_in_internal(w_bot_g, 1, w_all)
    proj, w_out_g = _in_proj_half(hn, w_all, 1, tm=tm, tn=1536, name="in_proj_bottom", prev=proj,
                                  comm=_GatherTwoLevel([w_out[0].astype(BF16)]))
    w_out_full = w_out_g.reshape(D_MIX, D_MODEL)
    xbc = _conv_fwd(proj, conv_w_full, conv_b_int)
    qr, kr, vb, dt_rep = _act_fwd(proj, cos_t, sin_t, expand, dt_bias_rep)
    mix, ytot, hprev = _ssd_fwd(xbc, dt_rep, proj, a_rep, dsk_rep, ssd_norm_w, tri)
    att, mix = _attn_fwd(qr, kr, vb, proj, sink_stack, mix)
    out = _matmul(mix, w_out_full, tm=tm, tn=1024, tk=D_MIX, out_dtype=F32, name="out_proj")
    dout, dy, g_norm_post, loss_part = _post_loss(out, x2d, loss_target[0], norm_post_w)

    dmix = _matmul(dout, w_out_full, trans_b=True, tm=tm, tn=1024, tk=D_MODEL, out_dtype=F32, name="dmix")
    dw_out = _matmul(mix, dout, trans_a=True, tm=512, tn=1024, tk=lp, out_dtype=BF16, name="dw_out")
    dqr, dkr, dv, dg, dsink_rows = _attn_bwd(qr, kr, vb, att, proj, dmix, sink_stack)
    dproj, dxbc, ddt_part, dd_part, da_part, g_ssd_norm, g_out = _ssd_bwd(
        dmix, ytot, proj, xbc, dt_rep, hprev, a_rep, dsk_rep, ssd_norm_w, tri,
        _Comm([(dw_out.reshape(N_DEV, D_MIX // N_DEV, D_MODEL), "scatter")]))
    dproj, dconv_w_int, dconv_b_int = _conv_bwd(dxbc, proj, conv_w_full, conv_b_int, dproj)
    dproj, ddt_bias = _act_bwd(dqr, dkr, dv, dg, ddt_part, cos_t, sin_t, reduce_t, dproj)
    dw_all = _matmul(hn, dproj, trans_a=True, tm=512, tn=1024, tk=lp, out_dtype=BF16, name="dw_in")
    dw_slabs = _w_in_slabs(dw_all)
    dw_chip = _pair_sum(dw_slabs, _exchange_sibling(dw_slabs, "dw_in_sibling"), "dw_in_pair_sum")
    dhn, g_in = _matmul(dproj, w_all, trans_b=True, tm=tm, tn=1024, tk=1536, out_dtype=F32, name="dhn",
                        comm=_Comm([(dw_chip, "scatter")], scope="chips"))
    grad_x, dh0, g_norm_pre = _prenorm_bwd(dhn, x2d, h0, dy, norm_pre_w)

    dmeta = dh0[PAD_LEAD:, :]
    dconv_w_ref = _unperm_xbc(dconv_w_int)
    heads = lambda part: part.reshape(SSD_HEADS, HEAD_DIM).sum(axis=1).reshape(1, SSD_HEADS)
    small_local = _pack_small({
        "norm_pre_w": g_norm_pre, "conv_b": _unperm_xbc(dconv_b_int), "dt_bias": ddt_bias[:, :SSD_HEADS],
        "a_log": heads(da_part) * (-jnp.exp(a_log)), "d_skip": heads(dd_part), "ssd_norm_w": g_ssd_norm,
        "attn_sinks": dsink_rows.reshape(Q_HEADS, CHUNK).sum(axis=1).reshape(1, Q_HEADS),
        "norm_post_w": g_norm_post}, loss=loss_part[0, 0])
    g_conv, g_meta, g_small = _exchange(
        [(jnp.transpose(dconv_w_ref.reshape(CONV_WIDTH, N_DEV, D_CONV // N_DEV), (1, 0, 2)), "scatter"),
         (jnp.transpose(dmeta.reshape(N_META, N_DEV, D_MODEL // N_DEV), (1, 0, 2)), "scatter"),
         (small_local, "gather")], "exchange_small")

    res = {}
    res["w_in"] = [o[None] for o in _adamw(g_in, w_in[0], m_w_in[0], v_w_in[0], "adamw_w_in")]
    res["w_out"] = [o[None] for o in _adamw(g_out, w_out[0], m_w_out[0], v_w_out[0], "adamw_w_out")]
    res["conv_w"] = [o[None] for o in _adamw(g_conv, conv_w[0], m_conv_w[0], v_conv_w[0], "adamw_conv_w")]
    res["meta_tokens"] = _adamw(g_meta, meta_tokens, m_meta_tokens, v_meta_tokens, "adamw_meta")
    given = dict(norm_pre_w=(norm_pre_w, m_norm_pre_w, v_norm_pre_w), conv_b=(conv_b, m_conv_b, v_conv_b),
                 dt_bias=(dt_bias, m_dt_bias, v_dt_bias), a_log=(a_log, m_a_log, v_a_log),
                 d_skip=(d_skip, m_d_skip, v_d_skip), ssd_norm_w=(ssd_norm_w, m_ssd_norm_w, v_ssd_norm_w),
                 attn_sinks=(attn_sinks, m_attn_sinks, v_attn_sinks),
                 norm_post_w=(norm_post_w, m_norm_post_w, v_norm_post_w))
    packed = [_pack_small({k: t[j] for k, t in given.items()}) for j in range(3)]
    small_out = _adamw(g_small, packed[0], packed[1], packed[2], "adamw_small")
    small_res = [_unpack_small(r) for r in small_out]
    loss = small_out[0][0, SMALL_USED]

    order = ["meta_tokens", "norm_pre_w", "w_in", "conv_w", "conv_b", "dt_bias", "a_log", "d_skip", "ssd_norm_w",
             "attn_sinks", "w_out", "norm_post_w"]
    outs = []
    for j in range(4):
        for name in order:
            outs.append(res[name][j] if name in res else small_res[j][name])
    return (loss, grad_x[None], *outs)
```

```python
import functools

import numpy as np
import jax
import jax.numpy as jnp
from jax import lax
from jax.experimental import pallas as pl
from jax.experimental.pallas import tpu as pltpu

F32 = jnp.float32
BF16 = jnp.bfloat16
HIGHEST = lax.Precision.HIGHEST

N_DEV = 8
D_MODEL = 2048
CHUNK = 64
N_META = 16
PAD_LEAD = CHUNK - N_META
EPS = 1e-6
N_GROUPS = 8
HEADS_PER_GROUP = 4
HEAD_DIM = 64
GROUP_W = HEADS_PER_GROUP * HEAD_DIM
D_STATE = 128
D_SSD = 2048
D_CONV = 4096
SSD_HEADS = 32
CONV_WIDTH = 4
Q_HEADS = 16
KV_HEADS = 4
REP = 4
D_ATT = 1024
D_KV = 256
WINDOW_CHUNKS = 2
BAND = (WINDOW_CHUNKS + 1) * CHUNK
ROPE_THETA = 10000.0
D_MIX = D_SSD + D_ATT
D_IN_PROJ = 8736
SHARD_IN = D_IN_PROJ // N_DEV

OFF_Z, OFF_XBC, OFF_Q, OFF_G, OFF_K, OFF_V, OFF_DT = 0, 2048, 6144, 7168, 8192, 8448, 8704
NP = 9216
TAIL_W = NP - OFF_Q
XBC_BLK = 512
SSD_GPS = 8
SSD_GPS_BWD = 8

ADAM_LR, ADAM_B1, ADAM_B2, ADAM_EPS, ADAM_WD, ADAM_STEP = 0.001, 0.9, 0.999, 1e-08, 0.01, 10

VMEM_LIMIT = 48 * 1024 * 1024

NN = (((1,), (0,)), ((), ()))
NT = (((1,), (1,)), ((), ()))
TN = (((0,), (0,)), ((), ()))
ANY = pl.BlockSpec(memory_space=pl.ANY)


def _dot(a, b, dims=NN, precision=None):
    return lax.dot_general(a, b, dims, precision=precision, preferred_element_type=F32)


def _tile(n, prefs):
    for t in prefs:
        if n % t == 0:
            return t
    return n


def _params(sem, vmem=VMEM_LIMIT, side_effects=False):
    return pltpu.CompilerParams(dimension_semantics=sem, vmem_limit_bytes=vmem, has_side_effects=side_effects)


def _sigmoid(x):
    return 1.0 / (1.0 + jnp.exp(-x))


class _Comm:
    def __init__(self, items, scope="devices"):
        self.items = items
        self.scope = scope
        self.slabs = slabs = N_DEV if scope == "devices" else N_DEV // 2
        self.n = n = len(items)
        self.operands = [arr for arr, _ in items]
        self.in_specs = [ANY] * n
        self.out_specs = [ANY] * n
        self.out_shape = [jax.ShapeDtypeStruct((slabs,) + tuple(arr.shape) if kind == "gather" else tuple(arr.shape),
                                               arr.dtype) for arr, kind in items]
        self.scratch = [pltpu.SemaphoreType.DMA((n, slabs - 1)), pltpu.SemaphoreType.DMA((n, slabs - 1)),
                        pltpu.SemaphoreType.DMA((n,))]

    def _places(self):
        pos = (lax.axis_index("x"), lax.axis_index("y"), lax.axis_index("c"))
        if self.scope == "devices":
            index = lambda p: 4 * p[0] + 2 * p[1] + p[2]
            masks = range(1, N_DEV)
        else:
            index = lambda p: 2 * p[0] + p[1]
            masks = (2, 4, 6)
        peers = []
        for k in masks:
            p = tuple(1 - pos[b] if (k >> (2 - b)) & 1 else pos[b] for b in range(3))
            peers.append((p, index(p)))
        return index(pos), peers

    def _copies(self, ins, outs, sems, landed):
        send_sems, recv_sems, local_sems = sems
        me, peers = self._places()
        local, remote = [], []
        for a, (_, kind) in enumerate(self.items):
            own = ins[a] if kind == "gather" else ins[a].at[me]
            local.append(pltpu.make_async_copy(own, outs[a].at[me], local_sems.at[a]))
            for k, (p, pid) in enumerate(peers):
                remote.append(pltpu.make_async_remote_copy(
                    src_ref=ins[a] if kind == "gather" else ins[a].at[pid],
                    dst_ref=outs[a].at[pid if landed else me],
                    send_sem=send_sems.at[a, k], recv_sem=recv_sems.at[a, k],
                    device_id=p, device_id_type=pl.DeviceIdType.MESH))
        return local, remote

    def start(self, ins, outs, sems):
        local, remote = self._copies(ins, outs, sems, landed=False)
        for cp in local + remote:
            cp.start()

    def wait(self, ins, outs, sems):
        local, remote = self._copies(ins, outs, sems, landed=True)
        for cp in remote + local:
            cp.wait()


def _exchange(items, name):
    comm = _Comm(items)
    n = comm.n

    def body(*refs):
        ins, outs, sems = refs[:n], refs[n:2 * n], refs[2 * n:]
        comm.start(ins, outs, sems)
        comm.wait(ins, outs, sems)

    return pl.pallas_call(
        body, name=name, in_specs=comm.in_specs, out_specs=comm.out_specs, out_shape=comm.out_shape,
        scratch_shapes=comm.scratch, compiler_params=pltpu.CompilerParams(has_side_effects=True),
    )(*comm.operands)


def _exchange_sibling(slabs, name):
    def body(g_ref, land_ref, send_sems, recv_sems):
        x, y, c = lax.axis_index("x"), lax.axis_index("y"), lax.axis_index("c")
        copies = [pltpu.make_async_remote_copy(
            src_ref=g_ref.at[2 * q + (1 - c)], dst_ref=land_ref.at[q], send_sem=send_sems.at[q], recv_sem=recv_sems.at[q],
            device_id=(x, y, 1 - c), device_id_type=pl.DeviceIdType.MESH) for q in range(N_DEV // 2)]
        for cp in copies:
            cp.start()
        for cp in copies:
            cp.wait()

    return pl.pallas_call(
        body, name=name, in_specs=[ANY], out_specs=ANY,
        out_shape=jax.ShapeDtypeStruct((N_DEV // 2,) + tuple(slabs.shape[1:]), slabs.dtype),
        scratch_shapes=[pltpu.SemaphoreType.DMA((N_DEV // 2,)), pltpu.SemaphoreType.DMA((N_DEV // 2,))],
        compiler_params=pltpu.CompilerParams(has_side_effects=True),
    )(slabs)


def _pair_sum(slabs, landed, name):
    _, rows, cols = slabs.shape
    tr = _tile(rows, (256, 128, 64, 16, 8))

    def body(g_ref, l_ref, o_ref):
        mine = g_ref[lax.axis_index("c")]
        o_ref[...] = (mine.astype(F32) + l_ref[...].astype(F32)).astype(o_ref.dtype)

    return pl.pallas_call(
        body, name=name, grid=(N_DEV // 2, rows // tr),
        in_specs=[pl.BlockSpec((None, 2, tr, cols), lambda q, i: (q, 0, i, 0)),
                  pl.BlockSpec((None, tr, cols), lambda q, i: (q, i, 0))],
        out_specs=pl.BlockSpec((None, tr, cols), lambda q, i: (q, i, 0)),
        out_shape=jax.ShapeDtypeStruct((N_DEV // 2, rows, cols), slabs.dtype),
        compiler_params=_params(("parallel", "parallel")),
    )(slabs.reshape(N_DEV // 2, 2, rows, cols), landed)


class _GatherTwoLevel:
    def __init__(self, arrays):
        self.arrays = arrays
        self.n = n = len(arrays)
        self.operands = list(arrays)
        self.in_specs = [ANY] * n
        self.out_specs = [ANY] * n
        self.out_shape = [jax.ShapeDtypeStruct((N_DEV,) + tuple(a.shape), a.dtype) for a in arrays]
        self.scratch = [pltpu.SemaphoreType.DMA((n, N_DEV - 1)), pltpu.SemaphoreType.DMA((n, N_DEV - 1)),
                        pltpu.SemaphoreType.DMA((n,))]

    def _plan(self, ins, outs, sems):
        send_sems, recv_sems, local_sems = sems
        x, y, c = lax.axis_index("x"), lax.axis_index("y"), lax.axis_index("c")
        me, sibling = (x, y, c), (x, y, 1 - c)
        chips = [(1 - x, y), (x, 1 - y), (1 - x, 1 - y)]

        def slab(a, place):
            return outs[a].at[4 * place[0] + 2 * place[1] + place[2]]

        def copy(a, k, block, to, src=None):
            return pltpu.make_async_remote_copy(
                src_ref=slab(a, block) if src is None else src, dst_ref=slab(a, block),
                send_sem=send_sems.at[a, k], recv_sem=recv_sems.at[a, k],
                device_id=to, device_id_type=pl.DeviceIdType.MESH)

        own, mine = [], []
        for a in range(self.n):
            mine.append(pltpu.make_async_copy(ins[a], slab(a, me), local_sems.at[a]))
            own.append(copy(a, 0, me, sibling, src=ins[a]))
            own += [copy(a, 1 + j, me, (*chip, c), src=ins[a]) for j, chip in enumerate(chips)]
        return me, sibling, chips, c, copy, own, mine

    def start(self, ins, outs, sems):
        _, _, _, _, _, own, mine = self._plan(ins, outs, sems)
        for cp in mine + own:
            cp.start()

    def wait(self, ins, outs, sems):
        me, sibling, chips, c, copy, own, mine = self._plan(ins, outs, sems)
        forwards = []
        for j, chip in enumerate(chips):
            for a in range(self.n):
                copy(a, 1 + j, (*chip, c), me).wait_recv()
                fwd = copy(a, 4 + j, (*chip, c), sibling)
                fwd.start()
                forwards.append(fwd)
        for a in range(self.n):
            copy(a, 0, sibling, me).wait_recv()
            for j, chip in enumerate(chips):
                copy(a, 4 + j, (*chip, 1 - c), me).wait_recv()
        for cp in own + forwards:
            cp.wait_send()
        for loc in mine:
            loc.wait()


def _gather_two_level(arrays, name):
    comm = _GatherTwoLevel(arrays)
    n = comm.n

    def body(*refs):
        ins, outs, sems = refs[:n], refs[n:2 * n], refs[2 * n:]
        comm.start(ins, outs, sems)
        comm.wait(ins, outs, sems)

    return pl.pallas_call(
        body, name=name, in_specs=comm.in_specs, out_specs=comm.out_specs, out_shape=comm.out_shape,
        scratch_shapes=comm.scratch, compiler_params=pltpu.CompilerParams(has_side_effects=True),
    )(*arrays)


def _matmul(a, b, *, tm, tn, tk, out_dtype, name, trans_a=False, trans_b=False, comm=None):
    m, k = (a.shape[1], a.shape[0]) if trans_a else a.shape
    n = b.shape[0] if trans_b else b.shape[1]
    nk = k // tk
    dims = TN if trans_a else (NT if trans_b else NN)
    assert not (trans_a and trans_b)
    nc = comm.n if comm else 0
    grid = (m // tm, n // tn, nk)

    def body(*refs):
        a_ref, b_ref = refs[:2]
        cin = refs[2:2 + nc]
        o_ref = refs[2 + nc]
        cout = refs[3 + nc:3 + 2 * nc]
        scratch = refs[3 + 2 * nc:]
        sems = scratch[len(scratch) - 3:] if comm else None
        i, j, kk = pl.program_id(0), pl.program_id(1), pl.program_id(2)
        if comm:
            @pl.when((i == 0) & (j == 0) & (kk == 0))
            def _():
                comm.start(cin, cout, sems)

        if nk == 1:
            o_ref[...] = _dot(a_ref[...], b_ref[...], dims).astype(out_dtype)
        else:
            acc_ref = scratch[0]

            @pl.when(kk == 0)
            def _():
                acc_ref[...] = jnp.zeros_like(acc_ref)

            acc_ref[...] += _dot(a_ref[...], b_ref[...], dims)

            @pl.when(kk == nk - 1)
            def _():
                o_ref[...] = acc_ref[...].astype(out_dtype)

        if comm:
            @pl.when((i == grid[0] - 1) & (j == grid[1] - 1) & (kk == nk - 1))
            def _():
                comm.wait(cin, cout, sems)

    a_spec = (pl.BlockSpec((tk, tm), lambda i, j, kk: (kk, i)) if trans_a
              else pl.BlockSpec((tm, tk), lambda i, j, kk: (i, kk)))
    b_spec = (pl.BlockSpec((tn, tk), lambda i, j, kk: (j, kk)) if trans_b
              else pl.BlockSpec((tk, tn), lambda i, j, kk: (kk, j)))
    sem = ("arbitrary",) * 3 if comm else ("parallel", "parallel", "arbitrary")
    res = pl.pallas_call(
        body, name=name, grid=grid,
        in_specs=[a_spec, b_spec] + (comm.in_specs if comm else []),
        out_specs=[pl.BlockSpec((tm, tn), lambda i, j, kk: (i, j))] + (comm.out_specs if comm else []),
        out_shape=[jax.ShapeDtypeStruct((m, n), out_dtype)] + (comm.out_shape if comm else []),
        scratch_shapes=([] if nk == 1 else [pltpu.VMEM((tm, tn), F32)]) + (comm.scratch if comm else []),
        compiler_params=_params(sem, side_effects=bool(comm)),
    )(a, b, *(comm.operands if comm else []))
    return res if comm else res[0]


ROW_K = 6
X_K = 8


def _x_specs(nx, k):
    return [pl.BlockSpec((CHUNK, D_MODEL), functools.partial(lambda u, i: (jnp.clip(k * i + u - 1, 0, nx - 1), 0), u))
            for u in range(k)]


def _chunk_of_row(i, k):
    return k * i + lax.broadcasted_iota(jnp.int32, (k * CHUNK, 1), 0) // CHUNK


def _prenorm(x2d, h0, w):
    nx = x2d.shape[0] // CHUNK
    nc = nx + 2
    k = _tile(nc, (ROW_K, 3, 2, 1))

    def body(*refs):
        x_refs = refs[:k]
        h0_ref, w_ref, o_ref = refs[k:]
        i = pl.program_id(0)
        x = jnp.concatenate([r[...] for r in x_refs], axis=0)
        head = jnp.concatenate([h0_ref[...], x[CHUNK:, :]], axis=0)
        chunk = _chunk_of_row(i, k)
        h = jnp.where(i == 0, head, x) * (chunk <= nx).astype(F32)
        rstd = lax.rsqrt(jnp.mean(h * h, axis=-1, keepdims=True) + EPS)
        o_ref[...] = (h * rstd * w_ref[...]).astype(BF16)

    rows = k * CHUNK
    return pl.pallas_call(
        body, name="prenorm", grid=(nc // k,),
        in_specs=_x_specs(nx, k) + [pl.BlockSpec((CHUNK, D_MODEL), lambda i: (0, 0)), pl.BlockSpec((1, D_MODEL), lambda i: (0, 0))],
        out_specs=pl.BlockSpec((rows, D_MODEL), lambda i: (i, 0)),
        out_shape=jax.ShapeDtypeStruct((nc * CHUNK, D_MODEL), BF16),
        compiler_params=_params(("parallel",)),
    )(*([x2d] * k), h0, w)


CONV_COLS = 512
HALO = 8


def _conv_pre(ext, w, b):
    taps = [ext[HALO:, :]] + [pltpu.roll(ext, j, 0)[HALO:, :] for j in range(1, CONV_WIDTH)]
    acc = b + w[3:4, :] * taps[0]
    for j in range(1, CONV_WIDTH):
        acc = acc + w[3 - j:4 - j, :] * taps[j]
    return acc, taps


def _conv_fwd(proj, conv_w, conv_b):
    lp = proj.shape[0]
    t = _tile(lp, (704, 384, 128, 64))
    hb = t // HALO
    c0 = OFF_XBC // CONV_COLS

    def body(u_ref, halo_ref, w_ref, b_ref, o_ref):
        i = pl.program_id(0)
        halo = halo_ref[...] * (i > 0).astype(F32)
        pre, _ = _conv_pre(jnp.concatenate([halo, u_ref[...]], axis=0), w_ref[...], b_ref[...])
        o_ref[...] = pre * _sigmoid(pre)

    return pl.pallas_call(
        body, name="conv_fwd", grid=(lp // t, D_CONV // CONV_COLS),
        in_specs=[
            pl.BlockSpec((t, CONV_COLS), lambda i, j: (i, c0 + j)),
            pl.BlockSpec((HALO, CONV_COLS), lambda i, j: (jnp.maximum(i * hb - 1, 0), c0 + j)),
            pl.BlockSpec((CONV_WIDTH, CONV_COLS), lambda i, j: (0, j)),
            pl.BlockSpec((1, CONV_COLS), lambda i, j: (0, j)),
        ],
        out_specs=pl.BlockSpec((t, CONV_COLS), lambda i, j: (i, j)),
        out_shape=jax.ShapeDtypeStruct((lp, D_CONV), F32),
        compiler_params=_params(("parallel", "parallel")),
    )(proj, proj, conv_w, conv_b)


def _conv_bwd(dxbc, proj, conv_w, conv_b, dproj):
    lp = proj.shape[0]
    t = _tile(lp, (704, 384, 128, 64))
    hb = t // HALO
    nt = lp // t
    c0 = OFF_XBC // CONV_COLS

    def body(dx_ref, dxn_ref, u_ref, up_ref, un_ref, w_ref, b_ref, _, du_ref, dw_ref, db_ref):
        i = pl.program_id(1)
        w = w_ref[...]
        up = up_ref[...] * (i > 0).astype(F32)
        ext = jnp.concatenate([up, u_ref[...], un_ref[...]], axis=0)
        pre, taps = _conv_pre(ext, w, b_ref[...])
        dxn = dxn_ref[...] * (i < nt - 1).astype(F32)
        dxe = jnp.concatenate([dx_ref[...], dxn], axis=0)
        sg = _sigmoid(pre)
        dpre = dxe * sg * (1.0 + pre * (1.0 - sg))
        du = w[3:4, :] * dpre[:t, :]
        for j in range(1, CONV_WIDTH):
            du = du + w[3 - j:4 - j, :] * pltpu.roll(dpre, t + HALO - j, 0)[:t, :]
        du_ref[...] = du.astype(BF16)

        @pl.when(i == 0)
        def _():
            dw_ref[...] = jnp.zeros_like(dw_ref)
            db_ref[...] = jnp.zeros_like(db_ref)

        dp = dpre[:t, :]
        db_ref[...] += jnp.sum(dp, axis=0, keepdims=True)
        for j in range(CONV_WIDTH):
            dw_ref[3 - j:4 - j, :] += jnp.sum(dp * taps[j][:t, :], axis=0, keepdims=True)

    nxt = lambda i: jnp.minimum((i + 1) * hb, lp // HALO - 1)
    return pl.pallas_call(
        body, name="conv_bwd", grid=(D_CONV // CONV_COLS, nt),
        in_specs=[
            pl.BlockSpec((t, CONV_COLS), lambda j, i: (i, j)),
            pl.BlockSpec((HALO, CONV_COLS), lambda j, i: (nxt(i), j)),
            pl.BlockSpec((t, CONV_COLS), lambda j, i: (i, c0 + j)),
            pl.BlockSpec((HALO, CONV_COLS), lambda j, i: (jnp.maximum(i * hb - 1, 0), c0 + j)),
            pl.BlockSpec((HALO, CONV_COLS), lambda j, i: (nxt(i), c0 + j)),
            pl.BlockSpec((CONV_WIDTH, CONV_COLS), lambda j, i: (0, j)),
            pl.BlockSpec((1, CONV_COLS), lambda j, i: (0, j)),
            ANY,
        ],
        out_specs=[
            pl.BlockSpec((t, CONV_COLS), lambda j, i: (i, c0 + j)),
            pl.BlockSpec((CONV_WIDTH, CONV_COLS), lambda j, i: (0, j)),
            pl.BlockSpec((1, CONV_COLS), lambda j, i: (0, j)),
        ],
        out_shape=[
            jax.ShapeDtypeStruct((lp, NP), BF16),
            jax.ShapeDtypeStruct((CONV_WIDTH, D_CONV), F32),
            jax.ShapeDtypeStruct((1, D_CONV), F32),
        ],
        input_output_aliases={7: 0},
        compiler_params=_params(("parallel", "arbitrary")),
    )(dxbc, dxbc, proj, proj, proj, conv_w, conv_b, dproj)


def _swap_halves(t):
    w = t.shape[-1]
    lane = lax.broadcasted_iota(jnp.int32, t.shape, 1)
    return jnp.where((lane % HEAD_DIM) < HEAD_DIM // 2, pltpu.roll(t, w - HEAD_DIM // 2, 1),
                     pltpu.roll(t, HEAD_DIM // 2, 1))


def _act_fwd(proj, cos_t, sin_t, expand, dt_bias_rep):
    lp = proj.shape[0]
    t = _tile(lp, (384, 128, 64))

    def body(q_ref, k_ref, v_ref, dt_ref, cos_ref, sin_ref, ex_ref, bias_ref, qo_ref, ko_ref, vo_ref, dto_ref):
        i = pl.program_id(0)
        cos = cos_ref[...]
        sin = sin_ref[...]
        q = q_ref[...]
        qo_ref[...] = (q * jnp.tile(cos, (1, D_ATT // 128)) + _swap_halves(q) * jnp.tile(sin, (1, D_ATT // 128))).astype(BF16)
        k = k_ref[...]
        ko_ref[...] = (k * jnp.tile(cos, (1, D_KV // 128)) + _swap_halves(k) * jnp.tile(sin, (1, D_KV // 128))).astype(BF16)
        vo_ref[...] = v_ref[...].astype(BF16)
        raw = _dot(dt_ref[...], ex_ref[...], NN, HIGHEST) + bias_ref[...]
        sp = jnp.maximum(raw, 0.0) + jnp.log1p(jnp.exp(-jnp.abs(raw)))
        row = i * t + lax.broadcasted_iota(jnp.int32, sp.shape, 0)
        dto_ref[...] = jnp.where(row >= PAD_LEAD, sp, 0.0)

    return pl.pallas_call(
        body, name="act_fwd", grid=(lp // t,),
        in_specs=[
            pl.BlockSpec((t, D_ATT), lambda i: (i, OFF_Q // D_ATT)),
            pl.BlockSpec((t, D_KV), lambda i: (i, OFF_K // D_KV)),
            pl.BlockSpec((t, D_KV), lambda i: (i, OFF_V // D_KV)),
            pl.BlockSpec((t, 128), lambda i: (i, OFF_DT // 128)),
            pl.BlockSpec((t, 128), lambda i: (i, 0)),
            pl.BlockSpec((t, 128), lambda i: (i, 0)),
            pl.BlockSpec((128, D_SSD), lambda i: (0, 0)),
            pl.BlockSpec((1, D_SSD), lambda i: (0, 0)),
        ],
        out_specs=[
            pl.BlockSpec((t, D_ATT), lambda i: (i, 0)),
            pl.BlockSpec((t, D_KV), lambda i: (i, 0)),
            pl.BlockSpec((t, D_KV), lambda i: (i, 0)),
            pl.BlockSpec((t, D_SSD), lambda i: (i, 0)),
        ],
        out_shape=[
            jax.ShapeDtypeStruct((lp, D_ATT), BF16),
            jax.ShapeDtypeStruct((lp, D_KV), BF16),
            jax.ShapeDtypeStruct((lp, D_KV), BF16),
            jax.ShapeDtypeStruct((lp, D_SSD), F32),
        ],
        compiler_params=_params(("parallel",)),
    )(proj, proj, proj, proj, cos_t, sin_t, expand, dt_bias_rep)


def _act_bwd(dqr, dkr, dv, dg, ddt_part, cos_t, sin_t, reduce_t, dproj):
    lp = dqr.shape[0]
    t = _tile(lp, (384, 128, 64))

    def body(dq_ref, dk_ref, dv_ref, dg_ref, ddt_ref, cos_ref, sin_ref, red_ref, _, o_ref, db_ref):
        i = pl.program_id(0)
        cos = cos_ref[...]
        sin = sin_ref[...]
        dq = dq_ref[...]
        dq = dq * jnp.tile(cos, (1, D_ATT // 128)) + _swap_halves(dq * jnp.tile(sin, (1, D_ATT // 128)))
        dk = dk_ref[...]
        dk = dk * jnp.tile(cos, (1, D_KV // 128)) + _swap_halves(dk * jnp.tile(sin, (1, D_KV // 128)))
        ddt = _dot(ddt_ref[...], red_ref[...], NN, HIGHEST)
        o_ref[...] = jnp.concatenate(
            [dq.astype(BF16), dg_ref[...].astype(BF16), dk.astype(BF16), dv_ref[...].astype(BF16), ddt.astype(BF16),
             jnp.zeros((t, NP - OFF_DT - 128), BF16)], axis=1)

        @pl.when(i == 0)
        def _():
            db_ref[...] = jnp.zeros_like(db_ref)

        db_ref[...] += jnp.sum(ddt, axis=0, keepdims=True)

    return pl.pallas_call(
        body, name="act_bwd", grid=(lp // t,),
        in_specs=[
            pl.BlockSpec((t, D_ATT), lambda i: (i, 0)),
            pl.BlockSpec((t, D_KV), lambda i: (i, 0)),
            pl.BlockSpec((t, D_KV), lambda i: (i, 0)),
            pl.BlockSpec((t, D_ATT), lambda i: (i, 0)),
            pl.BlockSpec((t, D_SSD), lambda i: (i, 0)),
            pl.BlockSpec((t, 128), lambda i: (i, 0)),
            pl.BlockSpec((t, 128), lambda i: (i, 0)),
            pl.BlockSpec((D_SSD, 128), lambda i: (0, 0)),
            ANY,
        ],
        out_specs=[pl.BlockSpec((t, TAIL_W), lambda i: (i, OFF_Q // TAIL_W)), pl.BlockSpec((1, 128), lambda i: (0, 0))],
        out_shape=[jax.ShapeDtypeStruct((lp, NP), BF16), jax.ShapeDtypeStruct((1, 128), F32)],
        input_output_aliases={8: 0},
        compiler_params=_params(("arbitrary",)),
    )(dqr, dkr, dv, dg, ddt_part, cos_t, sin_t, reduce_t, dproj)


def _cs_row(cs):
    row = lax.broadcasted_iota(jnp.int32, cs.shape, 0)
    lane = lax.broadcasted_iota(jnp.int32, cs.shape, 1)
    return jnp.sum(jnp.where(row == lane % HEAD_DIM, cs, 0.0), axis=0, keepdims=True)


def _decay_matrix(cs, cs_row, r, mask):
    seg = cs[:, r * HEAD_DIM:r * HEAD_DIM + 1] - cs_row[:, r * HEAD_DIM:(r + 1) * HEAD_DIM]
    return jnp.exp(jnp.where(mask, seg, -jnp.inf))


def _causal_mask():
    row = lax.broadcasted_iota(jnp.int32, (CHUNK, CHUNK), 0)
    col = lax.broadcasted_iota(jnp.int32, (CHUNK, CHUNK), 1)
    return row >= col


def _ssd_fwd(xbc, dt_rep, proj, a_rep, dsk_rep, wn, tri):
    lp = xbc.shape[0]
    nc = lp // CHUNK
    gw = SSD_GPS * GROUP_W

    def body(xbc_ref, dt_ref, z_ref, a_ref, dsk_ref, wn_ref, tri_ref, yn_ref, ytot_ref, hprev_ref, h_scr):
        @pl.when(pl.program_id(1) == 0)
        def _():
            h_scr[...] = jnp.zeros_like(h_scr)

        causal = _causal_mask()
        G = range(SSD_GPS)
        colsl = [slice(gi * GROUP_W, (gi + 1) * GROUP_W) for gi in G]
        xbc = [xbc_ref[:, gi * XBC_BLK:(gi + 1) * XBC_BLK] for gi in G]
        dt = [dt_ref[:, colsl[gi]] for gi in G]
        xs = [xbc[gi][:, :GROUP_W] for gi in G]
        b = [xbc[gi][:, GROUP_W:GROUP_W + D_STATE].astype(BF16) for gi in G]
        c = [xbc[gi][:, GROUP_W + D_STATE:].astype(BF16) for gi in G]
        hprev = [h_scr[gi] for gi in G]
        cs = [_dot(tri_ref[...], dt[gi] * a_ref[:, colsl[gi]], NN, HIGHEST) for gi in G]
        cb = [_dot(c[gi], b[gi], NT) for gi in G]
        yoff = [_dot(c[gi], hprev[gi].astype(BF16)) for gi in G]
        cs_t = [_cs_row(cs[gi]) for gi in G]
        xdt = [xs[gi] * dt[gi] for gi in G]
        cs_last = [cs[gi][CHUNK - 1:CHUNK, :] for gi in G]
        st = [_dot(b[gi], (xdt[gi] * jnp.exp(cs_last[gi] - cs[gi])).astype(BF16), TN) for gi in G]
        m = [[(cb[gi] * _decay_matrix(cs[gi], cs_t[gi], r, causal)).astype(BF16) for r in range(HEADS_PER_GROUP)] for gi in G]
        ydiag = [[_dot(m[gi][r], xdt[gi][:, r * HEAD_DIM:(r + 1) * HEAD_DIM].astype(BF16)) for r in range(HEADS_PER_GROUP)]
                 for gi in G]
        for gi in G:
            cols = colsl[gi]
            ytot = jnp.concatenate(ydiag[gi], axis=1) + yoff[gi] * jnp.exp(cs[gi]) + dsk_ref[:, cols] * xs[gi]
            z = z_ref[:, cols]
            gz = ytot * (z * _sigmoid(z))
            rstd = lax.rsqrt(jnp.mean(gz * gz, axis=-1, keepdims=True) + EPS)
            hprev_ref[gi] = hprev[gi]
            h_scr[gi] = hprev[gi] * jnp.exp(cs_last[gi]) + st[gi]
            ytot_ref[:, cols] = ytot
            yn_ref[:, cols] = (gz * rstd * wn_ref[:, cols]).astype(BF16)

    vec = pl.BlockSpec((1, gw), lambda g, c: (0, g))
    blk = pl.BlockSpec((CHUNK, gw), lambda g, c: (c, g))
    return pl.pallas_call(
        body, name="ssd_fwd", grid=(N_GROUPS // SSD_GPS, nc),
        in_specs=[
            pl.BlockSpec((CHUNK, SSD_GPS * XBC_BLK), lambda g, c: (c, g)),
            blk, blk, vec, vec, vec,
            pl.BlockSpec((CHUNK, CHUNK), lambda g, c: (0, 0)),
        ],
        out_specs=[blk, blk, pl.BlockSpec((None, SSD_GPS, D_STATE, GROUP_W), lambda g, c: (c, g, 0, 0))],
        out_shape=[
            jax.ShapeDtypeStruct((lp, D_MIX), BF16),
            jax.ShapeDtypeStruct((lp, D_SSD), F32),
            jax.ShapeDtypeStruct((nc, N_GROUPS, D_STATE, GROUP_W), F32),
        ],
        scratch_shapes=[pltpu.VMEM((SSD_GPS, D_STATE, GROUP_W), F32)],
        compiler_params=_params(("parallel", "arbitrary")),
    )(xbc, dt_rep, proj, a_rep, dsk_rep, wn, tri)


def _ssd_bwd(dmix, ytot, proj, xbc, dt_rep, hprev, a_rep, dsk_rep, wn, tri, comm):
    lp = xbc.shape[0]
    nc = lp // CHUNK
    gps = SSD_GPS_BWD
    gw = gps * GROUP_W
    ncm = comm.n
    n_in, n_out = 10, 6
    grid = (N_GROUPS // gps, nc)

    def all_groups(refs, causal):
        (dyn_ref, ytot_ref, z_ref, xbc_ref, dt_ref, hprev_ref, a_ref, dsk_ref, wn_ref, tri_ref,
         dz_ref, dxbc_ref, ddt_ref, dd_ref, da_ref, dwn_ref, dh_scr) = refs
        G = range(gps)
        H = range(HEADS_PER_GROUP)
        cl = [slice(gi * GROUP_W, (gi + 1) * GROUP_W) for gi in G]
        hl = [slice(r * HEAD_DIM, (r + 1) * HEAD_DIM) for r in H]
        tri = tri_ref[...]
        xbc = [xbc_ref[:, gi * XBC_BLK:(gi + 1) * XBC_BLK] for gi in G]
        dt = [dt_ref[:, cl[gi]] for gi in G]
        a = [a_ref[:, cl[gi]] for gi in G]
        xs = [xbc[gi][:, :GROUP_W] for gi in G]
        bbf = [xbc[gi][:, GROUP_W:GROUP_W + D_STATE].astype(BF16) for gi in G]
        cbf = [xbc[gi][:, GROUP_W + D_STATE:].astype(BF16) for gi in G]
        hprev = [hprev_ref[gi] for gi in G]
        hbf = [hprev[gi].astype(BF16) for gi in G]
        dhn = [dh_scr[gi] for gi in G]
        dhnb = [dhn[gi].astype(BF16) for gi in G]
        cs = [_dot(tri, dt[gi] * a[gi], NN, HIGHEST) for gi in G]
        cb = [_dot(cbf[gi], bbf[gi], NT) for gi in G]
        g = [_dot(cbf[gi], hbf[gi]) for gi in G]
        dxw = [_dot(bbf[gi], dhnb[gi]) for gi in G]
        cs_t = [_cs_row(cs[gi]) for gi in G]
        dy = []
        for gi in G:
            ytot = ytot_ref[:, cl[gi]]
            z = z_ref[:, cl[gi]]
            dyn = dyn_ref[:, cl[gi]]
            sz = _sigmoid(z)
            silu_z = z * sz
            gz = ytot * silu_z
            rstd = lax.rsqrt(jnp.mean(gz * gz, axis=-1, keepdims=True) + EPS)
            xhat = gz * rstd
            dwn_ref[:, cl[gi]] += jnp.sum(dyn * xhat, axis=0, keepdims=True)
            dxhat = dyn * wn_ref[:, cl[gi]]
            dgz = rstd * (dxhat - xhat * jnp.mean(dxhat * xhat, axis=-1, keepdims=True))
            dy.append(dgz * silu_z)
            dz_ref[:, cl[gi]] = (dgz * ytot * (sz * (1.0 + z * (1.0 - sz)))).astype(BF16)
            dd_ref[:, cl[gi]] += jnp.sum(dy[gi] * xs[gi], axis=0, keepdims=True)
        xdt = [xs[gi] * dt[gi] for gi in G]
        e = [jnp.exp(cs[gi]) for gi in G]
        cs_last = [cs[gi][CHUNK - 1:CHUNK, :] for gi in G]
        dte = [jnp.exp(cs_last[gi] - cs[gi]) for gi in G]
        cd = [jnp.exp(cs_last[gi]) for gi in G]
        dgb = [(dy[gi] * e[gi]).astype(BF16) for gi in G]
        dyb = [dy[gi].astype(BF16) for gi in G]
        xdtb = [xdt[gi].astype(BF16) for gi in G]
        dc = [_dot(dgb[gi], hbf[gi], NT) for gi in G]
        dhprev = [_dot(cbf[gi], dgb[gi], TN) for gi in G]
        db = [_dot((xdt[gi] * dte[gi]).astype(BF16), dhnb[gi], NT) for gi in G]
        dm = [[_dot(dyb[gi][:, hl[r]], xdtb[gi][:, hl[r]], NT) for r in H] for gi in G]
        mb, dseg, dcbb = [], [], []
        for gi in G:
            mb.append([])
            dseg.append([])
            dcb = None
            for r in H:
                lm = _decay_matrix(cs[gi], cs_t[gi], r, causal)
                m = cb[gi] * lm
                mb[gi].append(m.astype(BF16))
                dseg[gi].append(dm[gi][r] * m)
                dcb = dm[gi][r] * lm if r == 0 else dcb + dm[gi][r] * lm
            dcbb.append(dcb.astype(BF16))
        dxdt_diag = [[_dot(mb[gi][r], dyb[gi][:, hl[r]], TN) for r in H] for gi in G]
        ones = jnp.ones((CHUNK, HEAD_DIM), F32)
        colsum = [[_dot(dseg[gi][r], ones, TN, HIGHEST) for r in H] for gi in G]
        dc2 = [_dot(dcbb[gi], bbf[gi]) for gi in G]
        db2 = [_dot(dcbb[gi], cbf[gi], TN) for gi in G]
        dcs = []
        for gi in G:
            t_dte = dxw[gi] * xdt[gi] * dte[gi]
            dcs_last = (jnp.sum(dhn[gi] * hprev[gi], axis=0, keepdims=True) * cd[gi]
                        + jnp.sum(t_dte, axis=0, keepdims=True))
            diag = jnp.concatenate(
                [(jnp.sum(dseg[gi][r], axis=1, keepdims=True) - colsum[gi][r]) * (1.0 / HEAD_DIM) for r in H], axis=1)
            d = dy[gi] * g[gi] * e[gi] - t_dte + diag
            row = lax.broadcasted_iota(jnp.int32, d.shape, 0)
            dcs.append(d + jnp.where(row == CHUNK - 1, dcs_last, 0.0))
        dda = [_dot(tri, dcs[gi], TN, HIGHEST) for gi in G]
        for gi in G:
            dxdt = dxw[gi] * dte[gi] + jnp.concatenate(dxdt_diag[gi], axis=1)
            da_ref[:, cl[gi]] += jnp.sum(dda[gi] * dt[gi], axis=0, keepdims=True)
            ddt = dda[gi] * a[gi] + dxdt * xs[gi]
            dxs = dsk_ref[:, cl[gi]] * dy[gi] + dxdt * dt[gi]
            ddt_ref[:, cl[gi]] = ddt * (1.0 - jnp.exp(-dt[gi]))
            dxbc_ref[:, gi * XBC_BLK:(gi + 1) * XBC_BLK] = jnp.concatenate(
                [dxs, db[gi] + db2[gi], dc[gi] + dc2[gi]], axis=1)
            dh_scr[gi] = dhprev[gi] + dhn[gi] * cd[gi]

    def body(*refs):
        ins = refs[:n_in]
        cin = refs[n_in:n_in + ncm]
        outs = refs[n_in + ncm:n_in + ncm + n_out]
        cout = refs[n_in + ncm + n_out:n_in + 2 * ncm + n_out]
        dh_scr = refs[n_in + 2 * ncm + n_out]
        sems = refs[n_in + 2 * ncm + n_out + 1:]
        g, c = pl.program_id(0), pl.program_id(1)

        @pl.when((g == 0) & (c == 0))
        def _():
            comm.start(cin, cout, sems)

        @pl.when(c == 0)
        def _():
            dh_scr[...] = jnp.zeros_like(dh_scr)
            for ref in outs[3:]:
                ref[...] = jnp.zeros_like(ref)

        all_groups(ins + outs + (dh_scr,), _causal_mask())

        @pl.when((g == grid[0] - 1) & (c == nc - 1))
        def _():
            comm.wait(cin, cout, sems)

    rev = lambda c: nc - 1 - c
    vec = pl.BlockSpec((1, gw), lambda g, c: (0, g))
    blk = pl.BlockSpec((CHUNK, gw), lambda g, c: (rev(c), g))
    xblk = pl.BlockSpec((CHUNK, gps * XBC_BLK), lambda g, c: (rev(c), g))
    res = pl.pallas_call(
        body, name="ssd_bwd", grid=grid,
        in_specs=[blk, blk, blk, xblk, blk,
                  pl.BlockSpec((None, gps, D_STATE, GROUP_W), lambda g, c: (rev(c), g, 0, 0)),
                  vec, vec, vec,
                  pl.BlockSpec((CHUNK, CHUNK), lambda g, c: (0, 0))] + comm.in_specs,
        out_specs=[blk, xblk, blk, vec, vec, vec] + comm.out_specs,
        out_shape=[
            jax.ShapeDtypeStruct((lp, NP), BF16),
            jax.ShapeDtypeStruct((lp, D_CONV), F32),
            jax.ShapeDtypeStruct((lp, D_SSD), F32),
            jax.ShapeDtypeStruct((1, D_SSD), F32),
            jax.ShapeDtypeStruct((1, D_SSD), F32),
            jax.ShapeDtypeStruct((1, D_SSD), F32),
        ] + comm.out_shape,
        scratch_shapes=[pltpu.VMEM((gps, D_STATE, GROUP_W), F32)] + comm.scratch,
        compiler_params=_params(("arbitrary", "arbitrary"), side_effects=True),
    )(dmix, ytot, proj, xbc, dt_rep, hprev, a_rep, dsk_rep, wn, tri, *comm.operands)
    return res


def _stack_heads(t, h):
    return jnp.concatenate([t[:, (REP * h + r) * HEAD_DIM:(REP * h + r + 1) * HEAD_DIM] for r in range(REP)], axis=0)


def _band(t2, t1, t0, h):
    sl = slice(h * HEAD_DIM, (h + 1) * HEAD_DIM)
    return jnp.concatenate([t2[:, sl], t1[:, sl], t0[:, sl]], axis=0)


def _attn_probs(s, sink, qc):
    s = s * (HEAD_DIM ** -0.5)
    key_abs = (qc - WINDOW_CHUNKS) * CHUNK + lax.broadcasted_iota(jnp.int32, s.shape, 1)
    s = jnp.where(key_abs >= PAD_LEAD, s, -jnp.inf)
    m = jnp.maximum(jnp.max(s, axis=-1, keepdims=True), sink)
    p = jnp.exp(s - m)
    ps = jnp.exp(sink - m)
    denom = jnp.sum(p, axis=-1, keepdims=True) + ps
    return p / denom, ps / denom


ATT_QC_FWD = 6
ATT_QC_BWD = 2


def _kv_specs(width, newest_chunk_of, kc):
    return [pl.BlockSpec((CHUNK, width), functools.partial(lambda j, p: (jnp.maximum(newest_chunk_of(p) - j, 0), 0), j))
            for j in range(kc - 1, -1, -1)]


def _attn_fwd(qr, kr, vb, proj, sink_stack, mix):
    lp = qr.shape[0]
    nc = lp // CHUNK
    qn = _tile(nc, (ATT_QC_FWD, 2, 1))
    kn = WINDOW_CHUNKS + qn
    qrows = qn * CHUNK

    def body(q_ref, *rest):
        k_refs, v_refs = rest[:kn], rest[kn:2 * kn]
        g_ref, sink_ref, _, att_ref, mix_ref = rest[2 * kn:]
        p = pl.program_id(0)
        q = q_ref[...]
        ks = [r[...] for r in k_refs]
        vs = [r[...] for r in v_refs]
        units = [(u, h) for u in range(qn) for h in range(KV_HEADS)]
        s = [_dot(_stack_heads(q[u * CHUNK:(u + 1) * CHUNK, :], h), _band(*ks[u:u + 3], h), NT) for u, h in units]
        vbh = [_band(*vs[u:u + 3], h) for u, h in units]
        pn = [_attn_probs(s[i], sink_ref[h], qn * p + u)[0].astype(BF16) for i, (u, h) in enumerate(units)]
        o = [_dot(pn[i], vbh[i]) for i in range(len(units))]
        att = jnp.concatenate(
            [jnp.concatenate([o[u * KV_HEADS + h][r * CHUNK:(r + 1) * CHUNK, :] for h in range(KV_HEADS) for r in range(REP)],
                             axis=1) for u in range(qn)], axis=0)
        att_ref[...] = att
        g = g_ref[...]
        mix_ref[...] = (att * (g * _sigmoid(g))).astype(BF16)

    newest = lambda p: qn * p + qn - 1
    return pl.pallas_call(
        body, name="attn_fwd", grid=(nc // qn,),
        in_specs=[pl.BlockSpec((qrows, D_ATT), lambda p: (p, 0))] + _kv_specs(D_KV, newest, kn) + _kv_specs(D_KV, newest, kn) + [
            pl.BlockSpec((qrows, D_ATT), lambda p: (p, OFF_G // D_ATT)),
            pl.BlockSpec((KV_HEADS, REP * CHUNK, 1), lambda p: (0, 0, 0)),
            ANY,
        ],
        out_specs=[pl.BlockSpec((qrows, D_ATT), lambda p: (p, 0)),
                   pl.BlockSpec((qrows, D_ATT), lambda p: (p, D_SSD // D_ATT))],
        out_shape=[jax.ShapeDtypeStruct((lp, D_ATT), F32), jax.ShapeDtypeStruct((lp, D_MIX), BF16)],
        input_output_aliases={2 * kn + 3: 1},
        compiler_params=_params(("parallel",)),
    )(qr, *([kr] * kn), *([vb] * kn), proj, sink_stack, mix)


def _attn_bwd(qr, kr, vb, att, proj, dmix, sink_stack):
    lp = qr.shape[0]
    nc = lp // CHUNK
    qn = ATT_QC_BWD
    kn = WINDOW_CHUNKS + qn
    assert nc % qn == 0 and WINDOW_CHUNKS % qn == 0
    steps = nc // qn
    qrows = qn * CHUNK
    wrows = kn * CHUNK

    def body(q_ref, *rest):
        k_refs, v_refs = rest[:kn], rest[kn:2 * kn]
        (att_ref, g_ref, do_ref, sink_ref, dq_ref, dk_ref, dv_ref, dg_ref, dsink_ref, dk_acc, dv_acc) = rest[2 * kn:]
        step = pl.program_id(0)

        @pl.when(step == 0)
        def _():
            dk_acc[...] = jnp.zeros_like(dk_acc)
            dv_acc[...] = jnp.zeros_like(dv_acc)
            dsink_ref[...] = jnp.zeros_like(dsink_ref)

        q = q_ref[...]
        ks = [r[...] for r in k_refs]
        vs = [r[...] for r in v_refs]
        att = att_ref[...]
        g = g_ref[...]
        dog = do_ref[...]
        sg = _sigmoid(g)
        dg_ref[...] = dog * att * (sg * (1.0 + g * (1.0 - sg)))
        do = dog * (g * sg)
        units = [(u, h) for u in range(qn) for h in range(KV_HEADS)]
        n = range(len(units))
        rows = [slice(u * CHUNK, (u + 1) * CHUNK) for u in range(qn)]
        qs = [_stack_heads(q[rows[u], :], h) for u, h in units]
        kb = [_band(*ks[u:u + 3], h) for u, h in units]
        vbh = [_band(*vs[u:u + 3], h) for u, h in units]
        dos = [_stack_heads(do[rows[u], :], h) for u, h in units]
        dosb = [dos[i].astype(BF16) for i in n]
        s = [_dot(qs[i], kb[i], NT) for i in n]
        dp = [_dot(dosb[i], vbh[i], NT) for i in n]
        ds, pnb, dsink = [], [], []
        for i, (u, h) in enumerate(units):
            pn, psink = _attn_probs(s[i], sink_ref[h], qn * jnp.minimum(step, steps - 1) + u)
            delta = jnp.sum(dos[i] * _stack_heads(att[rows[u], :], h), axis=-1, keepdims=True)
            ds.append((pn * (dp[i] - delta)).astype(BF16))
            pnb.append(pn.astype(BF16))
            dsink.append(-(psink * delta))
        dqs = [_dot(ds[i], kb[i]) for i in n]
        dks = [_dot(ds[i], qs[i], TN) for i in n]
        dvs = [_dot(pnb[i], dosb[i], TN) for i in n]
        dq_ref[...] = jnp.concatenate(
            [jnp.concatenate([dqs[u * KV_HEADS + h][r * CHUNK:(r + 1) * CHUNK, :]
                              for h in range(KV_HEADS) for r in range(REP)], axis=1) for u in range(qn)],
            axis=0) * (HEAD_DIM ** -0.5)
        @pl.when(step < steps)
        def _():
            for i, (u, h) in enumerate(units):
                dsink_ref[h] += dsink[i]
            for u in range(qn):
                band = slice(u * CHUNK, u * CHUNK + BAND)
                dk_acc[band, :] += jnp.concatenate(dks[u * KV_HEADS:(u + 1) * KV_HEADS], axis=1) * (HEAD_DIM ** -0.5)
                dv_acc[band, :] += jnp.concatenate(dvs[u * KV_HEADS:(u + 1) * KV_HEADS], axis=1)

        dk_ref[...] = dk_acc[0:qrows, :]
        dv_ref[...] = dv_acc[0:qrows, :]
        for acc in (dk_acc, dv_acc):
            rest_rows = acc[qrows:wrows, :]
            acc[0:wrows - qrows, :] = rest_rows
            acc[wrows - qrows:wrows, :] = jnp.zeros((qrows, D_KV), F32)

    qp = lambda p: jnp.minimum(p, steps - 1)
    newest = lambda p: qn * qp(p) + qn - 1
    qblk = pl.BlockSpec((qrows, D_ATT), lambda p: (qp(p), 0))
    oldest = pl.BlockSpec((qrows, D_KV), lambda p: (jnp.maximum(p - 1, 0), 0))
    return pl.pallas_call(
        body, name="attn_bwd", grid=(steps + 1,),
        in_specs=[qblk] + _kv_specs(D_KV, newest, kn) + _kv_specs(D_KV, newest, kn) + [
            qblk,
            pl.BlockSpec((qrows, D_ATT), lambda p: (qp(p), OFF_G // D_ATT)),
            pl.BlockSpec((qrows, D_ATT), lambda p: (qp(p), D_SSD // D_ATT)),
            pl.BlockSpec((KV_HEADS, REP * CHUNK, 1), lambda p: (0, 0, 0)),
        ],
        out_specs=[qblk, oldest, oldest, qblk, pl.BlockSpec((KV_HEADS, REP * CHUNK, 1), lambda p: (0, 0, 0))],
        out_shape=[
            jax.ShapeDtypeStruct((lp, D_ATT), F32),
            jax.ShapeDtypeStruct((lp, D_KV), F32),
            jax.ShapeDtypeStruct((lp, D_KV), F32),
            jax.ShapeDtypeStruct((lp, D_ATT), F32),
            jax.ShapeDtypeStruct((KV_HEADS, REP * CHUNK, 1), F32),
        ],
        scratch_shapes=[pltpu.VMEM((wrows, D_KV), F32), pltpu.VMEM((wrows, D_KV), F32)],
        compiler_params=_params(("arbitrary",)),
    )(qr, *([kr] * kn), *([vb] * kn), att, proj, dmix, sink_stack)


def _post_loss(out, x2d, target, w):
    lp = out.shape[0]
    nc = lp // CHUNK
    nx = x2d.shape[0] // CHUNK

    k = _tile(nc, (ROW_K, 3, 2, 1))

    def body(o_ref, *rest):
        x_refs, t_refs = rest[:k], rest[k:2 * k]
        w_ref, do_ref, dy_ref, gw_ref, loss_ref = rest[2 * k:]
        i = pl.program_id(0)

        @pl.when(i == 0)
        def _():
            gw_ref[...] = jnp.zeros_like(gw_ref)
            loss_ref[...] = jnp.zeros_like(loss_ref)

        o = o_ref[...]
        w = w_ref[...]
        rstd = lax.rsqrt(jnp.mean(o * o, axis=-1, keepdims=True) + EPS)
        xhat = o * rstd
        x = jnp.concatenate([r[...] for r in x_refs], axis=0)
        t = jnp.concatenate([r[...] for r in t_refs], axis=0)
        chunk = _chunk_of_row(i, k)
        err = (x + xhat * w - t) * ((chunk > 0) & (chunk <= nx)).astype(F32)
        loss_ref[...] += 0.5 * jnp.sum(jnp.mean(err * err, axis=-1, keepdims=True), axis=0, keepdims=True)
        dy = err * (1.0 / D_MODEL)
        dy_ref[...] = dy
        gw_ref[...] += jnp.sum(dy * xhat, axis=0, keepdims=True)
        dxhat = dy * w
        do_ref[...] = (rstd * (dxhat - xhat * jnp.mean(dxhat * xhat, axis=-1, keepdims=True))).astype(BF16)

    row = pl.BlockSpec((k * CHUNK, D_MODEL), lambda i: (i, 0))
    return pl.pallas_call(
        body, name="post_loss", grid=(nc // k,),
        in_specs=[row] + _x_specs(nx, k) + _x_specs(nx, k) + [pl.BlockSpec((1, D_MODEL), lambda i: (0, 0))],
        out_specs=[row, row, pl.BlockSpec((1, D_MODEL), lambda i: (0, 0)), pl.BlockSpec((1, 128), lambda i: (0, 0))],
        out_shape=[
            jax.ShapeDtypeStruct((lp, D_MODEL), BF16),
            jax.ShapeDtypeStruct((lp, D_MODEL), F32),
            jax.ShapeDtypeStruct((1, D_MODEL), F32),
            jax.ShapeDtypeStruct((1, 128), F32),
        ],
        compiler_params=_params(("arbitrary",)),
    )(out, *([x2d] * k), *([target] * k), w)


def _prenorm_bwd(dhn, x2d, h0, dy, w):
    nx = x2d.shape[0] // CHUNK
    k = _tile(nx, (X_K, 4, 2, 1))
    rows = k * CHUNK

    def backward(h, dhn, w):
        rstd = lax.rsqrt(jnp.mean(h * h, axis=-1, keepdims=True) + EPS)
        xhat = h * rstd
        dxhat = dhn * w
        dh = rstd * (dxhat - xhat * jnp.mean(dxhat * xhat, axis=-1, keepdims=True))
        return dh, jnp.sum(dhn * xhat, axis=0, keepdims=True)

    def body(*refs):
        dhn_refs, dy_refs = refs[:k], refs[k:2 * k]
        x_ref, w_ref, gx_ref, gw_ref = refs[2 * k:]

        @pl.when(pl.program_id(0) == 0)
        def _():
            gw_ref[...] = jnp.zeros_like(gw_ref)

        dh, gw = backward(x_ref[...], jnp.concatenate([r[...] for r in dhn_refs], axis=0), w_ref[...])
        gx_ref[...] = dh + jnp.concatenate([r[...] for r in dy_refs], axis=0)
        gw_ref[...] += gw

    def body_meta(dhn_ref, h0_ref, w_ref, d0_ref, gw_ref):
        d0_ref[...], gw_ref[...] = backward(h0_ref[...], dhn_ref[...], w_ref[...])

    chunk_specs = [pl.BlockSpec((CHUNK, D_MODEL), functools.partial(lambda u, i: (k * i + 1 + u, 0), u)) for u in range(k)]
    first = pl.BlockSpec((CHUNK, D_MODEL), lambda i: (0, 0))
    vec = pl.BlockSpec((1, D_MODEL), lambda i: (0, 0))
    wide = pl.BlockSpec((rows, D_MODEL), lambda i: (i, 0))
    gx, gw_x = pl.pallas_call(
        body, name="prenorm_bwd", grid=(nx // k,),
        in_specs=chunk_specs + chunk_specs + [wide, vec],
        out_specs=[wide, vec],
        out_shape=[jax.ShapeDtypeStruct((nx * CHUNK, D_MODEL), F32), jax.ShapeDtypeStruct((1, D_MODEL), F32)],
        compiler_params=_params(("arbitrary",)),
    )(*([dhn] * k), *([dy] * k), x2d, w)
    d0, gw_0 = pl.pallas_call(
        body_meta, name="prenorm_bwd_meta", grid=(1,),
        in_specs=[first, first, vec], out_specs=[first, vec],
        out_shape=[jax.ShapeDtypeStruct((CHUNK, D_MODEL), F32), jax.ShapeDtypeStruct((1, D_MODEL), F32)],
        compiler_params=_params(("arbitrary",)),
    )(dhn, h0, w)
    return gx, d0, gw_x + gw_0


def _adamw(slabs, w, m, v, name):
    rows, cols = w.shape
    tr = _tile(rows, (256, 128, 64, 16, 8))
    c1 = 1.0 - ADAM_B1 ** ADAM_STEP
    c2 = 1.0 - ADAM_B2 ** ADAM_STEP

    def body(s_ref, w_ref, m_ref, v_ref, g_ref, d_ref, mo_ref, vo_ref):
        g = s_ref[0].astype(F32)
        for k in range(1, slabs.shape[0]):
            g = g + s_ref[k].astype(F32)
        w = w_ref[...]
        m = ADAM_B1 * m_ref[...] + (1.0 - ADAM_B1) * g
        v = ADAM_B2 * v_ref[...] + (1.0 - ADAM_B2) * (g * g)
        g_ref[...] = g
        mo_ref[...] = m
        vo_ref[...] = v
        d_ref[...] = -ADAM_LR * ((m / c1) / (jnp.sqrt(v / c2) + ADAM_EPS) + ADAM_WD * w)

    blk = pl.BlockSpec((tr, cols), lambda i: (i, 0))
    shape = jax.ShapeDtypeStruct((rows, cols), F32)
    return pl.pallas_call(
        body, name=name, grid=(rows // tr,),
        in_specs=[pl.BlockSpec((slabs.shape[0], tr, cols), lambda i: (0, i, 0)), blk, blk, blk],
        out_specs=[blk, blk, blk, blk],
        out_shape=[shape, shape, shape, shape],
        compiler_params=_params(("parallel",)),
    )(slabs, w, m, v)


def _perm_xbc(a):
    lead = a.shape[:-1]
    xs = a[..., :D_SSD].reshape(lead + (N_GROUPS, GROUP_W))
    b = a[..., D_SSD:D_SSD + N_GROUPS * D_STATE].reshape(lead + (N_GROUPS, D_STATE))
    c = a[..., D_SSD + N_GROUPS * D_STATE:].reshape(lead + (N_GROUPS, D_STATE))
    return jnp.concatenate([xs, b, c], axis=-1).reshape(lead + (D_CONV,))


def _unperm_xbc(a):
    lead = a.shape[:-1]
    t = a.reshape(lead + (N_GROUPS, XBC_BLK))
    xs = t[..., :GROUP_W].reshape(lead + (D_SSD,))
    b = t[..., GROUP_W:GROUP_W + D_STATE].reshape(lead + (N_GROUPS * D_STATE,))
    c = t[..., GROUP_W + D_STATE:].reshape(lead + (N_GROUPS * D_STATE,))
    return jnp.concatenate([xs, b, c], axis=-1)


R_Z, R_XBC, R_DT, R_Q, R_K, R_V, R_G = 0, 2048, 6144, 6176, 7200, 7456, 7712


def _internal_of_reference():
    ref = np.arange(D_IN_PROJ)
    out = np.empty(D_IN_PROJ, np.int64)
    out[R_Z:R_XBC] = OFF_Z + ref[:D_SSD]
    xs = np.arange(D_SSD)
    out[R_XBC:R_XBC + D_SSD] = OFF_XBC + (xs // GROUP_W) * XBC_BLK + xs % GROUP_W
    bc = np.arange(N_GROUPS * D_STATE)
    out[R_XBC + D_SSD:R_XBC + D_SSD + N_GROUPS * D_STATE] = OFF_XBC + (bc // D_STATE) * XBC_BLK + GROUP_W + bc % D_STATE
    out[R_XBC + D_SSD + N_GROUPS * D_STATE:R_DT] = OFF_XBC + (bc // D_STATE) * XBC_BLK + GROUP_W + D_STATE + bc % D_STATE
    out[R_DT:R_Q] = OFF_DT + np.arange(SSD_HEADS)
    out[R_Q:R_K] = OFF_Q + np.arange(D_ATT)
    out[R_K:R_V] = OFF_K + np.arange(D_KV)
    out[R_V:R_G] = OFF_V + np.arange(D_KV)
    out[R_G:] = OFF_G + np.arange(D_ATT)
    return out


def _runs(src, dst_break):
    runs, lo = [], 0
    for i in range(1, len(src) + 1):
        if i == len(src) or src[i] != src[i - 1] + 1 or dst_break[i] != dst_break[i - 1]:
            runs.append((lo, i))
            lo = i
    return runs


RELAYOUT_ROWS = 256


def _lane_window(ref, lead, c0, n):
    a0 = c0 // 128 * 128
    a1 = min(-(-(c0 + n) // 128) * 128, ref.shape[-1])
    return ref[lead + (slice(None), slice(a0, a1))][:, c0 - a0:c0 - a0 + n]


def _w_in_internal(w_gathered):
    int_of_ref = _internal_of_reference()
    ref_of_int = np.full(NP, -1, np.int64)
    ref_of_int[int_of_ref] = np.arange(D_IN_PROJ)
    shard = np.where(ref_of_int >= 0, ref_of_int // SHARD_IN, -1)
    src = np.where(ref_of_int >= 0, ref_of_int, -10 - 2 * np.arange(NP))
    plan, zeros = [], 0
    for lo, hi in _runs(src, shard):
        if ref_of_int[lo] < 0:
            zeros += hi - lo
            continue
        if zeros:
            plan.append((None, 0, zeros))
            zeros = 0
        plan.append((int(ref_of_int[lo] // SHARD_IN), int(ref_of_int[lo] % SHARD_IN), hi - lo))
    if zeros:
        plan.append((None, 0, zeros))
    tr = RELAYOUT_ROWS

    def body(w_ref, o_ref):
        o_ref[...] = jnp.concatenate(
            [jnp.zeros((tr, n), o_ref.dtype) if s is None else _lane_window(w_ref, (s,), c0, n) for s, c0, n in plan], axis=1)

    return pl.pallas_call(
        body, name="w_in_relayout", grid=(D_MODEL // tr,),
        in_specs=[pl.BlockSpec((N_DEV, tr, SHARD_IN), lambda i: (0, i, 0))],
        out_specs=pl.BlockSpec((tr, NP), lambda i: (i, 0)),
        out_shape=jax.ShapeDtypeStruct((D_MODEL, NP), w_gathered.dtype),
        compiler_params=_params(("parallel",)),
    )(w_gathered)


def _w_in_slabs(dw):
    int_of_ref = _internal_of_reference()
    plan = []
    for s in range(N_DEV):
        cols = int_of_ref[s * SHARD_IN:(s + 1) * SHARD_IN]
        plan.append([(int(cols[lo]), hi - lo) for lo, hi in _runs(cols, np.zeros_like(cols))])
    tr = RELAYOUT_ROWS

    def body(dw_ref, o_ref):
        for s in range(N_DEV):
            o_ref[s] = jnp.concatenate([_lane_window(dw_ref, (), c0, n) for c0, n in plan[s]], axis=1)

    return pl.pallas_call(
        body, name="dw_in_relayout", grid=(D_MODEL // tr,),
        in_specs=[pl.BlockSpec((tr, NP), lambda i: (i, 0))],
        out_specs=pl.BlockSpec((N_DEV, tr, SHARD_IN), lambda i: (0, i, 0)),
        out_shape=jax.ShapeDtypeStruct((N_DEV, D_MODEL, SHARD_IN), dw.dtype),
        compiler_params=_params(("parallel",)),
    )(dw)


def _rep_heads(a):
    return jnp.repeat(a, HEAD_DIM, axis=1)


SMALL = (("norm_pre_w", 2048), ("conv_b", 4096), ("dt_bias", 32), ("a_log", 32), ("d_skip", 32),
         ("ssd_norm_w", 2048), ("attn_sinks", 16), ("norm_post_w", 2048))
SMALL_USED = sum(size for _, size in SMALL)
SMALL_LEN = 10368


def _pack_small(d, loss=None):
    parts = [d[name].reshape(1, size) for name, size in SMALL]
    tail = jnp.zeros((1, SMALL_LEN - SMALL_USED), F32)
    if loss is not None:
        tail = tail.at[0, 0].set(loss)
    return jnp.concatenate(parts + [tail], axis=1)


def _unpack_small(vec):
    out, off = {}, 0
    for name, size in SMALL:
        out[name] = vec[:, off:off + size]
        off += size
    return out


def kernel(x, meta_tokens, norm_pre_w, w_in, conv_w, conv_b, dt_bias, a_log, d_skip, ssd_norm_w, attn_sinks, w_out, norm_post_w, loss_target, m_meta_tokens, m_norm_pre_w, m_w_in, m_conv_w, m_conv_b, m_dt_bias, m_a_log, m_d_skip, m_ssd_norm_w, m_attn_sinks, m_w_out, m_norm_post_w, v_meta_tokens, v_norm_pre_w, v_w_in, v_conv_w, v_conv_b, v_dt_bias, v_a_log, v_d_skip, v_ssd_norm_w, v_attn_sinks, v_w_out, v_norm_post_w):
    seq = x.shape[1]
    lp = seq + 2 * CHUNK
    x2d = x[0]

    w_in_g, conv_w_g, meta_g = _gather_two_level([w_in[0].astype(BF16), conv_w[0], meta_tokens], "gather_w_in")
    w_all = _w_in_internal(w_in_g)
    conv_w_full = _perm_xbc(jnp.transpose(conv_w_g, (1, 0, 2)).reshape(CONV_WIDTH, D_CONV))
    conv_b_int = _perm_xbc(conv_b)
    meta_full = jnp.transpose(meta_g, (1, 0, 2)).reshape(N_META, D_MODEL)
    h0 = jnp.concatenate([jnp.zeros((PAD_LEAD, D_MODEL), F32), meta_full], axis=0)

    pos = (jnp.arange(lp) - PAD_LEAD).astype(F32)
    half = HEAD_DIM // 2
    inv = ROPE_THETA ** (-jnp.arange(half, dtype=F32) / half)
    ang = pos[:, None] * inv[None, :]
    cos_t = jnp.tile(jnp.cos(ang), (1, 4))
    sin_t = jnp.tile(jnp.concatenate([-jnp.sin(ang), jnp.sin(ang)], axis=1), (1, 2))
    head_of_col = np.arange(D_SSD) // HEAD_DIM
    expand = jnp.asarray((np.arange(128)[:, None] == head_of_col[None, :]).astype(np.float32))
    reduce_t = jnp.asarray((head_of_col[:, None] == np.arange(128)[None, :]).astype(np.float32))
    tri = jnp.asarray(np.tril(np.ones((CHUNK, CHUNK), np.float32)))
    a_rep = _rep_heads(-jnp.exp(a_log))
    dsk_rep = _rep_heads(d_skip)
    dt_bias_rep = _rep_heads(dt_bias)
    sink_stack = jnp.repeat(attn_sinks.reshape(KV_HEADS, REP), CHUNK, axis=1).reshape(KV_HEADS, REP * CHUNK, 1)

    hn = _prenorm(x2d, h0, norm_pre_w)
    tm = _tile(lp, (1056, 704, 128, 64))
    proj, w_out_g = _matmul(hn, w_all, tm=tm, tn=1536, tk=D_MODEL, out_dtype=F32, name="in_proj",
                            comm=_Comm([(w_out[0].astype(BF16), "gather")]))
    w_out_full = w_out_g.reshape(D_MIX, D_MODEL)
    xbc = _conv_fwd(proj, conv_w_full, conv_b_int)
    qr, kr, vb, dt_rep = _act_fwd(proj, cos_t, sin_t, expand, dt_bias_rep)
    mix, ytot, hprev = _ssd_fwd(xbc, dt_rep, proj, a_rep, dsk_rep, ssd_norm_w, tri)
    att, mix = _attn_fwd(qr, kr, vb, proj, sink_stack, mix)
    out = _matmul(mix, w_out_full, tm=tm, tn=1024, tk=D_MIX, out_dtype=F32, name="out_proj")
    dout, dy, g_norm_post, loss_part = _post_loss(out, x2d, loss_target[0], norm_post_w)

    dmix = _matmul(dout, w_out_full, trans_b=True, tm=tm, tn=1024, tk=D_MODEL, out_dtype=F32, name="dmix")
    dw_out = _matmul(mix, dout, trans_a=True, tm=512, tn=1024, tk=lp, out_dtype=BF16, name="dw_out")
    dqr, dkr, dv, dg, dsink_rows = _attn_bwd(qr, kr, vb, att, proj, dmix, sink_stack)
    dproj, dxbc, ddt_part, dd_part, da_part, g_ssd_norm, g_out = _ssd_bwd(
        dmix, ytot, proj, xbc, dt_rep, hprev, a_rep, dsk_rep, ssd_norm_w, tri,
        _Comm([(dw_out.reshape(N_DEV, D_MIX // N_DEV, D_MODEL), "scatter")]))
    dproj, dconv_w_int, dconv_b_int = _conv_bwd(dxbc, proj, conv_w_full, conv_b_int, dproj)
    dproj, ddt_bias = _act_bwd(dqr, dkr, dv, dg, ddt_part, cos_t, sin_t, reduce_t, dproj)
    dw_all = _matmul(hn, dproj, trans_a=True, tm=512, tn=1024, tk=lp, out_dtype=BF16, name="dw_in")
    dw_slabs = _w_in_slabs(dw_all)
    dw_chip = _pair_sum(dw_slabs, _exchange_sibling(dw_slabs, "dw_in_sibling"), "dw_in_pair_sum")
    dhn, g_in = _matmul(dproj, w_all, trans_b=True, tm=tm, tn=1024, tk=1536, out_dtype=F32, name="dhn",
                        comm=_Comm([(dw_chip, "scatter")], scope="chips"))
    grad_x, dh0, g_norm_pre = _prenorm_bwd(dhn, x2d, h0, dy, norm_pre_w)

    dmeta = dh0[PAD_LEAD:, :]
    dconv_w_ref = _unperm_xbc(dconv_w_int)
    heads = lambda part: part.reshape(SSD_HEADS, HEAD_DIM).sum(axis=1).reshape(1, SSD_HEADS)
    small_local = _pack_small({
        "norm_pre_w": g_norm_pre, "conv_b": _unperm_xbc(dconv_b_int), "dt_bias": ddt_bias[:, :SSD_HEADS],
        "a_log": heads(da_part) * (-jnp.exp(a_log)), "d_skip": heads(dd_part), "ssd_norm_w": g_ssd_norm,
        "attn_sinks": dsink_rows.reshape(Q_HEADS, CHUNK).sum(axis=1).reshape(1, Q_HEADS),
        "norm_post_w": g_norm_post}, loss=loss_part[0, 0])
    g_conv, g_meta, g_small = _exchange(
        [(jnp.transpose(dconv_w_ref.reshape(CONV_WIDTH, N_DEV, D_CONV // N_DEV), (1, 0, 2)), "scatter"),
         (jnp.transpose(dmeta.reshape(N_META, N_DEV, D_MODEL // N_DEV), (1, 0, 2)), "scatter"),
         (small_local, "gather")], "exchange_small")

    res = {}
    res["w_in"] = [o[None] for o in _adamw(g_in, w_in[0], m_w_in[0], v_w_in[0], "adamw_w_in")]
    res["w_out"] = [o[None] for o in _adamw(g_out, w_out[0], m_w_out[0], v_w_out[0], "adamw_w_out")]
    res["conv_w"] = [o[None] for o in _adamw(g_conv, conv_w[0], m_conv_w[0], v_conv_w[0], "adamw_conv_w")]
    res["meta_tokens"] = _adamw(g_meta, meta_tokens, m_meta_tokens, v_meta_tokens, "adamw_meta")
    given = dict(norm_pre_w=(norm_pre_w, m_norm_pre_w, v_norm_pre_w), conv_b=(conv_b, m_conv_b, v_conv_b),
                 dt_bias=(dt_bias, m_dt_bias, v_dt_bias), a_log=(a_log, m_a_log, v_a_log),
                 d_skip=(d_skip, m_d_skip, v_d_skip), ssd_norm_w=(ssd_norm_w, m_ssd_norm_w, v_ssd_norm_w),
                 attn_sinks=(attn_sinks, m_attn_sinks, v_attn_sinks),
                 norm_post_w=(norm_post_w, m_norm_post_w, v_norm_post_w))
    packed = [_pack_small({k: t[j] for k, t in given.items()}) for j in range(3)]
    small_out = _adamw(g_small, packed[0], packed[1], packed[2], "adamw_small")
    small_res = [_unpack_small(r) for r in small_out]
    loss = small_out[0][0, SMALL_USED]

    order = ["meta_tokens", "norm_pre_w", "w_in", "conv_w", "conv_b", "dt_bias", "a_log", "d_skip", "ssd_norm_w",
             "attn_sinks", "w_out", "norm_post_w"]
    outs = []
    for j in range(4):
        for name in order:
            outs.append(res[name][j] if name in res else small_res[j][name])
    return (loss, grad_x[None], *outs)
```

```python
import functools

import numpy as np
import jax
import jax.numpy as jnp
from jax import lax
from jax.experimental import pallas as pl
from jax.experimental.pallas import tpu as pltpu

F32 = jnp.float32
BF16 = jnp.bfloat16
HIGHEST = lax.Precision.HIGHEST

N_DEV = 8
D_MODEL = 2048
CHUNK = 64
N_META = 16
PAD_LEAD = CHUNK - N_META
EPS = 1e-6
N_GROUPS = 8
HEADS_PER_GROUP = 4
HEAD_DIM = 64
GROUP_W = HEADS_PER_GROUP * HEAD_DIM
D_STATE = 128
D_SSD = 2048
D_CONV = 4096
SSD_HEADS = 32
CONV_WIDTH = 4
Q_HEADS = 16
KV_HEADS = 4
REP = 4
D_ATT = 1024
D_KV = 256
WINDOW_CHUNKS = 2
BAND = (WINDOW_CHUNKS + 1) * CHUNK
ROPE_THETA = 10000.0
D_MIX = D_SSD + D_ATT
D_IN_PROJ = 8736
SHARD_IN = D_IN_PROJ // N_DEV

OFF_Z, OFF_XBC, OFF_Q, OFF_G, OFF_K, OFF_V, OFF_DT = 0, 2048, 6144, 7168, 8192, 8448, 8704
NP = 9216
TAIL_W = NP - OFF_Q
XBC_BLK = 512
SSD_GPS = 8
SSD_CPS = 2

ADAM_LR, ADAM_B1, ADAM_B2, ADAM_EPS, ADAM_WD, ADAM_STEP = 0.001, 0.9, 0.999, 1e-08, 0.01, 10

VMEM_LIMIT = 48 * 1024 * 1024

NN = (((1,), (0,)), ((), ()))
NT = (((1,), (1,)), ((), ()))
TN = (((0,), (0,)), ((), ()))
ANY = pl.BlockSpec(memory_space=pl.ANY)


def _dot(a, b, dims=NN, precision=None):
    return lax.dot_general(a, b, dims, precision=precision, preferred_element_type=F32)


def _tile(n, prefs):
    for t in prefs:
        if n % t == 0:
            return t
    return n


def _params(sem, vmem=VMEM_LIMIT, side_effects=False):
    return pltpu.CompilerParams(dimension_semantics=sem, vmem_limit_bytes=vmem, has_side_effects=side_effects)


def _sigmoid(x):
    return 1.0 / (1.0 + jnp.exp(-x))


class _Comm:
    def __init__(self, items, scope="devices"):
        self.items = items
        self.scope = scope
        self.slabs = slabs = N_DEV if scope == "devices" else N_DEV // 2
        self.n = n = len(items)
        self.operands = [arr for arr, _ in items]
        self.in_specs = [ANY] * n
        self.out_specs = [ANY] * n
        self.out_shape = [jax.ShapeDtypeStruct((slabs,) + tuple(arr.shape) if kind == "gather" else tuple(arr.shape),
                                               arr.dtype) for arr, kind in items]
        self.scratch = [pltpu.SemaphoreType.DMA((n, slabs - 1)), pltpu.SemaphoreType.DMA((n, slabs - 1)),
                        pltpu.SemaphoreType.DMA((n,))]

    def _places(self):
        pos = (lax.axis_index("x"), lax.axis_index("y"), lax.axis_index("c"))
        if self.scope == "devices":
            index = lambda p: 4 * p[0] + 2 * p[1] + p[2]
            masks = range(1, N_DEV)
        else:
            index = lambda p: 2 * p[0] + p[1]
            masks = (2, 4, 6)
        peers = []
        for k in masks:
            p = tuple(1 - pos[b] if (k >> (2 - b)) & 1 else pos[b] for b in range(3))
            peers.append((p, index(p)))
        return index(pos), peers

    def _copies(self, ins, outs, sems, landed):
        send_sems, recv_sems, local_sems = sems
        me, peers = self._places()
        local, remote = [], []
        for a, (_, kind) in enumerate(self.items):
            own = ins[a] if kind == "gather" else ins[a].at[me]
            local.append(pltpu.make_async_copy(own, outs[a].at[me], local_sems.at[a]))
            for k, (p, pid) in enumerate(peers):
                remote.append(pltpu.make_async_remote_copy(
                    src_ref=ins[a] if kind == "gather" else ins[a].at[pid],
                    dst_ref=outs[a].at[pid if landed else me],
                    send_sem=send_sems.at[a, k], recv_sem=recv_sems.at[a, k],
                    device_id=p, device_id_type=pl.DeviceIdType.MESH))
        return local, remote

    def start(self, ins, outs, sems):
        local, remote = self._copies(ins, outs, sems, landed=False)
        for cp in local + remote:
            cp.start()

    def wait(self, ins, outs, sems):
        local, remote = self._copies(ins, outs, sems, landed=True)
        for cp in remote + local:
            cp.wait()


def _exchange(items, name):
    comm = _Comm(items)
    n = comm.n

    def body(*refs):
        ins, outs, sems = refs[:n], refs[n:2 * n], refs[2 * n:]
        comm.start(ins, outs, sems)
        comm.wait(ins, outs, sems)

    return pl.pallas_call(
        body, name=name, in_specs=comm.in_specs, out_specs=comm.out_specs, out_shape=comm.out_shape,
        scratch_shapes=comm.scratch, compiler_params=pltpu.CompilerParams(has_side_effects=True),
    )(*comm.operands)


def _exchange_sibling(slabs, name):
    def body(g_ref, land_ref, send_sems, recv_sems):
        x, y, c = lax.axis_index("x"), lax.axis_index("y"), lax.axis_index("c")
        copies = [pltpu.make_async_remote_copy(
            src_ref=g_ref.at[2 * q + (1 - c)], dst_ref=land_ref.at[q], send_sem=send_sems.at[q], recv_sem=recv_sems.at[q],
            device_id=(x, y, 1 - c), device_id_type=pl.DeviceIdType.MESH) for q in range(N_DEV // 2)]
        for cp in copies:
            cp.start()
        for cp in copies:
            cp.wait()

    return pl.pallas_call(
        body, name=name, in_specs=[ANY], out_specs=ANY,
        out_shape=jax.ShapeDtypeStruct((N_DEV // 2,) + tuple(slabs.shape[1:]), slabs.dtype),
        scratch_shapes=[pltpu.SemaphoreType.DMA((N_DEV // 2,)), pltpu.SemaphoreType.DMA((N_DEV // 2,))],
        compiler_params=pltpu.CompilerParams(has_side_effects=True),
    )(slabs)


def _pair_sum(slabs, landed, name):
    _, rows, cols = slabs.shape
    tr = _tile(rows, (256, 128, 64, 16, 8))

    def body(g_ref, l_ref, o_ref):
        mine = g_ref[lax.axis_index("c")]
        o_ref[...] = (mine.astype(F32) + l_ref[...].astype(F32)).astype(o_ref.dtype)

    return pl.pallas_call(
        body, name=name, grid=(N_DEV // 2, rows // tr),
        in_specs=[pl.BlockSpec((None, 2, tr, cols), lambda q, i: (q, 0, i, 0)),
                  pl.BlockSpec((None, tr, cols), lambda q, i: (q, i, 0))],
        out_specs=pl.BlockSpec((None, tr, cols), lambda q, i: (q, i, 0)),
        out_shape=jax.ShapeDtypeStruct((N_DEV // 2, rows, cols), slabs.dtype),
        compiler_params=_params(("parallel", "parallel")),
    )(slabs.reshape(N_DEV // 2, 2, rows, cols), landed)


class _GatherTwoLevel:
    def __init__(self, arrays):
        self.arrays = arrays
        self.n = n = len(arrays)
        self.operands = list(arrays)
        self.in_specs = [ANY] * n
        self.out_specs = [ANY] * n
        self.out_shape = [jax.ShapeDtypeStruct((N_DEV,) + tuple(a.shape), a.dtype) for a in arrays]
        self.scratch = [pltpu.SemaphoreType.DMA((n, N_DEV - 1)), pltpu.SemaphoreType.DMA((n, N_DEV - 1)),
                        pltpu.SemaphoreType.DMA((n,))]

    def _plan(self, ins, outs, sems):
        send_sems, recv_sems, local_sems = sems
        x, y, c = lax.axis_index("x"), lax.axis_index("y"), lax.axis_index("c")
        me, sibling = (x, y, c), (x, y, 1 - c)
        chips = [(1 - x, y), (x, 1 - y), (1 - x, 1 - y)]

        def slab(a, place):
            return outs[a].at[4 * place[0] + 2 * place[1] + place[2]]

        def copy(a, k, block, to, src=None):
            return pltpu.make_async_remote_copy(
                src_ref=slab(a, block) if src is None else src, dst_ref=slab(a, block),
                send_sem=send_sems.at[a, k], recv_sem=recv_sems.at[a, k],
                device_id=to, device_id_type=pl.DeviceIdType.MESH)

        own, mine = [], []
        for a in range(self.n):
            mine.append(pltpu.make_async_copy(ins[a], slab(a, me), local_sems.at[a]))
            own.append(copy(a, 0, me, sibling, src=ins[a]))
            own += [copy(a, 1 + j, me, (*chip, c), src=ins[a]) for j, chip in enumerate(chips)]
        return me, sibling, chips, c, copy, own, mine

    def start(self, ins, outs, sems):
        _, _, _, _, _, own, mine = self._plan(ins, outs, sems)
        for cp in mine + own:
            cp.start()

    def wait(self, ins, outs, sems):
        me, sibling, chips, c, copy, own, mine = self._plan(ins, outs, sems)
        forwards = []
        for j, chip in enumerate(chips):
            for a in range(self.n):
                copy(a, 1 + j, (*chip, c), me).wait_recv()
                fwd = copy(a, 4 + j, (*chip, c), sibling)
                fwd.start()
                forwards.append(fwd)
        for a in range(self.n):
            copy(a, 0, sibling, me).wait_recv()
            for j, chip in enumerate(chips):
                copy(a, 4 + j, (*chip, 1 - c), me).wait_recv()
        for cp in own + forwards:
            cp.wait_send()
        for loc in mine:
            loc.wait()


def _gather_two_level(arrays, name):
    comm = _GatherTwoLevel(arrays)
    n = comm.n

    def body(*refs):
        ins, outs, sems = refs[:n], refs[n:2 * n], refs[2 * n:]
        comm.start(ins, outs, sems)
        comm.wait(ins, outs, sems)

    return pl.pallas_call(
        body, name=name, in_specs=comm.in_specs, out_specs=comm.out_specs, out_shape=comm.out_shape,
        scratch_shapes=comm.scratch, compiler_params=pltpu.CompilerParams(has_side_effects=True),
    )(*arrays)


def _matmul(a, b, *, tm, tn, tk, out_dtype, name, trans_a=False, trans_b=False, comm=None):
    m, k = (a.shape[1], a.shape[0]) if trans_a else a.shape
    n = b.shape[0] if trans_b else b.shape[1]
    nk = k // tk
    dims = TN if trans_a else (NT if trans_b else NN)
    assert not (trans_a and trans_b)
    nc = comm.n if comm else 0
    grid = (m // tm, n // tn, nk)

    def body(*refs):
        a_ref, b_ref = refs[:2]
        cin = refs[2:2 + nc]
        o_ref = refs[2 + nc]
        cout = refs[3 + nc:3 + 2 * nc]
        scratch = refs[3 + 2 * nc:]
        sems = scratch[len(scratch) - 3:] if comm else None
        i, j, kk = pl.program_id(0), pl.program_id(1), pl.program_id(2)
        if comm:
            @pl.when((i == 0) & (j == 0) & (kk == 0))
            def _():
                comm.start(cin, cout, sems)

        if nk == 1:
            o_ref[...] = _dot(a_ref[...], b_ref[...], dims).astype(out_dtype)
        else:
            acc_ref = scratch[0]

            @pl.when(kk == 0)
            def _():
                acc_ref[...] = jnp.zeros_like(acc_ref)

            acc_ref[...] += _dot(a_ref[...], b_ref[...], dims)

            @pl.when(kk == nk - 1)
            def _():
                o_ref[...] = acc_ref[...].astype(out_dtype)

        if comm:
            @pl.when((i == grid[0] - 1) & (j == grid[1] - 1) & (kk == nk - 1))
            def _():
                comm.wait(cin, cout, sems)

    a_spec = (pl.BlockSpec((tk, tm), lambda i, j, kk: (kk, i)) if trans_a
              else pl.BlockSpec((tm, tk), lambda i, j, kk: (i, kk)))
    b_spec = (pl.BlockSpec((tn, tk), lambda i, j, kk: (j, kk)) if trans_b
              else pl.BlockSpec((tk, tn), lambda i, j, kk: (kk, j)))
    sem = ("arbitrary",) * 3 if comm else ("parallel", "parallel", "arbitrary")
    res = pl.pallas_call(
        body, name=name, grid=grid,
        in_specs=[a_spec, b_spec] + (comm.in_specs if comm else []),
        out_specs=[pl.BlockSpec((tm, tn), lambda i, j, kk: (i, j))] + (comm.out_specs if comm else []),
        out_shape=[jax.ShapeDtypeStruct((m, n), out_dtype)] + (comm.out_shape if comm else []),
        scratch_shapes=([] if nk == 1 else [pltpu.VMEM((tm, tn), F32)]) + (comm.scratch if comm else []),
        compiler_params=_params(sem, side_effects=bool(comm)),
    )(a, b, *(comm.operands if comm else []))
    return res if comm else res[0]


ROW_K = 6
X_K = 8


def _x_specs(nx, k):
    return [pl.BlockSpec((CHUNK, D_MODEL), functools.partial(lambda u, i: (jnp.clip(k * i + u - 1, 0, nx - 1), 0), u))
            for u in range(k)]


def _chunk_of_row(i, k):
    return k * i + lax.broadcasted_iota(jnp.int32, (k * CHUNK, 1), 0) // CHUNK


def _prenorm(x2d, h0, w):
    nx = x2d.shape[0] // CHUNK
    nc = nx + 2
    k = _tile(nc, (ROW_K, 3, 2, 1))

    def body(*refs):
        x_refs = refs[:k]
        h0_ref, w_ref, o_ref = refs[k:]
        i = pl.program_id(0)
        x = jnp.concatenate([r[...] for r in x_refs], axis=0)
        head = jnp.concatenate([h0_ref[...], x[CHUNK:, :]], axis=0)
        chunk = _chunk_of_row(i, k)
        h = jnp.where(i == 0, head, x) * (chunk <= nx).astype(F32)
        rstd = lax.rsqrt(jnp.mean(h * h, axis=-1, keepdims=True) + EPS)
        o_ref[...] = (h * rstd * w_ref[...]).astype(BF16)

    rows = k * CHUNK
    return pl.pallas_call(
        body, name="prenorm", grid=(nc // k,),
        in_specs=_x_specs(nx, k) + [pl.BlockSpec((CHUNK, D_MODEL), lambda i: (0, 0)), pl.BlockSpec((1, D_MODEL), lambda i: (0, 0))],
        out_specs=pl.BlockSpec((rows, D_MODEL), lambda i: (i, 0)),
        out_shape=jax.ShapeDtypeStruct((nc * CHUNK, D_MODEL), BF16),
        compiler_params=_params(("parallel",)),
    )(*([x2d] * k), h0, w)


CONV_COLS = 512
HALO = 8


def _conv_pre(ext, w, b):
    taps = [ext[HALO:, :]] + [pltpu.roll(ext, j, 0)[HALO:, :] for j in range(1, CONV_WIDTH)]
    acc = b + w[3:4, :] * taps[0]
    for j in range(1, CONV_WIDTH):
        acc = acc + w[3 - j:4 - j, :] * taps[j]
    return acc, taps


def _conv_fwd(proj, conv_w, conv_b):
    lp = proj.shape[0]
    t = _tile(lp, (704, 384, 128, 64))
    hb = t // HALO
    c0 = OFF_XBC // CONV_COLS

    def body(u_ref, halo_ref, w_ref, b_ref, o_ref):
        i = pl.program_id(0)
        halo = halo_ref[...] * (i > 0).astype(F32)
        pre, _ = _conv_pre(jnp.concatenate([halo, u_ref[...]], axis=0), w_ref[...], b_ref[...])
        o_ref[...] = pre * _sigmoid(pre)

    return pl.pallas_call(
        body, name="conv_fwd", grid=(lp // t, D_CONV // CONV_COLS),
        in_specs=[
            pl.BlockSpec((t, CONV_COLS), lambda i, j: (i, c0 + j)),
            pl.BlockSpec((HALO, CONV_COLS), lambda i, j: (jnp.maximum(i * hb - 1, 0), c0 + j)),
            pl.BlockSpec((CONV_WIDTH, CONV_COLS), lambda i, j: (0, j)),
            pl.BlockSpec((1, CONV_COLS), lambda i, j: (0, j)),
        ],
        out_specs=pl.BlockSpec((t, CONV_COLS), lambda i, j: (i, j)),
        out_shape=jax.ShapeDtypeStruct((lp, D_CONV), F32),
        compiler_params=_params(("parallel", "parallel")),
    )(proj, proj, conv_w, conv_b)


def _conv_bwd(dxbc, proj, conv_w, conv_b, dproj):
    lp = proj.shape[0]
    t = _tile(lp, (704, 384, 128, 64))
    hb = t // HALO
    nt = lp // t
    c0 = OFF_XBC // CONV_COLS

    def body(dx_ref, dxn_ref, u_ref, up_ref, un_ref, w_ref, b_ref, _, du_ref, dw_ref, db_ref):
        i = pl.program_id(1)
        w = w_ref[...]
        up = up_ref[...] * (i > 0).astype(F32)
        ext = jnp.concatenate([up, u_ref[...], un_ref[...]], axis=0)
        pre, taps = _conv_pre(ext, w, b_ref[...])
        dxn = dxn_ref[...] * (i < nt - 1).astype(F32)
        dxe = jnp.concatenate([dx_ref[...], dxn], axis=0)
        sg = _sigmoid(pre)
        dpre = dxe * sg * (1.0 + pre * (1.0 - sg))
        du = w[3:4, :] * dpre[:t, :]
        for j in range(1, CONV_WIDTH):
            du = du + w[3 - j:4 - j, :] * pltpu.roll(dpre, t + HALO - j, 0)[:t, :]
        du_ref[...] = du.astype(BF16)

        @pl.when(i == 0)
        def _():
            dw_ref[...] = jnp.zeros_like(dw_ref)
            db_ref[...] = jnp.zeros_like(db_ref)

        dp = dpre[:t, :]
        db_ref[...] += jnp.sum(dp, axis=0, keepdims=True)
        for j in range(CONV_WIDTH):
            dw_ref[3 - j:4 - j, :] += jnp.sum(dp * taps[j][:t, :], axis=0, keepdims=True)

    nxt = lambda i: jnp.minimum((i + 1) * hb, lp // HALO - 1)
    return pl.pallas_call(
        body, name="conv_bwd", grid=(D_CONV // CONV_COLS, nt),
        in_specs=[
            pl.BlockSpec((t, CONV_COLS), lambda j, i: (i, j)),
            pl.BlockSpec((HALO, CONV_COLS), lambda j, i: (nxt(i), j)),
            pl.BlockSpec((t, CONV_COLS), lambda j, i: (i, c0 + j)),
            pl.BlockSpec((HALO, CONV_COLS), lambda j, i: (jnp.maximum(i * hb - 1, 0), c0 + j)),
            pl.BlockSpec((HALO, CONV_COLS), lambda j, i: (nxt(i), c0 + j)),
            pl.BlockSpec((CONV_WIDTH, CONV_COLS), lambda j, i: (0, j)),
            pl.BlockSpec((1, CONV_COLS), lambda j, i: (0, j)),
            ANY,
        ],
        out_specs=[
            pl.BlockSpec((t, CONV_COLS), lambda j, i: (i, c0 + j)),
            pl.BlockSpec((CONV_WIDTH, CONV_COLS), lambda j, i: (0, j)),
            pl.BlockSpec((1, CONV_COLS), lambda j, i: (0, j)),
        ],
        out_shape=[
            jax.ShapeDtypeStruct((lp, NP), BF16),
            jax.ShapeDtypeStruct((CONV_WIDTH, D_CONV), F32),
            jax.ShapeDtypeStruct((1, D_CONV), F32),
        ],
        input_output_aliases={7: 0},
        compiler_params=_params(("parallel", "arbitrary")),
    )(dxbc, dxbc, proj, proj, proj, conv_w, conv_b, dproj)


def _swap_halves(t):
    w = t.shape[-1]
    lane = lax.broadcasted_iota(jnp.int32, t.shape, 1)
    return jnp.where((lane % HEAD_DIM) < HEAD_DIM // 2, pltpu.roll(t, w - HEAD_DIM // 2, 1),
                     pltpu.roll(t, HEAD_DIM // 2, 1))


def _act_fwd(proj, cos_t, sin_t, expand, dt_bias_rep):
    lp = proj.shape[0]
    t = _tile(lp, (384, 128, 64))

    def body(q_ref, k_ref, v_ref, dt_ref, cos_ref, sin_ref, ex_ref, bias_ref, qo_ref, ko_ref, vo_ref, dto_ref):
        i = pl.program_id(0)
        cos = cos_ref[...]
        sin = sin_ref[...]
        q = q_ref[...]
        qo_ref[...] = (q * jnp.tile(cos, (1, D_ATT // 128)) + _swap_halves(q) * jnp.tile(sin, (1, D_ATT // 128))).astype(BF16)
        k = k_ref[...]
        ko_ref[...] = (k * jnp.tile(cos, (1, D_KV // 128)) + _swap_halves(k) * jnp.tile(sin, (1, D_KV // 128))).astype(BF16)
        vo_ref[...] = v_ref[...].astype(BF16)
        raw = _dot(dt_ref[...], ex_ref[...], NN, HIGHEST) + bias_ref[...]
        sp = jnp.maximum(raw, 0.0) + jnp.log1p(jnp.exp(-jnp.abs(raw)))
        row = i * t + lax.broadcasted_iota(jnp.int32, sp.shape, 0)
        dto_ref[...] = jnp.where(row >= PAD_LEAD, sp, 0.0)

    return pl.pallas_call(
        body, name="act_fwd", grid=(lp // t,),
        in_specs=[
            pl.BlockSpec((t, D_ATT), lambda i: (i, OFF_Q // D_ATT)),
            pl.BlockSpec((t, D_KV), lambda i: (i, OFF_K // D_KV)),
            pl.BlockSpec((t, D_KV), lambda i: (i, OFF_V // D_KV)),
            pl.BlockSpec((t, 128), lambda i: (i, OFF_DT // 128)),
            pl.BlockSpec((t, 128), lambda i: (i, 0)),
            pl.BlockSpec((t, 128), lambda i: (i, 0)),
            pl.BlockSpec((128, D_SSD), lambda i: (0, 0)),
            pl.BlockSpec((1, D_SSD), lambda i: (0, 0)),
        ],
        out_specs=[
            pl.BlockSpec((t, D_ATT), lambda i: (i, 0)),
            pl.BlockSpec((t, D_KV), lambda i: (i, 0)),
            pl.BlockSpec((t, D_KV), lambda i: (i, 0)),
            pl.BlockSpec((t, D_SSD), lambda i: (i, 0)),
        ],
        out_shape=[
            jax.ShapeDtypeStruct((lp, D_ATT), BF16),
            jax.ShapeDtypeStruct((lp, D_KV), BF16),
            jax.ShapeDtypeStruct((lp, D_KV), BF16),
            jax.ShapeDtypeStruct((lp, D_SSD), F32),
        ],
        compiler_params=_params(("parallel",)),
    )(proj, proj, proj, proj, cos_t, sin_t, expand, dt_bias_rep)


def _act_bwd(dqr, dkr, dv, dg, ddt_part, cos_t, sin_t, reduce_t, dproj):
    lp = dqr.shape[0]
    t = _tile(lp, (384, 128, 64))

    def body(dq_ref, dk_ref, dv_ref, dg_ref, ddt_ref, cos_ref, sin_ref, red_ref, _, o_ref, db_ref):
        i = pl.program_id(0)
        cos = cos_ref[...]
        sin = sin_ref[...]
        dq = dq_ref[...]
        dq = dq * jnp.tile(cos, (1, D_ATT // 128)) + _swap_halves(dq * jnp.tile(sin, (1, D_ATT // 128)))
        dk = dk_ref[...]
        dk = dk * jnp.tile(cos, (1, D_KV // 128)) + _swap_halves(dk * jnp.tile(sin, (1, D_KV // 128)))
        ddt = _dot(ddt_ref[...], red_ref[...], NN, HIGHEST)
        o_ref[...] = jnp.concatenate(
            [dq.astype(BF16), dg_ref[...].astype(BF16), dk.astype(BF16), dv_ref[...].astype(BF16), ddt.astype(BF16),
             jnp.zeros((t, NP - OFF_DT - 128), BF16)], axis=1)

        @pl.when(i == 0)
        def _():
            db_ref[...] = jnp.zeros_like(db_ref)

        db_ref[...] += jnp.sum(ddt, axis=0, keepdims=True)

    return pl.pallas_call(
        body, name="act_bwd", grid=(lp // t,),
        in_specs=[
            pl.BlockSpec((t, D_ATT), lambda i: (i, 0)),
            pl.BlockSpec((t, D_KV), lambda i: (i, 0)),
            pl.BlockSpec((t, D_KV), lambda i: (i, 0)),
            pl.BlockSpec((t, D_ATT), lambda i: (i, 0)),
            pl.BlockSpec((t, D_SSD), lambda i: (i, 0)),
            pl.BlockSpec((t, 128), lambda i: (i, 0)),
            pl.BlockSpec((t, 128), lambda i: (i, 0)),
            pl.BlockSpec((D_SSD, 128), lambda i: (0, 0)),
            ANY,
        ],
        out_specs=[pl.BlockSpec((t, TAIL_W), lambda i: (i, OFF_Q // TAIL_W)), pl.BlockSpec((1, 128), lambda i: (0, 0))],
        out_shape=[jax.ShapeDtypeStruct((lp, NP), BF16), jax.ShapeDtypeStruct((1, 128), F32)],
        input_output_aliases={8: 0},
        compiler_params=_params(("arbitrary",)),
    )(dqr, dkr, dv, dg, ddt_part, cos_t, sin_t, reduce_t, dproj)


def _cs_row(cs):
    row = lax.broadcasted_iota(jnp.int32, cs.shape, 0)
    lane = lax.broadcasted_iota(jnp.int32, cs.shape, 1)
    return jnp.sum(jnp.where(row == lane % HEAD_DIM, cs, 0.0), axis=0, keepdims=True)


def _ssd_fwd(xbc, dt_rep, proj, a_rep, dsk_rep, wn, tri):
    lp = xbc.shape[0]
    nc = lp // CHUNK
    gw = SSD_GPS * GROUP_W
    cps = _tile(nc, (SSD_CPS, 1))

    def body(xbc_ref, dt_ref, z_ref, a_ref, dsk_ref, wn_ref, tri_ref, yn_ref, ytot_ref, hprev_ref, h_scr):
        @pl.when(pl.program_id(1) == 0)
        def _():
            h_scr[...] = jnp.zeros_like(h_scr)

        G = range(SSD_GPS)
        colsl = [slice(gi * GROUP_W, (gi + 1) * GROUP_W) for gi in G]
        rows4 = lax.broadcasted_iota(jnp.int32, (GROUP_W, GROUP_W), 0) // HEAD_DIM
        cols4 = lax.broadcasted_iota(jnp.int32, (GROUP_W, GROUP_W), 1) // HEAD_DIM
        lrow = lax.broadcasted_iota(jnp.int32, (CHUNK, GROUP_W), 0)
        lcol = lax.broadcasted_iota(jnp.int32, (CHUNK, GROUP_W), 1) % HEAD_DIM

        def one_chunk(sub):
            rs = slice(sub * CHUNK, (sub + 1) * CHUNK)
            xbc = [xbc_ref[rs, gi * XBC_BLK:(gi + 1) * XBC_BLK] for gi in G]
            dt = [dt_ref[rs, colsl[gi]] for gi in G]
            xs = [xbc[gi][:, :GROUP_W] for gi in G]
            b = [xbc[gi][:, GROUP_W:GROUP_W + D_STATE].astype(BF16) for gi in G]
            c = [xbc[gi][:, GROUP_W + D_STATE:].astype(BF16) for gi in G]
            hprev = [h_scr[gi] for gi in G]
            cs = [_dot(tri_ref[...], dt[gi] * a_ref[:, colsl[gi]], NN, HIGHEST) for gi in G]
            yoff = [_dot(c[gi], hprev[gi].astype(BF16)) for gi in G]
            cs_t = [_cs_row(cs[gi]) for gi in G]
            xdt = [xs[gi] * dt[gi] for gi in G]
            cs_last = [cs[gi][CHUNK - 1:CHUNK, :] for gi in G]
            st = [_dot(b[gi], (xdt[gi] * jnp.exp(cs_last[gi] - cs[gi])).astype(BF16), TN) for gi in G]
            cb4 = [_dot(c[gi], jnp.concatenate([b[gi]] * HEADS_PER_GROUP, axis=0), NT) for gi in G]
            m = [(cb4[gi] * jnp.exp(jnp.where(lrow >= lcol, cs[gi] - cs_t[gi], -jnp.inf))).astype(BF16) for gi in G]
            xbd = [jnp.where(rows4 == cols4, jnp.concatenate([xdt[gi].astype(BF16)] * HEADS_PER_GROUP, axis=0), 0.0)
                   for gi in G]
            ydiag = [_dot(m[gi], xbd[gi]) for gi in G]
            for gi in G:
                cols = colsl[gi]
                ytot = ydiag[gi] + yoff[gi] * jnp.exp(cs[gi]) + dsk_ref[:, cols] * xs[gi]
                z = z_ref[rs, cols]
                gz = ytot * (z * _sigmoid(z))
                rstd = lax.rsqrt(jnp.mean(gz * gz, axis=-1, keepdims=True) + EPS)
                hprev_ref[sub, gi] = hprev[gi]
                h_scr[gi] = hprev[gi] * jnp.exp(cs_last[gi]) + st[gi]
                ytot_ref[rs, cols] = ytot
                yn_ref[rs, cols] = (gz * rstd * wn_ref[:, cols]).astype(BF16)

        for sub in range(cps):
            one_chunk(sub)

    vec = pl.BlockSpec((1, gw), lambda g, c: (0, g))
    blk = pl.BlockSpec((cps * CHUNK, gw), lambda g, c: (c, g))
    return pl.pallas_call(
        body, name="ssd_fwd", grid=(N_GROUPS // SSD_GPS, nc // cps),
        in_specs=[
            pl.BlockSpec((cps * CHUNK, SSD_GPS * XBC_BLK), lambda g, c: (c, g)),
            blk, blk, vec, vec, vec,
            pl.BlockSpec((CHUNK, CHUNK), lambda g, c: (0, 0)),
        ],
        out_specs=[blk, blk, pl.BlockSpec((cps, SSD_GPS, D_STATE, GROUP_W), lambda g, c: (c, g, 0, 0))],
        out_shape=[
            jax.ShapeDtypeStruct((lp, D_MIX), BF16),
            jax.ShapeDtypeStruct((lp, D_SSD), F32),
            jax.ShapeDtypeStruct((nc, N_GROUPS, D_STATE, GROUP_W), F32),
        ],
        scratch_shapes=[pltpu.VMEM((SSD_GPS, D_STATE, GROUP_W), F32)],
        compiler_params=_params(("parallel", "arbitrary")),
    )(xbc, dt_rep, proj, a_rep, dsk_rep, wn, tri)


def _ssd_bwd(dmix, ytot, proj, xbc, dt_rep, hprev, a_rep, dsk_rep, wn, tri, comm):
    lp = xbc.shape[0]
    nc = lp // CHUNK
    gps = SSD_GPS
    gw = gps * GROUP_W
    cps = _tile(nc, (SSD_CPS, 1))
    ncm = comm.n
    n_in, n_out = 10, 6
    grid = (N_GROUPS // gps, nc // cps)

    def all_groups(refs):
        (dyn_ref, ytot_ref, z_ref, xbc_ref, dt_ref, hprev_ref, a_ref, dsk_ref, wn_ref, tri_ref,
         dz_ref, dxbc_ref, ddt_ref, dd_ref, da_ref, dwn_ref, dh_scr) = refs
        G = range(gps)
        H = range(HEADS_PER_GROUP)
        cl = [slice(gi * GROUP_W, (gi + 1) * GROUP_W) for gi in G]
        hl = [slice(r * HEAD_DIM, (r + 1) * HEAD_DIM) for r in H]
        tri = tri_ref[...]
        xbc = [xbc_ref[:, gi * XBC_BLK:(gi + 1) * XBC_BLK] for gi in G]
        dt = [dt_ref[:, cl[gi]] for gi in G]
        a = [a_ref[:, cl[gi]] for gi in G]
        xs = [xbc[gi][:, :GROUP_W] for gi in G]
        bbf = [xbc[gi][:, GROUP_W:GROUP_W + D_STATE].astype(BF16) for gi in G]
        cbf = [xbc[gi][:, GROUP_W + D_STATE:].astype(BF16) for gi in G]
        hprev = [hprev_ref[gi] for gi in G]
        hbf = [hprev[gi].astype(BF16) for gi in G]
        dhn = [dh_scr[gi] for gi in G]
        dhnb = [dhn[gi].astype(BF16) for gi in G]
        cs = [_dot(tri, dt[gi] * a[gi], NN, HIGHEST) for gi in G]
        g = [_dot(cbf[gi], hbf[gi]) for gi in G]
        dxw = [_dot(bbf[gi], dhnb[gi]) for gi in G]
        cs_t = [_cs_row(cs[gi]) for gi in G]
        dy = []
        for gi in G:
            ytot = ytot_ref[:, cl[gi]]
            z = z_ref[:, cl[gi]]
            dyn = dyn_ref[:, cl[gi]]
            sz = _sigmoid(z)
            silu_z = z * sz
            gz = ytot * silu_z
            rstd = lax.rsqrt(jnp.mean(gz * gz, axis=-1, keepdims=True) + EPS)
            xhat = gz * rstd
            dwn_ref[:, cl[gi]] += jnp.sum(dyn * xhat, axis=0, keepdims=True)
            dxhat = dyn * wn_ref[:, cl[gi]]
            dgz = rstd * (dxhat - xhat * jnp.mean(dxhat * xhat, axis=-1, keepdims=True))
            dy.append(dgz * silu_z)
            dz_ref[:, cl[gi]] = (dgz * ytot * (sz * (1.0 + z * (1.0 - sz)))).astype(BF16)
            dd_ref[:, cl[gi]] += jnp.sum(dy[gi] * xs[gi], axis=0, keepdims=True)
        xdt = [xs[gi] * dt[gi] for gi in G]
        e = [jnp.exp(cs[gi]) for gi in G]
        cs_last = [cs[gi][CHUNK - 1:CHUNK, :] for gi in G]
        dte = [jnp.exp(cs_last[gi] - cs[gi]) for gi in G]
        cd = [jnp.exp(cs_last[gi]) for gi in G]
        dgb = [(dy[gi] * e[gi]).astype(BF16) for gi in G]
        dyb = [dy[gi].astype(BF16) for gi in G]
        xdtb = [xdt[gi].astype(BF16) for gi in G]
        dc = [_dot(dgb[gi], hbf[gi], NT) for gi in G]
        dhprev = [_dot(cbf[gi], dgb[gi], TN) for gi in G]
        db = [_dot((xdt[gi] * dte[gi]).astype(BF16), dhnb[gi], NT) for gi in G]
        row = lax.broadcasted_iota(jnp.int32, (CHUNK, CHUNK), 0)
        causal = row >= lax.broadcasted_iota(jnp.int32, (CHUNK, CHUNK), 1)
        cb = [_dot(cbf[gi], bbf[gi], NT) for gi in G]
        dm = [[_dot(dyb[gi][:, hl[r]], xdtb[gi][:, hl[r]], NT) for r in H] for gi in G]
        mb, dseg, dcbb = [], [], []
        for gi in G:
            mb.append([])
            dseg.append([])
            dcb = None
            for r in H:
                seg = cs[gi][:, r * HEAD_DIM:r * HEAD_DIM + 1] - cs_t[gi][:, hl[r]]
                lm = jnp.exp(jnp.where(causal, seg, -jnp.inf))
                m = cb[gi] * lm
                mb[gi].append(m.astype(BF16))
                dseg[gi].append(dm[gi][r] * m)
                dcb = dm[gi][r] * lm if r == 0 else dcb + dm[gi][r] * lm
            dcbb.append(dcb.astype(BF16))
        dxdt_diag = [[_dot(mb[gi][r], dyb[gi][:, hl[r]], TN) for r in H] for gi in G]
        ones = jnp.ones((CHUNK, HEAD_DIM), F32)
        colsum = [[_dot(dseg[gi][r], ones, TN, HIGHEST) for r in H] for gi in G]
        dc2 = [_dot(dcbb[gi], bbf[gi]) for gi in G]
        db2 = [_dot(dcbb[gi], cbf[gi], TN) for gi in G]
        dcs = []
        for gi in G:
            t_dte = dxw[gi] * xdt[gi] * dte[gi]
            dcs_last = (jnp.sum(dhn[gi] * hprev[gi], axis=0, keepdims=True) * cd[gi]
                        + jnp.sum(t_dte, axis=0, keepdims=True))
            diag = jnp.concatenate(
                [(jnp.sum(dseg[gi][r], axis=1, keepdims=True) - colsum[gi][r]) * (1.0 / HEAD_DIM) for r in H], axis=1)
            d = dy[gi] * g[gi] * e[gi] - t_dte + diag
            row = lax.broadcasted_iota(jnp.int32, d.shape, 0)
            dcs.append(d + jnp.where(row == CHUNK - 1, dcs_last, 0.0))
        dda = [_dot(tri, dcs[gi], TN, HIGHEST) for gi in G]
        for gi in G:
            dxdt = dxw[gi] * dte[gi] + jnp.concatenate(dxdt_diag[gi], axis=1)
            da_ref[:, cl[gi]] += jnp.sum(dda[gi] * dt[gi], axis=0, keepdims=True)
            ddt = dda[gi] * a[gi] + dxdt * xs[gi]
            dxs = dsk_ref[:, cl[gi]] * dy[gi] + dxdt * dt[gi]
            ddt_ref[:, cl[gi]] = ddt * (1.0 - jnp.exp(-dt[gi]))
            dxbc_ref[:, gi * XBC_BLK:(gi + 1) * XBC_BLK] = jnp.concatenate(
                [dxs, db[gi] + db2[gi], dc[gi] + dc2[gi]], axis=1)
            dh_scr[gi] = dhprev[gi] + dhn[gi] * cd[gi]

    def body(*refs):
        ins = refs[:n_in]
        cin = refs[n_in:n_in + ncm]
        outs = refs[n_in + ncm:n_in + ncm + n_out]
        cout = refs[n_in + ncm + n_out:n_in + 2 * ncm + n_out]
        dh_scr = refs[n_in + 2 * ncm + n_out]
        sems = refs[n_in + 2 * ncm + n_out + 1:]
        g, c = pl.program_id(0), pl.program_id(1)

        @pl.when((g == 0) & (c == 0))
        def _():
            comm.start(cin, cout, sems)

        @pl.when(c == 0)
        def _():
            dh_scr[...] = jnp.zeros_like(dh_scr)
            for ref in outs[3:]:
                ref[...] = jnp.zeros_like(ref)

        for sub in reversed(range(cps)):
            rs = pl.ds(sub * CHUNK, CHUNK)
            chunk_ins = tuple(r.at[rs] for r in ins[:5]) + (ins[5].at[sub],) + ins[6:]
            chunk_outs = tuple(r.at[rs] for r in outs[:3]) + outs[3:]
            all_groups(chunk_ins + chunk_outs + (dh_scr,))

        @pl.when((g == grid[0] - 1) & (c == grid[1] - 1))
        def _():
            comm.wait(cin, cout, sems)

    rev = lambda c: grid[1] - 1 - c
    vec = pl.BlockSpec((1, gw), lambda g, c: (0, g))
    blk = pl.BlockSpec((cps * CHUNK, gw), lambda g, c: (rev(c), g))
    xblk = pl.BlockSpec((cps * CHUNK, gps * XBC_BLK), lambda g, c: (rev(c), g))
    res = pl.pallas_call(
        body, name="ssd_bwd", grid=grid,
        in_specs=[blk, blk, blk, xblk, blk,
                  pl.BlockSpec((cps, gps, D_STATE, GROUP_W), lambda g, c: (rev(c), g, 0, 0)),
                  vec, vec, vec,
                  pl.BlockSpec((CHUNK, CHUNK), lambda g, c: (0, 0))] + comm.in_specs,
        out_specs=[blk, xblk, blk, vec, vec, vec] + comm.out_specs,
        out_shape=[
            jax.ShapeDtypeStruct((lp, NP), BF16),
            jax.ShapeDtypeStruct((lp, D_CONV), F32),
            jax.ShapeDtypeStruct((lp, D_SSD), F32),
            jax.ShapeDtypeStruct((1, D_SSD), F32),
            jax.ShapeDtypeStruct((1, D_SSD), F32),
            jax.ShapeDtypeStruct((1, D_SSD), F32),
        ] + comm.out_shape,
        scratch_shapes=[pltpu.VMEM((gps, D_STATE, GROUP_W), F32)] + comm.scratch,
        compiler_params=_params(("arbitrary", "arbitrary"), side_effects=True),
    )(dmix, ytot, proj, xbc, dt_rep, hprev, a_rep, dsk_rep, wn, tri, *comm.operands)
    return res


def _stack_heads(t, h):
    return jnp.concatenate([t[:, (REP * h + r) * HEAD_DIM:(REP * h + r + 1) * HEAD_DIM] for r in range(REP)], axis=0)


def _band(t2, t1, t0, h):
    sl = slice(h * HEAD_DIM, (h + 1) * HEAD_DIM)
    return jnp.concatenate([t2[:, sl], t1[:, sl], t0[:, sl]], axis=0)


def _attn_probs(s, sink, qc):
    s = s * (HEAD_DIM ** -0.5)
    key_abs = (qc - WINDOW_CHUNKS) * CHUNK + lax.broadcasted_iota(jnp.int32, s.shape, 1)
    s = jnp.where(key_abs >= PAD_LEAD, s, -jnp.inf)
    m = jnp.maximum(jnp.max(s, axis=-1, keepdims=True), sink)
    p = jnp.exp(s - m)
    ps = jnp.exp(sink - m)
    denom = jnp.sum(p, axis=-1, keepdims=True) + ps
    return p / denom, ps / denom


ATT_QC_FWD = 6
ATT_QC_BWD = 2


def _kv_specs(width, newest_chunk_of, kc):
    return [pl.BlockSpec((CHUNK, width), functools.partial(lambda j, p: (jnp.maximum(newest_chunk_of(p) - j, 0), 0), j))
            for j in range(kc - 1, -1, -1)]


def _attn_fwd(qr, kr, vb, proj, sink_stack, mix):
    lp = qr.shape[0]
    nc = lp // CHUNK
    qn = _tile(nc, (ATT_QC_FWD, 2, 1))
    kn = WINDOW_CHUNKS + qn
    qrows = qn * CHUNK

    def body(q_ref, *rest):
        k_refs, v_refs = rest[:kn], rest[kn:2 * kn]
        g_ref, sink_ref, _, att_ref, mix_ref = rest[2 * kn:]
        p = pl.program_id(0)
        q = q_ref[...]
        ks = [r[...] for r in k_refs]
        vs = [r[...] for r in v_refs]
        units = [(u, h) for u in range(qn) for h in range(KV_HEADS)]
        s = [_dot(_stack_heads(q[u * CHUNK:(u + 1) * CHUNK, :], h), _band(*ks[u:u + 3], h), NT) for u, h in units]
        vbh = [_band(*vs[u:u + 3], h) for u, h in units]
        pn = [_attn_probs(s[i], sink_ref[h], qn * p + u)[0].astype(BF16) for i, (u, h) in enumerate(units)]
        o = [_dot(pn[i], vbh[i]) for i in range(len(units))]
        att = jnp.concatenate(
            [jnp.concatenate([o[u * KV_HEADS + h][r * CHUNK:(r + 1) * CHUNK, :] for h in range(KV_HEADS) for r in range(REP)],
                             axis=1) for u in range(qn)], axis=0)
        att_ref[...] = att
        g = g_ref[...]
        mix_ref[...] = (att * (g * _sigmoid(g))).astype(BF16)

    newest = lambda p: qn * p + qn - 1
    return pl.pallas_call(
        body, name="attn_fwd", grid=(nc // qn,),
        in_specs=[pl.BlockSpec((qrows, D_ATT), lambda p: (p, 0))] + _kv_specs(D_KV, newest, kn) + _kv_specs(D_KV, newest, kn) + [
            pl.BlockSpec((qrows, D_ATT), lambda p: (p, OFF_G // D_ATT)),
            pl.BlockSpec((KV_HEADS, REP * CHUNK, 1), lambda p: (0, 0, 0)),
            ANY,
        ],
        out_specs=[pl.BlockSpec((qrows, D_ATT), lambda p: (p, 0)),
                   pl.BlockSpec((qrows, D_ATT), lambda p: (p, D_SSD // D_ATT))],
        out_shape=[jax.ShapeDtypeStruct((lp, D_ATT), F32), jax.ShapeDtypeStruct((lp, D_MIX), BF16)],
        input_output_aliases={2 * kn + 3: 1},
        compiler_params=_params(("parallel",)),
    )(qr, *([kr] * kn), *([vb] * kn), proj, sink_stack, mix)


def _attn_bwd(qr, kr, vb, att, proj, dmix, sink_stack):
    lp = qr.shape[0]
    nc = lp // CHUNK
    qn = ATT_QC_BWD
    kn = WINDOW_CHUNKS + qn
    assert nc % qn == 0 and WINDOW_CHUNKS % qn == 0
    steps = nc // qn
    qrows = qn * CHUNK
    wrows = kn * CHUNK

    def body(q_ref, *rest):
        k_refs, v_refs = rest[:kn], rest[kn:2 * kn]
        (att_ref, g_ref, do_ref, sink_ref, dq_ref, dk_ref, dv_ref, dg_ref, dsink_ref, dk_acc, dv_acc) = rest[2 * kn:]
        step = pl.program_id(0)

        @pl.when(step == 0)
        def _():
            dk_acc[...] = jnp.zeros_like(dk_acc)
            dv_acc[...] = jnp.zeros_like(dv_acc)
            dsink_ref[...] = jnp.zeros_like(dsink_ref)

        q = q_ref[...]
        ks = [r[...] for r in k_refs]
        vs = [r[...] for r in v_refs]
        att = att_ref[...]
        g = g_ref[...]
        dog = do_ref[...]
        sg = _sigmoid(g)
        dg_ref[...] = dog * att * (sg * (1.0 + g * (1.0 - sg)))
        do = dog * (g * sg)
        units = [(u, h) for u in range(qn) for h in range(KV_HEADS)]
        n = range(len(units))
        rows = [slice(u * CHUNK, (u + 1) * CHUNK) for u in range(qn)]
        qs = [_stack_heads(q[rows[u], :], h) for u, h in units]
        kb = [_band(*ks[u:u + 3], h) for u, h in units]
        vbh = [_band(*vs[u:u + 3], h) for u, h in units]
        dos = [_stack_heads(do[rows[u], :], h) for u, h in units]
        dosb = [dos[i].astype(BF16) for i in n]
        s = [_dot(qs[i], kb[i], NT) for i in n]
        dp = [_dot(dosb[i], vbh[i], NT) for i in n]
        ds, pnb, dsink = [], [], []
        for i, (u, h) in enumerate(units):
            pn, psink = _attn_probs(s[i], sink_ref[h], qn * jnp.minimum(step, steps - 1) + u)
            delta = jnp.sum(dos[i] * _stack_heads(att[rows[u], :], h), axis=-1, keepdims=True)
            ds.append((pn * (dp[i] - delta)).astype(BF16))
            pnb.append(pn.astype(BF16))
            dsink.append(-(psink * delta))
        dqs = [_dot(ds[i], kb[i]) for i in n]
        dks = [_dot(ds[i], qs[i], TN) for i in n]
        dvs = [_dot(pnb[i], dosb[i], TN) for i in n]
        dq_ref[...] = jnp.concatenate(
            [jnp.concatenate([dqs[u * KV_HEADS + h][r * CHUNK:(r + 1) * CHUNK, :]
                              for h in range(KV_HEADS) for r in range(REP)], axis=1) for u in range(qn)],
            axis=0) * (HEAD_DIM ** -0.5)
        @pl.when(step < steps)
        def _():
            for i, (u, h) in enumerate(units):
                dsink_ref[h] += dsink[i]
            for u in range(qn):
                band = slice(u * CHUNK, u * CHUNK + BAND)
                dk_acc[band, :] += jnp.concatenate(dks[u * KV_HEADS:(u + 1) * KV_HEADS], axis=1) * (HEAD_DIM ** -0.5)
                dv_acc[band, :] += jnp.concatenate(dvs[u * KV_HEADS:(u + 1) * KV_HEADS], axis=1)

        dk_ref[...] = dk_acc[0:qrows, :]
        dv_ref[...] = dv_acc[0:qrows, :]
        for acc in (dk_acc, dv_acc):
            rest_rows = acc[qrows:wrows, :]
            acc[0:wrows - qrows, :] = rest_rows
            acc[wrows - qrows:wrows, :] = jnp.zeros((qrows, D_KV), F32)

    qp = lambda p: jnp.minimum(p, steps - 1)
    newest = lambda p: qn * qp(p) + qn - 1
    qblk = pl.BlockSpec((qrows, D_ATT), lambda p: (qp(p), 0))
    oldest = pl.BlockSpec((qrows, D_KV), lambda p: (jnp.maximum(p - 1, 0), 0))
    return pl.pallas_call(
        body, name="attn_bwd", grid=(steps + 1,),
        in_specs=[qblk] + _kv_specs(D_KV, newest, kn) + _kv_specs(D_KV, newest, kn) + [
            qblk,
            pl.BlockSpec((qrows, D_ATT), lambda p: (qp(p), OFF_G // D_ATT)),
            pl.BlockSpec((qrows, D_ATT), lambda p: (qp(p), D_SSD // D_ATT)),
            pl.BlockSpec((KV_HEADS, REP * CHUNK, 1), lambda p: (0, 0, 0)),
        ],
        out_specs=[qblk, oldest, oldest, qblk, pl.BlockSpec((KV_HEADS, REP * CHUNK, 1), lambda p: (0, 0, 0))],
        out_shape=[
            jax.ShapeDtypeStruct((lp, D_ATT), F32),
            jax.ShapeDtypeStruct((lp, D_KV), F32),
            jax.ShapeDtypeStruct((lp, D_KV), F32),
            jax.ShapeDtypeStruct((lp, D_ATT), F32),
            jax.ShapeDtypeStruct((KV_HEADS, REP * CHUNK, 1), F32),
        ],
        scratch_shapes=[pltpu.VMEM((wrows, D_KV), F32), pltpu.VMEM((wrows, D_KV), F32)],
        compiler_params=_params(("arbitrary",)),
    )(qr, *([kr] * kn), *([vb] * kn), att, proj, dmix, sink_stack)


def _post_loss(out, x2d, target, w):
    lp = out.shape[0]
    nc = lp // CHUNK
    nx = x2d.shape[0] // CHUNK

    k = _tile(nc, (ROW_K, 3, 2, 1))

    def body(o_ref, *rest):
        x_refs, t_refs = rest[:k], rest[k:2 * k]
        w_ref, do_ref, dy_ref, gw_ref, loss_ref = rest[2 * k:]
        i = pl.program_id(0)

        @pl.when(i == 0)
        def _():
            gw_ref[...] = jnp.zeros_like(gw_ref)
            loss_ref[...] = jnp.zeros_like(loss_ref)

        o = o_ref[...]
        w = w_ref[...]
        rstd = lax.rsqrt(jnp.mean(o * o, axis=-1, keepdims=True) + EPS)
        xhat = o * rstd
        x = jnp.concatenate([r[...] for r in x_refs], axis=0)
        t = jnp.concatenate([r[...] for r in t_refs], axis=0)
        chunk = _chunk_of_row(i, k)
        err = (x + xhat * w - t) * ((chunk > 0) & (chunk <= nx)).astype(F32)
        loss_ref[...] += 0.5 * jnp.sum(jnp.mean(err * err, axis=-1, keepdims=True), axis=0, keepdims=True)
        dy = err * (1.0 / D_MODEL)
        dy_ref[...] = dy
        gw_ref[...] += jnp.sum(dy * xhat, axis=0, keepdims=True)
        dxhat = dy * w
        do_ref[...] = (rstd * (dxhat - xhat * jnp.mean(dxhat * xhat, axis=-1, keepdims=True))).astype(BF16)

    row = pl.BlockSpec((k * CHUNK, D_MODEL), lambda i: (i, 0))
    return pl.pallas_call(
        body, name="post_loss", grid=(nc // k,),
        in_specs=[row] + _x_specs(nx, k) + _x_specs(nx, k) + [pl.BlockSpec((1, D_MODEL), lambda i: (0, 0))],
        out_specs=[row, row, pl.BlockSpec((1, D_MODEL), lambda i: (0, 0)), pl.BlockSpec((1, 128), lambda i: (0, 0))],
        out_shape=[
            jax.ShapeDtypeStruct((lp, D_MODEL), BF16),
            jax.ShapeDtypeStruct((lp, D_MODEL), F32),
            jax.ShapeDtypeStruct((1, D_MODEL), F32),
            jax.ShapeDtypeStruct((1, 128), F32),
        ],
        compiler_params=_params(("arbitrary",)),
    )(out, *([x2d] * k), *([target] * k), w)


def _prenorm_bwd(dhn, x2d, h0, dy, w):
    nx = x2d.shape[0] // CHUNK
    k = _tile(nx, (X_K, 4, 2, 1))
    rows = k * CHUNK

    def backward(h, dhn, w):
        rstd = lax.rsqrt(jnp.mean(h * h, axis=-1, keepdims=True) + EPS)
        xhat = h * rstd
        dxhat = dhn * w
        dh = rstd * (dxhat - xhat * jnp.mean(dxhat * xhat, axis=-1, keepdims=True))
        return dh, jnp.sum(dhn * xhat, axis=0, keepdims=True)

    def body(*refs):
        dhn_refs, dy_refs = refs[:k], refs[k:2 * k]
        x_ref, w_ref, gx_ref, gw_ref = refs[2 * k:]

        @pl.when(pl.program_id(0) == 0)
        def _():
            gw_ref[...] = jnp.zeros_like(gw_ref)

        dh, gw = backward(x_ref[...], jnp.concatenate([r[...] for r in dhn_refs], axis=0), w_ref[...])
        gx_ref[...] = dh + jnp.concatenate([r[...] for r in dy_refs], axis=0)
        gw_ref[...] += gw

    def body_meta(dhn_ref, h0_ref, w_ref, d0_ref, gw_ref):
        d0_ref[...], gw_ref[...] = backward(h0_ref[...], dhn_ref[...], w_ref[...])

    chunk_specs = [pl.BlockSpec((CHUNK, D_MODEL), functools.partial(lambda u, i: (k * i + 1 + u, 0), u)) for u in range(k)]
    first = pl.BlockSpec((CHUNK, D_MODEL), lambda i: (0, 0))
    vec = pl.BlockSpec((1, D_MODEL), lambda i: (0, 0))
    wide = pl.BlockSpec((rows, D_MODEL), lambda i: (i, 0))
    gx, gw_x = pl.pallas_call(
        body, name="prenorm_bwd", grid=(nx // k,),
        in_specs=chunk_specs + chunk_specs + [wide, vec],
        out_specs=[wide, vec],
        out_shape=[jax.ShapeDtypeStruct((nx * CHUNK, D_MODEL), F32), jax.ShapeDtypeStruct((1, D_MODEL), F32)],
        compiler_params=_params(("arbitrary",)),
    )(*([dhn] * k), *([dy] * k), x2d, w)
    d0, gw_0 = pl.pallas_call(
        body_meta, name="prenorm_bwd_meta", grid=(1,),
        in_specs=[first, first, vec], out_specs=[first, vec],
        out_shape=[jax.ShapeDtypeStruct((CHUNK, D_MODEL), F32), jax.ShapeDtypeStruct((1, D_MODEL), F32)],
        compiler_params=_params(("arbitrary",)),
    )(dhn, h0, w)
    return gx, d0, gw_x + gw_0


def _adamw(slabs, w, m, v, name):
    rows, cols = w.shape
    tr = _tile(rows, (256, 128, 64, 16, 8))
    c1 = 1.0 - ADAM_B1 ** ADAM_STEP
    c2 = 1.0 - ADAM_B2 ** ADAM_STEP

    def body(s_ref, w_ref, m_ref, v_ref, g_ref, d_ref, mo_ref, vo_ref):
        g = s_ref[0].astype(F32)
        for k in range(1, slabs.shape[0]):
            g = g + s_ref[k].astype(F32)
        w = w_ref[...]
        m = ADAM_B1 * m_ref[...] + (1.0 - ADAM_B1) * g
        v = ADAM_B2 * v_ref[...] + (1.0 - ADAM_B2) * (g * g)
        g_ref[...] = g
        mo_ref[...] = m
        vo_ref[...] = v
        d_ref[...] = -ADAM_LR * ((m / c1) / (jnp.sqrt(v / c2) + ADAM_EPS) + ADAM_WD * w)

    blk = pl.BlockSpec((tr, cols), lambda i: (i, 0))
    shape = jax.ShapeDtypeStruct((rows, cols), F32)
    return pl.pallas_call(
        body, name=name, grid=(rows // tr,),
        in_specs=[pl.BlockSpec((slabs.shape[0], tr, cols), lambda i: (0, i, 0)), blk, blk, blk],
        out_specs=[blk, blk, blk, blk],
        out_shape=[shape, shape, shape, shape],
        compiler_params=_params(("parallel",)),
    )(slabs, w, m, v)


def _perm_xbc(a):
    lead = a.shape[:-1]
    xs = a[..., :D_SSD].reshape(lead + (N_GROUPS, GROUP_W))
    b = a[..., D_SSD:D_SSD + N_GROUPS * D_STATE].reshape(lead + (N_GROUPS, D_STATE))
    c = a[..., D_SSD + N_GROUPS * D_STATE:].reshape(lead + (N_GROUPS, D_STATE))
    return jnp.concatenate([xs, b, c], axis=-1).reshape(lead + (D_CONV,))


def _unperm_xbc(a):
    lead = a.shape[:-1]
    t = a.reshape(lead + (N_GROUPS, XBC_BLK))
    xs = t[..., :GROUP_W].reshape(lead + (D_SSD,))
    b = t[..., GROUP_W:GROUP_W + D_STATE].reshape(lead + (N_GROUPS * D_STATE,))
    c = t[..., GROUP_W + D_STATE:].reshape(lead + (N_GROUPS * D_STATE,))
    return jnp.concatenate([xs, b, c], axis=-1)


R_Z, R_XBC, R_DT, R_Q, R_K, R_V, R_G = 0, 2048, 6144, 6176, 7200, 7456, 7712


def _internal_of_reference():
    ref = np.arange(D_IN_PROJ)
    out = np.empty(D_IN_PROJ, np.int64)
    out[R_Z:R_XBC] = OFF_Z + ref[:D_SSD]
    xs = np.arange(D_SSD)
    out[R_XBC:R_XBC + D_SSD] = OFF_XBC + (xs // GROUP_W) * XBC_BLK + xs % GROUP_W
    bc = np.arange(N_GROUPS * D_STATE)
    out[R_XBC + D_SSD:R_XBC + D_SSD + N_GROUPS * D_STATE] = OFF_XBC + (bc // D_STATE) * XBC_BLK + GROUP_W + bc % D_STATE
    out[R_XBC + D_SSD + N_GROUPS * D_STATE:R_DT] = OFF_XBC + (bc // D_STATE) * XBC_BLK + GROUP_W + D_STATE + bc % D_STATE
    out[R_DT:R_Q] = OFF_DT + np.arange(SSD_HEADS)
    out[R_Q:R_K] = OFF_Q + np.arange(D_ATT)
    out[R_K:R_V] = OFF_K + np.arange(D_KV)
    out[R_V:R_G] = OFF_V + np.arange(D_KV)
    out[R_G:] = OFF_G + np.arange(D_ATT)
    return out


def _runs(src, dst_break):
    runs, lo = [], 0
    for i in range(1, len(src) + 1):
        if i == len(src) or src[i] != src[i - 1] + 1 or dst_break[i] != dst_break[i - 1]:
            runs.append((lo, i))
            lo = i
    return runs


RELAYOUT_ROWS = 256


def _lane_window(ref, lead, c0, n):
    a0 = c0 // 128 * 128
    a1 = min(-(-(c0 + n) // 128) * 128, ref.shape[-1])
    return ref[lead + (slice(None), slice(a0, a1))][:, c0 - a0:c0 - a0 + n]


def _w_in_internal(w_gathered):
    int_of_ref = _internal_of_reference()
    ref_of_int = np.full(NP, -1, np.int64)
    ref_of_int[int_of_ref] = np.arange(D_IN_PROJ)
    shard = np.where(ref_of_int >= 0, ref_of_int // SHARD_IN, -1)
    src = np.where(ref_of_int >= 0, ref_of_int, -10 - 2 * np.arange(NP))
    plan, zeros = [], 0
    for lo, hi in _runs(src, shard):
        if ref_of_int[lo] < 0:
            zeros += hi - lo
            continue
        if zeros:
            plan.append((None, 0, zeros))
            zeros = 0
        plan.append((int(ref_of_int[lo] // SHARD_IN), int(ref_of_int[lo] % SHARD_IN), hi - lo))
    if zeros:
        plan.append((None, 0, zeros))
    tr = RELAYOUT_ROWS

    def body(w_ref, o_ref):
        o_ref[...] = jnp.concatenate(
            [jnp.zeros((tr, n), o_ref.dtype) if s is None else _lane_window(w_ref, (s,), c0, n) for s, c0, n in plan], axis=1)

    return pl.pallas_call(
        body, name="w_in_relayout", grid=(D_MODEL // tr,),
        in_specs=[pl.BlockSpec((N_DEV, tr, SHARD_IN), lambda i: (0, i, 0))],
        out_specs=pl.BlockSpec((tr, NP), lambda i: (i, 0)),
        out_shape=jax.ShapeDtypeStruct((D_MODEL, NP), w_gathered.dtype),
        compiler_params=_params(("parallel",)),
    )(w_gathered)


def _w_in_slabs(dw):
    int_of_ref = _internal_of_reference()
    plan = []
    for s in range(N_DEV):
        cols = int_of_ref[s * SHARD_IN:(s + 1) * SHARD_IN]
        plan.append([(int(cols[lo]), hi - lo) for lo, hi in _runs(cols, np.zeros_like(cols))])
    tr = RELAYOUT_ROWS

    def body(dw_ref, o_ref):
        for s in range(N_DEV):
            o_ref[s] = jnp.concatenate([_lane_window(dw_ref, (), c0, n) for c0, n in plan[s]], axis=1)

    return pl.pallas_call(
        body, name="dw_in_relayout", grid=(D_MODEL // tr,),
        in_specs=[pl.BlockSpec((tr, NP), lambda i: (i, 0))],
        out_specs=pl.BlockSpec((N_DEV, tr, SHARD_IN), lambda i: (0, i, 0)),
        out_shape=jax.ShapeDtypeStruct((N_DEV, D_MODEL, SHARD_IN), dw.dtype),
        compiler_params=_params(("parallel",)),
    )(dw)


def _rep_heads(a):
    return jnp.repeat(a, HEAD_DIM, axis=1)


SMALL = (("norm_pre_w", 2048), ("conv_b", 4096), ("dt_bias", 32), ("a_log", 32), ("d_skip", 32),
         ("ssd_norm_w", 2048), ("attn_sinks", 16), ("norm_post_w", 2048))
SMALL_USED = sum(size for _, size in SMALL)
SMALL_LEN = 10368


def _pack_small(d, loss=None):
    parts = [d[name].reshape(1, size) for name, size in SMALL]
    tail = jnp.zeros((1, SMALL_LEN - SMALL_USED), F32)
    if loss is not None:
        tail = tail.at[0, 0].set(loss)
    return jnp.concatenate(parts + [tail], axis=1)


def _unpack_small(vec):
    out, off = {}, 0
    for name, size in SMALL:
        out[name] = vec[:, off:off + size]
        off += size
    return out


def kernel(x, meta_tokens, norm_pre_w, w_in, conv_w, conv_b, dt_bias, a_log, d_skip, ssd_norm_w, attn_sinks, w_out, norm_post_w, loss_target, m_meta_tokens, m_norm_pre_w, m_w_in, m_conv_w, m_conv_b, m_dt_bias, m_a_log, m_d_skip, m_ssd_norm_w, m_attn_sinks, m_w_out, m_norm_post_w, v_meta_tokens, v_norm_pre_w, v_w_in, v_conv_w, v_conv_b, v_dt_bias, v_a_log, v_d_skip, v_ssd_norm_w, v_attn_sinks, v_w_out, v_norm_post_w):
    seq = x.shape[1]
    lp = seq + 2 * CHUNK
    x2d = x[0]

    w_in_g, conv_w_g, meta_g = _gather_two_level([w_in[0].astype(BF16), conv_w[0], meta_tokens], "gather_w_in")
    w_all = _w_in_internal(w_in_g)
    conv_w_full = _perm_xbc(jnp.transpose(conv_w_g, (1, 0, 2)).reshape(CONV_WIDTH, D_CONV))
    conv_b_int = _perm_xbc(conv_b)
    meta_full = jnp.transpose(meta_g, (1, 0, 2)).reshape(N_META, D_MODEL)
    h0 = jnp.concatenate([jnp.zeros((PAD_LEAD, D_MODEL), F32), meta_full], axis=0)

    pos = (jnp.arange(lp) - PAD_LEAD).astype(F32)
    half = HEAD_DIM // 2
    inv = ROPE_THETA ** (-jnp.arange(half, dtype=F32) / half)
    ang = pos[:, None] * inv[None, :]
    cos_t = jnp.tile(jnp.cos(ang), (1, 4))
    sin_t = jnp.tile(jnp.concatenate([-jnp.sin(ang), jnp.sin(ang)], axis=1), (1, 2))
    head_of_col = np.arange(D_SSD) // HEAD_DIM
    expand = jnp.asarray((np.arange(128)[:, None] == head_of_col[None, :]).astype(np.float32))
    reduce_t = jnp.asarray((head_of_col[:, None] == np.arange(128)[None, :]).astype(np.float32))
    tri = jnp.asarray(np.tril(np.ones((CHUNK, CHUNK), np.float32)))
    a_rep = _rep_heads(-jnp.exp(a_log))
    dsk_rep = _rep_heads(d_skip)
    dt_bias_rep = _rep_heads(dt_bias)
    sink_stack = jnp.repeat(attn_sinks.reshape(KV_HEADS, REP), CHUNK, axis=1).reshape(KV_HEADS, REP * CHUNK, 1)

    hn = _prenorm(x2d, h0, norm_pre_w)
    tm = _tile(lp, (1056, 704, 128, 64))
    proj, w_out_g = _matmul(hn, w_all, tm=tm, tn=1536, tk=D_MODEL, out_dtype=F32, name="in_proj",
                            comm=_Comm([(w_out[0].astype(BF16), "gather")]))
    w_out_full = w_out_g.reshape(D_MIX, D_MODEL)
    xbc = _conv_fwd(proj, conv_w_full, conv_b_int)
    qr, kr, vb, dt_rep = _act_fwd(proj, cos_t, sin_t, expand, dt_bias_rep)
    mix, ytot, hprev = _ssd_fwd(xbc, dt_rep, proj, a_rep, dsk_rep, ssd_norm_w, tri)
    att, mix = _attn_fwd(qr, kr, vb, proj, sink_stack, mix)
    out = _matmul(mix, w_out_full, tm=tm, tn=1024, tk=D_MIX, out_dtype=F32, name="out_proj")
    dout, dy, g_norm_post, loss_part = _post_loss(out, x2d, loss_target[0], norm_post_w)

    dmix = _matmul(dout, w_out_full, trans_b=True, tm=tm, tn=1024, tk=D_MODEL, out_dtype=F32, name="dmix")
    dw_out = _matmul(mix, dout, trans_a=True, tm=512, tn=1024, tk=lp, out_dtype=BF16, name="dw_out")
    dqr, dkr, dv, dg, dsink_rows = _attn_bwd(qr, kr, vb, att, proj, dmix, sink_stack)
    dproj, dxbc, ddt_part, dd_part, da_part, g_ssd_norm, g_out = _ssd_bwd(
        dmix, ytot, proj, xbc, dt_rep, hprev, a_rep, dsk_rep, ssd_norm_w, tri,
        _Comm([(dw_out.reshape(N_DEV, D_MIX // N_DEV, D_MODEL), "scatter")]))
    dproj, dconv_w_int, dconv_b_int = _conv_bwd(dxbc, proj, conv_w_full, conv_b_int, dproj)
    dproj, ddt_bias = _act_bwd(dqr, dkr, dv, dg, ddt_part, cos_t, sin_t, reduce_t, dproj)
    dw_all = _matmul(hn, dproj, trans_a=True, tm=512, tn=1024, tk=lp, out_dtype=BF16, name="dw_in")
    dw_slabs = _w_in_slabs(dw_all)
    dw_chip = _pair_sum(dw_slabs, _exchange_sibling(dw_slabs, "dw_in_sibling"), "dw_in_pair_sum")
    dhn, g_in = _matmul(dproj, w_all, trans_b=True, tm=tm, tn=1024, tk=1536, out_dtype=F32, name="dhn",
                        comm=_Comm([(dw_chip, "scatter")], scope="chips"))
    grad_x, dh0, g_norm_pre = _prenorm_bwd(dhn, x2d, h0, dy, norm_pre_w)

    dmeta = dh0[PAD_LEAD:, :]
    dconv_w_ref = _unperm_xbc(dconv_w_int)
    heads = lambda part: part.reshape(SSD_HEADS, HEAD_DIM).sum(axis=1).reshape(1, SSD_HEADS)
    small_local = _pack_small({
        "norm_pre_w": g_norm_pre, "conv_b": _unperm_xbc(dconv_b_int), "dt_bias": ddt_bias[:, :SSD_HEADS],
        "a_log": heads(da_part) * (-jnp.exp(a_log)), "d_skip": heads(dd_part), "ssd_norm_w": g_ssd_norm,
        "attn_sinks": dsink_rows.reshape(Q_HEADS, CHUNK).sum(axis=1).reshape(1, Q_HEADS),
        "norm_post_w": g_norm_post}, loss=loss_part[0, 0])
    g_conv, g_meta, g_small = _exchange(
        [(jnp.transpose(dconv_w_ref.reshape(CONV_WIDTH, N_DEV, D_CONV // N_DEV), (1, 0, 2)), "scatter"),
         (jnp.transpose(dmeta.reshape(N_META, N_DEV, D_MODEL // N_DEV), (1, 0, 2)), "scatter"),
         (small_local, "gather")], "exchange_small")

    res = {}
    res["w_in"] = [o[None] for o in _adamw(g_in, w_in[0], m_w_in[0], v_w_in[0], "adamw_w_in")]
    res["w_out"] = [o[None] for o in _adamw(g_out, w_out[0], m_w_out[0], v_w_out[0], "adamw_w_out")]
    res["conv_w"] = [o[None] for o in _adamw(g_conv, conv_w[0], m_conv_w[0], v_conv_w[0], "adamw_conv_w")]
    res["meta_tokens"] = _adamw(g_meta, meta_tokens, m_meta_tokens, v_meta_tokens, "adamw_meta")
    given = dict(norm_pre_w=(norm_pre_w, m_norm_pre_w, v_norm_pre_w), conv_b=(conv_b, m_conv_b, v_conv_b),
                 dt_bias=(dt_bias, m_dt_bias, v_dt_bias), a_log=(a_log, m_a_log, v_a_log),
                 d_skip=(d_skip, m_d_skip, v_d_skip), ssd_norm_w=(ssd_norm_w, m_ssd_norm_w, v_ssd_norm_w),
                 attn_sinks=(attn_sinks, m_attn_sinks, v_attn_sinks),
                 norm_post_w=(norm_post_w, m_norm_post_w, v_norm_post_w))
    packed = [_pack_small({k: t[j] for k, t in given.items()}) for j in range(3)]
    small_out = _adamw(g_small, packed[0], packed[1], packed[2], "adamw_small")
    small_res = [_unpack_small(r) for r in small_out]
    loss = small_out[0][0, SMALL_USED]

    order = ["meta_tokens", "norm_pre_w", "w_in", "conv_w", "conv_b", "dt_bias", "a_log", "d_skip", "ssd_norm_w",
             "attn_sinks", "w_out", "norm_post_w"]
    outs = []
    for j in range(4):
        for name in order:
            outs.append(res[name][j] if name in res else small_res[j][name])
    return (loss, grad_x[None], *outs)
```

```python
import functools

import numpy as np
import jax
import jax.numpy as jnp
from jax import lax
from jax.experimental import pallas as pl
from jax.experimental.pallas import tpu as pltpu

F32 = jnp.float32
BF16 = jnp.bfloat16
HIGHEST = lax.Precision.HIGHEST

N_DEV = 8
D_MODEL = 2048
CHUNK = 64
N_META = 16
PAD_LEAD = CHUNK - N_META
EPS = 1e-6
N_GROUPS = 8
HEADS_PER_GROUP = 4
HEAD_DIM = 64
GROUP_W = HEADS_PER_GROUP * HEAD_DIM
D_STATE = 128
D_SSD = 2048
D_CONV = 4096
SSD_HEADS = 32
CONV_WIDTH = 4
Q_HEADS = 16
KV_HEADS = 4
REP = 4
D_ATT = 1024
D_KV = 256
WINDOW_CHUNKS = 2
BAND = (WINDOW_CHUNKS + 1) * CHUNK
ROPE_THETA = 10000.0
D_MIX = D_SSD + D_ATT
D_IN_PROJ = 8736
SHARD_IN = D_IN_PROJ // N_DEV

OFF_Z, OFF_XBC, OFF_Q, OFF_G, OFF_K, OFF_V, OFF_DT = 0, 2048, 6144, 7168, 8192, 8448, 8704
NP = 9216
TAIL_W = NP - OFF_Q
XBC_BLK = 512
SSD_GPS = 8
SSD_CPS = 2

ADAM_LR, ADAM_B1, ADAM_B2, ADAM_EPS, ADAM_WD, ADAM_STEP = 0.001, 0.9, 0.999, 1e-08, 0.01, 10

VMEM_LIMIT = 48 * 1024 * 1024

NN = (((1,), (0,)), ((), ()))
NT = (((1,), (1,)), ((), ()))
TN = (((0,), (0,)), ((), ()))
ANY = pl.BlockSpec(memory_space=pl.ANY)


def _dot(a, b, dims=NN, precision=None):
    return lax.dot_general(a, b, dims, precision=precision, preferred_element_type=F32)


def _tile(n, prefs):
    for t in prefs:
        if n % t == 0:
            return t
    return n


def _params(sem, vmem=VMEM_LIMIT, side_effects=False):
    return pltpu.CompilerParams(dimension_semantics=sem, vmem_limit_bytes=vmem, has_side_effects=side_effects)


def _sigmoid(x):
    return 1.0 / (1.0 + jnp.exp(-x))


class _Comm:
    def __init__(self, items, scope="devices"):
        self.items = items
        self.scope = scope
        self.slabs = slabs = N_DEV if scope == "devices" else N_DEV // 2
        self.n = n = len(items)
        self.operands = [arr for arr, _ in items]
        self.in_specs = [ANY] * n
        self.out_specs = [ANY] * n
        self.out_shape = [jax.ShapeDtypeStruct((slabs,) + tuple(arr.shape) if kind == "gather" else tuple(arr.shape),
                                               arr.dtype) for arr, kind in items]
        self.scratch = [pltpu.SemaphoreType.DMA((n, slabs - 1)), pltpu.SemaphoreType.DMA((n, slabs - 1)),
                        pltpu.SemaphoreType.DMA((n,))]

    def _places(self):
        pos = (lax.axis_index("x"), lax.axis_index("y"), lax.axis_index("c"))
        if self.scope == "devices":
            index = lambda p: 4 * p[0] + 2 * p[1] + p[2]
            masks = range(1, N_DEV)
        else:
            index = lambda p: 2 * p[0] + p[1]
            masks = (2, 4, 6)
        peers = []
        for k in masks:
            p = tuple(1 - pos[b] if (k >> (2 - b)) & 1 else pos[b] for b in range(3))
            peers.append((p, index(p)))
        return index(pos), peers

    def _copies(self, ins, outs, sems, landed):
        send_sems, recv_sems, local_sems = sems
        me, peers = self._places()
        local, remote = [], []
        for a, (_, kind) in enumerate(self.items):
            own = ins[a] if kind == "gather" else ins[a].at[me]
            local.append(pltpu.make_async_copy(own, outs[a].at[me], local_sems.at[a]))
            for k, (p, pid) in enumerate(peers):
                remote.append(pltpu.make_async_remote_copy(
                    src_ref=ins[a] if kind == "gather" else ins[a].at[pid],
                    dst_ref=outs[a].at[pid if landed else me],
                    send_sem=send_sems.at[a, k], recv_sem=recv_sems.at[a, k],
                    device_id=p, device_id_type=pl.DeviceIdType.MESH))
        return local, remote

    def start(self, ins, outs, sems):
        local, remote = self._copies(ins, outs, sems, landed=False)
        for cp in local + remote:
            cp.start()

    def wait(self, ins, outs, sems):
        local, remote = self._copies(ins, outs, sems, landed=True)
        for cp in remote + local:
            cp.wait()


def _exchange(items, name):
    comm = _Comm(items)
    n = comm.n

    def body(*refs):
        ins, outs, sems = refs[:n], refs[n:2 * n], refs[2 * n:]
        comm.start(ins, outs, sems)
        comm.wait(ins, outs, sems)

    return pl.pallas_call(
        body, name=name, in_specs=comm.in_specs, out_specs=comm.out_specs, out_shape=comm.out_shape,
        scratch_shapes=comm.scratch, compiler_params=pltpu.CompilerParams(has_side_effects=True),
    )(*comm.operands)


def _exchange_sibling(slabs, name):
    def body(g_ref, land_ref, send_sems, recv_sems):
        x, y, c = lax.axis_index("x"), lax.axis_index("y"), lax.axis_index("c")
        copies = [pltpu.make_async_remote_copy(
            src_ref=g_ref.at[2 * q + (1 - c)], dst_ref=land_ref.at[q], send_sem=send_sems.at[q], recv_sem=recv_sems.at[q],
            device_id=(x, y, 1 - c), device_id_type=pl.DeviceIdType.MESH) for q in range(N_DEV // 2)]
        for cp in copies:
            cp.start()
        for cp in copies:
            cp.wait()

    return pl.pallas_call(
        body, name=name, in_specs=[ANY], out_specs=ANY,
        out_shape=jax.ShapeDtypeStruct((N_DEV // 2,) + tuple(slabs.shape[1:]), slabs.dtype),
        scratch_shapes=[pltpu.SemaphoreType.DMA((N_DEV // 2,)), pltpu.SemaphoreType.DMA((N_DEV // 2,))],
        compiler_params=pltpu.CompilerParams(has_side_effects=True),
    )(slabs)


def _pair_sum(slabs, landed, name):
    _, rows, cols = slabs.shape
    tr = _tile(rows, (256, 128, 64, 16, 8))

    def body(g_ref, l_ref, o_ref):
        mine = g_ref[lax.axis_index("c")]
        o_ref[...] = (mine.astype(F32) + l_ref[...].astype(F32)).astype(o_ref.dtype)

    return pl.pallas_call(
        body, name=name, grid=(N_DEV // 2, rows // tr),
        in_specs=[pl.BlockSpec((None, 2, tr, cols), lambda q, i: (q, 0, i, 0)),
                  pl.BlockSpec((None, tr, cols), lambda q, i: (q, i, 0))],
        out_specs=pl.BlockSpec((None, tr, cols), lambda q, i: (q, i, 0)),
        out_shape=jax.ShapeDtypeStruct((N_DEV // 2, rows, cols), slabs.dtype),
        compiler_params=_params(("parallel", "parallel")),
    )(slabs.reshape(N_DEV // 2, 2, rows, cols), landed)


class _GatherTwoLevel:
    def __init__(self, arrays):
        self.arrays = arrays
        self.n = n = len(arrays)
        self.operands = list(arrays)
        self.in_specs = [ANY] * n
        self.out_specs = [ANY] * n
        self.out_shape = [jax.ShapeDtypeStruct((N_DEV,) + tuple(a.shape), a.dtype) for a in arrays]
        self.scratch = [pltpu.SemaphoreType.DMA((n, N_DEV - 1)), pltpu.SemaphoreType.DMA((n, N_DEV - 1)),
                        pltpu.SemaphoreType.DMA((n,))]

    def _plan(self, ins, outs, sems):
        send_sems, recv_sems, local_sems = sems
        x, y, c = lax.axis_index("x"), lax.axis_index("y"), lax.axis_index("c")
        me, sibling = (x, y, c), (x, y, 1 - c)
        chips = [(1 - x, y), (x, 1 - y), (1 - x, 1 - y)]

        def slab(a, place):
            return outs[a].at[4 * place[0] + 2 * place[1] + place[2]]

        def copy(a, k, block, to, src=None):
            return pltpu.make_async_remote_copy(
                src_ref=slab(a, block) if src is None else src, dst_ref=slab(a, block),
                send_sem=send_sems.at[a, k], recv_sem=recv_sems.at[a, k],
                device_id=to, device_id_type=pl.DeviceIdType.MESH)

        own, mine = [], []
        for a in range(self.n):
            mine.append(pltpu.make_async_copy(ins[a], slab(a, me), local_sems.at[a]))
            own.append(copy(a, 0, me, sibling, src=ins[a]))
            own += [copy(a, 1 + j, me, (*chip, c), src=ins[a]) for j, chip in enumerate(chips)]
        return me, sibling, chips, c, copy, own, mine

    def start(self, ins, outs, sems):
        _, _, _, _, _, own, mine = self._plan(ins, outs, sems)
        for cp in mine + own:
            cp.start()

    def wait(self, ins, outs, sems):
        me, sibling, chips, c, copy, own, mine = self._plan(ins, outs, sems)
        forwards = []
        for j, chip in enumerate(chips):
            for a in range(self.n):
                copy(a, 1 + j, (*chip, c), me).wait_recv()
                fwd = copy(a, 4 + j, (*chip, c), sibling)
                fwd.start()
                forwards.append(fwd)
        for a in range(self.n):
            copy(a, 0, sibling, me).wait_recv()
            for j, chip in enumerate(chips):
                copy(a, 4 + j, (*chip, 1 - c), me).wait_recv()
        for cp in own + forwards:
            cp.wait_send()
        for loc in mine:
            loc.wait()


def _gather_two_level(arrays, name):
    comm = _GatherTwoLevel(arrays)
    n = comm.n

    def body(*refs):
        ins, outs, sems = refs[:n], refs[n:2 * n], refs[2 * n:]
        comm.start(ins, outs, sems)
        comm.wait(ins, outs, sems)

    return pl.pallas_call(
        body, name=name, in_specs=comm.in_specs, out_specs=comm.out_specs, out_shape=comm.out_shape,
        scratch_shapes=comm.scratch, compiler_params=pltpu.CompilerParams(has_side_effects=True),
    )(*arrays)


def _matmul(a, b, *, tm, tn, tk, out_dtype, name, trans_a=False, trans_b=False, comm=None):
    m, k = (a.shape[1], a.shape[0]) if trans_a else a.shape
    n = b.shape[0] if trans_b else b.shape[1]
    nk = k // tk
    dims = TN if trans_a else (NT if trans_b else NN)
    assert not (trans_a and trans_b)
    nc = comm.n if comm else 0
    grid = (m // tm, n // tn, nk)

    def body(*refs):
        a_ref, b_ref = refs[:2]
        cin = refs[2:2 + nc]
        o_ref = refs[2 + nc]
        cout = refs[3 + nc:3 + 2 * nc]
        scratch = refs[3 + 2 * nc:]
        sems = scratch[len(scratch) - 3:] if comm else None
        i, j, kk = pl.program_id(0), pl.program_id(1), pl.program_id(2)
        if comm:
            @pl.when((i == 0) & (j == 0) & (kk == 0))
            def _():
                comm.start(cin, cout, sems)

        if nk == 1:
            o_ref[...] = _dot(a_ref[...], b_ref[...], dims).astype(out_dtype)
        else:
            acc_ref = scratch[0]

            @pl.when(kk == 0)
            def _():
                acc_ref[...] = jnp.zeros_like(acc_ref)

            acc_ref[...] += _dot(a_ref[...], b_ref[...], dims)

            @pl.when(kk == nk - 1)
            def _():
                o_ref[...] = acc_ref[...].astype(out_dtype)

        if comm:
            @pl.when((i == grid[0] - 1) & (j == grid[1] - 1) & (kk == nk - 1))
            def _():
                comm.wait(cin, cout, sems)

    a_spec = (pl.BlockSpec((tk, tm), lambda i, j, kk: (kk, i)) if trans_a
              else pl.BlockSpec((tm, tk), lambda i, j, kk: (i, kk)))
    b_spec = (pl.BlockSpec((tn, tk), lambda i, j, kk: (j, kk)) if trans_b
              else pl.BlockSpec((tk, tn), lambda i, j, kk: (kk, j)))
    sem = ("arbitrary",) * 3 if comm else ("parallel", "parallel", "arbitrary")
    res = pl.pallas_call(
        body, name=name, grid=grid,
        in_specs=[a_spec, b_spec] + (comm.in_specs if comm else []),
        out_specs=[pl.BlockSpec((tm, tn), lambda i, j, kk: (i, j))] + (comm.out_specs if comm else []),
        out_shape=[jax.ShapeDtypeStruct((m, n), out_dtype)] + (comm.out_shape if comm else []),
        scratch_shapes=([] if nk == 1 else [pltpu.VMEM((tm, tn), F32)]) + (comm.scratch if comm else []),
        compiler_params=_params(sem, side_effects=bool(comm)),
    )(a, b, *(comm.operands if comm else []))
    return res if comm else res[0]


ROW_K = 6
X_K = 8


def _x_specs(nx, k):
    return [pl.BlockSpec((CHUNK, D_MODEL), functools.partial(lambda u, i: (jnp.clip(k * i + u - 1, 0, nx - 1), 0), u))
            for u in range(k)]


def _chunk_of_row(i, k):
    return k * i + lax.broadcasted_iota(jnp.int32, (k * CHUNK, 1), 0) // CHUNK


def _prenorm(x2d, h0, w):
    nx = x2d.shape[0] // CHUNK
    nc = nx + 2
    k = _tile(nc, (ROW_K, 3, 2, 1))

    def body(*refs):
        x_refs = refs[:k]
        h0_ref, w_ref, o_ref = refs[k:]
        i = pl.program_id(0)
        x = jnp.concatenate([r[...] for r in x_refs], axis=0)
        head = jnp.concatenate([h0_ref[...], x[CHUNK:, :]], axis=0)
        chunk = _chunk_of_row(i, k)
        h = jnp.where(i == 0, head, x) * (chunk <= nx).astype(F32)
        rstd = lax.rsqrt(jnp.mean(h * h, axis=-1, keepdims=True) + EPS)
        o_ref[...] = (h * rstd * w_ref[...]).astype(BF16)

    rows = k * CHUNK
    return pl.pallas_call(
        body, name="prenorm", grid=(nc // k,),
        in_specs=_x_specs(nx, k) + [pl.BlockSpec((CHUNK, D_MODEL), lambda i: (0, 0)), pl.BlockSpec((1, D_MODEL), lambda i: (0, 0))],
        out_specs=pl.BlockSpec((rows, D_MODEL), lambda i: (i, 0)),
        out_shape=jax.ShapeDtypeStruct((nc * CHUNK, D_MODEL), BF16),
        compiler_params=_params(("parallel",)),
    )(*([x2d] * k), h0, w)


CONV_COLS = 512
HALO = 8


def _conv_pre(ext, w, b):
    taps = [ext[HALO:, :]] + [pltpu.roll(ext, j, 0)[HALO:, :] for j in range(1, CONV_WIDTH)]
    acc = b + w[3:4, :] * taps[0]
    for j in range(1, CONV_WIDTH):
        acc = acc + w[3 - j:4 - j, :] * taps[j]
    return acc, taps


def _conv_bwd(dxbc, proj, conv_w, conv_b, dproj):
    lp = proj.shape[0]
    t = _tile(lp, (704, 384, 128, 64))
    hb = t // HALO
    nt = lp // t
    c0 = OFF_XBC // CONV_COLS

    def body(dx_ref, dxn_ref, u_ref, up_ref, un_ref, w_ref, b_ref, _, du_ref, dw_ref, db_ref):
        i = pl.program_id(1)
        w = w_ref[...]
        up = up_ref[...] * (i > 0).astype(F32)
        ext = jnp.concatenate([up, u_ref[...], un_ref[...]], axis=0)
        pre, taps = _conv_pre(ext, w, b_ref[...])
        dxn = dxn_ref[...] * (i < nt - 1).astype(F32)
        dxe = jnp.concatenate([dx_ref[...], dxn], axis=0)
        sg = _sigmoid(pre)
        dpre = dxe * sg * (1.0 + pre * (1.0 - sg))
        du = w[3:4, :] * dpre[:t, :]
        for j in range(1, CONV_WIDTH):
            du = du + w[3 - j:4 - j, :] * pltpu.roll(dpre, t + HALO - j, 0)[:t, :]
        du_ref[...] = du.astype(BF16)

        @pl.when(i == 0)
        def _():
            dw_ref[...] = jnp.zeros_like(dw_ref)
            db_ref[...] = jnp.zeros_like(db_ref)

        dp = dpre[:t, :]
        db_ref[...] += jnp.sum(dp, axis=0, keepdims=True)
        for j in range(CONV_WIDTH):
            dw_ref[3 - j:4 - j, :] += jnp.sum(dp * taps[j][:t, :], axis=0, keepdims=True)

    nxt = lambda i: jnp.minimum((i + 1) * hb, lp // HALO - 1)
    return pl.pallas_call(
        body, name="conv_bwd", grid=(D_CONV // CONV_COLS, nt),
        in_specs=[
            pl.BlockSpec((t, CONV_COLS), lambda j, i: (i, j)),
            pl.BlockSpec((HALO, CONV_COLS), lambda j, i: (nxt(i), j)),
            pl.BlockSpec((t, CONV_COLS), lambda j, i: (i, c0 + j)),
            pl.BlockSpec((HALO, CONV_COLS), lambda j, i: (jnp.maximum(i * hb - 1, 0), c0 + j)),
            pl.BlockSpec((HALO, CONV_COLS), lambda j, i: (nxt(i), c0 + j)),
            pl.BlockSpec((CONV_WIDTH, CONV_COLS), lambda j, i: (0, j)),
            pl.BlockSpec((1, CONV_COLS), lambda j, i: (0, j)),
            ANY,
        ],
        out_specs=[
            pl.BlockSpec((t, CONV_COLS), lambda j, i: (i, c0 + j)),
            pl.BlockSpec((CONV_WIDTH, CONV_COLS), lambda j, i: (0, j)),
            pl.BlockSpec((1, CONV_COLS), lambda j, i: (0, j)),
        ],
        out_shape=[
            jax.ShapeDtypeStruct((lp, NP), BF16),
            jax.ShapeDtypeStruct((CONV_WIDTH, D_CONV), F32),
            jax.ShapeDtypeStruct((1, D_CONV), F32),
        ],
        input_output_aliases={7: 0},
        compiler_params=_params(("parallel", "arbitrary")),
    )(dxbc, dxbc, proj, proj, proj, conv_w, conv_b, dproj)


def _swap_halves(t):
    w = t.shape[-1]
    lane = lax.broadcasted_iota(jnp.int32, t.shape, 1)
    return jnp.where((lane % HEAD_DIM) < HEAD_DIM // 2, pltpu.roll(t, w - HEAD_DIM // 2, 1),
                     pltpu.roll(t, HEAD_DIM // 2, 1))


def _act_fwd(proj, cos_t, sin_t, expand, dt_bias_rep):
    lp = proj.shape[0]
    t = _tile(lp, (384, 128, 64))

    def body(q_ref, k_ref, v_ref, dt_ref, cos_ref, sin_ref, ex_ref, bias_ref, qo_ref, ko_ref, vo_ref, dto_ref):
        i = pl.program_id(0)
        cos = cos_ref[...]
        sin = sin_ref[...]
        q = q_ref[...]
        qo_ref[...] = (q * jnp.tile(cos, (1, D_ATT // 128)) + _swap_halves(q) * jnp.tile(sin, (1, D_ATT // 128))).astype(BF16)
        k = k_ref[...]
        ko_ref[...] = (k * jnp.tile(cos, (1, D_KV // 128)) + _swap_halves(k) * jnp.tile(sin, (1, D_KV // 128))).astype(BF16)
        vo_ref[...] = v_ref[...].astype(BF16)
        raw = _dot(dt_ref[...], ex_ref[...], NN, HIGHEST) + bias_ref[...]
        sp = jnp.maximum(raw, 0.0) + jnp.log1p(jnp.exp(-jnp.abs(raw)))
        row = i * t + lax.broadcasted_iota(jnp.int32, sp.shape, 0)
        dto_ref[...] = jnp.where(row >= PAD_LEAD, sp, 0.0)

    return pl.pallas_call(
        body, name="act_fwd", grid=(lp // t,),
        in_specs=[
            pl.BlockSpec((t, D_ATT), lambda i: (i, OFF_Q // D_ATT)),
            pl.BlockSpec((t, D_KV), lambda i: (i, OFF_K // D_KV)),
            pl.BlockSpec((t, D_KV), lambda i: (i, OFF_V // D_KV)),
            pl.BlockSpec((t, 128), lambda i: (i, OFF_DT // 128)),
            pl.BlockSpec((t, 128), lambda i: (i, 0)),
            pl.BlockSpec((t, 128), lambda i: (i, 0)),
            pl.BlockSpec((128, D_SSD), lambda i: (0, 0)),
            pl.BlockSpec((1, D_SSD), lambda i: (0, 0)),
        ],
        out_specs=[
            pl.BlockSpec((t, D_ATT), lambda i: (i, 0)),
            pl.BlockSpec((t, D_KV), lambda i: (i, 0)),
            pl.BlockSpec((t, D_KV), lambda i: (i, 0)),
            pl.BlockSpec((t, D_SSD), lambda i: (i, 0)),
        ],
        out_shape=[
            jax.ShapeDtypeStruct((lp, D_ATT), BF16),
            jax.ShapeDtypeStruct((lp, D_KV), BF16),
            jax.ShapeDtypeStruct((lp, D_KV), BF16),
            jax.ShapeDtypeStruct((lp, D_SSD), F32),
        ],
        compiler_params=_params(("parallel",)),
    )(proj, proj, proj, proj, cos_t, sin_t, expand, dt_bias_rep)


def _act_bwd(dqr, dkr, dv, dg, ddt_part, cos_t, sin_t, reduce_t, dproj):
    lp = dqr.shape[0]
    t = _tile(lp, (384, 128, 64))

    def body(dq_ref, dk_ref, dv_ref, dg_ref, ddt_ref, cos_ref, sin_ref, red_ref, _, o_ref, db_ref):
        i = pl.program_id(0)
        cos = cos_ref[...]
        sin = sin_ref[...]
        dq = dq_ref[...]
        dq = dq * jnp.tile(cos, (1, D_ATT // 128)) + _swap_halves(dq * jnp.tile(sin, (1, D_ATT // 128)))
        dk = dk_ref[...]
        dk = dk * jnp.tile(cos, (1, D_KV // 128)) + _swap_halves(dk * jnp.tile(sin, (1, D_KV // 128)))
        ddt = _dot(ddt_ref[...], red_ref[...], NN, HIGHEST)
        o_ref[...] = jnp.concatenate(
            [dq.astype(BF16), dg_ref[...].astype(BF16), dk.astype(BF16), dv_ref[...].astype(BF16), ddt.astype(BF16),
             jnp.zeros((t, NP - OFF_DT - 128), BF16)], axis=1)

        @pl.when(i == 0)
        def _():
            db_ref[...] = jnp.zeros_like(db_ref)

        db_ref[...] += jnp.sum(ddt, axis=0, keepdims=True)

    return pl.pallas_call(
        body, name="act_bwd", grid=(lp // t,),
        in_specs=[
            pl.BlockSpec((t, D_ATT), lambda i: (i, 0)),
            pl.BlockSpec((t, D_KV), lambda i: (i, 0)),
            pl.BlockSpec((t, D_KV), lambda i: (i, 0)),
            pl.BlockSpec((t, D_ATT), lambda i: (i, 0)),
            pl.BlockSpec((t, D_SSD), lambda i: (i, 0)),
            pl.BlockSpec((t, 128), lambda i: (i, 0)),
            pl.BlockSpec((t, 128), lambda i: (i, 0)),
            pl.BlockSpec((D_SSD, 128), lambda i: (0, 0)),
            ANY,
        ],
        out_specs=[pl.BlockSpec((t, TAIL_W), lambda i: (i, OFF_Q // TAIL_W)), pl.BlockSpec((1, 128), lambda i: (0, 0))],
        out_shape=[jax.ShapeDtypeStruct((lp, NP), BF16), jax.ShapeDtypeStruct((1, 128), F32)],
        input_output_aliases={8: 0},
        compiler_params=_params(("arbitrary",)),
    )(dqr, dkr, dv, dg, ddt_part, cos_t, sin_t, reduce_t, dproj)


def _cs_row(cs):
    row = lax.broadcasted_iota(jnp.int32, cs.shape, 0)
    lane = lax.broadcasted_iota(jnp.int32, cs.shape, 1)
    return jnp.sum(jnp.where(row == lane % HEAD_DIM, cs, 0.0), axis=0, keepdims=True)


def _ssd_fwd(proj, conv_w, conv_b, dt_rep, a_rep, dsk_rep, wn, tri):
    lp = proj.shape[0]
    nc = lp // CHUNK
    assert SSD_GPS == N_GROUPS
    gw = SSD_GPS * GROUP_W
    cps = _tile(nc, (SSD_CPS, 1))

    def body(rawa_ref, rawb_ref, cw_ref, cb_ref, dt_ref, z_ref, a_ref, dsk_ref, wn_ref, tri_ref,
             yn_ref, ytot_ref, hprev_ref, xbc_ref, h_scr, tail_scr):
        @pl.when(pl.program_id(1) == 0)
        def _():
            h_scr[...] = jnp.zeros_like(h_scr)
            tail_scr[...] = jnp.zeros_like(tail_scr)

        for sub in range(cps):
            rs = slice(sub * CHUNK, (sub + 1) * CHUNK)
            raw = jnp.concatenate([rawa_ref[rs, :], rawb_ref[rs, :]], axis=1)
            pre, _ = _conv_pre(jnp.concatenate([tail_scr[...], raw], axis=0), cw_ref[...], cb_ref[...])
            tail_scr[...] = raw[CHUNK - HALO:, :]
            xbc_ref[rs, :] = pre * _sigmoid(pre)

        G = range(SSD_GPS)
        colsl = [slice(gi * GROUP_W, (gi + 1) * GROUP_W) for gi in G]
        rows4 = lax.broadcasted_iota(jnp.int32, (GROUP_W, GROUP_W), 0) // HEAD_DIM
        cols4 = lax.broadcasted_iota(jnp.int32, (GROUP_W, GROUP_W), 1) // HEAD_DIM
        lrow = lax.broadcasted_iota(jnp.int32, (CHUNK, GROUP_W), 0)
        lcol = lax.broadcasted_iota(jnp.int32, (CHUNK, GROUP_W), 1) % HEAD_DIM

        def one_chunk(sub):
            rs = slice(sub * CHUNK, (sub + 1) * CHUNK)
            xbc = [xbc_ref[rs, gi * XBC_BLK:(gi + 1) * XBC_BLK] for gi in G]
            dt = [dt_ref[rs, colsl[gi]] for gi in G]
            xs = [xbc[gi][:, :GROUP_W] for gi in G]
            b = [xbc[gi][:, GROUP_W:GROUP_W + D_STATE].astype(BF16) for gi in G]
            c = [xbc[gi][:, GROUP_W + D_STATE:].astype(BF16) for gi in G]
            hprev = [h_scr[gi] for gi in G]
            cs = [_dot(tri_ref[...], dt[gi] * a_ref[:, colsl[gi]], NN, HIGHEST) for gi in G]
            yoff = [_dot(c[gi], hprev[gi].astype(BF16)) for gi in G]
            cs_t = [_cs_row(cs[gi]) for gi in G]
            xdt = [xs[gi] * dt[gi] for gi in G]
            cs_last = [cs[gi][CHUNK - 1:CHUNK, :] for gi in G]
            st = [_dot(b[gi], (xdt[gi] * jnp.exp(cs_last[gi] - cs[gi])).astype(BF16), TN) for gi in G]
            cb4 = [_dot(c[gi], jnp.concatenate([b[gi]] * HEADS_PER_GROUP, axis=0), NT) for gi in G]
            m = [(cb4[gi] * jnp.exp(jnp.where(lrow >= lcol, cs[gi] - cs_t[gi], -jnp.inf))).astype(BF16) for gi in G]
            xbd = [jnp.where(rows4 == cols4, jnp.concatenate([xdt[gi].astype(BF16)] * HEADS_PER_GROUP, axis=0), 0.0)
                   for gi in G]
            ydiag = [_dot(m[gi], xbd[gi]) for gi in G]
            for gi in G:
                cols = colsl[gi]
                ytot = ydiag[gi] + yoff[gi] * jnp.exp(cs[gi]) + dsk_ref[:, cols] * xs[gi]
                z = z_ref[rs, cols]
                gz = ytot * (z * _sigmoid(z))
                rstd = lax.rsqrt(jnp.mean(gz * gz, axis=-1, keepdims=True) + EPS)
                hprev_ref[sub, gi] = hprev[gi]
                h_scr[gi] = hprev[gi] * jnp.exp(cs_last[gi]) + st[gi]
                ytot_ref[rs, cols] = ytot
                yn_ref[rs, cols] = (gz * rstd * wn_ref[:, cols]).astype(BF16)

        for sub in range(cps):
            one_chunk(sub)

    vec = pl.BlockSpec((1, gw), lambda g, c: (0, g))
    blk = pl.BlockSpec((cps * CHUNK, gw), lambda g, c: (c, g))
    half = D_CONV // 2
    return pl.pallas_call(
        body, name="ssd_fwd", grid=(N_GROUPS // SSD_GPS, nc // cps),
        in_specs=[
            pl.BlockSpec((cps * CHUNK, half), lambda g, c: (c, OFF_XBC // half)),
            pl.BlockSpec((cps * CHUNK, half), lambda g, c: (c, OFF_XBC // half + 1)),
            pl.BlockSpec((CONV_WIDTH, D_CONV), lambda g, c: (0, 0)),
            pl.BlockSpec((1, D_CONV), lambda g, c: (0, 0)),
            blk, blk, vec, vec, vec,
            pl.BlockSpec((CHUNK, CHUNK), lambda g, c: (0, 0)),
        ],
        out_specs=[blk, blk, pl.BlockSpec((cps, SSD_GPS, D_STATE, GROUP_W), lambda g, c: (c, g, 0, 0)),
                   pl.BlockSpec((cps * CHUNK, D_CONV), lambda g, c: (c, 0))],
        out_shape=[
            jax.ShapeDtypeStruct((lp, D_MIX), BF16),
            jax.ShapeDtypeStruct((lp, D_SSD), F32),
            jax.ShapeDtypeStruct((nc, N_GROUPS, D_STATE, GROUP_W), F32),
            jax.ShapeDtypeStruct((lp, D_CONV), F32),
        ],
        scratch_shapes=[pltpu.VMEM((SSD_GPS, D_STATE, GROUP_W), F32), pltpu.VMEM((HALO, D_CONV), F32)],
        compiler_params=_params(("arbitrary", "arbitrary")),
    )(proj, proj, conv_w, conv_b, dt_rep, proj, a_rep, dsk_rep, wn, tri)


def _ssd_bwd(dmix, ytot, proj, xbc, dt_rep, hprev, a_rep, dsk_rep, wn, tri, comm):
    lp = xbc.shape[0]
    nc = lp // CHUNK
    gps = SSD_GPS
    gw = gps * GROUP_W
    cps = _tile(nc, (SSD_CPS, 1))
    ncm = comm.n
    n_in, n_out = 10, 6
    grid = (N_GROUPS // gps, nc // cps)

    def all_groups(refs):
        (dyn_ref, ytot_ref, z_ref, xbc_ref, dt_ref, hprev_ref, a_ref, dsk_ref, wn_ref, tri_ref,
         dz_ref, dxbc_ref, ddt_ref, dd_ref, da_ref, dwn_ref, dh_scr) = refs
        G = range(gps)
        H = range(HEADS_PER_GROUP)
        cl = [slice(gi * GROUP_W, (gi + 1) * GROUP_W) for gi in G]
        hl = [slice(r * HEAD_DIM, (r + 1) * HEAD_DIM) for r in H]
        tri = tri_ref[...]
        xbc = [xbc_ref[:, gi * XBC_BLK:(gi + 1) * XBC_BLK] for gi in G]
        dt = [dt_ref[:, cl[gi]] for gi in G]
        a = [a_ref[:, cl[gi]] for gi in G]
        xs = [xbc[gi][:, :GROUP_W] for gi in G]
        bbf = [xbc[gi][:, GROUP_W:GROUP_W + D_STATE].astype(BF16) for gi in G]
        cbf = [xbc[gi][:, GROUP_W + D_STATE:].astype(BF16) for gi in G]
        hprev = [hprev_ref[gi] for gi in G]
        hbf = [hprev[gi].astype(BF16) for gi in G]
        dhn = [dh_scr[gi] for gi in G]
        dhnb = [dhn[gi].astype(BF16) for gi in G]
        cs = [_dot(tri, dt[gi] * a[gi], NN, HIGHEST) for gi in G]
        g = [_dot(cbf[gi], hbf[gi]) for gi in G]
        dxw = [_dot(bbf[gi], dhnb[gi]) for gi in G]
        cs_t = [_cs_row(cs[gi]) for gi in G]
        dy = []
        for gi in G:
            ytot = ytot_ref[:, cl[gi]]
            z = z_ref[:, cl[gi]]
            dyn = dyn_ref[:, cl[gi]]
            sz = _sigmoid(z)
            silu_z = z * sz
            gz = ytot * silu_z
            rstd = lax.rsqrt(jnp.mean(gz * gz, axis=-1, keepdims=True) + EPS)
            xhat = gz * rstd
            dwn_ref[:, cl[gi]] += jnp.sum(dyn * xhat, axis=0, keepdims=True)
            dxhat = dyn * wn_ref[:, cl[gi]]
            dgz = rstd * (dxhat - xhat * jnp.mean(dxhat * xhat, axis=-1, keepdims=True))
            dy.append(dgz * silu_z)
            dz_ref[:, cl[gi]] = (dgz * ytot * (sz * (1.0 + z * (1.0 - sz)))).astype(BF16)
            dd_ref[:, cl[gi]] += jnp.sum(dy[gi] * xs[gi], axis=0, keepdims=True)
        xdt = [xs[gi] * dt[gi] for gi in G]
        e = [jnp.exp(cs[gi]) for gi in G]
        cs_last = [cs[gi][CHUNK - 1:CHUNK, :] for gi in G]
        dte = [jnp.exp(cs_last[gi] - cs[gi]) for gi in G]
        cd = [jnp.exp(cs_last[gi]) for gi in G]
        dgb = [(dy[gi] * e[gi]).astype(BF16) for gi in G]
        dyb = [dy[gi].astype(BF16) for gi in G]
        xdtb = [xdt[gi].astype(BF16) for gi in G]
        dc = [_dot(dgb[gi], hbf[gi], NT) for gi in G]
        dhprev = [_dot(cbf[gi], dgb[gi], TN) for gi in G]
        db = [_dot((xdt[gi] * dte[gi]).astype(BF16), dhnb[gi], NT) for gi in G]
        row = lax.broadcasted_iota(jnp.int32, (CHUNK, CHUNK), 0)
        causal = row >= lax.broadcasted_iota(jnp.int32, (CHUNK, CHUNK), 1)
        cb = [_dot(cbf[gi], bbf[gi], NT) for gi in G]
        dm = [[_dot(dyb[gi][:, hl[r]], xdtb[gi][:, hl[r]], NT) for r in H] for gi in G]
        mb, dseg, dcbb = [], [], []
        for gi in G:
            mb.append([])
            dseg.append([])
            dcb = None
            for r in H:
                seg = cs[gi][:, r * HEAD_DIM:r * HEAD_DIM + 1] - cs_t[gi][:, hl[r]]
                lm = jnp.exp(jnp.where(causal, seg, -jnp.inf))
                m = cb[gi] * lm
                mb[gi].append(m.astype(BF16))
                dseg[gi].append(dm[gi][r] * m)
                dcb = dm[gi][r] * lm if r == 0 else dcb + dm[gi][r] * lm
            dcbb.append(dcb.astype(BF16))
        dxdt_diag = [[_dot(mb[gi][r], dyb[gi][:, hl[r]], TN) for r in H] for gi in G]
        ones = jnp.ones((CHUNK, HEAD_DIM), F32)
        colsum = [[_dot(dseg[gi][r], ones, TN, HIGHEST) for r in H] for gi in G]
        dc2 = [_dot(dcbb[gi], bbf[gi]) for gi in G]
        db2 = [_dot(dcbb[gi], cbf[gi], TN) for gi in G]
        dcs = []
        for gi in G:
            t_dte = dxw[gi] * xdt[gi] * dte[gi]
            dcs_last = (jnp.sum(dhn[gi] * hprev[gi], axis=0, keepdims=True) * cd[gi]
                        + jnp.sum(t_dte, axis=0, keepdims=True))
            diag = jnp.concatenate(
                [(jnp.sum(dseg[gi][r], axis=1, keepdims=True) - colsum[gi][r]) * (1.0 / HEAD_DIM) for r in H], axis=1)
            d = dy[gi] * g[gi] * e[gi] - t_dte + diag
            row = lax.broadcasted_iota(jnp.int32, d.shape, 0)
            dcs.append(d + jnp.where(row == CHUNK - 1, dcs_last, 0.0))
        dda = [_dot(tri, dcs[gi], TN, HIGHEST) for gi in G]
        for gi in G:
            dxdt = dxw[gi] * dte[gi] + jnp.concatenate(dxdt_diag[gi], axis=1)
            da_ref[:, cl[gi]] += jnp.sum(dda[gi] * dt[gi], axis=0, keepdims=True)
            ddt = dda[gi] * a[gi] + dxdt * xs[gi]
            dxs = dsk_ref[:, cl[gi]] * dy[gi] + dxdt * dt[gi]
            ddt_ref[:, cl[gi]] = ddt * (1.0 - jnp.exp(-dt[gi]))
            dxbc_ref[:, gi * XBC_BLK:(gi + 1) * XBC_BLK] = jnp.concatenate(
                [dxs, db[gi] + db2[gi], dc[gi] + dc2[gi]], axis=1)
            dh_scr[gi] = dhprev[gi] + dhn[gi] * cd[gi]

    def body(*refs):
        ins = refs[:n_in]
        cin = refs[n_in:n_in + ncm]
        outs = refs[n_in + ncm:n_in + ncm + n_out]
        cout = refs[n_in + ncm + n_out:n_in + 2 * ncm + n_out]
        dh_scr = refs[n_in + 2 * ncm + n_out]
        sems = refs[n_in + 2 * ncm + n_out + 1:]
        g, c = pl.program_id(0), pl.program_id(1)

        @pl.when((g == 0) & (c == 0))
        def _():
            comm.start(cin, cout, sems)

        @pl.when(c == 0)
        def _():
            dh_scr[...] = jnp.zeros_like(dh_scr)
            for ref in outs[3:]:
                ref[...] = jnp.zeros_like(ref)

        for sub in reversed(range(cps)):
            rs = pl.ds(sub * CHUNK, CHUNK)
            chunk_ins = tuple(r.at[rs] for r in ins[:5]) + (ins[5].at[sub],) + ins[6:]
            chunk_outs = tuple(r.at[rs] for r in outs[:3]) + outs[3:]
            all_groups(chunk_ins + chunk_outs + (dh_scr,))

        @pl.when((g == grid[0] - 1) & (c == grid[1] - 1))
        def _():
            comm.wait(cin, cout, sems)

    rev = lambda c: grid[1] - 1 - c
    vec = pl.BlockSpec((1, gw), lambda g, c: (0, g))
    blk = pl.BlockSpec((cps * CHUNK, gw), lambda g, c: (rev(c), g))
    xblk = pl.BlockSpec((cps * CHUNK, gps * XBC_BLK), lambda g, c: (rev(c), g))
    res = pl.pallas_call(
        body, name="ssd_bwd", grid=grid,
        in_specs=[blk, blk, blk, xblk, blk,
                  pl.BlockSpec((cps, gps, D_STATE, GROUP_W), lambda g, c: (rev(c), g, 0, 0)),
                  vec, vec, vec,
                  pl.BlockSpec((CHUNK, CHUNK), lambda g, c: (0, 0))] + comm.in_specs,
        out_specs=[blk, xblk, blk, vec, vec, vec] + comm.out_specs,
        out_shape=[
            jax.ShapeDtypeStruct((lp, NP), BF16),
            jax.ShapeDtypeStruct((lp, D_CONV), F32),
            jax.ShapeDtypeStruct((lp, D_SSD), F32),
            jax.ShapeDtypeStruct((1, D_SSD), F32),
            jax.ShapeDtypeStruct((1, D_SSD), F32),
            jax.ShapeDtypeStruct((1, D_SSD), F32),
        ] + comm.out_shape,
        scratch_shapes=[pltpu.VMEM((gps, D_STATE, GROUP_W), F32)] + comm.scratch,
        compiler_params=_params(("arbitrary", "arbitrary"), side_effects=True),
    )(dmix, ytot, proj, xbc, dt_rep, hprev, a_rep, dsk_rep, wn, tri, *comm.operands)
    return res


def _stack_heads(t, h):
    return jnp.concatenate([t[:, (REP * h + r) * HEAD_DIM:(REP * h + r + 1) * HEAD_DIM] for r in range(REP)], axis=0)


def _band(t2, t1, t0, h):
    sl = slice(h * HEAD_DIM, (h + 1) * HEAD_DIM)
    return jnp.concatenate([t2[:, sl], t1[:, sl], t0[:, sl]], axis=0)


def _attn_probs(s, sink, qc):
    s = s * (HEAD_DIM ** -0.5)
    key_abs = (qc - WINDOW_CHUNKS) * CHUNK + lax.broadcasted_iota(jnp.int32, s.shape, 1)
    s = jnp.where(key_abs >= PAD_LEAD, s, -jnp.inf)
    m = jnp.maximum(jnp.max(s, axis=-1, keepdims=True), sink)
    p = jnp.exp(s - m)
    ps = jnp.exp(sink - m)
    denom = jnp.sum(p, axis=-1, keepdims=True) + ps
    return p / denom, ps / denom


ATT_QC_FWD = 6
ATT_QC_BWD = 2


def _kv_specs(width, newest_chunk_of, kc):
    return [pl.BlockSpec((CHUNK, width), functools.partial(lambda j, p: (jnp.maximum(newest_chunk_of(p) - j, 0), 0), j))
            for j in range(kc - 1, -1, -1)]


def _attn_fwd(qr, kr, vb, proj, sink_stack, mix):
    lp = qr.shape[0]
    nc = lp // CHUNK
    qn = _tile(nc, (ATT_QC_FWD, 2, 1))
    kn = WINDOW_CHUNKS + qn
    qrows = qn * CHUNK

    def body(q_ref, *rest):
        k_refs, v_refs = rest[:kn], rest[kn:2 * kn]
        g_ref, sink_ref, _, att_ref, mix_ref = rest[2 * kn:]
        p = pl.program_id(0)
        q = q_ref[...]
        ks = [r[...] for r in k_refs]
        vs = [r[...] for r in v_refs]
        units = [(u, h) for u in range(qn) for h in range(KV_HEADS)]
        s = [_dot(_stack_heads(q[u * CHUNK:(u + 1) * CHUNK, :], h), _band(*ks[u:u + 3], h), NT) for u, h in units]
        vbh = [_band(*vs[u:u + 3], h) for u, h in units]
        pn = [_attn_probs(s[i], sink_ref[h], qn * p + u)[0].astype(BF16) for i, (u, h) in enumerate(units)]
        o = [_dot(pn[i], vbh[i]) for i in range(len(units))]
        att = jnp.concatenate(
            [jnp.concatenate([o[u * KV_HEADS + h][r * CHUNK:(r + 1) * CHUNK, :] for h in range(KV_HEADS) for r in range(REP)],
                             axis=1) for u in range(qn)], axis=0)
        att_ref[...] = att
        g = g_ref[...]
        mix_ref[...] = (att * (g * _sigmoid(g))).astype(BF16)

    newest = lambda p: qn * p + qn - 1
    return pl.pallas_call(
        body, name="attn_fwd", grid=(nc // qn,),
        in_specs=[pl.BlockSpec((qrows, D_ATT), lambda p: (p, 0))] + _kv_specs(D_KV, newest, kn) + _kv_specs(D_KV, newest, kn) + [
            pl.BlockSpec((qrows, D_ATT), lambda p: (p, OFF_G // D_ATT)),
            pl.BlockSpec((KV_HEADS, REP * CHUNK, 1), lambda p: (0, 0, 0)),
            ANY,
        ],
        out_specs=[pl.BlockSpec((qrows, D_ATT), lambda p: (p, 0)),
                   pl.BlockSpec((qrows, D_ATT), lambda p: (p, D_SSD // D_ATT))],
        out_shape=[jax.ShapeDtypeStruct((lp, D_ATT), F32), jax.ShapeDtypeStruct((lp, D_MIX), BF16)],
        input_output_aliases={2 * kn + 3: 1},
        compiler_params=_params(("parallel",)),
    )(qr, *([kr] * kn), *([vb] * kn), proj, sink_stack, mix)


def _attn_bwd(qr, kr, vb, att, proj, dmix, sink_stack):
    lp = qr.shape[0]
    nc = lp // CHUNK
    qn = ATT_QC_BWD
    kn = WINDOW_CHUNKS + qn
    assert nc % qn == 0 and WINDOW_CHUNKS % qn == 0
    steps = nc // qn
    qrows = qn * CHUNK
    wrows = kn * CHUNK

    def body(q_ref, *rest):
        k_refs, v_refs = rest[:kn], rest[kn:2 * kn]
        (att_ref, g_ref, do_ref, sink_ref, dq_ref, dk_ref, dv_ref, dg_ref, dsink_ref, dk_acc, dv_acc) = rest[2 * kn:]
        step = pl.program_id(0)

        @pl.when(step == 0)
        def _():
            dk_acc[...] = jnp.zeros_like(dk_acc)
            dv_acc[...] = jnp.zeros_like(dv_acc)
            dsink_ref[...] = jnp.zeros_like(dsink_ref)

        q = q_ref[...]
        ks = [r[...] for r in k_refs]
        vs = [r[...] for r in v_refs]
        att = att_ref[...]
        g = g_ref[...]
        dog = do_ref[...]
        sg = _sigmoid(g)
        dg_ref[...] = dog * att * (sg * (1.0 + g * (1.0 - sg)))
        do = dog * (g * sg)
        units = [(u, h) for u in range(qn) for h in range(KV_HEADS)]
        n = range(len(units))
        rows = [slice(u * CHUNK, (u + 1) * CHUNK) for u in range(qn)]
        qs = [_stack_heads(q[rows[u], :], h) for u, h in units]
        kb = [_band(*ks[u:u + 3], h) for u, h in units]
        vbh = [_band(*vs[u:u + 3], h) for u, h in units]
        dos = [_stack_heads(do[rows[u], :], h) for u, h in units]
        dosb = [dos[i].astype(BF16) for i in n]
        s = [_dot(qs[i], kb[i], NT) for i in n]
        dp = [_dot(dosb[i], vbh[i], NT) for i in n]
        ds, pnb, dsink = [], [], []
        for i, (u, h) in enumerate(units):
            pn, psink = _attn_probs(s[i], sink_ref[h], qn * jnp.minimum(step, steps - 1) + u)
            delta = jnp.sum(dos[i] * _stack_heads(att[rows[u], :], h), axis=-1, keepdims=True)
            ds.append((pn * (dp[i] - delta)).astype(BF16))
            pnb.append(pn.astype(BF16))
            dsink.append(-(psink * delta))
        dqs = [_dot(ds[i], kb[i]) for i in n]
        dks = [_dot(ds[i], qs[i], TN) for i in n]
        dvs = [_dot(pnb[i], dosb[i], TN) for i in n]
        dq_ref[...] = jnp.concatenate(
            [jnp.concatenate([dqs[u * KV_HEADS + h][r * CHUNK:(r + 1) * CHUNK, :]
                              for h in range(KV_HEADS) for r in range(REP)], axis=1) for u in range(qn)],
            axis=0) * (HEAD_DIM ** -0.5)
        @pl.when(step < steps)
        def _():
            for i, (u, h) in enumerate(units):
                dsink_ref[h] += dsink[i]
            for u in range(qn):
                band = slice(u * CHUNK, u * CHUNK + BAND)
                dk_acc[band, :] += jnp.concatenate(dks[u * KV_HEADS:(u + 1) * KV_HEADS], axis=1) * (HEAD_DIM ** -0.5)
                dv_acc[band, :] += jnp.concatenate(dvs[u * KV_HEADS:(u + 1) * KV_HEADS], axis=1)

        dk_ref[...] = dk_acc[0:qrows, :]
        dv_ref[...] = dv_acc[0:qrows, :]
        for acc in (dk_acc, dv_acc):
            rest_rows = acc[qrows:wrows, :]
            acc[0:wrows - qrows, :] = rest_rows
            acc[wrows - qrows:wrows, :] = jnp.zeros((qrows, D_KV), F32)

    qp = lambda p: jnp.minimum(p, steps - 1)
    newest = lambda p: qn * qp(p) + qn - 1
    qblk = pl.BlockSpec((qrows, D_ATT), lambda p: (qp(p), 0))
    oldest = pl.BlockSpec((qrows, D_KV), lambda p: (jnp.maximum(p - 1, 0), 0))
    return pl.pallas_call(
        body, name="attn_bwd", grid=(steps + 1,),
        in_specs=[qblk] + _kv_specs(D_KV, newest, kn) + _kv_specs(D_KV, newest, kn) + [
            qblk,
            pl.BlockSpec((qrows, D_ATT), lambda p: (qp(p), OFF_G // D_ATT)),
            pl.BlockSpec((qrows, D_ATT), lambda p: (qp(p), D_SSD // D_ATT)),
            pl.BlockSpec((KV_HEADS, REP * CHUNK, 1), lambda p: (0, 0, 0)),
        ],
        out_specs=[qblk, oldest, oldest, qblk, pl.BlockSpec((KV_HEADS, REP * CHUNK, 1), lambda p: (0, 0, 0))],
        out_shape=[
            jax.ShapeDtypeStruct((lp, D_ATT), F32),
            jax.ShapeDtypeStruct((lp, D_KV), F32),
            jax.ShapeDtypeStruct((lp, D_KV), F32),
            jax.ShapeDtypeStruct((lp, D_ATT), F32),
            jax.ShapeDtypeStruct((KV_HEADS, REP * CHUNK, 1), F32),
        ],
        scratch_shapes=[pltpu.VMEM((wrows, D_KV), F32), pltpu.VMEM((wrows, D_KV), F32)],
        compiler_params=_params(("arbitrary",)),
    )(qr, *([kr] * kn), *([vb] * kn), att, proj, dmix, sink_stack)


def _post_loss(out, x2d, target, w):
    lp = out.shape[0]
    nc = lp // CHUNK
    nx = x2d.shape[0] // CHUNK

    k = _tile(nc, (ROW_K, 3, 2, 1))

    def body(o_ref, *rest):
        x_refs, t_refs = rest[:k], rest[k:2 * k]
        w_ref, do_ref, dy_ref, gw_ref, loss_ref = rest[2 * k:]
        i = pl.program_id(0)

        @pl.when(i == 0)
        def _():
            gw_ref[...] = jnp.zeros_like(gw_ref)
            loss_ref[...] = jnp.zeros_like(loss_ref)

        o = o_ref[...]
        w = w_ref[...]
        rstd = lax.rsqrt(jnp.mean(o * o, axis=-1, keepdims=True) + EPS)
        xhat = o * rstd
        x = jnp.concatenate([r[...] for r in x_refs], axis=0)
        t = jnp.concatenate([r[...] for r in t_refs], axis=0)
        chunk = _chunk_of_row(i, k)
        err = (x + xhat * w - t) * ((chunk > 0) & (chunk <= nx)).astype(F32)
        loss_ref[...] += 0.5 * jnp.sum(jnp.mean(err * err, axis=-1, keepdims=True), axis=0, keepdims=True)
        dy = err * (1.0 / D_MODEL)
        dy_ref[...] = dy
        gw_ref[...] += jnp.sum(dy * xhat, axis=0, keepdims=True)
        dxhat = dy * w
        do_ref[...] = (rstd * (dxhat - xhat * jnp.mean(dxhat * xhat, axis=-1, keepdims=True))).astype(BF16)

    row = pl.BlockSpec((k * CHUNK, D_MODEL), lambda i: (i, 0))
    return pl.pallas_call(
        body, name="post_loss", grid=(nc // k,),
        in_specs=[row] + _x_specs(nx, k) + _x_specs(nx, k) + [pl.BlockSpec((1, D_MODEL), lambda i: (0, 0))],
        out_specs=[row, row, pl.BlockSpec((1, D_MODEL), lambda i: (0, 0)), pl.BlockSpec((1, 128), lambda i: (0, 0))],
        out_shape=[
            jax.ShapeDtypeStruct((lp, D_MODEL), BF16),
            jax.ShapeDtypeStruct((lp, D_MODEL), F32),
            jax.ShapeDtypeStruct((1, D_MODEL), F32),
            jax.ShapeDtypeStruct((1, 128), F32),
        ],
        compiler_params=_params(("arbitrary",)),
    )(out, *([x2d] * k), *([target] * k), w)


def _prenorm_bwd(dhn, x2d, h0, dy, w):
    nx = x2d.shape[0] // CHUNK
    k = _tile(nx, (X_K, 4, 2, 1))
    rows = k * CHUNK

    def backward(h, dhn, w):
        rstd = lax.rsqrt(jnp.mean(h * h, axis=-1, keepdims=True) + EPS)
        xhat = h * rstd
        dxhat = dhn * w
        dh = rstd * (dxhat - xhat * jnp.mean(dxhat * xhat, axis=-1, keepdims=True))
        return dh, jnp.sum(dhn * xhat, axis=0, keepdims=True)

    def body(*refs):
        dhn_refs, dy_refs = refs[:k], refs[k:2 * k]
        x_ref, w_ref, gx_ref, gw_ref = refs[2 * k:]

        @pl.when(pl.program_id(0) == 0)
        def _():
            gw_ref[...] = jnp.zeros_like(gw_ref)

        dh, gw = backward(x_ref[...], jnp.concatenate([r[...] for r in dhn_refs], axis=0), w_ref[...])
        gx_ref[...] = dh + jnp.concatenate([r[...] for r in dy_refs], axis=0)
        gw_ref[...] += gw

    def body_meta(dhn_ref, h0_ref, w_ref, d0_ref, gw_ref):
        d0_ref[...], gw_ref[...] = backward(h0_ref[...], dhn_ref[...], w_ref[...])

    chunk_specs = [pl.BlockSpec((CHUNK, D_MODEL), functools.partial(lambda u, i: (k * i + 1 + u, 0), u)) for u in range(k)]
    first = pl.BlockSpec((CHUNK, D_MODEL), lambda i: (0, 0))
    vec = pl.BlockSpec((1, D_MODEL), lambda i: (0, 0))
    wide = pl.BlockSpec((rows, D_MODEL), lambda i: (i, 0))
    gx, gw_x = pl.pallas_call(
        body, name="prenorm_bwd", grid=(nx // k,),
        in_specs=chunk_specs + chunk_specs + [wide, vec],
        out_specs=[wide, vec],
        out_shape=[jax.ShapeDtypeStruct((nx * CHUNK, D_MODEL), F32), jax.ShapeDtypeStruct((1, D_MODEL), F32)],
        compiler_params=_params(("arbitrary",)),
    )(*([dhn] * k), *([dy] * k), x2d, w)
    d0, gw_0 = pl.pallas_call(
        body_meta, name="prenorm_bwd_meta", grid=(1,),
        in_specs=[first, first, vec], out_specs=[first, vec],
        out_shape=[jax.ShapeDtypeStruct((CHUNK, D_MODEL), F32), jax.ShapeDtypeStruct((1, D_MODEL), F32)],
        compiler_params=_params(("arbitrary",)),
    )(dhn, h0, w)
    return gx, d0, gw_x + gw_0


def _adamw(slabs, w, m, v, name):
    rows, cols = w.shape
    tr = _tile(rows, (256, 128, 64, 16, 8))
    c1 = 1.0 - ADAM_B1 ** ADAM_STEP
    c2 = 1.0 - ADAM_B2 ** ADAM_STEP

    def body(s_ref, w_ref, m_ref, v_ref, g_ref, d_ref, mo_ref, vo_ref):
        g = s_ref[0].astype(F32)
        for k in range(1, slabs.shape[0]):
            g = g + s_ref[k].astype(F32)
        w = w_ref[...]
        m = ADAM_B1 * m_ref[...] + (1.0 - ADAM_B1) * g
        v = ADAM_B2 * v_ref[...] + (1.0 - ADAM_B2) * (g * g)
        g_ref[...] = g
        mo_ref[...] = m
        vo_ref[...] = v
        d_ref[...] = -ADAM_LR * ((m / c1) / (jnp.sqrt(v / c2) + ADAM_EPS) + ADAM_WD * w)

    blk = pl.BlockSpec((tr, cols), lambda i: (i, 0))
    shape = jax.ShapeDtypeStruct((rows, cols), F32)
    return pl.pallas_call(
        body, name=name, grid=(rows // tr,),
        in_specs=[pl.BlockSpec((slabs.shape[0], tr, cols), lambda i: (0, i, 0)), blk, blk, blk],
        out_specs=[blk, blk, blk, blk],
        out_shape=[shape, shape, shape, shape],
        compiler_params=_params(("parallel",)),
    )(slabs, w, m, v)


def _perm_xbc(a):
    lead = a.shape[:-1]
    xs = a[..., :D_SSD].reshape(lead + (N_GROUPS, GROUP_W))
    b = a[..., D_SSD:D_SSD + N_GROUPS * D_STATE].reshape(lead + (N_GROUPS, D_STATE))
    c = a[..., D_SSD + N_GROUPS * D_STATE:].reshape(lead + (N_GROUPS, D_STATE))
    return jnp.concatenate([xs, b, c], axis=-1).reshape(lead + (D_CONV,))


def _unperm_xbc(a):
    lead = a.shape[:-1]
    t = a.reshape(lead + (N_GROUPS, XBC_BLK))
    xs = t[..., :GROUP_W].reshape(lead + (D_SSD,))
    b = t[..., GROUP_W:GROUP_W + D_STATE].reshape(lead + (N_GROUPS * D_STATE,))
    c = t[..., GROUP_W + D_STATE:].reshape(lead + (N_GROUPS * D_STATE,))
    return jnp.concatenate([xs, b, c], axis=-1)


R_Z, R_XBC, R_DT, R_Q, R_K, R_V, R_G = 0, 2048, 6144, 6176, 7200, 7456, 7712


def _internal_of_reference():
    ref = np.arange(D_IN_PROJ)
    out = np.empty(D_IN_PROJ, np.int64)
    out[R_Z:R_XBC] = OFF_Z + ref[:D_SSD]
    xs = np.arange(D_SSD)
    out[R_XBC:R_XBC + D_SSD] = OFF_XBC + (xs // GROUP_W) * XBC_BLK + xs % GROUP_W
    bc = np.arange(N_GROUPS * D_STATE)
    out[R_XBC + D_SSD:R_XBC + D_SSD + N_GROUPS * D_STATE] = OFF_XBC + (bc // D_STATE) * XBC_BLK + GROUP_W + bc % D_STATE
    out[R_XBC + D_SSD + N_GROUPS * D_STATE:R_DT] = OFF_XBC + (bc // D_STATE) * XBC_BLK + GROUP_W + D_STATE + bc % D_STATE
    out[R_DT:R_Q] = OFF_DT + np.arange(SSD_HEADS)
    out[R_Q:R_K] = OFF_Q + np.arange(D_ATT)
    out[R_K:R_V] = OFF_K + np.arange(D_KV)
    out[R_V:R_G] = OFF_V + np.arange(D_KV)
    out[R_G:] = OFF_G + np.arange(D_ATT)
    return out


def _runs(src, dst_break):
    runs, lo = [], 0
    for i in range(1, len(src) + 1):
        if i == len(src) or src[i] != src[i - 1] + 1 or dst_break[i] != dst_break[i - 1]:
            runs.append((lo, i))
            lo = i
    return runs


RELAYOUT_ROWS = 256


def _lane_window(ref, lead, c0, n):
    a0 = c0 // 128 * 128
    a1 = min(-(-(c0 + n) // 128) * 128, ref.shape[-1])
    return ref[lead + (slice(None), slice(a0, a1))][:, c0 - a0:c0 - a0 + n]


def _w_in_internal(w_gathered):
    int_of_ref = _internal_of_reference()
    ref_of_int = np.full(NP, -1, np.int64)
    ref_of_int[int_of_ref] = np.arange(D_IN_PROJ)
    shard = np.where(ref_of_int >= 0, ref_of_int // SHARD_IN, -1)
    src = np.where(ref_of_int >= 0, ref_of_int, -10 - 2 * np.arange(NP))
    plan, zeros = [], 0
    for lo, hi in _runs(src, shard):
        if ref_of_int[lo] < 0:
            zeros += hi - lo
            continue
        if zeros:
            plan.append((None, 0, zeros))
            zeros = 0
        plan.append((int(ref_of_int[lo] // SHARD_IN), int(ref_of_int[lo] % SHARD_IN), hi - lo))
    if zeros:
        plan.append((None, 0, zeros))
    tr = RELAYOUT_ROWS

    def body(w_ref, o_ref):
        o_ref[...] = jnp.concatenate(
            [jnp.zeros((tr, n), o_ref.dtype) if s is None else _lane_window(w_ref, (s,), c0, n) for s, c0, n in plan], axis=1)

    return pl.pallas_call(
        body, name="w_in_relayout", grid=(D_MODEL // tr,),
        in_specs=[pl.BlockSpec((N_DEV, tr, SHARD_IN), lambda i: (0, i, 0))],
        out_specs=pl.BlockSpec((tr, NP), lambda i: (i, 0)),
        out_shape=jax.ShapeDtypeStruct((D_MODEL, NP), w_gathered.dtype),
        compiler_params=_params(("parallel",)),
    )(w_gathered)


def _w_in_slabs(dw):
    int_of_ref = _internal_of_reference()
    plan = []
    for s in range(N_DEV):
        cols = int_of_ref[s * SHARD_IN:(s + 1) * SHARD_IN]
        plan.append([(int(cols[lo]), hi - lo) for lo, hi in _runs(cols, np.zeros_like(cols))])
    tr = RELAYOUT_ROWS

    def body(dw_ref, o_ref):
        for s in range(N_DEV):
            o_ref[s] = jnp.concatenate([_lane_window(dw_ref, (), c0, n) for c0, n in plan[s]], axis=1)

    return pl.pallas_call(
        body, name="dw_in_relayout", grid=(D_MODEL // tr,),
        in_specs=[pl.BlockSpec((tr, NP), lambda i: (i, 0))],
        out_specs=pl.BlockSpec((N_DEV, tr, SHARD_IN), lambda i: (0, i, 0)),
        out_shape=jax.ShapeDtypeStruct((N_DEV, D_MODEL, SHARD_IN), dw.dtype),
        compiler_params=_params(("parallel",)),
    )(dw)


def _rep_heads(a):
    return jnp.repeat(a, HEAD_DIM, axis=1)


SMALL = (("norm_pre_w", 2048), ("conv_b", 4096), ("dt_bias", 32), ("a_log", 32), ("d_skip", 32),
         ("ssd_norm_w", 2048), ("attn_sinks", 16), ("norm_post_w", 2048))
SMALL_USED = sum(size for _, size in SMALL)
SMALL_LEN = 10368


def _pack_small(d, loss=None):
    parts = [d[name].reshape(1, size) for name, size in SMALL]
    tail = jnp.zeros((1, SMALL_LEN - SMALL_USED), F32)
    if loss is not None:
        tail = tail.at[0, 0].set(loss)
    return jnp.concatenate(parts + [tail], axis=1)


def _unpack_small(vec):
    out, off = {}, 0
    for name, size in SMALL:
        out[name] = vec[:, off:off + size]
        off += size
    return out


def kernel(x, meta_tokens, norm_pre_w, w_in, conv_w, conv_b, dt_bias, a_log, d_skip, ssd_norm_w, attn_sinks, w_out, norm_post_w, loss_target, m_meta_tokens, m_norm_pre_w, m_w_in, m_conv_w, m_conv_b, m_dt_bias, m_a_log, m_d_skip, m_ssd_norm_w, m_attn_sinks, m_w_out, m_norm_post_w, v_meta_tokens, v_norm_pre_w, v_w_in, v_conv_w, v_conv_b, v_dt_bias, v_a_log, v_d_skip, v_ssd_norm_w, v_attn_sinks, v_w_out, v_norm_post_w):
    seq = x.shape[1]
    lp = seq + 2 * CHUNK
    x2d = x[0]

    w_in_g, conv_w_g, meta_g = _gather_two_level([w_in[0].astype(BF16), conv_w[0], meta_tokens], "gather_w_in")
    w_all = _w_in_internal(w_in_g)
    conv_w_full = _perm_xbc(jnp.transpose(conv_w_g, (1, 0, 2)).reshape(CONV_WIDTH, D_CONV))
    conv_b_int = _perm_xbc(conv_b)
    meta_full = jnp.transpose(meta_g, (1, 0, 2)).reshape(N_META, D_MODEL)
    h0 = jnp.concatenate([jnp.zeros((PAD_LEAD, D_MODEL), F32), meta_full], axis=0)

    pos = (jnp.arange(lp) - PAD_LEAD).astype(F32)
    half = HEAD_DIM // 2
    inv = ROPE_THETA ** (-jnp.arange(half, dtype=F32) / half)
    ang = pos[:, None] * inv[None, :]
    cos_t = jnp.tile(jnp.cos(ang), (1, 4))
    sin_t = jnp.tile(jnp.concatenate([-jnp.sin(ang), jnp.sin(ang)], axis=1), (1, 2))
    head_of_col = np.arange(D_SSD) // HEAD_DIM
    expand = jnp.asarray((np.arange(128)[:, None] == head_of_col[None, :]).astype(np.float32))
    reduce_t = jnp.asarray((head_of_col[:, None] == np.arange(128)[None, :]).astype(np.float32))
    tri = jnp.asarray(np.tril(np.ones((CHUNK, CHUNK), np.float32)))
    a_rep = _rep_heads(-jnp.exp(a_log))
    dsk_rep = _rep_heads(d_skip)
    dt_bias_rep = _rep_heads(dt_bias)
    sink_stack = jnp.repeat(attn_sinks.reshape(KV_HEADS, REP), CHUNK, axis=1).reshape(KV_HEADS, REP * CHUNK, 1)

    hn = _prenorm(x2d, h0, norm_pre_w)
    tm = _tile(lp, (1056, 704, 128, 64))
    proj, w_out_g = _matmul(hn, w_all, tm=tm, tn=1536, tk=D_MODEL, out_dtype=F32, name="in_proj",
                            comm=_Comm([(w_out[0].astype(BF16), "gather")]))
    w_out_full = w_out_g.reshape(D_MIX, D_MODEL)
    qr, kr, vb, dt_rep = _act_fwd(proj, cos_t, sin_t, expand, dt_bias_rep)
    mix, ytot, hprev, xbc = _ssd_fwd(proj, conv_w_full, conv_b_int, dt_rep, a_rep, dsk_rep, ssd_norm_w, tri)
    att, mix = _attn_fwd(qr, kr, vb, proj, sink_stack, mix)
    out = _matmul(mix, w_out_full, tm=tm, tn=1024, tk=D_MIX, out_dtype=F32, name="out_proj")
    dout, dy, g_norm_post, loss_part = _post_loss(out, x2d, loss_target[0], norm_post_w)

    dmix = _matmul(dout, w_out_full, trans_b=True, tm=tm, tn=1024, tk=D_MODEL, out_dtype=F32, name="dmix")
    dw_out = _matmul(mix, dout, trans_a=True, tm=512, tn=1024, tk=lp, out_dtype=BF16, name="dw_out")
    dqr, dkr, dv, dg, dsink_rows = _attn_bwd(qr, kr, vb, att, proj, dmix, sink_stack)
    dproj, dxbc, ddt_part, dd_part, da_part, g_ssd_norm, g_out = _ssd_bwd(
        dmix, ytot, proj, xbc, dt_rep, hprev, a_rep, dsk_rep, ssd_norm_w, tri,
        _Comm([(dw_out.reshape(N_DEV, D_MIX // N_DEV, D_MODEL), "scatter")]))
    dproj, dconv_w_int, dconv_b_int = _conv_bwd(dxbc, proj, conv_w_full, conv_b_int, dproj)
    dproj, ddt_bias = _act_bwd(dqr, dkr, dv, dg, ddt_part, cos_t, sin_t, reduce_t, dproj)
    dw_all = _matmul(hn, dproj, trans_a=True, tm=512, tn=1024, tk=lp, out_dtype=BF16, name="dw_in")
    dw_slabs = _w_in_slabs(dw_all)
    dw_chip = _pair_sum(dw_slabs, _exchange_sibling(dw_slabs, "dw_in_sibling"), "dw_in_pair_sum")
    dhn, g_in = _matmul(dproj, w_all, trans_b=True, tm=tm, tn=1024, tk=1536, out_dtype=F32, name="dhn",
                        comm=_Comm([(dw_chip, "scatter")], scope="chips"))
    grad_x, dh0, g_norm_pre = _prenorm_bwd(dhn, x2d, h0, dy, norm_pre_w)

    dmeta = dh0[PAD_LEAD:, :]
    dconv_w_ref = _unperm_xbc(dconv_w_int)
    heads = lambda part: part.reshape(SSD_HEADS, HEAD_DIM).sum(axis=1).reshape(1, SSD_HEADS)
    small_local = _pack_small({
        "norm_pre_w": g_norm_pre, "conv_b": _unperm_xbc(dconv_b_int), "dt_bias": ddt_bias[:, :SSD_HEADS],
        "a_log": heads(da_part) * (-jnp.exp(a_log)), "d_skip": heads(dd_part), "ssd_norm_w": g_ssd_norm,
        "attn_sinks": dsink_rows.reshape(Q_HEADS, CHUNK).sum(axis=1).reshape(1, Q_HEADS),
        "norm_post_w": g_norm_post}, loss=loss_part[0, 0])
    g_conv, g_meta, g_small = _exchange(
        [(jnp.transpose(dconv_w_ref.reshape(CONV_WIDTH, N_DEV, D_CONV // N_DEV), (1, 0, 2)), "scatter"),
         (jnp.transpose(dmeta.reshape(N_META, N_DEV, D_MODEL // N_DEV), (1, 0, 2)), "scatter"),
         (small_local, "gather")], "exchange_small")

    res = {}
    res["w_in"] = [o[None] for o in _adamw(g_in, w_in[0], m_w_in[0], v_w_in[0], "adamw_w_in")]
    res["w_out"] = [o[None] for o in _adamw(g_out, w_out[0], m_w_out[0], v_w_out[0], "adamw_w_out")]
    res["conv_w"] = [o[None] for o in _adamw(g_conv, conv_w[0], m_conv_w[0], v_conv_w[0], "adamw_conv_w")]
    res["meta_tokens"] = _adamw(g_meta, meta_tokens, m_meta_tokens, v_meta_tokens, "adamw_meta")
    given = dict(norm_pre_w=(norm_pre_w, m_norm_pre_w, v_norm_pre_w), conv_b=(conv_b, m_conv_b, v_conv_b),
                 dt_bias=(dt_bias, m_dt_bias, v_dt_bias), a_log=(a_log, m_a_log, v_a_log),
                 d_skip=(d_skip, m_d_skip, v_d_skip), ssd_norm_w=(ssd_norm_w, m_ssd_norm_w, v_ssd_norm_w),
                 attn_sinks=(attn_sinks, m_attn_sinks, v_attn_sinks),
                 norm_post_w=(norm_post_w, m_norm_post_w, v_norm_post_w))
    packed = [_pack_small({k: t[j] for k, t in given.items()}) for j in range(3)]
    small_out = _adamw(g_small, packed[0], packed[1], packed[2], "adamw_small")
    small_res = [_unpack_small(r) for r in small_out]
    loss = small_out[0][0, SMALL_USED]

    order = ["meta_tokens", "norm_pre_w", "w_in", "conv_w", "conv_b", "dt_bias", "a_log", "d_skip", "ssd_norm_w",
             "attn_sinks", "w_out", "norm_post_w"]
    outs = []
    for j in range(4):
        for name in order:
            outs.append(res[name][j] if name in res else small_res[j][name])
    return (loss, grad_x[None], *outs)
```

```python
import functools

import numpy as np
import jax
import jax.numpy as jnp
from jax import lax
from jax.experimental import pallas as pl
from jax.experimental.pallas import tpu as pltpu

F32 = jnp.float32
BF16 = jnp.bfloat16
HIGHEST = lax.Precision.HIGHEST

N_DEV = 8
D_MODEL = 2048
CHUNK = 64
N_META = 16
PAD_LEAD = CHUNK - N_META
EPS = 1e-6
N_GROUPS = 8
HEADS_PER_GROUP = 4
HEAD_DIM = 64
GROUP_W = HEADS_PER_GROUP * HEAD_DIM
D_STATE = 128
D_SSD = 2048
D_CONV = 4096
SSD_HEADS = 32
CONV_WIDTH = 4
Q_HEADS = 16
KV_HEADS = 4
REP = 4
D_ATT = 1024
D_KV = 256
WINDOW_CHUNKS = 2
BAND = (WINDOW_CHUNKS + 1) * CHUNK
ROPE_THETA = 10000.0
D_MIX = D_SSD + D_ATT
D_IN_PROJ = 8736
SHARD_IN = D_IN_PROJ // N_DEV

OFF_Z, OFF_XBC, OFF_Q, OFF_G, OFF_K, OFF_V, OFF_DT = 0, 2048, 6144, 7168, 8192, 8448, 8704
NP = 9216
TAIL_W = NP - OFF_Q
XBC_BLK = 512
SSD_GPS = 8
SSD_CPS = 2

ADAM_LR, ADAM_B1, ADAM_B2, ADAM_EPS, ADAM_WD, ADAM_STEP = 0.001, 0.9, 0.999, 1e-08, 0.01, 10

VMEM_LIMIT = 48 * 1024 * 1024

NN = (((1,), (0,)), ((), ()))
NT = (((1,), (1,)), ((), ()))
TN = (((0,), (0,)), ((), ()))
ANY = pl.BlockSpec(memory_space=pl.ANY)


def _dot(a, b, dims=NN, precision=None):
    return lax.dot_general(a, b, dims, precision=precision, preferred_element_type=F32)


def _tile(n, prefs):
    for t in prefs:
        if n % t == 0:
            return t
    return n


def _params(sem, vmem=VMEM_LIMIT, side_effects=False):
    return pltpu.CompilerParams(dimension_semantics=sem, vmem_limit_bytes=vmem, has_side_effects=side_effects)


def _sigmoid(x):
    return 1.0 / (1.0 + jnp.exp(-x))


class _Comm:
    def __init__(self, items, scope="devices"):
        self.items = items
        self.scope = scope
        self.slabs = slabs = N_DEV if scope == "devices" else N_DEV // 2
        self.n = n = len(items)
        self.operands = [arr for arr, _ in items]
        self.in_specs = [ANY] * n
        self.out_specs = [ANY] * n
        self.out_shape = [jax.ShapeDtypeStruct((slabs,) + tuple(arr.shape) if kind == "gather" else tuple(arr.shape),
                                               arr.dtype) for arr, kind in items]
        self.scratch = [pltpu.SemaphoreType.DMA((n, slabs - 1)), pltpu.SemaphoreType.DMA((n, slabs - 1)),
                        pltpu.SemaphoreType.DMA((n,))]

    def _places(self):
        pos = (lax.axis_index("x"), lax.axis_index("y"), lax.axis_index("c"))
        if self.scope == "devices":
            index = lambda p: 4 * p[0] + 2 * p[1] + p[2]
            masks = range(1, N_DEV)
        else:
            index = lambda p: 2 * p[0] + p[1]
            masks = (2, 4, 6)
        peers = []
        for k in masks:
            p = tuple(1 - pos[b] if (k >> (2 - b)) & 1 else pos[b] for b in range(3))
            peers.append((p, index(p)))
        return index(pos), peers

    def _copies(self, ins, outs, sems, landed):
        send_sems, recv_sems, local_sems = sems
        me, peers = self._places()
        local, remote = [], []
        for a, (_, kind) in enumerate(self.items):
            own = ins[a] if kind == "gather" else ins[a].at[me]
            local.append(pltpu.make_async_copy(own, outs[a].at[me], local_sems.at[a]))
            for k, (p, pid) in enumerate(peers):
                remote.append(pltpu.make_async_remote_copy(
                    src_ref=ins[a] if kind == "gather" else ins[a].at[pid],
                    dst_ref=outs[a].at[pid if landed else me],
                    send_sem=send_sems.at[a, k], recv_sem=recv_sems.at[a, k],
                    device_id=p, device_id_type=pl.DeviceIdType.MESH))
        return local, remote

    def start(self, ins, outs, sems):
        local, remote = self._copies(ins, outs, sems, landed=False)
        for cp in local + remote:
            cp.start()

    def wait(self, ins, outs, sems):
        local, remote = self._copies(ins, outs, sems, landed=True)
        for cp in remote + local:
            cp.wait()


def _exchange(items, name):
    comm = _Comm(items)
    n = comm.n

    def body(*refs):
        ins, outs, sems = refs[:n], refs[n:2 * n], refs[2 * n:]
        comm.start(ins, outs, sems)
        comm.wait(ins, outs, sems)

    return pl.pallas_call(
        body, name=name, in_specs=comm.in_specs, out_specs=comm.out_specs, out_shape=comm.out_shape,
        scratch_shapes=comm.scratch, compiler_params=pltpu.CompilerParams(has_side_effects=True),
    )(*comm.operands)


def _exchange_sibling(slabs, name):
    def body(g_ref, land_ref, send_sems, recv_sems):
        x, y, c = lax.axis_index("x"), lax.axis_index("y"), lax.axis_index("c")
        copies = [pltpu.make_async_remote_copy(
            src_ref=g_ref.at[2 * q + (1 - c)], dst_ref=land_ref.at[q], send_sem=send_sems.at[q], recv_sem=recv_sems.at[q],
            device_id=(x, y, 1 - c), device_id_type=pl.DeviceIdType.MESH) for q in range(N_DEV // 2)]
        for cp in copies:
            cp.start()
        for cp in copies:
            cp.wait()

    return pl.pallas_call(
        body, name=name, in_specs=[ANY], out_specs=ANY,
        out_shape=jax.ShapeDtypeStruct((N_DEV // 2,) + tuple(slabs.shape[1:]), slabs.dtype),
        scratch_shapes=[pltpu.SemaphoreType.DMA((N_DEV // 2,)), pltpu.SemaphoreType.DMA((N_DEV // 2,))],
        compiler_params=pltpu.CompilerParams(has_side_effects=True),
    )(slabs)


def _pair_sum(slabs, landed, name):
    _, rows, cols = slabs.shape
    tr = _tile(rows, (256, 128, 64, 16, 8))

    def body(g_ref, l_ref, o_ref):
        mine = g_ref[lax.axis_index("c")]
        o_ref[...] = (mine.astype(F32) + l_ref[...].astype(F32)).astype(o_ref.dtype)

    return pl.pallas_call(
        body, name=name, grid=(N_DEV // 2, rows // tr),
        in_specs=[pl.BlockSpec((None, 2, tr, cols), lambda q, i: (q, 0, i, 0)),
                  pl.BlockSpec((None, tr, cols), lambda q, i: (q, i, 0))],
        out_specs=pl.BlockSpec((None, tr, cols), lambda q, i: (q, i, 0)),
        out_shape=jax.ShapeDtypeStruct((N_DEV // 2, rows, cols), slabs.dtype),
        compiler_params=_params(("parallel", "parallel")),
    )(slabs.reshape(N_DEV // 2, 2, rows, cols), landed)


class _GatherTwoLevel:
    def __init__(self, arrays):
        self.arrays = arrays
        self.n = n = len(arrays)
        self.operands = list(arrays)
        self.in_specs = [ANY] * n
        self.out_specs = [ANY] * n
        self.out_shape = [jax.ShapeDtypeStruct((N_DEV,) + tuple(a.shape), a.dtype) for a in arrays]
        self.scratch = [pltpu.SemaphoreType.DMA((n, N_DEV - 1)), pltpu.SemaphoreType.DMA((n, N_DEV - 1)),
                        pltpu.SemaphoreType.DMA((n,))]

    def _plan(self, ins, outs, sems):
        send_sems, recv_sems, local_sems = sems
        x, y, c = lax.axis_index("x"), lax.axis_index("y"), lax.axis_index("c")
        me, sibling = (x, y, c), (x, y, 1 - c)
        chips = [(1 - x, y), (x, 1 - y), (1 - x, 1 - y)]

        def slab(a, place):
            return outs[a].at[4 * place[0] + 2 * place[1] + place[2]]

        def copy(a, k, block, to, src=None):
            return pltpu.make_async_remote_copy(
                src_ref=slab(a, block) if src is None else src, dst_ref=slab(a, block),
                send_sem=send_sems.at[a, k], recv_sem=recv_sems.at[a, k],
                device_id=to, device_id_type=pl.DeviceIdType.MESH)

        own, mine = [], []
        for a in range(self.n):
            mine.append(pltpu.make_async_copy(ins[a], slab(a, me), local_sems.at[a]))
            own.append(copy(a, 0, me, sibling, src=ins[a]))
            own += [copy(a, 1 + j, me, (*chip, c), src=ins[a]) for j, chip in enumerate(chips)]
        return me, sibling, chips, c, copy, own, mine

    def start(self, ins, outs, sems):
        _, _, _, _, _, own, mine = self._plan(ins, outs, sems)
        for cp in mine + own:
            cp.start()

    def wait(self, ins, outs, sems):
        me, sibling, chips, c, copy, own, mine = self._plan(ins, outs, sems)
        forwards = []
        for j, chip in enumerate(chips):
            for a in range(self.n):
                copy(a, 1 + j, (*chip, c), me).wait_recv()
                fwd = copy(a, 4 + j, (*chip, c), sibling)
                fwd.start()
                forwards.append(fwd)
        for a in range(self.n):
            copy(a, 0, sibling, me).wait_recv()
            for j, chip in enumerate(chips):
                copy(a, 4 + j, (*chip, 1 - c), me).wait_recv()
        for cp in own + forwards:
            cp.wait_send()
        for loc in mine:
            loc.wait()


def _gather_two_level(arrays, name):
    comm = _GatherTwoLevel(arrays)
    n = comm.n

    def body(*refs):
        ins, outs, sems = refs[:n], refs[n:2 * n], refs[2 * n:]
        comm.start(ins, outs, sems)
        comm.wait(ins, outs, sems)

    return pl.pallas_call(
        body, name=name, in_specs=comm.in_specs, out_specs=comm.out_specs, out_shape=comm.out_shape,
        scratch_shapes=comm.scratch, compiler_params=pltpu.CompilerParams(has_side_effects=True),
    )(*arrays)


def _matmul(a, b, *, tm, tn, tk, out_dtype, name, trans_a=False, trans_b=False, comm=None):
    m, k = (a.shape[1], a.shape[0]) if trans_a else a.shape
    n = b.shape[0] if trans_b else b.shape[1]
    nk = k // tk
    dims = TN if trans_a else (NT if trans_b else NN)
    assert not (trans_a and trans_b)
    nc = comm.n if comm else 0
    grid = (m // tm, n // tn, nk)

    def body(*refs):
        a_ref, b_ref = refs[:2]
        cin = refs[2:2 + nc]
        o_ref = refs[2 + nc]
        cout = refs[3 + nc:3 + 2 * nc]
        scratch = refs[3 + 2 * nc:]
        sems = scratch[len(scratch) - 3:] if comm else None
        i, j, kk = pl.program_id(0), pl.program_id(1), pl.program_id(2)
        if comm:
            @pl.when((i == 0) & (j == 0) & (kk == 0))
            def _():
                comm.start(cin, cout, sems)

        if nk == 1:
            o_ref[...] = _dot(a_ref[...], b_ref[...], dims).astype(out_dtype)
        else:
            acc_ref = scratch[0]

            @pl.when(kk == 0)
            def _():
                acc_ref[...] = jnp.zeros_like(acc_ref)

            acc_ref[...] += _dot(a_ref[...], b_ref[...], dims)

            @pl.when(kk == nk - 1)
            def _():
                o_ref[...] = acc_ref[...].astype(out_dtype)

        if comm:
            @pl.when((i == grid[0] - 1) & (j == grid[1] - 1) & (kk == nk - 1))
            def _():
                comm.wait(cin, cout, sems)

    a_spec = (pl.BlockSpec((tk, tm), lambda i, j, kk: (kk, i)) if trans_a
              else pl.BlockSpec((tm, tk), lambda i, j, kk: (i, kk)))
    b_spec = (pl.BlockSpec((tn, tk), lambda i, j, kk: (j, kk)) if trans_b
              else pl.BlockSpec((tk, tn), lambda i, j, kk: (kk, j)))
    sem = ("arbitrary",) * 3 if comm else ("parallel", "parallel", "arbitrary")
    res = pl.pallas_call(
        body, name=name, grid=grid,
        in_specs=[a_spec, b_spec] + (comm.in_specs if comm else []),
        out_specs=[pl.BlockSpec((tm, tn), lambda i, j, kk: (i, j))] + (comm.out_specs if comm else []),
        out_shape=[jax.ShapeDtypeStruct((m, n), out_dtype)] + (comm.out_shape if comm else []),
        scratch_shapes=([] if nk == 1 else [pltpu.VMEM((tm, tn), F32)]) + (comm.scratch if comm else []),
        compiler_params=_params(sem, side_effects=bool(comm)),
    )(a, b, *(comm.operands if comm else []))
    return res if comm else res[0]


ROW_K = 6
X_K = 8


def _x_specs(nx, k):
    return [pl.BlockSpec((CHUNK, D_MODEL), functools.partial(lambda u, i: (jnp.clip(k * i + u - 1, 0, nx - 1), 0), u))
            for u in range(k)]


def _chunk_of_row(i, k):
    return k * i + lax.broadcasted_iota(jnp.int32, (k * CHUNK, 1), 0) // CHUNK


def _prenorm(x2d, h0, w):
    nx = x2d.shape[0] // CHUNK
    nc = nx + 2
    k = _tile(nc, (ROW_K, 3, 2, 1))

    def body(*refs):
        x_refs = refs[:k]
        h0_ref, w_ref, o_ref = refs[k:]
        i = pl.program_id(0)
        x = jnp.concatenate([r[...] for r in x_refs], axis=0)
        head = jnp.concatenate([h0_ref[...], x[CHUNK:, :]], axis=0)
        chunk = _chunk_of_row(i, k)
        h = jnp.where(i == 0, head, x) * (chunk <= nx).astype(F32)
        rstd = lax.rsqrt(jnp.mean(h * h, axis=-1, keepdims=True) + EPS)
        o_ref[...] = (h * rstd * w_ref[...]).astype(BF16)

    rows = k * CHUNK
    return pl.pallas_call(
        body, name="prenorm", grid=(nc // k,),
        in_specs=_x_specs(nx, k) + [pl.BlockSpec((CHUNK, D_MODEL), lambda i: (0, 0)), pl.BlockSpec((1, D_MODEL), lambda i: (0, 0))],
        out_specs=pl.BlockSpec((rows, D_MODEL), lambda i: (i, 0)),
        out_shape=jax.ShapeDtypeStruct((nc * CHUNK, D_MODEL), BF16),
        compiler_params=_params(("parallel",)),
    )(*([x2d] * k), h0, w)


CONV_COLS = 512
HALO = 8


def _conv_pre(ext, w, b):
    taps = [ext[HALO:, :]] + [pltpu.roll(ext, j, 0)[HALO:, :] for j in range(1, CONV_WIDTH)]
    acc = b + w[3:4, :] * taps[0]
    for j in range(1, CONV_WIDTH):
        acc = acc + w[3 - j:4 - j, :] * taps[j]
    return acc, taps


def _conv_bwd(dxbc, proj, conv_w, conv_b, dproj):
    lp = proj.shape[0]
    t = _tile(lp, (704, 384, 128, 64))
    hb = t // HALO
    nt = lp // t
    c0 = OFF_XBC // CONV_COLS

    def body(dx_ref, dxn_ref, u_ref, up_ref, un_ref, w_ref, b_ref, _, du_ref, dw_ref, db_ref):
        i = pl.program_id(1)
        w = w_ref[...]
        up = up_ref[...] * (i > 0).astype(F32)
        ext = jnp.concatenate([up, u_ref[...], un_ref[...]], axis=0)
        pre, taps = _conv_pre(ext, w, b_ref[...])
        dxn = dxn_ref[...] * (i < nt - 1).astype(F32)
        dxe = jnp.concatenate([dx_ref[...], dxn], axis=0)
        sg = _sigmoid(pre)
        dpre = dxe * sg * (1.0 + pre * (1.0 - sg))
        du = w[3:4, :] * dpre[:t, :]
        for j in range(1, CONV_WIDTH):
            du = du + w[3 - j:4 - j, :] * pltpu.roll(dpre, t + HALO - j, 0)[:t, :]
        du_ref[...] = du.astype(BF16)

        @pl.when(i == 0)
        def _():
            dw_ref[...] = jnp.zeros_like(dw_ref)
            db_ref[...] = jnp.zeros_like(db_ref)

        dp = dpre[:t, :]
        db_ref[...] += jnp.sum(dp, axis=0, keepdims=True)
        for j in range(CONV_WIDTH):
            dw_ref[3 - j:4 - j, :] += jnp.sum(dp * taps[j][:t, :], axis=0, keepdims=True)

    nxt = lambda i: jnp.minimum((i + 1) * hb, lp // HALO - 1)
    return pl.pallas_call(
        body, name="conv_bwd", grid=(D_CONV // CONV_COLS, nt),
        in_specs=[
            pl.BlockSpec((t, CONV_COLS), lambda j, i: (i, j)),
            pl.BlockSpec((HALO, CONV_COLS), lambda j, i: (nxt(i), j)),
            pl.BlockSpec((t, CONV_COLS), lambda j, i: (i, c0 + j)),
            pl.BlockSpec((HALO, CONV_COLS), lambda j, i: (jnp.maximum(i * hb - 1, 0), c0 + j)),
            pl.BlockSpec((HALO, CONV_COLS), lambda j, i: (nxt(i), c0 + j)),
            pl.BlockSpec((CONV_WIDTH, CONV_COLS), lambda j, i: (0, j)),
            pl.BlockSpec((1, CONV_COLS), lambda j, i: (0, j)),
            ANY,
        ],
        out_specs=[
            pl.BlockSpec((t, CONV_COLS), lambda j, i: (i, c0 + j)),
            pl.BlockSpec((CONV_WIDTH, CONV_COLS), lambda j, i: (0, j)),
            pl.BlockSpec((1, CONV_COLS), lambda j, i: (0, j)),
        ],
        out_shape=[
            jax.ShapeDtypeStruct((lp, NP), BF16),
            jax.ShapeDtypeStruct((CONV_WIDTH, D_CONV), F32),
            jax.ShapeDtypeStruct((1, D_CONV), F32),
        ],
        input_output_aliases={7: 0},
        compiler_params=_params(("parallel", "arbitrary")),
    )(dxbc, dxbc, proj, proj, proj, conv_w, conv_b, dproj)


def _swap_halves(t):
    w = t.shape[-1]
    lane = lax.broadcasted_iota(jnp.int32, t.shape, 1)
    return jnp.where((lane % HEAD_DIM) < HEAD_DIM // 2, pltpu.roll(t, w - HEAD_DIM // 2, 1),
                     pltpu.roll(t, HEAD_DIM // 2, 1))


def _act_fwd(proj, cos_t, sin_t, expand, dt_bias_pad):
    lp = proj.shape[0]
    t = _tile(lp, (384, 128, 64))

    def body(q_ref, k_ref, v_ref, dt_ref, cos_ref, sin_ref, ex_ref, bias_ref, qo_ref, ko_ref, vo_ref, dto_ref):
        i = pl.program_id(0)
        cos = cos_ref[...]
        sin = sin_ref[...]
        q = q_ref[...]
        qo_ref[...] = (q * jnp.tile(cos, (1, D_ATT // 128)) + _swap_halves(q) * jnp.tile(sin, (1, D_ATT // 128))).astype(BF16)
        k = k_ref[...]
        ko_ref[...] = (k * jnp.tile(cos, (1, D_KV // 128)) + _swap_halves(k) * jnp.tile(sin, (1, D_KV // 128))).astype(BF16)
        vo_ref[...] = v_ref[...].astype(BF16)
        raw = dt_ref[...] + bias_ref[...]
        sp = jnp.maximum(raw, 0.0) + jnp.log1p(jnp.exp(-jnp.abs(raw)))
        row = i * t + lax.broadcasted_iota(jnp.int32, sp.shape, 0)
        dto_ref[...] = _dot(jnp.where(row >= PAD_LEAD, sp, 0.0), ex_ref[...], NN, HIGHEST)

    return pl.pallas_call(
        body, name="act_fwd", grid=(lp // t,),
        in_specs=[
            pl.BlockSpec((t, D_ATT), lambda i: (i, OFF_Q // D_ATT)),
            pl.BlockSpec((t, D_KV), lambda i: (i, OFF_K // D_KV)),
            pl.BlockSpec((t, D_KV), lambda i: (i, OFF_V // D_KV)),
            pl.BlockSpec((t, 128), lambda i: (i, OFF_DT // 128)),
            pl.BlockSpec((t, 128), lambda i: (i, 0)),
            pl.BlockSpec((t, 128), lambda i: (i, 0)),
            pl.BlockSpec((128, D_SSD), lambda i: (0, 0)),
            pl.BlockSpec((1, 128), lambda i: (0, 0)),
        ],
        out_specs=[
            pl.BlockSpec((t, D_ATT), lambda i: (i, 0)),
            pl.BlockSpec((t, D_KV), lambda i: (i, 0)),
            pl.BlockSpec((t, D_KV), lambda i: (i, 0)),
            pl.BlockSpec((t, D_SSD), lambda i: (i, 0)),
        ],
        out_shape=[
            jax.ShapeDtypeStruct((lp, D_ATT), BF16),
            jax.ShapeDtypeStruct((lp, D_KV), BF16),
            jax.ShapeDtypeStruct((lp, D_KV), BF16),
            jax.ShapeDtypeStruct((lp, D_SSD), F32),
        ],
        compiler_params=_params(("parallel",)),
    )(proj, proj, proj, proj, cos_t, sin_t, expand, dt_bias_pad)


def _act_bwd(dqr, dkr, dv, dg, ddt_part, cos_t, sin_t, reduce_t, dproj):
    lp = dqr.shape[0]
    t = _tile(lp, (384, 128, 64))

    def body(dq_ref, dk_ref, dv_ref, dg_ref, ddt_ref, cos_ref, sin_ref, red_ref, _, o_ref, db_ref):
        i = pl.program_id(0)
        cos = cos_ref[...]
        sin = sin_ref[...]
        dq = dq_ref[...]
        dq = dq * jnp.tile(cos, (1, D_ATT // 128)) + _swap_halves(dq * jnp.tile(sin, (1, D_ATT // 128)))
        dk = dk_ref[...]
        dk = dk * jnp.tile(cos, (1, D_KV // 128)) + _swap_halves(dk * jnp.tile(sin, (1, D_KV // 128)))
        ddt = _dot(ddt_ref[...], red_ref[...], NN, HIGHEST)
        o_ref[...] = jnp.concatenate(
            [dq.astype(BF16), dg_ref[...].astype(BF16), dk.astype(BF16), dv_ref[...].astype(BF16), ddt.astype(BF16),
             jnp.zeros((t, NP - OFF_DT - 128), BF16)], axis=1)

        @pl.when(i == 0)
        def _():
            db_ref[...] = jnp.zeros_like(db_ref)

        db_ref[...] += jnp.sum(ddt, axis=0, keepdims=True)

    return pl.pallas_call(
        body, name="act_bwd", grid=(lp // t,),
        in_specs=[
            pl.BlockSpec((t, D_ATT), lambda i: (i, 0)),
            pl.BlockSpec((t, D_KV), lambda i: (i, 0)),
            pl.BlockSpec((t, D_KV), lambda i: (i, 0)),
            pl.BlockSpec((t, D_ATT), lambda i: (i, 0)),
            pl.BlockSpec((t, D_SSD), lambda i: (i, 0)),
            pl.BlockSpec((t, 128), lambda i: (i, 0)),
            pl.BlockSpec((t, 128), lambda i: (i, 0)),
            pl.BlockSpec((D_SSD, 128), lambda i: (0, 0)),
            ANY,
        ],
        out_specs=[pl.BlockSpec((t, TAIL_W), lambda i: (i, OFF_Q // TAIL_W)), pl.BlockSpec((1, 128), lambda i: (0, 0))],
        out_shape=[jax.ShapeDtypeStruct((lp, NP), BF16), jax.ShapeDtypeStruct((1, 128), F32)],
        input_output_aliases={8: 0},
        compiler_params=_params(("arbitrary",)),
    )(dqr, dkr, dv, dg, ddt_part, cos_t, sin_t, reduce_t, dproj)


def _cs_row(cs):
    row = lax.broadcasted_iota(jnp.int32, cs.shape, 0)
    lane = lax.broadcasted_iota(jnp.int32, cs.shape, 1)
    return jnp.sum(jnp.where(row == lane % HEAD_DIM, cs, 0.0), axis=0, keepdims=True)


def _ssd_fwd(proj, conv_w, conv_b, dt_rep, a_rep, dsk_rep, wn, tri):
    lp = proj.shape[0]
    nc = lp // CHUNK
    assert SSD_GPS == N_GROUPS
    gw = SSD_GPS * GROUP_W
    cps = _tile(nc, (SSD_CPS, 1))

    def body(rawa_ref, rawb_ref, cw_ref, cb_ref, dt_ref, z_ref, a_ref, dsk_ref, wn_ref, tri_ref,
             yn_ref, ytot_ref, hprev_ref, xbc_ref, h_scr, tail_scr):
        @pl.when(pl.program_id(1) == 0)
        def _():
            h_scr[...] = jnp.zeros_like(h_scr)
            tail_scr[...] = jnp.zeros_like(tail_scr)

        for sub in range(cps):
            rs = slice(sub * CHUNK, (sub + 1) * CHUNK)
            raw = jnp.concatenate([rawa_ref[rs, :], rawb_ref[rs, :]], axis=1)
            pre, _ = _conv_pre(jnp.concatenate([tail_scr[...], raw], axis=0), cw_ref[...], cb_ref[...])
            tail_scr[...] = raw[CHUNK - HALO:, :]
            xbc_ref[rs, :] = pre * _sigmoid(pre)

        G = range(SSD_GPS)
        colsl = [slice(gi * GROUP_W, (gi + 1) * GROUP_W) for gi in G]
        rows4 = lax.broadcasted_iota(jnp.int32, (GROUP_W, GROUP_W), 0) // HEAD_DIM
        cols4 = lax.broadcasted_iota(jnp.int32, (GROUP_W, GROUP_W), 1) // HEAD_DIM
        lrow = lax.broadcasted_iota(jnp.int32, (CHUNK, GROUP_W), 0)
        lcol = lax.broadcasted_iota(jnp.int32, (CHUNK, GROUP_W), 1) % HEAD_DIM

        def one_chunk(sub):
            rs = slice(sub * CHUNK, (sub + 1) * CHUNK)
            xbc = [xbc_ref[rs, gi * XBC_BLK:(gi + 1) * XBC_BLK] for gi in G]
            dt = [dt_ref[rs, colsl[gi]] for gi in G]
            xs = [xbc[gi][:, :GROUP_W] for gi in G]
            b = [xbc[gi][:, GROUP_W:GROUP_W + D_STATE].astype(BF16) for gi in G]
            c = [xbc[gi][:, GROUP_W + D_STATE:].astype(BF16) for gi in G]
            hprev = [h_scr[gi] for gi in G]
            cs = [_dot(tri_ref[...], dt[gi] * a_ref[:, colsl[gi]], NN, HIGHEST) for gi in G]
            yoff = [_dot(c[gi], hprev[gi].astype(BF16)) for gi in G]
            cs_t = [_cs_row(cs[gi]) for gi in G]
            xdt = [xs[gi] * dt[gi] for gi in G]
            cs_last = [cs[gi][CHUNK - 1:CHUNK, :] for gi in G]
            st = [_dot(b[gi], (xdt[gi] * jnp.exp(cs_last[gi] - cs[gi])).astype(BF16), TN) for gi in G]
            cb4 = [_dot(c[gi], jnp.concatenate([b[gi]] * HEADS_PER_GROUP, axis=0), NT) for gi in G]
            m = [(cb4[gi] * jnp.exp(jnp.where(lrow >= lcol, cs[gi] - cs_t[gi], -jnp.inf))).astype(BF16) for gi in G]
            xbd = [jnp.where(rows4 == cols4, jnp.concatenate([xdt[gi].astype(BF16)] * HEADS_PER_GROUP, axis=0), 0.0)
                   for gi in G]
            ydiag = [_dot(m[gi], xbd[gi]) for gi in G]
            for gi in G:
                cols = colsl[gi]
                ytot = ydiag[gi] + yoff[gi] * jnp.exp(cs[gi]) + dsk_ref[:, cols] * xs[gi]
                z = z_ref[rs, cols]
                gz = ytot * (z * _sigmoid(z))
                rstd = lax.rsqrt(jnp.mean(gz * gz, axis=-1, keepdims=True) + EPS)
                hprev_ref[sub, gi] = hprev[gi]
                h_scr[gi] = hprev[gi] * jnp.exp(cs_last[gi]) + st[gi]
                ytot_ref[rs, cols] = ytot
                yn_ref[rs, cols] = (gz * rstd * wn_ref[:, cols]).astype(BF16)

        for sub in range(cps):
            one_chunk(sub)

    vec = pl.BlockSpec((1, gw), lambda g, c: (0, g))
    blk = pl.BlockSpec((cps * CHUNK, gw), lambda g, c: (c, g))
    half = D_CONV // 2
    return pl.pallas_call(
        body, name="ssd_fwd", grid=(N_GROUPS // SSD_GPS, nc // cps),
        in_specs=[
            pl.BlockSpec((cps * CHUNK, half), lambda g, c: (c, OFF_XBC // half)),
            pl.BlockSpec((cps * CHUNK, half), lambda g, c: (c, OFF_XBC // half + 1)),
            pl.BlockSpec((CONV_WIDTH, D_CONV), lambda g, c: (0, 0)),
            pl.BlockSpec((1, D_CONV), lambda g, c: (0, 0)),
            blk, blk, vec, vec, vec,
            pl.BlockSpec((CHUNK, CHUNK), lambda g, c: (0, 0)),
        ],
        out_specs=[blk, blk, pl.BlockSpec((cps, SSD_GPS, D_STATE, GROUP_W), lambda g, c: (c, g, 0, 0)),
                   pl.BlockSpec((cps * CHUNK, D_CONV), lambda g, c: (c, 0))],
        out_shape=[
            jax.ShapeDtypeStruct((lp, D_MIX), BF16),
            jax.ShapeDtypeStruct((lp, D_SSD), F32),
            jax.ShapeDtypeStruct((nc, N_GROUPS, D_STATE, GROUP_W), F32),
            jax.ShapeDtypeStruct((lp, D_CONV), F32),
        ],
        scratch_shapes=[pltpu.VMEM((SSD_GPS, D_STATE, GROUP_W), F32), pltpu.VMEM((HALO, D_CONV), F32)],
        compiler_params=_params(("arbitrary", "arbitrary")),
    )(proj, proj, conv_w, conv_b, dt_rep, proj, a_rep, dsk_rep, wn, tri)


def _ssd_bwd(dmix, ytot, proj, xbc, dt_rep, hprev, a_rep, dsk_rep, wn, tri, comm):
    lp = xbc.shape[0]
    nc = lp // CHUNK
    gps = SSD_GPS
    gw = gps * GROUP_W
    cps = _tile(nc, (SSD_CPS, 1))
    ncm = comm.n
    n_in, n_out = 10, 6
    grid = (N_GROUPS // gps, nc // cps)

    def all_groups(refs):
        (dyn_ref, ytot_ref, z_ref, xbc_ref, dt_ref, hprev_ref, a_ref, dsk_ref, wn_ref, tri_ref,
         dz_ref, dxbc_ref, ddt_ref, dd_ref, da_ref, dwn_ref, dh_scr) = refs
        G = range(gps)
        H = range(HEADS_PER_GROUP)
        cl = [slice(gi * GROUP_W, (gi + 1) * GROUP_W) for gi in G]
        hl = [slice(r * HEAD_DIM, (r + 1) * HEAD_DIM) for r in H]
        tri = tri_ref[...]
        xbc = [xbc_ref[:, gi * XBC_BLK:(gi + 1) * XBC_BLK] for gi in G]
        dt = [dt_ref[:, cl[gi]] for gi in G]
        a = [a_ref[:, cl[gi]] for gi in G]
        xs = [xbc[gi][:, :GROUP_W] for gi in G]
        bbf = [xbc[gi][:, GROUP_W:GROUP_W + D_STATE].astype(BF16) for gi in G]
        cbf = [xbc[gi][:, GROUP_W + D_STATE:].astype(BF16) for gi in G]
        hprev = [hprev_ref[gi] for gi in G]
        hbf = [hprev[gi].astype(BF16) for gi in G]
        dhn = [dh_scr[gi] for gi in G]
        dhnb = [dhn[gi].astype(BF16) for gi in G]
        cs = [_dot(tri, dt[gi] * a[gi], NN, HIGHEST) for gi in G]
        g = [_dot(cbf[gi], hbf[gi]) for gi in G]
        dxw = [_dot(bbf[gi], dhnb[gi]) for gi in G]
        cs_t = [_cs_row(cs[gi]) for gi in G]
        dy = []
        for gi in G:
            ytot = ytot_ref[:, cl[gi]]
            z = z_ref[:, cl[gi]]
            dyn = dyn_ref[:, cl[gi]]
            sz = _sigmoid(z)
            silu_z = z * sz
            gz = ytot * silu_z
            rstd = lax.rsqrt(jnp.mean(gz * gz, axis=-1, keepdims=True) + EPS)
            xhat = gz * rstd
            dwn_ref[:, cl[gi]] += jnp.sum(dyn * xhat, axis=0, keepdims=True)
            dxhat = dyn * wn_ref[:, cl[gi]]
            dgz = rstd * (dxhat - xhat * jnp.mean(dxhat * xhat, axis=-1, keepdims=True))
            dy.append(dgz * silu_z)
            dz_ref[:, cl[gi]] = (dgz * ytot * (sz * (1.0 + z * (1.0 - sz)))).astype(BF16)
            dd_ref[:, cl[gi]] += jnp.sum(dy[gi] * xs[gi], axis=0, keepdims=True)
        xdt = [xs[gi] * dt[gi] for gi in G]
        e = [jnp.exp(cs[gi]) for gi in G]
        cs_last = [cs[gi][CHUNK - 1:CHUNK, :] for gi in G]
        dte = [jnp.exp(cs_last[gi] - cs[gi]) for gi in G]
        cd = [jnp.exp(cs_last[gi]) for gi in G]
        dgb = [(dy[gi] * e[gi]).astype(BF16) for gi in G]
        dyb = [dy[gi].astype(BF16) for gi in G]
        xdtb = [xdt[gi].astype(BF16) for gi in G]
        dc = [_dot(dgb[gi], hbf[gi], NT) for gi in G]
        dhprev = [_dot(cbf[gi], dgb[gi], TN) for gi in G]
        db = [_dot((xdt[gi] * dte[gi]).astype(BF16), dhnb[gi], NT) for gi in G]
        row = lax.broadcasted_iota(jnp.int32, (CHUNK, CHUNK), 0)
        causal = row >= lax.broadcasted_iota(jnp.int32, (CHUNK, CHUNK), 1)
        cb = [_dot(cbf[gi], bbf[gi], NT) for gi in G]
        dm = [[_dot(dyb[gi][:, hl[r]], xdtb[gi][:, hl[r]], NT) for r in H] for gi in G]
        mb, dseg, dcbb = [], [], []
        for gi in G:
            mb.append([])
            dseg.append([])
            dcb = None
            for r in H:
                seg = cs[gi][:, r * HEAD_DIM:r * HEAD_DIM + 1] - cs_t[gi][:, hl[r]]
                lm = jnp.exp(jnp.where(causal, seg, -jnp.inf))
                m = cb[gi] * lm
                mb[gi].append(m.astype(BF16))
                dseg[gi].append(dm[gi][r] * m)
                dcb = dm[gi][r] * lm if r == 0 else dcb + dm[gi][r] * lm
            dcbb.append(dcb.astype(BF16))
        dxdt_diag = [[_dot(mb[gi][r], dyb[gi][:, hl[r]], TN) for r in H] for gi in G]
        ones = jnp.ones((CHUNK, HEAD_DIM), F32)
        colsum = [[_dot(dseg[gi][r], ones, TN, HIGHEST) for r in H] for gi in G]
        dc2 = [_dot(dcbb[gi], bbf[gi]) for gi in G]
        db2 = [_dot(dcbb[gi], cbf[gi], TN) for gi in G]
        dcs = []
        for gi in G:
            t_dte = dxw[gi] * xdt[gi] * dte[gi]
            dcs_last = (jnp.sum(dhn[gi] * hprev[gi], axis=0, keepdims=True) * cd[gi]
                        + jnp.sum(t_dte, axis=0, keepdims=True))
            diag = jnp.concatenate(
                [(jnp.sum(dseg[gi][r], axis=1, keepdims=True) - colsum[gi][r]) * (1.0 / HEAD_DIM) for r in H], axis=1)
            d = dy[gi] * g[gi] * e[gi] - t_dte + diag
            row = lax.broadcasted_iota(jnp.int32, d.shape, 0)
            dcs.append(d + jnp.where(row == CHUNK - 1, dcs_last, 0.0))
        dda = [_dot(tri, dcs[gi], TN, HIGHEST) for gi in G]
        for gi in G:
            dxdt = dxw[gi] * dte[gi] + jnp.concatenate(dxdt_diag[gi], axis=1)
            da_ref[:, cl[gi]] += jnp.sum(dda[gi] * dt[gi], axis=0, keepdims=True)
            ddt = dda[gi] * a[gi] + dxdt * xs[gi]
            dxs = dsk_ref[:, cl[gi]] * dy[gi] + dxdt * dt[gi]
            ddt_ref[:, cl[gi]] = ddt * (1.0 - jnp.exp(-dt[gi]))
            dxbc_ref[:, gi * XBC_BLK:(gi + 1) * XBC_BLK] = jnp.concatenate(
                [dxs, db[gi] + db2[gi], dc[gi] + dc2[gi]], axis=1)
            dh_scr[gi] = dhprev[gi] + dhn[gi] * cd[gi]

    def body(*refs):
        ins = refs[:n_in]
        cin = refs[n_in:n_in + ncm]
        outs = refs[n_in + ncm:n_in + ncm + n_out]
        cout = refs[n_in + ncm + n_out:n_in + 2 * ncm + n_out]
        dh_scr = refs[n_in + 2 * ncm + n_out]
        sems = refs[n_in + 2 * ncm + n_out + 1:]
        g, c = pl.program_id(0), pl.program_id(1)

        @pl.when((g == 0) & (c == 0))
        def _():
            comm.start(cin, cout, sems)

        @pl.when(c == 0)
        def _():
            dh_scr[...] = jnp.zeros_like(dh_scr)
            for ref in outs[3:]:
                ref[...] = jnp.zeros_like(ref)

        for sub in reversed(range(cps)):
            rs = pl.ds(sub * CHUNK, CHUNK)
            chunk_ins = tuple(r.at[rs] for r in ins[:5]) + (ins[5].at[sub],) + ins[6:]
            chunk_outs = tuple(r.at[rs] for r in outs[:3]) + outs[3:]
            all_groups(chunk_ins + chunk_outs + (dh_scr,))

        @pl.when((g == grid[0] - 1) & (c == grid[1] - 1))
        def _():
            comm.wait(cin, cout, sems)

    rev = lambda c: grid[1] - 1 - c
    vec = pl.BlockSpec((1, gw), lambda g, c: (0, g))
    blk = pl.BlockSpec((cps * CHUNK, gw), lambda g, c: (rev(c), g))
    xblk = pl.BlockSpec((cps * CHUNK, gps * XBC_BLK), lambda g, c: (rev(c), g))
    res = pl.pallas_call(
        body, name="ssd_bwd", grid=grid,
        in_specs=[blk, blk, blk, xblk, blk,
                  pl.BlockSpec((cps, gps, D_STATE, GROUP_W), lambda g, c: (rev(c), g, 0, 0)),
                  vec, vec, vec,
                  pl.BlockSpec((CHUNK, CHUNK), lambda g, c: (0, 0))] + comm.in_specs,
        out_specs=[blk, xblk, blk, vec, vec, vec] + comm.out_specs,
        out_shape=[
            jax.ShapeDtypeStruct((lp, NP), BF16),
            jax.ShapeDtypeStruct((lp, D_CONV), F32),
            jax.ShapeDtypeStruct((lp, D_SSD), F32),
            jax.ShapeDtypeStruct((1, D_SSD), F32),
            jax.ShapeDtypeStruct((1, D_SSD), F32),
            jax.ShapeDtypeStruct((1, D_SSD), F32),
        ] + comm.out_shape,
        scratch_shapes=[pltpu.VMEM((gps, D_STATE, GROUP_W), F32)] + comm.scratch,
        compiler_params=_params(("arbitrary", "arbitrary"), side_effects=True),
    )(dmix, ytot, proj, xbc, dt_rep, hprev, a_rep, dsk_rep, wn, tri, *comm.operands)
    return res


def _stack_heads(t, h):
    return jnp.concatenate([t[:, (REP * h + r) * HEAD_DIM:(REP * h + r + 1) * HEAD_DIM] for r in range(REP)], axis=0)


def _band(t2, t1, t0, h):
    sl = slice(h * HEAD_DIM, (h + 1) * HEAD_DIM)
    return jnp.concatenate([t2[:, sl], t1[:, sl], t0[:, sl]], axis=0)


def _attn_probs(s, sink, qc):
    s = s * (HEAD_DIM ** -0.5)
    key_abs = (qc - WINDOW_CHUNKS) * CHUNK + lax.broadcasted_iota(jnp.int32, s.shape, 1)
    s = jnp.where(key_abs >= PAD_LEAD, s, -jnp.inf)
    m = jnp.maximum(jnp.max(s, axis=-1, keepdims=True), sink)
    p = jnp.exp(s - m)
    ps = jnp.exp(sink - m)
    denom = jnp.sum(p, axis=-1, keepdims=True) + ps
    return p / denom, ps / denom


ATT_QC_FWD = 6
ATT_QC_BWD = 2


def _kv_specs(width, newest_chunk_of, kc):
    return [pl.BlockSpec((CHUNK, width), functools.partial(lambda j, p: (jnp.maximum(newest_chunk_of(p) - j, 0), 0), j))
            for j in range(kc - 1, -1, -1)]


def _attn_fwd(qr, kr, vb, proj, sink_stack, mix):
    lp = qr.shape[0]
    nc = lp // CHUNK
    qn = _tile(nc, (ATT_QC_FWD, 2, 1))
    kn = WINDOW_CHUNKS + qn
    qrows = qn * CHUNK

    def body(q_ref, *rest):
        k_refs, v_refs = rest[:kn], rest[kn:2 * kn]
        g_ref, sink_ref, _, att_ref, mix_ref = rest[2 * kn:]
        p = pl.program_id(0)
        q = q_ref[...]
        ks = [r[...] for r in k_refs]
        vs = [r[...] for r in v_refs]
        units = [(u, h) for u in range(qn) for h in range(KV_HEADS)]
        s = [_dot(_stack_heads(q[u * CHUNK:(u + 1) * CHUNK, :], h), _band(*ks[u:u + 3], h), NT) for u, h in units]
        vbh = [_band(*vs[u:u + 3], h) for u, h in units]
        pn = [_attn_probs(s[i], sink_ref[h], qn * p + u)[0].astype(BF16) for i, (u, h) in enumerate(units)]
        o = [_dot(pn[i], vbh[i]) for i in range(len(units))]
        att = jnp.concatenate(
            [jnp.concatenate([o[u * KV_HEADS + h][r * CHUNK:(r + 1) * CHUNK, :] for h in range(KV_HEADS) for r in range(REP)],
                             axis=1) for u in range(qn)], axis=0)
        att_ref[...] = att
        g = g_ref[...]
        mix_ref[...] = (att * (g * _sigmoid(g))).astype(BF16)

    newest = lambda p: qn * p + qn - 1
    return pl.pallas_call(
        body, name="attn_fwd", grid=(nc // qn,),
        in_specs=[pl.BlockSpec((qrows, D_ATT), lambda p: (p, 0))] + _kv_specs(D_KV, newest, kn) + _kv_specs(D_KV, newest, kn) + [
            pl.BlockSpec((qrows, D_ATT), lambda p: (p, OFF_G // D_ATT)),
            pl.BlockSpec((KV_HEADS, REP * CHUNK, 1), lambda p: (0, 0, 0)),
            ANY,
        ],
        out_specs=[pl.BlockSpec((qrows, D_ATT), lambda p: (p, 0)),
                   pl.BlockSpec((qrows, D_ATT), lambda p: (p, D_SSD // D_ATT))],
        out_shape=[jax.ShapeDtypeStruct((lp, D_ATT), F32), jax.ShapeDtypeStruct((lp, D_MIX), BF16)],
        input_output_aliases={2 * kn + 3: 1},
        compiler_params=_params(("parallel",)),
    )(qr, *([kr] * kn), *([vb] * kn), proj, sink_stack, mix)


def _attn_bwd(qr, kr, vb, att, proj, dmix, sink_stack):
    lp = qr.shape[0]
    nc = lp // CHUNK
    qn = ATT_QC_BWD
    kn = WINDOW_CHUNKS + qn
    assert nc % qn == 0 and WINDOW_CHUNKS % qn == 0
    steps = nc // qn
    qrows = qn * CHUNK
    wrows = kn * CHUNK

    def body(q_ref, *rest):
        k_refs, v_refs = rest[:kn], rest[kn:2 * kn]
        (att_ref, g_ref, do_ref, sink_ref, dq_ref, dk_ref, dv_ref, dg_ref, dsink_ref, dk_acc, dv_acc) = rest[2 * kn:]
        step = pl.program_id(0)

        @pl.when(step == 0)
        def _():
            dk_acc[...] = jnp.zeros_like(dk_acc)
            dv_acc[...] = jnp.zeros_like(dv_acc)
            dsink_ref[...] = jnp.zeros_like(dsink_ref)

        q = q_ref[...]
        ks = [r[...] for r in k_refs]
        vs = [r[...] for r in v_refs]
        att = att_ref[...]
        g = g_ref[...]
        dog = do_ref[...]
        sg = _sigmoid(g)
        dg_ref[...] = dog * att * (sg * (1.0 + g * (1.0 - sg)))
        do = dog * (g * sg)
        units = [(u, h) for u in range(qn) for h in range(KV_HEADS)]
        n = range(len(units))
        rows = [slice(u * CHUNK, (u + 1) * CHUNK) for u in range(qn)]
        qs = [_stack_heads(q[rows[u], :], h) for u, h in units]
        kb = [_band(*ks[u:u + 3], h) for u, h in units]
        vbh = [_band(*vs[u:u + 3], h) for u, h in units]
        dos = [_stack_heads(do[rows[u], :], h) for u, h in units]
        dosb = [dos[i].astype(BF16) for i in n]
        s = [_dot(qs[i], kb[i], NT) for i in n]
        dp = [_dot(dosb[i], vbh[i], NT) for i in n]
        ds, pnb, dsink = [], [], []
        for i, (u, h) in enumerate(units):
            pn, psink = _attn_probs(s[i], sink_ref[h], qn * jnp.minimum(step, steps - 1) + u)
            delta = jnp.sum(dos[i] * _stack_heads(att[rows[u], :], h), axis=-1, keepdims=True)
            ds.append((pn * (dp[i] - delta)).astype(BF16))
            pnb.append(pn.astype(BF16))
            dsink.append(-(psink * delta))
        dqs = [_dot(ds[i], kb[i]) for i in n]
        dks = [_dot(ds[i], qs[i], TN) for i in n]
        dvs = [_dot(pnb[i], dosb[i], TN) for i in n]
        dq_ref[...] = jnp.concatenate(
            [jnp.concatenate([dqs[u * KV_HEADS + h][r * CHUNK:(r + 1) * CHUNK, :]
                              for h in range(KV_HEADS) for r in range(REP)], axis=1) for u in range(qn)],
            axis=0) * (HEAD_DIM ** -0.5)
        @pl.when(step < steps)
        def _():
            for i, (u, h) in enumerate(units):
                dsink_ref[h] += dsink[i]
            for u in range(qn):
                band = slice(u * CHUNK, u * CHUNK + BAND)
                dk_acc[band, :] += jnp.concatenate(dks[u * KV_HEADS:(u + 1) * KV_HEADS], axis=1) * (HEAD_DIM ** -0.5)
                dv_acc[band, :] += jnp.concatenate(dvs[u * KV_HEADS:(u + 1) * KV_HEADS], axis=1)

        dk_ref[...] = dk_acc[0:qrows, :]
        dv_ref[...] = dv_acc[0:qrows, :]
        for acc in (dk_acc, dv_acc):
            rest_rows = acc[qrows:wrows, :]
            acc[0:wrows - qrows, :] = rest_rows
            acc[wrows - qrows:wrows, :] = jnp.zeros((qrows, D_KV), F32)

    qp = lambda p: jnp.minimum(p, steps - 1)
    newest = lambda p: qn * qp(p) + qn - 1
    qblk = pl.BlockSpec((qrows, D_ATT), lambda p: (qp(p), 0))
    oldest = pl.BlockSpec((qrows, D_KV), lambda p: (jnp.maximum(p - 1, 0), 0))
    return pl.pallas_call(
        body, name="attn_bwd", grid=(steps + 1,),
        in_specs=[qblk] + _kv_specs(D_KV, newest, kn) + _kv_specs(D_KV, newest, kn) + [
            qblk,
            pl.BlockSpec((qrows, D_ATT), lambda p: (qp(p), OFF_G // D_ATT)),
            pl.BlockSpec((qrows, D_ATT), lambda p: (qp(p), D_SSD // D_ATT)),
            pl.BlockSpec((KV_HEADS, REP * CHUNK, 1), lambda p: (0, 0, 0)),
        ],
        out_specs=[qblk, oldest, oldest, qblk, pl.BlockSpec((KV_HEADS, REP * CHUNK, 1), lambda p: (0, 0, 0))],
        out_shape=[
            jax.ShapeDtypeStruct((lp, D_ATT), F32),
            jax.ShapeDtypeStruct((lp, D_KV), F32),
            jax.ShapeDtypeStruct((lp, D_KV), F32),
            jax.ShapeDtypeStruct((lp, D_ATT), F32),
            jax.ShapeDtypeStruct((KV_HEADS, REP * CHUNK, 1), F32),
        ],
        scratch_shapes=[pltpu.VMEM((wrows, D_KV), F32), pltpu.VMEM((wrows, D_KV), F32)],
        compiler_params=_params(("arbitrary",)),
    )(qr, *([kr] * kn), *([vb] * kn), att, proj, dmix, sink_stack)


def _post_loss(out, x2d, target, w):
    lp = out.shape[0]
    nc = lp // CHUNK
    nx = x2d.shape[0] // CHUNK

    k = _tile(nc, (ROW_K, 3, 2, 1))

    def body(o_ref, *rest):
        x_refs, t_refs = rest[:k], rest[k:2 * k]
        w_ref, do_ref, dy_ref, gw_ref, loss_ref = rest[2 * k:]
        i = pl.program_id(0)

        @pl.when(i == 0)
        def _():
            gw_ref[...] = jnp.zeros_like(gw_ref)
            loss_ref[...] = jnp.zeros_like(loss_ref)

        o = o_ref[...]
        w = w_ref[...]
        rstd = lax.rsqrt(jnp.mean(o * o, axis=-1, keepdims=True) + EPS)
        xhat = o * rstd
        x = jnp.concatenate([r[...] for r in x_refs], axis=0)
        t = jnp.concatenate([r[...] for r in t_refs], axis=0)
        chunk = _chunk_of_row(i, k)
        err = (x + xhat * w - t) * ((chunk > 0) & (chunk <= nx)).astype(F32)
        loss_ref[...] += 0.5 * jnp.sum(jnp.mean(err * err, axis=-1, keepdims=True), axis=0, keepdims=True)
        dy = err * (1.0 / D_MODEL)
        dy_ref[...] = dy
        gw_ref[...] += jnp.sum(dy * xhat, axis=0, keepdims=True)
        dxhat = dy * w
        do_ref[...] = (rstd * (dxhat - xhat * jnp.mean(dxhat * xhat, axis=-1, keepdims=True))).astype(BF16)

    row = pl.BlockSpec((k * CHUNK, D_MODEL), lambda i: (i, 0))
    return pl.pallas_call(
        body, name="post_loss", grid=(nc // k,),
        in_specs=[row] + _x_specs(nx, k) + _x_specs(nx, k) + [pl.BlockSpec((1, D_MODEL), lambda i: (0, 0))],
        out_specs=[row, row, pl.BlockSpec((1, D_MODEL), lambda i: (0, 0)), pl.BlockSpec((1, 128), lambda i: (0, 0))],
        out_shape=[
            jax.ShapeDtypeStruct((lp, D_MODEL), BF16),
            jax.ShapeDtypeStruct((lp, D_MODEL), F32),
            jax.ShapeDtypeStruct((1, D_MODEL), F32),
            jax.ShapeDtypeStruct((1, 128), F32),
        ],
        compiler_params=_params(("arbitrary",)),
    )(out, *([x2d] * k), *([target] * k), w)


def _prenorm_bwd(dhn, x2d, h0, dy, w):
    nx = x2d.shape[0] // CHUNK
    k = _tile(nx, (X_K, 4, 2, 1))
    rows = k * CHUNK

    def backward(h, dhn, w):
        rstd = lax.rsqrt(jnp.mean(h * h, axis=-1, keepdims=True) + EPS)
        xhat = h * rstd
        dxhat = dhn * w
        dh = rstd * (dxhat - xhat * jnp.mean(dxhat * xhat, axis=-1, keepdims=True))
        return dh, jnp.sum(dhn * xhat, axis=0, keepdims=True)

    def body(*refs):
        dhn_refs, dy_refs = refs[:k], refs[k:2 * k]
        x_ref, w_ref, gx_ref, gw_ref = refs[2 * k:]

        @pl.when(pl.program_id(0) == 0)
        def _():
            gw_ref[...] = jnp.zeros_like(gw_ref)

        dh, gw = backward(x_ref[...], jnp.concatenate([r[...] for r in dhn_refs], axis=0), w_ref[...])
        gx_ref[...] = dh + jnp.concatenate([r[...] for r in dy_refs], axis=0)
        gw_ref[...] += gw

    def body_meta(dhn_ref, h0_ref, w_ref, d0_ref, gw_ref):
        d0_ref[...], gw_ref[...] = backward(h0_ref[...], dhn_ref[...], w_ref[...])

    chunk_specs = [pl.BlockSpec((CHUNK, D_MODEL), functools.partial(lambda u, i: (k * i + 1 + u, 0), u)) for u in range(k)]
    first = pl.BlockSpec((CHUNK, D_MODEL), lambda i: (0, 0))
    vec = pl.BlockSpec((1, D_MODEL), lambda i: (0, 0))
    wide = pl.BlockSpec((rows, D_MODEL), lambda i: (i, 0))
    gx, gw_x = pl.pallas_call(
        body, name="prenorm_bwd", grid=(nx // k,),
        in_specs=chunk_specs + chunk_specs + [wide, vec],
        out_specs=[wide, vec],
        out_shape=[jax.ShapeDtypeStruct((nx * CHUNK, D_MODEL), F32), jax.ShapeDtypeStruct((1, D_MODEL), F32)],
        compiler_params=_params(("arbitrary",)),
    )(*([dhn] * k), *([dy] * k), x2d, w)
    d0, gw_0 = pl.pallas_call(
        body_meta, name="prenorm_bwd_meta", grid=(1,),
        in_specs=[first, first, vec], out_specs=[first, vec],
        out_shape=[jax.ShapeDtypeStruct((CHUNK, D_MODEL), F32), jax.ShapeDtypeStruct((1, D_MODEL), F32)],
        compiler_params=_params(("arbitrary",)),
    )(dhn, h0, w)
    return gx, d0, gw_x + gw_0


def _adamw(slabs, w, m, v, name):
    rows, cols = w.shape
    tr = _tile(rows, (256, 128, 64, 16, 8))
    c1 = 1.0 - ADAM_B1 ** ADAM_STEP
    c2 = 1.0 - ADAM_B2 ** ADAM_STEP

    def body(s_ref, w_ref, m_ref, v_ref, g_ref, d_ref, mo_ref, vo_ref):
        g = s_ref[0].astype(F32)
        for k in range(1, slabs.shape[0]):
            g = g + s_ref[k].astype(F32)
        w = w_ref[...]
        m = ADAM_B1 * m_ref[...] + (1.0 - ADAM_B1) * g
        v = ADAM_B2 * v_ref[...] + (1.0 - ADAM_B2) * (g * g)
        g_ref[...] = g
        mo_ref[...] = m
        vo_ref[...] = v
        d_ref[...] = -ADAM_LR * ((m / c1) / (jnp.sqrt(v / c2) + ADAM_EPS) + ADAM_WD * w)

    blk = pl.BlockSpec((tr, cols), lambda i: (i, 0))
    shape = jax.ShapeDtypeStruct((rows, cols), F32)
    return pl.pallas_call(
        body, name=name, grid=(rows // tr,),
        in_specs=[pl.BlockSpec((slabs.shape[0], tr, cols), lambda i: (0, i, 0)), blk, blk, blk],
        out_specs=[blk, blk, blk, blk],
        out_shape=[shape, shape, shape, shape],
        compiler_params=_params(("parallel",)),
    )(slabs, w, m, v)


def _perm_xbc(a):
    lead = a.shape[:-1]
    xs = a[..., :D_SSD].reshape(lead + (N_GROUPS, GROUP_W))
    b = a[..., D_SSD:D_SSD + N_GROUPS * D_STATE].reshape(lead + (N_GROUPS, D_STATE))
    c = a[..., D_SSD + N_GROUPS * D_STATE:].reshape(lead + (N_GROUPS, D_STATE))
    return jnp.concatenate([xs, b, c], axis=-1).reshape(lead + (D_CONV,))


def _unperm_xbc(a):
    lead = a.shape[:-1]
    t = a.reshape(lead + (N_GROUPS, XBC_BLK))
    xs = t[..., :GROUP_W].reshape(lead + (D_SSD,))
    b = t[..., GROUP_W:GROUP_W + D_STATE].reshape(lead + (N_GROUPS * D_STATE,))
    c = t[..., GROUP_W + D_STATE:].reshape(lead + (N_GROUPS * D_STATE,))
    return jnp.concatenate([xs, b, c], axis=-1)


R_Z, R_XBC, R_DT, R_Q, R_K, R_V, R_G = 0, 2048, 6144, 6176, 7200, 7456, 7712


def _internal_of_reference():
    ref = np.arange(D_IN_PROJ)
    out = np.empty(D_IN_PROJ, np.int64)
    out[R_Z:R_XBC] = OFF_Z + ref[:D_SSD]
    xs = np.arange(D_SSD)
    out[R_XBC:R_XBC + D_SSD] = OFF_XBC + (xs // GROUP_W) * XBC_BLK + xs % GROUP_W
    bc = np.arange(N_GROUPS * D_STATE)
    out[R_XBC + D_SSD:R_XBC + D_SSD + N_GROUPS * D_STATE] = OFF_XBC + (bc // D_STATE) * XBC_BLK + GROUP_W + bc % D_STATE
    out[R_XBC + D_SSD + N_GROUPS * D_STATE:R_DT] = OFF_XBC + (bc // D_STATE) * XBC_BLK + GROUP_W + D_STATE + bc % D_STATE
    out[R_DT:R_Q] = OFF_DT + np.arange(SSD_HEADS)
    out[R_Q:R_K] = OFF_Q + np.arange(D_ATT)
    out[R_K:R_V] = OFF_K + np.arange(D_KV)
    out[R_V:R_G] = OFF_V + np.arange(D_KV)
    out[R_G:] = OFF_G + np.arange(D_ATT)
    return out


def _runs(src, dst_break):
    runs, lo = [], 0
    for i in range(1, len(src) + 1):
        if i == len(src) or src[i] != src[i - 1] + 1 or dst_break[i] != dst_break[i - 1]:
            runs.append((lo, i))
            lo = i
    return runs


RELAYOUT_ROWS = 256


def _lane_window(ref, lead, c0, n):
    a0 = c0 // 128 * 128
    a1 = min(-(-(c0 + n) // 128) * 128, ref.shape[-1])
    return ref[lead + (slice(None), slice(a0, a1))][:, c0 - a0:c0 - a0 + n]


def _w_in_internal(w_gathered):
    int_of_ref = _internal_of_reference()
    ref_of_int = np.full(NP, -1, np.int64)
    ref_of_int[int_of_ref] = np.arange(D_IN_PROJ)
    shard = np.where(ref_of_int >= 0, ref_of_int // SHARD_IN, -1)
    src = np.where(ref_of_int >= 0, ref_of_int, -10 - 2 * np.arange(NP))
    plan, zeros = [], 0
    for lo, hi in _runs(src, shard):
        if ref_of_int[lo] < 0:
            zeros += hi - lo
            continue
        if zeros:
            plan.append((None, 0, zeros))
            zeros = 0
        plan.append((int(ref_of_int[lo] // SHARD_IN), int(ref_of_int[lo] % SHARD_IN), hi - lo))
    if zeros:
        plan.append((None, 0, zeros))
    tr = RELAYOUT_ROWS

    def body(w_ref, o_ref):
        o_ref[...] = jnp.concatenate(
            [jnp.zeros((tr, n), o_ref.dtype) if s is None else _lane_window(w_ref, (s,), c0, n) for s, c0, n in plan], axis=1)

    return pl.pallas_call(
        body, name="w_in_relayout", grid=(D_MODEL // tr,),
        in_specs=[pl.BlockSpec((N_DEV, tr, SHARD_IN), lambda i: (0, i, 0))],
        out_specs=pl.BlockSpec((tr, NP), lambda i: (i, 0)),
        out_shape=jax.ShapeDtypeStruct((D_MODEL, NP), w_gathered.dtype),
        compiler_params=_params(("parallel",)),
    )(w_gathered)


def _w_in_slabs(dw):
    int_of_ref = _internal_of_reference()
    plan = []
    for s in range(N_DEV):
        cols = int_of_ref[s * SHARD_IN:(s + 1) * SHARD_IN]
        plan.append([(int(cols[lo]), hi - lo) for lo, hi in _runs(cols, np.zeros_like(cols))])
    tr = RELAYOUT_ROWS

    def body(dw_ref, o_ref):
        for s in range(N_DEV):
            o_ref[s] = jnp.concatenate([_lane_window(dw_ref, (), c0, n) for c0, n in plan[s]], axis=1)

    return pl.pallas_call(
        body, name="dw_in_relayout", grid=(D_MODEL // tr,),
        in_specs=[pl.BlockSpec((tr, NP), lambda i: (i, 0))],
        out_specs=pl.BlockSpec((N_DEV, tr, SHARD_IN), lambda i: (0, i, 0)),
        out_shape=jax.ShapeDtypeStruct((N_DEV, D_MODEL, SHARD_IN), dw.dtype),
        compiler_params=_params(("parallel",)),
    )(dw)


def _rep_heads(a):
    return jnp.repeat(a, HEAD_DIM, axis=1)


SMALL = (("norm_pre_w", 2048), ("conv_b", 4096), ("dt_bias", 32), ("a_log", 32), ("d_skip", 32),
         ("ssd_norm_w", 2048), ("attn_sinks", 16), ("norm_post_w", 2048))
SMALL_USED = sum(size for _, size in SMALL)
SMALL_LEN = 10368


def _pack_small(d, loss=None):
    parts = [d[name].reshape(1, size) for name, size in SMALL]
    tail = jnp.zeros((1, SMALL_LEN - SMALL_USED), F32)
    if loss is not None:
        tail = tail.at[0, 0].set(loss)
    return jnp.concatenate(parts + [tail], axis=1)


def _unpack_small(vec):
    out, off = {}, 0
    for name, size in SMALL:
        out[name] = vec[:, off:off + size]
        off += size
    return out


def kernel(x, meta_tokens, norm_pre_w, w_in, conv_w, conv_b, dt_bias, a_log, d_skip, ssd_norm_w, attn_sinks, w_out, norm_post_w, loss_target, m_meta_tokens, m_norm_pre_w, m_w_in, m_conv_w, m_conv_b, m_dt_bias, m_a_log, m_d_skip, m_ssd_norm_w, m_attn_sinks, m_w_out, m_norm_post_w, v_meta_tokens, v_norm_pre_w, v_w_in, v_conv_w, v_conv_b, v_dt_bias, v_a_log, v_d_skip, v_ssd_norm_w, v_attn_sinks, v_w_out, v_norm_post_w):
    seq = x.shape[1]
    lp = seq + 2 * CHUNK
    x2d = x[0]

    w_in_g, conv_w_g, meta_g = _gather_two_level([w_in[0].astype(BF16), conv_w[0], meta_tokens], "gather_w_in")
    w_all = _w_in_internal(w_in_g)
    conv_w_full = _perm_xbc(jnp.transpose(conv_w_g, (1, 0, 2)).reshape(CONV_WIDTH, D_CONV))
    conv_b_int = _perm_xbc(conv_b)
    meta_full = jnp.transpose(meta_g, (1, 0, 2)).reshape(N_META, D_MODEL)
    h0 = jnp.concatenate([jnp.zeros((PAD_LEAD, D_MODEL), F32), meta_full], axis=0)

    pos = (jnp.arange(lp) - PAD_LEAD).astype(F32)
    half = HEAD_DIM // 2
    inv = ROPE_THETA ** (-jnp.arange(half, dtype=F32) / half)
    ang = pos[:, None] * inv[None, :]
    cos_t = jnp.tile(jnp.cos(ang), (1, 4))
    sin_t = jnp.tile(jnp.concatenate([-jnp.sin(ang), jnp.sin(ang)], axis=1), (1, 2))
    head_of_col = np.arange(D_SSD) // HEAD_DIM
    expand = jnp.asarray((np.arange(128)[:, None] == head_of_col[None, :]).astype(np.float32))
    reduce_t = jnp.asarray((head_of_col[:, None] == np.arange(128)[None, :]).astype(np.float32))
    tri = jnp.asarray(np.tril(np.ones((CHUNK, CHUNK), np.float32)))
    a_rep = _rep_heads(-jnp.exp(a_log))
    dsk_rep = _rep_heads(d_skip)
    dt_bias_pad = jnp.pad(dt_bias, ((0, 0), (0, 128 - SSD_HEADS)))
    sink_stack = jnp.repeat(attn_sinks.reshape(KV_HEADS, REP), CHUNK, axis=1).reshape(KV_HEADS, REP * CHUNK, 1)

    hn = _prenorm(x2d, h0, norm_pre_w)
    tm = _tile(lp, (1056, 704, 128, 64))
    proj, w_out_g = _matmul(hn, w_all, tm=tm, tn=1536, tk=D_MODEL, out_dtype=F32, name="in_proj",
                            comm=_Comm([(w_out[0].astype(BF16), "gather")]))
    w_out_full = w_out_g.reshape(D_MIX, D_MODEL)
    qr, kr, vb, dt_rep = _act_fwd(proj, cos_t, sin_t, expand, dt_bias_pad)
    mix, ytot, hprev, xbc = _ssd_fwd(proj, conv_w_full, conv_b_int, dt_rep, a_rep, dsk_rep, ssd_norm_w, tri)
    att, mix = _attn_fwd(qr, kr, vb, proj, sink_stack, mix)
    out = _matmul(mix, w_out_full, tm=tm, tn=1024, tk=D_MIX, out_dtype=F32, name="out_proj")
    dout, dy, g_norm_post, loss_part = _post_loss(out, x2d, loss_target[0], norm_post_w)

    dmix = _matmul(dout, w_out_full, trans_b=True, tm=tm, tn=1024, tk=D_MODEL, out_dtype=F32, name="dmix")
    dw_out = _matmul(mix, dout, trans_a=True, tm=512, tn=1024, tk=lp, out_dtype=BF16, name="dw_out")
    dqr, dkr, dv, dg, dsink_rows = _attn_bwd(qr, kr, vb, att, proj, dmix, sink_stack)
    dproj, dxbc, ddt_part, dd_part, da_part, g_ssd_norm, g_out = _ssd_bwd(
        dmix, ytot, proj, xbc, dt_rep, hprev, a_rep, dsk_rep, ssd_norm_w, tri,
        _Comm([(dw_out.reshape(N_DEV, D_MIX // N_DEV, D_MODEL), "scatter")]))
    dproj, dconv_w_int, dconv_b_int = _conv_bwd(dxbc, proj, conv_w_full, conv_b_int, dproj)
    dproj, ddt_bias = _act_bwd(dqr, dkr, dv, dg, ddt_part, cos_t, sin_t, reduce_t, dproj)
    dw_all = _matmul(hn, dproj, trans_a=True, tm=512, tn=1024, tk=lp, out_dtype=BF16, name="dw_in")
    dw_slabs = _w_in_slabs(dw_all)
    dw_chip = _pair_sum(dw_slabs, _exchange_sibling(dw_slabs, "dw_in_sibling"), "dw_in_pair_sum")
    dhn, g_in = _matmul(dproj, w_all, trans_b=True, tm=tm, tn=1024, tk=1536, out_dtype=F32, name="dhn",
                        comm=_Comm([(dw_chip, "scatter")], scope="chips"))
    grad_x, dh0, g_norm_pre = _prenorm_bwd(dhn, x2d, h0, dy, norm_pre_w)

    dmeta = dh0[PAD_LEAD:, :]
    dconv_w_ref = _unperm_xbc(dconv_w_int)
    heads = lambda part: part.reshape(SSD_HEADS, HEAD_DIM).sum(axis=1).reshape(1, SSD_HEADS)
    small_local = _pack_small({
        "norm_pre_w": g_norm_pre, "conv_b": _unperm_xbc(dconv_b_int), "dt_bias": ddt_bias[:, :SSD_HEADS],
        "a_log": heads(da_part) * (-jnp.exp(a_log)), "d_skip": heads(dd_part), "ssd_norm_w": g_ssd_norm,
        "attn_sinks": dsink_rows.reshape(Q_HEADS, CHUNK).sum(axis=1).reshape(1, Q_HEADS),
        "norm_post_w": g_norm_post}, loss=loss_part[0, 0])
    g_conv, g_meta, g_small = _exchange(
        [(jnp.transpose(dconv_w_ref.reshape(CONV_WIDTH, N_DEV, D_CONV // N_DEV), (1, 0, 2)), "scatter"),
         (jnp.transpose(dmeta.reshape(N_META, N_DEV, D_MODEL // N_DEV), (1, 0, 2)), "scatter"),
         (small_local, "gather")], "exchange_small")

    res = {}
    res["w_in"] = [o[None] for o in _adamw(g_in, w_in[0], m_w_in[0], v_w_in[0], "adamw_w_in")]
    res["w_out"] = [o[None] for o in _adamw(g_out, w_out[0], m_w_out[0], v_w_out[0], "adamw_w_out")]
    res["conv_w"] = [o[None] for o in _adamw(g_conv, conv_w[0], m_conv_w[0], v_conv_w[0], "adamw_conv_w")]
    res["meta_tokens"] = _adamw(g_meta, meta_tokens, m_meta_tokens, v_meta_tokens, "adamw_meta")
    given = dict(norm_pre_w=(norm_pre_w, m_norm_pre_w, v_norm_pre_w), conv_b=(conv_b, m_conv_b, v_conv_b),
                 dt_bias=(dt_bias, m_dt_bias, v_dt_bias), a_log=(a_log, m_a_log, v_a_log),
                 d_skip=(d_skip, m_d_skip, v_d_skip), ssd_norm_w=(ssd_norm_w, m_ssd_norm_w, v_ssd_norm_w),
                 attn_sinks=(attn_sinks, m_attn_sinks, v_attn_sinks),
                 norm_post_w=(norm_post_w, m_norm_post_w, v_norm_post_w))
    packed = [_pack_small({k: t[j] for k, t in given.items()}) for j in range(3)]
    small_out = _adamw(g_small, packed[0], packed[1], packed[2], "adamw_small")
    small_res = [_unpack_small(r) for r in small_out]
    loss = small_out[0][0, SMALL_USED]

    order = ["meta_tokens", "norm_pre_w", "w_in", "conv_w", "conv_b", "dt_bias", "a_log", "d_skip", "ssd_norm_w",
             "attn_sinks", "w_out", "norm_post_w"]
    outs = []
    for j in range(4):
        for name in order:
            outs.append(res[name][j] if name in res else small_res[j][name])
    return (loss, grad_x[None], *outs)
```

```python
import functools

import numpy as np
import jax
import jax.numpy as jnp
from jax import lax
from jax.experimental import pallas as pl
from jax.experimental.pallas import tpu as pltpu

F32 = jnp.float32
BF16 = jnp.bfloat16
HIGHEST = lax.Precision.HIGHEST

N_DEV = 8
D_MODEL = 2048
CHUNK = 64
N_META = 16
PAD_LEAD = CHUNK - N_META
EPS = 1e-6
N_GROUPS = 8
HEADS_PER_GROUP = 4
HEAD_DIM = 64
GROUP_W = HEADS_PER_GROUP * HEAD_DIM
D_STATE = 128
D_SSD = 2048
D_CONV = 4096
SSD_HEADS = 32
CONV_WIDTH = 4
Q_HEADS = 16
KV_HEADS = 4
REP = 4
D_ATT = 1024
D_KV = 256
WINDOW_CHUNKS = 2
BAND = (WINDOW_CHUNKS + 1) * CHUNK
ROPE_THETA = 10000.0
D_MIX = D_SSD + D_ATT
D_IN_PROJ = 8736
SHARD_IN = D_IN_PROJ // N_DEV

OFF_Z, OFF_XBC, OFF_Q, OFF_G, OFF_K, OFF_V, OFF_DT = 0, 2048, 6144, 7168, 8192, 8448, 8704
NP = 9216
TAIL_W = NP - OFF_Q
XBC_BLK = 512
SSD_GPS = 8
SSD_CPS = 2

ADAM_LR, ADAM_B1, ADAM_B2, ADAM_EPS, ADAM_WD, ADAM_STEP = 0.001, 0.9, 0.999, 1e-08, 0.01, 10

VMEM_LIMIT = 48 * 1024 * 1024

NN = (((1,), (0,)), ((), ()))
NT = (((1,), (1,)), ((), ()))
TN = (((0,), (0,)), ((), ()))
ANY = pl.BlockSpec(memory_space=pl.ANY)


def _dot(a, b, dims=NN, precision=None):
    return lax.dot_general(a, b, dims, precision=precision, preferred_element_type=F32)


def _tile(n, prefs):
    for t in prefs:
        if n % t == 0:
            return t
    return n


def _params(sem, vmem=VMEM_LIMIT, side_effects=False):
    return pltpu.CompilerParams(dimension_semantics=sem, vmem_limit_bytes=vmem, has_side_effects=side_effects)


def _sigmoid(x):
    return 1.0 / (1.0 + jnp.exp(-x))


class _Comm:
    def __init__(self, items, scope="devices"):
        self.items = items
        self.scope = scope
        self.slabs = slabs = N_DEV if scope == "devices" else N_DEV // 2
        self.n = n = len(items)
        self.operands = [arr for arr, _ in items]
        self.in_specs = [ANY] * n
        self.out_specs = [ANY] * n
        self.out_shape = [jax.ShapeDtypeStruct((slabs,) + tuple(arr.shape) if kind == "gather" else tuple(arr.shape),
                                               arr.dtype) for arr, kind in items]
        self.scratch = [pltpu.SemaphoreType.DMA((n, slabs - 1)), pltpu.SemaphoreType.DMA((n, slabs - 1)),
                        pltpu.SemaphoreType.DMA((n,))]

    def _places(self):
        pos = (lax.axis_index("x"), lax.axis_index("y"), lax.axis_index("c"))
        if self.scope == "devices":
            index = lambda p: 4 * p[0] + 2 * p[1] + p[2]
            masks = range(1, N_DEV)
        else:
            index = lambda p: 2 * p[0] + p[1]
            masks = (2, 4, 6)
        peers = []
        for k in masks:
            p = tuple(1 - pos[b] if (k >> (2 - b)) & 1 else pos[b] for b in range(3))
            peers.append((p, index(p)))
        return index(pos), peers

    def _copies(self, ins, outs, sems, landed):
        send_sems, recv_sems, local_sems = sems
        me, peers = self._places()
        local, remote = [], []
        for a, (_, kind) in enumerate(self.items):
            own = ins[a] if kind == "gather" else ins[a].at[me]
            local.append(pltpu.make_async_copy(own, outs[a].at[me], local_sems.at[a]))
            for k, (p, pid) in enumerate(peers):
                remote.append(pltpu.make_async_remote_copy(
                    src_ref=ins[a] if kind == "gather" else ins[a].at[pid],
                    dst_ref=outs[a].at[pid if landed else me],
                    send_sem=send_sems.at[a, k], recv_sem=recv_sems.at[a, k],
                    device_id=p, device_id_type=pl.DeviceIdType.MESH))
        return local, remote

    def start(self, ins, outs, sems):
        local, remote = self._copies(ins, outs, sems, landed=False)
        for cp in local + remote:
            cp.start()

    def wait(self, ins, outs, sems):
        local, remote = self._copies(ins, outs, sems, landed=True)
        for cp in remote + local:
            cp.wait()


def _exchange(items, name):
    comm = _Comm(items)
    n = comm.n

    def body(*refs):
        ins, outs, sems = refs[:n], refs[n:2 * n], refs[2 * n:]
        comm.start(ins, outs, sems)
        comm.wait(ins, outs, sems)

    return pl.pallas_call(
        body, name=name, in_specs=comm.in_specs, out_specs=comm.out_specs, out_shape=comm.out_shape,
        scratch_shapes=comm.scratch, compiler_params=pltpu.CompilerParams(has_side_effects=True),
    )(*comm.operands)


def _pair_sum(slabs, landed, name):
    _, rows, cols = slabs.shape
    tr = _tile(rows, (256, 128, 64, 16, 8))

    def body(g_ref, l_ref, o_ref):
        mine = g_ref[lax.axis_index("c")]
        o_ref[...] = (mine.astype(F32) + l_ref[...].astype(F32)).astype(o_ref.dtype)

    return pl.pallas_call(
        body, name=name, grid=(N_DEV // 2, rows // tr),
        in_specs=[pl.BlockSpec((None, 2, tr, cols), lambda q, i: (q, 0, i, 0)),
                  pl.BlockSpec((None, tr, cols), lambda q, i: (q, i, 0))],
        out_specs=pl.BlockSpec((None, tr, cols), lambda q, i: (q, i, 0)),
        out_shape=jax.ShapeDtypeStruct((N_DEV // 2, rows, cols), slabs.dtype),
        compiler_params=_params(("parallel", "parallel")),
    )(slabs.reshape(N_DEV // 2, 2, rows, cols), landed)


class _GatherTwoLevel:
    def __init__(self, arrays):
        self.arrays = arrays
        self.n = n = len(arrays)
        self.operands = list(arrays)
        self.in_specs = [ANY] * n
        self.out_specs = [ANY] * n
        self.out_shape = [jax.ShapeDtypeStruct((N_DEV,) + tuple(a.shape), a.dtype) for a in arrays]
        self.scratch = [pltpu.SemaphoreType.DMA((n, N_DEV - 1)), pltpu.SemaphoreType.DMA((n, N_DEV - 1)),
                        pltpu.SemaphoreType.DMA((n,))]

    def _plan(self, ins, outs, sems):
        send_sems, recv_sems, local_sems = sems
        x, y, c = lax.axis_index("x"), lax.axis_index("y"), lax.axis_index("c")
        me, sibling = (x, y, c), (x, y, 1 - c)
        chips = [(1 - x, y), (x, 1 - y), (1 - x, 1 - y)]

        def slab(a, place):
            return outs[a].at[4 * place[0] + 2 * place[1] + place[2]]

        def copy(a, k, block, to, src=None):
            return pltpu.make_async_remote_copy(
                src_ref=slab(a, block) if src is None else src, dst_ref=slab(a, block),
                send_sem=send_sems.at[a, k], recv_sem=recv_sems.at[a, k],
                device_id=to, device_id_type=pl.DeviceIdType.MESH)

        own, mine = [], []
        for a in range(self.n):
            mine.append(pltpu.make_async_copy(ins[a], slab(a, me), local_sems.at[a]))
            own.append(copy(a, 0, me, sibling, src=ins[a]))
            own += [copy(a, 1 + j, me, (*chip, c), src=ins[a]) for j, chip in enumerate(chips)]
        return me, sibling, chips, c, copy, own, mine

    def start(self, ins, outs, sems):
        _, _, _, _, _, own, mine = self._plan(ins, outs, sems)
        for cp in mine + own:
            cp.start()

    def wait(self, ins, outs, sems):
        me, sibling, chips, c, copy, own, mine = self._plan(ins, outs, sems)
        forwards = []
        for j, chip in enumerate(chips):
            for a in range(self.n):
                copy(a, 1 + j, (*chip, c), me).wait_recv()
                fwd = copy(a, 4 + j, (*chip, c), sibling)
                fwd.start()
                forwards.append(fwd)
        for a in range(self.n):
            copy(a, 0, sibling, me).wait_recv()
            for j, chip in enumerate(chips):
                copy(a, 4 + j, (*chip, 1 - c), me).wait_recv()
        for cp in own + forwards:
            cp.wait_send()
        for loc in mine:
            loc.wait()


def _gather_two_level(arrays, name):
    comm = _GatherTwoLevel(arrays)
    n = comm.n

    def body(*refs):
        ins, outs, sems = refs[:n], refs[n:2 * n], refs[2 * n:]
        comm.start(ins, outs, sems)
        comm.wait(ins, outs, sems)

    return pl.pallas_call(
        body, name=name, in_specs=comm.in_specs, out_specs=comm.out_specs, out_shape=comm.out_shape,
        scratch_shapes=comm.scratch, compiler_params=pltpu.CompilerParams(has_side_effects=True),
    )(*arrays)


def _matmul(a, b, *, tm, tn, tk, out_dtype, name, trans_a=False, trans_b=False, comm=None):
    m, k = (a.shape[1], a.shape[0]) if trans_a else a.shape
    n = b.shape[0] if trans_b else b.shape[1]
    nk = k // tk
    dims = TN if trans_a else (NT if trans_b else NN)
    assert not (trans_a and trans_b)
    nc = comm.n if comm else 0
    grid = (m // tm, n // tn, nk)

    def body(*refs):
        a_ref, b_ref = refs[:2]
        cin = refs[2:2 + nc]
        o_ref = refs[2 + nc]
        cout = refs[3 + nc:3 + 2 * nc]
        scratch = refs[3 + 2 * nc:]
        sems = scratch[len(scratch) - 3:] if comm else None
        i, j, kk = pl.program_id(0), pl.program_id(1), pl.program_id(2)
        if comm:
            @pl.when((i == 0) & (j == 0) & (kk == 0))
            def _():
                comm.start(cin, cout, sems)

        if nk == 1:
            o_ref[...] = _dot(a_ref[...], b_ref[...], dims).astype(out_dtype)
        else:
            acc_ref = scratch[0]

            @pl.when(kk == 0)
            def _():
                acc_ref[...] = jnp.zeros_like(acc_ref)

            acc_ref[...] += _dot(a_ref[...], b_ref[...], dims)

            @pl.when(kk == nk - 1)
            def _():
                o_ref[...] = acc_ref[...].astype(out_dtype)

        if comm:
            @pl.when((i == grid[0] - 1) & (j == grid[1] - 1) & (kk == nk - 1))
            def _():
                comm.wait(cin, cout, sems)

    a_spec = (pl.BlockSpec((tk, tm), lambda i, j, kk: (kk, i)) if trans_a
              else pl.BlockSpec((tm, tk), lambda i, j, kk: (i, kk)))
    b_spec = (pl.BlockSpec((tn, tk), lambda i, j, kk: (j, kk)) if trans_b
              else pl.BlockSpec((tk, tn), lambda i, j, kk: (kk, j)))
    sem = ("arbitrary",) * 3 if comm else ("parallel", "parallel", "arbitrary")
    res = pl.pallas_call(
        body, name=name, grid=grid,
        in_specs=[a_spec, b_spec] + (comm.in_specs if comm else []),
        out_specs=[pl.BlockSpec((tm, tn), lambda i, j, kk: (i, j))] + (comm.out_specs if comm else []),
        out_shape=[jax.ShapeDtypeStruct((m, n), out_dtype)] + (comm.out_shape if comm else []),
        scratch_shapes=([] if nk == 1 else [pltpu.VMEM((tm, tn), F32)]) + (comm.scratch if comm else []),
        compiler_params=_params(sem, side_effects=bool(comm)),
    )(a, b, *(comm.operands if comm else []))
    return res if comm else res[0]


ROW_K = 6
X_K = 8


def _x_specs(nx, k):
    return [pl.BlockSpec((CHUNK, D_MODEL), functools.partial(lambda u, i: (jnp.clip(k * i + u - 1, 0, nx - 1), 0), u))
            for u in range(k)]


def _chunk_of_row(i, k):
    return k * i + lax.broadcasted_iota(jnp.int32, (k * CHUNK, 1), 0) // CHUNK


def _prenorm(x2d, h0, w):
    nx = x2d.shape[0] // CHUNK
    nc = nx + 2
    k = _tile(nc, (ROW_K, 3, 2, 1))

    def body(*refs):
        x_refs = refs[:k]
        h0_ref, w_ref, o_ref = refs[k:]
        i = pl.program_id(0)
        x = jnp.concatenate([r[...] for r in x_refs], axis=0)
        head = jnp.concatenate([h0_ref[...], x[CHUNK:, :]], axis=0)
        chunk = _chunk_of_row(i, k)
        h = jnp.where(i == 0, head, x) * (chunk <= nx).astype(F32)
        rstd = lax.rsqrt(jnp.mean(h * h, axis=-1, keepdims=True) + EPS)
        o_ref[...] = (h * rstd * w_ref[...]).astype(BF16)

    rows = k * CHUNK
    return pl.pallas_call(
        body, name="prenorm", grid=(nc // k,),
        in_specs=_x_specs(nx, k) + [pl.BlockSpec((CHUNK, D_MODEL), lambda i: (0, 0)), pl.BlockSpec((1, D_MODEL), lambda i: (0, 0))],
        out_specs=pl.BlockSpec((rows, D_MODEL), lambda i: (i, 0)),
        out_shape=jax.ShapeDtypeStruct((nc * CHUNK, D_MODEL), BF16),
        compiler_params=_params(("parallel",)),
    )(*([x2d] * k), h0, w)


CONV_COLS = 512
HALO = 8


def _conv_pre(ext, w, b):
    taps = [ext[HALO:, :]] + [pltpu.roll(ext, j, 0)[HALO:, :] for j in range(1, CONV_WIDTH)]
    acc = b + w[3:4, :] * taps[0]
    for j in range(1, CONV_WIDTH):
        acc = acc + w[3 - j:4 - j, :] * taps[j]
    return acc, taps


def _conv_bwd(dxbc, proj, conv_w, conv_b, dproj):
    lp = proj.shape[0]
    t = _tile(lp, (704, 384, 128, 64))
    hb = t // HALO
    nt = lp // t
    c0 = OFF_XBC // CONV_COLS

    def body(dx_ref, dxn_ref, u_ref, up_ref, un_ref, w_ref, b_ref, _, du_ref, dw_ref, db_ref):
        i = pl.program_id(1)
        w = w_ref[...]
        up = up_ref[...] * (i > 0).astype(F32)
        ext = jnp.concatenate([up, u_ref[...], un_ref[...]], axis=0)
        pre, taps = _conv_pre(ext, w, b_ref[...])
        dxn = dxn_ref[...] * (i < nt - 1).astype(F32)
        dxe = jnp.concatenate([dx_ref[...], dxn], axis=0)
        sg = _sigmoid(pre)
        dpre = dxe * sg * (1.0 + pre * (1.0 - sg))
        du = w[3:4, :] * dpre[:t, :]
        for j in range(1, CONV_WIDTH):
            du = du + w[3 - j:4 - j, :] * pltpu.roll(dpre, t + HALO - j, 0)[:t, :]
        du_ref[...] = du.astype(BF16)

        @pl.when(i == 0)
        def _():
            dw_ref[...] = jnp.zeros_like(dw_ref)
            db_ref[...] = jnp.zeros_like(db_ref)

        dp = dpre[:t, :]
        db_ref[...] += jnp.sum(dp, axis=0, keepdims=True)
        for j in range(CONV_WIDTH):
            dw_ref[3 - j:4 - j, :] += jnp.sum(dp * taps[j][:t, :], axis=0, keepdims=True)

    nxt = lambda i: jnp.minimum((i + 1) * hb, lp // HALO - 1)
    return pl.pallas_call(
        body, name="conv_bwd", grid=(D_CONV // CONV_COLS, nt),
        in_specs=[
            pl.BlockSpec((t, CONV_COLS), lambda j, i: (i, j)),
            pl.BlockSpec((HALO, CONV_COLS), lambda j, i: (nxt(i), j)),
            pl.BlockSpec((t, CONV_COLS), lambda j, i: (i, c0 + j)),
            pl.BlockSpec((HALO, CONV_COLS), lambda j, i: (jnp.maximum(i * hb - 1, 0), c0 + j)),
            pl.BlockSpec((HALO, CONV_COLS), lambda j, i: (nxt(i), c0 + j)),
            pl.BlockSpec((CONV_WIDTH, CONV_COLS), lambda j, i: (0, j)),
            pl.BlockSpec((1, CONV_COLS), lambda j, i: (0, j)),
            ANY,
        ],
        out_specs=[
            pl.BlockSpec((t, CONV_COLS), lambda j, i: (i, c0 + j)),
            pl.BlockSpec((CONV_WIDTH, CONV_COLS), lambda j, i: (0, j)),
            pl.BlockSpec((1, CONV_COLS), lambda j, i: (0, j)),
        ],
        out_shape=[
            jax.ShapeDtypeStruct((lp, NP), BF16),
            jax.ShapeDtypeStruct((CONV_WIDTH, D_CONV), F32),
            jax.ShapeDtypeStruct((1, D_CONV), F32),
        ],
        input_output_aliases={7: 0},
        compiler_params=_params(("parallel", "arbitrary")),
    )(dxbc, dxbc, proj, proj, proj, conv_w, conv_b, dproj)


def _swap_halves(t):
    w = t.shape[-1]
    lane = lax.broadcasted_iota(jnp.int32, t.shape, 1)
    return jnp.where((lane % HEAD_DIM) < HEAD_DIM // 2, pltpu.roll(t, w - HEAD_DIM // 2, 1),
                     pltpu.roll(t, HEAD_DIM // 2, 1))


def _act_fwd(proj, cos_t, sin_t, expand, dt_bias_pad):
    lp = proj.shape[0]
    t = _tile(lp, (384, 128, 64))

    def body(q_ref, k_ref, v_ref, dt_ref, cos_ref, sin_ref, ex_ref, bias_ref, qo_ref, ko_ref, vo_ref, dto_ref):
        i = pl.program_id(0)
        cos = cos_ref[...]
        sin = sin_ref[...]
        q = q_ref[...]
        qo_ref[...] = (q * jnp.tile(cos, (1, D_ATT // 128)) + _swap_halves(q) * jnp.tile(sin, (1, D_ATT // 128))).astype(BF16)
        k = k_ref[...]
        ko_ref[...] = (k * jnp.tile(cos, (1, D_KV // 128)) + _swap_halves(k) * jnp.tile(sin, (1, D_KV // 128))).astype(BF16)
        vo_ref[...] = v_ref[...].astype(BF16)
        raw = dt_ref[...] + bias_ref[...]
        sp = jnp.maximum(raw, 0.0) + jnp.log1p(jnp.exp(-jnp.abs(raw)))
        row = i * t + lax.broadcasted_iota(jnp.int32, sp.shape, 0)
        dto_ref[...] = _dot(jnp.where(row >= PAD_LEAD, sp, 0.0), ex_ref[...], NN, HIGHEST)

    return pl.pallas_call(
        body, name="act_fwd", grid=(lp // t,),
        in_specs=[
            pl.BlockSpec((t, D_ATT), lambda i: (i, OFF_Q // D_ATT)),
            pl.BlockSpec((t, D_KV), lambda i: (i, OFF_K // D_KV)),
            pl.BlockSpec((t, D_KV), lambda i: (i, OFF_V // D_KV)),
            pl.BlockSpec((t, 128), lambda i: (i, OFF_DT // 128)),
            pl.BlockSpec((t, 128), lambda i: (i, 0)),
            pl.BlockSpec((t, 128), lambda i: (i, 0)),
            pl.BlockSpec((128, D_SSD), lambda i: (0, 0)),
            pl.BlockSpec((1, 128), lambda i: (0, 0)),
        ],
        out_specs=[
            pl.BlockSpec((t, D_ATT), lambda i: (i, 0)),
            pl.BlockSpec((t, D_KV), lambda i: (i, 0)),
            pl.BlockSpec((t, D_KV), lambda i: (i, 0)),
            pl.BlockSpec((t, D_SSD), lambda i: (i, 0)),
        ],
        out_shape=[
            jax.ShapeDtypeStruct((lp, D_ATT), BF16),
            jax.ShapeDtypeStruct((lp, D_KV), BF16),
            jax.ShapeDtypeStruct((lp, D_KV), BF16),
            jax.ShapeDtypeStruct((lp, D_SSD), F32),
        ],
        compiler_params=_params(("parallel",)),
    )(proj, proj, proj, proj, cos_t, sin_t, expand, dt_bias_pad)


def _act_bwd(dqr, dkr, dv, dg, ddt_part, cos_t, sin_t, reduce_t, dproj):
    lp = dqr.shape[0]
    t = _tile(lp, (384, 128, 64))

    def body(dq_ref, dk_ref, dv_ref, dg_ref, ddt_ref, cos_ref, sin_ref, red_ref, _, o_ref, db_ref):
        i = pl.program_id(0)
        cos = cos_ref[...]
        sin = sin_ref[...]
        dq = dq_ref[...]
        dq = dq * jnp.tile(cos, (1, D_ATT // 128)) + _swap_halves(dq * jnp.tile(sin, (1, D_ATT // 128)))
        dk = dk_ref[...]
        dk = dk * jnp.tile(cos, (1, D_KV // 128)) + _swap_halves(dk * jnp.tile(sin, (1, D_KV // 128)))
        ddt = _dot(ddt_ref[...], red_ref[...], NN, HIGHEST)
        o_ref[...] = jnp.concatenate(
            [dq.astype(BF16), dg_ref[...].astype(BF16), dk.astype(BF16), dv_ref[...].astype(BF16), ddt.astype(BF16),
             jnp.zeros((t, NP - OFF_DT - 128), BF16)], axis=1)

        @pl.when(i == 0)
        def _():
            db_ref[...] = jnp.zeros_like(db_ref)

        db_ref[...] += jnp.sum(ddt, axis=0, keepdims=True)

    return pl.pallas_call(
        body, name="act_bwd", grid=(lp // t,),
        in_specs=[
            pl.BlockSpec((t, D_ATT), lambda i: (i, 0)),
            pl.BlockSpec((t, D_KV), lambda i: (i, 0)),
            pl.BlockSpec((t, D_KV), lambda i: (i, 0)),
            pl.BlockSpec((t, D_ATT), lambda i: (i, 0)),
            pl.BlockSpec((t, D_SSD), lambda i: (i, 0)),
            pl.BlockSpec((t, 128), lambda i: (i, 0)),
            pl.BlockSpec((t, 128), lambda i: (i, 0)),
            pl.BlockSpec((D_SSD, 128), lambda i: (0, 0)),
            ANY,
        ],
        out_specs=[pl.BlockSpec((t, TAIL_W), lambda i: (i, OFF_Q // TAIL_W)), pl.BlockSpec((1, 128), lambda i: (0, 0))],
        out_shape=[jax.ShapeDtypeStruct((lp, NP), BF16), jax.ShapeDtypeStruct((1, 128), F32)],
        input_output_aliases={8: 0},
        compiler_params=_params(("arbitrary",)),
    )(dqr, dkr, dv, dg, ddt_part, cos_t, sin_t, reduce_t, dproj)


def _cs_row(cs):
    row = lax.broadcasted_iota(jnp.int32, cs.shape, 0)
    lane = lax.broadcasted_iota(jnp.int32, cs.shape, 1)
    return jnp.sum(jnp.where(row == lane % HEAD_DIM, cs, 0.0), axis=0, keepdims=True)


def _ssd_fwd(proj, conv_w, conv_b, dt_rep, a_rep, dsk_rep, wn, tri):
    lp = proj.shape[0]
    nc = lp // CHUNK
    assert SSD_GPS == N_GROUPS
    gw = SSD_GPS * GROUP_W
    cps = _tile(nc, (SSD_CPS, 1))

    def body(rawa_ref, rawb_ref, cw_ref, cb_ref, dt_ref, z_ref, a_ref, dsk_ref, wn_ref, tri_ref,
             yn_ref, ytot_ref, hprev_ref, xbc_ref, h_scr, tail_scr):
        @pl.when(pl.program_id(1) == 0)
        def _():
            h_scr[...] = jnp.zeros_like(h_scr)
            tail_scr[...] = jnp.zeros_like(tail_scr)

        for sub in range(cps):
            rs = slice(sub * CHUNK, (sub + 1) * CHUNK)
            raw = jnp.concatenate([rawa_ref[rs, :], rawb_ref[rs, :]], axis=1)
            pre, _ = _conv_pre(jnp.concatenate([tail_scr[...], raw], axis=0), cw_ref[...], cb_ref[...])
            tail_scr[...] = raw[CHUNK - HALO:, :]
            xbc_ref[rs, :] = pre * _sigmoid(pre)

        G = range(SSD_GPS)
        colsl = [slice(gi * GROUP_W, (gi + 1) * GROUP_W) for gi in G]
        rows4 = lax.broadcasted_iota(jnp.int32, (GROUP_W, GROUP_W), 0) // HEAD_DIM
        cols4 = lax.broadcasted_iota(jnp.int32, (GROUP_W, GROUP_W), 1) // HEAD_DIM
        lrow = lax.broadcasted_iota(jnp.int32, (CHUNK, GROUP_W), 0)
        lcol = lax.broadcasted_iota(jnp.int32, (CHUNK, GROUP_W), 1) % HEAD_DIM

        def one_chunk(sub):
            rs = slice(sub * CHUNK, (sub + 1) * CHUNK)
            xbc = [xbc_ref[rs, gi * XBC_BLK:(gi + 1) * XBC_BLK] for gi in G]
            dt = [dt_ref[rs, colsl[gi]] for gi in G]
            xs = [xbc[gi][:, :GROUP_W] for gi in G]
            b = [xbc[gi][:, GROUP_W:GROUP_W + D_STATE].astype(BF16) for gi in G]
            c = [xbc[gi][:, GROUP_W + D_STATE:].astype(BF16) for gi in G]
            hprev = [h_scr[gi] for gi in G]
            cs = [_dot(tri_ref[...], dt[gi] * a_ref[:, colsl[gi]], NN, HIGHEST) for gi in G]
            yoff = [_dot(c[gi], hprev[gi].astype(BF16)) for gi in G]
            cs_t = [_cs_row(cs[gi]) for gi in G]
            xdt = [xs[gi] * dt[gi] for gi in G]
            cs_last = [cs[gi][CHUNK - 1:CHUNK, :] for gi in G]
            st = [_dot(b[gi], (xdt[gi] * jnp.exp(cs_last[gi] - cs[gi])).astype(BF16), TN) for gi in G]
            cb4 = [_dot(c[gi], jnp.concatenate([b[gi]] * HEADS_PER_GROUP, axis=0), NT) for gi in G]
            m = [(cb4[gi] * jnp.exp(jnp.where(lrow >= lcol, cs[gi] - cs_t[gi], -jnp.inf))).astype(BF16) for gi in G]
            xbd = [jnp.where(rows4 == cols4, jnp.concatenate([xdt[gi].astype(BF16)] * HEADS_PER_GROUP, axis=0), 0.0)
                   for gi in G]
            ydiag = [_dot(m[gi], xbd[gi]) for gi in G]
            for gi in G:
                cols = colsl[gi]
                ytot = ydiag[gi] + yoff[gi] * jnp.exp(cs[gi]) + dsk_ref[:, cols] * xs[gi]
                z = z_ref[rs, cols]
                gz = ytot * (z * _sigmoid(z))
                rstd = lax.rsqrt(jnp.mean(gz * gz, axis=-1, keepdims=True) + EPS)
                hprev_ref[sub, gi] = hprev[gi]
                h_scr[gi] = hprev[gi] * jnp.exp(cs_last[gi]) + st[gi]
                ytot_ref[rs, cols] = ytot
                yn_ref[rs, cols] = (gz * rstd * wn_ref[:, cols]).astype(BF16)

        for sub in range(cps):
            one_chunk(sub)

    vec = pl.BlockSpec((1, gw), lambda g, c: (0, g))
    blk = pl.BlockSpec((cps * CHUNK, gw), lambda g, c: (c, g))
    half = D_CONV // 2
    return pl.pallas_call(
        body, name="ssd_fwd", grid=(N_GROUPS // SSD_GPS, nc // cps),
        in_specs=[
            pl.BlockSpec((cps * CHUNK, half), lambda g, c: (c, OFF_XBC // half)),
            pl.BlockSpec((cps * CHUNK, half), lambda g, c: (c, OFF_XBC // half + 1)),
            pl.BlockSpec((CONV_WIDTH, D_CONV), lambda g, c: (0, 0)),
            pl.BlockSpec((1, D_CONV), lambda g, c: (0, 0)),
            blk, blk, vec, vec, vec,
            pl.BlockSpec((CHUNK, CHUNK), lambda g, c: (0, 0)),
        ],
        out_specs=[blk, blk, pl.BlockSpec((cps, SSD_GPS, D_STATE, GROUP_W), lambda g, c: (c, g, 0, 0)),
                   pl.BlockSpec((cps * CHUNK, D_CONV), lambda g, c: (c, 0))],
        out_shape=[
            jax.ShapeDtypeStruct((lp, D_MIX), BF16),
            jax.ShapeDtypeStruct((lp, D_SSD), F32),
            jax.ShapeDtypeStruct((nc, N_GROUPS, D_STATE, GROUP_W), F32),
            jax.ShapeDtypeStruct((lp, D_CONV), F32),
        ],
        scratch_shapes=[pltpu.VMEM((SSD_GPS, D_STATE, GROUP_W), F32), pltpu.VMEM((HALO, D_CONV), F32)],
        compiler_params=_params(("arbitrary", "arbitrary")),
    )(proj, proj, conv_w, conv_b, dt_rep, proj, a_rep, dsk_rep, wn, tri)


def _ssd_bwd(dmix, ytot, proj, xbc, dt_rep, hprev, a_rep, dsk_rep, wn, tri, comm):
    lp = xbc.shape[0]
    nc = lp // CHUNK
    gps = SSD_GPS
    gw = gps * GROUP_W
    cps = _tile(nc, (SSD_CPS, 1))
    ncm = comm.n
    n_in, n_out = 10, 6
    grid = (N_GROUPS // gps, nc // cps)

    def all_groups(refs):
        (dyn_ref, ytot_ref, z_ref, xbc_ref, dt_ref, hprev_ref, a_ref, dsk_ref, wn_ref, tri_ref,
         dz_ref, dxbc_ref, ddt_ref, dd_ref, da_ref, dwn_ref, dh_scr) = refs
        G = range(gps)
        H = range(HEADS_PER_GROUP)
        cl = [slice(gi * GROUP_W, (gi + 1) * GROUP_W) for gi in G]
        hl = [slice(r * HEAD_DIM, (r + 1) * HEAD_DIM) for r in H]
        tri = tri_ref[...]
        xbc = [xbc_ref[:, gi * XBC_BLK:(gi + 1) * XBC_BLK] for gi in G]
        dt = [dt_ref[:, cl[gi]] for gi in G]
        a = [a_ref[:, cl[gi]] for gi in G]
        xs = [xbc[gi][:, :GROUP_W] for gi in G]
        bbf = [xbc[gi][:, GROUP_W:GROUP_W + D_STATE].astype(BF16) for gi in G]
        cbf = [xbc[gi][:, GROUP_W + D_STATE:].astype(BF16) for gi in G]
        hprev = [hprev_ref[gi] for gi in G]
        hbf = [hprev[gi].astype(BF16) for gi in G]
        dhn = [dh_scr[gi] for gi in G]
        dhnb = [dhn[gi].astype(BF16) for gi in G]
        cs = [_dot(tri, dt[gi] * a[gi], NN, HIGHEST) for gi in G]
        g = [_dot(cbf[gi], hbf[gi]) for gi in G]
        dxw = [_dot(bbf[gi], dhnb[gi]) for gi in G]
        cs_t = [_cs_row(cs[gi]) for gi in G]
        dy = []
        for gi in G:
            ytot = ytot_ref[:, cl[gi]]
            z = z_ref[:, cl[gi]]
            dyn = dyn_ref[:, cl[gi]]
            sz = _sigmoid(z)
            silu_z = z * sz
            gz = ytot * silu_z
            rstd = lax.rsqrt(jnp.mean(gz * gz, axis=-1, keepdims=True) + EPS)
            xhat = gz * rstd
            dwn_ref[:, cl[gi]] += jnp.sum(dyn * xhat, axis=0, keepdims=True)
            dxhat = dyn * wn_ref[:, cl[gi]]
            dgz = rstd * (dxhat - xhat * jnp.mean(dxhat * xhat, axis=-1, keepdims=True))
            dy.append(dgz * silu_z)
            dz_ref[:, cl[gi]] = (dgz * ytot * (sz * (1.0 + z * (1.0 - sz)))).astype(BF16)
            dd_ref[:, cl[gi]] += jnp.sum(dy[gi] * xs[gi], axis=0, keepdims=True)
        xdt = [xs[gi] * dt[gi] for gi in G]
        e = [jnp.exp(cs[gi]) for gi in G]
        cs_last = [cs[gi][CHUNK - 1:CHUNK, :] for gi in G]
        dte = [jnp.exp(cs_last[gi] - cs[gi]) for gi in G]
        cd = [jnp.exp(cs_last[gi]) for gi in G]
        dgb = [(dy[gi] * e[gi]).astype(BF16) for gi in G]
        dyb = [dy[gi].astype(BF16) for gi in G]
        xdtb = [xdt[gi].astype(BF16) for gi in G]
        dc = [_dot(dgb[gi], hbf[gi], NT) for gi in G]
        dhprev = [_dot(cbf[gi], dgb[gi], TN) for gi in G]
        db = [_dot((xdt[gi] * dte[gi]).astype(BF16), dhnb[gi], NT) for gi in G]
        row = lax.broadcasted_iota(jnp.int32, (CHUNK, CHUNK), 0)
        causal = row >= lax.broadcasted_iota(jnp.int32, (CHUNK, CHUNK), 1)
        cb = [_dot(cbf[gi], bbf[gi], NT) for gi in G]
        dm = [[_dot(dyb[gi][:, hl[r]], xdtb[gi][:, hl[r]], NT) for r in H] for gi in G]
        mb, dseg, dcbb = [], [], []
        for gi in G:
            mb.append([])
            dseg.append([])
            dcb = None
            for r in H:
                seg = cs[gi][:, r * HEAD_DIM:r * HEAD_DIM + 1] - cs_t[gi][:, hl[r]]
                lm = jnp.exp(jnp.where(causal, seg, -jnp.inf))
                m = cb[gi] * lm
                mb[gi].append(m.astype(BF16))
                dseg[gi].append(dm[gi][r] * m)
                dcb = dm[gi][r] * lm if r == 0 else dcb + dm[gi][r] * lm
            dcbb.append(dcb.astype(BF16))
        dxdt_diag = [[_dot(mb[gi][r], dyb[gi][:, hl[r]], TN) for r in H] for gi in G]
        ones = jnp.ones((CHUNK, HEAD_DIM), F32)
        colsum = [[_dot(dseg[gi][r], ones, TN, HIGHEST) for r in H] for gi in G]
        dc2 = [_dot(dcbb[gi], bbf[gi]) for gi in G]
        db2 = [_dot(dcbb[gi], cbf[gi], TN) for gi in G]
        dcs = []
        for gi in G:
            t_dte = dxw[gi] * xdt[gi] * dte[gi]
            dcs_last = (jnp.sum(dhn[gi] * hprev[gi], axis=0, keepdims=True) * cd[gi]
                        + jnp.sum(t_dte, axis=0, keepdims=True))
            diag = jnp.concatenate(
                [(jnp.sum(dseg[gi][r], axis=1, keepdims=True) - colsum[gi][r]) * (1.0 / HEAD_DIM) for r in H], axis=1)
            d = dy[gi] * g[gi] * e[gi] - t_dte + diag
            row = lax.broadcasted_iota(jnp.int32, d.shape, 0)
            dcs.append(d + jnp.where(row == CHUNK - 1, dcs_last, 0.0))
        dda = [_dot(tri, dcs[gi], TN, HIGHEST) for gi in G]
        for gi in G:
            dxdt = dxw[gi] * dte[gi] + jnp.concatenate(dxdt_diag[gi], axis=1)
            da_ref[:, cl[gi]] += jnp.sum(dda[gi] * dt[gi], axis=0, keepdims=True)
            ddt = dda[gi] * a[gi] + dxdt * xs[gi]
            dxs = dsk_ref[:, cl[gi]] * dy[gi] + dxdt * dt[gi]
            ddt_ref[:, cl[gi]] = ddt * (1.0 - jnp.exp(-dt[gi]))
            dxbc_ref[:, gi * XBC_BLK:(gi + 1) * XBC_BLK] = jnp.concatenate(
                [dxs, db[gi] + db2[gi], dc[gi] + dc2[gi]], axis=1)
            dh_scr[gi] = dhprev[gi] + dhn[gi] * cd[gi]

    def body(*refs):
        ins = refs[:n_in]
        cin = refs[n_in:n_in + ncm]
        outs = refs[n_in + ncm:n_in + ncm + n_out]
        cout = refs[n_in + ncm + n_out:n_in + 2 * ncm + n_out]
        dh_scr = refs[n_in + 2 * ncm + n_out]
        sems = refs[n_in + 2 * ncm + n_out + 1:]
        g, c = pl.program_id(0), pl.program_id(1)

        @pl.when((g == 0) & (c == 0))
        def _():
            comm.start(cin, cout, sems)

        @pl.when(c == 0)
        def _():
            dh_scr[...] = jnp.zeros_like(dh_scr)
            for ref in outs[3:]:
                ref[...] = jnp.zeros_like(ref)

        for sub in reversed(range(cps)):
            rs = pl.ds(sub * CHUNK, CHUNK)
            chunk_ins = tuple(r.at[rs] for r in ins[:5]) + (ins[5].at[sub],) + ins[6:]
            chunk_outs = tuple(r.at[rs] for r in outs[:3]) + outs[3:]
            all_groups(chunk_ins + chunk_outs + (dh_scr,))

        @pl.when((g == grid[0] - 1) & (c == grid[1] - 1))
        def _():
            comm.wait(cin, cout, sems)

    rev = lambda c: grid[1] - 1 - c
    vec = pl.BlockSpec((1, gw), lambda g, c: (0, g))
    blk = pl.BlockSpec((cps * CHUNK, gw), lambda g, c: (rev(c), g))
    xblk = pl.BlockSpec((cps * CHUNK, gps * XBC_BLK), lambda g, c: (rev(c), g))
    res = pl.pallas_call(
        body, name="ssd_bwd", grid=grid,
        in_specs=[blk, blk, blk, xblk, blk,
                  pl.BlockSpec((cps, gps, D_STATE, GROUP_W), lambda g, c: (rev(c), g, 0, 0)),
                  vec, vec, vec,
                  pl.BlockSpec((CHUNK, CHUNK), lambda g, c: (0, 0))] + comm.in_specs,
        out_specs=[blk, xblk, blk, vec, vec, vec] + comm.out_specs,
        out_shape=[
            jax.ShapeDtypeStruct((lp, NP), BF16),
            jax.ShapeDtypeStruct((lp, D_CONV), F32),
            jax.ShapeDtypeStruct((lp, D_SSD), F32),
            jax.ShapeDtypeStruct((1, D_SSD), F32),
            jax.ShapeDtypeStruct((1, D_SSD), F32),
            jax.ShapeDtypeStruct((1, D_SSD), F32),
        ] + comm.out_shape,
        scratch_shapes=[pltpu.VMEM((gps, D_STATE, GROUP_W), F32)] + comm.scratch,
        compiler_params=_params(("arbitrary", "arbitrary"), side_effects=True),
    )(dmix, ytot, proj, xbc, dt_rep, hprev, a_rep, dsk_rep, wn, tri, *comm.operands)
    return res


def _stack_heads(t, h):
    return jnp.concatenate([t[:, (REP * h + r) * HEAD_DIM:(REP * h + r + 1) * HEAD_DIM] for r in range(REP)], axis=0)


def _band(t2, t1, t0, h):
    sl = slice(h * HEAD_DIM, (h + 1) * HEAD_DIM)
    return jnp.concatenate([t2[:, sl], t1[:, sl], t0[:, sl]], axis=0)


def _attn_probs(s, sink, qc):
    s = s * (HEAD_DIM ** -0.5)
    key_abs = (qc - WINDOW_CHUNKS) * CHUNK + lax.broadcasted_iota(jnp.int32, s.shape, 1)
    s = jnp.where(key_abs >= PAD_LEAD, s, -jnp.inf)
    m = jnp.maximum(jnp.max(s, axis=-1, keepdims=True), sink)
    p = jnp.exp(s - m)
    ps = jnp.exp(sink - m)
    denom = jnp.sum(p, axis=-1, keepdims=True) + ps
    return p / denom, ps / denom


ATT_QC_FWD = 6
ATT_QC_BWD = 2


def _kv_specs(width, newest_chunk_of, kc):
    return [pl.BlockSpec((CHUNK, width), functools.partial(lambda j, p: (jnp.maximum(newest_chunk_of(p) - j, 0), 0), j))
            for j in range(kc - 1, -1, -1)]


def _attn_fwd(qr, kr, vb, proj, sink_stack, mix):
    lp = qr.shape[0]
    nc = lp // CHUNK
    qn = _tile(nc, (ATT_QC_FWD, 2, 1))
    kn = WINDOW_CHUNKS + qn
    qrows = qn * CHUNK

    def body(q_ref, *rest):
        k_refs, v_refs = rest[:kn], rest[kn:2 * kn]
        g_ref, sink_ref, _, att_ref, mix_ref = rest[2 * kn:]
        p = pl.program_id(0)
        q = q_ref[...]
        ks = [r[...] for r in k_refs]
        vs = [r[...] for r in v_refs]
        units = [(u, h) for u in range(qn) for h in range(KV_HEADS)]
        s = [_dot(_stack_heads(q[u * CHUNK:(u + 1) * CHUNK, :], h), _band(*ks[u:u + 3], h), NT) for u, h in units]
        vbh = [_band(*vs[u:u + 3], h) for u, h in units]
        pn = [_attn_probs(s[i], sink_ref[h], qn * p + u)[0].astype(BF16) for i, (u, h) in enumerate(units)]
        o = [_dot(pn[i], vbh[i]) for i in range(len(units))]
        att = jnp.concatenate(
            [jnp.concatenate([o[u * KV_HEADS + h][r * CHUNK:(r + 1) * CHUNK, :] for h in range(KV_HEADS) for r in range(REP)],
                             axis=1) for u in range(qn)], axis=0)
        att_ref[...] = att
        g = g_ref[...]
        mix_ref[...] = (att * (g * _sigmoid(g))).astype(BF16)

    newest = lambda p: qn * p + qn - 1
    return pl.pallas_call(
        body, name="attn_fwd", grid=(nc // qn,),
        in_specs=[pl.BlockSpec((qrows, D_ATT), lambda p: (p, 0))] + _kv_specs(D_KV, newest, kn) + _kv_specs(D_KV, newest, kn) + [
            pl.BlockSpec((qrows, D_ATT), lambda p: (p, OFF_G // D_ATT)),
            pl.BlockSpec((KV_HEADS, REP * CHUNK, 1), lambda p: (0, 0, 0)),
            ANY,
        ],
        out_specs=[pl.BlockSpec((qrows, D_ATT), lambda p: (p, 0)),
                   pl.BlockSpec((qrows, D_ATT), lambda p: (p, D_SSD // D_ATT))],
        out_shape=[jax.ShapeDtypeStruct((lp, D_ATT), F32), jax.ShapeDtypeStruct((lp, D_MIX), BF16)],
        input_output_aliases={2 * kn + 3: 1},
        compiler_params=_params(("parallel",)),
    )(qr, *([kr] * kn), *([vb] * kn), proj, sink_stack, mix)


def _attn_bwd(qr, kr, vb, att, proj, dmix, sink_stack):
    lp = qr.shape[0]
    nc = lp // CHUNK
    qn = ATT_QC_BWD
    kn = WINDOW_CHUNKS + qn
    assert nc % qn == 0 and WINDOW_CHUNKS % qn == 0
    steps = nc // qn
    qrows = qn * CHUNK
    wrows = kn * CHUNK

    def body(q_ref, *rest):
        k_refs, v_refs = rest[:kn], rest[kn:2 * kn]
        (att_ref, g_ref, do_ref, sink_ref, dq_ref, dk_ref, dv_ref, dg_ref, dsink_ref, dk_acc, dv_acc) = rest[2 * kn:]
        step = pl.program_id(0)

        @pl.when(step == 0)
        def _():
            dk_acc[...] = jnp.zeros_like(dk_acc)
            dv_acc[...] = jnp.zeros_like(dv_acc)
            dsink_ref[...] = jnp.zeros_like(dsink_ref)

        q = q_ref[...]
        ks = [r[...] for r in k_refs]
        vs = [r[...] for r in v_refs]
        att = att_ref[...]
        g = g_ref[...]
        dog = do_ref[...]
        sg = _sigmoid(g)
        dg_ref[...] = dog * att * (sg * (1.0 + g * (1.0 - sg)))
        do = dog * (g * sg)
        units = [(u, h) for u in range(qn) for h in range(KV_HEADS)]
        n = range(len(units))
        rows = [slice(u * CHUNK, (u + 1) * CHUNK) for u in range(qn)]
        qs = [_stack_heads(q[rows[u], :], h) for u, h in units]
        kb = [_band(*ks[u:u + 3], h) for u, h in units]
        vbh = [_band(*vs[u:u + 3], h) for u, h in units]
        dos = [_stack_heads(do[rows[u], :], h) for u, h in units]
        dosb = [dos[i].astype(BF16) for i in n]
        s = [_dot(qs[i], kb[i], NT) for i in n]
        dp = [_dot(dosb[i], vbh[i], NT) for i in n]
        ds, pnb, dsink = [], [], []
        for i, (u, h) in enumerate(units):
            pn, psink = _attn_probs(s[i], sink_ref[h], qn * jnp.minimum(step, steps - 1) + u)
            delta = jnp.sum(dos[i] * _stack_heads(att[rows[u], :], h), axis=-1, keepdims=True)
            ds.append((pn * (dp[i] - delta)).astype(BF16))
            pnb.append(pn.astype(BF16))
            dsink.append(-(psink * delta))
        dqs = [_dot(ds[i], kb[i]) for i in n]
        dks = [_dot(ds[i], qs[i], TN) for i in n]
        dvs = [_dot(pnb[i], dosb[i], TN) for i in n]
        dq_ref[...] = jnp.concatenate(
            [jnp.concatenate([dqs[u * KV_HEADS + h][r * CHUNK:(r + 1) * CHUNK, :]
                              for h in range(KV_HEADS) for r in range(REP)], axis=1) for u in range(qn)],
            axis=0) * (HEAD_DIM ** -0.5)
        @pl.when(step < steps)
        def _():
            for i, (u, h) in enumerate(units):
                dsink_ref[h] += dsink[i]
            for u in range(qn):
                band = slice(u * CHUNK, u * CHUNK + BAND)
                dk_acc[band, :] += jnp.concatenate(dks[u * KV_HEADS:(u + 1) * KV_HEADS], axis=1) * (HEAD_DIM ** -0.5)
                dv_acc[band, :] += jnp.concatenate(dvs[u * KV_HEADS:(u + 1) * KV_HEADS], axis=1)

        dk_ref[...] = dk_acc[0:qrows, :]
        dv_ref[...] = dv_acc[0:qrows, :]
        for acc in (dk_acc, dv_acc):
            rest_rows = acc[qrows:wrows, :]
            acc[0:wrows - qrows, :] = rest_rows
            acc[wrows - qrows:wrows, :] = jnp.zeros((qrows, D_KV), F32)

    qp = lambda p: jnp.minimum(p, steps - 1)
    newest = lambda p: qn * qp(p) + qn - 1
    qblk = pl.BlockSpec((qrows, D_ATT), lambda p: (qp(p), 0))
    oldest = pl.BlockSpec((qrows, D_KV), lambda p: (jnp.maximum(p - 1, 0), 0))
    return pl.pallas_call(
        body, name="attn_bwd", grid=(steps + 1,),
        in_specs=[qblk] + _kv_specs(D_KV, newest, kn) + _kv_specs(D_KV, newest, kn) + [
            qblk,
            pl.BlockSpec((qrows, D_ATT), lambda p: (qp(p), OFF_G // D_ATT)),
            pl.BlockSpec((qrows, D_ATT), lambda p: (qp(p), D_SSD // D_ATT)),
            pl.BlockSpec((KV_HEADS, REP * CHUNK, 1), lambda p: (0, 0, 0)),
        ],
        out_specs=[qblk, oldest, oldest, qblk, pl.BlockSpec((KV_HEADS, REP * CHUNK, 1), lambda p: (0, 0, 0))],
        out_shape=[
            jax.ShapeDtypeStruct((lp, D_ATT), F32),
            jax.ShapeDtypeStruct((lp, D_KV), F32),
            jax.ShapeDtypeStruct((lp, D_KV), F32),
            jax.ShapeDtypeStruct((lp, D_ATT), F32),
            jax.ShapeDtypeStruct((KV_HEADS, REP * CHUNK, 1), F32),
        ],
        scratch_shapes=[pltpu.VMEM((wrows, D_KV), F32), pltpu.VMEM((wrows, D_KV), F32)],
        compiler_params=_params(("arbitrary",)),
    )(qr, *([kr] * kn), *([vb] * kn), att, proj, dmix, sink_stack)


def _post_loss(out, x2d, target, w):
    lp = out.shape[0]
    nc = lp // CHUNK
    nx = x2d.shape[0] // CHUNK

    k = _tile(nc, (ROW_K, 3, 2, 1))

    def body(o_ref, *rest):
        x_refs, t_refs = rest[:k], rest[k:2 * k]
        w_ref, do_ref, dy_ref, gw_ref, loss_ref = rest[2 * k:]
        i = pl.program_id(0)

        @pl.when(i == 0)
        def _():
            gw_ref[...] = jnp.zeros_like(gw_ref)
            loss_ref[...] = jnp.zeros_like(loss_ref)

        o = o_ref[...]
        w = w_ref[...]
        rstd = lax.rsqrt(jnp.mean(o * o, axis=-1, keepdims=True) + EPS)
        xhat = o * rstd
        x = jnp.concatenate([r[...] for r in x_refs], axis=0)
        t = jnp.concatenate([r[...] for r in t_refs], axis=0)
        chunk = _chunk_of_row(i, k)
        err = (x + xhat * w - t) * ((chunk > 0) & (chunk <= nx)).astype(F32)
        loss_ref[...] += 0.5 * jnp.sum(jnp.mean(err * err, axis=-1, keepdims=True), axis=0, keepdims=True)
        dy = err * (1.0 / D_MODEL)
        dy_ref[...] = dy
        gw_ref[...] += jnp.sum(dy * xhat, axis=0, keepdims=True)
        dxhat = dy * w
        do_ref[...] = (rstd * (dxhat - xhat * jnp.mean(dxhat * xhat, axis=-1, keepdims=True))).astype(BF16)

    row = pl.BlockSpec((k * CHUNK, D_MODEL), lambda i: (i, 0))
    return pl.pallas_call(
        body, name="post_loss", grid=(nc // k,),
        in_specs=[row] + _x_specs(nx, k) + _x_specs(nx, k) + [pl.BlockSpec((1, D_MODEL), lambda i: (0, 0))],
        out_specs=[row, row, pl.BlockSpec((1, D_MODEL), lambda i: (0, 0)), pl.BlockSpec((1, 128), lambda i: (0, 0))],
        out_shape=[
            jax.ShapeDtypeStruct((lp, D_MODEL), BF16),
            jax.ShapeDtypeStruct((lp, D_MODEL), F32),
            jax.ShapeDtypeStruct((1, D_MODEL), F32),
            jax.ShapeDtypeStruct((1, 128), F32),
        ],
        compiler_params=_params(("arbitrary",)),
    )(out, *([x2d] * k), *([target] * k), w)


def _prenorm_bwd(dhn, x2d, h0, dy, w):
    nx = x2d.shape[0] // CHUNK
    k = _tile(nx, (X_K, 4, 2, 1))
    rows = k * CHUNK

    def backward(h, dhn, w):
        rstd = lax.rsqrt(jnp.mean(h * h, axis=-1, keepdims=True) + EPS)
        xhat = h * rstd
        dxhat = dhn * w
        dh = rstd * (dxhat - xhat * jnp.mean(dxhat * xhat, axis=-1, keepdims=True))
        return dh, jnp.sum(dhn * xhat, axis=0, keepdims=True)

    def body(*refs):
        dhn_refs, dy_refs = refs[:k], refs[k:2 * k]
        x_ref, w_ref, gx_ref, gw_ref = refs[2 * k:]

        @pl.when(pl.program_id(0) == 0)
        def _():
            gw_ref[...] = jnp.zeros_like(gw_ref)

        dh, gw = backward(x_ref[...], jnp.concatenate([r[...] for r in dhn_refs], axis=0), w_ref[...])
        gx_ref[...] = dh + jnp.concatenate([r[...] for r in dy_refs], axis=0)
        gw_ref[...] += gw

    def body_meta(dhn_ref, h0_ref, w_ref, d0_ref, gw_ref):
        d0_ref[...], gw_ref[...] = backward(h0_ref[...], dhn_ref[...], w_ref[...])

    chunk_specs = [pl.BlockSpec((CHUNK, D_MODEL), functools.partial(lambda u, i: (k * i + 1 + u, 0), u)) for u in range(k)]
    first = pl.BlockSpec((CHUNK, D_MODEL), lambda i: (0, 0))
    vec = pl.BlockSpec((1, D_MODEL), lambda i: (0, 0))
    wide = pl.BlockSpec((rows, D_MODEL), lambda i: (i, 0))
    gx, gw_x = pl.pallas_call(
        body, name="prenorm_bwd", grid=(nx // k,),
        in_specs=chunk_specs + chunk_specs + [wide, vec],
        out_specs=[wide, vec],
        out_shape=[jax.ShapeDtypeStruct((nx * CHUNK, D_MODEL), F32), jax.ShapeDtypeStruct((1, D_MODEL), F32)],
        compiler_params=_params(("arbitrary",)),
    )(*([dhn] * k), *([dy] * k), x2d, w)
    d0, gw_0 = pl.pallas_call(
        body_meta, name="prenorm_bwd_meta", grid=(1,),
        in_specs=[first, first, vec], out_specs=[first, vec],
        out_shape=[jax.ShapeDtypeStruct((CHUNK, D_MODEL), F32), jax.ShapeDtypeStruct((1, D_MODEL), F32)],
        compiler_params=_params(("arbitrary",)),
    )(dhn, h0, w)
    return gx, d0, gw_x + gw_0


def _adamw(slabs, w, m, v, name):
    rows, cols = w.shape
    tr = _tile(rows, (256, 128, 64, 16, 8))
    c1 = 1.0 - ADAM_B1 ** ADAM_STEP
    c2 = 1.0 - ADAM_B2 ** ADAM_STEP

    def body(s_ref, w_ref, m_ref, v_ref, g_ref, d_ref, mo_ref, vo_ref):
        g = s_ref[0].astype(F32)
        for k in range(1, slabs.shape[0]):
            g = g + s_ref[k].astype(F32)
        w = w_ref[...]
        m = ADAM_B1 * m_ref[...] + (1.0 - ADAM_B1) * g
        v = ADAM_B2 * v_ref[...] + (1.0 - ADAM_B2) * (g * g)
        g_ref[...] = g
        mo_ref[...] = m
        vo_ref[...] = v
        d_ref[...] = -ADAM_LR * ((m / c1) / (jnp.sqrt(v / c2) + ADAM_EPS) + ADAM_WD * w)

    blk = pl.BlockSpec((tr, cols), lambda i: (i, 0))
    shape = jax.ShapeDtypeStruct((rows, cols), F32)
    return pl.pallas_call(
        body, name=name, grid=(rows // tr,),
        in_specs=[pl.BlockSpec((slabs.shape[0], tr, cols), lambda i: (0, i, 0)), blk, blk, blk],
        out_specs=[blk, blk, blk, blk],
        out_shape=[shape, shape, shape, shape],
        compiler_params=_params(("parallel",)),
    )(slabs, w, m, v)


def _perm_xbc(a):
    lead = a.shape[:-1]
    xs = a[..., :D_SSD].reshape(lead + (N_GROUPS, GROUP_W))
    b = a[..., D_SSD:D_SSD + N_GROUPS * D_STATE].reshape(lead + (N_GROUPS, D_STATE))
    c = a[..., D_SSD + N_GROUPS * D_STATE:].reshape(lead + (N_GROUPS, D_STATE))
    return jnp.concatenate([xs, b, c], axis=-1).reshape(lead + (D_CONV,))


def _unperm_xbc(a):
    lead = a.shape[:-1]
    t = a.reshape(lead + (N_GROUPS, XBC_BLK))
    xs = t[..., :GROUP_W].reshape(lead + (D_SSD,))
    b = t[..., GROUP_W:GROUP_W + D_STATE].reshape(lead + (N_GROUPS * D_STATE,))
    c = t[..., GROUP_W + D_STATE:].reshape(lead + (N_GROUPS * D_STATE,))
    return jnp.concatenate([xs, b, c], axis=-1)


R_Z, R_XBC, R_DT, R_Q, R_K, R_V, R_G = 0, 2048, 6144, 6176, 7200, 7456, 7712


def _internal_of_reference():
    ref = np.arange(D_IN_PROJ)
    out = np.empty(D_IN_PROJ, np.int64)
    out[R_Z:R_XBC] = OFF_Z + ref[:D_SSD]
    xs = np.arange(D_SSD)
    out[R_XBC:R_XBC + D_SSD] = OFF_XBC + (xs // GROUP_W) * XBC_BLK + xs % GROUP_W
    bc = np.arange(N_GROUPS * D_STATE)
    out[R_XBC + D_SSD:R_XBC + D_SSD + N_GROUPS * D_STATE] = OFF_XBC + (bc // D_STATE) * XBC_BLK + GROUP_W + bc % D_STATE
    out[R_XBC + D_SSD + N_GROUPS * D_STATE:R_DT] = OFF_XBC + (bc // D_STATE) * XBC_BLK + GROUP_W + D_STATE + bc % D_STATE
    out[R_DT:R_Q] = OFF_DT + np.arange(SSD_HEADS)
    out[R_Q:R_K] = OFF_Q + np.arange(D_ATT)
    out[R_K:R_V] = OFF_K + np.arange(D_KV)
    out[R_V:R_G] = OFF_V + np.arange(D_KV)
    out[R_G:] = OFF_G + np.arange(D_ATT)
    return out


def _runs(src, dst_break):
    runs, lo = [], 0
    for i in range(1, len(src) + 1):
        if i == len(src) or src[i] != src[i - 1] + 1 or dst_break[i] != dst_break[i - 1]:
            runs.append((lo, i))
            lo = i
    return runs


RELAYOUT_ROWS = 256


def _lane_window(ref, lead, c0, n):
    a0 = c0 // 128 * 128
    a1 = min(-(-(c0 + n) // 128) * 128, ref.shape[-1])
    return ref[lead + (slice(None), slice(a0, a1))][:, c0 - a0:c0 - a0 + n]


def _w_in_internal(w_gathered):
    int_of_ref = _internal_of_reference()
    ref_of_int = np.full(NP, -1, np.int64)
    ref_of_int[int_of_ref] = np.arange(D_IN_PROJ)
    shard = np.where(ref_of_int >= 0, ref_of_int // SHARD_IN, -1)
    src = np.where(ref_of_int >= 0, ref_of_int, -10 - 2 * np.arange(NP))
    plan, zeros = [], 0
    for lo, hi in _runs(src, shard):
        if ref_of_int[lo] < 0:
            zeros += hi - lo
            continue
        if zeros:
            plan.append((None, 0, zeros))
            zeros = 0
        plan.append((int(ref_of_int[lo] // SHARD_IN), int(ref_of_int[lo] % SHARD_IN), hi - lo))
    if zeros:
        plan.append((None, 0, zeros))
    tr = RELAYOUT_ROWS

    def body(w_ref, o_ref):
        o_ref[...] = jnp.concatenate(
            [jnp.zeros((tr, n), o_ref.dtype) if s is None else _lane_window(w_ref, (s,), c0, n) for s, c0, n in plan], axis=1)

    return pl.pallas_call(
        body, name="w_in_relayout", grid=(D_MODEL // tr,),
        in_specs=[pl.BlockSpec((N_DEV, tr, SHARD_IN), lambda i: (0, i, 0))],
        out_specs=pl.BlockSpec((tr, NP), lambda i: (i, 0)),
        out_shape=jax.ShapeDtypeStruct((D_MODEL, NP), w_gathered.dtype),
        compiler_params=_params(("parallel",)),
    )(w_gathered)


def _w_in_slabs(dw):
    int_of_ref = _internal_of_reference()
    plan = []
    for s in range(N_DEV):
        cols = int_of_ref[s * SHARD_IN:(s + 1) * SHARD_IN]
        plan.append([(int(cols[lo]), hi - lo) for lo, hi in _runs(cols, np.zeros_like(cols))])
    tr = RELAYOUT_ROWS
    steps = D_MODEL // tr
    pairs = N_DEV // 2

    def body(dw_ref, o_ref, land_ref, send_buf, send_sems, recv_sems):
        i = pl.program_id(0)
        x, y, c = lax.axis_index("x"), lax.axis_index("y"), lax.axis_index("c")
        slot = i % 2

        def sends(step, sl):
            return [pltpu.make_async_remote_copy(
                src_ref=send_buf.at[sl, 2 * q + (1 - c)], dst_ref=land_ref.at[q, pl.ds(step * tr, tr), :],
                send_sem=send_sems.at[sl, q], recv_sem=recv_sems.at[q],
                device_id=(x, y, 1 - c), device_id_type=pl.DeviceIdType.MESH) for q in range(pairs)]

        @pl.when(i >= 2)
        def _():
            for cp in sends(i - 2, slot):
                cp.wait_send()

        for s in range(N_DEV):
            tile = jnp.concatenate([_lane_window(dw_ref, (), c0, n) for c0, n in plan[s]], axis=1)
            o_ref[s] = tile
            send_buf[slot, s] = tile
        for cp in sends(i, slot):
            cp.start()

        @pl.when(i == steps - 1)
        def _():
            for cp in sends(i - 1, 1 - slot) + sends(i, slot):
                cp.wait_send()
            for q in range(pairs):
                pltpu.make_async_remote_copy(
                    src_ref=land_ref.at[q], dst_ref=land_ref.at[q], send_sem=send_sems.at[0, q], recv_sem=recv_sems.at[q],
                    device_id=(x, y, 1 - c), device_id_type=pl.DeviceIdType.MESH).wait_recv()

    return pl.pallas_call(
        body, name="dw_in_relayout", grid=(steps,),
        in_specs=[pl.BlockSpec((tr, NP), lambda i: (i, 0))],
        out_specs=[pl.BlockSpec((N_DEV, tr, SHARD_IN), lambda i: (0, i, 0)), ANY],
        out_shape=[jax.ShapeDtypeStruct((N_DEV, D_MODEL, SHARD_IN), dw.dtype),
                   jax.ShapeDtypeStruct((pairs, D_MODEL, SHARD_IN), dw.dtype)],
        scratch_shapes=[pltpu.VMEM((2, N_DEV, tr, SHARD_IN), dw.dtype), pltpu.SemaphoreType.DMA((2, pairs)),
                        pltpu.SemaphoreType.DMA((pairs,))],
        compiler_params=_params(("arbitrary",), side_effects=True),
    )(dw)


def _rep_heads(a):
    return jnp.repeat(a, HEAD_DIM, axis=1)


SMALL = (("norm_pre_w", 2048), ("conv_b", 4096), ("dt_bias", 32), ("a_log", 32), ("d_skip", 32),
         ("ssd_norm_w", 2048), ("attn_sinks", 16), ("norm_post_w", 2048))
SMALL_USED = sum(size for _, size in SMALL)
SMALL_LEN = 10368


def _pack_small(d, loss=None):
    parts = [d[name].reshape(1, size) for name, size in SMALL]
    tail = jnp.zeros((1, SMALL_LEN - SMALL_USED), F32)
    if loss is not None:
        tail = tail.at[0, 0].set(loss)
    return jnp.concatenate(parts + [tail], axis=1)


def _unpack_small(vec):
    out, off = {}, 0
    for name, size in SMALL:
        out[name] = vec[:, off:off + size]
        off += size
    return out


def kernel(x, meta_tokens, norm_pre_w, w_in, conv_w, conv_b, dt_bias, a_log, d_skip, ssd_norm_w, attn_sinks, w_out, norm_post_w, loss_target, m_meta_tokens, m_norm_pre_w, m_w_in, m_conv_w, m_conv_b, m_dt_bias, m_a_log, m_d_skip, m_ssd_norm_w, m_attn_sinks, m_w_out, m_norm_post_w, v_meta_tokens, v_norm_pre_w, v_w_in, v_conv_w, v_conv_b, v_dt_bias, v_a_log, v_d_skip, v_ssd_norm_w, v_attn_sinks, v_w_out, v_norm_post_w):
    seq = x.shape[1]
    lp = seq + 2 * CHUNK
    x2d = x[0]

    w_in_g, conv_w_g, meta_g = _gather_two_level([w_in[0].astype(BF16), conv_w[0], meta_tokens], "gather_w_in")
    w_all = _w_in_internal(w_in_g)
    conv_w_full = _perm_xbc(jnp.transpose(conv_w_g, (1, 0, 2)).reshape(CONV_WIDTH, D_CONV))
    conv_b_int = _perm_xbc(conv_b)
    meta_full = jnp.transpose(meta_g, (1, 0, 2)).reshape(N_META, D_MODEL)
    h0 = jnp.concatenate([jnp.zeros((PAD_LEAD, D_MODEL), F32), meta_full], axis=0)

    pos = (jnp.arange(lp) - PAD_LEAD).astype(F32)
    half = HEAD_DIM // 2
    inv = ROPE_THETA ** (-jnp.arange(half, dtype=F32) / half)
    ang = pos[:, None] * inv[None, :]
    cos_t = jnp.tile(jnp.cos(ang), (1, 4))
    sin_t = jnp.tile(jnp.concatenate([-jnp.sin(ang), jnp.sin(ang)], axis=1), (1, 2))
    head_of_col = np.arange(D_SSD) // HEAD_DIM
    expand = jnp.asarray((np.arange(128)[:, None] == head_of_col[None, :]).astype(np.float32))
    reduce_t = jnp.asarray((head_of_col[:, None] == np.arange(128)[None, :]).astype(np.float32))
    tri = jnp.asarray(np.tril(np.ones((CHUNK, CHUNK), np.float32)))
    a_rep = _rep_heads(-jnp.exp(a_log))
    dsk_rep = _rep_heads(d_skip)
    dt_bias_pad = jnp.pad(dt_bias, ((0, 0), (0, 128 - SSD_HEADS)))
    sink_stack = jnp.repeat(attn_sinks.reshape(KV_HEADS, REP), CHUNK, axis=1).reshape(KV_HEADS, REP * CHUNK, 1)

    hn = _prenorm(x2d, h0, norm_pre_w)
    tm = _tile(lp, (1056, 704, 128, 64))
    proj, w_out_g = _matmul(hn, w_all, tm=tm, tn=1536, tk=D_MODEL, out_dtype=F32, name="in_proj",
                            comm=_Comm([(w_out[0].astype(BF16), "gather")]))
    w_out_full = w_out_g.reshape(D_MIX, D_MODEL)
    qr, kr, vb, dt_rep = _act_fwd(proj, cos_t, sin_t, expand, dt_bias_pad)
    mix, ytot, hprev, xbc = _ssd_fwd(proj, conv_w_full, conv_b_int, dt_rep, a_rep, dsk_rep, ssd_norm_w, tri)
    att, mix = _attn_fwd(qr, kr, vb, proj, sink_stack, mix)
    out = _matmul(mix, w_out_full, tm=tm, tn=1024, tk=D_MIX, out_dtype=F32, name="out_proj")
    dout, dy, g_norm_post, loss_part = _post_loss(out, x2d, loss_target[0], norm_post_w)

    dmix = _matmul(dout, w_out_full, trans_b=True, tm=tm, tn=1024, tk=D_MODEL, out_dtype=F32, name="dmix")
    dw_out = _matmul(mix, dout, trans_a=True, tm=512, tn=1024, tk=lp, out_dtype=BF16, name="dw_out")
    dqr, dkr, dv, dg, dsink_rows = _attn_bwd(qr, kr, vb, att, proj, dmix, sink_stack)
    dproj, dxbc, ddt_part, dd_part, da_part, g_ssd_norm, g_out = _ssd_bwd(
        dmix, ytot, proj, xbc, dt_rep, hprev, a_rep, dsk_rep, ssd_norm_w, tri,
        _Comm([(dw_out.reshape(N_DEV, D_MIX // N_DEV, D_MODEL), "scatter")]))
    dproj, dconv_w_int, dconv_b_int = _conv_bwd(dxbc, proj, conv_w_full, conv_b_int, dproj)
    dproj, ddt_bias = _act_bwd(dqr, dkr, dv, dg, ddt_part, cos_t, sin_t, reduce_t, dproj)
    dw_all = _matmul(hn, dproj, trans_a=True, tm=512, tn=1024, tk=lp, out_dtype=BF16, name="dw_in")
    dw_slabs, dw_landed = _w_in_slabs(dw_all)
    dw_chip = _pair_sum(dw_slabs, dw_landed, "dw_in_pair_sum")
    dhn, g_in = _matmul(dproj, w_all, trans_b=True, tm=tm, tn=1024, tk=1536, out_dtype=F32, name="dhn",
                        comm=_Comm([(dw_chip, "scatter")], scope="chips"))
    grad_x, dh0, g_norm_pre = _prenorm_bwd(dhn, x2d, h0, dy, norm_pre_w)

    dmeta = dh0[PAD_LEAD:, :]
    dconv_w_ref = _unperm_xbc(dconv_w_int)
    heads = lambda part: part.reshape(SSD_HEADS, HEAD_DIM).sum(axis=1).reshape(1, SSD_HEADS)
    small_local = _pack_small({
        "norm_pre_w": g_norm_pre, "conv_b": _unperm_xbc(dconv_b_int), "dt_bias": ddt_bias[:, :SSD_HEADS],
        "a_log": heads(da_part) * (-jnp.exp(a_log)), "d_skip": heads(dd_part), "ssd_norm_w": g_ssd_norm,
        "attn_sinks": dsink_rows.reshape(Q_HEADS, CHUNK).sum(axis=1).reshape(1, Q_HEADS),
        "norm_post_w": g_norm_post}, loss=loss_part[0, 0])
    g_conv, g_meta, g_small = _exchange(
        [(jnp.transpose(dconv_w_ref.reshape(CONV_WIDTH, N_DEV, D_CONV // N_DEV), (1, 0, 2)), "scatter"),
         (jnp.transpose(dmeta.reshape(N_META, N_DEV, D_MODEL // N_DEV), (1, 0, 2)), "scatter"),
         (small_local, "gather")], "exchange_small")

    res = {}
    res["w_in"] = [o[None] for o in _adamw(g_in, w_in[0], m_w_in[0], v_w_in[0], "adamw_w_in")]
    res["w_out"] = [o[None] for o in _adamw(g_out, w_out[0], m_w_out[0], v_w_out[0], "adamw_w_out")]
    res["conv_w"] = [o[None] for o in _adamw(g_conv, conv_w[0], m_conv_w[0], v_conv_w[0], "adamw_conv_w")]
    res["meta_tokens"] = _adamw(g_meta, meta_tokens, m_meta_tokens, v_meta_tokens, "adamw_meta")
    given = dict(norm_pre_w=(norm_pre_w, m_norm_pre_w, v_norm_pre_w), conv_b=(conv_b, m_conv_b, v_conv_b),
                 dt_bias=(dt_bias, m_dt_bias, v_dt_bias), a_log=(a_log, m_a_log, v_a_log),
                 d_skip=(d_skip, m_d_skip, v_d_skip), ssd_norm_w=(ssd_norm_w, m_ssd_norm_w, v_ssd_norm_w),
                 attn_sinks=(attn_sinks, m_attn_sinks, v_attn_sinks),
                 norm_post_w=(norm_post_w, m_norm_post_w, v_norm_post_w))
    packed = [_pack_small({k: t[j] for k, t in given.items()}) for j in range(3)]
    small_out = _adamw(g_small, packed[0], packed[1], packed[2], "adamw_small")
    small_res = [_unpack_small(r) for r in small_out]
    loss = small_out[0][0, SMALL_USED]

    order = ["meta_tokens", "norm_pre_w", "w_in", "conv_w", "conv_b", "dt_bias", "a_log", "d_skip", "ssd_norm_w",
             "attn_sinks", "w_out", "norm_post_w"]
    outs = []
    for j in range(4):
        for name in order:
            outs.append(res[name][j] if name in res else small_res[j][name])
    return (loss, grad_x[None], *outs)
```

```python
import functools

import numpy as np
import jax
import jax.numpy as jnp
from jax import lax
from jax.experimental import pallas as pl
from jax.experimental.pallas import tpu as pltpu

F32 = jnp.float32
BF16 = jnp.bfloat16
HIGHEST = lax.Precision.HIGHEST

N_DEV = 8
D_MODEL = 2048
CHUNK = 64
N_META = 16
PAD_LEAD = CHUNK - N_META
EPS = 1e-6
N_GROUPS = 8
HEADS_PER_GROUP = 4
HEAD_DIM = 64
GROUP_W = HEADS_PER_GROUP * HEAD_DIM
D_STATE = 128
D_SSD = 2048
D_CONV = 4096
SSD_HEADS = 32
CONV_WIDTH = 4
Q_HEADS = 16
KV_HEADS = 4
REP = 4
D_ATT = 1024
D_KV = 256
WINDOW_CHUNKS = 2
BAND = (WINDOW_CHUNKS + 1) * CHUNK
ROPE_THETA = 10000.0
D_MIX = D_SSD + D_ATT
D_IN_PROJ = 8736
SHARD_IN = D_IN_PROJ // N_DEV

OFF_Z, OFF_XBC, OFF_Q, OFF_G, OFF_K, OFF_V, OFF_DT = 0, 2048, 6144, 7168, 8192, 8448, 8704
NP = 9216
TAIL_W = NP - OFF_Q
XBC_BLK = 512
SSD_GPS = 8
SSD_CPS = 2

ADAM_LR, ADAM_B1, ADAM_B2, ADAM_EPS, ADAM_WD, ADAM_STEP = 0.001, 0.9, 0.999, 1e-08, 0.01, 10

VMEM_LIMIT = 48 * 1024 * 1024

NN = (((1,), (0,)), ((), ()))
NT = (((1,), (1,)), ((), ()))
TN = (((0,), (0,)), ((), ()))
ANY = pl.BlockSpec(memory_space=pl.ANY)


def _dot(a, b, dims=NN, precision=None):
    return lax.dot_general(a, b, dims, precision=precision, preferred_element_type=F32)


def _tile(n, prefs):
    for t in prefs:
        if n % t == 0:
            return t
    return n


def _params(sem, vmem=VMEM_LIMIT, side_effects=False):
    return pltpu.CompilerParams(dimension_semantics=sem, vmem_limit_bytes=vmem, has_side_effects=side_effects)


def _sigmoid(x):
    return 1.0 / (1.0 + jnp.exp(-x))


class _Comm:
    def __init__(self, items, scope="devices"):
        self.items = items
        self.scope = scope
        self.slabs = slabs = N_DEV if scope == "devices" else N_DEV // 2
        self.n = n = len(items)
        self.operands = [arr for arr, _ in items]
        self.in_specs = [ANY] * n
        self.out_specs = [ANY] * n
        self.out_shape = [jax.ShapeDtypeStruct((slabs,) + tuple(arr.shape) if kind == "gather" else tuple(arr.shape),
                                               arr.dtype) for arr, kind in items]
        self.scratch = [pltpu.SemaphoreType.DMA((n, slabs - 1)), pltpu.SemaphoreType.DMA((n, slabs - 1)),
                        pltpu.SemaphoreType.DMA((n,))]

    def _places(self):
        pos = (lax.axis_index("x"), lax.axis_index("y"), lax.axis_index("c"))
        if self.scope == "devices":
            index = lambda p: 4 * p[0] + 2 * p[1] + p[2]
            masks = range(1, N_DEV)
        else:
            index = lambda p: 2 * p[0] + p[1]
            masks = (2, 4, 6)
        peers = []
        for k in masks:
            p = tuple(1 - pos[b] if (k >> (2 - b)) & 1 else pos[b] for b in range(3))
            peers.append((p, index(p)))
        return index(pos), peers

    def _copies(self, ins, outs, sems, landed):
        send_sems, recv_sems, local_sems = sems
        me, peers = self._places()
        local, remote = [], []
        for a, (_, kind) in enumerate(self.items):
            own = ins[a] if kind == "gather" else ins[a].at[me]
            local.append(pltpu.make_async_copy(own, outs[a].at[me], local_sems.at[a]))
            for k, (p, pid) in enumerate(peers):
                remote.append(pltpu.make_async_remote_copy(
                    src_ref=ins[a] if kind == "gather" else ins[a].at[pid],
                    dst_ref=outs[a].at[pid if landed else me],
                    send_sem=send_sems.at[a, k], recv_sem=recv_sems.at[a, k],
                    device_id=p, device_id_type=pl.DeviceIdType.MESH))
        return local, remote

    def start(self, ins, outs, sems):
        local, remote = self._copies(ins, outs, sems, landed=False)
        for cp in local + remote:
            cp.start()

    def wait(self, ins, outs, sems):
        local, remote = self._copies(ins, outs, sems, landed=True)
        for cp in remote + local:
            cp.wait()


def _exchange(items, name):
    comm = _Comm(items)
    n = comm.n

    def body(*refs):
        ins, outs, sems = refs[:n], refs[n:2 * n], refs[2 * n:]
        comm.start(ins, outs, sems)
        comm.wait(ins, outs, sems)

    return pl.pallas_call(
        body, name=name, in_specs=comm.in_specs, out_specs=comm.out_specs, out_shape=comm.out_shape,
        scratch_shapes=comm.scratch, compiler_params=pltpu.CompilerParams(has_side_effects=True),
    )(*comm.operands)


class _GatherTwoLevel:
    def __init__(self, arrays):
        self.arrays = arrays
        self.n = n = len(arrays)
        self.operands = list(arrays)
        self.in_specs = [ANY] * n
        self.out_specs = [ANY] * n
        self.out_shape = [jax.ShapeDtypeStruct((N_DEV,) + tuple(a.shape), a.dtype) for a in arrays]
        self.scratch = [pltpu.SemaphoreType.DMA((n, N_DEV - 1)), pltpu.SemaphoreType.DMA((n, N_DEV - 1)),
                        pltpu.SemaphoreType.DMA((n,))]

    def _plan(self, ins, outs, sems):
        send_sems, recv_sems, local_sems = sems
        x, y, c = lax.axis_index("x"), lax.axis_index("y"), lax.axis_index("c")
        me, sibling = (x, y, c), (x, y, 1 - c)
        chips = [(1 - x, y), (x, 1 - y), (1 - x, 1 - y)]

        def slab(a, place):
            return outs[a].at[4 * place[0] + 2 * place[1] + place[2]]

        def copy(a, k, block, to, src=None):
            return pltpu.make_async_remote_copy(
                src_ref=slab(a, block) if src is None else src, dst_ref=slab(a, block),
                send_sem=send_sems.at[a, k], recv_sem=recv_sems.at[a, k],
                device_id=to, device_id_type=pl.DeviceIdType.MESH)

        own, mine = [], []
        for a in range(self.n):
            mine.append(pltpu.make_async_copy(ins[a], slab(a, me), local_sems.at[a]))
            own.append(copy(a, 0, me, sibling, src=ins[a]))
            own += [copy(a, 1 + j, me, (*chip, c), src=ins[a]) for j, chip in enumerate(chips)]
        return me, sibling, chips, c, copy, own, mine

    def start(self, ins, outs, sems):
        _, _, _, _, _, own, mine = self._plan(ins, outs, sems)
        for cp in mine + own:
            cp.start()

    def wait(self, ins, outs, sems):
        me, sibling, chips, c, copy, own, mine = self._plan(ins, outs, sems)
        forwards = []
        for j, chip in enumerate(chips):
            for a in range(self.n):
                copy(a, 1 + j, (*chip, c), me).wait_recv()
                fwd = copy(a, 4 + j, (*chip, c), sibling)
                fwd.start()
                forwards.append(fwd)
        for a in range(self.n):
            copy(a, 0, sibling, me).wait_recv()
            for j, chip in enumerate(chips):
                copy(a, 4 + j, (*chip, 1 - c), me).wait_recv()
        for cp in own + forwards:
            cp.wait_send()
        for loc in mine:
            loc.wait()


def _gather_two_level(arrays, name):
    comm = _GatherTwoLevel(arrays)
    n = comm.n

    def body(*refs):
        ins, outs, sems = refs[:n], refs[n:2 * n], refs[2 * n:]
        comm.start(ins, outs, sems)
        comm.wait(ins, outs, sems)

    return pl.pallas_call(
        body, name=name, in_specs=comm.in_specs, out_specs=comm.out_specs, out_shape=comm.out_shape,
        scratch_shapes=comm.scratch, compiler_params=pltpu.CompilerParams(has_side_effects=True),
    )(*arrays)


def _matmul(a, b, *, tm, tn, tk, out_dtype, name, trans_a=False, trans_b=False, comm=None):
    m, k = (a.shape[1], a.shape[0]) if trans_a else a.shape
    n = b.shape[0] if trans_b else b.shape[1]
    nk = k // tk
    dims = TN if trans_a else (NT if trans_b else NN)
    assert not (trans_a and trans_b)
    nc = comm.n if comm else 0
    grid = (m // tm, n // tn, nk)

    def body(*refs):
        a_ref, b_ref = refs[:2]
        cin = refs[2:2 + nc]
        o_ref = refs[2 + nc]
        cout = refs[3 + nc:3 + 2 * nc]
        scratch = refs[3 + 2 * nc:]
        sems = scratch[len(scratch) - 3:] if comm else None
        i, j, kk = pl.program_id(0), pl.program_id(1), pl.program_id(2)
        if comm:
            @pl.when((i == 0) & (j == 0) & (kk == 0))
            def _():
                comm.start(cin, cout, sems)

        if nk == 1:
            o_ref[...] = _dot(a_ref[...], b_ref[...], dims).astype(out_dtype)
        else:
            acc_ref = scratch[0]

            @pl.when(kk == 0)
            def _():
                acc_ref[...] = jnp.zeros_like(acc_ref)

            acc_ref[...] += _dot(a_ref[...], b_ref[...], dims)

            @pl.when(kk == nk - 1)
            def _():
                o_ref[...] = acc_ref[...].astype(out_dtype)

        if comm:
            @pl.when((i == grid[0] - 1) & (j == grid[1] - 1) & (kk == nk - 1))
            def _():
                comm.wait(cin, cout, sems)

    a_spec = (pl.BlockSpec((tk, tm), lambda i, j, kk: (kk, i)) if trans_a
              else pl.BlockSpec((tm, tk), lambda i, j, kk: (i, kk)))
    b_spec = (pl.BlockSpec((tn, tk), lambda i, j, kk: (j, kk)) if trans_b
              else pl.BlockSpec((tk, tn), lambda i, j, kk: (kk, j)))
    sem = ("arbitrary",) * 3 if comm else ("parallel", "parallel", "arbitrary")
    res = pl.pallas_call(
        body, name=name, grid=grid,
        in_specs=[a_spec, b_spec] + (comm.in_specs if comm else []),
        out_specs=[pl.BlockSpec((tm, tn), lambda i, j, kk: (i, j))] + (comm.out_specs if comm else []),
        out_shape=[jax.ShapeDtypeStruct((m, n), out_dtype)] + (comm.out_shape if comm else []),
        scratch_shapes=([] if nk == 1 else [pltpu.VMEM((tm, tn), F32)]) + (comm.scratch if comm else []),
        compiler_params=_params(sem, side_effects=bool(comm)),
    )(a, b, *(comm.operands if comm else []))
    return res if comm else res[0]


ROW_K = 6
X_K = 8


def _x_specs(nx, k):
    return [pl.BlockSpec((CHUNK, D_MODEL), functools.partial(lambda u, i: (jnp.clip(k * i + u - 1, 0, nx - 1), 0), u))
            for u in range(k)]


def _chunk_of_row(i, k):
    return k * i + lax.broadcasted_iota(jnp.int32, (k * CHUNK, 1), 0) // CHUNK


def _prenorm(x2d, h0, w):
    nx = x2d.shape[0] // CHUNK
    nc = nx + 2
    k = _tile(nc, (ROW_K, 3, 2, 1))

    def body(*refs):
        x_refs = refs[:k]
        h0_ref, w_ref, o_ref = refs[k:]
        i = pl.program_id(0)
        x = jnp.concatenate([r[...] for r in x_refs], axis=0)
        head = jnp.concatenate([h0_ref[...], x[CHUNK:, :]], axis=0)
        chunk = _chunk_of_row(i, k)
        h = jnp.where(i == 0, head, x) * (chunk <= nx).astype(F32)
        rstd = lax.rsqrt(jnp.mean(h * h, axis=-1, keepdims=True) + EPS)
        o_ref[...] = (h * rstd * w_ref[...]).astype(BF16)

    rows = k * CHUNK
    return pl.pallas_call(
        body, name="prenorm", grid=(nc // k,),
        in_specs=_x_specs(nx, k) + [pl.BlockSpec((CHUNK, D_MODEL), lambda i: (0, 0)), pl.BlockSpec((1, D_MODEL), lambda i: (0, 0))],
        out_specs=pl.BlockSpec((rows, D_MODEL), lambda i: (i, 0)),
        out_shape=jax.ShapeDtypeStruct((nc * CHUNK, D_MODEL), BF16),
        compiler_params=_params(("parallel",)),
    )(*([x2d] * k), h0, w)


CONV_COLS = 512
HALO = 8


def _conv_pre(ext, w, b):
    taps = [ext[HALO:, :]] + [pltpu.roll(ext, j, 0)[HALO:, :] for j in range(1, CONV_WIDTH)]
    acc = b + w[3:4, :] * taps[0]
    for j in range(1, CONV_WIDTH):
        acc = acc + w[3 - j:4 - j, :] * taps[j]
    return acc, taps


def _conv_bwd(dxbc, proj, conv_w, conv_b, dproj):
    lp = proj.shape[0]
    t = _tile(lp, (704, 384, 128, 64))
    hb = t // HALO
    nt = lp // t
    c0 = OFF_XBC // CONV_COLS

    def body(dx_ref, dxn_ref, u_ref, up_ref, un_ref, w_ref, b_ref, _, du_ref, dw_ref, db_ref):
        i = pl.program_id(1)
        w = w_ref[...]
        up = up_ref[...] * (i > 0).astype(F32)
        ext = jnp.concatenate([up, u_ref[...], un_ref[...]], axis=0)
        pre, taps = _conv_pre(ext, w, b_ref[...])
        dxn = dxn_ref[...] * (i < nt - 1).astype(F32)
        dxe = jnp.concatenate([dx_ref[...], dxn], axis=0)
        sg = _sigmoid(pre)
        dpre = dxe * sg * (1.0 + pre * (1.0 - sg))
        du = w[3:4, :] * dpre[:t, :]
        for j in range(1, CONV_WIDTH):
            du = du + w[3 - j:4 - j, :] * pltpu.roll(dpre, t + HALO - j, 0)[:t, :]
        du_ref[...] = du.astype(BF16)

        @pl.when(i == 0)
        def _():
            dw_ref[...] = jnp.zeros_like(dw_ref)
            db_ref[...] = jnp.zeros_like(db_ref)

        dp = dpre[:t, :]
        db_ref[...] += jnp.sum(dp, axis=0, keepdims=True)
        for j in range(CONV_WIDTH):
            dw_ref[3 - j:4 - j, :] += jnp.sum(dp * taps[j][:t, :], axis=0, keepdims=True)

    nxt = lambda i: jnp.minimum((i + 1) * hb, lp // HALO - 1)
    return pl.pallas_call(
        body, name="conv_bwd", grid=(D_CONV // CONV_COLS, nt),
        in_specs=[
            pl.BlockSpec((t, CONV_COLS), lambda j, i: (i, j)),
            pl.BlockSpec((HALO, CONV_COLS), lambda j, i: (nxt(i), j)),
            pl.BlockSpec((t, CONV_COLS), lambda j, i: (i, c0 + j)),
            pl.BlockSpec((HALO, CONV_COLS), lambda j, i: (jnp.maximum(i * hb - 1, 0), c0 + j)),
            pl.BlockSpec((HALO, CONV_COLS), lambda j, i: (nxt(i), c0 + j)),
            pl.BlockSpec((CONV_WIDTH, CONV_COLS), lambda j, i: (0, j)),
            pl.BlockSpec((1, CONV_COLS), lambda j, i: (0, j)),
            ANY,
        ],
        out_specs=[
            pl.BlockSpec((t, CONV_COLS), lambda j, i: (i, c0 + j)),
            pl.BlockSpec((CONV_WIDTH, CONV_COLS), lambda j, i: (0, j)),
            pl.BlockSpec((1, CONV_COLS), lambda j, i: (0, j)),
        ],
        out_shape=[
            jax.ShapeDtypeStruct((lp, NP), BF16),
            jax.ShapeDtypeStruct((CONV_WIDTH, D_CONV), F32),
            jax.ShapeDtypeStruct((1, D_CONV), F32),
        ],
        input_output_aliases={7: 0},
        compiler_params=_params(("parallel", "arbitrary")),
    )(dxbc, dxbc, proj, proj, proj, conv_w, conv_b, dproj)


def _swap_halves(t):
    w = t.shape[-1]
    lane = lax.broadcasted_iota(jnp.int32, t.shape, 1)
    return jnp.where((lane % HEAD_DIM) < HEAD_DIM // 2, pltpu.roll(t, w - HEAD_DIM // 2, 1),
                     pltpu.roll(t, HEAD_DIM // 2, 1))


def _act_fwd(proj, cos_t, sin_t, expand, dt_bias_pad):
    lp = proj.shape[0]
    t = _tile(lp, (384, 128, 64))

    def body(q_ref, k_ref, v_ref, dt_ref, cos_ref, sin_ref, ex_ref, bias_ref, qo_ref, ko_ref, vo_ref, dto_ref):
        i = pl.program_id(0)
        cos = cos_ref[...]
        sin = sin_ref[...]
        q = q_ref[...]
        qo_ref[...] = (q * jnp.tile(cos, (1, D_ATT // 128)) + _swap_halves(q) * jnp.tile(sin, (1, D_ATT // 128))).astype(BF16)
        k = k_ref[...]
        ko_ref[...] = (k * jnp.tile(cos, (1, D_KV // 128)) + _swap_halves(k) * jnp.tile(sin, (1, D_KV // 128))).astype(BF16)
        vo_ref[...] = v_ref[...].astype(BF16)
        raw = dt_ref[...] + bias_ref[...]
        sp = jnp.maximum(raw, 0.0) + jnp.log1p(jnp.exp(-jnp.abs(raw)))
        row = i * t + lax.broadcasted_iota(jnp.int32, sp.shape, 0)
        dto_ref[...] = _dot(jnp.where(row >= PAD_LEAD, sp, 0.0), ex_ref[...], NN, HIGHEST)

    return pl.pallas_call(
        body, name="act_fwd", grid=(lp // t,),
        in_specs=[
            pl.BlockSpec((t, D_ATT), lambda i: (i, OFF_Q // D_ATT)),
            pl.BlockSpec((t, D_KV), lambda i: (i, OFF_K // D_KV)),
            pl.BlockSpec((t, D_KV), lambda i: (i, OFF_V // D_KV)),
            pl.BlockSpec((t, 128), lambda i: (i, OFF_DT // 128)),
            pl.BlockSpec((t, 128), lambda i: (i, 0)),
            pl.BlockSpec((t, 128), lambda i: (i, 0)),
            pl.BlockSpec((128, D_SSD), lambda i: (0, 0)),
            pl.BlockSpec((1, 128), lambda i: (0, 0)),
        ],
        out_specs=[
            pl.BlockSpec((t, D_ATT), lambda i: (i, 0)),
            pl.BlockSpec((t, D_KV), lambda i: (i, 0)),
            pl.BlockSpec((t, D_KV), lambda i: (i, 0)),
            pl.BlockSpec((t, D_SSD), lambda i: (i, 0)),
        ],
        out_shape=[
            jax.ShapeDtypeStruct((lp, D_ATT), BF16),
            jax.ShapeDtypeStruct((lp, D_KV), BF16),
            jax.ShapeDtypeStruct((lp, D_KV), BF16),
            jax.ShapeDtypeStruct((lp, D_SSD), F32),
        ],
        compiler_params=_params(("parallel",)),
    )(proj, proj, proj, proj, cos_t, sin_t, expand, dt_bias_pad)


def _act_bwd(dqr, dkr, dv, dg, ddt_part, cos_t, sin_t, reduce_t, dproj):
    lp = dqr.shape[0]
    t = _tile(lp, (384, 128, 64))

    def body(dq_ref, dk_ref, dv_ref, dg_ref, ddt_ref, cos_ref, sin_ref, red_ref, _, o_ref, db_ref):
        i = pl.program_id(0)
        cos = cos_ref[...]
        sin = sin_ref[...]
        dq = dq_ref[...]
        dq = dq * jnp.tile(cos, (1, D_ATT // 128)) + _swap_halves(dq * jnp.tile(sin, (1, D_ATT // 128)))
        dk = dk_ref[...]
        dk = dk * jnp.tile(cos, (1, D_KV // 128)) + _swap_halves(dk * jnp.tile(sin, (1, D_KV // 128)))
        ddt = _dot(ddt_ref[...], red_ref[...], NN, HIGHEST)
        o_ref[...] = jnp.concatenate(
            [dq.astype(BF16), dg_ref[...].astype(BF16), dk.astype(BF16), dv_ref[...].astype(BF16), ddt.astype(BF16),
             jnp.zeros((t, NP - OFF_DT - 128), BF16)], axis=1)

        @pl.when(i == 0)
        def _():
            db_ref[...] = jnp.zeros_like(db_ref)

        db_ref[...] += jnp.sum(ddt, axis=0, keepdims=True)

    return pl.pallas_call(
        body, name="act_bwd", grid=(lp // t,),
        in_specs=[
            pl.BlockSpec((t, D_ATT), lambda i: (i, 0)),
            pl.BlockSpec((t, D_KV), lambda i: (i, 0)),
            pl.BlockSpec((t, D_KV), lambda i: (i, 0)),
            pl.BlockSpec((t, D_ATT), lambda i: (i, 0)),
            pl.BlockSpec((t, D_SSD), lambda i: (i, 0)),
            pl.BlockSpec((t, 128), lambda i: (i, 0)),
            pl.BlockSpec((t, 128), lambda i: (i, 0)),
            pl.BlockSpec((D_SSD, 128), lambda i: (0, 0)),
            ANY,
        ],
        out_specs=[pl.BlockSpec((t, TAIL_W), lambda i: (i, OFF_Q // TAIL_W)), pl.BlockSpec((1, 128), lambda i: (0, 0))],
        out_shape=[jax.ShapeDtypeStruct((lp, NP), BF16), jax.ShapeDtypeStruct((1, 128), F32)],
        input_output_aliases={8: 0},
        compiler_params=_params(("arbitrary",)),
    )(dqr, dkr, dv, dg, ddt_part, cos_t, sin_t, reduce_t, dproj)


def _cs_row(cs):
    row = lax.broadcasted_iota(jnp.int32, cs.shape, 0)
    lane = lax.broadcasted_iota(jnp.int32, cs.shape, 1)
    return jnp.sum(jnp.where(row == lane % HEAD_DIM, cs, 0.0), axis=0, keepdims=True)


def _ssd_fwd(proj, conv_w, conv_b, dt_rep, a_rep, dsk_rep, wn, tri):
    lp = proj.shape[0]
    nc = lp // CHUNK
    assert SSD_GPS == N_GROUPS
    gw = SSD_GPS * GROUP_W
    cps = _tile(nc, (SSD_CPS, 1))

    def body(rawa_ref, rawb_ref, cw_ref, cb_ref, dt_ref, z_ref, a_ref, dsk_ref, wn_ref, tri_ref,
             yn_ref, ytot_ref, hprev_ref, xbc_ref, h_scr, tail_scr):
        @pl.when(pl.program_id(1) == 0)
        def _():
            h_scr[...] = jnp.zeros_like(h_scr)
            tail_scr[...] = jnp.zeros_like(tail_scr)

        for sub in range(cps):
            rs = slice(sub * CHUNK, (sub + 1) * CHUNK)
            raw = jnp.concatenate([rawa_ref[rs, :], rawb_ref[rs, :]], axis=1)
            pre, _ = _conv_pre(jnp.concatenate([tail_scr[...], raw], axis=0), cw_ref[...], cb_ref[...])
            tail_scr[...] = raw[CHUNK - HALO:, :]
            xbc_ref[rs, :] = pre * _sigmoid(pre)

        G = range(SSD_GPS)
        colsl = [slice(gi * GROUP_W, (gi + 1) * GROUP_W) for gi in G]
        rows4 = lax.broadcasted_iota(jnp.int32, (GROUP_W, GROUP_W), 0) // HEAD_DIM
        cols4 = lax.broadcasted_iota(jnp.int32, (GROUP_W, GROUP_W), 1) // HEAD_DIM
        lrow = lax.broadcasted_iota(jnp.int32, (CHUNK, GROUP_W), 0)
        lcol = lax.broadcasted_iota(jnp.int32, (CHUNK, GROUP_W), 1) % HEAD_DIM

        def one_chunk(sub):
            rs = slice(sub * CHUNK, (sub + 1) * CHUNK)
            xbc = [xbc_ref[rs, gi * XBC_BLK:(gi + 1) * XBC_BLK] for gi in G]
            dt = [dt_ref[rs, colsl[gi]] for gi in G]
            xs = [xbc[gi][:, :GROUP_W] for gi in G]
            b = [xbc[gi][:, GROUP_W:GROUP_W + D_STATE].astype(BF16) for gi in G]
            c = [xbc[gi][:, GROUP_W + D_STATE:].astype(BF16) for gi in G]
            hprev = [h_scr[gi] for gi in G]
            cs = [_dot(tri_ref[...], dt[gi] * a_ref[:, colsl[gi]], NN, HIGHEST) for gi in G]
            yoff = [_dot(c[gi], hprev[gi].astype(BF16)) for gi in G]
            cs_t = [_cs_row(cs[gi]) for gi in G]
            xdt = [xs[gi] * dt[gi] for gi in G]
            cs_last = [cs[gi][CHUNK - 1:CHUNK, :] for gi in G]
            st = [_dot(b[gi], (xdt[gi] * jnp.exp(cs_last[gi] - cs[gi])).astype(BF16), TN) for gi in G]
            cb4 = [_dot(c[gi], jnp.concatenate([b[gi]] * HEADS_PER_GROUP, axis=0), NT) for gi in G]
            m = [(cb4[gi] * jnp.exp(jnp.where(lrow >= lcol, cs[gi] - cs_t[gi], -jnp.inf))).astype(BF16) for gi in G]
            xbd = [jnp.where(rows4 == cols4, jnp.concatenate([xdt[gi].astype(BF16)] * HEADS_PER_GROUP, axis=0), 0.0)
                   for gi in G]
            ydiag = [_dot(m[gi], xbd[gi]) for gi in G]
            for gi in G:
                cols = colsl[gi]
                ytot = ydiag[gi] + yoff[gi] * jnp.exp(cs[gi]) + dsk_ref[:, cols] * xs[gi]
                z = z_ref[rs, cols]
                gz = ytot * (z * _sigmoid(z))
                rstd = lax.rsqrt(jnp.mean(gz * gz, axis=-1, keepdims=True) + EPS)
                hprev_ref[sub, gi] = hprev[gi]
                h_scr[gi] = hprev[gi] * jnp.exp(cs_last[gi]) + st[gi]
                ytot_ref[rs, cols] = ytot
                yn_ref[rs, cols] = (gz * rstd * wn_ref[:, cols]).astype(BF16)

        for sub in range(cps):
            one_chunk(sub)

    vec = pl.BlockSpec((1, gw), lambda g, c: (0, g))
    blk = pl.BlockSpec((cps * CHUNK, gw), lambda g, c: (c, g))
    half = D_CONV // 2
    return pl.pallas_call(
        body, name="ssd_fwd", grid=(N_GROUPS // SSD_GPS, nc // cps),
        in_specs=[
            pl.BlockSpec((cps * CHUNK, half), lambda g, c: (c, OFF_XBC // half)),
            pl.BlockSpec((cps * CHUNK, half), lambda g, c: (c, OFF_XBC // half + 1)),
            pl.BlockSpec((CONV_WIDTH, D_CONV), lambda g, c: (0, 0)),
            pl.BlockSpec((1, D_CONV), lambda g, c: (0, 0)),
            blk, blk, vec, vec, vec,
            pl.BlockSpec((CHUNK, CHUNK), lambda g, c: (0, 0)),
        ],
        out_specs=[blk, blk, pl.BlockSpec((cps, SSD_GPS, D_STATE, GROUP_W), lambda g, c: (c, g, 0, 0)),
                   pl.BlockSpec((cps * CHUNK, D_CONV), lambda g, c: (c, 0))],
        out_shape=[
            jax.ShapeDtypeStruct((lp, D_MIX), BF16),
            jax.ShapeDtypeStruct((lp, D_SSD), F32),
            jax.ShapeDtypeStruct((nc, N_GROUPS, D_STATE, GROUP_W), F32),
            jax.ShapeDtypeStruct((lp, D_CONV), F32),
        ],
        scratch_shapes=[pltpu.VMEM((SSD_GPS, D_STATE, GROUP_W), F32), pltpu.VMEM((HALO, D_CONV), F32)],
        compiler_params=_params(("arbitrary", "arbitrary")),
    )(proj, proj, conv_w, conv_b, dt_rep, proj, a_rep, dsk_rep, wn, tri)


def _ssd_bwd(dmix, ytot, proj, xbc, dt_rep, hprev, a_rep, dsk_rep, wn, tri, comm):
    lp = xbc.shape[0]
    nc = lp // CHUNK
    gps = SSD_GPS
    gw = gps * GROUP_W
    cps = _tile(nc, (SSD_CPS, 1))
    ncm = comm.n
    n_in, n_out = 10, 6
    grid = (N_GROUPS // gps, nc // cps)

    def all_groups(refs):
        (dyn_ref, ytot_ref, z_ref, xbc_ref, dt_ref, hprev_ref, a_ref, dsk_ref, wn_ref, tri_ref,
         dz_ref, dxbc_ref, ddt_ref, dd_ref, da_ref, dwn_ref, dh_scr) = refs
        G = range(gps)
        H = range(HEADS_PER_GROUP)
        cl = [slice(gi * GROUP_W, (gi + 1) * GROUP_W) for gi in G]
        hl = [slice(r * HEAD_DIM, (r + 1) * HEAD_DIM) for r in H]
        tri = tri_ref[...]
        xbc = [xbc_ref[:, gi * XBC_BLK:(gi + 1) * XBC_BLK] for gi in G]
        dt = [dt_ref[:, cl[gi]] for gi in G]
        a = [a_ref[:, cl[gi]] for gi in G]
        xs = [xbc[gi][:, :GROUP_W] for gi in G]
        bbf = [xbc[gi][:, GROUP_W:GROUP_W + D_STATE].astype(BF16) for gi in G]
        cbf = [xbc[gi][:, GROUP_W + D_STATE:].astype(BF16) for gi in G]
        hprev = [hprev_ref[gi] for gi in G]
        hbf = [hprev[gi].astype(BF16) for gi in G]
        dhn = [dh_scr[gi] for gi in G]
        dhnb = [dhn[gi].astype(BF16) for gi in G]
        cs = [_dot(tri, dt[gi] * a[gi], NN, HIGHEST) for gi in G]
        g = [_dot(cbf[gi], hbf[gi]) for gi in G]
        dxw = [_dot(bbf[gi], dhnb[gi]) for gi in G]
        cs_t = [_cs_row(cs[gi]) for gi in G]
        dy = []
        for gi in G:
            ytot = ytot_ref[:, cl[gi]]
            z = z_ref[:, cl[gi]]
            dyn = dyn_ref[:, cl[gi]]
            sz = _sigmoid(z)
            silu_z = z * sz
            gz = ytot * silu_z
            rstd = lax.rsqrt(jnp.mean(gz * gz, axis=-1, keepdims=True) + EPS)
            xhat = gz * rstd
            dwn_ref[:, cl[gi]] += jnp.sum(dyn * xhat, axis=0, keepdims=True)
            dxhat = dyn * wn_ref[:, cl[gi]]
            dgz = rstd * (dxhat - xhat * jnp.mean(dxhat * xhat, axis=-1, keepdims=True))
            dy.append(dgz * silu_z)
            dz_ref[:, cl[gi]] = (dgz * ytot * (sz * (1.0 + z * (1.0 - sz)))).astype(BF16)
            dd_ref[:, cl[gi]] += jnp.sum(dy[gi] * xs[gi], axis=0, keepdims=True)
        xdt = [xs[gi] * dt[gi] for gi in G]
        e = [jnp.exp(cs[gi]) for gi in G]
        cs_last = [cs[gi][CHUNK - 1:CHUNK, :] for gi in G]
        dte = [jnp.exp(cs_last[gi] - cs[gi]) for gi in G]
        cd = [jnp.exp(cs_last[gi]) for gi in G]
        dgb = [(dy[gi] * e[gi]).astype(BF16) for gi in G]
        dyb = [dy[gi].astype(BF16) for gi in G]
        xdtb = [xdt[gi].astype(BF16) for gi in G]
        dc = [_dot(dgb[gi], hbf[gi], NT) for gi in G]
        dhprev = [_dot(cbf[gi], dgb[gi], TN) for gi in G]
        db = [_dot((xdt[gi] * dte[gi]).astype(BF16), dhnb[gi], NT) for gi in G]
        row = lax.broadcasted_iota(jnp.int32, (CHUNK, CHUNK), 0)
        causal = row >= lax.broadcasted_iota(jnp.int32, (CHUNK, CHUNK), 1)
        cb = [_dot(cbf[gi], bbf[gi], NT) for gi in G]
        dm = [[_dot(dyb[gi][:, hl[r]], xdtb[gi][:, hl[r]], NT) for r in H] for gi in G]
        mb, dseg, dcbb = [], [], []
        for gi in G:
            mb.append([])
            dseg.append([])
            dcb = None
            for r in H:
                seg = cs[gi][:, r * HEAD_DIM:r * HEAD_DIM + 1] - cs_t[gi][:, hl[r]]
                lm = jnp.exp(jnp.where(causal, seg, -jnp.inf))
                m = cb[gi] * lm
                mb[gi].append(m.astype(BF16))
                dseg[gi].append(dm[gi][r] * m)
                dcb = dm[gi][r] * lm if r == 0 else dcb + dm[gi][r] * lm
            dcbb.append(dcb.astype(BF16))
        dxdt_diag = [[_dot(mb[gi][r], dyb[gi][:, hl[r]], TN) for r in H] for gi in G]
        ones = jnp.ones((CHUNK, HEAD_DIM), F32)
        colsum = [[_dot(dseg[gi][r], ones, TN, HIGHEST) for r in H] for gi in G]
        dc2 = [_dot(dcbb[gi], bbf[gi]) for gi in G]
        db2 = [_dot(dcbb[gi], cbf[gi], TN) for gi in G]
        dcs = []
        for gi in G:
            t_dte = dxw[gi] * xdt[gi] * dte[gi]
            dcs_last = (jnp.sum(dhn[gi] * hprev[gi], axis=0, keepdims=True) * cd[gi]
                        + jnp.sum(t_dte, axis=0, keepdims=True))
            diag = jnp.concatenate(
                [(jnp.sum(dseg[gi][r], axis=1, keepdims=True) - colsum[gi][r]) * (1.0 / HEAD_DIM) for r in H], axis=1)
            d = dy[gi] * g[gi] * e[gi] - t_dte + diag
            row = lax.broadcasted_iota(jnp.int32, d.shape, 0)
            dcs.append(d + jnp.where(row == CHUNK - 1, dcs_last, 0.0))
        dda = [_dot(tri, dcs[gi], TN, HIGHEST) for gi in G]
        for gi in G:
            dxdt = dxw[gi] * dte[gi] + jnp.concatenate(dxdt_diag[gi], axis=1)
            da_ref[:, cl[gi]] += jnp.sum(dda[gi] * dt[gi], axis=0, keepdims=True)
            ddt = dda[gi] * a[gi] + dxdt * xs[gi]
            dxs = dsk_ref[:, cl[gi]] * dy[gi] + dxdt * dt[gi]
            ddt_ref[:, cl[gi]] = ddt * (1.0 - jnp.exp(-dt[gi]))
            dxbc_ref[:, gi * XBC_BLK:(gi + 1) * XBC_BLK] = jnp.concatenate(
                [dxs, db[gi] + db2[gi], dc[gi] + dc2[gi]], axis=1)
            dh_scr[gi] = dhprev[gi] + dhn[gi] * cd[gi]

    def body(*refs):
        ins = refs[:n_in]
        cin = refs[n_in:n_in + ncm]
        outs = refs[n_in + ncm:n_in + ncm + n_out]
        cout = refs[n_in + ncm + n_out:n_in + 2 * ncm + n_out]
        dh_scr = refs[n_in + 2 * ncm + n_out]
        sems = refs[n_in + 2 * ncm + n_out + 1:]
        g, c = pl.program_id(0), pl.program_id(1)

        @pl.when((g == 0) & (c == 0))
        def _():
            comm.start(cin, cout, sems)

        @pl.when(c == 0)
        def _():
            dh_scr[...] = jnp.zeros_like(dh_scr)
            for ref in outs[3:]:
                ref[...] = jnp.zeros_like(ref)

        for sub in reversed(range(cps)):
            rs = pl.ds(sub * CHUNK, CHUNK)
            chunk_ins = tuple(r.at[rs] for r in ins[:5]) + (ins[5].at[sub],) + ins[6:]
            chunk_outs = tuple(r.at[rs] for r in outs[:3]) + outs[3:]
            all_groups(chunk_ins + chunk_outs + (dh_scr,))

        @pl.when((g == grid[0] - 1) & (c == grid[1] - 1))
        def _():
            comm.wait(cin, cout, sems)

    rev = lambda c: grid[1] - 1 - c
    vec = pl.BlockSpec((1, gw), lambda g, c: (0, g))
    blk = pl.BlockSpec((cps * CHUNK, gw), lambda g, c: (rev(c), g))
    xblk = pl.BlockSpec((cps * CHUNK, gps * XBC_BLK), lambda g, c: (rev(c), g))
    res = pl.pallas_call(
        body, name="ssd_bwd", grid=grid,
        in_specs=[blk, blk, blk, xblk, blk,
                  pl.BlockSpec((cps, gps, D_STATE, GROUP_W), lambda g, c: (rev(c), g, 0, 0)),
                  vec, vec, vec,
                  pl.BlockSpec((CHUNK, CHUNK), lambda g, c: (0, 0))] + comm.in_specs,
        out_specs=[blk, xblk, blk, vec, vec, vec] + comm.out_specs,
        out_shape=[
            jax.ShapeDtypeStruct((lp, NP), BF16),
            jax.ShapeDtypeStruct((lp, D_CONV), F32),
            jax.ShapeDtypeStruct((lp, D_SSD), F32),
            jax.ShapeDtypeStruct((1, D_SSD), F32),
            jax.ShapeDtypeStruct((1, D_SSD), F32),
            jax.ShapeDtypeStruct((1, D_SSD), F32),
        ] + comm.out_shape,
        scratch_shapes=[pltpu.VMEM((gps, D_STATE, GROUP_W), F32)] + comm.scratch,
        compiler_params=_params(("arbitrary", "arbitrary"), side_effects=True),
    )(dmix, ytot, proj, xbc, dt_rep, hprev, a_rep, dsk_rep, wn, tri, *comm.operands)
    return res


def _stack_heads(t, h):
    return jnp.concatenate([t[:, (REP * h + r) * HEAD_DIM:(REP * h + r + 1) * HEAD_DIM] for r in range(REP)], axis=0)


def _band(t2, t1, t0, h):
    sl = slice(h * HEAD_DIM, (h + 1) * HEAD_DIM)
    return jnp.concatenate([t2[:, sl], t1[:, sl], t0[:, sl]], axis=0)


def _attn_probs(s, sink, qc):
    s = s * (HEAD_DIM ** -0.5)
    key_abs = (qc - WINDOW_CHUNKS) * CHUNK + lax.broadcasted_iota(jnp.int32, s.shape, 1)
    s = jnp.where(key_abs >= PAD_LEAD, s, -jnp.inf)
    m = jnp.maximum(jnp.max(s, axis=-1, keepdims=True), sink)
    p = jnp.exp(s - m)
    ps = jnp.exp(sink - m)
    denom = jnp.sum(p, axis=-1, keepdims=True) + ps
    return p / denom, ps / denom


ATT_QC_FWD = 6
ATT_QC_BWD = 2


def _kv_specs(width, newest_chunk_of, kc):
    return [pl.BlockSpec((CHUNK, width), functools.partial(lambda j, p: (jnp.maximum(newest_chunk_of(p) - j, 0), 0), j))
            for j in range(kc - 1, -1, -1)]


def _attn_fwd(qr, kr, vb, proj, sink_stack, mix):
    lp = qr.shape[0]
    nc = lp // CHUNK
    qn = _tile(nc, (ATT_QC_FWD, 2, 1))
    kn = WINDOW_CHUNKS + qn
    qrows = qn * CHUNK

    def body(q_ref, *rest):
        k_refs, v_refs = rest[:kn], rest[kn:2 * kn]
        g_ref, sink_ref, _, att_ref, mix_ref = rest[2 * kn:]
        p = pl.program_id(0)
        q = q_ref[...]
        ks = [r[...] for r in k_refs]
        vs = [r[...] for r in v_refs]
        units = [(u, h) for u in range(qn) for h in range(KV_HEADS)]
        s = [_dot(_stack_heads(q[u * CHUNK:(u + 1) * CHUNK, :], h), _band(*ks[u:u + 3], h), NT) for u, h in units]
        vbh = [_band(*vs[u:u + 3], h) for u, h in units]
        pn = [_attn_probs(s[i], sink_ref[h], qn * p + u)[0].astype(BF16) for i, (u, h) in enumerate(units)]
        o = [_dot(pn[i], vbh[i]) for i in range(len(units))]
        att = jnp.concatenate(
            [jnp.concatenate([o[u * KV_HEADS + h][r * CHUNK:(r + 1) * CHUNK, :] for h in range(KV_HEADS) for r in range(REP)],
                             axis=1) for u in range(qn)], axis=0)
        att_ref[...] = att
        g = g_ref[...]
        mix_ref[...] = (att * (g * _sigmoid(g))).astype(BF16)

    newest = lambda p: qn * p + qn - 1
    return pl.pallas_call(
        body, name="attn_fwd", grid=(nc // qn,),
        in_specs=[pl.BlockSpec((qrows, D_ATT), lambda p: (p, 0))] + _kv_specs(D_KV, newest, kn) + _kv_specs(D_KV, newest, kn) + [
            pl.BlockSpec((qrows, D_ATT), lambda p: (p, OFF_G // D_ATT)),
            pl.BlockSpec((KV_HEADS, REP * CHUNK, 1), lambda p: (0, 0, 0)),
            ANY,
        ],
        out_specs=[pl.BlockSpec((qrows, D_ATT), lambda p: (p, 0)),
                   pl.BlockSpec((qrows, D_ATT), lambda p: (p, D_SSD // D_ATT))],
        out_shape=[jax.ShapeDtypeStruct((lp, D_ATT), F32), jax.ShapeDtypeStruct((lp, D_MIX), BF16)],
        input_output_aliases={2 * kn + 3: 1},
        compiler_params=_params(("parallel",)),
    )(qr, *([kr] * kn), *([vb] * kn), proj, sink_stack, mix)


def _attn_bwd(qr, kr, vb, att, proj, dmix, sink_stack):
    lp = qr.shape[0]
    nc = lp // CHUNK
    qn = ATT_QC_BWD
    kn = WINDOW_CHUNKS + qn
    assert nc % qn == 0 and WINDOW_CHUNKS % qn == 0
    steps = nc // qn
    qrows = qn * CHUNK
    wrows = kn * CHUNK

    def body(q_ref, *rest):
        k_refs, v_refs = rest[:kn], rest[kn:2 * kn]
        (att_ref, g_ref, do_ref, sink_ref, dq_ref, dk_ref, dv_ref, dg_ref, dsink_ref, dk_acc, dv_acc) = rest[2 * kn:]
        step = pl.program_id(0)

        @pl.when(step == 0)
        def _():
            dk_acc[...] = jnp.zeros_like(dk_acc)
            dv_acc[...] = jnp.zeros_like(dv_acc)
            dsink_ref[...] = jnp.zeros_like(dsink_ref)

        q = q_ref[...]
        ks = [r[...] for r in k_refs]
        vs = [r[...] for r in v_refs]
        att = att_ref[...]
        g = g_ref[...]
        dog = do_ref[...]
        sg = _sigmoid(g)
        dg_ref[...] = dog * att * (sg * (1.0 + g * (1.0 - sg)))
        do = dog * (g * sg)
        units = [(u, h) for u in range(qn) for h in range(KV_HEADS)]
        n = range(len(units))
        rows = [slice(u * CHUNK, (u + 1) * CHUNK) for u in range(qn)]
        qs = [_stack_heads(q[rows[u], :], h) for u, h in units]
        kb = [_band(*ks[u:u + 3], h) for u, h in units]
        vbh = [_band(*vs[u:u + 3], h) for u, h in units]
        dos = [_stack_heads(do[rows[u], :], h) for u, h in units]
        dosb = [dos[i].astype(BF16) for i in n]
        s = [_dot(qs[i], kb[i], NT) for i in n]
        dp = [_dot(dosb[i], vbh[i], NT) for i in n]
        ds, pnb, dsink = [], [], []
        for i, (u, h) in enumerate(units):
            pn, psink = _attn_probs(s[i], sink_ref[h], qn * jnp.minimum(step, steps - 1) + u)
            delta = jnp.sum(dos[i] * _stack_heads(att[rows[u], :], h), axis=-1, keepdims=True)
            ds.append((pn * (dp[i] - delta)).astype(BF16))
            pnb.append(pn.astype(BF16))
            dsink.append(-(psink * delta))
        dqs = [_dot(ds[i], kb[i]) for i in n]
        dks = [_dot(ds[i], qs[i], TN) for i in n]
        dvs = [_dot(pnb[i], dosb[i], TN) for i in n]
        dq_ref[...] = jnp.concatenate(
            [jnp.concatenate([dqs[u * KV_HEADS + h][r * CHUNK:(r + 1) * CHUNK, :]
                              for h in range(KV_HEADS) for r in range(REP)], axis=1) for u in range(qn)],
            axis=0) * (HEAD_DIM ** -0.5)
        @pl.when(step < steps)
        def _():
            for i, (u, h) in enumerate(units):
                dsink_ref[h] += dsink[i]
            for u in range(qn):
                band = slice(u * CHUNK, u * CHUNK + BAND)
                dk_acc[band, :] += jnp.concatenate(dks[u * KV_HEADS:(u + 1) * KV_HEADS], axis=1) * (HEAD_DIM ** -0.5)
                dv_acc[band, :] += jnp.concatenate(dvs[u * KV_HEADS:(u + 1) * KV_HEADS], axis=1)

        dk_ref[...] = dk_acc[0:qrows, :]
        dv_ref[...] = dv_acc[0:qrows, :]
        for acc in (dk_acc, dv_acc):
            rest_rows = acc[qrows:wrows, :]
            acc[0:wrows - qrows, :] = rest_rows
            acc[wrows - qrows:wrows, :] = jnp.zeros((qrows, D_KV), F32)

    qp = lambda p: jnp.minimum(p, steps - 1)
    newest = lambda p: qn * qp(p) + qn - 1
    qblk = pl.BlockSpec((qrows, D_ATT), lambda p: (qp(p), 0))
    oldest = pl.BlockSpec((qrows, D_KV), lambda p: (jnp.maximum(p - 1, 0), 0))
    return pl.pallas_call(
        body, name="attn_bwd", grid=(steps + 1,),
        in_specs=[qblk] + _kv_specs(D_KV, newest, kn) + _kv_specs(D_KV, newest, kn) + [
            qblk,
            pl.BlockSpec((qrows, D_ATT), lambda p: (qp(p), OFF_G // D_ATT)),
            pl.BlockSpec((qrows, D_ATT), lambda p: (qp(p), D_SSD // D_ATT)),
            pl.BlockSpec((KV_HEADS, REP * CHUNK, 1), lambda p: (0, 0, 0)),
        ],
        out_specs=[qblk, oldest, oldest, qblk, pl.BlockSpec((KV_HEADS, REP * CHUNK, 1), lambda p: (0, 0, 0))],
        out_shape=[
            jax.ShapeDtypeStruct((lp, D_ATT), F32),
            jax.ShapeDtypeStruct((lp, D_KV), F32),
            jax.ShapeDtypeStruct((lp, D_KV), F32),
            jax.ShapeDtypeStruct((lp, D_ATT), F32),
            jax.ShapeDtypeStruct((KV_HEADS, REP * CHUNK, 1), F32),
        ],
        scratch_shapes=[pltpu.VMEM((wrows, D_KV), F32), pltpu.VMEM((wrows, D_KV), F32)],
        compiler_params=_params(("arbitrary",)),
    )(qr, *([kr] * kn), *([vb] * kn), att, proj, dmix, sink_stack)


def _post_loss(out, x2d, target, w):
    lp = out.shape[0]
    nc = lp // CHUNK
    nx = x2d.shape[0] // CHUNK

    k = _tile(nc, (ROW_K, 3, 2, 1))

    def body(o_ref, *rest):
        x_refs, t_refs = rest[:k], rest[k:2 * k]
        w_ref, do_ref, dy_ref, gw_ref, loss_ref = rest[2 * k:]
        i = pl.program_id(0)

        @pl.when(i == 0)
        def _():
            gw_ref[...] = jnp.zeros_like(gw_ref)
            loss_ref[...] = jnp.zeros_like(loss_ref)

        o = o_ref[...]
        w = w_ref[...]
        rstd = lax.rsqrt(jnp.mean(o * o, axis=-1, keepdims=True) + EPS)
        xhat = o * rstd
        x = jnp.concatenate([r[...] for r in x_refs], axis=0)
        t = jnp.concatenate([r[...] for r in t_refs], axis=0)
        chunk = _chunk_of_row(i, k)
        err = (x + xhat * w - t) * ((chunk > 0) & (chunk <= nx)).astype(F32)
        loss_ref[...] += 0.5 * jnp.sum(jnp.mean(err * err, axis=-1, keepdims=True), axis=0, keepdims=True)
        dy = err * (1.0 / D_MODEL)
        dy_ref[...] = dy
        gw_ref[...] += jnp.sum(dy * xhat, axis=0, keepdims=True)
        dxhat = dy * w
        do_ref[...] = (rstd * (dxhat - xhat * jnp.mean(dxhat * xhat, axis=-1, keepdims=True))).astype(BF16)

    row = pl.BlockSpec((k * CHUNK, D_MODEL), lambda i: (i, 0))
    return pl.pallas_call(
        body, name="post_loss", grid=(nc // k,),
        in_specs=[row] + _x_specs(nx, k) + _x_specs(nx, k) + [pl.BlockSpec((1, D_MODEL), lambda i: (0, 0))],
        out_specs=[row, row, pl.BlockSpec((1, D_MODEL), lambda i: (0, 0)), pl.BlockSpec((1, 128), lambda i: (0, 0))],
        out_shape=[
            jax.ShapeDtypeStruct((lp, D_MODEL), BF16),
            jax.ShapeDtypeStruct((lp, D_MODEL), F32),
            jax.ShapeDtypeStruct((1, D_MODEL), F32),
            jax.ShapeDtypeStruct((1, 128), F32),
        ],
        compiler_params=_params(("arbitrary",)),
    )(out, *([x2d] * k), *([target] * k), w)


def _prenorm_bwd(dhn, x2d, h0, dy, w):
    nx = x2d.shape[0] // CHUNK
    k = _tile(nx, (X_K, 4, 2, 1))
    rows = k * CHUNK

    def backward(h, dhn, w):
        rstd = lax.rsqrt(jnp.mean(h * h, axis=-1, keepdims=True) + EPS)
        xhat = h * rstd
        dxhat = dhn * w
        dh = rstd * (dxhat - xhat * jnp.mean(dxhat * xhat, axis=-1, keepdims=True))
        return dh, jnp.sum(dhn * xhat, axis=0, keepdims=True)

    def body(*refs):
        dhn_refs, dy_refs = refs[:k], refs[k:2 * k]
        x_ref, w_ref, gx_ref, gw_ref = refs[2 * k:]

        @pl.when(pl.program_id(0) == 0)
        def _():
            gw_ref[...] = jnp.zeros_like(gw_ref)

        dh, gw = backward(x_ref[...], jnp.concatenate([r[...] for r in dhn_refs], axis=0), w_ref[...])
        gx_ref[...] = dh + jnp.concatenate([r[...] for r in dy_refs], axis=0)
        gw_ref[...] += gw

    def body_meta(dhn_ref, h0_ref, w_ref, d0_ref, gw_ref):
        d0_ref[...], gw_ref[...] = backward(h0_ref[...], dhn_ref[...], w_ref[...])

    chunk_specs = [pl.BlockSpec((CHUNK, D_MODEL), functools.partial(lambda u, i: (k * i + 1 + u, 0), u)) for u in range(k)]
    first = pl.BlockSpec((CHUNK, D_MODEL), lambda i: (0, 0))
    vec = pl.BlockSpec((1, D_MODEL), lambda i: (0, 0))
    wide = pl.BlockSpec((rows, D_MODEL), lambda i: (i, 0))
    gx, gw_x = pl.pallas_call(
        body, name="prenorm_bwd", grid=(nx // k,),
        in_specs=chunk_specs + chunk_specs + [wide, vec],
        out_specs=[wide, vec],
        out_shape=[jax.ShapeDtypeStruct((nx * CHUNK, D_MODEL), F32), jax.ShapeDtypeStruct((1, D_MODEL), F32)],
        compiler_params=_params(("arbitrary",)),
    )(*([dhn] * k), *([dy] * k), x2d, w)
    d0, gw_0 = pl.pallas_call(
        body_meta, name="prenorm_bwd_meta", grid=(1,),
        in_specs=[first, first, vec], out_specs=[first, vec],
        out_shape=[jax.ShapeDtypeStruct((CHUNK, D_MODEL), F32), jax.ShapeDtypeStruct((1, D_MODEL), F32)],
        compiler_params=_params(("arbitrary",)),
    )(dhn, h0, w)
    return gx, d0, gw_x + gw_0


def _adamw(slabs, w, m, v, name):
    rows, cols = w.shape
    tr = _tile(rows, (256, 128, 64, 16, 8))
    c1 = 1.0 - ADAM_B1 ** ADAM_STEP
    c2 = 1.0 - ADAM_B2 ** ADAM_STEP

    def body(s_ref, w_ref, m_ref, v_ref, g_ref, d_ref, mo_ref, vo_ref):
        g = s_ref[0].astype(F32)
        for k in range(1, slabs.shape[0]):
            g = g + s_ref[k].astype(F32)
        w = w_ref[...]
        m = ADAM_B1 * m_ref[...] + (1.0 - ADAM_B1) * g
        v = ADAM_B2 * v_ref[...] + (1.0 - ADAM_B2) * (g * g)
        g_ref[...] = g
        mo_ref[...] = m
        vo_ref[...] = v
        d_ref[...] = -ADAM_LR * ((m / c1) / (jnp.sqrt(v / c2) + ADAM_EPS) + ADAM_WD * w)

    blk = pl.BlockSpec((tr, cols), lambda i: (i, 0))
    shape = jax.ShapeDtypeStruct((rows, cols), F32)
    return pl.pallas_call(
        body, name=name, grid=(rows // tr,),
        in_specs=[pl.BlockSpec((slabs.shape[0], tr, cols), lambda i: (0, i, 0)), blk, blk, blk],
        out_specs=[blk, blk, blk, blk],
        out_shape=[shape, shape, shape, shape],
        compiler_params=_params(("parallel",)),
    )(slabs, w, m, v)


def _perm_xbc(a):
    lead = a.shape[:-1]
    xs = a[..., :D_SSD].reshape(lead + (N_GROUPS, GROUP_W))
    b = a[..., D_SSD:D_SSD + N_GROUPS * D_STATE].reshape(lead + (N_GROUPS, D_STATE))
    c = a[..., D_SSD + N_GROUPS * D_STATE:].reshape(lead + (N_GROUPS, D_STATE))
    return jnp.concatenate([xs, b, c], axis=-1).reshape(lead + (D_CONV,))


def _unperm_xbc(a):
    lead = a.shape[:-1]
    t = a.reshape(lead + (N_GROUPS, XBC_BLK))
    xs = t[..., :GROUP_W].reshape(lead + (D_SSD,))
    b = t[..., GROUP_W:GROUP_W + D_STATE].reshape(lead + (N_GROUPS * D_STATE,))
    c = t[..., GROUP_W + D_STATE:].reshape(lead + (N_GROUPS * D_STATE,))
    return jnp.concatenate([xs, b, c], axis=-1)


R_Z, R_XBC, R_DT, R_Q, R_K, R_V, R_G = 0, 2048, 6144, 6176, 7200, 7456, 7712


def _internal_of_reference():
    ref = np.arange(D_IN_PROJ)
    out = np.empty(D_IN_PROJ, np.int64)
    out[R_Z:R_XBC] = OFF_Z + ref[:D_SSD]
    xs = np.arange(D_SSD)
    out[R_XBC:R_XBC + D_SSD] = OFF_XBC + (xs // GROUP_W) * XBC_BLK + xs % GROUP_W
    bc = np.arange(N_GROUPS * D_STATE)
    out[R_XBC + D_SSD:R_XBC + D_SSD + N_GROUPS * D_STATE] = OFF_XBC + (bc // D_STATE) * XBC_BLK + GROUP_W + bc % D_STATE
    out[R_XBC + D_SSD + N_GROUPS * D_STATE:R_DT] = OFF_XBC + (bc // D_STATE) * XBC_BLK + GROUP_W + D_STATE + bc % D_STATE
    out[R_DT:R_Q] = OFF_DT + np.arange(SSD_HEADS)
    out[R_Q:R_K] = OFF_Q + np.arange(D_ATT)
    out[R_K:R_V] = OFF_K + np.arange(D_KV)
    out[R_V:R_G] = OFF_V + np.arange(D_KV)
    out[R_G:] = OFF_G + np.arange(D_ATT)
    return out


def _runs(src, dst_break):
    runs, lo = [], 0
    for i in range(1, len(src) + 1):
        if i == len(src) or src[i] != src[i - 1] + 1 or dst_break[i] != dst_break[i - 1]:
            runs.append((lo, i))
            lo = i
    return runs


RELAYOUT_ROWS = 256


def _lane_window(ref, lead, c0, n):
    a0 = c0 // 128 * 128
    a1 = min(-(-(c0 + n) // 128) * 128, ref.shape[-1])
    return ref[lead + (slice(None), slice(a0, a1))][:, c0 - a0:c0 - a0 + n]


def _w_in_internal(w_gathered):
    int_of_ref = _internal_of_reference()
    ref_of_int = np.full(NP, -1, np.int64)
    ref_of_int[int_of_ref] = np.arange(D_IN_PROJ)
    shard = np.where(ref_of_int >= 0, ref_of_int // SHARD_IN, -1)
    src = np.where(ref_of_int >= 0, ref_of_int, -10 - 2 * np.arange(NP))
    plan, zeros = [], 0
    for lo, hi in _runs(src, shard):
        if ref_of_int[lo] < 0:
            zeros += hi - lo
            continue
        if zeros:
            plan.append((None, 0, zeros))
            zeros = 0
        plan.append((int(ref_of_int[lo] // SHARD_IN), int(ref_of_int[lo] % SHARD_IN), hi - lo))
    if zeros:
        plan.append((None, 0, zeros))
    tr = RELAYOUT_ROWS

    def body(w_ref, o_ref):
        o_ref[...] = jnp.concatenate(
            [jnp.zeros((tr, n), o_ref.dtype) if s is None else _lane_window(w_ref, (s,), c0, n) for s, c0, n in plan], axis=1)

    return pl.pallas_call(
        body, name="w_in_relayout", grid=(D_MODEL // tr,),
        in_specs=[pl.BlockSpec((N_DEV, tr, SHARD_IN), lambda i: (0, i, 0))],
        out_specs=pl.BlockSpec((tr, NP), lambda i: (i, 0)),
        out_shape=jax.ShapeDtypeStruct((D_MODEL, NP), w_gathered.dtype),
        compiler_params=_params(("parallel",)),
    )(w_gathered)


def _w_in_chip_slabs(dw):
    int_of_ref = _internal_of_reference()
    plan = []
    for s in range(N_DEV):
        cols = int_of_ref[s * SHARD_IN:(s + 1) * SHARD_IN]
        plan.append([(int(cols[lo]), hi - lo) for lo, hi in _runs(cols, np.zeros_like(cols))])
    tr = RELAYOUT_ROWS
    steps = D_MODEL // tr
    pairs = N_DEV // 2

    def body(dw_ref, o_ref, land_ref, send_buf, land_buf, send_sems, recv_sems, load_sems):
        i = pl.program_id(0)
        x, y, c = lax.axis_index("x"), lax.axis_index("y"), lax.axis_index("c")
        slot = i % 2

        def sends(step, sl):
            return [pltpu.make_async_remote_copy(
                src_ref=send_buf.at[sl, 2 * q + (1 - c)], dst_ref=land_ref.at[q, pl.ds(step * tr, tr), :],
                send_sem=send_sems.at[sl, q], recv_sem=recv_sems.at[step, q],
                device_id=(x, y, 1 - c), device_id_type=pl.DeviceIdType.MESH) for q in range(pairs)]

        @pl.when(i < steps)
        def _():
            @pl.when(i >= 2)
            def _():
                for cp in sends(i - 2, slot):
                    cp.wait_send()

            for s in range(N_DEV):
                send_buf[slot, s] = jnp.concatenate([_lane_window(dw_ref, (), c0, n) for c0, n in plan[s]], axis=1)
            for cp in sends(i, slot):
                cp.start()

        @pl.when(i >= 1)
        def _():
            loads = []
            for q, cp in enumerate(sends(i - 1, 1 - slot)):
                cp.wait_recv()
                loads.append(pltpu.make_async_copy(land_ref.at[q, pl.ds((i - 1) * tr, tr), :], land_buf.at[q], load_sems.at[q]))
                loads[q].start()
            for q in range(pairs):
                loads[q].wait()
                o_ref[q] = (send_buf[1 - slot, 2 * q + c].astype(F32) + land_buf[q].astype(F32)).astype(o_ref.dtype)

        @pl.when(i == steps)
        def _():
            for cp in sends(steps - 2, steps % 2) + sends(steps - 1, 1 - steps % 2):
                cp.wait_send()

    last = steps - 1
    return pl.pallas_call(
        body, name="dw_in_relayout", grid=(steps + 1,),
        in_specs=[pl.BlockSpec((tr, NP), lambda i: (jnp.minimum(i, last), 0))],
        out_specs=[pl.BlockSpec((pairs, tr, SHARD_IN), lambda i: (0, jnp.maximum(i - 1, 0), 0)), ANY],
        out_shape=[jax.ShapeDtypeStruct((pairs, D_MODEL, SHARD_IN), dw.dtype),
                   jax.ShapeDtypeStruct((pairs, D_MODEL, SHARD_IN), dw.dtype)],
        scratch_shapes=[pltpu.VMEM((2, N_DEV, tr, SHARD_IN), dw.dtype), pltpu.VMEM((pairs, tr, SHARD_IN), dw.dtype),
                        pltpu.SemaphoreType.DMA((2, pairs)), pltpu.SemaphoreType.DMA((steps, pairs)),
                        pltpu.SemaphoreType.DMA((pairs,))],
        compiler_params=_params(("arbitrary",), side_effects=True),
    )(dw)[0]


def _rep_heads(a):
    return jnp.repeat(a, HEAD_DIM, axis=1)


SMALL = (("norm_pre_w", 2048), ("conv_b", 4096), ("dt_bias", 32), ("a_log", 32), ("d_skip", 32),
         ("ssd_norm_w", 2048), ("attn_sinks", 16), ("norm_post_w", 2048))
SMALL_USED = sum(size for _, size in SMALL)
SMALL_LEN = 10368


def _pack_small(d, loss=None):
    parts = [d[name].reshape(1, size) for name, size in SMALL]
    tail = jnp.zeros((1, SMALL_LEN - SMALL_USED), F32)
    if loss is not None:
        tail = tail.at[0, 0].set(loss)
    return jnp.concatenate(parts + [tail], axis=1)


def _unpack_small(vec):
    out, off = {}, 0
    for name, size in SMALL:
        out[name] = vec[:, off:off + size]
        off += size
    return out


def kernel(x, meta_tokens, norm_pre_w, w_in, conv_w, conv_b, dt_bias, a_log, d_skip, ssd_norm_w, attn_sinks, w_out, norm_post_w, loss_target, m_meta_tokens, m_norm_pre_w, m_w_in, m_conv_w, m_conv_b, m_dt_bias, m_a_log, m_d_skip, m_ssd_norm_w, m_attn_sinks, m_w_out, m_norm_post_w, v_meta_tokens, v_norm_pre_w, v_w_in, v_conv_w, v_conv_b, v_dt_bias, v_a_log, v_d_skip, v_ssd_norm_w, v_attn_sinks, v_w_out, v_norm_post_w):
    seq = x.shape[1]
    lp = seq + 2 * CHUNK
    x2d = x[0]

    w_in_g, conv_w_g, meta_g = _gather_two_level([w_in[0].astype(BF16), conv_w[0], meta_tokens], "gather_w_in")
    w_all = _w_in_internal(w_in_g)
    conv_w_full = _perm_xbc(jnp.transpose(conv_w_g, (1, 0, 2)).reshape(CONV_WIDTH, D_CONV))
    conv_b_int = _perm_xbc(conv_b)
    meta_full = jnp.transpose(meta_g, (1, 0, 2)).reshape(N_META, D_MODEL)
    h0 = jnp.concatenate([jnp.zeros((PAD_LEAD, D_MODEL), F32), meta_full], axis=0)

    pos = (jnp.arange(lp) - PAD_LEAD).astype(F32)
    half = HEAD_DIM // 2
    inv = ROPE_THETA ** (-jnp.arange(half, dtype=F32) / half)
    ang = pos[:, None] * inv[None, :]
    cos_t = jnp.tile(jnp.cos(ang), (1, 4))
    sin_t = jnp.tile(jnp.concatenate([-jnp.sin(ang), jnp.sin(ang)], axis=1), (1, 2))
    head_of_col = np.arange(D_SSD) // HEAD_DIM
    expand = jnp.asarray((np.arange(128)[:, None] == head_of_col[None, :]).astype(np.float32))
    reduce_t = jnp.asarray((head_of_col[:, None] == np.arange(128)[None, :]).astype(np.float32))
    tri = jnp.asarray(np.tril(np.ones((CHUNK, CHUNK), np.float32)))
    a_rep = _rep_heads(-jnp.exp(a_log))
    dsk_rep = _rep_heads(d_skip)
    dt_bias_pad = jnp.pad(dt_bias, ((0, 0), (0, 128 - SSD_HEADS)))
    sink_stack = jnp.repeat(attn_sinks.reshape(KV_HEADS, REP), CHUNK, axis=1).reshape(KV_HEADS, REP * CHUNK, 1)

    hn = _prenorm(x2d, h0, norm_pre_w)
    tm = _tile(lp, (1056, 704, 128, 64))
    proj, w_out_g = _matmul(hn, w_all, tm=tm, tn=1536, tk=D_MODEL, out_dtype=F32, name="in_proj",
                            comm=_Comm([(w_out[0].astype(BF16), "gather")]))
    w_out_full = w_out_g.reshape(D_MIX, D_MODEL)
    qr, kr, vb, dt_rep = _act_fwd(proj, cos_t, sin_t, expand, dt_bias_pad)
    mix, ytot, hprev, xbc = _ssd_fwd(proj, conv_w_full, conv_b_int, dt_rep, a_rep, dsk_rep, ssd_norm_w, tri)
    att, mix = _attn_fwd(qr, kr, vb, proj, sink_stack, mix)
    out = _matmul(mix, w_out_full, tm=tm, tn=1024, tk=D_MIX, out_dtype=F32, name="out_proj")
    dout, dy, g_norm_post, loss_part = _post_loss(out, x2d, loss_target[0], norm_post_w)

    dmix = _matmul(dout, w_out_full, trans_b=True, tm=tm, tn=1024, tk=D_MODEL, out_dtype=F32, name="dmix")
    dw_out = _matmul(mix, dout, trans_a=True, tm=512, tn=1024, tk=lp, out_dtype=BF16, name="dw_out")
    dqr, dkr, dv, dg, dsink_rows = _attn_bwd(qr, kr, vb, att, proj, dmix, sink_stack)
    dproj, dxbc, ddt_part, dd_part, da_part, g_ssd_norm, g_out = _ssd_bwd(
        dmix, ytot, proj, xbc, dt_rep, hprev, a_rep, dsk_rep, ssd_norm_w, tri,
        _Comm([(dw_out.reshape(N_DEV, D_MIX // N_DEV, D_MODEL), "scatter")]))
    dproj, dconv_w_int, dconv_b_int = _conv_bwd(dxbc, proj, conv_w_full, conv_b_int, dproj)
    dproj, ddt_bias = _act_bwd(dqr, dkr, dv, dg, ddt_part, cos_t, sin_t, reduce_t, dproj)
    dw_all = _matmul(hn, dproj, trans_a=True, tm=512, tn=1024, tk=lp, out_dtype=BF16, name="dw_in")
    dw_chip = _w_in_chip_slabs(dw_all)
    dhn, g_in = _matmul(dproj, w_all, trans_b=True, tm=tm, tn=1024, tk=1536, out_dtype=F32, name="dhn",
                        comm=_Comm([(dw_chip, "scatter")], scope="chips"))
    grad_x, dh0, g_norm_pre = _prenorm_bwd(dhn, x2d, h0, dy, norm_pre_w)

    dmeta = dh0[PAD_LEAD:, :]
    dconv_w_ref = _unperm_xbc(dconv_w_int)
    heads = lambda part: part.reshape(SSD_HEADS, HEAD_DIM).sum(axis=1).reshape(1, SSD_HEADS)
    small_local = _pack_small({
        "norm_pre_w": g_norm_pre, "conv_b": _unperm_xbc(dconv_b_int), "dt_bias": ddt_bias[:, :SSD_HEADS],
        "a_log": heads(da_part) * (-jnp.exp(a_log)), "d_skip": heads(dd_part), "ssd_norm_w": g_ssd_norm,
        "attn_sinks": dsink_rows.reshape(Q_HEADS, CHUNK).sum(axis=1).reshape(1, Q_HEADS),
        "norm_post_w": g_norm_post}, loss=loss_part[0, 0])
    g_conv, g_meta, g_small = _exchange(
        [(jnp.transpose(dconv_w_ref.reshape(CONV_WIDTH, N_DEV, D_CONV // N_DEV), (1, 0, 2)), "scatter"),
         (jnp.transpose(dmeta.reshape(N_META, N_DEV, D_MODEL // N_DEV), (1, 0, 2)), "scatter"),
         (small_local, "gather")], "exchange_small")

    res = {}
    res["w_in"] = [o[None] for o in _adamw(g_in, w_in[0], m_w_in[0], v_w_in[0], "adamw_w_in")]
    res["w_out"] = [o[None] for o in _adamw(g_out, w_out[0], m_w_out[0], v_w_out[0], "adamw_w_out")]
    res["conv_w"] = [o[None] for o in _adamw(g_conv, conv_w[0], m_conv_w[0], v_conv_w[0], "adamw_conv_w")]
    res["meta_tokens"] = _adamw(g_meta, meta_tokens, m_meta_tokens, v_meta_tokens, "adamw_meta")
    given = dict(norm_pre_w=(norm_pre_w, m_norm_pre_w, v_norm_pre_w), conv_b=(conv_b, m_conv_b, v_conv_b),
                 dt_bias=(dt_bias, m_dt_bias, v_dt_bias), a_log=(a_log, m_a_log, v_a_log),
                 d_skip=(d_skip, m_d_skip, v_d_skip), ssd_norm_w=(ssd_norm_w, m_ssd_norm_w, v_ssd_norm_w),
                 attn_sinks=(attn_sinks, m_attn_sinks, v_attn_sinks),
                 norm_post_w=(norm_post_w, m_norm_post_w, v_norm_post_w))
    packed = [_pack_small({k: t[j] for k, t in given.items()}) for j in range(3)]
    small_out = _adamw(g_small, packed[0], packed[1], packed[2], "adamw_small")
    small_res = [_unpack_small(r) for r in small_out]
    loss = small_out[0][0, SMALL_USED]

    order = ["meta_tokens", "norm_pre_w", "w_in", "conv_w", "conv_b", "dt_bias", "a_log", "d_skip", "ssd_norm_w",
             "attn_sinks", "w_out", "norm_post_w"]
    outs = []
    for j in range(4):
        for name in order:
            outs.append(res[name][j] if name in res else small_res[j][name])
    return (loss, grad_x[None], *outs)
```

```python
import functools

import numpy as np
import jax
import jax.numpy as jnp
from jax import lax
from jax.experimental import pallas as pl
from jax.experimental.pallas import tpu as pltpu

F32 = jnp.float32
BF16 = jnp.bfloat16
HIGHEST = lax.Precision.HIGHEST

N_DEV = 8
D_MODEL = 2048
CHUNK = 64
N_META = 16
PAD_LEAD = CHUNK - N_META
EPS = 1e-6
N_GROUPS = 8
HEADS_PER_GROUP = 4
HEAD_DIM = 64
GROUP_W = HEADS_PER_GROUP * HEAD_DIM
D_STATE = 128
D_SSD = 2048
D_CONV = 4096
SSD_HEADS = 32
CONV_WIDTH = 4
Q_HEADS = 16
KV_HEADS = 4
REP = 4
D_ATT = 1024
D_KV = 256
WINDOW_CHUNKS = 2
BAND = (WINDOW_CHUNKS + 1) * CHUNK
ROPE_THETA = 10000.0
D_MIX = D_SSD + D_ATT
D_IN_PROJ = 8736
SHARD_IN = D_IN_PROJ // N_DEV

OFF_Z, OFF_XBC, OFF_Q, OFF_G, OFF_K, OFF_V, OFF_DT = 0, 2048, 6144, 7168, 8192, 8448, 8704
NP = 9216
TAIL_W = NP - OFF_Q
XBC_BLK = 512
SSD_GPS = 8
SSD_CPS = 2

ADAM_LR, ADAM_B1, ADAM_B2, ADAM_EPS, ADAM_WD, ADAM_STEP = 0.001, 0.9, 0.999, 1e-08, 0.01, 10

VMEM_LIMIT = 48 * 1024 * 1024

NN = (((1,), (0,)), ((), ()))
NT = (((1,), (1,)), ((), ()))
TN = (((0,), (0,)), ((), ()))
ANY = pl.BlockSpec(memory_space=pl.ANY)


def _dot(a, b, dims=NN, precision=None):
    return lax.dot_general(a, b, dims, precision=precision, preferred_element_type=F32)


def _tile(n, prefs):
    for t in prefs:
        if n % t == 0:
            return t
    return n


def _params(sem, vmem=VMEM_LIMIT, side_effects=False):
    return pltpu.CompilerParams(dimension_semantics=sem, vmem_limit_bytes=vmem, has_side_effects=side_effects)


def _sigmoid(x):
    return 1.0 / (1.0 + jnp.exp(-x))


class _Comm:
    def __init__(self, items, scope="devices"):
        self.items = items
        self.scope = scope
        self.slabs = slabs = N_DEV if scope == "devices" else N_DEV // 2
        self.n = n = len(items)
        self.operands = [arr for arr, _ in items]
        self.in_specs = [ANY] * n
        self.out_specs = [ANY] * n
        self.out_shape = [jax.ShapeDtypeStruct((slabs,) + tuple(arr.shape) if kind == "gather" else tuple(arr.shape),
                                               arr.dtype) for arr, kind in items]
        self.scratch = [pltpu.SemaphoreType.DMA((n, slabs - 1)), pltpu.SemaphoreType.DMA((n, slabs - 1)),
                        pltpu.SemaphoreType.DMA((n,))]

    def _places(self):
        pos = (lax.axis_index("x"), lax.axis_index("y"), lax.axis_index("c"))
        if self.scope == "devices":
            index = lambda p: 4 * p[0] + 2 * p[1] + p[2]
            masks = range(1, N_DEV)
        else:
            index = lambda p: 2 * p[0] + p[1]
            masks = (2, 4, 6)
        peers = []
        for k in masks:
            p = tuple(1 - pos[b] if (k >> (2 - b)) & 1 else pos[b] for b in range(3))
            peers.append((p, index(p)))
        return index(pos), peers

    def _copies(self, ins, outs, sems, landed):
        send_sems, recv_sems, local_sems = sems
        me, peers = self._places()
        local, remote = [], []
        for a, (_, kind) in enumerate(self.items):
            own = ins[a] if kind == "gather" else ins[a].at[me]
            local.append(pltpu.make_async_copy(own, outs[a].at[me], local_sems.at[a]))
            for k, (p, pid) in enumerate(peers):
                remote.append(pltpu.make_async_remote_copy(
                    src_ref=ins[a] if kind == "gather" else ins[a].at[pid],
                    dst_ref=outs[a].at[pid if landed else me],
                    send_sem=send_sems.at[a, k], recv_sem=recv_sems.at[a, k],
                    device_id=p, device_id_type=pl.DeviceIdType.MESH))
        return local, remote

    def start(self, ins, outs, sems):
        local, remote = self._copies(ins, outs, sems, landed=False)
        for cp in local + remote:
            cp.start()

    def wait(self, ins, outs, sems):
        local, remote = self._copies(ins, outs, sems, landed=True)
        for cp in remote + local:
            cp.wait()


def _exchange(items, name):
    comm = _Comm(items)
    n = comm.n

    def body(*refs):
        ins, outs, sems = refs[:n], refs[n:2 * n], refs[2 * n:]
        comm.start(ins, outs, sems)
        comm.wait(ins, outs, sems)

    return pl.pallas_call(
        body, name=name, in_specs=comm.in_specs, out_specs=comm.out_specs, out_shape=comm.out_shape,
        scratch_shapes=comm.scratch, compiler_params=pltpu.CompilerParams(has_side_effects=True),
    )(*comm.operands)


class _GatherTwoLevel:
    def __init__(self, arrays):
        self.arrays = arrays
        self.n = n = len(arrays)
        self.operands = list(arrays)
        self.in_specs = [ANY] * n
        self.out_specs = [ANY] * n
        self.out_shape = [jax.ShapeDtypeStruct((N_DEV,) + tuple(a.shape), a.dtype) for a in arrays]
        self.scratch = [pltpu.SemaphoreType.DMA((n, N_DEV - 1)), pltpu.SemaphoreType.DMA((n, N_DEV - 1)),
                        pltpu.SemaphoreType.DMA((n,))]

    def _plan(self, ins, outs, sems):
        send_sems, recv_sems, local_sems = sems
        x, y, c = lax.axis_index("x"), lax.axis_index("y"), lax.axis_index("c")
        me, sibling = (x, y, c), (x, y, 1 - c)
        chips = [(1 - x, y), (x, 1 - y), (1 - x, 1 - y)]

        def slab(a, place):
            return outs[a].at[4 * place[0] + 2 * place[1] + place[2]]

        def copy(a, k, block, to, src=None):
            return pltpu.make_async_remote_copy(
                src_ref=slab(a, block) if src is None else src, dst_ref=slab(a, block),
                send_sem=send_sems.at[a, k], recv_sem=recv_sems.at[a, k],
                device_id=to, device_id_type=pl.DeviceIdType.MESH)

        own, mine = [], []
        for a in range(self.n):
            mine.append(pltpu.make_async_copy(ins[a], slab(a, me), local_sems.at[a]))
            own.append(copy(a, 0, me, sibling, src=ins[a]))
            own += [copy(a, 1 + j, me, (*chip, c), src=ins[a]) for j, chip in enumerate(chips)]
        return me, sibling, chips, c, copy, own, mine

    def start(self, ins, outs, sems):
        _, _, _, _, _, own, mine = self._plan(ins, outs, sems)
        for cp in mine + own:
            cp.start()

    def wait(self, ins, outs, sems):
        me, sibling, chips, c, copy, own, mine = self._plan(ins, outs, sems)
        forwards = []
        for j, chip in enumerate(chips):
            for a in range(self.n):
                copy(a, 1 + j, (*chip, c), me).wait_recv()
                fwd = copy(a, 4 + j, (*chip, c), sibling)
                fwd.start()
                forwards.append(fwd)
        for a in range(self.n):
            copy(a, 0, sibling, me).wait_recv()
            for j, chip in enumerate(chips):
                copy(a, 4 + j, (*chip, 1 - c), me).wait_recv()
        for cp in own + forwards:
            cp.wait_send()
        for loc in mine:
            loc.wait()


def _gather_two_level(arrays, name):
    comm = _GatherTwoLevel(arrays)
    n = comm.n

    def body(*refs):
        ins, outs, sems = refs[:n], refs[n:2 * n], refs[2 * n:]
        comm.start(ins, outs, sems)
        comm.wait(ins, outs, sems)

    return pl.pallas_call(
        body, name=name, in_specs=comm.in_specs, out_specs=comm.out_specs, out_shape=comm.out_shape,
        scratch_shapes=comm.scratch, compiler_params=pltpu.CompilerParams(has_side_effects=True),
    )(*arrays)


def _matmul(a, b, *, tm, tn, tk, out_dtype, name, trans_a=False, trans_b=False, comm=None):
    m, k = (a.shape[1], a.shape[0]) if trans_a else a.shape
    n = b.shape[0] if trans_b else b.shape[1]
    nk = k // tk
    dims = TN if trans_a else (NT if trans_b else NN)
    assert not (trans_a and trans_b)
    nc = comm.n if comm else 0
    grid = (m // tm, n // tn, nk)

    def body(*refs):
        a_ref, b_ref = refs[:2]
        cin = refs[2:2 + nc]
        o_ref = refs[2 + nc]
        cout = refs[3 + nc:3 + 2 * nc]
        scratch = refs[3 + 2 * nc:]
        sems = scratch[len(scratch) - 3:] if comm else None
        i, j, kk = pl.program_id(0), pl.program_id(1), pl.program_id(2)
        if comm:
            @pl.when((i == 0) & (j == 0) & (kk == 0))
            def _():
                comm.start(cin, cout, sems)

        if nk == 1:
            o_ref[...] = _dot(a_ref[...], b_ref[...], dims).astype(out_dtype)
        else:
            acc_ref = scratch[0]

            @pl.when(kk == 0)
            def _():
                acc_ref[...] = jnp.zeros_like(acc_ref)

            acc_ref[...] += _dot(a_ref[...], b_ref[...], dims)

            @pl.when(kk == nk - 1)
            def _():
                o_ref[...] = acc_ref[...].astype(out_dtype)

        if comm:
            @pl.when((i == grid[0] - 1) & (j == grid[1] - 1) & (kk == nk - 1))
            def _():
                comm.wait(cin, cout, sems)

    a_spec = (pl.BlockSpec((tk, tm), lambda i, j, kk: (kk, i)) if trans_a
              else pl.BlockSpec((tm, tk), lambda i, j, kk: (i, kk)))
    b_spec = (pl.BlockSpec((tn, tk), lambda i, j, kk: (j, kk)) if trans_b
              else pl.BlockSpec((tk, tn), lambda i, j, kk: (kk, j)))
    sem = ("arbitrary",) * 3 if comm else ("parallel", "parallel", "arbitrary")
    res = pl.pallas_call(
        body, name=name, grid=grid,
        in_specs=[a_spec, b_spec] + (comm.in_specs if comm else []),
        out_specs=[pl.BlockSpec((tm, tn), lambda i, j, kk: (i, j))] + (comm.out_specs if comm else []),
        out_shape=[jax.ShapeDtypeStruct((m, n), out_dtype)] + (comm.out_shape if comm else []),
        scratch_shapes=([] if nk == 1 else [pltpu.VMEM((tm, tn), F32)]) + (comm.scratch if comm else []),
        compiler_params=_params(sem, side_effects=bool(comm)),
    )(a, b, *(comm.operands if comm else []))
    return res if comm else res[0]


ROW_K = 6
X_K = 8


def _x_specs(nx, k):
    return [pl.BlockSpec((CHUNK, D_MODEL), functools.partial(lambda u, i: (jnp.clip(k * i + u - 1, 0, nx - 1), 0), u))
            for u in range(k)]


def _chunk_of_row(i, k):
    return k * i + lax.broadcasted_iota(jnp.int32, (k * CHUNK, 1), 0) // CHUNK


def _prenorm(x2d, h0, w):
    nx = x2d.shape[0] // CHUNK
    nc = nx + 2
    k = _tile(nc, (ROW_K, 3, 2, 1))

    def body(*refs):
        x_refs = refs[:k]
        h0_ref, w_ref, o_ref = refs[k:]
        i = pl.program_id(0)
        x = jnp.concatenate([r[...] for r in x_refs], axis=0)
        head = jnp.concatenate([h0_ref[...], x[CHUNK:, :]], axis=0)
        chunk = _chunk_of_row(i, k)
        h = jnp.where(i == 0, head, x) * (chunk <= nx).astype(F32)
        rstd = lax.rsqrt(jnp.mean(h * h, axis=-1, keepdims=True) + EPS)
        o_ref[...] = (h * rstd * w_ref[...]).astype(BF16)

    rows = k * CHUNK
    return pl.pallas_call(
        body, name="prenorm", grid=(nc // k,),
        in_specs=_x_specs(nx, k) + [pl.BlockSpec((CHUNK, D_MODEL), lambda i: (0, 0)), pl.BlockSpec((1, D_MODEL), lambda i: (0, 0))],
        out_specs=pl.BlockSpec((rows, D_MODEL), lambda i: (i, 0)),
        out_shape=jax.ShapeDtypeStruct((nc * CHUNK, D_MODEL), BF16),
        compiler_params=_params(("parallel",)),
    )(*([x2d] * k), h0, w)


CONV_COLS = 512
HALO = 8


def _conv_pre(ext, w, b):
    taps = [ext[HALO:, :]] + [pltpu.roll(ext, j, 0)[HALO:, :] for j in range(1, CONV_WIDTH)]
    acc = b + w[3:4, :] * taps[0]
    for j in range(1, CONV_WIDTH):
        acc = acc + w[3 - j:4 - j, :] * taps[j]
    return acc, taps


def _conv_bwd(dxbc, proj, conv_w, conv_b, dproj):
    lp = proj.shape[0]
    t = _tile(lp, (704, 384, 128, 64))
    hb = t // HALO
    nt = lp // t
    c0 = OFF_XBC // CONV_COLS

    def body(dx_ref, dxn_ref, u_ref, up_ref, un_ref, w_ref, b_ref, _, du_ref, dw_ref, db_ref):
        i = pl.program_id(1)
        w = w_ref[...]
        up = up_ref[...] * (i > 0).astype(F32)
        ext = jnp.concatenate([up, u_ref[...], un_ref[...]], axis=0)
        pre, taps = _conv_pre(ext, w, b_ref[...])
        dxn = dxn_ref[...] * (i < nt - 1).astype(F32)
        dxe = jnp.concatenate([dx_ref[...], dxn], axis=0)
        sg = _sigmoid(pre)
        dpre = dxe * sg * (1.0 + pre * (1.0 - sg))
        du = w[3:4, :] * dpre[:t, :]
        for j in range(1, CONV_WIDTH):
            du = du + w[3 - j:4 - j, :] * pltpu.roll(dpre, t + HALO - j, 0)[:t, :]
        du_ref[...] = du.astype(BF16)

        @pl.when(i == 0)
        def _():
            dw_ref[...] = jnp.zeros_like(dw_ref)
            db_ref[...] = jnp.zeros_like(db_ref)

        dp = dpre[:t, :]
        db_ref[...] += jnp.sum(dp, axis=0, keepdims=True)
        for j in range(CONV_WIDTH):
            dw_ref[3 - j:4 - j, :] += jnp.sum(dp * taps[j][:t, :], axis=0, keepdims=True)

    nxt = lambda i: jnp.minimum((i + 1) * hb, lp // HALO - 1)
    return pl.pallas_call(
        body, name="conv_bwd", grid=(D_CONV // CONV_COLS, nt),
        in_specs=[
            pl.BlockSpec((t, CONV_COLS), lambda j, i: (i, j)),
            pl.BlockSpec((HALO, CONV_COLS), lambda j, i: (nxt(i), j)),
            pl.BlockSpec((t, CONV_COLS), lambda j, i: (i, c0 + j)),
            pl.BlockSpec((HALO, CONV_COLS), lambda j, i: (jnp.maximum(i * hb - 1, 0), c0 + j)),
            pl.BlockSpec((HALO, CONV_COLS), lambda j, i: (nxt(i), c0 + j)),
            pl.BlockSpec((CONV_WIDTH, CONV_COLS), lambda j, i: (0, j)),
            pl.BlockSpec((1, CONV_COLS), lambda j, i: (0, j)),
            ANY,
        ],
        out_specs=[
            pl.BlockSpec((t, CONV_COLS), lambda j, i: (i, c0 + j)),
            pl.BlockSpec((CONV_WIDTH, CONV_COLS), lambda j, i: (0, j)),
            pl.BlockSpec((1, CONV_COLS), lambda j, i: (0, j)),
        ],
        out_shape=[
            jax.ShapeDtypeStruct((lp, NP), BF16),
            jax.ShapeDtypeStruct((CONV_WIDTH, D_CONV), F32),
            jax.ShapeDtypeStruct((1, D_CONV), F32),
        ],
        input_output_aliases={7: 0},
        compiler_params=_params(("parallel", "arbitrary")),
    )(dxbc, dxbc, proj, proj, proj, conv_w, conv_b, dproj)


def _swap_halves(t):
    w = t.shape[-1]
    lane = lax.broadcasted_iota(jnp.int32, t.shape, 1)
    return jnp.where((lane % HEAD_DIM) < HEAD_DIM // 2, pltpu.roll(t, w - HEAD_DIM // 2, 1),
                     pltpu.roll(t, HEAD_DIM // 2, 1))


def _act_fwd(proj, cos_t, sin_t, expand, dt_bias_pad):
    lp = proj.shape[0]
    t = _tile(lp, (384, 128, 64))

    def body(q_ref, k_ref, v_ref, dt_ref, cos_ref, sin_ref, ex_ref, bias_ref, qo_ref, ko_ref, vo_ref, dto_ref):
        i = pl.program_id(0)
        cos = cos_ref[...]
        sin = sin_ref[...]
        q = q_ref[...]
        qo_ref[...] = (q * jnp.tile(cos, (1, D_ATT // 128)) + _swap_halves(q) * jnp.tile(sin, (1, D_ATT // 128))).astype(BF16)
        k = k_ref[...]
        ko_ref[...] = (k * jnp.tile(cos, (1, D_KV // 128)) + _swap_halves(k) * jnp.tile(sin, (1, D_KV // 128))).astype(BF16)
        vo_ref[...] = v_ref[...].astype(BF16)
        raw = dt_ref[...] + bias_ref[...]
        sp = jnp.maximum(raw, 0.0) + jnp.log1p(jnp.exp(-jnp.abs(raw)))
        row = i * t + lax.broadcasted_iota(jnp.int32, sp.shape, 0)
        dto_ref[...] = _dot(jnp.where(row >= PAD_LEAD, sp, 0.0), ex_ref[...], NN, HIGHEST)

    return pl.pallas_call(
        body, name="act_fwd", grid=(lp // t,),
        in_specs=[
            pl.BlockSpec((t, D_ATT), lambda i: (i, OFF_Q // D_ATT)),
            pl.BlockSpec((t, D_KV), lambda i: (i, OFF_K // D_KV)),
            pl.BlockSpec((t, D_KV), lambda i: (i, OFF_V // D_KV)),
            pl.BlockSpec((t, 128), lambda i: (i, OFF_DT // 128)),
            pl.BlockSpec((t, 128), lambda i: (i, 0)),
            pl.BlockSpec((t, 128), lambda i: (i, 0)),
            pl.BlockSpec((128, D_SSD), lambda i: (0, 0)),
            pl.BlockSpec((1, 128), lambda i: (0, 0)),
        ],
        out_specs=[
            pl.BlockSpec((t, D_ATT), lambda i: (i, 0)),
            pl.BlockSpec((t, D_KV), lambda i: (i, 0)),
            pl.BlockSpec((t, D_KV), lambda i: (i, 0)),
            pl.BlockSpec((t, D_SSD), lambda i: (i, 0)),
        ],
        out_shape=[
            jax.ShapeDtypeStruct((lp, D_ATT), BF16),
            jax.ShapeDtypeStruct((lp, D_KV), BF16),
            jax.ShapeDtypeStruct((lp, D_KV), BF16),
            jax.ShapeDtypeStruct((lp, D_SSD), F32),
        ],
        compiler_params=_params(("parallel",)),
    )(proj, proj, proj, proj, cos_t, sin_t, expand, dt_bias_pad)


def _act_bwd(dqr, dkr, dv, dg, ddt_part, cos_t, sin_t, reduce_t, dproj):
    lp = dqr.shape[0]
    t = _tile(lp, (384, 128, 64))

    def body(dq_ref, dk_ref, dv_ref, dg_ref, ddt_ref, cos_ref, sin_ref, red_ref, _, o_ref, db_ref):
        i = pl.program_id(0)
        cos = cos_ref[...]
        sin = sin_ref[...]
        dq = dq_ref[...]
        dq = dq * jnp.tile(cos, (1, D_ATT // 128)) + _swap_halves(dq * jnp.tile(sin, (1, D_ATT // 128)))
        dk = dk_ref[...]
        dk = dk * jnp.tile(cos, (1, D_KV // 128)) + _swap_halves(dk * jnp.tile(sin, (1, D_KV // 128)))
        ddt = _dot(ddt_ref[...], red_ref[...], NN, HIGHEST)
        o_ref[...] = jnp.concatenate(
            [dq.astype(BF16), dg_ref[...].astype(BF16), dk.astype(BF16), dv_ref[...].astype(BF16), ddt.astype(BF16),
             jnp.zeros((t, NP - OFF_DT - 128), BF16)], axis=1)

        @pl.when(i == 0)
        def _():
            db_ref[...] = jnp.zeros_like(db_ref)

        db_ref[...] += jnp.sum(ddt, axis=0, keepdims=True)

    return pl.pallas_call(
        body, name="act_bwd", grid=(lp // t,),
        in_specs=[
            pl.BlockSpec((t, D_ATT), lambda i: (i, 0)),
            pl.BlockSpec((t, D_KV), lambda i: (i, 0)),
            pl.BlockSpec((t, D_KV), lambda i: (i, 0)),
            pl.BlockSpec((t, D_ATT), lambda i: (i, 0)),
            pl.BlockSpec((t, D_SSD), lambda i: (i, 0)),
            pl.BlockSpec((t, 128), lambda i: (i, 0)),
            pl.BlockSpec((t, 128), lambda i: (i, 0)),
            pl.BlockSpec((D_SSD, 128), lambda i: (0, 0)),
            ANY,
        ],
        out_specs=[pl.BlockSpec((t, TAIL_W), lambda i: (i, OFF_Q // TAIL_W)), pl.BlockSpec((1, 128), lambda i: (0, 0))],
        out_shape=[jax.ShapeDtypeStruct((lp, NP), BF16), jax.ShapeDtypeStruct((1, 128), F32)],
        input_output_aliases={8: 0},
        compiler_params=_params(("arbitrary",)),
    )(dqr, dkr, dv, dg, ddt_part, cos_t, sin_t, reduce_t, dproj)


def _cs_row(cs):
    row = lax.broadcasted_iota(jnp.int32, cs.shape, 0)
    lane = lax.broadcasted_iota(jnp.int32, cs.shape, 1)
    return jnp.sum(jnp.where(row == lane % HEAD_DIM, cs, 0.0), axis=0, keepdims=True)


def _ssd_fwd(proj, conv_w, conv_b, dt_rep, a_rep, dsk_rep, wn, tri):
    lp = proj.shape[0]
    nc = lp // CHUNK
    assert SSD_GPS == N_GROUPS
    gw = SSD_GPS * GROUP_W
    cps = _tile(nc, (SSD_CPS, 1))

    def body(rawa_ref, rawb_ref, cw_ref, cb_ref, dt_ref, z_ref, a_ref, dsk_ref, wn_ref, tri_ref,
             yn_ref, ytot_ref, hprev_ref, xbc_ref, h_scr, tail_scr):
        @pl.when(pl.program_id(1) == 0)
        def _():
            h_scr[...] = jnp.zeros_like(h_scr)
            tail_scr[...] = jnp.zeros_like(tail_scr)

        for sub in range(cps):
            rs = slice(sub * CHUNK, (sub + 1) * CHUNK)
            raw = jnp.concatenate([rawa_ref[rs, :], rawb_ref[rs, :]], axis=1)
            pre, _ = _conv_pre(jnp.concatenate([tail_scr[...], raw], axis=0), cw_ref[...], cb_ref[...])
            tail_scr[...] = raw[CHUNK - HALO:, :]
            xbc_ref[rs, :] = pre * _sigmoid(pre)

        G = range(SSD_GPS)
        colsl = [slice(gi * GROUP_W, (gi + 1) * GROUP_W) for gi in G]
        rows4 = lax.broadcasted_iota(jnp.int32, (GROUP_W, GROUP_W), 0) // HEAD_DIM
        cols4 = lax.broadcasted_iota(jnp.int32, (GROUP_W, GROUP_W), 1) // HEAD_DIM
        lrow = lax.broadcasted_iota(jnp.int32, (CHUNK, GROUP_W), 0)
        lcol = lax.broadcasted_iota(jnp.int32, (CHUNK, GROUP_W), 1) % HEAD_DIM

        def one_chunk(sub):
            rs = slice(sub * CHUNK, (sub + 1) * CHUNK)
            xbc = [xbc_ref[rs, gi * XBC_BLK:(gi + 1) * XBC_BLK] for gi in G]
            dt = [dt_ref[rs, colsl[gi]] for gi in G]
            xs = [xbc[gi][:, :GROUP_W] for gi in G]
            b = [xbc[gi][:, GROUP_W:GROUP_W + D_STATE].astype(BF16) for gi in G]
            c = [xbc[gi][:, GROUP_W + D_STATE:].astype(BF16) for gi in G]
            hprev = [h_scr[gi] for gi in G]
            cs = [_dot(tri_ref[...], dt[gi] * a_ref[:, colsl[gi]], NN, HIGHEST) for gi in G]
            yoff = [_dot(c[gi], hprev[gi].astype(BF16)) for gi in G]
            cs_t = [_cs_row(cs[gi]) for gi in G]
            xdt = [xs[gi] * dt[gi] for gi in G]
            cs_last = [cs[gi][CHUNK - 1:CHUNK, :] for gi in G]
            st = [_dot(b[gi], (xdt[gi] * jnp.exp(cs_last[gi] - cs[gi])).astype(BF16), TN) for gi in G]
            cb4 = [_dot(c[gi], jnp.concatenate([b[gi]] * HEADS_PER_GROUP, axis=0), NT) for gi in G]
            m = [(cb4[gi] * jnp.exp(jnp.where(lrow >= lcol, cs[gi] - cs_t[gi], -jnp.inf))).astype(BF16) for gi in G]
            xbd = [jnp.where(rows4 == cols4, jnp.concatenate([xdt[gi].astype(BF16)] * HEADS_PER_GROUP, axis=0), 0.0)
                   for gi in G]
            ydiag = [_dot(m[gi], xbd[gi]) for gi in G]
            for gi in G:
                cols = colsl[gi]
                ytot = ydiag[gi] + yoff[gi] * jnp.exp(cs[gi]) + dsk_ref[:, cols] * xs[gi]
                z = z_ref[rs, cols]
                gz = ytot * (z * _sigmoid(z))
                rstd = lax.rsqrt(jnp.mean(gz * gz, axis=-1, keepdims=True) + EPS)
                hprev_ref[sub, gi] = hprev[gi]
                h_scr[gi] = hprev[gi] * jnp.exp(cs_last[gi]) + st[gi]
                ytot_ref[rs, cols] = ytot
                yn_ref[rs, cols] = (gz * rstd * wn_ref[:, cols]).astype(BF16)

        for sub in range(cps):
            one_chunk(sub)

    vec = pl.BlockSpec((1, gw), lambda g, c: (0, g))
    blk = pl.BlockSpec((cps * CHUNK, gw), lambda g, c: (c, g))
    half = D_CONV // 2
    return pl.pallas_call(
        body, name="ssd_fwd", grid=(N_GROUPS // SSD_GPS, nc // cps),
        in_specs=[
            pl.BlockSpec((cps * CHUNK, half), lambda g, c: (c, OFF_XBC // half)),
            pl.BlockSpec((cps * CHUNK, half), lambda g, c: (c, OFF_XBC // half + 1)),
            pl.BlockSpec((CONV_WIDTH, D_CONV), lambda g, c: (0, 0)),
            pl.BlockSpec((1, D_CONV), lambda g, c: (0, 0)),
            blk, blk, vec, vec, vec,
            pl.BlockSpec((CHUNK, CHUNK), lambda g, c: (0, 0)),
        ],
        out_specs=[blk, blk, pl.BlockSpec((cps, SSD_GPS, D_STATE, GROUP_W), lambda g, c: (c, g, 0, 0)),
                   pl.BlockSpec((cps * CHUNK, D_CONV), lambda g, c: (c, 0))],
        out_shape=[
            jax.ShapeDtypeStruct((lp, D_MIX), BF16),
            jax.ShapeDtypeStruct((lp, D_SSD), F32),
            jax.ShapeDtypeStruct((nc, N_GROUPS, D_STATE, GROUP_W), F32),
            jax.ShapeDtypeStruct((lp, D_CONV), F32),
        ],
        scratch_shapes=[pltpu.VMEM((SSD_GPS, D_STATE, GROUP_W), F32), pltpu.VMEM((HALO, D_CONV), F32)],
        compiler_params=_params(("arbitrary", "arbitrary")),
    )(proj, proj, conv_w, conv_b, dt_rep, proj, a_rep, dsk_rep, wn, tri)


def _ssd_bwd(dmix, ytot, proj, xbc, dt_rep, hprev, a_rep, dsk_rep, wn, tri, comm):
    lp = xbc.shape[0]
    nc = lp // CHUNK
    gps = SSD_GPS
    gw = gps * GROUP_W
    cps = _tile(nc, (SSD_CPS, 1))
    ncm = comm.n
    n_in, n_out = 10, 6
    grid = (N_GROUPS // gps, nc // cps)

    def all_groups(refs):
        (dyn_ref, ytot_ref, z_ref, xbc_ref, dt_ref, hprev_ref, a_ref, dsk_ref, wn_ref, tri_ref,
         dz_ref, dxbc_ref, ddt_ref, dd_ref, da_ref, dwn_ref, dh_scr) = refs
        G = range(gps)
        H = range(HEADS_PER_GROUP)
        cl = [slice(gi * GROUP_W, (gi + 1) * GROUP_W) for gi in G]
        hl = [slice(r * HEAD_DIM, (r + 1) * HEAD_DIM) for r in H]
        tri = tri_ref[...]
        xbc = [xbc_ref[:, gi * XBC_BLK:(gi + 1) * XBC_BLK] for gi in G]
        dt = [dt_ref[:, cl[gi]] for gi in G]
        a = [a_ref[:, cl[gi]] for gi in G]
        xs = [xbc[gi][:, :GROUP_W] for gi in G]
        bbf = [xbc[gi][:, GROUP_W:GROUP_W + D_STATE].astype(BF16) for gi in G]
        cbf = [xbc[gi][:, GROUP_W + D_STATE:].astype(BF16) for gi in G]
        hprev = [hprev_ref[gi] for gi in G]
        hbf = [hprev[gi].astype(BF16) for gi in G]
        dhn = [dh_scr[gi] for gi in G]
        dhnb = [dhn[gi].astype(BF16) for gi in G]
        cs = [_dot(tri, dt[gi] * a[gi], NN, HIGHEST) for gi in G]
        g = [_dot(cbf[gi], hbf[gi]) for gi in G]
        dxw = [_dot(bbf[gi], dhnb[gi]) for gi in G]
        cs_t = [_cs_row(cs[gi]) for gi in G]
        dy = []
        for gi in G:
            ytot = ytot_ref[:, cl[gi]]
            z = z_ref[:, cl[gi]]
            dyn = dyn_ref[:, cl[gi]]
            sz = _sigmoid(z)
            silu_z = z * sz
            gz = ytot * silu_z
            rstd = lax.rsqrt(jnp.mean(gz * gz, axis=-1, keepdims=True) + EPS)
            xhat = gz * rstd
            dwn_ref[:, cl[gi]] += jnp.sum(dyn * xhat, axis=0, keepdims=True)
            dxhat = dyn * wn_ref[:, cl[gi]]
            dgz = rstd * (dxhat - xhat * jnp.mean(dxhat * xhat, axis=-1, keepdims=True))
            dy.append(dgz * silu_z)
            dz_ref[:, cl[gi]] = (dgz * ytot * (sz * (1.0 + z * (1.0 - sz)))).astype(BF16)
            dd_ref[:, cl[gi]] += jnp.sum(dy[gi] * xs[gi], axis=0, keepdims=True)
        xdt = [xs[gi] * dt[gi] for gi in G]
        e = [jnp.exp(cs[gi]) for gi in G]
        cs_last = [cs[gi][CHUNK - 1:CHUNK, :] for gi in G]
        dte = [jnp.exp(cs_last[gi] - cs[gi]) for gi in G]
        cd = [jnp.exp(cs_last[gi]) for gi in G]
        dgb = [(dy[gi] * e[gi]).astype(BF16) for gi in G]
        dyb = [dy[gi].astype(BF16) for gi in G]
        xdtb = [xdt[gi].astype(BF16) for gi in G]
        dc = [_dot(dgb[gi], hbf[gi], NT) for gi in G]
        dhprev = [_dot(cbf[gi], dgb[gi], TN) for gi in G]
        db = [_dot((xdt[gi] * dte[gi]).astype(BF16), dhnb[gi], NT) for gi in G]
        row = lax.broadcasted_iota(jnp.int32, (CHUNK, CHUNK), 0)
        causal = row >= lax.broadcasted_iota(jnp.int32, (CHUNK, CHUNK), 1)
        cb = [_dot(cbf[gi], bbf[gi], NT) for gi in G]
        dm = [[_dot(dyb[gi][:, hl[r]], xdtb[gi][:, hl[r]], NT) for r in H] for gi in G]
        mb, dseg, dcbb = [], [], []
        for gi in G:
            mb.append([])
            dseg.append([])
            dcb = None
            for r in H:
                seg = cs[gi][:, r * HEAD_DIM:r * HEAD_DIM + 1] - cs_t[gi][:, hl[r]]
                lm = jnp.exp(jnp.where(causal, seg, -jnp.inf))
                m = cb[gi] * lm
                mb[gi].append(m.astype(BF16))
                dseg[gi].append(dm[gi][r] * m)
                dcb = dm[gi][r] * lm if r == 0 else dcb + dm[gi][r] * lm
            dcbb.append(dcb.astype(BF16))
        dxdt_diag = [[_dot(mb[gi][r], dyb[gi][:, hl[r]], TN) for r in H] for gi in G]
        ones = jnp.ones((CHUNK, HEAD_DIM), F32)
        colsum = [[_dot(dseg[gi][r], ones, TN, HIGHEST) for r in H] for gi in G]
        dc2 = [_dot(dcbb[gi], bbf[gi]) for gi in G]
        db2 = [_dot(dcbb[gi], cbf[gi], TN) for gi in G]
        dcs = []
        for gi in G:
            t_dte = dxw[gi] * xdt[gi] * dte[gi]
            dcs_last = (jnp.sum(dhn[gi] * hprev[gi], axis=0, keepdims=True) * cd[gi]
                        + jnp.sum(t_dte, axis=0, keepdims=True))
            diag = jnp.concatenate(
                [(jnp.sum(dseg[gi][r], axis=1, keepdims=True) - colsum[gi][r]) * (1.0 / HEAD_DIM) for r in H], axis=1)
            d = dy[gi] * g[gi] * e[gi] - t_dte + diag
            row = lax.broadcasted_iota(jnp.int32, d.shape, 0)
            dcs.append(d + jnp.where(row == CHUNK - 1, dcs_last, 0.0))
        dda = [_dot(tri, dcs[gi], TN, HIGHEST) for gi in G]
        for gi in G:
            dxdt = dxw[gi] * dte[gi] + jnp.concatenate(dxdt_diag[gi], axis=1)
            da_ref[:, cl[gi]] += jnp.sum(dda[gi] * dt[gi], axis=0, keepdims=True)
            ddt = dda[gi] * a[gi] + dxdt * xs[gi]
            dxs = dsk_ref[:, cl[gi]] * dy[gi] + dxdt * dt[gi]
            ddt_ref[:, cl[gi]] = ddt * (1.0 - jnp.exp(-dt[gi]))
            dxbc_ref[:, gi * XBC_BLK:(gi + 1) * XBC_BLK] = jnp.concatenate(
                [dxs, db[gi] + db2[gi], dc[gi] + dc2[gi]], axis=1)
            dh_scr[gi] = dhprev[gi] + dhn[gi] * cd[gi]

    def body(*refs):
        ins = refs[:n_in]
        cin = refs[n_in:n_in + ncm]
        outs = refs[n_in + ncm:n_in + ncm + n_out]
        cout = refs[n_in + ncm + n_out:n_in + 2 * ncm + n_out]
        dh_scr = refs[n_in + 2 * ncm + n_out]
        sems = refs[n_in + 2 * ncm + n_out + 1:]
        g, c = pl.program_id(0), pl.program_id(1)

        @pl.when((g == 0) & (c == 0))
        def _():
            comm.start(cin, cout, sems)

        @pl.when(c == 0)
        def _():
            dh_scr[...] = jnp.zeros_like(dh_scr)
            for ref in outs[3:]:
                ref[...] = jnp.zeros_like(ref)

        for sub in reversed(range(cps)):
            rs = pl.ds(sub * CHUNK, CHUNK)
            chunk_ins = tuple(r.at[rs] for r in ins[:5]) + (ins[5].at[sub],) + ins[6:]
            chunk_outs = tuple(r.at[rs] for r in outs[:3]) + outs[3:]
            all_groups(chunk_ins + chunk_outs + (dh_scr,))

        @pl.when((g == grid[0] - 1) & (c == grid[1] - 1))
        def _():
            comm.wait(cin, cout, sems)

    rev = lambda c: grid[1] - 1 - c
    vec = pl.BlockSpec((1, gw), lambda g, c: (0, g))
    blk = pl.BlockSpec((cps * CHUNK, gw), lambda g, c: (rev(c), g))
    xblk = pl.BlockSpec((cps * CHUNK, gps * XBC_BLK), lambda g, c: (rev(c), g))
    res = pl.pallas_call(
        body, name="ssd_bwd", grid=grid,
        in_specs=[blk, blk, blk, xblk, blk,
                  pl.BlockSpec((cps, gps, D_STATE, GROUP_W), lambda g, c: (rev(c), g, 0, 0)),
                  vec, vec, vec,
                  pl.BlockSpec((CHUNK, CHUNK), lambda g, c: (0, 0))] + comm.in_specs,
        out_specs=[blk, xblk, blk, vec, vec, vec] + comm.out_specs,
        out_shape=[
            jax.ShapeDtypeStruct((lp, NP), BF16),
            jax.ShapeDtypeStruct((lp, D_CONV), F32),
            jax.ShapeDtypeStruct((lp, D_SSD), F32),
            jax.ShapeDtypeStruct((1, D_SSD), F32),
            jax.ShapeDtypeStruct((1, D_SSD), F32),
            jax.ShapeDtypeStruct((1, D_SSD), F32),
        ] + comm.out_shape,
        scratch_shapes=[pltpu.VMEM((gps, D_STATE, GROUP_W), F32)] + comm.scratch,
        compiler_params=_params(("arbitrary", "arbitrary"), side_effects=True),
    )(dmix, ytot, proj, xbc, dt_rep, hprev, a_rep, dsk_rep, wn, tri, *comm.operands)
    return res


def _stack_heads(t, h):
    return jnp.concatenate([t[:, (REP * h + r) * HEAD_DIM:(REP * h + r + 1) * HEAD_DIM] for r in range(REP)], axis=0)


def _band(t2, t1, t0, h):
    sl = slice(h * HEAD_DIM, (h + 1) * HEAD_DIM)
    return jnp.concatenate([t2[:, sl], t1[:, sl], t0[:, sl]], axis=0)


def _attn_probs(s, sink, qc):
    s = s * (HEAD_DIM ** -0.5)
    key_abs = (qc - WINDOW_CHUNKS) * CHUNK + lax.broadcasted_iota(jnp.int32, s.shape, 1)
    s = jnp.where(key_abs >= PAD_LEAD, s, -jnp.inf)
    m = jnp.maximum(jnp.max(s, axis=-1, keepdims=True), sink)
    p = jnp.exp(s - m)
    ps = jnp.exp(sink - m)
    inv = 1.0 / (jnp.sum(p, axis=-1, keepdims=True) + ps)
    return p * inv, ps * inv


ATT_QC_FWD = 6
ATT_QC_BWD = 2


def _kv_specs(width, newest_chunk_of, kc):
    return [pl.BlockSpec((CHUNK, width), functools.partial(lambda j, p: (jnp.maximum(newest_chunk_of(p) - j, 0), 0), j))
            for j in range(kc - 1, -1, -1)]


def _attn_fwd(qr, kr, vb, proj, sink_stack, mix):
    lp = qr.shape[0]
    nc = lp // CHUNK
    qn = _tile(nc, (ATT_QC_FWD, 2, 1))
    kn = WINDOW_CHUNKS + qn
    qrows = qn * CHUNK

    def body(q_ref, *rest):
        k_refs, v_refs = rest[:kn], rest[kn:2 * kn]
        g_ref, sink_ref, _, att_ref, mix_ref = rest[2 * kn:]
        p = pl.program_id(0)
        q = q_ref[...]
        ks = [r[...] for r in k_refs]
        vs = [r[...] for r in v_refs]
        units = [(u, h) for u in range(qn) for h in range(KV_HEADS)]
        s = [_dot(_stack_heads(q[u * CHUNK:(u + 1) * CHUNK, :], h), _band(*ks[u:u + 3], h), NT) for u, h in units]
        vbh = [_band(*vs[u:u + 3], h) for u, h in units]
        pn = [_attn_probs(s[i], sink_ref[h], qn * p + u)[0].astype(BF16) for i, (u, h) in enumerate(units)]
        o = [_dot(pn[i], vbh[i]) for i in range(len(units))]
        att = jnp.concatenate(
            [jnp.concatenate([o[u * KV_HEADS + h][r * CHUNK:(r + 1) * CHUNK, :] for h in range(KV_HEADS) for r in range(REP)],
                             axis=1) for u in range(qn)], axis=0)
        att_ref[...] = att
        g = g_ref[...]
        mix_ref[...] = (att * (g * _sigmoid(g))).astype(BF16)

    newest = lambda p: qn * p + qn - 1
    return pl.pallas_call(
        body, name="attn_fwd", grid=(nc // qn,),
        in_specs=[pl.BlockSpec((qrows, D_ATT), lambda p: (p, 0))] + _kv_specs(D_KV, newest, kn) + _kv_specs(D_KV, newest, kn) + [
            pl.BlockSpec((qrows, D_ATT), lambda p: (p, OFF_G // D_ATT)),
            pl.BlockSpec((KV_HEADS, REP * CHUNK, 1), lambda p: (0, 0, 0)),
            ANY,
        ],
        out_specs=[pl.BlockSpec((qrows, D_ATT), lambda p: (p, 0)),
                   pl.BlockSpec((qrows, D_ATT), lambda p: (p, D_SSD // D_ATT))],
        out_shape=[jax.ShapeDtypeStruct((lp, D_ATT), F32), jax.ShapeDtypeStruct((lp, D_MIX), BF16)],
        input_output_aliases={2 * kn + 3: 1},
        compiler_params=_params(("parallel",)),
    )(qr, *([kr] * kn), *([vb] * kn), proj, sink_stack, mix)


def _attn_bwd(qr, kr, vb, att, proj, dmix, sink_stack):
    lp = qr.shape[0]
    nc = lp // CHUNK
    qn = ATT_QC_BWD
    kn = WINDOW_CHUNKS + qn
    assert nc % qn == 0 and WINDOW_CHUNKS % qn == 0
    steps = nc // qn
    qrows = qn * CHUNK
    wrows = kn * CHUNK

    def body(q_ref, *rest):
        k_refs, v_refs = rest[:kn], rest[kn:2 * kn]
        (att_ref, g_ref, do_ref, sink_ref, dq_ref, dk_ref, dv_ref, dg_ref, dsink_ref, dk_acc, dv_acc) = rest[2 * kn:]
        step = pl.program_id(0)

        @pl.when(step == 0)
        def _():
            dk_acc[...] = jnp.zeros_like(dk_acc)
            dv_acc[...] = jnp.zeros_like(dv_acc)
            dsink_ref[...] = jnp.zeros_like(dsink_ref)

        q = q_ref[...]
        ks = [r[...] for r in k_refs]
        vs = [r[...] for r in v_refs]
        att = att_ref[...]
        g = g_ref[...]
        dog = do_ref[...]
        sg = _sigmoid(g)
        dg_ref[...] = dog * att * (sg * (1.0 + g * (1.0 - sg)))
        do = dog * (g * sg)
        units = [(u, h) for u in range(qn) for h in range(KV_HEADS)]
        n = range(len(units))
        rows = [slice(u * CHUNK, (u + 1) * CHUNK) for u in range(qn)]
        qs = [_stack_heads(q[rows[u], :], h) for u, h in units]
        kb = [_band(*ks[u:u + 3], h) for u, h in units]
        vbh = [_band(*vs[u:u + 3], h) for u, h in units]
        dos = [_stack_heads(do[rows[u], :], h) for u, h in units]
        dosb = [dos[i].astype(BF16) for i in n]
        s = [_dot(qs[i], kb[i], NT) for i in n]
        dp = [_dot(dosb[i], vbh[i], NT) for i in n]
        ds, pnb, dsink = [], [], []
        for i, (u, h) in enumerate(units):
            pn, psink = _attn_probs(s[i], sink_ref[h], qn * jnp.minimum(step, steps - 1) + u)
            delta = jnp.sum(dos[i] * _stack_heads(att[rows[u], :], h), axis=-1, keepdims=True)
            ds.append((pn * (dp[i] - delta)).astype(BF16))
            pnb.append(pn.astype(BF16))
            dsink.append(-(psink * delta))
        dqs = [_dot(ds[i], kb[i]) for i in n]
        dks = [_dot(ds[i], qs[i], TN) for i in n]
        dvs = [_dot(pnb[i], dosb[i], TN) for i in n]
        dq_ref[...] = jnp.concatenate(
            [jnp.concatenate([dqs[u * KV_HEADS + h][r * CHUNK:(r + 1) * CHUNK, :]
                              for h in range(KV_HEADS) for r in range(REP)], axis=1) for u in range(qn)],
            axis=0) * (HEAD_DIM ** -0.5)
        @pl.when(step < steps)
        def _():
            for i, (u, h) in enumerate(units):
                dsink_ref[h] += dsink[i]
            for u in range(qn):
                band = slice(u * CHUNK, u * CHUNK + BAND)
                dk_acc[band, :] += jnp.concatenate(dks[u * KV_HEADS:(u + 1) * KV_HEADS], axis=1) * (HEAD_DIM ** -0.5)
                dv_acc[band, :] += jnp.concatenate(dvs[u * KV_HEADS:(u + 1) * KV_HEADS], axis=1)

        dk_ref[...] = dk_acc[0:qrows, :]
        dv_ref[...] = dv_acc[0:qrows, :]
        for acc in (dk_acc, dv_acc):
            rest_rows = acc[qrows:wrows, :]
            acc[0:wrows - qrows, :] = rest_rows
            acc[wrows - qrows:wrows, :] = jnp.zeros((qrows, D_KV), F32)

    qp = lambda p: jnp.minimum(p, steps - 1)
    newest = lambda p: qn * qp(p) + qn - 1
    qblk = pl.BlockSpec((qrows, D_ATT), lambda p: (qp(p), 0))
    oldest = pl.BlockSpec((qrows, D_KV), lambda p: (jnp.maximum(p - 1, 0), 0))
    return pl.pallas_call(
        body, name="attn_bwd", grid=(steps + 1,),
        in_specs=[qblk] + _kv_specs(D_KV, newest, kn) + _kv_specs(D_KV, newest, kn) + [
            qblk,
            pl.BlockSpec((qrows, D_ATT), lambda p: (qp(p), OFF_G // D_ATT)),
            pl.BlockSpec((qrows, D_ATT), lambda p: (qp(p), D_SSD // D_ATT)),
            pl.BlockSpec((KV_HEADS, REP * CHUNK, 1), lambda p: (0, 0, 0)),
        ],
        out_specs=[qblk, oldest, oldest, qblk, pl.BlockSpec((KV_HEADS, REP * CHUNK, 1), lambda p: (0, 0, 0))],
        out_shape=[
            jax.ShapeDtypeStruct((lp, D_ATT), F32),
            jax.ShapeDtypeStruct((lp, D_KV), F32),
            jax.ShapeDtypeStruct((lp, D_KV), F32),
            jax.ShapeDtypeStruct((lp, D_ATT), F32),
            jax.ShapeDtypeStruct((KV_HEADS, REP * CHUNK, 1), F32),
        ],
        scratch_shapes=[pltpu.VMEM((wrows, D_KV), F32), pltpu.VMEM((wrows, D_KV), F32)],
        compiler_params=_params(("arbitrary",)),
    )(qr, *([kr] * kn), *([vb] * kn), att, proj, dmix, sink_stack)


def _post_loss(out, x2d, target, w):
    lp = out.shape[0]
    nc = lp // CHUNK
    nx = x2d.shape[0] // CHUNK

    k = _tile(nc, (ROW_K, 3, 2, 1))

    def body(o_ref, *rest):
        x_refs, t_refs = rest[:k], rest[k:2 * k]
        w_ref, do_ref, dy_ref, gw_ref, loss_ref = rest[2 * k:]
        i = pl.program_id(0)

        @pl.when(i == 0)
        def _():
            gw_ref[...] = jnp.zeros_like(gw_ref)
            loss_ref[...] = jnp.zeros_like(loss_ref)

        o = o_ref[...]
        w = w_ref[...]
        rstd = lax.rsqrt(jnp.mean(o * o, axis=-1, keepdims=True) + EPS)
        xhat = o * rstd
        x = jnp.concatenate([r[...] for r in x_refs], axis=0)
        t = jnp.concatenate([r[...] for r in t_refs], axis=0)
        chunk = _chunk_of_row(i, k)
        err = (x + xhat * w - t) * ((chunk > 0) & (chunk <= nx)).astype(F32)
        loss_ref[...] += 0.5 * jnp.sum(jnp.mean(err * err, axis=-1, keepdims=True), axis=0, keepdims=True)
        dy = err * (1.0 / D_MODEL)
        dy_ref[...] = dy
        gw_ref[...] += jnp.sum(dy * xhat, axis=0, keepdims=True)
        dxhat = dy * w
        do_ref[...] = (rstd * (dxhat - xhat * jnp.mean(dxhat * xhat, axis=-1, keepdims=True))).astype(BF16)

    row = pl.BlockSpec((k * CHUNK, D_MODEL), lambda i: (i, 0))
    return pl.pallas_call(
        body, name="post_loss", grid=(nc // k,),
        in_specs=[row] + _x_specs(nx, k) + _x_specs(nx, k) + [pl.BlockSpec((1, D_MODEL), lambda i: (0, 0))],
        out_specs=[row, row, pl.BlockSpec((1, D_MODEL), lambda i: (0, 0)), pl.BlockSpec((1, 128), lambda i: (0, 0))],
        out_shape=[
            jax.ShapeDtypeStruct((lp, D_MODEL), BF16),
            jax.ShapeDtypeStruct((lp, D_MODEL), F32),
            jax.ShapeDtypeStruct((1, D_MODEL), F32),
            jax.ShapeDtypeStruct((1, 128), F32),
        ],
        compiler_params=_params(("arbitrary",)),
    )(out, *([x2d] * k), *([target] * k), w)


def _prenorm_bwd(dhn, x2d, h0, dy, w):
    nx = x2d.shape[0] // CHUNK
    k = _tile(nx, (X_K, 4, 2, 1))
    rows = k * CHUNK

    def backward(h, dhn, w):
        rstd = lax.rsqrt(jnp.mean(h * h, axis=-1, keepdims=True) + EPS)
        xhat = h * rstd
        dxhat = dhn * w
        dh = rstd * (dxhat - xhat * jnp.mean(dxhat * xhat, axis=-1, keepdims=True))
        return dh, jnp.sum(dhn * xhat, axis=0, keepdims=True)

    def body(*refs):
        dhn_refs, dy_refs = refs[:k], refs[k:2 * k]
        x_ref, w_ref, gx_ref, gw_ref = refs[2 * k:]

        @pl.when(pl.program_id(0) == 0)
        def _():
            gw_ref[...] = jnp.zeros_like(gw_ref)

        dh, gw = backward(x_ref[...], jnp.concatenate([r[...] for r in dhn_refs], axis=0), w_ref[...])
        gx_ref[...] = dh + jnp.concatenate([r[...] for r in dy_refs], axis=0)
        gw_ref[...] += gw

    def body_meta(dhn_ref, h0_ref, w_ref, d0_ref, gw_ref):
        d0_ref[...], gw_ref[...] = backward(h0_ref[...], dhn_ref[...], w_ref[...])

    chunk_specs = [pl.BlockSpec((CHUNK, D_MODEL), functools.partial(lambda u, i: (k * i + 1 + u, 0), u)) for u in range(k)]
    first = pl.BlockSpec((CHUNK, D_MODEL), lambda i: (0, 0))
    vec = pl.BlockSpec((1, D_MODEL), lambda i: (0, 0))
    wide = pl.BlockSpec((rows, D_MODEL), lambda i: (i, 0))
    gx, gw_x = pl.pallas_call(
        body, name="prenorm_bwd", grid=(nx // k,),
        in_specs=chunk_specs + chunk_specs + [wide, vec],
        out_specs=[wide, vec],
        out_shape=[jax.ShapeDtypeStruct((nx * CHUNK, D_MODEL), F32), jax.ShapeDtypeStruct((1, D_MODEL), F32)],
        compiler_params=_params(("arbitrary",)),
    )(*([dhn] * k), *([dy] * k), x2d, w)
    d0, gw_0 = pl.pallas_call(
        body_meta, name="prenorm_bwd_meta", grid=(1,),
        in_specs=[first, first, vec], out_specs=[first, vec],
        out_shape=[jax.ShapeDtypeStruct((CHUNK, D_MODEL), F32), jax.ShapeDtypeStruct((1, D_MODEL), F32)],
        compiler_params=_params(("arbitrary",)),
    )(dhn, h0, w)
    return gx, d0, gw_x + gw_0


def _adamw(slabs, w, m, v, name):
    rows, cols = w.shape
    tr = _tile(rows, (256, 128, 64, 16, 8))
    c1 = 1.0 - ADAM_B1 ** ADAM_STEP
    c2 = 1.0 - ADAM_B2 ** ADAM_STEP

    def body(s_ref, w_ref, m_ref, v_ref, g_ref, d_ref, mo_ref, vo_ref):
        g = s_ref[0].astype(F32)
        for k in range(1, slabs.shape[0]):
            g = g + s_ref[k].astype(F32)
        w = w_ref[...]
        m = ADAM_B1 * m_ref[...] + (1.0 - ADAM_B1) * g
        v = ADAM_B2 * v_ref[...] + (1.0 - ADAM_B2) * (g * g)
        g_ref[...] = g
        mo_ref[...] = m
        vo_ref[...] = v
        d_ref[...] = -ADAM_LR * ((m / c1) / (jnp.sqrt(v / c2) + ADAM_EPS) + ADAM_WD * w)

    blk = pl.BlockSpec((tr, cols), lambda i: (i, 0))
    shape = jax.ShapeDtypeStruct((rows, cols), F32)
    return pl.pallas_call(
        body, name=name, grid=(rows // tr,),
        in_specs=[pl.BlockSpec((slabs.shape[0], tr, cols), lambda i: (0, i, 0)), blk, blk, blk],
        out_specs=[blk, blk, blk, blk],
        out_shape=[shape, shape, shape, shape],
        compiler_params=_params(("parallel",)),
    )(slabs, w, m, v)


def _perm_xbc(a):
    lead = a.shape[:-1]
    xs = a[..., :D_SSD].reshape(lead + (N_GROUPS, GROUP_W))
    b = a[..., D_SSD:D_SSD + N_GROUPS * D_STATE].reshape(lead + (N_GROUPS, D_STATE))
    c = a[..., D_SSD + N_GROUPS * D_STATE:].reshape(lead + (N_GROUPS, D_STATE))
    return jnp.concatenate([xs, b, c], axis=-1).reshape(lead + (D_CONV,))


def _unperm_xbc(a):
    lead = a.shape[:-1]
    t = a.reshape(lead + (N_GROUPS, XBC_BLK))
    xs = t[..., :GROUP_W].reshape(lead + (D_SSD,))
    b = t[..., GROUP_W:GROUP_W + D_STATE].reshape(lead + (N_GROUPS * D_STATE,))
    c = t[..., GROUP_W + D_STATE:].reshape(lead + (N_GROUPS * D_STATE,))
    return jnp.concatenate([xs, b, c], axis=-1)


R_Z, R_XBC, R_DT, R_Q, R_K, R_V, R_G = 0, 2048, 6144, 6176, 7200, 7456, 7712


def _internal_of_reference():
    ref = np.arange(D_IN_PROJ)
    out = np.empty(D_IN_PROJ, np.int64)
    out[R_Z:R_XBC] = OFF_Z + ref[:D_SSD]
    xs = np.arange(D_SSD)
    out[R_XBC:R_XBC + D_SSD] = OFF_XBC + (xs // GROUP_W) * XBC_BLK + xs % GROUP_W
    bc = np.arange(N_GROUPS * D_STATE)
    out[R_XBC + D_SSD:R_XBC + D_SSD + N_GROUPS * D_STATE] = OFF_XBC + (bc // D_STATE) * XBC_BLK + GROUP_W + bc % D_STATE
    out[R_XBC + D_SSD + N_GROUPS * D_STATE:R_DT] = OFF_XBC + (bc // D_STATE) * XBC_BLK + GROUP_W + D_STATE + bc % D_STATE
    out[R_DT:R_Q] = OFF_DT + np.arange(SSD_HEADS)
    out[R_Q:R_K] = OFF_Q + np.arange(D_ATT)
    out[R_K:R_V] = OFF_K + np.arange(D_KV)
    out[R_V:R_G] = OFF_V + np.arange(D_KV)
    out[R_G:] = OFF_G + np.arange(D_ATT)
    return out


def _runs(src, dst_break):
    runs, lo = [], 0
    for i in range(1, len(src) + 1):
        if i == len(src) or src[i] != src[i - 1] + 1 or dst_break[i] != dst_break[i - 1]:
            runs.append((lo, i))
            lo = i
    return runs


RELAYOUT_ROWS = 256


def _lane_window(ref, lead, c0, n):
    a0 = c0 // 128 * 128
    a1 = min(-(-(c0 + n) // 128) * 128, ref.shape[-1])
    return ref[lead + (slice(None), slice(a0, a1))][:, c0 - a0:c0 - a0 + n]


def _w_in_internal(w_gathered):
    int_of_ref = _internal_of_reference()
    ref_of_int = np.full(NP, -1, np.int64)
    ref_of_int[int_of_ref] = np.arange(D_IN_PROJ)
    shard = np.where(ref_of_int >= 0, ref_of_int // SHARD_IN, -1)
    src = np.where(ref_of_int >= 0, ref_of_int, -10 - 2 * np.arange(NP))
    plan, zeros = [], 0
    for lo, hi in _runs(src, shard):
        if ref_of_int[lo] < 0:
            zeros += hi - lo
            continue
        if zeros:
            plan.append((None, 0, zeros))
            zeros = 0
        plan.append((int(ref_of_int[lo] // SHARD_IN), int(ref_of_int[lo] % SHARD_IN), hi - lo))
    if zeros:
        plan.append((None, 0, zeros))
    tr = RELAYOUT_ROWS

    def body(w_ref, o_ref):
        o_ref[...] = jnp.concatenate(
            [jnp.zeros((tr, n), o_ref.dtype) if s is None else _lane_window(w_ref, (s,), c0, n) for s, c0, n in plan], axis=1)

    return pl.pallas_call(
        body, name="w_in_relayout", grid=(D_MODEL // tr,),
        in_specs=[pl.BlockSpec((N_DEV, tr, SHARD_IN), lambda i: (0, i, 0))],
        out_specs=pl.BlockSpec((tr, NP), lambda i: (i, 0)),
        out_shape=jax.ShapeDtypeStruct((D_MODEL, NP), w_gathered.dtype),
        compiler_params=_params(("parallel",)),
    )(w_gathered)


def _w_in_chip_slabs(dw):
    int_of_ref = _internal_of_reference()
    plan = []
    for s in range(N_DEV):
        cols = int_of_ref[s * SHARD_IN:(s + 1) * SHARD_IN]
        plan.append([(int(cols[lo]), hi - lo) for lo, hi in _runs(cols, np.zeros_like(cols))])
    tr = RELAYOUT_ROWS
    steps = D_MODEL // tr
    pairs = N_DEV // 2

    def body(dw_ref, o_ref, land_ref, send_buf, land_buf, send_sems, recv_sems, load_sems):
        i = pl.program_id(0)
        x, y, c = lax.axis_index("x"), lax.axis_index("y"), lax.axis_index("c")
        slot = i % 2

        def sends(step, sl):
            return [pltpu.make_async_remote_copy(
                src_ref=send_buf.at[sl, 2 * q + (1 - c)], dst_ref=land_ref.at[q, pl.ds(step * tr, tr), :],
                send_sem=send_sems.at[sl, q], recv_sem=recv_sems.at[step, q],
                device_id=(x, y, 1 - c), device_id_type=pl.DeviceIdType.MESH) for q in range(pairs)]

        @pl.when(i < steps)
        def _():
            @pl.when(i >= 2)
            def _():
                for cp in sends(i - 2, slot):
                    cp.wait_send()

            for s in range(N_DEV):
                send_buf[slot, s] = jnp.concatenate([_lane_window(dw_ref, (), c0, n) for c0, n in plan[s]], axis=1)
            for cp in sends(i, slot):
                cp.start()

        @pl.when(i >= 1)
        def _():
            loads = []
            for q, cp in enumerate(sends(i - 1, 1 - slot)):
                cp.wait_recv()
                loads.append(pltpu.make_async_copy(land_ref.at[q, pl.ds((i - 1) * tr, tr), :], land_buf.at[q], load_sems.at[q]))
                loads[q].start()
            for q in range(pairs):
                loads[q].wait()
                o_ref[q] = (send_buf[1 - slot, 2 * q + c].astype(F32) + land_buf[q].astype(F32)).astype(o_ref.dtype)

        @pl.when(i == steps)
        def _():
            for cp in sends(steps - 2, steps % 2) + sends(steps - 1, 1 - steps % 2):
                cp.wait_send()

    last = steps - 1
    return pl.pallas_call(
        body, name="dw_in_relayout", grid=(steps + 1,),
        in_specs=[pl.BlockSpec((tr, NP), lambda i: (jnp.minimum(i, last), 0))],
        out_specs=[pl.BlockSpec((pairs, tr, SHARD_IN), lambda i: (0, jnp.maximum(i - 1, 0), 0)), ANY],
        out_shape=[jax.ShapeDtypeStruct((pairs, D_MODEL, SHARD_IN), dw.dtype),
                   jax.ShapeDtypeStruct((pairs, D_MODEL, SHARD_IN), dw.dtype)],
        scratch_shapes=[pltpu.VMEM((2, N_DEV, tr, SHARD_IN), dw.dtype), pltpu.VMEM((pairs, tr, SHARD_IN), dw.dtype),
                        pltpu.SemaphoreType.DMA((2, pairs)), pltpu.SemaphoreType.DMA((steps, pairs)),
                        pltpu.SemaphoreType.DMA((pairs,))],
        compiler_params=_params(("arbitrary",), side_effects=True),
    )(dw)[0]


def _rep_heads(a):
    return jnp.repeat(a, HEAD_DIM, axis=1)


SMALL = (("norm_pre_w", 2048), ("conv_b", 4096), ("dt_bias", 32), ("a_log", 32), ("d_skip", 32),
         ("ssd_norm_w", 2048), ("attn_sinks", 16), ("norm_post_w", 2048))
SMALL_USED = sum(size for _, size in SMALL)
SMALL_LEN = 10368


def _pack_small(d, loss=None):
    parts = [d[name].reshape(1, size) for name, size in SMALL]
    tail = jnp.zeros((1, SMALL_LEN - SMALL_USED), F32)
    if loss is not None:
        tail = tail.at[0, 0].set(loss)
    return jnp.concatenate(parts + [tail], axis=1)


def _unpack_small(vec):
    out, off = {}, 0
    for name, size in SMALL:
        out[name] = vec[:, off:off + size]
        off += size
    return out


def kernel(x, meta_tokens, norm_pre_w, w_in, conv_w, conv_b, dt_bias, a_log, d_skip, ssd_norm_w, attn_sinks, w_out, norm_post_w, loss_target, m_meta_tokens, m_norm_pre_w, m_w_in, m_conv_w, m_conv_b, m_dt_bias, m_a_log, m_d_skip, m_ssd_norm_w, m_attn_sinks, m_w_out, m_norm_post_w, v_meta_tokens, v_norm_pre_w, v_w_in, v_conv_w, v_conv_b, v_dt_bias, v_a_log, v_d_skip, v_ssd_norm_w, v_attn_sinks, v_w_out, v_norm_post_w):
    seq = x.shape[1]
    lp = seq + 2 * CHUNK
    x2d = x[0]

    w_in_g, conv_w_g, meta_g = _gather_two_level([w_in[0].astype(BF16), conv_w[0], meta_tokens], "gather_w_in")
    w_all = _w_in_internal(w_in_g)
    conv_w_full = _perm_xbc(jnp.transpose(conv_w_g, (1, 0, 2)).reshape(CONV_WIDTH, D_CONV))
    conv_b_int = _perm_xbc(conv_b)
    meta_full = jnp.transpose(meta_g, (1, 0, 2)).reshape(N_META, D_MODEL)
    h0 = jnp.concatenate([jnp.zeros((PAD_LEAD, D_MODEL), F32), meta_full], axis=0)

    pos = (jnp.arange(lp) - PAD_LEAD).astype(F32)
    half = HEAD_DIM // 2
    inv = ROPE_THETA ** (-jnp.arange(half, dtype=F32) / half)
    ang = pos[:, None] * inv[None, :]
    cos_t = jnp.tile(jnp.cos(ang), (1, 4))
    sin_t = jnp.tile(jnp.concatenate([-jnp.sin(ang), jnp.sin(ang)], axis=1), (1, 2))
    head_of_col = np.arange(D_SSD) // HEAD_DIM
    expand = jnp.asarray((np.arange(128)[:, None] == head_of_col[None, :]).astype(np.float32))
    reduce_t = jnp.asarray((head_of_col[:, None] == np.arange(128)[None, :]).astype(np.float32))
    tri = jnp.asarray(np.tril(np.ones((CHUNK, CHUNK), np.float32)))
    a_rep = _rep_heads(-jnp.exp(a_log))
    dsk_rep = _rep_heads(d_skip)
    dt_bias_pad = jnp.pad(dt_bias, ((0, 0), (0, 128 - SSD_HEADS)))
    sink_stack = jnp.repeat(attn_sinks.reshape(KV_HEADS, REP), CHUNK, axis=1).reshape(KV_HEADS, REP * CHUNK, 1)

    hn = _prenorm(x2d, h0, norm_pre_w)
    tm = _tile(lp, (1056, 704, 128, 64))
    proj, w_out_g = _matmul(hn, w_all, tm=tm, tn=1536, tk=D_MODEL, out_dtype=F32, name="in_proj",
                            comm=_Comm([(w_out[0].astype(BF16), "gather")]))
    w_out_full = w_out_g.reshape(D_MIX, D_MODEL)
    qr, kr, vb, dt_rep = _act_fwd(proj, cos_t, sin_t, expand, dt_bias_pad)
    mix, ytot, hprev, xbc = _ssd_fwd(proj, conv_w_full, conv_b_int, dt_rep, a_rep, dsk_rep, ssd_norm_w, tri)
    att, mix = _attn_fwd(qr, kr, vb, proj, sink_stack, mix)
    out = _matmul(mix, w_out_full, tm=tm, tn=1024, tk=D_MIX, out_dtype=F32, name="out_proj")
    dout, dy, g_norm_post, loss_part = _post_loss(out, x2d, loss_target[0], norm_post_w)

    dmix = _matmul(dout, w_out_full, trans_b=True, tm=tm, tn=1024, tk=D_MODEL, out_dtype=F32, name="dmix")
    dw_out = _matmul(mix, dout, trans_a=True, tm=512, tn=1024, tk=lp, out_dtype=BF16, name="dw_out")
    dqr, dkr, dv, dg, dsink_rows = _attn_bwd(qr, kr, vb, att, proj, dmix, sink_stack)
    dproj, dxbc, ddt_part, dd_part, da_part, g_ssd_norm, g_out = _ssd_bwd(
        dmix, ytot, proj, xbc, dt_rep, hprev, a_rep, dsk_rep, ssd_norm_w, tri,
        _Comm([(dw_out.reshape(N_DEV, D_MIX // N_DEV, D_MODEL), "scatter")]))
    dproj, dconv_w_int, dconv_b_int = _conv_bwd(dxbc, proj, conv_w_full, conv_b_int, dproj)
    dproj, ddt_bias = _act_bwd(dqr, dkr, dv, dg, ddt_part, cos_t, sin_t, reduce_t, dproj)
    dw_all = _matmul(hn, dproj, trans_a=True, tm=512, tn=1024, tk=lp, out_dtype=BF16, name="dw_in")
    dw_chip = _w_in_chip_slabs(dw_all)
    dhn, g_in = _matmul(dproj, w_all, trans_b=True, tm=tm, tn=1024, tk=1536, out_dtype=F32, name="dhn",
                        comm=_Comm([(dw_chip, "scatter")], scope="chips"))
    grad_x, dh0, g_norm_pre = _prenorm_bwd(dhn, x2d, h0, dy, norm_pre_w)

    dmeta = dh0[PAD_LEAD:, :]
    dconv_w_ref = _unperm_xbc(dconv_w_int)
    heads = lambda part: part.reshape(SSD_HEADS, HEAD_DIM).sum(axis=1).reshape(1, SSD_HEADS)
    small_local = _pack_small({
        "norm_pre_w": g_norm_pre, "conv_b": _unperm_xbc(dconv_b_int), "dt_bias": ddt_bias[:, :SSD_HEADS],
        "a_log": heads(da_part) * (-jnp.exp(a_log)), "d_skip": heads(dd_part), "ssd_norm_w": g_ssd_norm,
        "attn_sinks": dsink_rows.reshape(Q_HEADS, CHUNK).sum(axis=1).reshape(1, Q_HEADS),
        "norm_post_w": g_norm_post}, loss=loss_part[0, 0])
    g_conv, g_meta, g_small = _exchange(
        [(jnp.transpose(dconv_w_ref.reshape(CONV_WIDTH, N_DEV, D_CONV // N_DEV), (1, 0, 2)), "scatter"),
         (jnp.transpose(dmeta.reshape(N_META, N_DEV, D_MODEL // N_DEV), (1, 0, 2)), "scatter"),
         (small_local, "gather")], "exchange_small")

    res = {}
    res["w_in"] = [o[None] for o in _adamw(g_in, w_in[0], m_w_in[0], v_w_in[0], "adamw_w_in")]
    res["w_out"] = [o[None] for o in _adamw(g_out, w_out[0], m_w_out[0], v_w_out[0], "adamw_w_out")]
    res["conv_w"] = [o[None] for o in _adamw(g_conv, conv_w[0], m_conv_w[0], v_conv_w[0], "adamw_conv_w")]
    res["meta_tokens"] = _adamw(g_meta, meta_tokens, m_meta_tokens, v_meta_tokens, "adamw_meta")
    given = dict(norm_pre_w=(norm_pre_w, m_norm_pre_w, v_norm_pre_w), conv_b=(conv_b, m_conv_b, v_conv_b),
                 dt_bias=(dt_bias, m_dt_bias, v_dt_bias), a_log=(a_log, m_a_log, v_a_log),
                 d_skip=(d_skip, m_d_skip, v_d_skip), ssd_norm_w=(ssd_norm_w, m_ssd_norm_w, v_ssd_norm_w),
                 attn_sinks=(attn_sinks, m_attn_sinks, v_attn_sinks),
                 norm_post_w=(norm_post_w, m_norm_post_w, v_norm_post_w))
    packed = [_pack_small({k: t[j] for k, t in given.items()}) for j in range(3)]
    small_out = _adamw(g_small, packed[0], packed[1], packed[2], "adamw_small")
    small_res = [_unpack_small(r) for r in small_out]
    loss = small_out[0][0, SMALL_USED]

    order = ["meta_tokens", "norm_pre_w", "w_in", "conv_w", "conv_b", "dt_bias", "a_log", "d_skip", "ssd_norm_w",
             "attn_sinks", "w_out", "norm_post_w"]
    outs = []
    for j in range(4):
        for name in order:
            outs.append(res[name][j] if name in res else small_res[j][name])
    return (loss, grad_x[None], *outs)
```

```python
import functools

import numpy as np
import jax
import jax.numpy as jnp
from jax import lax
from jax.experimental import pallas as pl
from jax.experimental.pallas import tpu as pltpu

F32 = jnp.float32
BF16 = jnp.bfloat16
HIGHEST = lax.Precision.HIGHEST

N_DEV = 8
D_MODEL = 2048
CHUNK = 64
N_META = 16
PAD_LEAD = CHUNK - N_META
EPS = 1e-6
N_GROUPS = 8
HEADS_PER_GROUP = 4
HEAD_DIM = 64
GROUP_W = HEADS_PER_GROUP * HEAD_DIM
D_STATE = 128
D_SSD = 2048
D_CONV = 4096
SSD_HEADS = 32
CONV_WIDTH = 4
Q_HEADS = 16
KV_HEADS = 4
REP = 4
D_ATT = 1024
D_KV = 256
WINDOW_CHUNKS = 2
BAND = (WINDOW_CHUNKS + 1) * CHUNK
ROPE_THETA = 10000.0
D_MIX = D_SSD + D_ATT
D_IN_PROJ = 8736
SHARD_IN = D_IN_PROJ // N_DEV

OFF_Z, OFF_XBC, OFF_Q, OFF_G, OFF_K, OFF_V, OFF_DT = 0, 2048, 6144, 7168, 8192, 8448, 8704
NP = 9216
TAIL_W = NP - OFF_Q
XBC_BLK = 512
SSD_GPS = 8
SSD_CPS = 2

ADAM_LR, ADAM_B1, ADAM_B2, ADAM_EPS, ADAM_WD, ADAM_STEP = 0.001, 0.9, 0.999, 1e-08, 0.01, 10

VMEM_LIMIT = 48 * 1024 * 1024

NN = (((1,), (0,)), ((), ()))
NT = (((1,), (1,)), ((), ()))
TN = (((0,), (0,)), ((), ()))
ANY = pl.BlockSpec(memory_space=pl.ANY)


def _dot(a, b, dims=NN, precision=None):
    return lax.dot_general(a, b, dims, precision=precision, preferred_element_type=F32)


def _tile(n, prefs):
    for t in prefs:
        if n % t == 0:
            return t
    return n


def _params(sem, vmem=VMEM_LIMIT, side_effects=False):
    return pltpu.CompilerParams(dimension_semantics=sem, vmem_limit_bytes=vmem, has_side_effects=side_effects)


def _sigmoid(x):
    return 1.0 / (1.0 + jnp.exp(-x))


class _Comm:
    def __init__(self, items, scope="devices"):
        self.items = items
        self.scope = scope
        self.slabs = slabs = N_DEV if scope == "devices" else N_DEV // 2
        self.n = n = len(items)
        self.operands = [arr for arr, _ in items]
        self.in_specs = [ANY] * n
        self.out_specs = [ANY] * n
        self.out_shape = [jax.ShapeDtypeStruct((slabs,) + tuple(arr.shape) if kind == "gather" else tuple(arr.shape),
                                               arr.dtype) for arr, kind in items]
        self.scratch = [pltpu.SemaphoreType.DMA((n, slabs - 1)), pltpu.SemaphoreType.DMA((n, slabs - 1)),
                        pltpu.SemaphoreType.DMA((n,))]

    def _places(self):
        pos = (lax.axis_index("x"), lax.axis_index("y"), lax.axis_index("c"))
        if self.scope == "devices":
            index = lambda p: 4 * p[0] + 2 * p[1] + p[2]
            masks = range(1, N_DEV)
        else:
            index = lambda p: 2 * p[0] + p[1]
            masks = (2, 4, 6)
        peers = []
        for k in masks:
            p = tuple(1 - pos[b] if (k >> (2 - b)) & 1 else pos[b] for b in range(3))
            peers.append((p, index(p)))
        return index(pos), peers

    def _copies(self, ins, outs, sems, landed):
        send_sems, recv_sems, local_sems = sems
        me, peers = self._places()
        local, remote = [], []
        for a, (_, kind) in enumerate(self.items):
            own = ins[a] if kind == "gather" else ins[a].at[me]
            local.append(pltpu.make_async_copy(own, outs[a].at[me], local_sems.at[a]))
            for k, (p, pid) in enumerate(peers):
                remote.append(pltpu.make_async_remote_copy(
                    src_ref=ins[a] if kind == "gather" else ins[a].at[pid],
                    dst_ref=outs[a].at[pid if landed else me],
                    send_sem=send_sems.at[a, k], recv_sem=recv_sems.at[a, k],
                    device_id=p, device_id_type=pl.DeviceIdType.MESH))
        return local, remote

    def start(self, ins, outs, sems):
        local, remote = self._copies(ins, outs, sems, landed=False)
        for cp in local + remote:
            cp.start()

    def wait(self, ins, outs, sems):
        local, remote = self._copies(ins, outs, sems, landed=True)
        for cp in remote + local:
            cp.wait()


def _exchange(items, name):
    comm = _Comm(items)
    n = comm.n

    def body(*refs):
        ins, outs, sems = refs[:n], refs[n:2 * n], refs[2 * n:]
        comm.start(ins, outs, sems)
        comm.wait(ins, outs, sems)

    return pl.pallas_call(
        body, name=name, in_specs=comm.in_specs, out_specs=comm.out_specs, out_shape=comm.out_shape,
        scratch_shapes=comm.scratch, compiler_params=pltpu.CompilerParams(has_side_effects=True),
    )(*comm.operands)


class _GatherTwoLevel:
    def __init__(self, arrays):
        self.arrays = arrays
        self.n = n = len(arrays)
        self.operands = list(arrays)
        self.in_specs = [ANY] * n
        self.out_specs = [ANY] * n
        self.out_shape = [jax.ShapeDtypeStruct((N_DEV,) + tuple(a.shape), a.dtype) for a in arrays]
        self.scratch = [pltpu.SemaphoreType.DMA((n, N_DEV - 1)), pltpu.SemaphoreType.DMA((n, N_DEV - 1)),
                        pltpu.SemaphoreType.DMA((n,))]

    def _plan(self, ins, outs, sems):
        send_sems, recv_sems, local_sems = sems
        x, y, c = lax.axis_index("x"), lax.axis_index("y"), lax.axis_index("c")
        me, sibling = (x, y, c), (x, y, 1 - c)
        chips = [(1 - x, y), (x, 1 - y), (1 - x, 1 - y)]

        def slab(a, place):
            return outs[a].at[4 * place[0] + 2 * place[1] + place[2]]

        def copy(a, k, block, to, src=None):
            return pltpu.make_async_remote_copy(
                src_ref=slab(a, block) if src is None else src, dst_ref=slab(a, block),
                send_sem=send_sems.at[a, k], recv_sem=recv_sems.at[a, k],
                device_id=to, device_id_type=pl.DeviceIdType.MESH)

        own, mine = [], []
        for a in range(self.n):
            mine.append(pltpu.make_async_copy(ins[a], slab(a, me), local_sems.at[a]))
            own.append(copy(a, 0, me, sibling, src=ins[a]))
            own += [copy(a, 1 + j, me, (*chip, c), src=ins[a]) for j, chip in enumerate(chips)]
        return me, sibling, chips, c, copy, own, mine

    def start(self, ins, outs, sems):
        _, _, _, _, _, own, mine = self._plan(ins, outs, sems)
        for cp in mine + own:
            cp.start()

    def wait(self, ins, outs, sems):
        me, sibling, chips, c, copy, own, mine = self._plan(ins, outs, sems)
        forwards = []
        for j, chip in enumerate(chips):
            for a in range(self.n):
                copy(a, 1 + j, (*chip, c), me).wait_recv()
                fwd = copy(a, 4 + j, (*chip, c), sibling)
                fwd.start()
                forwards.append(fwd)
        for a in range(self.n):
            copy(a, 0, sibling, me).wait_recv()
            for j, chip in enumerate(chips):
                copy(a, 4 + j, (*chip, 1 - c), me).wait_recv()
        for cp in own + forwards:
            cp.wait_send()
        for loc in mine:
            loc.wait()


def _matmul(a, b, *, tm, tn, tk, out_dtype, name, trans_a=False, trans_b=False, comm=None):
    m, k = (a.shape[1], a.shape[0]) if trans_a else a.shape
    n = b.shape[0] if trans_b else b.shape[1]
    nk = k // tk
    dims = TN if trans_a else (NT if trans_b else NN)
    assert not (trans_a and trans_b)
    nc = comm.n if comm else 0
    grid = (m // tm, n // tn, nk)

    def body(*refs):
        a_ref, b_ref = refs[:2]
        cin = refs[2:2 + nc]
        o_ref = refs[2 + nc]
        cout = refs[3 + nc:3 + 2 * nc]
        scratch = refs[3 + 2 * nc:]
        sems = scratch[len(scratch) - 3:] if comm else None
        i, j, kk = pl.program_id(0), pl.program_id(1), pl.program_id(2)
        if comm:
            @pl.when((i == 0) & (j == 0) & (kk == 0))
            def _():
                comm.start(cin, cout, sems)

        if nk == 1:
            o_ref[...] = _dot(a_ref[...], b_ref[...], dims).astype(out_dtype)
        else:
            acc_ref = scratch[0]

            @pl.when(kk == 0)
            def _():
                acc_ref[...] = jnp.zeros_like(acc_ref)

            acc_ref[...] += _dot(a_ref[...], b_ref[...], dims)

            @pl.when(kk == nk - 1)
            def _():
                o_ref[...] = acc_ref[...].astype(out_dtype)

        if comm:
            @pl.when((i == grid[0] - 1) & (j == grid[1] - 1) & (kk == nk - 1))
            def _():
                comm.wait(cin, cout, sems)

    a_spec = (pl.BlockSpec((tk, tm), lambda i, j, kk: (kk, i)) if trans_a
              else pl.BlockSpec((tm, tk), lambda i, j, kk: (i, kk)))
    b_spec = (pl.BlockSpec((tn, tk), lambda i, j, kk: (j, kk)) if trans_b
              else pl.BlockSpec((tk, tn), lambda i, j, kk: (kk, j)))
    sem = ("arbitrary",) * 3 if comm else ("parallel", "parallel", "arbitrary")
    res = pl.pallas_call(
        body, name=name, grid=grid,
        in_specs=[a_spec, b_spec] + (comm.in_specs if comm else []),
        out_specs=[pl.BlockSpec((tm, tn), lambda i, j, kk: (i, j))] + (comm.out_specs if comm else []),
        out_shape=[jax.ShapeDtypeStruct((m, n), out_dtype)] + (comm.out_shape if comm else []),
        scratch_shapes=([] if nk == 1 else [pltpu.VMEM((tm, tn), F32)]) + (comm.scratch if comm else []),
        compiler_params=_params(sem, side_effects=bool(comm)),
    )(a, b, *(comm.operands if comm else []))
    return res if comm else res[0]


ROW_K = 6
X_K = 8


def _x_specs(nx, k):
    return [pl.BlockSpec((CHUNK, D_MODEL), functools.partial(lambda u, i: (jnp.clip(k * i + u - 1, 0, nx - 1), 0), u))
            for u in range(k)]


def _chunk_of_row(i, k):
    return k * i + lax.broadcasted_iota(jnp.int32, (k * CHUNK, 1), 0) // CHUNK


def _prenorm(x2d, h0, w):
    nx = x2d.shape[0] // CHUNK
    nc = nx + 2
    k = _tile(nc, (ROW_K, 3, 2, 1))

    def body(*refs):
        x_refs = refs[:k]
        h0_ref, w_ref, o_ref = refs[k:]
        i = pl.program_id(0)
        x = jnp.concatenate([r[...] for r in x_refs], axis=0)
        head = jnp.concatenate([h0_ref[...], x[CHUNK:, :]], axis=0)
        chunk = _chunk_of_row(i, k)
        h = jnp.where(i == 0, head, x) * (chunk <= nx).astype(F32)
        rstd = lax.rsqrt(jnp.mean(h * h, axis=-1, keepdims=True) + EPS)
        o_ref[...] = (h * rstd * w_ref[...]).astype(BF16)

    rows = k * CHUNK
    return pl.pallas_call(
        body, name="prenorm", grid=(nc // k,),
        in_specs=_x_specs(nx, k) + [pl.BlockSpec((CHUNK, D_MODEL), lambda i: (0, 0)), pl.BlockSpec((1, D_MODEL), lambda i: (0, 0))],
        out_specs=pl.BlockSpec((rows, D_MODEL), lambda i: (i, 0)),
        out_shape=jax.ShapeDtypeStruct((nc * CHUNK, D_MODEL), BF16),
        compiler_params=_params(("parallel",)),
    )(*([x2d] * k), h0, w)


CONV_COLS = 512
HALO = 8


def _conv_pre(ext, w, b):
    taps = [ext[HALO:, :]] + [pltpu.roll(ext, j, 0)[HALO:, :] for j in range(1, CONV_WIDTH)]
    acc = b + w[3:4, :] * taps[0]
    for j in range(1, CONV_WIDTH):
        acc = acc + w[3 - j:4 - j, :] * taps[j]
    return acc, taps


def _conv_bwd(dxbc, proj, conv_w, conv_b, dproj):
    lp = proj.shape[0]
    t = _tile(lp, (704, 384, 128, 64))
    hb = t // HALO
    nt = lp // t
    c0 = OFF_XBC // CONV_COLS

    def body(dx_ref, dxn_ref, u_ref, up_ref, un_ref, w_ref, b_ref, _, du_ref, dw_ref, db_ref):
        i = pl.program_id(1)
        w = w_ref[...]
        up = up_ref[...] * (i > 0).astype(F32)
        ext = jnp.concatenate([up, u_ref[...], un_ref[...]], axis=0)
        pre, taps = _conv_pre(ext, w, b_ref[...])
        dxn = dxn_ref[...] * (i < nt - 1).astype(F32)
        dxe = jnp.concatenate([dx_ref[...], dxn], axis=0)
        sg = _sigmoid(pre)
        dpre = dxe * sg * (1.0 + pre * (1.0 - sg))
        du = w[3:4, :] * dpre[:t, :]
        for j in range(1, CONV_WIDTH):
            du = du + w[3 - j:4 - j, :] * pltpu.roll(dpre, t + HALO - j, 0)[:t, :]
        du_ref[...] = du.astype(BF16)

        @pl.when(i == 0)
        def _():
            dw_ref[...] = jnp.zeros_like(dw_ref)
            db_ref[...] = jnp.zeros_like(db_ref)

        dp = dpre[:t, :]
        db_ref[...] += jnp.sum(dp, axis=0, keepdims=True)
        for j in range(CONV_WIDTH):
            dw_ref[3 - j:4 - j, :] += jnp.sum(dp * taps[j][:t, :], axis=0, keepdims=True)

    nxt = lambda i: jnp.minimum((i + 1) * hb, lp // HALO - 1)
    return pl.pallas_call(
        body, name="conv_bwd", grid=(D_CONV // CONV_COLS, nt),
        in_specs=[
            pl.BlockSpec((t, CONV_COLS), lambda j, i: (i, j)),
            pl.BlockSpec((HALO, CONV_COLS), lambda j, i: (nxt(i), j)),
            pl.BlockSpec((t, CONV_COLS), lambda j, i: (i, c0 + j)),
            pl.BlockSpec((HALO, CONV_COLS), lambda j, i: (jnp.maximum(i * hb - 1, 0), c0 + j)),
            pl.BlockSpec((HALO, CONV_COLS), lambda j, i: (nxt(i), c0 + j)),
            pl.BlockSpec((CONV_WIDTH, CONV_COLS), lambda j, i: (0, j)),
            pl.BlockSpec((1, CONV_COLS), lambda j, i: (0, j)),
            ANY,
        ],
        out_specs=[
            pl.BlockSpec((t, CONV_COLS), lambda j, i: (i, c0 + j)),
            pl.BlockSpec((CONV_WIDTH, CONV_COLS), lambda j, i: (0, j)),
            pl.BlockSpec((1, CONV_COLS), lambda j, i: (0, j)),
        ],
        out_shape=[
            jax.ShapeDtypeStruct((lp, NP), BF16),
            jax.ShapeDtypeStruct((CONV_WIDTH, D_CONV), F32),
            jax.ShapeDtypeStruct((1, D_CONV), F32),
        ],
        input_output_aliases={7: 0},
        compiler_params=_params(("parallel", "arbitrary")),
    )(dxbc, dxbc, proj, proj, proj, conv_w, conv_b, dproj)


def _swap_halves(t):
    w = t.shape[-1]
    lane = lax.broadcasted_iota(jnp.int32, t.shape, 1)
    return jnp.where((lane % HEAD_DIM) < HEAD_DIM // 2, pltpu.roll(t, w - HEAD_DIM // 2, 1),
                     pltpu.roll(t, HEAD_DIM // 2, 1))


def _act_fwd(proj, cos_t, sin_t, expand, dt_bias_pad):
    lp = proj.shape[0]
    t = _tile(lp, (384, 128, 64))

    def body(q_ref, k_ref, v_ref, dt_ref, cos_ref, sin_ref, ex_ref, bias_ref, qo_ref, ko_ref, vo_ref, dto_ref):
        i = pl.program_id(0)
        cos = cos_ref[...]
        sin = sin_ref[...]
        q = q_ref[...]
        qo_ref[...] = (q * jnp.tile(cos, (1, D_ATT // 128)) + _swap_halves(q) * jnp.tile(sin, (1, D_ATT // 128))).astype(BF16)
        k = k_ref[...]
        ko_ref[...] = (k * jnp.tile(cos, (1, D_KV // 128)) + _swap_halves(k) * jnp.tile(sin, (1, D_KV // 128))).astype(BF16)
        vo_ref[...] = v_ref[...].astype(BF16)
        raw = dt_ref[...] + bias_ref[...]
        sp = jnp.maximum(raw, 0.0) + jnp.log1p(jnp.exp(-jnp.abs(raw)))
        row = i * t + lax.broadcasted_iota(jnp.int32, sp.shape, 0)
        dto_ref[...] = _dot(jnp.where(row >= PAD_LEAD, sp, 0.0), ex_ref[...], NN, HIGHEST)

    return pl.pallas_call(
        body, name="act_fwd", grid=(lp // t,),
        in_specs=[
            pl.BlockSpec((t, D_ATT), lambda i: (i, OFF_Q // D_ATT)),
            pl.BlockSpec((t, D_KV), lambda i: (i, OFF_K // D_KV)),
            pl.BlockSpec((t, D_KV), lambda i: (i, OFF_V // D_KV)),
            pl.BlockSpec((t, 128), lambda i: (i, OFF_DT // 128)),
            pl.BlockSpec((t, 128), lambda i: (i, 0)),
            pl.BlockSpec((t, 128), lambda i: (i, 0)),
            pl.BlockSpec((128, D_SSD), lambda i: (0, 0)),
            pl.BlockSpec((1, 128), lambda i: (0, 0)),
        ],
        out_specs=[
            pl.BlockSpec((t, D_ATT), lambda i: (i, 0)),
            pl.BlockSpec((t, D_KV), lambda i: (i, 0)),
            pl.BlockSpec((t, D_KV), lambda i: (i, 0)),
            pl.BlockSpec((t, D_SSD), lambda i: (i, 0)),
        ],
        out_shape=[
            jax.ShapeDtypeStruct((lp, D_ATT), BF16),
            jax.ShapeDtypeStruct((lp, D_KV), BF16),
            jax.ShapeDtypeStruct((lp, D_KV), BF16),
            jax.ShapeDtypeStruct((lp, D_SSD), F32),
        ],
        compiler_params=_params(("parallel",)),
    )(proj, proj, proj, proj, cos_t, sin_t, expand, dt_bias_pad)


def _act_bwd(dqr, dkr, dv, dg, ddt_part, cos_t, sin_t, reduce_t, dproj):
    lp = dqr.shape[0]
    t = _tile(lp, (384, 128, 64))

    def body(dq_ref, dk_ref, dv_ref, dg_ref, ddt_ref, cos_ref, sin_ref, red_ref, _, o_ref, db_ref):
        i = pl.program_id(0)
        cos = cos_ref[...]
        sin = sin_ref[...]
        dq = dq_ref[...]
        dq = dq * jnp.tile(cos, (1, D_ATT // 128)) + _swap_halves(dq * jnp.tile(sin, (1, D_ATT // 128)))
        dk = dk_ref[...]
        dk = dk * jnp.tile(cos, (1, D_KV // 128)) + _swap_halves(dk * jnp.tile(sin, (1, D_KV // 128)))
        ddt = _dot(ddt_ref[...], red_ref[...], NN, HIGHEST)
        o_ref[...] = jnp.concatenate(
            [dq.astype(BF16), dg_ref[...].astype(BF16), dk.astype(BF16), dv_ref[...].astype(BF16), ddt.astype(BF16),
             jnp.zeros((t, NP - OFF_DT - 128), BF16)], axis=1)

        @pl.when(i == 0)
        def _():
            db_ref[...] = jnp.zeros_like(db_ref)

        db_ref[...] += jnp.sum(ddt, axis=0, keepdims=True)

    return pl.pallas_call(
        body, name="act_bwd", grid=(lp // t,),
        in_specs=[
            pl.BlockSpec((t, D_ATT), lambda i: (i, 0)),
            pl.BlockSpec((t, D_KV), lambda i: (i, 0)),
            pl.BlockSpec((t, D_KV), lambda i: (i, 0)),
            pl.BlockSpec((t, D_ATT), lambda i: (i, 0)),
            pl.BlockSpec((t, D_SSD), lambda i: (i, 0)),
            pl.BlockSpec((t, 128), lambda i: (i, 0)),
            pl.BlockSpec((t, 128), lambda i: (i, 0)),
            pl.BlockSpec((D_SSD, 128), lambda i: (0, 0)),
            ANY,
        ],
        out_specs=[pl.BlockSpec((t, TAIL_W), lambda i: (i, OFF_Q // TAIL_W)), pl.BlockSpec((1, 128), lambda i: (0, 0))],
        out_shape=[jax.ShapeDtypeStruct((lp, NP), BF16), jax.ShapeDtypeStruct((1, 128), F32)],
        input_output_aliases={8: 0},
        compiler_params=_params(("arbitrary",)),
    )(dqr, dkr, dv, dg, ddt_part, cos_t, sin_t, reduce_t, dproj)


def _cs_row(cs):
    row = lax.broadcasted_iota(jnp.int32, cs.shape, 0)
    lane = lax.broadcasted_iota(jnp.int32, cs.shape, 1)
    return jnp.sum(jnp.where(row == lane % HEAD_DIM, cs, 0.0), axis=0, keepdims=True)


def _ssd_fwd(proj, conv_w, conv_b, dt_rep, a_rep, dsk_rep, wn, tri):
    lp = proj.shape[0]
    nc = lp // CHUNK
    assert SSD_GPS == N_GROUPS
    gw = SSD_GPS * GROUP_W
    cps = _tile(nc, (SSD_CPS, 1))

    def body(rawa_ref, rawb_ref, cw_ref, cb_ref, dt_ref, z_ref, a_ref, dsk_ref, wn_ref, tri_ref,
             yn_ref, ytot_ref, hprev_ref, xbc_ref, h_scr, tail_scr):
        @pl.when(pl.program_id(1) == 0)
        def _():
            h_scr[...] = jnp.zeros_like(h_scr)
            tail_scr[...] = jnp.zeros_like(tail_scr)

        for sub in range(cps):
            rs = slice(sub * CHUNK, (sub + 1) * CHUNK)
            raw = jnp.concatenate([rawa_ref[rs, :], rawb_ref[rs, :]], axis=1)
            pre, _ = _conv_pre(jnp.concatenate([tail_scr[...], raw], axis=0), cw_ref[...], cb_ref[...])
            tail_scr[...] = raw[CHUNK - HALO:, :]
            xbc_ref[rs, :] = pre * _sigmoid(pre)

        G = range(SSD_GPS)
        colsl = [slice(gi * GROUP_W, (gi + 1) * GROUP_W) for gi in G]
        rows4 = lax.broadcasted_iota(jnp.int32, (GROUP_W, GROUP_W), 0) // HEAD_DIM
        cols4 = lax.broadcasted_iota(jnp.int32, (GROUP_W, GROUP_W), 1) // HEAD_DIM
        lrow = lax.broadcasted_iota(jnp.int32, (CHUNK, GROUP_W), 0)
        lcol = lax.broadcasted_iota(jnp.int32, (CHUNK, GROUP_W), 1) % HEAD_DIM

        def one_chunk(sub):
            rs = slice(sub * CHUNK, (sub + 1) * CHUNK)
            xbc = [xbc_ref[rs, gi * XBC_BLK:(gi + 1) * XBC_BLK] for gi in G]
            dt = [dt_ref[rs, colsl[gi]] for gi in G]
            xs = [xbc[gi][:, :GROUP_W] for gi in G]
            b = [xbc[gi][:, GROUP_W:GROUP_W + D_STATE].astype(BF16) for gi in G]
            c = [xbc[gi][:, GROUP_W + D_STATE:].astype(BF16) for gi in G]
            hprev = [h_scr[gi] for gi in G]
            cs = [_dot(tri_ref[...], dt[gi] * a_ref[:, colsl[gi]], NN, HIGHEST) for gi in G]
            yoff = [_dot(c[gi], hprev[gi].astype(BF16)) for gi in G]
            cs_t = [_cs_row(cs[gi]) for gi in G]
            xdt = [xs[gi] * dt[gi] for gi in G]
            cs_last = [cs[gi][CHUNK - 1:CHUNK, :] for gi in G]
            st = [_dot(b[gi], (xdt[gi] * jnp.exp(cs_last[gi] - cs[gi])).astype(BF16), TN) for gi in G]
            cb4 = [_dot(c[gi], jnp.concatenate([b[gi]] * HEADS_PER_GROUP, axis=0), NT) for gi in G]
            m = [(cb4[gi] * jnp.exp(jnp.where(lrow >= lcol, cs[gi] - cs_t[gi], -jnp.inf))).astype(BF16) for gi in G]
            xbd = [jnp.where(rows4 == cols4, jnp.concatenate([xdt[gi].astype(BF16)] * HEADS_PER_GROUP, axis=0), 0.0)
                   for gi in G]
            ydiag = [_dot(m[gi], xbd[gi]) for gi in G]
            for gi in G:
                cols = colsl[gi]
                ytot = ydiag[gi] + yoff[gi] * jnp.exp(cs[gi]) + dsk_ref[:, cols] * xs[gi]
                z = z_ref[rs, cols]
                gz = ytot * (z * _sigmoid(z))
                rstd = lax.rsqrt(jnp.mean(gz * gz, axis=-1, keepdims=True) + EPS)
                hprev_ref[sub, gi] = hprev[gi]
                h_scr[gi] = hprev[gi] * jnp.exp(cs_last[gi]) + st[gi]
                ytot_ref[rs, cols] = ytot
                yn_ref[rs, cols] = (gz * rstd * wn_ref[:, cols]).astype(BF16)

        for sub in range(cps):
            one_chunk(sub)

    vec = pl.BlockSpec((1, gw), lambda g, c: (0, g))
    blk = pl.BlockSpec((cps * CHUNK, gw), lambda g, c: (c, g))
    half = D_CONV // 2
    return pl.pallas_call(
        body, name="ssd_fwd", grid=(N_GROUPS // SSD_GPS, nc // cps),
        in_specs=[
            pl.BlockSpec((cps * CHUNK, half), lambda g, c: (c, OFF_XBC // half)),
            pl.BlockSpec((cps * CHUNK, half), lambda g, c: (c, OFF_XBC // half + 1)),
            pl.BlockSpec((CONV_WIDTH, D_CONV), lambda g, c: (0, 0)),
            pl.BlockSpec((1, D_CONV), lambda g, c: (0, 0)),
            blk, blk, vec, vec, vec,
            pl.BlockSpec((CHUNK, CHUNK), lambda g, c: (0, 0)),
        ],
        out_specs=[blk, blk, pl.BlockSpec((cps, SSD_GPS, D_STATE, GROUP_W), lambda g, c: (c, g, 0, 0)),
                   pl.BlockSpec((cps * CHUNK, D_CONV), lambda g, c: (c, 0))],
        out_shape=[
            jax.ShapeDtypeStruct((lp, D_MIX), BF16),
            jax.ShapeDtypeStruct((lp, D_SSD), F32),
            jax.ShapeDtypeStruct((nc, N_GROUPS, D_STATE, GROUP_W), F32),
            jax.ShapeDtypeStruct((lp, D_CONV), F32),
        ],
        scratch_shapes=[pltpu.VMEM((SSD_GPS, D_STATE, GROUP_W), F32), pltpu.VMEM((HALO, D_CONV), F32)],
        compiler_params=_params(("arbitrary", "arbitrary")),
    )(proj, proj, conv_w, conv_b, dt_rep, proj, a_rep, dsk_rep, wn, tri)


def _ssd_bwd(dmix, ytot, proj, xbc, dt_rep, hprev, a_rep, dsk_rep, wn, tri, comm):
    lp = xbc.shape[0]
    nc = lp // CHUNK
    gps = SSD_GPS
    gw = gps * GROUP_W
    cps = _tile(nc, (SSD_CPS, 1))
    ncm = comm.n
    n_in, n_out = 10, 6
    grid = (N_GROUPS // gps, nc // cps)

    def all_groups(refs):
        (dyn_ref, ytot_ref, z_ref, xbc_ref, dt_ref, hprev_ref, a_ref, dsk_ref, wn_ref, tri_ref,
         dz_ref, dxbc_ref, ddt_ref, dd_ref, da_ref, dwn_ref, dh_scr) = refs
        G = range(gps)
        H = range(HEADS_PER_GROUP)
        cl = [slice(gi * GROUP_W, (gi + 1) * GROUP_W) for gi in G]
        hl = [slice(r * HEAD_DIM, (r + 1) * HEAD_DIM) for r in H]
        tri = tri_ref[...]
        xbc = [xbc_ref[:, gi * XBC_BLK:(gi + 1) * XBC_BLK] for gi in G]
        dt = [dt_ref[:, cl[gi]] for gi in G]
        a = [a_ref[:, cl[gi]] for gi in G]
        xs = [xbc[gi][:, :GROUP_W] for gi in G]
        bbf = [xbc[gi][:, GROUP_W:GROUP_W + D_STATE].astype(BF16) for gi in G]
        cbf = [xbc[gi][:, GROUP_W + D_STATE:].astype(BF16) for gi in G]
        hprev = [hprev_ref[gi] for gi in G]
        hbf = [hprev[gi].astype(BF16) for gi in G]
        dhn = [dh_scr[gi] for gi in G]
        dhnb = [dhn[gi].astype(BF16) for gi in G]
        cs = [_dot(tri, dt[gi] * a[gi], NN, HIGHEST) for gi in G]
        g = [_dot(cbf[gi], hbf[gi]) for gi in G]
        dxw = [_dot(bbf[gi], dhnb[gi]) for gi in G]
        cs_t = [_cs_row(cs[gi]) for gi in G]
        dy = []
        for gi in G:
            ytot = ytot_ref[:, cl[gi]]
            z = z_ref[:, cl[gi]]
            dyn = dyn_ref[:, cl[gi]]
            sz = _sigmoid(z)
            silu_z = z * sz
            gz = ytot * silu_z
            rstd = lax.rsqrt(jnp.mean(gz * gz, axis=-1, keepdims=True) + EPS)
            xhat = gz * rstd
            dwn_ref[:, cl[gi]] += jnp.sum(dyn * xhat, axis=0, keepdims=True)
            dxhat = dyn * wn_ref[:, cl[gi]]
            dgz = rstd * (dxhat - xhat * jnp.mean(dxhat * xhat, axis=-1, keepdims=True))
            dy.append(dgz * silu_z)
            dz_ref[:, cl[gi]] = (dgz * ytot * (sz * (1.0 + z * (1.0 - sz)))).astype(BF16)
            dd_ref[:, cl[gi]] += jnp.sum(dy[gi] * xs[gi], axis=0, keepdims=True)
        xdt = [xs[gi] * dt[gi] for gi in G]
        e = [jnp.exp(cs[gi]) for gi in G]
        cs_last = [cs[gi][CHUNK - 1:CHUNK, :] for gi in G]
        dte = [jnp.exp(cs_last[gi] - cs[gi]) for gi in G]
        cd = [jnp.exp(cs_last[gi]) for gi in G]
        dgb = [(dy[gi] * e[gi]).astype(BF16) for gi in G]
        dyb = [dy[gi].astype(BF16) for gi in G]
        xdtb = [xdt[gi].astype(BF16) for gi in G]
        dc = [_dot(dgb[gi], hbf[gi], NT) for gi in G]
        dhprev = [_dot(cbf[gi], dgb[gi], TN) for gi in G]
        db = [_dot((xdt[gi] * dte[gi]).astype(BF16), dhnb[gi], NT) for gi in G]
        row = lax.broadcasted_iota(jnp.int32, (CHUNK, CHUNK), 0)
        causal = row >= lax.broadcasted_iota(jnp.int32, (CHUNK, CHUNK), 1)
        cb = [_dot(cbf[gi], bbf[gi], NT) for gi in G]
        dm = [[_dot(dyb[gi][:, hl[r]], xdtb[gi][:, hl[r]], NT) for r in H] for gi in G]
        mb, dseg, dcbb = [], [], []
        for gi in G:
            mb.append([])
            dseg.append([])
            dcb = None
            for r in H:
                seg = cs[gi][:, r * HEAD_DIM:r * HEAD_DIM + 1] - cs_t[gi][:, hl[r]]
                lm = jnp.exp(jnp.where(causal, seg, -jnp.inf))
                m = cb[gi] * lm
                mb[gi].append(m.astype(BF16))
                dseg[gi].append(dm[gi][r] * m)
                dcb = dm[gi][r] * lm if r == 0 else dcb + dm[gi][r] * lm
            dcbb.append(dcb.astype(BF16))
        dxdt_diag = [[_dot(mb[gi][r], dyb[gi][:, hl[r]], TN) for r in H] for gi in G]
        ones = jnp.ones((CHUNK, HEAD_DIM), F32)
        colsum = [[_dot(dseg[gi][r], ones, TN, HIGHEST) for r in H] for gi in G]
        dc2 = [_dot(dcbb[gi], bbf[gi]) for gi in G]
        db2 = [_dot(dcbb[gi], cbf[gi], TN) for gi in G]
        dcs = []
        for gi in G:
            t_dte = dxw[gi] * xdt[gi] * dte[gi]
            dcs_last = (jnp.sum(dhn[gi] * hprev[gi], axis=0, keepdims=True) * cd[gi]
                        + jnp.sum(t_dte, axis=0, keepdims=True))
            diag = jnp.concatenate(
                [(jnp.sum(dseg[gi][r], axis=1, keepdims=True) - colsum[gi][r]) * (1.0 / HEAD_DIM) for r in H], axis=1)
            d = dy[gi] * g[gi] * e[gi] - t_dte + diag
            row = lax.broadcasted_iota(jnp.int32, d.shape, 0)
            dcs.append(d + jnp.where(row == CHUNK - 1, dcs_last, 0.0))
        dda = [_dot(tri, dcs[gi], TN, HIGHEST) for gi in G]
        for gi in G:
            dxdt = dxw[gi] * dte[gi] + jnp.concatenate(dxdt_diag[gi], axis=1)
            da_ref[:, cl[gi]] += jnp.sum(dda[gi] * dt[gi], axis=0, keepdims=True)
            ddt = dda[gi] * a[gi] + dxdt * xs[gi]
            dxs = dsk_ref[:, cl[gi]] * dy[gi] + dxdt * dt[gi]
            ddt_ref[:, cl[gi]] = ddt * (1.0 - jnp.exp(-dt[gi]))
            dxbc_ref[:, gi * XBC_BLK:(gi + 1) * XBC_BLK] = jnp.concatenate(
                [dxs, db[gi] + db2[gi], dc[gi] + dc2[gi]], axis=1)
            dh_scr[gi] = dhprev[gi] + dhn[gi] * cd[gi]

    def body(*refs):
        ins = refs[:n_in]
        cin = refs[n_in:n_in + ncm]
        outs = refs[n_in + ncm:n_in + ncm + n_out]
        cout = refs[n_in + ncm + n_out:n_in + 2 * ncm + n_out]
        dh_scr = refs[n_in + 2 * ncm + n_out]
        sems = refs[n_in + 2 * ncm + n_out + 1:]
        g, c = pl.program_id(0), pl.program_id(1)

        @pl.when((g == 0) & (c == 0))
        def _():
            comm.start(cin, cout, sems)

        @pl.when(c == 0)
        def _():
            dh_scr[...] = jnp.zeros_like(dh_scr)
            for ref in outs[3:]:
                ref[...] = jnp.zeros_like(ref)

        for sub in reversed(range(cps)):
            rs = pl.ds(sub * CHUNK, CHUNK)
            chunk_ins = tuple(r.at[rs] for r in ins[:5]) + (ins[5].at[sub],) + ins[6:]
            chunk_outs = tuple(r.at[rs] for r in outs[:3]) + outs[3:]
            all_groups(chunk_ins + chunk_outs + (dh_scr,))

        @pl.when((g == grid[0] - 1) & (c == grid[1] - 1))
        def _():
            comm.wait(cin, cout, sems)

    rev = lambda c: grid[1] - 1 - c
    vec = pl.BlockSpec((1, gw), lambda g, c: (0, g))
    blk = pl.BlockSpec((cps * CHUNK, gw), lambda g, c: (rev(c), g))
    xblk = pl.BlockSpec((cps * CHUNK, gps * XBC_BLK), lambda g, c: (rev(c), g))
    res = pl.pallas_call(
        body, name="ssd_bwd", grid=grid,
        in_specs=[blk, blk, blk, xblk, blk,
                  pl.BlockSpec((cps, gps, D_STATE, GROUP_W), lambda g, c: (rev(c), g, 0, 0)),
                  vec, vec, vec,
                  pl.BlockSpec((CHUNK, CHUNK), lambda g, c: (0, 0))] + comm.in_specs,
        out_specs=[blk, xblk, blk, vec, vec, vec] + comm.out_specs,
        out_shape=[
            jax.ShapeDtypeStruct((lp, NP), BF16),
            jax.ShapeDtypeStruct((lp, D_CONV), F32),
            jax.ShapeDtypeStruct((lp, D_SSD), F32),
            jax.ShapeDtypeStruct((1, D_SSD), F32),
            jax.ShapeDtypeStruct((1, D_SSD), F32),
            jax.ShapeDtypeStruct((1, D_SSD), F32),
        ] + comm.out_shape,
        scratch_shapes=[pltpu.VMEM((gps, D_STATE, GROUP_W), F32)] + comm.scratch,
        compiler_params=_params(("arbitrary", "arbitrary"), side_effects=True),
    )(dmix, ytot, proj, xbc, dt_rep, hprev, a_rep, dsk_rep, wn, tri, *comm.operands)
    return res


def _stack_heads(t, h):
    return jnp.concatenate([t[:, (REP * h + r) * HEAD_DIM:(REP * h + r + 1) * HEAD_DIM] for r in range(REP)], axis=0)


def _band(t2, t1, t0, h):
    sl = slice(h * HEAD_DIM, (h + 1) * HEAD_DIM)
    return jnp.concatenate([t2[:, sl], t1[:, sl], t0[:, sl]], axis=0)


def _attn_probs(s, sink, qc):
    s = s * (HEAD_DIM ** -0.5)
    key_abs = (qc - WINDOW_CHUNKS) * CHUNK + lax.broadcasted_iota(jnp.int32, s.shape, 1)
    s = jnp.where(key_abs >= PAD_LEAD, s, -jnp.inf)
    m = jnp.maximum(jnp.max(s, axis=-1, keepdims=True), sink)
    p = jnp.exp(s - m)
    ps = jnp.exp(sink - m)
    denom = jnp.sum(p, axis=-1, keepdims=True) + ps
    return p / denom, ps / denom


ATT_QC_FWD = 6
ATT_QC_BWD = 2


def _kv_specs(width, newest_chunk_of, kc):
    return [pl.BlockSpec((CHUNK, width), functools.partial(lambda j, p: (jnp.maximum(newest_chunk_of(p) - j, 0), 0), j))
            for j in range(kc - 1, -1, -1)]


def _attn_fwd(qr, kr, vb, proj, sink_stack, mix):
    lp = qr.shape[0]
    nc = lp // CHUNK
    qn = _tile(nc, (ATT_QC_FWD, 2, 1))
    kn = WINDOW_CHUNKS + qn
    qrows = qn * CHUNK

    def body(q_ref, *rest):
        k_refs, v_refs = rest[:kn], rest[kn:2 * kn]
        g_ref, sink_ref, _, att_ref, mix_ref = rest[2 * kn:]
        p = pl.program_id(0)
        q = q_ref[...]
        ks = [r[...] for r in k_refs]
        vs = [r[...] for r in v_refs]
        units = [(u, h) for u in range(qn) for h in range(KV_HEADS)]
        s = [_dot(_stack_heads(q[u * CHUNK:(u + 1) * CHUNK, :], h), _band(*ks[u:u + 3], h), NT) for u, h in units]
        vbh = [_band(*vs[u:u + 3], h) for u, h in units]
        pn = [_attn_probs(s[i], sink_ref[h], qn * p + u)[0].astype(BF16) for i, (u, h) in enumerate(units)]
        o = [_dot(pn[i], vbh[i]) for i in range(len(units))]
        att = jnp.concatenate(
            [jnp.concatenate([o[u * KV_HEADS + h][r * CHUNK:(r + 1) * CHUNK, :] for h in range(KV_HEADS) for r in range(REP)],
                             axis=1) for u in range(qn)], axis=0)
        att_ref[...] = att
        g = g_ref[...]
        mix_ref[...] = (att * (g * _sigmoid(g))).astype(BF16)

    newest = lambda p: qn * p + qn - 1
    return pl.pallas_call(
        body, name="attn_fwd", grid=(nc // qn,),
        in_specs=[pl.BlockSpec((qrows, D_ATT), lambda p: (p, 0))] + _kv_specs(D_KV, newest, kn) + _kv_specs(D_KV, newest, kn) + [
            pl.BlockSpec((qrows, D_ATT), lambda p: (p, OFF_G // D_ATT)),
            pl.BlockSpec((KV_HEADS, REP * CHUNK, 1), lambda p: (0, 0, 0)),
            ANY,
        ],
        out_specs=[pl.BlockSpec((qrows, D_ATT), lambda p: (p, 0)),
                   pl.BlockSpec((qrows, D_ATT), lambda p: (p, D_SSD // D_ATT))],
        out_shape=[jax.ShapeDtypeStruct((lp, D_ATT), F32), jax.ShapeDtypeStruct((lp, D_MIX), BF16)],
        input_output_aliases={2 * kn + 3: 1},
        compiler_params=_params(("parallel",)),
    )(qr, *([kr] * kn), *([vb] * kn), proj, sink_stack, mix)


def _attn_bwd(qr, kr, vb, att, proj, dmix, sink_stack):
    lp = qr.shape[0]
    nc = lp // CHUNK
    qn = ATT_QC_BWD
    kn = WINDOW_CHUNKS + qn
    assert nc % qn == 0 and WINDOW_CHUNKS % qn == 0
    steps = nc // qn
    qrows = qn * CHUNK
    wrows = kn * CHUNK

    def body(q_ref, *rest):
        k_refs, v_refs = rest[:kn], rest[kn:2 * kn]
        (att_ref, g_ref, do_ref, sink_ref, dq_ref, dk_ref, dv_ref, dg_ref, dsink_ref, dk_acc, dv_acc) = rest[2 * kn:]
        step = pl.program_id(0)

        @pl.when(step == 0)
        def _():
            dk_acc[...] = jnp.zeros_like(dk_acc)
            dv_acc[...] = jnp.zeros_like(dv_acc)
            dsink_ref[...] = jnp.zeros_like(dsink_ref)

        q = q_ref[...]
        ks = [r[...] for r in k_refs]
        vs = [r[...] for r in v_refs]
        att = att_ref[...]
        g = g_ref[...]
        dog = do_ref[...]
        sg = _sigmoid(g)
        dg_ref[...] = dog * att * (sg * (1.0 + g * (1.0 - sg)))
        do = dog * (g * sg)
        units = [(u, h) for u in range(qn) for h in range(KV_HEADS)]
        n = range(len(units))
        rows = [slice(u * CHUNK, (u + 1) * CHUNK) for u in range(qn)]
        qs = [_stack_heads(q[rows[u], :], h) for u, h in units]
        kb = [_band(*ks[u:u + 3], h) for u, h in units]
        vbh = [_band(*vs[u:u + 3], h) for u, h in units]
        dos = [_stack_heads(do[rows[u], :], h) for u, h in units]
        dosb = [dos[i].astype(BF16) for i in n]
        s = [_dot(qs[i], kb[i], NT) for i in n]
        dp = [_dot(dosb[i], vbh[i], NT) for i in n]
        ds, pnb, dsink = [], [], []
        for i, (u, h) in enumerate(units):
            pn, psink = _attn_probs(s[i], sink_ref[h], qn * jnp.minimum(step, steps - 1) + u)
            delta = jnp.sum(dos[i] * _stack_heads(att[rows[u], :], h), axis=-1, keepdims=True)
            ds.append((pn * (dp[i] - delta)).astype(BF16))
            pnb.append(pn.astype(BF16))
            dsink.append(-(psink * delta))
        dqs = [_dot(ds[i], kb[i]) for i in n]
        dks = [_dot(ds[i], qs[i], TN) for i in n]
        dvs = [_dot(pnb[i], dosb[i], TN) for i in n]
        dq_ref[...] = jnp.concatenate(
            [jnp.concatenate([dqs[u * KV_HEADS + h][r * CHUNK:(r + 1) * CHUNK, :]
                              for h in range(KV_HEADS) for r in range(REP)], axis=1) for u in range(qn)],
            axis=0) * (HEAD_DIM ** -0.5)
        @pl.when(step < steps)
        def _():
            for i, (u, h) in enumerate(units):
                dsink_ref[h] += dsink[i]
            for u in range(qn):
                band = slice(u * CHUNK, u * CHUNK + BAND)
                dk_acc[band, :] += jnp.concatenate(dks[u * KV_HEADS:(u + 1) * KV_HEADS], axis=1) * (HEAD_DIM ** -0.5)
                dv_acc[band, :] += jnp.concatenate(dvs[u * KV_HEADS:(u + 1) * KV_HEADS], axis=1)

        dk_ref[...] = dk_acc[0:qrows, :]
        dv_ref[...] = dv_acc[0:qrows, :]
        for acc in (dk_acc, dv_acc):
            rest_rows = acc[qrows:wrows, :]
            acc[0:wrows - qrows, :] = rest_rows
            acc[wrows - qrows:wrows, :] = jnp.zeros((qrows, D_KV), F32)

    qp = lambda p: jnp.minimum(p, steps - 1)
    newest = lambda p: qn * qp(p) + qn - 1
    qblk = pl.BlockSpec((qrows, D_ATT), lambda p: (qp(p), 0))
    oldest = pl.BlockSpec((qrows, D_KV), lambda p: (jnp.maximum(p - 1, 0), 0))
    return pl.pallas_call(
        body, name="attn_bwd", grid=(steps + 1,),
        in_specs=[qblk] + _kv_specs(D_KV, newest, kn) + _kv_specs(D_KV, newest, kn) + [
            qblk,
            pl.BlockSpec((qrows, D_ATT), lambda p: (qp(p), OFF_G // D_ATT)),
            pl.BlockSpec((qrows, D_ATT), lambda p: (qp(p), D_SSD // D_ATT)),
            pl.BlockSpec((KV_HEADS, REP * CHUNK, 1), lambda p: (0, 0, 0)),
        ],
        out_specs=[qblk, oldest, oldest, qblk, pl.BlockSpec((KV_HEADS, REP * CHUNK, 1), lambda p: (0, 0, 0))],
        out_shape=[
            jax.ShapeDtypeStruct((lp, D_ATT), F32),
            jax.ShapeDtypeStruct((lp, D_KV), F32),
            jax.ShapeDtypeStruct((lp, D_KV), F32),
            jax.ShapeDtypeStruct((lp, D_ATT), F32),
            jax.ShapeDtypeStruct((KV_HEADS, REP * CHUNK, 1), F32),
        ],
        scratch_shapes=[pltpu.VMEM((wrows, D_KV), F32), pltpu.VMEM((wrows, D_KV), F32)],
        compiler_params=_params(("arbitrary",)),
    )(qr, *([kr] * kn), *([vb] * kn), att, proj, dmix, sink_stack)


def _post_loss(out, x2d, target, w):
    lp = out.shape[0]
    nc = lp // CHUNK
    nx = x2d.shape[0] // CHUNK

    k = _tile(nc, (ROW_K, 3, 2, 1))

    def body(o_ref, *rest):
        x_refs, t_refs = rest[:k], rest[k:2 * k]
        w_ref, do_ref, dy_ref, gw_ref, loss_ref = rest[2 * k:]
        i = pl.program_id(0)

        @pl.when(i == 0)
        def _():
            gw_ref[...] = jnp.zeros_like(gw_ref)
            loss_ref[...] = jnp.zeros_like(loss_ref)

        o = o_ref[...]
        w = w_ref[...]
        rstd = lax.rsqrt(jnp.mean(o * o, axis=-1, keepdims=True) + EPS)
        xhat = o * rstd
        x = jnp.concatenate([r[...] for r in x_refs], axis=0)
        t = jnp.concatenate([r[...] for r in t_refs], axis=0)
        chunk = _chunk_of_row(i, k)
        err = (x + xhat * w - t) * ((chunk > 0) & (chunk <= nx)).astype(F32)
        loss_ref[...] += 0.5 * jnp.sum(jnp.mean(err * err, axis=-1, keepdims=True), axis=0, keepdims=True)
        dy = err * (1.0 / D_MODEL)
        dy_ref[...] = dy
        gw_ref[...] += jnp.sum(dy * xhat, axis=0, keepdims=True)
        dxhat = dy * w
        do_ref[...] = (rstd * (dxhat - xhat * jnp.mean(dxhat * xhat, axis=-1, keepdims=True))).astype(BF16)

    row = pl.BlockSpec((k * CHUNK, D_MODEL), lambda i: (i, 0))
    return pl.pallas_call(
        body, name="post_loss", grid=(nc // k,),
        in_specs=[row] + _x_specs(nx, k) + _x_specs(nx, k) + [pl.BlockSpec((1, D_MODEL), lambda i: (0, 0))],
        out_specs=[row, row, pl.BlockSpec((1, D_MODEL), lambda i: (0, 0)), pl.BlockSpec((1, 128), lambda i: (0, 0))],
        out_shape=[
            jax.ShapeDtypeStruct((lp, D_MODEL), BF16),
            jax.ShapeDtypeStruct((lp, D_MODEL), F32),
            jax.ShapeDtypeStruct((1, D_MODEL), F32),
            jax.ShapeDtypeStruct((1, 128), F32),
        ],
        compiler_params=_params(("arbitrary",)),
    )(out, *([x2d] * k), *([target] * k), w)


def _prenorm_bwd(dhn, x2d, h0, dy, w):
    nx = x2d.shape[0] // CHUNK
    k = _tile(nx, (X_K, 4, 2, 1))
    rows = k * CHUNK

    def backward(h, dhn, w):
        rstd = lax.rsqrt(jnp.mean(h * h, axis=-1, keepdims=True) + EPS)
        xhat = h * rstd
        dxhat = dhn * w
        dh = rstd * (dxhat - xhat * jnp.mean(dxhat * xhat, axis=-1, keepdims=True))
        return dh, jnp.sum(dhn * xhat, axis=0, keepdims=True)

    def body(*refs):
        dhn_refs, dy_refs = refs[:k], refs[k:2 * k]
        x_ref, w_ref, gx_ref, gw_ref = refs[2 * k:]

        @pl.when(pl.program_id(0) == 0)
        def _():
            gw_ref[...] = jnp.zeros_like(gw_ref)

        dh, gw = backward(x_ref[...], jnp.concatenate([r[...] for r in dhn_refs], axis=0), w_ref[...])
        gx_ref[...] = dh + jnp.concatenate([r[...] for r in dy_refs], axis=0)
        gw_ref[...] += gw

    def body_meta(dhn_ref, h0_ref, w_ref, d0_ref, gw_ref):
        d0_ref[...], gw_ref[...] = backward(h0_ref[...], dhn_ref[...], w_ref[...])

    chunk_specs = [pl.BlockSpec((CHUNK, D_MODEL), functools.partial(lambda u, i: (k * i + 1 + u, 0), u)) for u in range(k)]
    first = pl.BlockSpec((CHUNK, D_MODEL), lambda i: (0, 0))
    vec = pl.BlockSpec((1, D_MODEL), lambda i: (0, 0))
    wide = pl.BlockSpec((rows, D_MODEL), lambda i: (i, 0))
    gx, gw_x = pl.pallas_call(
        body, name="prenorm_bwd", grid=(nx // k,),
        in_specs=chunk_specs + chunk_specs + [wide, vec],
        out_specs=[wide, vec],
        out_shape=[jax.ShapeDtypeStruct((nx * CHUNK, D_MODEL), F32), jax.ShapeDtypeStruct((1, D_MODEL), F32)],
        compiler_params=_params(("arbitrary",)),
    )(*([dhn] * k), *([dy] * k), x2d, w)
    d0, gw_0 = pl.pallas_call(
        body_meta, name="prenorm_bwd_meta", grid=(1,),
        in_specs=[first, first, vec], out_specs=[first, vec],
        out_shape=[jax.ShapeDtypeStruct((CHUNK, D_MODEL), F32), jax.ShapeDtypeStruct((1, D_MODEL), F32)],
        compiler_params=_params(("arbitrary",)),
    )(dhn, h0, w)
    return gx, d0, gw_x + gw_0


def _adamw(slabs, w, m, v, name):
    rows, cols = w.shape
    tr = _tile(rows, (256, 128, 64, 16, 8))
    c1 = 1.0 - ADAM_B1 ** ADAM_STEP
    c2 = 1.0 - ADAM_B2 ** ADAM_STEP

    def body(s_ref, w_ref, m_ref, v_ref, g_ref, d_ref, mo_ref, vo_ref):
        g = s_ref[0].astype(F32)
        for k in range(1, slabs.shape[0]):
            g = g + s_ref[k].astype(F32)
        w = w_ref[...]
        m = ADAM_B1 * m_ref[...] + (1.0 - ADAM_B1) * g
        v = ADAM_B2 * v_ref[...] + (1.0 - ADAM_B2) * (g * g)
        g_ref[...] = g
        mo_ref[...] = m
        vo_ref[...] = v
        d_ref[...] = -ADAM_LR * ((m / c1) / (jnp.sqrt(v / c2) + ADAM_EPS) + ADAM_WD * w)

    blk = pl.BlockSpec((tr, cols), lambda i: (i, 0))
    shape = jax.ShapeDtypeStruct((rows, cols), F32)
    return pl.pallas_call(
        body, name=name, grid=(rows // tr,),
        in_specs=[pl.BlockSpec((slabs.shape[0], tr, cols), lambda i: (0, i, 0)), blk, blk, blk],
        out_specs=[blk, blk, blk, blk],
        out_shape=[shape, shape, shape, shape],
        compiler_params=_params(("parallel",)),
    )(slabs, w, m, v)


def _perm_xbc(a):
    lead = a.shape[:-1]
    xs = a[..., :D_SSD].reshape(lead + (N_GROUPS, GROUP_W))
    b = a[..., D_SSD:D_SSD + N_GROUPS * D_STATE].reshape(lead + (N_GROUPS, D_STATE))
    c = a[..., D_SSD + N_GROUPS * D_STATE:].reshape(lead + (N_GROUPS, D_STATE))
    return jnp.concatenate([xs, b, c], axis=-1).reshape(lead + (D_CONV,))


def _unperm_xbc(a):
    lead = a.shape[:-1]
    t = a.reshape(lead + (N_GROUPS, XBC_BLK))
    xs = t[..., :GROUP_W].reshape(lead + (D_SSD,))
    b = t[..., GROUP_W:GROUP_W + D_STATE].reshape(lead + (N_GROUPS * D_STATE,))
    c = t[..., GROUP_W + D_STATE:].reshape(lead + (N_GROUPS * D_STATE,))
    return jnp.concatenate([xs, b, c], axis=-1)


R_Z, R_XBC, R_DT, R_Q, R_K, R_V, R_G = 0, 2048, 6144, 6176, 7200, 7456, 7712


def _internal_of_reference():
    ref = np.arange(D_IN_PROJ)
    out = np.empty(D_IN_PROJ, np.int64)
    out[R_Z:R_XBC] = OFF_Z + ref[:D_SSD]
    xs = np.arange(D_SSD)
    out[R_XBC:R_XBC + D_SSD] = OFF_XBC + (xs // GROUP_W) * XBC_BLK + xs % GROUP_W
    bc = np.arange(N_GROUPS * D_STATE)
    out[R_XBC + D_SSD:R_XBC + D_SSD + N_GROUPS * D_STATE] = OFF_XBC + (bc // D_STATE) * XBC_BLK + GROUP_W + bc % D_STATE
    out[R_XBC + D_SSD + N_GROUPS * D_STATE:R_DT] = OFF_XBC + (bc // D_STATE) * XBC_BLK + GROUP_W + D_STATE + bc % D_STATE
    out[R_DT:R_Q] = OFF_DT + np.arange(SSD_HEADS)
    out[R_Q:R_K] = OFF_Q + np.arange(D_ATT)
    out[R_K:R_V] = OFF_K + np.arange(D_KV)
    out[R_V:R_G] = OFF_V + np.arange(D_KV)
    out[R_G:] = OFF_G + np.arange(D_ATT)
    return out


def _runs(src, dst_break):
    runs, lo = [], 0
    for i in range(1, len(src) + 1):
        if i == len(src) or src[i] != src[i - 1] + 1 or dst_break[i] != dst_break[i - 1]:
            runs.append((lo, i))
            lo = i
    return runs


RELAYOUT_ROWS = 256


def _lane_window(ref, lead, c0, n):
    a0 = c0 // 128 * 128
    a1 = min(-(-(c0 + n) // 128) * 128, ref.shape[-1])
    return ref[lead + (slice(None), slice(a0, a1))][:, c0 - a0:c0 - a0 + n]


def _gather_w_in(w_shard, small):
    int_of_ref = _internal_of_reference()
    ref_of_int = np.full(NP, -1, np.int64)
    ref_of_int[int_of_ref] = np.arange(D_IN_PROJ)
    shard = np.where(ref_of_int >= 0, ref_of_int // SHARD_IN, -1)
    src = np.where(ref_of_int >= 0, ref_of_int, -10 - 2 * np.arange(NP))
    plan, zeros = [], 0
    for lo, hi in _runs(src, shard):
        if ref_of_int[lo] < 0:
            zeros += hi - lo
            continue
        if zeros:
            plan.append((None, 0, zeros))
            zeros = 0
        plan.append((int(ref_of_int[lo] // SHARD_IN), int(ref_of_int[lo] % SHARD_IN), hi - lo))
    if zeros:
        plan.append((None, 0, zeros))
    tr = RELAYOUT_ROWS
    steps = D_MODEL // tr
    side = _GatherTwoLevel(small)
    ns = side.n

    def body(w_ref, *rest):
        s_in, o_ref, land_ref, s_out = rest[:ns], rest[ns], rest[ns + 1], rest[ns + 2:2 * ns + 2]
        tile_buf, send_sems, recv_sems, load_sems, local_sem = rest[2 * ns + 2:2 * ns + 7]
        side_sems = rest[2 * ns + 7:]
        i = pl.program_id(0)
        x, y, c = lax.axis_index("x"), lax.axis_index("y"), lax.axis_index("c")
        me, sibling = (x, y, c), (x, y, 1 - c)
        chips = [(1 - x, y), (x, 1 - y), (1 - x, 1 - y)]

        def rows(t):
            return pl.ds(t * tr, tr)

        def tile(place, t):
            return land_ref.at[4 * place[0] + 2 * place[1] + place[2], rows(t), :]

        def copy(t, k, block, to, src=None):
            return pltpu.make_async_remote_copy(
                src_ref=tile(block, t) if src is None else src, dst_ref=tile(block, t),
                send_sem=send_sems.at[t, k], recv_sem=recv_sems.at[t, k], device_id=to, device_id_type=pl.DeviceIdType.MESH)

        def own_sends(t):
            mine = w_ref.at[rows(t), :]
            return [copy(t, 0, me, sibling, src=mine)] + [copy(t, 1 + j, me, (*chip, c), src=mine)
                                                          for j, chip in enumerate(chips)]

        own = pltpu.make_async_copy(w_ref, land_ref.at[4 * x + 2 * y + c], local_sem)

        @pl.when(i == 0)
        def _():
            side.start(s_in, s_out, side_sems)
            own.start()
            for t in range(steps):
                for cp in own_sends(t):
                    cp.start()
            own.wait()

        forwards = []
        for j, chip in enumerate(chips):
            copy(i, 1 + j, (*chip, c), me).wait_recv()
            forwards.append(copy(i, 4 + j, (*chip, c), sibling))
            forwards[j].start()
        copy(i, 0, sibling, me).wait_recv()
        for j, chip in enumerate(chips):
            copy(i, 4 + j, (*chip, 1 - c), me).wait_recv()
        loads = [pltpu.make_async_copy(land_ref.at[s, rows(i), :], tile_buf.at[s], load_sems.at[s]) for s in range(N_DEV)]
        for cp in loads:
            cp.start()
        for cp in loads:
            cp.wait()
        o_ref[...] = jnp.concatenate(
            [jnp.zeros((tr, n), o_ref.dtype) if s is None else _lane_window(tile_buf, (s,), c0, n) for s, c0, n in plan], axis=1)
        for cp in forwards:
            cp.wait_send()

        @pl.when(i == steps - 1)
        def _():
            for t in range(steps):
                for cp in own_sends(t):
                    cp.wait_send()
            side.wait(s_in, s_out, side_sems)

    res = pl.pallas_call(
        body, name="gather_w_in", grid=(steps,),
        in_specs=[ANY] + side.in_specs,
        out_specs=[pl.BlockSpec((tr, NP), lambda i: (i, 0)), ANY] + side.out_specs,
        out_shape=[jax.ShapeDtypeStruct((D_MODEL, NP), w_shard.dtype),
                   jax.ShapeDtypeStruct((N_DEV, D_MODEL, SHARD_IN), w_shard.dtype)] + side.out_shape,
        scratch_shapes=[pltpu.VMEM((N_DEV, tr, SHARD_IN), w_shard.dtype),
                        pltpu.SemaphoreType.DMA((steps, N_DEV - 1)), pltpu.SemaphoreType.DMA((steps, N_DEV - 1)),
                        pltpu.SemaphoreType.DMA((N_DEV,)), pltpu.SemaphoreType.DMA(())] + side.scratch,
        compiler_params=_params(("arbitrary",), side_effects=True),
    )(w_shard, *side.operands)
    return [res[0]] + list(res[2:])


def _w_in_chip_slabs(dw):
    int_of_ref = _internal_of_reference()
    plan = []
    for s in range(N_DEV):
        cols = int_of_ref[s * SHARD_IN:(s + 1) * SHARD_IN]
        plan.append([(int(cols[lo]), hi - lo) for lo, hi in _runs(cols, np.zeros_like(cols))])
    tr = RELAYOUT_ROWS
    steps = D_MODEL // tr
    pairs = N_DEV // 2

    def body(dw_ref, o_ref, land_ref, send_buf, land_buf, send_sems, recv_sems, load_sems):
        i = pl.program_id(0)
        x, y, c = lax.axis_index("x"), lax.axis_index("y"), lax.axis_index("c")
        slot = i % 2

        def sends(step, sl):
            return [pltpu.make_async_remote_copy(
                src_ref=send_buf.at[sl, 2 * q + (1 - c)], dst_ref=land_ref.at[q, pl.ds(step * tr, tr), :],
                send_sem=send_sems.at[sl, q], recv_sem=recv_sems.at[step, q],
                device_id=(x, y, 1 - c), device_id_type=pl.DeviceIdType.MESH) for q in range(pairs)]

        @pl.when(i < steps)
        def _():
            @pl.when(i >= 2)
            def _():
                for cp in sends(i - 2, slot):
                    cp.wait_send()

            for s in range(N_DEV):
                send_buf[slot, s] = jnp.concatenate([_lane_window(dw_ref, (), c0, n) for c0, n in plan[s]], axis=1)
            for cp in sends(i, slot):
                cp.start()

        @pl.when(i >= 1)
        def _():
            loads = []
            for q, cp in enumerate(sends(i - 1, 1 - slot)):
                cp.wait_recv()
                loads.append(pltpu.make_async_copy(land_ref.at[q, pl.ds((i - 1) * tr, tr), :], land_buf.at[q], load_sems.at[q]))
                loads[q].start()
            for q in range(pairs):
                loads[q].wait()
                o_ref[q] = (send_buf[1 - slot, 2 * q + c].astype(F32) + land_buf[q].astype(F32)).astype(o_ref.dtype)

        @pl.when(i == steps)
        def _():
            for cp in sends(steps - 2, steps % 2) + sends(steps - 1, 1 - steps % 2):
                cp.wait_send()

    last = steps - 1
    return pl.pallas_call(
        body, name="dw_in_relayout", grid=(steps + 1,),
        in_specs=[pl.BlockSpec((tr, NP), lambda i: (jnp.minimum(i, last), 0))],
        out_specs=[pl.BlockSpec((pairs, tr, SHARD_IN), lambda i: (0, jnp.maximum(i - 1, 0), 0)), ANY],
        out_shape=[jax.ShapeDtypeStruct((pairs, D_MODEL, SHARD_IN), dw.dtype),
                   jax.ShapeDtypeStruct((pairs, D_MODEL, SHARD_IN), dw.dtype)],
        scratch_shapes=[pltpu.VMEM((2, N_DEV, tr, SHARD_IN), dw.dtype), pltpu.VMEM((pairs, tr, SHARD_IN), dw.dtype),
                        pltpu.SemaphoreType.DMA((2, pairs)), pltpu.SemaphoreType.DMA((steps, pairs)),
                        pltpu.SemaphoreType.DMA((pairs,))],
        compiler_params=_params(("arbitrary",), side_effects=True),
    )(dw)[0]


def _rep_heads(a):
    return jnp.repeat(a, HEAD_DIM, axis=1)


SMALL = (("norm_pre_w", 2048), ("conv_b", 4096), ("dt_bias", 32), ("a_log", 32), ("d_skip", 32),
         ("ssd_norm_w", 2048), ("attn_sinks", 16), ("norm_post_w", 2048))
SMALL_USED = sum(size for _, size in SMALL)
SMALL_LEN = 10368


def _pack_small(d, loss=None):
    parts = [d[name].reshape(1, size) for name, size in SMALL]
    tail = jnp.zeros((1, SMALL_LEN - SMALL_USED), F32)
    if loss is not None:
        tail = tail.at[0, 0].set(loss)
    return jnp.concatenate(parts + [tail], axis=1)


def _unpack_small(vec):
    out, off = {}, 0
    for name, size in SMALL:
        out[name] = vec[:, off:off + size]
        off += size
    return out


def kernel(x, meta_tokens, norm_pre_w, w_in, conv_w, conv_b, dt_bias, a_log, d_skip, ssd_norm_w, attn_sinks, w_out, norm_post_w, loss_target, m_meta_tokens, m_norm_pre_w, m_w_in, m_conv_w, m_conv_b, m_dt_bias, m_a_log, m_d_skip, m_ssd_norm_w, m_attn_sinks, m_w_out, m_norm_post_w, v_meta_tokens, v_norm_pre_w, v_w_in, v_conv_w, v_conv_b, v_dt_bias, v_a_log, v_d_skip, v_ssd_norm_w, v_attn_sinks, v_w_out, v_norm_post_w):
    seq = x.shape[1]
    lp = seq + 2 * CHUNK
    x2d = x[0]

    w_all, conv_w_g, meta_g = _gather_w_in(w_in[0].astype(BF16), [conv_w[0], meta_tokens])
    conv_w_full = _perm_xbc(jnp.transpose(conv_w_g, (1, 0, 2)).reshape(CONV_WIDTH, D_CONV))
    conv_b_int = _perm_xbc(conv_b)
    meta_full = jnp.transpose(meta_g, (1, 0, 2)).reshape(N_META, D_MODEL)
    h0 = jnp.concatenate([jnp.zeros((PAD_LEAD, D_MODEL), F32), meta_full], axis=0)

    pos = (jnp.arange(lp) - PAD_LEAD).astype(F32)
    half = HEAD_DIM // 2
    inv = ROPE_THETA ** (-jnp.arange(half, dtype=F32) / half)
    ang = pos[:, None] * inv[None, :]
    cos_t = jnp.tile(jnp.cos(ang), (1, 4))
    sin_t = jnp.tile(jnp.concatenate([-jnp.sin(ang), jnp.sin(ang)], axis=1), (1, 2))
    head_of_col = np.arange(D_SSD) // HEAD_DIM
    expand = jnp.asarray((np.arange(128)[:, None] == head_of_col[None, :]).astype(np.float32))
    reduce_t = jnp.asarray((head_of_col[:, None] == np.arange(128)[None, :]).astype(np.float32))
    tri = jnp.asarray(np.tril(np.ones((CHUNK, CHUNK), np.float32)))
    a_rep = _rep_heads(-jnp.exp(a_log))
    dsk_rep = _rep_heads(d_skip)
    dt_bias_pad = jnp.pad(dt_bias, ((0, 0), (0, 128 - SSD_HEADS)))
    sink_stack = jnp.repeat(attn_sinks.reshape(KV_HEADS, REP), CHUNK, axis=1).reshape(KV_HEADS, REP * CHUNK, 1)

    hn = _prenorm(x2d, h0, norm_pre_w)
    tm = _tile(lp, (1056, 704, 128, 64))
    proj, w_out_g = _matmul(hn, w_all, tm=tm, tn=1536, tk=D_MODEL, out_dtype=F32, name="in_proj",
                            comm=_Comm([(w_out[0].astype(BF16), "gather")]))
    w_out_full = w_out_g.reshape(D_MIX, D_MODEL)
    qr, kr, vb, dt_rep = _act_fwd(proj, cos_t, sin_t, expand, dt_bias_pad)
    mix, ytot, hprev, xbc = _ssd_fwd(proj, conv_w_full, conv_b_int, dt_rep, a_rep, dsk_rep, ssd_norm_w, tri)
    att, mix = _attn_fwd(qr, kr, vb, proj, sink_stack, mix)
    out = _matmul(mix, w_out_full, tm=tm, tn=1024, tk=D_MIX, out_dtype=F32, name="out_proj")
    dout, dy, g_norm_post, loss_part = _post_loss(out, x2d, loss_target[0], norm_post_w)

    dmix = _matmul(dout, w_out_full, trans_b=True, tm=tm, tn=1024, tk=D_MODEL, out_dtype=F32, name="dmix")
    dw_out = _matmul(mix, dout, trans_a=True, tm=512, tn=1024, tk=lp, out_dtype=BF16, name="dw_out")
    dqr, dkr, dv, dg, dsink_rows = _attn_bwd(qr, kr, vb, att, proj, dmix, sink_stack)
    dproj, dxbc, ddt_part, dd_part, da_part, g_ssd_norm, g_out = _ssd_bwd(
        dmix, ytot, proj, xbc, dt_rep, hprev, a_rep, dsk_rep, ssd_norm_w, tri,
        _Comm([(dw_out.reshape(N_DEV, D_MIX // N_DEV, D_MODEL), "scatter")]))
    dproj, dconv_w_int, dconv_b_int = _conv_bwd(dxbc, proj, conv_w_full, conv_b_int, dproj)
    dproj, ddt_bias = _act_bwd(dqr, dkr, dv, dg, ddt_part, cos_t, sin_t, reduce_t, dproj)
    dw_all = _matmul(hn, dproj, trans_a=True, tm=512, tn=1024, tk=lp, out_dtype=BF16, name="dw_in")
    dw_chip = _w_in_chip_slabs(dw_all)
    dhn, g_in = _matmul(dproj, w_all, trans_b=True, tm=tm, tn=1024, tk=1536, out_dtype=F32, name="dhn",
                        comm=_Comm([(dw_chip, "scatter")], scope="chips"))
    grad_x, dh0, g_norm_pre = _prenorm_bwd(dhn, x2d, h0, dy, norm_pre_w)

    dmeta = dh0[PAD_LEAD:, :]
    dconv_w_ref = _unperm_xbc(dconv_w_int)
    heads = lambda part: part.reshape(SSD_HEADS, HEAD_DIM).sum(axis=1).reshape(1, SSD_HEADS)
    small_local = _pack_small({
        "norm_pre_w": g_norm_pre, "conv_b": _unperm_xbc(dconv_b_int), "dt_bias": ddt_bias[:, :SSD_HEADS],
        "a_log": heads(da_part) * (-jnp.exp(a_log)), "d_skip": heads(dd_part), "ssd_norm_w": g_ssd_norm,
        "attn_sinks": dsink_rows.reshape(Q_HEADS, CHUNK).sum(axis=1).reshape(1, Q_HEADS),
        "norm_post_w": g_norm_post}, loss=loss_part[0, 0])
    g_conv, g_meta, g_small = _exchange(
        [(jnp.transpose(dconv_w_ref.reshape(CONV_WIDTH, N_DEV, D_CONV // N_DEV), (1, 0, 2)), "scatter"),
         (jnp.transpose(dmeta.reshape(N_META, N_DEV, D_MODEL // N_DEV), (1, 0, 2)), "scatter"),
         (small_local, "gather")], "exchange_small")

    res = {}
    res["w_in"] = [o[None] for o in _adamw(g_in, w_in[0], m_w_in[0], v_w_in[0], "adamw_w_in")]
    res["w_out"] = [o[None] for o in _adamw(g_out, w_out[0], m_w_out[0], v_w_out[0], "adamw_w_out")]
    res["conv_w"] = [o[None] for o in _adamw(g_conv, conv_w[0], m_conv_w[0], v_conv_w[0], "adamw_conv_w")]
    res["meta_tokens"] = _adamw(g_meta, meta_tokens, m_meta_tokens, v_meta_tokens, "adamw_meta")
    given = dict(norm_pre_w=(norm_pre_w, m_norm_pre_w, v_norm_pre_w), conv_b=(conv_b, m_conv_b, v_conv_b),
                 dt_bias=(dt_bias, m_dt_bias, v_dt_bias), a_log=(a_log, m_a_log, v_a_log),
                 d_skip=(d_skip, m_d_skip, v_d_skip), ssd_norm_w=(ssd_norm_w, m_ssd_norm_w, v_ssd_norm_w),
                 attn_sinks=(attn_sinks, m_attn_sinks, v_attn_sinks),
                 norm_post_w=(norm_post_w, m_norm_post_w, v_norm_post_w))
    packed = [_pack_small({k: t[j] for k, t in given.items()}) for j in range(3)]
    small_out = _adamw(g_small, packed[0], packed[1], packed[2], "adamw_small")
    small_res = [_unpack_small(r) for r in small_out]
    loss = small_out[0][0, SMALL_USED]

    order = ["meta_tokens", "norm_pre_w", "w_in", "conv_w", "conv_b", "dt_bias", "a_log", "d_skip", "ssd_norm_w",
             "attn_sinks", "w_out", "norm_post_w"]
    outs = []
    for j in range(4):
        for name in order:
            outs.append(res[name][j] if name in res else small_res[j][name])
    return (loss, grad_x[None], *outs)
```

```python
import functools

import numpy as np
import jax
import jax.numpy as jnp
from jax import lax
from jax.experimental import pallas as pl
from jax.experimental.pallas import tpu as pltpu

F32 = jnp.float32
BF16 = jnp.bfloat16
HIGHEST = lax.Precision.HIGHEST

N_DEV = 8
D_MODEL = 2048
CHUNK = 64
N_META = 16
PAD_LEAD = CHUNK - N_META
EPS = 1e-6
N_GROUPS = 8
HEADS_PER_GROUP = 4
HEAD_DIM = 64
GROUP_W = HEADS_PER_GROUP * HEAD_DIM
D_STATE = 128
D_SSD = 2048
D_CONV = 4096
SSD_HEADS = 32
CONV_WIDTH = 4
Q_HEADS = 16
KV_HEADS = 4
REP = 4
D_ATT = 1024
D_KV = 256
WINDOW_CHUNKS = 2
BAND = (WINDOW_CHUNKS + 1) * CHUNK
ROPE_THETA = 10000.0
D_MIX = D_SSD + D_ATT
D_IN_PROJ = 8736
SHARD_IN = D_IN_PROJ // N_DEV

OFF_Z, OFF_XBC, OFF_Q, OFF_G, OFF_K, OFF_V, OFF_DT = 0, 2048, 6144, 7168, 8192, 8448, 8704
NP = 9216
TAIL_W = NP - OFF_Q
XBC_BLK = 512
SSD_GPS = 8
SSD_CPS = 2

ADAM_LR, ADAM_B1, ADAM_B2, ADAM_EPS, ADAM_WD, ADAM_STEP = 0.001, 0.9, 0.999, 1e-08, 0.01, 10

VMEM_LIMIT = 48 * 1024 * 1024

NN = (((1,), (0,)), ((), ()))
NT = (((1,), (1,)), ((), ()))
TN = (((0,), (0,)), ((), ()))
ANY = pl.BlockSpec(memory_space=pl.ANY)


def _dot(a, b, dims=NN, precision=None):
    return lax.dot_general(a, b, dims, precision=precision, preferred_element_type=F32)


def _tile(n, prefs):
    for t in prefs:
        if n % t == 0:
            return t
    return n


def _params(sem, vmem=VMEM_LIMIT, side_effects=False):
    return pltpu.CompilerParams(dimension_semantics=sem, vmem_limit_bytes=vmem, has_side_effects=side_effects)


def _sigmoid(x):
    return 1.0 / (1.0 + jnp.exp(-x))


class _Comm:
    def __init__(self, items, scope="devices"):
        self.items = items
        self.scope = scope
        self.slabs = slabs = N_DEV if scope == "devices" else N_DEV // 2
        self.n = n = len(items)
        self.operands = [arr for arr, _ in items]
        self.in_specs = [ANY] * n
        self.out_specs = [ANY] * n
        self.out_shape = [jax.ShapeDtypeStruct((slabs,) + tuple(arr.shape) if kind == "gather" else tuple(arr.shape),
                                               arr.dtype) for arr, kind in items]
        self.scratch = [pltpu.SemaphoreType.DMA((n, slabs - 1)), pltpu.SemaphoreType.DMA((n, slabs - 1)),
                        pltpu.SemaphoreType.DMA((n,))]

    def _places(self):
        pos = (lax.axis_index("x"), lax.axis_index("y"), lax.axis_index("c"))
        if self.scope == "devices":
            index = lambda p: 4 * p[0] + 2 * p[1] + p[2]
            masks = range(1, N_DEV)
        else:
            index = lambda p: 2 * p[0] + p[1]
            masks = (2, 4, 6)
        peers = []
        for k in masks:
            p = tuple(1 - pos[b] if (k >> (2 - b)) & 1 else pos[b] for b in range(3))
            peers.append((p, index(p)))
        return index(pos), peers

    def _copies(self, ins, outs, sems, landed):
        send_sems, recv_sems, local_sems = sems
        me, peers = self._places()
        local, remote = [], []
        for a, (_, kind) in enumerate(self.items):
            own = ins[a] if kind == "gather" else ins[a].at[me]
            local.append(pltpu.make_async_copy(own, outs[a].at[me], local_sems.at[a]))
            for k, (p, pid) in enumerate(peers):
                remote.append(pltpu.make_async_remote_copy(
                    src_ref=ins[a] if kind == "gather" else ins[a].at[pid],
                    dst_ref=outs[a].at[pid if landed else me],
                    send_sem=send_sems.at[a, k], recv_sem=recv_sems.at[a, k],
                    device_id=p, device_id_type=pl.DeviceIdType.MESH))
        return local, remote

    def start(self, ins, outs, sems):
        local, remote = self._copies(ins, outs, sems, landed=False)
        for cp in local + remote:
            cp.start()

    def wait(self, ins, outs, sems):
        local, remote = self._copies(ins, outs, sems, landed=True)
        for cp in remote + local:
            cp.wait()


def _exchange(items, name):
    comm = _Comm(items)
    n = comm.n

    def body(*refs):
        ins, outs, sems = refs[:n], refs[n:2 * n], refs[2 * n:]
        comm.start(ins, outs, sems)
        comm.wait(ins, outs, sems)

    return pl.pallas_call(
        body, name=name, in_specs=comm.in_specs, out_specs=comm.out_specs, out_shape=comm.out_shape,
        scratch_shapes=comm.scratch, compiler_params=pltpu.CompilerParams(has_side_effects=True),
    )(*comm.operands)


class _GatherTwoLevel:
    def __init__(self, arrays):
        self.arrays = arrays
        self.n = n = len(arrays)
        self.operands = list(arrays)
        self.in_specs = [ANY] * n
        self.out_specs = [ANY] * n
        self.out_shape = [jax.ShapeDtypeStruct((N_DEV,) + tuple(a.shape), a.dtype) for a in arrays]
        self.scratch = [pltpu.SemaphoreType.DMA((n, N_DEV - 1)), pltpu.SemaphoreType.DMA((n, N_DEV - 1)),
                        pltpu.SemaphoreType.DMA((n,))]

    def _plan(self, ins, outs, sems):
        send_sems, recv_sems, local_sems = sems
        x, y, c = lax.axis_index("x"), lax.axis_index("y"), lax.axis_index("c")
        me, sibling = (x, y, c), (x, y, 1 - c)
        chips = [(1 - x, y), (x, 1 - y), (1 - x, 1 - y)]

        def slab(a, place):
            return outs[a].at[4 * place[0] + 2 * place[1] + place[2]]

        def copy(a, k, block, to, src=None):
            return pltpu.make_async_remote_copy(
                src_ref=slab(a, block) if src is None else src, dst_ref=slab(a, block),
                send_sem=send_sems.at[a, k], recv_sem=recv_sems.at[a, k],
                device_id=to, device_id_type=pl.DeviceIdType.MESH)

        own, mine = [], []
        for a in range(self.n):
            mine.append(pltpu.make_async_copy(ins[a], slab(a, me), local_sems.at[a]))
            own.append(copy(a, 0, me, sibling, src=ins[a]))
            own += [copy(a, 1 + j, me, (*chip, c), src=ins[a]) for j, chip in enumerate(chips)]
        return me, sibling, chips, c, copy, own, mine

    def start(self, ins, outs, sems):
        _, _, _, _, _, own, mine = self._plan(ins, outs, sems)
        for cp in mine + own:
            cp.start()

    def wait(self, ins, outs, sems):
        me, sibling, chips, c, copy, own, mine = self._plan(ins, outs, sems)
        forwards = []
        for j, chip in enumerate(chips):
            for a in range(self.n):
                copy(a, 1 + j, (*chip, c), me).wait_recv()
                fwd = copy(a, 4 + j, (*chip, c), sibling)
                fwd.start()
                forwards.append(fwd)
        for a in range(self.n):
            copy(a, 0, sibling, me).wait_recv()
            for j, chip in enumerate(chips):
                copy(a, 4 + j, (*chip, 1 - c), me).wait_recv()
        for cp in own + forwards:
            cp.wait_send()
        for loc in mine:
            loc.wait()


def _matmul(a, b, *, tm, tn, tk, out_dtype, name, trans_a=False, trans_b=False, comm=None):
    m, k = (a.shape[1], a.shape[0]) if trans_a else a.shape
    n = b.shape[0] if trans_b else b.shape[1]
    nk = k // tk
    dims = TN if trans_a else (NT if trans_b else NN)
    assert not (trans_a and trans_b)
    nc = comm.n if comm else 0
    grid = (m // tm, n // tn, nk)

    def body(*refs):
        a_ref, b_ref = refs[:2]
        cin = refs[2:2 + nc]
        o_ref = refs[2 + nc]
        cout = refs[3 + nc:3 + 2 * nc]
        scratch = refs[3 + 2 * nc:]
        sems = scratch[len(scratch) - 3:] if comm else None
        i, j, kk = pl.program_id(0), pl.program_id(1), pl.program_id(2)
        if comm:
            @pl.when((i == 0) & (j == 0) & (kk == 0))
            def _():
                comm.start(cin, cout, sems)

        if nk == 1:
            o_ref[...] = _dot(a_ref[...], b_ref[...], dims).astype(out_dtype)
        else:
            acc_ref = scratch[0]

            @pl.when(kk == 0)
            def _():
                acc_ref[...] = jnp.zeros_like(acc_ref)

            acc_ref[...] += _dot(a_ref[...], b_ref[...], dims)

            @pl.when(kk == nk - 1)
            def _():
                o_ref[...] = acc_ref[...].astype(out_dtype)

        if comm:
            @pl.when((i == grid[0] - 1) & (j == grid[1] - 1) & (kk == nk - 1))
            def _():
                comm.wait(cin, cout, sems)

    a_spec = (pl.BlockSpec((tk, tm), lambda i, j, kk: (kk, i)) if trans_a
              else pl.BlockSpec((tm, tk), lambda i, j, kk: (i, kk)))
    b_spec = (pl.BlockSpec((tn, tk), lambda i, j, kk: (j, kk)) if trans_b
              else pl.BlockSpec((tk, tn), lambda i, j, kk: (kk, j)))
    sem = ("arbitrary",) * 3 if comm else ("parallel", "parallel", "arbitrary")
    res = pl.pallas_call(
        body, name=name, grid=grid,
        in_specs=[a_spec, b_spec] + (comm.in_specs if comm else []),
        out_specs=[pl.BlockSpec((tm, tn), lambda i, j, kk: (i, j))] + (comm.out_specs if comm else []),
        out_shape=[jax.ShapeDtypeStruct((m, n), out_dtype)] + (comm.out_shape if comm else []),
        scratch_shapes=([] if nk == 1 else [pltpu.VMEM((tm, tn), F32)]) + (comm.scratch if comm else []),
        compiler_params=_params(sem, side_effects=bool(comm)),
    )(a, b, *(comm.operands if comm else []))
    return res if comm else res[0]


ROW_K = 6
X_K = 8


def _x_specs(nx, k, tile_of=lambda i: i):
    return [pl.BlockSpec((CHUNK, D_MODEL), functools.partial(lambda u, i: (jnp.clip(k * tile_of(i) + u - 1, 0, nx - 1), 0), u))
            for u in range(k)]


def _chunk_of_row(i, k):
    return k * i + lax.broadcasted_iota(jnp.int32, (k * CHUNK, 1), 0) // CHUNK


CONV_COLS = 512
HALO = 8


def _conv_pre(ext, w, b):
    taps = [ext[HALO:, :]] + [pltpu.roll(ext, j, 0)[HALO:, :] for j in range(1, CONV_WIDTH)]
    acc = b + w[3:4, :] * taps[0]
    for j in range(1, CONV_WIDTH):
        acc = acc + w[3 - j:4 - j, :] * taps[j]
    return acc, taps


def _conv_bwd(dxbc, proj, conv_w, conv_b, dproj):
    lp = proj.shape[0]
    t = _tile(lp, (704, 384, 128, 64))
    hb = t // HALO
    nt = lp // t
    c0 = OFF_XBC // CONV_COLS

    def body(dx_ref, dxn_ref, u_ref, up_ref, un_ref, w_ref, b_ref, _, du_ref, dw_ref, db_ref):
        i = pl.program_id(1)
        w = w_ref[...]
        up = up_ref[...] * (i > 0).astype(F32)
        ext = jnp.concatenate([up, u_ref[...], un_ref[...]], axis=0)
        pre, taps = _conv_pre(ext, w, b_ref[...])
        dxn = dxn_ref[...] * (i < nt - 1).astype(F32)
        dxe = jnp.concatenate([dx_ref[...], dxn], axis=0)
        sg = _sigmoid(pre)
        dpre = dxe * sg * (1.0 + pre * (1.0 - sg))
        du = w[3:4, :] * dpre[:t, :]
        for j in range(1, CONV_WIDTH):
            du = du + w[3 - j:4 - j, :] * pltpu.roll(dpre, t + HALO - j, 0)[:t, :]
        du_ref[...] = du.astype(BF16)

        @pl.when(i == 0)
        def _():
            dw_ref[...] = jnp.zeros_like(dw_ref)
            db_ref[...] = jnp.zeros_like(db_ref)

        dp = dpre[:t, :]
        db_ref[...] += jnp.sum(dp, axis=0, keepdims=True)
        for j in range(CONV_WIDTH):
            dw_ref[3 - j:4 - j, :] += jnp.sum(dp * taps[j][:t, :], axis=0, keepdims=True)

    nxt = lambda i: jnp.minimum((i + 1) * hb, lp // HALO - 1)
    return pl.pallas_call(
        body, name="conv_bwd", grid=(D_CONV // CONV_COLS, nt),
        in_specs=[
            pl.BlockSpec((t, CONV_COLS), lambda j, i: (i, j)),
            pl.BlockSpec((HALO, CONV_COLS), lambda j, i: (nxt(i), j)),
            pl.BlockSpec((t, CONV_COLS), lambda j, i: (i, c0 + j)),
            pl.BlockSpec((HALO, CONV_COLS), lambda j, i: (jnp.maximum(i * hb - 1, 0), c0 + j)),
            pl.BlockSpec((HALO, CONV_COLS), lambda j, i: (nxt(i), c0 + j)),
            pl.BlockSpec((CONV_WIDTH, CONV_COLS), lambda j, i: (0, j)),
            pl.BlockSpec((1, CONV_COLS), lambda j, i: (0, j)),
            ANY,
        ],
        out_specs=[
            pl.BlockSpec((t, CONV_COLS), lambda j, i: (i, c0 + j)),
            pl.BlockSpec((CONV_WIDTH, CONV_COLS), lambda j, i: (0, j)),
            pl.BlockSpec((1, CONV_COLS), lambda j, i: (0, j)),
        ],
        out_shape=[
            jax.ShapeDtypeStruct((lp, NP), BF16),
            jax.ShapeDtypeStruct((CONV_WIDTH, D_CONV), F32),
            jax.ShapeDtypeStruct((1, D_CONV), F32),
        ],
        input_output_aliases={7: 0},
        compiler_params=_params(("parallel", "arbitrary")),
    )(dxbc, dxbc, proj, proj, proj, conv_w, conv_b, dproj)


def _swap_halves(t):
    w = t.shape[-1]
    lane = lax.broadcasted_iota(jnp.int32, t.shape, 1)
    return jnp.where((lane % HEAD_DIM) < HEAD_DIM // 2, pltpu.roll(t, w - HEAD_DIM // 2, 1),
                     pltpu.roll(t, HEAD_DIM // 2, 1))


def _act_fwd(proj, cos_t, sin_t, expand, dt_bias_pad):
    lp = proj.shape[0]
    t = _tile(lp, (384, 128, 64))

    def body(q_ref, k_ref, v_ref, dt_ref, cos_ref, sin_ref, ex_ref, bias_ref, qo_ref, ko_ref, vo_ref, dto_ref):
        i = pl.program_id(0)
        cos = cos_ref[...]
        sin = sin_ref[...]
        q = q_ref[...]
        qo_ref[...] = (q * jnp.tile(cos, (1, D_ATT // 128)) + _swap_halves(q) * jnp.tile(sin, (1, D_ATT // 128))).astype(BF16)
        k = k_ref[...]
        ko_ref[...] = (k * jnp.tile(cos, (1, D_KV // 128)) + _swap_halves(k) * jnp.tile(sin, (1, D_KV // 128))).astype(BF16)
        vo_ref[...] = v_ref[...].astype(BF16)
        raw = dt_ref[...] + bias_ref[...]
        sp = jnp.maximum(raw, 0.0) + jnp.log1p(jnp.exp(-jnp.abs(raw)))
        row = i * t + lax.broadcasted_iota(jnp.int32, sp.shape, 0)
        dto_ref[...] = _dot(jnp.where(row >= PAD_LEAD, sp, 0.0), ex_ref[...], NN, HIGHEST)

    return pl.pallas_call(
        body, name="act_fwd", grid=(lp // t,),
        in_specs=[
            pl.BlockSpec((t, D_ATT), lambda i: (i, OFF_Q // D_ATT)),
            pl.BlockSpec((t, D_KV), lambda i: (i, OFF_K // D_KV)),
            pl.BlockSpec((t, D_KV), lambda i: (i, OFF_V // D_KV)),
            pl.BlockSpec((t, 128), lambda i: (i, OFF_DT // 128)),
            pl.BlockSpec((t, 128), lambda i: (i, 0)),
            pl.BlockSpec((t, 128), lambda i: (i, 0)),
            pl.BlockSpec((128, D_SSD), lambda i: (0, 0)),
            pl.BlockSpec((1, 128), lambda i: (0, 0)),
        ],
        out_specs=[
            pl.BlockSpec((t, D_ATT), lambda i: (i, 0)),
            pl.BlockSpec((t, D_KV), lambda i: (i, 0)),
            pl.BlockSpec((t, D_KV), lambda i: (i, 0)),
            pl.BlockSpec((t, D_SSD), lambda i: (i, 0)),
        ],
        out_shape=[
            jax.ShapeDtypeStruct((lp, D_ATT), BF16),
            jax.ShapeDtypeStruct((lp, D_KV), BF16),
            jax.ShapeDtypeStruct((lp, D_KV), BF16),
            jax.ShapeDtypeStruct((lp, D_SSD), F32),
        ],
        compiler_params=_params(("parallel",)),
    )(proj, proj, proj, proj, cos_t, sin_t, expand, dt_bias_pad)


def _act_bwd(dqr, dkr, dv, dg, ddt_part, cos_t, sin_t, reduce_t, dproj):
    lp = dqr.shape[0]
    t = _tile(lp, (384, 128, 64))

    def body(dq_ref, dk_ref, dv_ref, dg_ref, ddt_ref, cos_ref, sin_ref, red_ref, _, o_ref, db_ref):
        i = pl.program_id(0)
        cos = cos_ref[...]
        sin = sin_ref[...]
        dq = dq_ref[...]
        dq = dq * jnp.tile(cos, (1, D_ATT // 128)) + _swap_halves(dq * jnp.tile(sin, (1, D_ATT // 128)))
        dk = dk_ref[...]
        dk = dk * jnp.tile(cos, (1, D_KV // 128)) + _swap_halves(dk * jnp.tile(sin, (1, D_KV // 128)))
        ddt = _dot(ddt_ref[...], red_ref[...], NN, HIGHEST)
        o_ref[...] = jnp.concatenate(
            [dq.astype(BF16), dg_ref[...].astype(BF16), dk.astype(BF16), dv_ref[...].astype(BF16), ddt.astype(BF16),
             jnp.zeros((t, NP - OFF_DT - 128), BF16)], axis=1)

        @pl.when(i == 0)
        def _():
            db_ref[...] = jnp.zeros_like(db_ref)

        db_ref[...] += jnp.sum(ddt, axis=0, keepdims=True)

    return pl.pallas_call(
        body, name="act_bwd", grid=(lp // t,),
        in_specs=[
            pl.BlockSpec((t, D_ATT), lambda i: (i, 0)),
            pl.BlockSpec((t, D_KV), lambda i: (i, 0)),
            pl.BlockSpec((t, D_KV), lambda i: (i, 0)),
            pl.BlockSpec((t, D_ATT), lambda i: (i, 0)),
            pl.BlockSpec((t, D_SSD), lambda i: (i, 0)),
            pl.BlockSpec((t, 128), lambda i: (i, 0)),
            pl.BlockSpec((t, 128), lambda i: (i, 0)),
            pl.BlockSpec((D_SSD, 128), lambda i: (0, 0)),
            ANY,
        ],
        out_specs=[pl.BlockSpec((t, TAIL_W), lambda i: (i, OFF_Q // TAIL_W)), pl.BlockSpec((1, 128), lambda i: (0, 0))],
        out_shape=[jax.ShapeDtypeStruct((lp, NP), BF16), jax.ShapeDtypeStruct((1, 128), F32)],
        input_output_aliases={8: 0},
        compiler_params=_params(("arbitrary",)),
    )(dqr, dkr, dv, dg, ddt_part, cos_t, sin_t, reduce_t, dproj)


def _cs_row(cs):
    row = lax.broadcasted_iota(jnp.int32, cs.shape, 0)
    lane = lax.broadcasted_iota(jnp.int32, cs.shape, 1)
    return jnp.sum(jnp.where(row == lane % HEAD_DIM, cs, 0.0), axis=0, keepdims=True)


def _ssd_fwd(proj, conv_w, conv_b, dt_rep, a_rep, dsk_rep, wn, tri):
    lp = proj.shape[0]
    nc = lp // CHUNK
    assert SSD_GPS == N_GROUPS
    gw = SSD_GPS * GROUP_W
    cps = _tile(nc, (SSD_CPS, 1))

    def body(rawa_ref, rawb_ref, cw_ref, cb_ref, dt_ref, z_ref, a_ref, dsk_ref, wn_ref, tri_ref,
             yn_ref, ytot_ref, hprev_ref, xbc_ref, h_scr, tail_scr):
        @pl.when(pl.program_id(1) == 0)
        def _():
            h_scr[...] = jnp.zeros_like(h_scr)
            tail_scr[...] = jnp.zeros_like(tail_scr)

        for sub in range(cps):
            rs = slice(sub * CHUNK, (sub + 1) * CHUNK)
            raw = jnp.concatenate([rawa_ref[rs, :], rawb_ref[rs, :]], axis=1)
            pre, _ = _conv_pre(jnp.concatenate([tail_scr[...], raw], axis=0), cw_ref[...], cb_ref[...])
            tail_scr[...] = raw[CHUNK - HALO:, :]
            xbc_ref[rs, :] = pre * _sigmoid(pre)

        G = range(SSD_GPS)
        colsl = [slice(gi * GROUP_W, (gi + 1) * GROUP_W) for gi in G]
        rows4 = lax.broadcasted_iota(jnp.int32, (GROUP_W, GROUP_W), 0) // HEAD_DIM
        cols4 = lax.broadcasted_iota(jnp.int32, (GROUP_W, GROUP_W), 1) // HEAD_DIM
        lrow = lax.broadcasted_iota(jnp.int32, (CHUNK, GROUP_W), 0)
        lcol = lax.broadcasted_iota(jnp.int32, (CHUNK, GROUP_W), 1) % HEAD_DIM

        def one_chunk(sub):
            rs = slice(sub * CHUNK, (sub + 1) * CHUNK)
            xbc = [xbc_ref[rs, gi * XBC_BLK:(gi + 1) * XBC_BLK] for gi in G]
            dt = [dt_ref[rs, colsl[gi]] for gi in G]
            xs = [xbc[gi][:, :GROUP_W] for gi in G]
            b = [xbc[gi][:, GROUP_W:GROUP_W + D_STATE].astype(BF16) for gi in G]
            c = [xbc[gi][:, GROUP_W + D_STATE:].astype(BF16) for gi in G]
            hprev = [h_scr[gi] for gi in G]
            cs = [_dot(tri_ref[...], dt[gi] * a_ref[:, colsl[gi]], NN, HIGHEST) for gi in G]
            yoff = [_dot(c[gi], hprev[gi].astype(BF16)) for gi in G]
            cs_t = [_cs_row(cs[gi]) for gi in G]
            xdt = [xs[gi] * dt[gi] for gi in G]
            cs_last = [cs[gi][CHUNK - 1:CHUNK, :] for gi in G]
            st = [_dot(b[gi], (xdt[gi] * jnp.exp(cs_last[gi] - cs[gi])).astype(BF16), TN) for gi in G]
            cb4 = [_dot(c[gi], jnp.concatenate([b[gi]] * HEADS_PER_GROUP, axis=0), NT) for gi in G]
            m = [(cb4[gi] * jnp.exp(jnp.where(lrow >= lcol, cs[gi] - cs_t[gi], -jnp.inf))).astype(BF16) for gi in G]
            xbd = [jnp.where(rows4 == cols4, jnp.concatenate([xdt[gi].astype(BF16)] * HEADS_PER_GROUP, axis=0), 0.0)
                   for gi in G]
            ydiag = [_dot(m[gi], xbd[gi]) for gi in G]
            for gi in G:
                cols = colsl[gi]
                ytot = ydiag[gi] + yoff[gi] * jnp.exp(cs[gi]) + dsk_ref[:, cols] * xs[gi]
                z = z_ref[rs, cols]
                gz = ytot * (z * _sigmoid(z))
                rstd = lax.rsqrt(jnp.mean(gz * gz, axis=-1, keepdims=True) + EPS)
                hprev_ref[sub, gi] = hprev[gi]
                h_scr[gi] = hprev[gi] * jnp.exp(cs_last[gi]) + st[gi]
                ytot_ref[rs, cols] = ytot
                yn_ref[rs, cols] = (gz * rstd * wn_ref[:, cols]).astype(BF16)

        for sub in range(cps):
            one_chunk(sub)

    vec = pl.BlockSpec((1, gw), lambda g, c: (0, g))
    blk = pl.BlockSpec((cps * CHUNK, gw), lambda g, c: (c, g))
    half = D_CONV // 2
    return pl.pallas_call(
        body, name="ssd_fwd", grid=(N_GROUPS // SSD_GPS, nc // cps),
        in_specs=[
            pl.BlockSpec((cps * CHUNK, half), lambda g, c: (c, OFF_XBC // half)),
            pl.BlockSpec((cps * CHUNK, half), lambda g, c: (c, OFF_XBC // half + 1)),
            pl.BlockSpec((CONV_WIDTH, D_CONV), lambda g, c: (0, 0)),
            pl.BlockSpec((1, D_CONV), lambda g, c: (0, 0)),
            blk, blk, vec, vec, vec,
            pl.BlockSpec((CHUNK, CHUNK), lambda g, c: (0, 0)),
        ],
        out_specs=[blk, blk, pl.BlockSpec((cps, SSD_GPS, D_STATE, GROUP_W), lambda g, c: (c, g, 0, 0)),
                   pl.BlockSpec((cps * CHUNK, D_CONV), lambda g, c: (c, 0))],
        out_shape=[
            jax.ShapeDtypeStruct((lp, D_MIX), BF16),
            jax.ShapeDtypeStruct((lp, D_SSD), F32),
            jax.ShapeDtypeStruct((nc, N_GROUPS, D_STATE, GROUP_W), F32),
            jax.ShapeDtypeStruct((lp, D_CONV), F32),
        ],
        scratch_shapes=[pltpu.VMEM((SSD_GPS, D_STATE, GROUP_W), F32), pltpu.VMEM((HALO, D_CONV), F32)],
        compiler_params=_params(("arbitrary", "arbitrary")),
    )(proj, proj, conv_w, conv_b, dt_rep, proj, a_rep, dsk_rep, wn, tri)


def _ssd_bwd(dmix, ytot, proj, xbc, dt_rep, hprev, a_rep, dsk_rep, wn, tri, comm):
    lp = xbc.shape[0]
    nc = lp // CHUNK
    gps = SSD_GPS
    gw = gps * GROUP_W
    cps = _tile(nc, (SSD_CPS, 1))
    ncm = comm.n
    n_in, n_out = 10, 6
    grid = (N_GROUPS // gps, nc // cps)

    def all_groups(refs):
        (dyn_ref, ytot_ref, z_ref, xbc_ref, dt_ref, hprev_ref, a_ref, dsk_ref, wn_ref, tri_ref,
         dz_ref, dxbc_ref, ddt_ref, dd_ref, da_ref, dwn_ref, dh_scr) = refs
        G = range(gps)
        H = range(HEADS_PER_GROUP)
        cl = [slice(gi * GROUP_W, (gi + 1) * GROUP_W) for gi in G]
        hl = [slice(r * HEAD_DIM, (r + 1) * HEAD_DIM) for r in H]
        tri = tri_ref[...]
        xbc = [xbc_ref[:, gi * XBC_BLK:(gi + 1) * XBC_BLK] for gi in G]
        dt = [dt_ref[:, cl[gi]] for gi in G]
        a = [a_ref[:, cl[gi]] for gi in G]
        xs = [xbc[gi][:, :GROUP_W] for gi in G]
        bbf = [xbc[gi][:, GROUP_W:GROUP_W + D_STATE].astype(BF16) for gi in G]
        cbf = [xbc[gi][:, GROUP_W + D_STATE:].astype(BF16) for gi in G]
        hprev = [hprev_ref[gi] for gi in G]
        hbf = [hprev[gi].astype(BF16) for gi in G]
        dhn = [dh_scr[gi] for gi in G]
        dhnb = [dhn[gi].astype(BF16) for gi in G]
        cs = [_dot(tri, dt[gi] * a[gi], NN, HIGHEST) for gi in G]
        g = [_dot(cbf[gi], hbf[gi]) for gi in G]
        dxw = [_dot(bbf[gi], dhnb[gi]) for gi in G]
        cs_t = [_cs_row(cs[gi]) for gi in G]
        dy = []
        for gi in G:
            ytot = ytot_ref[:, cl[gi]]
            z = z_ref[:, cl[gi]]
            dyn = dyn_ref[:, cl[gi]]
            sz = _sigmoid(z)
            silu_z = z * sz
            gz = ytot * silu_z
            rstd = lax.rsqrt(jnp.mean(gz * gz, axis=-1, keepdims=True) + EPS)
            xhat = gz * rstd
            dwn_ref[:, cl[gi]] += jnp.sum(dyn * xhat, axis=0, keepdims=True)
            dxhat = dyn * wn_ref[:, cl[gi]]
            dgz = rstd * (dxhat - xhat * jnp.mean(dxhat * xhat, axis=-1, keepdims=True))
            dy.append(dgz * silu_z)
            dz_ref[:, cl[gi]] = (dgz * ytot * (sz * (1.0 + z * (1.0 - sz)))).astype(BF16)
            dd_ref[:, cl[gi]] += jnp.sum(dy[gi] * xs[gi], axis=0, keepdims=True)
        xdt = [xs[gi] * dt[gi] for gi in G]
        e = [jnp.exp(cs[gi]) for gi in G]
        cs_last = [cs[gi][CHUNK - 1:CHUNK, :] for gi in G]
        dte = [jnp.exp(cs_last[gi] - cs[gi]) for gi in G]
        cd = [jnp.exp(cs_last[gi]) for gi in G]
        dgb = [(dy[gi] * e[gi]).astype(BF16) for gi in G]
        dyb = [dy[gi].astype(BF16) for gi in G]
        xdtb = [xdt[gi].astype(BF16) for gi in G]
        dc = [_dot(dgb[gi], hbf[gi], NT) for gi in G]
        dhprev = [_dot(cbf[gi], dgb[gi], TN) for gi in G]
        db = [_dot((xdt[gi] * dte[gi]).astype(BF16), dhnb[gi], NT) for gi in G]
        row = lax.broadcasted_iota(jnp.int32, (CHUNK, CHUNK), 0)
        causal = row >= lax.broadcasted_iota(jnp.int32, (CHUNK, CHUNK), 1)
        cb = [_dot(cbf[gi], bbf[gi], NT) for gi in G]
        dm = [[_dot(dyb[gi][:, hl[r]], xdtb[gi][:, hl[r]], NT) for r in H] for gi in G]
        mb, dseg, dcbb = [], [], []
        for gi in G:
            mb.append([])
            dseg.append([])
            dcb = None
            for r in H:
                seg = cs[gi][:, r * HEAD_DIM:r * HEAD_DIM + 1] - cs_t[gi][:, hl[r]]
                lm = jnp.exp(jnp.where(causal, seg, -jnp.inf))
                m = cb[gi] * lm
                mb[gi].append(m.astype(BF16))
                dseg[gi].append(dm[gi][r] * m)
                dcb = dm[gi][r] * lm if r == 0 else dcb + dm[gi][r] * lm
            dcbb.append(dcb.astype(BF16))
        dxdt_diag = [[_dot(mb[gi][r], dyb[gi][:, hl[r]], TN) for r in H] for gi in G]
        ones = jnp.ones((CHUNK, HEAD_DIM), F32)
        colsum = [[_dot(dseg[gi][r], ones, TN, HIGHEST) for r in H] for gi in G]
        dc2 = [_dot(dcbb[gi], bbf[gi]) for gi in G]
        db2 = [_dot(dcbb[gi], cbf[gi], TN) for gi in G]
        dcs = []
        for gi in G:
            t_dte = dxw[gi] * xdt[gi] * dte[gi]
            dcs_last = (jnp.sum(dhn[gi] * hprev[gi], axis=0, keepdims=True) * cd[gi]
                        + jnp.sum(t_dte, axis=0, keepdims=True))
            diag = jnp.concatenate(
                [(jnp.sum(dseg[gi][r], axis=1, keepdims=True) - colsum[gi][r]) * (1.0 / HEAD_DIM) for r in H], axis=1)
            d = dy[gi] * g[gi] * e[gi] - t_dte + diag
            row = lax.broadcasted_iota(jnp.int32, d.shape, 0)
            dcs.append(d + jnp.where(row == CHUNK - 1, dcs_last, 0.0))
        dda = [_dot(tri, dcs[gi], TN, HIGHEST) for gi in G]
        for gi in G:
            dxdt = dxw[gi] * dte[gi] + jnp.concatenate(dxdt_diag[gi], axis=1)
            da_ref[:, cl[gi]] += jnp.sum(dda[gi] * dt[gi], axis=0, keepdims=True)
            ddt = dda[gi] * a[gi] + dxdt * xs[gi]
            dxs = dsk_ref[:, cl[gi]] * dy[gi] + dxdt * dt[gi]
            ddt_ref[:, cl[gi]] = ddt * (1.0 - jnp.exp(-dt[gi]))
            dxbc_ref[:, gi * XBC_BLK:(gi + 1) * XBC_BLK] = jnp.concatenate(
                [dxs, db[gi] + db2[gi], dc[gi] + dc2[gi]], axis=1)
            dh_scr[gi] = dhprev[gi] + dhn[gi] * cd[gi]

    def body(*refs):
        ins = refs[:n_in]
        cin = refs[n_in:n_in + ncm]
        outs = refs[n_in + ncm:n_in + ncm + n_out]
        cout = refs[n_in + ncm + n_out:n_in + 2 * ncm + n_out]
        dh_scr = refs[n_in + 2 * ncm + n_out]
        sems = refs[n_in + 2 * ncm + n_out + 1:]
        g, c = pl.program_id(0), pl.program_id(1)

        @pl.when((g == 0) & (c == 0))
        def _():
            comm.start(cin, cout, sems)

        @pl.when(c == 0)
        def _():
            dh_scr[...] = jnp.zeros_like(dh_scr)
            for ref in outs[3:]:
                ref[...] = jnp.zeros_like(ref)

        for sub in reversed(range(cps)):
            rs = pl.ds(sub * CHUNK, CHUNK)
            chunk_ins = tuple(r.at[rs] for r in ins[:5]) + (ins[5].at[sub],) + ins[6:]
            chunk_outs = tuple(r.at[rs] for r in outs[:3]) + outs[3:]
            all_groups(chunk_ins + chunk_outs + (dh_scr,))

        @pl.when((g == grid[0] - 1) & (c == grid[1] - 1))
        def _():
            comm.wait(cin, cout, sems)

    rev = lambda c: grid[1] - 1 - c
    vec = pl.BlockSpec((1, gw), lambda g, c: (0, g))
    blk = pl.BlockSpec((cps * CHUNK, gw), lambda g, c: (rev(c), g))
    xblk = pl.BlockSpec((cps * CHUNK, gps * XBC_BLK), lambda g, c: (rev(c), g))
    res = pl.pallas_call(
        body, name="ssd_bwd", grid=grid,
        in_specs=[blk, blk, blk, xblk, blk,
                  pl.BlockSpec((cps, gps, D_STATE, GROUP_W), lambda g, c: (rev(c), g, 0, 0)),
                  vec, vec, vec,
                  pl.BlockSpec((CHUNK, CHUNK), lambda g, c: (0, 0))] + comm.in_specs,
        out_specs=[blk, xblk, blk, vec, vec, vec] + comm.out_specs,
        out_shape=[
            jax.ShapeDtypeStruct((lp, NP), BF16),
            jax.ShapeDtypeStruct((lp, D_CONV), F32),
            jax.ShapeDtypeStruct((lp, D_SSD), F32),
            jax.ShapeDtypeStruct((1, D_SSD), F32),
            jax.ShapeDtypeStruct((1, D_SSD), F32),
            jax.ShapeDtypeStruct((1, D_SSD), F32),
        ] + comm.out_shape,
        scratch_shapes=[pltpu.VMEM((gps, D_STATE, GROUP_W), F32)] + comm.scratch,
        compiler_params=_params(("arbitrary", "arbitrary"), side_effects=True),
    )(dmix, ytot, proj, xbc, dt_rep, hprev, a_rep, dsk_rep, wn, tri, *comm.operands)
    return res


def _stack_heads(t, h):
    return jnp.concatenate([t[:, (REP * h + r) * HEAD_DIM:(REP * h + r + 1) * HEAD_DIM] for r in range(REP)], axis=0)


def _band(t2, t1, t0, h):
    sl = slice(h * HEAD_DIM, (h + 1) * HEAD_DIM)
    return jnp.concatenate([t2[:, sl], t1[:, sl], t0[:, sl]], axis=0)


def _attn_probs(s, sink, qc):
    s = s * (HEAD_DIM ** -0.5)
    key_abs = (qc - WINDOW_CHUNKS) * CHUNK + lax.broadcasted_iota(jnp.int32, s.shape, 1)
    s = jnp.where(key_abs >= PAD_LEAD, s, -jnp.inf)
    m = jnp.maximum(jnp.max(s, axis=-1, keepdims=True), sink)
    p = jnp.exp(s - m)
    ps = jnp.exp(sink - m)
    denom = jnp.sum(p, axis=-1, keepdims=True) + ps
    return p / denom, ps / denom


ATT_QC_FWD = 6
ATT_QC_BWD = 2


def _kv_specs(width, newest_chunk_of, kc):
    return [pl.BlockSpec((CHUNK, width), functools.partial(lambda j, p: (jnp.maximum(newest_chunk_of(p) - j, 0), 0), j))
            for j in range(kc - 1, -1, -1)]


def _attn_fwd(qr, kr, vb, proj, sink_stack, mix):
    lp = qr.shape[0]
    nc = lp // CHUNK
    qn = _tile(nc, (ATT_QC_FWD, 2, 1))
    kn = WINDOW_CHUNKS + qn
    qrows = qn * CHUNK

    def body(q_ref, *rest):
        k_refs, v_refs = rest[:kn], rest[kn:2 * kn]
        g_ref, sink_ref, _, att_ref, mix_ref = rest[2 * kn:]
        p = pl.program_id(0)
        q = q_ref[...]
        ks = [r[...] for r in k_refs]
        vs = [r[...] for r in v_refs]
        units = [(u, h) for u in range(qn) for h in range(KV_HEADS)]
        s = [_dot(_stack_heads(q[u * CHUNK:(u + 1) * CHUNK, :], h), _band(*ks[u:u + 3], h), NT) for u, h in units]
        vbh = [_band(*vs[u:u + 3], h) for u, h in units]
        pn = [_attn_probs(s[i], sink_ref[h], qn * p + u)[0].astype(BF16) for i, (u, h) in enumerate(units)]
        o = [_dot(pn[i], vbh[i]) for i in range(len(units))]
        att = jnp.concatenate(
            [jnp.concatenate([o[u * KV_HEADS + h][r * CHUNK:(r + 1) * CHUNK, :] for h in range(KV_HEADS) for r in range(REP)],
                             axis=1) for u in range(qn)], axis=0)
        att_ref[...] = att
        g = g_ref[...]
        mix_ref[...] = (att * (g * _sigmoid(g))).astype(BF16)

    newest = lambda p: qn * p + qn - 1
    return pl.pallas_call(
        body, name="attn_fwd", grid=(nc // qn,),
        in_specs=[pl.BlockSpec((qrows, D_ATT), lambda p: (p, 0))] + _kv_specs(D_KV, newest, kn) + _kv_specs(D_KV, newest, kn) + [
            pl.BlockSpec((qrows, D_ATT), lambda p: (p, OFF_G // D_ATT)),
            pl.BlockSpec((KV_HEADS, REP * CHUNK, 1), lambda p: (0, 0, 0)),
            ANY,
        ],
        out_specs=[pl.BlockSpec((qrows, D_ATT), lambda p: (p, 0)),
                   pl.BlockSpec((qrows, D_ATT), lambda p: (p, D_SSD // D_ATT))],
        out_shape=[jax.ShapeDtypeStruct((lp, D_ATT), F32), jax.ShapeDtypeStruct((lp, D_MIX), BF16)],
        input_output_aliases={2 * kn + 3: 1},
        compiler_params=_params(("parallel",)),
    )(qr, *([kr] * kn), *([vb] * kn), proj, sink_stack, mix)


def _attn_bwd(qr, kr, vb, att, proj, dmix, sink_stack):
    lp = qr.shape[0]
    nc = lp // CHUNK
    qn = ATT_QC_BWD
    kn = WINDOW_CHUNKS + qn
    assert nc % qn == 0 and WINDOW_CHUNKS % qn == 0
    steps = nc // qn
    qrows = qn * CHUNK
    wrows = kn * CHUNK

    def body(q_ref, *rest):
        k_refs, v_refs = rest[:kn], rest[kn:2 * kn]
        (att_ref, g_ref, do_ref, sink_ref, dq_ref, dk_ref, dv_ref, dg_ref, dsink_ref, dk_acc, dv_acc) = rest[2 * kn:]
        step = pl.program_id(0)

        @pl.when(step == 0)
        def _():
            dk_acc[...] = jnp.zeros_like(dk_acc)
            dv_acc[...] = jnp.zeros_like(dv_acc)
            dsink_ref[...] = jnp.zeros_like(dsink_ref)

        q = q_ref[...]
        ks = [r[...] for r in k_refs]
        vs = [r[...] for r in v_refs]
        att = att_ref[...]
        g = g_ref[...]
        dog = do_ref[...]
        sg = _sigmoid(g)
        dg_ref[...] = dog * att * (sg * (1.0 + g * (1.0 - sg)))
        do = dog * (g * sg)
        units = [(u, h) for u in range(qn) for h in range(KV_HEADS)]
        n = range(len(units))
        rows = [slice(u * CHUNK, (u + 1) * CHUNK) for u in range(qn)]
        qs = [_stack_heads(q[rows[u], :], h) for u, h in units]
        kb = [_band(*ks[u:u + 3], h) for u, h in units]
        vbh = [_band(*vs[u:u + 3], h) for u, h in units]
        dos = [_stack_heads(do[rows[u], :], h) for u, h in units]
        dosb = [dos[i].astype(BF16) for i in n]
        s = [_dot(qs[i], kb[i], NT) for i in n]
        dp = [_dot(dosb[i], vbh[i], NT) for i in n]
        ds, pnb, dsink = [], [], []
        for i, (u, h) in enumerate(units):
            pn, psink = _attn_probs(s[i], sink_ref[h], qn * jnp.minimum(step, steps - 1) + u)
            delta = jnp.sum(dos[i] * _stack_heads(att[rows[u], :], h), axis=-1, keepdims=True)
            ds.append((pn * (dp[i] - delta)).astype(BF16))
            pnb.append(pn.astype(BF16))
            dsink.append(-(psink * delta))
        dqs = [_dot(ds[i], kb[i]) for i in n]
        dks = [_dot(ds[i], qs[i], TN) for i in n]
        dvs = [_dot(pnb[i], dosb[i], TN) for i in n]
        dq_ref[...] = jnp.concatenate(
            [jnp.concatenate([dqs[u * KV_HEADS + h][r * CHUNK:(r + 1) * CHUNK, :]
                              for h in range(KV_HEADS) for r in range(REP)], axis=1) for u in range(qn)],
            axis=0) * (HEAD_DIM ** -0.5)
        @pl.when(step < steps)
        def _():
            for i, (u, h) in enumerate(units):
                dsink_ref[h] += dsink[i]
            for u in range(qn):
                band = slice(u * CHUNK, u * CHUNK + BAND)
                dk_acc[band, :] += jnp.concatenate(dks[u * KV_HEADS:(u + 1) * KV_HEADS], axis=1) * (HEAD_DIM ** -0.5)
                dv_acc[band, :] += jnp.concatenate(dvs[u * KV_HEADS:(u + 1) * KV_HEADS], axis=1)

        dk_ref[...] = dk_acc[0:qrows, :]
        dv_ref[...] = dv_acc[0:qrows, :]
        for acc in (dk_acc, dv_acc):
            rest_rows = acc[qrows:wrows, :]
            acc[0:wrows - qrows, :] = rest_rows
            acc[wrows - qrows:wrows, :] = jnp.zeros((qrows, D_KV), F32)

    qp = lambda p: jnp.minimum(p, steps - 1)
    newest = lambda p: qn * qp(p) + qn - 1
    qblk = pl.BlockSpec((qrows, D_ATT), lambda p: (qp(p), 0))
    oldest = pl.BlockSpec((qrows, D_KV), lambda p: (jnp.maximum(p - 1, 0), 0))
    return pl.pallas_call(
        body, name="attn_bwd", grid=(steps + 1,),
        in_specs=[qblk] + _kv_specs(D_KV, newest, kn) + _kv_specs(D_KV, newest, kn) + [
            qblk,
            pl.BlockSpec((qrows, D_ATT), lambda p: (qp(p), OFF_G // D_ATT)),
            pl.BlockSpec((qrows, D_ATT), lambda p: (qp(p), D_SSD // D_ATT)),
            pl.BlockSpec((KV_HEADS, REP * CHUNK, 1), lambda p: (0, 0, 0)),
        ],
        out_specs=[qblk, oldest, oldest, qblk, pl.BlockSpec((KV_HEADS, REP * CHUNK, 1), lambda p: (0, 0, 0))],
        out_shape=[
            jax.ShapeDtypeStruct((lp, D_ATT), F32),
            jax.ShapeDtypeStruct((lp, D_KV), F32),
            jax.ShapeDtypeStruct((lp, D_KV), F32),
            jax.ShapeDtypeStruct((lp, D_ATT), F32),
            jax.ShapeDtypeStruct((KV_HEADS, REP * CHUNK, 1), F32),
        ],
        scratch_shapes=[pltpu.VMEM((wrows, D_KV), F32), pltpu.VMEM((wrows, D_KV), F32)],
        compiler_params=_params(("arbitrary",)),
    )(qr, *([kr] * kn), *([vb] * kn), att, proj, dmix, sink_stack)


def _post_loss(out, x2d, target, w):
    lp = out.shape[0]
    nc = lp // CHUNK
    nx = x2d.shape[0] // CHUNK

    k = _tile(nc, (ROW_K, 3, 2, 1))

    def body(o_ref, *rest):
        x_refs, t_refs = rest[:k], rest[k:2 * k]
        w_ref, do_ref, dy_ref, gw_ref, loss_ref = rest[2 * k:]
        i = pl.program_id(0)

        @pl.when(i == 0)
        def _():
            gw_ref[...] = jnp.zeros_like(gw_ref)
            loss_ref[...] = jnp.zeros_like(loss_ref)

        o = o_ref[...]
        w = w_ref[...]
        rstd = lax.rsqrt(jnp.mean(o * o, axis=-1, keepdims=True) + EPS)
        xhat = o * rstd
        x = jnp.concatenate([r[...] for r in x_refs], axis=0)
        t = jnp.concatenate([r[...] for r in t_refs], axis=0)
        chunk = _chunk_of_row(i, k)
        err = (x + xhat * w - t) * ((chunk > 0) & (chunk <= nx)).astype(F32)
        loss_ref[...] += 0.5 * jnp.sum(jnp.mean(err * err, axis=-1, keepdims=True), axis=0, keepdims=True)
        dy = err * (1.0 / D_MODEL)
        dy_ref[...] = dy
        gw_ref[...] += jnp.sum(dy * xhat, axis=0, keepdims=True)
        dxhat = dy * w
        do_ref[...] = (rstd * (dxhat - xhat * jnp.mean(dxhat * xhat, axis=-1, keepdims=True))).astype(BF16)

    row = pl.BlockSpec((k * CHUNK, D_MODEL), lambda i: (i, 0))
    return pl.pallas_call(
        body, name="post_loss", grid=(nc // k,),
        in_specs=[row] + _x_specs(nx, k) + _x_specs(nx, k) + [pl.BlockSpec((1, D_MODEL), lambda i: (0, 0))],
        out_specs=[row, row, pl.BlockSpec((1, D_MODEL), lambda i: (0, 0)), pl.BlockSpec((1, 128), lambda i: (0, 0))],
        out_shape=[
            jax.ShapeDtypeStruct((lp, D_MODEL), BF16),
            jax.ShapeDtypeStruct((lp, D_MODEL), F32),
            jax.ShapeDtypeStruct((1, D_MODEL), F32),
            jax.ShapeDtypeStruct((1, 128), F32),
        ],
        compiler_params=_params(("arbitrary",)),
    )(out, *([x2d] * k), *([target] * k), w)


def _prenorm_bwd(dhn, x2d, h0, dy, w):
    nx = x2d.shape[0] // CHUNK
    k = _tile(nx, (X_K, 4, 2, 1))
    rows = k * CHUNK

    def backward(h, dhn, w):
        rstd = lax.rsqrt(jnp.mean(h * h, axis=-1, keepdims=True) + EPS)
        xhat = h * rstd
        dxhat = dhn * w
        dh = rstd * (dxhat - xhat * jnp.mean(dxhat * xhat, axis=-1, keepdims=True))
        return dh, jnp.sum(dhn * xhat, axis=0, keepdims=True)

    def body(*refs):
        dhn_refs, dy_refs = refs[:k], refs[k:2 * k]
        x_ref, w_ref, gx_ref, gw_ref = refs[2 * k:]

        @pl.when(pl.program_id(0) == 0)
        def _():
            gw_ref[...] = jnp.zeros_like(gw_ref)

        dh, gw = backward(x_ref[...], jnp.concatenate([r[...] for r in dhn_refs], axis=0), w_ref[...])
        gx_ref[...] = dh + jnp.concatenate([r[...] for r in dy_refs], axis=0)
        gw_ref[...] += gw

    def body_meta(dhn_ref, h0_ref, w_ref, d0_ref, gw_ref):
        d0_ref[...], gw_ref[...] = backward(h0_ref[...], dhn_ref[...], w_ref[...])

    chunk_specs = [pl.BlockSpec((CHUNK, D_MODEL), functools.partial(lambda u, i: (k * i + 1 + u, 0), u)) for u in range(k)]
    first = pl.BlockSpec((CHUNK, D_MODEL), lambda i: (0, 0))
    vec = pl.BlockSpec((1, D_MODEL), lambda i: (0, 0))
    wide = pl.BlockSpec((rows, D_MODEL), lambda i: (i, 0))
    gx, gw_x = pl.pallas_call(
        body, name="prenorm_bwd", grid=(nx // k,),
        in_specs=chunk_specs + chunk_specs + [wide, vec],
        out_specs=[wide, vec],
        out_shape=[jax.ShapeDtypeStruct((nx * CHUNK, D_MODEL), F32), jax.ShapeDtypeStruct((1, D_MODEL), F32)],
        compiler_params=_params(("arbitrary",)),
    )(*([dhn] * k), *([dy] * k), x2d, w)
    d0, gw_0 = pl.pallas_call(
        body_meta, name="prenorm_bwd_meta", grid=(1,),
        in_specs=[first, first, vec], out_specs=[first, vec],
        out_shape=[jax.ShapeDtypeStruct((CHUNK, D_MODEL), F32), jax.ShapeDtypeStruct((1, D_MODEL), F32)],
        compiler_params=_params(("arbitrary",)),
    )(dhn, h0, w)
    return gx, d0, gw_x + gw_0


def _adamw(slabs, w, m, v, name):
    rows, cols = w.shape
    tr = _tile(rows, (256, 128, 64, 16, 8))
    c1 = 1.0 - ADAM_B1 ** ADAM_STEP
    c2 = 1.0 - ADAM_B2 ** ADAM_STEP

    def body(s_ref, w_ref, m_ref, v_ref, g_ref, d_ref, mo_ref, vo_ref):
        g = s_ref[0].astype(F32)
        for k in range(1, slabs.shape[0]):
            g = g + s_ref[k].astype(F32)
        w = w_ref[...]
        m = ADAM_B1 * m_ref[...] + (1.0 - ADAM_B1) * g
        v = ADAM_B2 * v_ref[...] + (1.0 - ADAM_B2) * (g * g)
        g_ref[...] = g
        mo_ref[...] = m
        vo_ref[...] = v
        d_ref[...] = -ADAM_LR * ((m / c1) / (jnp.sqrt(v / c2) + ADAM_EPS) + ADAM_WD * w)

    blk = pl.BlockSpec((tr, cols), lambda i: (i, 0))
    shape = jax.ShapeDtypeStruct((rows, cols), F32)
    return pl.pallas_call(
        body, name=name, grid=(rows // tr,),
        in_specs=[pl.BlockSpec((slabs.shape[0], tr, cols), lambda i: (0, i, 0)), blk, blk, blk],
        out_specs=[blk, blk, blk, blk],
        out_shape=[shape, shape, shape, shape],
        compiler_params=_params(("parallel",)),
    )(slabs, w, m, v)


def _perm_xbc(a):
    lead = a.shape[:-1]
    xs = a[..., :D_SSD].reshape(lead + (N_GROUPS, GROUP_W))
    b = a[..., D_SSD:D_SSD + N_GROUPS * D_STATE].reshape(lead + (N_GROUPS, D_STATE))
    c = a[..., D_SSD + N_GROUPS * D_STATE:].reshape(lead + (N_GROUPS, D_STATE))
    return jnp.concatenate([xs, b, c], axis=-1).reshape(lead + (D_CONV,))


def _unperm_xbc(a):
    lead = a.shape[:-1]
    t = a.reshape(lead + (N_GROUPS, XBC_BLK))
    xs = t[..., :GROUP_W].reshape(lead + (D_SSD,))
    b = t[..., GROUP_W:GROUP_W + D_STATE].reshape(lead + (N_GROUPS * D_STATE,))
    c = t[..., GROUP_W + D_STATE:].reshape(lead + (N_GROUPS * D_STATE,))
    return jnp.concatenate([xs, b, c], axis=-1)


R_Z, R_XBC, R_DT, R_Q, R_K, R_V, R_G = 0, 2048, 6144, 6176, 7200, 7456, 7712


def _internal_of_reference():
    ref = np.arange(D_IN_PROJ)
    out = np.empty(D_IN_PROJ, np.int64)
    out[R_Z:R_XBC] = OFF_Z + ref[:D_SSD]
    xs = np.arange(D_SSD)
    out[R_XBC:R_XBC + D_SSD] = OFF_XBC + (xs // GROUP_W) * XBC_BLK + xs % GROUP_W
    bc = np.arange(N_GROUPS * D_STATE)
    out[R_XBC + D_SSD:R_XBC + D_SSD + N_GROUPS * D_STATE] = OFF_XBC + (bc // D_STATE) * XBC_BLK + GROUP_W + bc % D_STATE
    out[R_XBC + D_SSD + N_GROUPS * D_STATE:R_DT] = OFF_XBC + (bc // D_STATE) * XBC_BLK + GROUP_W + D_STATE + bc % D_STATE
    out[R_DT:R_Q] = OFF_DT + np.arange(SSD_HEADS)
    out[R_Q:R_K] = OFF_Q + np.arange(D_ATT)
    out[R_K:R_V] = OFF_K + np.arange(D_KV)
    out[R_V:R_G] = OFF_V + np.arange(D_KV)
    out[R_G:] = OFF_G + np.arange(D_ATT)
    return out


def _runs(src, dst_break):
    runs, lo = [], 0
    for i in range(1, len(src) + 1):
        if i == len(src) or src[i] != src[i - 1] + 1 or dst_break[i] != dst_break[i - 1]:
            runs.append((lo, i))
            lo = i
    return runs


RELAYOUT_ROWS = 256


def _lane_window(ref, lead, c0, n):
    a0 = c0 // 128 * 128
    a1 = min(-(-(c0 + n) // 128) * 128, ref.shape[-1])
    return ref[lead + (slice(None), slice(a0, a1))][:, c0 - a0:c0 - a0 + n]


def _gather_w_in(w_shard, small, x2d, norm_w):
    int_of_ref = _internal_of_reference()
    ref_of_int = np.full(NP, -1, np.int64)
    ref_of_int[int_of_ref] = np.arange(D_IN_PROJ)
    shard = np.where(ref_of_int >= 0, ref_of_int // SHARD_IN, -1)
    src = np.where(ref_of_int >= 0, ref_of_int, -10 - 2 * np.arange(NP))
    plan, zeros = [], 0
    for lo, hi in _runs(src, shard):
        if ref_of_int[lo] < 0:
            zeros += hi - lo
            continue
        if zeros:
            plan.append((None, 0, zeros))
            zeros = 0
        plan.append((int(ref_of_int[lo] // SHARD_IN), int(ref_of_int[lo] % SHARD_IN), hi - lo))
    if zeros:
        plan.append((None, 0, zeros))
    tr = RELAYOUT_ROWS
    steps = D_MODEL // tr
    side = _GatherTwoLevel(small)
    ns = side.n
    nx = x2d.shape[0] // CHUNK
    nc = nx + 2
    k = _tile(nc, (ROW_K, 3, 2, 1))
    pn = nc // k
    n = max(pn, steps)
    tile_of = lambda i: jnp.where(i < pn - 1, i + 1, 0)

    def body(w_ref, *rest):
        s_in, x_refs, nw_ref = rest[:ns], rest[ns:ns + k], rest[ns + k]
        o_ref, land_ref, s_out, hn_ref = rest[ns + k + 1], rest[ns + k + 2], rest[ns + k + 3:2 * ns + k + 3], rest[2 * ns + k + 3]
        tile_buf, send_sems, recv_sems, load_sems, local_sem, meta_buf, meta_sem = rest[2 * ns + k + 4:2 * ns + k + 11]
        side_sems = rest[2 * ns + k + 11:]
        i = pl.program_id(0)
        x, y, c = lax.axis_index("x"), lax.axis_index("y"), lax.axis_index("c")
        me, sibling = (x, y, c), (x, y, 1 - c)
        chips = [(1 - x, y), (x, 1 - y), (1 - x, 1 - y)]

        def prenorm_tile(t, head=None):
            h = jnp.concatenate([r[...] for r in x_refs], axis=0)
            if head is not None:
                h = jnp.concatenate([head, h[CHUNK:, :]], axis=0)
            h = h * (_chunk_of_row(t, k) <= nx).astype(F32)
            rstd = lax.rsqrt(jnp.mean(h * h, axis=-1, keepdims=True) + EPS)
            hn_ref[...] = (h * rstd * nw_ref[...]).astype(BF16)

        if pn > 1:
            @pl.when(i < pn - 1)
            def _():
                prenorm_tile(i + 1)

        def rows(t):
            return pl.ds(t * tr, tr)

        def tile(place, t):
            return land_ref.at[4 * place[0] + 2 * place[1] + place[2], rows(t), :]

        def copy(t, k, block, to, src=None):
            return pltpu.make_async_remote_copy(
                src_ref=tile(block, t) if src is None else src, dst_ref=tile(block, t),
                send_sem=send_sems.at[t, k], recv_sem=recv_sems.at[t, k], device_id=to, device_id_type=pl.DeviceIdType.MESH)

        def own_sends(t):
            mine = w_ref.at[rows(t), :]
            return [copy(t, 0, me, sibling, src=mine)] + [copy(t, 1 + j, me, (*chip, c), src=mine)
                                                          for j, chip in enumerate(chips)]

        own = pltpu.make_async_copy(w_ref, land_ref.at[4 * x + 2 * y + c], local_sem)

        @pl.when(i < steps)
        def _():
            @pl.when(i == 0)
            def _():
                side.start(s_in, s_out, side_sems)
                own.start()
                for t in range(steps):
                    for cp in own_sends(t):
                        cp.start()
                own.wait()

            forwards = []
            for j, chip in enumerate(chips):
                copy(i, 1 + j, (*chip, c), me).wait_recv()
                forwards.append(copy(i, 4 + j, (*chip, c), sibling))
                forwards[j].start()
            copy(i, 0, sibling, me).wait_recv()
            for j, chip in enumerate(chips):
                copy(i, 4 + j, (*chip, 1 - c), me).wait_recv()
            loads = [pltpu.make_async_copy(land_ref.at[s, rows(i), :], tile_buf.at[s], load_sems.at[s]) for s in range(N_DEV)]
            for cp in loads:
                cp.start()
            for cp in loads:
                cp.wait()
            o_ref[...] = jnp.concatenate(
                [jnp.zeros((tr, w), o_ref.dtype) if s is None else _lane_window(tile_buf, (s,), c0, w) for s, c0, w in plan],
                axis=1)
            for cp in forwards:
                cp.wait_send()

            @pl.when(i == steps - 1)
            def _():
                for t in range(steps):
                    for cp in own_sends(t):
                        cp.wait_send()
                side.wait(s_in, s_out, side_sems)

        @pl.when(i == n - 1)
        def _():
            meta = pltpu.make_async_copy(s_out[ns - 1], meta_buf, meta_sem)
            meta.start()
            meta.wait()
            tokens = jnp.concatenate([meta_buf[s] for s in range(N_DEV)], axis=1)
            prenorm_tile(0, jnp.concatenate([jnp.zeros((PAD_LEAD, D_MODEL), F32), tokens], axis=0))

    last = steps - 1
    res = pl.pallas_call(
        body, name="gather_w_in", grid=(n,),
        in_specs=[ANY] + side.in_specs + _x_specs(nx, k, tile_of) + [pl.BlockSpec((1, D_MODEL), lambda i: (0, 0))],
        out_specs=[pl.BlockSpec((tr, NP), lambda i: (jnp.minimum(i, last), 0)), ANY] + side.out_specs + [
            pl.BlockSpec((k * CHUNK, D_MODEL), lambda i: (tile_of(i), 0))],
        out_shape=[jax.ShapeDtypeStruct((D_MODEL, NP), w_shard.dtype),
                   jax.ShapeDtypeStruct((N_DEV, D_MODEL, SHARD_IN), w_shard.dtype)] + side.out_shape + [
            jax.ShapeDtypeStruct((nc * CHUNK, D_MODEL), BF16)],
        scratch_shapes=[pltpu.VMEM((N_DEV, tr, SHARD_IN), w_shard.dtype),
                        pltpu.SemaphoreType.DMA((steps, N_DEV - 1)), pltpu.SemaphoreType.DMA((steps, N_DEV - 1)),
                        pltpu.SemaphoreType.DMA((N_DEV,)), pltpu.SemaphoreType.DMA(()),
                        pltpu.VMEM((N_DEV,) + tuple(small[ns - 1].shape), F32),
                        pltpu.SemaphoreType.DMA(())] + side.scratch,
        compiler_params=_params(("arbitrary",), side_effects=True),
    )(w_shard, *side.operands, *([x2d] * k), norm_w)
    return [res[0]] + list(res[2:])


def _w_in_chip_slabs(dw):
    int_of_ref = _internal_of_reference()
    plan = []
    for s in range(N_DEV):
        cols = int_of_ref[s * SHARD_IN:(s + 1) * SHARD_IN]
        plan.append([(int(cols[lo]), hi - lo) for lo, hi in _runs(cols, np.zeros_like(cols))])
    tr = RELAYOUT_ROWS
    steps = D_MODEL // tr
    pairs = N_DEV // 2

    def body(dw_ref, o_ref, land_ref, send_buf, land_buf, send_sems, recv_sems, load_sems):
        i = pl.program_id(0)
        x, y, c = lax.axis_index("x"), lax.axis_index("y"), lax.axis_index("c")
        slot = i % 2

        def sends(step, sl):
            return [pltpu.make_async_remote_copy(
                src_ref=send_buf.at[sl, 2 * q + (1 - c)], dst_ref=land_ref.at[q, pl.ds(step * tr, tr), :],
                send_sem=send_sems.at[sl, q], recv_sem=recv_sems.at[step, q],
                device_id=(x, y, 1 - c), device_id_type=pl.DeviceIdType.MESH) for q in range(pairs)]

        @pl.when(i < steps)
        def _():
            @pl.when(i >= 2)
            def _():
                for cp in sends(i - 2, slot):
                    cp.wait_send()

            for s in range(N_DEV):
                send_buf[slot, s] = jnp.concatenate([_lane_window(dw_ref, (), c0, n) for c0, n in plan[s]], axis=1)
            for cp in sends(i, slot):
                cp.start()

        @pl.when(i >= 1)
        def _():
            loads = []
            for q, cp in enumerate(sends(i - 1, 1 - slot)):
                cp.wait_recv()
                loads.append(pltpu.make_async_copy(land_ref.at[q, pl.ds((i - 1) * tr, tr), :], land_buf.at[q], load_sems.at[q]))
                loads[q].start()
            for q in range(pairs):
                loads[q].wait()
                o_ref[q] = (send_buf[1 - slot, 2 * q + c].astype(F32) + land_buf[q].astype(F32)).astype(o_ref.dtype)

        @pl.when(i == steps)
        def _():
            for cp in sends(steps - 2, steps % 2) + sends(steps - 1, 1 - steps % 2):
                cp.wait_send()

    last = steps - 1
    return pl.pallas_call(
        body, name="dw_in_relayout", grid=(steps + 1,),
        in_specs=[pl.BlockSpec((tr, NP), lambda i: (jnp.minimum(i, last), 0))],
        out_specs=[pl.BlockSpec((pairs, tr, SHARD_IN), lambda i: (0, jnp.maximum(i - 1, 0), 0)), ANY],
        out_shape=[jax.ShapeDtypeStruct((pairs, D_MODEL, SHARD_IN), dw.dtype),
                   jax.ShapeDtypeStruct((pairs, D_MODEL, SHARD_IN), dw.dtype)],
        scratch_shapes=[pltpu.VMEM((2, N_DEV, tr, SHARD_IN), dw.dtype), pltpu.VMEM((pairs, tr, SHARD_IN), dw.dtype),
                        pltpu.SemaphoreType.DMA((2, pairs)), pltpu.SemaphoreType.DMA((steps, pairs)),
                        pltpu.SemaphoreType.DMA((pairs,))],
        compiler_params=_params(("arbitrary",), side_effects=True),
    )(dw)[0]


def _rep_heads(a):
    return jnp.repeat(a, HEAD_DIM, axis=1)


SMALL = (("norm_pre_w", 2048), ("conv_b", 4096), ("dt_bias", 32), ("a_log", 32), ("d_skip", 32),
         ("ssd_norm_w", 2048), ("attn_sinks", 16), ("norm_post_w", 2048))
SMALL_USED = sum(size for _, size in SMALL)
SMALL_LEN = 10368


def _pack_small(d, loss=None):
    parts = [d[name].reshape(1, size) for name, size in SMALL]
    tail = jnp.zeros((1, SMALL_LEN - SMALL_USED), F32)
    if loss is not None:
        tail = tail.at[0, 0].set(loss)
    return jnp.concatenate(parts + [tail], axis=1)


def _unpack_small(vec):
    out, off = {}, 0
    for name, size in SMALL:
        out[name] = vec[:, off:off + size]
        off += size
    return out


def kernel(x, meta_tokens, norm_pre_w, w_in, conv_w, conv_b, dt_bias, a_log, d_skip, ssd_norm_w, attn_sinks, w_out, norm_post_w, loss_target, m_meta_tokens, m_norm_pre_w, m_w_in, m_conv_w, m_conv_b, m_dt_bias, m_a_log, m_d_skip, m_ssd_norm_w, m_attn_sinks, m_w_out, m_norm_post_w, v_meta_tokens, v_norm_pre_w, v_w_in, v_conv_w, v_conv_b, v_dt_bias, v_a_log, v_d_skip, v_ssd_norm_w, v_attn_sinks, v_w_out, v_norm_post_w):
    seq = x.shape[1]
    lp = seq + 2 * CHUNK
    x2d = x[0]

    w_all, conv_w_g, meta_g, hn = _gather_w_in(
        w_in[0].astype(BF16), [conv_w[0], meta_tokens], x2d, norm_pre_w)
    conv_w_full = _perm_xbc(jnp.transpose(conv_w_g, (1, 0, 2)).reshape(CONV_WIDTH, D_CONV))
    conv_b_int = _perm_xbc(conv_b)
    meta_full = jnp.transpose(meta_g, (1, 0, 2)).reshape(N_META, D_MODEL)
    h0 = jnp.concatenate([jnp.zeros((PAD_LEAD, D_MODEL), F32), meta_full], axis=0)

    pos = (jnp.arange(lp) - PAD_LEAD).astype(F32)
    half = HEAD_DIM // 2
    inv = ROPE_THETA ** (-jnp.arange(half, dtype=F32) / half)
    ang = pos[:, None] * inv[None, :]
    cos_t = jnp.tile(jnp.cos(ang), (1, 4))
    sin_t = jnp.tile(jnp.concatenate([-jnp.sin(ang), jnp.sin(ang)], axis=1), (1, 2))
    head_of_col = np.arange(D_SSD) // HEAD_DIM
    expand = jnp.asarray((np.arange(128)[:, None] == head_of_col[None, :]).astype(np.float32))
    reduce_t = jnp.asarray((head_of_col[:, None] == np.arange(128)[None, :]).astype(np.float32))
    tri = jnp.asarray(np.tril(np.ones((CHUNK, CHUNK), np.float32)))
    a_rep = _rep_heads(-jnp.exp(a_log))
    dsk_rep = _rep_heads(d_skip)
    dt_bias_pad = jnp.pad(dt_bias, ((0, 0), (0, 128 - SSD_HEADS)))
    sink_stack = jnp.repeat(attn_sinks.reshape(KV_HEADS, REP), CHUNK, axis=1).reshape(KV_HEADS, REP * CHUNK, 1)

    tm = _tile(lp, (1056, 704, 128, 64))
    proj, w_out_g = _matmul(hn, w_all, tm=tm, tn=1536, tk=D_MODEL, out_dtype=F32, name="in_proj",
                            comm=_Comm([(w_out[0].astype(BF16), "gather")]))
    w_out_full = w_out_g.reshape(D_MIX, D_MODEL)
    qr, kr, vb, dt_rep = _act_fwd(proj, cos_t, sin_t, expand, dt_bias_pad)
    mix, ytot, hprev, xbc = _ssd_fwd(proj, conv_w_full, conv_b_int, dt_rep, a_rep, dsk_rep, ssd_norm_w, tri)
    att, mix = _attn_fwd(qr, kr, vb, proj, sink_stack, mix)
    out = _matmul(mix, w_out_full, tm=tm, tn=1024, tk=D_MIX, out_dtype=F32, name="out_proj")
    dout, dy, g_norm_post, loss_part = _post_loss(out, x2d, loss_target[0], norm_post_w)

    dmix = _matmul(dout, w_out_full, trans_b=True, tm=tm, tn=1024, tk=D_MODEL, out_dtype=F32, name="dmix")
    dw_out = _matmul(mix, dout, trans_a=True, tm=512, tn=1024, tk=lp, out_dtype=BF16, name="dw_out")
    dqr, dkr, dv, dg, dsink_rows = _attn_bwd(qr, kr, vb, att, proj, dmix, sink_stack)
    dproj, dxbc, ddt_part, dd_part, da_part, g_ssd_norm, g_out = _ssd_bwd(
        dmix, ytot, proj, xbc, dt_rep, hprev, a_rep, dsk_rep, ssd_norm_w, tri,
        _Comm([(dw_out.reshape(N_DEV, D_MIX // N_DEV, D_MODEL), "scatter")]))
    dproj, dconv_w_int, dconv_b_int = _conv_bwd(dxbc, proj, conv_w_full, conv_b_int, dproj)
    dproj, ddt_bias = _act_bwd(dqr, dkr, dv, dg, ddt_part, cos_t, sin_t, reduce_t, dproj)
    dw_all = _matmul(hn, dproj, trans_a=True, tm=512, tn=1024, tk=lp, out_dtype=BF16, name="dw_in")
    dw_chip = _w_in_chip_slabs(dw_all)
    dhn, g_in = _matmul(dproj, w_all, trans_b=True, tm=tm, tn=1024, tk=1536, out_dtype=F32, name="dhn",
                        comm=_Comm([(dw_chip, "scatter")], scope="chips"))
    grad_x, dh0, g_norm_pre = _prenorm_bwd(dhn, x2d, h0, dy, norm_pre_w)

    dmeta = dh0[PAD_LEAD:, :]
    dconv_w_ref = _unperm_xbc(dconv_w_int)
    heads = lambda part: part.reshape(SSD_HEADS, HEAD_DIM).sum(axis=1).reshape(1, SSD_HEADS)
    small_local = _pack_small({
        "norm_pre_w": g_norm_pre, "conv_b": _unperm_xbc(dconv_b_int), "dt_bias": ddt_bias[:, :SSD_HEADS],
        "a_log": heads(da_part) * (-jnp.exp(a_log)), "d_skip": heads(dd_part), "ssd_norm_w": g_ssd_norm,
        "attn_sinks": dsink_rows.reshape(Q_HEADS, CHUNK).sum(axis=1).reshape(1, Q_HEADS),
        "norm_post_w": g_norm_post}, loss=loss_part[0, 0])
    g_conv, g_meta, g_small = _exchange(
        [(jnp.transpose(dconv_w_ref.reshape(CONV_WIDTH, N_DEV, D_CONV // N_DEV), (1, 0, 2)), "scatter"),
         (jnp.transpose(dmeta.reshape(N_META, N_DEV, D_MODEL // N_DEV), (1, 0, 2)), "scatter"),
         (small_local, "gather")], "exchange_small")

    res = {}
    res["w_in"] = [o[None] for o in _adamw(g_in, w_in[0], m_w_in[0], v_w_in[0], "adamw_w_in")]
    res["w_out"] = [o[None] for o in _adamw(g_out, w_out[0], m_w_out[0], v_w_out[0], "adamw_w_out")]
    res["conv_w"] = [o[None] for o in _adamw(g_conv, conv_w[0], m_conv_w[0], v_conv_w[0], "adamw_conv_w")]
    res["meta_tokens"] = _adamw(g_meta, meta_tokens, m_meta_tokens, v_meta_tokens, "adamw_meta")
    given = dict(norm_pre_w=(norm_pre_w, m_norm_pre_w, v_norm_pre_w), conv_b=(conv_b, m_conv_b, v_conv_b),
                 dt_bias=(dt_bias, m_dt_bias, v_dt_bias), a_log=(a_log, m_a_log, v_a_log),
                 d_skip=(d_skip, m_d_skip, v_d_skip), ssd_norm_w=(ssd_norm_w, m_ssd_norm_w, v_ssd_norm_w),
                 attn_sinks=(attn_sinks, m_attn_sinks, v_attn_sinks),
                 norm_post_w=(norm_post_w, m_norm_post_w, v_norm_post_w))
    packed = [_pack_small({k: t[j] for k, t in given.items()}) for j in range(3)]
    small_out = _adamw(g_small, packed[0], packed[1], packed[2], "adamw_small")
    small_res = [_unpack_small(r) for r in small_out]
    loss = small_out[0][0, SMALL_USED]

    order = ["meta_tokens", "norm_pre_w", "w_in", "conv_w", "conv_b", "dt_bias", "a_log", "d_skip", "ssd_norm_w",
             "attn_sinks", "w_out", "norm_post_w"]
    outs = []
    for j in range(4):
        for name in order:
            outs.append(res[name][j] if name in res else small_res[j][name])
    return (loss, grad_x[None], *outs)
```

```python
import functools

import numpy as np
import jax
import jax.numpy as jnp
from jax import lax
from jax.experimental import pallas as pl
from jax.experimental.pallas import tpu as pltpu

F32 = jnp.float32
BF16 = jnp.bfloat16
HIGHEST = lax.Precision.HIGHEST

N_DEV = 8
D_MODEL = 2048
CHUNK = 64
N_META = 16
PAD_LEAD = CHUNK - N_META
EPS = 1e-6
N_GROUPS = 8
HEADS_PER_GROUP = 4
HEAD_DIM = 64
GROUP_W = HEADS_PER_GROUP * HEAD_DIM
D_STATE = 128
D_SSD = 2048
D_CONV = 4096
SSD_HEADS = 32
CONV_WIDTH = 4
Q_HEADS = 16
KV_HEADS = 4
REP = 4
D_ATT = 1024
D_KV = 256
WINDOW_CHUNKS = 2
BAND = (WINDOW_CHUNKS + 1) * CHUNK
ROPE_THETA = 10000.0
D_MIX = D_SSD + D_ATT
D_IN_PROJ = 8736
SHARD_IN = D_IN_PROJ // N_DEV

OFF_Z, OFF_XBC, OFF_Q, OFF_G, OFF_K, OFF_V, OFF_DT = 0, 2048, 6144, 7168, 8192, 8448, 8704
NP = 9216
TAIL_W = NP - OFF_Q
XBC_BLK = 512
SSD_GPS = 8
SSD_CPS = 2

ADAM_LR, ADAM_B1, ADAM_B2, ADAM_EPS, ADAM_WD, ADAM_STEP = 0.001, 0.9, 0.999, 1e-08, 0.01, 10

VMEM_LIMIT = 48 * 1024 * 1024

NN = (((1,), (0,)), ((), ()))
NT = (((1,), (1,)), ((), ()))
TN = (((0,), (0,)), ((), ()))
ANY = pl.BlockSpec(memory_space=pl.ANY)


def _dot(a, b, dims=NN, precision=None):
    return lax.dot_general(a, b, dims, precision=precision, preferred_element_type=F32)


def _tile(n, prefs):
    for t in prefs:
        if n % t == 0:
            return t
    return n


def _params(sem, vmem=VMEM_LIMIT, side_effects=False):
    return pltpu.CompilerParams(dimension_semantics=sem, vmem_limit_bytes=vmem, has_side_effects=side_effects)


def _sigmoid(x):
    return 1.0 / (1.0 + jnp.exp(-x))


class _Comm:
    def __init__(self, items, scope="devices"):
        self.items = items
        self.scope = scope
        self.slabs = slabs = N_DEV if scope == "devices" else N_DEV // 2
        self.n = n = len(items)
        self.operands = [arr for arr, _ in items]
        self.in_specs = [ANY] * n
        self.out_specs = [ANY] * n
        self.out_shape = [jax.ShapeDtypeStruct((slabs,) + tuple(arr.shape) if kind == "gather" else tuple(arr.shape),
                                               arr.dtype) for arr, kind in items]
        self.scratch = [pltpu.SemaphoreType.DMA((n, slabs - 1)), pltpu.SemaphoreType.DMA((n, slabs - 1)),
                        pltpu.SemaphoreType.DMA((n,))]

    def _places(self):
        pos = (lax.axis_index("x"), lax.axis_index("y"), lax.axis_index("c"))
        if self.scope == "devices":
            index = lambda p: 4 * p[0] + 2 * p[1] + p[2]
            masks = range(1, N_DEV)
        else:
            index = lambda p: 2 * p[0] + p[1]
            masks = (2, 4, 6)
        peers = []
        for k in masks:
            p = tuple(1 - pos[b] if (k >> (2 - b)) & 1 else pos[b] for b in range(3))
            peers.append((p, index(p)))
        return index(pos), peers

    def _copies(self, ins, outs, sems, landed):
        send_sems, recv_sems, local_sems = sems
        me, peers = self._places()
        local, remote = [], []
        for a, (_, kind) in enumerate(self.items):
            own = ins[a] if kind == "gather" else ins[a].at[me]
            local.append(pltpu.make_async_copy(own, outs[a].at[me], local_sems.at[a]))
            for k, (p, pid) in enumerate(peers):
                remote.append(pltpu.make_async_remote_copy(
                    src_ref=ins[a] if kind == "gather" else ins[a].at[pid],
                    dst_ref=outs[a].at[pid if landed else me],
                    send_sem=send_sems.at[a, k], recv_sem=recv_sems.at[a, k],
                    device_id=p, device_id_type=pl.DeviceIdType.MESH))
        return local, remote

    def start(self, ins, outs, sems):
        local, remote = self._copies(ins, outs, sems, landed=False)
        for cp in local + remote:
            cp.start()

    def wait(self, ins, outs, sems):
        local, remote = self._copies(ins, outs, sems, landed=True)
        for cp in remote + local:
            cp.wait()


def _exchange(items, name):
    comm = _Comm(items)
    n = comm.n

    def body(*refs):
        ins, outs, sems = refs[:n], refs[n:2 * n], refs[2 * n:]
        comm.start(ins, outs, sems)
        comm.wait(ins, outs, sems)

    return pl.pallas_call(
        body, name=name, in_specs=comm.in_specs, out_specs=comm.out_specs, out_shape=comm.out_shape,
        scratch_shapes=comm.scratch, compiler_params=pltpu.CompilerParams(has_side_effects=True),
    )(*comm.operands)


class _GatherTwoLevel:
    def __init__(self, arrays):
        self.arrays = arrays
        self.n = n = len(arrays)
        self.operands = list(arrays)
        self.in_specs = [ANY] * n
        self.out_specs = [ANY] * n
        self.out_shape = [jax.ShapeDtypeStruct((N_DEV,) + tuple(a.shape), a.dtype) for a in arrays]
        self.scratch = [pltpu.SemaphoreType.DMA((n, N_DEV - 1)), pltpu.SemaphoreType.DMA((n, N_DEV - 1)),
                        pltpu.SemaphoreType.DMA((n,))]

    def _plan(self, ins, outs, sems):
        send_sems, recv_sems, local_sems = sems
        x, y, c = lax.axis_index("x"), lax.axis_index("y"), lax.axis_index("c")
        me, sibling = (x, y, c), (x, y, 1 - c)
        chips = [(1 - x, y), (x, 1 - y), (1 - x, 1 - y)]

        def slab(a, place):
            return outs[a].at[4 * place[0] + 2 * place[1] + place[2]]

        def copy(a, k, block, to, src=None):
            return pltpu.make_async_remote_copy(
                src_ref=slab(a, block) if src is None else src, dst_ref=slab(a, block),
                send_sem=send_sems.at[a, k], recv_sem=recv_sems.at[a, k],
                device_id=to, device_id_type=pl.DeviceIdType.MESH)

        own, mine = [], []
        for a in range(self.n):
            mine.append(pltpu.make_async_copy(ins[a], slab(a, me), local_sems.at[a]))
            own.append(copy(a, 0, me, sibling, src=ins[a]))
            own += [copy(a, 1 + j, me, (*chip, c), src=ins[a]) for j, chip in enumerate(chips)]
        return me, sibling, chips, c, copy, own, mine

    def start(self, ins, outs, sems):
        _, _, _, _, _, own, mine = self._plan(ins, outs, sems)
        for cp in mine + own:
            cp.start()

    def wait(self, ins, outs, sems):
        me, sibling, chips, c, copy, own, mine = self._plan(ins, outs, sems)
        forwards = []
        for j, chip in enumerate(chips):
            for a in range(self.n):
                copy(a, 1 + j, (*chip, c), me).wait_recv()
                fwd = copy(a, 4 + j, (*chip, c), sibling)
                fwd.start()
                forwards.append(fwd)
        for a in range(self.n):
            copy(a, 0, sibling, me).wait_recv()
            for j, chip in enumerate(chips):
                copy(a, 4 + j, (*chip, 1 - c), me).wait_recv()
        for cp in own + forwards:
            cp.wait_send()
        for loc in mine:
            loc.wait()


def _matmul(a, b, *, tm, tn, tk, out_dtype, name, trans_a=False, trans_b=False, comm=None):
    m, k = (a.shape[1], a.shape[0]) if trans_a else a.shape
    n = b.shape[0] if trans_b else b.shape[1]
    nk = k // tk
    dims = TN if trans_a else (NT if trans_b else NN)
    assert not (trans_a and trans_b)
    nc = comm.n if comm else 0
    grid = (m // tm, n // tn, nk)

    def body(*refs):
        a_ref, b_ref = refs[:2]
        cin = refs[2:2 + nc]
        o_ref = refs[2 + nc]
        cout = refs[3 + nc:3 + 2 * nc]
        scratch = refs[3 + 2 * nc:]
        sems = scratch[len(scratch) - 3:] if comm else None
        i, j, kk = pl.program_id(0), pl.program_id(1), pl.program_id(2)
        if comm:
            @pl.when((i == 0) & (j == 0) & (kk == 0))
            def _():
                comm.start(cin, cout, sems)

        if nk == 1:
            o_ref[...] = _dot(a_ref[...], b_ref[...], dims).astype(out_dtype)
        else:
            acc_ref = scratch[0]

            @pl.when(kk == 0)
            def _():
                acc_ref[...] = jnp.zeros_like(acc_ref)

            acc_ref[...] += _dot(a_ref[...], b_ref[...], dims)

            @pl.when(kk == nk - 1)
            def _():
                o_ref[...] = acc_ref[...].astype(out_dtype)

        if comm:
            @pl.when((i == grid[0] - 1) & (j == grid[1] - 1) & (kk == nk - 1))
            def _():
                comm.wait(cin, cout, sems)

    a_spec = (pl.BlockSpec((tk, tm), lambda i, j, kk: (kk, i)) if trans_a
              else pl.BlockSpec((tm, tk), lambda i, j, kk: (i, kk)))
    b_spec = (pl.BlockSpec((tn, tk), lambda i, j, kk: (j, kk)) if trans_b
              else pl.BlockSpec((tk, tn), lambda i, j, kk: (kk, j)))
    sem = ("arbitrary",) * 3 if comm else ("parallel", "parallel", "arbitrary")
    res = pl.pallas_call(
        body, name=name, grid=grid,
        in_specs=[a_spec, b_spec] + (comm.in_specs if comm else []),
        out_specs=[pl.BlockSpec((tm, tn), lambda i, j, kk: (i, j))] + (comm.out_specs if comm else []),
        out_shape=[jax.ShapeDtypeStruct((m, n), out_dtype)] + (comm.out_shape if comm else []),
        scratch_shapes=([] if nk == 1 else [pltpu.VMEM((tm, tn), F32)]) + (comm.scratch if comm else []),
        compiler_params=_params(sem, side_effects=bool(comm)),
    )(a, b, *(comm.operands if comm else []))
    return res if comm else res[0]


ROW_K = 6
X_K = 8


def _x_specs(nx, k, tile_of=lambda i: i):
    return [pl.BlockSpec((CHUNK, D_MODEL), functools.partial(lambda u, i: (jnp.clip(k * tile_of(i) + u - 1, 0, nx - 1), 0), u))
            for u in range(k)]


def _chunk_of_row(i, k):
    return k * i + lax.broadcasted_iota(jnp.int32, (k * CHUNK, 1), 0) // CHUNK


CONV_COLS = 512
HALO = 8


def _conv_pre(ext, w, b):
    taps = [ext[HALO:, :]] + [pltpu.roll(ext, j, 0)[HALO:, :] for j in range(1, CONV_WIDTH)]
    acc = b + w[3:4, :] * taps[0]
    for j in range(1, CONV_WIDTH):
        acc = acc + w[3 - j:4 - j, :] * taps[j]
    return acc, taps


def _conv_bwd(dxbc, proj, conv_w, conv_b, dproj):
    lp = proj.shape[0]
    t = _tile(lp, (704, 384, 128, 64))
    hb = t // HALO
    nt = lp // t
    c0 = OFF_XBC // CONV_COLS

    def body(dx_ref, dxn_ref, u_ref, up_ref, un_ref, w_ref, b_ref, _, du_ref, dw_ref, db_ref):
        i = pl.program_id(1)
        w = w_ref[...]
        up = up_ref[...] * (i > 0).astype(F32)
        ext = jnp.concatenate([up, u_ref[...], un_ref[...]], axis=0)
        pre, taps = _conv_pre(ext, w, b_ref[...])
        dxn = dxn_ref[...] * (i < nt - 1).astype(F32)
        dxe = jnp.concatenate([dx_ref[...], dxn], axis=0)
        sg = _sigmoid(pre)
        dpre = dxe * sg * (1.0 + pre * (1.0 - sg))
        du = w[3:4, :] * dpre[:t, :]
        for j in range(1, CONV_WIDTH):
            du = du + w[3 - j:4 - j, :] * pltpu.roll(dpre, t + HALO - j, 0)[:t, :]
        du_ref[...] = du.astype(BF16)

        @pl.when(i == 0)
        def _():
            dw_ref[...] = jnp.zeros_like(dw_ref)
            db_ref[...] = jnp.zeros_like(db_ref)

        dp = dpre[:t, :]
        db_ref[...] += jnp.sum(dp, axis=0, keepdims=True)
        for j in range(CONV_WIDTH):
            dw_ref[3 - j:4 - j, :] += jnp.sum(dp * taps[j][:t, :], axis=0, keepdims=True)

    nxt = lambda i: jnp.minimum((i + 1) * hb, lp // HALO - 1)
    return pl.pallas_call(
        body, name="conv_bwd", grid=(D_CONV // CONV_COLS, nt),
        in_specs=[
            pl.BlockSpec((t, CONV_COLS), lambda j, i: (i, j)),
            pl.BlockSpec((HALO, CONV_COLS), lambda j, i: (nxt(i), j)),
            pl.BlockSpec((t, CONV_COLS), lambda j, i: (i, c0 + j)),
            pl.BlockSpec((HALO, CONV_COLS), lambda j, i: (jnp.maximum(i * hb - 1, 0), c0 + j)),
            pl.BlockSpec((HALO, CONV_COLS), lambda j, i: (nxt(i), c0 + j)),
            pl.BlockSpec((CONV_WIDTH, CONV_COLS), lambda j, i: (0, j)),
            pl.BlockSpec((1, CONV_COLS), lambda j, i: (0, j)),
            ANY,
        ],
        out_specs=[
            pl.BlockSpec((t, CONV_COLS), lambda j, i: (i, c0 + j)),
            pl.BlockSpec((CONV_WIDTH, CONV_COLS), lambda j, i: (0, j)),
            pl.BlockSpec((1, CONV_COLS), lambda j, i: (0, j)),
        ],
        out_shape=[
            jax.ShapeDtypeStruct((lp, NP), BF16),
            jax.ShapeDtypeStruct((CONV_WIDTH, D_CONV), F32),
            jax.ShapeDtypeStruct((1, D_CONV), F32),
        ],
        input_output_aliases={7: 0},
        compiler_params=_params(("parallel", "arbitrary")),
    )(dxbc, dxbc, proj, proj, proj, conv_w, conv_b, dproj)


def _swap_halves(t):
    w = t.shape[-1]
    lane = lax.broadcasted_iota(jnp.int32, t.shape, 1)
    return jnp.where((lane % HEAD_DIM) < HEAD_DIM // 2, pltpu.roll(t, w - HEAD_DIM // 2, 1),
                     pltpu.roll(t, HEAD_DIM // 2, 1))


def _act_fwd(proj, cos_t, sin_t, expand, dt_bias_pad):
    lp = proj.shape[0]
    t = _tile(lp, (384, 128, 64))

    def body(q_ref, k_ref, v_ref, dt_ref, cos_ref, sin_ref, ex_ref, bias_ref, qo_ref, ko_ref, vo_ref, dto_ref):
        i = pl.program_id(0)
        cos = cos_ref[...]
        sin = sin_ref[...]
        q = q_ref[...]
        qo_ref[...] = (q * jnp.tile(cos, (1, D_ATT // 128)) + _swap_halves(q) * jnp.tile(sin, (1, D_ATT // 128))).astype(BF16)
        k = k_ref[...]
        ko_ref[...] = (k * jnp.tile(cos, (1, D_KV // 128)) + _swap_halves(k) * jnp.tile(sin, (1, D_KV // 128))).astype(BF16)
        vo_ref[...] = v_ref[...].astype(BF16)
        raw = dt_ref[...] + bias_ref[...]
        sp = jnp.maximum(raw, 0.0) + jnp.log1p(jnp.exp(-jnp.abs(raw)))
        row = i * t + lax.broadcasted_iota(jnp.int32, sp.shape, 0)
        dto_ref[...] = _dot(jnp.where(row >= PAD_LEAD, sp, 0.0), ex_ref[...], NN, HIGHEST)

    return pl.pallas_call(
        body, name="act_fwd", grid=(lp // t,),
        in_specs=[
            pl.BlockSpec((t, D_ATT), lambda i: (i, OFF_Q // D_ATT)),
            pl.BlockSpec((t, D_KV), lambda i: (i, OFF_K // D_KV)),
            pl.BlockSpec((t, D_KV), lambda i: (i, OFF_V // D_KV)),
            pl.BlockSpec((t, 128), lambda i: (i, OFF_DT // 128)),
            pl.BlockSpec((t, 128), lambda i: (i, 0)),
            pl.BlockSpec((t, 128), lambda i: (i, 0)),
            pl.BlockSpec((128, D_SSD), lambda i: (0, 0)),
            pl.BlockSpec((1, 128), lambda i: (0, 0)),
        ],
        out_specs=[
            pl.BlockSpec((t, D_ATT), lambda i: (i, 0)),
            pl.BlockSpec((t, D_KV), lambda i: (i, 0)),
            pl.BlockSpec((t, D_KV), lambda i: (i, 0)),
            pl.BlockSpec((t, D_SSD), lambda i: (i, 0)),
        ],
        out_shape=[
            jax.ShapeDtypeStruct((lp, D_ATT), BF16),
            jax.ShapeDtypeStruct((lp, D_KV), BF16),
            jax.ShapeDtypeStruct((lp, D_KV), BF16),
            jax.ShapeDtypeStruct((lp, D_SSD), F32),
        ],
        compiler_params=_params(("parallel",)),
    )(proj, proj, proj, proj, cos_t, sin_t, expand, dt_bias_pad)


def _act_bwd(dqr, dkr, dv, dg, ddt_part, cos_t, sin_t, reduce_t, dproj):
    lp = dqr.shape[0]
    t = _tile(lp, (384, 128, 64))

    def body(dq_ref, dk_ref, dv_ref, dg_ref, ddt_ref, cos_ref, sin_ref, red_ref, _, o_ref, db_ref):
        i = pl.program_id(0)
        cos = cos_ref[...]
        sin = sin_ref[...]
        dq = dq_ref[...]
        dq = dq * jnp.tile(cos, (1, D_ATT // 128)) + _swap_halves(dq * jnp.tile(sin, (1, D_ATT // 128)))
        dk = dk_ref[...]
        dk = dk * jnp.tile(cos, (1, D_KV // 128)) + _swap_halves(dk * jnp.tile(sin, (1, D_KV // 128)))
        ddt = _dot(ddt_ref[...], red_ref[...], NN, HIGHEST)
        o_ref[...] = jnp.concatenate(
            [dq.astype(BF16), dg_ref[...].astype(BF16), dk.astype(BF16), dv_ref[...].astype(BF16), ddt.astype(BF16),
             jnp.zeros((t, NP - OFF_DT - 128), BF16)], axis=1)

        @pl.when(i == 0)
        def _():
            db_ref[...] = jnp.zeros_like(db_ref)

        db_ref[...] += jnp.sum(ddt, axis=0, keepdims=True)

    return pl.pallas_call(
        body, name="act_bwd", grid=(lp // t,),
        in_specs=[
            pl.BlockSpec((t, D_ATT), lambda i: (i, 0)),
            pl.BlockSpec((t, D_KV), lambda i: (i, 0)),
            pl.BlockSpec((t, D_KV), lambda i: (i, 0)),
            pl.BlockSpec((t, D_ATT), lambda i: (i, 0)),
            pl.BlockSpec((t, D_SSD), lambda i: (i, 0)),
            pl.BlockSpec((t, 128), lambda i: (i, 0)),
            pl.BlockSpec((t, 128), lambda i: (i, 0)),
            pl.BlockSpec((D_SSD, 128), lambda i: (0, 0)),
            ANY,
        ],
        out_specs=[pl.BlockSpec((t, TAIL_W), lambda i: (i, OFF_Q // TAIL_W)), pl.BlockSpec((1, 128), lambda i: (0, 0))],
        out_shape=[jax.ShapeDtypeStruct((lp, NP), BF16), jax.ShapeDtypeStruct((1, 128), F32)],
        input_output_aliases={8: 0},
        compiler_params=_params(("arbitrary",)),
    )(dqr, dkr, dv, dg, ddt_part, cos_t, sin_t, reduce_t, dproj)


def _cs_row(cs):
    row = lax.broadcasted_iota(jnp.int32, cs.shape, 0)
    lane = lax.broadcasted_iota(jnp.int32, cs.shape, 1)
    return jnp.sum(jnp.where(row == lane % HEAD_DIM, cs, 0.0), axis=0, keepdims=True)


def _ssd_fwd(proj, conv_w, conv_b, dt_rep, a_rep, dsk_rep, wn, tri):
    lp = proj.shape[0]
    nc = lp // CHUNK
    assert SSD_GPS == N_GROUPS
    gw = SSD_GPS * GROUP_W
    cps = _tile(nc, (SSD_CPS, 1))

    def body(rawa_ref, rawb_ref, cw_ref, cb_ref, dt_ref, z_ref, a_ref, dsk_ref, wn_ref, tri_ref,
             yn_ref, ytot_ref, hprev_ref, xbc_ref, h_scr, tail_scr):
        @pl.when(pl.program_id(1) == 0)
        def _():
            h_scr[...] = jnp.zeros_like(h_scr)
            tail_scr[...] = jnp.zeros_like(tail_scr)

        for sub in range(cps):
            rs = slice(sub * CHUNK, (sub + 1) * CHUNK)
            raw = jnp.concatenate([rawa_ref[rs, :], rawb_ref[rs, :]], axis=1)
            pre, _ = _conv_pre(jnp.concatenate([tail_scr[...], raw], axis=0), cw_ref[...], cb_ref[...])
            tail_scr[...] = raw[CHUNK - HALO:, :]
            xbc_ref[rs, :] = pre * _sigmoid(pre)

        G = range(SSD_GPS)
        colsl = [slice(gi * GROUP_W, (gi + 1) * GROUP_W) for gi in G]
        rows4 = lax.broadcasted_iota(jnp.int32, (GROUP_W, GROUP_W), 0) // HEAD_DIM
        cols4 = lax.broadcasted_iota(jnp.int32, (GROUP_W, GROUP_W), 1) // HEAD_DIM
        lrow = lax.broadcasted_iota(jnp.int32, (CHUNK, GROUP_W), 0)
        lcol = lax.broadcasted_iota(jnp.int32, (CHUNK, GROUP_W), 1) % HEAD_DIM

        def one_chunk(sub):
            rs = slice(sub * CHUNK, (sub + 1) * CHUNK)
            xbc = [xbc_ref[rs, gi * XBC_BLK:(gi + 1) * XBC_BLK] for gi in G]
            dt = [dt_ref[rs, colsl[gi]] for gi in G]
            xs = [xbc[gi][:, :GROUP_W] for gi in G]
            b = [xbc[gi][:, GROUP_W:GROUP_W + D_STATE].astype(BF16) for gi in G]
            c = [xbc[gi][:, GROUP_W + D_STATE:].astype(BF16) for gi in G]
            hprev = [h_scr[gi] for gi in G]
            cs = [_dot(tri_ref[...], dt[gi] * a_ref[:, colsl[gi]], NN, HIGHEST) for gi in G]
            yoff = [_dot(c[gi], hprev[gi].astype(BF16)) for gi in G]
            cs_t = [_cs_row(cs[gi]) for gi in G]
            xdt = [xs[gi] * dt[gi] for gi in G]
            cs_last = [cs[gi][CHUNK - 1:CHUNK, :] for gi in G]
            st = [_dot(b[gi], (xdt[gi] * jnp.exp(cs_last[gi] - cs[gi])).astype(BF16), TN) for gi in G]
            cb4 = [_dot(c[gi], jnp.concatenate([b[gi]] * HEADS_PER_GROUP, axis=0), NT) for gi in G]
            m = [(cb4[gi] * jnp.exp(jnp.where(lrow >= lcol, cs[gi] - cs_t[gi], -jnp.inf))).astype(BF16) for gi in G]
            xbd = [jnp.where(rows4 == cols4, jnp.concatenate([xdt[gi].astype(BF16)] * HEADS_PER_GROUP, axis=0), 0.0)
                   for gi in G]
            ydiag = [_dot(m[gi], xbd[gi]) for gi in G]
            for gi in G:
                cols = colsl[gi]
                ytot = ydiag[gi] + yoff[gi] * jnp.exp(cs[gi]) + dsk_ref[:, cols] * xs[gi]
                z = z_ref[rs, cols]
                gz = ytot * (z * _sigmoid(z))
                rstd = lax.rsqrt(jnp.mean(gz * gz, axis=-1, keepdims=True) + EPS)
                hprev_ref[sub, gi] = hprev[gi]
                h_scr[gi] = hprev[gi] * jnp.exp(cs_last[gi]) + st[gi]
                ytot_ref[rs, cols] = ytot
                yn_ref[rs, cols] = (gz * rstd * wn_ref[:, cols]).astype(BF16)

        for sub in range(cps):
            one_chunk(sub)

    vec = pl.BlockSpec((1, gw), lambda g, c: (0, g))
    blk = pl.BlockSpec((cps * CHUNK, gw), lambda g, c: (c, g))
    half = D_CONV // 2
    return pl.pallas_call(
        body, name="ssd_fwd", grid=(N_GROUPS // SSD_GPS, nc // cps),
        in_specs=[
            pl.BlockSpec((cps * CHUNK, half), lambda g, c: (c, OFF_XBC // half)),
            pl.BlockSpec((cps * CHUNK, half), lambda g, c: (c, OFF_XBC // half + 1)),
            pl.BlockSpec((CONV_WIDTH, D_CONV), lambda g, c: (0, 0)),
            pl.BlockSpec((1, D_CONV), lambda g, c: (0, 0)),
            blk, blk, vec, vec, vec,
            pl.BlockSpec((CHUNK, CHUNK), lambda g, c: (0, 0)),
        ],
        out_specs=[blk, blk, pl.BlockSpec((cps, SSD_GPS, D_STATE, GROUP_W), lambda g, c: (c, g, 0, 0)),
                   pl.BlockSpec((cps * CHUNK, D_CONV), lambda g, c: (c, 0))],
        out_shape=[
            jax.ShapeDtypeStruct((lp, D_MIX), BF16),
            jax.ShapeDtypeStruct((lp, D_SSD), F32),
            jax.ShapeDtypeStruct((nc, N_GROUPS, D_STATE, GROUP_W), F32),
            jax.ShapeDtypeStruct((lp, D_CONV), F32),
        ],
        scratch_shapes=[pltpu.VMEM((SSD_GPS, D_STATE, GROUP_W), F32), pltpu.VMEM((HALO, D_CONV), F32)],
        compiler_params=_params(("arbitrary", "arbitrary")),
    )(proj, proj, conv_w, conv_b, dt_rep, proj, a_rep, dsk_rep, wn, tri)


def _ssd_bwd(dmix, ytot, proj, xbc, dt_rep, hprev, a_rep, dsk_rep, wn, tri, comm):
    lp = xbc.shape[0]
    nc = lp // CHUNK
    gps = SSD_GPS
    gw = gps * GROUP_W
    cps = _tile(nc, (SSD_CPS, 1))
    ncm = comm.n
    n_in, n_out = 10, 6
    grid = (N_GROUPS // gps, nc // cps)

    def all_groups(refs):
        (dyn_ref, ytot_ref, z_ref, xbc_ref, dt_ref, hprev_ref, a_ref, dsk_ref, wn_ref, tri_ref,
         dz_ref, dxbc_ref, ddt_ref, dd_ref, da_ref, dwn_ref, dh_scr) = refs
        G = range(gps)
        H = range(HEADS_PER_GROUP)
        cl = [slice(gi * GROUP_W, (gi + 1) * GROUP_W) for gi in G]
        hl = [slice(r * HEAD_DIM, (r + 1) * HEAD_DIM) for r in H]
        tri = tri_ref[...]
        xbc = [xbc_ref[:, gi * XBC_BLK:(gi + 1) * XBC_BLK] for gi in G]
        dt = [dt_ref[:, cl[gi]] for gi in G]
        a = [a_ref[:, cl[gi]] for gi in G]
        xs = [xbc[gi][:, :GROUP_W] for gi in G]
        bbf = [xbc[gi][:, GROUP_W:GROUP_W + D_STATE].astype(BF16) for gi in G]
        cbf = [xbc[gi][:, GROUP_W + D_STATE:].astype(BF16) for gi in G]
        hprev = [hprev_ref[gi] for gi in G]
        hbf = [hprev[gi].astype(BF16) for gi in G]
        dhn = [dh_scr[gi] for gi in G]
        dhnb = [dhn[gi].astype(BF16) for gi in G]
        cs = [_dot(tri, dt[gi] * a[gi], NN, HIGHEST) for gi in G]
        g = [_dot(cbf[gi], hbf[gi]) for gi in G]
        dxw = [_dot(bbf[gi], dhnb[gi]) for gi in G]
        cs_t = [_cs_row(cs[gi]) for gi in G]
        dy = []
        for gi in G:
            ytot = ytot_ref[:, cl[gi]]
            z = z_ref[:, cl[gi]]
            dyn = dyn_ref[:, cl[gi]]
            sz = _sigmoid(z)
            silu_z = z * sz
            gz = ytot * silu_z
            rstd = lax.rsqrt(jnp.mean(gz * gz, axis=-1, keepdims=True) + EPS)
            xhat = gz * rstd
            dwn_ref[:, cl[gi]] += jnp.sum(dyn * xhat, axis=0, keepdims=True)
            dxhat = dyn * wn_ref[:, cl[gi]]
            dgz = rstd * (dxhat - xhat * jnp.mean(dxhat * xhat, axis=-1, keepdims=True))
            dy.append(dgz * silu_z)
            dz_ref[:, cl[gi]] = (dgz * ytot * (sz * (1.0 + z * (1.0 - sz)))).astype(BF16)
            dd_ref[:, cl[gi]] += jnp.sum(dy[gi] * xs[gi], axis=0, keepdims=True)
        xdt = [xs[gi] * dt[gi] for gi in G]
        e = [jnp.exp(cs[gi]) for gi in G]
        cs_last = [cs[gi][CHUNK - 1:CHUNK, :] for gi in G]
        dte = [jnp.exp(cs_last[gi] - cs[gi]) for gi in G]
        cd = [jnp.exp(cs_last[gi]) for gi in G]
        dgb = [(dy[gi] * e[gi]).astype(BF16) for gi in G]
        dyb = [dy[gi].astype(BF16) for gi in G]
        xdtb = [xdt[gi].astype(BF16) for gi in G]
        dc = [_dot(dgb[gi], hbf[gi], NT) for gi in G]
        dhprev = [_dot(cbf[gi], dgb[gi], TN) for gi in G]
        db = [_dot((xdt[gi] * dte[gi]).astype(BF16), dhnb[gi], NT) for gi in G]
        row = lax.broadcasted_iota(jnp.int32, (CHUNK, CHUNK), 0)
        causal = row >= lax.broadcasted_iota(jnp.int32, (CHUNK, CHUNK), 1)
        cb = [_dot(cbf[gi], bbf[gi], NT) for gi in G]
        dm = [[_dot(dyb[gi][:, hl[r]], xdtb[gi][:, hl[r]], NT) for r in H] for gi in G]
        mb, dseg, dcbb = [], [], []
        for gi in G:
            mb.append([])
            dseg.append([])
            dcb = None
            for r in H:
                seg = cs[gi][:, r * HEAD_DIM:r * HEAD_DIM + 1] - cs_t[gi][:, hl[r]]
                lm = jnp.exp(jnp.where(causal, seg, -jnp.inf))
                m = cb[gi] * lm
                mb[gi].append(m.astype(BF16))
                dseg[gi].append(dm[gi][r] * m)
                dcb = dm[gi][r] * lm if r == 0 else dcb + dm[gi][r] * lm
            dcbb.append(dcb.astype(BF16))
        dxdt_diag = [[_dot(mb[gi][r], dyb[gi][:, hl[r]], TN) for r in H] for gi in G]
        ones = jnp.ones((CHUNK, HEAD_DIM), F32)
        colsum = [[_dot(dseg[gi][r], ones, TN, HIGHEST) for r in H] for gi in G]
        dc2 = [_dot(dcbb[gi], bbf[gi]) for gi in G]
        db2 = [_dot(dcbb[gi], cbf[gi], TN) for gi in G]
        dcs = []
        for gi in G:
            t_dte = dxw[gi] * xdt[gi] * dte[gi]
            dcs_last = (jnp.sum(dhn[gi] * hprev[gi], axis=0, keepdims=True) * cd[gi]
                        + jnp.sum(t_dte, axis=0, keepdims=True))
            diag = jnp.concatenate(
                [(jnp.sum(dseg[gi][r], axis=1, keepdims=True) - colsum[gi][r]) * (1.0 / HEAD_DIM) for r in H], axis=1)
            d = dy[gi] * g[gi] * e[gi] - t_dte + diag
            row = lax.broadcasted_iota(jnp.int32, d.shape, 0)
            dcs.append(d + jnp.where(row == CHUNK - 1, dcs_last, 0.0))
        dda = [_dot(tri, dcs[gi], TN, HIGHEST) for gi in G]
        for gi in G:
            dxdt = dxw[gi] * dte[gi] + jnp.concatenate(dxdt_diag[gi], axis=1)
            da_ref[:, cl[gi]] += jnp.sum(dda[gi] * dt[gi], axis=0, keepdims=True)
            ddt = dda[gi] * a[gi] + dxdt * xs[gi]
            dxs = dsk_ref[:, cl[gi]] * dy[gi] + dxdt * dt[gi]
            ddt_ref[:, cl[gi]] = ddt * (1.0 - jnp.exp(-dt[gi]))
            dxbc_ref[:, gi * XBC_BLK:(gi + 1) * XBC_BLK] = jnp.concatenate(
                [dxs, db[gi] + db2[gi], dc[gi] + dc2[gi]], axis=1)
            dh_scr[gi] = dhprev[gi] + dhn[gi] * cd[gi]

    def body(*refs):
        ins = refs[:n_in]
        cin = refs[n_in:n_in + ncm]
        outs = refs[n_in + ncm:n_in + ncm + n_out]
        cout = refs[n_in + ncm + n_out:n_in + 2 * ncm + n_out]
        dh_scr = refs[n_in + 2 * ncm + n_out]
        sems = refs[n_in + 2 * ncm + n_out + 1:]
        g, c = pl.program_id(0), pl.program_id(1)

        @pl.when((g == 0) & (c == 0))
        def _():
            comm.start(cin, cout, sems)

        @pl.when(c == 0)
        def _():
            dh_scr[...] = jnp.zeros_like(dh_scr)
            for ref in outs[3:]:
                ref[...] = jnp.zeros_like(ref)

        for sub in reversed(range(cps)):
            rs = pl.ds(sub * CHUNK, CHUNK)
            chunk_ins = tuple(r.at[rs] for r in ins[:5]) + (ins[5].at[sub],) + ins[6:]
            chunk_outs = tuple(r.at[rs] for r in outs[:3]) + outs[3:]
            all_groups(chunk_ins + chunk_outs + (dh_scr,))

        @pl.when((g == grid[0] - 1) & (c == grid[1] - 1))
        def _():
            comm.wait(cin, cout, sems)

    rev = lambda c: grid[1] - 1 - c
    vec = pl.BlockSpec((1, gw), lambda g, c: (0, g))
    blk = pl.BlockSpec((cps * CHUNK, gw), lambda g, c: (rev(c), g))
    xblk = pl.BlockSpec((cps * CHUNK, gps * XBC_BLK), lambda g, c: (rev(c), g))
    res = pl.pallas_call(
        body, name="ssd_bwd", grid=grid,
        in_specs=[blk, blk, blk, xblk, blk,
                  pl.BlockSpec((cps, gps, D_STATE, GROUP_W), lambda g, c: (rev(c), g, 0, 0)),
                  vec, vec, vec,
                  pl.BlockSpec((CHUNK, CHUNK), lambda g, c: (0, 0))] + comm.in_specs,
        out_specs=[blk, xblk, blk, vec, vec, vec] + comm.out_specs,
        out_shape=[
            jax.ShapeDtypeStruct((lp, NP), BF16),
            jax.ShapeDtypeStruct((lp, D_CONV), F32),
            jax.ShapeDtypeStruct((lp, D_SSD), F32),
            jax.ShapeDtypeStruct((1, D_SSD), F32),
            jax.ShapeDtypeStruct((1, D_SSD), F32),
            jax.ShapeDtypeStruct((1, D_SSD), F32),
        ] + comm.out_shape,
        scratch_shapes=[pltpu.VMEM((gps, D_STATE, GROUP_W), F32)] + comm.scratch,
        compiler_params=_params(("arbitrary", "arbitrary"), side_effects=True),
    )(dmix, ytot, proj, xbc, dt_rep, hprev, a_rep, dsk_rep, wn, tri, *comm.operands)
    return res


def _stack_heads(t, h):
    return jnp.concatenate([t[:, (REP * h + r) * HEAD_DIM:(REP * h + r + 1) * HEAD_DIM] for r in range(REP)], axis=0)


def _band(t2, t1, t0, h):
    sl = slice(h * HEAD_DIM, (h + 1) * HEAD_DIM)
    return jnp.concatenate([t2[:, sl], t1[:, sl], t0[:, sl]], axis=0)


def _attn_probs(s, sink, qc):
    s = s * (HEAD_DIM ** -0.5)
    key_abs = (qc - WINDOW_CHUNKS) * CHUNK + lax.broadcasted_iota(jnp.int32, s.shape, 1)
    s = jnp.where(key_abs >= PAD_LEAD, s, -jnp.inf)
    m = jnp.maximum(jnp.max(s, axis=-1, keepdims=True), sink)
    p = jnp.exp(s - m)
    ps = jnp.exp(sink - m)
    denom = jnp.sum(p, axis=-1, keepdims=True) + ps
    return p / denom, ps / denom


ATT_QC_FWD = 6
ATT_QC_BWD = 2


def _kv_specs(width, newest_chunk_of, kc):
    return [pl.BlockSpec((CHUNK, width), functools.partial(lambda j, p: (jnp.maximum(newest_chunk_of(p) - j, 0), 0), j))
            for j in range(kc - 1, -1, -1)]


def _attn_fwd(qr, kr, vb, proj, sink_stack, mix):
    lp = qr.shape[0]
    nc = lp // CHUNK
    qn = _tile(nc, (ATT_QC_FWD, 2, 1))
    kn = WINDOW_CHUNKS + qn
    qrows = qn * CHUNK

    def body(q_ref, *rest):
        k_refs, v_refs = rest[:kn], rest[kn:2 * kn]
        g_ref, sink_ref, _, att_ref, mix_ref = rest[2 * kn:]
        p = pl.program_id(0)
        q = q_ref[...]
        ks = [r[...] for r in k_refs]
        vs = [r[...] for r in v_refs]
        units = [(u, h) for u in range(qn) for h in range(KV_HEADS)]
        s = [_dot(_stack_heads(q[u * CHUNK:(u + 1) * CHUNK, :], h), _band(*ks[u:u + 3], h), NT) for u, h in units]
        vbh = [_band(*vs[u:u + 3], h) for u, h in units]
        pn = [_attn_probs(s[i], sink_ref[h], qn * p + u)[0].astype(BF16) for i, (u, h) in enumerate(units)]
        o = [_dot(pn[i], vbh[i]) for i in range(len(units))]
        att = jnp.concatenate(
            [jnp.concatenate([o[u * KV_HEADS + h][r * CHUNK:(r + 1) * CHUNK, :] for h in range(KV_HEADS) for r in range(REP)],
                             axis=1) for u in range(qn)], axis=0)
        att_ref[...] = att
        g = g_ref[...]
        mix_ref[...] = (att * (g * _sigmoid(g))).astype(BF16)

    newest = lambda p: qn * p + qn - 1
    return pl.pallas_call(
        body, name="attn_fwd", grid=(nc // qn,),
        in_specs=[pl.BlockSpec((qrows, D_ATT), lambda p: (p, 0))] + _kv_specs(D_KV, newest, kn) + _kv_specs(D_KV, newest, kn) + [
            pl.BlockSpec((qrows, D_ATT), lambda p: (p, OFF_G // D_ATT)),
            pl.BlockSpec((KV_HEADS, REP * CHUNK, 1), lambda p: (0, 0, 0)),
            ANY,
        ],
        out_specs=[pl.BlockSpec((qrows, D_ATT), lambda p: (p, 0)),
                   pl.BlockSpec((qrows, D_ATT), lambda p: (p, D_SSD // D_ATT))],
        out_shape=[jax.ShapeDtypeStruct((lp, D_ATT), F32), jax.ShapeDtypeStruct((lp, D_MIX), BF16)],
        input_output_aliases={2 * kn + 3: 1},
        compiler_params=_params(("parallel",)),
    )(qr, *([kr] * kn), *([vb] * kn), proj, sink_stack, mix)


def _attn_bwd(qr, kr, vb, att, proj, dmix, sink_stack):
    lp = qr.shape[0]
    nc = lp // CHUNK
    qn = ATT_QC_BWD
    kn = WINDOW_CHUNKS + qn
    assert nc % qn == 0 and WINDOW_CHUNKS % qn == 0
    steps = nc // qn
    qrows = qn * CHUNK
    wrows = kn * CHUNK

    def body(q_ref, *rest):
        k_refs, v_refs = rest[:kn], rest[kn:2 * kn]
        (att_ref, g_ref, do_ref, sink_ref, dq_ref, dk_ref, dv_ref, dg_ref, dsink_ref, dk_acc, dv_acc) = rest[2 * kn:]
        step = pl.program_id(0)

        @pl.when(step == 0)
        def _():
            dk_acc[...] = jnp.zeros_like(dk_acc)
            dv_acc[...] = jnp.zeros_like(dv_acc)
            dsink_ref[...] = jnp.zeros_like(dsink_ref)

        @pl.when(step < steps)
        def _():
            q = q_ref[...]
            ks = [r[...] for r in k_refs]
            vs = [r[...] for r in v_refs]
            att = att_ref[...]
            g = g_ref[...]
            dog = do_ref[...]
            sg = _sigmoid(g)
            dg_ref[...] = dog * att * (sg * (1.0 + g * (1.0 - sg)))
            do = dog * (g * sg)
            units = [(u, h) for u in range(qn) for h in range(KV_HEADS)]
            n = range(len(units))
            rows = [slice(u * CHUNK, (u + 1) * CHUNK) for u in range(qn)]
            qs = [_stack_heads(q[rows[u], :], h) for u, h in units]
            kb = [_band(*ks[u:u + 3], h) for u, h in units]
            vbh = [_band(*vs[u:u + 3], h) for u, h in units]
            dos = [_stack_heads(do[rows[u], :], h) for u, h in units]
            dosb = [dos[i].astype(BF16) for i in n]
            s = [_dot(qs[i], kb[i], NT) for i in n]
            dp = [_dot(dosb[i], vbh[i], NT) for i in n]
            ds, pnb = [], []
            for i, (u, h) in enumerate(units):
                pn, psink = _attn_probs(s[i], sink_ref[h], qn * step + u)
                delta = jnp.sum(dos[i] * _stack_heads(att[rows[u], :], h), axis=-1, keepdims=True)
                ds.append((pn * (dp[i] - delta)).astype(BF16))
                pnb.append(pn.astype(BF16))
                dsink_ref[h] += -(psink * delta)
            dqs = [_dot(ds[i], kb[i]) for i in n]
            dks = [_dot(ds[i], qs[i], TN) for i in n]
            dvs = [_dot(pnb[i], dosb[i], TN) for i in n]
            dq_ref[...] = jnp.concatenate(
                [jnp.concatenate([dqs[u * KV_HEADS + h][r * CHUNK:(r + 1) * CHUNK, :]
                                  for h in range(KV_HEADS) for r in range(REP)], axis=1) for u in range(qn)],
                axis=0) * (HEAD_DIM ** -0.5)
            for u in range(qn):
                band = slice(u * CHUNK, u * CHUNK + BAND)
                dk_acc[band, :] += jnp.concatenate(dks[u * KV_HEADS:(u + 1) * KV_HEADS], axis=1) * (HEAD_DIM ** -0.5)
                dv_acc[band, :] += jnp.concatenate(dvs[u * KV_HEADS:(u + 1) * KV_HEADS], axis=1)

        dk_ref[...] = dk_acc[0:qrows, :]
        dv_ref[...] = dv_acc[0:qrows, :]
        for acc in (dk_acc, dv_acc):
            rest_rows = acc[qrows:wrows, :]
            acc[0:wrows - qrows, :] = rest_rows
            acc[wrows - qrows:wrows, :] = jnp.zeros((qrows, D_KV), F32)

    qp = lambda p: jnp.minimum(p, steps - 1)
    newest = lambda p: qn * qp(p) + qn - 1
    qblk = pl.BlockSpec((qrows, D_ATT), lambda p: (qp(p), 0))
    oldest = pl.BlockSpec((qrows, D_KV), lambda p: (jnp.maximum(p - 1, 0), 0))
    return pl.pallas_call(
        body, name="attn_bwd", grid=(steps + 1,),
        in_specs=[qblk] + _kv_specs(D_KV, newest, kn) + _kv_specs(D_KV, newest, kn) + [
            qblk,
            pl.BlockSpec((qrows, D_ATT), lambda p: (qp(p), OFF_G // D_ATT)),
            pl.BlockSpec((qrows, D_ATT), lambda p: (qp(p), D_SSD // D_ATT)),
            pl.BlockSpec((KV_HEADS, REP * CHUNK, 1), lambda p: (0, 0, 0)),
        ],
        out_specs=[qblk, oldest, oldest, qblk, pl.BlockSpec((KV_HEADS, REP * CHUNK, 1), lambda p: (0, 0, 0))],
        out_shape=[
            jax.ShapeDtypeStruct((lp, D_ATT), F32),
            jax.ShapeDtypeStruct((lp, D_KV), F32),
            jax.ShapeDtypeStruct((lp, D_KV), F32),
            jax.ShapeDtypeStruct((lp, D_ATT), F32),
            jax.ShapeDtypeStruct((KV_HEADS, REP * CHUNK, 1), F32),
        ],
        scratch_shapes=[pltpu.VMEM((wrows, D_KV), F32), pltpu.VMEM((wrows, D_KV), F32)],
        compiler_params=_params(("arbitrary",)),
    )(qr, *([kr] * kn), *([vb] * kn), att, proj, dmix, sink_stack)


def _post_loss(out, x2d, target, w):
    lp = out.shape[0]
    nc = lp // CHUNK
    nx = x2d.shape[0] // CHUNK

    k = _tile(nc, (ROW_K, 3, 2, 1))

    def body(o_ref, *rest):
        x_refs, t_refs = rest[:k], rest[k:2 * k]
        w_ref, do_ref, dy_ref, gw_ref, loss_ref = rest[2 * k:]
        i = pl.program_id(0)

        @pl.when(i == 0)
        def _():
            gw_ref[...] = jnp.zeros_like(gw_ref)
            loss_ref[...] = jnp.zeros_like(loss_ref)

        o = o_ref[...]
        w = w_ref[...]
        rstd = lax.rsqrt(jnp.mean(o * o, axis=-1, keepdims=True) + EPS)
        xhat = o * rstd
        x = jnp.concatenate([r[...] for r in x_refs], axis=0)
        t = jnp.concatenate([r[...] for r in t_refs], axis=0)
        chunk = _chunk_of_row(i, k)
        err = (x + xhat * w - t) * ((chunk > 0) & (chunk <= nx)).astype(F32)
        loss_ref[...] += 0.5 * jnp.sum(jnp.mean(err * err, axis=-1, keepdims=True), axis=0, keepdims=True)
        dy = err * (1.0 / D_MODEL)
        dy_ref[...] = dy
        gw_ref[...] += jnp.sum(dy * xhat, axis=0, keepdims=True)
        dxhat = dy * w
        do_ref[...] = (rstd * (dxhat - xhat * jnp.mean(dxhat * xhat, axis=-1, keepdims=True))).astype(BF16)

    row = pl.BlockSpec((k * CHUNK, D_MODEL), lambda i: (i, 0))
    return pl.pallas_call(
        body, name="post_loss", grid=(nc // k,),
        in_specs=[row] + _x_specs(nx, k) + _x_specs(nx, k) + [pl.BlockSpec((1, D_MODEL), lambda i: (0, 0))],
        out_specs=[row, row, pl.BlockSpec((1, D_MODEL), lambda i: (0, 0)), pl.BlockSpec((1, 128), lambda i: (0, 0))],
        out_shape=[
            jax.ShapeDtypeStruct((lp, D_MODEL), BF16),
            jax.ShapeDtypeStruct((lp, D_MODEL), F32),
            jax.ShapeDtypeStruct((1, D_MODEL), F32),
            jax.ShapeDtypeStruct((1, 128), F32),
        ],
        compiler_params=_params(("arbitrary",)),
    )(out, *([x2d] * k), *([target] * k), w)


def _prenorm_bwd(dhn, x2d, h0, dy, w):
    nx = x2d.shape[0] // CHUNK
    k = _tile(nx, (X_K, 4, 2, 1))
    rows = k * CHUNK

    def backward(h, dhn, w):
        rstd = lax.rsqrt(jnp.mean(h * h, axis=-1, keepdims=True) + EPS)
        xhat = h * rstd
        dxhat = dhn * w
        dh = rstd * (dxhat - xhat * jnp.mean(dxhat * xhat, axis=-1, keepdims=True))
        return dh, jnp.sum(dhn * xhat, axis=0, keepdims=True)

    def body(*refs):
        dhn_refs, dy_refs = refs[:k], refs[k:2 * k]
        x_ref, w_ref, gx_ref, gw_ref = refs[2 * k:]

        @pl.when(pl.program_id(0) == 0)
        def _():
            gw_ref[...] = jnp.zeros_like(gw_ref)

        dh, gw = backward(x_ref[...], jnp.concatenate([r[...] for r in dhn_refs], axis=0), w_ref[...])
        gx_ref[...] = dh + jnp.concatenate([r[...] for r in dy_refs], axis=0)
        gw_ref[...] += gw

    def body_meta(dhn_ref, h0_ref, w_ref, d0_ref, gw_ref):
        d0_ref[...], gw_ref[...] = backward(h0_ref[...], dhn_ref[...], w_ref[...])

    chunk_specs = [pl.BlockSpec((CHUNK, D_MODEL), functools.partial(lambda u, i: (k * i + 1 + u, 0), u)) for u in range(k)]
    first = pl.BlockSpec((CHUNK, D_MODEL), lambda i: (0, 0))
    vec = pl.BlockSpec((1, D_MODEL), lambda i: (0, 0))
    wide = pl.BlockSpec((rows, D_MODEL), lambda i: (i, 0))
    gx, gw_x = pl.pallas_call(
        body, name="prenorm_bwd", grid=(nx // k,),
        in_specs=chunk_specs + chunk_specs + [wide, vec],
        out_specs=[wide, vec],
        out_shape=[jax.ShapeDtypeStruct((nx * CHUNK, D_MODEL), F32), jax.ShapeDtypeStruct((1, D_MODEL), F32)],
        compiler_params=_params(("arbitrary",)),
    )(*([dhn] * k), *([dy] * k), x2d, w)
    d0, gw_0 = pl.pallas_call(
        body_meta, name="prenorm_bwd_meta", grid=(1,),
        in_specs=[first, first, vec], out_specs=[first, vec],
        out_shape=[jax.ShapeDtypeStruct((CHUNK, D_MODEL), F32), jax.ShapeDtypeStruct((1, D_MODEL), F32)],
        compiler_params=_params(("arbitrary",)),
    )(dhn, h0, w)
    return gx, d0, gw_x + gw_0


def _adamw(slabs, w, m, v, name):
    rows, cols = w.shape
    tr = _tile(rows, (256, 128, 64, 16, 8))
    c1 = 1.0 - ADAM_B1 ** ADAM_STEP
    c2 = 1.0 - ADAM_B2 ** ADAM_STEP

    def body(s_ref, w_ref, m_ref, v_ref, g_ref, d_ref, mo_ref, vo_ref):
        g = s_ref[0].astype(F32)
        for k in range(1, slabs.shape[0]):
            g = g + s_ref[k].astype(F32)
        w = w_ref[...]
        m = ADAM_B1 * m_ref[...] + (1.0 - ADAM_B1) * g
        v = ADAM_B2 * v_ref[...] + (1.0 - ADAM_B2) * (g * g)
        g_ref[...] = g
        mo_ref[...] = m
        vo_ref[...] = v
        d_ref[...] = -ADAM_LR * ((m / c1) / (jnp.sqrt(v / c2) + ADAM_EPS) + ADAM_WD * w)

    blk = pl.BlockSpec((tr, cols), lambda i: (i, 0))
    shape = jax.ShapeDtypeStruct((rows, cols), F32)
    return pl.pallas_call(
        body, name=name, grid=(rows // tr,),
        in_specs=[pl.BlockSpec((slabs.shape[0], tr, cols), lambda i: (0, i, 0)), blk, blk, blk],
        out_specs=[blk, blk, blk, blk],
        out_shape=[shape, shape, shape, shape],
        compiler_params=_params(("parallel",)),
    )(slabs, w, m, v)


def _perm_xbc(a):
    lead = a.shape[:-1]
    xs = a[..., :D_SSD].reshape(lead + (N_GROUPS, GROUP_W))
    b = a[..., D_SSD:D_SSD + N_GROUPS * D_STATE].reshape(lead + (N_GROUPS, D_STATE))
    c = a[..., D_SSD + N_GROUPS * D_STATE:].reshape(lead + (N_GROUPS, D_STATE))
    return jnp.concatenate([xs, b, c], axis=-1).reshape(lead + (D_CONV,))


def _unperm_xbc(a):
    lead = a.shape[:-1]
    t = a.reshape(lead + (N_GROUPS, XBC_BLK))
    xs = t[..., :GROUP_W].reshape(lead + (D_SSD,))
    b = t[..., GROUP_W:GROUP_W + D_STATE].reshape(lead + (N_GROUPS * D_STATE,))
    c = t[..., GROUP_W + D_STATE:].reshape(lead + (N_GROUPS * D_STATE,))
    return jnp.concatenate([xs, b, c], axis=-1)


R_Z, R_XBC, R_DT, R_Q, R_K, R_V, R_G = 0, 2048, 6144, 6176, 7200, 7456, 7712


def _internal_of_reference():
    ref = np.arange(D_IN_PROJ)
    out = np.empty(D_IN_PROJ, np.int64)
    out[R_Z:R_XBC] = OFF_Z + ref[:D_SSD]
    xs = np.arange(D_SSD)
    out[R_XBC:R_XBC + D_SSD] = OFF_XBC + (xs // GROUP_W) * XBC_BLK + xs % GROUP_W
    bc = np.arange(N_GROUPS * D_STATE)
    out[R_XBC + D_SSD:R_XBC + D_SSD + N_GROUPS * D_STATE] = OFF_XBC + (bc // D_STATE) * XBC_BLK + GROUP_W + bc % D_STATE
    out[R_XBC + D_SSD + N_GROUPS * D_STATE:R_DT] = OFF_XBC + (bc // D_STATE) * XBC_BLK + GROUP_W + D_STATE + bc % D_STATE
    out[R_DT:R_Q] = OFF_DT + np.arange(SSD_HEADS)
    out[R_Q:R_K] = OFF_Q + np.arange(D_ATT)
    out[R_K:R_V] = OFF_K + np.arange(D_KV)
    out[R_V:R_G] = OFF_V + np.arange(D_KV)
    out[R_G:] = OFF_G + np.arange(D_ATT)
    return out


def _runs(src, dst_break):
    runs, lo = [], 0
    for i in range(1, len(src) + 1):
        if i == len(src) or src[i] != src[i - 1] + 1 or dst_break[i] != dst_break[i - 1]:
            runs.append((lo, i))
            lo = i
    return runs


RELAYOUT_ROWS = 256


def _lane_window(ref, lead, c0, n):
    a0 = c0 // 128 * 128
    a1 = min(-(-(c0 + n) // 128) * 128, ref.shape[-1])
    return ref[lead + (slice(None), slice(a0, a1))][:, c0 - a0:c0 - a0 + n]


def _gather_w_in(w_shard, small, x2d, norm_w):
    int_of_ref = _internal_of_reference()
    ref_of_int = np.full(NP, -1, np.int64)
    ref_of_int[int_of_ref] = np.arange(D_IN_PROJ)
    shard = np.where(ref_of_int >= 0, ref_of_int // SHARD_IN, -1)
    src = np.where(ref_of_int >= 0, ref_of_int, -10 - 2 * np.arange(NP))
    plan, zeros = [], 0
    for lo, hi in _runs(src, shard):
        if ref_of_int[lo] < 0:
            zeros += hi - lo
            continue
        if zeros:
            plan.append((None, 0, zeros))
            zeros = 0
        plan.append((int(ref_of_int[lo] // SHARD_IN), int(ref_of_int[lo] % SHARD_IN), hi - lo))
    if zeros:
        plan.append((None, 0, zeros))
    tr = RELAYOUT_ROWS
    steps = D_MODEL // tr
    side = _GatherTwoLevel(small)
    ns = side.n
    nx = x2d.shape[0] // CHUNK
    nc = nx + 2
    k = _tile(nc, (11, ROW_K, 3, 2, 1))
    pn = nc // k
    n = max(pn, steps)
    tile_of = lambda i: jnp.where(i < pn - 1, i + 1, 0)

    def body(w_ref, *rest):
        s_in, x_refs, nw_ref = rest[:ns], rest[ns:ns + k], rest[ns + k]
        o_ref, land_ref, s_out, hn_ref = rest[ns + k + 1], rest[ns + k + 2], rest[ns + k + 3:2 * ns + k + 3], rest[2 * ns + k + 3]
        tile_buf, send_sems, recv_sems, load_sems, local_sem, meta_buf, meta_sem = rest[2 * ns + k + 4:2 * ns + k + 11]
        side_sems = rest[2 * ns + k + 11:]
        i = pl.program_id(0)
        x, y, c = lax.axis_index("x"), lax.axis_index("y"), lax.axis_index("c")
        me, sibling = (x, y, c), (x, y, 1 - c)
        chips = [(1 - x, y), (x, 1 - y), (1 - x, 1 - y)]

        def prenorm_tile(t, head=None):
            h = jnp.concatenate([r[...] for r in x_refs], axis=0)
            if head is not None:
                h = jnp.concatenate([head, h[CHUNK:, :]], axis=0)
            h = h * (_chunk_of_row(t, k) <= nx).astype(F32)
            rstd = lax.rsqrt(jnp.mean(h * h, axis=-1, keepdims=True) + EPS)
            hn_ref[...] = (h * rstd * nw_ref[...]).astype(BF16)

        if pn > 1:
            @pl.when(i < pn - 1)
            def _():
                prenorm_tile(i + 1)

        def rows(t):
            return pl.ds(t * tr, tr)

        def tile(place, t):
            return land_ref.at[4 * place[0] + 2 * place[1] + place[2], rows(t), :]

        def copy(t, k, block, to, src=None):
            return pltpu.make_async_remote_copy(
                src_ref=tile(block, t) if src is None else src, dst_ref=tile(block, t),
                send_sem=send_sems.at[t, k], recv_sem=recv_sems.at[t, k], device_id=to, device_id_type=pl.DeviceIdType.MESH)

        def own_sends(t):
            mine = w_ref.at[rows(t), :]
            return [copy(t, 0, me, sibling, src=mine)] + [copy(t, 1 + j, me, (*chip, c), src=mine)
                                                          for j, chip in enumerate(chips)]

        own = pltpu.make_async_copy(w_ref, land_ref.at[4 * x + 2 * y + c], local_sem)

        @pl.when(i < steps)
        def _():
            @pl.when(i == 0)
            def _():
                side.start(s_in, s_out, side_sems)
                own.start()
                for t in range(steps):
                    for cp in own_sends(t):
                        cp.start()
                own.wait()

            forwards = []
            for j, chip in enumerate(chips):
                copy(i, 1 + j, (*chip, c), me).wait_recv()
                forwards.append(copy(i, 4 + j, (*chip, c), sibling))
                forwards[j].start()
            copy(i, 0, sibling, me).wait_recv()
            for j, chip in enumerate(chips):
                copy(i, 4 + j, (*chip, 1 - c), me).wait_recv()
            loads = [pltpu.make_async_copy(land_ref.at[s, rows(i), :], tile_buf.at[s], load_sems.at[s]) for s in range(N_DEV)]
            for cp in loads:
                cp.start()
            for cp in loads:
                cp.wait()
            o_ref[...] = jnp.concatenate(
                [jnp.zeros((tr, w), o_ref.dtype) if s is None else _lane_window(tile_buf, (s,), c0, w) for s, c0, w in plan],
                axis=1)
            for cp in forwards:
                cp.wait_send()

            @pl.when(i == steps - 1)
            def _():
                for t in range(steps):
                    for cp in own_sends(t):
                        cp.wait_send()
                side.wait(s_in, s_out, side_sems)

        @pl.when(i == n - 1)
        def _():
            meta = pltpu.make_async_copy(s_out[ns - 1], meta_buf, meta_sem)
            meta.start()
            meta.wait()
            tokens = jnp.concatenate([meta_buf[s] for s in range(N_DEV)], axis=1)
            prenorm_tile(0, jnp.concatenate([jnp.zeros((PAD_LEAD, D_MODEL), F32), tokens], axis=0))

    last = steps - 1
    res = pl.pallas_call(
        body, name="gather_w_in", grid=(n,),
        in_specs=[ANY] + side.in_specs + _x_specs(nx, k, tile_of) + [pl.BlockSpec((1, D_MODEL), lambda i: (0, 0))],
        out_specs=[pl.BlockSpec((tr, NP), lambda i: (jnp.minimum(i, last), 0)), ANY] + side.out_specs + [
            pl.BlockSpec((k * CHUNK, D_MODEL), lambda i: (tile_of(i), 0))],
        out_shape=[jax.ShapeDtypeStruct((D_MODEL, NP), w_shard.dtype),
                   jax.ShapeDtypeStruct((N_DEV, D_MODEL, SHARD_IN), w_shard.dtype)] + side.out_shape + [
            jax.ShapeDtypeStruct((nc * CHUNK, D_MODEL), BF16)],
        scratch_shapes=[pltpu.VMEM((N_DEV, tr, SHARD_IN), w_shard.dtype),
                        pltpu.SemaphoreType.DMA((steps, N_DEV - 1)), pltpu.SemaphoreType.DMA((steps, N_DEV - 1)),
                        pltpu.SemaphoreType.DMA((N_DEV,)), pltpu.SemaphoreType.DMA(()),
                        pltpu.VMEM((N_DEV,) + tuple(small[ns - 1].shape), F32),
                        pltpu.SemaphoreType.DMA(())] + side.scratch,
        compiler_params=_params(("arbitrary",), side_effects=True),
    )(w_shard, *side.operands, *([x2d] * k), norm_w)
    return [res[0]] + list(res[2:])


def _w_in_chip_slabs(dw):
    int_of_ref = _internal_of_reference()
    plan = []
    for s in range(N_DEV):
        cols = int_of_ref[s * SHARD_IN:(s + 1) * SHARD_IN]
        plan.append([(int(cols[lo]), hi - lo) for lo, hi in _runs(cols, np.zeros_like(cols))])
    tr = RELAYOUT_ROWS
    steps = D_MODEL // tr
    pairs = N_DEV // 2

    def body(dw_ref, o_ref, land_ref, send_buf, land_buf, send_sems, recv_sems, load_sems):
        i = pl.program_id(0)
        x, y, c = lax.axis_index("x"), lax.axis_index("y"), lax.axis_index("c")
        slot = i % 2

        def sends(step, sl):
            return [pltpu.make_async_remote_copy(
                src_ref=send_buf.at[sl, 2 * q + (1 - c)], dst_ref=land_ref.at[q, pl.ds(step * tr, tr), :],
                send_sem=send_sems.at[sl, q], recv_sem=recv_sems.at[step, q],
                device_id=(x, y, 1 - c), device_id_type=pl.DeviceIdType.MESH) for q in range(pairs)]

        @pl.when(i < steps)
        def _():
            @pl.when(i >= 2)
            def _():
                for cp in sends(i - 2, slot):
                    cp.wait_send()

            for s in range(N_DEV):
                send_buf[slot, s] = jnp.concatenate([_lane_window(dw_ref, (), c0, n) for c0, n in plan[s]], axis=1)
            for cp in sends(i, slot):
                cp.start()

        @pl.when(i >= 1)
        def _():
            loads = []
            for q, cp in enumerate(sends(i - 1, 1 - slot)):
                cp.wait_recv()
                loads.append(pltpu.make_async_copy(land_ref.at[q, pl.ds((i - 1) * tr, tr), :], land_buf.at[q], load_sems.at[q]))
                loads[q].start()
            for q in range(pairs):
                loads[q].wait()
                o_ref[q] = (send_buf[1 - slot, 2 * q + c].astype(F32) + land_buf[q].astype(F32)).astype(o_ref.dtype)

        @pl.when(i == steps)
        def _():
            for cp in sends(steps - 2, steps % 2) + sends(steps - 1, 1 - steps % 2):
                cp.wait_send()

    last = steps - 1
    return pl.pallas_call(
        body, name="dw_in_relayout", grid=(steps + 1,),
        in_specs=[pl.BlockSpec((tr, NP), lambda i: (jnp.minimum(i, last), 0))],
        out_specs=[pl.BlockSpec((pairs, tr, SHARD_IN), lambda i: (0, jnp.maximum(i - 1, 0), 0)), ANY],
        out_shape=[jax.ShapeDtypeStruct((pairs, D_MODEL, SHARD_IN), dw.dtype),
                   jax.ShapeDtypeStruct((pairs, D_MODEL, SHARD_IN), dw.dtype)],
        scratch_shapes=[pltpu.VMEM((2, N_DEV, tr, SHARD_IN), dw.dtype), pltpu.VMEM((pairs, tr, SHARD_IN), dw.dtype),
                        pltpu.SemaphoreType.DMA((2, pairs)), pltpu.SemaphoreType.DMA((steps, pairs)),
                        pltpu.SemaphoreType.DMA((pairs,))],
        compiler_params=_params(("arbitrary",), side_effects=True),
    )(dw)[0]


def _rep_heads(a):
    return jnp.repeat(a, HEAD_DIM, axis=1)


SMALL = (("norm_pre_w", 2048), ("conv_b", 4096), ("dt_bias", 32), ("a_log", 32), ("d_skip", 32),
         ("ssd_norm_w", 2048), ("attn_sinks", 16), ("norm_post_w", 2048))
SMALL_USED = sum(size for _, size in SMALL)
SMALL_LEN = 10368


def _pack_small(d, loss=None):
    parts = [d[name].reshape(1, size) for name, size in SMALL]
    tail = jnp.zeros((1, SMALL_LEN - SMALL_USED), F32)
    if loss is not None:
        tail = tail.at[0, 0].set(loss)
    return jnp.concatenate(parts + [tail], axis=1)


def _unpack_small(vec):
    out, off = {}, 0
    for name, size in SMALL:
        out[name] = vec[:, off:off + size]
        off += size
    return out


def kernel(x, meta_tokens, norm_pre_w, w_in, conv_w, conv_b, dt_bias, a_log, d_skip, ssd_norm_w, attn_sinks, w_out, norm_post_w, loss_target, m_meta_tokens, m_norm_pre_w, m_w_in, m_conv_w, m_conv_b, m_dt_bias, m_a_log, m_d_skip, m_ssd_norm_w, m_attn_sinks, m_w_out, m_norm_post_w, v_meta_tokens, v_norm_pre_w, v_w_in, v_conv_w, v_conv_b, v_dt_bias, v_a_log, v_d_skip, v_ssd_norm_w, v_attn_sinks, v_w_out, v_norm_post_w):
    seq = x.shape[1]
    lp = seq + 2 * CHUNK
    x2d = x[0]

    w_all, conv_w_g, meta_g, hn = _gather_w_in(
        w_in[0].astype(BF16), [conv_w[0], meta_tokens], x2d, norm_pre_w)
    conv_w_full = _perm_xbc(jnp.transpose(conv_w_g, (1, 0, 2)).reshape(CONV_WIDTH, D_CONV))
    conv_b_int = _perm_xbc(conv_b)
    meta_full = jnp.transpose(meta_g, (1, 0, 2)).reshape(N_META, D_MODEL)
    h0 = jnp.concatenate([jnp.zeros((PAD_LEAD, D_MODEL), F32), meta_full], axis=0)

    pos = (jnp.arange(lp) - PAD_LEAD).astype(F32)
    half = HEAD_DIM // 2
    inv = ROPE_THETA ** (-jnp.arange(half, dtype=F32) / half)
    ang = pos[:, None] * inv[None, :]
    cos_t = jnp.tile(jnp.cos(ang), (1, 4))
    sin_t = jnp.tile(jnp.concatenate([-jnp.sin(ang), jnp.sin(ang)], axis=1), (1, 2))
    head_of_col = np.arange(D_SSD) // HEAD_DIM
    expand = jnp.asarray((np.arange(128)[:, None] == head_of_col[None, :]).astype(np.float32))
    reduce_t = jnp.asarray((head_of_col[:, None] == np.arange(128)[None, :]).astype(np.float32))
    tri = jnp.asarray(np.tril(np.ones((CHUNK, CHUNK), np.float32)))
    a_rep = _rep_heads(-jnp.exp(a_log))
    dsk_rep = _rep_heads(d_skip)
    dt_bias_pad = jnp.pad(dt_bias, ((0, 0), (0, 128 - SSD_HEADS)))
    sink_stack = jnp.repeat(attn_sinks.reshape(KV_HEADS, REP), CHUNK, axis=1).reshape(KV_HEADS, REP * CHUNK, 1)

    tm = _tile(lp, (1056, 704, 128, 64))
    proj, w_out_g = _matmul(hn, w_all, tm=tm, tn=1536, tk=D_MODEL, out_dtype=F32, name="in_proj",
                            comm=_Comm([(w_out[0].astype(BF16), "gather")]))
    w_out_full = w_out_g.reshape(D_MIX, D_MODEL)
    qr, kr, vb, dt_rep = _act_fwd(proj, cos_t, sin_t, expand, dt_bias_pad)
    mix, ytot, hprev, xbc = _ssd_fwd(proj, conv_w_full, conv_b_int, dt_rep, a_rep, dsk_rep, ssd_norm_w, tri)
    att, mix = _attn_fwd(qr, kr, vb, proj, sink_stack, mix)
    out = _matmul(mix, w_out_full, tm=tm, tn=1024, tk=D_MIX, out_dtype=F32, name="out_proj")
    dout, dy, g_norm_post, loss_part = _post_loss(out, x2d, loss_target[0], norm_post_w)

    dmix = _matmul(dout, w_out_full, trans_b=True, tm=tm, tn=1024, tk=D_MODEL, out_dtype=F32, name="dmix")
    dw_out = _matmul(mix, dout, trans_a=True, tm=512, tn=1024, tk=lp, out_dtype=BF16, name="dw_out")
    dqr, dkr, dv, dg, dsink_rows = _attn_bwd(qr, kr, vb, att, proj, dmix, sink_stack)
    dproj, dxbc, ddt_part, dd_part, da_part, g_ssd_norm, g_out = _ssd_bwd(
        dmix, ytot, proj, xbc, dt_rep, hprev, a_rep, dsk_rep, ssd_norm_w, tri,
        _Comm([(dw_out.reshape(N_DEV, D_MIX // N_DEV, D_MODEL), "scatter")]))
    dproj, dconv_w_int, dconv_b_int = _conv_bwd(dxbc, proj, conv_w_full, conv_b_int, dproj)
    dproj, ddt_bias = _act_bwd(dqr, dkr, dv, dg, ddt_part, cos_t, sin_t, reduce_t, dproj)
    dw_all = _matmul(hn, dproj, trans_a=True, tm=512, tn=1024, tk=lp, out_dtype=BF16, name="dw_in")
    dw_chip = _w_in_chip_slabs(dw_all)
    dhn, g_in = _matmul(dproj, w_all, trans_b=True, tm=tm, tn=1024, tk=1536, out_dtype=F32, name="dhn",
                        comm=_Comm([(dw_chip, "scatter")], scope="chips"))
    grad_x, dh0, g_norm_pre = _prenorm_bwd(dhn, x2d, h0, dy, norm_pre_w)

    dmeta = dh0[PAD_LEAD:, :]
    dconv_w_ref = _unperm_xbc(dconv_w_int)
    heads = lambda part: part.reshape(SSD_HEADS, HEAD_DIM).sum(axis=1).reshape(1, SSD_HEADS)
    small_local = _pack_small({
        "norm_pre_w": g_norm_pre, "conv_b": _unperm_xbc(dconv_b_int), "dt_bias": ddt_bias[:, :SSD_HEADS],
        "a_log": heads(da_part) * (-jnp.exp(a_log)), "d_skip": heads(dd_part), "ssd_norm_w": g_ssd_norm,
        "attn_sinks": dsink_rows.reshape(Q_HEADS, CHUNK).sum(axis=1).reshape(1, Q_HEADS),
        "norm_post_w": g_norm_post}, loss=loss_part[0, 0])
    g_conv, g_meta, g_small = _exchange(
        [(jnp.transpose(dconv_w_ref.reshape(CONV_WIDTH, N_DEV, D_CONV // N_DEV), (1, 0, 2)), "scatter"),
         (jnp.transpose(dmeta.reshape(N_META, N_DEV, D_MODEL // N_DEV), (1, 0, 2)), "scatter"),
         (small_local, "gather")], "exchange_small")

    res = {}
    res["w_in"] = [o[None] for o in _adamw(g_in, w_in[0], m_w_in[0], v_w_in[0], "adamw_w_in")]
    res["w_out"] = [o[None] for o in _adamw(g_out, w_out[0], m_w_out[0], v_w_out[0], "adamw_w_out")]
    res["conv_w"] = [o[None] for o in _adamw(g_conv, conv_w[0], m_conv_w[0], v_conv_w[0], "adamw_conv_w")]
    res["meta_tokens"] = _adamw(g_meta, meta_tokens, m_meta_tokens, v_meta_tokens, "adamw_meta")
    given = dict(norm_pre_w=(norm_pre_w, m_norm_pre_w, v_norm_pre_w), conv_b=(conv_b, m_conv_b, v_conv_b),
                 dt_bias=(dt_bias, m_dt_bias, v_dt_bias), a_log=(a_log, m_a_log, v_a_log),
                 d_skip=(d_skip, m_d_skip, v_d_skip), ssd_norm_w=(ssd_norm_w, m_ssd_norm_w, v_ssd_norm_w),
                 attn_sinks=(attn_sinks, m_attn_sinks, v_attn_sinks),
                 norm_post_w=(norm_post_w, m_norm_post_w, v_norm_post_w))
    packed = [_pack_small({k: t[j] for k, t in given.items()}) for j in range(3)]
    small_out = _adamw(g_small, packed[0], packed[1], packed[2], "adamw_small")
    small_res = [_unpack_small(r) for r in small_out]
    loss = small_out[0][0, SMALL_USED]

    order = ["meta_tokens", "norm_pre_w", "w_in", "conv_w", "conv_b", "dt_bias", "a_log", "d_skip", "ssd_norm_w",
             "attn_sinks", "w_out", "norm_post_w"]
    outs = []
    for j in range(4):
        for name in order:
            outs.append(res[name][j] if name in res else small_res[j][name])
    return (loss, grad_x[None], *outs)
```

```python
import functools

import numpy as np
import jax
import jax.numpy as jnp
from jax import lax
from jax.experimental import pallas as pl
from jax.experimental.pallas import tpu as pltpu

F32 = jnp.float32
BF16 = jnp.bfloat16
HIGHEST = lax.Precision.HIGHEST

N_DEV = 8
D_MODEL = 2048
CHUNK = 64
N_META = 16
PAD_LEAD = CHUNK - N_META
EPS = 1e-6
N_GROUPS = 8
HEADS_PER_GROUP = 4
HEAD_DIM = 64
GROUP_W = HEADS_PER_GROUP * HEAD_DIM
D_STATE = 128
D_SSD = 2048
D_CONV = 4096
SSD_HEADS = 32
CONV_WIDTH = 4
Q_HEADS = 16
KV_HEADS = 4
REP = 4
D_ATT = 1024
D_KV = 256
WINDOW_CHUNKS = 2
BAND = (WINDOW_CHUNKS + 1) * CHUNK
ROPE_THETA = 10000.0
D_MIX = D_SSD + D_ATT
D_IN_PROJ = 8736
SHARD_IN = D_IN_PROJ // N_DEV

OFF_Z, OFF_XBC, OFF_Q, OFF_G, OFF_K, OFF_V, OFF_DT = 0, 2048, 6144, 7168, 8192, 8448, 8704
NP = 9216
TAIL_W = NP - OFF_Q
XBC_BLK = 512
SSD_GPS = 8
SSD_CPS = 2

ADAM_LR, ADAM_B1, ADAM_B2, ADAM_EPS, ADAM_WD, ADAM_STEP = 0.001, 0.9, 0.999, 1e-08, 0.01, 10

VMEM_LIMIT = 48 * 1024 * 1024

NN = (((1,), (0,)), ((), ()))
NT = (((1,), (1,)), ((), ()))
TN = (((0,), (0,)), ((), ()))
ANY = pl.BlockSpec(memory_space=pl.ANY)


def _dot(a, b, dims=NN, precision=None):
    return lax.dot_general(a, b, dims, precision=precision, preferred_element_type=F32)


def _tile(n, prefs):
    for t in prefs:
        if n % t == 0:
            return t
    return n


def _params(sem, vmem=VMEM_LIMIT, side_effects=False):
    return pltpu.CompilerParams(dimension_semantics=sem, vmem_limit_bytes=vmem, has_side_effects=side_effects)


def _sigmoid(x):
    return 1.0 / (1.0 + jnp.exp(-x))


class _Comm:
    def __init__(self, items, scope="devices"):
        self.items = items
        self.scope = scope
        self.slabs = slabs = N_DEV if scope == "devices" else N_DEV // 2
        self.n = n = len(items)
        self.operands = [arr for arr, _ in items]
        self.in_specs = [ANY] * n
        self.out_specs = [ANY] * n
        self.out_shape = [jax.ShapeDtypeStruct((slabs,) + tuple(arr.shape) if kind == "gather" else tuple(arr.shape),
                                               arr.dtype) for arr, kind in items]
        self.scratch = [pltpu.SemaphoreType.DMA((n, slabs - 1)), pltpu.SemaphoreType.DMA((n, slabs - 1)),
                        pltpu.SemaphoreType.DMA((n,))]

    def _places(self):
        pos = (lax.axis_index("x"), lax.axis_index("y"), lax.axis_index("c"))
        if self.scope == "devices":
            index = lambda p: 4 * p[0] + 2 * p[1] + p[2]
            masks = range(1, N_DEV)
        else:
            index = lambda p: 2 * p[0] + p[1]
            masks = (2, 4, 6)
        peers = []
        for k in masks:
            p = tuple(1 - pos[b] if (k >> (2 - b)) & 1 else pos[b] for b in range(3))
            peers.append((p, index(p)))
        return index(pos), peers

    def _copies(self, ins, outs, sems, landed):
        send_sems, recv_sems, local_sems = sems
        me, peers = self._places()
        local, remote = [], []
        for a, (_, kind) in enumerate(self.items):
            own = ins[a] if kind == "gather" else ins[a].at[me]
            local.append(pltpu.make_async_copy(own, outs[a].at[me], local_sems.at[a]))
            for k, (p, pid) in enumerate(peers):
                remote.append(pltpu.make_async_remote_copy(
                    src_ref=ins[a] if kind == "gather" else ins[a].at[pid],
                    dst_ref=outs[a].at[pid if landed else me],
                    send_sem=send_sems.at[a, k], recv_sem=recv_sems.at[a, k],
                    device_id=p, device_id_type=pl.DeviceIdType.MESH))
        return local, remote

    def start(self, ins, outs, sems):
        local, remote = self._copies(ins, outs, sems, landed=False)
        for cp in local + remote:
            cp.start()

    def wait(self, ins, outs, sems):
        local, remote = self._copies(ins, outs, sems, landed=True)
        for cp in remote + local:
            cp.wait()


def _exchange(items, name):
    comm = _Comm(items)
    n = comm.n

    def body(*refs):
        ins, outs, sems = refs[:n], refs[n:2 * n], refs[2 * n:]
        comm.start(ins, outs, sems)
        comm.wait(ins, outs, sems)

    return pl.pallas_call(
        body, name=name, in_specs=comm.in_specs, out_specs=comm.out_specs, out_shape=comm.out_shape,
        scratch_shapes=comm.scratch, compiler_params=pltpu.CompilerParams(has_side_effects=True),
    )(*comm.operands)


class _GatherTwoLevel:
    def __init__(self, arrays):
        self.arrays = arrays
        self.n = n = len(arrays)
        self.operands = list(arrays)
        self.in_specs = [ANY] * n
        self.out_specs = [ANY] * n
        self.out_shape = [jax.ShapeDtypeStruct((N_DEV,) + tuple(a.shape), a.dtype) for a in arrays]
        self.scratch = [pltpu.SemaphoreType.DMA((n, N_DEV - 1)), pltpu.SemaphoreType.DMA((n, N_DEV - 1)),
                        pltpu.SemaphoreType.DMA((n,))]

    def _plan(self, ins, outs, sems):
        send_sems, recv_sems, local_sems = sems
        x, y, c = lax.axis_index("x"), lax.axis_index("y"), lax.axis_index("c")
        me, sibling = (x, y, c), (x, y, 1 - c)
        chips = [(1 - x, y), (x, 1 - y), (1 - x, 1 - y)]

        def slab(a, place):
            return outs[a].at[4 * place[0] + 2 * place[1] + place[2]]

        def copy(a, k, block, to, src=None):
            return pltpu.make_async_remote_copy(
                src_ref=slab(a, block) if src is None else src, dst_ref=slab(a, block),
                send_sem=send_sems.at[a, k], recv_sem=recv_sems.at[a, k],
                device_id=to, device_id_type=pl.DeviceIdType.MESH)

        own, mine = [], []
        for a in range(self.n):
            mine.append(pltpu.make_async_copy(ins[a], slab(a, me), local_sems.at[a]))
            own.append(copy(a, 0, me, sibling, src=ins[a]))
            own += [copy(a, 1 + j, me, (*chip, c), src=ins[a]) for j, chip in enumerate(chips)]
        return me, sibling, chips, c, copy, own, mine

    def start(self, ins, outs, sems):
        _, _, _, _, _, own, mine = self._plan(ins, outs, sems)
        for cp in mine + own:
            cp.start()

    def wait(self, ins, outs, sems):
        me, sibling, chips, c, copy, own, mine = self._plan(ins, outs, sems)
        forwards = []
        for j, chip in enumerate(chips):
            for a in range(self.n):
                copy(a, 1 + j, (*chip, c), me).wait_recv()
                fwd = copy(a, 4 + j, (*chip, c), sibling)
                fwd.start()
                forwards.append(fwd)
        for a in range(self.n):
            copy(a, 0, sibling, me).wait_recv()
            for j, chip in enumerate(chips):
                copy(a, 4 + j, (*chip, 1 - c), me).wait_recv()
        for cp in own + forwards:
            cp.wait_send()
        for loc in mine:
            loc.wait()


def _matmul(a, b, *, tm, tn, tk, out_dtype, name, trans_a=False, trans_b=False, comm=None):
    m, k = (a.shape[1], a.shape[0]) if trans_a else a.shape
    n = b.shape[0] if trans_b else b.shape[1]
    nk = k // tk
    dims = TN if trans_a else (NT if trans_b else NN)
    assert not (trans_a and trans_b)
    nc = comm.n if comm else 0
    grid = (m // tm, n // tn, nk)

    def body(*refs):
        a_ref, b_ref = refs[:2]
        cin = refs[2:2 + nc]
        o_ref = refs[2 + nc]
        cout = refs[3 + nc:3 + 2 * nc]
        scratch = refs[3 + 2 * nc:]
        sems = scratch[len(scratch) - 3:] if comm else None
        i, j, kk = pl.program_id(0), pl.program_id(1), pl.program_id(2)
        if comm:
            @pl.when((i == 0) & (j == 0) & (kk == 0))
            def _():
                comm.start(cin, cout, sems)

        if nk == 1:
            o_ref[...] = _dot(a_ref[...], b_ref[...], dims).astype(out_dtype)
        else:
            acc_ref = scratch[0]

            @pl.when(kk == 0)
            def _():
                acc_ref[...] = jnp.zeros_like(acc_ref)

            acc_ref[...] += _dot(a_ref[...], b_ref[...], dims)

            @pl.when(kk == nk - 1)
            def _():
                o_ref[...] = acc_ref[...].astype(out_dtype)

        if comm:
            @pl.when((i == grid[0] - 1) & (j == grid[1] - 1) & (kk == nk - 1))
            def _():
                comm.wait(cin, cout, sems)

    a_spec = (pl.BlockSpec((tk, tm), lambda i, j, kk: (kk, i)) if trans_a
              else pl.BlockSpec((tm, tk), lambda i, j, kk: (i, kk)))
    b_spec = (pl.BlockSpec((tn, tk), lambda i, j, kk: (j, kk)) if trans_b
              else pl.BlockSpec((tk, tn), lambda i, j, kk: (kk, j)))
    sem = ("arbitrary",) * 3 if comm else ("parallel", "parallel", "arbitrary")
    res = pl.pallas_call(
        body, name=name, grid=grid,
        in_specs=[a_spec, b_spec] + (comm.in_specs if comm else []),
        out_specs=[pl.BlockSpec((tm, tn), lambda i, j, kk: (i, j))] + (comm.out_specs if comm else []),
        out_shape=[jax.ShapeDtypeStruct((m, n), out_dtype)] + (comm.out_shape if comm else []),
        scratch_shapes=([] if nk == 1 else [pltpu.VMEM((tm, tn), F32)]) + (comm.scratch if comm else []),
        compiler_params=_params(sem, side_effects=bool(comm)),
    )(a, b, *(comm.operands if comm else []))
    return res if comm else res[0]


ROW_K = 6
X_K = 8


def _x_specs(nx, k, tile_of=lambda i: i):
    return [pl.BlockSpec((CHUNK, D_MODEL), functools.partial(lambda u, i: (jnp.clip(k * tile_of(i) + u - 1, 0, nx - 1), 0), u))
            for u in range(k)]


def _chunk_of_row(i, k):
    return k * i + lax.broadcasted_iota(jnp.int32, (k * CHUNK, 1), 0) // CHUNK


CONV_COLS = 512
HALO = 8


def _conv_pre(ext, w, b):
    taps = [ext[HALO:, :]] + [pltpu.roll(ext, j, 0)[HALO:, :] for j in range(1, CONV_WIDTH)]
    acc = b + w[3:4, :] * taps[0]
    for j in range(1, CONV_WIDTH):
        acc = acc + w[3 - j:4 - j, :] * taps[j]
    return acc, taps


def _conv_bwd(dxbc, proj, conv_w, conv_b, dproj):
    lp = proj.shape[0]
    t = _tile(lp, (704, 384, 128, 64))
    hb = t // HALO
    nt = lp // t
    c0 = OFF_XBC // CONV_COLS

    def body(dx_ref, dxn_ref, u_ref, up_ref, un_ref, w_ref, b_ref, _, du_ref, dw_ref, db_ref):
        i = pl.program_id(1)
        w = w_ref[...]
        up = up_ref[...] * (i > 0).astype(F32)
        ext = jnp.concatenate([up, u_ref[...], un_ref[...]], axis=0)
        pre, taps = _conv_pre(ext, w, b_ref[...])
        dxn = dxn_ref[...] * (i < nt - 1).astype(F32)
        dxe = jnp.concatenate([dx_ref[...], dxn], axis=0)
        sg = _sigmoid(pre)
        dpre = dxe * sg * (1.0 + pre * (1.0 - sg))
        du = w[3:4, :] * dpre[:t, :]
        for j in range(1, CONV_WIDTH):
            du = du + w[3 - j:4 - j, :] * pltpu.roll(dpre, t + HALO - j, 0)[:t, :]
        du_ref[...] = du.astype(BF16)

        @pl.when(i == 0)
        def _():
            dw_ref[...] = jnp.zeros_like(dw_ref)
            db_ref[...] = jnp.zeros_like(db_ref)

        dp = dpre[:t, :]
        db_ref[...] += jnp.sum(dp, axis=0, keepdims=True)
        for j in range(CONV_WIDTH):
            dw_ref[3 - j:4 - j, :] += jnp.sum(dp * taps[j][:t, :], axis=0, keepdims=True)

    nxt = lambda i: jnp.minimum((i + 1) * hb, lp // HALO - 1)
    return pl.pallas_call(
        body, name="conv_bwd", grid=(D_CONV // CONV_COLS, nt),
        in_specs=[
            pl.BlockSpec((t, CONV_COLS), lambda j, i: (i, j)),
            pl.BlockSpec((HALO, CONV_COLS), lambda j, i: (nxt(i), j)),
            pl.BlockSpec((t, CONV_COLS), lambda j, i: (i, c0 + j)),
            pl.BlockSpec((HALO, CONV_COLS), lambda j, i: (jnp.maximum(i * hb - 1, 0), c0 + j)),
            pl.BlockSpec((HALO, CONV_COLS), lambda j, i: (nxt(i), c0 + j)),
            pl.BlockSpec((CONV_WIDTH, CONV_COLS), lambda j, i: (0, j)),
            pl.BlockSpec((1, CONV_COLS), lambda j, i: (0, j)),
            ANY,
        ],
        out_specs=[
            pl.BlockSpec((t, CONV_COLS), lambda j, i: (i, c0 + j)),
            pl.BlockSpec((CONV_WIDTH, CONV_COLS), lambda j, i: (0, j)),
            pl.BlockSpec((1, CONV_COLS), lambda j, i: (0, j)),
        ],
        out_shape=[
            jax.ShapeDtypeStruct((lp, NP), BF16),
            jax.ShapeDtypeStruct((CONV_WIDTH, D_CONV), F32),
            jax.ShapeDtypeStruct((1, D_CONV), F32),
        ],
        input_output_aliases={7: 0},
        compiler_params=_params(("parallel", "arbitrary")),
    )(dxbc, dxbc, proj, proj, proj, conv_w, conv_b, dproj)


def _swap_halves(t):
    w = t.shape[-1]
    lane = lax.broadcasted_iota(jnp.int32, t.shape, 1)
    return jnp.where((lane % HEAD_DIM) < HEAD_DIM // 2, pltpu.roll(t, w - HEAD_DIM // 2, 1),
                     pltpu.roll(t, HEAD_DIM // 2, 1))


def _act_fwd(proj, cos_t, sin_t, expand, dt_bias_pad):
    lp = proj.shape[0]
    t = _tile(lp, (384, 128, 64))

    def body(q_ref, k_ref, v_ref, dt_ref, cos_ref, sin_ref, ex_ref, bias_ref, qo_ref, ko_ref, vo_ref, dto_ref):
        i = pl.program_id(0)
        cos = cos_ref[...]
        sin = sin_ref[...]
        q = q_ref[...]
        qo_ref[...] = (q * jnp.tile(cos, (1, D_ATT // 128)) + _swap_halves(q) * jnp.tile(sin, (1, D_ATT // 128))).astype(BF16)
        k = k_ref[...]
        ko_ref[...] = (k * jnp.tile(cos, (1, D_KV // 128)) + _swap_halves(k) * jnp.tile(sin, (1, D_KV // 128))).astype(BF16)
        vo_ref[...] = v_ref[...].astype(BF16)
        raw = dt_ref[...] + bias_ref[...]
        sp = jnp.maximum(raw, 0.0) + jnp.log1p(jnp.exp(-jnp.abs(raw)))
        row = i * t + lax.broadcasted_iota(jnp.int32, sp.shape, 0)
        dto_ref[...] = _dot(jnp.where(row >= PAD_LEAD, sp, 0.0), ex_ref[...], NN, HIGHEST)

    return pl.pallas_call(
        body, name="act_fwd", grid=(lp // t,),
        in_specs=[
            pl.BlockSpec((t, D_ATT), lambda i: (i, OFF_Q // D_ATT)),
            pl.BlockSpec((t, D_KV), lambda i: (i, OFF_K // D_KV)),
            pl.BlockSpec((t, D_KV), lambda i: (i, OFF_V // D_KV)),
            pl.BlockSpec((t, 128), lambda i: (i, OFF_DT // 128)),
            pl.BlockSpec((t, 128), lambda i: (i, 0)),
            pl.BlockSpec((t, 128), lambda i: (i, 0)),
            pl.BlockSpec((128, D_SSD), lambda i: (0, 0)),
            pl.BlockSpec((1, 128), lambda i: (0, 0)),
        ],
        out_specs=[
            pl.BlockSpec((t, D_ATT), lambda i: (i, 0)),
            pl.BlockSpec((t, D_KV), lambda i: (i, 0)),
            pl.BlockSpec((t, D_KV), lambda i: (i, 0)),
            pl.BlockSpec((t, D_SSD), lambda i: (i, 0)),
        ],
        out_shape=[
            jax.ShapeDtypeStruct((lp, D_ATT), BF16),
            jax.ShapeDtypeStruct((lp, D_KV), BF16),
            jax.ShapeDtypeStruct((lp, D_KV), BF16),
            jax.ShapeDtypeStruct((lp, D_SSD), F32),
        ],
        compiler_params=_params(("parallel",)),
    )(proj, proj, proj, proj, cos_t, sin_t, expand, dt_bias_pad)


def _act_bwd(dqr, dkr, dv, dg, ddt_raw, cos_t, sin_t, dproj):
    lp = dqr.shape[0]
    t = _tile(lp, (384, 128, 64))

    def body(dq_ref, dk_ref, dv_ref, dg_ref, ddt_ref, cos_ref, sin_ref, _, o_ref, db_ref):
        i = pl.program_id(0)
        cos = cos_ref[...]
        sin = sin_ref[...]
        dq = dq_ref[...]
        dq = dq * jnp.tile(cos, (1, D_ATT // 128)) + _swap_halves(dq * jnp.tile(sin, (1, D_ATT // 128)))
        dk = dk_ref[...]
        dk = dk * jnp.tile(cos, (1, D_KV // 128)) + _swap_halves(dk * jnp.tile(sin, (1, D_KV // 128)))
        ddt = ddt_ref[...]
        o_ref[...] = jnp.concatenate(
            [dq.astype(BF16), dg_ref[...].astype(BF16), dk.astype(BF16), dv_ref[...].astype(BF16), ddt.astype(BF16),
             jnp.zeros((t, NP - OFF_DT - 128), BF16)], axis=1)

        @pl.when(i == 0)
        def _():
            db_ref[...] = jnp.zeros_like(db_ref)

        db_ref[...] += jnp.sum(ddt, axis=0, keepdims=True)

    return pl.pallas_call(
        body, name="act_bwd", grid=(lp // t,),
        in_specs=[
            pl.BlockSpec((t, D_ATT), lambda i: (i, 0)),
            pl.BlockSpec((t, D_KV), lambda i: (i, 0)),
            pl.BlockSpec((t, D_KV), lambda i: (i, 0)),
            pl.BlockSpec((t, D_ATT), lambda i: (i, 0)),
            pl.BlockSpec((t, 128), lambda i: (i, 0)),
            pl.BlockSpec((t, 128), lambda i: (i, 0)),
            pl.BlockSpec((t, 128), lambda i: (i, 0)),
            ANY,
        ],
        out_specs=[pl.BlockSpec((t, TAIL_W), lambda i: (i, OFF_Q // TAIL_W)), pl.BlockSpec((1, 128), lambda i: (0, 0))],
        out_shape=[jax.ShapeDtypeStruct((lp, NP), BF16), jax.ShapeDtypeStruct((1, 128), F32)],
        input_output_aliases={7: 0},
        compiler_params=_params(("arbitrary",)),
    )(dqr, dkr, dv, dg, ddt_raw, cos_t, sin_t, dproj)


def _cs_row(cs):
    row = lax.broadcasted_iota(jnp.int32, cs.shape, 0)
    lane = lax.broadcasted_iota(jnp.int32, cs.shape, 1)
    return jnp.sum(jnp.where(row == lane % HEAD_DIM, cs, 0.0), axis=0, keepdims=True)


def _ssd_fwd(proj, conv_w, conv_b, dt_rep, a_rep, dsk_rep, wn, tri):
    lp = proj.shape[0]
    nc = lp // CHUNK
    assert SSD_GPS == N_GROUPS
    gw = SSD_GPS * GROUP_W
    cps = _tile(nc, (SSD_CPS, 1))

    def body(rawa_ref, rawb_ref, cw_ref, cb_ref, dt_ref, z_ref, a_ref, dsk_ref, wn_ref, tri_ref,
             yn_ref, ytot_ref, hprev_ref, xbc_ref, h_scr, tail_scr):
        @pl.when(pl.program_id(1) == 0)
        def _():
            h_scr[...] = jnp.zeros_like(h_scr)
            tail_scr[...] = jnp.zeros_like(tail_scr)

        for sub in range(cps):
            rs = slice(sub * CHUNK, (sub + 1) * CHUNK)
            raw = jnp.concatenate([rawa_ref[rs, :], rawb_ref[rs, :]], axis=1)
            pre, _ = _conv_pre(jnp.concatenate([tail_scr[...], raw], axis=0), cw_ref[...], cb_ref[...])
            tail_scr[...] = raw[CHUNK - HALO:, :]
            xbc_ref[rs, :] = pre * _sigmoid(pre)

        G = range(SSD_GPS)
        colsl = [slice(gi * GROUP_W, (gi + 1) * GROUP_W) for gi in G]
        rows4 = lax.broadcasted_iota(jnp.int32, (GROUP_W, GROUP_W), 0) // HEAD_DIM
        cols4 = lax.broadcasted_iota(jnp.int32, (GROUP_W, GROUP_W), 1) // HEAD_DIM
        lrow = lax.broadcasted_iota(jnp.int32, (CHUNK, GROUP_W), 0)
        lcol = lax.broadcasted_iota(jnp.int32, (CHUNK, GROUP_W), 1) % HEAD_DIM

        def one_chunk(sub):
            rs = slice(sub * CHUNK, (sub + 1) * CHUNK)
            xbc = [xbc_ref[rs, gi * XBC_BLK:(gi + 1) * XBC_BLK] for gi in G]
            dt = [dt_ref[rs, colsl[gi]] for gi in G]
            xs = [xbc[gi][:, :GROUP_W] for gi in G]
            b = [xbc[gi][:, GROUP_W:GROUP_W + D_STATE].astype(BF16) for gi in G]
            c = [xbc[gi][:, GROUP_W + D_STATE:].astype(BF16) for gi in G]
            hprev = [h_scr[gi] for gi in G]
            cs = [_dot(tri_ref[...], dt[gi] * a_ref[:, colsl[gi]], NN, HIGHEST) for gi in G]
            yoff = [_dot(c[gi], hprev[gi].astype(BF16)) for gi in G]
            cs_t = [_cs_row(cs[gi]) for gi in G]
            xdt = [xs[gi] * dt[gi] for gi in G]
            cs_last = [cs[gi][CHUNK - 1:CHUNK, :] for gi in G]
            st = [_dot(b[gi], (xdt[gi] * jnp.exp(cs_last[gi] - cs[gi])).astype(BF16), TN) for gi in G]
            cb4 = [_dot(c[gi], jnp.concatenate([b[gi]] * HEADS_PER_GROUP, axis=0), NT) for gi in G]
            m = [(cb4[gi] * jnp.exp(jnp.where(lrow >= lcol, cs[gi] - cs_t[gi], -jnp.inf))).astype(BF16) for gi in G]
            xbd = [jnp.where(rows4 == cols4, jnp.concatenate([xdt[gi].astype(BF16)] * HEADS_PER_GROUP, axis=0), 0.0)
                   for gi in G]
            ydiag = [_dot(m[gi], xbd[gi]) for gi in G]
            for gi in G:
                cols = colsl[gi]
                ytot = ydiag[gi] + yoff[gi] * jnp.exp(cs[gi]) + dsk_ref[:, cols] * xs[gi]
                z = z_ref[rs, cols]
                gz = ytot * (z * _sigmoid(z))
                rstd = lax.rsqrt(jnp.mean(gz * gz, axis=-1, keepdims=True) + EPS)
                hprev_ref[sub, gi] = hprev[gi]
                h_scr[gi] = hprev[gi] * jnp.exp(cs_last[gi]) + st[gi]
                ytot_ref[rs, cols] = ytot
                yn_ref[rs, cols] = (gz * rstd * wn_ref[:, cols]).astype(BF16)

        for sub in range(cps):
            one_chunk(sub)

    vec = pl.BlockSpec((1, gw), lambda g, c: (0, g))
    blk = pl.BlockSpec((cps * CHUNK, gw), lambda g, c: (c, g))
    half = D_CONV // 2
    return pl.pallas_call(
        body, name="ssd_fwd", grid=(N_GROUPS // SSD_GPS, nc // cps),
        in_specs=[
            pl.BlockSpec((cps * CHUNK, half), lambda g, c: (c, OFF_XBC // half)),
            pl.BlockSpec((cps * CHUNK, half), lambda g, c: (c, OFF_XBC // half + 1)),
            pl.BlockSpec((CONV_WIDTH, D_CONV), lambda g, c: (0, 0)),
            pl.BlockSpec((1, D_CONV), lambda g, c: (0, 0)),
            blk, blk, vec, vec, vec,
            pl.BlockSpec((CHUNK, CHUNK), lambda g, c: (0, 0)),
        ],
        out_specs=[blk, blk, pl.BlockSpec((cps, SSD_GPS, D_STATE, GROUP_W), lambda g, c: (c, g, 0, 0)),
                   pl.BlockSpec((cps * CHUNK, D_CONV), lambda g, c: (c, 0))],
        out_shape=[
            jax.ShapeDtypeStruct((lp, D_MIX), BF16),
            jax.ShapeDtypeStruct((lp, D_SSD), F32),
            jax.ShapeDtypeStruct((nc, N_GROUPS, D_STATE, GROUP_W), F32),
            jax.ShapeDtypeStruct((lp, D_CONV), F32),
        ],
        scratch_shapes=[pltpu.VMEM((SSD_GPS, D_STATE, GROUP_W), F32), pltpu.VMEM((HALO, D_CONV), F32)],
        compiler_params=_params(("arbitrary", "arbitrary")),
    )(proj, proj, conv_w, conv_b, dt_rep, proj, a_rep, dsk_rep, wn, tri)


def _ssd_bwd(dmix, ytot, proj, xbc, dt_rep, hprev, a_rep, dsk_rep, wn, tri, reduce_t, comm):
    lp = xbc.shape[0]
    nc = lp // CHUNK
    gps = SSD_GPS
    gw = gps * GROUP_W
    cps = _tile(nc, (SSD_CPS, 1))
    ncm = comm.n
    n_in, n_out = 11, 6
    grid = (N_GROUPS // gps, nc // cps)
    assert gps == N_GROUPS

    def all_groups(refs):
        (dyn_ref, ytot_ref, z_ref, xbc_ref, dt_ref, hprev_ref, a_ref, dsk_ref, wn_ref, tri_ref,
         dz_ref, dxbc_ref, ddt_ref, dd_ref, da_ref, dwn_ref, dh_scr) = refs
        G = range(gps)
        H = range(HEADS_PER_GROUP)
        cl = [slice(gi * GROUP_W, (gi + 1) * GROUP_W) for gi in G]
        hl = [slice(r * HEAD_DIM, (r + 1) * HEAD_DIM) for r in H]
        tri = tri_ref[...]
        xbc = [xbc_ref[:, gi * XBC_BLK:(gi + 1) * XBC_BLK] for gi in G]
        dt = [dt_ref[:, cl[gi]] for gi in G]
        a = [a_ref[:, cl[gi]] for gi in G]
        xs = [xbc[gi][:, :GROUP_W] for gi in G]
        bbf = [xbc[gi][:, GROUP_W:GROUP_W + D_STATE].astype(BF16) for gi in G]
        cbf = [xbc[gi][:, GROUP_W + D_STATE:].astype(BF16) for gi in G]
        hprev = [hprev_ref[gi] for gi in G]
        hbf = [hprev[gi].astype(BF16) for gi in G]
        dhn = [dh_scr[gi] for gi in G]
        dhnb = [dhn[gi].astype(BF16) for gi in G]
        cs = [_dot(tri, dt[gi] * a[gi], NN, HIGHEST) for gi in G]
        g = [_dot(cbf[gi], hbf[gi]) for gi in G]
        dxw = [_dot(bbf[gi], dhnb[gi]) for gi in G]
        cs_t = [_cs_row(cs[gi]) for gi in G]
        dy = []
        for gi in G:
            ytot = ytot_ref[:, cl[gi]]
            z = z_ref[:, cl[gi]]
            dyn = dyn_ref[:, cl[gi]]
            sz = _sigmoid(z)
            silu_z = z * sz
            gz = ytot * silu_z
            rstd = lax.rsqrt(jnp.mean(gz * gz, axis=-1, keepdims=True) + EPS)
            xhat = gz * rstd
            dwn_ref[:, cl[gi]] += jnp.sum(dyn * xhat, axis=0, keepdims=True)
            dxhat = dyn * wn_ref[:, cl[gi]]
            dgz = rstd * (dxhat - xhat * jnp.mean(dxhat * xhat, axis=-1, keepdims=True))
            dy.append(dgz * silu_z)
            dz_ref[:, cl[gi]] = (dgz * ytot * (sz * (1.0 + z * (1.0 - sz)))).astype(BF16)
            dd_ref[:, cl[gi]] += jnp.sum(dy[gi] * xs[gi], axis=0, keepdims=True)
        xdt = [xs[gi] * dt[gi] for gi in G]
        e = [jnp.exp(cs[gi]) for gi in G]
        cs_last = [cs[gi][CHUNK - 1:CHUNK, :] for gi in G]
        dte = [jnp.exp(cs_last[gi] - cs[gi]) for gi in G]
        cd = [jnp.exp(cs_last[gi]) for gi in G]
        dgb = [(dy[gi] * e[gi]).astype(BF16) for gi in G]
        dyb = [dy[gi].astype(BF16) for gi in G]
        xdtb = [xdt[gi].astype(BF16) for gi in G]
        dc = [_dot(dgb[gi], hbf[gi], NT) for gi in G]
        dhprev = [_dot(cbf[gi], dgb[gi], TN) for gi in G]
        db = [_dot((xdt[gi] * dte[gi]).astype(BF16), dhnb[gi], NT) for gi in G]
        row = lax.broadcasted_iota(jnp.int32, (CHUNK, CHUNK), 0)
        causal = row >= lax.broadcasted_iota(jnp.int32, (CHUNK, CHUNK), 1)
        cb = [_dot(cbf[gi], bbf[gi], NT) for gi in G]
        dm = [[_dot(dyb[gi][:, hl[r]], xdtb[gi][:, hl[r]], NT) for r in H] for gi in G]
        mb, dseg, dcbb = [], [], []
        for gi in G:
            mb.append([])
            dseg.append([])
            dcb = None
            for r in H:
                seg = cs[gi][:, r * HEAD_DIM:r * HEAD_DIM + 1] - cs_t[gi][:, hl[r]]
                lm = jnp.exp(jnp.where(causal, seg, -jnp.inf))
                m = cb[gi] * lm
                mb[gi].append(m.astype(BF16))
                dseg[gi].append(dm[gi][r] * m)
                dcb = dm[gi][r] * lm if r == 0 else dcb + dm[gi][r] * lm
            dcbb.append(dcb.astype(BF16))
        dxdt_diag = [[_dot(mb[gi][r], dyb[gi][:, hl[r]], TN) for r in H] for gi in G]
        ones = jnp.ones((CHUNK, HEAD_DIM), F32)
        colsum = [[_dot(dseg[gi][r], ones, TN, HIGHEST) for r in H] for gi in G]
        dc2 = [_dot(dcbb[gi], bbf[gi]) for gi in G]
        db2 = [_dot(dcbb[gi], cbf[gi], TN) for gi in G]
        dcs = []
        for gi in G:
            t_dte = dxw[gi] * xdt[gi] * dte[gi]
            dcs_last = (jnp.sum(dhn[gi] * hprev[gi], axis=0, keepdims=True) * cd[gi]
                        + jnp.sum(t_dte, axis=0, keepdims=True))
            diag = jnp.concatenate(
                [(jnp.sum(dseg[gi][r], axis=1, keepdims=True) - colsum[gi][r]) * (1.0 / HEAD_DIM) for r in H], axis=1)
            d = dy[gi] * g[gi] * e[gi] - t_dte + diag
            row = lax.broadcasted_iota(jnp.int32, d.shape, 0)
            dcs.append(d + jnp.where(row == CHUNK - 1, dcs_last, 0.0))
        dda = [_dot(tri, dcs[gi], TN, HIGHEST) for gi in G]
        for gi in G:
            dxdt = dxw[gi] * dte[gi] + jnp.concatenate(dxdt_diag[gi], axis=1)
            da_ref[:, cl[gi]] += jnp.sum(dda[gi] * dt[gi], axis=0, keepdims=True)
            ddt = dda[gi] * a[gi] + dxdt * xs[gi]
            dxs = dsk_ref[:, cl[gi]] * dy[gi] + dxdt * dt[gi]
            ddt_ref[:, cl[gi]] = ddt * (1.0 - jnp.exp(-dt[gi]))
            dxbc_ref[:, gi * XBC_BLK:(gi + 1) * XBC_BLK] = jnp.concatenate(
                [dxs, db[gi] + db2[gi], dc[gi] + dc2[gi]], axis=1)
            dh_scr[gi] = dhprev[gi] + dhn[gi] * cd[gi]

    def body(*refs):
        ins = refs[:n_in]
        cin = refs[n_in:n_in + ncm]
        outs = refs[n_in + ncm:n_in + ncm + n_out]
        cout = refs[n_in + ncm + n_out:n_in + 2 * ncm + n_out]
        dh_scr, ddt_scr = refs[n_in + 2 * ncm + n_out:n_in + 2 * ncm + n_out + 2]
        sems = refs[n_in + 2 * ncm + n_out + 2:]
        g, c = pl.program_id(0), pl.program_id(1)

        @pl.when((g == 0) & (c == 0))
        def _():
            comm.start(cin, cout, sems)

        @pl.when(c == 0)
        def _():
            dh_scr[...] = jnp.zeros_like(dh_scr)
            for ref in outs[3:]:
                ref[...] = jnp.zeros_like(ref)

        for sub in reversed(range(cps)):
            rs = pl.ds(sub * CHUNK, CHUNK)
            chunk_ins = tuple(r.at[rs] for r in ins[:5]) + (ins[5].at[sub],) + ins[6:10]
            chunk_outs = (outs[0].at[rs], outs[1].at[rs], ddt_scr) + outs[3:]
            all_groups(chunk_ins + chunk_outs + (dh_scr,))
            outs[2][rs, :] = _dot(ddt_scr[...], ins[10][...], NN, HIGHEST)

        @pl.when((g == grid[0] - 1) & (c == grid[1] - 1))
        def _():
            comm.wait(cin, cout, sems)

    rev = lambda c: grid[1] - 1 - c
    vec = pl.BlockSpec((1, gw), lambda g, c: (0, g))
    blk = pl.BlockSpec((cps * CHUNK, gw), lambda g, c: (rev(c), g))
    xblk = pl.BlockSpec((cps * CHUNK, gps * XBC_BLK), lambda g, c: (rev(c), g))
    res = pl.pallas_call(
        body, name="ssd_bwd", grid=grid,
        in_specs=[blk, blk, blk, xblk, blk,
                  pl.BlockSpec((cps, gps, D_STATE, GROUP_W), lambda g, c: (rev(c), g, 0, 0)),
                  vec, vec, vec,
                  pl.BlockSpec((CHUNK, CHUNK), lambda g, c: (0, 0)),
                  pl.BlockSpec((D_SSD, 128), lambda g, c: (0, 0))] + comm.in_specs,
        out_specs=[blk, xblk, pl.BlockSpec((cps * CHUNK, 128), lambda g, c: (rev(c), 0)), vec, vec, vec] + comm.out_specs,
        out_shape=[
            jax.ShapeDtypeStruct((lp, NP), BF16),
            jax.ShapeDtypeStruct((lp, D_CONV), F32),
            jax.ShapeDtypeStruct((lp, 128), F32),
            jax.ShapeDtypeStruct((1, D_SSD), F32),
            jax.ShapeDtypeStruct((1, D_SSD), F32),
            jax.ShapeDtypeStruct((1, D_SSD), F32),
        ] + comm.out_shape,
        scratch_shapes=[pltpu.VMEM((gps, D_STATE, GROUP_W), F32), pltpu.VMEM((CHUNK, D_SSD), F32)] + comm.scratch,
        compiler_params=_params(("arbitrary", "arbitrary"), side_effects=True),
    )(dmix, ytot, proj, xbc, dt_rep, hprev, a_rep, dsk_rep, wn, tri, reduce_t, *comm.operands)
    return res


def _stack_heads(t, h):
    return jnp.concatenate([t[:, (REP * h + r) * HEAD_DIM:(REP * h + r + 1) * HEAD_DIM] for r in range(REP)], axis=0)


def _band(t2, t1, t0, h):
    sl = slice(h * HEAD_DIM, (h + 1) * HEAD_DIM)
    return jnp.concatenate([t2[:, sl], t1[:, sl], t0[:, sl]], axis=0)


def _attn_probs(s, sink, qc):
    s = s * (HEAD_DIM ** -0.5)
    key_abs = (qc - WINDOW_CHUNKS) * CHUNK + lax.broadcasted_iota(jnp.int32, s.shape, 1)
    s = jnp.where(key_abs >= PAD_LEAD, s, -jnp.inf)
    m = jnp.maximum(jnp.max(s, axis=-1, keepdims=True), sink)
    p = jnp.exp(s - m)
    ps = jnp.exp(sink - m)
    denom = jnp.sum(p, axis=-1, keepdims=True) + ps
    return p / denom, ps / denom


ATT_QC_FWD = 6
ATT_QC_BWD = 2


def _kv_specs(width, newest_chunk_of, kc):
    return [pl.BlockSpec((CHUNK, width), functools.partial(lambda j, p: (jnp.maximum(newest_chunk_of(p) - j, 0), 0), j))
            for j in range(kc - 1, -1, -1)]


def _attn_fwd(qr, kr, vb, proj, sink_stack, mix):
    lp = qr.shape[0]
    nc = lp // CHUNK
    qn = _tile(nc, (ATT_QC_FWD, 2, 1))
    kn = WINDOW_CHUNKS + qn
    qrows = qn * CHUNK

    def body(q_ref, *rest):
        k_refs, v_refs = rest[:kn], rest[kn:2 * kn]
        g_ref, sink_ref, _, att_ref, mix_ref = rest[2 * kn:]
        p = pl.program_id(0)
        q = q_ref[...]
        ks = [r[...] for r in k_refs]
        vs = [r[...] for r in v_refs]
        units = [(u, h) for u in range(qn) for h in range(KV_HEADS)]
        s = [_dot(_stack_heads(q[u * CHUNK:(u + 1) * CHUNK, :], h), _band(*ks[u:u + 3], h), NT) for u, h in units]
        vbh = [_band(*vs[u:u + 3], h) for u, h in units]
        pn = [_attn_probs(s[i], sink_ref[h], qn * p + u)[0].astype(BF16) for i, (u, h) in enumerate(units)]
        o = [_dot(pn[i], vbh[i]) for i in range(len(units))]
        att = jnp.concatenate(
            [jnp.concatenate([o[u * KV_HEADS + h][r * CHUNK:(r + 1) * CHUNK, :] for h in range(KV_HEADS) for r in range(REP)],
                             axis=1) for u in range(qn)], axis=0)
        att_ref[...] = att
        g = g_ref[...]
        mix_ref[...] = (att * (g * _sigmoid(g))).astype(BF16)

    newest = lambda p: qn * p + qn - 1
    return pl.pallas_call(
        body, name="attn_fwd", grid=(nc // qn,),
        in_specs=[pl.BlockSpec((qrows, D_ATT), lambda p: (p, 0))] + _kv_specs(D_KV, newest, kn) + _kv_specs(D_KV, newest, kn) + [
            pl.BlockSpec((qrows, D_ATT), lambda p: (p, OFF_G // D_ATT)),
            pl.BlockSpec((KV_HEADS, REP * CHUNK, 1), lambda p: (0, 0, 0)),
            ANY,
        ],
        out_specs=[pl.BlockSpec((qrows, D_ATT), lambda p: (p, 0)),
                   pl.BlockSpec((qrows, D_ATT), lambda p: (p, D_SSD // D_ATT))],
        out_shape=[jax.ShapeDtypeStruct((lp, D_ATT), F32), jax.ShapeDtypeStruct((lp, D_MIX), BF16)],
        input_output_aliases={2 * kn + 3: 1},
        compiler_params=_params(("parallel",)),
    )(qr, *([kr] * kn), *([vb] * kn), proj, sink_stack, mix)


def _attn_bwd(qr, kr, vb, att, proj, dmix, sink_stack):
    lp = qr.shape[0]
    nc = lp // CHUNK
    qn = ATT_QC_BWD
    kn = WINDOW_CHUNKS + qn
    assert nc % qn == 0 and WINDOW_CHUNKS % qn == 0
    steps = nc // qn
    qrows = qn * CHUNK
    wrows = kn * CHUNK

    def body(q_ref, *rest):
        k_refs, v_refs = rest[:kn], rest[kn:2 * kn]
        (att_ref, g_ref, do_ref, sink_ref, dq_ref, dk_ref, dv_ref, dg_ref, dsink_ref, dk_acc, dv_acc) = rest[2 * kn:]
        step = pl.program_id(0)

        @pl.when(step == 0)
        def _():
            dk_acc[...] = jnp.zeros_like(dk_acc)
            dv_acc[...] = jnp.zeros_like(dv_acc)
            dsink_ref[...] = jnp.zeros_like(dsink_ref)

        @pl.when(step < steps)
        def _():
            q = q_ref[...]
            ks = [r[...] for r in k_refs]
            vs = [r[...] for r in v_refs]
            att = att_ref[...]
            g = g_ref[...]
            dog = do_ref[...]
            sg = _sigmoid(g)
            dg_ref[...] = dog * att * (sg * (1.0 + g * (1.0 - sg)))
            do = dog * (g * sg)
            units = [(u, h) for u in range(qn) for h in range(KV_HEADS)]
            n = range(len(units))
            rows = [slice(u * CHUNK, (u + 1) * CHUNK) for u in range(qn)]
            qs = [_stack_heads(q[rows[u], :], h) for u, h in units]
            kb = [_band(*ks[u:u + 3], h) for u, h in units]
            vbh = [_band(*vs[u:u + 3], h) for u, h in units]
            dos = [_stack_heads(do[rows[u], :], h) for u, h in units]
            dosb = [dos[i].astype(BF16) for i in n]
            s = [_dot(qs[i], kb[i], NT) for i in n]
            dp = [_dot(dosb[i], vbh[i], NT) for i in n]
            ds, pnb = [], []
            for i, (u, h) in enumerate(units):
                pn, psink = _attn_probs(s[i], sink_ref[h], qn * step + u)
                delta = jnp.sum(dos[i] * _stack_heads(att[rows[u], :], h), axis=-1, keepdims=True)
                ds.append((pn * (dp[i] - delta)).astype(BF16))
                pnb.append(pn.astype(BF16))
                dsink_ref[h] += -(psink * delta)
            dqs = [_dot(ds[i], kb[i]) for i in n]
            dks = [_dot(ds[i], qs[i], TN) for i in n]
            dvs = [_dot(pnb[i], dosb[i], TN) for i in n]
            dq_ref[...] = jnp.concatenate(
                [jnp.concatenate([dqs[u * KV_HEADS + h][r * CHUNK:(r + 1) * CHUNK, :]
                                  for h in range(KV_HEADS) for r in range(REP)], axis=1) for u in range(qn)],
                axis=0) * (HEAD_DIM ** -0.5)
            for u in range(qn):
                band = slice(u * CHUNK, u * CHUNK + BAND)
                dk_acc[band, :] += jnp.concatenate(dks[u * KV_HEADS:(u + 1) * KV_HEADS], axis=1) * (HEAD_DIM ** -0.5)
                dv_acc[band, :] += jnp.concatenate(dvs[u * KV_HEADS:(u + 1) * KV_HEADS], axis=1)

        dk_ref[...] = dk_acc[0:qrows, :]
        dv_ref[...] = dv_acc[0:qrows, :]
        for acc in (dk_acc, dv_acc):
            rest_rows = acc[qrows:wrows, :]
            acc[0:wrows - qrows, :] = rest_rows
            acc[wrows - qrows:wrows, :] = jnp.zeros((qrows, D_KV), F32)

    qp = lambda p: jnp.minimum(p, steps - 1)
    newest = lambda p: qn * qp(p) + qn - 1
    qblk = pl.BlockSpec((qrows, D_ATT), lambda p: (qp(p), 0))
    oldest = pl.BlockSpec((qrows, D_KV), lambda p: (jnp.maximum(p - 1, 0), 0))
    return pl.pallas_call(
        body, name="attn_bwd", grid=(steps + 1,),
        in_specs=[qblk] + _kv_specs(D_KV, newest, kn) + _kv_specs(D_KV, newest, kn) + [
            qblk,
            pl.BlockSpec((qrows, D_ATT), lambda p: (qp(p), OFF_G // D_ATT)),
            pl.BlockSpec((qrows, D_ATT), lambda p: (qp(p), D_SSD // D_ATT)),
            pl.BlockSpec((KV_HEADS, REP * CHUNK, 1), lambda p: (0, 0, 0)),
        ],
        out_specs=[qblk, oldest, oldest, qblk, pl.BlockSpec((KV_HEADS, REP * CHUNK, 1), lambda p: (0, 0, 0))],
        out_shape=[
            jax.ShapeDtypeStruct((lp, D_ATT), F32),
            jax.ShapeDtypeStruct((lp, D_KV), F32),
            jax.ShapeDtypeStruct((lp, D_KV), F32),
            jax.ShapeDtypeStruct((lp, D_ATT), F32),
            jax.ShapeDtypeStruct((KV_HEADS, REP * CHUNK, 1), F32),
        ],
        scratch_shapes=[pltpu.VMEM((wrows, D_KV), F32), pltpu.VMEM((wrows, D_KV), F32)],
        compiler_params=_params(("arbitrary",)),
    )(qr, *([kr] * kn), *([vb] * kn), att, proj, dmix, sink_stack)


def _post_loss(out, x2d, target, w):
    lp = out.shape[0]
    nc = lp // CHUNK
    nx = x2d.shape[0] // CHUNK

    k = _tile(nc, (ROW_K, 3, 2, 1))

    def body(o_ref, *rest):
        x_refs, t_refs = rest[:k], rest[k:2 * k]
        w_ref, do_ref, dy_ref, gw_ref, loss_ref = rest[2 * k:]
        i = pl.program_id(0)

        @pl.when(i == 0)
        def _():
            gw_ref[...] = jnp.zeros_like(gw_ref)
            loss_ref[...] = jnp.zeros_like(loss_ref)

        o = o_ref[...]
        w = w_ref[...]
        rstd = lax.rsqrt(jnp.mean(o * o, axis=-1, keepdims=True) + EPS)
        xhat = o * rstd
        x = jnp.concatenate([r[...] for r in x_refs], axis=0)
        t = jnp.concatenate([r[...] for r in t_refs], axis=0)
        chunk = _chunk_of_row(i, k)
        err = (x + xhat * w - t) * ((chunk > 0) & (chunk <= nx)).astype(F32)
        loss_ref[...] += 0.5 * jnp.sum(jnp.mean(err * err, axis=-1, keepdims=True), axis=0, keepdims=True)
        dy = err * (1.0 / D_MODEL)
        dy_ref[...] = dy
        gw_ref[...] += jnp.sum(dy * xhat, axis=0, keepdims=True)
        dxhat = dy * w
        do_ref[...] = (rstd * (dxhat - xhat * jnp.mean(dxhat * xhat, axis=-1, keepdims=True))).astype(BF16)

    row = pl.BlockSpec((k * CHUNK, D_MODEL), lambda i: (i, 0))
    return pl.pallas_call(
        body, name="post_loss", grid=(nc // k,),
        in_specs=[row] + _x_specs(nx, k) + _x_specs(nx, k) + [pl.BlockSpec((1, D_MODEL), lambda i: (0, 0))],
        out_specs=[row, row, pl.BlockSpec((1, D_MODEL), lambda i: (0, 0)), pl.BlockSpec((1, 128), lambda i: (0, 0))],
        out_shape=[
            jax.ShapeDtypeStruct((lp, D_MODEL), BF16),
            jax.ShapeDtypeStruct((lp, D_MODEL), F32),
            jax.ShapeDtypeStruct((1, D_MODEL), F32),
            jax.ShapeDtypeStruct((1, 128), F32),
        ],
        compiler_params=_params(("arbitrary",)),
    )(out, *([x2d] * k), *([target] * k), w)


def _prenorm_bwd(dhn, x2d, h0, dy, w):
    nx = x2d.shape[0] // CHUNK
    k = _tile(nx, (X_K, 4, 2, 1))
    rows = k * CHUNK

    def backward(h, dhn, w):
        rstd = lax.rsqrt(jnp.mean(h * h, axis=-1, keepdims=True) + EPS)
        xhat = h * rstd
        dxhat = dhn * w
        dh = rstd * (dxhat - xhat * jnp.mean(dxhat * xhat, axis=-1, keepdims=True))
        return dh, jnp.sum(dhn * xhat, axis=0, keepdims=True)

    def body(*refs):
        dhn_refs, dy_refs = refs[:k], refs[k:2 * k]
        x_ref, w_ref, gx_ref, gw_ref = refs[2 * k:]

        @pl.when(pl.program_id(0) == 0)
        def _():
            gw_ref[...] = jnp.zeros_like(gw_ref)

        dh, gw = backward(x_ref[...], jnp.concatenate([r[...] for r in dhn_refs], axis=0), w_ref[...])
        gx_ref[...] = dh + jnp.concatenate([r[...] for r in dy_refs], axis=0)
        gw_ref[...] += gw

    def body_meta(dhn_ref, h0_ref, w_ref, d0_ref, gw_ref):
        d0_ref[...], gw_ref[...] = backward(h0_ref[...], dhn_ref[...], w_ref[...])

    chunk_specs = [pl.BlockSpec((CHUNK, D_MODEL), functools.partial(lambda u, i: (k * i + 1 + u, 0), u)) for u in range(k)]
    first = pl.BlockSpec((CHUNK, D_MODEL), lambda i: (0, 0))
    vec = pl.BlockSpec((1, D_MODEL), lambda i: (0, 0))
    wide = pl.BlockSpec((rows, D_MODEL), lambda i: (i, 0))
    gx, gw_x = pl.pallas_call(
        body, name="prenorm_bwd", grid=(nx // k,),
        in_specs=chunk_specs + chunk_specs + [wide, vec],
        out_specs=[wide, vec],
        out_shape=[jax.ShapeDtypeStruct((nx * CHUNK, D_MODEL), F32), jax.ShapeDtypeStruct((1, D_MODEL), F32)],
        compiler_params=_params(("arbitrary",)),
    )(*([dhn] * k), *([dy] * k), x2d, w)
    d0, gw_0 = pl.pallas_call(
        body_meta, name="prenorm_bwd_meta", grid=(1,),
        in_specs=[first, first, vec], out_specs=[first, vec],
        out_shape=[jax.ShapeDtypeStruct((CHUNK, D_MODEL), F32), jax.ShapeDtypeStruct((1, D_MODEL), F32)],
        compiler_params=_params(("arbitrary",)),
    )(dhn, h0, w)
    return gx, d0, gw_x + gw_0


def _adamw(slabs, w, m, v, name):
    rows, cols = w.shape
    tr = _tile(rows, (256, 128, 64, 16, 8))
    c1 = 1.0 - ADAM_B1 ** ADAM_STEP
    c2 = 1.0 - ADAM_B2 ** ADAM_STEP

    def body(s_ref, w_ref, m_ref, v_ref, g_ref, d_ref, mo_ref, vo_ref):
        g = s_ref[0].astype(F32)
        for k in range(1, slabs.shape[0]):
            g = g + s_ref[k].astype(F32)
        w = w_ref[...]
        m = ADAM_B1 * m_ref[...] + (1.0 - ADAM_B1) * g
        v = ADAM_B2 * v_ref[...] + (1.0 - ADAM_B2) * (g * g)
        g_ref[...] = g
        mo_ref[...] = m
        vo_ref[...] = v
        d_ref[...] = -ADAM_LR * ((m / c1) / (jnp.sqrt(v / c2) + ADAM_EPS) + ADAM_WD * w)

    blk = pl.BlockSpec((tr, cols), lambda i: (i, 0))
    shape = jax.ShapeDtypeStruct((rows, cols), F32)
    return pl.pallas_call(
        body, name=name, grid=(rows // tr,),
        in_specs=[pl.BlockSpec((slabs.shape[0], tr, cols), lambda i: (0, i, 0)), blk, blk, blk],
        out_specs=[blk, blk, blk, blk],
        out_shape=[shape, shape, shape, shape],
        compiler_params=_params(("parallel",)),
    )(slabs, w, m, v)


def _perm_xbc(a):
    lead = a.shape[:-1]
    xs = a[..., :D_SSD].reshape(lead + (N_GROUPS, GROUP_W))
    b = a[..., D_SSD:D_SSD + N_GROUPS * D_STATE].reshape(lead + (N_GROUPS, D_STATE))
    c = a[..., D_SSD + N_GROUPS * D_STATE:].reshape(lead + (N_GROUPS, D_STATE))
    return jnp.concatenate([xs, b, c], axis=-1).reshape(lead + (D_CONV,))


def _unperm_xbc(a):
    lead = a.shape[:-1]
    t = a.reshape(lead + (N_GROUPS, XBC_BLK))
    xs = t[..., :GROUP_W].reshape(lead + (D_SSD,))
    b = t[..., GROUP_W:GROUP_W + D_STATE].reshape(lead + (N_GROUPS * D_STATE,))
    c = t[..., GROUP_W + D_STATE:].reshape(lead + (N_GROUPS * D_STATE,))
    return jnp.concatenate([xs, b, c], axis=-1)


R_Z, R_XBC, R_DT, R_Q, R_K, R_V, R_G = 0, 2048, 6144, 6176, 7200, 7456, 7712


def _internal_of_reference():
    ref = np.arange(D_IN_PROJ)
    out = np.empty(D_IN_PROJ, np.int64)
    out[R_Z:R_XBC] = OFF_Z + ref[:D_SSD]
    xs = np.arange(D_SSD)
    out[R_XBC:R_XBC + D_SSD] = OFF_XBC + (xs // GROUP_W) * XBC_BLK + xs % GROUP_W
    bc = np.arange(N_GROUPS * D_STATE)
    out[R_XBC + D_SSD:R_XBC + D_SSD + N_GROUPS * D_STATE] = OFF_XBC + (bc // D_STATE) * XBC_BLK + GROUP_W + bc % D_STATE
    out[R_XBC + D_SSD + N_GROUPS * D_STATE:R_DT] = OFF_XBC + (bc // D_STATE) * XBC_BLK + GROUP_W + D_STATE + bc % D_STATE
    out[R_DT:R_Q] = OFF_DT + np.arange(SSD_HEADS)
    out[R_Q:R_K] = OFF_Q + np.arange(D_ATT)
    out[R_K:R_V] = OFF_K + np.arange(D_KV)
    out[R_V:R_G] = OFF_V + np.arange(D_KV)
    out[R_G:] = OFF_G + np.arange(D_ATT)
    return out


def _runs(src, dst_break):
    runs, lo = [], 0
    for i in range(1, len(src) + 1):
        if i == len(src) or src[i] != src[i - 1] + 1 or dst_break[i] != dst_break[i - 1]:
            runs.append((lo, i))
            lo = i
    return runs


RELAYOUT_ROWS = 256


def _lane_window(ref, lead, c0, n):
    a0 = c0 // 128 * 128
    a1 = min(-(-(c0 + n) // 128) * 128, ref.shape[-1])
    return ref[lead + (slice(None), slice(a0, a1))][:, c0 - a0:c0 - a0 + n]


def _gather_w_in(w_shard, small, x2d, norm_w):
    int_of_ref = _internal_of_reference()
    ref_of_int = np.full(NP, -1, np.int64)
    ref_of_int[int_of_ref] = np.arange(D_IN_PROJ)
    shard = np.where(ref_of_int >= 0, ref_of_int // SHARD_IN, -1)
    src = np.where(ref_of_int >= 0, ref_of_int, -10 - 2 * np.arange(NP))
    plan, zeros = [], 0
    for lo, hi in _runs(src, shard):
        if ref_of_int[lo] < 0:
            zeros += hi - lo
            continue
        if zeros:
            plan.append((None, 0, zeros))
            zeros = 0
        plan.append((int(ref_of_int[lo] // SHARD_IN), int(ref_of_int[lo] % SHARD_IN), hi - lo))
    if zeros:
        plan.append((None, 0, zeros))
    tr = RELAYOUT_ROWS
    steps = D_MODEL // tr
    side = _GatherTwoLevel(small)
    ns = side.n
    nx = x2d.shape[0] // CHUNK
    nc = nx + 2
    k = _tile(nc, (11, ROW_K, 3, 2, 1))
    pn = nc // k
    n = max(pn, steps)
    tile_of = lambda i: jnp.where(i < pn - 1, i + 1, 0)

    def body(w_ref, *rest):
        s_in, x_refs, nw_ref = rest[:ns], rest[ns:ns + k], rest[ns + k]
        o_ref, land_ref, s_out, hn_ref = rest[ns + k + 1], rest[ns + k + 2], rest[ns + k + 3:2 * ns + k + 3], rest[2 * ns + k + 3]
        tile_buf, send_sems, recv_sems, load_sems, local_sem, meta_buf, meta_sem = rest[2 * ns + k + 4:2 * ns + k + 11]
        side_sems = rest[2 * ns + k + 11:]
        i = pl.program_id(0)
        x, y, c = lax.axis_index("x"), lax.axis_index("y"), lax.axis_index("c")
        me, sibling = (x, y, c), (x, y, 1 - c)
        chips = [(1 - x, y), (x, 1 - y), (1 - x, 1 - y)]

        def prenorm_tile(t, head=None):
            h = jnp.concatenate([r[...] for r in x_refs], axis=0)
            if head is not None:
                h = jnp.concatenate([head, h[CHUNK:, :]], axis=0)
            h = h * (_chunk_of_row(t, k) <= nx).astype(F32)
            rstd = lax.rsqrt(jnp.mean(h * h, axis=-1, keepdims=True) + EPS)
            hn_ref[...] = (h * rstd * nw_ref[...]).astype(BF16)

        if pn > 1:
            @pl.when(i < pn - 1)
            def _():
                prenorm_tile(i + 1)

        def rows(t):
            return pl.ds(t * tr, tr)

        def tile(place, t):
            return land_ref.at[4 * place[0] + 2 * place[1] + place[2], rows(t), :]

        def copy(t, k, block, to, src=None):
            return pltpu.make_async_remote_copy(
                src_ref=tile(block, t) if src is None else src, dst_ref=tile(block, t),
                send_sem=send_sems.at[t, k], recv_sem=recv_sems.at[t, k], device_id=to, device_id_type=pl.DeviceIdType.MESH)

        def own_sends(t):
            mine = w_ref.at[rows(t), :]
            return [copy(t, 0, me, sibling, src=mine)] + [copy(t, 1 + j, me, (*chip, c), src=mine)
                                                          for j, chip in enumerate(chips)]

        own = pltpu.make_async_copy(w_ref, land_ref.at[4 * x + 2 * y + c], local_sem)

        @pl.when(i < steps)
        def _():
            @pl.when(i == 0)
            def _():
                side.start(s_in, s_out, side_sems)
                own.start()
                for t in range(steps):
                    for cp in own_sends(t):
                        cp.start()
                own.wait()

            forwards = []
            for j, chip in enumerate(chips):
                copy(i, 1 + j, (*chip, c), me).wait_recv()
                forwards.append(copy(i, 4 + j, (*chip, c), sibling))
                forwards[j].start()
            copy(i, 0, sibling, me).wait_recv()
            for j, chip in enumerate(chips):
                copy(i, 4 + j, (*chip, 1 - c), me).wait_recv()
            loads = [pltpu.make_async_copy(land_ref.at[s, rows(i), :], tile_buf.at[s], load_sems.at[s]) for s in range(N_DEV)]
            for cp in loads:
                cp.start()
            for cp in loads:
                cp.wait()
            o_ref[...] = jnp.concatenate(
                [jnp.zeros((tr, w), o_ref.dtype) if s is None else _lane_window(tile_buf, (s,), c0, w) for s, c0, w in plan],
                axis=1)
            for cp in forwards:
                cp.wait_send()

            @pl.when(i == steps - 1)
            def _():
                for t in range(steps):
                    for cp in own_sends(t):
                        cp.wait_send()
                side.wait(s_in, s_out, side_sems)

        @pl.when(i == n - 1)
        def _():
            meta = pltpu.make_async_copy(s_out[ns - 1], meta_buf, meta_sem)
            meta.start()
            meta.wait()
            tokens = jnp.concatenate([meta_buf[s] for s in range(N_DEV)], axis=1)
            prenorm_tile(0, jnp.concatenate([jnp.zeros((PAD_LEAD, D_MODEL), F32), tokens], axis=0))

    last = steps - 1
    res = pl.pallas_call(
        body, name="gather_w_in", grid=(n,),
        in_specs=[ANY] + side.in_specs + _x_specs(nx, k, tile_of) + [pl.BlockSpec((1, D_MODEL), lambda i: (0, 0))],
        out_specs=[pl.BlockSpec((tr, NP), lambda i: (jnp.minimum(i, last), 0)), ANY] + side.out_specs + [
            pl.BlockSpec((k * CHUNK, D_MODEL), lambda i: (tile_of(i), 0))],
        out_shape=[jax.ShapeDtypeStruct((D_MODEL, NP), w_shard.dtype),
                   jax.ShapeDtypeStruct((N_DEV, D_MODEL, SHARD_IN), w_shard.dtype)] + side.out_shape + [
            jax.ShapeDtypeStruct((nc * CHUNK, D_MODEL), BF16)],
        scratch_shapes=[pltpu.VMEM((N_DEV, tr, SHARD_IN), w_shard.dtype),
                        pltpu.SemaphoreType.DMA((steps, N_DEV - 1)), pltpu.SemaphoreType.DMA((steps, N_DEV - 1)),
                        pltpu.SemaphoreType.DMA((N_DEV,)), pltpu.SemaphoreType.DMA(()),
                        pltpu.VMEM((N_DEV,) + tuple(small[ns - 1].shape), F32),
                        pltpu.SemaphoreType.DMA(())] + side.scratch,
        compiler_params=_params(("arbitrary",), side_effects=True),
    )(w_shard, *side.operands, *([x2d] * k), norm_w)
    return [res[0]] + list(res[2:])


def _w_in_chip_slabs(dw):
    int_of_ref = _internal_of_reference()
    plan = []
    for s in range(N_DEV):
        cols = int_of_ref[s * SHARD_IN:(s + 1) * SHARD_IN]
        plan.append([(int(cols[lo]), hi - lo) for lo, hi in _runs(cols, np.zeros_like(cols))])
    tr = RELAYOUT_ROWS
    steps = D_MODEL // tr
    pairs = N_DEV // 2

    def body(dw_ref, o_ref, land_ref, send_buf, land_buf, send_sems, recv_sems, load_sems):
        i = pl.program_id(0)
        x, y, c = lax.axis_index("x"), lax.axis_index("y"), lax.axis_index("c")
        slot = i % 2

        def sends(step, sl):
            return [pltpu.make_async_remote_copy(
                src_ref=send_buf.at[sl, 2 * q + (1 - c)], dst_ref=land_ref.at[q, pl.ds(step * tr, tr), :],
                send_sem=send_sems.at[sl, q], recv_sem=recv_sems.at[step, q],
                device_id=(x, y, 1 - c), device_id_type=pl.DeviceIdType.MESH) for q in range(pairs)]

        @pl.when(i < steps)
        def _():
            @pl.when(i >= 2)
            def _():
                for cp in sends(i - 2, slot):
                    cp.wait_send()

            for s in range(N_DEV):
                send_buf[slot, s] = jnp.concatenate([_lane_window(dw_ref, (), c0, n) for c0, n in plan[s]], axis=1)
            for cp in sends(i, slot):
                cp.start()

        @pl.when(i >= 1)
        def _():
            loads = []
            for q, cp in enumerate(sends(i - 1, 1 - slot)):
                cp.wait_recv()
                loads.append(pltpu.make_async_copy(land_ref.at[q, pl.ds((i - 1) * tr, tr), :], land_buf.at[q], load_sems.at[q]))
                loads[q].start()
            for q in range(pairs):
                loads[q].wait()
                o_ref[q] = (send_buf[1 - slot, 2 * q + c].astype(F32) + land_buf[q].astype(F32)).astype(o_ref.dtype)

        @pl.when(i == steps)
        def _():
            for cp in sends(steps - 2, steps % 2) + sends(steps - 1, 1 - steps % 2):
                cp.wait_send()

    last = steps - 1
    return pl.pallas_call(
        body, name="dw_in_relayout", grid=(steps + 1,),
        in_specs=[pl.BlockSpec((tr, NP), lambda i: (jnp.minimum(i, last), 0))],
        out_specs=[pl.BlockSpec((pairs, tr, SHARD_IN), lambda i: (0, jnp.maximum(i - 1, 0), 0)), ANY],
        out_shape=[jax.ShapeDtypeStruct((pairs, D_MODEL, SHARD_IN), dw.dtype),
                   jax.ShapeDtypeStruct((pairs, D_MODEL, SHARD_IN), dw.dtype)],
        scratch_shapes=[pltpu.VMEM((2, N_DEV, tr, SHARD_IN), dw.dtype), pltpu.VMEM((pairs, tr, SHARD_IN), dw.dtype),
                        pltpu.SemaphoreType.DMA((2, pairs)), pltpu.SemaphoreType.DMA((steps, pairs)),
                        pltpu.SemaphoreType.DMA((pairs,))],
        compiler_params=_params(("arbitrary",), side_effects=True),
    )(dw)[0]


def _rep_heads(a):
    return jnp.repeat(a, HEAD_DIM, axis=1)


SMALL = (("norm_pre_w", 2048), ("conv_b", 4096), ("dt_bias", 32), ("a_log", 32), ("d_skip", 32),
         ("ssd_norm_w", 2048), ("attn_sinks", 16), ("norm_post_w", 2048))
SMALL_USED = sum(size for _, size in SMALL)
SMALL_LEN = 10368


def _pack_small(d, loss=None):
    parts = [d[name].reshape(1, size) for name, size in SMALL]
    tail = jnp.zeros((1, SMALL_LEN - SMALL_USED), F32)
    if loss is not None:
        tail = tail.at[0, 0].set(loss)
    return jnp.concatenate(parts + [tail], axis=1)


def _unpack_small(vec):
    out, off = {}, 0
    for name, size in SMALL:
        out[name] = vec[:, off:off + size]
        off += size
    return out


def kernel(x, meta_tokens, norm_pre_w, w_in, conv_w, conv_b, dt_bias, a_log, d_skip, ssd_norm_w, attn_sinks, w_out, norm_post_w, loss_target, m_meta_tokens, m_norm_pre_w, m_w_in, m_conv_w, m_conv_b, m_dt_bias, m_a_log, m_d_skip, m_ssd_norm_w, m_attn_sinks, m_w_out, m_norm_post_w, v_meta_tokens, v_norm_pre_w, v_w_in, v_conv_w, v_conv_b, v_dt_bias, v_a_log, v_d_skip, v_ssd_norm_w, v_attn_sinks, v_w_out, v_norm_post_w):
    seq = x.shape[1]
    lp = seq + 2 * CHUNK
    x2d = x[0]

    w_all, conv_w_g, meta_g, hn = _gather_w_in(
        w_in[0].astype(BF16), [conv_w[0], meta_tokens], x2d, norm_pre_w)
    conv_w_full = _perm_xbc(jnp.transpose(conv_w_g, (1, 0, 2)).reshape(CONV_WIDTH, D_CONV))
    conv_b_int = _perm_xbc(conv_b)
    meta_full = jnp.transpose(meta_g, (1, 0, 2)).reshape(N_META, D_MODEL)
    h0 = jnp.concatenate([jnp.zeros((PAD_LEAD, D_MODEL), F32), meta_full], axis=0)

    pos = (jnp.arange(lp) - PAD_LEAD).astype(F32)
    half = HEAD_DIM // 2
    inv = ROPE_THETA ** (-jnp.arange(half, dtype=F32) / half)
    ang = pos[:, None] * inv[None, :]
    cos_t = jnp.tile(jnp.cos(ang), (1, 4))
    sin_t = jnp.tile(jnp.concatenate([-jnp.sin(ang), jnp.sin(ang)], axis=1), (1, 2))
    head_of_col = np.arange(D_SSD) // HEAD_DIM
    expand = jnp.asarray((np.arange(128)[:, None] == head_of_col[None, :]).astype(np.float32))
    reduce_t = jnp.asarray((head_of_col[:, None] == np.arange(128)[None, :]).astype(np.float32))
    tri = jnp.asarray(np.tril(np.ones((CHUNK, CHUNK), np.float32)))
    a_rep = _rep_heads(-jnp.exp(a_log))
    dsk_rep = _rep_heads(d_skip)
    dt_bias_pad = jnp.pad(dt_bias, ((0, 0), (0, 128 - SSD_HEADS)))
    sink_stack = jnp.repeat(attn_sinks.reshape(KV_HEADS, REP), CHUNK, axis=1).reshape(KV_HEADS, REP * CHUNK, 1)

    tm = _tile(lp, (1056, 704, 128, 64))
    proj, w_out_g = _matmul(hn, w_all, tm=tm, tn=1536, tk=D_MODEL, out_dtype=F32, name="in_proj",
                            comm=_Comm([(w_out[0].astype(BF16), "gather")]))
    w_out_full = w_out_g.reshape(D_MIX, D_MODEL)
    qr, kr, vb, dt_rep = _act_fwd(proj, cos_t, sin_t, expand, dt_bias_pad)
    mix, ytot, hprev, xbc = _ssd_fwd(proj, conv_w_full, conv_b_int, dt_rep, a_rep, dsk_rep, ssd_norm_w, tri)
    att, mix = _attn_fwd(qr, kr, vb, proj, sink_stack, mix)
    out = _matmul(mix, w_out_full, tm=tm, tn=1024, tk=D_MIX, out_dtype=F32, name="out_proj")
    dout, dy, g_norm_post, loss_part = _post_loss(out, x2d, loss_target[0], norm_post_w)

    dmix = _matmul(dout, w_out_full, trans_b=True, tm=tm, tn=1024, tk=D_MODEL, out_dtype=F32, name="dmix")
    dw_out = _matmul(mix, dout, trans_a=True, tm=512, tn=1024, tk=lp, out_dtype=BF16, name="dw_out")
    dqr, dkr, dv, dg, dsink_rows = _attn_bwd(qr, kr, vb, att, proj, dmix, sink_stack)
    dproj, dxbc, ddt_raw, dd_part, da_part, g_ssd_norm, g_out = _ssd_bwd(
        dmix, ytot, proj, xbc, dt_rep, hprev, a_rep, dsk_rep, ssd_norm_w, tri, reduce_t,
        _Comm([(dw_out.reshape(N_DEV, D_MIX // N_DEV, D_MODEL), "scatter")]))
    dproj, dconv_w_int, dconv_b_int = _conv_bwd(dxbc, proj, conv_w_full, conv_b_int, dproj)
    dproj, ddt_bias = _act_bwd(dqr, dkr, dv, dg, ddt_raw, cos_t, sin_t, dproj)
    dw_all = _matmul(hn, dproj, trans_a=True, tm=512, tn=1024, tk=lp, out_dtype=BF16, name="dw_in")
    dw_chip = _w_in_chip_slabs(dw_all)
    dhn, g_in = _matmul(dproj, w_all, trans_b=True, tm=tm, tn=1024, tk=1536, out_dtype=F32, name="dhn",
                        comm=_Comm([(dw_chip, "scatter")], scope="chips"))
    grad_x, dh0, g_norm_pre = _prenorm_bwd(dhn, x2d, h0, dy, norm_pre_w)

    dmeta = dh0[PAD_LEAD:, :]
    dconv_w_ref = _unperm_xbc(dconv_w_int)
    heads = lambda part: part.reshape(SSD_HEADS, HEAD_DIM).sum(axis=1).reshape(1, SSD_HEADS)
    small_local = _pack_small({
        "norm_pre_w": g_norm_pre, "conv_b": _unperm_xbc(dconv_b_int), "dt_bias": ddt_bias[:, :SSD_HEADS],
        "a_log": heads(da_part) * (-jnp.exp(a_log)), "d_skip": heads(dd_part), "ssd_norm_w": g_ssd_norm,
        "attn_sinks": dsink_rows.reshape(Q_HEADS, CHUNK).sum(axis=1).reshape(1, Q_HEADS),
        "norm_post_w": g_norm_post}, loss=loss_part[0, 0])
    g_conv, g_meta, g_small = _exchange(
        [(jnp.transpose(dconv_w_ref.reshape(CONV_WIDTH, N_DEV, D_CONV // N_DEV), (1, 0, 2)), "scatter"),
         (jnp.transpose(dmeta.reshape(N_META, N_DEV, D_MODEL // N_DEV), (1, 0, 2)), "scatter"),
         (small_local, "gather")], "exchange_small")

    res = {}
    res["w_in"] = [o[None] for o in _adamw(g_in, w_in[0], m_w_in[0], v_w_in[0], "adamw_w_in")]
    res["w_out"] = [o[None] for o in _adamw(g_out, w_out[0], m_w_out[0], v_w_out[0], "adamw_w_out")]
    res["conv_w"] = [o[None] for o in _adamw(g_conv, conv_w[0], m_conv_w[0], v_conv_w[0], "adamw_conv_w")]
    res["meta_tokens"] = _adamw(g_meta, meta_tokens, m_meta_tokens, v_meta_tokens, "adamw_meta")
    given = dict(norm_pre_w=(norm_pre_w, m_norm_pre_w, v_norm_pre_w), conv_b=(conv_b, m_conv_b, v_conv_b),
                 dt_bias=(dt_bias, m_dt_bias, v_dt_bias), a_log=(a_log, m_a_log, v_a_log),
                 d_skip=(d_skip, m_d_skip, v_d_skip), ssd_norm_w=(ssd_norm_w, m_ssd_norm_w, v_ssd_norm_w),
                 attn_sinks=(attn_sinks, m_attn_sinks, v_attn_sinks),
                 norm_post_w=(norm_post_w, m_norm_post_w, v_norm_post_w))
    packed = [_pack_small({k: t[j] for k, t in given.items()}) for j in range(3)]
    small_out = _adamw(g_small, packed[0], packed[1], packed[2], "adamw_small")
    small_res = [_unpack_small(r) for r in small_out]
    loss = small_out[0][0, SMALL_USED]

    order = ["meta_tokens", "norm_pre_w", "w_in", "conv_w", "conv_b", "dt_bias", "a_log", "d_skip", "ssd_norm_w",
             "attn_sinks", "w_out", "norm_post_w"]
    outs = []
    for j in range(4):
        for name in order:
            outs.append(res[name][j] if name in res else small_res[j][name])
    return (loss, grad_x[None], *outs)
```

```python
import functools

import numpy as np
import jax
import jax.numpy as jnp
from jax import lax
from jax.experimental import pallas as pl
from jax.experimental.pallas import tpu as pltpu

F32 = jnp.float32
BF16 = jnp.bfloat16
HIGHEST = lax.Precision.HIGHEST

N_DEV = 8
D_MODEL = 2048
CHUNK = 64
N_META = 16
PAD_LEAD = CHUNK - N_META
EPS = 1e-6
N_GROUPS = 8
HEADS_PER_GROUP = 4
HEAD_DIM = 64
GROUP_W = HEADS_PER_GROUP * HEAD_DIM
D_STATE = 128
D_SSD = 2048
D_CONV = 4096
SSD_HEADS = 32
CONV_WIDTH = 4
Q_HEADS = 16
KV_HEADS = 4
REP = 4
D_ATT = 1024
D_KV = 256
WINDOW_CHUNKS = 2
BAND = (WINDOW_CHUNKS + 1) * CHUNK
ROPE_THETA = 10000.0
D_MIX = D_SSD + D_ATT
D_IN_PROJ = 8736
SHARD_IN = D_IN_PROJ // N_DEV

OFF_Z, OFF_XBC, OFF_Q, OFF_G, OFF_K, OFF_V, OFF_DT = 0, 2048, 6144, 7168, 8192, 8448, 8704
NP = 9216
TAIL_W = NP - OFF_Q
XBC_BLK = 512
SSD_GPS = 8
SSD_CPS = 2

ADAM_LR, ADAM_B1, ADAM_B2, ADAM_EPS, ADAM_WD, ADAM_STEP = 0.001, 0.9, 0.999, 1e-08, 0.01, 10

VMEM_LIMIT = 48 * 1024 * 1024

NN = (((1,), (0,)), ((), ()))
NT = (((1,), (1,)), ((), ()))
TN = (((0,), (0,)), ((), ()))
ANY = pl.BlockSpec(memory_space=pl.ANY)


def _dot(a, b, dims=NN, precision=None):
    return lax.dot_general(a, b, dims, precision=precision, preferred_element_type=F32)


def _tile(n, prefs):
    for t in prefs:
        if n % t == 0:
            return t
    return n


def _params(sem, vmem=VMEM_LIMIT, side_effects=False):
    return pltpu.CompilerParams(dimension_semantics=sem, vmem_limit_bytes=vmem, has_side_effects=side_effects)


def _sigmoid(x):
    return 1.0 / (1.0 + jnp.exp(-x))


class _Comm:
    def __init__(self, items, scope="devices"):
        self.items = items
        self.scope = scope
        self.slabs = slabs = N_DEV if scope == "devices" else N_DEV // 2
        self.n = n = len(items)
        self.operands = [arr for arr, _ in items]
        self.in_specs = [ANY] * n
        self.out_specs = [ANY] * n
        self.out_shape = [jax.ShapeDtypeStruct((slabs,) + tuple(arr.shape) if kind == "gather" else tuple(arr.shape),
                                               arr.dtype) for arr, kind in items]
        self.scratch = [pltpu.SemaphoreType.DMA((n, slabs - 1)), pltpu.SemaphoreType.DMA((n, slabs - 1)),
                        pltpu.SemaphoreType.DMA((n,))]

    def _places(self):
        pos = (lax.axis_index("x"), lax.axis_index("y"), lax.axis_index("c"))
        if self.scope == "devices":
            index = lambda p: 4 * p[0] + 2 * p[1] + p[2]
            masks = range(1, N_DEV)
        else:
            index = lambda p: 2 * p[0] + p[1]
            masks = (2, 4, 6)
        peers = []
        for k in masks:
            p = tuple(1 - pos[b] if (k >> (2 - b)) & 1 else pos[b] for b in range(3))
            peers.append((p, index(p)))
        return index(pos), peers

    def _copies(self, ins, outs, sems, landed):
        send_sems, recv_sems, local_sems = sems
        me, peers = self._places()
        local, remote = [], []
        for a, (_, kind) in enumerate(self.items):
            own = ins[a] if kind == "gather" else ins[a].at[me]
            local.append(pltpu.make_async_copy(own, outs[a].at[me], local_sems.at[a]))
            for k, (p, pid) in enumerate(peers):
                remote.append(pltpu.make_async_remote_copy(
                    src_ref=ins[a] if kind == "gather" else ins[a].at[pid],
                    dst_ref=outs[a].at[pid if landed else me],
                    send_sem=send_sems.at[a, k], recv_sem=recv_sems.at[a, k],
                    device_id=p, device_id_type=pl.DeviceIdType.MESH))
        return local, remote

    def start(self, ins, outs, sems):
        local, remote = self._copies(ins, outs, sems, landed=False)
        for cp in local + remote:
            cp.start()

    def wait(self, ins, outs, sems):
        local, remote = self._copies(ins, outs, sems, landed=True)
        for cp in remote + local:
            cp.wait()


def _exchange(items, name):
    comm = _Comm(items)
    n = comm.n

    def body(*refs):
        ins, outs, sems = refs[:n], refs[n:2 * n], refs[2 * n:]
        comm.start(ins, outs, sems)
        comm.wait(ins, outs, sems)

    return pl.pallas_call(
        body, name=name, in_specs=comm.in_specs, out_specs=comm.out_specs, out_shape=comm.out_shape,
        scratch_shapes=comm.scratch, compiler_params=pltpu.CompilerParams(has_side_effects=True),
    )(*comm.operands)


class _GatherTwoLevel:
    def __init__(self, arrays):
        self.arrays = arrays
        self.n = n = len(arrays)
        self.operands = list(arrays)
        self.in_specs = [ANY] * n
        self.out_specs = [ANY] * n
        self.out_shape = [jax.ShapeDtypeStruct((N_DEV,) + tuple(a.shape), a.dtype) for a in arrays]
        self.scratch = [pltpu.SemaphoreType.DMA((n, N_DEV - 1)), pltpu.SemaphoreType.DMA((n, N_DEV - 1)),
                        pltpu.SemaphoreType.DMA((n,))]

    def _plan(self, ins, outs, sems):
        send_sems, recv_sems, local_sems = sems
        x, y, c = lax.axis_index("x"), lax.axis_index("y"), lax.axis_index("c")
        me, sibling = (x, y, c), (x, y, 1 - c)
        chips = [(1 - x, y), (x, 1 - y), (1 - x, 1 - y)]

        def slab(a, place):
            return outs[a].at[4 * place[0] + 2 * place[1] + place[2]]

        def copy(a, k, block, to, src=None):
            return pltpu.make_async_remote_copy(
                src_ref=slab(a, block) if src is None else src, dst_ref=slab(a, block),
                send_sem=send_sems.at[a, k], recv_sem=recv_sems.at[a, k],
                device_id=to, device_id_type=pl.DeviceIdType.MESH)

        own, mine = [], []
        for a in range(self.n):
            mine.append(pltpu.make_async_copy(ins[a], slab(a, me), local_sems.at[a]))
            own.append(copy(a, 0, me, sibling, src=ins[a]))
            own += [copy(a, 1 + j, me, (*chip, c), src=ins[a]) for j, chip in enumerate(chips)]
        return me, sibling, chips, c, copy, own, mine

    def start(self, ins, outs, sems):
        _, _, _, _, _, own, mine = self._plan(ins, outs, sems)
        for cp in mine + own:
            cp.start()

    def wait(self, ins, outs, sems):
        me, sibling, chips, c, copy, own, mine = self._plan(ins, outs, sems)
        forwards = []
        for j, chip in enumerate(chips):
            for a in range(self.n):
                copy(a, 1 + j, (*chip, c), me).wait_recv()
                fwd = copy(a, 4 + j, (*chip, c), sibling)
                fwd.start()
                forwards.append(fwd)
        for a in range(self.n):
            copy(a, 0, sibling, me).wait_recv()
            for j, chip in enumerate(chips):
                copy(a, 4 + j, (*chip, 1 - c), me).wait_recv()
        for cp in own + forwards:
            cp.wait_send()
        for loc in mine:
            loc.wait()


def _matmul(a, b, *, tm, tn, tk, out_dtype, name, trans_a=False, trans_b=False, comm=None):
    m, k = (a.shape[1], a.shape[0]) if trans_a else a.shape
    n = b.shape[0] if trans_b else b.shape[1]
    nk = k // tk
    dims = TN if trans_a else (NT if trans_b else NN)
    assert not (trans_a and trans_b)
    nc = comm.n if comm else 0
    grid = (m // tm, n // tn, nk)

    def body(*refs):
        a_ref, b_ref = refs[:2]
        cin = refs[2:2 + nc]
        o_ref = refs[2 + nc]
        cout = refs[3 + nc:3 + 2 * nc]
        scratch = refs[3 + 2 * nc:]
        sems = scratch[len(scratch) - 3:] if comm else None
        i, j, kk = pl.program_id(0), pl.program_id(1), pl.program_id(2)
        if comm:
            @pl.when((i == 0) & (j == 0) & (kk == 0))
            def _():
                comm.start(cin, cout, sems)

        if nk == 1:
            o_ref[...] = _dot(a_ref[...], b_ref[...], dims).astype(out_dtype)
        else:
            acc_ref = scratch[0]

            @pl.when(kk == 0)
            def _():
                acc_ref[...] = jnp.zeros_like(acc_ref)

            acc_ref[...] += _dot(a_ref[...], b_ref[...], dims)

            @pl.when(kk == nk - 1)
            def _():
                o_ref[...] = acc_ref[...].astype(out_dtype)

        if comm:
            @pl.when((i == grid[0] - 1) & (j == grid[1] - 1) & (kk == nk - 1))
            def _():
                comm.wait(cin, cout, sems)

    a_spec = (pl.BlockSpec((tk, tm), lambda i, j, kk: (kk, i)) if trans_a
              else pl.BlockSpec((tm, tk), lambda i, j, kk: (i, kk)))
    b_spec = (pl.BlockSpec((tn, tk), lambda i, j, kk: (j, kk)) if trans_b
              else pl.BlockSpec((tk, tn), lambda i, j, kk: (kk, j)))
    sem = ("arbitrary",) * 3 if comm else ("parallel", "parallel", "arbitrary")
    res = pl.pallas_call(
        body, name=name, grid=grid,
        in_specs=[a_spec, b_spec] + (comm.in_specs if comm else []),
        out_specs=[pl.BlockSpec((tm, tn), lambda i, j, kk: (i, j))] + (comm.out_specs if comm else []),
        out_shape=[jax.ShapeDtypeStruct((m, n), out_dtype)] + (comm.out_shape if comm else []),
        scratch_shapes=([] if nk == 1 else [pltpu.VMEM((tm, tn), F32)]) + (comm.scratch if comm else []),
        compiler_params=_params(sem, side_effects=bool(comm)),
    )(a, b, *(comm.operands if comm else []))
    return res if comm else res[0]


ROW_K = 6
X_K = 8


def _x_specs(nx, k, tile_of=lambda i: i):
    return [pl.BlockSpec((CHUNK, D_MODEL), functools.partial(lambda u, i: (jnp.clip(k * tile_of(i) + u - 1, 0, nx - 1), 0), u))
            for u in range(k)]


def _chunk_of_row(i, k):
    return k * i + lax.broadcasted_iota(jnp.int32, (k * CHUNK, 1), 0) // CHUNK


CONV_COLS = 512
HALO = 8


def _conv_pre(ext, w, b):
    taps = [ext[HALO:, :]] + [pltpu.roll(ext, j, 0)[HALO:, :] for j in range(1, CONV_WIDTH)]
    acc = b + w[3:4, :] * taps[0]
    for j in range(1, CONV_WIDTH):
        acc = acc + w[3 - j:4 - j, :] * taps[j]
    return acc, taps


def _conv_bwd(dxbc, dsilu, proj, conv_w, dproj):
    lp = proj.shape[0]
    t = _tile(lp, (704, 384, 128, 64))
    hb = t // HALO
    nt = lp // t
    c0 = OFF_XBC // CONV_COLS

    def body(dx_ref, dxn_ref, ds_ref, dsn_ref, u_ref, up_ref, w_ref, _, du_ref, dw_ref, db_ref):
        i = pl.program_id(1)
        w = w_ref[...]
        up = up_ref[...] * (i > 0).astype(F32)
        ext = jnp.concatenate([up, u_ref[...]], axis=0)
        taps = [ext[HALO:, :]] + [pltpu.roll(ext, j, 0)[HALO:, :] for j in range(1, CONV_WIDTH)]
        dxn = dxn_ref[...] * (i < nt - 1).astype(F32)
        dpre = jnp.concatenate([dx_ref[...] * ds_ref[...], dxn * dsn_ref[...]], axis=0)
        du = w[3:4, :] * dpre[:t, :]
        for j in range(1, CONV_WIDTH):
            du = du + w[3 - j:4 - j, :] * pltpu.roll(dpre, t + HALO - j, 0)[:t, :]
        du_ref[...] = du.astype(BF16)

        @pl.when(i == 0)
        def _():
            dw_ref[...] = jnp.zeros_like(dw_ref)
            db_ref[...] = jnp.zeros_like(db_ref)

        dp = dpre[:t, :]
        db_ref[...] += jnp.sum(dp, axis=0, keepdims=True)
        for j in range(CONV_WIDTH):
            dw_ref[3 - j:4 - j, :] += jnp.sum(dp * taps[j], axis=0, keepdims=True)

    nxt = lambda i: jnp.minimum((i + 1) * hb, lp // HALO - 1)
    return pl.pallas_call(
        body, name="conv_bwd", grid=(D_CONV // CONV_COLS, nt),
        in_specs=[
            pl.BlockSpec((t, CONV_COLS), lambda j, i: (i, j)),
            pl.BlockSpec((HALO, CONV_COLS), lambda j, i: (nxt(i), j)),
            pl.BlockSpec((t, CONV_COLS), lambda j, i: (i, j)),
            pl.BlockSpec((HALO, CONV_COLS), lambda j, i: (nxt(i), j)),
            pl.BlockSpec((t, CONV_COLS), lambda j, i: (i, c0 + j)),
            pl.BlockSpec((HALO, CONV_COLS), lambda j, i: (jnp.maximum(i * hb - 1, 0), c0 + j)),
            pl.BlockSpec((CONV_WIDTH, CONV_COLS), lambda j, i: (0, j)),
            ANY,
        ],
        out_specs=[
            pl.BlockSpec((t, CONV_COLS), lambda j, i: (i, c0 + j)),
            pl.BlockSpec((CONV_WIDTH, CONV_COLS), lambda j, i: (0, j)),
            pl.BlockSpec((1, CONV_COLS), lambda j, i: (0, j)),
        ],
        out_shape=[
            jax.ShapeDtypeStruct((lp, NP), BF16),
            jax.ShapeDtypeStruct((CONV_WIDTH, D_CONV), F32),
            jax.ShapeDtypeStruct((1, D_CONV), F32),
        ],
        input_output_aliases={7: 0},
        compiler_params=_params(("parallel", "arbitrary")),
    )(dxbc, dxbc, dsilu, dsilu, proj, proj, conv_w, dproj)


def _swap_halves(t):
    w = t.shape[-1]
    lane = lax.broadcasted_iota(jnp.int32, t.shape, 1)
    return jnp.where((lane % HEAD_DIM) < HEAD_DIM // 2, pltpu.roll(t, w - HEAD_DIM // 2, 1),
                     pltpu.roll(t, HEAD_DIM // 2, 1))


def _act_fwd(proj, cos_t, sin_t, expand, dt_bias_pad):
    lp = proj.shape[0]
    t = _tile(lp, (384, 128, 64))

    def body(q_ref, k_ref, v_ref, dt_ref, cos_ref, sin_ref, ex_ref, bias_ref, qo_ref, ko_ref, vo_ref, dto_ref):
        i = pl.program_id(0)
        cos = cos_ref[...]
        sin = sin_ref[...]
        q = q_ref[...]
        qo_ref[...] = (q * jnp.tile(cos, (1, D_ATT // 128)) + _swap_halves(q) * jnp.tile(sin, (1, D_ATT // 128))).astype(BF16)
        k = k_ref[...]
        ko_ref[...] = (k * jnp.tile(cos, (1, D_KV // 128)) + _swap_halves(k) * jnp.tile(sin, (1, D_KV // 128))).astype(BF16)
        vo_ref[...] = v_ref[...].astype(BF16)
        raw = dt_ref[...] + bias_ref[...]
        sp = jnp.maximum(raw, 0.0) + jnp.log1p(jnp.exp(-jnp.abs(raw)))
        row = i * t + lax.broadcasted_iota(jnp.int32, sp.shape, 0)
        dto_ref[...] = _dot(jnp.where(row >= PAD_LEAD, sp, 0.0), ex_ref[...], NN, HIGHEST)

    return pl.pallas_call(
        body, name="act_fwd", grid=(lp // t,),
        in_specs=[
            pl.BlockSpec((t, D_ATT), lambda i: (i, OFF_Q // D_ATT)),
            pl.BlockSpec((t, D_KV), lambda i: (i, OFF_K // D_KV)),
            pl.BlockSpec((t, D_KV), lambda i: (i, OFF_V // D_KV)),
            pl.BlockSpec((t, 128), lambda i: (i, OFF_DT // 128)),
            pl.BlockSpec((t, 128), lambda i: (i, 0)),
            pl.BlockSpec((t, 128), lambda i: (i, 0)),
            pl.BlockSpec((128, D_SSD), lambda i: (0, 0)),
            pl.BlockSpec((1, 128), lambda i: (0, 0)),
        ],
        out_specs=[
            pl.BlockSpec((t, D_ATT), lambda i: (i, 0)),
            pl.BlockSpec((t, D_KV), lambda i: (i, 0)),
            pl.BlockSpec((t, D_KV), lambda i: (i, 0)),
            pl.BlockSpec((t, D_SSD), lambda i: (i, 0)),
        ],
        out_shape=[
            jax.ShapeDtypeStruct((lp, D_ATT), BF16),
            jax.ShapeDtypeStruct((lp, D_KV), BF16),
            jax.ShapeDtypeStruct((lp, D_KV), BF16),
            jax.ShapeDtypeStruct((lp, D_SSD), F32),
        ],
        compiler_params=_params(("parallel",)),
    )(proj, proj, proj, proj, cos_t, sin_t, expand, dt_bias_pad)


def _act_bwd(dqr, dkr, dv, dg, ddt_part, cos_t, sin_t, reduce_t, dproj):
    lp = dqr.shape[0]
    t = _tile(lp, (384, 128, 64))

    def body(dq_ref, dk_ref, dv_ref, dg_ref, ddt_ref, cos_ref, sin_ref, red_ref, _, o_ref, db_ref):
        i = pl.program_id(0)
        cos = cos_ref[...]
        sin = sin_ref[...]
        dq = dq_ref[...]
        dq = dq * jnp.tile(cos, (1, D_ATT // 128)) + _swap_halves(dq * jnp.tile(sin, (1, D_ATT // 128)))
        dk = dk_ref[...]
        dk = dk * jnp.tile(cos, (1, D_KV // 128)) + _swap_halves(dk * jnp.tile(sin, (1, D_KV // 128)))
        ddt = _dot(ddt_ref[...], red_ref[...], NN, HIGHEST)
        o_ref[...] = jnp.concatenate(
            [dq.astype(BF16), dg_ref[...].astype(BF16), dk.astype(BF16), dv_ref[...].astype(BF16), ddt.astype(BF16),
             jnp.zeros((t, NP - OFF_DT - 128), BF16)], axis=1)

        @pl.when(i == 0)
        def _():
            db_ref[...] = jnp.zeros_like(db_ref)

        db_ref[...] += jnp.sum(ddt, axis=0, keepdims=True)

    return pl.pallas_call(
        body, name="act_bwd", grid=(lp // t,),
        in_specs=[
            pl.BlockSpec((t, D_ATT), lambda i: (i, 0)),
            pl.BlockSpec((t, D_KV), lambda i: (i, 0)),
            pl.BlockSpec((t, D_KV), lambda i: (i, 0)),
            pl.BlockSpec((t, D_ATT), lambda i: (i, 0)),
            pl.BlockSpec((t, D_SSD), lambda i: (i, 0)),
            pl.BlockSpec((t, 128), lambda i: (i, 0)),
            pl.BlockSpec((t, 128), lambda i: (i, 0)),
            pl.BlockSpec((D_SSD, 128), lambda i: (0, 0)),
            ANY,
        ],
        out_specs=[pl.BlockSpec((t, TAIL_W), lambda i: (i, OFF_Q // TAIL_W)), pl.BlockSpec((1, 128), lambda i: (0, 0))],
        out_shape=[jax.ShapeDtypeStruct((lp, NP), BF16), jax.ShapeDtypeStruct((1, 128), F32)],
        input_output_aliases={8: 0},
        compiler_params=_params(("arbitrary",)),
    )(dqr, dkr, dv, dg, ddt_part, cos_t, sin_t, reduce_t, dproj)


def _cs_row(cs):
    row = lax.broadcasted_iota(jnp.int32, cs.shape, 0)
    lane = lax.broadcasted_iota(jnp.int32, cs.shape, 1)
    return jnp.sum(jnp.where(row == lane % HEAD_DIM, cs, 0.0), axis=0, keepdims=True)


def _ssd_fwd(proj, conv_w, conv_b, dt_rep, a_rep, dsk_rep, wn, tri):
    lp = proj.shape[0]
    nc = lp // CHUNK
    assert SSD_GPS == N_GROUPS
    gw = SSD_GPS * GROUP_W
    cps = _tile(nc, (SSD_CPS, 1))

    def body(rawa_ref, rawb_ref, cw_ref, cb_ref, dt_ref, z_ref, a_ref, dsk_ref, wn_ref, tri_ref,
             yn_ref, ytot_ref, hprev_ref, xbc_ref, dsilu_ref, h_scr, tail_scr):
        @pl.when(pl.program_id(1) == 0)
        def _():
            h_scr[...] = jnp.zeros_like(h_scr)
            tail_scr[...] = jnp.zeros_like(tail_scr)

        for sub in range(cps):
            rs = slice(sub * CHUNK, (sub + 1) * CHUNK)
            raw = jnp.concatenate([rawa_ref[rs, :], rawb_ref[rs, :]], axis=1)
            pre, _ = _conv_pre(jnp.concatenate([tail_scr[...], raw], axis=0), cw_ref[...], cb_ref[...])
            tail_scr[...] = raw[CHUNK - HALO:, :]
            sg = _sigmoid(pre)
            xbc_ref[rs, :] = pre * sg
            dsilu_ref[rs, :] = sg * (1.0 + pre * (1.0 - sg))

        G = range(SSD_GPS)
        colsl = [slice(gi * GROUP_W, (gi + 1) * GROUP_W) for gi in G]
        rows4 = lax.broadcasted_iota(jnp.int32, (GROUP_W, GROUP_W), 0) // HEAD_DIM
        cols4 = lax.broadcasted_iota(jnp.int32, (GROUP_W, GROUP_W), 1) // HEAD_DIM
        lrow = lax.broadcasted_iota(jnp.int32, (CHUNK, GROUP_W), 0)
        lcol = lax.broadcasted_iota(jnp.int32, (CHUNK, GROUP_W), 1) % HEAD_DIM

        def one_chunk(sub):
            rs = slice(sub * CHUNK, (sub + 1) * CHUNK)
            xbc = [xbc_ref[rs, gi * XBC_BLK:(gi + 1) * XBC_BLK] for gi in G]
            dt = [dt_ref[rs, colsl[gi]] for gi in G]
            xs = [xbc[gi][:, :GROUP_W] for gi in G]
            b = [xbc[gi][:, GROUP_W:GROUP_W + D_STATE].astype(BF16) for gi in G]
            c = [xbc[gi][:, GROUP_W + D_STATE:].astype(BF16) for gi in G]
            hprev = [h_scr[gi] for gi in G]
            cs = [_dot(tri_ref[...], dt[gi] * a_ref[:, colsl[gi]], NN, HIGHEST) for gi in G]
            yoff = [_dot(c[gi], hprev[gi].astype(BF16)) for gi in G]
            cs_t = [_cs_row(cs[gi]) for gi in G]
            xdt = [xs[gi] * dt[gi] for gi in G]
            cs_last = [cs[gi][CHUNK - 1:CHUNK, :] for gi in G]
            st = [_dot(b[gi], (xdt[gi] * jnp.exp(cs_last[gi] - cs[gi])).astype(BF16), TN) for gi in G]
            cb4 = [_dot(c[gi], jnp.concatenate([b[gi]] * HEADS_PER_GROUP, axis=0), NT) for gi in G]
            m = [(cb4[gi] * jnp.exp(jnp.where(lrow >= lcol, cs[gi] - cs_t[gi], -jnp.inf))).astype(BF16) for gi in G]
            xbd = [jnp.where(rows4 == cols4, jnp.concatenate([xdt[gi].astype(BF16)] * HEADS_PER_GROUP, axis=0), 0.0)
                   for gi in G]
            ydiag = [_dot(m[gi], xbd[gi]) for gi in G]
            for gi in G:
                cols = colsl[gi]
                ytot = ydiag[gi] + yoff[gi] * jnp.exp(cs[gi]) + dsk_ref[:, cols] * xs[gi]
                z = z_ref[rs, cols]
                gz = ytot * (z * _sigmoid(z))
                rstd = lax.rsqrt(jnp.mean(gz * gz, axis=-1, keepdims=True) + EPS)
                hprev_ref[sub, gi] = hprev[gi]
                h_scr[gi] = hprev[gi] * jnp.exp(cs_last[gi]) + st[gi]
                ytot_ref[rs, cols] = ytot
                yn_ref[rs, cols] = (gz * rstd * wn_ref[:, cols]).astype(BF16)

        for sub in range(cps):
            one_chunk(sub)

    vec = pl.BlockSpec((1, gw), lambda g, c: (0, g))
    blk = pl.BlockSpec((cps * CHUNK, gw), lambda g, c: (c, g))
    xblk = pl.BlockSpec((cps * CHUNK, D_CONV), lambda g, c: (c, 0))
    half = D_CONV // 2
    return pl.pallas_call(
        body, name="ssd_fwd", grid=(N_GROUPS // SSD_GPS, nc // cps),
        in_specs=[
            pl.BlockSpec((cps * CHUNK, half), lambda g, c: (c, OFF_XBC // half)),
            pl.BlockSpec((cps * CHUNK, half), lambda g, c: (c, OFF_XBC // half + 1)),
            pl.BlockSpec((CONV_WIDTH, D_CONV), lambda g, c: (0, 0)),
            pl.BlockSpec((1, D_CONV), lambda g, c: (0, 0)),
            blk, blk, vec, vec, vec,
            pl.BlockSpec((CHUNK, CHUNK), lambda g, c: (0, 0)),
        ],
        out_specs=[blk, blk, pl.BlockSpec((cps, SSD_GPS, D_STATE, GROUP_W), lambda g, c: (c, g, 0, 0)), xblk, xblk],
        out_shape=[
            jax.ShapeDtypeStruct((lp, D_MIX), BF16),
            jax.ShapeDtypeStruct((lp, D_SSD), F32),
            jax.ShapeDtypeStruct((nc, N_GROUPS, D_STATE, GROUP_W), F32),
            jax.ShapeDtypeStruct((lp, D_CONV), F32),
            jax.ShapeDtypeStruct((lp, D_CONV), F32),
        ],
        scratch_shapes=[pltpu.VMEM((SSD_GPS, D_STATE, GROUP_W), F32), pltpu.VMEM((HALO, D_CONV), F32)],
        compiler_params=_params(("arbitrary", "arbitrary")),
    )(proj, proj, conv_w, conv_b, dt_rep, proj, a_rep, dsk_rep, wn, tri)


def _ssd_bwd(dmix, ytot, proj, xbc, dt_rep, hprev, a_rep, dsk_rep, wn, tri, comm):
    lp = xbc.shape[0]
    nc = lp // CHUNK
    gps = SSD_GPS
    gw = gps * GROUP_W
    cps = _tile(nc, (SSD_CPS, 1))
    ncm = comm.n
    n_in, n_out = 10, 6
    grid = (N_GROUPS // gps, nc // cps)

    def all_groups(refs):
        (dyn_ref, ytot_ref, z_ref, xbc_ref, dt_ref, hprev_ref, a_ref, dsk_ref, wn_ref, tri_ref,
         dz_ref, dxbc_ref, ddt_ref, dd_ref, da_ref, dwn_ref, dh_scr) = refs
        G = range(gps)
        H = range(HEADS_PER_GROUP)
        cl = [slice(gi * GROUP_W, (gi + 1) * GROUP_W) for gi in G]
        hl = [slice(r * HEAD_DIM, (r + 1) * HEAD_DIM) for r in H]
        tri = tri_ref[...]
        xbc = [xbc_ref[:, gi * XBC_BLK:(gi + 1) * XBC_BLK] for gi in G]
        dt = [dt_ref[:, cl[gi]] for gi in G]
        a = [a_ref[:, cl[gi]] for gi in G]
        xs = [xbc[gi][:, :GROUP_W] for gi in G]
        bbf = [xbc[gi][:, GROUP_W:GROUP_W + D_STATE].astype(BF16) for gi in G]
        cbf = [xbc[gi][:, GROUP_W + D_STATE:].astype(BF16) for gi in G]
        hprev = [hprev_ref[gi] for gi in G]
        hbf = [hprev[gi].astype(BF16) for gi in G]
        dhn = [dh_scr[gi] for gi in G]
        dhnb = [dhn[gi].astype(BF16) for gi in G]
        cs = [_dot(tri, dt[gi] * a[gi], NN, HIGHEST) for gi in G]
        g = [_dot(cbf[gi], hbf[gi]) for gi in G]
        dxw = [_dot(bbf[gi], dhnb[gi]) for gi in G]
        cs_t = [_cs_row(cs[gi]) for gi in G]
        dy = []
        for gi in G:
            ytot = ytot_ref[:, cl[gi]]
            z = z_ref[:, cl[gi]]
            dyn = dyn_ref[:, cl[gi]]
            sz = _sigmoid(z)
            silu_z = z * sz
            gz = ytot * silu_z
            rstd = lax.rsqrt(jnp.mean(gz * gz, axis=-1, keepdims=True) + EPS)
            xhat = gz * rstd
            dwn_ref[:, cl[gi]] += jnp.sum(dyn * xhat, axis=0, keepdims=True)
            dxhat = dyn * wn_ref[:, cl[gi]]
            dgz = rstd * (dxhat - xhat * jnp.mean(dxhat * xhat, axis=-1, keepdims=True))
            dy.append(dgz * silu_z)
            dz_ref[:, cl[gi]] = (dgz * ytot * (sz * (1.0 + z * (1.0 - sz)))).astype(BF16)
            dd_ref[:, cl[gi]] += jnp.sum(dy[gi] * xs[gi], axis=0, keepdims=True)
        xdt = [xs[gi] * dt[gi] for gi in G]
        e = [jnp.exp(cs[gi]) for gi in G]
        cs_last = [cs[gi][CHUNK - 1:CHUNK, :] for gi in G]
        dte = [jnp.exp(cs_last[gi] - cs[gi]) for gi in G]
        cd = [jnp.exp(cs_last[gi]) for gi in G]
        dgb = [(dy[gi] * e[gi]).astype(BF16) for gi in G]
        dyb = [dy[gi].astype(BF16) for gi in G]
        xdtb = [xdt[gi].astype(BF16) for gi in G]
        dc = [_dot(dgb[gi], hbf[gi], NT) for gi in G]
        dhprev = [_dot(cbf[gi], dgb[gi], TN) for gi in G]
        db = [_dot((xdt[gi] * dte[gi]).astype(BF16), dhnb[gi], NT) for gi in G]
        row = lax.broadcasted_iota(jnp.int32, (CHUNK, CHUNK), 0)
        causal = row >= lax.broadcasted_iota(jnp.int32, (CHUNK, CHUNK), 1)
        cb = [_dot(cbf[gi], bbf[gi], NT) for gi in G]
        dm = [[_dot(dyb[gi][:, hl[r]], xdtb[gi][:, hl[r]], NT) for r in H] for gi in G]
        mb, dseg, dcbb = [], [], []
        for gi in G:
            mb.append([])
            dseg.append([])
            dcb = None
            for r in H:
                seg = cs[gi][:, r * HEAD_DIM:r * HEAD_DIM + 1] - cs_t[gi][:, hl[r]]
                lm = jnp.exp(jnp.where(causal, seg, -jnp.inf))
                m = cb[gi] * lm
                mb[gi].append(m.astype(BF16))
                dseg[gi].append(dm[gi][r] * m)
                dcb = dm[gi][r] * lm if r == 0 else dcb + dm[gi][r] * lm
            dcbb.append(dcb.astype(BF16))
        dxdt_diag = [[_dot(mb[gi][r], dyb[gi][:, hl[r]], TN) for r in H] for gi in G]
        ones = jnp.ones((CHUNK, HEAD_DIM), F32)
        colsum = [[_dot(dseg[gi][r], ones, TN, HIGHEST) for r in H] for gi in G]
        dc2 = [_dot(dcbb[gi], bbf[gi]) for gi in G]
        db2 = [_dot(dcbb[gi], cbf[gi], TN) for gi in G]
        dcs = []
        for gi in G:
            t_dte = dxw[gi] * xdt[gi] * dte[gi]
            dcs_last = (jnp.sum(dhn[gi] * hprev[gi], axis=0, keepdims=True) * cd[gi]
                        + jnp.sum(t_dte, axis=0, keepdims=True))
            diag = jnp.concatenate(
                [(jnp.sum(dseg[gi][r], axis=1, keepdims=True) - colsum[gi][r]) * (1.0 / HEAD_DIM) for r in H], axis=1)
            d = dy[gi] * g[gi] * e[gi] - t_dte + diag
            row = lax.broadcasted_iota(jnp.int32, d.shape, 0)
            dcs.append(d + jnp.where(row == CHUNK - 1, dcs_last, 0.0))
        dda = [_dot(tri, dcs[gi], TN, HIGHEST) for gi in G]
        for gi in G:
            dxdt = dxw[gi] * dte[gi] + jnp.concatenate(dxdt_diag[gi], axis=1)
            da_ref[:, cl[gi]] += jnp.sum(dda[gi] * dt[gi], axis=0, keepdims=True)
            ddt = dda[gi] * a[gi] + dxdt * xs[gi]
            dxs = dsk_ref[:, cl[gi]] * dy[gi] + dxdt * dt[gi]
            ddt_ref[:, cl[gi]] = ddt * (1.0 - jnp.exp(-dt[gi]))
            dxbc_ref[:, gi * XBC_BLK:(gi + 1) * XBC_BLK] = jnp.concatenate(
                [dxs, db[gi] + db2[gi], dc[gi] + dc2[gi]], axis=1)
            dh_scr[gi] = dhprev[gi] + dhn[gi] * cd[gi]

    def body(*refs):
        ins = refs[:n_in]
        cin = refs[n_in:n_in + ncm]
        outs = refs[n_in + ncm:n_in + ncm + n_out]
        cout = refs[n_in + ncm + n_out:n_in + 2 * ncm + n_out]
        dh_scr = refs[n_in + 2 * ncm + n_out]
        sems = refs[n_in + 2 * ncm + n_out + 1:]
        g, c = pl.program_id(0), pl.program_id(1)

        @pl.when((g == 0) & (c == 0))
        def _():
            comm.start(cin, cout, sems)

        @pl.when(c == 0)
        def _():
            dh_scr[...] = jnp.zeros_like(dh_scr)
            for ref in outs[3:]:
                ref[...] = jnp.zeros_like(ref)

        for sub in reversed(range(cps)):
            rs = pl.ds(sub * CHUNK, CHUNK)
            chunk_ins = tuple(r.at[rs] for r in ins[:5]) + (ins[5].at[sub],) + ins[6:]
            chunk_outs = tuple(r.at[rs] for r in outs[:3]) + outs[3:]
            all_groups(chunk_ins + chunk_outs + (dh_scr,))

        @pl.when((g == grid[0] - 1) & (c == grid[1] - 1))
        def _():
            comm.wait(cin, cout, sems)

    rev = lambda c: grid[1] - 1 - c
    vec = pl.BlockSpec((1, gw), lambda g, c: (0, g))
    blk = pl.BlockSpec((cps * CHUNK, gw), lambda g, c: (rev(c), g))
    xblk = pl.BlockSpec((cps * CHUNK, gps * XBC_BLK), lambda g, c: (rev(c), g))
    res = pl.pallas_call(
        body, name="ssd_bwd", grid=grid,
        in_specs=[blk, blk, blk, xblk, blk,
                  pl.BlockSpec((cps, gps, D_STATE, GROUP_W), lambda g, c: (rev(c), g, 0, 0)),
                  vec, vec, vec,
                  pl.BlockSpec((CHUNK, CHUNK), lambda g, c: (0, 0))] + comm.in_specs,
        out_specs=[blk, xblk, blk, vec, vec, vec] + comm.out_specs,
        out_shape=[
            jax.ShapeDtypeStruct((lp, NP), BF16),
            jax.ShapeDtypeStruct((lp, D_CONV), F32),
            jax.ShapeDtypeStruct((lp, D_SSD), F32),
            jax.ShapeDtypeStruct((1, D_SSD), F32),
            jax.ShapeDtypeStruct((1, D_SSD), F32),
            jax.ShapeDtypeStruct((1, D_SSD), F32),
        ] + comm.out_shape,
        scratch_shapes=[pltpu.VMEM((gps, D_STATE, GROUP_W), F32)] + comm.scratch,
        compiler_params=_params(("arbitrary", "arbitrary"), side_effects=True),
    )(dmix, ytot, proj, xbc, dt_rep, hprev, a_rep, dsk_rep, wn, tri, *comm.operands)
    return res


def _stack_heads(t, h):
    return jnp.concatenate([t[:, (REP * h + r) * HEAD_DIM:(REP * h + r + 1) * HEAD_DIM] for r in range(REP)], axis=0)


def _band(t2, t1, t0, h):
    sl = slice(h * HEAD_DIM, (h + 1) * HEAD_DIM)
    return jnp.concatenate([t2[:, sl], t1[:, sl], t0[:, sl]], axis=0)


def _attn_probs(s, sink, qc):
    s = s * (HEAD_DIM ** -0.5)
    key_abs = (qc - WINDOW_CHUNKS) * CHUNK + lax.broadcasted_iota(jnp.int32, s.shape, 1)
    s = jnp.where(key_abs >= PAD_LEAD, s, -jnp.inf)
    m = jnp.maximum(jnp.max(s, axis=-1, keepdims=True), sink)
    p = jnp.exp(s - m)
    ps = jnp.exp(sink - m)
    denom = jnp.sum(p, axis=-1, keepdims=True) + ps
    return p / denom, ps / denom


ATT_QC_FWD = 6
ATT_QC_BWD = 2


def _kv_specs(width, newest_chunk_of, kc):
    return [pl.BlockSpec((CHUNK, width), functools.partial(lambda j, p: (jnp.maximum(newest_chunk_of(p) - j, 0), 0), j))
            for j in range(kc - 1, -1, -1)]


def _attn_fwd(qr, kr, vb, proj, sink_stack, mix):
    lp = qr.shape[0]
    nc = lp // CHUNK
    qn = _tile(nc, (ATT_QC_FWD, 2, 1))
    kn = WINDOW_CHUNKS + qn
    qrows = qn * CHUNK

    def body(q_ref, *rest):
        k_refs, v_refs = rest[:kn], rest[kn:2 * kn]
        g_ref, sink_ref, _, att_ref, mix_ref = rest[2 * kn:]
        p = pl.program_id(0)
        q = q_ref[...]
        ks = [r[...] for r in k_refs]
        vs = [r[...] for r in v_refs]
        units = [(u, h) for u in range(qn) for h in range(KV_HEADS)]
        s = [_dot(_stack_heads(q[u * CHUNK:(u + 1) * CHUNK, :], h), _band(*ks[u:u + 3], h), NT) for u, h in units]
        vbh = [_band(*vs[u:u + 3], h) for u, h in units]
        pn = [_attn_probs(s[i], sink_ref[h], qn * p + u)[0].astype(BF16) for i, (u, h) in enumerate(units)]
        o = [_dot(pn[i], vbh[i]) for i in range(len(units))]
        att = jnp.concatenate(
            [jnp.concatenate([o[u * KV_HEADS + h][r * CHUNK:(r + 1) * CHUNK, :] for h in range(KV_HEADS) for r in range(REP)],
                             axis=1) for u in range(qn)], axis=0)
        att_ref[...] = att
        g = g_ref[...]
        mix_ref[...] = (att * (g * _sigmoid(g))).astype(BF16)

    newest = lambda p: qn * p + qn - 1
    return pl.pallas_call(
        body, name="attn_fwd", grid=(nc // qn,),
        in_specs=[pl.BlockSpec((qrows, D_ATT), lambda p: (p, 0))] + _kv_specs(D_KV, newest, kn) + _kv_specs(D_KV, newest, kn) + [
            pl.BlockSpec((qrows, D_ATT), lambda p: (p, OFF_G // D_ATT)),
            pl.BlockSpec((KV_HEADS, REP * CHUNK, 1), lambda p: (0, 0, 0)),
            ANY,
        ],
        out_specs=[pl.BlockSpec((qrows, D_ATT), lambda p: (p, 0)),
                   pl.BlockSpec((qrows, D_ATT), lambda p: (p, D_SSD // D_ATT))],
        out_shape=[jax.ShapeDtypeStruct((lp, D_ATT), F32), jax.ShapeDtypeStruct((lp, D_MIX), BF16)],
        input_output_aliases={2 * kn + 3: 1},
        compiler_params=_params(("parallel",)),
    )(qr, *([kr] * kn), *([vb] * kn), proj, sink_stack, mix)


def _attn_bwd(qr, kr, vb, att, proj, dmix, sink_stack):
    lp = qr.shape[0]
    nc = lp // CHUNK
    qn = ATT_QC_BWD
    kn = WINDOW_CHUNKS + qn
    assert nc % qn == 0 and WINDOW_CHUNKS % qn == 0
    steps = nc // qn
    qrows = qn * CHUNK
    wrows = kn * CHUNK

    def body(q_ref, *rest):
        k_refs, v_refs = rest[:kn], rest[kn:2 * kn]
        (att_ref, g_ref, do_ref, sink_ref, dq_ref, dk_ref, dv_ref, dg_ref, dsink_ref, dk_acc, dv_acc) = rest[2 * kn:]
        step = pl.program_id(0)

        @pl.when(step == 0)
        def _():
            dk_acc[...] = jnp.zeros_like(dk_acc)
            dv_acc[...] = jnp.zeros_like(dv_acc)
            dsink_ref[...] = jnp.zeros_like(dsink_ref)

        @pl.when(step < steps)
        def _():
            q = q_ref[...]
            ks = [r[...] for r in k_refs]
            vs = [r[...] for r in v_refs]
            att = att_ref[...]
            g = g_ref[...]
            dog = do_ref[...]
            sg = _sigmoid(g)
            dg_ref[...] = dog * att * (sg * (1.0 + g * (1.0 - sg)))
            do = dog * (g * sg)
            units = [(u, h) for u in range(qn) for h in range(KV_HEADS)]
            n = range(len(units))
            rows = [slice(u * CHUNK, (u + 1) * CHUNK) for u in range(qn)]
            qs = [_stack_heads(q[rows[u], :], h) for u, h in units]
            kb = [_band(*ks[u:u + 3], h) for u, h in units]
            vbh = [_band(*vs[u:u + 3], h) for u, h in units]
            dos = [_stack_heads(do[rows[u], :], h) for u, h in units]
            dosb = [dos[i].astype(BF16) for i in n]
            s = [_dot(qs[i], kb[i], NT) for i in n]
            dp = [_dot(dosb[i], vbh[i], NT) for i in n]
            ds, pnb = [], []
            for i, (u, h) in enumerate(units):
                pn, psink = _attn_probs(s[i], sink_ref[h], qn * step + u)
                delta = jnp.sum(dos[i] * _stack_heads(att[rows[u], :], h), axis=-1, keepdims=True)
                ds.append((pn * (dp[i] - delta)).astype(BF16))
                pnb.append(pn.astype(BF16))
                dsink_ref[h] += -(psink * delta)
            dqs = [_dot(ds[i], kb[i]) for i in n]
            dks = [_dot(ds[i], qs[i], TN) for i in n]
            dvs = [_dot(pnb[i], dosb[i], TN) for i in n]
            dq_ref[...] = jnp.concatenate(
                [jnp.concatenate([dqs[u * KV_HEADS + h][r * CHUNK:(r + 1) * CHUNK, :]
                                  for h in range(KV_HEADS) for r in range(REP)], axis=1) for u in range(qn)],
                axis=0) * (HEAD_DIM ** -0.5)
            for u in range(qn):
                band = slice(u * CHUNK, u * CHUNK + BAND)
                dk_acc[band, :] += jnp.concatenate(dks[u * KV_HEADS:(u + 1) * KV_HEADS], axis=1) * (HEAD_DIM ** -0.5)
                dv_acc[band, :] += jnp.concatenate(dvs[u * KV_HEADS:(u + 1) * KV_HEADS], axis=1)

        dk_ref[...] = dk_acc[0:qrows, :]
        dv_ref[...] = dv_acc[0:qrows, :]
        for acc in (dk_acc, dv_acc):
            rest_rows = acc[qrows:wrows, :]
            acc[0:wrows - qrows, :] = rest_rows
            acc[wrows - qrows:wrows, :] = jnp.zeros((qrows, D_KV), F32)

    qp = lambda p: jnp.minimum(p, steps - 1)
    newest = lambda p: qn * qp(p) + qn - 1
    qblk = pl.BlockSpec((qrows, D_ATT), lambda p: (qp(p), 0))
    oldest = pl.BlockSpec((qrows, D_KV), lambda p: (jnp.maximum(p - 1, 0), 0))
    return pl.pallas_call(
        body, name="attn_bwd", grid=(steps + 1,),
        in_specs=[qblk] + _kv_specs(D_KV, newest, kn) + _kv_specs(D_KV, newest, kn) + [
            qblk,
            pl.BlockSpec((qrows, D_ATT), lambda p: (qp(p), OFF_G // D_ATT)),
            pl.BlockSpec((qrows, D_ATT), lambda p: (qp(p), D_SSD // D_ATT)),
            pl.BlockSpec((KV_HEADS, REP * CHUNK, 1), lambda p: (0, 0, 0)),
        ],
        out_specs=[qblk, oldest, oldest, qblk, pl.BlockSpec((KV_HEADS, REP * CHUNK, 1), lambda p: (0, 0, 0))],
        out_shape=[
            jax.ShapeDtypeStruct((lp, D_ATT), F32),
            jax.ShapeDtypeStruct((lp, D_KV), F32),
            jax.ShapeDtypeStruct((lp, D_KV), F32),
            jax.ShapeDtypeStruct((lp, D_ATT), F32),
            jax.ShapeDtypeStruct((KV_HEADS, REP * CHUNK, 1), F32),
        ],
        scratch_shapes=[pltpu.VMEM((wrows, D_KV), F32), pltpu.VMEM((wrows, D_KV), F32)],
        compiler_params=_params(("arbitrary",)),
    )(qr, *([kr] * kn), *([vb] * kn), att, proj, dmix, sink_stack)


def _post_loss(out, x2d, target, w):
    lp = out.shape[0]
    nc = lp // CHUNK
    nx = x2d.shape[0] // CHUNK

    k = _tile(nc, (ROW_K, 3, 2, 1))

    def body(o_ref, *rest):
        x_refs, t_refs = rest[:k], rest[k:2 * k]
        w_ref, do_ref, dy_ref, gw_ref, loss_ref = rest[2 * k:]
        i = pl.program_id(0)

        @pl.when(i == 0)
        def _():
            gw_ref[...] = jnp.zeros_like(gw_ref)
            loss_ref[...] = jnp.zeros_like(loss_ref)

        o = o_ref[...]
        w = w_ref[...]
        rstd = lax.rsqrt(jnp.mean(o * o, axis=-1, keepdims=True) + EPS)
        xhat = o * rstd
        x = jnp.concatenate([r[...] for r in x_refs], axis=0)
        t = jnp.concatenate([r[...] for r in t_refs], axis=0)
        chunk = _chunk_of_row(i, k)
        err = (x + xhat * w - t) * ((chunk > 0) & (chunk <= nx)).astype(F32)
        loss_ref[...] += 0.5 * jnp.sum(jnp.mean(err * err, axis=-1, keepdims=True), axis=0, keepdims=True)
        dy = err * (1.0 / D_MODEL)
        dy_ref[...] = dy
        gw_ref[...] += jnp.sum(dy * xhat, axis=0, keepdims=True)
        dxhat = dy * w
        do_ref[...] = (rstd * (dxhat - xhat * jnp.mean(dxhat * xhat, axis=-1, keepdims=True))).astype(BF16)

    row = pl.BlockSpec((k * CHUNK, D_MODEL), lambda i: (i, 0))
    return pl.pallas_call(
        body, name="post_loss", grid=(nc // k,),
        in_specs=[row] + _x_specs(nx, k) + _x_specs(nx, k) + [pl.BlockSpec((1, D_MODEL), lambda i: (0, 0))],
        out_specs=[row, row, pl.BlockSpec((1, D_MODEL), lambda i: (0, 0)), pl.BlockSpec((1, 128), lambda i: (0, 0))],
        out_shape=[
            jax.ShapeDtypeStruct((lp, D_MODEL), BF16),
            jax.ShapeDtypeStruct((lp, D_MODEL), F32),
            jax.ShapeDtypeStruct((1, D_MODEL), F32),
            jax.ShapeDtypeStruct((1, 128), F32),
        ],
        compiler_params=_params(("arbitrary",)),
    )(out, *([x2d] * k), *([target] * k), w)


def _prenorm_bwd(dhn, x2d, h0, dy, w):
    nx = x2d.shape[0] // CHUNK
    k = _tile(nx, (X_K, 4, 2, 1))
    rows = k * CHUNK

    def backward(h, dhn, w):
        rstd = lax.rsqrt(jnp.mean(h * h, axis=-1, keepdims=True) + EPS)
        xhat = h * rstd
        dxhat = dhn * w
        dh = rstd * (dxhat - xhat * jnp.mean(dxhat * xhat, axis=-1, keepdims=True))
        return dh, jnp.sum(dhn * xhat, axis=0, keepdims=True)

    def body(*refs):
        dhn_refs, dy_refs = refs[:k], refs[k:2 * k]
        x_ref, w_ref, gx_ref, gw_ref = refs[2 * k:]

        @pl.when(pl.program_id(0) == 0)
        def _():
            gw_ref[...] = jnp.zeros_like(gw_ref)

        dh, gw = backward(x_ref[...], jnp.concatenate([r[...] for r in dhn_refs], axis=0), w_ref[...])
        gx_ref[...] = dh + jnp.concatenate([r[...] for r in dy_refs], axis=0)
        gw_ref[...] += gw

    def body_meta(dhn_ref, h0_ref, w_ref, d0_ref, gw_ref):
        d0_ref[...], gw_ref[...] = backward(h0_ref[...], dhn_ref[...], w_ref[...])

    chunk_specs = [pl.BlockSpec((CHUNK, D_MODEL), functools.partial(lambda u, i: (k * i + 1 + u, 0), u)) for u in range(k)]
    first = pl.BlockSpec((CHUNK, D_MODEL), lambda i: (0, 0))
    vec = pl.BlockSpec((1, D_MODEL), lambda i: (0, 0))
    wide = pl.BlockSpec((rows, D_MODEL), lambda i: (i, 0))
    gx, gw_x = pl.pallas_call(
        body, name="prenorm_bwd", grid=(nx // k,),
        in_specs=chunk_specs + chunk_specs + [wide, vec],
        out_specs=[wide, vec],
        out_shape=[jax.ShapeDtypeStruct((nx * CHUNK, D_MODEL), F32), jax.ShapeDtypeStruct((1, D_MODEL), F32)],
        compiler_params=_params(("arbitrary",)),
    )(*([dhn] * k), *([dy] * k), x2d, w)
    d0, gw_0 = pl.pallas_call(
        body_meta, name="prenorm_bwd_meta", grid=(1,),
        in_specs=[first, first, vec], out_specs=[first, vec],
        out_shape=[jax.ShapeDtypeStruct((CHUNK, D_MODEL), F32), jax.ShapeDtypeStruct((1, D_MODEL), F32)],
        compiler_params=_params(("arbitrary",)),
    )(dhn, h0, w)
    return gx, d0, gw_x + gw_0


def _adamw(slabs, w, m, v, name):
    rows, cols = w.shape
    tr = _tile(rows, (256, 128, 64, 16, 8))
    c1 = 1.0 - ADAM_B1 ** ADAM_STEP
    c2 = 1.0 - ADAM_B2 ** ADAM_STEP

    def body(s_ref, w_ref, m_ref, v_ref, g_ref, d_ref, mo_ref, vo_ref):
        g = s_ref[0].astype(F32)
        for k in range(1, slabs.shape[0]):
            g = g + s_ref[k].astype(F32)
        w = w_ref[...]
        m = ADAM_B1 * m_ref[...] + (1.0 - ADAM_B1) * g
        v = ADAM_B2 * v_ref[...] + (1.0 - ADAM_B2) * (g * g)
        g_ref[...] = g
        mo_ref[...] = m
        vo_ref[...] = v
        d_ref[...] = -ADAM_LR * ((m / c1) / (jnp.sqrt(v / c2) + ADAM_EPS) + ADAM_WD * w)

    blk = pl.BlockSpec((tr, cols), lambda i: (i, 0))
    shape = jax.ShapeDtypeStruct((rows, cols), F32)
    return pl.pallas_call(
        body, name=name, grid=(rows // tr,),
        in_specs=[pl.BlockSpec((slabs.shape[0], tr, cols), lambda i: (0, i, 0)), blk, blk, blk],
        out_specs=[blk, blk, blk, blk],
        out_shape=[shape, shape, shape, shape],
        compiler_params=_params(("parallel",)),
    )(slabs, w, m, v)


def _perm_xbc(a):
    lead = a.shape[:-1]
    xs = a[..., :D_SSD].reshape(lead + (N_GROUPS, GROUP_W))
    b = a[..., D_SSD:D_SSD + N_GROUPS * D_STATE].reshape(lead + (N_GROUPS, D_STATE))
    c = a[..., D_SSD + N_GROUPS * D_STATE:].reshape(lead + (N_GROUPS, D_STATE))
    return jnp.concatenate([xs, b, c], axis=-1).reshape(lead + (D_CONV,))


def _unperm_xbc(a):
    lead = a.shape[:-1]
    t = a.reshape(lead + (N_GROUPS, XBC_BLK))
    xs = t[..., :GROUP_W].reshape(lead + (D_SSD,))
    b = t[..., GROUP_W:GROUP_W + D_STATE].reshape(lead + (N_GROUPS * D_STATE,))
    c = t[..., GROUP_W + D_STATE:].reshape(lead + (N_GROUPS * D_STATE,))
    return jnp.concatenate([xs, b, c], axis=-1)


R_Z, R_XBC, R_DT, R_Q, R_K, R_V, R_G = 0, 2048, 6144, 6176, 7200, 7456, 7712


def _internal_of_reference():
    ref = np.arange(D_IN_PROJ)
    out = np.empty(D_IN_PROJ, np.int64)
    out[R_Z:R_XBC] = OFF_Z + ref[:D_SSD]
    xs = np.arange(D_SSD)
    out[R_XBC:R_XBC + D_SSD] = OFF_XBC + (xs // GROUP_W) * XBC_BLK + xs % GROUP_W
    bc = np.arange(N_GROUPS * D_STATE)
    out[R_XBC + D_SSD:R_XBC + D_SSD + N_GROUPS * D_STATE] = OFF_XBC + (bc // D_STATE) * XBC_BLK + GROUP_W + bc % D_STATE
    out[R_XBC + D_SSD + N_GROUPS * D_STATE:R_DT] = OFF_XBC + (bc // D_STATE) * XBC_BLK + GROUP_W + D_STATE + bc % D_STATE
    out[R_DT:R_Q] = OFF_DT + np.arange(SSD_HEADS)
    out[R_Q:R_K] = OFF_Q + np.arange(D_ATT)
    out[R_K:R_V] = OFF_K + np.arange(D_KV)
    out[R_V:R_G] = OFF_V + np.arange(D_KV)
    out[R_G:] = OFF_G + np.arange(D_ATT)
    return out


def _runs(src, dst_break):
    runs, lo = [], 0
    for i in range(1, len(src) + 1):
        if i == len(src) or src[i] != src[i - 1] + 1 or dst_break[i] != dst_break[i - 1]:
            runs.append((lo, i))
            lo = i
    return runs


RELAYOUT_ROWS = 256


def _lane_window(ref, lead, c0, n):
    a0 = c0 // 128 * 128
    a1 = min(-(-(c0 + n) // 128) * 128, ref.shape[-1])
    return ref[lead + (slice(None), slice(a0, a1))][:, c0 - a0:c0 - a0 + n]


def _gather_w_in(w_shard, small, x2d, norm_w):
    int_of_ref = _internal_of_reference()
    ref_of_int = np.full(NP, -1, np.int64)
    ref_of_int[int_of_ref] = np.arange(D_IN_PROJ)
    shard = np.where(ref_of_int >= 0, ref_of_int // SHARD_IN, -1)
    src = np.where(ref_of_int >= 0, ref_of_int, -10 - 2 * np.arange(NP))
    plan, zeros = [], 0
    for lo, hi in _runs(src, shard):
        if ref_of_int[lo] < 0:
            zeros += hi - lo
            continue
        if zeros:
            plan.append((None, 0, zeros))
            zeros = 0
        plan.append((int(ref_of_int[lo] // SHARD_IN), int(ref_of_int[lo] % SHARD_IN), hi - lo))
    if zeros:
        plan.append((None, 0, zeros))
    tr = RELAYOUT_ROWS
    steps = D_MODEL // tr
    side = _GatherTwoLevel(small)
    ns = side.n
    nx = x2d.shape[0] // CHUNK
    nc = nx + 2
    k = _tile(nc, (11, ROW_K, 3, 2, 1))
    pn = nc // k
    n = max(pn, steps)
    tile_of = lambda i: jnp.where(i < pn - 1, i + 1, 0)

    def body(w_ref, *rest):
        s_in, x_refs, nw_ref = rest[:ns], rest[ns:ns + k], rest[ns + k]
        o_ref, land_ref, s_out, hn_ref = rest[ns + k + 1], rest[ns + k + 2], rest[ns + k + 3:2 * ns + k + 3], rest[2 * ns + k + 3]
        tile_buf, send_sems, recv_sems, load_sems, local_sem, meta_buf, meta_sem = rest[2 * ns + k + 4:2 * ns + k + 11]
        side_sems = rest[2 * ns + k + 11:]
        i = pl.program_id(0)
        x, y, c = lax.axis_index("x"), lax.axis_index("y"), lax.axis_index("c")
        me, sibling = (x, y, c), (x, y, 1 - c)
        chips = [(1 - x, y), (x, 1 - y), (1 - x, 1 - y)]

        def prenorm_tile(t, head=None):
            h = jnp.concatenate([r[...] for r in x_refs], axis=0)
            if head is not None:
                h = jnp.concatenate([head, h[CHUNK:, :]], axis=0)
            h = h * (_chunk_of_row(t, k) <= nx).astype(F32)
            rstd = lax.rsqrt(jnp.mean(h * h, axis=-1, keepdims=True) + EPS)
            hn_ref[...] = (h * rstd * nw_ref[...]).astype(BF16)

        if pn > 1:
            @pl.when(i < pn - 1)
            def _():
                prenorm_tile(i + 1)

        def rows(t):
            return pl.ds(t * tr, tr)

        def tile(place, t):
            return land_ref.at[4 * place[0] + 2 * place[1] + place[2], rows(t), :]

        def copy(t, k, block, to, src=None):
            return pltpu.make_async_remote_copy(
                src_ref=tile(block, t) if src is None else src, dst_ref=tile(block, t),
                send_sem=send_sems.at[t, k], recv_sem=recv_sems.at[t, k], device_id=to, device_id_type=pl.DeviceIdType.MESH)

        def own_sends(t):
            mine = w_ref.at[rows(t), :]
            return [copy(t, 0, me, sibling, src=mine)] + [copy(t, 1 + j, me, (*chip, c), src=mine)
                                                          for j, chip in enumerate(chips)]

        own = pltpu.make_async_copy(w_ref, land_ref.at[4 * x + 2 * y + c], local_sem)

        @pl.when(i < steps)
        def _():
            @pl.when(i == 0)
            def _():
                side.start(s_in, s_out, side_sems)
                own.start()
                for t in range(steps):
                    for cp in own_sends(t):
                        cp.start()
                own.wait()

            forwards = []
            for j, chip in enumerate(chips):
                copy(i, 1 + j, (*chip, c), me).wait_recv()
                forwards.append(copy(i, 4 + j, (*chip, c), sibling))
                forwards[j].start()
            copy(i, 0, sibling, me).wait_recv()
            for j, chip in enumerate(chips):
                copy(i, 4 + j, (*chip, 1 - c), me).wait_recv()
            loads = [pltpu.make_async_copy(land_ref.at[s, rows(i), :], tile_buf.at[s], load_sems.at[s]) for s in range(N_DEV)]
            for cp in loads:
                cp.start()
            for cp in loads:
                cp.wait()
            o_ref[...] = jnp.concatenate(
                [jnp.zeros((tr, w), o_ref.dtype) if s is None else _lane_window(tile_buf, (s,), c0, w) for s, c0, w in plan],
                axis=1)
            for cp in forwards:
                cp.wait_send()

            @pl.when(i == steps - 1)
            def _():
                for t in range(steps):
                    for cp in own_sends(t):
                        cp.wait_send()
                side.wait(s_in, s_out, side_sems)

        @pl.when(i == n - 1)
        def _():
            meta = pltpu.make_async_copy(s_out[ns - 1], meta_buf, meta_sem)
            meta.start()
            meta.wait()
            tokens = jnp.concatenate([meta_buf[s] for s in range(N_DEV)], axis=1)
            prenorm_tile(0, jnp.concatenate([jnp.zeros((PAD_LEAD, D_MODEL), F32), tokens], axis=0))

    last = steps - 1
    res = pl.pallas_call(
        body, name="gather_w_in", grid=(n,),
        in_specs=[ANY] + side.in_specs + _x_specs(nx, k, tile_of) + [pl.BlockSpec((1, D_MODEL), lambda i: (0, 0))],
        out_specs=[pl.BlockSpec((tr, NP), lambda i: (jnp.minimum(i, last), 0)), ANY] + side.out_specs + [
            pl.BlockSpec((k * CHUNK, D_MODEL), lambda i: (tile_of(i), 0))],
        out_shape=[jax.ShapeDtypeStruct((D_MODEL, NP), w_shard.dtype),
                   jax.ShapeDtypeStruct((N_DEV, D_MODEL, SHARD_IN), w_shard.dtype)] + side.out_shape + [
            jax.ShapeDtypeStruct((nc * CHUNK, D_MODEL), BF16)],
        scratch_shapes=[pltpu.VMEM((N_DEV, tr, SHARD_IN), w_shard.dtype),
                        pltpu.SemaphoreType.DMA((steps, N_DEV - 1)), pltpu.SemaphoreType.DMA((steps, N_DEV - 1)),
                        pltpu.SemaphoreType.DMA((N_DEV,)), pltpu.SemaphoreType.DMA(()),
                        pltpu.VMEM((N_DEV,) + tuple(small[ns - 1].shape), F32),
                        pltpu.SemaphoreType.DMA(())] + side.scratch,
        compiler_params=_params(("arbitrary",), side_effects=True),
    )(w_shard, *side.operands, *([x2d] * k), norm_w)
    return [res[0]] + list(res[2:])


def _w_in_chip_slabs(dw):
    int_of_ref = _internal_of_reference()
    plan = []
    for s in range(N_DEV):
        cols = int_of_ref[s * SHARD_IN:(s + 1) * SHARD_IN]
        plan.append([(int(cols[lo]), hi - lo) for lo, hi in _runs(cols, np.zeros_like(cols))])
    tr = RELAYOUT_ROWS
    steps = D_MODEL // tr
    pairs = N_DEV // 2

    def body(dw_ref, o_ref, land_ref, send_buf, land_buf, send_sems, recv_sems, load_sems):
        i = pl.program_id(0)
        x, y, c = lax.axis_index("x"), lax.axis_index("y"), lax.axis_index("c")
        slot = i % 2

        def sends(step, sl):
            return [pltpu.make_async_remote_copy(
                src_ref=send_buf.at[sl, 2 * q + (1 - c)], dst_ref=land_ref.at[q, pl.ds(step * tr, tr), :],
                send_sem=send_sems.at[sl, q], recv_sem=recv_sems.at[step, q],
                device_id=(x, y, 1 - c), device_id_type=pl.DeviceIdType.MESH) for q in range(pairs)]

        @pl.when(i < steps)
        def _():
            @pl.when(i >= 2)
            def _():
                for cp in sends(i - 2, slot):
                    cp.wait_send()

            for s in range(N_DEV):
                send_buf[slot, s] = jnp.concatenate([_lane_window(dw_ref, (), c0, n) for c0, n in plan[s]], axis=1)
            for cp in sends(i, slot):
                cp.start()

        @pl.when(i >= 1)
        def _():
            loads = []
            for q, cp in enumerate(sends(i - 1, 1 - slot)):
                cp.wait_recv()
                loads.append(pltpu.make_async_copy(land_ref.at[q, pl.ds((i - 1) * tr, tr), :], land_buf.at[q], load_sems.at[q]))
                loads[q].start()
            for q in range(pairs):
                loads[q].wait()
                o_ref[q] = (send_buf[1 - slot, 2 * q + c].astype(F32) + land_buf[q].astype(F32)).astype(o_ref.dtype)

        @pl.when(i == steps)
        def _():
            for cp in sends(steps - 2, steps % 2) + sends(steps - 1, 1 - steps % 2):
                cp.wait_send()

    last = steps - 1
    return pl.pallas_call(
        body, name="dw_in_relayout", grid=(steps + 1,),
        in_specs=[pl.BlockSpec((tr, NP), lambda i: (jnp.minimum(i, last), 0))],
        out_specs=[pl.BlockSpec((pairs, tr, SHARD_IN), lambda i: (0, jnp.maximum(i - 1, 0), 0)), ANY],
        out_shape=[jax.ShapeDtypeStruct((pairs, D_MODEL, SHARD_IN), dw.dtype),
                   jax.ShapeDtypeStruct((pairs, D_MODEL, SHARD_IN), dw.dtype)],
        scratch_shapes=[pltpu.VMEM((2, N_DEV, tr, SHARD_IN), dw.dtype), pltpu.VMEM((pairs, tr, SHARD_IN), dw.dtype),
                        pltpu.SemaphoreType.DMA((2, pairs)), pltpu.SemaphoreType.DMA((steps, pairs)),
                        pltpu.SemaphoreType.DMA((pairs,))],
        compiler_params=_params(("arbitrary",), side_effects=True),
    )(dw)[0]


def _rep_heads(a):
    return jnp.repeat(a, HEAD_DIM, axis=1)


SMALL = (("norm_pre_w", 2048), ("conv_b", 4096), ("dt_bias", 32), ("a_log", 32), ("d_skip", 32),
         ("ssd_norm_w", 2048), ("attn_sinks", 16), ("norm_post_w", 2048))
SMALL_USED = sum(size for _, size in SMALL)
SMALL_LEN = 10368


def _pack_small(d, loss=None):
    parts = [d[name].reshape(1, size) for name, size in SMALL]
    tail = jnp.zeros((1, SMALL_LEN - SMALL_USED), F32)
    if loss is not None:
        tail = tail.at[0, 0].set(loss)
    return jnp.concatenate(parts + [tail], axis=1)


def _unpack_small(vec):
    out, off = {}, 0
    for name, size in SMALL:
        out[name] = vec[:, off:off + size]
        off += size
    return out


def kernel(x, meta_tokens, norm_pre_w, w_in, conv_w, conv_b, dt_bias, a_log, d_skip, ssd_norm_w, attn_sinks, w_out, norm_post_w, loss_target, m_meta_tokens, m_norm_pre_w, m_w_in, m_conv_w, m_conv_b, m_dt_bias, m_a_log, m_d_skip, m_ssd_norm_w, m_attn_sinks, m_w_out, m_norm_post_w, v_meta_tokens, v_norm_pre_w, v_w_in, v_conv_w, v_conv_b, v_dt_bias, v_a_log, v_d_skip, v_ssd_norm_w, v_attn_sinks, v_w_out, v_norm_post_w):
    seq = x.shape[1]
    lp = seq + 2 * CHUNK
    x2d = x[0]

    w_all, conv_w_g, meta_g, hn = _gather_w_in(
        w_in[0].astype(BF16), [conv_w[0], meta_tokens], x2d, norm_pre_w)
    conv_w_full = _perm_xbc(jnp.transpose(conv_w_g, (1, 0, 2)).reshape(CONV_WIDTH, D_CONV))
    conv_b_int = _perm_xbc(conv_b)
    meta_full = jnp.transpose(meta_g, (1, 0, 2)).reshape(N_META, D_MODEL)
    h0 = jnp.concatenate([jnp.zeros((PAD_LEAD, D_MODEL), F32), meta_full], axis=0)

    pos = (jnp.arange(lp) - PAD_LEAD).astype(F32)
    half = HEAD_DIM // 2
    inv = ROPE_THETA ** (-jnp.arange(half, dtype=F32) / half)
    ang = pos[:, None] * inv[None, :]
    cos_t = jnp.tile(jnp.cos(ang), (1, 4))
    sin_t = jnp.tile(jnp.concatenate([-jnp.sin(ang), jnp.sin(ang)], axis=1), (1, 2))
    head_of_col = np.arange(D_SSD) // HEAD_DIM
    expand = jnp.asarray((np.arange(128)[:, None] == head_of_col[None, :]).astype(np.float32))
    reduce_t = jnp.asarray((head_of_col[:, None] == np.arange(128)[None, :]).astype(np.float32))
    tri = jnp.asarray(np.tril(np.ones((CHUNK, CHUNK), np.float32)))
    a_rep = _rep_heads(-jnp.exp(a_log))
    dsk_rep = _rep_heads(d_skip)
    dt_bias_pad = jnp.pad(dt_bias, ((0, 0), (0, 128 - SSD_HEADS)))
    sink_stack = jnp.repeat(attn_sinks.reshape(KV_HEADS, REP), CHUNK, axis=1).reshape(KV_HEADS, REP * CHUNK, 1)

    tm = _tile(lp, (1056, 704, 128, 64))
    proj, w_out_g = _matmul(hn, w_all, tm=tm, tn=1536, tk=D_MODEL, out_dtype=F32, name="in_proj",
                            comm=_Comm([(w_out[0].astype(BF16), "gather")]))
    w_out_full = w_out_g.reshape(D_MIX, D_MODEL)
    qr, kr, vb, dt_rep = _act_fwd(proj, cos_t, sin_t, expand, dt_bias_pad)
    mix, ytot, hprev, xbc, dsilu = _ssd_fwd(proj, conv_w_full, conv_b_int, dt_rep, a_rep, dsk_rep, ssd_norm_w, tri)
    att, mix = _attn_fwd(qr, kr, vb, proj, sink_stack, mix)
    out = _matmul(mix, w_out_full, tm=tm, tn=1024, tk=D_MIX, out_dtype=F32, name="out_proj")
    dout, dy, g_norm_post, loss_part = _post_loss(out, x2d, loss_target[0], norm_post_w)

    dmix = _matmul(dout, w_out_full, trans_b=True, tm=tm, tn=1024, tk=D_MODEL, out_dtype=F32, name="dmix")
    dw_out = _matmul(mix, dout, trans_a=True, tm=512, tn=1024, tk=lp, out_dtype=BF16, name="dw_out")
    dqr, dkr, dv, dg, dsink_rows = _attn_bwd(qr, kr, vb, att, proj, dmix, sink_stack)
    dproj, dxbc, ddt_part, dd_part, da_part, g_ssd_norm, g_out = _ssd_bwd(
        dmix, ytot, proj, xbc, dt_rep, hprev, a_rep, dsk_rep, ssd_norm_w, tri,
        _Comm([(dw_out.reshape(N_DEV, D_MIX // N_DEV, D_MODEL), "scatter")]))
    dproj, dconv_w_int, dconv_b_int = _conv_bwd(dxbc, dsilu, proj, conv_w_full, dproj)
    dproj, ddt_bias = _act_bwd(dqr, dkr, dv, dg, ddt_part, cos_t, sin_t, reduce_t, dproj)
    dw_all = _matmul(hn, dproj, trans_a=True, tm=512, tn=1024, tk=lp, out_dtype=BF16, name="dw_in")
    dw_chip = _w_in_chip_slabs(dw_all)
    dhn, g_in = _matmul(dproj, w_all, trans_b=True, tm=tm, tn=1024, tk=1536, out_dtype=F32, name="dhn",
                        comm=_Comm([(dw_chip, "scatter")], scope="chips"))
    grad_x, dh0, g_norm_pre = _prenorm_bwd(dhn, x2d, h0, dy, norm_pre_w)

    dmeta = dh0[PAD_LEAD:, :]
    dconv_w_ref = _unperm_xbc(dconv_w_int)
    heads = lambda part: part.reshape(SSD_HEADS, HEAD_DIM).sum(axis=1).reshape(1, SSD_HEADS)
    small_local = _pack_small({
        "norm_pre_w": g_norm_pre, "conv_b": _unperm_xbc(dconv_b_int), "dt_bias": ddt_bias[:, :SSD_HEADS],
        "a_log": heads(da_part) * (-jnp.exp(a_log)), "d_skip": heads(dd_part), "ssd_norm_w": g_ssd_norm,
        "attn_sinks": dsink_rows.reshape(Q_HEADS, CHUNK).sum(axis=1).reshape(1, Q_HEADS),
        "norm_post_w": g_norm_post}, loss=loss_part[0, 0])
    g_conv, g_meta, g_small = _exchange(
        [(jnp.transpose(dconv_w_ref.reshape(CONV_WIDTH, N_DEV, D_CONV // N_DEV), (1, 0, 2)), "scatter"),
         (jnp.transpose(dmeta.reshape(N_META, N_DEV, D_MODEL // N_DEV), (1, 0, 2)), "scatter"),
         (small_local, "gather")], "exchange_small")

    res = {}
    res["w_in"] = [o[None] for o in _adamw(g_in, w_in[0], m_w_in[0], v_w_in[0], "adamw_w_in")]
    res["w_out"] = [o[None] for o in _adamw(g_out, w_out[0], m_w_out[0], v_w_out[0], "adamw_w_out")]
    res["conv_w"] = [o[None] for o in _adamw(g_conv, conv_w[0], m_conv_w[0], v_conv_w[0], "adamw_conv_w")]
    res["meta_tokens"] = _adamw(g_meta, meta_tokens, m_meta_tokens, v_meta_tokens, "adamw_meta")
    given = dict(norm_pre_w=(norm_pre_w, m_norm_pre_w, v_norm_pre_w), conv_b=(conv_b, m_conv_b, v_conv_b),
                 dt_bias=(dt_bias, m_dt_bias, v_dt_bias), a_log=(a_log, m_a_log, v_a_log),
                 d_skip=(d_skip, m_d_skip, v_d_skip), ssd_norm_w=(ssd_norm_w, m_ssd_norm_w, v_ssd_norm_w),
                 attn_sinks=(attn_sinks, m_attn_sinks, v_attn_sinks),
                 norm_post_w=(norm_post_w, m_norm_post_w, v_norm_post_w))
    packed = [_pack_small({k: t[j] for k, t in given.items()}) for j in range(3)]
    small_out = _adamw(g_small, packed[0], packed[1], packed[2], "adamw_small")
    small_res = [_unpack_small(r) for r in small_out]
    loss = small_out[0][0, SMALL_USED]

    order = ["meta_tokens", "norm_pre_w", "w_in", "conv_w", "conv_b", "dt_bias", "a_log", "d_skip", "ssd_norm_w",
             "attn_sinks", "w_out", "norm_post_w"]
    outs = []
    for j in range(4):
        for name in order:
            outs.append(res[name][j] if name in res else small_res[j][name])
    return (loss, grad_x[None], *outs)
```

```python
import functools

import numpy as np
import jax
import jax.numpy as jnp
from jax import lax
from jax.experimental import pallas as pl
from jax.experimental.pallas import tpu as pltpu

F32 = jnp.float32
BF16 = jnp.bfloat16
HIGHEST = lax.Precision.HIGHEST

N_DEV = 8
D_MODEL = 2048
CHUNK = 64
N_META = 16
PAD_LEAD = CHUNK - N_META
EPS = 1e-6
N_GROUPS = 8
HEADS_PER_GROUP = 4
HEAD_DIM = 64
GROUP_W = HEADS_PER_GROUP * HEAD_DIM
D_STATE = 128
D_SSD = 2048
D_CONV = 4096
SSD_HEADS = 32
CONV_WIDTH = 4
Q_HEADS = 16
KV_HEADS = 4
REP = 4
D_ATT = 1024
D_KV = 256
WINDOW_CHUNKS = 2
BAND = (WINDOW_CHUNKS + 1) * CHUNK
ROPE_THETA = 10000.0
D_MIX = D_SSD + D_ATT
D_IN_PROJ = 8736
SHARD_IN = D_IN_PROJ // N_DEV

OFF_Z, OFF_XBC, OFF_Q, OFF_G, OFF_K, OFF_V, OFF_DT = 0, 2048, 6144, 7168, 8192, 8448, 8704
NP = 9216
TAIL_W = NP - OFF_Q
XBC_BLK = 512
SSD_GPS = 8
SSD_CPS = 2

ADAM_LR, ADAM_B1, ADAM_B2, ADAM_EPS, ADAM_WD, ADAM_STEP = 0.001, 0.9, 0.999, 1e-08, 0.01, 10

VMEM_LIMIT = 48 * 1024 * 1024

NN = (((1,), (0,)), ((), ()))
NT = (((1,), (1,)), ((), ()))
TN = (((0,), (0,)), ((), ()))
ANY = pl.BlockSpec(memory_space=pl.ANY)


def _dot(a, b, dims=NN, precision=None):
    return lax.dot_general(a, b, dims, precision=precision, preferred_element_type=F32)


def _tile(n, prefs):
    for t in prefs:
        if n % t == 0:
            return t
    return n


def _params(sem, vmem=VMEM_LIMIT, side_effects=False):
    return pltpu.CompilerParams(dimension_semantics=sem, vmem_limit_bytes=vmem, has_side_effects=side_effects)


def _sigmoid(x):
    return 1.0 / (1.0 + jnp.exp(-x))


class _Comm:
    def __init__(self, items, scope="devices"):
        self.items = items
        self.scope = scope
        self.slabs = slabs = N_DEV if scope == "devices" else N_DEV // 2
        self.n = n = len(items)
        self.operands = [arr for arr, _ in items]
        self.in_specs = [ANY] * n
        self.out_specs = [ANY] * n
        self.out_shape = [jax.ShapeDtypeStruct((slabs,) + tuple(arr.shape) if kind == "gather" else tuple(arr.shape),
                                               arr.dtype) for arr, kind in items]
        self.scratch = [pltpu.SemaphoreType.DMA((n, slabs - 1)), pltpu.SemaphoreType.DMA((n, slabs - 1)),
                        pltpu.SemaphoreType.DMA((n,))]

    def _places(self):
        pos = (lax.axis_index("x"), lax.axis_index("y"), lax.axis_index("c"))
        if self.scope == "devices":
            index = lambda p: 4 * p[0] + 2 * p[1] + p[2]
            masks = range(1, N_DEV)
        else:
            index = lambda p: 2 * p[0] + p[1]
            masks = (2, 4, 6)
        peers = []
        for k in masks:
            p = tuple(1 - pos[b] if (k >> (2 - b)) & 1 else pos[b] for b in range(3))
            peers.append((p, index(p)))
        return index(pos), peers

    def _copies(self, ins, outs, sems, landed):
        send_sems, recv_sems, local_sems = sems
        me, peers = self._places()
        local, remote = [], []
        for a, (_, kind) in enumerate(self.items):
            own = ins[a] if kind == "gather" else ins[a].at[me]
            local.append(pltpu.make_async_copy(own, outs[a].at[me], local_sems.at[a]))
            for k, (p, pid) in enumerate(peers):
                remote.append(pltpu.make_async_remote_copy(
                    src_ref=ins[a] if kind == "gather" else ins[a].at[pid],
                    dst_ref=outs[a].at[pid if landed else me],
                    send_sem=send_sems.at[a, k], recv_sem=recv_sems.at[a, k],
                    device_id=p, device_id_type=pl.DeviceIdType.MESH))
        return local, remote

    def start(self, ins, outs, sems):
        local, remote = self._copies(ins, outs, sems, landed=False)
        for cp in local + remote:
            cp.start()

    def wait(self, ins, outs, sems):
        local, remote = self._copies(ins, outs, sems, landed=True)
        for cp in remote + local:
            cp.wait()


def _exchange(items, name):
    comm = _Comm(items)
    n = comm.n

    def body(*refs):
        ins, outs, sems = refs[:n], refs[n:2 * n], refs[2 * n:]
        comm.start(ins, outs, sems)
        comm.wait(ins, outs, sems)

    return pl.pallas_call(
        body, name=name, in_specs=comm.in_specs, out_specs=comm.out_specs, out_shape=comm.out_shape,
        scratch_shapes=comm.scratch, compiler_params=pltpu.CompilerParams(has_side_effects=True),
    )(*comm.operands)


class _GatherTwoLevel:
    def __init__(self, arrays):
        self.arrays = arrays
        self.n = n = len(arrays)
        self.operands = list(arrays)
        self.in_specs = [ANY] * n
        self.out_specs = [ANY] * n
        self.out_shape = [jax.ShapeDtypeStruct((N_DEV,) + tuple(a.shape), a.dtype) for a in arrays]
        self.scratch = [pltpu.SemaphoreType.DMA((n, N_DEV - 1)), pltpu.SemaphoreType.DMA((n, N_DEV - 1)),
                        pltpu.SemaphoreType.DMA((n,))]

    def _plan(self, ins, outs, sems):
        send_sems, recv_sems, local_sems = sems
        x, y, c = lax.axis_index("x"), lax.axis_index("y"), lax.axis_index("c")
        me, sibling = (x, y, c), (x, y, 1 - c)
        chips = [(1 - x, y), (x, 1 - y), (1 - x, 1 - y)]

        def slab(a, place):
            return outs[a].at[4 * place[0] + 2 * place[1] + place[2]]

        def copy(a, k, block, to, src=None):
            return pltpu.make_async_remote_copy(
                src_ref=slab(a, block) if src is None else src, dst_ref=slab(a, block),
                send_sem=send_sems.at[a, k], recv_sem=recv_sems.at[a, k],
                device_id=to, device_id_type=pl.DeviceIdType.MESH)

        own, mine = [], []
        for a in range(self.n):
            mine.append(pltpu.make_async_copy(ins[a], slab(a, me), local_sems.at[a]))
            own.append(copy(a, 0, me, sibling, src=ins[a]))
            own += [copy(a, 1 + j, me, (*chip, c), src=ins[a]) for j, chip in enumerate(chips)]
        return me, sibling, chips, c, copy, own, mine

    def start(self, ins, outs, sems):
        _, _, _, _, _, own, mine = self._plan(ins, outs, sems)
        for cp in mine + own:
            cp.start()

    def wait(self, ins, outs, sems):
        me, sibling, chips, c, copy, own, mine = self._plan(ins, outs, sems)
        forwards = []
        for j, chip in enumerate(chips):
            for a in range(self.n):
                copy(a, 1 + j, (*chip, c), me).wait_recv()
                fwd = copy(a, 4 + j, (*chip, c), sibling)
                fwd.start()
                forwards.append(fwd)
        for a in range(self.n):
            copy(a, 0, sibling, me).wait_recv()
            for j, chip in enumerate(chips):
                copy(a, 4 + j, (*chip, 1 - c), me).wait_recv()
        for cp in own + forwards:
            cp.wait_send()
        for loc in mine:
            loc.wait()


def _matmul(a, b, *, tm, tn, tk, out_dtype, name, trans_a=False, trans_b=False, comm=None):
    m, k = (a.shape[1], a.shape[0]) if trans_a else a.shape
    n = b.shape[0] if trans_b else b.shape[1]
    nk = k // tk
    dims = TN if trans_a else (NT if trans_b else NN)
    assert not (trans_a and trans_b)
    nc = comm.n if comm else 0
    grid = (m // tm, n // tn, nk)

    def body(*refs):
        a_ref, b_ref = refs[:2]
        cin = refs[2:2 + nc]
        o_ref = refs[2 + nc]
        cout = refs[3 + nc:3 + 2 * nc]
        scratch = refs[3 + 2 * nc:]
        sems = scratch[len(scratch) - 3:] if comm else None
        i, j, kk = pl.program_id(0), pl.program_id(1), pl.program_id(2)
        if comm:
            @pl.when((i == 0) & (j == 0) & (kk == 0))
            def _():
                comm.start(cin, cout, sems)

        if nk == 1:
            o_ref[...] = _dot(a_ref[...], b_ref[...], dims).astype(out_dtype)
        else:
            acc_ref = scratch[0]

            @pl.when(kk == 0)
            def _():
                acc_ref[...] = jnp.zeros_like(acc_ref)

            acc_ref[...] += _dot(a_ref[...], b_ref[...], dims)

            @pl.when(kk == nk - 1)
            def _():
                o_ref[...] = acc_ref[...].astype(out_dtype)

        if comm:
            @pl.when((i == grid[0] - 1) & (j == grid[1] - 1) & (kk == nk - 1))
            def _():
                comm.wait(cin, cout, sems)

    a_spec = (pl.BlockSpec((tk, tm), lambda i, j, kk: (kk, i)) if trans_a
              else pl.BlockSpec((tm, tk), lambda i, j, kk: (i, kk)))
    b_spec = (pl.BlockSpec((tn, tk), lambda i, j, kk: (j, kk)) if trans_b
              else pl.BlockSpec((tk, tn), lambda i, j, kk: (kk, j)))
    sem = ("arbitrary",) * 3 if comm else ("parallel", "parallel", "arbitrary")
    res = pl.pallas_call(
        body, name=name, grid=grid,
        in_specs=[a_spec, b_spec] + (comm.in_specs if comm else []),
        out_specs=[pl.BlockSpec((tm, tn), lambda i, j, kk: (i, j))] + (comm.out_specs if comm else []),
        out_shape=[jax.ShapeDtypeStruct((m, n), out_dtype)] + (comm.out_shape if comm else []),
        scratch_shapes=([] if nk == 1 else [pltpu.VMEM((tm, tn), F32)]) + (comm.scratch if comm else []),
        compiler_params=_params(sem, side_effects=bool(comm)),
    )(a, b, *(comm.operands if comm else []))
    return res if comm else res[0]


ROW_K = 6
X_K = 8


def _x_specs(nx, k, tile_of=lambda i: i):
    return [pl.BlockSpec((CHUNK, D_MODEL), functools.partial(lambda u, i: (jnp.clip(k * tile_of(i) + u - 1, 0, nx - 1), 0), u))
            for u in range(k)]


def _chunk_of_row(i, k):
    return k * i + lax.broadcasted_iota(jnp.int32, (k * CHUNK, 1), 0) // CHUNK


CONV_COLS = 512
HALO = 8


def _conv_pre(ext, w, b):
    taps = [ext[HALO:, :]] + [pltpu.roll(ext, j, 0)[HALO:, :] for j in range(1, CONV_WIDTH)]
    acc = b + w[3:4, :] * taps[0]
    for j in range(1, CONV_WIDTH):
        acc = acc + w[3 - j:4 - j, :] * taps[j]
    return acc, taps


def _conv_bwd(dxbc, dsilu, proj, conv_w, dproj):
    lp = proj.shape[0]
    t = _tile(lp, (704, 384, 128, 64))
    hb = t // HALO
    nt = lp // t
    c0 = OFF_XBC // CONV_COLS

    def body(dx_ref, dxn_ref, ds_ref, dsn_ref, u_ref, w_ref, _, du_ref, dw_ref, db_ref):
        i = pl.program_id(1)
        w = w_ref[...]
        u = u_ref[...].astype(F32)
        dxn = dxn_ref[...] * (i < nt - 1).astype(F32)
        dpre = jnp.concatenate([dx_ref[...] * ds_ref[...], dxn * dsn_ref[...]], axis=0)
        ahead = [dpre[:t, :]] + [pltpu.roll(dpre, t + HALO - j, 0)[:t, :] for j in range(1, CONV_WIDTH)]
        du = w[3:4, :] * ahead[0]
        for j in range(1, CONV_WIDTH):
            du = du + w[3 - j:4 - j, :] * ahead[j]
        du_ref[...] = du.astype(BF16)

        @pl.when(i == 0)
        def _():
            dw_ref[...] = jnp.zeros_like(dw_ref)
            db_ref[...] = jnp.zeros_like(db_ref)

        db_ref[...] += jnp.sum(ahead[0], axis=0, keepdims=True)
        for j in range(CONV_WIDTH):
            dw_ref[3 - j:4 - j, :] += jnp.sum(ahead[j] * u, axis=0, keepdims=True)

    nxt = lambda i: jnp.minimum((i + 1) * hb, lp // HALO - 1)
    return pl.pallas_call(
        body, name="conv_bwd", grid=(D_CONV // CONV_COLS, nt),
        in_specs=[
            pl.BlockSpec((t, CONV_COLS), lambda j, i: (i, j)),
            pl.BlockSpec((HALO, CONV_COLS), lambda j, i: (nxt(i), j)),
            pl.BlockSpec((t, CONV_COLS), lambda j, i: (i, j)),
            pl.BlockSpec((HALO, CONV_COLS), lambda j, i: (nxt(i), j)),
            pl.BlockSpec((t, CONV_COLS), lambda j, i: (i, c0 + j)),
            pl.BlockSpec((CONV_WIDTH, CONV_COLS), lambda j, i: (0, j)),
            ANY,
        ],
        out_specs=[
            pl.BlockSpec((t, CONV_COLS), lambda j, i: (i, c0 + j)),
            pl.BlockSpec((CONV_WIDTH, CONV_COLS), lambda j, i: (0, j)),
            pl.BlockSpec((1, CONV_COLS), lambda j, i: (0, j)),
        ],
        out_shape=[
            jax.ShapeDtypeStruct((lp, NP), BF16),
            jax.ShapeDtypeStruct((CONV_WIDTH, D_CONV), F32),
            jax.ShapeDtypeStruct((1, D_CONV), F32),
        ],
        input_output_aliases={6: 0},
        compiler_params=_params(("parallel", "arbitrary")),
    )(dxbc, dxbc, dsilu, dsilu, proj, conv_w, dproj)


def _swap_halves(t):
    w = t.shape[-1]
    lane = lax.broadcasted_iota(jnp.int32, t.shape, 1)
    return jnp.where((lane % HEAD_DIM) < HEAD_DIM // 2, pltpu.roll(t, w - HEAD_DIM // 2, 1),
                     pltpu.roll(t, HEAD_DIM // 2, 1))


def _act_fwd(proj, cos_t, sin_t, expand, dt_bias_pad):
    lp = proj.shape[0]
    t = _tile(lp, (384, 128, 64))

    def body(q_ref, k_ref, v_ref, dt_ref, cos_ref, sin_ref, ex_ref, bias_ref, qo_ref, ko_ref, vo_ref, dto_ref):
        i = pl.program_id(0)
        cos = cos_ref[...]
        sin = sin_ref[...]
        q = q_ref[...]
        qo_ref[...] = (q * jnp.tile(cos, (1, D_ATT // 128)) + _swap_halves(q) * jnp.tile(sin, (1, D_ATT // 128))).astype(BF16)
        k = k_ref[...]
        ko_ref[...] = (k * jnp.tile(cos, (1, D_KV // 128)) + _swap_halves(k) * jnp.tile(sin, (1, D_KV // 128))).astype(BF16)
        vo_ref[...] = v_ref[...].astype(BF16)
        raw = dt_ref[...] + bias_ref[...]
        sp = jnp.maximum(raw, 0.0) + jnp.log1p(jnp.exp(-jnp.abs(raw)))
        row = i * t + lax.broadcasted_iota(jnp.int32, sp.shape, 0)
        dto_ref[...] = _dot(jnp.where(row >= PAD_LEAD, sp, 0.0), ex_ref[...], NN, HIGHEST)

    return pl.pallas_call(
        body, name="act_fwd", grid=(lp // t,),
        in_specs=[
            pl.BlockSpec((t, D_ATT), lambda i: (i, OFF_Q // D_ATT)),
            pl.BlockSpec((t, D_KV), lambda i: (i, OFF_K // D_KV)),
            pl.BlockSpec((t, D_KV), lambda i: (i, OFF_V // D_KV)),
            pl.BlockSpec((t, 128), lambda i: (i, OFF_DT // 128)),
            pl.BlockSpec((t, 128), lambda i: (i, 0)),
            pl.BlockSpec((t, 128), lambda i: (i, 0)),
            pl.BlockSpec((128, D_SSD), lambda i: (0, 0)),
            pl.BlockSpec((1, 128), lambda i: (0, 0)),
        ],
        out_specs=[
            pl.BlockSpec((t, D_ATT), lambda i: (i, 0)),
            pl.BlockSpec((t, D_KV), lambda i: (i, 0)),
            pl.BlockSpec((t, D_KV), lambda i: (i, 0)),
            pl.BlockSpec((t, D_SSD), lambda i: (i, 0)),
        ],
        out_shape=[
            jax.ShapeDtypeStruct((lp, D_ATT), BF16),
            jax.ShapeDtypeStruct((lp, D_KV), BF16),
            jax.ShapeDtypeStruct((lp, D_KV), BF16),
            jax.ShapeDtypeStruct((lp, D_SSD), F32),
        ],
        compiler_params=_params(("parallel",)),
    )(proj, proj, proj, proj, cos_t, sin_t, expand, dt_bias_pad)


def _act_bwd(dqr, dkr, dv, dg, ddt_part, cos_t, sin_t, reduce_t, dproj):
    lp = dqr.shape[0]
    t = _tile(lp, (384, 128, 64))

    def body(dq_ref, dk_ref, dv_ref, dg_ref, ddt_ref, cos_ref, sin_ref, red_ref, _, o_ref, db_ref):
        i = pl.program_id(0)
        cos = cos_ref[...]
        sin = sin_ref[...]
        dq = dq_ref[...]
        dq = dq * jnp.tile(cos, (1, D_ATT // 128)) + _swap_halves(dq * jnp.tile(sin, (1, D_ATT // 128)))
        dk = dk_ref[...]
        dk = dk * jnp.tile(cos, (1, D_KV // 128)) + _swap_halves(dk * jnp.tile(sin, (1, D_KV // 128)))
        ddt = _dot(ddt_ref[...], red_ref[...], NN, HIGHEST)
        o_ref[...] = jnp.concatenate(
            [dq.astype(BF16), dg_ref[...].astype(BF16), dk.astype(BF16), dv_ref[...].astype(BF16), ddt.astype(BF16),
             jnp.zeros((t, NP - OFF_DT - 128), BF16)], axis=1)

        @pl.when(i == 0)
        def _():
            db_ref[...] = jnp.zeros_like(db_ref)

        db_ref[...] += jnp.sum(ddt, axis=0, keepdims=True)

    return pl.pallas_call(
        body, name="act_bwd", grid=(lp // t,),
        in_specs=[
            pl.BlockSpec((t, D_ATT), lambda i: (i, 0)),
            pl.BlockSpec((t, D_KV), lambda i: (i, 0)),
            pl.BlockSpec((t, D_KV), lambda i: (i, 0)),
            pl.BlockSpec((t, D_ATT), lambda i: (i, 0)),
            pl.BlockSpec((t, D_SSD), lambda i: (i, 0)),
            pl.BlockSpec((t, 128), lambda i: (i, 0)),
            pl.BlockSpec((t, 128), lambda i: (i, 0)),
            pl.BlockSpec((D_SSD, 128), lambda i: (0, 0)),
            ANY,
        ],
        out_specs=[pl.BlockSpec((t, TAIL_W), lambda i: (i, OFF_Q // TAIL_W)), pl.BlockSpec((1, 128), lambda i: (0, 0))],
        out_shape=[jax.ShapeDtypeStruct((lp, NP), BF16), jax.ShapeDtypeStruct((1, 128), F32)],
        input_output_aliases={8: 0},
        compiler_params=_params(("arbitrary",)),
    )(dqr, dkr, dv, dg, ddt_part, cos_t, sin_t, reduce_t, dproj)


def _cs_row(cs):
    row = lax.broadcasted_iota(jnp.int32, cs.shape, 0)
    lane = lax.broadcasted_iota(jnp.int32, cs.shape, 1)
    return jnp.sum(jnp.where(row == lane % HEAD_DIM, cs, 0.0), axis=0, keepdims=True)


def _ssd_fwd(proj, conv_w, conv_b, dt_rep, a_rep, dsk_rep, wn, tri):
    lp = proj.shape[0]
    nc = lp // CHUNK
    assert SSD_GPS == N_GROUPS
    gw = SSD_GPS * GROUP_W
    cps = _tile(nc, (SSD_CPS, 1))

    def body(rawa_ref, rawb_ref, cw_ref, cb_ref, dt_ref, z_ref, a_ref, dsk_ref, wn_ref, tri_ref,
             yn_ref, ytot_ref, hprev_ref, xbc_ref, dsilu_ref, h_scr, tail_scr):
        @pl.when(pl.program_id(1) == 0)
        def _():
            h_scr[...] = jnp.zeros_like(h_scr)
            tail_scr[...] = jnp.zeros_like(tail_scr)

        for sub in range(cps):
            rs = slice(sub * CHUNK, (sub + 1) * CHUNK)
            raw = jnp.concatenate([rawa_ref[rs, :], rawb_ref[rs, :]], axis=1)
            pre, _ = _conv_pre(jnp.concatenate([tail_scr[...], raw], axis=0), cw_ref[...], cb_ref[...])
            tail_scr[...] = raw[CHUNK - HALO:, :]
            sg = _sigmoid(pre)
            xbc_ref[rs, :] = pre * sg
            dsilu_ref[rs, :] = sg * (1.0 + pre * (1.0 - sg))

        G = range(SSD_GPS)
        colsl = [slice(gi * GROUP_W, (gi + 1) * GROUP_W) for gi in G]
        rows4 = lax.broadcasted_iota(jnp.int32, (GROUP_W, GROUP_W), 0) // HEAD_DIM
        cols4 = lax.broadcasted_iota(jnp.int32, (GROUP_W, GROUP_W), 1) // HEAD_DIM
        lrow = lax.broadcasted_iota(jnp.int32, (CHUNK, GROUP_W), 0)
        lcol = lax.broadcasted_iota(jnp.int32, (CHUNK, GROUP_W), 1) % HEAD_DIM

        def one_chunk(sub):
            rs = slice(sub * CHUNK, (sub + 1) * CHUNK)
            xbc = [xbc_ref[rs, gi * XBC_BLK:(gi + 1) * XBC_BLK] for gi in G]
            dt = [dt_ref[rs, colsl[gi]] for gi in G]
            xs = [xbc[gi][:, :GROUP_W] for gi in G]
            b = [xbc[gi][:, GROUP_W:GROUP_W + D_STATE].astype(BF16) for gi in G]
            c = [xbc[gi][:, GROUP_W + D_STATE:].astype(BF16) for gi in G]
            hprev = [h_scr[gi] for gi in G]
            cs = [_dot(tri_ref[...], dt[gi] * a_ref[:, colsl[gi]], NN, HIGHEST) for gi in G]
            yoff = [_dot(c[gi], hprev[gi].astype(BF16)) for gi in G]
            cs_t = [_cs_row(cs[gi]) for gi in G]
            xdt = [xs[gi] * dt[gi] for gi in G]
            cs_last = [cs[gi][CHUNK - 1:CHUNK, :] for gi in G]
            st = [_dot(b[gi], (xdt[gi] * jnp.exp(cs_last[gi] - cs[gi])).astype(BF16), TN) for gi in G]
            cb4 = [_dot(c[gi], jnp.concatenate([b[gi]] * HEADS_PER_GROUP, axis=0), NT) for gi in G]
            m = [(cb4[gi] * jnp.exp(jnp.where(lrow >= lcol, cs[gi] - cs_t[gi], -jnp.inf))).astype(BF16) for gi in G]
            xbd = [jnp.where(rows4 == cols4, jnp.concatenate([xdt[gi].astype(BF16)] * HEADS_PER_GROUP, axis=0), 0.0)
                   for gi in G]
            ydiag = [_dot(m[gi], xbd[gi]) for gi in G]
            for gi in G:
                cols = colsl[gi]
                ytot = ydiag[gi] + yoff[gi] * jnp.exp(cs[gi]) + dsk_ref[:, cols] * xs[gi]
                z = z_ref[rs, cols]
                gz = ytot * (z * _sigmoid(z))
                rstd = lax.rsqrt(jnp.mean(gz * gz, axis=-1, keepdims=True) + EPS)
                hprev_ref[sub, gi] = hprev[gi]
                h_scr[gi] = hprev[gi] * jnp.exp(cs_last[gi]) + st[gi]
                ytot_ref[rs, cols] = ytot
                yn_ref[rs, cols] = (gz * rstd * wn_ref[:, cols]).astype(BF16)

        for sub in range(cps):
            one_chunk(sub)

    vec = pl.BlockSpec((1, gw), lambda g, c: (0, g))
    blk = pl.BlockSpec((cps * CHUNK, gw), lambda g, c: (c, g))
    xblk = pl.BlockSpec((cps * CHUNK, D_CONV), lambda g, c: (c, 0))
    half = D_CONV // 2
    return pl.pallas_call(
        body, name="ssd_fwd", grid=(N_GROUPS // SSD_GPS, nc // cps),
        in_specs=[
            pl.BlockSpec((cps * CHUNK, half), lambda g, c: (c, OFF_XBC // half)),
            pl.BlockSpec((cps * CHUNK, half), lambda g, c: (c, OFF_XBC // half + 1)),
            pl.BlockSpec((CONV_WIDTH, D_CONV), lambda g, c: (0, 0)),
            pl.BlockSpec((1, D_CONV), lambda g, c: (0, 0)),
            blk, blk, vec, vec, vec,
            pl.BlockSpec((CHUNK, CHUNK), lambda g, c: (0, 0)),
        ],
        out_specs=[blk, blk, pl.BlockSpec((cps, SSD_GPS, D_STATE, GROUP_W), lambda g, c: (c, g, 0, 0)), xblk, xblk],
        out_shape=[
            jax.ShapeDtypeStruct((lp, D_MIX), BF16),
            jax.ShapeDtypeStruct((lp, D_SSD), F32),
            jax.ShapeDtypeStruct((nc, N_GROUPS, D_STATE, GROUP_W), F32),
            jax.ShapeDtypeStruct((lp, D_CONV), F32),
            jax.ShapeDtypeStruct((lp, D_CONV), F32),
        ],
        scratch_shapes=[pltpu.VMEM((SSD_GPS, D_STATE, GROUP_W), F32), pltpu.VMEM((HALO, D_CONV), F32)],
        compiler_params=_params(("arbitrary", "arbitrary")),
    )(proj, proj, conv_w, conv_b, dt_rep, proj, a_rep, dsk_rep, wn, tri)


def _ssd_bwd(dmix, ytot, proj, xbc, dt_rep, hprev, a_rep, dsk_rep, wn, tri, comm):
    lp = xbc.shape[0]
    nc = lp // CHUNK
    gps = SSD_GPS
    gw = gps * GROUP_W
    cps = _tile(nc, (SSD_CPS, 1))
    ncm = comm.n
    n_in, n_out = 10, 6
    grid = (N_GROUPS // gps, nc // cps)

    def all_groups(refs):
        (dyn_ref, ytot_ref, z_ref, xbc_ref, dt_ref, hprev_ref, a_ref, dsk_ref, wn_ref, tri_ref,
         dz_ref, dxbc_ref, ddt_ref, dd_ref, da_ref, dwn_ref, dh_scr) = refs
        G = range(gps)
        H = range(HEADS_PER_GROUP)
        cl = [slice(gi * GROUP_W, (gi + 1) * GROUP_W) for gi in G]
        hl = [slice(r * HEAD_DIM, (r + 1) * HEAD_DIM) for r in H]
        tri = tri_ref[...]
        xbc = [xbc_ref[:, gi * XBC_BLK:(gi + 1) * XBC_BLK] for gi in G]
        dt = [dt_ref[:, cl[gi]] for gi in G]
        a = [a_ref[:, cl[gi]] for gi in G]
        xs = [xbc[gi][:, :GROUP_W] for gi in G]
        bbf = [xbc[gi][:, GROUP_W:GROUP_W + D_STATE].astype(BF16) for gi in G]
        cbf = [xbc[gi][:, GROUP_W + D_STATE:].astype(BF16) for gi in G]
        hprev = [hprev_ref[gi] for gi in G]
        hbf = [hprev[gi].astype(BF16) for gi in G]
        dhn = [dh_scr[gi] for gi in G]
        dhnb = [dhn[gi].astype(BF16) for gi in G]
        cs = [_dot(tri, dt[gi] * a[gi], NN, HIGHEST) for gi in G]
        g = [_dot(cbf[gi], hbf[gi]) for gi in G]
        dxw = [_dot(bbf[gi], dhnb[gi]) for gi in G]
        cs_t = [_cs_row(cs[gi]) for gi in G]
        dy = []
        for gi in G:
            ytot = ytot_ref[:, cl[gi]]
            z = z_ref[:, cl[gi]]
            dyn = dyn_ref[:, cl[gi]]
            sz = _sigmoid(z)
            silu_z = z * sz
            gz = ytot * silu_z
            rstd = lax.rsqrt(jnp.mean(gz * gz, axis=-1, keepdims=True) + EPS)
            xhat = gz * rstd
            dwn_ref[:, cl[gi]] += jnp.sum(dyn * xhat, axis=0, keepdims=True)
            dxhat = dyn * wn_ref[:, cl[gi]]
            dgz = rstd * (dxhat - xhat * jnp.mean(dxhat * xhat, axis=-1, keepdims=True))
            dy.append(dgz * silu_z)
            dz_ref[:, cl[gi]] = (dgz * ytot * (sz * (1.0 + z * (1.0 - sz)))).astype(BF16)
            dd_ref[:, cl[gi]] += jnp.sum(dy[gi] * xs[gi], axis=0, keepdims=True)
        xdt = [xs[gi] * dt[gi] for gi in G]
        e = [jnp.exp(cs[gi]) for gi in G]
        cs_last = [cs[gi][CHUNK - 1:CHUNK, :] for gi in G]
        dte = [jnp.exp(cs_last[gi] - cs[gi]) for gi in G]
        cd = [jnp.exp(cs_last[gi]) for gi in G]
        dgb = [(dy[gi] * e[gi]).astype(BF16) for gi in G]
        dyb = [dy[gi].astype(BF16) for gi in G]
        xdtb = [xdt[gi].astype(BF16) for gi in G]
        dc = [_dot(dgb[gi], hbf[gi], NT) for gi in G]
        dhprev = [_dot(cbf[gi], dgb[gi], TN) for gi in G]
        db = [_dot((xdt[gi] * dte[gi]).astype(BF16), dhnb[gi], NT) for gi in G]
        row = lax.broadcasted_iota(jnp.int32, (CHUNK, CHUNK), 0)
        causal = row >= lax.broadcasted_iota(jnp.int32, (CHUNK, CHUNK), 1)
        cb = [_dot(cbf[gi], bbf[gi], NT) for gi in G]
        dm = [[_dot(dyb[gi][:, hl[r]], xdtb[gi][:, hl[r]], NT) for r in H] for gi in G]
        mb, dseg, dcbb = [], [], []
        for gi in G:
            mb.append([])
            dseg.append([])
            dcb = None
            for r in H:
                seg = cs[gi][:, r * HEAD_DIM:r * HEAD_DIM + 1] - cs_t[gi][:, hl[r]]
                lm = jnp.exp(jnp.where(causal, seg, -jnp.inf))
                m = cb[gi] * lm
                mb[gi].append(m.astype(BF16))
                dseg[gi].append(dm[gi][r] * m)
                dcb = dm[gi][r] * lm if r == 0 else dcb + dm[gi][r] * lm
            dcbb.append(dcb.astype(BF16))
        dxdt_diag = [[_dot(mb[gi][r], dyb[gi][:, hl[r]], TN) for r in H] for gi in G]
        ones = jnp.ones((CHUNK, HEAD_DIM), F32)
        colsum = [[_dot(dseg[gi][r], ones, TN, HIGHEST) for r in H] for gi in G]
        dc2 = [_dot(dcbb[gi], bbf[gi]) for gi in G]
        db2 = [_dot(dcbb[gi], cbf[gi], TN) for gi in G]
        dcs = []
        for gi in G:
            t_dte = dxw[gi] * xdt[gi] * dte[gi]
            dcs_last = (jnp.sum(dhn[gi] * hprev[gi], axis=0, keepdims=True) * cd[gi]
                        + jnp.sum(t_dte, axis=0, keepdims=True))
            diag = jnp.concatenate(
                [(jnp.sum(dseg[gi][r], axis=1, keepdims=True) - colsum[gi][r]) * (1.0 / HEAD_DIM) for r in H], axis=1)
            d = dy[gi] * g[gi] * e[gi] - t_dte + diag
            row = lax.broadcasted_iota(jnp.int32, d.shape, 0)
            dcs.append(d + jnp.where(row == CHUNK - 1, dcs_last, 0.0))
        dda = [_dot(tri, dcs[gi], TN, HIGHEST) for gi in G]
        for gi in G:
            dxdt = dxw[gi] * dte[gi] + jnp.concatenate(dxdt_diag[gi], axis=1)
            da_ref[:, cl[gi]] += jnp.sum(dda[gi] * dt[gi], axis=0, keepdims=True)
            ddt = dda[gi] * a[gi] + dxdt * xs[gi]
            dxs = dsk_ref[:, cl[gi]] * dy[gi] + dxdt * dt[gi]
            ddt_ref[:, cl[gi]] = ddt * (1.0 - jnp.exp(-dt[gi]))
            dxbc_ref[:, gi * XBC_BLK:(gi + 1) * XBC_BLK] = jnp.concatenate(
                [dxs, db[gi] + db2[gi], dc[gi] + dc2[gi]], axis=1)
            dh_scr[gi] = dhprev[gi] + dhn[gi] * cd[gi]

    def body(*refs):
        ins = refs[:n_in]
        cin = refs[n_in:n_in + ncm]
        outs = refs[n_in + ncm:n_in + ncm + n_out]
        cout = refs[n_in + ncm + n_out:n_in + 2 * ncm + n_out]
        dh_scr = refs[n_in + 2 * ncm + n_out]
        sems = refs[n_in + 2 * ncm + n_out + 1:]
        g, c = pl.program_id(0), pl.program_id(1)

        @pl.when((g == 0) & (c == 0))
        def _():
            comm.start(cin, cout, sems)

        @pl.when(c == 0)
        def _():
            dh_scr[...] = jnp.zeros_like(dh_scr)
            for ref in outs[3:]:
                ref[...] = jnp.zeros_like(ref)

        for sub in reversed(range(cps)):
            rs = pl.ds(sub * CHUNK, CHUNK)
            chunk_ins = tuple(r.at[rs] for r in ins[:5]) + (ins[5].at[sub],) + ins[6:]
            chunk_outs = tuple(r.at[rs] for r in outs[:3]) + outs[3:]
            all_groups(chunk_ins + chunk_outs + (dh_scr,))

        @pl.when((g == grid[0] - 1) & (c == grid[1] - 1))
        def _():
            comm.wait(cin, cout, sems)

    rev = lambda c: grid[1] - 1 - c
    vec = pl.BlockSpec((1, gw), lambda g, c: (0, g))
    blk = pl.BlockSpec((cps * CHUNK, gw), lambda g, c: (rev(c), g))
    xblk = pl.BlockSpec((cps * CHUNK, gps * XBC_BLK), lambda g, c: (rev(c), g))
    res = pl.pallas_call(
        body, name="ssd_bwd", grid=grid,
        in_specs=[blk, blk, blk, xblk, blk,
                  pl.BlockSpec((cps, gps, D_STATE, GROUP_W), lambda g, c: (rev(c), g, 0, 0)),
                  vec, vec, vec,
                  pl.BlockSpec((CHUNK, CHUNK), lambda g, c: (0, 0))] + comm.in_specs,
        out_specs=[blk, xblk, blk, vec, vec, vec] + comm.out_specs,
        out_shape=[
            jax.ShapeDtypeStruct((lp, NP), BF16),
            jax.ShapeDtypeStruct((lp, D_CONV), F32),
            jax.ShapeDtypeStruct((lp, D_SSD), F32),
            jax.ShapeDtypeStruct((1, D_SSD), F32),
            jax.ShapeDtypeStruct((1, D_SSD), F32),
            jax.ShapeDtypeStruct((1, D_SSD), F32),
        ] + comm.out_shape,
        scratch_shapes=[pltpu.VMEM((gps, D_STATE, GROUP_W), F32)] + comm.scratch,
        compiler_params=_params(("arbitrary", "arbitrary"), side_effects=True),
    )(dmix, ytot, proj, xbc, dt_rep, hprev, a_rep, dsk_rep, wn, tri, *comm.operands)
    return res


def _stack_heads(t, h):
    return jnp.concatenate([t[:, (REP * h + r) * HEAD_DIM:(REP * h + r + 1) * HEAD_DIM] for r in range(REP)], axis=0)


def _band(t2, t1, t0, h):
    sl = slice(h * HEAD_DIM, (h + 1) * HEAD_DIM)
    return jnp.concatenate([t2[:, sl], t1[:, sl], t0[:, sl]], axis=0)


def _attn_probs(s, sink, qc):
    s = s * (HEAD_DIM ** -0.5)
    key_abs = (qc - WINDOW_CHUNKS) * CHUNK + lax.broadcasted_iota(jnp.int32, s.shape, 1)
    s = jnp.where(key_abs >= PAD_LEAD, s, -jnp.inf)
    m = jnp.maximum(jnp.max(s, axis=-1, keepdims=True), sink)
    p = jnp.exp(s - m)
    ps = jnp.exp(sink - m)
    denom = jnp.sum(p, axis=-1, keepdims=True) + ps
    return p / denom, ps / denom


ATT_QC_FWD = 6
ATT_QC_BWD = 2


def _kv_specs(width, newest_chunk_of, kc):
    return [pl.BlockSpec((CHUNK, width), functools.partial(lambda j, p: (jnp.maximum(newest_chunk_of(p) - j, 0), 0), j))
            for j in range(kc - 1, -1, -1)]


def _attn_fwd(qr, kr, vb, proj, sink_stack, mix):
    lp = qr.shape[0]
    nc = lp // CHUNK
    qn = _tile(nc, (ATT_QC_FWD, 2, 1))
    kn = WINDOW_CHUNKS + qn
    qrows = qn * CHUNK

    def body(q_ref, *rest):
        k_refs, v_refs = rest[:kn], rest[kn:2 * kn]
        g_ref, sink_ref, _, att_ref, mix_ref = rest[2 * kn:]
        p = pl.program_id(0)
        q = q_ref[...]
        ks = [r[...] for r in k_refs]
        vs = [r[...] for r in v_refs]
        units = [(u, h) for u in range(qn) for h in range(KV_HEADS)]
        s = [_dot(_stack_heads(q[u * CHUNK:(u + 1) * CHUNK, :], h), _band(*ks[u:u + 3], h), NT) for u, h in units]
        vbh = [_band(*vs[u:u + 3], h) for u, h in units]
        pn = [_attn_probs(s[i], sink_ref[h], qn * p + u)[0].astype(BF16) for i, (u, h) in enumerate(units)]
        o = [_dot(pn[i], vbh[i]) for i in range(len(units))]
        att = jnp.concatenate(
            [jnp.concatenate([o[u * KV_HEADS + h][r * CHUNK:(r + 1) * CHUNK, :] for h in range(KV_HEADS) for r in range(REP)],
                             axis=1) for u in range(qn)], axis=0)
        att_ref[...] = att
        g = g_ref[...]
        mix_ref[...] = (att * (g * _sigmoid(g))).astype(BF16)

    newest = lambda p: qn * p + qn - 1
    return pl.pallas_call(
        body, name="attn_fwd", grid=(nc // qn,),
        in_specs=[pl.BlockSpec((qrows, D_ATT), lambda p: (p, 0))] + _kv_specs(D_KV, newest, kn) + _kv_specs(D_KV, newest, kn) + [
            pl.BlockSpec((qrows, D_ATT), lambda p: (p, OFF_G // D_ATT)),
            pl.BlockSpec((KV_HEADS, REP * CHUNK, 1), lambda p: (0, 0, 0)),
            ANY,
        ],
        out_specs=[pl.BlockSpec((qrows, D_ATT), lambda p: (p, 0)),
                   pl.BlockSpec((qrows, D_ATT), lambda p: (p, D_SSD // D_ATT))],
        out_shape=[jax.ShapeDtypeStruct((lp, D_ATT), F32), jax.ShapeDtypeStruct((lp, D_MIX), BF16)],
        input_output_aliases={2 * kn + 3: 1},
        compiler_params=_params(("parallel",)),
    )(qr, *([kr] * kn), *([vb] * kn), proj, sink_stack, mix)


def _attn_bwd(qr, kr, vb, att, proj, dmix, sink_stack):
    lp = qr.shape[0]
    nc = lp // CHUNK
    qn = ATT_QC_BWD
    kn = WINDOW_CHUNKS + qn
    assert nc % qn == 0 and WINDOW_CHUNKS % qn == 0
    steps = nc // qn
    qrows = qn * CHUNK
    wrows = kn * CHUNK

    def body(q_ref, *rest):
        k_refs, v_refs = rest[:kn], rest[kn:2 * kn]
        (att_ref, g_ref, do_ref, sink_ref, dq_ref, dk_ref, dv_ref, dg_ref, dsink_ref, dk_acc, dv_acc) = rest[2 * kn:]
        step = pl.program_id(0)

        @pl.when(step == 0)
        def _():
            dk_acc[...] = jnp.zeros_like(dk_acc)
            dv_acc[...] = jnp.zeros_like(dv_acc)
            dsink_ref[...] = jnp.zeros_like(dsink_ref)

        @pl.when(step < steps)
        def _():
            q = q_ref[...]
            ks = [r[...] for r in k_refs]
            vs = [r[...] for r in v_refs]
            att = att_ref[...]
            g = g_ref[...]
            dog = do_ref[...]
            sg = _sigmoid(g)
            dg_ref[...] = dog * att * (sg * (1.0 + g * (1.0 - sg)))
            do = dog * (g * sg)
            units = [(u, h) for u in range(qn) for h in range(KV_HEADS)]
            n = range(len(units))
            rows = [slice(u * CHUNK, (u + 1) * CHUNK) for u in range(qn)]
            qs = [_stack_heads(q[rows[u], :], h) for u, h in units]
            kb = [_band(*ks[u:u + 3], h) for u, h in units]
            vbh = [_band(*vs[u:u + 3], h) for u, h in units]
            dos = [_stack_heads(do[rows[u], :], h) for u, h in units]
            dosb = [dos[i].astype(BF16) for i in n]
            s = [_dot(qs[i], kb[i], NT) for i in n]
            dp = [_dot(dosb[i], vbh[i], NT) for i in n]
            ds, pnb = [], []
            for i, (u, h) in enumerate(units):
                pn, psink = _attn_probs(s[i], sink_ref[h], qn * step + u)
                delta = jnp.sum(dos[i] * _stack_heads(att[rows[u], :], h), axis=-1, keepdims=True)
                ds.append((pn * (dp[i] - delta)).astype(BF16))
                pnb.append(pn.astype(BF16))
                dsink_ref[h] += -(psink * delta)
            dqs = [_dot(ds[i], kb[i]) for i in n]
            dks = [_dot(ds[i], qs[i], TN) for i in n]
            dvs = [_dot(pnb[i], dosb[i], TN) for i in n]
            dq_ref[...] = jnp.concatenate(
                [jnp.concatenate([dqs[u * KV_HEADS + h][r * CHUNK:(r + 1) * CHUNK, :]
                                  for h in range(KV_HEADS) for r in range(REP)], axis=1) for u in range(qn)],
                axis=0) * (HEAD_DIM ** -0.5)
            for u in range(qn):
                band = slice(u * CHUNK, u * CHUNK + BAND)
                dk_acc[band, :] += jnp.concatenate(dks[u * KV_HEADS:(u + 1) * KV_HEADS], axis=1) * (HEAD_DIM ** -0.5)
                dv_acc[band, :] += jnp.concatenate(dvs[u * KV_HEADS:(u + 1) * KV_HEADS], axis=1)

        dk_ref[...] = dk_acc[0:qrows, :]
        dv_ref[...] = dv_acc[0:qrows, :]
        for acc in (dk_acc, dv_acc):
            rest_rows = acc[qrows:wrows, :]
            acc[0:wrows - qrows, :] = rest_rows
            acc[wrows - qrows:wrows, :] = jnp.zeros((qrows, D_KV), F32)

    qp = lambda p: jnp.minimum(p, steps - 1)
    newest = lambda p: qn * qp(p) + qn - 1
    qblk = pl.BlockSpec((qrows, D_ATT), lambda p: (qp(p), 0))
    oldest = pl.BlockSpec((qrows, D_KV), lambda p: (jnp.maximum(p - 1, 0), 0))
    return pl.pallas_call(
        body, name="attn_bwd", grid=(steps + 1,),
        in_specs=[qblk] + _kv_specs(D_KV, newest, kn) + _kv_specs(D_KV, newest, kn) + [
            qblk,
            pl.BlockSpec((qrows, D_ATT), lambda p: (qp(p), OFF_G // D_ATT)),
            pl.BlockSpec((qrows, D_ATT), lambda p: (qp(p), D_SSD // D_ATT)),
            pl.BlockSpec((KV_HEADS, REP * CHUNK, 1), lambda p: (0, 0, 0)),
        ],
        out_specs=[qblk, oldest, oldest, qblk, pl.BlockSpec((KV_HEADS, REP * CHUNK, 1), lambda p: (0, 0, 0))],
        out_shape=[
            jax.ShapeDtypeStruct((lp, D_ATT), F32),
            jax.ShapeDtypeStruct((lp, D_KV), F32),
            jax.ShapeDtypeStruct((lp, D_KV), F32),
            jax.ShapeDtypeStruct((lp, D_ATT), F32),
            jax.ShapeDtypeStruct((KV_HEADS, REP * CHUNK, 1), F32),
        ],
        scratch_shapes=[pltpu.VMEM((wrows, D_KV), F32), pltpu.VMEM((wrows, D_KV), F32)],
        compiler_params=_params(("arbitrary",)),
    )(qr, *([kr] * kn), *([vb] * kn), att, proj, dmix, sink_stack)


def _post_loss(out, x2d, target, w):
    lp = out.shape[0]
    nc = lp // CHUNK
    nx = x2d.shape[0] // CHUNK

    k = _tile(nc, (ROW_K, 3, 2, 1))

    def body(o_ref, *rest):
        x_refs, t_refs = rest[:k], rest[k:2 * k]
        w_ref, do_ref, dy_ref, gw_ref, loss_ref = rest[2 * k:]
        i = pl.program_id(0)

        @pl.when(i == 0)
        def _():
            gw_ref[...] = jnp.zeros_like(gw_ref)
            loss_ref[...] = jnp.zeros_like(loss_ref)

        o = o_ref[...]
        w = w_ref[...]
        rstd = lax.rsqrt(jnp.mean(o * o, axis=-1, keepdims=True) + EPS)
        xhat = o * rstd
        x = jnp.concatenate([r[...] for r in x_refs], axis=0)
        t = jnp.concatenate([r[...] for r in t_refs], axis=0)
        chunk = _chunk_of_row(i, k)
        err = (x + xhat * w - t) * ((chunk > 0) & (chunk <= nx)).astype(F32)
        loss_ref[...] += 0.5 * jnp.sum(jnp.mean(err * err, axis=-1, keepdims=True), axis=0, keepdims=True)
        dy = err * (1.0 / D_MODEL)
        dy_ref[...] = dy
        gw_ref[...] += jnp.sum(dy * xhat, axis=0, keepdims=True)
        dxhat = dy * w
        do_ref[...] = (rstd * (dxhat - xhat * jnp.mean(dxhat * xhat, axis=-1, keepdims=True))).astype(BF16)

    row = pl.BlockSpec((k * CHUNK, D_MODEL), lambda i: (i, 0))
    return pl.pallas_call(
        body, name="post_loss", grid=(nc // k,),
        in_specs=[row] + _x_specs(nx, k) + _x_specs(nx, k) + [pl.BlockSpec((1, D_MODEL), lambda i: (0, 0))],
        out_specs=[row, row, pl.BlockSpec((1, D_MODEL), lambda i: (0, 0)), pl.BlockSpec((1, 128), lambda i: (0, 0))],
        out_shape=[
            jax.ShapeDtypeStruct((lp, D_MODEL), BF16),
            jax.ShapeDtypeStruct((lp, D_MODEL), F32),
            jax.ShapeDtypeStruct((1, D_MODEL), F32),
            jax.ShapeDtypeStruct((1, 128), F32),
        ],
        compiler_params=_params(("arbitrary",)),
    )(out, *([x2d] * k), *([target] * k), w)


def _prenorm_bwd(dhn, x2d, h0, dy, w):
    nx = x2d.shape[0] // CHUNK
    k = _tile(nx, (X_K, 4, 2, 1))
    rows = k * CHUNK

    def backward(h, dhn, w):
        rstd = lax.rsqrt(jnp.mean(h * h, axis=-1, keepdims=True) + EPS)
        xhat = h * rstd
        dxhat = dhn * w
        dh = rstd * (dxhat - xhat * jnp.mean(dxhat * xhat, axis=-1, keepdims=True))
        return dh, jnp.sum(dhn * xhat, axis=0, keepdims=True)

    def body(*refs):
        dhn_refs, dy_refs = refs[:k], refs[k:2 * k]
        x_ref, w_ref, gx_ref, gw_ref = refs[2 * k:]

        @pl.when(pl.program_id(0) == 0)
        def _():
            gw_ref[...] = jnp.zeros_like(gw_ref)

        dh, gw = backward(x_ref[...], jnp.concatenate([r[...] for r in dhn_refs], axis=0), w_ref[...])
        gx_ref[...] = dh + jnp.concatenate([r[...] for r in dy_refs], axis=0)
        gw_ref[...] += gw

    def body_meta(dhn_ref, h0_ref, w_ref, d0_ref, gw_ref):
        d0_ref[...], gw_ref[...] = backward(h0_ref[...], dhn_ref[...], w_ref[...])

    chunk_specs = [pl.BlockSpec((CHUNK, D_MODEL), functools.partial(lambda u, i: (k * i + 1 + u, 0), u)) for u in range(k)]
    first = pl.BlockSpec((CHUNK, D_MODEL), lambda i: (0, 0))
    vec = pl.BlockSpec((1, D_MODEL), lambda i: (0, 0))
    wide = pl.BlockSpec((rows, D_MODEL), lambda i: (i, 0))
    gx, gw_x = pl.pallas_call(
        body, name="prenorm_bwd", grid=(nx // k,),
        in_specs=chunk_specs + chunk_specs + [wide, vec],
        out_specs=[wide, vec],
        out_shape=[jax.ShapeDtypeStruct((nx * CHUNK, D_MODEL), F32), jax.ShapeDtypeStruct((1, D_MODEL), F32)],
        compiler_params=_params(("arbitrary",)),
    )(*([dhn] * k), *([dy] * k), x2d, w)
    d0, gw_0 = pl.pallas_call(
        body_meta, name="prenorm_bwd_meta", grid=(1,),
        in_specs=[first, first, vec], out_specs=[first, vec],
        out_shape=[jax.ShapeDtypeStruct((CHUNK, D_MODEL), F32), jax.ShapeDtypeStruct((1, D_MODEL), F32)],
        compiler_params=_params(("arbitrary",)),
    )(dhn, h0, w)
    return gx, d0, gw_x + gw_0


def _adamw(slabs, w, m, v, name):
    rows, cols = w.shape
    tr = _tile(rows, (256, 128, 64, 16, 8))
    c1 = 1.0 - ADAM_B1 ** ADAM_STEP
    c2 = 1.0 - ADAM_B2 ** ADAM_STEP

    def body(s_ref, w_ref, m_ref, v_ref, g_ref, d_ref, mo_ref, vo_ref):
        g = s_ref[0].astype(F32)
        for k in range(1, slabs.shape[0]):
            g = g + s_ref[k].astype(F32)
        w = w_ref[...]
        m = ADAM_B1 * m_ref[...] + (1.0 - ADAM_B1) * g
        v = ADAM_B2 * v_ref[...] + (1.0 - ADAM_B2) * (g * g)
        g_ref[...] = g
        mo_ref[...] = m
        vo_ref[...] = v
        d_ref[...] = -ADAM_LR * ((m / c1) / (jnp.sqrt(v / c2) + ADAM_EPS) + ADAM_WD * w)

    blk = pl.BlockSpec((tr, cols), lambda i: (i, 0))
    shape = jax.ShapeDtypeStruct((rows, cols), F32)
    return pl.pallas_call(
        body, name=name, grid=(rows // tr,),
        in_specs=[pl.BlockSpec((slabs.shape[0], tr, cols), lambda i: (0, i, 0)), blk, blk, blk],
        out_specs=[blk, blk, blk, blk],
        out_shape=[shape, shape, shape, shape],
        compiler_params=_params(("parallel",)),
    )(slabs, w, m, v)


def _perm_xbc(a):
    lead = a.shape[:-1]
    xs = a[..., :D_SSD].reshape(lead + (N_GROUPS, GROUP_W))
    b = a[..., D_SSD:D_SSD + N_GROUPS * D_STATE].reshape(lead + (N_GROUPS, D_STATE))
    c = a[..., D_SSD + N_GROUPS * D_STATE:].reshape(lead + (N_GROUPS, D_STATE))
    return jnp.concatenate([xs, b, c], axis=-1).reshape(lead + (D_CONV,))


def _unperm_xbc(a):
    lead = a.shape[:-1]
    t = a.reshape(lead + (N_GROUPS, XBC_BLK))
    xs = t[..., :GROUP_W].reshape(lead + (D_SSD,))
    b = t[..., GROUP_W:GROUP_W + D_STATE].reshape(lead + (N_GROUPS * D_STATE,))
    c = t[..., GROUP_W + D_STATE:].reshape(lead + (N_GROUPS * D_STATE,))
    return jnp.concatenate([xs, b, c], axis=-1)


R_Z, R_XBC, R_DT, R_Q, R_K, R_V, R_G = 0, 2048, 6144, 6176, 7200, 7456, 7712


def _internal_of_reference():
    ref = np.arange(D_IN_PROJ)
    out = np.empty(D_IN_PROJ, np.int64)
    out[R_Z:R_XBC] = OFF_Z + ref[:D_SSD]
    xs = np.arange(D_SSD)
    out[R_XBC:R_XBC + D_SSD] = OFF_XBC + (xs // GROUP_W) * XBC_BLK + xs % GROUP_W
    bc = np.arange(N_GROUPS * D_STATE)
    out[R_XBC + D_SSD:R_XBC + D_SSD + N_GROUPS * D_STATE] = OFF_XBC + (bc // D_STATE) * XBC_BLK + GROUP_W + bc % D_STATE
    out[R_XBC + D_SSD + N_GROUPS * D_STATE:R_DT] = OFF_XBC + (bc // D_STATE) * XBC_BLK + GROUP_W + D_STATE + bc % D_STATE
    out[R_DT:R_Q] = OFF_DT + np.arange(SSD_HEADS)
    out[R_Q:R_K] = OFF_Q + np.arange(D_ATT)
    out[R_K:R_V] = OFF_K + np.arange(D_KV)
    out[R_V:R_G] = OFF_V + np.arange(D_KV)
    out[R_G:] = OFF_G + np.arange(D_ATT)
    return out


def _runs(src, dst_break):
    runs, lo = [], 0
    for i in range(1, len(src) + 1):
        if i == len(src) or src[i] != src[i - 1] + 1 or dst_break[i] != dst_break[i - 1]:
            runs.append((lo, i))
            lo = i
    return runs


RELAYOUT_ROWS = 256


def _lane_window(ref, lead, c0, n):
    a0 = c0 // 128 * 128
    a1 = min(-(-(c0 + n) // 128) * 128, ref.shape[-1])
    return ref[lead + (slice(None), slice(a0, a1))][:, c0 - a0:c0 - a0 + n]


def _gather_w_in(w_shard, small, x2d, norm_w):
    int_of_ref = _internal_of_reference()
    ref_of_int = np.full(NP, -1, np.int64)
    ref_of_int[int_of_ref] = np.arange(D_IN_PROJ)
    shard = np.where(ref_of_int >= 0, ref_of_int // SHARD_IN, -1)
    src = np.where(ref_of_int >= 0, ref_of_int, -10 - 2 * np.arange(NP))
    plan, zeros = [], 0
    for lo, hi in _runs(src, shard):
        if ref_of_int[lo] < 0:
            zeros += hi - lo
            continue
        if zeros:
            plan.append((None, 0, zeros))
            zeros = 0
        plan.append((int(ref_of_int[lo] // SHARD_IN), int(ref_of_int[lo] % SHARD_IN), hi - lo))
    if zeros:
        plan.append((None, 0, zeros))
    tr = RELAYOUT_ROWS
    steps = D_MODEL // tr
    side = _GatherTwoLevel(small)
    ns = side.n
    nx = x2d.shape[0] // CHUNK
    nc = nx + 2
    k = _tile(nc, (11, ROW_K, 3, 2, 1))
    pn = nc // k
    n = max(pn, steps)
    tile_of = lambda i: jnp.where(i < pn - 1, i + 1, 0)

    def body(w_ref, *rest):
        s_in, x_refs, nw_ref = rest[:ns], rest[ns:ns + k], rest[ns + k]
        o_ref, land_ref, s_out, hn_ref = rest[ns + k + 1], rest[ns + k + 2], rest[ns + k + 3:2 * ns + k + 3], rest[2 * ns + k + 3]
        tile_buf, send_sems, recv_sems, load_sems, local_sem, meta_buf, meta_sem = rest[2 * ns + k + 4:2 * ns + k + 11]
        side_sems = rest[2 * ns + k + 11:]
        i = pl.program_id(0)
        x, y, c = lax.axis_index("x"), lax.axis_index("y"), lax.axis_index("c")
        me, sibling = (x, y, c), (x, y, 1 - c)
        chips = [(1 - x, y), (x, 1 - y), (1 - x, 1 - y)]

        def prenorm_tile(t, head=None):
            h = jnp.concatenate([r[...] for r in x_refs], axis=0)
            if head is not None:
                h = jnp.concatenate([head, h[CHUNK:, :]], axis=0)
            h = h * (_chunk_of_row(t, k) <= nx).astype(F32)
            rstd = lax.rsqrt(jnp.mean(h * h, axis=-1, keepdims=True) + EPS)
            hn_ref[...] = (h * rstd * nw_ref[...]).astype(BF16)

        if pn > 1:
            @pl.when(i < pn - 1)
            def _():
                prenorm_tile(i + 1)

        def rows(t):
            return pl.ds(t * tr, tr)

        def tile(place, t):
            return land_ref.at[4 * place[0] + 2 * place[1] + place[2], rows(t), :]

        def copy(t, k, block, to, src=None):
            return pltpu.make_async_remote_copy(
                src_ref=tile(block, t) if src is None else src, dst_ref=tile(block, t),
                send_sem=send_sems.at[t, k], recv_sem=recv_sems.at[t, k], device_id=to, device_id_type=pl.DeviceIdType.MESH)

        def own_sends(t):
            mine = w_ref.at[rows(t), :]
            return [copy(t, 0, me, sibling, src=mine)] + [copy(t, 1 + j, me, (*chip, c), src=mine)
                                                          for j, chip in enumerate(chips)]

        own = pltpu.make_async_copy(w_ref, land_ref.at[4 * x + 2 * y + c], local_sem)

        @pl.when(i < steps)
        def _():
            @pl.when(i == 0)
            def _():
                side.start(s_in, s_out, side_sems)
                own.start()
                for t in range(steps):
                    for cp in own_sends(t):
                        cp.start()
                own.wait()

            forwards = []
            for j, chip in enumerate(chips):
                copy(i, 1 + j, (*chip, c), me).wait_recv()
                forwards.append(copy(i, 4 + j, (*chip, c), sibling))
                forwards[j].start()
            copy(i, 0, sibling, me).wait_recv()
            for j, chip in enumerate(chips):
                copy(i, 4 + j, (*chip, 1 - c), me).wait_recv()
            loads = [pltpu.make_async_copy(land_ref.at[s, rows(i), :], tile_buf.at[s], load_sems.at[s]) for s in range(N_DEV)]
            for cp in loads:
                cp.start()
            for cp in loads:
                cp.wait()
            o_ref[...] = jnp.concatenate(
                [jnp.zeros((tr, w), o_ref.dtype) if s is None else _lane_window(tile_buf, (s,), c0, w) for s, c0, w in plan],
                axis=1)
            for cp in forwards:
                cp.wait_send()

            @pl.when(i == steps - 1)
            def _():
                for t in range(steps):
                    for cp in own_sends(t):
                        cp.wait_send()
                side.wait(s_in, s_out, side_sems)

        @pl.when(i == n - 1)
        def _():
            meta = pltpu.make_async_copy(s_out[ns - 1], meta_buf, meta_sem)
            meta.start()
            meta.wait()
            tokens = jnp.concatenate([meta_buf[s] for s in range(N_DEV)], axis=1)
            prenorm_tile(0, jnp.concatenate([jnp.zeros((PAD_LEAD, D_MODEL), F32), tokens], axis=0))

    last = steps - 1
    res = pl.pallas_call(
        body, name="gather_w_in", grid=(n,),
        in_specs=[ANY] + side.in_specs + _x_specs(nx, k, tile_of) + [pl.BlockSpec((1, D_MODEL), lambda i: (0, 0))],
        out_specs=[pl.BlockSpec((tr, NP), lambda i: (jnp.minimum(i, last), 0)), ANY] + side.out_specs + [
            pl.BlockSpec((k * CHUNK, D_MODEL), lambda i: (tile_of(i), 0))],
        out_shape=[jax.ShapeDtypeStruct((D_MODEL, NP), w_shard.dtype),
                   jax.ShapeDtypeStruct((N_DEV, D_MODEL, SHARD_IN), w_shard.dtype)] + side.out_shape + [
            jax.ShapeDtypeStruct((nc * CHUNK, D_MODEL), BF16)],
        scratch_shapes=[pltpu.VMEM((N_DEV, tr, SHARD_IN), w_shard.dtype),
                        pltpu.SemaphoreType.DMA((steps, N_DEV - 1)), pltpu.SemaphoreType.DMA((steps, N_DEV - 1)),
                        pltpu.SemaphoreType.DMA((N_DEV,)), pltpu.SemaphoreType.DMA(()),
                        pltpu.VMEM((N_DEV,) + tuple(small[ns - 1].shape), F32),
                        pltpu.SemaphoreType.DMA(())] + side.scratch,
        compiler_params=_params(("arbitrary",), side_effects=True),
    )(w_shard, *side.operands, *([x2d] * k), norm_w)
    return [res[0]] + list(res[2:])


def _w_in_chip_slabs(dw):
    int_of_ref = _internal_of_reference()
    plan = []
    for s in range(N_DEV):
        cols = int_of_ref[s * SHARD_IN:(s + 1) * SHARD_IN]
        plan.append([(int(cols[lo]), hi - lo) for lo, hi in _runs(cols, np.zeros_like(cols))])
    tr = RELAYOUT_ROWS
    steps = D_MODEL // tr
    pairs = N_DEV // 2

    def body(dw_ref, o_ref, land_ref, send_buf, land_buf, send_sems, recv_sems, load_sems):
        i = pl.program_id(0)
        x, y, c = lax.axis_index("x"), lax.axis_index("y"), lax.axis_index("c")
        slot = i % 2

        def sends(step, sl):
            return [pltpu.make_async_remote_copy(
                src_ref=send_buf.at[sl, 2 * q + (1 - c)], dst_ref=land_ref.at[q, pl.ds(step * tr, tr), :],
                send_sem=send_sems.at[sl, q], recv_sem=recv_sems.at[step, q],
                device_id=(x, y, 1 - c), device_id_type=pl.DeviceIdType.MESH) for q in range(pairs)]

        @pl.when(i < steps)
        def _():
            @pl.when(i >= 2)
            def _():
                for cp in sends(i - 2, slot):
                    cp.wait_send()

            for s in range(N_DEV):
                send_buf[slot, s] = jnp.concatenate([_lane_window(dw_ref, (), c0, n) for c0, n in plan[s]], axis=1)
            for cp in sends(i, slot):
                cp.start()

        @pl.when(i >= 1)
        def _():
            loads = []
            for q, cp in enumerate(sends(i - 1, 1 - slot)):
                cp.wait_recv()
                loads.append(pltpu.make_async_copy(land_ref.at[q, pl.ds((i - 1) * tr, tr), :], land_buf.at[q], load_sems.at[q]))
                loads[q].start()
            for q in range(pairs):
                loads[q].wait()
                o_ref[q] = (send_buf[1 - slot, 2 * q + c].astype(F32) + land_buf[q].astype(F32)).astype(o_ref.dtype)

        @pl.when(i == steps)
        def _():
            for cp in sends(steps - 2, steps % 2) + sends(steps - 1, 1 - steps % 2):
                cp.wait_send()

    last = steps - 1
    return pl.pallas_call(
        body, name="dw_in_relayout", grid=(steps + 1,),
        in_specs=[pl.BlockSpec((tr, NP), lambda i: (jnp.minimum(i, last), 0))],
        out_specs=[pl.BlockSpec((pairs, tr, SHARD_IN), lambda i: (0, jnp.maximum(i - 1, 0), 0)), ANY],
        out_shape=[jax.ShapeDtypeStruct((pairs, D_MODEL, SHARD_IN), dw.dtype),
                   jax.ShapeDtypeStruct((pairs, D_MODEL, SHARD_IN), dw.dtype)],
        scratch_shapes=[pltpu.VMEM((2, N_DEV, tr, SHARD_IN), dw.dtype), pltpu.VMEM((pairs, tr, SHARD_IN), dw.dtype),
                        pltpu.SemaphoreType.DMA((2, pairs)), pltpu.SemaphoreType.DMA((steps, pairs)),
                        pltpu.SemaphoreType.DMA((pairs,))],
        compiler_params=_params(("arbitrary",), side_effects=True),
    )(dw)[0]


def _rep_heads(a):
    return jnp.repeat(a, HEAD_DIM, axis=1)


SMALL = (("norm_pre_w", 2048), ("conv_b", 4096), ("dt_bias", 32), ("a_log", 32), ("d_skip", 32),
         ("ssd_norm_w", 2048), ("attn_sinks", 16), ("norm_post_w", 2048))
SMALL_USED = sum(size for _, size in SMALL)
SMALL_LEN = 10368


def _pack_small(d, loss=None):
    parts = [d[name].reshape(1, size) for name, size in SMALL]
    tail = jnp.zeros((1, SMALL_LEN - SMALL_USED), F32)
    if loss is not None:
        tail = tail.at[0, 0].set(loss)
    return jnp.concatenate(parts + [tail], axis=1)


def _unpack_small(vec):
    out, off = {}, 0
    for name, size in SMALL:
        out[name] = vec[:, off:off + size]
        off += size
    return out


def kernel(x, meta_tokens, norm_pre_w, w_in, conv_w, conv_b, dt_bias, a_log, d_skip, ssd_norm_w, attn_sinks, w_out, norm_post_w, loss_target, m_meta_tokens, m_norm_pre_w, m_w_in, m_conv_w, m_conv_b, m_dt_bias, m_a_log, m_d_skip, m_ssd_norm_w, m_attn_sinks, m_w_out, m_norm_post_w, v_meta_tokens, v_norm_pre_w, v_w_in, v_conv_w, v_conv_b, v_dt_bias, v_a_log, v_d_skip, v_ssd_norm_w, v_attn_sinks, v_w_out, v_norm_post_w):
    seq = x.shape[1]
    lp = seq + 2 * CHUNK
    x2d = x[0]

    w_all, conv_w_g, meta_g, hn = _gather_w_in(
        w_in[0].astype(BF16), [conv_w[0], meta_tokens], x2d, norm_pre_w)
    conv_w_full = _perm_xbc(jnp.transpose(conv_w_g, (1, 0, 2)).reshape(CONV_WIDTH, D_CONV))
    conv_b_int = _perm_xbc(conv_b)
    meta_full = jnp.transpose(meta_g, (1, 0, 2)).reshape(N_META, D_MODEL)
    h0 = jnp.concatenate([jnp.zeros((PAD_LEAD, D_MODEL), F32), meta_full], axis=0)

    pos = (jnp.arange(lp) - PAD_LEAD).astype(F32)
    half = HEAD_DIM // 2
    inv = ROPE_THETA ** (-jnp.arange(half, dtype=F32) / half)
    ang = pos[:, None] * inv[None, :]
    cos_t = jnp.tile(jnp.cos(ang), (1, 4))
    sin_t = jnp.tile(jnp.concatenate([-jnp.sin(ang), jnp.sin(ang)], axis=1), (1, 2))
    head_of_col = np.arange(D_SSD) // HEAD_DIM
    expand = jnp.asarray((np.arange(128)[:, None] == head_of_col[None, :]).astype(np.float32))
    reduce_t = jnp.asarray((head_of_col[:, None] == np.arange(128)[None, :]).astype(np.float32))
    tri = jnp.asarray(np.tril(np.ones((CHUNK, CHUNK), np.float32)))
    a_rep = _rep_heads(-jnp.exp(a_log))
    dsk_rep = _rep_heads(d_skip)
    dt_bias_pad = jnp.pad(dt_bias, ((0, 0), (0, 128 - SSD_HEADS)))
    sink_stack = jnp.repeat(attn_sinks.reshape(KV_HEADS, REP), CHUNK, axis=1).reshape(KV_HEADS, REP * CHUNK, 1)

    tm = _tile(lp, (1056, 704, 128, 64))
    proj, w_out_g = _matmul(hn, w_all, tm=tm, tn=1536, tk=D_MODEL, out_dtype=F32, name="in_proj",
                            comm=_Comm([(w_out[0].astype(BF16), "gather")]))
    w_out_full = w_out_g.reshape(D_MIX, D_MODEL)
    qr, kr, vb, dt_rep = _act_fwd(proj, cos_t, sin_t, expand, dt_bias_pad)
    mix, ytot, hprev, xbc, dsilu = _ssd_fwd(proj, conv_w_full, conv_b_int, dt_rep, a_rep, dsk_rep, ssd_norm_w, tri)
    att, mix = _attn_fwd(qr, kr, vb, proj, sink_stack, mix)
    out = _matmul(mix, w_out_full, tm=tm, tn=1024, tk=D_MIX, out_dtype=F32, name="out_proj")
    dout, dy, g_norm_post, loss_part = _post_loss(out, x2d, loss_target[0], norm_post_w)

    dmix = _matmul(dout, w_out_full, trans_b=True, tm=tm, tn=1024, tk=D_MODEL, out_dtype=F32, name="dmix")
    dw_out = _matmul(mix, dout, trans_a=True, tm=512, tn=1024, tk=lp, out_dtype=BF16, name="dw_out")
    dqr, dkr, dv, dg, dsink_rows = _attn_bwd(qr, kr, vb, att, proj, dmix, sink_stack)
    dproj, dxbc, ddt_part, dd_part, da_part, g_ssd_norm, g_out = _ssd_bwd(
        dmix, ytot, proj, xbc, dt_rep, hprev, a_rep, dsk_rep, ssd_norm_w, tri,
        _Comm([(dw_out.reshape(N_DEV, D_MIX // N_DEV, D_MODEL), "scatter")]))
    dproj, dconv_w_int, dconv_b_int = _conv_bwd(dxbc, dsilu, proj, conv_w_full, dproj)
    dproj, ddt_bias = _act_bwd(dqr, dkr, dv, dg, ddt_part, cos_t, sin_t, reduce_t, dproj)
    dw_all = _matmul(hn, dproj, trans_a=True, tm=512, tn=1024, tk=lp, out_dtype=BF16, name="dw_in")
    dw_chip = _w_in_chip_slabs(dw_all)
    dhn, g_in = _matmul(dproj, w_all, trans_b=True, tm=tm, tn=1024, tk=1536, out_dtype=F32, name="dhn",
                        comm=_Comm([(dw_chip, "scatter")], scope="chips"))
    grad_x, dh0, g_norm_pre = _prenorm_bwd(dhn, x2d, h0, dy, norm_pre_w)

    dmeta = dh0[PAD_LEAD:, :]
    dconv_w_ref = _unperm_xbc(dconv_w_int)
    heads = lambda part: part.reshape(SSD_HEADS, HEAD_DIM).sum(axis=1).reshape(1, SSD_HEADS)
    small_local = _pack_small({
        "norm_pre_w": g_norm_pre, "conv_b": _unperm_xbc(dconv_b_int), "dt_bias": ddt_bias[:, :SSD_HEADS],
        "a_log": heads(da_part) * (-jnp.exp(a_log)), "d_skip": heads(dd_part), "ssd_norm_w": g_ssd_norm,
        "attn_sinks": dsink_rows.reshape(Q_HEADS, CHUNK).sum(axis=1).reshape(1, Q_HEADS),
        "norm_post_w": g_norm_post}, loss=loss_part[0, 0])
    g_conv, g_meta, g_small = _exchange(
        [(jnp.transpose(dconv_w_ref.reshape(CONV_WIDTH, N_DEV, D_CONV // N_DEV), (1, 0, 2)), "scatter"),
         (jnp.transpose(dmeta.reshape(N_META, N_DEV, D_MODEL // N_DEV), (1, 0, 2)), "scatter"),
         (small_local, "gather")], "exchange_small")

    res = {}
    res["w_in"] = [o[None] for o in _adamw(g_in, w_in[0], m_w_in[0], v_w_in[0], "adamw_w_in")]
    res["w_out"] = [o[None] for o in _adamw(g_out, w_out[0], m_w_out[0], v_w_out[0], "adamw_w_out")]
    res["conv_w"] = [o[None] for o in _adamw(g_conv, conv_w[0], m_conv_w[0], v_conv_w[0], "adamw_conv_w")]
    res["meta_tokens"] = _adamw(g_meta, meta_tokens, m_meta_tokens, v_meta_tokens, "adamw_meta")
    given = dict(norm_pre_w=(norm_pre_w, m_norm_pre_w, v_norm_pre_w), conv_b=(conv_b, m_conv_b, v_conv_b),
                 dt_bias=(dt_bias, m_dt_bias, v_dt_bias), a_log=(a_log, m_a_log, v_a_log),
                 d_skip=(d_skip, m_d_skip, v_d_skip), ssd_norm_w=(ssd_norm_w, m_ssd_norm_w, v_ssd_norm_w),
                 attn_sinks=(attn_sinks, m_attn_sinks, v_attn_sinks),
                 norm_post_w=(norm_post_w, m_norm_post_w, v_norm_post_w))
    packed = [_pack_small({k: t[j] for k, t in given.items()}) for j in range(3)]
    small_out = _adamw(g_small, packed[0], packed[1], packed[2], "adamw_small")
    small_res = [_unpack_small(r) for r in small_out]
    loss = small_out[0][0, SMALL_USED]

    order = ["meta_tokens", "norm_pre_w", "w_in", "conv_w", "conv_b", "dt_bias", "a_log", "d_skip", "ssd_norm_w",
             "attn_sinks", "w_out", "norm_post_w"]
    outs = []
    for j in range(4):
        for name in order:
            outs.append(res[name][j] if name in res else small_res[j][name])
    return (loss, grad_x[None], *outs)
```

```python
import functools

import numpy as np
import jax
import jax.numpy as jnp
from jax import lax
from jax.experimental import pallas as pl
from jax.experimental.pallas import tpu as pltpu

F32 = jnp.float32
BF16 = jnp.bfloat16
HIGHEST = lax.Precision.HIGHEST

N_DEV = 8
D_MODEL = 2048
CHUNK = 64
N_META = 16
PAD_LEAD = CHUNK - N_META
EPS = 1e-6
N_GROUPS = 8
HEADS_PER_GROUP = 4
HEAD_DIM = 64
GROUP_W = HEADS_PER_GROUP * HEAD_DIM
D_STATE = 128
D_SSD = 2048
D_CONV = 4096
SSD_HEADS = 32
CONV_WIDTH = 4
Q_HEADS = 16
KV_HEADS = 4
REP = 4
D_ATT = 1024
D_KV = 256
WINDOW_CHUNKS = 2
BAND = (WINDOW_CHUNKS + 1) * CHUNK
ROPE_THETA = 10000.0
D_MIX = D_SSD + D_ATT
D_IN_PROJ = 8736
SHARD_IN = D_IN_PROJ // N_DEV

OFF_Z, OFF_XBC, OFF_Q, OFF_G, OFF_K, OFF_V, OFF_DT = 0, 2048, 6144, 7168, 8192, 8448, 8704
NP = 9216
TAIL_W = NP - OFF_Q
XBC_BLK = 512
SSD_GPS = 8
SSD_CPS = 2

ADAM_LR, ADAM_B1, ADAM_B2, ADAM_EPS, ADAM_WD, ADAM_STEP = 0.001, 0.9, 0.999, 1e-08, 0.01, 10

VMEM_LIMIT = 48 * 1024 * 1024

NN = (((1,), (0,)), ((), ()))
NT = (((1,), (1,)), ((), ()))
TN = (((0,), (0,)), ((), ()))
ANY = pl.BlockSpec(memory_space=pl.ANY)


def _dot(a, b, dims=NN, precision=None):
    return lax.dot_general(a, b, dims, precision=precision, preferred_element_type=F32)


def _tile(n, prefs):
    for t in prefs:
        if n % t == 0:
            return t
    return n


def _params(sem, vmem=VMEM_LIMIT, side_effects=False):
    return pltpu.CompilerParams(dimension_semantics=sem, vmem_limit_bytes=vmem, has_side_effects=side_effects)


def _sigmoid(x):
    return 1.0 / (1.0 + jnp.exp(-x))


class _Comm:
    def __init__(self, items, scope="devices"):
        self.items = items
        self.scope = scope
        self.slabs = slabs = N_DEV if scope == "devices" else N_DEV // 2
        self.n = n = len(items)
        self.operands = [arr for arr, _ in items]
        self.in_specs = [ANY] * n
        self.out_specs = [ANY] * n
        self.out_shape = [jax.ShapeDtypeStruct((slabs,) + tuple(arr.shape) if kind == "gather" else tuple(arr.shape),
                                               arr.dtype) for arr, kind in items]
        self.scratch = [pltpu.SemaphoreType.DMA((n, slabs - 1)), pltpu.SemaphoreType.DMA((n, slabs - 1)),
                        pltpu.SemaphoreType.DMA((n,))]

    def _places(self):
        pos = (lax.axis_index("x"), lax.axis_index("y"), lax.axis_index("c"))
        if self.scope == "devices":
            index = lambda p: 4 * p[0] + 2 * p[1] + p[2]
            masks = range(1, N_DEV)
        else:
            index = lambda p: 2 * p[0] + p[1]
            masks = (2, 4, 6)
        peers = []
        for k in masks:
            p = tuple(1 - pos[b] if (k >> (2 - b)) & 1 else pos[b] for b in range(3))
            peers.append((p, index(p)))
        return index(pos), peers

    def _copies(self, ins, outs, sems, landed):
        send_sems, recv_sems, local_sems = sems
        me, peers = self._places()
        local, remote = [], []
        for a, (_, kind) in enumerate(self.items):
            own = ins[a] if kind == "gather" else ins[a].at[me]
            local.append(pltpu.make_async_copy(own, outs[a].at[me], local_sems.at[a]))
            for k, (p, pid) in enumerate(peers):
                remote.append(pltpu.make_async_remote_copy(
                    src_ref=ins[a] if kind == "gather" else ins[a].at[pid],
                    dst_ref=outs[a].at[pid if landed else me],
                    send_sem=send_sems.at[a, k], recv_sem=recv_sems.at[a, k],
                    device_id=p, device_id_type=pl.DeviceIdType.MESH))
        return local, remote

    def start(self, ins, outs, sems):
        local, remote = self._copies(ins, outs, sems, landed=False)
        for cp in local + remote:
            cp.start()

    def wait(self, ins, outs, sems):
        local, remote = self._copies(ins, outs, sems, landed=True)
        for cp in remote + local:
            cp.wait()


def _exchange(items, name):
    comm = _Comm(items)
    n = comm.n

    def body(*refs):
        ins, outs, sems = refs[:n], refs[n:2 * n], refs[2 * n:]
        comm.start(ins, outs, sems)
        comm.wait(ins, outs, sems)

    return pl.pallas_call(
        body, name=name, in_specs=comm.in_specs, out_specs=comm.out_specs, out_shape=comm.out_shape,
        scratch_shapes=comm.scratch, compiler_params=pltpu.CompilerParams(has_side_effects=True),
    )(*comm.operands)


class _GatherTwoLevel:
    def __init__(self, arrays):
        self.arrays = arrays
        self.n = n = len(arrays)
        self.operands = list(arrays)
        self.in_specs = [ANY] * n
        self.out_specs = [ANY] * n
        self.out_shape = [jax.ShapeDtypeStruct((N_DEV,) + tuple(a.shape), a.dtype) for a in arrays]
        self.scratch = [pltpu.SemaphoreType.DMA((n, N_DEV - 1)), pltpu.SemaphoreType.DMA((n, N_DEV - 1)),
                        pltpu.SemaphoreType.DMA((n,))]

    def _plan(self, ins, outs, sems):
        send_sems, recv_sems, local_sems = sems
        x, y, c = lax.axis_index("x"), lax.axis_index("y"), lax.axis_index("c")
        me, sibling = (x, y, c), (x, y, 1 - c)
        chips = [(1 - x, y), (x, 1 - y), (1 - x, 1 - y)]

        def slab(a, place):
            return outs[a].at[4 * place[0] + 2 * place[1] + place[2]]

        def copy(a, k, block, to, src=None):
            return pltpu.make_async_remote_copy(
                src_ref=slab(a, block) if src is None else src, dst_ref=slab(a, block),
                send_sem=send_sems.at[a, k], recv_sem=recv_sems.at[a, k],
                device_id=to, device_id_type=pl.DeviceIdType.MESH)

        own, mine = [], []
        for a in range(self.n):
            mine.append(pltpu.make_async_copy(ins[a], slab(a, me), local_sems.at[a]))
            own.append(copy(a, 0, me, sibling, src=ins[a]))
            own += [copy(a, 1 + j, me, (*chip, c), src=ins[a]) for j, chip in enumerate(chips)]
        return me, sibling, chips, c, copy, own, mine

    def start(self, ins, outs, sems):
        _, _, _, _, _, own, mine = self._plan(ins, outs, sems)
        for cp in mine + own:
            cp.start()

    def wait(self, ins, outs, sems):
        me, sibling, chips, c, copy, own, mine = self._plan(ins, outs, sems)
        forwards = []
        for j, chip in enumerate(chips):
            for a in range(self.n):
                copy(a, 1 + j, (*chip, c), me).wait_recv()
                fwd = copy(a, 4 + j, (*chip, c), sibling)
                fwd.start()
                forwards.append(fwd)
        for a in range(self.n):
            copy(a, 0, sibling, me).wait_recv()
            for j, chip in enumerate(chips):
                copy(a, 4 + j, (*chip, 1 - c), me).wait_recv()
        for cp in own + forwards:
            cp.wait_send()
        for loc in mine:
            loc.wait()


def _matmul(a, b, *, tm, tn, tk, out_dtype, name, trans_a=False, trans_b=False, comm=None):
    m, k = (a.shape[1], a.shape[0]) if trans_a else a.shape
    n = b.shape[0] if trans_b else b.shape[1]
    nk = k // tk
    dims = TN if trans_a else (NT if trans_b else NN)
    assert not (trans_a and trans_b)
    nc = comm.n if comm else 0
    grid = (m // tm, n // tn, nk)

    def body(*refs):
        a_ref, b_ref = refs[:2]
        cin = refs[2:2 + nc]
        o_ref = refs[2 + nc]
        cout = refs[3 + nc:3 + 2 * nc]
        scratch = refs[3 + 2 * nc:]
        sems = scratch[len(scratch) - 3:] if comm else None
        i, j, kk = pl.program_id(0), pl.program_id(1), pl.program_id(2)
        if comm:
            @pl.when((i == 0) & (j == 0) & (kk == 0))
            def _():
                comm.start(cin, cout, sems)

        if nk == 1:
            o_ref[...] = _dot(a_ref[...], b_ref[...], dims).astype(out_dtype)
        else:
            acc_ref = scratch[0]

            @pl.when(kk == 0)
            def _():
                acc_ref[...] = jnp.zeros_like(acc_ref)

            acc_ref[...] += _dot(a_ref[...], b_ref[...], dims)

            @pl.when(kk == nk - 1)
            def _():
                o_ref[...] = acc_ref[...].astype(out_dtype)

        if comm:
            @pl.when((i == grid[0] - 1) & (j == grid[1] - 1) & (kk == nk - 1))
            def _():
                comm.wait(cin, cout, sems)

    a_spec = (pl.BlockSpec((tk, tm), lambda i, j, kk: (kk, i)) if trans_a
              else pl.BlockSpec((tm, tk), lambda i, j, kk: (i, kk)))
    b_spec = (pl.BlockSpec((tn, tk), lambda i, j, kk: (j, kk)) if trans_b
              else pl.BlockSpec((tk, tn), lambda i, j, kk: (kk, j)))
    sem = ("arbitrary",) * 3 if comm else ("parallel", "parallel", "arbitrary")
    res = pl.pallas_call(
        body, name=name, grid=grid,
        in_specs=[a_spec, b_spec] + (comm.in_specs if comm else []),
        out_specs=[pl.BlockSpec((tm, tn), lambda i, j, kk: (i, j))] + (comm.out_specs if comm else []),
        out_shape=[jax.ShapeDtypeStruct((m, n), out_dtype)] + (comm.out_shape if comm else []),
        scratch_shapes=([] if nk == 1 else [pltpu.VMEM((tm, tn), F32)]) + (comm.scratch if comm else []),
        compiler_params=_params(sem, side_effects=bool(comm)),
    )(a, b, *(comm.operands if comm else []))
    return res if comm else res[0]


ROW_K = 6
X_K = 8


def _x_specs(nx, k, tile_of=lambda i: i):
    return [pl.BlockSpec((CHUNK, D_MODEL), functools.partial(lambda u, i: (jnp.clip(k * tile_of(i) + u - 1, 0, nx - 1), 0), u))
            for u in range(k)]


def _chunk_of_row(i, k):
    return k * i + lax.broadcasted_iota(jnp.int32, (k * CHUNK, 1), 0) // CHUNK


CONV_COLS = 512
HALO = 8


def _conv_pre(ext, w, b):
    taps = [ext[HALO:, :]] + [pltpu.roll(ext, j, 0)[HALO:, :] for j in range(1, CONV_WIDTH)]
    acc = b + w[3:4, :] * taps[0]
    for j in range(1, CONV_WIDTH):
        acc = acc + w[3 - j:4 - j, :] * taps[j]
    return acc, taps


def _conv_bwd(dpre, proj, conv_w, dproj):
    lp = proj.shape[0]
    t = _tile(lp, (704, 384, 128, 64))
    hb = t // HALO
    nt = lp // t
    c0 = OFF_XBC // CONV_COLS

    def body(dp_ref, dpn_ref, u_ref, w_ref, _, du_ref, dw_ref, db_ref):
        i = pl.program_id(1)
        w = w_ref[...]
        u = u_ref[...].astype(F32)
        dpre = jnp.concatenate([dp_ref[...], dpn_ref[...] * (i < nt - 1).astype(F32)], axis=0)
        ahead = [dpre[:t, :]] + [pltpu.roll(dpre, t + HALO - j, 0)[:t, :] for j in range(1, CONV_WIDTH)]
        du = w[3:4, :] * ahead[0]
        for j in range(1, CONV_WIDTH):
            du = du + w[3 - j:4 - j, :] * ahead[j]
        du_ref[...] = du.astype(BF16)

        @pl.when(i == 0)
        def _():
            dw_ref[...] = jnp.zeros_like(dw_ref)
            db_ref[...] = jnp.zeros_like(db_ref)

        db_ref[...] += jnp.sum(ahead[0], axis=0, keepdims=True)
        for j in range(CONV_WIDTH):
            dw_ref[3 - j:4 - j, :] += jnp.sum(ahead[j] * u, axis=0, keepdims=True)

    nxt = lambda i: jnp.minimum((i + 1) * hb, lp // HALO - 1)
    return pl.pallas_call(
        body, name="conv_bwd", grid=(D_CONV // CONV_COLS, nt),
        in_specs=[
            pl.BlockSpec((t, CONV_COLS), lambda j, i: (i, j)),
            pl.BlockSpec((HALO, CONV_COLS), lambda j, i: (nxt(i), j)),
            pl.BlockSpec((t, CONV_COLS), lambda j, i: (i, c0 + j)),
            pl.BlockSpec((CONV_WIDTH, CONV_COLS), lambda j, i: (0, j)),
            ANY,
        ],
        out_specs=[
            pl.BlockSpec((t, CONV_COLS), lambda j, i: (i, c0 + j)),
            pl.BlockSpec((CONV_WIDTH, CONV_COLS), lambda j, i: (0, j)),
            pl.BlockSpec((1, CONV_COLS), lambda j, i: (0, j)),
        ],
        out_shape=[
            jax.ShapeDtypeStruct((lp, NP), BF16),
            jax.ShapeDtypeStruct((CONV_WIDTH, D_CONV), F32),
            jax.ShapeDtypeStruct((1, D_CONV), F32),
        ],
        input_output_aliases={4: 0},
        compiler_params=_params(("parallel", "arbitrary")),
    )(dpre, dpre, proj, conv_w, dproj)


def _swap_halves(t):
    w = t.shape[-1]
    lane = lax.broadcasted_iota(jnp.int32, t.shape, 1)
    return jnp.where((lane % HEAD_DIM) < HEAD_DIM // 2, pltpu.roll(t, w - HEAD_DIM // 2, 1),
                     pltpu.roll(t, HEAD_DIM // 2, 1))


def _act_fwd(proj, cos_t, sin_t, expand, dt_bias_pad):
    lp = proj.shape[0]
    t = _tile(lp, (384, 128, 64))

    def body(q_ref, k_ref, v_ref, dt_ref, cos_ref, sin_ref, ex_ref, bias_ref, qo_ref, ko_ref, vo_ref, dto_ref):
        i = pl.program_id(0)
        cos = cos_ref[...]
        sin = sin_ref[...]
        q = q_ref[...]
        qo_ref[...] = (q * jnp.tile(cos, (1, D_ATT // 128)) + _swap_halves(q) * jnp.tile(sin, (1, D_ATT // 128))).astype(BF16)
        k = k_ref[...]
        ko_ref[...] = (k * jnp.tile(cos, (1, D_KV // 128)) + _swap_halves(k) * jnp.tile(sin, (1, D_KV // 128))).astype(BF16)
        vo_ref[...] = v_ref[...].astype(BF16)
        raw = dt_ref[...] + bias_ref[...]
        sp = jnp.maximum(raw, 0.0) + jnp.log1p(jnp.exp(-jnp.abs(raw)))
        row = i * t + lax.broadcasted_iota(jnp.int32, sp.shape, 0)
        dto_ref[...] = _dot(jnp.where(row >= PAD_LEAD, sp, 0.0), ex_ref[...], NN, HIGHEST)

    return pl.pallas_call(
        body, name="act_fwd", grid=(lp // t,),
        in_specs=[
            pl.BlockSpec((t, D_ATT), lambda i: (i, OFF_Q // D_ATT)),
            pl.BlockSpec((t, D_KV), lambda i: (i, OFF_K // D_KV)),
            pl.BlockSpec((t, D_KV), lambda i: (i, OFF_V // D_KV)),
            pl.BlockSpec((t, 128), lambda i: (i, OFF_DT // 128)),
            pl.BlockSpec((t, 128), lambda i: (i, 0)),
            pl.BlockSpec((t, 128), lambda i: (i, 0)),
            pl.BlockSpec((128, D_SSD), lambda i: (0, 0)),
            pl.BlockSpec((1, 128), lambda i: (0, 0)),
        ],
        out_specs=[
            pl.BlockSpec((t, D_ATT), lambda i: (i, 0)),
            pl.BlockSpec((t, D_KV), lambda i: (i, 0)),
            pl.BlockSpec((t, D_KV), lambda i: (i, 0)),
            pl.BlockSpec((t, D_SSD), lambda i: (i, 0)),
        ],
        out_shape=[
            jax.ShapeDtypeStruct((lp, D_ATT), BF16),
            jax.ShapeDtypeStruct((lp, D_KV), BF16),
            jax.ShapeDtypeStruct((lp, D_KV), BF16),
            jax.ShapeDtypeStruct((lp, D_SSD), F32),
        ],
        compiler_params=_params(("parallel",)),
    )(proj, proj, proj, proj, cos_t, sin_t, expand, dt_bias_pad)


def _act_bwd(dqr, dkr, dv, dg, ddt_part, cos_t, sin_t, reduce_t, dproj):
    lp = dqr.shape[0]
    t = _tile(lp, (384, 128, 64))

    def body(dq_ref, dk_ref, dv_ref, dg_ref, ddt_ref, cos_ref, sin_ref, red_ref, _, o_ref, db_ref):
        i = pl.program_id(0)
        cos = cos_ref[...]
        sin = sin_ref[...]
        dq = dq_ref[...]
        dq = dq * jnp.tile(cos, (1, D_ATT // 128)) + _swap_halves(dq * jnp.tile(sin, (1, D_ATT // 128)))
        dk = dk_ref[...]
        dk = dk * jnp.tile(cos, (1, D_KV // 128)) + _swap_halves(dk * jnp.tile(sin, (1, D_KV // 128)))
        ddt = _dot(ddt_ref[...], red_ref[...], NN, HIGHEST)
        o_ref[...] = jnp.concatenate(
            [dq.astype(BF16), dg_ref[...].astype(BF16), dk.astype(BF16), dv_ref[...].astype(BF16), ddt.astype(BF16),
             jnp.zeros((t, NP - OFF_DT - 128), BF16)], axis=1)

        @pl.when(i == 0)
        def _():
            db_ref[...] = jnp.zeros_like(db_ref)

        db_ref[...] += jnp.sum(ddt, axis=0, keepdims=True)

    return pl.pallas_call(
        body, name="act_bwd", grid=(lp // t,),
        in_specs=[
            pl.BlockSpec((t, D_ATT), lambda i: (i, 0)),
            pl.BlockSpec((t, D_KV), lambda i: (i, 0)),
            pl.BlockSpec((t, D_KV), lambda i: (i, 0)),
            pl.BlockSpec((t, D_ATT), lambda i: (i, 0)),
            pl.BlockSpec((t, D_SSD), lambda i: (i, 0)),
            pl.BlockSpec((t, 128), lambda i: (i, 0)),
            pl.BlockSpec((t, 128), lambda i: (i, 0)),
            pl.BlockSpec((D_SSD, 128), lambda i: (0, 0)),
            ANY,
        ],
        out_specs=[pl.BlockSpec((t, TAIL_W), lambda i: (i, OFF_Q // TAIL_W)), pl.BlockSpec((1, 128), lambda i: (0, 0))],
        out_shape=[jax.ShapeDtypeStruct((lp, NP), BF16), jax.ShapeDtypeStruct((1, 128), F32)],
        input_output_aliases={8: 0},
        compiler_params=_params(("arbitrary",)),
    )(dqr, dkr, dv, dg, ddt_part, cos_t, sin_t, reduce_t, dproj)


def _cs_row(cs):
    row = lax.broadcasted_iota(jnp.int32, cs.shape, 0)
    lane = lax.broadcasted_iota(jnp.int32, cs.shape, 1)
    return jnp.sum(jnp.where(row == lane % HEAD_DIM, cs, 0.0), axis=0, keepdims=True)


def _ssd_fwd(proj, conv_w, conv_b, dt_rep, a_rep, dsk_rep, wn, tri):
    lp = proj.shape[0]
    nc = lp // CHUNK
    assert SSD_GPS == N_GROUPS
    gw = SSD_GPS * GROUP_W
    cps = _tile(nc, (SSD_CPS, 1))

    def body(rawa_ref, rawb_ref, cw_ref, cb_ref, dt_ref, z_ref, a_ref, dsk_ref, wn_ref, tri_ref,
             yn_ref, ytot_ref, hprev_ref, xbc_ref, dsilu_ref, h_scr, tail_scr):
        @pl.when(pl.program_id(1) == 0)
        def _():
            h_scr[...] = jnp.zeros_like(h_scr)
            tail_scr[...] = jnp.zeros_like(tail_scr)

        for sub in range(cps):
            rs = slice(sub * CHUNK, (sub + 1) * CHUNK)
            raw = jnp.concatenate([rawa_ref[rs, :], rawb_ref[rs, :]], axis=1)
            pre, _ = _conv_pre(jnp.concatenate([tail_scr[...], raw], axis=0), cw_ref[...], cb_ref[...])
            tail_scr[...] = raw[CHUNK - HALO:, :]
            sg = _sigmoid(pre)
            xbc_ref[rs, :] = pre * sg
            dsilu_ref[rs, :] = sg * (1.0 + pre * (1.0 - sg))

        G = range(SSD_GPS)
        colsl = [slice(gi * GROUP_W, (gi + 1) * GROUP_W) for gi in G]
        rows4 = lax.broadcasted_iota(jnp.int32, (GROUP_W, GROUP_W), 0) // HEAD_DIM
        cols4 = lax.broadcasted_iota(jnp.int32, (GROUP_W, GROUP_W), 1) // HEAD_DIM
        lrow = lax.broadcasted_iota(jnp.int32, (CHUNK, GROUP_W), 0)
        lcol = lax.broadcasted_iota(jnp.int32, (CHUNK, GROUP_W), 1) % HEAD_DIM

        def one_chunk(sub):
            rs = slice(sub * CHUNK, (sub + 1) * CHUNK)
            xbc = [xbc_ref[rs, gi * XBC_BLK:(gi + 1) * XBC_BLK] for gi in G]
            dt = [dt_ref[rs, colsl[gi]] for gi in G]
            xs = [xbc[gi][:, :GROUP_W] for gi in G]
            b = [xbc[gi][:, GROUP_W:GROUP_W + D_STATE].astype(BF16) for gi in G]
            c = [xbc[gi][:, GROUP_W + D_STATE:].astype(BF16) for gi in G]
            hprev = [h_scr[gi] for gi in G]
            cs = [_dot(tri_ref[...], dt[gi] * a_ref[:, colsl[gi]], NN, HIGHEST) for gi in G]
            yoff = [_dot(c[gi], hprev[gi].astype(BF16)) for gi in G]
            cs_t = [_cs_row(cs[gi]) for gi in G]
            xdt = [xs[gi] * dt[gi] for gi in G]
            cs_last = [cs[gi][CHUNK - 1:CHUNK, :] for gi in G]
            st = [_dot(b[gi], (xdt[gi] * jnp.exp(cs_last[gi] - cs[gi])).astype(BF16), TN) for gi in G]
            cb4 = [_dot(c[gi], jnp.concatenate([b[gi]] * HEADS_PER_GROUP, axis=0), NT) for gi in G]
            m = [(cb4[gi] * jnp.exp(jnp.where(lrow >= lcol, cs[gi] - cs_t[gi], -jnp.inf))).astype(BF16) for gi in G]
            xbd = [jnp.where(rows4 == cols4, jnp.concatenate([xdt[gi].astype(BF16)] * HEADS_PER_GROUP, axis=0), 0.0)
                   for gi in G]
            ydiag = [_dot(m[gi], xbd[gi]) for gi in G]
            for gi in G:
                cols = colsl[gi]
                ytot = ydiag[gi] + yoff[gi] * jnp.exp(cs[gi]) + dsk_ref[:, cols] * xs[gi]
                z = z_ref[rs, cols]
                gz = ytot * (z * _sigmoid(z))
                rstd = lax.rsqrt(jnp.mean(gz * gz, axis=-1, keepdims=True) + EPS)
                hprev_ref[sub, gi] = hprev[gi]
                h_scr[gi] = hprev[gi] * jnp.exp(cs_last[gi]) + st[gi]
                ytot_ref[rs, cols] = ytot
                yn_ref[rs, cols] = (gz * rstd * wn_ref[:, cols]).astype(BF16)

        for sub in range(cps):
            one_chunk(sub)

    vec = pl.BlockSpec((1, gw), lambda g, c: (0, g))
    blk = pl.BlockSpec((cps * CHUNK, gw), lambda g, c: (c, g))
    xblk = pl.BlockSpec((cps * CHUNK, D_CONV), lambda g, c: (c, 0))
    half = D_CONV // 2
    return pl.pallas_call(
        body, name="ssd_fwd", grid=(N_GROUPS // SSD_GPS, nc // cps),
        in_specs=[
            pl.BlockSpec((cps * CHUNK, half), lambda g, c: (c, OFF_XBC // half)),
            pl.BlockSpec((cps * CHUNK, half), lambda g, c: (c, OFF_XBC // half + 1)),
            pl.BlockSpec((CONV_WIDTH, D_CONV), lambda g, c: (0, 0)),
            pl.BlockSpec((1, D_CONV), lambda g, c: (0, 0)),
            blk, blk, vec, vec, vec,
            pl.BlockSpec((CHUNK, CHUNK), lambda g, c: (0, 0)),
        ],
        out_specs=[blk, blk, pl.BlockSpec((cps, SSD_GPS, D_STATE, GROUP_W), lambda g, c: (c, g, 0, 0)), xblk, xblk],
        out_shape=[
            jax.ShapeDtypeStruct((lp, D_MIX), BF16),
            jax.ShapeDtypeStruct((lp, D_SSD), F32),
            jax.ShapeDtypeStruct((nc, N_GROUPS, D_STATE, GROUP_W), F32),
            jax.ShapeDtypeStruct((lp, D_CONV), F32),
            jax.ShapeDtypeStruct((lp, D_CONV), F32),
        ],
        scratch_shapes=[pltpu.VMEM((SSD_GPS, D_STATE, GROUP_W), F32), pltpu.VMEM((HALO, D_CONV), F32)],
        compiler_params=_params(("arbitrary", "arbitrary")),
    )(proj, proj, conv_w, conv_b, dt_rep, proj, a_rep, dsk_rep, wn, tri)


def _ssd_bwd(dmix, ytot, proj, xbc, dsilu, dt_rep, hprev, a_rep, dsk_rep, wn, tri, comm):
    lp = xbc.shape[0]
    nc = lp // CHUNK
    gps = SSD_GPS
    gw = gps * GROUP_W
    cps = _tile(nc, (SSD_CPS, 1))
    ncm = comm.n
    n_in, n_out = 11, 6
    grid = (N_GROUPS // gps, nc // cps)

    def all_groups(refs):
        (dyn_ref, ytot_ref, z_ref, xbc_ref, dsilu_ref, dt_ref, hprev_ref, a_ref, dsk_ref, wn_ref, tri_ref,
         dz_ref, dxbc_ref, ddt_ref, dd_ref, da_ref, dwn_ref, dh_scr) = refs
        G = range(gps)
        H = range(HEADS_PER_GROUP)
        cl = [slice(gi * GROUP_W, (gi + 1) * GROUP_W) for gi in G]
        hl = [slice(r * HEAD_DIM, (r + 1) * HEAD_DIM) for r in H]
        tri = tri_ref[...]
        xbc = [xbc_ref[:, gi * XBC_BLK:(gi + 1) * XBC_BLK] for gi in G]
        dt = [dt_ref[:, cl[gi]] for gi in G]
        a = [a_ref[:, cl[gi]] for gi in G]
        xs = [xbc[gi][:, :GROUP_W] for gi in G]
        bbf = [xbc[gi][:, GROUP_W:GROUP_W + D_STATE].astype(BF16) for gi in G]
        cbf = [xbc[gi][:, GROUP_W + D_STATE:].astype(BF16) for gi in G]
        hprev = [hprev_ref[gi] for gi in G]
        hbf = [hprev[gi].astype(BF16) for gi in G]
        dhn = [dh_scr[gi] for gi in G]
        dhnb = [dhn[gi].astype(BF16) for gi in G]
        cs = [_dot(tri, dt[gi] * a[gi], NN, HIGHEST) for gi in G]
        g = [_dot(cbf[gi], hbf[gi]) for gi in G]
        dxw = [_dot(bbf[gi], dhnb[gi]) for gi in G]
        cs_t = [_cs_row(cs[gi]) for gi in G]
        dy = []
        for gi in G:
            ytot = ytot_ref[:, cl[gi]]
            z = z_ref[:, cl[gi]]
            dyn = dyn_ref[:, cl[gi]]
            sz = _sigmoid(z)
            silu_z = z * sz
            gz = ytot * silu_z
            rstd = lax.rsqrt(jnp.mean(gz * gz, axis=-1, keepdims=True) + EPS)
            xhat = gz * rstd
            dwn_ref[:, cl[gi]] += jnp.sum(dyn * xhat, axis=0, keepdims=True)
            dxhat = dyn * wn_ref[:, cl[gi]]
            dgz = rstd * (dxhat - xhat * jnp.mean(dxhat * xhat, axis=-1, keepdims=True))
            dy.append(dgz * silu_z)
            dz_ref[:, cl[gi]] = (dgz * ytot * (sz * (1.0 + z * (1.0 - sz)))).astype(BF16)
            dd_ref[:, cl[gi]] += jnp.sum(dy[gi] * xs[gi], axis=0, keepdims=True)
        xdt = [xs[gi] * dt[gi] for gi in G]
        e = [jnp.exp(cs[gi]) for gi in G]
        cs_last = [cs[gi][CHUNK - 1:CHUNK, :] for gi in G]
        dte = [jnp.exp(cs_last[gi] - cs[gi]) for gi in G]
        cd = [jnp.exp(cs_last[gi]) for gi in G]
        dgb = [(dy[gi] * e[gi]).astype(BF16) for gi in G]
        dyb = [dy[gi].astype(BF16) for gi in G]
        xdtb = [xdt[gi].astype(BF16) for gi in G]
        dc = [_dot(dgb[gi], hbf[gi], NT) for gi in G]
        dhprev = [_dot(cbf[gi], dgb[gi], TN) for gi in G]
        db = [_dot((xdt[gi] * dte[gi]).astype(BF16), dhnb[gi], NT) for gi in G]
        row = lax.broadcasted_iota(jnp.int32, (CHUNK, CHUNK), 0)
        causal = row >= lax.broadcasted_iota(jnp.int32, (CHUNK, CHUNK), 1)
        cb = [_dot(cbf[gi], bbf[gi], NT) for gi in G]
        dm = [[_dot(dyb[gi][:, hl[r]], xdtb[gi][:, hl[r]], NT) for r in H] for gi in G]
        mb, dseg, dcbb = [], [], []
        for gi in G:
            mb.append([])
            dseg.append([])
            dcb = None
            for r in H:
                seg = cs[gi][:, r * HEAD_DIM:r * HEAD_DIM + 1] - cs_t[gi][:, hl[r]]
                lm = jnp.exp(jnp.where(causal, seg, -jnp.inf))
                m = cb[gi] * lm
                mb[gi].append(m.astype(BF16))
                dseg[gi].append(dm[gi][r] * m)
                dcb = dm[gi][r] * lm if r == 0 else dcb + dm[gi][r] * lm
            dcbb.append(dcb.astype(BF16))
        dxdt_diag = [[_dot(mb[gi][r], dyb[gi][:, hl[r]], TN) for r in H] for gi in G]
        ones = jnp.ones((CHUNK, HEAD_DIM), F32)
        colsum = [[_dot(dseg[gi][r], ones, TN, HIGHEST) for r in H] for gi in G]
        dc2 = [_dot(dcbb[gi], bbf[gi]) for gi in G]
        db2 = [_dot(dcbb[gi], cbf[gi], TN) for gi in G]
        dcs = []
        for gi in G:
            t_dte = dxw[gi] * xdt[gi] * dte[gi]
            dcs_last = (jnp.sum(dhn[gi] * hprev[gi], axis=0, keepdims=True) * cd[gi]
                        + jnp.sum(t_dte, axis=0, keepdims=True))
            diag = jnp.concatenate(
                [(jnp.sum(dseg[gi][r], axis=1, keepdims=True) - colsum[gi][r]) * (1.0 / HEAD_DIM) for r in H], axis=1)
            d = dy[gi] * g[gi] * e[gi] - t_dte + diag
            row = lax.broadcasted_iota(jnp.int32, d.shape, 0)
            dcs.append(d + jnp.where(row == CHUNK - 1, dcs_last, 0.0))
        dda = [_dot(tri, dcs[gi], TN, HIGHEST) for gi in G]
        for gi in G:
            dxdt = dxw[gi] * dte[gi] + jnp.concatenate(dxdt_diag[gi], axis=1)
            da_ref[:, cl[gi]] += jnp.sum(dda[gi] * dt[gi], axis=0, keepdims=True)
            ddt = dda[gi] * a[gi] + dxdt * xs[gi]
            dxs = dsk_ref[:, cl[gi]] * dy[gi] + dxdt * dt[gi]
            ddt_ref[:, cl[gi]] = ddt * (1.0 - jnp.exp(-dt[gi]))
            xcols = slice(gi * XBC_BLK, (gi + 1) * XBC_BLK)
            dxbc_ref[:, xcols] = jnp.concatenate([dxs, db[gi] + db2[gi], dc[gi] + dc2[gi]], axis=1) * dsilu_ref[:, xcols]
            dh_scr[gi] = dhprev[gi] + dhn[gi] * cd[gi]

    def body(*refs):
        ins = refs[:n_in]
        cin = refs[n_in:n_in + ncm]
        outs = refs[n_in + ncm:n_in + ncm + n_out]
        cout = refs[n_in + ncm + n_out:n_in + 2 * ncm + n_out]
        dh_scr = refs[n_in + 2 * ncm + n_out]
        sems = refs[n_in + 2 * ncm + n_out + 1:]
        g, c = pl.program_id(0), pl.program_id(1)

        @pl.when((g == 0) & (c == 0))
        def _():
            comm.start(cin, cout, sems)

        @pl.when(c == 0)
        def _():
            dh_scr[...] = jnp.zeros_like(dh_scr)
            for ref in outs[3:]:
                ref[...] = jnp.zeros_like(ref)

        for sub in reversed(range(cps)):
            rs = pl.ds(sub * CHUNK, CHUNK)
            chunk_ins = tuple(r.at[rs] for r in ins[:6]) + (ins[6].at[sub],) + ins[7:]
            chunk_outs = tuple(r.at[rs] for r in outs[:3]) + outs[3:]
            all_groups(chunk_ins + chunk_outs + (dh_scr,))

        @pl.when((g == grid[0] - 1) & (c == grid[1] - 1))
        def _():
            comm.wait(cin, cout, sems)

    rev = lambda c: grid[1] - 1 - c
    vec = pl.BlockSpec((1, gw), lambda g, c: (0, g))
    blk = pl.BlockSpec((cps * CHUNK, gw), lambda g, c: (rev(c), g))
    xblk = pl.BlockSpec((cps * CHUNK, gps * XBC_BLK), lambda g, c: (rev(c), g))
    res = pl.pallas_call(
        body, name="ssd_bwd", grid=grid,
        in_specs=[blk, blk, blk, xblk, xblk, blk,
                  pl.BlockSpec((cps, gps, D_STATE, GROUP_W), lambda g, c: (rev(c), g, 0, 0)),
                  vec, vec, vec,
                  pl.BlockSpec((CHUNK, CHUNK), lambda g, c: (0, 0))] + comm.in_specs,
        out_specs=[blk, xblk, blk, vec, vec, vec] + comm.out_specs,
        out_shape=[
            jax.ShapeDtypeStruct((lp, NP), BF16),
            jax.ShapeDtypeStruct((lp, D_CONV), F32),
            jax.ShapeDtypeStruct((lp, D_SSD), F32),
            jax.ShapeDtypeStruct((1, D_SSD), F32),
            jax.ShapeDtypeStruct((1, D_SSD), F32),
            jax.ShapeDtypeStruct((1, D_SSD), F32),
        ] + comm.out_shape,
        scratch_shapes=[pltpu.VMEM((gps, D_STATE, GROUP_W), F32)] + comm.scratch,
        compiler_params=_params(("arbitrary", "arbitrary"), side_effects=True),
    )(dmix, ytot, proj, xbc, dsilu, dt_rep, hprev, a_rep, dsk_rep, wn, tri, *comm.operands)
    return res


def _stack_heads(t, h):
    return jnp.concatenate([t[:, (REP * h + r) * HEAD_DIM:(REP * h + r + 1) * HEAD_DIM] for r in range(REP)], axis=0)


def _band(t2, t1, t0, h):
    sl = slice(h * HEAD_DIM, (h + 1) * HEAD_DIM)
    return jnp.concatenate([t2[:, sl], t1[:, sl], t0[:, sl]], axis=0)


def _attn_probs(s, sink, qc):
    s = s * (HEAD_DIM ** -0.5)
    key_abs = (qc - WINDOW_CHUNKS) * CHUNK + lax.broadcasted_iota(jnp.int32, s.shape, 1)
    s = jnp.where(key_abs >= PAD_LEAD, s, -jnp.inf)
    m = jnp.maximum(jnp.max(s, axis=-1, keepdims=True), sink)
    p = jnp.exp(s - m)
    ps = jnp.exp(sink - m)
    denom = jnp.sum(p, axis=-1, keepdims=True) + ps
    return p / denom, ps / denom


ATT_QC_FWD = 6
ATT_QC_BWD = 2


def _kv_specs(width, newest_chunk_of, kc):
    return [pl.BlockSpec((CHUNK, width), functools.partial(lambda j, p: (jnp.maximum(newest_chunk_of(p) - j, 0), 0), j))
            for j in range(kc - 1, -1, -1)]


def _attn_fwd(qr, kr, vb, proj, sink_stack, mix):
    lp = qr.shape[0]
    nc = lp // CHUNK
    qn = _tile(nc, (ATT_QC_FWD, 2, 1))
    kn = WINDOW_CHUNKS + qn
    qrows = qn * CHUNK

    def body(q_ref, *rest):
        k_refs, v_refs = rest[:kn], rest[kn:2 * kn]
        g_ref, sink_ref, _, att_ref, mix_ref = rest[2 * kn:]
        p = pl.program_id(0)
        q = q_ref[...]
        ks = [r[...] for r in k_refs]
        vs = [r[...] for r in v_refs]
        units = [(u, h) for u in range(qn) for h in range(KV_HEADS)]
        s = [_dot(_stack_heads(q[u * CHUNK:(u + 1) * CHUNK, :], h), _band(*ks[u:u + 3], h), NT) for u, h in units]
        vbh = [_band(*vs[u:u + 3], h) for u, h in units]
        pn = [_attn_probs(s[i], sink_ref[h], qn * p + u)[0].astype(BF16) for i, (u, h) in enumerate(units)]
        o = [_dot(pn[i], vbh[i]) for i in range(len(units))]
        att = jnp.concatenate(
            [jnp.concatenate([o[u * KV_HEADS + h][r * CHUNK:(r + 1) * CHUNK, :] for h in range(KV_HEADS) for r in range(REP)],
                             axis=1) for u in range(qn)], axis=0)
        att_ref[...] = att
        g = g_ref[...]
        mix_ref[...] = (att * (g * _sigmoid(g))).astype(BF16)

    newest = lambda p: qn * p + qn - 1
    return pl.pallas_call(
        body, name="attn_fwd", grid=(nc // qn,),
        in_specs=[pl.BlockSpec((qrows, D_ATT), lambda p: (p, 0))] + _kv_specs(D_KV, newest, kn) + _kv_specs(D_KV, newest, kn) + [
            pl.BlockSpec((qrows, D_ATT), lambda p: (p, OFF_G // D_ATT)),
            pl.BlockSpec((KV_HEADS, REP * CHUNK, 1), lambda p: (0, 0, 0)),
            ANY,
        ],
        out_specs=[pl.BlockSpec((qrows, D_ATT), lambda p: (p, 0)),
                   pl.BlockSpec((qrows, D_ATT), lambda p: (p, D_SSD // D_ATT))],
        out_shape=[jax.ShapeDtypeStruct((lp, D_ATT), F32), jax.ShapeDtypeStruct((lp, D_MIX), BF16)],
        input_output_aliases={2 * kn + 3: 1},
        compiler_params=_params(("parallel",)),
    )(qr, *([kr] * kn), *([vb] * kn), proj, sink_stack, mix)


def _attn_bwd(qr, kr, vb, att, proj, dmix, sink_stack):
    lp = qr.shape[0]
    nc = lp // CHUNK
    qn = ATT_QC_BWD
    kn = WINDOW_CHUNKS + qn
    assert nc % qn == 0 and WINDOW_CHUNKS % qn == 0
    steps = nc // qn
    qrows = qn * CHUNK
    wrows = kn * CHUNK

    def body(q_ref, *rest):
        k_refs, v_refs = rest[:kn], rest[kn:2 * kn]
        (att_ref, g_ref, do_ref, sink_ref, dq_ref, dk_ref, dv_ref, dg_ref, dsink_ref, dk_acc, dv_acc) = rest[2 * kn:]
        step = pl.program_id(0)

        @pl.when(step == 0)
        def _():
            dk_acc[...] = jnp.zeros_like(dk_acc)
            dv_acc[...] = jnp.zeros_like(dv_acc)
            dsink_ref[...] = jnp.zeros_like(dsink_ref)

        @pl.when(step < steps)
        def _():
            q = q_ref[...]
            ks = [r[...] for r in k_refs]
            vs = [r[...] for r in v_refs]
            att = att_ref[...]
            g = g_ref[...]
            dog = do_ref[...]
            sg = _sigmoid(g)
            dg_ref[...] = dog * att * (sg * (1.0 + g * (1.0 - sg)))
            do = dog * (g * sg)
            units = [(u, h) for u in range(qn) for h in range(KV_HEADS)]
            n = range(len(units))
            rows = [slice(u * CHUNK, (u + 1) * CHUNK) for u in range(qn)]
            qs = [_stack_heads(q[rows[u], :], h) for u, h in units]
            kb = [_band(*ks[u:u + 3], h) for u, h in units]
            vbh = [_band(*vs[u:u + 3], h) for u, h in units]
            dos = [_stack_heads(do[rows[u], :], h) for u, h in units]
            dosb = [dos[i].astype(BF16) for i in n]
            s = [_dot(qs[i], kb[i], NT) for i in n]
            dp = [_dot(dosb[i], vbh[i], NT) for i in n]
            ds, pnb = [], []
            for i, (u, h) in enumerate(units):
                pn, psink = _attn_probs(s[i], sink_ref[h], qn * step + u)
                delta = jnp.sum(dos[i] * _stack_heads(att[rows[u], :], h), axis=-1, keepdims=True)
                ds.append((pn * (dp[i] - delta)).astype(BF16))
                pnb.append(pn.astype(BF16))
                dsink_ref[h] += -(psink * delta)
            dqs = [_dot(ds[i], kb[i]) for i in n]
            dks = [_dot(ds[i], qs[i], TN) for i in n]
            dvs = [_dot(pnb[i], dosb[i], TN) for i in n]
            dq_ref[...] = jnp.concatenate(
                [jnp.concatenate([dqs[u * KV_HEADS + h][r * CHUNK:(r + 1) * CHUNK, :]
                                  for h in range(KV_HEADS) for r in range(REP)], axis=1) for u in range(qn)],
                axis=0) * (HEAD_DIM ** -0.5)
            for u in range(qn):
                band = slice(u * CHUNK, u * CHUNK + BAND)
                dk_acc[band, :] += jnp.concatenate(dks[u * KV_HEADS:(u + 1) * KV_HEADS], axis=1) * (HEAD_DIM ** -0.5)
                dv_acc[band, :] += jnp.concatenate(dvs[u * KV_HEADS:(u + 1) * KV_HEADS], axis=1)

        dk_ref[...] = dk_acc[0:qrows, :]
        dv_ref[...] = dv_acc[0:qrows, :]
        for acc in (dk_acc, dv_acc):
            rest_rows = acc[qrows:wrows, :]
            acc[0:wrows - qrows, :] = rest_rows
            acc[wrows - qrows:wrows, :] = jnp.zeros((qrows, D_KV), F32)

    qp = lambda p: jnp.minimum(p, steps - 1)
    newest = lambda p: qn * qp(p) + qn - 1
    qblk = pl.BlockSpec((qrows, D_ATT), lambda p: (qp(p), 0))
    oldest = pl.BlockSpec((qrows, D_KV), lambda p: (jnp.maximum(p - 1, 0), 0))
    return pl.pallas_call(
        body, name="attn_bwd", grid=(steps + 1,),
        in_specs=[qblk] + _kv_specs(D_KV, newest, kn) + _kv_specs(D_KV, newest, kn) + [
            qblk,
            pl.BlockSpec((qrows, D_ATT), lambda p: (qp(p), OFF_G // D_ATT)),
            pl.BlockSpec((qrows, D_ATT), lambda p: (qp(p), D_SSD // D_ATT)),
            pl.BlockSpec((KV_HEADS, REP * CHUNK, 1), lambda p: (0, 0, 0)),
        ],
        out_specs=[qblk, oldest, oldest, qblk, pl.BlockSpec((KV_HEADS, REP * CHUNK, 1), lambda p: (0, 0, 0))],
        out_shape=[
            jax.ShapeDtypeStruct((lp, D_ATT), F32),
            jax.ShapeDtypeStruct((lp, D_KV), F32),
            jax.ShapeDtypeStruct((lp, D_KV), F32),
            jax.ShapeDtypeStruct((lp, D_ATT), F32),
            jax.ShapeDtypeStruct((KV_HEADS, REP * CHUNK, 1), F32),
        ],
        scratch_shapes=[pltpu.VMEM((wrows, D_KV), F32), pltpu.VMEM((wrows, D_KV), F32)],
        compiler_params=_params(("arbitrary",)),
    )(qr, *([kr] * kn), *([vb] * kn), att, proj, dmix, sink_stack)


def _post_loss(out, x2d, target, w):
    lp = out.shape[0]
    nc = lp // CHUNK
    nx = x2d.shape[0] // CHUNK

    k = _tile(nc, (ROW_K, 3, 2, 1))

    def body(o_ref, *rest):
        x_refs, t_refs = rest[:k], rest[k:2 * k]
        w_ref, do_ref, dy_ref, gw_ref, loss_ref = rest[2 * k:]
        i = pl.program_id(0)

        @pl.when(i == 0)
        def _():
            gw_ref[...] = jnp.zeros_like(gw_ref)
            loss_ref[...] = jnp.zeros_like(loss_ref)

        o = o_ref[...]
        w = w_ref[...]
        rstd = lax.rsqrt(jnp.mean(o * o, axis=-1, keepdims=True) + EPS)
        xhat = o * rstd
        x = jnp.concatenate([r[...] for r in x_refs], axis=0)
        t = jnp.concatenate([r[...] for r in t_refs], axis=0)
        chunk = _chunk_of_row(i, k)
        err = (x + xhat * w - t) * ((chunk > 0) & (chunk <= nx)).astype(F32)
        loss_ref[...] += 0.5 * jnp.sum(jnp.mean(err * err, axis=-1, keepdims=True), axis=0, keepdims=True)
        dy = err * (1.0 / D_MODEL)
        dy_ref[...] = dy
        gw_ref[...] += jnp.sum(dy * xhat, axis=0, keepdims=True)
        dxhat = dy * w
        do_ref[...] = (rstd * (dxhat - xhat * jnp.mean(dxhat * xhat, axis=-1, keepdims=True))).astype(BF16)

    row = pl.BlockSpec((k * CHUNK, D_MODEL), lambda i: (i, 0))
    return pl.pallas_call(
        body, name="post_loss", grid=(nc // k,),
        in_specs=[row] + _x_specs(nx, k) + _x_specs(nx, k) + [pl.BlockSpec((1, D_MODEL), lambda i: (0, 0))],
        out_specs=[row, row, pl.BlockSpec((1, D_MODEL), lambda i: (0, 0)), pl.BlockSpec((1, 128), lambda i: (0, 0))],
        out_shape=[
            jax.ShapeDtypeStruct((lp, D_MODEL), BF16),
            jax.ShapeDtypeStruct((lp, D_MODEL), F32),
            jax.ShapeDtypeStruct((1, D_MODEL), F32),
            jax.ShapeDtypeStruct((1, 128), F32),
        ],
        compiler_params=_params(("arbitrary",)),
    )(out, *([x2d] * k), *([target] * k), w)


def _prenorm_bwd(dhn, x2d, h0, dy, w):
    nx = x2d.shape[0] // CHUNK
    k = _tile(nx, (X_K, 4, 2, 1))
    rows = k * CHUNK

    def backward(h, dhn, w):
        rstd = lax.rsqrt(jnp.mean(h * h, axis=-1, keepdims=True) + EPS)
        xhat = h * rstd
        dxhat = dhn * w
        dh = rstd * (dxhat - xhat * jnp.mean(dxhat * xhat, axis=-1, keepdims=True))
        return dh, jnp.sum(dhn * xhat, axis=0, keepdims=True)

    def body(*refs):
        dhn_refs, dy_refs = refs[:k], refs[k:2 * k]
        x_ref, w_ref, gx_ref, gw_ref = refs[2 * k:]

        @pl.when(pl.program_id(0) == 0)
        def _():
            gw_ref[...] = jnp.zeros_like(gw_ref)

        dh, gw = backward(x_ref[...], jnp.concatenate([r[...] for r in dhn_refs], axis=0), w_ref[...])
        gx_ref[...] = dh + jnp.concatenate([r[...] for r in dy_refs], axis=0)
        gw_ref[...] += gw

    def body_meta(dhn_ref, h0_ref, w_ref, d0_ref, gw_ref):
        d0_ref[...], gw_ref[...] = backward(h0_ref[...], dhn_ref[...], w_ref[...])

    chunk_specs = [pl.BlockSpec((CHUNK, D_MODEL), functools.partial(lambda u, i: (k * i + 1 + u, 0), u)) for u in range(k)]
    first = pl.BlockSpec((CHUNK, D_MODEL), lambda i: (0, 0))
    vec = pl.BlockSpec((1, D_MODEL), lambda i: (0, 0))
    wide = pl.BlockSpec((rows, D_MODEL), lambda i: (i, 0))
    gx, gw_x = pl.pallas_call(
        body, name="prenorm_bwd", grid=(nx // k,),
        in_specs=chunk_specs + chunk_specs + [wide, vec],
        out_specs=[wide, vec],
        out_shape=[jax.ShapeDtypeStruct((nx * CHUNK, D_MODEL), F32), jax.ShapeDtypeStruct((1, D_MODEL), F32)],
        compiler_params=_params(("arbitrary",)),
    )(*([dhn] * k), *([dy] * k), x2d, w)
    d0, gw_0 = pl.pallas_call(
        body_meta, name="prenorm_bwd_meta", grid=(1,),
        in_specs=[first, first, vec], out_specs=[first, vec],
        out_shape=[jax.ShapeDtypeStruct((CHUNK, D_MODEL), F32), jax.ShapeDtypeStruct((1, D_MODEL), F32)],
        compiler_params=_params(("arbitrary",)),
    )(dhn, h0, w)
    return gx, d0, gw_x + gw_0


def _adamw(slabs, w, m, v, name):
    rows, cols = w.shape
    tr = _tile(rows, (256, 128, 64, 16, 8))
    c1 = 1.0 - ADAM_B1 ** ADAM_STEP
    c2 = 1.0 - ADAM_B2 ** ADAM_STEP

    def body(s_ref, w_ref, m_ref, v_ref, g_ref, d_ref, mo_ref, vo_ref):
        g = s_ref[0].astype(F32)
        for k in range(1, slabs.shape[0]):
            g = g + s_ref[k].astype(F32)
        w = w_ref[...]
        m = ADAM_B1 * m_ref[...] + (1.0 - ADAM_B1) * g
        v = ADAM_B2 * v_ref[...] + (1.0 - ADAM_B2) * (g * g)
        g_ref[...] = g
        mo_ref[...] = m
        vo_ref[...] = v
        d_ref[...] = -ADAM_LR * ((m / c1) / (jnp.sqrt(v / c2) + ADAM_EPS) + ADAM_WD * w)

    blk = pl.BlockSpec((tr, cols), lambda i: (i, 0))
    shape = jax.ShapeDtypeStruct((rows, cols), F32)
    return pl.pallas_call(
        body, name=name, grid=(rows // tr,),
        in_specs=[pl.BlockSpec((slabs.shape[0], tr, cols), lambda i: (0, i, 0)), blk, blk, blk],
        out_specs=[blk, blk, blk, blk],
        out_shape=[shape, shape, shape, shape],
        compiler_params=_params(("parallel",)),
    )(slabs, w, m, v)


def _perm_xbc(a):
    lead = a.shape[:-1]
    xs = a[..., :D_SSD].reshape(lead + (N_GROUPS, GROUP_W))
    b = a[..., D_SSD:D_SSD + N_GROUPS * D_STATE].reshape(lead + (N_GROUPS, D_STATE))
    c = a[..., D_SSD + N_GROUPS * D_STATE:].reshape(lead + (N_GROUPS, D_STATE))
    return jnp.concatenate([xs, b, c], axis=-1).reshape(lead + (D_CONV,))


def _unperm_xbc(a):
    lead = a.shape[:-1]
    t = a.reshape(lead + (N_GROUPS, XBC_BLK))
    xs = t[..., :GROUP_W].reshape(lead + (D_SSD,))
    b = t[..., GROUP_W:GROUP_W + D_STATE].reshape(lead + (N_GROUPS * D_STATE,))
    c = t[..., GROUP_W + D_STATE:].reshape(lead + (N_GROUPS * D_STATE,))
    return jnp.concatenate([xs, b, c], axis=-1)


R_Z, R_XBC, R_DT, R_Q, R_K, R_V, R_G = 0, 2048, 6144, 6176, 7200, 7456, 7712


def _internal_of_reference():
    ref = np.arange(D_IN_PROJ)
    out = np.empty(D_IN_PROJ, np.int64)
    out[R_Z:R_XBC] = OFF_Z + ref[:D_SSD]
    xs = np.arange(D_SSD)
    out[R_XBC:R_XBC + D_SSD] = OFF_XBC + (xs // GROUP_W) * XBC_BLK + xs % GROUP_W
    bc = np.arange(N_GROUPS * D_STATE)
    out[R_XBC + D_SSD:R_XBC + D_SSD + N_GROUPS * D_STATE] = OFF_XBC + (bc // D_STATE) * XBC_BLK + GROUP_W + bc % D_STATE
    out[R_XBC + D_SSD + N_GROUPS * D_STATE:R_DT] = OFF_XBC + (bc // D_STATE) * XBC_BLK + GROUP_W + D_STATE + bc % D_STATE
    out[R_DT:R_Q] = OFF_DT + np.arange(SSD_HEADS)
    out[R_Q:R_K] = OFF_Q + np.arange(D_ATT)
    out[R_K:R_V] = OFF_K + np.arange(D_KV)
    out[R_V:R_G] = OFF_V + np.arange(D_KV)
    out[R_G:] = OFF_G + np.arange(D_ATT)
    return out


def _runs(src, dst_break):
    runs, lo = [], 0
    for i in range(1, len(src) + 1):
        if i == len(src) or src[i] != src[i - 1] + 1 or dst_break[i] != dst_break[i - 1]:
            runs.append((lo, i))
            lo = i
    return runs


RELAYOUT_ROWS = 256


def _lane_window(ref, lead, c0, n):
    a0 = c0 // 128 * 128
    a1 = min(-(-(c0 + n) // 128) * 128, ref.shape[-1])
    return ref[lead + (slice(None), slice(a0, a1))][:, c0 - a0:c0 - a0 + n]


def _gather_w_in(w_shard, small, x2d, norm_w):
    int_of_ref = _internal_of_reference()
    ref_of_int = np.full(NP, -1, np.int64)
    ref_of_int[int_of_ref] = np.arange(D_IN_PROJ)
    shard = np.where(ref_of_int >= 0, ref_of_int // SHARD_IN, -1)
    src = np.where(ref_of_int >= 0, ref_of_int, -10 - 2 * np.arange(NP))
    plan, zeros = [], 0
    for lo, hi in _runs(src, shard):
        if ref_of_int[lo] < 0:
            zeros += hi - lo
            continue
        if zeros:
            plan.append((None, 0, zeros))
            zeros = 0
        plan.append((int(ref_of_int[lo] // SHARD_IN), int(ref_of_int[lo] % SHARD_IN), hi - lo))
    if zeros:
        plan.append((None, 0, zeros))
    tr = RELAYOUT_ROWS
    steps = D_MODEL // tr
    side = _GatherTwoLevel(small)
    ns = side.n
    nx = x2d.shape[0] // CHUNK
    nc = nx + 2
    k = _tile(nc, (11, ROW_K, 3, 2, 1))
    pn = nc // k
    n = max(pn, steps)
    tile_of = lambda i: jnp.where(i < pn - 1, i + 1, 0)

    def body(w_ref, *rest):
        s_in, x_refs, nw_ref = rest[:ns], rest[ns:ns + k], rest[ns + k]
        o_ref, land_ref, s_out, hn_ref = rest[ns + k + 1], rest[ns + k + 2], rest[ns + k + 3:2 * ns + k + 3], rest[2 * ns + k + 3]
        tile_buf, send_sems, recv_sems, load_sems, local_sem, meta_buf, meta_sem = rest[2 * ns + k + 4:2 * ns + k + 11]
        side_sems = rest[2 * ns + k + 11:]
        i = pl.program_id(0)
        x, y, c = lax.axis_index("x"), lax.axis_index("y"), lax.axis_index("c")
        me, sibling = (x, y, c), (x, y, 1 - c)
        chips = [(1 - x, y), (x, 1 - y), (1 - x, 1 - y)]

        def prenorm_tile(t, head=None):
            h = jnp.concatenate([r[...] for r in x_refs], axis=0)
            if head is not None:
                h = jnp.concatenate([head, h[CHUNK:, :]], axis=0)
            h = h * (_chunk_of_row(t, k) <= nx).astype(F32)
            rstd = lax.rsqrt(jnp.mean(h * h, axis=-1, keepdims=True) + EPS)
            hn_ref[...] = (h * rstd * nw_ref[...]).astype(BF16)

        if pn > 1:
            @pl.when(i < pn - 1)
            def _():
                prenorm_tile(i + 1)

        def rows(t):
            return pl.ds(t * tr, tr)

        def tile(place, t):
            return land_ref.at[4 * place[0] + 2 * place[1] + place[2], rows(t), :]

        def copy(t, k, block, to, src=None):
            return pltpu.make_async_remote_copy(
                src_ref=tile(block, t) if src is None else src, dst_ref=tile(block, t),
                send_sem=send_sems.at[t, k], recv_sem=recv_sems.at[t, k], device_id=to, device_id_type=pl.DeviceIdType.MESH)

        def own_sends(t):
            mine = w_ref.at[rows(t), :]
            return [copy(t, 0, me, sibling, src=mine)] + [copy(t, 1 + j, me, (*chip, c), src=mine)
                                                          for j, chip in enumerate(chips)]

        own = pltpu.make_async_copy(w_ref, land_ref.at[4 * x + 2 * y + c], local_sem)

        @pl.when(i < steps)
        def _():
            @pl.when(i == 0)
            def _():
                side.start(s_in, s_out, side_sems)
                own.start()
                for t in range(steps):
                    for cp in own_sends(t):
                        cp.start()
                own.wait()

            forwards = []
            for j, chip in enumerate(chips):
                copy(i, 1 + j, (*chip, c), me).wait_recv()
                forwards.append(copy(i, 4 + j, (*chip, c), sibling))
                forwards[j].start()
            copy(i, 0, sibling, me).wait_recv()
            for j, chip in enumerate(chips):
                copy(i, 4 + j, (*chip, 1 - c), me).wait_recv()
            loads = [pltpu.make_async_copy(land_ref.at[s, rows(i), :], tile_buf.at[s], load_sems.at[s]) for s in range(N_DEV)]
            for cp in loads:
                cp.start()
            for cp in loads:
                cp.wait()
            o_ref[...] = jnp.concatenate(
                [jnp.zeros((tr, w), o_ref.dtype) if s is None else _lane_window(tile_buf, (s,), c0, w) for s, c0, w in plan],
                axis=1)
            for cp in forwards:
                cp.wait_send()

            @pl.when(i == steps - 1)
            def _():
                for t in range(steps):
                    for cp in own_sends(t):
                        cp.wait_send()
                side.wait(s_in, s_out, side_sems)

        @pl.when(i == n - 1)
        def _():
            meta = pltpu.make_async_copy(s_out[ns - 1], meta_buf, meta_sem)
            meta.start()
            meta.wait()
            tokens = jnp.concatenate([meta_buf[s] for s in range(N_DEV)], axis=1)
            prenorm_tile(0, jnp.concatenate([jnp.zeros((PAD_LEAD, D_MODEL), F32), tokens], axis=0))

    last = steps - 1
    res = pl.pallas_call(
        body, name="gather_w_in", grid=(n,),
        in_specs=[ANY] + side.in_specs + _x_specs(nx, k, tile_of) + [pl.BlockSpec((1, D_MODEL), lambda i: (0, 0))],
        out_specs=[pl.BlockSpec((tr, NP), lambda i: (jnp.minimum(i, last), 0)), ANY] + side.out_specs + [
            pl.BlockSpec((k * CHUNK, D_MODEL), lambda i: (tile_of(i), 0))],
        out_shape=[jax.ShapeDtypeStruct((D_MODEL, NP), w_shard.dtype),
                   jax.ShapeDtypeStruct((N_DEV, D_MODEL, SHARD_IN), w_shard.dtype)] + side.out_shape + [
            jax.ShapeDtypeStruct((nc * CHUNK, D_MODEL), BF16)],
        scratch_shapes=[pltpu.VMEM((N_DEV, tr, SHARD_IN), w_shard.dtype),
                        pltpu.SemaphoreType.DMA((steps, N_DEV - 1)), pltpu.SemaphoreType.DMA((steps, N_DEV - 1)),
                        pltpu.SemaphoreType.DMA((N_DEV,)), pltpu.SemaphoreType.DMA(()),
                        pltpu.VMEM((N_DEV,) + tuple(small[ns - 1].shape), F32),
                        pltpu.SemaphoreType.DMA(())] + side.scratch,
        compiler_params=_params(("arbitrary",), side_effects=True),
    )(w_shard, *side.operands, *([x2d] * k), norm_w)
    return [res[0]] + list(res[2:])


def _w_in_chip_slabs(dw):
    int_of_ref = _internal_of_reference()
    plan = []
    for s in range(N_DEV):
        cols = int_of_ref[s * SHARD_IN:(s + 1) * SHARD_IN]
        plan.append([(int(cols[lo]), hi - lo) for lo, hi in _runs(cols, np.zeros_like(cols))])
    tr = RELAYOUT_ROWS
    steps = D_MODEL // tr
    pairs = N_DEV // 2

    def body(dw_ref, o_ref, land_ref, send_buf, land_buf, send_sems, recv_sems, load_sems):
        i = pl.program_id(0)
        x, y, c = lax.axis_index("x"), lax.axis_index("y"), lax.axis_index("c")
        slot = i % 2

        def sends(step, sl):
            return [pltpu.make_async_remote_copy(
                src_ref=send_buf.at[sl, 2 * q + (1 - c)], dst_ref=land_ref.at[q, pl.ds(step * tr, tr), :],
                send_sem=send_sems.at[sl, q], recv_sem=recv_sems.at[step, q],
                device_id=(x, y, 1 - c), device_id_type=pl.DeviceIdType.MESH) for q in range(pairs)]

        @pl.when(i < steps)
        def _():
            @pl.when(i >= 2)
            def _():
                for cp in sends(i - 2, slot):
                    cp.wait_send()

            for s in range(N_DEV):
                send_buf[slot, s] = jnp.concatenate([_lane_window(dw_ref, (), c0, n) for c0, n in plan[s]], axis=1)
            for cp in sends(i, slot):
                cp.start()

        @pl.when(i >= 1)
        def _():
            loads = []
            for q, cp in enumerate(sends(i - 1, 1 - slot)):
                cp.wait_recv()
                loads.append(pltpu.make_async_copy(land_ref.at[q, pl.ds((i - 1) * tr, tr), :], land_buf.at[q], load_sems.at[q]))
                loads[q].start()
            for q in range(pairs):
                loads[q].wait()
                o_ref[q] = (send_buf[1 - slot, 2 * q + c].astype(F32) + land_buf[q].astype(F32)).astype(o_ref.dtype)

        @pl.when(i == steps)
        def _():
            for cp in sends(steps - 2, steps % 2) + sends(steps - 1, 1 - steps % 2):
                cp.wait_send()

    last = steps - 1
    return pl.pallas_call(
        body, name="dw_in_relayout", grid=(steps + 1,),
        in_specs=[pl.BlockSpec((tr, NP), lambda i: (jnp.minimum(i, last), 0))],
        out_specs=[pl.BlockSpec((pairs, tr, SHARD_IN), lambda i: (0, jnp.maximum(i - 1, 0), 0)), ANY],
        out_shape=[jax.ShapeDtypeStruct((pairs, D_MODEL, SHARD_IN), dw.dtype),
                   jax.ShapeDtypeStruct((pairs, D_MODEL, SHARD_IN), dw.dtype)],
        scratch_shapes=[pltpu.VMEM((2, N_DEV, tr, SHARD_IN), dw.dtype), pltpu.VMEM((pairs, tr, SHARD_IN), dw.dtype),
                        pltpu.SemaphoreType.DMA((2, pairs)), pltpu.SemaphoreType.DMA((steps, pairs)),
                        pltpu.SemaphoreType.DMA((pairs,))],
        compiler_params=_params(("arbitrary",), side_effects=True),
    )(dw)[0]


def _rep_heads(a):
    return jnp.repeat(a, HEAD_DIM, axis=1)


SMALL = (("norm_pre_w", 2048), ("conv_b", 4096), ("dt_bias", 32), ("a_log", 32), ("d_skip", 32),
         ("ssd_norm_w", 2048), ("attn_sinks", 16), ("norm_post_w", 2048))
SMALL_USED = sum(size for _, size in SMALL)
SMALL_LEN = 10368


def _pack_small(d, loss=None):
    parts = [d[name].reshape(1, size) for name, size in SMALL]
    tail = jnp.zeros((1, SMALL_LEN - SMALL_USED), F32)
    if loss is not None:
        tail = tail.at[0, 0].set(loss)
    return jnp.concatenate(parts + [tail], axis=1)


def _unpack_small(vec):
    out, off = {}, 0
    for name, size in SMALL:
        out[name] = vec[:, off:off + size]
        off += size
    return out


def kernel(x, meta_tokens, norm_pre_w, w_in, conv_w, conv_b, dt_bias, a_log, d_skip, ssd_norm_w, attn_sinks, w_out, norm_post_w, loss_target, m_meta_tokens, m_norm_pre_w, m_w_in, m_conv_w, m_conv_b, m_dt_bias, m_a_log, m_d_skip, m_ssd_norm_w, m_attn_sinks, m_w_out, m_norm_post_w, v_meta_tokens, v_norm_pre_w, v_w_in, v_conv_w, v_conv_b, v_dt_bias, v_a_log, v_d_skip, v_ssd_norm_w, v_attn_sinks, v_w_out, v_norm_post_w):
    seq = x.shape[1]
    lp = seq + 2 * CHUNK
    x2d = x[0]

    w_all, conv_w_g, meta_g, hn = _gather_w_in(
        w_in[0].astype(BF16), [conv_w[0], meta_tokens], x2d, norm_pre_w)
    conv_w_full = _perm_xbc(jnp.transpose(conv_w_g, (1, 0, 2)).reshape(CONV_WIDTH, D_CONV))
    conv_b_int = _perm_xbc(conv_b)
    meta_full = jnp.transpose(meta_g, (1, 0, 2)).reshape(N_META, D_MODEL)
    h0 = jnp.concatenate([jnp.zeros((PAD_LEAD, D_MODEL), F32), meta_full], axis=0)

    pos = (jnp.arange(lp) - PAD_LEAD).astype(F32)
    half = HEAD_DIM // 2
    inv = ROPE_THETA ** (-jnp.arange(half, dtype=F32) / half)
    ang = pos[:, None] * inv[None, :]
    cos_t = jnp.tile(jnp.cos(ang), (1, 4))
    sin_t = jnp.tile(jnp.concatenate([-jnp.sin(ang), jnp.sin(ang)], axis=1), (1, 2))
    head_of_col = np.arange(D_SSD) // HEAD_DIM
    expand = jnp.asarray((np.arange(128)[:, None] == head_of_col[None, :]).astype(np.float32))
    reduce_t = jnp.asarray((head_of_col[:, None] == np.arange(128)[None, :]).astype(np.float32))
    tri = jnp.asarray(np.tril(np.ones((CHUNK, CHUNK), np.float32)))
    a_rep = _rep_heads(-jnp.exp(a_log))
    dsk_rep = _rep_heads(d_skip)
    dt_bias_pad = jnp.pad(dt_bias, ((0, 0), (0, 128 - SSD_HEADS)))
    sink_stack = jnp.repeat(attn_sinks.reshape(KV_HEADS, REP), CHUNK, axis=1).reshape(KV_HEADS, REP * CHUNK, 1)

    tm = _tile(lp, (1056, 704, 128, 64))
    proj, w_out_g = _matmul(hn, w_all, tm=tm, tn=1536, tk=D_MODEL, out_dtype=F32, name="in_proj",
                            comm=_Comm([(w_out[0].astype(BF16), "gather")]))
    w_out_full = w_out_g.reshape(D_MIX, D_MODEL)
    qr, kr, vb, dt_rep = _act_fwd(proj, cos_t, sin_t, expand, dt_bias_pad)
    mix, ytot, hprev, xbc, dsilu = _ssd_fwd(proj, conv_w_full, conv_b_int, dt_rep, a_rep, dsk_rep, ssd_norm_w, tri)
    att, mix = _attn_fwd(qr, kr, vb, proj, sink_stack, mix)
    out = _matmul(mix, w_out_full, tm=tm, tn=1024, tk=D_MIX, out_dtype=F32, name="out_proj")
    dout, dy, g_norm_post, loss_part = _post_loss(out, x2d, loss_target[0], norm_post_w)

    dmix = _matmul(dout, w_out_full, trans_b=True, tm=tm, tn=1024, tk=D_MODEL, out_dtype=F32, name="dmix")
    dw_out = _matmul(mix, dout, trans_a=True, tm=512, tn=1024, tk=lp, out_dtype=BF16, name="dw_out")
    dqr, dkr, dv, dg, dsink_rows = _attn_bwd(qr, kr, vb, att, proj, dmix, sink_stack)
    dproj, dpre, ddt_part, dd_part, da_part, g_ssd_norm, g_out = _ssd_bwd(
        dmix, ytot, proj, xbc, dsilu, dt_rep, hprev, a_rep, dsk_rep, ssd_norm_w, tri,
        _Comm([(dw_out.reshape(N_DEV, D_MIX // N_DEV, D_MODEL), "scatter")]))
    dproj, dconv_w_int, dconv_b_int = _conv_bwd(dpre, proj, conv_w_full, dproj)
    dproj, ddt_bias = _act_bwd(dqr, dkr, dv, dg, ddt_part, cos_t, sin_t, reduce_t, dproj)
    dw_all = _matmul(hn, dproj, trans_a=True, tm=512, tn=1024, tk=lp, out_dtype=BF16, name="dw_in")
    dw_chip = _w_in_chip_slabs(dw_all)
    dhn, g_in = _matmul(dproj, w_all, trans_b=True, tm=tm, tn=1024, tk=1536, out_dtype=F32, name="dhn",
                        comm=_Comm([(dw_chip, "scatter")], scope="chips"))
    grad_x, dh0, g_norm_pre = _prenorm_bwd(dhn, x2d, h0, dy, norm_pre_w)

    dmeta = dh0[PAD_LEAD:, :]
    dconv_w_ref = _unperm_xbc(dconv_w_int)
    heads = lambda part: part.reshape(SSD_HEADS, HEAD_DIM).sum(axis=1).reshape(1, SSD_HEADS)
    small_local = _pack_small({
        "norm_pre_w": g_norm_pre, "conv_b": _unperm_xbc(dconv_b_int), "dt_bias": ddt_bias[:, :SSD_HEADS],
        "a_log": heads(da_part) * (-jnp.exp(a_log)), "d_skip": heads(dd_part), "ssd_norm_w": g_ssd_norm,
        "attn_sinks": dsink_rows.reshape(Q_HEADS, CHUNK).sum(axis=1).reshape(1, Q_HEADS),
        "norm_post_w": g_norm_post}, loss=loss_part[0, 0])
    g_conv, g_meta, g_small = _exchange(
        [(jnp.transpose(dconv_w_ref.reshape(CONV_WIDTH, N_DEV, D_CONV // N_DEV), (1, 0, 2)), "scatter"),
         (jnp.transpose(dmeta.reshape(N_META, N_DEV, D_MODEL // N_DEV), (1, 0, 2)), "scatter"),
         (small_local, "gather")], "exchange_small")

    res = {}
    res["w_in"] = [o[None] for o in _adamw(g_in, w_in[0], m_w_in[0], v_w_in[0], "adamw_w_in")]
    res["w_out"] = [o[None] for o in _adamw(g_out, w_out[0], m_w_out[0], v_w_out[0], "adamw_w_out")]
    res["conv_w"] = [o[None] for o in _adamw(g_conv, conv_w[0], m_conv_w[0], v_conv_w[0], "adamw_conv_w")]
    res["meta_tokens"] = _adamw(g_meta, meta_tokens, m_meta_tokens, v_meta_tokens, "adamw_meta")
    given = dict(norm_pre_w=(norm_pre_w, m_norm_pre_w, v_norm_pre_w), conv_b=(conv_b, m_conv_b, v_conv_b),
                 dt_bias=(dt_bias, m_dt_bias, v_dt_bias), a_log=(a_log, m_a_log, v_a_log),
                 d_skip=(d_skip, m_d_skip, v_d_skip), ssd_norm_w=(ssd_norm_w, m_ssd_norm_w, v_ssd_norm_w),
                 attn_sinks=(attn_sinks, m_attn_sinks, v_attn_sinks),
                 norm_post_w=(norm_post_w, m_norm_post_w, v_norm_post_w))
    packed = [_pack_small({k: t[j] for k, t in given.items()}) for j in range(3)]
    small_out = _adamw(g_small, packed[0], packed[1], packed[2], "adamw_small")
    small_res = [_unpack_small(r) for r in small_out]
    loss = small_out[0][0, SMALL_USED]

    order = ["meta_tokens", "norm_pre_w", "w_in", "conv_w", "conv_b", "dt_bias", "a_log", "d_skip", "ssd_norm_w",
             "attn_sinks", "w_out", "norm_post_w"]
    outs = []
    for j in range(4):
        for name in order:
            outs.append(res[name][j] if name in res else small_res[j][name])
    return (loss, grad_x[None], *outs)
```

```python
import functools

import numpy as np
import jax
import jax.numpy as jnp
from jax import lax
from jax.experimental import pallas as pl
from jax.experimental.pallas import tpu as pltpu

F32 = jnp.float32
BF16 = jnp.bfloat16
HIGHEST = lax.Precision.HIGHEST

N_DEV = 8
D_MODEL = 2048
CHUNK = 64
N_META = 16
PAD_LEAD = CHUNK - N_META
EPS = 1e-6
N_GROUPS = 8
HEADS_PER_GROUP = 4
HEAD_DIM = 64
GROUP_W = HEADS_PER_GROUP * HEAD_DIM
D_STATE = 128
D_SSD = 2048
D_CONV = 4096
SSD_HEADS = 32
CONV_WIDTH = 4
Q_HEADS = 16
KV_HEADS = 4
REP = 4
D_ATT = 1024
D_KV = 256
WINDOW_CHUNKS = 2
BAND = (WINDOW_CHUNKS + 1) * CHUNK
ROPE_THETA = 10000.0
D_MIX = D_SSD + D_ATT
D_IN_PROJ = 8736
SHARD_IN = D_IN_PROJ // N_DEV

OFF_Z, OFF_XBC, OFF_Q, OFF_G, OFF_K, OFF_V, OFF_DT = 0, 2048, 6144, 7168, 8192, 8448, 8704
NP = 9216
TAIL_W = NP - OFF_Q
XBC_BLK = 512
SSD_GPS = 8
SSD_CPS = 2

ADAM_LR, ADAM_B1, ADAM_B2, ADAM_EPS, ADAM_WD, ADAM_STEP = 0.001, 0.9, 0.999, 1e-08, 0.01, 10

VMEM_LIMIT = 48 * 1024 * 1024

NN = (((1,), (0,)), ((), ()))
NT = (((1,), (1,)), ((), ()))
TN = (((0,), (0,)), ((), ()))
ANY = pl.BlockSpec(memory_space=pl.ANY)


def _dot(a, b, dims=NN, precision=None):
    return lax.dot_general(a, b, dims, precision=precision, preferred_element_type=F32)


def _tile(n, prefs):
    for t in prefs:
        if n % t == 0:
            return t
    return n


def _params(sem, vmem=VMEM_LIMIT, side_effects=False):
    return pltpu.CompilerParams(dimension_semantics=sem, vmem_limit_bytes=vmem, has_side_effects=side_effects)


def _sigmoid(x):
    return 1.0 / (1.0 + jnp.exp(-x))


class _Comm:
    def __init__(self, items, scope="devices"):
        self.items = items
        self.scope = scope
        self.slabs = slabs = N_DEV if scope == "devices" else N_DEV // 2
        self.n = n = len(items)
        self.operands = [arr for arr, _ in items]
        self.in_specs = [ANY] * n
        self.out_specs = [ANY] * n
        self.out_shape = [jax.ShapeDtypeStruct((slabs,) + tuple(arr.shape) if kind == "gather" else tuple(arr.shape),
                                               arr.dtype) for arr, kind in items]
        self.scratch = [pltpu.SemaphoreType.DMA((n, slabs - 1)), pltpu.SemaphoreType.DMA((n, slabs - 1)),
                        pltpu.SemaphoreType.DMA((n,))]

    def _places(self):
        pos = (lax.axis_index("x"), lax.axis_index("y"), lax.axis_index("c"))
        if self.scope == "devices":
            index = lambda p: 4 * p[0] + 2 * p[1] + p[2]
            masks = range(1, N_DEV)
        else:
            index = lambda p: 2 * p[0] + p[1]
            masks = (2, 4, 6)
        peers = []
        for k in masks:
            p = tuple(1 - pos[b] if (k >> (2 - b)) & 1 else pos[b] for b in range(3))
            peers.append((p, index(p)))
        return index(pos), peers

    def _copies(self, ins, outs, sems, landed):
        send_sems, recv_sems, local_sems = sems
        me, peers = self._places()
        local, remote = [], []
        for a, (_, kind) in enumerate(self.items):
            own = ins[a] if kind == "gather" else ins[a].at[me]
            local.append(pltpu.make_async_copy(own, outs[a].at[me], local_sems.at[a]))
            for k, (p, pid) in enumerate(peers):
                remote.append(pltpu.make_async_remote_copy(
                    src_ref=ins[a] if kind == "gather" else ins[a].at[pid],
                    dst_ref=outs[a].at[pid if landed else me],
                    send_sem=send_sems.at[a, k], recv_sem=recv_sems.at[a, k],
                    device_id=p, device_id_type=pl.DeviceIdType.MESH))
        return local, remote

    def start(self, ins, outs, sems):
        local, remote = self._copies(ins, outs, sems, landed=False)
        for cp in local + remote:
            cp.start()

    def wait(self, ins, outs, sems):
        local, remote = self._copies(ins, outs, sems, landed=True)
        for cp in remote + local:
            cp.wait()


def _exchange(items, name):
    comm = _Comm(items)
    n = comm.n

    def body(*refs):
        ins, outs, sems = refs[:n], refs[n:2 * n], refs[2 * n:]
        comm.start(ins, outs, sems)
        comm.wait(ins, outs, sems)

    return pl.pallas_call(
        body, name=name, in_specs=comm.in_specs, out_specs=comm.out_specs, out_shape=comm.out_shape,
        scratch_shapes=comm.scratch, compiler_params=pltpu.CompilerParams(has_side_effects=True),
    )(*comm.operands)


class _GatherTwoLevel:
    def __init__(self, arrays):
        self.arrays = arrays
        self.n = n = len(arrays)
        self.operands = list(arrays)
        self.in_specs = [ANY] * n
        self.out_specs = [ANY] * n
        self.out_shape = [jax.ShapeDtypeStruct((N_DEV,) + tuple(a.shape), a.dtype) for a in arrays]
        self.scratch = [pltpu.SemaphoreType.DMA((n, N_DEV - 1)), pltpu.SemaphoreType.DMA((n, N_DEV - 1)),
                        pltpu.SemaphoreType.DMA((n,))]

    def _plan(self, ins, outs, sems):
        send_sems, recv_sems, local_sems = sems
        x, y, c = lax.axis_index("x"), lax.axis_index("y"), lax.axis_index("c")
        me, sibling = (x, y, c), (x, y, 1 - c)
        chips = [(1 - x, y), (x, 1 - y), (1 - x, 1 - y)]

        def slab(a, place):
            return outs[a].at[4 * place[0] + 2 * place[1] + place[2]]

        def copy(a, k, block, to, src=None):
            return pltpu.make_async_remote_copy(
                src_ref=slab(a, block) if src is None else src, dst_ref=slab(a, block),
                send_sem=send_sems.at[a, k], recv_sem=recv_sems.at[a, k],
                device_id=to, device_id_type=pl.DeviceIdType.MESH)

        own, mine = [], []
        for a in range(self.n):
            mine.append(pltpu.make_async_copy(ins[a], slab(a, me), local_sems.at[a]))
            own.append(copy(a, 0, me, sibling, src=ins[a]))
            own += [copy(a, 1 + j, me, (*chip, c), src=ins[a]) for j, chip in enumerate(chips)]
        return me, sibling, chips, c, copy, own, mine

    def start(self, ins, outs, sems):
        _, _, _, _, _, own, mine = self._plan(ins, outs, sems)
        for cp in mine + own:
            cp.start()

    def wait(self, ins, outs, sems):
        me, sibling, chips, c, copy, own, mine = self._plan(ins, outs, sems)
        forwards = []
        for j, chip in enumerate(chips):
            for a in range(self.n):
                copy(a, 1 + j, (*chip, c), me).wait_recv()
                fwd = copy(a, 4 + j, (*chip, c), sibling)
                fwd.start()
                forwards.append(fwd)
        for a in range(self.n):
            copy(a, 0, sibling, me).wait_recv()
            for j, chip in enumerate(chips):
                copy(a, 4 + j, (*chip, 1 - c), me).wait_recv()
        for cp in own + forwards:
            cp.wait_send()
        for loc in mine:
            loc.wait()


def _matmul(a, b, *, tm, tn, tk, out_dtype, name, trans_a=False, trans_b=False, comm=None):
    m, k = (a.shape[1], a.shape[0]) if trans_a else a.shape
    n = b.shape[0] if trans_b else b.shape[1]
    nk = k // tk
    dims = TN if trans_a else (NT if trans_b else NN)
    assert not (trans_a and trans_b)
    nc = comm.n if comm else 0
    grid = (m // tm, n // tn, nk)

    def body(*refs):
        a_ref, b_ref = refs[:2]
        cin = refs[2:2 + nc]
        o_ref = refs[2 + nc]
        cout = refs[3 + nc:3 + 2 * nc]
        scratch = refs[3 + 2 * nc:]
        sems = scratch[len(scratch) - 3:] if comm else None
        i, j, kk = pl.program_id(0), pl.program_id(1), pl.program_id(2)
        if comm:
            @pl.when((i == 0) & (j == 0) & (kk == 0))
            def _():
                comm.start(cin, cout, sems)

        if nk == 1:
            o_ref[...] = _dot(a_ref[...], b_ref[...], dims).astype(out_dtype)
        else:
            acc_ref = scratch[0]

            @pl.when(kk == 0)
            def _():
                acc_ref[...] = jnp.zeros_like(acc_ref)

            acc_ref[...] += _dot(a_ref[...], b_ref[...], dims)

            @pl.when(kk == nk - 1)
            def _():
                o_ref[...] = acc_ref[...].astype(out_dtype)

        if comm:
            @pl.when((i == grid[0] - 1) & (j == grid[1] - 1) & (kk == nk - 1))
            def _():
                comm.wait(cin, cout, sems)

    a_spec = (pl.BlockSpec((tk, tm), lambda i, j, kk: (kk, i)) if trans_a
              else pl.BlockSpec((tm, tk), lambda i, j, kk: (i, kk)))
    b_spec = (pl.BlockSpec((tn, tk), lambda i, j, kk: (j, kk)) if trans_b
              else pl.BlockSpec((tk, tn), lambda i, j, kk: (kk, j)))
    sem = ("arbitrary",) * 3 if comm else ("parallel", "parallel", "arbitrary")
    res = pl.pallas_call(
        body, name=name, grid=grid,
        in_specs=[a_spec, b_spec] + (comm.in_specs if comm else []),
        out_specs=[pl.BlockSpec((tm, tn), lambda i, j, kk: (i, j))] + (comm.out_specs if comm else []),
        out_shape=[jax.ShapeDtypeStruct((m, n), out_dtype)] + (comm.out_shape if comm else []),
        scratch_shapes=([] if nk == 1 else [pltpu.VMEM((tm, tn), F32)]) + (comm.scratch if comm else []),
        compiler_params=_params(sem, side_effects=bool(comm)),
    )(a, b, *(comm.operands if comm else []))
    return res if comm else res[0]


ROW_K = 6
X_K = 4


def _x_specs(nx, k, tile_of=lambda i: i):
    return [pl.BlockSpec((CHUNK, D_MODEL), functools.partial(lambda u, i: (jnp.clip(k * tile_of(i) + u - 1, 0, nx - 1), 0), u))
            for u in range(k)]


def _chunk_of_row(i, k):
    return k * i + lax.broadcasted_iota(jnp.int32, (k * CHUNK, 1), 0) // CHUNK


CONV_COLS = 512
HALO = 8


def _conv_pre(ext, w, b):
    taps = [ext[HALO:, :]] + [pltpu.roll(ext, j, 0)[HALO:, :] for j in range(1, CONV_WIDTH)]
    acc = b + w[3:4, :] * taps[0]
    for j in range(1, CONV_WIDTH):
        acc = acc + w[3 - j:4 - j, :] * taps[j]
    return acc


def _conv_bwd(dpre, proj, conv_w, dproj):
    lp = proj.shape[0]
    t = _tile(lp, (704, 384, 128, 64))
    hb = t // HALO
    nt = lp // t
    c0 = OFF_XBC // CONV_COLS

    def body(dp_ref, dpn_ref, u_ref, w_ref, _, du_ref, dw_ref, db_ref):
        i = pl.program_id(1)
        w = w_ref[...]
        u = u_ref[...].astype(F32)
        dpre = jnp.concatenate([dp_ref[...], dpn_ref[...] * (i < nt - 1).astype(F32)], axis=0)
        ahead = [dpre[:t, :]] + [pltpu.roll(dpre, t + HALO - j, 0)[:t, :] for j in range(1, CONV_WIDTH)]
        du = w[3:4, :] * ahead[0]
        for j in range(1, CONV_WIDTH):
            du = du + w[3 - j:4 - j, :] * ahead[j]
        du_ref[...] = du.astype(BF16)

        @pl.when(i == 0)
        def _():
            dw_ref[...] = jnp.zeros_like(dw_ref)
            db_ref[...] = jnp.zeros_like(db_ref)

        db_ref[...] += jnp.sum(ahead[0], axis=0, keepdims=True)
        for j in range(CONV_WIDTH):
            dw_ref[3 - j:4 - j, :] += jnp.sum(ahead[j] * u, axis=0, keepdims=True)

    nxt = lambda i: jnp.minimum((i + 1) * hb, lp // HALO - 1)
    return pl.pallas_call(
        body, name="conv_bwd", grid=(D_CONV // CONV_COLS, nt),
        in_specs=[
            pl.BlockSpec((t, CONV_COLS), lambda j, i: (i, j)),
            pl.BlockSpec((HALO, CONV_COLS), lambda j, i: (nxt(i), j)),
            pl.BlockSpec((t, CONV_COLS), lambda j, i: (i, c0 + j)),
            pl.BlockSpec((CONV_WIDTH, CONV_COLS), lambda j, i: (0, j)),
            ANY,
        ],
        out_specs=[
            pl.BlockSpec((t, CONV_COLS), lambda j, i: (i, c0 + j)),
            pl.BlockSpec((CONV_WIDTH, CONV_COLS), lambda j, i: (0, j)),
            pl.BlockSpec((1, CONV_COLS), lambda j, i: (0, j)),
        ],
        out_shape=[
            jax.ShapeDtypeStruct((lp, NP), BF16),
            jax.ShapeDtypeStruct((CONV_WIDTH, D_CONV), F32),
            jax.ShapeDtypeStruct((1, D_CONV), F32),
        ],
        input_output_aliases={4: 0},
        compiler_params=_params(("parallel", "arbitrary")),
    )(dpre, dpre, proj, conv_w, dproj)


def _swap_halves(t):
    w = t.shape[-1]
    lane = lax.broadcasted_iota(jnp.int32, t.shape, 1)
    return jnp.where((lane % HEAD_DIM) < HEAD_DIM // 2, pltpu.roll(t, w - HEAD_DIM // 2, 1),
                     pltpu.roll(t, HEAD_DIM // 2, 1))


def _act_fwd(proj, cos_t, sin_t, expand, dt_bias_pad):
    lp = proj.shape[0]
    t = _tile(lp, (384, 128, 64))

    def body(q_ref, k_ref, v_ref, dt_ref, cos_ref, sin_ref, ex_ref, bias_ref, qo_ref, ko_ref, vo_ref, dto_ref):
        i = pl.program_id(0)
        cos = cos_ref[...]
        sin = sin_ref[...]
        q = q_ref[...]
        qo_ref[...] = (q * jnp.tile(cos, (1, D_ATT // 128)) + _swap_halves(q) * jnp.tile(sin, (1, D_ATT // 128))).astype(BF16)
        k = k_ref[...]
        ko_ref[...] = (k * jnp.tile(cos, (1, D_KV // 128)) + _swap_halves(k) * jnp.tile(sin, (1, D_KV // 128))).astype(BF16)
        vo_ref[...] = v_ref[...].astype(BF16)
        raw = dt_ref[...] + bias_ref[...]
        sp = jnp.maximum(raw, 0.0) + jnp.log1p(jnp.exp(-jnp.abs(raw)))
        row = i * t + lax.broadcasted_iota(jnp.int32, sp.shape, 0)
        dto_ref[...] = _dot(jnp.where(row >= PAD_LEAD, sp, 0.0), ex_ref[...], NN, HIGHEST)

    return pl.pallas_call(
        body, name="act_fwd", grid=(lp // t,),
        in_specs=[
            pl.BlockSpec((t, D_ATT), lambda i: (i, OFF_Q // D_ATT)),
            pl.BlockSpec((t, D_KV), lambda i: (i, OFF_K // D_KV)),
            pl.BlockSpec((t, D_KV), lambda i: (i, OFF_V // D_KV)),
            pl.BlockSpec((t, 128), lambda i: (i, OFF_DT // 128)),
            pl.BlockSpec((t, 128), lambda i: (i, 0)),
            pl.BlockSpec((t, 128), lambda i: (i, 0)),
            pl.BlockSpec((128, D_SSD), lambda i: (0, 0)),
            pl.BlockSpec((1, 128), lambda i: (0, 0)),
        ],
        out_specs=[
            pl.BlockSpec((t, D_ATT), lambda i: (i, 0)),
            pl.BlockSpec((t, D_KV), lambda i: (i, 0)),
            pl.BlockSpec((t, D_KV), lambda i: (i, 0)),
            pl.BlockSpec((t, D_SSD), lambda i: (i, 0)),
        ],
        out_shape=[
            jax.ShapeDtypeStruct((lp, D_ATT), BF16),
            jax.ShapeDtypeStruct((lp, D_KV), BF16),
            jax.ShapeDtypeStruct((lp, D_KV), BF16),
            jax.ShapeDtypeStruct((lp, D_SSD), F32),
        ],
        compiler_params=_params(("parallel",)),
    )(proj, proj, proj, proj, cos_t, sin_t, expand, dt_bias_pad)


def _act_bwd(dqr, dkr, dv, dg, ddt_part, cos_t, sin_t, reduce_t, dproj):
    lp = dqr.shape[0]
    t = _tile(lp, (384, 128, 64))

    def body(dq_ref, dk_ref, dv_ref, dg_ref, ddt_ref, cos_ref, sin_ref, red_ref, _, o_ref, db_ref):
        i = pl.program_id(0)
        cos = cos_ref[...]
        sin = sin_ref[...]
        dq = dq_ref[...]
        dq = dq * jnp.tile(cos, (1, D_ATT // 128)) + _swap_halves(dq * jnp.tile(sin, (1, D_ATT // 128)))
        dk = dk_ref[...]
        dk = dk * jnp.tile(cos, (1, D_KV // 128)) + _swap_halves(dk * jnp.tile(sin, (1, D_KV // 128)))
        ddt = _dot(ddt_ref[...], red_ref[...], NN, HIGHEST)
        o_ref[...] = jnp.concatenate(
            [dq.astype(BF16), dg_ref[...].astype(BF16), dk.astype(BF16), dv_ref[...].astype(BF16), ddt.astype(BF16),
             jnp.zeros((t, NP - OFF_DT - 128), BF16)], axis=1)

        @pl.when(i == 0)
        def _():
            db_ref[...] = jnp.zeros_like(db_ref)

        db_ref[...] += jnp.sum(ddt, axis=0, keepdims=True)

    return pl.pallas_call(
        body, name="act_bwd", grid=(lp // t,),
        in_specs=[
            pl.BlockSpec((t, D_ATT), lambda i: (i, 0)),
            pl.BlockSpec((t, D_KV), lambda i: (i, 0)),
            pl.BlockSpec((t, D_KV), lambda i: (i, 0)),
            pl.BlockSpec((t, D_ATT), lambda i: (i, 0)),
            pl.BlockSpec((t, D_SSD), lambda i: (i, 0)),
            pl.BlockSpec((t, 128), lambda i: (i, 0)),
            pl.BlockSpec((t, 128), lambda i: (i, 0)),
            pl.BlockSpec((D_SSD, 128), lambda i: (0, 0)),
            ANY,
        ],
        out_specs=[pl.BlockSpec((t, TAIL_W), lambda i: (i, OFF_Q // TAIL_W)), pl.BlockSpec((1, 128), lambda i: (0, 0))],
        out_shape=[jax.ShapeDtypeStruct((lp, NP), BF16), jax.ShapeDtypeStruct((1, 128), F32)],
        input_output_aliases={8: 0},
        compiler_params=_params(("arbitrary",)),
    )(dqr, dkr, dv, dg, ddt_part, cos_t, sin_t, reduce_t, dproj)


def _cs_row(cs):
    row = lax.broadcasted_iota(jnp.int32, cs.shape, 0)
    lane = lax.broadcasted_iota(jnp.int32, cs.shape, 1)
    return jnp.sum(jnp.where(row == lane % HEAD_DIM, cs, 0.0), axis=0, keepdims=True)


def _ssd_fwd(proj, conv_w, conv_b, dt_rep, a_rep, dsk_rep, wn, tri):
    lp = proj.shape[0]
    nc = lp // CHUNK
    assert SSD_GPS == N_GROUPS
    gw = SSD_GPS * GROUP_W
    cps = _tile(nc, (SSD_CPS, 1))

    def body(rawa_ref, rawb_ref, cw_ref, cb_ref, dt_ref, z_ref, a_ref, dsk_ref, wn_ref, tri_ref,
             yn_ref, ytot_ref, hprev_ref, xbc_ref, dsilu_ref, h_scr, tail_scr):
        @pl.when(pl.program_id(1) == 0)
        def _():
            h_scr[...] = jnp.zeros_like(h_scr)
            tail_scr[...] = jnp.zeros_like(tail_scr)

        for sub in range(cps):
            rs = slice(sub * CHUNK, (sub + 1) * CHUNK)
            raw = jnp.concatenate([rawa_ref[rs, :], rawb_ref[rs, :]], axis=1)
            pre = _conv_pre(jnp.concatenate([tail_scr[...], raw], axis=0), cw_ref[...], cb_ref[...])
            tail_scr[...] = raw[CHUNK - HALO:, :]
            sg = _sigmoid(pre)
            xbc_ref[rs, :] = pre * sg
            dsilu_ref[rs, :] = sg * (1.0 + pre * (1.0 - sg))

        G = range(SSD_GPS)
        colsl = [slice(gi * GROUP_W, (gi + 1) * GROUP_W) for gi in G]
        rows4 = lax.broadcasted_iota(jnp.int32, (GROUP_W, GROUP_W), 0) // HEAD_DIM
        cols4 = lax.broadcasted_iota(jnp.int32, (GROUP_W, GROUP_W), 1) // HEAD_DIM
        lrow = lax.broadcasted_iota(jnp.int32, (CHUNK, GROUP_W), 0)
        lcol = lax.broadcasted_iota(jnp.int32, (CHUNK, GROUP_W), 1) % HEAD_DIM

        def one_chunk(sub):
            rs = slice(sub * CHUNK, (sub + 1) * CHUNK)
            xbc = [xbc_ref[rs, gi * XBC_BLK:(gi + 1) * XBC_BLK] for gi in G]
            dt = [dt_ref[rs, colsl[gi]] for gi in G]
            xs = [xbc[gi][:, :GROUP_W] for gi in G]
            b = [xbc[gi][:, GROUP_W:GROUP_W + D_STATE].astype(BF16) for gi in G]
            c = [xbc[gi][:, GROUP_W + D_STATE:].astype(BF16) for gi in G]
            hprev = [h_scr[gi] for gi in G]
            cs = [_dot(tri_ref[...], dt[gi] * a_ref[:, colsl[gi]], NN, HIGHEST) for gi in G]
            yoff = [_dot(c[gi], hprev[gi].astype(BF16)) for gi in G]
            cs_t = [_cs_row(cs[gi]) for gi in G]
            xdt = [xs[gi] * dt[gi] for gi in G]
            cs_last = [cs[gi][CHUNK - 1:CHUNK, :] for gi in G]
            st = [_dot(b[gi], (xdt[gi] * jnp.exp(cs_last[gi] - cs[gi])).astype(BF16), TN) for gi in G]
            cb4 = [_dot(c[gi], jnp.concatenate([b[gi]] * HEADS_PER_GROUP, axis=0), NT) for gi in G]
            m = [(cb4[gi] * jnp.exp(jnp.where(lrow >= lcol, cs[gi] - cs_t[gi], -jnp.inf))).astype(BF16) for gi in G]
            xbd = [jnp.where(rows4 == cols4, jnp.concatenate([xdt[gi].astype(BF16)] * HEADS_PER_GROUP, axis=0), 0.0)
                   for gi in G]
            ydiag = [_dot(m[gi], xbd[gi]) for gi in G]
            for gi in G:
                cols = colsl[gi]
                ytot = ydiag[gi] + yoff[gi] * jnp.exp(cs[gi]) + dsk_ref[:, cols] * xs[gi]
                z = z_ref[rs, cols]
                gz = ytot * (z * _sigmoid(z))
                rstd = lax.rsqrt(jnp.mean(gz * gz, axis=-1, keepdims=True) + EPS)
                hprev_ref[sub, gi] = hprev[gi]
                h_scr[gi] = hprev[gi] * jnp.exp(cs_last[gi]) + st[gi]
                ytot_ref[rs, cols] = ytot
                yn_ref[rs, cols] = (gz * rstd * wn_ref[:, cols]).astype(BF16)

        for sub in range(cps):
            one_chunk(sub)

    vec = pl.BlockSpec((1, gw), lambda g, c: (0, g))
    blk = pl.BlockSpec((cps * CHUNK, gw), lambda g, c: (c, g))
    xblk = pl.BlockSpec((cps * CHUNK, D_CONV), lambda g, c: (c, 0))
    half = D_CONV // 2
    return pl.pallas_call(
        body, name="ssd_fwd", grid=(N_GROUPS // SSD_GPS, nc // cps),
        in_specs=[
            pl.BlockSpec((cps * CHUNK, half), lambda g, c: (c, OFF_XBC // half)),
            pl.BlockSpec((cps * CHUNK, half), lambda g, c: (c, OFF_XBC // half + 1)),
            pl.BlockSpec((CONV_WIDTH, D_CONV), lambda g, c: (0, 0)),
            pl.BlockSpec((1, D_CONV), lambda g, c: (0, 0)),
            blk, blk, vec, vec, vec,
            pl.BlockSpec((CHUNK, CHUNK), lambda g, c: (0, 0)),
        ],
        out_specs=[blk, blk, pl.BlockSpec((cps, SSD_GPS, D_STATE, GROUP_W), lambda g, c: (c, g, 0, 0)), xblk, xblk],
        out_shape=[
            jax.ShapeDtypeStruct((lp, D_MIX), BF16),
            jax.ShapeDtypeStruct((lp, D_SSD), F32),
            jax.ShapeDtypeStruct((nc, N_GROUPS, D_STATE, GROUP_W), F32),
            jax.ShapeDtypeStruct((lp, D_CONV), F32),
            jax.ShapeDtypeStruct((lp, D_CONV), F32),
        ],
        scratch_shapes=[pltpu.VMEM((SSD_GPS, D_STATE, GROUP_W), F32), pltpu.VMEM((HALO, D_CONV), F32)],
        compiler_params=_params(("arbitrary", "arbitrary")),
    )(proj, proj, conv_w, conv_b, dt_rep, proj, a_rep, dsk_rep, wn, tri)


def _ssd_bwd(dmix, ytot, proj, xbc, dsilu, dt_rep, hprev, a_rep, dsk_rep, wn, tri, comm):
    lp = xbc.shape[0]
    nc = lp // CHUNK
    gps = SSD_GPS
    gw = gps * GROUP_W
    cps = _tile(nc, (SSD_CPS, 1))
    ncm = comm.n
    n_in, n_out = 11, 6
    grid = (N_GROUPS // gps, nc // cps)

    def all_groups(refs):
        (dyn_ref, ytot_ref, z_ref, xbc_ref, dsilu_ref, dt_ref, hprev_ref, a_ref, dsk_ref, wn_ref, tri_ref,
         dz_ref, dxbc_ref, ddt_ref, dd_ref, da_ref, dwn_ref, dh_scr) = refs
        G = range(gps)
        H = range(HEADS_PER_GROUP)
        cl = [slice(gi * GROUP_W, (gi + 1) * GROUP_W) for gi in G]
        hl = [slice(r * HEAD_DIM, (r + 1) * HEAD_DIM) for r in H]
        tri = tri_ref[...]
        xbc = [xbc_ref[:, gi * XBC_BLK:(gi + 1) * XBC_BLK] for gi in G]
        dt = [dt_ref[:, cl[gi]] for gi in G]
        a = [a_ref[:, cl[gi]] for gi in G]
        xs = [xbc[gi][:, :GROUP_W] for gi in G]
        bbf = [xbc[gi][:, GROUP_W:GROUP_W + D_STATE].astype(BF16) for gi in G]
        cbf = [xbc[gi][:, GROUP_W + D_STATE:].astype(BF16) for gi in G]
        hprev = [hprev_ref[gi] for gi in G]
        hbf = [hprev[gi].astype(BF16) for gi in G]
        dhn = [dh_scr[gi] for gi in G]
        dhnb = [dhn[gi].astype(BF16) for gi in G]
        cs = [_dot(tri, dt[gi] * a[gi], NN, HIGHEST) for gi in G]
        g = [_dot(cbf[gi], hbf[gi]) for gi in G]
        dxw = [_dot(bbf[gi], dhnb[gi]) for gi in G]
        cs_t = [_cs_row(cs[gi]) for gi in G]
        dy = []
        for gi in G:
            ytot = ytot_ref[:, cl[gi]]
            z = z_ref[:, cl[gi]]
            dyn = dyn_ref[:, cl[gi]]
            sz = _sigmoid(z)
            silu_z = z * sz
            gz = ytot * silu_z
            rstd = lax.rsqrt(jnp.mean(gz * gz, axis=-1, keepdims=True) + EPS)
            xhat = gz * rstd
            dwn_ref[:, cl[gi]] += jnp.sum(dyn * xhat, axis=0, keepdims=True)
            dxhat = dyn * wn_ref[:, cl[gi]]
            dgz = rstd * (dxhat - xhat * jnp.mean(dxhat * xhat, axis=-1, keepdims=True))
            dy.append(dgz * silu_z)
            dz_ref[:, cl[gi]] = (dgz * ytot * (sz * (1.0 + z * (1.0 - sz)))).astype(BF16)
            dd_ref[:, cl[gi]] += jnp.sum(dy[gi] * xs[gi], axis=0, keepdims=True)
        xdt = [xs[gi] * dt[gi] for gi in G]
        e = [jnp.exp(cs[gi]) for gi in G]
        cs_last = [cs[gi][CHUNK - 1:CHUNK, :] for gi in G]
        dte = [jnp.exp(cs_last[gi] - cs[gi]) for gi in G]
        cd = [jnp.exp(cs_last[gi]) for gi in G]
        dgb = [(dy[gi] * e[gi]).astype(BF16) for gi in G]
        dyb = [dy[gi].astype(BF16) for gi in G]
        xdtb = [xdt[gi].astype(BF16) for gi in G]
        dc = [_dot(dgb[gi], hbf[gi], NT) for gi in G]
        dhprev = [_dot(cbf[gi], dgb[gi], TN) for gi in G]
        db = [_dot((xdt[gi] * dte[gi]).astype(BF16), dhnb[gi], NT) for gi in G]
        row = lax.broadcasted_iota(jnp.int32, (CHUNK, CHUNK), 0)
        causal = row >= lax.broadcasted_iota(jnp.int32, (CHUNK, CHUNK), 1)
        cb = [_dot(cbf[gi], bbf[gi], NT) for gi in G]
        dm = [[_dot(dyb[gi][:, hl[r]], xdtb[gi][:, hl[r]], NT) for r in H] for gi in G]
        mb, dseg, dcbb = [], [], []
        for gi in G:
            mb.append([])
            dseg.append([])
            dcb = None
            for r in H:
                seg = cs[gi][:, r * HEAD_DIM:r * HEAD_DIM + 1] - cs_t[gi][:, hl[r]]
                lm = jnp.exp(jnp.where(causal, seg, -jnp.inf))
                m = cb[gi] * lm
                mb[gi].append(m.astype(BF16))
                dseg[gi].append(dm[gi][r] * m)
                dcb = dm[gi][r] * lm if r == 0 else dcb + dm[gi][r] * lm
            dcbb.append(dcb.astype(BF16))
        dxdt_diag = [[_dot(mb[gi][r], dyb[gi][:, hl[r]], TN) for r in H] for gi in G]
        ones = jnp.ones((CHUNK, HEAD_DIM), F32)
        colsum = [[_dot(dseg[gi][r], ones, TN, HIGHEST) for r in H] for gi in G]
        dc2 = [_dot(dcbb[gi], bbf[gi]) for gi in G]
        db2 = [_dot(dcbb[gi], cbf[gi], TN) for gi in G]
        dcs = []
        for gi in G:
            t_dte = dxw[gi] * xdt[gi] * dte[gi]
            dcs_last = (jnp.sum(dhn[gi] * hprev[gi], axis=0, keepdims=True) * cd[gi]
                        + jnp.sum(t_dte, axis=0, keepdims=True))
            diag = jnp.concatenate(
                [(jnp.sum(dseg[gi][r], axis=1, keepdims=True) - colsum[gi][r]) * (1.0 / HEAD_DIM) for r in H], axis=1)
            d = dy[gi] * g[gi] * e[gi] - t_dte + diag
            row = lax.broadcasted_iota(jnp.int32, d.shape, 0)
            dcs.append(d + jnp.where(row == CHUNK - 1, dcs_last, 0.0))
        dda = [_dot(tri, dcs[gi], TN, HIGHEST) for gi in G]
        for gi in G:
            dxdt = dxw[gi] * dte[gi] + jnp.concatenate(dxdt_diag[gi], axis=1)
            da_ref[:, cl[gi]] += jnp.sum(dda[gi] * dt[gi], axis=0, keepdims=True)
            ddt = dda[gi] * a[gi] + dxdt * xs[gi]
            dxs = dsk_ref[:, cl[gi]] * dy[gi] + dxdt * dt[gi]
            ddt_ref[:, cl[gi]] = ddt * (1.0 - jnp.exp(-dt[gi]))
            xcols = slice(gi * XBC_BLK, (gi + 1) * XBC_BLK)
            dxbc_ref[:, xcols] = jnp.concatenate([dxs, db[gi] + db2[gi], dc[gi] + dc2[gi]], axis=1) * dsilu_ref[:, xcols]
            dh_scr[gi] = dhprev[gi] + dhn[gi] * cd[gi]

    def body(*refs):
        ins = refs[:n_in]
        cin = refs[n_in:n_in + ncm]
        outs = refs[n_in + ncm:n_in + ncm + n_out]
        cout = refs[n_in + ncm + n_out:n_in + 2 * ncm + n_out]
        dh_scr = refs[n_in + 2 * ncm + n_out]
        sems = refs[n_in + 2 * ncm + n_out + 1:]
        g, c = pl.program_id(0), pl.program_id(1)

        @pl.when((g == 0) & (c == 0))
        def _():
            comm.start(cin, cout, sems)

        @pl.when(c == 0)
        def _():
            dh_scr[...] = jnp.zeros_like(dh_scr)
            for ref in outs[3:]:
                ref[...] = jnp.zeros_like(ref)

        for sub in reversed(range(cps)):
            rs = pl.ds(sub * CHUNK, CHUNK)
            chunk_ins = tuple(r.at[rs] for r in ins[:6]) + (ins[6].at[sub],) + ins[7:]
            chunk_outs = tuple(r.at[rs] for r in outs[:3]) + outs[3:]
            all_groups(chunk_ins + chunk_outs + (dh_scr,))

        @pl.when((g == grid[0] - 1) & (c == grid[1] - 1))
        def _():
            comm.wait(cin, cout, sems)

    rev = lambda c: grid[1] - 1 - c
    vec = pl.BlockSpec((1, gw), lambda g, c: (0, g))
    blk = pl.BlockSpec((cps * CHUNK, gw), lambda g, c: (rev(c), g))
    xblk = pl.BlockSpec((cps * CHUNK, gps * XBC_BLK), lambda g, c: (rev(c), g))
    res = pl.pallas_call(
        body, name="ssd_bwd", grid=grid,
        in_specs=[blk, blk, blk, xblk, xblk, blk,
                  pl.BlockSpec((cps, gps, D_STATE, GROUP_W), lambda g, c: (rev(c), g, 0, 0)),
                  vec, vec, vec,
                  pl.BlockSpec((CHUNK, CHUNK), lambda g, c: (0, 0))] + comm.in_specs,
        out_specs=[blk, xblk, blk, vec, vec, vec] + comm.out_specs,
        out_shape=[
            jax.ShapeDtypeStruct((lp, NP), BF16),
            jax.ShapeDtypeStruct((lp, D_CONV), F32),
            jax.ShapeDtypeStruct((lp, D_SSD), F32),
            jax.ShapeDtypeStruct((1, D_SSD), F32),
            jax.ShapeDtypeStruct((1, D_SSD), F32),
            jax.ShapeDtypeStruct((1, D_SSD), F32),
        ] + comm.out_shape,
        scratch_shapes=[pltpu.VMEM((gps, D_STATE, GROUP_W), F32)] + comm.scratch,
        compiler_params=_params(("arbitrary", "arbitrary"), side_effects=True),
    )(dmix, ytot, proj, xbc, dsilu, dt_rep, hprev, a_rep, dsk_rep, wn, tri, *comm.operands)
    return res


def _stack_heads(t, h):
    return jnp.concatenate([t[:, (REP * h + r) * HEAD_DIM:(REP * h + r + 1) * HEAD_DIM] for r in range(REP)], axis=0)


def _band(t2, t1, t0, h):
    sl = slice(h * HEAD_DIM, (h + 1) * HEAD_DIM)
    return jnp.concatenate([t2[:, sl], t1[:, sl], t0[:, sl]], axis=0)


def _attn_probs(s, sink, qc):
    s = s * (HEAD_DIM ** -0.5)
    key_abs = (qc - WINDOW_CHUNKS) * CHUNK + lax.broadcasted_iota(jnp.int32, s.shape, 1)
    s = jnp.where(key_abs >= PAD_LEAD, s, -jnp.inf)
    m = jnp.maximum(jnp.max(s, axis=-1, keepdims=True), sink)
    p = jnp.exp(s - m)
    ps = jnp.exp(sink - m)
    denom = jnp.sum(p, axis=-1, keepdims=True) + ps
    return p / denom, ps / denom


ATT_QC_FWD = 6
ATT_QC_BWD = 2


def _kv_specs(width, newest_chunk_of, kc):
    return [pl.BlockSpec((CHUNK, width), functools.partial(lambda j, p: (jnp.maximum(newest_chunk_of(p) - j, 0), 0), j))
            for j in range(kc - 1, -1, -1)]


def _attn_fwd(qr, kr, vb, proj, sink_stack, mix):
    lp = qr.shape[0]
    nc = lp // CHUNK
    qn = _tile(nc, (ATT_QC_FWD, 2, 1))
    kn = WINDOW_CHUNKS + qn
    qrows = qn * CHUNK

    def body(q_ref, *rest):
        k_refs, v_refs = rest[:kn], rest[kn:2 * kn]
        g_ref, sink_ref, _, att_ref, mix_ref = rest[2 * kn:]
        p = pl.program_id(0)
        q = q_ref[...]
        ks = [r[...] for r in k_refs]
        vs = [r[...] for r in v_refs]
        units = [(u, h) for u in range(qn) for h in range(KV_HEADS)]
        s = [_dot(_stack_heads(q[u * CHUNK:(u + 1) * CHUNK, :], h), _band(*ks[u:u + 3], h), NT) for u, h in units]
        vbh = [_band(*vs[u:u + 3], h) for u, h in units]
        pn = [_attn_probs(s[i], sink_ref[h], qn * p + u)[0].astype(BF16) for i, (u, h) in enumerate(units)]
        o = [_dot(pn[i], vbh[i]) for i in range(len(units))]
        att = jnp.concatenate(
            [jnp.concatenate([o[u * KV_HEADS + h][r * CHUNK:(r + 1) * CHUNK, :] for h in range(KV_HEADS) for r in range(REP)],
                             axis=1) for u in range(qn)], axis=0)
        att_ref[...] = att
        g = g_ref[...]
        mix_ref[...] = (att * (g * _sigmoid(g))).astype(BF16)

    newest = lambda p: qn * p + qn - 1
    return pl.pallas_call(
        body, name="attn_fwd", grid=(nc // qn,),
        in_specs=[pl.BlockSpec((qrows, D_ATT), lambda p: (p, 0))] + _kv_specs(D_KV, newest, kn) + _kv_specs(D_KV, newest, kn) + [
            pl.BlockSpec((qrows, D_ATT), lambda p: (p, OFF_G // D_ATT)),
            pl.BlockSpec((KV_HEADS, REP * CHUNK, 1), lambda p: (0, 0, 0)),
            ANY,
        ],
        out_specs=[pl.BlockSpec((qrows, D_ATT), lambda p: (p, 0)),
                   pl.BlockSpec((qrows, D_ATT), lambda p: (p, D_SSD // D_ATT))],
        out_shape=[jax.ShapeDtypeStruct((lp, D_ATT), F32), jax.ShapeDtypeStruct((lp, D_MIX), BF16)],
        input_output_aliases={2 * kn + 3: 1},
        compiler_params=_params(("parallel",)),
    )(qr, *([kr] * kn), *([vb] * kn), proj, sink_stack, mix)


def _attn_bwd(qr, kr, vb, att, proj, dmix, sink_stack):
    lp = qr.shape[0]
    nc = lp // CHUNK
    qn = ATT_QC_BWD
    kn = WINDOW_CHUNKS + qn
    assert nc % qn == 0 and WINDOW_CHUNKS % qn == 0
    steps = nc // qn
    qrows = qn * CHUNK
    wrows = kn * CHUNK

    def body(q_ref, *rest):
        k_refs, v_refs = rest[:kn], rest[kn:2 * kn]
        (att_ref, g_ref, do_ref, sink_ref, dq_ref, dk_ref, dv_ref, dg_ref, dsink_ref, dk_acc, dv_acc) = rest[2 * kn:]
        step = pl.program_id(0)

        @pl.when(step == 0)
        def _():
            dk_acc[...] = jnp.zeros_like(dk_acc)
            dv_acc[...] = jnp.zeros_like(dv_acc)
            dsink_ref[...] = jnp.zeros_like(dsink_ref)

        @pl.when(step < steps)
        def _():
            q = q_ref[...]
            ks = [r[...] for r in k_refs]
            vs = [r[...] for r in v_refs]
            att = att_ref[...]
            g = g_ref[...]
            dog = do_ref[...]
            sg = _sigmoid(g)
            dg_ref[...] = dog * att * (sg * (1.0 + g * (1.0 - sg)))
            do = dog * (g * sg)
            units = [(u, h) for u in range(qn) for h in range(KV_HEADS)]
            n = range(len(units))
            rows = [slice(u * CHUNK, (u + 1) * CHUNK) for u in range(qn)]
            qs = [_stack_heads(q[rows[u], :], h) for u, h in units]
            kb = [_band(*ks[u:u + 3], h) for u, h in units]
            vbh = [_band(*vs[u:u + 3], h) for u, h in units]
            dos = [_stack_heads(do[rows[u], :], h) for u, h in units]
            dosb = [dos[i].astype(BF16) for i in n]
            s = [_dot(qs[i], kb[i], NT) for i in n]
            dp = [_dot(dosb[i], vbh[i], NT) for i in n]
            ds, pnb = [], []
            for i, (u, h) in enumerate(units):
                pn, psink = _attn_probs(s[i], sink_ref[h], qn * step + u)
                delta = jnp.sum(dos[i] * _stack_heads(att[rows[u], :], h), axis=-1, keepdims=True)
                ds.append((pn * (dp[i] - delta)).astype(BF16))
                pnb.append(pn.astype(BF16))
                dsink_ref[h] += -(psink * delta)
            dqs = [_dot(ds[i], kb[i]) for i in n]
            dks = [_dot(ds[i], qs[i], TN) for i in n]
            dvs = [_dot(pnb[i], dosb[i], TN) for i in n]
            dq_ref[...] = jnp.concatenate(
                [jnp.concatenate([dqs[u * KV_HEADS + h][r * CHUNK:(r + 1) * CHUNK, :]
                                  for h in range(KV_HEADS) for r in range(REP)], axis=1) for u in range(qn)],
                axis=0) * (HEAD_DIM ** -0.5)
            for u in range(qn):
                band = slice(u * CHUNK, u * CHUNK + BAND)
                dk_acc[band, :] += jnp.concatenate(dks[u * KV_HEADS:(u + 1) * KV_HEADS], axis=1) * (HEAD_DIM ** -0.5)
                dv_acc[band, :] += jnp.concatenate(dvs[u * KV_HEADS:(u + 1) * KV_HEADS], axis=1)

        dk_ref[...] = dk_acc[0:qrows, :]
        dv_ref[...] = dv_acc[0:qrows, :]
        for acc in (dk_acc, dv_acc):
            rest_rows = acc[qrows:wrows, :]
            acc[0:wrows - qrows, :] = rest_rows
            acc[wrows - qrows:wrows, :] = jnp.zeros((qrows, D_KV), F32)

    qp = lambda p: jnp.minimum(p, steps - 1)
    newest = lambda p: qn * qp(p) + qn - 1
    qblk = pl.BlockSpec((qrows, D_ATT), lambda p: (qp(p), 0))
    oldest = pl.BlockSpec((qrows, D_KV), lambda p: (jnp.maximum(p - 1, 0), 0))
    return pl.pallas_call(
        body, name="attn_bwd", grid=(steps + 1,),
        in_specs=[qblk] + _kv_specs(D_KV, newest, kn) + _kv_specs(D_KV, newest, kn) + [
            qblk,
            pl.BlockSpec((qrows, D_ATT), lambda p: (qp(p), OFF_G // D_ATT)),
            pl.BlockSpec((qrows, D_ATT), lambda p: (qp(p), D_SSD // D_ATT)),
            pl.BlockSpec((KV_HEADS, REP * CHUNK, 1), lambda p: (0, 0, 0)),
        ],
        out_specs=[qblk, oldest, oldest, qblk, pl.BlockSpec((KV_HEADS, REP * CHUNK, 1), lambda p: (0, 0, 0))],
        out_shape=[
            jax.ShapeDtypeStruct((lp, D_ATT), F32),
            jax.ShapeDtypeStruct((lp, D_KV), F32),
            jax.ShapeDtypeStruct((lp, D_KV), F32),
            jax.ShapeDtypeStruct((lp, D_ATT), F32),
            jax.ShapeDtypeStruct((KV_HEADS, REP * CHUNK, 1), F32),
        ],
        scratch_shapes=[pltpu.VMEM((wrows, D_KV), F32), pltpu.VMEM((wrows, D_KV), F32)],
        compiler_params=_params(("arbitrary",)),
    )(qr, *([kr] * kn), *([vb] * kn), att, proj, dmix, sink_stack)


def _post_loss(out, x2d, target, w):
    lp = out.shape[0]
    nc = lp // CHUNK
    nx = x2d.shape[0] // CHUNK

    k = _tile(nc, (ROW_K, 3, 2, 1))

    def body(o_ref, *rest):
        x_refs, t_refs = rest[:k], rest[k:2 * k]
        w_ref, do_ref, dy_ref, gw_ref, loss_ref = rest[2 * k:]
        i = pl.program_id(0)

        @pl.when(i == 0)
        def _():
            gw_ref[...] = jnp.zeros_like(gw_ref)
            loss_ref[...] = jnp.zeros_like(loss_ref)

        o = o_ref[...]
        w = w_ref[...]
        rstd = lax.rsqrt(jnp.mean(o * o, axis=-1, keepdims=True) + EPS)
        xhat = o * rstd
        x = jnp.concatenate([r[...] for r in x_refs], axis=0)
        t = jnp.concatenate([r[...] for r in t_refs], axis=0)
        chunk = _chunk_of_row(i, k)
        err = (x + xhat * w - t) * ((chunk > 0) & (chunk <= nx)).astype(F32)
        loss_ref[...] += 0.5 * jnp.sum(jnp.mean(err * err, axis=-1, keepdims=True), axis=0, keepdims=True)
        dy = err * (1.0 / D_MODEL)
        dy_ref[...] = dy
        gw_ref[...] += jnp.sum(dy * xhat, axis=0, keepdims=True)
        dxhat = dy * w
        do_ref[...] = (rstd * (dxhat - xhat * jnp.mean(dxhat * xhat, axis=-1, keepdims=True))).astype(BF16)

    row = pl.BlockSpec((k * CHUNK, D_MODEL), lambda i: (i, 0))
    return pl.pallas_call(
        body, name="post_loss", grid=(nc // k,),
        in_specs=[row] + _x_specs(nx, k) + _x_specs(nx, k) + [pl.BlockSpec((1, D_MODEL), lambda i: (0, 0))],
        out_specs=[row, row, pl.BlockSpec((1, D_MODEL), lambda i: (0, 0)), pl.BlockSpec((1, 128), lambda i: (0, 0))],
        out_shape=[
            jax.ShapeDtypeStruct((lp, D_MODEL), BF16),
            jax.ShapeDtypeStruct((lp, D_MODEL), F32),
            jax.ShapeDtypeStruct((1, D_MODEL), F32),
            jax.ShapeDtypeStruct((1, 128), F32),
        ],
        compiler_params=_params(("arbitrary",)),
    )(out, *([x2d] * k), *([target] * k), w)


def _prenorm_bwd(dhn, x2d, h0, dy, w):
    nx = x2d.shape[0] // CHUNK
    k = _tile(nx, (X_K, 4, 2, 1))
    rows = k * CHUNK

    def backward(h, dhn, w):
        rstd = lax.rsqrt(jnp.mean(h * h, axis=-1, keepdims=True) + EPS)
        xhat = h * rstd
        dxhat = dhn * w
        dh = rstd * (dxhat - xhat * jnp.mean(dxhat * xhat, axis=-1, keepdims=True))
        return dh, jnp.sum(dhn * xhat, axis=0, keepdims=True)

    steps = nx // k
    depth = 3
    assert steps >= depth

    def body(dhn_hbm, dy_hbm, x_hbm, w_ref, gx_ref, gw_ref, dhn_buf, dy_buf, x_buf, sems):
        i = pl.program_id(0)

        def loads(step, slot):
            padded = pl.ds(pl.multiple_of((step * k + 1) * CHUNK, CHUNK), rows)
            plain = pl.ds(pl.multiple_of(step * rows, rows), rows)
            return [pltpu.make_async_copy(dhn_hbm.at[padded, :], dhn_buf.at[slot], sems.at[0, slot]),
                    pltpu.make_async_copy(dy_hbm.at[padded, :], dy_buf.at[slot], sems.at[1, slot]),
                    pltpu.make_async_copy(x_hbm.at[plain, :], x_buf.at[slot], sems.at[2, slot])]

        @pl.when(i == 0)
        def _():
            gw_ref[...] = jnp.zeros_like(gw_ref)
            for s in range(depth - 1):
                for c in loads(s, s):
                    c.start()

        ahead = i + depth - 1

        @pl.when(ahead < steps)
        def _():
            for c in loads(ahead, ahead % depth):
                c.start()

        slot = i % depth
        for c in loads(i, slot):
            c.wait()
        dh, gw = backward(x_buf[slot], dhn_buf[slot], w_ref[...])
        gx_ref[...] = dh + dy_buf[slot]
        gw_ref[...] += gw

    def body_meta(dhn_ref, h0_ref, w_ref, d0_ref, gw_ref):
        d0_ref[...], gw_ref[...] = backward(h0_ref[...], dhn_ref[...], w_ref[...])

    first = pl.BlockSpec((CHUNK, D_MODEL), lambda i: (0, 0))
    vec = pl.BlockSpec((1, D_MODEL), lambda i: (0, 0))
    wide = pl.BlockSpec((rows, D_MODEL), lambda i: (i, 0))
    gx, gw_x = pl.pallas_call(
        body, name="prenorm_bwd", grid=(steps,),
        in_specs=[ANY, ANY, ANY, vec],
        out_specs=[wide, vec],
        out_shape=[jax.ShapeDtypeStruct((nx * CHUNK, D_MODEL), F32), jax.ShapeDtypeStruct((1, D_MODEL), F32)],
        scratch_shapes=[pltpu.VMEM((depth, rows, D_MODEL), F32)] * 3 + [pltpu.SemaphoreType.DMA((3, depth))],
        compiler_params=_params(("arbitrary",)),
    )(dhn, dy, x2d, w)
    d0, gw_0 = pl.pallas_call(
        body_meta, name="prenorm_bwd_meta", grid=(1,),
        in_specs=[first, first, vec], out_specs=[first, vec],
        out_shape=[jax.ShapeDtypeStruct((CHUNK, D_MODEL), F32), jax.ShapeDtypeStruct((1, D_MODEL), F32)],
        compiler_params=_params(("arbitrary",)),
    )(dhn, h0, w)
    return gx, d0, gw_x + gw_0


def _adamw(slabs, w, m, v, name):
    rows, cols = w.shape
    tr = _tile(rows, (256, 128, 64, 16, 8))
    c1 = 1.0 - ADAM_B1 ** ADAM_STEP
    c2 = 1.0 - ADAM_B2 ** ADAM_STEP

    def body(s_ref, w_ref, m_ref, v_ref, g_ref, d_ref, mo_ref, vo_ref):
        g = s_ref[0].astype(F32)
        for k in range(1, slabs.shape[0]):
            g = g + s_ref[k].astype(F32)
        w = w_ref[...]
        m = ADAM_B1 * m_ref[...] + (1.0 - ADAM_B1) * g
        v = ADAM_B2 * v_ref[...] + (1.0 - ADAM_B2) * (g * g)
        g_ref[...] = g
        mo_ref[...] = m
        vo_ref[...] = v
        d_ref[...] = -ADAM_LR * ((m / c1) / (jnp.sqrt(v / c2) + ADAM_EPS) + ADAM_WD * w)

    blk = pl.BlockSpec((tr, cols), lambda i: (i, 0))
    shape = jax.ShapeDtypeStruct((rows, cols), F32)
    return pl.pallas_call(
        body, name=name, grid=(rows // tr,),
        in_specs=[pl.BlockSpec((slabs.shape[0], tr, cols), lambda i: (0, i, 0)), blk, blk, blk],
        out_specs=[blk, blk, blk, blk],
        out_shape=[shape, shape, shape, shape],
        compiler_params=_params(("parallel",)),
    )(slabs, w, m, v)


def _perm_xbc(a):
    lead = a.shape[:-1]
    xs = a[..., :D_SSD].reshape(lead + (N_GROUPS, GROUP_W))
    b = a[..., D_SSD:D_SSD + N_GROUPS * D_STATE].reshape(lead + (N_GROUPS, D_STATE))
    c = a[..., D_SSD + N_GROUPS * D_STATE:].reshape(lead + (N_GROUPS, D_STATE))
    return jnp.concatenate([xs, b, c], axis=-1).reshape(lead + (D_CONV,))


def _unperm_xbc(a):
    lead = a.shape[:-1]
    t = a.reshape(lead + (N_GROUPS, XBC_BLK))
    xs = t[..., :GROUP_W].reshape(lead + (D_SSD,))
    b = t[..., GROUP_W:GROUP_W + D_STATE].reshape(lead + (N_GROUPS * D_STATE,))
    c = t[..., GROUP_W + D_STATE:].reshape(lead + (N_GROUPS * D_STATE,))
    return jnp.concatenate([xs, b, c], axis=-1)


R_Z, R_XBC, R_DT, R_Q, R_K, R_V, R_G = 0, 2048, 6144, 6176, 7200, 7456, 7712


def _internal_of_reference():
    ref = np.arange(D_IN_PROJ)
    out = np.empty(D_IN_PROJ, np.int64)
    out[R_Z:R_XBC] = OFF_Z + ref[:D_SSD]
    xs = np.arange(D_SSD)
    out[R_XBC:R_XBC + D_SSD] = OFF_XBC + (xs // GROUP_W) * XBC_BLK + xs % GROUP_W
    bc = np.arange(N_GROUPS * D_STATE)
    out[R_XBC + D_SSD:R_XBC + D_SSD + N_GROUPS * D_STATE] = OFF_XBC + (bc // D_STATE) * XBC_BLK + GROUP_W + bc % D_STATE
    out[R_XBC + D_SSD + N_GROUPS * D_STATE:R_DT] = OFF_XBC + (bc // D_STATE) * XBC_BLK + GROUP_W + D_STATE + bc % D_STATE
    out[R_DT:R_Q] = OFF_DT + np.arange(SSD_HEADS)
    out[R_Q:R_K] = OFF_Q + np.arange(D_ATT)
    out[R_K:R_V] = OFF_K + np.arange(D_KV)
    out[R_V:R_G] = OFF_V + np.arange(D_KV)
    out[R_G:] = OFF_G + np.arange(D_ATT)
    return out


def _runs(src, dst_break):
    runs, lo = [], 0
    for i in range(1, len(src) + 1):
        if i == len(src) or src[i] != src[i - 1] + 1 or dst_break[i] != dst_break[i - 1]:
            runs.append((lo, i))
            lo = i
    return runs


RELAYOUT_ROWS = 256


def _lane_window(ref, lead, c0, n):
    a0 = c0 // 128 * 128
    a1 = min(-(-(c0 + n) // 128) * 128, ref.shape[-1])
    return ref[lead + (slice(None), slice(a0, a1))][:, c0 - a0:c0 - a0 + n]


def _gather_w_in(w_shard, small, x2d, norm_w):
    int_of_ref = _internal_of_reference()
    ref_of_int = np.full(NP, -1, np.int64)
    ref_of_int[int_of_ref] = np.arange(D_IN_PROJ)
    shard = np.where(ref_of_int >= 0, ref_of_int // SHARD_IN, -1)
    src = np.where(ref_of_int >= 0, ref_of_int, -10 - 2 * np.arange(NP))
    plan, zeros = [], 0
    for lo, hi in _runs(src, shard):
        if ref_of_int[lo] < 0:
            zeros += hi - lo
            continue
        if zeros:
            plan.append((None, 0, zeros))
            zeros = 0
        plan.append((int(ref_of_int[lo] // SHARD_IN), int(ref_of_int[lo] % SHARD_IN), hi - lo))
    if zeros:
        plan.append((None, 0, zeros))
    tr = RELAYOUT_ROWS
    steps = D_MODEL // tr
    side = _GatherTwoLevel(small)
    ns = side.n
    nx = x2d.shape[0] // CHUNK
    nc = nx + 2
    k = _tile(nc, (11, ROW_K, 3, 2, 1))
    pn = nc // k
    n = max(pn, steps)
    tile_of = lambda i: jnp.where(i < pn - 1, i + 1, 0)

    def body(w_ref, *rest):
        s_in, x_refs, nw_ref = rest[:ns], rest[ns:ns + k], rest[ns + k]
        o_ref, land_ref, s_out, hn_ref = rest[ns + k + 1], rest[ns + k + 2], rest[ns + k + 3:2 * ns + k + 3], rest[2 * ns + k + 3]
        tile_buf, send_sems, recv_sems, load_sems, local_sem, meta_buf, meta_sem = rest[2 * ns + k + 4:2 * ns + k + 11]
        side_sems = rest[2 * ns + k + 11:]
        i = pl.program_id(0)
        x, y, c = lax.axis_index("x"), lax.axis_index("y"), lax.axis_index("c")
        me, sibling = (x, y, c), (x, y, 1 - c)
        chips = [(1 - x, y), (x, 1 - y), (1 - x, 1 - y)]

        def prenorm_tile(t, head=None):
            h = jnp.concatenate([r[...] for r in x_refs], axis=0)
            if head is not None:
                h = jnp.concatenate([head, h[CHUNK:, :]], axis=0)
            h = h * (_chunk_of_row(t, k) <= nx).astype(F32)
            rstd = lax.rsqrt(jnp.mean(h * h, axis=-1, keepdims=True) + EPS)
            hn_ref[...] = (h * rstd * nw_ref[...]).astype(BF16)

        if pn > 1:
            @pl.when(i < pn - 1)
            def _():
                prenorm_tile(i + 1)

        def rows(t):
            return pl.ds(t * tr, tr)

        def tile(place, t):
            return land_ref.at[4 * place[0] + 2 * place[1] + place[2], rows(t), :]

        def copy(t, k, block, to, src=None):
            return pltpu.make_async_remote_copy(
                src_ref=tile(block, t) if src is None else src, dst_ref=tile(block, t),
                send_sem=send_sems.at[t, k], recv_sem=recv_sems.at[t, k], device_id=to, device_id_type=pl.DeviceIdType.MESH)

        def own_sends(t):
            mine = w_ref.at[rows(t), :]
            return [copy(t, 0, me, sibling, src=mine)] + [copy(t, 1 + j, me, (*chip, c), src=mine)
                                                          for j, chip in enumerate(chips)]

        own = pltpu.make_async_copy(w_ref, land_ref.at[4 * x + 2 * y + c], local_sem)

        @pl.when(i < steps)
        def _():
            @pl.when(i == 0)
            def _():
                side.start(s_in, s_out, side_sems)
                own.start()
                for t in range(steps):
                    for cp in own_sends(t):
                        cp.start()
                own.wait()

            forwards = []
            for j, chip in enumerate(chips):
                copy(i, 1 + j, (*chip, c), me).wait_recv()
                forwards.append(copy(i, 4 + j, (*chip, c), sibling))
                forwards[j].start()
            copy(i, 0, sibling, me).wait_recv()
            for j, chip in enumerate(chips):
                copy(i, 4 + j, (*chip, 1 - c), me).wait_recv()
            loads = [pltpu.make_async_copy(land_ref.at[s, rows(i), :], tile_buf.at[s], load_sems.at[s]) for s in range(N_DEV)]
            for cp in loads:
                cp.start()
            for cp in loads:
                cp.wait()
            o_ref[...] = jnp.concatenate(
                [jnp.zeros((tr, w), o_ref.dtype) if s is None else _lane_window(tile_buf, (s,), c0, w) for s, c0, w in plan],
                axis=1)
            for cp in forwards:
                cp.wait_send()

            @pl.when(i == steps - 1)
            def _():
                for t in range(steps):
                    for cp in own_sends(t):
                        cp.wait_send()
                side.wait(s_in, s_out, side_sems)

        @pl.when(i == n - 1)
        def _():
            meta = pltpu.make_async_copy(s_out[ns - 1], meta_buf, meta_sem)
            meta.start()
            meta.wait()
            tokens = jnp.concatenate([meta_buf[s] for s in range(N_DEV)], axis=1)
            prenorm_tile(0, jnp.concatenate([jnp.zeros((PAD_LEAD, D_MODEL), F32), tokens], axis=0))

    last = steps - 1
    res = pl.pallas_call(
        body, name="gather_w_in", grid=(n,),
        in_specs=[ANY] + side.in_specs + _x_specs(nx, k, tile_of) + [pl.BlockSpec((1, D_MODEL), lambda i: (0, 0))],
        out_specs=[pl.BlockSpec((tr, NP), lambda i: (jnp.minimum(i, last), 0)), ANY] + side.out_specs + [
            pl.BlockSpec((k * CHUNK, D_MODEL), lambda i: (tile_of(i), 0))],
        out_shape=[jax.ShapeDtypeStruct((D_MODEL, NP), w_shard.dtype),
                   jax.ShapeDtypeStruct((N_DEV, D_MODEL, SHARD_IN), w_shard.dtype)] + side.out_shape + [
            jax.ShapeDtypeStruct((nc * CHUNK, D_MODEL), BF16)],
        scratch_shapes=[pltpu.VMEM((N_DEV, tr, SHARD_IN), w_shard.dtype),
                        pltpu.SemaphoreType.DMA((steps, N_DEV - 1)), pltpu.SemaphoreType.DMA((steps, N_DEV - 1)),
                        pltpu.SemaphoreType.DMA((N_DEV,)), pltpu.SemaphoreType.DMA(()),
                        pltpu.VMEM((N_DEV,) + tuple(small[ns - 1].shape), F32),
                        pltpu.SemaphoreType.DMA(())] + side.scratch,
        compiler_params=_params(("arbitrary",), side_effects=True),
    )(w_shard, *side.operands, *([x2d] * k), norm_w)
    return [res[0]] + list(res[2:])


def _w_in_chip_slabs(dw):
    int_of_ref = _internal_of_reference()
    plan = []
    for s in range(N_DEV):
        cols = int_of_ref[s * SHARD_IN:(s + 1) * SHARD_IN]
        plan.append([(int(cols[lo]), hi - lo) for lo, hi in _runs(cols, np.zeros_like(cols))])
    tr = RELAYOUT_ROWS
    steps = D_MODEL // tr
    pairs = N_DEV // 2

    def body(dw_ref, o_ref, land_ref, send_buf, land_buf, send_sems, recv_sems, load_sems):
        i = pl.program_id(0)
        x, y, c = lax.axis_index("x"), lax.axis_index("y"), lax.axis_index("c")
        slot = i % 2

        def sends(step, sl):
            return [pltpu.make_async_remote_copy(
                src_ref=send_buf.at[sl, 2 * q + (1 - c)], dst_ref=land_ref.at[q, pl.ds(step * tr, tr), :],
                send_sem=send_sems.at[sl, q], recv_sem=recv_sems.at[step, q],
                device_id=(x, y, 1 - c), device_id_type=pl.DeviceIdType.MESH) for q in range(pairs)]

        @pl.when(i < steps)
        def _():
            @pl.when(i >= 2)
            def _():
                for cp in sends(i - 2, slot):
                    cp.wait_send()

            for s in range(N_DEV):
                send_buf[slot, s] = jnp.concatenate([_lane_window(dw_ref, (), c0, n) for c0, n in plan[s]], axis=1)
            for cp in sends(i, slot):
                cp.start()

        @pl.when(i >= 1)
        def _():
            loads = []
            for q, cp in enumerate(sends(i - 1, 1 - slot)):
                cp.wait_recv()
                loads.append(pltpu.make_async_copy(land_ref.at[q, pl.ds((i - 1) * tr, tr), :], land_buf.at[q], load_sems.at[q]))
                loads[q].start()
            for q in range(pairs):
                loads[q].wait()
                o_ref[q] = (send_buf[1 - slot, 2 * q + c].astype(F32) + land_buf[q].astype(F32)).astype(o_ref.dtype)

        @pl.when(i == steps)
        def _():
            for cp in sends(steps - 2, steps % 2) + sends(steps - 1, 1 - steps % 2):
                cp.wait_send()

    last = steps - 1
    return pl.pallas_call(
        body, name="dw_in_relayout", grid=(steps + 1,),
        in_specs=[pl.BlockSpec((tr, NP), lambda i: (jnp.minimum(i, last), 0))],
        out_specs=[pl.BlockSpec((pairs, tr, SHARD_IN), lambda i: (0, jnp.maximum(i - 1, 0), 0)), ANY],
        out_shape=[jax.ShapeDtypeStruct((pairs, D_MODEL, SHARD_IN), dw.dtype),
                   jax.ShapeDtypeStruct((pairs, D_MODEL, SHARD_IN), dw.dtype)],
        scratch_shapes=[pltpu.VMEM((2, N_DEV, tr, SHARD_IN), dw.dtype), pltpu.VMEM((pairs, tr, SHARD_IN), dw.dtype),
                        pltpu.SemaphoreType.DMA((2, pairs)), pltpu.SemaphoreType.DMA((steps, pairs)),
                        pltpu.SemaphoreType.DMA((pairs,))],
        compiler_params=_params(("arbitrary",), side_effects=True),
    )(dw)[0]


def _rep_heads(a):
    return jnp.repeat(a, HEAD_DIM, axis=1)


SMALL = (("norm_pre_w", 2048), ("conv_b", 4096), ("dt_bias", 32), ("a_log", 32), ("d_skip", 32),
         ("ssd_norm_w", 2048), ("attn_sinks", 16), ("norm_post_w", 2048))
SMALL_USED = sum(size for _, size in SMALL)
SMALL_LEN = 10368


def _pack_small(d, loss=None):
    parts = [d[name].reshape(1, size) for name, size in SMALL]
    tail = jnp.zeros((1, SMALL_LEN - SMALL_USED), F32)
    if loss is not None:
        tail = tail.at[0, 0].set(loss)
    return jnp.concatenate(parts + [tail], axis=1)


def _unpack_small(vec):
    out, off = {}, 0
    for name, size in SMALL:
        out[name] = vec[:, off:off + size]
        off += size
    return out


def kernel(x, meta_tokens, norm_pre_w, w_in, conv_w, conv_b, dt_bias, a_log, d_skip, ssd_norm_w, attn_sinks, w_out, norm_post_w, loss_target, m_meta_tokens, m_norm_pre_w, m_w_in, m_conv_w, m_conv_b, m_dt_bias, m_a_log, m_d_skip, m_ssd_norm_w, m_attn_sinks, m_w_out, m_norm_post_w, v_meta_tokens, v_norm_pre_w, v_w_in, v_conv_w, v_conv_b, v_dt_bias, v_a_log, v_d_skip, v_ssd_norm_w, v_attn_sinks, v_w_out, v_norm_post_w):
    seq = x.shape[1]
    lp = seq + 2 * CHUNK
    x2d = x[0]

    w_all, conv_w_g, meta_g, hn = _gather_w_in(
        w_in[0].astype(BF16), [conv_w[0], meta_tokens], x2d, norm_pre_w)
    conv_w_full = _perm_xbc(jnp.transpose(conv_w_g, (1, 0, 2)).reshape(CONV_WIDTH, D_CONV))
    conv_b_int = _perm_xbc(conv_b)
    meta_full = jnp.transpose(meta_g, (1, 0, 2)).reshape(N_META, D_MODEL)
    h0 = jnp.concatenate([jnp.zeros((PAD_LEAD, D_MODEL), F32), meta_full], axis=0)

    pos = (jnp.arange(lp) - PAD_LEAD).astype(F32)
    half = HEAD_DIM // 2
    inv = ROPE_THETA ** (-jnp.arange(half, dtype=F32) / half)
    ang = pos[:, None] * inv[None, :]
    cos_t = jnp.tile(jnp.cos(ang), (1, 4))
    sin_t = jnp.tile(jnp.concatenate([-jnp.sin(ang), jnp.sin(ang)], axis=1), (1, 2))
    head_of_col = np.arange(D_SSD) // HEAD_DIM
    expand = jnp.asarray((np.arange(128)[:, None] == head_of_col[None, :]).astype(np.float32))
    reduce_t = jnp.asarray((head_of_col[:, None] == np.arange(128)[None, :]).astype(np.float32))
    tri = jnp.asarray(np.tril(np.ones((CHUNK, CHUNK), np.float32)))
    a_rep = _rep_heads(-jnp.exp(a_log))
    dsk_rep = _rep_heads(d_skip)
    dt_bias_pad = jnp.pad(dt_bias, ((0, 0), (0, 128 - SSD_HEADS)))
    sink_stack = jnp.repeat(attn_sinks.reshape(KV_HEADS, REP), CHUNK, axis=1).reshape(KV_HEADS, REP * CHUNK, 1)

    tm = _tile(lp, (1056, 704, 128, 64))
    proj, w_out_g = _matmul(hn, w_all, tm=tm, tn=1536, tk=D_MODEL, out_dtype=F32, name="in_proj",
                            comm=_Comm([(w_out[0].astype(BF16), "gather")]))
    w_out_full = w_out_g.reshape(D_MIX, D_MODEL)
    qr, kr, vb, dt_rep = _act_fwd(proj, cos_t, sin_t, expand, dt_bias_pad)
    mix, ytot, hprev, xbc, dsilu = _ssd_fwd(proj, conv_w_full, conv_b_int, dt_rep, a_rep, dsk_rep, ssd_norm_w, tri)
    att, mix = _attn_fwd(qr, kr, vb, proj, sink_stack, mix)
    out = _matmul(mix, w_out_full, tm=tm, tn=1024, tk=D_MIX, out_dtype=F32, name="out_proj")
    dout, dy, g_norm_post, loss_part = _post_loss(out, x2d, loss_target[0], norm_post_w)

    dmix = _matmul(dout, w_out_full, trans_b=True, tm=tm, tn=1024, tk=D_MODEL, out_dtype=F32, name="dmix")
    dw_out = _matmul(mix, dout, trans_a=True, tm=512, tn=1024, tk=lp, out_dtype=BF16, name="dw_out")
    dqr, dkr, dv, dg, dsink_rows = _attn_bwd(qr, kr, vb, att, proj, dmix, sink_stack)
    dproj, dpre, ddt_part, dd_part, da_part, g_ssd_norm, g_out = _ssd_bwd(
        dmix, ytot, proj, xbc, dsilu, dt_rep, hprev, a_rep, dsk_rep, ssd_norm_w, tri,
        _Comm([(dw_out.reshape(N_DEV, D_MIX // N_DEV, D_MODEL), "scatter")]))
    dproj, dconv_w_int, dconv_b_int = _conv_bwd(dpre, proj, conv_w_full, dproj)
    dproj, ddt_bias = _act_bwd(dqr, dkr, dv, dg, ddt_part, cos_t, sin_t, reduce_t, dproj)
    dw_all = _matmul(hn, dproj, trans_a=True, tm=512, tn=1024, tk=lp, out_dtype=BF16, name="dw_in")
    dw_chip = _w_in_chip_slabs(dw_all)
    dhn, g_in = _matmul(dproj, w_all, trans_b=True, tm=tm, tn=1024, tk=1536, out_dtype=F32, name="dhn",
                        comm=_Comm([(dw_chip, "scatter")], scope="chips"))
    grad_x, dh0, g_norm_pre = _prenorm_bwd(dhn, x2d, h0, dy, norm_pre_w)

    dmeta = dh0[PAD_LEAD:, :]
    dconv_w_ref = _unperm_xbc(dconv_w_int)
    heads = lambda part: part.reshape(SSD_HEADS, HEAD_DIM).sum(axis=1).reshape(1, SSD_HEADS)
    small_local = _pack_small({
        "norm_pre_w": g_norm_pre, "conv_b": _unperm_xbc(dconv_b_int), "dt_bias": ddt_bias[:, :SSD_HEADS],
        "a_log": heads(da_part) * (-jnp.exp(a_log)), "d_skip": heads(dd_part), "ssd_norm_w": g_ssd_norm,
        "attn_sinks": dsink_rows.reshape(Q_HEADS, CHUNK).sum(axis=1).reshape(1, Q_HEADS),
        "norm_post_w": g_norm_post}, loss=loss_part[0, 0])
    g_conv, g_meta, g_small = _exchange(
        [(jnp.transpose(dconv_w_ref.reshape(CONV_WIDTH, N_DEV, D_CONV // N_DEV), (1, 0, 2)), "scatter"),
         (jnp.transpose(dmeta.reshape(N_META, N_DEV, D_MODEL // N_DEV), (1, 0, 2)), "scatter"),
         (small_local, "gather")], "exchange_small")

    res = {}
    res["w_in"] = [o[None] for o in _adamw(g_in, w_in[0], m_w_in[0], v_w_in[0], "adamw_w_in")]
    res["w_out"] = [o[None] for o in _adamw(g_out, w_out[0], m_w_out[0], v_w_out[0], "adamw_w_out")]
    res["conv_w"] = [o[None] for o in _adamw(g_conv, conv_w[0], m_conv_w[0], v_conv_w[0], "adamw_conv_w")]
    res["meta_tokens"] = _adamw(g_meta, meta_tokens, m_meta_tokens, v_meta_tokens, "adamw_meta")
    given = dict(norm_pre_w=(norm_pre_w, m_norm_pre_w, v_norm_pre_w), conv_b=(conv_b, m_conv_b, v_conv_b),
                 dt_bias=(dt_bias, m_dt_bias, v_dt_bias), a_log=(a_log, m_a_log, v_a_log),
                 d_skip=(d_skip, m_d_skip, v_d_skip), ssd_norm_w=(ssd_norm_w, m_ssd_norm_w, v_ssd_norm_w),
                 attn_sinks=(attn_sinks, m_attn_sinks, v_attn_sinks),
                 norm_post_w=(norm_post_w, m_norm_post_w, v_norm_post_w))
    packed = [_pack_small({k: t[j] for k, t in given.items()}) for j in range(3)]
    small_out = _adamw(g_small, packed[0], packed[1], packed[2], "adamw_small")
    small_res = [_unpack_small(r) for r in small_out]
    loss = small_out[0][0, SMALL_USED]

    order = ["meta_tokens", "norm_pre_w", "w_in", "conv_w", "conv_b", "dt_bias", "a_log", "d_skip", "ssd_norm_w",
             "attn_sinks", "w_out", "norm_post_w"]
    outs = []
    for j in range(4):
        for name in order:
            outs.append(res[name][j] if name in res else small_res[j][name])
    return (loss, grad_x[None], *outs)
```

```python
import functools

import numpy as np
import jax
import jax.numpy as jnp
from jax import lax
from jax.experimental import pallas as pl
from jax.experimental.pallas import tpu as pltpu

F32 = jnp.float32
BF16 = jnp.bfloat16
HIGHEST = lax.Precision.HIGHEST

N_DEV = 8
D_MODEL = 2048
CHUNK = 64
N_META = 16
PAD_LEAD = CHUNK - N_META
EPS = 1e-6
N_GROUPS = 8
HEADS_PER_GROUP = 4
HEAD_DIM = 64
GROUP_W = HEADS_PER_GROUP * HEAD_DIM
D_STATE = 128
D_SSD = 2048
D_CONV = 4096
SSD_HEADS = 32
CONV_WIDTH = 4
Q_HEADS = 16
KV_HEADS = 4
REP = 4
D_ATT = 1024
D_KV = 256
WINDOW_CHUNKS = 2
BAND = (WINDOW_CHUNKS + 1) * CHUNK
ROPE_THETA = 10000.0
D_MIX = D_SSD + D_ATT
D_IN_PROJ = 8736
SHARD_IN = D_IN_PROJ // N_DEV

OFF_Z, OFF_XBC, OFF_Q, OFF_G, OFF_K, OFF_V, OFF_DT = 0, 2048, 6144, 7168, 8192, 8448, 8704
NP = 9216
TAIL_W = NP - OFF_Q
XBC_BLK = 512
SSD_GPS = 8
SSD_CPS = 2

ADAM_LR, ADAM_B1, ADAM_B2, ADAM_EPS, ADAM_WD, ADAM_STEP = 0.001, 0.9, 0.999, 1e-08, 0.01, 10

VMEM_LIMIT = 48 * 1024 * 1024

NN = (((1,), (0,)), ((), ()))
NT = (((1,), (1,)), ((), ()))
TN = (((0,), (0,)), ((), ()))
ANY = pl.BlockSpec(memory_space=pl.ANY)


def _dot(a, b, dims=NN, precision=None):
    return lax.dot_general(a, b, dims, precision=precision, preferred_element_type=F32)


def _tile(n, prefs):
    for t in prefs:
        if n % t == 0:
            return t
    return n


def _params(sem, vmem=VMEM_LIMIT, side_effects=False):
    return pltpu.CompilerParams(dimension_semantics=sem, vmem_limit_bytes=vmem, has_side_effects=side_effects)


def _sigmoid(x):
    return 1.0 / (1.0 + jnp.exp(-x))


class _Comm:
    def __init__(self, items, scope="devices"):
        self.items = items
        self.scope = scope
        self.slabs = slabs = N_DEV if scope == "devices" else N_DEV // 2
        self.n = n = len(items)
        self.operands = [arr for arr, _ in items]
        self.in_specs = [ANY] * n
        self.out_specs = [ANY] * n
        self.out_shape = [jax.ShapeDtypeStruct((slabs,) + tuple(arr.shape) if kind == "gather" else tuple(arr.shape),
                                               arr.dtype) for arr, kind in items]
        self.scratch = [pltpu.SemaphoreType.DMA((n, slabs - 1)), pltpu.SemaphoreType.DMA((n, slabs - 1)),
                        pltpu.SemaphoreType.DMA((n,))]

    def _places(self):
        pos = (lax.axis_index("x"), lax.axis_index("y"), lax.axis_index("c"))
        if self.scope == "devices":
            index = lambda p: 4 * p[0] + 2 * p[1] + p[2]
            masks = range(1, N_DEV)
        else:
            index = lambda p: 2 * p[0] + p[1]
            masks = (2, 4, 6)
        peers = []
        for k in masks:
            p = tuple(1 - pos[b] if (k >> (2 - b)) & 1 else pos[b] for b in range(3))
            peers.append((p, index(p)))
        return index(pos), peers

    def _copies(self, ins, outs, sems, landed):
        send_sems, recv_sems, local_sems = sems
        me, peers = self._places()
        local, remote = [], []
        for a, (_, kind) in enumerate(self.items):
            own = ins[a] if kind == "gather" else ins[a].at[me]
            local.append(pltpu.make_async_copy(own, outs[a].at[me], local_sems.at[a]))
            for k, (p, pid) in enumerate(peers):
                remote.append(pltpu.make_async_remote_copy(
                    src_ref=ins[a] if kind == "gather" else ins[a].at[pid],
                    dst_ref=outs[a].at[pid if landed else me],
                    send_sem=send_sems.at[a, k], recv_sem=recv_sems.at[a, k],
                    device_id=p, device_id_type=pl.DeviceIdType.MESH))
        return local, remote

    def start(self, ins, outs, sems):
        local, remote = self._copies(ins, outs, sems, landed=False)
        for cp in local + remote:
            cp.start()

    def wait(self, ins, outs, sems):
        local, remote = self._copies(ins, outs, sems, landed=True)
        for cp in remote + local:
            cp.wait()


def _exchange(items, name):
    comm = _Comm(items)
    n = comm.n

    def body(*refs):
        ins, outs, sems = refs[:n], refs[n:2 * n], refs[2 * n:]
        comm.start(ins, outs, sems)
        comm.wait(ins, outs, sems)

    return pl.pallas_call(
        body, name=name, in_specs=comm.in_specs, out_specs=comm.out_specs, out_shape=comm.out_shape,
        scratch_shapes=comm.scratch, compiler_params=pltpu.CompilerParams(has_side_effects=True),
    )(*comm.operands)


class _GatherTwoLevel:
    def __init__(self, arrays):
        self.arrays = arrays
        self.n = n = len(arrays)
        self.operands = list(arrays)
        self.in_specs = [ANY] * n
        self.out_specs = [ANY] * n
        self.out_shape = [jax.ShapeDtypeStruct((N_DEV,) + tuple(a.shape), a.dtype) for a in arrays]
        self.scratch = [pltpu.SemaphoreType.DMA((n, N_DEV - 1)), pltpu.SemaphoreType.DMA((n, N_DEV - 1)),
                        pltpu.SemaphoreType.DMA((n,))]

    def _plan(self, ins, outs, sems):
        send_sems, recv_sems, local_sems = sems
        x, y, c = lax.axis_index("x"), lax.axis_index("y"), lax.axis_index("c")
        me, sibling = (x, y, c), (x, y, 1 - c)
        chips = [(1 - x, y), (x, 1 - y), (1 - x, 1 - y)]

        def slab(a, place):
            return outs[a].at[4 * place[0] + 2 * place[1] + place[2]]

        def copy(a, k, block, to, src=None):
            return pltpu.make_async_remote_copy(
                src_ref=slab(a, block) if src is None else src, dst_ref=slab(a, block),
                send_sem=send_sems.at[a, k], recv_sem=recv_sems.at[a, k],
                device_id=to, device_id_type=pl.DeviceIdType.MESH)

        own, mine = [], []
        for a in range(self.n):
            mine.append(pltpu.make_async_copy(ins[a], slab(a, me), local_sems.at[a]))
            own.append(copy(a, 0, me, sibling, src=ins[a]))
            own += [copy(a, 1 + j, me, (*chip, c), src=ins[a]) for j, chip in enumerate(chips)]
        return me, sibling, chips, c, copy, own, mine

    def start(self, ins, outs, sems):
        _, _, _, _, _, own, mine = self._plan(ins, outs, sems)
        for cp in mine + own:
            cp.start()

    def wait(self, ins, outs, sems):
        me, sibling, chips, c, copy, own, mine = self._plan(ins, outs, sems)
        forwards = []
        for j, chip in enumerate(chips):
            for a in range(self.n):
                copy(a, 1 + j, (*chip, c), me).wait_recv()
                fwd = copy(a, 4 + j, (*chip, c), sibling)
                fwd.start()
                forwards.append(fwd)
        for a in range(self.n):
            copy(a, 0, sibling, me).wait_recv()
            for j, chip in enumerate(chips):
                copy(a, 4 + j, (*chip, 1 - c), me).wait_recv()
        for cp in own + forwards:
            cp.wait_send()
        for loc in mine:
            loc.wait()


def _matmul(a, b, *, tm, tn, tk, out_dtype, name, trans_a=False, trans_b=False, comm=None):
    m, k = (a.shape[1], a.shape[0]) if trans_a else a.shape
    n = b.shape[0] if trans_b else b.shape[1]
    nk = k // tk
    dims = TN if trans_a else (NT if trans_b else NN)
    assert not (trans_a and trans_b)
    nc = comm.n if comm else 0
    grid = (m // tm, n // tn, nk)

    def body(*refs):
        a_ref, b_ref = refs[:2]
        cin = refs[2:2 + nc]
        o_ref = refs[2 + nc]
        cout = refs[3 + nc:3 + 2 * nc]
        scratch = refs[3 + 2 * nc:]
        sems = scratch[len(scratch) - 3:] if comm else None
        i, j, kk = pl.program_id(0), pl.program_id(1), pl.program_id(2)
        if comm:
            @pl.when((i == 0) & (j == 0) & (kk == 0))
            def _():
                comm.start(cin, cout, sems)

        if nk == 1:
            o_ref[...] = _dot(a_ref[...], b_ref[...], dims).astype(out_dtype)
        else:
            acc_ref = scratch[0]

            @pl.when(kk == 0)
            def _():
                acc_ref[...] = jnp.zeros_like(acc_ref)

            acc_ref[...] += _dot(a_ref[...], b_ref[...], dims)

            @pl.when(kk == nk - 1)
            def _():
                o_ref[...] = acc_ref[...].astype(out_dtype)

        if comm:
            @pl.when((i == grid[0] - 1) & (j == grid[1] - 1) & (kk == nk - 1))
            def _():
                comm.wait(cin, cout, sems)

    a_spec = (pl.BlockSpec((tk, tm), lambda i, j, kk: (kk, i)) if trans_a
              else pl.BlockSpec((tm, tk), lambda i, j, kk: (i, kk)))
    b_spec = (pl.BlockSpec((tn, tk), lambda i, j, kk: (j, kk)) if trans_b
              else pl.BlockSpec((tk, tn), lambda i, j, kk: (kk, j)))
    sem = ("arbitrary",) * 3 if comm else ("parallel", "parallel", "arbitrary")
    res = pl.pallas_call(
        body, name=name, grid=grid,
        in_specs=[a_spec, b_spec] + (comm.in_specs if comm else []),
        out_specs=[pl.BlockSpec((tm, tn), lambda i, j, kk: (i, j))] + (comm.out_specs if comm else []),
        out_shape=[jax.ShapeDtypeStruct((m, n), out_dtype)] + (comm.out_shape if comm else []),
        scratch_shapes=([] if nk == 1 else [pltpu.VMEM((tm, tn), F32)]) + (comm.scratch if comm else []),
        compiler_params=_params(sem, side_effects=bool(comm)),
    )(a, b, *(comm.operands if comm else []))
    return res if comm else res[0]


ROW_K = 6
X_K = 4


def _x_specs(nx, k, tile_of=lambda i: i):
    return [pl.BlockSpec((CHUNK, D_MODEL), functools.partial(lambda u, i: (jnp.clip(k * tile_of(i) + u - 1, 0, nx - 1), 0), u))
            for u in range(k)]


def _chunk_of_row(i, k):
    return k * i + lax.broadcasted_iota(jnp.int32, (k * CHUNK, 1), 0) // CHUNK


CONV_COLS = 512
HALO = 8


def _conv_pre(ext, w, b):
    taps = [ext[HALO:, :]] + [pltpu.roll(ext, j, 0)[HALO:, :] for j in range(1, CONV_WIDTH)]
    acc = b + w[3:4, :] * taps[0]
    for j in range(1, CONV_WIDTH):
        acc = acc + w[3 - j:4 - j, :] * taps[j]
    return acc


def _conv_bwd(dpre, proj, conv_w, dproj):
    lp = proj.shape[0]
    t = _tile(lp, (704, 384, 128, 64))
    hb = t // HALO
    nt = lp // t
    c0 = OFF_XBC // CONV_COLS

    def body(dp_ref, dpn_ref, u_ref, w_ref, _, du_ref, dw_ref, db_ref):
        i = pl.program_id(1)
        w = w_ref[...]
        u = u_ref[...].astype(F32)
        dpre = jnp.concatenate([dp_ref[...], dpn_ref[...] * (i < nt - 1).astype(F32)], axis=0)
        ahead = [dpre[:t, :]] + [pltpu.roll(dpre, t + HALO - j, 0)[:t, :] for j in range(1, CONV_WIDTH)]
        du = w[3:4, :] * ahead[0]
        for j in range(1, CONV_WIDTH):
            du = du + w[3 - j:4 - j, :] * ahead[j]
        du_ref[...] = du.astype(BF16)

        @pl.when(i == 0)
        def _():
            dw_ref[...] = jnp.zeros_like(dw_ref)
            db_ref[...] = jnp.zeros_like(db_ref)

        db_ref[...] += jnp.sum(ahead[0], axis=0, keepdims=True)
        for j in range(CONV_WIDTH):
            dw_ref[3 - j:4 - j, :] += jnp.sum(ahead[j] * u, axis=0, keepdims=True)

    nxt = lambda i: jnp.minimum((i + 1) * hb, lp // HALO - 1)
    return pl.pallas_call(
        body, name="conv_bwd", grid=(D_CONV // CONV_COLS, nt),
        in_specs=[
            pl.BlockSpec((t, CONV_COLS), lambda j, i: (i, j)),
            pl.BlockSpec((HALO, CONV_COLS), lambda j, i: (nxt(i), j)),
            pl.BlockSpec((t, CONV_COLS), lambda j, i: (i, c0 + j)),
            pl.BlockSpec((CONV_WIDTH, CONV_COLS), lambda j, i: (0, j)),
            ANY,
        ],
        out_specs=[
            pl.BlockSpec((t, CONV_COLS), lambda j, i: (i, c0 + j)),
            pl.BlockSpec((CONV_WIDTH, CONV_COLS), lambda j, i: (0, j)),
            pl.BlockSpec((1, CONV_COLS), lambda j, i: (0, j)),
        ],
        out_shape=[
            jax.ShapeDtypeStruct((lp, NP), BF16),
            jax.ShapeDtypeStruct((CONV_WIDTH, D_CONV), F32),
            jax.ShapeDtypeStruct((1, D_CONV), F32),
        ],
        input_output_aliases={4: 0},
        compiler_params=_params(("parallel", "arbitrary")),
    )(dpre, dpre, proj, conv_w, dproj)


def _swap_halves(t):
    w = t.shape[-1]
    lane = lax.broadcasted_iota(jnp.int32, t.shape, 1)
    return jnp.where((lane % HEAD_DIM) < HEAD_DIM // 2, pltpu.roll(t, w - HEAD_DIM // 2, 1),
                     pltpu.roll(t, HEAD_DIM // 2, 1))


def _act_fwd(proj, cos_t, sin_t, expand, dt_bias_pad):
    lp = proj.shape[0]
    t = _tile(lp, (384, 128, 64))

    def body(q_ref, k_ref, v_ref, dt_ref, cos_ref, sin_ref, ex_ref, bias_ref, qo_ref, ko_ref, vo_ref, dto_ref):
        i = pl.program_id(0)
        cos = cos_ref[...]
        sin = sin_ref[...]
        q = q_ref[...]
        qo_ref[...] = (q * jnp.tile(cos, (1, D_ATT // 128)) + _swap_halves(q) * jnp.tile(sin, (1, D_ATT // 128))).astype(BF16)
        k = k_ref[...]
        ko_ref[...] = (k * jnp.tile(cos, (1, D_KV // 128)) + _swap_halves(k) * jnp.tile(sin, (1, D_KV // 128))).astype(BF16)
        vo_ref[...] = v_ref[...].astype(BF16)
        raw = dt_ref[...] + bias_ref[...]
        sp = jnp.maximum(raw, 0.0) + jnp.log1p(jnp.exp(-jnp.abs(raw)))
        row = i * t + lax.broadcasted_iota(jnp.int32, sp.shape, 0)
        dto_ref[...] = _dot(jnp.where(row >= PAD_LEAD, sp, 0.0), ex_ref[...], NN, HIGHEST)

    return pl.pallas_call(
        body, name="act_fwd", grid=(lp // t,),
        in_specs=[
            pl.BlockSpec((t, D_ATT), lambda i: (i, OFF_Q // D_ATT)),
            pl.BlockSpec((t, D_KV), lambda i: (i, OFF_K // D_KV)),
            pl.BlockSpec((t, D_KV), lambda i: (i, OFF_V // D_KV)),
            pl.BlockSpec((t, 128), lambda i: (i, OFF_DT // 128)),
            pl.BlockSpec((t, 128), lambda i: (i, 0)),
            pl.BlockSpec((t, 128), lambda i: (i, 0)),
            pl.BlockSpec((128, D_SSD), lambda i: (0, 0)),
            pl.BlockSpec((1, 128), lambda i: (0, 0)),
        ],
        out_specs=[
            pl.BlockSpec((t, D_ATT), lambda i: (i, 0)),
            pl.BlockSpec((t, D_KV), lambda i: (i, 0)),
            pl.BlockSpec((t, D_KV), lambda i: (i, 0)),
            pl.BlockSpec((t, D_SSD), lambda i: (i, 0)),
        ],
        out_shape=[
            jax.ShapeDtypeStruct((lp, D_ATT), BF16),
            jax.ShapeDtypeStruct((lp, D_KV), BF16),
            jax.ShapeDtypeStruct((lp, D_KV), BF16),
            jax.ShapeDtypeStruct((lp, D_SSD), F32),
        ],
        compiler_params=_params(("parallel",)),
    )(proj, proj, proj, proj, cos_t, sin_t, expand, dt_bias_pad)


def _act_bwd(dqr, dkr, dv, dg, ddt_part, cos_t, sin_t, reduce_t, dproj):
    lp = dqr.shape[0]
    t = _tile(lp, (384, 128, 64))

    def body(dq_ref, dk_ref, dv_ref, dg_ref, ddt_ref, cos_ref, sin_ref, red_ref, _, o_ref, db_ref):
        i = pl.program_id(0)
        cos = cos_ref[...]
        sin = sin_ref[...]
        dq = dq_ref[...]
        dq = dq * jnp.tile(cos, (1, D_ATT // 128)) + _swap_halves(dq * jnp.tile(sin, (1, D_ATT // 128)))
        dk = dk_ref[...]
        dk = dk * jnp.tile(cos, (1, D_KV // 128)) + _swap_halves(dk * jnp.tile(sin, (1, D_KV // 128)))
        ddt = _dot(ddt_ref[...], red_ref[...], NN, HIGHEST)
        o_ref[...] = jnp.concatenate(
            [dq.astype(BF16), dg_ref[...].astype(BF16), dk.astype(BF16), dv_ref[...].astype(BF16), ddt.astype(BF16),
             jnp.zeros((t, NP - OFF_DT - 128), BF16)], axis=1)

        @pl.when(i == 0)
        def _():
            db_ref[...] = jnp.zeros_like(db_ref)

        db_ref[...] += jnp.sum(ddt, axis=0, keepdims=True)

    return pl.pallas_call(
        body, name="act_bwd", grid=(lp // t,),
        in_specs=[
            pl.BlockSpec((t, D_ATT), lambda i: (i, 0)),
            pl.BlockSpec((t, D_KV), lambda i: (i, 0)),
            pl.BlockSpec((t, D_KV), lambda i: (i, 0)),
            pl.BlockSpec((t, D_ATT), lambda i: (i, 0)),
            pl.BlockSpec((t, D_SSD), lambda i: (i, 0)),
            pl.BlockSpec((t, 128), lambda i: (i, 0)),
            pl.BlockSpec((t, 128), lambda i: (i, 0)),
            pl.BlockSpec((D_SSD, 128), lambda i: (0, 0)),
            ANY,
        ],
        out_specs=[pl.BlockSpec((t, TAIL_W), lambda i: (i, OFF_Q // TAIL_W)), pl.BlockSpec((1, 128), lambda i: (0, 0))],
        out_shape=[jax.ShapeDtypeStruct((lp, NP), BF16), jax.ShapeDtypeStruct((1, 128), F32)],
        input_output_aliases={8: 0},
        compiler_params=_params(("arbitrary",)),
    )(dqr, dkr, dv, dg, ddt_part, cos_t, sin_t, reduce_t, dproj)


def _cs_row(cs):
    row = lax.broadcasted_iota(jnp.int32, cs.shape, 0)
    lane = lax.broadcasted_iota(jnp.int32, cs.shape, 1)
    return jnp.sum(jnp.where(row == lane % HEAD_DIM, cs, 0.0), axis=0, keepdims=True)


def _ssd_fwd(proj, conv_w, conv_b, dt_rep, a_rep, dsk_rep, wn, tri):
    lp = proj.shape[0]
    nc = lp // CHUNK
    assert SSD_GPS == N_GROUPS
    gw = SSD_GPS * GROUP_W
    cps = _tile(nc, (SSD_CPS, 1))

    def body(rawa_ref, rawb_ref, cw_ref, cb_ref, dt_ref, z_ref, a_ref, dsk_ref, wn_ref, tri_ref,
             yn_ref, ytot_ref, hprev_ref, xbc_ref, dsilu_ref, h_scr, tail_scr):
        @pl.when(pl.program_id(1) == 0)
        def _():
            h_scr[...] = jnp.zeros_like(h_scr)
            tail_scr[...] = jnp.zeros_like(tail_scr)

        for sub in range(cps):
            rs = slice(sub * CHUNK, (sub + 1) * CHUNK)
            raw = jnp.concatenate([rawa_ref[rs, :], rawb_ref[rs, :]], axis=1)
            pre = _conv_pre(jnp.concatenate([tail_scr[...], raw], axis=0), cw_ref[...], cb_ref[...])
            tail_scr[...] = raw[CHUNK - HALO:, :]
            sg = _sigmoid(pre)
            xbc_ref[rs, :] = pre * sg
            dsilu_ref[rs, :] = sg * (1.0 + pre * (1.0 - sg))

        G = range(SSD_GPS)
        colsl = [slice(gi * GROUP_W, (gi + 1) * GROUP_W) for gi in G]
        rows4 = lax.broadcasted_iota(jnp.int32, (GROUP_W, GROUP_W), 0) // HEAD_DIM
        cols4 = lax.broadcasted_iota(jnp.int32, (GROUP_W, GROUP_W), 1) // HEAD_DIM
        lrow = lax.broadcasted_iota(jnp.int32, (CHUNK, GROUP_W), 0)
        lcol = lax.broadcasted_iota(jnp.int32, (CHUNK, GROUP_W), 1) % HEAD_DIM

        def one_chunk(sub):
            rs = slice(sub * CHUNK, (sub + 1) * CHUNK)
            xbc = [xbc_ref[rs, gi * XBC_BLK:(gi + 1) * XBC_BLK] for gi in G]
            dt = [dt_ref[rs, colsl[gi]] for gi in G]
            xs = [xbc[gi][:, :GROUP_W] for gi in G]
            b = [xbc[gi][:, GROUP_W:GROUP_W + D_STATE].astype(BF16) for gi in G]
            c = [xbc[gi][:, GROUP_W + D_STATE:].astype(BF16) for gi in G]
            hprev = [h_scr[gi] for gi in G]
            cs = [_dot(tri_ref[...], dt[gi] * a_ref[:, colsl[gi]], NN, HIGHEST) for gi in G]
            yoff = [_dot(c[gi], hprev[gi].astype(BF16)) for gi in G]
            cs_t = [_cs_row(cs[gi]) for gi in G]
            xdt = [xs[gi] * dt[gi] for gi in G]
            cs_last = [cs[gi][CHUNK - 1:CHUNK, :] for gi in G]
            st = [_dot(b[gi], (xdt[gi] * jnp.exp(cs_last[gi] - cs[gi])).astype(BF16), TN) for gi in G]
            cb4 = [_dot(c[gi], jnp.concatenate([b[gi]] * HEADS_PER_GROUP, axis=0), NT) for gi in G]
            m = [(cb4[gi] * jnp.exp(jnp.where(lrow >= lcol, cs[gi] - cs_t[gi], -jnp.inf))).astype(BF16) for gi in G]
            xbd = [jnp.where(rows4 == cols4, jnp.concatenate([xdt[gi].astype(BF16)] * HEADS_PER_GROUP, axis=0), 0.0)
                   for gi in G]
            ydiag = [_dot(m[gi], xbd[gi]) for gi in G]
            for gi in G:
                cols = colsl[gi]
                ytot = ydiag[gi] + yoff[gi] * jnp.exp(cs[gi]) + dsk_ref[:, cols] * xs[gi]
                z = z_ref[rs, cols]
                gz = ytot * (z * _sigmoid(z))
                rstd = lax.rsqrt(jnp.mean(gz * gz, axis=-1, keepdims=True) + EPS)
                hprev_ref[sub, gi] = hprev[gi]
                h_scr[gi] = hprev[gi] * jnp.exp(cs_last[gi]) + st[gi]
                ytot_ref[rs, cols] = ytot
                yn_ref[rs, cols] = (gz * rstd * wn_ref[:, cols]).astype(BF16)

        for sub in range(cps):
            one_chunk(sub)

    vec = pl.BlockSpec((1, gw), lambda g, c: (0, g))
    blk = pl.BlockSpec((cps * CHUNK, gw), lambda g, c: (c, g))
    xblk = pl.BlockSpec((cps * CHUNK, D_CONV), lambda g, c: (c, 0))
    half = D_CONV // 2
    return pl.pallas_call(
        body, name="ssd_fwd", grid=(N_GROUPS // SSD_GPS, nc // cps),
        in_specs=[
            pl.BlockSpec((cps * CHUNK, half), lambda g, c: (c, OFF_XBC // half)),
            pl.BlockSpec((cps * CHUNK, half), lambda g, c: (c, OFF_XBC // half + 1)),
            pl.BlockSpec((CONV_WIDTH, D_CONV), lambda g, c: (0, 0)),
            pl.BlockSpec((1, D_CONV), lambda g, c: (0, 0)),
            blk, blk, vec, vec, vec,
            pl.BlockSpec((CHUNK, CHUNK), lambda g, c: (0, 0)),
        ],
        out_specs=[blk, blk, pl.BlockSpec((cps, SSD_GPS, D_STATE, GROUP_W), lambda g, c: (c, g, 0, 0)), xblk, xblk],
        out_shape=[
            jax.ShapeDtypeStruct((lp, D_MIX), BF16),
            jax.ShapeDtypeStruct((lp, D_SSD), F32),
            jax.ShapeDtypeStruct((nc, N_GROUPS, D_STATE, GROUP_W), F32),
            jax.ShapeDtypeStruct((lp, D_CONV), F32),
            jax.ShapeDtypeStruct((lp, D_CONV), F32),
        ],
        scratch_shapes=[pltpu.VMEM((SSD_GPS, D_STATE, GROUP_W), F32), pltpu.VMEM((HALO, D_CONV), F32)],
        compiler_params=_params(("arbitrary", "arbitrary")),
    )(proj, proj, conv_w, conv_b, dt_rep, proj, a_rep, dsk_rep, wn, tri)


def _ssd_bwd(dmix, ytot, proj, xbc, dsilu, dt_rep, hprev, a_rep, dsk_rep, wn, tri, comm):
    lp = xbc.shape[0]
    nc = lp // CHUNK
    gps = SSD_GPS
    gw = gps * GROUP_W
    cps = _tile(nc, (SSD_CPS, 1))
    ncm = comm.n
    n_in, n_out = 11, 6
    grid = (N_GROUPS // gps, nc // cps)

    def all_groups(refs):
        (dyn_ref, ytot_ref, z_ref, xbc_ref, dsilu_ref, dt_ref, hprev_ref, a_ref, dsk_ref, wn_ref, tri_ref,
         dz_ref, dxbc_ref, ddt_ref, dd_ref, da_ref, dwn_ref, dh_scr) = refs
        G = range(gps)
        H = range(HEADS_PER_GROUP)
        cl = [slice(gi * GROUP_W, (gi + 1) * GROUP_W) for gi in G]
        hl = [slice(r * HEAD_DIM, (r + 1) * HEAD_DIM) for r in H]
        tri = tri_ref[...]
        xbc = [xbc_ref[:, gi * XBC_BLK:(gi + 1) * XBC_BLK] for gi in G]
        dt = [dt_ref[:, cl[gi]] for gi in G]
        a = [a_ref[:, cl[gi]] for gi in G]
        xs = [xbc[gi][:, :GROUP_W] for gi in G]
        bbf = [xbc[gi][:, GROUP_W:GROUP_W + D_STATE].astype(BF16) for gi in G]
        cbf = [xbc[gi][:, GROUP_W + D_STATE:].astype(BF16) for gi in G]
        hprev = [hprev_ref[gi] for gi in G]
        hbf = [hprev[gi].astype(BF16) for gi in G]
        dhn = [dh_scr[gi] for gi in G]
        dhnb = [dhn[gi].astype(BF16) for gi in G]
        cs = [_dot(tri, dt[gi] * a[gi], NN, HIGHEST) for gi in G]
        g = [_dot(cbf[gi], hbf[gi]) for gi in G]
        dxw = [_dot(bbf[gi], dhnb[gi]) for gi in G]
        cs_t = [_cs_row(cs[gi]) for gi in G]
        dy = []
        for gi in G:
            ytot = ytot_ref[:, cl[gi]]
            z = z_ref[:, cl[gi]]
            dyn = dyn_ref[:, cl[gi]]
            sz = _sigmoid(z)
            silu_z = z * sz
            gz = ytot * silu_z
            rstd = lax.rsqrt(jnp.mean(gz * gz, axis=-1, keepdims=True) + EPS)
            xhat = gz * rstd
            dwn_ref[:, cl[gi]] += jnp.sum(dyn * xhat, axis=0, keepdims=True)
            dxhat = dyn * wn_ref[:, cl[gi]]
            dgz = rstd * (dxhat - xhat * jnp.mean(dxhat * xhat, axis=-1, keepdims=True))
            dy.append(dgz * silu_z)
            dz_ref[:, cl[gi]] = (dgz * ytot * (sz * (1.0 + z * (1.0 - sz)))).astype(BF16)
            dd_ref[:, cl[gi]] += jnp.sum(dy[gi] * xs[gi], axis=0, keepdims=True)
        xdt = [xs[gi] * dt[gi] for gi in G]
        e = [jnp.exp(cs[gi]) for gi in G]
        cs_last = [cs[gi][CHUNK - 1:CHUNK, :] for gi in G]
        dte = [jnp.exp(cs_last[gi] - cs[gi]) for gi in G]
        cd = [jnp.exp(cs_last[gi]) for gi in G]
        dgb = [(dy[gi] * e[gi]).astype(BF16) for gi in G]
        dyb = [dy[gi].astype(BF16) for gi in G]
        xdtb = [xdt[gi].astype(BF16) for gi in G]
        dc = [_dot(dgb[gi], hbf[gi], NT) for gi in G]
        dhprev = [_dot(cbf[gi], dgb[gi], TN) for gi in G]
        db = [_dot((xdt[gi] * dte[gi]).astype(BF16), dhnb[gi], NT) for gi in G]
        row = lax.broadcasted_iota(jnp.int32, (CHUNK, CHUNK), 0)
        causal = row >= lax.broadcasted_iota(jnp.int32, (CHUNK, CHUNK), 1)
        cb = [_dot(cbf[gi], bbf[gi], NT) for gi in G]
        dm = [[_dot(dyb[gi][:, hl[r]], xdtb[gi][:, hl[r]], NT) for r in H] for gi in G]
        mb, dseg, dcbb = [], [], []
        for gi in G:
            mb.append([])
            dseg.append([])
            dcb = None
            for r in H:
                seg = cs[gi][:, r * HEAD_DIM:r * HEAD_DIM + 1] - cs_t[gi][:, hl[r]]
                lm = jnp.exp(jnp.where(causal, seg, -jnp.inf))
                m = cb[gi] * lm
                mb[gi].append(m.astype(BF16))
                dseg[gi].append(dm[gi][r] * m)
                dcb = dm[gi][r] * lm if r == 0 else dcb + dm[gi][r] * lm
            dcbb.append(dcb.astype(BF16))
        dxdt_diag = [[_dot(mb[gi][r], dyb[gi][:, hl[r]], TN) for r in H] for gi in G]
        ones = jnp.ones((CHUNK, HEAD_DIM), F32)
        colsum = [[_dot(dseg[gi][r], ones, TN, HIGHEST) for r in H] for gi in G]
        dc2 = [_dot(dcbb[gi], bbf[gi]) for gi in G]
        db2 = [_dot(dcbb[gi], cbf[gi], TN) for gi in G]
        dcs = []
        for gi in G:
            t_dte = dxw[gi] * xdt[gi] * dte[gi]
            dcs_last = (jnp.sum(dhn[gi] * hprev[gi], axis=0, keepdims=True) * cd[gi]
                        + jnp.sum(t_dte, axis=0, keepdims=True))
            diag = jnp.concatenate(
                [(jnp.sum(dseg[gi][r], axis=1, keepdims=True) - colsum[gi][r]) * (1.0 / HEAD_DIM) for r in H], axis=1)
            d = dy[gi] * g[gi] * e[gi] - t_dte + diag
            row = lax.broadcasted_iota(jnp.int32, d.shape, 0)
            dcs.append(d + jnp.where(row == CHUNK - 1, dcs_last, 0.0))
        dda = [_dot(tri, dcs[gi], TN, HIGHEST) for gi in G]
        for gi in G:
            dxdt = dxw[gi] * dte[gi] + jnp.concatenate(dxdt_diag[gi], axis=1)
            da_ref[:, cl[gi]] += jnp.sum(dda[gi] * dt[gi], axis=0, keepdims=True)
            ddt = dda[gi] * a[gi] + dxdt * xs[gi]
            dxs = dsk_ref[:, cl[gi]] * dy[gi] + dxdt * dt[gi]
            ddt_ref[:, cl[gi]] = ddt * (1.0 - jnp.exp(-dt[gi]))
            xcols = slice(gi * XBC_BLK, (gi + 1) * XBC_BLK)
            dxbc_ref[:, xcols] = jnp.concatenate([dxs, db[gi] + db2[gi], dc[gi] + dc2[gi]], axis=1) * dsilu_ref[:, xcols]
            dh_scr[gi] = dhprev[gi] + dhn[gi] * cd[gi]

    def body(*refs):
        ins = refs[:n_in]
        cin = refs[n_in:n_in + ncm]
        outs = refs[n_in + ncm:n_in + ncm + n_out]
        cout = refs[n_in + ncm + n_out:n_in + 2 * ncm + n_out]
        dh_scr = refs[n_in + 2 * ncm + n_out]
        sems = refs[n_in + 2 * ncm + n_out + 1:]
        g, c = pl.program_id(0), pl.program_id(1)

        @pl.when((g == 0) & (c == 0))
        def _():
            comm.start(cin, cout, sems)

        @pl.when(c == 0)
        def _():
            dh_scr[...] = jnp.zeros_like(dh_scr)
            for ref in outs[3:]:
                ref[...] = jnp.zeros_like(ref)

        for sub in reversed(range(cps)):
            rs = pl.ds(sub * CHUNK, CHUNK)
            chunk_ins = tuple(r.at[rs] for r in ins[:6]) + (ins[6].at[sub],) + ins[7:]
            chunk_outs = tuple(r.at[rs] for r in outs[:3]) + outs[3:]
            all_groups(chunk_ins + chunk_outs + (dh_scr,))

        @pl.when((g == grid[0] - 1) & (c == grid[1] - 1))
        def _():
            comm.wait(cin, cout, sems)

    rev = lambda c: grid[1] - 1 - c
    vec = pl.BlockSpec((1, gw), lambda g, c: (0, g))
    blk = pl.BlockSpec((cps * CHUNK, gw), lambda g, c: (rev(c), g))
    xblk = pl.BlockSpec((cps * CHUNK, gps * XBC_BLK), lambda g, c: (rev(c), g))
    res = pl.pallas_call(
        body, name="ssd_bwd", grid=grid,
        in_specs=[blk, blk, blk, xblk, xblk, blk,
                  pl.BlockSpec((cps, gps, D_STATE, GROUP_W), lambda g, c: (rev(c), g, 0, 0)),
                  vec, vec, vec,
                  pl.BlockSpec((CHUNK, CHUNK), lambda g, c: (0, 0))] + comm.in_specs,
        out_specs=[blk, xblk, blk, vec, vec, vec] + comm.out_specs,
        out_shape=[
            jax.ShapeDtypeStruct((lp, NP), BF16),
            jax.ShapeDtypeStruct((lp, D_CONV), F32),
            jax.ShapeDtypeStruct((lp, D_SSD), F32),
            jax.ShapeDtypeStruct((1, D_SSD), F32),
            jax.ShapeDtypeStruct((1, D_SSD), F32),
            jax.ShapeDtypeStruct((1, D_SSD), F32),
        ] + comm.out_shape,
        scratch_shapes=[pltpu.VMEM((gps, D_STATE, GROUP_W), F32)] + comm.scratch,
        compiler_params=_params(("arbitrary", "arbitrary"), side_effects=True),
    )(dmix, ytot, proj, xbc, dsilu, dt_rep, hprev, a_rep, dsk_rep, wn, tri, *comm.operands)
    return res


def _stack_heads(t, h):
    return jnp.concatenate([t[:, (REP * h + r) * HEAD_DIM:(REP * h + r + 1) * HEAD_DIM] for r in range(REP)], axis=0)


def _band(t2, t1, t0, h):
    sl = slice(h * HEAD_DIM, (h + 1) * HEAD_DIM)
    return jnp.concatenate([t2[:, sl], t1[:, sl], t0[:, sl]], axis=0)


def _attn_probs(s, sink, qc):
    s = s * (HEAD_DIM ** -0.5)
    key_abs = (qc - WINDOW_CHUNKS) * CHUNK + lax.broadcasted_iota(jnp.int32, s.shape, 1)
    s = jnp.where(key_abs >= PAD_LEAD, s, -jnp.inf)
    m = jnp.maximum(jnp.max(s, axis=-1, keepdims=True), sink)
    p = jnp.exp(s - m)
    ps = jnp.exp(sink - m)
    denom = jnp.sum(p, axis=-1, keepdims=True) + ps
    return p / denom, ps / denom


ATT_QC_FWD = 6
ATT_QC_BWD = 2


def _kv_specs(width, newest_chunk_of, kc):
    return [pl.BlockSpec((CHUNK, width), functools.partial(lambda j, p: (jnp.maximum(newest_chunk_of(p) - j, 0), 0), j))
            for j in range(kc - 1, -1, -1)]


def _attn_fwd(qr, kr, vb, proj, sink_stack, mix):
    lp = qr.shape[0]
    nc = lp // CHUNK
    qn = _tile(nc, (ATT_QC_FWD, 2, 1))
    kn = WINDOW_CHUNKS + qn
    qrows = qn * CHUNK

    def body(q_ref, *rest):
        k_refs, v_refs = rest[:kn], rest[kn:2 * kn]
        g_ref, sink_ref, _, att_ref, mix_ref = rest[2 * kn:]
        p = pl.program_id(0)
        q = q_ref[...]
        ks = [r[...] for r in k_refs]
        vs = [r[...] for r in v_refs]
        units = [(u, h) for u in range(qn) for h in range(KV_HEADS)]
        s = [_dot(_stack_heads(q[u * CHUNK:(u + 1) * CHUNK, :], h), _band(*ks[u:u + 3], h), NT) for u, h in units]
        vbh = [_band(*vs[u:u + 3], h) for u, h in units]
        pn = [_attn_probs(s[i], sink_ref[h], qn * p + u)[0].astype(BF16) for i, (u, h) in enumerate(units)]
        o = [_dot(pn[i], vbh[i]) for i in range(len(units))]
        att = jnp.concatenate(
            [jnp.concatenate([o[u * KV_HEADS + h][r * CHUNK:(r + 1) * CHUNK, :] for h in range(KV_HEADS) for r in range(REP)],
                             axis=1) for u in range(qn)], axis=0)
        att_ref[...] = att
        g = g_ref[...]
        mix_ref[...] = (att * (g * _sigmoid(g))).astype(BF16)

    newest = lambda p: qn * p + qn - 1
    return pl.pallas_call(
        body, name="attn_fwd", grid=(nc // qn,),
        in_specs=[pl.BlockSpec((qrows, D_ATT), lambda p: (p, 0))] + _kv_specs(D_KV, newest, kn) + _kv_specs(D_KV, newest, kn) + [
            pl.BlockSpec((qrows, D_ATT), lambda p: (p, OFF_G // D_ATT)),
            pl.BlockSpec((KV_HEADS, REP * CHUNK, 1), lambda p: (0, 0, 0)),
            ANY,
        ],
        out_specs=[pl.BlockSpec((qrows, D_ATT), lambda p: (p, 0)),
                   pl.BlockSpec((qrows, D_ATT), lambda p: (p, D_SSD // D_ATT))],
        out_shape=[jax.ShapeDtypeStruct((lp, D_ATT), F32), jax.ShapeDtypeStruct((lp, D_MIX), BF16)],
        input_output_aliases={2 * kn + 3: 1},
        compiler_params=_params(("parallel",)),
    )(qr, *([kr] * kn), *([vb] * kn), proj, sink_stack, mix)


def _attn_bwd(qr, kr, vb, att, proj, dmix, sink_stack):
    lp = qr.shape[0]
    nc = lp // CHUNK
    qn = ATT_QC_BWD
    kn = WINDOW_CHUNKS + qn
    assert nc % qn == 0 and WINDOW_CHUNKS % qn == 0
    steps = nc // qn
    qrows = qn * CHUNK
    wrows = kn * CHUNK

    def body(q_ref, *rest):
        k_refs, v_refs = rest[:kn], rest[kn:2 * kn]
        (att_ref, g_ref, do_ref, sink_ref, dq_ref, dk_ref, dv_ref, dg_ref, dsink_ref, dk_acc, dv_acc) = rest[2 * kn:]
        step = pl.program_id(0)

        @pl.when(step == 0)
        def _():
            dk_acc[...] = jnp.zeros_like(dk_acc)
            dv_acc[...] = jnp.zeros_like(dv_acc)
            dsink_ref[...] = jnp.zeros_like(dsink_ref)

        @pl.when(step < steps)
        def _():
            q = q_ref[...]
            ks = [r[...] for r in k_refs]
            vs = [r[...] for r in v_refs]
            att = att_ref[...]
            g = g_ref[...]
            dog = do_ref[...]
            sg = _sigmoid(g)
            dg_ref[...] = dog * att * (sg * (1.0 + g * (1.0 - sg)))
            do = dog * (g * sg)
            units = [(u, h) for u in range(qn) for h in range(KV_HEADS)]
            n = range(len(units))
            rows = [slice(u * CHUNK, (u + 1) * CHUNK) for u in range(qn)]
            qs = [_stack_heads(q[rows[u], :], h) for u, h in units]
            kb = [_band(*ks[u:u + 3], h) for u, h in units]
            vbh = [_band(*vs[u:u + 3], h) for u, h in units]
            dos = [_stack_heads(do[rows[u], :], h) for u, h in units]
            dosb = [dos[i].astype(BF16) for i in n]
            s = [_dot(qs[i], kb[i], NT) for i in n]
            dp = [_dot(dosb[i], vbh[i], NT) for i in n]
            ds, pnb = [], []
            for i, (u, h) in enumerate(units):
                pn, psink = _attn_probs(s[i], sink_ref[h], qn * step + u)
                delta = jnp.sum(dos[i] * _stack_heads(att[rows[u], :], h), axis=-1, keepdims=True)
                ds.append((pn * (dp[i] - delta)).astype(BF16))
                pnb.append(pn.astype(BF16))
                dsink_ref[h] += -(psink * delta)
            dqs = [_dot(ds[i], kb[i]) for i in n]
            dks = [_dot(ds[i], qs[i], TN) for i in n]
            dvs = [_dot(pnb[i], dosb[i], TN) for i in n]
            dq_ref[...] = jnp.concatenate(
                [jnp.concatenate([dqs[u * KV_HEADS + h][r * CHUNK:(r + 1) * CHUNK, :]
                                  for h in range(KV_HEADS) for r in range(REP)], axis=1) for u in range(qn)],
                axis=0) * (HEAD_DIM ** -0.5)
            for u in range(qn):
                band = slice(u * CHUNK, u * CHUNK + BAND)
                dk_acc[band, :] += jnp.concatenate(dks[u * KV_HEADS:(u + 1) * KV_HEADS], axis=1) * (HEAD_DIM ** -0.5)
                dv_acc[band, :] += jnp.concatenate(dvs[u * KV_HEADS:(u + 1) * KV_HEADS], axis=1)

        dk_ref[...] = dk_acc[0:qrows, :]
        dv_ref[...] = dv_acc[0:qrows, :]
        for acc in (dk_acc, dv_acc):
            rest_rows = acc[qrows:wrows, :]
            acc[0:wrows - qrows, :] = rest_rows
            acc[wrows - qrows:wrows, :] = jnp.zeros((qrows, D_KV), F32)

    qp = lambda p: jnp.minimum(p, steps - 1)
    newest = lambda p: qn * qp(p) + qn - 1
    qblk = pl.BlockSpec((qrows, D_ATT), lambda p: (qp(p), 0))
    oldest = pl.BlockSpec((qrows, D_KV), lambda p: (jnp.maximum(p - 1, 0), 0))
    return pl.pallas_call(
        body, name="attn_bwd", grid=(steps + 1,),
        in_specs=[qblk] + _kv_specs(D_KV, newest, kn) + _kv_specs(D_KV, newest, kn) + [
            qblk,
            pl.BlockSpec((qrows, D_ATT), lambda p: (qp(p), OFF_G // D_ATT)),
            pl.BlockSpec((qrows, D_ATT), lambda p: (qp(p), D_SSD // D_ATT)),
            pl.BlockSpec((KV_HEADS, REP * CHUNK, 1), lambda p: (0, 0, 0)),
        ],
        out_specs=[qblk, oldest, oldest, qblk, pl.BlockSpec((KV_HEADS, REP * CHUNK, 1), lambda p: (0, 0, 0))],
        out_shape=[
            jax.ShapeDtypeStruct((lp, D_ATT), F32),
            jax.ShapeDtypeStruct((lp, D_KV), F32),
            jax.ShapeDtypeStruct((lp, D_KV), F32),
            jax.ShapeDtypeStruct((lp, D_ATT), F32),
            jax.ShapeDtypeStruct((KV_HEADS, REP * CHUNK, 1), F32),
        ],
        scratch_shapes=[pltpu.VMEM((wrows, D_KV), F32), pltpu.VMEM((wrows, D_KV), F32)],
        compiler_params=_params(("arbitrary",)),
    )(qr, *([kr] * kn), *([vb] * kn), att, proj, dmix, sink_stack)


def _post_loss(out, x2d, target, w):
    lp = out.shape[0]
    nc = lp // CHUNK
    nx = x2d.shape[0] // CHUNK

    k = _tile(nc, (ROW_K, 3, 2, 1))

    def body(o_ref, *rest):
        x_refs, t_refs = rest[:k], rest[k:2 * k]
        w_ref, do_ref, dy_ref, gw_ref, loss_ref = rest[2 * k:]
        i = pl.program_id(0)

        @pl.when(i == 0)
        def _():
            gw_ref[...] = jnp.zeros_like(gw_ref)
            loss_ref[...] = jnp.zeros_like(loss_ref)

        o = o_ref[...]
        w = w_ref[...]
        rstd = lax.rsqrt(jnp.mean(o * o, axis=-1, keepdims=True) + EPS)
        xhat = o * rstd
        x = jnp.concatenate([r[...] for r in x_refs], axis=0)
        t = jnp.concatenate([r[...] for r in t_refs], axis=0)
        chunk = _chunk_of_row(i, k)
        err = (x + xhat * w - t) * ((chunk > 0) & (chunk <= nx)).astype(F32)
        loss_ref[...] += 0.5 * jnp.sum(jnp.mean(err * err, axis=-1, keepdims=True), axis=0, keepdims=True)
        dy = err * (1.0 / D_MODEL)
        dy_ref[...] = dy
        gw_ref[...] += jnp.sum(dy * xhat, axis=0, keepdims=True)
        dxhat = dy * w
        do_ref[...] = (rstd * (dxhat - xhat * jnp.mean(dxhat * xhat, axis=-1, keepdims=True))).astype(BF16)

    row = pl.BlockSpec((k * CHUNK, D_MODEL), lambda i: (i, 0))
    return pl.pallas_call(
        body, name="post_loss", grid=(nc // k,),
        in_specs=[row] + _x_specs(nx, k) + _x_specs(nx, k) + [pl.BlockSpec((1, D_MODEL), lambda i: (0, 0))],
        out_specs=[row, row, pl.BlockSpec((1, D_MODEL), lambda i: (0, 0)), pl.BlockSpec((1, 128), lambda i: (0, 0))],
        out_shape=[
            jax.ShapeDtypeStruct((lp, D_MODEL), BF16),
            jax.ShapeDtypeStruct((lp, D_MODEL), F32),
            jax.ShapeDtypeStruct((1, D_MODEL), F32),
            jax.ShapeDtypeStruct((1, 128), F32),
        ],
        compiler_params=_params(("arbitrary",)),
    )(out, *([x2d] * k), *([target] * k), w)


def _prenorm_bwd(dhn, x2d, h0, dy, w):
    nx = x2d.shape[0] // CHUNK
    k = _tile(nx, (X_K, 4, 2, 1))
    rows = k * CHUNK

    def backward(h, dhn, w):
        rstd = lax.rsqrt(jnp.mean(h * h, axis=-1, keepdims=True) + EPS)
        xhat = h * rstd
        dxhat = dhn * w
        dh = rstd * (dxhat - xhat * jnp.mean(dxhat * xhat, axis=-1, keepdims=True))
        return dh, jnp.sum(dhn * xhat, axis=0, keepdims=True)

    steps = nx // k
    depth = 4
    assert steps >= depth

    def body(dhn_hbm, dy_hbm, x_hbm, w_ref, gx_ref, gw_ref, dhn_buf, dy_buf, x_buf, sems):
        i = pl.program_id(0)

        def loads(step, slot):
            padded = pl.ds(pl.multiple_of((step * k + 1) * CHUNK, CHUNK), rows)
            plain = pl.ds(pl.multiple_of(step * rows, rows), rows)
            return [pltpu.make_async_copy(dhn_hbm.at[padded, :], dhn_buf.at[slot], sems.at[0, slot]),
                    pltpu.make_async_copy(dy_hbm.at[padded, :], dy_buf.at[slot], sems.at[1, slot]),
                    pltpu.make_async_copy(x_hbm.at[plain, :], x_buf.at[slot], sems.at[2, slot])]

        @pl.when(i == 0)
        def _():
            gw_ref[...] = jnp.zeros_like(gw_ref)
            for s in range(depth - 1):
                for c in loads(s, s):
                    c.start()

        ahead = i + depth - 1

        @pl.when(ahead < steps)
        def _():
            for c in loads(ahead, ahead % depth):
                c.start()

        slot = i % depth
        for c in loads(i, slot):
            c.wait()
        dh, gw = backward(x_buf[slot], dhn_buf[slot], w_ref[...])
        gx_ref[...] = dh + dy_buf[slot]
        gw_ref[...] += gw

    def body_meta(dhn_ref, h0_ref, w_ref, d0_ref, gw_ref):
        d0_ref[...], gw_ref[...] = backward(h0_ref[...], dhn_ref[...], w_ref[...])

    first = pl.BlockSpec((CHUNK, D_MODEL), lambda i: (0, 0))
    vec = pl.BlockSpec((1, D_MODEL), lambda i: (0, 0))
    wide = pl.BlockSpec((rows, D_MODEL), lambda i: (i, 0))
    gx, gw_x = pl.pallas_call(
        body, name="prenorm_bwd", grid=(steps,),
        in_specs=[ANY, ANY, ANY, vec],
        out_specs=[wide, vec],
        out_shape=[jax.ShapeDtypeStruct((nx * CHUNK, D_MODEL), F32), jax.ShapeDtypeStruct((1, D_MODEL), F32)],
        scratch_shapes=[pltpu.VMEM((depth, rows, D_MODEL), F32)] * 3 + [pltpu.SemaphoreType.DMA((3, depth))],
        compiler_params=_params(("arbitrary",)),
    )(dhn, dy, x2d, w)
    d0, gw_0 = pl.pallas_call(
        body_meta, name="prenorm_bwd_meta", grid=(1,),
        in_specs=[first, first, vec], out_specs=[first, vec],
        out_shape=[jax.ShapeDtypeStruct((CHUNK, D_MODEL), F32), jax.ShapeDtypeStruct((1, D_MODEL), F32)],
        compiler_params=_params(("arbitrary",)),
    )(dhn, h0, w)
    return gx, d0, gw_x + gw_0


def _adamw(slabs, w, m, v, name):
    rows, cols = w.shape
    tr = _tile(rows, (256, 128, 64, 16, 8))
    c1 = 1.0 - ADAM_B1 ** ADAM_STEP
    c2 = 1.0 - ADAM_B2 ** ADAM_STEP

    def body(s_ref, w_ref, m_ref, v_ref, g_ref, d_ref, mo_ref, vo_ref):
        g = s_ref[0].astype(F32)
        for k in range(1, slabs.shape[0]):
            g = g + s_ref[k].astype(F32)
        w = w_ref[...]
        m = ADAM_B1 * m_ref[...] + (1.0 - ADAM_B1) * g
        v = ADAM_B2 * v_ref[...] + (1.0 - ADAM_B2) * (g * g)
        g_ref[...] = g
        mo_ref[...] = m
        vo_ref[...] = v
        d_ref[...] = -ADAM_LR * ((m / c1) / (jnp.sqrt(v / c2) + ADAM_EPS) + ADAM_WD * w)

    blk = pl.BlockSpec((tr, cols), lambda i: (i, 0))
    shape = jax.ShapeDtypeStruct((rows, cols), F32)
    return pl.pallas_call(
        body, name=name, grid=(rows // tr,),
        in_specs=[pl.BlockSpec((slabs.shape[0], tr, cols), lambda i: (0, i, 0)), blk, blk, blk],
        out_specs=[blk, blk, blk, blk],
        out_shape=[shape, shape, shape, shape],
        compiler_params=_params(("parallel",)),
    )(slabs, w, m, v)


def _perm_xbc(a):
    lead = a.shape[:-1]
    xs = a[..., :D_SSD].reshape(lead + (N_GROUPS, GROUP_W))
    b = a[..., D_SSD:D_SSD + N_GROUPS * D_STATE].reshape(lead + (N_GROUPS, D_STATE))
    c = a[..., D_SSD + N_GROUPS * D_STATE:].reshape(lead + (N_GROUPS, D_STATE))
    return jnp.concatenate([xs, b, c], axis=-1).reshape(lead + (D_CONV,))


def _unperm_xbc(a):
    lead = a.shape[:-1]
    t = a.reshape(lead + (N_GROUPS, XBC_BLK))
    xs = t[..., :GROUP_W].reshape(lead + (D_SSD,))
    b = t[..., GROUP_W:GROUP_W + D_STATE].reshape(lead + (N_GROUPS * D_STATE,))
    c = t[..., GROUP_W + D_STATE:].reshape(lead + (N_GROUPS * D_STATE,))
    return jnp.concatenate([xs, b, c], axis=-1)


R_Z, R_XBC, R_DT, R_Q, R_K, R_V, R_G = 0, 2048, 6144, 6176, 7200, 7456, 7712


def _internal_of_reference():
    ref = np.arange(D_IN_PROJ)
    out = np.empty(D_IN_PROJ, np.int64)
    out[R_Z:R_XBC] = OFF_Z + ref[:D_SSD]
    xs = np.arange(D_SSD)
    out[R_XBC:R_XBC + D_SSD] = OFF_XBC + (xs // GROUP_W) * XBC_BLK + xs % GROUP_W
    bc = np.arange(N_GROUPS * D_STATE)
    out[R_XBC + D_SSD:R_XBC + D_SSD + N_GROUPS * D_STATE] = OFF_XBC + (bc // D_STATE) * XBC_BLK + GROUP_W + bc % D_STATE
    out[R_XBC + D_SSD + N_GROUPS * D_STATE:R_DT] = OFF_XBC + (bc // D_STATE) * XBC_BLK + GROUP_W + D_STATE + bc % D_STATE
    out[R_DT:R_Q] = OFF_DT + np.arange(SSD_HEADS)
    out[R_Q:R_K] = OFF_Q + np.arange(D_ATT)
    out[R_K:R_V] = OFF_K + np.arange(D_KV)
    out[R_V:R_G] = OFF_V + np.arange(D_KV)
    out[R_G:] = OFF_G + np.arange(D_ATT)
    return out


def _runs(src, dst_break):
    runs, lo = [], 0
    for i in range(1, len(src) + 1):
        if i == len(src) or src[i] != src[i - 1] + 1 or dst_break[i] != dst_break[i - 1]:
            runs.append((lo, i))
            lo = i
    return runs


RELAYOUT_ROWS = 256


def _lane_window(ref, lead, c0, n):
    a0 = c0 // 128 * 128
    a1 = min(-(-(c0 + n) // 128) * 128, ref.shape[-1])
    return ref[lead + (slice(None), slice(a0, a1))][:, c0 - a0:c0 - a0 + n]


def _gather_w_in(w_shard, small, x2d, norm_w):
    int_of_ref = _internal_of_reference()
    ref_of_int = np.full(NP, -1, np.int64)
    ref_of_int[int_of_ref] = np.arange(D_IN_PROJ)
    shard = np.where(ref_of_int >= 0, ref_of_int // SHARD_IN, -1)
    src = np.where(ref_of_int >= 0, ref_of_int, -10 - 2 * np.arange(NP))
    plan, zeros = [], 0
    for lo, hi in _runs(src, shard):
        if ref_of_int[lo] < 0:
            zeros += hi - lo
            continue
        if zeros:
            plan.append((None, 0, zeros))
            zeros = 0
        plan.append((int(ref_of_int[lo] // SHARD_IN), int(ref_of_int[lo] % SHARD_IN), hi - lo))
    if zeros:
        plan.append((None, 0, zeros))
    tr = RELAYOUT_ROWS
    steps = D_MODEL // tr
    side = _GatherTwoLevel(small)
    ns = side.n
    nx = x2d.shape[0] // CHUNK
    nc = nx + 2
    k = _tile(nc, (11, ROW_K, 3, 2, 1))
    pn = nc // k
    n = max(pn, steps)
    tile_of = lambda i: jnp.where(i < pn - 1, i + 1, 0)

    def body(w_ref, *rest):
        s_in, x_refs, nw_ref = rest[:ns], rest[ns:ns + k], rest[ns + k]
        o_ref, land_ref, s_out, hn_ref = rest[ns + k + 1], rest[ns + k + 2], rest[ns + k + 3:2 * ns + k + 3], rest[2 * ns + k + 3]
        tile_buf, send_sems, recv_sems, load_sems, local_sem, meta_buf, meta_sem = rest[2 * ns + k + 4:2 * ns + k + 11]
        side_sems = rest[2 * ns + k + 11:]
        i = pl.program_id(0)
        x, y, c = lax.axis_index("x"), lax.axis_index("y"), lax.axis_index("c")
        me, sibling = (x, y, c), (x, y, 1 - c)
        chips = [(1 - x, y), (x, 1 - y), (1 - x, 1 - y)]

        def prenorm_tile(t, head=None):
            h = jnp.concatenate([r[...] for r in x_refs], axis=0)
            if head is not None:
                h = jnp.concatenate([head, h[CHUNK:, :]], axis=0)
            h = h * (_chunk_of_row(t, k) <= nx).astype(F32)
            rstd = lax.rsqrt(jnp.mean(h * h, axis=-1, keepdims=True) + EPS)
            hn_ref[...] = (h * rstd * nw_ref[...]).astype(BF16)

        if pn > 1:
            @pl.when(i < pn - 1)
            def _():
                prenorm_tile(i + 1)

        def rows(t):
            return pl.ds(t * tr, tr)

        def tile(place, t):
            return land_ref.at[4 * place[0] + 2 * place[1] + place[2], rows(t), :]

        def copy(t, k, block, to, src=None):
            return pltpu.make_async_remote_copy(
                src_ref=tile(block, t) if src is None else src, dst_ref=tile(block, t),
                send_sem=send_sems.at[t, k], recv_sem=recv_sems.at[t, k], device_id=to, device_id_type=pl.DeviceIdType.MESH)

        def own_sends(t):
            mine = w_ref.at[rows(t), :]
            return [copy(t, 0, me, sibling, src=mine)] + [copy(t, 1 + j, me, (*chip, c), src=mine)
                                                          for j, chip in enumerate(chips)]

        own = pltpu.make_async_copy(w_ref, land_ref.at[4 * x + 2 * y + c], local_sem)

        @pl.when(i < steps)
        def _():
            @pl.when(i == 0)
            def _():
                side.start(s_in, s_out, side_sems)
                own.start()
                for t in range(steps):
                    for cp in own_sends(t):
                        cp.start()
                own.wait()

            forwards = []
            for j, chip in enumerate(chips):
                copy(i, 1 + j, (*chip, c), me).wait_recv()
                forwards.append(copy(i, 4 + j, (*chip, c), sibling))
                forwards[j].start()
            copy(i, 0, sibling, me).wait_recv()
            for j, chip in enumerate(chips):
                copy(i, 4 + j, (*chip, 1 - c), me).wait_recv()
            loads = [pltpu.make_async_copy(land_ref.at[s, rows(i), :], tile_buf.at[s], load_sems.at[s]) for s in range(N_DEV)]
            for cp in loads:
                cp.start()
            for cp in loads:
                cp.wait()
            o_ref[...] = jnp.concatenate(
                [jnp.zeros((tr, w), o_ref.dtype) if s is None else _lane_window(tile_buf, (s,), c0, w) for s, c0, w in plan],
                axis=1)
            for cp in forwards:
                cp.wait_send()

            @pl.when(i == steps - 1)
            def _():
                for t in range(steps):
                    for cp in own_sends(t):
                        cp.wait_send()
                side.wait(s_in, s_out, side_sems)

        @pl.when(i == n - 1)
        def _():
            meta = pltpu.make_async_copy(s_out[ns - 1], meta_buf, meta_sem)
            meta.start()
            meta.wait()
            tokens = jnp.concatenate([meta_buf[s] for s in range(N_DEV)], axis=1)
            prenorm_tile(0, jnp.concatenate([jnp.zeros((PAD_LEAD, D_MODEL), F32), tokens], axis=0))

    last = steps - 1
    res = pl.pallas_call(
        body, name="gather_w_in", grid=(n,),
        in_specs=[ANY] + side.in_specs + _x_specs(nx, k, tile_of) + [pl.BlockSpec((1, D_MODEL), lambda i: (0, 0))],
        out_specs=[pl.BlockSpec((tr, NP), lambda i: (jnp.minimum(i, last), 0)), ANY] + side.out_specs + [
            pl.BlockSpec((k * CHUNK, D_MODEL), lambda i: (tile_of(i), 0))],
        out_shape=[jax.ShapeDtypeStruct((D_MODEL, NP), w_shard.dtype),
                   jax.ShapeDtypeStruct((N_DEV, D_MODEL, SHARD_IN), w_shard.dtype)] + side.out_shape + [
            jax.ShapeDtypeStruct((nc * CHUNK, D_MODEL), BF16)],
        scratch_shapes=[pltpu.VMEM((N_DEV, tr, SHARD_IN), w_shard.dtype),
                        pltpu.SemaphoreType.DMA((steps, N_DEV - 1)), pltpu.SemaphoreType.DMA((steps, N_DEV - 1)),
                        pltpu.SemaphoreType.DMA((N_DEV,)), pltpu.SemaphoreType.DMA(()),
                        pltpu.VMEM((N_DEV,) + tuple(small[ns - 1].shape), F32),
                        pltpu.SemaphoreType.DMA(())] + side.scratch,
        compiler_params=_params(("arbitrary",), side_effects=True),
    )(w_shard, *side.operands, *([x2d] * k), norm_w)
    return [res[0]] + list(res[2:])


def _w_in_chip_slabs(dw):
    int_of_ref = _internal_of_reference()
    plan = []
    for s in range(N_DEV):
        cols = int_of_ref[s * SHARD_IN:(s + 1) * SHARD_IN]
        plan.append([(int(cols[lo]), hi - lo) for lo, hi in _runs(cols, np.zeros_like(cols))])
    tr = RELAYOUT_ROWS
    steps = D_MODEL // tr
    pairs = N_DEV // 2

    def body(dw_ref, o_ref, land_ref, send_buf, land_buf, send_sems, recv_sems, load_sems):
        i = pl.program_id(0)
        x, y, c = lax.axis_index("x"), lax.axis_index("y"), lax.axis_index("c")
        slot = i % 2

        def sends(step, sl):
            return [pltpu.make_async_remote_copy(
                src_ref=send_buf.at[sl, 2 * q + (1 - c)], dst_ref=land_ref.at[q, pl.ds(step * tr, tr), :],
                send_sem=send_sems.at[sl, q], recv_sem=recv_sems.at[step, q],
                device_id=(x, y, 1 - c), device_id_type=pl.DeviceIdType.MESH) for q in range(pairs)]

        @pl.when(i < steps)
        def _():
            @pl.when(i >= 2)
            def _():
                for cp in sends(i - 2, slot):
                    cp.wait_send()

            for s in range(N_DEV):
                send_buf[slot, s] = jnp.concatenate([_lane_window(dw_ref, (), c0, n) for c0, n in plan[s]], axis=1)
            for cp in sends(i, slot):
                cp.start()

        @pl.when(i >= 1)
        def _():
            loads = []
            for q, cp in enumerate(sends(i - 1, 1 - slot)):
                cp.wait_recv()
                loads.append(pltpu.make_async_copy(land_ref.at[q, pl.ds((i - 1) * tr, tr), :], land_buf.at[q], load_sems.at[q]))
                loads[q].start()
            for q in range(pairs):
                loads[q].wait()
                o_ref[q] = (send_buf[1 - slot, 2 * q + c].astype(F32) + land_buf[q].astype(F32)).astype(o_ref.dtype)

        @pl.when(i == steps)
        def _():
            for cp in sends(steps - 2, steps % 2) + sends(steps - 1, 1 - steps % 2):
                cp.wait_send()

    last = steps - 1
    return pl.pallas_call(
        body, name="dw_in_relayout", grid=(steps + 1,),
        in_specs=[pl.BlockSpec((tr, NP), lambda i: (jnp.minimum(i, last), 0))],
        out_specs=[pl.BlockSpec((pairs, tr, SHARD_IN), lambda i: (0, jnp.maximum(i - 1, 0), 0)), ANY],
        out_shape=[jax.ShapeDtypeStruct((pairs, D_MODEL, SHARD_IN), dw.dtype),
                   jax.ShapeDtypeStruct((pairs, D_MODEL, SHARD_IN), dw.dtype)],
        scratch_shapes=[pltpu.VMEM((2, N_DEV, tr, SHARD_IN), dw.dtype), pltpu.VMEM((pairs, tr, SHARD_IN), dw.dtype),
                        pltpu.SemaphoreType.DMA((2, pairs)), pltpu.SemaphoreType.DMA((steps, pairs)),
                        pltpu.SemaphoreType.DMA((pairs,))],
        compiler_params=_params(("arbitrary",), side_effects=True),
    )(dw)[0]


def _rep_heads(a):
    return jnp.repeat(a, HEAD_DIM, axis=1)


SMALL = (("norm_pre_w", 2048), ("conv_b", 4096), ("dt_bias", 32), ("a_log", 32), ("d_skip", 32),
         ("ssd_norm_w", 2048), ("attn_sinks", 16), ("norm_post_w", 2048))
SMALL_USED = sum(size for _, size in SMALL)
SMALL_LEN = 10368


def _pack_small(d, loss=None):
    parts = [d[name].reshape(1, size) for name, size in SMALL]
    tail = jnp.zeros((1, SMALL_LEN - SMALL_USED), F32)
    if loss is not None:
        tail = tail.at[0, 0].set(loss)
    return jnp.concatenate(parts + [tail], axis=1)


def _unpack_small(vec):
    out, off = {}, 0
    for name, size in SMALL:
        out[name] = vec[:, off:off + size]
        off += size
    return out


def kernel(x, meta_tokens, norm_pre_w, w_in, conv_w, conv_b, dt_bias, a_log, d_skip, ssd_norm_w, attn_sinks, w_out, norm_post_w, loss_target, m_meta_tokens, m_norm_pre_w, m_w_in, m_conv_w, m_conv_b, m_dt_bias, m_a_log, m_d_skip, m_ssd_norm_w, m_attn_sinks, m_w_out, m_norm_post_w, v_meta_tokens, v_norm_pre_w, v_w_in, v_conv_w, v_conv_b, v_dt_bias, v_a_log, v_d_skip, v_ssd_norm_w, v_attn_sinks, v_w_out, v_norm_post_w):
    seq = x.shape[1]
    lp = seq + 2 * CHUNK
    x2d = x[0]

    w_all, conv_w_g, meta_g, hn = _gather_w_in(
        w_in[0].astype(BF16), [conv_w[0], meta_tokens], x2d, norm_pre_w)
    conv_w_full = _perm_xbc(jnp.transpose(conv_w_g, (1, 0, 2)).reshape(CONV_WIDTH, D_CONV))
    conv_b_int = _perm_xbc(conv_b)
    meta_full = jnp.transpose(meta_g, (1, 0, 2)).reshape(N_META, D_MODEL)
    h0 = jnp.concatenate([jnp.zeros((PAD_LEAD, D_MODEL), F32), meta_full], axis=0)

    pos = (jnp.arange(lp) - PAD_LEAD).astype(F32)
    half = HEAD_DIM // 2
    inv = ROPE_THETA ** (-jnp.arange(half, dtype=F32) / half)
    ang = pos[:, None] * inv[None, :]
    cos_t = jnp.tile(jnp.cos(ang), (1, 4))
    sin_t = jnp.tile(jnp.concatenate([-jnp.sin(ang), jnp.sin(ang)], axis=1), (1, 2))
    head_of_col = np.arange(D_SSD) // HEAD_DIM
    expand = jnp.asarray((np.arange(128)[:, None] == head_of_col[None, :]).astype(np.float32))
    reduce_t = jnp.asarray((head_of_col[:, None] == np.arange(128)[None, :]).astype(np.float32))
    tri = jnp.asarray(np.tril(np.ones((CHUNK, CHUNK), np.float32)))
    a_rep = _rep_heads(-jnp.exp(a_log))
    dsk_rep = _rep_heads(d_skip)
    dt_bias_pad = jnp.pad(dt_bias, ((0, 0), (0, 128 - SSD_HEADS)))
    sink_stack = jnp.repeat(attn_sinks.reshape(KV_HEADS, REP), CHUNK, axis=1).reshape(KV_HEADS, REP * CHUNK, 1)

    tm = _tile(lp, (1056, 704, 128, 64))
    proj, w_out_g = _matmul(hn, w_all, tm=tm, tn=1536, tk=D_MODEL, out_dtype=F32, name="in_proj",
                            comm=_Comm([(w_out[0].astype(BF16), "gather")]))
    w_out_full = w_out_g.reshape(D_MIX, D_MODEL)
    qr, kr, vb, dt_rep = _act_fwd(proj, cos_t, sin_t, expand, dt_bias_pad)
    mix, ytot, hprev, xbc, dsilu = _ssd_fwd(proj, conv_w_full, conv_b_int, dt_rep, a_rep, dsk_rep, ssd_norm_w, tri)
    att, mix = _attn_fwd(qr, kr, vb, proj, sink_stack, mix)
    out = _matmul(mix, w_out_full, tm=tm, tn=1024, tk=D_MIX, out_dtype=F32, name="out_proj")
    dout, dy, g_norm_post, loss_part = _post_loss(out, x2d, loss_target[0], norm_post_w)

    dmix = _matmul(dout, w_out_full, trans_b=True, tm=tm, tn=1024, tk=D_MODEL, out_dtype=F32, name="dmix")
    dw_out = _matmul(mix, dout, trans_a=True, tm=512, tn=1024, tk=lp, out_dtype=BF16, name="dw_out")
    dqr, dkr, dv, dg, dsink_rows = _attn_bwd(qr, kr, vb, att, proj, dmix, sink_stack)
    dproj, dpre, ddt_part, dd_part, da_part, g_ssd_norm, g_out = _ssd_bwd(
        dmix, ytot, proj, xbc, dsilu, dt_rep, hprev, a_rep, dsk_rep, ssd_norm_w, tri,
        _Comm([(dw_out.reshape(N_DEV, D_MIX // N_DEV, D_MODEL), "scatter")]))
    dproj, dconv_w_int, dconv_b_int = _conv_bwd(dpre, proj, conv_w_full, dproj)
    dproj, ddt_bias = _act_bwd(dqr, dkr, dv, dg, ddt_part, cos_t, sin_t, reduce_t, dproj)
    dw_all = _matmul(hn, dproj, trans_a=True, tm=512, tn=1024, tk=lp, out_dtype=BF16, name="dw_in")
    dw_chip = _w_in_chip_slabs(dw_all)
    dhn, g_in = _matmul(dproj, w_all, trans_b=True, tm=tm, tn=1024, tk=1536, out_dtype=F32, name="dhn",
                        comm=_Comm([(dw_chip, "scatter")], scope="chips"))
    grad_x, dh0, g_norm_pre = _prenorm_bwd(dhn, x2d, h0, dy, norm_pre_w)

    dmeta = dh0[PAD_LEAD:, :]
    dconv_w_ref = _unperm_xbc(dconv_w_int)
    heads = lambda part: part.reshape(SSD_HEADS, HEAD_DIM).sum(axis=1).reshape(1, SSD_HEADS)
    small_local = _pack_small({
        "norm_pre_w": g_norm_pre, "conv_b": _unperm_xbc(dconv_b_int), "dt_bias": ddt_bias[:, :SSD_HEADS],
        "a_log": heads(da_part) * (-jnp.exp(a_log)), "d_skip": heads(dd_part), "ssd_norm_w": g_ssd_norm,
        "attn_sinks": dsink_rows.reshape(Q_HEADS, CHUNK).sum(axis=1).reshape(1, Q_HEADS),
        "norm_post_w": g_norm_post}, loss=loss_part[0, 0])
    g_conv, g_meta, g_small = _exchange(
        [(jnp.transpose(dconv_w_ref.reshape(CONV_WIDTH, N_DEV, D_CONV // N_DEV), (1, 0, 2)), "scatter"),
         (jnp.transpose(dmeta.reshape(N_META, N_DEV, D_MODEL // N_DEV), (1, 0, 2)), "scatter"),
         (small_local, "gather")], "exchange_small")

    res = {}
    res["w_in"] = [o[None] for o in _adamw(g_in, w_in[0], m_w_in[0], v_w_in[0], "adamw_w_in")]
    res["w_out"] = [o[None] for o in _adamw(g_out, w_out[0], m_w_out[0], v_w_out[0], "adamw_w_out")]
    res["conv_w"] = [o[None] for o in _adamw(g_conv, conv_w[0], m_conv_w[0], v_conv_w[0], "adamw_conv_w")]
    res["meta_tokens"] = _adamw(g_meta, meta_tokens, m_meta_tokens, v_meta_tokens, "adamw_meta")
    given = dict(norm_pre_w=(norm_pre_w, m_norm_pre_w, v_norm_pre_w), conv_b=(conv_b, m_conv_b, v_conv_b),
                 dt_bias=(dt_bias, m_dt_bias, v_dt_bias), a_log=(a_log, m_a_log, v_a_log),
                 d_skip=(d_skip, m_d_skip, v_d_skip), ssd_norm_w=(ssd_norm_w, m_ssd_norm_w, v_ssd_norm_w),
                 attn_sinks=(attn_sinks, m_attn_sinks, v_attn_sinks),
                 norm_post_w=(norm_post_w, m_norm_post_w, v_norm_post_w))
    packed = [_pack_small({k: t[j] for k, t in given.items()}) for j in range(3)]
    small_out = _adamw(g_small, packed[0], packed[1], packed[2], "adamw_small")
    small_res = [_unpack_small(r) for r in small_out]
    loss = small_out[0][0, SMALL_USED]

    order = ["meta_tokens", "norm_pre_w", "w_in", "conv_w", "conv_b", "dt_bias", "a_log", "d_skip", "ssd_norm_w",
             "attn_sinks", "w_out", "norm_post_w"]
    outs = []
    for j in range(4):
        for name in order:
            outs.append(res[name][j] if name in res else small_res[j][name])
    return (loss, grad_x[None], *outs)
```
